```python
import jax, jax.numpy as jnp
from jax import lax
import numpy as np

D_MODEL = 2048
BATCH = 8
SEQ = 8192
DEPTH = 1

CTX_LEN = 256
GRID_W = 64
D_MIX = D_MODEL
W_CONV = D_MIX // 2
W_LRU = D_MIX - W_CONV
N_CONV_GROUPS = 16
N_LRU_HEADS = 16
LRU_HEAD_DIM = W_LRU // N_LRU_HEADS
CONV_A_WIDTH = 3
CONV_A_LEFT = 1
CONV_B_WIDTH = 4
CONV_B_LEFT = 2
LRU_C = 8.0
N_DIR = 2
D_IN_PROJ = 4 * W_CONV + 2 * W_LRU
EPS = 1e-6

kernel_name = "hybrid_conv_rglru_parallel_heads_dit"


def _rmsnorm(x, g):
    xf = x.astype(jnp.float32)
    y = xf * lax.rsqrt(jnp.mean(xf * xf, axis=-1, keepdims=True) + EPS)
    return (y * g.astype(jnp.float32)).astype(x.dtype)


def _dwconv(x, w, axis, left):
    k = w.shape[0]
    n = x.shape[axis]
    pad = [(0, 0)] * x.ndim
    pad[axis] = (left, k - 1 - left)
    xp = jnp.pad(x, pad)
    out = lax.slice_in_dim(xp, 0, n, axis=axis) * w[0]
    for j in range(1, k):
        out = out + lax.slice_in_dim(xp, j, j + n, axis=axis) * w[j]
    return out


def _conv_latent(x, w, left):
    b, l, ch = x.shape
    rows = l // GRID_W
    return _dwconv(x.reshape(b, rows, GRID_W, ch), w, 2, left).reshape(b, l, ch)


def _conv_context(x, w, left):
    return _dwconv(x, w, 1, left)


def _blockdiag(x, w, b):
    bsz, l, _ = x.shape
    y = jnp.einsum('blhi,hij->blhj', x.reshape(bsz, l, N_LRU_HEADS, LRU_HEAD_DIM), w)
    return y.reshape(bsz, l, W_LRU) + b


def _lru_coeffs(xb, wa, ba, wx, bx, lam):
    xf = xb.astype(jnp.float32)
    r = jax.nn.sigmoid(_blockdiag(xf, wa.astype(jnp.float32), ba.astype(jnp.float32)))
    i = jax.nn.sigmoid(_blockdiag(xf, wx.astype(jnp.float32), bx.astype(jnp.float32)))
    log_a = -LRU_C * r * jax.nn.softplus(-lam.astype(jnp.float32))
    a = jnp.exp(log_a)
    bterm = jnp.sqrt(-jnp.expm1(2.0 * log_a)) * (i * xf)
    return a, bterm


def _combine(e1, e2):
    a1, b1 = e1
    a2, b2 = e2
    return a1 * a2, a2 * b1 + b2


def _linear_scan(a, b, h0, reverse):
    if h0 is not None:
        idx = -1 if reverse else 0
        b = b.at[:, idx].add(a[:, idx] * h0)
    _, h = lax.associative_scan(_combine, (a, b), reverse=reverse, axis=1)
    return h


def _split_proj(p):
    cuts = [W_CONV, 2 * W_CONV, 3 * W_CONV, 4 * W_CONV, 4 * W_CONV + W_LRU]
    return jnp.split(p, cuts, axis=-1)


def _fwd_setup_inputs(seed: int = 0) -> dict:
    key = jax.random.key(seed)
    ks = jax.random.split(key, 20)
    f = jnp.float32
    nrm = lambda k, s, sc: jax.random.normal(k, s, f) * sc
    a0 = jax.random.uniform(ks[17], (DEPTH, N_DIR, W_LRU), f, 0.9, 0.999)
    s = a0 ** (1.0 / LRU_C)
    lru_lambda = jnp.log(s) - jnp.log1p(-s)
    return {
        "x": nrm(ks[0], (BATCH, SEQ, D_MODEL), 1.0),
        "c": nrm(ks[1], (BATCH, D_MODEL), 1.0),
        "ctx": nrm(ks[2], (BATCH, CTX_LEN, D_MODEL), 1.0),
        "c_ctx": nrm(ks[3], (D_MODEL,), 1.0),
        "norm_g": 1.0 + nrm(ks[4], (DEPTH, D_MODEL), 0.02),
        "w_ada": nrm(ks[5], (DEPTH, D_MODEL, 3 * D_MODEL), 0.5 * D_MODEL ** -0.5),
        "b_ada": nrm(ks[6], (DEPTH, 3 * D_MODEL), 0.02),
        "w_in": nrm(ks[7], (DEPTH, D_MODEL, D_IN_PROJ), D_MODEL ** -0.5),
        "w_conv_a": nrm(ks[8], (DEPTH, CONV_A_WIDTH, W_CONV), CONV_A_WIDTH ** -0.5),
        "w_conv_b": nrm(ks[9], (DEPTH, CONV_B_WIDTH, W_LRU), CONV_B_WIDTH ** -0.5),
        "b_conv_b": nrm(ks[10], (DEPTH, W_LRU), 0.02),
        "lru_wa": nrm(ks[11], (DEPTH, N_DIR, N_LRU_HEADS, LRU_HEAD_DIM, LRU_HEAD_DIM), LRU_HEAD_DIM ** -0.5),
        "lru_ba": nrm(ks[12], (DEPTH, N_DIR, W_LRU), 0.02),
        "lru_wx": nrm(ks[13], (DEPTH, N_DIR, N_LRU_HEADS, LRU_HEAD_DIM, LRU_HEAD_DIM), LRU_HEAD_DIM ** -0.5),
        "lru_bx": nrm(ks[14], (DEPTH, N_DIR, W_LRU), 0.02),
        "lru_lambda": lru_lambda,
        "w_out": nrm(ks[15], (DEPTH, D_MIX, D_MODEL), D_MIX ** -0.5),
        "final_g": 1.0 + nrm(ks[16], (D_MODEL,), 0.02),
    }


def _fwd_reference(x, c, ctx, c_ctx, norm_g, w_ada, b_ada, w_in, w_conv_a, w_conv_b, b_conv_b,
              lru_wa, lru_ba, lru_wx, lru_bx, lru_lambda, w_out, final_g):
    h_lat = x
    h_ctx = ctx
    for l in range(DEPTH):
        last = l == DEPTH - 1
        mod_lat = jax.nn.silu(c) @ w_ada[l] + b_ada[l]
        sh_l, sc_l, gt_l = jnp.split(mod_lat, 3, axis=-1)
        mod_ctx = jax.nn.silu(c_ctx) @ w_ada[l] + b_ada[l]
        sh_c, sc_c, gt_c = jnp.split(mod_ctx, 3, axis=-1)

        hl = _rmsnorm(h_lat, norm_g[l]) * (1.0 + sc_l[:, None]) + sh_l[:, None]
        hc = _rmsnorm(h_ctx, norm_g[l]) * (1.0 + sc_c) + sh_c

        bl, cl, ul, gl, vl, ql = _split_proj(hl @ w_in[l])
        bc, cc_, uc, gc, vc, qc = _split_proj(hc @ w_in[l])

        ya = bl * _conv_latent(cl * ul, w_conv_a[l], CONV_A_LEFT) * jax.nn.silu(gl)

        xbl = _conv_latent(vl, w_conv_b[l], CONV_B_LEFT) + b_conv_b[l]
        xbc = _conv_context(vc, w_conv_b[l], CONV_B_LEFT) + b_conv_b[l]
        y_lru = None
        ctx_states = []
        for d, rev in enumerate((False, True)):
            a_c, b_c = _lru_coeffs(xbc, lru_wa[l, d], lru_ba[l, d], lru_wx[l, d], lru_bx[l, d], lru_lambda[l, d])
            hs_c = _linear_scan(a_c, b_c, None, rev)
            h0 = hs_c[:, 0] if rev else hs_c[:, -1]
            a_l, b_l = _lru_coeffs(xbl, lru_wa[l, d], lru_ba[l, d], lru_wx[l, d], lru_bx[l, d], lru_lambda[l, d])
            hs_l = _linear_scan(a_l, b_l, h0, rev)
            y_lru = hs_l if y_lru is None else y_lru + hs_l
            ctx_states.append(hs_c)
        yb = y_lru.astype(h_lat.dtype) * jax.nn.silu(ql)

        out_lat = jnp.concatenate([ya, yb], axis=-1) @ w_out[l]
        new_lat = h_lat + gt_l[:, None] * out_lat

        if not last:
            ya_c = bc * _conv_context(cc_ * uc, w_conv_a[l], CONV_A_LEFT) * jax.nn.silu(gc)
            yb_c = (ctx_states[0] + ctx_states[1]).astype(h_ctx.dtype) * jax.nn.silu(qc)
            out_ctx = jnp.concatenate([ya_c, yb_c], axis=-1) @ w_out[l]
            h_ctx = h_ctx + gt_c * out_ctx
        h_lat = new_lat
    return _rmsnorm(h_lat, final_g)


import jax as _jax
import jax.numpy as _jnp

TWIN_FORMAT = 'train_step'
FWD_PARAMS = ['x', 'c', 'ctx', 'c_ctx', 'norm_g', 'w_ada', 'b_ada', 'w_in', 'w_conv_a', 'w_conv_b', 'b_conv_b', 'lru_wa', 'lru_ba', 'lru_wx', 'lru_bx', 'lru_lambda', 'w_out', 'final_g']
TWIN_WEIGHTS = ['c_ctx', 'norm_g', 'w_ada', 'b_ada', 'w_in', 'w_conv_a', 'w_conv_b', 'b_conv_b', 'lru_wa', 'lru_ba', 'lru_wx', 'lru_bx', 'lru_lambda', 'w_out', 'final_g']
TWIN_DIFF_INPUT = 'x'
TWIN_INPUTS = ['x', 'c', 'ctx', 'c_ctx', 'norm_g', 'w_ada', 'b_ada', 'w_in', 'w_conv_a', 'w_conv_b', 'b_conv_b', 'lru_wa', 'lru_ba', 'lru_wx', 'lru_bx', 'lru_lambda', 'w_out', 'final_g', 'loss_target', 'm_c_ctx', 'm_norm_g', 'm_w_ada', 'm_b_ada', 'm_w_in', 'm_w_conv_a', 'm_w_conv_b', 'm_b_conv_b', 'm_lru_wa', 'm_lru_ba', 'm_lru_wx', 'm_lru_bx', 'm_lru_lambda', 'm_w_out', 'm_final_g', 'v_c_ctx', 'v_norm_g', 'v_w_ada', 'v_b_ada', 'v_w_in', 'v_w_conv_a', 'v_w_conv_b', 'v_b_conv_b', 'v_lru_wa', 'v_lru_ba', 'v_lru_wx', 'v_lru_bx', 'v_lru_lambda', 'v_w_out', 'v_final_g']
TWIN_OUTPUTS = ['loss', 'grad_x', 'grad_c_ctx', 'grad_norm_g', 'grad_w_ada', 'grad_b_ada', 'grad_w_in', 'grad_w_conv_a', 'grad_w_conv_b', 'grad_b_conv_b', 'grad_lru_wa', 'grad_lru_ba', 'grad_lru_wx', 'grad_lru_bx', 'grad_lru_lambda', 'grad_w_out', 'grad_final_g', 'delta_c_ctx', 'delta_norm_g', 'delta_w_ada', 'delta_b_ada', 'delta_w_in', 'delta_w_conv_a', 'delta_w_conv_b', 'delta_b_conv_b', 'delta_lru_wa', 'delta_lru_ba', 'delta_lru_wx', 'delta_lru_bx', 'delta_lru_lambda', 'delta_w_out', 'delta_final_g', 'new_m_c_ctx', 'new_m_norm_g', 'new_m_w_ada', 'new_m_b_ada', 'new_m_w_in', 'new_m_w_conv_a', 'new_m_w_conv_b', 'new_m_b_conv_b', 'new_m_lru_wa', 'new_m_lru_ba', 'new_m_lru_wx', 'new_m_lru_bx', 'new_m_lru_lambda', 'new_m_w_out', 'new_m_final_g', 'new_v_c_ctx', 'new_v_norm_g', 'new_v_w_ada', 'new_v_b_ada', 'new_v_w_in', 'new_v_w_conv_a', 'new_v_w_conv_b', 'new_v_b_conv_b', 'new_v_lru_wa', 'new_v_lru_ba', 'new_v_lru_wx', 'new_v_lru_bx', 'new_v_lru_lambda', 'new_v_w_out', 'new_v_final_g']
TWIN_LEAF_KINDS = {'loss': 'loss', 'grad_x': 'grad_x', 'grad_c_ctx': 'grad_w', 'grad_norm_g': 'grad_w', 'grad_w_ada': 'grad_w', 'grad_b_ada': 'grad_w', 'grad_w_in': 'grad_w', 'grad_w_conv_a': 'grad_w', 'grad_w_conv_b': 'grad_w', 'grad_b_conv_b': 'grad_w', 'grad_lru_wa': 'grad_w', 'grad_lru_ba': 'grad_w', 'grad_lru_wx': 'grad_w', 'grad_lru_bx': 'grad_w', 'grad_lru_lambda': 'grad_w', 'grad_w_out': 'grad_w', 'grad_final_g': 'grad_w', 'delta_c_ctx': 'delta_w', 'delta_norm_g': 'delta_w', 'delta_w_ada': 'delta_w', 'delta_b_ada': 'delta_w', 'delta_w_in': 'delta_w', 'delta_w_conv_a': 'delta_w', 'delta_w_conv_b': 'delta_w', 'delta_b_conv_b': 'delta_w', 'delta_lru_wa': 'delta_w', 'delta_lru_ba': 'delta_w', 'delta_lru_wx': 'delta_w', 'delta_lru_bx': 'delta_w', 'delta_lru_lambda': 'delta_w', 'delta_w_out': 'delta_w', 'delta_final_g': 'delta_w', 'new_m_c_ctx': 'new_m', 'new_m_norm_g': 'new_m', 'new_m_w_ada': 'new_m', 'new_m_b_ada': 'new_m', 'new_m_w_in': 'new_m', 'new_m_w_conv_a': 'new_m', 'new_m_w_conv_b': 'new_m', 'new_m_b_conv_b': 'new_m', 'new_m_lru_wa': 'new_m', 'new_m_lru_ba': 'new_m', 'new_m_lru_wx': 'new_m', 'new_m_lru_bx': 'new_m', 'new_m_lru_lambda': 'new_m', 'new_m_w_out': 'new_m', 'new_m_final_g': 'new_m', 'new_v_c_ctx': 'new_v', 'new_v_norm_g': 'new_v', 'new_v_w_ada': 'new_v', 'new_v_b_ada': 'new_v', 'new_v_w_in': 'new_v', 'new_v_w_conv_a': 'new_v', 'new_v_w_conv_b': 'new_v', 'new_v_b_conv_b': 'new_v', 'new_v_lru_wa': 'new_v', 'new_v_lru_ba': 'new_v', 'new_v_lru_wx': 'new_v', 'new_v_lru_bx': 'new_v', 'new_v_lru_lambda': 'new_v', 'new_v_w_out': 'new_v', 'new_v_final_g': 'new_v'}


def _forward(args):
    return _fwd_reference(*[args[k] for k in FWD_PARAMS])


def _output_shape():
    def fwd():
        inp = _fwd_setup_inputs(0)
        return _fwd_reference(*[inp[k] for k in FWD_PARAMS])
    out = _jax.eval_shape(fwd)
    return out.shape, out.dtype

N_MICROBATCH = 1
ADAM_LR = 0.001
ADAM_B1 = 0.9
ADAM_B2 = 0.999
ADAM_EPS = 1e-08
ADAM_WD = 0.01
ADAM_STEP = 10
PER_EXAMPLE_BATCH_AXIS = {'x': 0, 'c': 0, 'ctx': 0, 'loss_target': 0}
SHARED_INPUTS = []
_WEIGHT_DTYPES = {'c_ctx': _jnp.float32, 'norm_g': _jnp.float32, 'w_ada': _jnp.float32, 'b_ada': _jnp.float32, 'w_in': _jnp.float32, 'w_conv_a': _jnp.float32, 'w_conv_b': _jnp.float32, 'b_conv_b': _jnp.float32, 'lru_wa': _jnp.float32, 'lru_ba': _jnp.float32, 'lru_wx': _jnp.float32, 'lru_bx': _jnp.float32, 'lru_lambda': _jnp.float32, 'w_out': _jnp.float32, 'final_g': _jnp.float32}
MOMENT_SCALE = {'c_ctx': 1.901591e-02, 'norm_g': 7.264563e-02, 'w_ada': 1.363484e-01, 'b_ada': 2.423018e-01, 'w_in': 5.804252e-02, 'w_conv_a': 2.972041e-02, 'w_conv_b': 8.278231e-02, 'b_conv_b': 3.114227e-01, 'lru_wa': 8.027062e-03, 'lru_ba': 8.819900e-03, 'lru_wx': 1.565864e-02, 'lru_bx': 1.625554e-02, 'lru_lambda': 2.321093e-02, 'w_out': 6.917672e-02, 'final_g': 3.216029e+01}


def _to_microbatches(a, axis):
    t = _jnp.moveaxis(a, axis, 0)
    t = t.reshape((N_MICROBATCH, t.shape[0] // N_MICROBATCH) + t.shape[1:])
    return _jnp.moveaxis(t, 1, axis + 1)


def setup_inputs(seed: int = 0) -> dict:
    inp = _fwd_setup_inputs(seed)
    key = _jax.random.fold_in(_jax.random.key(seed), 7919)
    shape, _ = _output_shape()
    out = dict(inp)
    out["loss_target"] = _jax.random.normal(_jax.random.fold_in(key, 0), shape, _jnp.float32)
    for i, name in enumerate(TWIN_WEIGHTS):
        w = inp[name].astype(_jnp.float32)
        if MOMENT_SCALE is None:
            s = _jnp.sqrt(_jnp.mean(_jnp.square(w)) + 1e-30)
        else:
            s = MOMENT_SCALE[name]
        km, kv = _jax.random.split(_jax.random.fold_in(key, i + 1))
        out[name] = w
        out["m_" + name] = s * _jax.random.normal(km, w.shape, _jnp.float32)
        out["v_" + name] = (s * s) * _jax.random.uniform(kv, w.shape, _jnp.float32, 0.5, 1.5)
    if N_MICROBATCH > 1:
        for name, axis in PER_EXAMPLE_BATCH_AXIS.items():
            out[name] = _to_microbatches(out[name], axis)
    return {'x': out['x'], 'c': out['c'], 'ctx': out['ctx'], 'c_ctx': out['c_ctx'], 'norm_g': out['norm_g'], 'w_ada': out['w_ada'], 'b_ada': out['b_ada'], 'w_in': out['w_in'], 'w_conv_a': out['w_conv_a'], 'w_conv_b': out['w_conv_b'], 'b_conv_b': out['b_conv_b'], 'lru_wa': out['lru_wa'], 'lru_ba': out['lru_ba'], 'lru_wx': out['lru_wx'], 'lru_bx': out['lru_bx'], 'lru_lambda': out['lru_lambda'], 'w_out': out['w_out'], 'final_g': out['final_g'], 'loss_target': out['loss_target'], 'm_c_ctx': out['m_c_ctx'], 'm_norm_g': out['m_norm_g'], 'm_w_ada': out['m_w_ada'], 'm_b_ada': out['m_b_ada'], 'm_w_in': out['m_w_in'], 'm_w_conv_a': out['m_w_conv_a'], 'm_w_conv_b': out['m_w_conv_b'], 'm_b_conv_b': out['m_b_conv_b'], 'm_lru_wa': out['m_lru_wa'], 'm_lru_ba': out['m_lru_ba'], 'm_lru_wx': out['m_lru_wx'], 'm_lru_bx': out['m_lru_bx'], 'm_lru_lambda': out['m_lru_lambda'], 'm_w_out': out['m_w_out'], 'm_final_g': out['m_final_g'], 'v_c_ctx': out['v_c_ctx'], 'v_norm_g': out['v_norm_g'], 'v_w_ada': out['v_w_ada'], 'v_b_ada': out['v_b_ada'], 'v_w_in': out['v_w_in'], 'v_w_conv_a': out['v_w_conv_a'], 'v_w_conv_b': out['v_w_conv_b'], 'v_b_conv_b': out['v_b_conv_b'], 'v_lru_wa': out['v_lru_wa'], 'v_lru_ba': out['v_lru_ba'], 'v_lru_wx': out['v_lru_wx'], 'v_lru_bx': out['v_lru_bx'], 'v_lru_lambda': out['v_lru_lambda'], 'v_w_out': out['v_w_out'], 'v_final_g': out['v_final_g']}


def _loss(weights, diff, rest, loss_target):
    with _jax.named_scope("forward"):
        args = {**rest, TWIN_DIFF_INPUT: diff, **{k: w.astype(_WEIGHT_DTYPES[k]) for k, w in weights.items()}}
        y = _forward(args)
    with _jax.named_scope("loss_head"):
        err = _jnp.square(y.astype(_jnp.float32) - loss_target)
        return 0.5 * _jnp.sum(_jnp.mean(err, axis=-1)) if err.ndim else 0.5 * err


def _adamw(w, g, m, v):
    m = ADAM_B1 * m + (1.0 - ADAM_B1) * g
    v = ADAM_B2 * v + (1.0 - ADAM_B2) * _jnp.square(g)
    m_hat = m / (1.0 - ADAM_B1 ** ADAM_STEP)
    v_hat = v / (1.0 - ADAM_B2 ** ADAM_STEP)
    delta = -ADAM_LR * (m_hat / (_jnp.sqrt(v_hat) + ADAM_EPS) + ADAM_WD * w)
    return delta, m, v


def reference(x, c, ctx, c_ctx, norm_g, w_ada, b_ada, w_in, w_conv_a, w_conv_b, b_conv_b, lru_wa, lru_ba, lru_wx, lru_bx, lru_lambda, w_out, final_g, loss_target, m_c_ctx, m_norm_g, m_w_ada, m_b_ada, m_w_in, m_w_conv_a, m_w_conv_b, m_b_conv_b, m_lru_wa, m_lru_ba, m_lru_wx, m_lru_bx, m_lru_lambda, m_w_out, m_final_g, v_c_ctx, v_norm_g, v_w_ada, v_b_ada, v_w_in, v_w_conv_a, v_w_conv_b, v_b_conv_b, v_lru_wa, v_lru_ba, v_lru_wx, v_lru_bx, v_lru_lambda, v_w_out, v_final_g):
    given = dict(x=x, c=c, ctx=ctx, c_ctx=c_ctx, norm_g=norm_g, w_ada=w_ada, b_ada=b_ada, w_in=w_in, w_conv_a=w_conv_a, w_conv_b=w_conv_b, b_conv_b=b_conv_b, lru_wa=lru_wa, lru_ba=lru_ba, lru_wx=lru_wx, lru_bx=lru_bx, lru_lambda=lru_lambda, w_out=w_out, final_g=final_g, loss_target=loss_target, m_c_ctx=m_c_ctx, m_norm_g=m_norm_g, m_w_ada=m_w_ada, m_b_ada=m_b_ada, m_w_in=m_w_in, m_w_conv_a=m_w_conv_a, m_w_conv_b=m_w_conv_b, m_b_conv_b=m_b_conv_b, m_lru_wa=m_lru_wa, m_lru_ba=m_lru_ba, m_lru_wx=m_lru_wx, m_lru_bx=m_lru_bx, m_lru_lambda=m_lru_lambda, m_w_out=m_w_out, m_final_g=m_final_g, v_c_ctx=v_c_ctx, v_norm_g=v_norm_g, v_w_ada=v_w_ada, v_b_ada=v_b_ada, v_w_in=v_w_in, v_w_conv_a=v_w_conv_a, v_w_conv_b=v_w_conv_b, v_b_conv_b=v_b_conv_b, v_lru_wa=v_lru_wa, v_lru_ba=v_lru_ba, v_lru_wx=v_lru_wx, v_lru_bx=v_lru_bx, v_lru_lambda=v_lru_lambda, v_w_out=v_w_out, v_final_g=v_final_g)
    weights = {n: given[n] for n in TWIN_WEIGHTS}
    shared = {n: given[n] for n in SHARED_INPUTS}
    per_example = {n: given[n] for n in ['x', 'c', 'ctx']}
    grad_fn = _jax.value_and_grad(_loss, argnums=(0, 1))

    def one_microbatch(ex, loss_target):
        ex = dict(ex)
        diff = ex.pop(TWIN_DIFF_INPUT)
        return grad_fn(weights, diff, {**shared, **ex}, loss_target)

    if N_MICROBATCH == 1:
        loss, (grad_w, grad_x) = one_microbatch(per_example, given["loss_target"])
    else:
        def body(carry, xs):
            loss_sum, grad_sum = carry
            l_k, (gw_k, gx_k) = one_microbatch(xs[0], xs[1])
            with _jax.named_scope("update"):
                return (loss_sum + l_k, _jax.tree.map(_jnp.add, grad_sum, gw_k)), gx_k

        init = (_jnp.zeros((), _jnp.float32), _jax.tree.map(_jnp.zeros_like, weights))
        (loss, grad_w), grad_x = _jax.lax.scan(body, init, (per_example, given["loss_target"]))
    with _jax.named_scope("update"):
        delta_w, new_m, new_v = {}, {}, {}
        for n in TWIN_WEIGHTS:
            delta_w[n], new_m[n], new_v[n] = _adamw(weights[n], grad_w[n], given["m_" + n], given["v_" + n])
    return (loss, grad_x, *[grad_w[n] for n in TWIN_WEIGHTS], *[delta_w[n] for n in TWIN_WEIGHTS],
            *[new_m[n] for n in TWIN_WEIGHTS], *[new_v[n] for n in TWIN_WEIGHTS])
```

```python
import functools

import jax
import jax.numpy as jnp
from jax import lax
from jax.experimental import pallas as pl
from jax.experimental.pallas import tpu as pltpu

F32 = jnp.float32
BF16 = jnp.bfloat16
MESH_ID = pl.DeviceIdType.MESH
AXES = ("x", "y", "c")

EPS = 1e-6
LRU_C = 8.0
GRID_W = 64
ADAM_LR = 0.001
ADAM_B1 = 0.9
ADAM_B2 = 0.999
ADAM_EPS = 1e-08
ADAM_WD = 0.01
ADAM_STEP = 10

LANES = 128
SUBLANES = 8
PACK_COLS = 1024
VMEM_LIMIT = 56 * 2**20
LRU_GROUP = 256


def _params(sem=None):
    return pltpu.CompilerParams(vmem_limit_bytes=VMEM_LIMIT, dimension_semantics=sem)


def _pick(dim, pref, quantum=LANES):
    if dim <= pref:
        return dim
    best = None
    for t in range(quantum, pref + 1, quantum):
        if dim % t == 0:
            best = t
    assert best is not None, (dim, pref)
    return best


def _pos():
    return lax.axis_index("x"), lax.axis_index("y"), lax.axis_index("c")


def _flip(v, bit):
    return 1 - v if bit else v


def _sigmoid(v):
    return jax.nn.sigmoid(v)


def _silu(v):
    return v * _sigmoid(v)


def _dsilu(v):
    s = _sigmoid(v)
    return s * (1.0 + v * (1.0 - s))


def _neg_expm1(z):
    series = -z * (1.0 + z * (0.5 + z * (1.0 / 6.0 + z * (1.0 / 24.0))))
    return jnp.where(z > -0.03, series, 1.0 - jnp.exp(z))


def _softplus(z):
    return jnp.maximum(z, 0.0) + jnp.log1p(jnp.exp(-jnp.abs(z)))


def _matmul(a, b, *, ta=False, tb=False, tm=512, tn=512, tk=512, out_dtype=F32, name,
            a_rows=None, a_off=0, a_act=None, init=None, bias=None, dsilu_mul=None):
    rows_a = a.shape[0] if a_rows is None else a_rows
    if ta:
        K, M = rows_a, a.shape[1]
    else:
        M, K = rows_a, a.shape[1]
    N = b.shape[0] if tb else b.shape[1]
    tm, tn, tk = _pick(M, tm, SUBLANES), _pick(N, tn), _pick(K, tk)
    t_rows = tk if ta else tm
    assert a_off % t_rows == 0
    nk = K // tk
    off_blocks = a_off // t_rows
    dims = (((0 if ta else 1,), (1 if tb else 0,)), ((), ()))
    extras = [e for e in (init, bias, dsilu_mul) if e is not None]

    def body(a_ref, b_ref, *rest):
        rest = list(rest)
        init_ref = rest.pop(0) if init is not None else None
        bias_ref = rest.pop(0) if bias is not None else None
        dsm_ref = rest.pop(0) if dsilu_mul is not None else None
        o_ref, acc_ref = rest
        k = pl.program_id(2)

        @pl.when(k == 0)
        def _():
            if init_ref is None:
                acc_ref[...] = jnp.zeros_like(acc_ref)
            else:
                acc_ref[...] = init_ref[...]

        av = a_ref[...]
        if a_act == "silu":
            av = _silu(av)
        acc_ref[...] += lax.dot_general(av, b_ref[...], dims, preferred_element_type=F32)

        @pl.when(k == nk - 1)
        def _():
            r = acc_ref[...]
            if bias_ref is not None:
                r = r + bias_ref[...]
            if dsm_ref is not None:
                r = r * _dsilu(dsm_ref[...])
            o_ref[...] = r.astype(o_ref.dtype)

    if ta:
        a_spec = pl.BlockSpec((tk, tm), lambda i, j, k: (k + off_blocks, i))
    else:
        a_spec = pl.BlockSpec((tm, tk), lambda i, j, k: (i + off_blocks, k))
    if tb:
        b_spec = pl.BlockSpec((tn, tk), lambda i, j, k: (j, k))
    else:
        b_spec = pl.BlockSpec((tk, tn), lambda i, j, k: (k, j))
    in_specs = [a_spec, b_spec]
    if init is not None:
        in_specs.append(pl.BlockSpec((tm, tn), lambda i, j, k: (i, j)))
    if bias is not None:
        in_specs.append(pl.BlockSpec((1, tn), lambda i, j, k: (0, j)))
    if dsilu_mul is not None:
        in_specs.append(pl.BlockSpec((1, tn), lambda i, j, k: (0, j)))
    return pl.pallas_call(
        body, name=name, grid=(M // tm, N // tn, nk),
        in_specs=in_specs, out_specs=pl.BlockSpec((tm, tn), lambda i, j, k: (i, j)),
        out_shape=jax.ShapeDtypeStruct((M, N), out_dtype),
        scratch_shapes=[pltpu.VMEM((tm, tn), F32)],
        compiler_params=_params(("parallel", "parallel", "arbitrary")),
    )(a, b, *extras)


def _elementwise(fn, ins, outs, *, rows, cols, name, tr=256):
    tr = _pick(rows, tr, 2 * SUBLANES)
    n_in = len(ins)

    def body(*refs):
        vals = fn(*[r[...] for r in refs[:n_in]])
        if not isinstance(vals, (tuple, list)):
            vals = (vals,)
        for r, v in zip(refs[n_in:], vals, strict=True):
            r[...] = v.astype(r.dtype)

    def spec(off):
        assert off % tr == 0
        ob = off // tr
        return pl.BlockSpec((tr, cols), lambda i: (i + ob, 0))

    res = pl.pallas_call(
        body, name=name, grid=(rows // tr,),
        in_specs=[spec(off) for _, off in ins],
        out_specs=[spec(0) for _ in outs],
        out_shape=[jax.ShapeDtypeStruct((rows, cols), dt) for dt in outs],
        compiler_params=_params(("parallel",)),
    )(*[a for a, _ in ins])
    return res


def _adam_math(w, g, m, v):
    m = ADAM_B1 * m + (1.0 - ADAM_B1) * g
    v = ADAM_B2 * v + (1.0 - ADAM_B2) * (g * g)
    m_hat = m / (1.0 - ADAM_B1 ** ADAM_STEP)
    v_hat = v / (1.0 - ADAM_B2 ** ADAM_STEP)
    delta = -ADAM_LR * (m_hat / (jnp.sqrt(v_hat) + ADAM_EPS) + ADAM_WD * w)
    return delta, m, v


def _adam(w, g, m, v, name):
    rows, cols = w.shape
    return _elementwise(_adam_math, [(w, 0), (g, 0), (m, 0), (v, 0)], [F32, F32, F32],
                        rows=rows, cols=cols, name=name)


def _pack(arrs, row_quantum):
    flat = jnp.concatenate([a.reshape(-1).astype(F32) for a in arrs])
    n = flat.shape[0]
    q = row_quantum * PACK_COLS
    total = -(-n // q) * q
    flat = jnp.pad(flat, (0, total - n))
    return flat.reshape(total // PACK_COLS, PACK_COLS)


def _unpack(buf, shapes):
    flat = buf.reshape(-1)
    out, off = [], 0
    for s in shapes:
        n = 1
        for d in s:
            n *= d
        out.append(flat[off:off + n].reshape(s))
        off += n
    return out


def _allreduce8(buf, name):
    R, C = buf.shape
    assert R % (8 * SUBLANES) == 0
    m = R // 8

    def body(x_ref, o_ref, recv, red, s1, r1, s2, r2):
        x, y, c = _pos()
        me = 4 * x + 2 * y + c

        def peer(k):
            px, py, pc = _flip(x, (k >> 2) & 1), _flip(y, (k >> 1) & 1), _flip(c, k & 1)
            return (px, py, pc), 4 * px + 2 * py + pc

        def rows(ref, idx):
            return ref.at[pl.ds(pl.multiple_of(idx * m, SUBLANES), m), :]

        def scatter(k):
            dev, p = peer(k)
            return pltpu.make_async_remote_copy(src_ref=rows(x_ref, p), dst_ref=recv.at[k], send_sem=s1.at[k],
                                                recv_sem=r1.at[k], device_id=dev, device_id_type=MESH_ID)

        def share(k):
            dev, p = peer(k)
            return pltpu.make_async_remote_copy(src_ref=red, dst_ref=rows(o_ref, me), send_sem=s2.at[k],
                                                recv_sem=r2.at[k], device_id=dev, device_id_type=MESH_ID)

        def shared_from(k):
            dev, p = peer(k)
            return pltpu.make_async_remote_copy(src_ref=red, dst_ref=rows(o_ref, p), send_sem=s2.at[k],
                                                recv_sem=r2.at[k], device_id=dev, device_id_type=MESH_ID)

        for k in range(1, 8):
            scatter(k).start()
        acc = rows(x_ref, me)[...]
        for k in range(1, 8):
            scatter(k).wait_recv()
            acc = acc + recv[k]
        red[...] = acc
        rows(o_ref, me)[...] = acc
        for k in range(1, 8):
            share(k).start()
        for k in range(1, 8):
            shared_from(k).wait_recv()
        for k in range(1, 8):
            scatter(k).wait_send()
            share(k).wait_send()

    return pl.pallas_call(
        body, name=name,
        in_specs=[pl.BlockSpec(memory_space=pltpu.VMEM)],
        out_specs=pl.BlockSpec(memory_space=pltpu.VMEM),
        out_shape=jax.ShapeDtypeStruct((R, C), F32),
        scratch_shapes=[pltpu.VMEM((8, m, C), F32), pltpu.VMEM((m, C), F32),
                        pltpu.SemaphoreType.DMA((8,)), pltpu.SemaphoreType.DMA((8,)),
                        pltpu.SemaphoreType.DMA((8,)), pltpu.SemaphoreType.DMA((8,))],
        compiler_params=_params(),
    )(buf)


def _chunk(ref, axis, idx, size):
    start = idx * size
    if axis == 0:
        return ref.at[pl.ds(start, size), :]
    return ref.at[:, pl.ds(start, size)]


def _gather_weights(win, wout):
    D, nq = win.shape
    dq, D2 = wout.shape
    ops = ((0, 1, nq, D // 2), (1, 0, dq, dq // 2))

    def body(win_ref, wout_ref, gin_ref, gout_ref, lsem, ssem, rsem, fsem, gsem):
        x, y, c = _pos()
        q = 2 * x + y
        srcs = (win_ref, wout_ref)
        dsts = (gin_ref, gout_ref)

        def shard_window(o, chip):
            _, axis, size, _ = ops[o]
            return _chunk(dsts[o], axis, chip, size)

        def half(ref, o, core):
            return ref.at[pl.ds(core * ops[o][3], ops[o][3]), :]

        def half_window(o, chip, core):
            _, axis, size, hs = ops[o]
            if axis == 1:
                return dsts[o].at[pl.ds(core * hs, hs), pl.ds(chip * size, size)]
            return dsts[o].at[pl.ds(chip * size + core * hs, hs), :]

        def chip_of(k):
            px, py = _flip(x, (k >> 1) & 1), _flip(y, k & 1)
            return px, py, 2 * px + py

        local = [pltpu.make_async_copy(srcs[o], shard_window(o, q), lsem.at[o]) for o in range(2)]
        for cp in local:
            cp.start()
        sends, fwds = [], []
        for o in range(2):
            for k in range(1, 4):
                px, py, _ = chip_of(k)
                cp = pltpu.make_async_remote_copy(
                    src_ref=half(srcs[o], o, c), dst_ref=half_window(o, q, c), send_sem=ssem.at[o, k],
                    recv_sem=rsem.at[o, k], device_id=(px, py, c), device_id_type=MESH_ID)
                cp.start()
                sends.append(cp)
        for o in range(2):
            for k in range(1, 4):
                px, py, pq = chip_of(k)
                landed = half_window(o, pq, c)
                pltpu.make_async_remote_copy(src_ref=landed, dst_ref=landed, send_sem=ssem.at[o, k],
                                             recv_sem=rsem.at[o, k], device_id=(px, py, c),
                                             device_id_type=MESH_ID).wait_recv()
                fw = pltpu.make_async_remote_copy(src_ref=landed, dst_ref=landed, send_sem=fsem.at[o, k],
                                                  recv_sem=gsem.at[o, k], device_id=(x, y, 1 - c),
                                                  device_id_type=MESH_ID)
                fw.start()
                fwds.append(fw)
        for o in range(2):
            for k in range(1, 4):
                _, _, pq = chip_of(k)
                theirs = half_window(o, pq, 1 - c)
                pltpu.make_async_remote_copy(src_ref=theirs, dst_ref=theirs, send_sem=fsem.at[o, k],
                                             recv_sem=gsem.at[o, k], device_id=(x, y, 1 - c),
                                             device_id_type=MESH_ID).wait_recv()
        for cp in sends + fwds:
            cp.wait_send()
        for cp in local:
            cp.wait()

    hbm = pl.BlockSpec(memory_space=pl.ANY)
    return pl.pallas_call(
        body, name="gather_weights", in_specs=[hbm, hbm], out_specs=[hbm, hbm],
        out_shape=[jax.ShapeDtypeStruct((D, 4 * nq), win.dtype), jax.ShapeDtypeStruct((4 * dq, D2), wout.dtype)],
        scratch_shapes=[pltpu.SemaphoreType.DMA((2,))] + [pltpu.SemaphoreType.DMA((2, 4)) for _ in range(4)],
        compiler_params=_params(),
    )(win, wout)


def _rs_to_sibling(gs, axes):
    n = len(gs)
    shapes = []
    for g, ax in zip(gs, axes):
        s = list(g.shape)
        s[ax] //= 8
        shapes.append(tuple(s))

    def body(*refs):
        g_refs, mine, landed = refs[:n], refs[n:2 * n], refs[2 * n:3 * n]
        lsem, ssem, rsem = refs[3 * n:]
        x, y, c = _pos()
        cps = []
        for o in range(n):
            size = shapes[o][axes[o]]
            for j in range(4):
                lc = pltpu.make_async_copy(_chunk(g_refs[o], axes[o], 2 * j + c, size), mine[o].at[j], lsem.at[o, j])
                rc = pltpu.make_async_remote_copy(
                    src_ref=_chunk(g_refs[o], axes[o], 2 * j + 1 - c, size), dst_ref=landed[o].at[j],
                    send_sem=ssem.at[o, j], recv_sem=rsem.at[o, j], device_id=(x, y, 1 - c), device_id_type=MESH_ID)
                lc.start()
                rc.start()
                cps.append((lc, rc))
        for lc, rc in cps:
            lc.wait()
            rc.wait()

    hbm = pl.BlockSpec(memory_space=pl.ANY)
    outs = [jax.ShapeDtypeStruct((4,) + s, g.dtype) for s, g in zip(shapes, gs)]
    res = pl.pallas_call(
        body, name="rs_to_sibling", in_specs=[hbm] * n, out_specs=[hbm] * (2 * n), out_shape=outs + outs,
        scratch_shapes=[pltpu.SemaphoreType.DMA((n, 4)) for _ in range(3)],
        compiler_params=_params(),
    )(*gs)
    return res[:n], res[n:]


def _rs_to_chips(parts):
    n = len(parts)

    def body(*refs):
        p_refs, slots = refs[:n], refs[n:2 * n]
        lsem, ssem, rsem = refs[2 * n:]
        x, y, c = _pos()
        q = 2 * x + y
        cps = []
        for o in range(n):
            lc = pltpu.make_async_copy(p_refs[o].at[q], slots[o].at[0], lsem.at[o])
            lc.start()
            cps.append(lc)
            for k in range(1, 4):
                px, py = _flip(x, (k >> 1) & 1), _flip(y, k & 1)
                rc = pltpu.make_async_remote_copy(
                    src_ref=p_refs[o].at[2 * px + py], dst_ref=slots[o].at[k], send_sem=ssem.at[o, k],
                    recv_sem=rsem.at[o, k], device_id=(px, py, c), device_id_type=MESH_ID)
                rc.start()
                cps.append(rc)
        for cp in cps:
            cp.wait()

    hbm = pl.BlockSpec(memory_space=pl.ANY)
    return pl.pallas_call(
        body, name="rs_to_chips", in_specs=[hbm] * n, out_specs=[hbm] * n,
        out_shape=[jax.ShapeDtypeStruct(p.shape, p.dtype) for p in parts],
        scratch_shapes=[pltpu.SemaphoreType.DMA((n,)), pltpu.SemaphoreType.DMA((n, 4)), pltpu.SemaphoreType.DMA((n, 4))],
        compiler_params=_params(),
    )(*parts)


def _rs_share(rs, axes):
    n = len(rs)
    shapes = []
    for r, ax in zip(rs, axes):
        s = list(r.shape)
        s[ax] *= 2
        shapes.append(tuple(s))

    def body(*refs):
        r_refs, outs = refs[:n], refs[n:2 * n]
        lsem, ssem, rsem = refs[2 * n:]
        x, y, c = _pos()
        cps = []
        for o in range(n):
            size = r_refs[o].shape[axes[o]]
            window = _chunk(outs[o], axes[o], c, size)
            lc = pltpu.make_async_copy(r_refs[o], window, lsem.at[o])
            rc = pltpu.make_async_remote_copy(src_ref=r_refs[o], dst_ref=window, send_sem=ssem.at[o], recv_sem=rsem.at[o],
                                              device_id=(x, y, 1 - c), device_id_type=MESH_ID)
            lc.start()
            rc.start()
            cps += [lc, rc]
        for cp in cps:
            cp.wait()

    hbm = pl.BlockSpec(memory_space=pl.ANY)
    return pl.pallas_call(
        body, name="rs_share", in_specs=[hbm] * n, out_specs=[hbm] * n,
        out_shape=[jax.ShapeDtypeStruct(s, r.dtype) for s, r in zip(shapes, rs)],
        scratch_shapes=[pltpu.SemaphoreType.DMA((n,)) for _ in range(3)],
        compiler_params=_params(),
    )(*rs)


def _reduce_scatter(gs, axes):
    mine, landed = _rs_to_sibling(gs, axes)
    pair_sums = []
    for o, (mi, la) in enumerate(zip(mine, landed)):
        rows, cols = mi.shape[0] * mi.shape[1], mi.shape[2]
        s = _elementwise(lambda a, b: a.astype(F32) + b.astype(F32), [(mi.reshape(rows, cols), 0), (la.reshape(rows, cols), 0)],
                         [BF16], rows=rows, cols=cols, name=f"rs_pair_sum{o}")[0]
        pair_sums.append(s.reshape(mi.shape))
    slots = _rs_to_chips(pair_sums)
    reduced = []
    for o, sl in enumerate(slots):
        rows, cols = sl.shape[1], sl.shape[2]
        flat = sl.reshape(4 * rows, cols)
        r = _elementwise(lambda a, b, c, d: (a.astype(F32) + b.astype(F32)) + (c.astype(F32) + d.astype(F32)),
                         [(flat, k * rows) for k in range(4)], [F32], rows=rows, cols=cols, name=f"rs_chip_sum{o}")[0]
        reduced.append(r)
    return _rs_share(reduced, axes)


def _norm_in(x, ctx, g, sc_l, sh_l, sc_c, sh_c, tr):
    L, D = x.shape
    T = ctx.shape[0]
    nx, nc = L // tr, T // tr

    def body(x_ref, c_ref, g_ref, scl, shl, scc, shc, o_ref):
        i = pl.program_id(0)

        def run(src, sc, sh):
            v = src[...]
            r = lax.rsqrt(jnp.mean(v * v, axis=-1, keepdims=True) + EPS)
            o_ref[...] = ((v * r * g_ref[...]) * (1.0 + sc[...]) + sh[...]).astype(o_ref.dtype)

        @pl.when(i < nx)
        def _():
            run(x_ref, scl, shl)

        @pl.when(i >= nx)
        def _():
            run(c_ref, scc, shc)

    vec = pl.BlockSpec((1, D), lambda i: (0, 0))
    return pl.pallas_call(
        body, name="norm_in", grid=(nx + nc,),
        in_specs=[pl.BlockSpec((tr, D), lambda i: (jnp.minimum(i, nx - 1), 0)),
                  pl.BlockSpec((tr, D), lambda i: (jnp.maximum(i - nx, 0), 0)), vec, vec, vec, vec, vec],
        out_specs=pl.BlockSpec((tr, D), lambda i: (i, 0)),
        out_shape=jax.ShapeDtypeStruct((L + T, D), BF16),
        compiler_params=_params(("arbitrary",)),
    )(x, ctx, g, sc_l, sh_l, sc_c, sh_c)


def _tmod(tl, row_w):
    assert row_w & (row_w - 1) == 0
    return lax.broadcasted_iota(jnp.int32, (tl, 1), 0) & (row_w - 1)


def _shift(z, k, tmod, row_w):
    tl = z.shape[0]
    rolled = pltpu.roll(z, k % tl, 0)
    mask = (tmod >= k) if k > 0 else (tmod < row_w + k)
    return jnp.where(mask, rolled, 0.0)


def _conv(z, w_ref, taps, left, tmod, row_w):
    out = None
    for j in range(taps):
        k = left - j
        term = (z if k == 0 else _shift(z, k, tmod, row_w)) * w_ref[j:j + 1, :]
        out = term if out is None else out + term
    return out


def _conv_bwd(dz, z, w_ref, taps, left, tmod, row_w):
    din = None
    dws = []
    for j in range(taps):
        k = left - j
        term = (dz if k == 0 else _shift(dz, -k, tmod, row_w)) * w_ref[j:j + 1, :]
        din = term if din is None else din + term
        zs = z if k == 0 else _shift(z, k, tmod, row_w)
        dws.append(jnp.sum(dz * zs, axis=0, keepdims=True))
    return din, dws


def _coeffs(xb, wd_ref, ba_ref, bx_ref, lam_ref, pre_scr, ng, gs):
    W = xb.shape[1]
    xb16 = xb.astype(BF16)
    for g in range(ng):
        pg = jnp.dot(xb16[:, g * gs:(g + 1) * gs], wd_ref[g], preferred_element_type=F32)
        pre_scr[:, g * gs:(g + 1) * gs] = pg[:, :gs]
        pre_scr[:, W + g * gs:W + (g + 1) * gs] = pg[:, gs:]
    r = _sigmoid(pre_scr[:, :W] + ba_ref[...])
    ig = _sigmoid(pre_scr[:, W:] + bx_ref[...])
    sp = _softplus(-lam_ref[...])
    la = (-LRU_C) * r * sp
    a = jnp.exp(la)
    m = jnp.sqrt(_neg_expm1(2.0 * la))
    return r, ig, sp, a, m


def _row_loop(tl, rev, step, init):
    nchunk = tl // SUBLANES

    def chunk(j, carry):
        jj = (nchunk - 1 - j) if rev else j
        c0 = pl.multiple_of(jj * SUBLANES, SUBLANES)
        for r in (range(SUBLANES - 1, -1, -1) if rev else range(SUBLANES)):
            carry = step(c0 + r, carry)
        return carry

    return lax.fori_loop(0, nchunk, chunk, init)


def _mix_fwd(P, d, h_init, wts, *, rows, row_off, row_w, tl, h_other=None, name):
    W = P.shape[1] // 6
    nt = rows // tl
    ob = row_off // tl
    rev = d == 1
    gs = min(LRU_GROUP, W)
    ng = W // gs
    wca, wcb, bcb = wts["wca"], wts["wcb"], wts["bcb"]
    wd, ba, bx, lam = wts["wd"][d], wts["ba"][d], wts["bx"][d], wts["lam"][d]

    def tile(i):
        return (nt - 1 - i) if rev else i

    def pcol(j):
        return pl.BlockSpec((tl, W), lambda i: (tile(i) + ob, j))

    vec = pl.BlockSpec((1, W), lambda i: (0, 0))
    taps = pl.BlockSpec((SUBLANES, W), lambda i: (0, 0))
    wd_spec = pl.BlockSpec(wd.shape, lambda i: (0, 0, 0))
    seq = pl.BlockSpec((tl, W), lambda i: (tile(i), 0))

    def body(*refs):
        if rev:
            (bl, cl, ul, gl, vl, ql, ho, wca_r, wcb_r, bcb_r, wd_r, ba_r, bx_r, lam_r, hin, hseq, cat,
             a_scr, b_scr, pre_scr, carry) = refs
        else:
            vl, wcb_r, bcb_r, wd_r, ba_r, bx_r, lam_r, hin, hseq, a_scr, b_scr, pre_scr, carry = refs
        i = pl.program_id(0)

        @pl.when(i == 0)
        def _():
            carry[...] = hin[...]

        tmod = _tmod(tl, row_w)
        xb = _conv(vl[...], wcb_r, 4, 2, tmod, row_w) + bcb_r[...]
        r, ig, sp, a, m = _coeffs(xb, wd_r, ba_r, bx_r, lam_r, pre_scr, ng, gs)
        a_scr[...] = a
        b_scr[...] = m * (ig * xb)

        def step(t, h):
            h = a_scr[pl.ds(t, 1), :] * h + b_scr[pl.ds(t, 1), :]
            hseq[pl.ds(t, 1), :] = h
            return h

        carry[...] = _row_loop(tl, rev, step, carry[...])

        if rev:
            ylru = ho[...] + hseq[...]
            yb = ylru * _silu(ql[...])
            ya = bl[...] * _conv(cl[...] * ul[...], wca_r, 3, 1, tmod, row_w) * _silu(gl[...])
            cat[:, :W] = ya.astype(cat.dtype)
            cat[:, W:] = yb.astype(cat.dtype)

    scratch = [pltpu.VMEM((tl, W), F32), pltpu.VMEM((tl, W), F32), pltpu.VMEM((tl, 2 * W), F32), pltpu.VMEM((1, W), F32)]
    if rev:
        in_specs = [pcol(j) for j in range(6)] + [seq, taps, taps, vec, wd_spec, vec, vec, vec, vec]
        args = [P] * 6 + [h_other, wca, wcb, bcb, wd, ba, bx, lam, h_init]
        out_specs = [seq, pl.BlockSpec((tl, 2 * W), lambda i: (tile(i), 0))]
        out_shape = [jax.ShapeDtypeStruct((rows, W), F32), jax.ShapeDtypeStruct((rows, 2 * W), BF16)]
    else:
        in_specs = [pcol(4), taps, vec, wd_spec, vec, vec, vec, vec]
        args = [P, wcb, bcb, wd, ba, bx, lam, h_init]
        out_specs = [seq]
        out_shape = [jax.ShapeDtypeStruct((rows, W), F32)]
    return pl.pallas_call(
        body, name=name, grid=(nt,), in_specs=in_specs, out_specs=out_specs, out_shape=out_shape,
        scratch_shapes=scratch, compiler_params=_params(("arbitrary",)),
    )(*args)


def _lru_bwd_tile(d, xb, dyl, hseq_v, hedge, c_scr, wd_r, ba_r, bx_r, lam_r, a_scr, d_scr, g_scr, pre_scr,
                  acc, first, tl, ng, gs):
    W = xb.shape[1]
    rev = d == 0
    r, ig, sp, a, m = _coeffs(xb, wd_r, ba_r, bx_r, lam_r, pre_scr, ng, gs)
    a_scr[...] = a
    d_scr[...] = dyl

    def step(t, c):
        g = d_scr[pl.ds(t, 1), :] + c
        g_scr[pl.ds(t, 1), :] = g
        return a_scr[pl.ds(t, 1), :] * g

    c_scr[...] = _row_loop(tl, rev, step, c_scr[...])
    g = g_scr[...]
    row = lax.broadcasted_iota(jnp.int32, (tl, 1), 0)
    if d == 0:
        hprev = jnp.where(row == 0, hedge, pltpu.roll(hseq_v, 1, 0))
    else:
        hprev = jnp.where(row == tl - 1, hedge, pltpu.roll(hseq_v, tl - 1, 0))
    u = ig * xb
    d_u = g * m
    d_la = (g * hprev) * a - (g * u) * (a * a) / m
    d_ig = d_u * xb
    dxb = d_u * ig
    d_pr = d_la * ((-LRU_C) * sp) * (r * (1.0 - r))
    d_pi = d_ig * (ig * (1.0 - ig))
    dsp = jnp.sum(d_la * ((-LRU_C) * r), axis=0, keepdims=True)
    dlam = dsp * (-_sigmoid(-lam_r[...]))
    dba = jnp.sum(d_pr, axis=0, keepdims=True)
    dbx = jnp.sum(d_pi, axis=0, keepdims=True)
    dwd_ref, dba_ref, dbx_ref, dlam_ref = acc

    @pl.when(first)
    def _():
        dwd_ref[...] = jnp.zeros_like(dwd_ref)
        dba_ref[...] = jnp.zeros_like(dba_ref)
        dbx_ref[...] = jnp.zeros_like(dbx_ref)
        dlam_ref[...] = jnp.zeros_like(dlam_ref)

    dba_ref[...] += dba
    dbx_ref[...] += dbx
    dlam_ref[...] += dlam
    xb16 = xb.astype(BF16)
    dpr16 = d_pr.astype(BF16)
    dpi16 = d_pi.astype(BF16)
    parts = []
    for gi in range(ng):
        sl = slice(gi * gs, (gi + 1) * gs)
        dp = jnp.concatenate([dpr16[:, sl], dpi16[:, sl]], axis=1)
        parts.append(lax.dot_general(dp, wd_r[gi], (((1,), (1,)), ((), ())), preferred_element_type=F32))
        dwd_ref[gi] += lax.dot_general(xb16[:, sl], dp, (((0,), (0,)), ((), ())), preferred_element_type=F32)
    return dxb + (parts[0] if ng == 1 else jnp.concatenate(parts, axis=1))


def _edge_block(h, tl, nt, d):
    W = h.shape[1]
    per = tl // SUBLANES
    if d == 0:
        return pl.BlockSpec((SUBLANES, W), lambda i: (jnp.maximum((nt - 1 - i) * per - 1, 0), 0))
    return pl.BlockSpec((SUBLANES, W), lambda i: (jnp.minimum((i + 1) * per, nt * per - 1), 0))


def _mix_bwd0(P, dcat, h0s, h_init, c_init, wts, *, rows, row_off, row_w, tl, name):
    W = P.shape[1] // 6
    nt = rows // tl
    ob = row_off // tl
    gs = min(LRU_GROUP, W)
    ng = W // gs
    wcb, bcb = wts["wcb"], wts["bcb"]
    wd, ba, bx, lam = wts["wd"][0], wts["ba"][0], wts["bx"][0], wts["lam"][0]

    def tile(i):
        return nt - 1 - i

    vec = pl.BlockSpec((1, W), lambda i: (0, 0))
    taps = pl.BlockSpec((SUBLANES, W), lambda i: (0, 0))
    wd_spec = pl.BlockSpec(wd.shape, lambda i: (0, 0, 0))
    seq = pl.BlockSpec((tl, W), lambda i: (tile(i), 0))

    def body(vl, ql, dyb, hs, hedge8, wcb_r, bcb_r, wd_r, ba_r, bx_r, lam_r, hin, cin,
             dxb_o, dwd_o, dba_o, dbx_o, dlam_o, cfin, a_scr, d_scr, g_scr, pre_scr, c_scr):
        i = pl.program_id(0)

        @pl.when(i == 0)
        def _():
            c_scr[...] = cin[...]

        tmod = _tmod(tl, row_w)
        xb = _conv(vl[...], wcb_r, 4, 2, tmod, row_w) + bcb_r[...]
        dyl = dyb[...] * _silu(ql[...])
        hedge = jnp.where(i == nt - 1, hin[...], hedge8[SUBLANES - 1:SUBLANES, :])
        dxb_o[...] = _lru_bwd_tile(0, xb, dyl, hs[...], hedge, c_scr, wd_r, ba_r, bx_r, lam_r, a_scr, d_scr, g_scr,
                                   pre_scr, (dwd_o, dba_o, dbx_o, dlam_o), i == 0, tl, ng, gs)
        cfin[...] = c_scr[...]

    return pl.pallas_call(
        body, name=name, grid=(nt,),
        in_specs=[pl.BlockSpec((tl, W), lambda i: (tile(i) + ob, 4)), pl.BlockSpec((tl, W), lambda i: (tile(i) + ob, 5)),
                  pl.BlockSpec((tl, W), lambda i: (tile(i), 1)), seq, _edge_block(h0s, tl, nt, 0),
                  taps, vec, wd_spec, vec, vec, vec, vec, vec],
        out_specs=[seq, wd_spec, vec, vec, vec, vec],
        out_shape=[jax.ShapeDtypeStruct((rows, W), F32), jax.ShapeDtypeStruct(wd.shape, F32)]
        + [jax.ShapeDtypeStruct((1, W), F32)] * 4,
        scratch_shapes=[pltpu.VMEM((tl, W), F32)] * 3 + [pltpu.VMEM((tl, 2 * W), F32), pltpu.VMEM((1, W), F32)],
        compiler_params=_params(("arbitrary",)),
    )(P, P, dcat, h0s, h0s, wcb, bcb, wd, ba, bx, lam, h_init, c_init)


def _mix_bwd1(P, dcat, h0s, h1s, dxb0, h_init, c_init, wts, *, rows, row_off, row_w, tl, name):
    W = P.shape[1] // 6
    nt = rows // tl
    ob = row_off // tl
    gs = min(LRU_GROUP, W)
    ng = W // gs
    wca, wcb, bcb = wts["wca"], wts["wcb"], wts["bcb"]
    wd, ba, bx, lam = wts["wd"][1], wts["ba"][1], wts["bx"][1], wts["lam"][1]

    vec = pl.BlockSpec((1, W), lambda i: (0, 0))
    taps = pl.BlockSpec((SUBLANES, W), lambda i: (0, 0))
    wd_spec = pl.BlockSpec(wd.shape, lambda i: (0, 0, 0))
    seq = pl.BlockSpec((tl, W), lambda i: (i, 0))

    def body(bl, cl, ul, gl, vl, ql, dya, dyb, h0, h1, hedge8, dx0, wca_r, wcb_r, bcb_r, wd_r, ba_r, bx_r, lam_r, hin, cin,
             dp_o, dwd_o, dba_o, dbx_o, dlam_o, dwca_o, dwcb_o, dbcb_o, cfin, a_scr, d_scr, g_scr, pre_scr, c_scr):
        i = pl.program_id(0)

        @pl.when(i == 0)
        def _():
            c_scr[...] = cin[...]
            dwca_o[...] = jnp.zeros_like(dwca_o)
            dwcb_o[...] = jnp.zeros_like(dwcb_o)
            dbcb_o[...] = jnp.zeros_like(dbcb_o)

        tmod = _tmod(tl, row_w)
        v = vl[...]
        q = ql[...]
        xb = _conv(v, wcb_r, 4, 2, tmod, row_w) + bcb_r[...]
        sq = _sigmoid(q)
        dyl = dyb[...] * (q * sq)
        hedge = jnp.where(i == nt - 1, hin[...], hedge8[0:1, :])
        dxb = dx0[...] + _lru_bwd_tile(1, xb, dyl, h1[...], hedge, c_scr, wd_r, ba_r, bx_r, lam_r, a_scr, d_scr, g_scr,
                                       pre_scr, (dwd_o, dba_o, dbx_o, dlam_o), i == 0, tl, ng, gs)
        cfin[...] = c_scr[...]
        dv, dwb = _conv_bwd(dxb, v, wcb_r, 4, 2, tmod, row_w)
        for j in range(4):
            dwcb_o[j:j + 1, :] += dwb[j]
        dbcb_o[...] += jnp.sum(dxb, axis=0, keepdims=True)
        dq = dyb[...] * (h0[...] + h1[...]) * (sq * (1.0 + q * (1.0 - sq)))
        b_, c_, u_, g_ = bl[...], cl[...], ul[...], gl[...]
        z = c_ * u_
        cz = _conv(z, wca_r, 3, 1, tmod, row_w)
        sgm = _sigmoid(g_)
        sg = g_ * sgm
        da = dya[...]
        dz, dwa = _conv_bwd(da * b_ * sg, z, wca_r, 3, 1, tmod, row_w)
        for j in range(3):
            dwca_o[j:j + 1, :] += dwa[j]
        dp_o[:, 0 * W:1 * W] = (da * cz * sg).astype(dp_o.dtype)
        dp_o[:, 1 * W:2 * W] = (dz * u_).astype(dp_o.dtype)
        dp_o[:, 2 * W:3 * W] = (dz * c_).astype(dp_o.dtype)
        dp_o[:, 3 * W:4 * W] = (da * b_ * cz * (sgm * (1.0 + g_ * (1.0 - sgm)))).astype(dp_o.dtype)
        dp_o[:, 4 * W:5 * W] = dv.astype(dp_o.dtype)
        dp_o[:, 5 * W:6 * W] = dq.astype(dp_o.dtype)

    def pcol(j):
        return pl.BlockSpec((tl, W), lambda i: (i + ob, j))

    return pl.pallas_call(
        body, name=name, grid=(nt,),
        in_specs=[pcol(j) for j in range(6)]
        + [pl.BlockSpec((tl, W), lambda i: (i, 0)), pl.BlockSpec((tl, W), lambda i: (i, 1)), seq, seq,
           _edge_block(h1s, tl, nt, 1), seq, taps, taps, vec, wd_spec, vec, vec, vec, vec, vec],
        out_specs=[pl.BlockSpec((tl, 6 * W), lambda i: (i, 0)), wd_spec, vec, vec, vec, taps, taps, vec, vec],
        out_shape=[jax.ShapeDtypeStruct((rows, 6 * W), BF16), jax.ShapeDtypeStruct(wd.shape, F32)]
        + [jax.ShapeDtypeStruct((1, W), F32)] * 3
        + [jax.ShapeDtypeStruct((SUBLANES, W), F32)] * 2 + [jax.ShapeDtypeStruct((1, W), F32)] * 2,
        scratch_shapes=[pltpu.VMEM((tl, W), F32)] * 3 + [pltpu.VMEM((tl, 2 * W), F32), pltpu.VMEM((1, W), F32)],
        compiler_params=_params(("arbitrary",)),
    )(*([P] * 6), dcat, dcat, h0s, h1s, h1s, dxb0, wca, wcb, bcb, wd, ba, bx, lam, h_init, c_init)


def _loss_head(out, x, tgt, gt, fg, tr):
    L, D = x.shape

    def body(o_ref, x_ref, t_ref, gt_ref, fg_ref, dn_o, do_o, dfg_o, dgt_o, loss_o):
        i = pl.program_id(0)

        @pl.when(i == 0)
        def _():
            dfg_o[...] = jnp.zeros_like(dfg_o)
            dgt_o[...] = jnp.zeros_like(dgt_o)
            loss_o[...] = jnp.zeros_like(loss_o)

        o = o_ref[...]
        gt_v = gt_ref[...]
        fg_v = fg_ref[...]
        n = x_ref[...] + gt_v * o
        r = lax.rsqrt(jnp.mean(n * n, axis=-1, keepdims=True) + EPS)
        nr = n * r
        e = nr * fg_v - t_ref[...]
        loss_o[...] += 0.5 * jnp.sum(jnp.mean(e * e, axis=-1, keepdims=True))
        dy = e * (1.0 / D)
        dfg_o[...] += jnp.sum(dy * nr, axis=0, keepdims=True)
        qv = dy * fg_v
        dn = r * (qv - nr * jnp.mean(qv * nr, axis=-1, keepdims=True))
        dgt_o[...] += jnp.sum(dn * o, axis=0, keepdims=True)
        dn_o[...] = dn
        do_o[...] = (dn * gt_v).astype(do_o.dtype)

    blk = pl.BlockSpec((tr, D), lambda i: (i, 0))
    vec = pl.BlockSpec((1, D), lambda i: (0, 0))
    return pl.pallas_call(
        body, name="loss_head", grid=(L // tr,), in_specs=[blk, blk, blk, vec, vec],
        out_specs=[blk, blk, vec, vec, pl.BlockSpec((SUBLANES, LANES), lambda i: (0, 0))],
        out_shape=[jax.ShapeDtypeStruct((L, D), F32), jax.ShapeDtypeStruct((L, D), BF16),
                   jax.ShapeDtypeStruct((1, D), F32), jax.ShapeDtypeStruct((1, D), F32),
                   jax.ShapeDtypeStruct((SUBLANES, LANES), F32)],
        compiler_params=_params(("arbitrary",)),
    )(out, x, tgt, gt, fg)


def _norm_bwd(dhl, x, dn, g, sc, tr, name):
    L, D = x.shape
    with_x = dn is not None

    def body(*refs):
        if with_x:
            d_ref, x_ref, dn_ref, g_ref, sc_ref, gx_o, dsh_o, dsc_o, dg_o = refs
        else:
            d_ref, x_ref, g_ref, sc_ref, dsh_o, dsc_o, dg_o = refs
        i = pl.program_id(0)

        @pl.when(i == 0)
        def _():
            dsh_o[...] = jnp.zeros_like(dsh_o)
            dsc_o[...] = jnp.zeros_like(dsc_o)
            dg_o[...] = jnp.zeros_like(dg_o)

        d = d_ref[...]
        xv = x_ref[...]
        g_v = g_ref[...]
        r = lax.rsqrt(jnp.mean(xv * xv, axis=-1, keepdims=True) + EPS)
        xr = xv * r
        dsh_o[...] += jnp.sum(d, axis=0, keepdims=True)
        dsc_o[...] += jnp.sum(d * (xr * g_v), axis=0, keepdims=True)
        dxn = d * (1.0 + sc_ref[...])
        dg_o[...] += jnp.sum(dxn * xr, axis=0, keepdims=True)
        if with_x:
            qv = dxn * g_v
            gx_o[...] = r * (qv - xr * jnp.mean(qv * xr, axis=-1, keepdims=True)) + dn_ref[...]

    blk = pl.BlockSpec((tr, D), lambda i: (i, 0))
    vec = pl.BlockSpec((1, D), lambda i: (0, 0))
    vshape = jax.ShapeDtypeStruct((1, D), F32)
    res = pl.pallas_call(
        body, name=name, grid=(L // tr,),
        in_specs=[blk, blk] + ([blk] if with_x else []) + [vec, vec],
        out_specs=([blk] if with_x else []) + [vec, vec, vec],
        out_shape=([jax.ShapeDtypeStruct((L, D), F32)] if with_x else []) + [vshape] * 3,
        compiler_params=_params(("arbitrary",)),
    )(*([dhl, x] + ([dn] if with_x else []) + [g, sc]))
    return res if with_x else [None] + list(res)


def _pack_blockdiag(wa, wx, gs):
    H, hd, _ = wa.shape
    hp = gs // hd
    ng = H // hp
    eye = jnp.eye(hp, dtype=wa.dtype)

    def bd(w):
        return jnp.einsum("gpij,pq->gpiqj", w.reshape(ng, hp, hd, hd), eye).reshape(ng, gs, gs)

    return jnp.concatenate([bd(wa), bd(wx)], axis=-1).astype(BF16)


def _unpack_blockdiag(dwd, H, hd, gs):
    hp = gs // hd
    ng = H // hp
    eye = jnp.eye(hp, dtype=dwd.dtype)

    def diag(dm):
        return jnp.einsum("gpiqj,pq->gpij", dm.reshape(ng, hp, hd, hp, hd), eye).reshape(H, hd, hd)

    return diag(dwd[:, :, :gs]), diag(dwd[:, :, gs:])


def kernel(x, c, ctx, c_ctx, norm_g, w_ada, b_ada, w_in, w_conv_a, w_conv_b, b_conv_b, lru_wa, lru_ba, lru_wx, lru_bx, lru_lambda, w_out, final_g, loss_target, m_c_ctx, m_norm_g, m_w_ada, m_b_ada, m_w_in, m_w_conv_a, m_w_conv_b, m_b_conv_b, m_lru_wa, m_lru_ba, m_lru_wx, m_lru_bx, m_lru_lambda, m_w_out, m_final_g, v_c_ctx, v_norm_g, v_w_ada, v_b_ada, v_w_in, v_w_conv_a, v_w_conv_b, v_b_conv_b, v_lru_wa, v_lru_ba, v_lru_wx, v_lru_bx, v_lru_lambda, v_w_out, v_final_g):
    xi, yi, ci = _pos()
    me = 4 * xi + 2 * yi + ci
    q = 2 * xi + yi
    first_core = (ci == 0).astype(F32)

    L, D = x.shape[1], x.shape[2]
    T = ctx.shape[1]
    W = D // 2
    Wq = W // 4
    H, hd = lru_wa.shape[2], lru_wa.shape[3]
    gs = min(LRU_GROUP, W)
    nq = w_ada.shape[2]
    tl = min(256, T, L)
    tr = min(256, T, L)
    x2, ctx2, tgt2 = x[0], ctx[0], loss_target[0]

    win_full, wout_full = _gather_weights(w_in[0].astype(BF16), w_out[0].astype(BF16))

    def place(shard, full_cols):
        z = jnp.zeros((shard.shape[0], full_cols), F32)
        return lax.dynamic_update_slice(z, shard * first_core, (0, q * shard.shape[1]))

    c_rows = lax.dynamic_update_slice(jnp.zeros((8, D), F32), c, (me, 0))
    small_in = [c_rows, place(w_conv_a[0], W), place(w_conv_b[0], W), place(lru_ba[0], W), place(lru_bx[0], W),
                place(lru_lambda[0], W)]
    small_shapes = [a.shape for a in small_in]
    gathered = _allreduce8(_pack(small_in, 8 * SUBLANES), "gather_small")
    c_all, wca, wcb, ba_all, bx_all, lam_all = _unpack(gathered, small_shapes)

    s_rows = jnp.concatenate([c_all, c_ctx[None, :], jnp.zeros((7, D), F32)], axis=0)
    mod_part = _matmul(s_rows, w_ada[0], a_act="silu", bias=lax.dynamic_slice(b_ada, (0, q * nq), (1, nq)),
                       tm=16, tn=nq, tk=512, name="ada_fwd")
    mod_all = _allreduce8(_pack([place(mod_part, 4 * nq)], 8 * SUBLANES), "gather_mod")
    mod_all = _unpack(mod_all, [(16, 4 * nq)])[0]
    mod_l = lax.dynamic_slice(mod_all, (me, 0), (1, 3 * D))
    mod_c = mod_all[8:9]
    sh_l, sc_l, gt_l = mod_l[:, :D], mod_l[:, D:2 * D], mod_l[:, 2 * D:]
    sh_c, sc_c = mod_c[:, :D], mod_c[:, D:2 * D]

    pad_taps = lambda w: jnp.pad(w, ((0, SUBLANES - w.shape[0]), (0, 0)))
    wts = {
        "wca": pad_taps(wca), "wcb": pad_taps(wcb), "bcb": b_conv_b,
        "wd": [_pack_blockdiag(lru_wa[0, d], lru_wx[0, d], gs) for d in range(2)],
        "ba": [ba_all[d:d + 1] for d in range(2)], "bx": [bx_all[d:d + 1] for d in range(2)],
        "lam": [lam_all[d:d + 1] for d in range(2)],
    }

    hl = _norm_in(x2, ctx2, norm_g, sc_l, sh_l, sc_c, sh_c, tr)
    p_lat = _matmul(hl, win_full, a_rows=L, tm=1024, tn=1536, tk=D, name="in_proj")
    p_ctx = _matmul(hl, win_full, a_rows=T, a_off=L, tm=T, tn=1536, tk=D, name="in_proj_ctx")
    zero_w = jnp.zeros((1, W), F32)
    c0s = _mix_fwd(p_ctx, 0, zero_w, wts, rows=T, row_off=0, row_w=T, tl=tl, name="ctx_fwd0")[0]
    c1s, _ = _mix_fwd(p_ctx, 1, zero_w, wts, rows=T, row_off=0, row_w=T, tl=tl, h_other=c0s, name="ctx_fwd1")
    h0_init, h1_init = c0s[T - 1:T], c1s[0:1]
    h0s = _mix_fwd(p_lat, 0, h0_init, wts, rows=L, row_off=0, row_w=GRID_W, tl=tl, name="mix_fwd0")[0]
    h1s, cat = _mix_fwd(p_lat, 1, h1_init, wts, rows=L, row_off=0, row_w=GRID_W, tl=tl, h_other=h0s, name="mix_fwd1")
    out = _matmul(cat, wout_full, tm=1024, tn=D, tk=512, name="out_proj")
    dn, dout, dfg, dgt, loss_blk = _loss_head(out, x2, tgt2, gt_l, final_g[None, :], tr)

    dcat = _matmul(dout, wout_full, tb=True, tm=1024, tn=2 * W, tk=512, name="out_proj_bwd")
    gw_out = _matmul(cat, dout, ta=True, tm=1024, tn=D, tk=512, out_dtype=BF16, name="w_out_grad")
    dxb0, dwd0, dba0, dbx0, dlam0, ch0 = _mix_bwd0(p_lat, dcat, h0s, h0_init, zero_w, wts, rows=L, row_off=0,
                                                   row_w=GRID_W, tl=tl, name="mix_bwd0")
    dp, dwd1, dba1, dbx1, dlam1, dwca, dwcb, dbcb, ch1 = _mix_bwd1(
        p_lat, dcat, h0s, h1s, dxb0, h1_init, zero_w, wts, rows=L, row_off=0, row_w=GRID_W, tl=tl, name="mix_bwd1")
    zero_cat = jnp.zeros((T, 2 * W), F32)
    cxb0, cwd0, cba0, cbx0, clam0, _ = _mix_bwd0(p_ctx, zero_cat, c0s, zero_w, ch0, wts, rows=T, row_off=0,
                                                 row_w=T, tl=tl, name="ctx_bwd0")
    dp_c, cwd1, cba1, cbx1, clam1, cwca, cwcb, cbcb, _ = _mix_bwd1(
        p_ctx, zero_cat, c0s, c1s, cxb0, zero_w, ch1, wts, rows=T, row_off=0, row_w=T, tl=tl, name="ctx_bwd1")

    dhl = _matmul(dp, win_full, tb=True, tm=512, tn=D, tk=512, name="in_proj_bwd")
    dhc = _matmul(dp_c, win_full, tb=True, tm=T, tn=D, tk=512, name="in_proj_bwd_ctx")
    gx, dsh_l, dsc_l, dng_l = _norm_bwd(dhl, x2, dn, norm_g, sc_l, tr, "norm_bwd")
    _, dsh_c, dsc_c, dng_c = _norm_bwd(dhc, ctx2, None, norm_g, sc_c, tr, "norm_bwd_ctx")
    gw_in_ctx = _matmul(hl, dp_c, ta=True, a_rows=T, a_off=L, tm=1024, tn=1536, tk=T, name="w_in_grad_ctx")
    gw_in = _matmul(hl, dp, ta=True, a_rows=L, tm=1024, tn=1536, tk=1024, init=gw_in_ctx, out_dtype=BF16,
                    name="w_in_grad")

    g_in_shard, g_out_shard = _reduce_scatter([gw_in, gw_out], [1, 0])

    dwa0, dwx0 = _unpack_blockdiag(dwd0 + cwd0, H, hd, gs)
    dwa1, dwx1 = _unpack_blockdiag(dwd1 + cwd1, H, hd, gs)
    zeros_d = jnp.zeros((1, D), F32)
    dmod_l = jnp.concatenate([dsh_l, dsc_l, dgt], axis=1)
    dmod_c = jnp.concatenate([dsh_c, dsc_c, zeros_d], axis=1)
    small_g = [
        lax.dynamic_update_slice(jnp.zeros((8, 3 * D), F32), dmod_l, (me, 0)), dmod_c,
        dfg, dng_l + dng_c, (dwca + cwca)[:3], (dwcb + cwcb)[:4], dbcb + cbcb,
        jnp.stack([dwa0, dwa1]), jnp.stack([dwx0, dwx1]),
        jnp.concatenate([dba0 + cba0, dba1 + cba1], axis=0), jnp.concatenate([dbx0 + cbx0, dbx1 + cbx1], axis=0),
        jnp.concatenate([dlam0 + clam0, dlam1 + clam1], axis=0),
    ]
    g_shapes = [a.shape for a in small_g]
    (g_rows, g_modc, g_fg, g_ng, g_wca, g_wcb, g_bcb, g_wa, g_wx, g_ba, g_bx, g_lam) = _unpack(
        _allreduce8(_pack(small_g, 8 * SUBLANES), "reduce_small"), g_shapes)

    g_mod = jnp.concatenate([g_rows, g_modc, jnp.zeros((7, 3 * D), F32)], axis=0)
    g_mod_q = lax.dynamic_slice(g_mod, (0, q * nq), (16, nq))
    g_w_ada = _matmul(s_rows, g_mod_q, ta=True, a_act="silu", tm=1024, tn=nq, tk=16, name="w_ada_grad")
    g_b_ada = jnp.sum(g_mod[:9], axis=0, keepdims=True)
    gc_part = _matmul(jnp.pad(lax.dynamic_slice(g_modc, (0, q * nq), (1, nq)), ((0, 7), (0, 0))), w_ada[0], tb=True,
                      dsilu_mul=c_ctx[None, :], tm=8, tn=D, tk=512, name="c_ctx_grad")
    g_c_ctx = _unpack(_allreduce8(_pack([gc_part[0:1] * first_core], 8 * SUBLANES), "reduce_c_ctx"), [(D,)])[0]

    def shard_cols(a, width):
        return lax.dynamic_slice(a, (0, q * width), (a.shape[0], width))

    grads = {
        "c_ctx": g_c_ctx, "norm_g": g_ng, "b_ada": g_b_ada,
        "w_conv_a": shard_cols(g_wca, Wq)[None], "w_conv_b": shard_cols(g_wcb, Wq)[None], "b_conv_b": g_bcb,
        "lru_wa": g_wa[None], "lru_ba": shard_cols(g_ba, Wq)[None], "lru_wx": g_wx[None],
        "lru_bx": shard_cols(g_bx, Wq)[None], "lru_lambda": shard_cols(g_lam, Wq)[None], "final_g": g_fg[0],
    }
    small_names = list(grads)
    given = dict(c_ctx=(c_ctx, m_c_ctx, v_c_ctx), norm_g=(norm_g, m_norm_g, v_norm_g), b_ada=(b_ada, m_b_ada, v_b_ada),
                 w_conv_a=(w_conv_a, m_w_conv_a, v_w_conv_a), w_conv_b=(w_conv_b, m_w_conv_b, v_w_conv_b),
                 b_conv_b=(b_conv_b, m_b_conv_b, v_b_conv_b), lru_wa=(lru_wa, m_lru_wa, v_lru_wa),
                 lru_ba=(lru_ba, m_lru_ba, v_lru_ba), lru_wx=(lru_wx, m_lru_wx, v_lru_wx),
                 lru_bx=(lru_bx, m_lru_bx, v_lru_bx), lru_lambda=(lru_lambda, m_lru_lambda, v_lru_lambda),
                 final_g=(final_g, m_final_g, v_final_g))
    shapes = [given[n][0].shape for n in small_names]
    packed = [_pack([given[n][j] for n in small_names], 2 * SUBLANES) for j in range(3)]
    packed_g = _pack([grads[n] for n in small_names], 2 * SUBLANES)
    sd, sm, sv = _adam(packed[0], packed_g, packed[1], packed[2], "adam_small")
    delta_s = dict(zip(small_names, _unpack(sd, shapes)))
    newm_s = dict(zip(small_names, _unpack(sm, shapes)))
    newv_s = dict(zip(small_names, _unpack(sv, shapes)))
    grads = {n: grads[n].reshape(given[n][0].shape) for n in small_names}

    big = {"w_ada": (w_ada, g_w_ada, m_w_ada, v_w_ada), "w_in": (w_in, g_in_shard, m_w_in, v_w_in),
           "w_out": (w_out, g_out_shard, m_w_out, v_w_out)}
    delta_b, newm_b, newv_b = {}, {}, {}
    for n, (w, g, m, v) in big.items():
        grads[n] = g[None]
        d_, m_, v_ = _adam(w[0], g, m[0], v[0], "adam_" + n)
        delta_b[n], newm_b[n], newv_b[n] = d_[None], m_[None], v_[None]

    loss = lax.psum(loss_blk[0, 0], AXES)
    order = ["c_ctx", "norm_g", "w_ada", "b_ada", "w_in", "w_conv_a", "w_conv_b", "b_conv_b", "lru_wa", "lru_ba",
             "lru_wx", "lru_bx", "lru_lambda", "w_out", "final_g"]
    delta = {**delta_s, **delta_b}
    newm = {**newm_s, **newm_b}
    newv = {**newv_s, **newv_b}
    return (loss, gx[None], *[grads[n] for n in order], *[delta[n] for n in order], *[newm[n] for n in order],
            *[newv[n] for n in order])
```

```python
import functools

import jax
import jax.numpy as jnp
from jax import lax
from jax.experimental import pallas as pl
from jax.experimental.pallas import tpu as pltpu

F32 = jnp.float32
BF16 = jnp.bfloat16
MESH_ID = pl.DeviceIdType.MESH
AXES = ("x", "y", "c")

EPS = 1e-6
LRU_C = 8.0
GRID_W = 64
ADAM_LR = 0.001
ADAM_B1 = 0.9
ADAM_B2 = 0.999
ADAM_EPS = 1e-08
ADAM_WD = 0.01
ADAM_STEP = 10

LANES = 128
SUBLANES = 8
PACK_COLS = 1024
VMEM_LIMIT = 56 * 2**20
LRU_GROUP = 256


def _params(sem=None):
    return pltpu.CompilerParams(vmem_limit_bytes=VMEM_LIMIT, dimension_semantics=sem)


def _pick(dim, pref, quantum=LANES):
    if dim <= pref:
        return dim
    best = None
    for t in range(quantum, pref + 1, quantum):
        if dim % t == 0:
            best = t
    assert best is not None, (dim, pref)
    return best


def _pos():
    return lax.axis_index("x"), lax.axis_index("y"), lax.axis_index("c")


def _flip(v, bit):
    return 1 - v if bit else v


def _sigmoid(v):
    return jax.nn.sigmoid(v)


def _silu(v):
    return v * _sigmoid(v)


def _dsilu(v):
    s = _sigmoid(v)
    return s * (1.0 + v * (1.0 - s))


def _neg_expm1(z):
    series = -z * (1.0 + z * (0.5 + z * (1.0 / 6.0 + z * (1.0 / 24.0))))
    return jnp.where(z > -0.03, series, 1.0 - jnp.exp(z))


def _softplus(z):
    return jnp.maximum(z, 0.0) + jnp.log1p(jnp.exp(-jnp.abs(z)))


def _matmul(a, b, *, ta=False, tb=False, tm=512, tn=512, tk=512, out_dtype=F32, name,
            a_rows=None, a_off=0, a_act=None, init=None, bias=None, dsilu_mul=None):
    rows_a = a.shape[0] if a_rows is None else a_rows
    if ta:
        K, M = rows_a, a.shape[1]
    else:
        M, K = rows_a, a.shape[1]
    N = b.shape[0] if tb else b.shape[1]
    tm, tn, tk = _pick(M, tm, SUBLANES), _pick(N, tn), _pick(K, tk)
    t_rows = tk if ta else tm
    assert a_off % t_rows == 0
    nk = K // tk
    off_blocks = a_off // t_rows
    dims = (((0 if ta else 1,), (1 if tb else 0,)), ((), ()))
    extras = [e for e in (init, bias, dsilu_mul) if e is not None]

    def body(a_ref, b_ref, *rest):
        rest = list(rest)
        init_ref = rest.pop(0) if init is not None else None
        bias_ref = rest.pop(0) if bias is not None else None
        dsm_ref = rest.pop(0) if dsilu_mul is not None else None
        o_ref = rest[0]
        av = a_ref[...]
        if a_act == "silu":
            av = _silu(av)
        prod = lax.dot_general(av, b_ref[...], dims, preferred_element_type=F32)

        def finish(r):
            if bias_ref is not None:
                r = r + bias_ref[...]
            if dsm_ref is not None:
                r = r * _dsilu(dsm_ref[...])
            o_ref[...] = r.astype(o_ref.dtype)

        if nk == 1:
            finish(prod if init_ref is None else prod + init_ref[...])
            return
        acc_ref = rest[1]
        k = pl.program_id(2)

        @pl.when(k == 0)
        def _():
            acc_ref[...] = prod if init_ref is None else prod + init_ref[...]

        @pl.when(k > 0)
        def _():
            acc_ref[...] += prod

        @pl.when(k == nk - 1)
        def _():
            finish(acc_ref[...])

    if ta:
        a_spec = pl.BlockSpec((tk, tm), lambda i, j, k: (k + off_blocks, i))
    else:
        a_spec = pl.BlockSpec((tm, tk), lambda i, j, k: (i + off_blocks, k))
    if tb:
        b_spec = pl.BlockSpec((tn, tk), lambda i, j, k: (j, k))
    else:
        b_spec = pl.BlockSpec((tk, tn), lambda i, j, k: (k, j))
    in_specs = [a_spec, b_spec]
    if init is not None:
        in_specs.append(pl.BlockSpec((tm, tn), lambda i, j, k: (i, j)))
    if bias is not None:
        in_specs.append(pl.BlockSpec((1, tn), lambda i, j, k: (0, j)))
    if dsilu_mul is not None:
        in_specs.append(pl.BlockSpec((1, tn), lambda i, j, k: (0, j)))
    return pl.pallas_call(
        body, name=name, grid=(M // tm, N // tn, nk),
        in_specs=in_specs, out_specs=pl.BlockSpec((tm, tn), lambda i, j, k: (i, j)),
        out_shape=jax.ShapeDtypeStruct((M, N), out_dtype),
        scratch_shapes=[pltpu.VMEM((tm, tn), F32)] if nk > 1 else [],
        compiler_params=_params(("parallel", "parallel", "arbitrary")),
    )(a, b, *extras)


def _elementwise(fn, ins, outs, *, rows, cols, name, tr=256):
    tr = _pick(rows, tr, 2 * SUBLANES)
    n_in = len(ins)

    def body(*refs):
        vals = fn(*[r[...] for r in refs[:n_in]])
        if not isinstance(vals, (tuple, list)):
            vals = (vals,)
        for r, v in zip(refs[n_in:], vals, strict=True):
            r[...] = v.astype(r.dtype)

    def spec(off):
        assert off % tr == 0
        ob = off // tr
        return pl.BlockSpec((tr, cols), lambda i: (i + ob, 0))

    res = pl.pallas_call(
        body, name=name, grid=(rows // tr,),
        in_specs=[spec(off) for _, off in ins],
        out_specs=[spec(0) for _ in outs],
        out_shape=[jax.ShapeDtypeStruct((rows, cols), dt) for dt in outs],
        compiler_params=_params(("parallel",)),
    )(*[a for a, _ in ins])
    return res


def _adam_math(w, g, m, v):
    m = ADAM_B1 * m + (1.0 - ADAM_B1) * g
    v = ADAM_B2 * v + (1.0 - ADAM_B2) * (g * g)
    m_hat = m / (1.0 - ADAM_B1 ** ADAM_STEP)
    v_hat = v / (1.0 - ADAM_B2 ** ADAM_STEP)
    delta = -ADAM_LR * (m_hat / (jnp.sqrt(v_hat) + ADAM_EPS) + ADAM_WD * w)
    return delta, m, v


def _adam(w, g, m, v, name):
    rows, cols = w.shape
    return _elementwise(_adam_math, [(w, 0), (g, 0), (m, 0), (v, 0)], [F32, F32, F32],
                        rows=rows, cols=cols, name=name)


def _pack(arrs, row_quantum):
    flat = jnp.concatenate([a.reshape(-1).astype(F32) for a in arrs])
    n = flat.shape[0]
    q = row_quantum * PACK_COLS
    total = -(-n // q) * q
    flat = jnp.pad(flat, (0, total - n))
    return flat.reshape(total // PACK_COLS, PACK_COLS)


def _unpack(buf, shapes):
    flat = buf.reshape(-1)
    out, off = [], 0
    for s in shapes:
        n = 1
        for d in s:
            n *= d
        out.append(flat[off:off + n].reshape(s))
        off += n
    return out


def _allreduce8(buf, name):
    R, C = buf.shape
    assert R % (8 * SUBLANES) == 0
    m = R // 8

    def body(x_ref, o_ref, recv, red, s1, r1, s2, r2):
        x, y, c = _pos()
        me = 4 * x + 2 * y + c

        def peer(k):
            px, py, pc = _flip(x, (k >> 2) & 1), _flip(y, (k >> 1) & 1), _flip(c, k & 1)
            return (px, py, pc), 4 * px + 2 * py + pc

        def rows(ref, idx):
            return ref.at[pl.ds(pl.multiple_of(idx * m, SUBLANES), m), :]

        def scatter(k):
            dev, p = peer(k)
            return pltpu.make_async_remote_copy(src_ref=rows(x_ref, p), dst_ref=recv.at[k], send_sem=s1.at[k],
                                                recv_sem=r1.at[k], device_id=dev, device_id_type=MESH_ID)

        def share(k):
            dev, p = peer(k)
            return pltpu.make_async_remote_copy(src_ref=red, dst_ref=rows(o_ref, me), send_sem=s2.at[k],
                                                recv_sem=r2.at[k], device_id=dev, device_id_type=MESH_ID)

        def shared_from(k):
            dev, p = peer(k)
            return pltpu.make_async_remote_copy(src_ref=red, dst_ref=rows(o_ref, p), send_sem=s2.at[k],
                                                recv_sem=r2.at[k], device_id=dev, device_id_type=MESH_ID)

        for k in range(1, 8):
            scatter(k).start()
        acc = rows(x_ref, me)[...]
        for k in range(1, 8):
            scatter(k).wait_recv()
            acc = acc + recv[k]
        red[...] = acc
        rows(o_ref, me)[...] = acc
        for k in range(1, 8):
            share(k).start()
        for k in range(1, 8):
            shared_from(k).wait_recv()
        for k in range(1, 8):
            scatter(k).wait_send()
            share(k).wait_send()

    return pl.pallas_call(
        body, name=name,
        in_specs=[pl.BlockSpec(memory_space=pltpu.VMEM)],
        out_specs=pl.BlockSpec(memory_space=pltpu.VMEM),
        out_shape=jax.ShapeDtypeStruct((R, C), F32),
        scratch_shapes=[pltpu.VMEM((8, m, C), F32), pltpu.VMEM((m, C), F32),
                        pltpu.SemaphoreType.DMA((8,)), pltpu.SemaphoreType.DMA((8,)),
                        pltpu.SemaphoreType.DMA((8,)), pltpu.SemaphoreType.DMA((8,))],
        compiler_params=_params(),
    )(buf)


def _bounce(src, dst, buf, sem):
    cin = pltpu.make_async_copy(src, buf, sem)
    cin.start()
    cin.wait()
    cout = pltpu.make_async_copy(buf, dst, sem)
    cout.start()
    cout.wait()


def _chunk(ref, axis, idx, size):
    start = idx * size
    if axis == 0:
        return ref.at[pl.ds(start, size), :]
    return ref.at[:, pl.ds(start, size)]


def _gather_weights(win, wout):
    D, nq = win.shape
    dq, D2 = wout.shape
    ops = ((0, 1, nq, D // 2), (1, 0, dq, dq // 2))

    def body(win_ref, wout_ref, gin_ref, gout_ref, buf_in, buf_out, lsem, ssem, rsem, fsem, gsem):
        x, y, c = _pos()
        q = 2 * x + y
        srcs = (win_ref, wout_ref)
        dsts = (gin_ref, gout_ref)

        def shard_window(o, chip):
            _, axis, size, _ = ops[o]
            return _chunk(dsts[o], axis, chip, size)

        def half(ref, o, core):
            return ref.at[pl.ds(core * ops[o][3], ops[o][3]), :]

        def half_window(o, chip, core):
            _, axis, size, hs = ops[o]
            if axis == 1:
                return dsts[o].at[pl.ds(core * hs, hs), pl.ds(chip * size, size)]
            return dsts[o].at[pl.ds(chip * size + core * hs, hs), :]

        def chip_of(k):
            px, py = _flip(x, (k >> 1) & 1), _flip(y, k & 1)
            return px, py, 2 * px + py

        sends, fwds = [], []
        for o in range(2):
            for k in range(1, 4):
                px, py, _ = chip_of(k)
                cp = pltpu.make_async_remote_copy(
                    src_ref=half(srcs[o], o, c), dst_ref=half_window(o, q, c), send_sem=ssem.at[o, k],
                    recv_sem=rsem.at[o, k], device_id=(px, py, c), device_id_type=MESH_ID)
                cp.start()
                sends.append(cp)
        for o, buf in enumerate((buf_in, buf_out)):
            _bounce(srcs[o], shard_window(o, q), buf, lsem.at[o])
        for o in range(2):
            for k in range(1, 4):
                px, py, pq = chip_of(k)
                landed = half_window(o, pq, c)
                pltpu.make_async_remote_copy(src_ref=landed, dst_ref=landed, send_sem=ssem.at[o, k],
                                             recv_sem=rsem.at[o, k], device_id=(px, py, c),
                                             device_id_type=MESH_ID).wait_recv()
                fw = pltpu.make_async_remote_copy(src_ref=landed, dst_ref=landed, send_sem=fsem.at[o, k],
                                                  recv_sem=gsem.at[o, k], device_id=(x, y, 1 - c),
                                                  device_id_type=MESH_ID)
                fw.start()
                fwds.append(fw)
        for o in range(2):
            for k in range(1, 4):
                _, _, pq = chip_of(k)
                theirs = half_window(o, pq, 1 - c)
                pltpu.make_async_remote_copy(src_ref=theirs, dst_ref=theirs, send_sem=fsem.at[o, k],
                                             recv_sem=gsem.at[o, k], device_id=(x, y, 1 - c),
                                             device_id_type=MESH_ID).wait_recv()
        for cp in sends + fwds:
            cp.wait_send()

    hbm = pl.BlockSpec(memory_space=pl.ANY)
    return pl.pallas_call(
        body, name="gather_weights", in_specs=[hbm, hbm], out_specs=[hbm, hbm],
        out_shape=[jax.ShapeDtypeStruct((D, 4 * nq), win.dtype), jax.ShapeDtypeStruct((4 * dq, D2), wout.dtype)],
        scratch_shapes=[pltpu.VMEM(win.shape, win.dtype), pltpu.VMEM(wout.shape, wout.dtype), pltpu.SemaphoreType.DMA((2,))]
        + [pltpu.SemaphoreType.DMA((2, 4)) for _ in range(4)],
        compiler_params=_params(),
    )(win, wout)


def _rs_to_sibling(gs, axes):
    n = len(gs)
    shapes = []
    for g, ax in zip(gs, axes):
        s = list(g.shape)
        s[ax] //= 8
        shapes.append(tuple(s))

    def body(*refs):
        g_refs, mine, landed = refs[:n], refs[n:2 * n], refs[2 * n:3 * n]
        bufs = refs[3 * n:4 * n]
        lsem, ssem, rsem = refs[4 * n:]
        x, y, c = _pos()
        cps = []
        for o in range(n):
            size = shapes[o][axes[o]]
            for j in range(4):
                rc = pltpu.make_async_remote_copy(
                    src_ref=_chunk(g_refs[o], axes[o], 2 * j + 1 - c, size), dst_ref=landed[o].at[j],
                    send_sem=ssem.at[o, j], recv_sem=rsem.at[o, j], device_id=(x, y, 1 - c), device_id_type=MESH_ID)
                rc.start()
                cps.append(rc)
        for o in range(n):
            size = shapes[o][axes[o]]
            for j in range(4):
                _bounce(_chunk(g_refs[o], axes[o], 2 * j + c, size), mine[o].at[j], bufs[o], lsem.at[o])
        for rc in cps:
            rc.wait()

    hbm = pl.BlockSpec(memory_space=pl.ANY)
    outs = [jax.ShapeDtypeStruct((4,) + s, g.dtype) for s, g in zip(shapes, gs)]
    res = pl.pallas_call(
        body, name="rs_to_sibling", in_specs=[hbm] * n, out_specs=[hbm] * (2 * n), out_shape=outs + outs,
        scratch_shapes=[pltpu.VMEM(s, g.dtype) for s, g in zip(shapes, gs)]
        + [pltpu.SemaphoreType.DMA((n,)), pltpu.SemaphoreType.DMA((n, 4)), pltpu.SemaphoreType.DMA((n, 4))],
        compiler_params=_params(),
    )(*gs)
    return res[:n], res[n:]


def _rs_to_chips(parts):
    n = len(parts)

    def body(*refs):
        p_refs, slots = refs[:n], refs[n:2 * n]
        bufs = refs[2 * n:3 * n]
        lsem, ssem, rsem = refs[3 * n:]
        x, y, c = _pos()
        q = 2 * x + y
        cps = []
        for o in range(n):
            for k in range(1, 4):
                px, py = _flip(x, (k >> 1) & 1), _flip(y, k & 1)
                rc = pltpu.make_async_remote_copy(
                    src_ref=p_refs[o].at[2 * px + py], dst_ref=slots[o].at[k], send_sem=ssem.at[o, k],
                    recv_sem=rsem.at[o, k], device_id=(px, py, c), device_id_type=MESH_ID)
                rc.start()
                cps.append(rc)
        for o in range(n):
            _bounce(p_refs[o].at[q], slots[o].at[0], bufs[o], lsem.at[o])
        for cp in cps:
            cp.wait()

    hbm = pl.BlockSpec(memory_space=pl.ANY)
    return pl.pallas_call(
        body, name="rs_to_chips", in_specs=[hbm] * n, out_specs=[hbm] * n,
        out_shape=[jax.ShapeDtypeStruct(p.shape, p.dtype) for p in parts],
        scratch_shapes=[pltpu.VMEM(p.shape[1:], p.dtype) for p in parts]
        + [pltpu.SemaphoreType.DMA((n,)), pltpu.SemaphoreType.DMA((n, 4)), pltpu.SemaphoreType.DMA((n, 4))],
        compiler_params=_params(),
    )(*parts)


def _rs_share(rs, axes):
    n = len(rs)
    shapes = []
    for r, ax in zip(rs, axes):
        s = list(r.shape)
        s[ax] *= 2
        shapes.append(tuple(s))

    def body(*refs):
        r_refs, outs = refs[:n], refs[n:2 * n]
        bufs = refs[2 * n:3 * n]
        lsem, ssem, rsem = refs[3 * n:]
        x, y, c = _pos()
        cps = []
        for o in range(n):
            size = r_refs[o].shape[axes[o]]
            window = _chunk(outs[o], axes[o], c, size)
            rc = pltpu.make_async_remote_copy(src_ref=r_refs[o], dst_ref=window, send_sem=ssem.at[o], recv_sem=rsem.at[o],
                                              device_id=(x, y, 1 - c), device_id_type=MESH_ID)
            rc.start()
            cps.append(rc)
        for o in range(n):
            size = r_refs[o].shape[axes[o]]
            _bounce(r_refs[o], _chunk(outs[o], axes[o], c, size), bufs[o], lsem.at[o])
        for cp in cps:
            cp.wait()

    hbm = pl.BlockSpec(memory_space=pl.ANY)
    return pl.pallas_call(
        body, name="rs_share", in_specs=[hbm] * n, out_specs=[hbm] * n,
        out_shape=[jax.ShapeDtypeStruct(s, r.dtype) for s, r in zip(shapes, rs)],
        scratch_shapes=[pltpu.VMEM(r.shape, r.dtype) for r in rs] + [pltpu.SemaphoreType.DMA((n,)) for _ in range(3)],
        compiler_params=_params(),
    )(*rs)


def _reduce_scatter(gs, axes):
    mine, landed = _rs_to_sibling(gs, axes)
    pair_sums = []
    for o, (mi, la) in enumerate(zip(mine, landed)):
        rows, cols = mi.shape[0] * mi.shape[1], mi.shape[2]
        s = _elementwise(lambda a, b: a.astype(F32) + b.astype(F32), [(mi.reshape(rows, cols), 0), (la.reshape(rows, cols), 0)],
                         [BF16], rows=rows, cols=cols, name=f"rs_pair_sum{o}")[0]
        pair_sums.append(s.reshape(mi.shape))
    slots = _rs_to_chips(pair_sums)
    reduced = []
    for o, sl in enumerate(slots):
        rows, cols = sl.shape[1], sl.shape[2]
        flat = sl.reshape(4 * rows, cols)
        r = _elementwise(lambda a, b, c, d: (a.astype(F32) + b.astype(F32)) + (c.astype(F32) + d.astype(F32)),
                         [(flat, k * rows) for k in range(4)], [F32], rows=rows, cols=cols, name=f"rs_chip_sum{o}")[0]
        reduced.append(r)
    return _rs_share(reduced, axes)


def _norm_in(x, ctx, g, sc_l, sh_l, sc_c, sh_c, tr):
    L, D = x.shape
    T = ctx.shape[0]
    nx, nc = L // tr, T // tr

    def body(x_ref, c_ref, g_ref, scl, shl, scc, shc, o_ref):
        i = pl.program_id(0)

        def run(src, sc, sh):
            v = src[...]
            r = lax.rsqrt(jnp.mean(v * v, axis=-1, keepdims=True) + EPS)
            o_ref[...] = ((v * r * g_ref[...]) * (1.0 + sc[...]) + sh[...]).astype(o_ref.dtype)

        @pl.when(i < nx)
        def _():
            run(x_ref, scl, shl)

        @pl.when(i >= nx)
        def _():
            run(c_ref, scc, shc)

    vec = pl.BlockSpec((1, D), lambda i: (0, 0))
    return pl.pallas_call(
        body, name="norm_in", grid=(nx + nc,),
        in_specs=[pl.BlockSpec((tr, D), lambda i: (jnp.minimum(i, nx - 1), 0)),
                  pl.BlockSpec((tr, D), lambda i: (jnp.maximum(i - nx, 0), 0)), vec, vec, vec, vec, vec],
        out_specs=pl.BlockSpec((tr, D), lambda i: (i, 0)),
        out_shape=jax.ShapeDtypeStruct((L + T, D), BF16),
        compiler_params=_params(("arbitrary",)),
    )(x, ctx, g, sc_l, sh_l, sc_c, sh_c)


def _tmod(tl, row_w):
    assert row_w & (row_w - 1) == 0
    return lax.broadcasted_iota(jnp.int32, (tl, 1), 0) & (row_w - 1)


def _shift(z, k, tmod, row_w):
    tl = z.shape[0]
    rolled = pltpu.roll(z, k % tl, 0)
    mask = (tmod >= k) if k > 0 else (tmod < row_w + k)
    return jnp.where(mask, rolled, 0.0)


def _conv(z, w_ref, taps, left, tmod, row_w):
    out = None
    for j in range(taps):
        k = left - j
        term = (z if k == 0 else _shift(z, k, tmod, row_w)) * w_ref[j:j + 1, :]
        out = term if out is None else out + term
    return out


def _conv_bwd(dz, z, w_ref, taps, left, tmod, row_w):
    din = None
    dws = []
    for j in range(taps):
        k = left - j
        term = (dz if k == 0 else _shift(dz, -k, tmod, row_w)) * w_ref[j:j + 1, :]
        din = term if din is None else din + term
        zs = z if k == 0 else _shift(z, k, tmod, row_w)
        dws.append(jnp.sum(dz * zs, axis=0, keepdims=True))
    return din, dws


def _coeffs(xb, wd_ref, ba_ref, bx_ref, lam_ref, pre_scr, ng, gs):
    W = xb.shape[1]
    xb16 = xb.astype(BF16)
    for g in range(ng):
        pg = jnp.dot(xb16[:, g * gs:(g + 1) * gs], wd_ref[g], preferred_element_type=F32)
        pre_scr[:, g * gs:(g + 1) * gs] = pg[:, :gs]
        pre_scr[:, W + g * gs:W + (g + 1) * gs] = pg[:, gs:]
    r = _sigmoid(pre_scr[:, :W] + ba_ref[...])
    ig = _sigmoid(pre_scr[:, W:] + bx_ref[...])
    sp = _softplus(-lam_ref[...])
    la = (-LRU_C) * r * sp
    a = jnp.exp(la)
    m = jnp.sqrt(_neg_expm1(2.0 * la))
    return r, ig, sp, a, m


def _row_loop(tl, rev, step, init):
    nchunk = tl // SUBLANES

    def chunk(j, carry):
        jj = (nchunk - 1 - j) if rev else j
        c0 = pl.multiple_of(jj * SUBLANES, SUBLANES)
        for r in (range(SUBLANES - 1, -1, -1) if rev else range(SUBLANES)):
            carry = step(c0 + r, carry)
        return carry

    return lax.fori_loop(0, nchunk, chunk, init)


def _mix_fwd(P, d, h_init, wts, *, rows, row_off, row_w, tl, h_other=None, name):
    W = P.shape[1] // 6
    nt = rows // tl
    ob = row_off // tl
    rev = d == 1
    gs = min(LRU_GROUP, W)
    ng = W // gs
    wca, wcb, bcb = wts["wca"], wts["wcb"], wts["bcb"]
    wd, ba, bx, lam = wts["wd"][d], wts["ba"][d], wts["bx"][d], wts["lam"][d]

    def tile(i):
        return (nt - 1 - i) if rev else i

    def pcol(j):
        return pl.BlockSpec((tl, W), lambda i: (tile(i) + ob, j))

    vec = pl.BlockSpec((1, W), lambda i: (0, 0))
    taps = pl.BlockSpec((SUBLANES, W), lambda i: (0, 0))
    wd_spec = pl.BlockSpec(wd.shape, lambda i: (0, 0, 0))
    seq = pl.BlockSpec((tl, W), lambda i: (tile(i), 0))

    def body(*refs):
        if rev:
            (bl, cl, ul, gl, vl, ql, ho, wca_r, wcb_r, bcb_r, wd_r, ba_r, bx_r, lam_r, hin, hseq, cat,
             a_scr, b_scr, pre_scr, carry) = refs
        else:
            vl, wcb_r, bcb_r, wd_r, ba_r, bx_r, lam_r, hin, hseq, a_scr, b_scr, pre_scr, carry = refs
        i = pl.program_id(0)

        @pl.when(i == 0)
        def _():
            carry[...] = hin[...]

        tmod = _tmod(tl, row_w)
        xb = _conv(vl[...], wcb_r, 4, 2, tmod, row_w) + bcb_r[...]
        r, ig, sp, a, m = _coeffs(xb, wd_r, ba_r, bx_r, lam_r, pre_scr, ng, gs)
        a_scr[...] = a
        b_scr[...] = m * (ig * xb)

        def step(t, h):
            h = a_scr[pl.ds(t, 1), :] * h + b_scr[pl.ds(t, 1), :]
            hseq[pl.ds(t, 1), :] = h
            return h

        carry[...] = _row_loop(tl, rev, step, carry[...])

        if rev:
            ylru = ho[...] + hseq[...]
            yb = ylru * _silu(ql[...])
            ya = bl[...] * _conv(cl[...] * ul[...], wca_r, 3, 1, tmod, row_w) * _silu(gl[...])
            cat[:, :W] = ya.astype(cat.dtype)
            cat[:, W:] = yb.astype(cat.dtype)

    scratch = [pltpu.VMEM((tl, W), F32), pltpu.VMEM((tl, W), F32), pltpu.VMEM((tl, 2 * W), F32), pltpu.VMEM((1, W), F32)]
    if rev:
        in_specs = [pcol(j) for j in range(6)] + [seq, taps, taps, vec, wd_spec, vec, vec, vec, vec]
        args = [P] * 6 + [h_other, wca, wcb, bcb, wd, ba, bx, lam, h_init]
        out_specs = [seq, pl.BlockSpec((tl, 2 * W), lambda i: (tile(i), 0))]
        out_shape = [jax.ShapeDtypeStruct((rows, W), F32), jax.ShapeDtypeStruct((rows, 2 * W), BF16)]
    else:
        in_specs = [pcol(4), taps, vec, wd_spec, vec, vec, vec, vec]
        args = [P, wcb, bcb, wd, ba, bx, lam, h_init]
        out_specs = [seq]
        out_shape = [jax.ShapeDtypeStruct((rows, W), F32)]
    return pl.pallas_call(
        body, name=name, grid=(nt,), in_specs=in_specs, out_specs=out_specs, out_shape=out_shape,
        scratch_shapes=scratch, compiler_params=_params(("arbitrary",)),
    )(*args)


def _lru_bwd_tile(d, xb, dyl, hseq_v, hedge, c_scr, wd_r, ba_r, bx_r, lam_r, a_scr, d_scr, g_scr, pre_scr,
                  acc, first, tl, ng, gs):
    W = xb.shape[1]
    rev = d == 0
    r, ig, sp, a, m = _coeffs(xb, wd_r, ba_r, bx_r, lam_r, pre_scr, ng, gs)
    a_scr[...] = a
    d_scr[...] = dyl

    def step(t, c):
        g = d_scr[pl.ds(t, 1), :] + c
        g_scr[pl.ds(t, 1), :] = g
        return a_scr[pl.ds(t, 1), :] * g

    c_scr[...] = _row_loop(tl, rev, step, c_scr[...])
    g = g_scr[...]
    row = lax.broadcasted_iota(jnp.int32, (tl, 1), 0)
    if d == 0:
        hprev = jnp.where(row == 0, hedge, pltpu.roll(hseq_v, 1, 0))
    else:
        hprev = jnp.where(row == tl - 1, hedge, pltpu.roll(hseq_v, tl - 1, 0))
    u = ig * xb
    d_u = g * m
    d_la = (g * hprev) * a - (g * u) * (a * a) / m
    d_ig = d_u * xb
    dxb = d_u * ig
    d_pr = d_la * ((-LRU_C) * sp) * (r * (1.0 - r))
    d_pi = d_ig * (ig * (1.0 - ig))
    dsp = jnp.sum(d_la * ((-LRU_C) * r), axis=0, keepdims=True)
    dlam = dsp * (-_sigmoid(-lam_r[...]))
    dba = jnp.sum(d_pr, axis=0, keepdims=True)
    dbx = jnp.sum(d_pi, axis=0, keepdims=True)
    dwd_ref, dba_ref, dbx_ref, dlam_ref = acc

    @pl.when(first)
    def _():
        dwd_ref[...] = jnp.zeros_like(dwd_ref)
        dba_ref[...] = jnp.zeros_like(dba_ref)
        dbx_ref[...] = jnp.zeros_like(dbx_ref)
        dlam_ref[...] = jnp.zeros_like(dlam_ref)

    dba_ref[...] += dba
    dbx_ref[...] += dbx
    dlam_ref[...] += dlam
    xb16 = xb.astype(BF16)
    dpr16 = d_pr.astype(BF16)
    dpi16 = d_pi.astype(BF16)
    parts = []
    for gi in range(ng):
        sl = slice(gi * gs, (gi + 1) * gs)
        dp = jnp.concatenate([dpr16[:, sl], dpi16[:, sl]], axis=1)
        parts.append(lax.dot_general(dp, wd_r[gi], (((1,), (1,)), ((), ())), preferred_element_type=F32))
        dwd_ref[gi] += lax.dot_general(xb16[:, sl], dp, (((0,), (0,)), ((), ())), preferred_element_type=F32)
    return dxb + (parts[0] if ng == 1 else jnp.concatenate(parts, axis=1))


def _edge_block(h, tl, nt, d):
    W = h.shape[1]
    per = tl // SUBLANES
    if d == 0:
        return pl.BlockSpec((SUBLANES, W), lambda i: (jnp.maximum((nt - 1 - i) * per - 1, 0), 0))
    return pl.BlockSpec((SUBLANES, W), lambda i: (jnp.minimum((i + 1) * per, nt * per - 1), 0))


def _mix_bwd0(P, dcat, h0s, h_init, c_init, wts, *, rows, row_off, row_w, tl, name):
    W = P.shape[1] // 6
    nt = rows // tl
    ob = row_off // tl
    gs = min(LRU_GROUP, W)
    ng = W // gs
    wcb, bcb = wts["wcb"], wts["bcb"]
    wd, ba, bx, lam = wts["wd"][0], wts["ba"][0], wts["bx"][0], wts["lam"][0]

    def tile(i):
        return nt - 1 - i

    vec = pl.BlockSpec((1, W), lambda i: (0, 0))
    taps = pl.BlockSpec((SUBLANES, W), lambda i: (0, 0))
    wd_spec = pl.BlockSpec(wd.shape, lambda i: (0, 0, 0))
    seq = pl.BlockSpec((tl, W), lambda i: (tile(i), 0))

    def body(vl, ql, dyb, hs, hedge8, wcb_r, bcb_r, wd_r, ba_r, bx_r, lam_r, hin, cin,
             dxb_o, dwd_o, dba_o, dbx_o, dlam_o, cfin, a_scr, d_scr, g_scr, pre_scr, c_scr):
        i = pl.program_id(0)

        @pl.when(i == 0)
        def _():
            c_scr[...] = cin[...]

        tmod = _tmod(tl, row_w)
        xb = _conv(vl[...], wcb_r, 4, 2, tmod, row_w) + bcb_r[...]
        dyl = dyb[...] * _silu(ql[...])
        hedge = jnp.where(i == nt - 1, hin[...], hedge8[SUBLANES - 1:SUBLANES, :])
        dxb_o[...] = _lru_bwd_tile(0, xb, dyl, hs[...], hedge, c_scr, wd_r, ba_r, bx_r, lam_r, a_scr, d_scr, g_scr,
                                   pre_scr, (dwd_o, dba_o, dbx_o, dlam_o), i == 0, tl, ng, gs)
        cfin[...] = c_scr[...]

    return pl.pallas_call(
        body, name=name, grid=(nt,),
        in_specs=[pl.BlockSpec((tl, W), lambda i: (tile(i) + ob, 4)), pl.BlockSpec((tl, W), lambda i: (tile(i) + ob, 5)),
                  pl.BlockSpec((tl, W), lambda i: (tile(i), 1)), seq, _edge_block(h0s, tl, nt, 0),
                  taps, vec, wd_spec, vec, vec, vec, vec, vec],
        out_specs=[seq, wd_spec, vec, vec, vec, vec],
        out_shape=[jax.ShapeDtypeStruct((rows, W), F32), jax.ShapeDtypeStruct(wd.shape, F32)]
        + [jax.ShapeDtypeStruct((1, W), F32)] * 4,
        scratch_shapes=[pltpu.VMEM((tl, W), F32)] * 3 + [pltpu.VMEM((tl, 2 * W), F32), pltpu.VMEM((1, W), F32)],
        compiler_params=_params(("arbitrary",)),
    )(P, P, dcat, h0s, h0s, wcb, bcb, wd, ba, bx, lam, h_init, c_init)


def _mix_bwd1(P, dcat, h0s, h1s, dxb0, h_init, c_init, wts, *, rows, row_off, row_w, tl, name):
    W = P.shape[1] // 6
    nt = rows // tl
    ob = row_off // tl
    gs = min(LRU_GROUP, W)
    ng = W // gs
    wca, wcb, bcb = wts["wca"], wts["wcb"], wts["bcb"]
    wd, ba, bx, lam = wts["wd"][1], wts["ba"][1], wts["bx"][1], wts["lam"][1]

    vec = pl.BlockSpec((1, W), lambda i: (0, 0))
    taps = pl.BlockSpec((SUBLANES, W), lambda i: (0, 0))
    wd_spec = pl.BlockSpec(wd.shape, lambda i: (0, 0, 0))
    seq = pl.BlockSpec((tl, W), lambda i: (i, 0))

    def body(bl, cl, ul, gl, vl, ql, dya, dyb, h0, h1, hedge8, dx0, wca_r, wcb_r, bcb_r, wd_r, ba_r, bx_r, lam_r, hin, cin,
             dp_o, dwd_o, dba_o, dbx_o, dlam_o, dwca_o, dwcb_o, dbcb_o, cfin, a_scr, d_scr, g_scr, pre_scr, c_scr):
        i = pl.program_id(0)

        @pl.when(i == 0)
        def _():
            c_scr[...] = cin[...]
            dwca_o[...] = jnp.zeros_like(dwca_o)
            dwcb_o[...] = jnp.zeros_like(dwcb_o)
            dbcb_o[...] = jnp.zeros_like(dbcb_o)

        tmod = _tmod(tl, row_w)
        v = vl[...]
        q = ql[...]
        xb = _conv(v, wcb_r, 4, 2, tmod, row_w) + bcb_r[...]
        sq = _sigmoid(q)
        dyl = dyb[...] * (q * sq)
        hedge = jnp.where(i == nt - 1, hin[...], hedge8[0:1, :])
        dxb = dx0[...] + _lru_bwd_tile(1, xb, dyl, h1[...], hedge, c_scr, wd_r, ba_r, bx_r, lam_r, a_scr, d_scr, g_scr,
                                       pre_scr, (dwd_o, dba_o, dbx_o, dlam_o), i == 0, tl, ng, gs)
        cfin[...] = c_scr[...]
        dv, dwb = _conv_bwd(dxb, v, wcb_r, 4, 2, tmod, row_w)
        for j in range(4):
            dwcb_o[j:j + 1, :] += dwb[j]
        dbcb_o[...] += jnp.sum(dxb, axis=0, keepdims=True)
        dq = dyb[...] * (h0[...] + h1[...]) * (sq * (1.0 + q * (1.0 - sq)))
        b_, c_, u_, g_ = bl[...], cl[...], ul[...], gl[...]
        z = c_ * u_
        cz = _conv(z, wca_r, 3, 1, tmod, row_w)
        sgm = _sigmoid(g_)
        sg = g_ * sgm
        da = dya[...]
        dz, dwa = _conv_bwd(da * b_ * sg, z, wca_r, 3, 1, tmod, row_w)
        for j in range(3):
            dwca_o[j:j + 1, :] += dwa[j]
        dp_o[:, 0 * W:1 * W] = (da * cz * sg).astype(dp_o.dtype)
        dp_o[:, 1 * W:2 * W] = (dz * u_).astype(dp_o.dtype)
        dp_o[:, 2 * W:3 * W] = (dz * c_).astype(dp_o.dtype)
        dp_o[:, 3 * W:4 * W] = (da * b_ * cz * (sgm * (1.0 + g_ * (1.0 - sgm)))).astype(dp_o.dtype)
        dp_o[:, 4 * W:5 * W] = dv.astype(dp_o.dtype)
        dp_o[:, 5 * W:6 * W] = dq.astype(dp_o.dtype)

    def pcol(j):
        return pl.BlockSpec((tl, W), lambda i: (i + ob, j))

    return pl.pallas_call(
        body, name=name, grid=(nt,),
        in_specs=[pcol(j) for j in range(6)]
        + [pl.BlockSpec((tl, W), lambda i: (i, 0)), pl.BlockSpec((tl, W), lambda i: (i, 1)), seq, seq,
           _edge_block(h1s, tl, nt, 1), seq, taps, taps, vec, wd_spec, vec, vec, vec, vec, vec],
        out_specs=[pl.BlockSpec((tl, 6 * W), lambda i: (i, 0)), wd_spec, vec, vec, vec, taps, taps, vec, vec],
        out_shape=[jax.ShapeDtypeStruct((rows, 6 * W), BF16), jax.ShapeDtypeStruct(wd.shape, F32)]
        + [jax.ShapeDtypeStruct((1, W), F32)] * 3
        + [jax.ShapeDtypeStruct((SUBLANES, W), F32)] * 2 + [jax.ShapeDtypeStruct((1, W), F32)] * 2,
        scratch_shapes=[pltpu.VMEM((tl, W), F32)] * 3 + [pltpu.VMEM((tl, 2 * W), F32), pltpu.VMEM((1, W), F32)],
        compiler_params=_params(("arbitrary",)),
    )(*([P] * 6), dcat, dcat, h0s, h1s, h1s, dxb0, wca, wcb, bcb, wd, ba, bx, lam, h_init, c_init)


def _loss_head(out, x, tgt, gt, fg, tr):
    L, D = x.shape

    def body(o_ref, x_ref, t_ref, gt_ref, fg_ref, dn_o, do_o, dfg_o, dgt_o, loss_o):
        i = pl.program_id(0)

        @pl.when(i == 0)
        def _():
            dfg_o[...] = jnp.zeros_like(dfg_o)
            dgt_o[...] = jnp.zeros_like(dgt_o)
            loss_o[...] = jnp.zeros_like(loss_o)

        o = o_ref[...]
        gt_v = gt_ref[...]
        fg_v = fg_ref[...]
        n = x_ref[...] + gt_v * o
        r = lax.rsqrt(jnp.mean(n * n, axis=-1, keepdims=True) + EPS)
        nr = n * r
        e = nr * fg_v - t_ref[...]
        loss_o[...] += 0.5 * jnp.sum(jnp.mean(e * e, axis=-1, keepdims=True))
        dy = e * (1.0 / D)
        dfg_o[...] += jnp.sum(dy * nr, axis=0, keepdims=True)
        qv = dy * fg_v
        dn = r * (qv - nr * jnp.mean(qv * nr, axis=-1, keepdims=True))
        dgt_o[...] += jnp.sum(dn * o, axis=0, keepdims=True)
        dn_o[...] = dn
        do_o[...] = (dn * gt_v).astype(do_o.dtype)

    blk = pl.BlockSpec((tr, D), lambda i: (i, 0))
    vec = pl.BlockSpec((1, D), lambda i: (0, 0))
    return pl.pallas_call(
        body, name="loss_head", grid=(L // tr,), in_specs=[blk, blk, blk, vec, vec],
        out_specs=[blk, blk, vec, vec, pl.BlockSpec((SUBLANES, LANES), lambda i: (0, 0))],
        out_shape=[jax.ShapeDtypeStruct((L, D), F32), jax.ShapeDtypeStruct((L, D), BF16),
                   jax.ShapeDtypeStruct((1, D), F32), jax.ShapeDtypeStruct((1, D), F32),
                   jax.ShapeDtypeStruct((SUBLANES, LANES), F32)],
        compiler_params=_params(("arbitrary",)),
    )(out, x, tgt, gt, fg)


def _norm_bwd(dhl, x, dn, g, sc, tr, name):
    L, D = x.shape
    with_x = dn is not None

    def body(*refs):
        if with_x:
            d_ref, x_ref, dn_ref, g_ref, sc_ref, gx_o, dsh_o, dsc_o, dg_o = refs
        else:
            d_ref, x_ref, g_ref, sc_ref, dsh_o, dsc_o, dg_o = refs
        i = pl.program_id(0)

        @pl.when(i == 0)
        def _():
            dsh_o[...] = jnp.zeros_like(dsh_o)
            dsc_o[...] = jnp.zeros_like(dsc_o)
            dg_o[...] = jnp.zeros_like(dg_o)

        d = d_ref[...]
        xv = x_ref[...]
        g_v = g_ref[...]
        r = lax.rsqrt(jnp.mean(xv * xv, axis=-1, keepdims=True) + EPS)
        xr = xv * r
        dsh_o[...] += jnp.sum(d, axis=0, keepdims=True)
        dsc_o[...] += jnp.sum(d * (xr * g_v), axis=0, keepdims=True)
        dxn = d * (1.0 + sc_ref[...])
        dg_o[...] += jnp.sum(dxn * xr, axis=0, keepdims=True)
        if with_x:
            qv = dxn * g_v
            gx_o[...] = r * (qv - xr * jnp.mean(qv * xr, axis=-1, keepdims=True)) + dn_ref[...]

    blk = pl.BlockSpec((tr, D), lambda i: (i, 0))
    vec = pl.BlockSpec((1, D), lambda i: (0, 0))
    vshape = jax.ShapeDtypeStruct((1, D), F32)
    res = pl.pallas_call(
        body, name=name, grid=(L // tr,),
        in_specs=[blk, blk] + ([blk] if with_x else []) + [vec, vec],
        out_specs=([blk] if with_x else []) + [vec, vec, vec],
        out_shape=([jax.ShapeDtypeStruct((L, D), F32)] if with_x else []) + [vshape] * 3,
        compiler_params=_params(("arbitrary",)),
    )(*([dhl, x] + ([dn] if with_x else []) + [g, sc]))
    return res if with_x else [None] + list(res)


def _pack_blockdiag(wa, wx, gs):
    H, hd, _ = wa.shape
    hp = gs // hd
    ng = H // hp
    eye = jnp.eye(hp, dtype=wa.dtype)

    def bd(w):
        return jnp.einsum("gpij,pq->gpiqj", w.reshape(ng, hp, hd, hd), eye).reshape(ng, gs, gs)

    return jnp.concatenate([bd(wa), bd(wx)], axis=-1).astype(BF16)


def _unpack_blockdiag(dwd, H, hd, gs):
    hp = gs // hd
    ng = H // hp
    eye = jnp.eye(hp, dtype=dwd.dtype)

    def diag(dm):
        return jnp.einsum("gpiqj,pq->gpij", dm.reshape(ng, hp, hd, hp, hd), eye).reshape(H, hd, hd)

    return diag(dwd[:, :, :gs]), diag(dwd[:, :, gs:])


def kernel(x, c, ctx, c_ctx, norm_g, w_ada, b_ada, w_in, w_conv_a, w_conv_b, b_conv_b, lru_wa, lru_ba, lru_wx, lru_bx, lru_lambda, w_out, final_g, loss_target, m_c_ctx, m_norm_g, m_w_ada, m_b_ada, m_w_in, m_w_conv_a, m_w_conv_b, m_b_conv_b, m_lru_wa, m_lru_ba, m_lru_wx, m_lru_bx, m_lru_lambda, m_w_out, m_final_g, v_c_ctx, v_norm_g, v_w_ada, v_b_ada, v_w_in, v_w_conv_a, v_w_conv_b, v_b_conv_b, v_lru_wa, v_lru_ba, v_lru_wx, v_lru_bx, v_lru_lambda, v_w_out, v_final_g):
    xi, yi, ci = _pos()
    me = 4 * xi + 2 * yi + ci
    q = 2 * xi + yi
    first_core = (ci == 0).astype(F32)

    L, D = x.shape[1], x.shape[2]
    T = ctx.shape[1]
    W = D // 2
    Wq = W // 4
    H, hd = lru_wa.shape[2], lru_wa.shape[3]
    gs = min(LRU_GROUP, W)
    nq = w_ada.shape[2]
    tl = min(256, T, L)
    tr = min(256, T, L)
    x2, ctx2, tgt2 = x[0], ctx[0], loss_target[0]

    win_full, wout_full = _gather_weights(w_in[0].astype(BF16), w_out[0].astype(BF16))

    def place(shard, full_cols):
        z = jnp.zeros((shard.shape[0], full_cols), F32)
        return lax.dynamic_update_slice(z, shard * first_core, (0, q * shard.shape[1]))

    c_rows = lax.dynamic_update_slice(jnp.zeros((8, D), F32), c, (me, 0))
    small_in = [c_rows, place(w_conv_a[0], W), place(w_conv_b[0], W), place(lru_ba[0], W), place(lru_bx[0], W),
                place(lru_lambda[0], W)]
    small_shapes = [a.shape for a in small_in]
    gathered = _allreduce8(_pack(small_in, 8 * SUBLANES), "gather_small")
    c_all, wca, wcb, ba_all, bx_all, lam_all = _unpack(gathered, small_shapes)

    s_rows = jnp.concatenate([c_all, c_ctx[None, :], jnp.zeros((7, D), F32)], axis=0)
    mod_part = _matmul(s_rows, w_ada[0], a_act="silu", bias=lax.dynamic_slice(b_ada, (0, q * nq), (1, nq)),
                       tm=16, tn=nq, tk=512, name="ada_fwd")
    mod_all = _allreduce8(_pack([place(mod_part, 4 * nq)], 8 * SUBLANES), "gather_mod")
    mod_all = _unpack(mod_all, [(16, 4 * nq)])[0]
    mod_l = lax.dynamic_slice(mod_all, (me, 0), (1, 3 * D))
    mod_c = mod_all[8:9]
    sh_l, sc_l, gt_l = mod_l[:, :D], mod_l[:, D:2 * D], mod_l[:, 2 * D:]
    sh_c, sc_c = mod_c[:, :D], mod_c[:, D:2 * D]

    pad_taps = lambda w: jnp.pad(w, ((0, SUBLANES - w.shape[0]), (0, 0)))
    wts = {
        "wca": pad_taps(wca), "wcb": pad_taps(wcb), "bcb": b_conv_b,
        "wd": [_pack_blockdiag(lru_wa[0, d], lru_wx[0, d], gs) for d in range(2)],
        "ba": [ba_all[d:d + 1] for d in range(2)], "bx": [bx_all[d:d + 1] for d in range(2)],
        "lam": [lam_all[d:d + 1] for d in range(2)],
    }

    hl = _norm_in(x2, ctx2, norm_g, sc_l, sh_l, sc_c, sh_c, tr)
    p_lat = _matmul(hl, win_full, a_rows=L, tm=1024, tn=1536, tk=D, name="in_proj")
    p_ctx = _matmul(hl, win_full, a_rows=T, a_off=L, tm=T, tn=1536, tk=D, name="in_proj_ctx")
    zero_w = jnp.zeros((1, W), F32)
    c0s = _mix_fwd(p_ctx, 0, zero_w, wts, rows=T, row_off=0, row_w=T, tl=tl, name="ctx_fwd0")[0]
    c1s, _ = _mix_fwd(p_ctx, 1, zero_w, wts, rows=T, row_off=0, row_w=T, tl=tl, h_other=c0s, name="ctx_fwd1")
    h0_init, h1_init = c0s[T - 1:T], c1s[0:1]
    h0s = _mix_fwd(p_lat, 0, h0_init, wts, rows=L, row_off=0, row_w=GRID_W, tl=tl, name="mix_fwd0")[0]
    h1s, cat = _mix_fwd(p_lat, 1, h1_init, wts, rows=L, row_off=0, row_w=GRID_W, tl=tl, h_other=h0s, name="mix_fwd1")
    out = _matmul(cat, wout_full, tm=512, tn=D, tk=2 * W, name="out_proj")
    dn, dout, dfg, dgt, loss_blk = _loss_head(out, x2, tgt2, gt_l, final_g[None, :], tr)

    dcat = _matmul(dout, wout_full, tb=True, tm=512, tn=2 * W, tk=D, name="out_proj_bwd")
    gw_out = _matmul(cat, dout, ta=True, tm=1024, tn=D, tk=1024, out_dtype=BF16, name="w_out_grad")
    dxb0, dwd0, dba0, dbx0, dlam0, ch0 = _mix_bwd0(p_lat, dcat, h0s, h0_init, zero_w, wts, rows=L, row_off=0,
                                                   row_w=GRID_W, tl=tl, name="mix_bwd0")
    dp, dwd1, dba1, dbx1, dlam1, dwca, dwcb, dbcb, ch1 = _mix_bwd1(
        p_lat, dcat, h0s, h1s, dxb0, h1_init, zero_w, wts, rows=L, row_off=0, row_w=GRID_W, tl=tl, name="mix_bwd1")
    zero_cat = jnp.zeros((T, 2 * W), F32)
    cxb0, cwd0, cba0, cbx0, clam0, _ = _mix_bwd0(p_ctx, zero_cat, c0s, zero_w, ch0, wts, rows=T, row_off=0,
                                                 row_w=T, tl=tl, name="ctx_bwd0")
    dp_c, cwd1, cba1, cbx1, clam1, cwca, cwcb, cbcb, _ = _mix_bwd1(
        p_ctx, zero_cat, c0s, c1s, cxb0, zero_w, ch1, wts, rows=T, row_off=0, row_w=T, tl=tl, name="ctx_bwd1")

    dhl = _matmul(dp, win_full, tb=True, tm=1024, tn=D, tk=1024, name="in_proj_bwd")
    dhc = _matmul(dp_c, win_full, tb=True, tm=T, tn=D, tk=512, name="in_proj_bwd_ctx")
    gx, dsh_l, dsc_l, dng_l = _norm_bwd(dhl, x2, dn, norm_g, sc_l, tr, "norm_bwd")
    _, dsh_c, dsc_c, dng_c = _norm_bwd(dhc, ctx2, None, norm_g, sc_c, tr, "norm_bwd_ctx")
    gw_in_ctx = _matmul(hl, dp_c, ta=True, a_rows=T, a_off=L, tm=1024, tn=1536, tk=T, name="w_in_grad_ctx")
    gw_in = _matmul(hl, dp, ta=True, a_rows=L, tm=1024, tn=1536, tk=1024, init=gw_in_ctx, out_dtype=BF16,
                    name="w_in_grad")

    g_in_shard, g_out_shard = _reduce_scatter([gw_in, gw_out], [1, 0])

    dwa0, dwx0 = _unpack_blockdiag(dwd0 + cwd0, H, hd, gs)
    dwa1, dwx1 = _unpack_blockdiag(dwd1 + cwd1, H, hd, gs)
    zeros_d = jnp.zeros((1, D), F32)
    dmod_l = jnp.concatenate([dsh_l, dsc_l, dgt], axis=1)
    dmod_c = jnp.concatenate([dsh_c, dsc_c, zeros_d], axis=1)
    small_g = [
        lax.dynamic_update_slice(jnp.zeros((8, 3 * D), F32), dmod_l, (me, 0)), dmod_c,
        dfg, dng_l + dng_c, (dwca + cwca)[:3], (dwcb + cwcb)[:4], dbcb + cbcb,
        jnp.stack([dwa0, dwa1]), jnp.stack([dwx0, dwx1]),
        jnp.concatenate([dba0 + cba0, dba1 + cba1], axis=0), jnp.concatenate([dbx0 + cbx0, dbx1 + cbx1], axis=0),
        jnp.concatenate([dlam0 + clam0, dlam1 + clam1], axis=0),
    ]
    g_shapes = [a.shape for a in small_g]
    (g_rows, g_modc, g_fg, g_ng, g_wca, g_wcb, g_bcb, g_wa, g_wx, g_ba, g_bx, g_lam) = _unpack(
        _allreduce8(_pack(small_g, 8 * SUBLANES), "reduce_small"), g_shapes)

    g_mod = jnp.concatenate([g_rows, g_modc, jnp.zeros((7, 3 * D), F32)], axis=0)
    g_mod_q = lax.dynamic_slice(g_mod, (0, q * nq), (16, nq))
    g_w_ada = _matmul(s_rows, g_mod_q, ta=True, a_act="silu", tm=1024, tn=nq, tk=16, name="w_ada_grad")
    g_b_ada = jnp.sum(g_mod[:9], axis=0, keepdims=True)
    gc_part = _matmul(jnp.pad(lax.dynamic_slice(g_modc, (0, q * nq), (1, nq)), ((0, 7), (0, 0))), w_ada[0], tb=True,
                      dsilu_mul=c_ctx[None, :], tm=8, tn=D, tk=512, name="c_ctx_grad")
    g_c_ctx = _unpack(_allreduce8(_pack([gc_part[0:1] * first_core], 8 * SUBLANES), "reduce_c_ctx"), [(D,)])[0]

    def shard_cols(a, width):
        return lax.dynamic_slice(a, (0, q * width), (a.shape[0], width))

    grads = {
        "c_ctx": g_c_ctx, "norm_g": g_ng, "b_ada": g_b_ada,
        "w_conv_a": shard_cols(g_wca, Wq)[None], "w_conv_b": shard_cols(g_wcb, Wq)[None], "b_conv_b": g_bcb,
        "lru_wa": g_wa[None], "lru_ba": shard_cols(g_ba, Wq)[None], "lru_wx": g_wx[None],
        "lru_bx": shard_cols(g_bx, Wq)[None], "lru_lambda": shard_cols(g_lam, Wq)[None], "final_g": g_fg[0],
    }
    small_names = list(grads)
    given = dict(c_ctx=(c_ctx, m_c_ctx, v_c_ctx), norm_g=(norm_g, m_norm_g, v_norm_g), b_ada=(b_ada, m_b_ada, v_b_ada),
                 w_conv_a=(w_conv_a, m_w_conv_a, v_w_conv_a), w_conv_b=(w_conv_b, m_w_conv_b, v_w_conv_b),
                 b_conv_b=(b_conv_b, m_b_conv_b, v_b_conv_b), lru_wa=(lru_wa, m_lru_wa, v_lru_wa),
                 lru_ba=(lru_ba, m_lru_ba, v_lru_ba), lru_wx=(lru_wx, m_lru_wx, v_lru_wx),
                 lru_bx=(lru_bx, m_lru_bx, v_lru_bx), lru_lambda=(lru_lambda, m_lru_lambda, v_lru_lambda),
                 final_g=(final_g, m_final_g, v_final_g))
    shapes = [given[n][0].shape for n in small_names]
    packed = [_pack([given[n][j] for n in small_names], 2 * SUBLANES) for j in range(3)]
    packed_g = _pack([grads[n] for n in small_names], 2 * SUBLANES)
    sd, sm, sv = _adam(packed[0], packed_g, packed[1], packed[2], "adam_small")
    delta_s = dict(zip(small_names, _unpack(sd, shapes)))
    newm_s = dict(zip(small_names, _unpack(sm, shapes)))
    newv_s = dict(zip(small_names, _unpack(sv, shapes)))
    grads = {n: grads[n].reshape(given[n][0].shape) for n in small_names}

    big = {"w_ada": (w_ada, g_w_ada, m_w_ada, v_w_ada), "w_in": (w_in, g_in_shard, m_w_in, v_w_in),
           "w_out": (w_out, g_out_shard, m_w_out, v_w_out)}
    delta_b, newm_b, newv_b = {}, {}, {}
    for n, (w, g, m, v) in big.items():
        grads[n] = g[None]
        d_, m_, v_ = _adam(w[0], g, m[0], v[0], "adam_" + n)
        delta_b[n], newm_b[n], newv_b[n] = d_[None], m_[None], v_[None]

    loss = lax.psum(loss_blk[0, 0], AXES)
    order = ["c_ctx", "norm_g", "w_ada", "b_ada", "w_in", "w_conv_a", "w_conv_b", "b_conv_b", "lru_wa", "lru_ba",
             "lru_wx", "lru_bx", "lru_lambda", "w_out", "final_g"]
    delta = {**delta_s, **delta_b}
    newm = {**newm_s, **newm_b}
    newv = {**newv_s, **newv_b}
    return (loss, gx[None], *[grads[n] for n in order], *[delta[n] for n in order], *[newm[n] for n in order],
            *[newv[n] for n in order])
```

```python
import functools

import jax
import jax.numpy as jnp
from jax import lax
from jax.experimental import pallas as pl
from jax.experimental.pallas import tpu as pltpu

F32 = jnp.float32
BF16 = jnp.bfloat16
MESH_ID = pl.DeviceIdType.MESH
AXES = ("x", "y", "c")

EPS = 1e-6
LRU_C = 8.0
GRID_W = 64
ADAM_LR = 0.001
ADAM_B1 = 0.9
ADAM_B2 = 0.999
ADAM_EPS = 1e-08
ADAM_WD = 0.01
ADAM_STEP = 10

LANES = 128
SUBLANES = 8
PACK_COLS = 1024
VMEM_LIMIT = 56 * 2**20
LRU_GROUP = 256


def _params(sem=None):
    return pltpu.CompilerParams(vmem_limit_bytes=VMEM_LIMIT, dimension_semantics=sem)


def _pick(dim, pref, quantum=LANES):
    if dim <= pref:
        return dim
    best = None
    for t in range(quantum, pref + 1, quantum):
        if dim % t == 0:
            best = t
    assert best is not None, (dim, pref)
    return best


def _pos():
    return lax.axis_index("x"), lax.axis_index("y"), lax.axis_index("c")


def _flip(v, bit):
    return 1 - v if bit else v


def _sigmoid(v):
    return jax.nn.sigmoid(v)


def _silu(v):
    return v * _sigmoid(v)


def _dsilu(v):
    s = _sigmoid(v)
    return s * (1.0 + v * (1.0 - s))


def _neg_expm1(z):
    series = -z * (1.0 + z * (0.5 + z * (1.0 / 6.0 + z * (1.0 / 24.0))))
    return jnp.where(z > -0.03, series, 1.0 - jnp.exp(z))


def _softplus(z):
    return jnp.maximum(z, 0.0) + jnp.log1p(jnp.exp(-jnp.abs(z)))


def _matmul(a, b, *, ta=False, tb=False, tm=512, tn=512, tk=512, out_dtype=F32, name,
            a_rows=None, a_off=0, a_act=None, init=None, bias=None, dsilu_mul=None, side=None):
    rows_a = a.shape[0] if a_rows is None else a_rows
    if ta:
        K, M = rows_a, a.shape[1]
    else:
        M, K = rows_a, a.shape[1]
    N = b.shape[0] if tb else b.shape[1]
    tm, tn, tk = _pick(M, tm, SUBLANES), _pick(N, tn), _pick(K, tk)
    t_rows = tk if ta else tm
    assert a_off % t_rows == 0
    nk = K // tk
    gi, gj = M // tm, N // tn
    off_blocks = a_off // t_rows
    dims = (((0 if ta else 1,), (1 if tb else 0,)), ((), ()))
    extras = [e for e in (init, bias, dsilu_mul) if e is not None]
    n_sin = len(side["ins"]) if side else 0
    n_sout = len(side["outs"]) if side else 0

    def body(a_ref, b_ref, *rest):
        rest = list(rest)
        init_ref = rest.pop(0) if init is not None else None
        bias_ref = rest.pop(0) if bias is not None else None
        dsm_ref = rest.pop(0) if dsilu_mul is not None else None
        side_in = [rest.pop(0) for _ in range(n_sin)]
        o_ref = rest.pop(0)
        side_out = [rest.pop(0) for _ in range(n_sout)]
        acc_ref = rest.pop(0) if nk > 1 else None
        side_scr = rest
        i, j, k = pl.program_id(0), pl.program_id(1), pl.program_id(2)

        if side:
            @pl.when((i == 0) & (j == 0) & (k == 0))
            def _():
                side["start"](side_in, side_out, side_scr)

        av = a_ref[...]
        if a_act == "silu":
            av = _silu(av)
        prod = lax.dot_general(av, b_ref[...], dims, preferred_element_type=F32)

        def finish(r):
            if bias_ref is not None:
                r = r + bias_ref[...]
            if dsm_ref is not None:
                r = r * _dsilu(dsm_ref[...])
            o_ref[...] = r.astype(o_ref.dtype)

        if nk == 1:
            finish(prod if init_ref is None else prod + init_ref[...])
        else:
            @pl.when(k == 0)
            def _():
                acc_ref[...] = prod if init_ref is None else prod + init_ref[...]

            @pl.when(k > 0)
            def _():
                acc_ref[...] += prod

            @pl.when(k == nk - 1)
            def _():
                finish(acc_ref[...])

        if side:
            @pl.when((i == gi - 1) & (j == gj - 1) & (k == nk - 1))
            def _():
                side["finish"](side_in, side_out, side_scr)

    if ta:
        a_spec = pl.BlockSpec((tk, tm), lambda i, j, k: (k + off_blocks, i))
    else:
        a_spec = pl.BlockSpec((tm, tk), lambda i, j, k: (i + off_blocks, k))
    if tb:
        b_spec = pl.BlockSpec((tn, tk), lambda i, j, k: (j, k))
    else:
        b_spec = pl.BlockSpec((tk, tn), lambda i, j, k: (k, j))
    in_specs = [a_spec, b_spec]
    if init is not None:
        in_specs.append(pl.BlockSpec((tm, tn), lambda i, j, k: (i, j)))
    if bias is not None:
        in_specs.append(pl.BlockSpec((1, tn), lambda i, j, k: (0, j)))
    if dsilu_mul is not None:
        in_specs.append(pl.BlockSpec((1, tn), lambda i, j, k: (0, j)))
    hbm = pl.BlockSpec(memory_space=pl.ANY)
    res = pl.pallas_call(
        body, name=name, grid=(gi, gj, nk),
        in_specs=in_specs + [hbm] * n_sin,
        out_specs=[pl.BlockSpec((tm, tn), lambda i, j, k: (i, j))] + [hbm] * n_sout,
        out_shape=[jax.ShapeDtypeStruct((M, N), out_dtype)] + (list(side["outs"]) if side else []),
        scratch_shapes=([pltpu.VMEM((tm, tn), F32)] if nk > 1 else []) + (list(side["scratch"]) if side else []),
        compiler_params=_params(("arbitrary",) * 3 if side else ("parallel", "parallel", "arbitrary")),
    )(a, b, *extras, *(side["ins"] if side else []))
    return (res[0], res[1:]) if side else res[0]


def _elementwise(fn, ins, outs, *, rows, cols, name, tr=256):
    tr = _pick(rows, tr, 2 * SUBLANES)
    n_in = len(ins)

    def body(*refs):
        vals = fn(*[r[...] for r in refs[:n_in]])
        if not isinstance(vals, (tuple, list)):
            vals = (vals,)
        for r, v in zip(refs[n_in:], vals, strict=True):
            r[...] = v.astype(r.dtype)

    def spec(off):
        assert off % tr == 0
        ob = off // tr
        return pl.BlockSpec((tr, cols), lambda i: (i + ob, 0))

    res = pl.pallas_call(
        body, name=name, grid=(rows // tr,),
        in_specs=[spec(off) for _, off in ins],
        out_specs=[spec(0) for _ in outs],
        out_shape=[jax.ShapeDtypeStruct((rows, cols), dt) for dt in outs],
        compiler_params=_params(("parallel",)),
    )(*[a for a, _ in ins])
    return res


def _adam_math(w, g, m, v):
    m = ADAM_B1 * m + (1.0 - ADAM_B1) * g
    v = ADAM_B2 * v + (1.0 - ADAM_B2) * (g * g)
    m_hat = m / (1.0 - ADAM_B1 ** ADAM_STEP)
    v_hat = v / (1.0 - ADAM_B2 ** ADAM_STEP)
    delta = -ADAM_LR * (m_hat / (jnp.sqrt(v_hat) + ADAM_EPS) + ADAM_WD * w)
    return delta, m, v


def _adam(w, g, m, v, name):
    rows, cols = w.shape
    return _elementwise(_adam_math, [(w, 0), (g, 0), (m, 0), (v, 0)], [F32, F32, F32],
                        rows=rows, cols=cols, name=name)


def _pack(arrs, row_quantum):
    flat = jnp.concatenate([a.reshape(-1).astype(F32) for a in arrs])
    n = flat.shape[0]
    q = row_quantum * PACK_COLS
    total = -(-n // q) * q
    flat = jnp.pad(flat, (0, total - n))
    return flat.reshape(total // PACK_COLS, PACK_COLS)


def _unpack(buf, shapes):
    flat = buf.reshape(-1)
    out, off = [], 0
    for s in shapes:
        n = 1
        for d in s:
            n *= d
        out.append(flat[off:off + n].reshape(s))
        off += n
    return out


def _allreduce8(buf, name):
    R, C = buf.shape
    assert R % (8 * SUBLANES) == 0
    m = R // 8

    def body(x_ref, o_ref, recv, red, s1, r1, s2, r2):
        x, y, c = _pos()
        me = 4 * x + 2 * y + c

        def peer(k):
            px, py, pc = _flip(x, (k >> 2) & 1), _flip(y, (k >> 1) & 1), _flip(c, k & 1)
            return (px, py, pc), 4 * px + 2 * py + pc

        def rows(ref, idx):
            return ref.at[pl.ds(pl.multiple_of(idx * m, SUBLANES), m), :]

        def scatter(k):
            dev, p = peer(k)
            return pltpu.make_async_remote_copy(src_ref=rows(x_ref, p), dst_ref=recv.at[k], send_sem=s1.at[k],
                                                recv_sem=r1.at[k], device_id=dev, device_id_type=MESH_ID)

        def share(k):
            dev, p = peer(k)
            return pltpu.make_async_remote_copy(src_ref=red, dst_ref=rows(o_ref, me), send_sem=s2.at[k],
                                                recv_sem=r2.at[k], device_id=dev, device_id_type=MESH_ID)

        def shared_from(k):
            dev, p = peer(k)
            return pltpu.make_async_remote_copy(src_ref=red, dst_ref=rows(o_ref, p), send_sem=s2.at[k],
                                                recv_sem=r2.at[k], device_id=dev, device_id_type=MESH_ID)

        for k in range(1, 8):
            scatter(k).start()
        acc = rows(x_ref, me)[...]
        for k in range(1, 8):
            scatter(k).wait_recv()
            acc = acc + recv[k]
        red[...] = acc
        rows(o_ref, me)[...] = acc
        for k in range(1, 8):
            share(k).start()
        for k in range(1, 8):
            shared_from(k).wait_recv()
        for k in range(1, 8):
            scatter(k).wait_send()
            share(k).wait_send()

    return pl.pallas_call(
        body, name=name,
        in_specs=[pl.BlockSpec(memory_space=pltpu.VMEM)],
        out_specs=pl.BlockSpec(memory_space=pltpu.VMEM),
        out_shape=jax.ShapeDtypeStruct((R, C), F32),
        scratch_shapes=[pltpu.VMEM((8, m, C), F32), pltpu.VMEM((m, C), F32),
                        pltpu.SemaphoreType.DMA((8,)), pltpu.SemaphoreType.DMA((8,)),
                        pltpu.SemaphoreType.DMA((8,)), pltpu.SemaphoreType.DMA((8,))],
        compiler_params=_params(),
    )(buf)


def _bounce(src, dst, buf, sem):
    cin = pltpu.make_async_copy(src, buf, sem)
    cin.start()
    cin.wait()
    cout = pltpu.make_async_copy(buf, dst, sem)
    cout.start()
    cout.wait()


def _chunk(ref, axis, idx, size):
    start = idx * size
    if axis == 0:
        return ref.at[pl.ds(start, size), :]
    return ref.at[:, pl.ds(start, size)]


def _in_proj_gather(hl, win, wout, q_arr, *, rows, tm):
    D, nq = win.shape
    dq, D2 = wout.shape
    ni = rows // tm
    ops = ((0, 1, nq, D // 2), (1, 0, dq, dq // 2))

    def body(q_ref, a_ref, win_ref, wout_ref, p_ref, gin_ref, gout_ref, b_scr, buf_out, lsem, ssem, rsem, fsem, gsem):
        j, i = pl.program_id(0), pl.program_id(1)
        x, y, c = _pos()
        q = 2 * x + y
        srcs = (win_ref, wout_ref)
        dsts = (gin_ref, gout_ref)

        def shard_window(o, chip):
            _, axis, size, _ = ops[o]
            return _chunk(dsts[o], axis, chip, size)

        def half(ref, o, core):
            return ref.at[pl.ds(core * ops[o][3], ops[o][3]), :]

        def half_window(o, chip, core):
            _, axis, size, hs = ops[o]
            if axis == 1:
                return dsts[o].at[pl.ds(core * hs, hs), pl.ds(chip * size, size)]
            return dsts[o].at[pl.ds(chip * size + core * hs, hs), :]

        def chip_of(k):
            px, py = _flip(x, (k >> 1) & 1), _flip(y, k & 1)
            return px, py, 2 * px + py

        def send(o, k):
            px, py, _ = chip_of(k)
            return pltpu.make_async_remote_copy(
                src_ref=half(srcs[o], o, c), dst_ref=half_window(o, q, c), send_sem=ssem.at[o, k],
                recv_sem=rsem.at[o, k], device_id=(px, py, c), device_id_type=MESH_ID)

        def arrive(o, k):
            px, py, pq = chip_of(k)
            landed = half_window(o, pq, c)
            pltpu.make_async_remote_copy(src_ref=landed, dst_ref=landed, send_sem=ssem.at[o, k], recv_sem=rsem.at[o, k],
                                         device_id=(px, py, c), device_id_type=MESH_ID).wait_recv()
            fw = pltpu.make_async_remote_copy(src_ref=landed, dst_ref=landed, send_sem=fsem.at[o, k],
                                              recv_sem=gsem.at[o, k], device_id=(x, y, 1 - c), device_id_type=MESH_ID)
            fw.start()
            theirs = half_window(o, pq, 1 - c)
            pltpu.make_async_remote_copy(src_ref=theirs, dst_ref=theirs, send_sem=fsem.at[o, k], recv_sem=gsem.at[o, k],
                                         device_id=(x, y, 1 - c), device_id_type=MESH_ID).wait_recv()
            fw.wait_send()

        def load_b(src):
            cp = pltpu.make_async_copy(src, b_scr, lsem.at[0])
            cp.start()
            cp.wait()

        @pl.when((j == 0) & (i == 0))
        def _():
            for o in range(2):
                for k in range(1, 4):
                    send(o, k).start()
            load_b(win_ref)
            own = pltpu.make_async_copy(b_scr, shard_window(0, q), lsem.at[0])
            own.start()
            own.wait()
            _bounce(wout_ref, shard_window(1, q), buf_out, lsem.at[1])

        for jj in range(1, 4):
            @pl.when((j == jj) & (i == 0))
            def _(jj=jj):
                arrive(0, jj)
                load_b(shard_window(0, chip_of(jj)[2]))

        p_ref[...] = jnp.dot(a_ref[...], b_scr[...], preferred_element_type=F32)

        @pl.when((j == 3) & (i == ni - 1))
        def _():
            for k in range(1, 4):
                arrive(1, k)
            for o in range(2):
                for k in range(1, 4):
                    send(o, k).wait_send()

    hbm = pl.BlockSpec(memory_space=pl.ANY)
    grid_spec = pltpu.PrefetchScalarGridSpec(
        num_scalar_prefetch=1, grid=(4, ni),
        in_specs=[pl.BlockSpec((tm, D), lambda j, i, qr: (i, 0)), hbm, hbm],
        out_specs=[pl.BlockSpec((tm, nq), lambda j, i, qr: (i, jnp.bitwise_xor(qr[0], j))), hbm, hbm],
        scratch_shapes=[pltpu.VMEM(win.shape, win.dtype), pltpu.VMEM(wout.shape, wout.dtype), pltpu.SemaphoreType.DMA((2,))]
        + [pltpu.SemaphoreType.DMA((2, 4)) for _ in range(4)])
    return pl.pallas_call(
        body, name="in_proj_gather", grid_spec=grid_spec,
        out_shape=[jax.ShapeDtypeStruct((rows, 4 * nq), F32), jax.ShapeDtypeStruct((D, 4 * nq), win.dtype),
                   jax.ShapeDtypeStruct((4 * dq, D2), wout.dtype)],
        compiler_params=_params(("arbitrary", "arbitrary")),
    )(q_arr, hl, win, wout)


def _rs_to_sibling(gs, axes):
    n = len(gs)
    shapes = []
    for g, ax in zip(gs, axes):
        s = list(g.shape)
        s[ax] //= 8
        shapes.append(tuple(s))

    def body(*refs):
        g_refs, mine, landed = refs[:n], refs[n:2 * n], refs[2 * n:3 * n]
        bufs = refs[3 * n:4 * n]
        lsem, ssem, rsem = refs[4 * n:]
        x, y, c = _pos()
        cps = []
        for o in range(n):
            size = shapes[o][axes[o]]
            for j in range(4):
                rc = pltpu.make_async_remote_copy(
                    src_ref=_chunk(g_refs[o], axes[o], 2 * j + 1 - c, size), dst_ref=landed[o].at[j],
                    send_sem=ssem.at[o, j], recv_sem=rsem.at[o, j], device_id=(x, y, 1 - c), device_id_type=MESH_ID)
                rc.start()
                cps.append(rc)
        for o in range(n):
            size = shapes[o][axes[o]]
            for j in range(4):
                _bounce(_chunk(g_refs[o], axes[o], 2 * j + c, size), mine[o].at[j], bufs[o], lsem.at[o])
        for rc in cps:
            rc.wait()

    hbm = pl.BlockSpec(memory_space=pl.ANY)
    outs = [jax.ShapeDtypeStruct((4,) + s, g.dtype) for s, g in zip(shapes, gs)]
    res = pl.pallas_call(
        body, name="rs_to_sibling", in_specs=[hbm] * n, out_specs=[hbm] * (2 * n), out_shape=outs + outs,
        scratch_shapes=[pltpu.VMEM(s, g.dtype) for s, g in zip(shapes, gs)]
        + [pltpu.SemaphoreType.DMA((n,)), pltpu.SemaphoreType.DMA((n, 4)), pltpu.SemaphoreType.DMA((n, 4))],
        compiler_params=_params(),
    )(*gs)
    return res[:n], res[n:]


def _rs_chips_side(parts):
    n = len(parts)

    def copies(p_refs, slots, scr):
        ssem, rsem = scr[n + 1], scr[n + 2]
        x, y, c = _pos()
        cps = []
        for o in range(n):
            for k in range(1, 4):
                px, py = _flip(x, (k >> 1) & 1), _flip(y, k & 1)
                cps.append(pltpu.make_async_remote_copy(
                    src_ref=p_refs[o].at[2 * px + py], dst_ref=slots[o].at[k], send_sem=ssem.at[o, k],
                    recv_sem=rsem.at[o, k], device_id=(px, py, c), device_id_type=MESH_ID))
        return cps

    def start(p_refs, slots, scr):
        for cp in copies(p_refs, slots, scr):
            cp.start()

    def finish(p_refs, slots, scr):
        x, y, _ = _pos()
        q = 2 * x + y
        for o in range(n):
            _bounce(p_refs[o].at[q], slots[o].at[0], scr[o], scr[n].at[o])
        for cp in copies(p_refs, slots, scr):
            cp.wait()

    return dict(
        ins=list(parts), outs=[jax.ShapeDtypeStruct(p.shape, p.dtype) for p in parts],
        scratch=[pltpu.VMEM(p.shape[1:], p.dtype) for p in parts]
        + [pltpu.SemaphoreType.DMA((n,)), pltpu.SemaphoreType.DMA((n, 4)), pltpu.SemaphoreType.DMA((n, 4))],
        start=start, finish=finish)


def _rs_share(rs, axes):
    n = len(rs)
    shapes = []
    for r, ax in zip(rs, axes):
        s = list(r.shape)
        s[ax] *= 2
        shapes.append(tuple(s))

    def body(*refs):
        r_refs, outs = refs[:n], refs[n:2 * n]
        bufs = refs[2 * n:3 * n]
        lsem, ssem, rsem = refs[3 * n:]
        x, y, c = _pos()
        cps = []
        for o in range(n):
            size = r_refs[o].shape[axes[o]]
            window = _chunk(outs[o], axes[o], c, size)
            rc = pltpu.make_async_remote_copy(src_ref=r_refs[o], dst_ref=window, send_sem=ssem.at[o], recv_sem=rsem.at[o],
                                              device_id=(x, y, 1 - c), device_id_type=MESH_ID)
            rc.start()
            cps.append(rc)
        for o in range(n):
            size = r_refs[o].shape[axes[o]]
            _bounce(r_refs[o], _chunk(outs[o], axes[o], c, size), bufs[o], lsem.at[o])
        for cp in cps:
            cp.wait()

    hbm = pl.BlockSpec(memory_space=pl.ANY)
    return pl.pallas_call(
        body, name="rs_share", in_specs=[hbm] * n, out_specs=[hbm] * n,
        out_shape=[jax.ShapeDtypeStruct(s, r.dtype) for s, r in zip(shapes, rs)],
        scratch_shapes=[pltpu.VMEM(r.shape, r.dtype) for r in rs] + [pltpu.SemaphoreType.DMA((n,)) for _ in range(3)],
        compiler_params=_params(),
    )(*rs)


def _rs_pair_sums(gs, axes):
    mine, landed = _rs_to_sibling(gs, axes)
    pair_sums = []
    for o, (mi, la) in enumerate(zip(mine, landed)):
        rows, cols = mi.shape[0] * mi.shape[1], mi.shape[2]
        s = _elementwise(lambda a, b: a.astype(F32) + b.astype(F32), [(mi.reshape(rows, cols), 0), (la.reshape(rows, cols), 0)],
                         [BF16], rows=rows, cols=cols, name=f"rs_pair_sum{o}")[0]
        pair_sums.append(s.reshape(mi.shape))
    return pair_sums


def _rs_finish(slots, axes):
    reduced = []
    for o, sl in enumerate(slots):
        rows, cols = sl.shape[1], sl.shape[2]
        flat = sl.reshape(4 * rows, cols)
        r = _elementwise(lambda a, b, c, d: (a.astype(F32) + b.astype(F32)) + (c.astype(F32) + d.astype(F32)),
                         [(flat, k * rows) for k in range(4)], [F32], rows=rows, cols=cols, name=f"rs_chip_sum{o}")[0]
        reduced.append(r)
    return _rs_share(reduced, axes)


def _norm_in(x, ctx, g, sc_l, sh_l, sc_c, sh_c, tr):
    L, D = x.shape
    T = ctx.shape[0]
    nx, nc = L // tr, T // tr

    def body(x_ref, c_ref, g_ref, scl, shl, scc, shc, o_ref):
        i = pl.program_id(0)

        def run(src, sc, sh):
            v = src[...]
            r = lax.rsqrt(jnp.mean(v * v, axis=-1, keepdims=True) + EPS)
            o_ref[...] = ((v * r * g_ref[...]) * (1.0 + sc[...]) + sh[...]).astype(o_ref.dtype)

        @pl.when(i < nx)
        def _():
            run(x_ref, scl, shl)

        @pl.when(i >= nx)
        def _():
            run(c_ref, scc, shc)

    vec = pl.BlockSpec((1, D), lambda i: (0, 0))
    return pl.pallas_call(
        body, name="norm_in", grid=(nx + nc,),
        in_specs=[pl.BlockSpec((tr, D), lambda i: (jnp.minimum(i, nx - 1), 0)),
                  pl.BlockSpec((tr, D), lambda i: (jnp.maximum(i - nx, 0), 0)), vec, vec, vec, vec, vec],
        out_specs=pl.BlockSpec((tr, D), lambda i: (i, 0)),
        out_shape=jax.ShapeDtypeStruct((L + T, D), BF16),
        compiler_params=_params(("arbitrary",)),
    )(x, ctx, g, sc_l, sh_l, sc_c, sh_c)


def _tmod(tl, row_w):
    assert row_w & (row_w - 1) == 0
    return lax.broadcasted_iota(jnp.int32, (tl, 1), 0) & (row_w - 1)


def _shift(z, k, tmod, row_w):
    tl = z.shape[0]
    rolled = pltpu.roll(z, k % tl, 0)
    mask = (tmod >= k) if k > 0 else (tmod < row_w + k)
    return jnp.where(mask, rolled, 0.0)


def _conv(z, w_ref, taps, left, tmod, row_w):
    out = None
    for j in range(taps):
        k = left - j
        term = (z if k == 0 else _shift(z, k, tmod, row_w)) * w_ref[j:j + 1, :]
        out = term if out is None else out + term
    return out


def _conv_bwd(dz, z, w_ref, taps, left, tmod, row_w):
    din = None
    dws = []
    for j in range(taps):
        k = left - j
        term = (dz if k == 0 else _shift(dz, -k, tmod, row_w)) * w_ref[j:j + 1, :]
        din = term if din is None else din + term
        zs = z if k == 0 else _shift(z, k, tmod, row_w)
        dws.append(jnp.sum(dz * zs, axis=0, keepdims=True))
    return din, dws


def _coeffs(xb, wd_ref, ba_ref, bx_ref, lam_ref, pre_scr, ng, gs):
    W = xb.shape[1]
    xb16 = xb.astype(BF16)
    for g in range(ng):
        pg = jnp.dot(xb16[:, g * gs:(g + 1) * gs], wd_ref[g], preferred_element_type=F32)
        pre_scr[:, g * gs:(g + 1) * gs] = pg[:, :gs]
        pre_scr[:, W + g * gs:W + (g + 1) * gs] = pg[:, gs:]
    r = _sigmoid(pre_scr[:, :W] + ba_ref[...])
    ig = _sigmoid(pre_scr[:, W:] + bx_ref[...])
    sp = _softplus(-lam_ref[...])
    la = (-LRU_C) * r * sp
    a = jnp.exp(la)
    m = jnp.sqrt(_neg_expm1(2.0 * la))
    return r, ig, sp, a, m


def _row_loop(tl, rev, step, init):
    nchunk = tl // SUBLANES

    def chunk(j, carry):
        jj = (nchunk - 1 - j) if rev else j
        c0 = pl.multiple_of(jj * SUBLANES, SUBLANES)
        for r in (range(SUBLANES - 1, -1, -1) if rev else range(SUBLANES)):
            carry = step(c0 + r, carry)
        return carry

    return lax.fori_loop(0, nchunk, chunk, init)


def _mix_fwd(P, d, h_init, wts, *, rows, row_off, row_w, tl, h_other=None, name):
    W = P.shape[1] // 6
    nt = rows // tl
    ob = row_off // tl
    rev = d == 1
    gs = min(LRU_GROUP, W)
    ng = W // gs
    wca, wcb, bcb = wts["wca"], wts["wcb"], wts["bcb"]
    wd, ba, bx, lam = wts["wd"][d], wts["ba"][d], wts["bx"][d], wts["lam"][d]

    def tile(i):
        return (nt - 1 - i) if rev else i

    def pcol(j):
        return pl.BlockSpec((tl, W), lambda i: (tile(i) + ob, j))

    vec = pl.BlockSpec((1, W), lambda i: (0, 0))
    taps = pl.BlockSpec((SUBLANES, W), lambda i: (0, 0))
    wd_spec = pl.BlockSpec(wd.shape, lambda i: (0, 0, 0))
    seq = pl.BlockSpec((tl, W), lambda i: (tile(i), 0))

    def body(*refs):
        if rev:
            (bl, cl, ul, gl, vl, ql, ho, wca_r, wcb_r, bcb_r, wd_r, ba_r, bx_r, lam_r, hin, hseq, cat,
             a_scr, b_scr, pre_scr, carry) = refs
        else:
            vl, wcb_r, bcb_r, wd_r, ba_r, bx_r, lam_r, hin, hseq, a_scr, b_scr, pre_scr, carry = refs
        i = pl.program_id(0)

        @pl.when(i == 0)
        def _():
            carry[...] = hin[...]

        tmod = _tmod(tl, row_w)
        xb = _conv(vl[...], wcb_r, 4, 2, tmod, row_w) + bcb_r[...]
        r, ig, sp, a, m = _coeffs(xb, wd_r, ba_r, bx_r, lam_r, pre_scr, ng, gs)
        a_scr[...] = a
        b_scr[...] = m * (ig * xb)

        def step(t, h):
            h = a_scr[pl.ds(t, 1), :] * h + b_scr[pl.ds(t, 1), :]
            hseq[pl.ds(t, 1), :] = h
            return h

        carry[...] = _row_loop(tl, rev, step, carry[...])

        if rev:
            ylru = ho[...] + hseq[...]
            yb = ylru * _silu(ql[...])
            ya = bl[...] * _conv(cl[...] * ul[...], wca_r, 3, 1, tmod, row_w) * _silu(gl[...])
            cat[:, :W] = ya.astype(cat.dtype)
            cat[:, W:] = yb.astype(cat.dtype)

    scratch = [pltpu.VMEM((tl, W), F32), pltpu.VMEM((tl, W), F32), pltpu.VMEM((tl, 2 * W), F32), pltpu.VMEM((1, W), F32)]
    if rev:
        in_specs = [pcol(j) for j in range(6)] + [seq, taps, taps, vec, wd_spec, vec, vec, vec, vec]
        args = [P] * 6 + [h_other, wca, wcb, bcb, wd, ba, bx, lam, h_init]
        out_specs = [seq, pl.BlockSpec((tl, 2 * W), lambda i: (tile(i), 0))]
        out_shape = [jax.ShapeDtypeStruct((rows, W), F32), jax.ShapeDtypeStruct((rows, 2 * W), BF16)]
    else:
        in_specs = [pcol(4), taps, vec, wd_spec, vec, vec, vec, vec]
        args = [P, wcb, bcb, wd, ba, bx, lam, h_init]
        out_specs = [seq]
        out_shape = [jax.ShapeDtypeStruct((rows, W), F32)]
    return pl.pallas_call(
        body, name=name, grid=(nt,), in_specs=in_specs, out_specs=out_specs, out_shape=out_shape,
        scratch_shapes=scratch, compiler_params=_params(("arbitrary",)),
    )(*args)


def _lru_bwd_tile(d, xb, dyl, hseq_v, hedge, c_scr, wd_r, ba_r, bx_r, lam_r, a_scr, d_scr, g_scr, pre_scr,
                  acc, first, tl, ng, gs):
    W = xb.shape[1]
    rev = d == 0
    r, ig, sp, a, m = _coeffs(xb, wd_r, ba_r, bx_r, lam_r, pre_scr, ng, gs)
    a_scr[...] = a
    d_scr[...] = dyl

    def step(t, c):
        g = d_scr[pl.ds(t, 1), :] + c
        g_scr[pl.ds(t, 1), :] = g
        return a_scr[pl.ds(t, 1), :] * g

    c_scr[...] = _row_loop(tl, rev, step, c_scr[...])
    g = g_scr[...]
    row = lax.broadcasted_iota(jnp.int32, (tl, 1), 0)
    if d == 0:
        hprev = jnp.where(row == 0, hedge, pltpu.roll(hseq_v, 1, 0))
    else:
        hprev = jnp.where(row == tl - 1, hedge, pltpu.roll(hseq_v, tl - 1, 0))
    u = ig * xb
    d_u = g * m
    d_la = (g * hprev) * a - (g * u) * (a * a) / m
    d_ig = d_u * xb
    dxb = d_u * ig
    d_pr = d_la * ((-LRU_C) * sp) * (r * (1.0 - r))
    d_pi = d_ig * (ig * (1.0 - ig))
    dsp = jnp.sum(d_la * ((-LRU_C) * r), axis=0, keepdims=True)
    dlam = dsp * (-_sigmoid(-lam_r[...]))
    dba = jnp.sum(d_pr, axis=0, keepdims=True)
    dbx = jnp.sum(d_pi, axis=0, keepdims=True)
    dwd_ref, dba_ref, dbx_ref, dlam_ref = acc

    @pl.when(first)
    def _():
        dwd_ref[...] = jnp.zeros_like(dwd_ref)
        dba_ref[...] = jnp.zeros_like(dba_ref)
        dbx_ref[...] = jnp.zeros_like(dbx_ref)
        dlam_ref[...] = jnp.zeros_like(dlam_ref)

    dba_ref[...] += dba
    dbx_ref[...] += dbx
    dlam_ref[...] += dlam
    xb16 = xb.astype(BF16)
    dpr16 = d_pr.astype(BF16)
    dpi16 = d_pi.astype(BF16)
    parts = []
    for gi in range(ng):
        sl = slice(gi * gs, (gi + 1) * gs)
        dp = jnp.concatenate([dpr16[:, sl], dpi16[:, sl]], axis=1)
        parts.append(lax.dot_general(dp, wd_r[gi], (((1,), (1,)), ((), ())), preferred_element_type=F32))
        dwd_ref[gi] += lax.dot_general(xb16[:, sl], dp, (((0,), (0,)), ((), ())), preferred_element_type=F32)
    return dxb + (parts[0] if ng == 1 else jnp.concatenate(parts, axis=1))


def _edge_block(h, tl, nt, d):
    W = h.shape[1]
    per = tl // SUBLANES
    if d == 0:
        return pl.BlockSpec((SUBLANES, W), lambda i: (jnp.maximum((nt - 1 - i) * per - 1, 0), 0))
    return pl.BlockSpec((SUBLANES, W), lambda i: (jnp.minimum((i + 1) * per, nt * per - 1), 0))


def _mix_bwd0(P, dcat, h0s, h_init, c_init, wts, *, rows, row_off, row_w, tl, name):
    W = P.shape[1] // 6
    nt = rows // tl
    ob = row_off // tl
    gs = min(LRU_GROUP, W)
    ng = W // gs
    wcb, bcb = wts["wcb"], wts["bcb"]
    wd, ba, bx, lam = wts["wd"][0], wts["ba"][0], wts["bx"][0], wts["lam"][0]

    def tile(i):
        return nt - 1 - i

    vec = pl.BlockSpec((1, W), lambda i: (0, 0))
    taps = pl.BlockSpec((SUBLANES, W), lambda i: (0, 0))
    wd_spec = pl.BlockSpec(wd.shape, lambda i: (0, 0, 0))
    seq = pl.BlockSpec((tl, W), lambda i: (tile(i), 0))

    def body(vl, ql, dyb, hs, hedge8, wcb_r, bcb_r, wd_r, ba_r, bx_r, lam_r, hin, cin,
             dxb_o, dwd_o, dba_o, dbx_o, dlam_o, cfin, a_scr, d_scr, g_scr, pre_scr, c_scr):
        i = pl.program_id(0)

        @pl.when(i == 0)
        def _():
            c_scr[...] = cin[...]

        tmod = _tmod(tl, row_w)
        xb = _conv(vl[...], wcb_r, 4, 2, tmod, row_w) + bcb_r[...]
        dyl = dyb[...] * _silu(ql[...])
        hedge = jnp.where(i == nt - 1, hin[...], hedge8[SUBLANES - 1:SUBLANES, :])
        dxb_o[...] = _lru_bwd_tile(0, xb, dyl, hs[...], hedge, c_scr, wd_r, ba_r, bx_r, lam_r, a_scr, d_scr, g_scr,
                                   pre_scr, (dwd_o, dba_o, dbx_o, dlam_o), i == 0, tl, ng, gs)
        cfin[...] = c_scr[...]

    return pl.pallas_call(
        body, name=name, grid=(nt,),
        in_specs=[pl.BlockSpec((tl, W), lambda i: (tile(i) + ob, 4)), pl.BlockSpec((tl, W), lambda i: (tile(i) + ob, 5)),
                  pl.BlockSpec((tl, W), lambda i: (tile(i), 1)), seq, _edge_block(h0s, tl, nt, 0),
                  taps, vec, wd_spec, vec, vec, vec, vec, vec],
        out_specs=[seq, wd_spec, vec, vec, vec, vec],
        out_shape=[jax.ShapeDtypeStruct((rows, W), F32), jax.ShapeDtypeStruct(wd.shape, F32)]
        + [jax.ShapeDtypeStruct((1, W), F32)] * 4,
        scratch_shapes=[pltpu.VMEM((tl, W), F32)] * 3 + [pltpu.VMEM((tl, 2 * W), F32), pltpu.VMEM((1, W), F32)],
        compiler_params=_params(("arbitrary",)),
    )(P, P, dcat, h0s, h0s, wcb, bcb, wd, ba, bx, lam, h_init, c_init)


def _mix_bwd1(P, dcat, h0s, h1s, dxb0, h_init, c_init, wts, *, rows, row_off, row_w, tl, name):
    W = P.shape[1] // 6
    nt = rows // tl
    ob = row_off // tl
    gs = min(LRU_GROUP, W)
    ng = W // gs
    wca, wcb, bcb = wts["wca"], wts["wcb"], wts["bcb"]
    wd, ba, bx, lam = wts["wd"][1], wts["ba"][1], wts["bx"][1], wts["lam"][1]

    vec = pl.BlockSpec((1, W), lambda i: (0, 0))
    taps = pl.BlockSpec((SUBLANES, W), lambda i: (0, 0))
    wd_spec = pl.BlockSpec(wd.shape, lambda i: (0, 0, 0))
    seq = pl.BlockSpec((tl, W), lambda i: (i, 0))

    def body(bl, cl, ul, gl, vl, ql, dya, dyb, h0, h1, hedge8, dx0, wca_r, wcb_r, bcb_r, wd_r, ba_r, bx_r, lam_r, hin, cin,
             dp_o, dwd_o, dba_o, dbx_o, dlam_o, dwca_o, dwcb_o, dbcb_o, cfin, a_scr, d_scr, g_scr, pre_scr, c_scr):
        i = pl.program_id(0)

        @pl.when(i == 0)
        def _():
            c_scr[...] = cin[...]
            dwca_o[...] = jnp.zeros_like(dwca_o)
            dwcb_o[...] = jnp.zeros_like(dwcb_o)
            dbcb_o[...] = jnp.zeros_like(dbcb_o)

        tmod = _tmod(tl, row_w)
        v = vl[...]
        q = ql[...]
        xb = _conv(v, wcb_r, 4, 2, tmod, row_w) + bcb_r[...]
        sq = _sigmoid(q)
        dyl = dyb[...] * (q * sq)
        hedge = jnp.where(i == nt - 1, hin[...], hedge8[0:1, :])
        dxb = dx0[...] + _lru_bwd_tile(1, xb, dyl, h1[...], hedge, c_scr, wd_r, ba_r, bx_r, lam_r, a_scr, d_scr, g_scr,
                                       pre_scr, (dwd_o, dba_o, dbx_o, dlam_o), i == 0, tl, ng, gs)
        cfin[...] = c_scr[...]
        dv, dwb = _conv_bwd(dxb, v, wcb_r, 4, 2, tmod, row_w)
        for j in range(4):
            dwcb_o[j:j + 1, :] += dwb[j]
        dbcb_o[...] += jnp.sum(dxb, axis=0, keepdims=True)
        dq = dyb[...] * (h0[...] + h1[...]) * (sq * (1.0 + q * (1.0 - sq)))
        b_, c_, u_, g_ = bl[...], cl[...], ul[...], gl[...]
        z = c_ * u_
        cz = _conv(z, wca_r, 3, 1, tmod, row_w)
        sgm = _sigmoid(g_)
        sg = g_ * sgm
        da = dya[...]
        dz, dwa = _conv_bwd(da * b_ * sg, z, wca_r, 3, 1, tmod, row_w)
        for j in range(3):
            dwca_o[j:j + 1, :] += dwa[j]
        dp_o[:, 0 * W:1 * W] = (da * cz * sg).astype(dp_o.dtype)
        dp_o[:, 1 * W:2 * W] = (dz * u_).astype(dp_o.dtype)
        dp_o[:, 2 * W:3 * W] = (dz * c_).astype(dp_o.dtype)
        dp_o[:, 3 * W:4 * W] = (da * b_ * cz * (sgm * (1.0 + g_ * (1.0 - sgm)))).astype(dp_o.dtype)
        dp_o[:, 4 * W:5 * W] = dv.astype(dp_o.dtype)
        dp_o[:, 5 * W:6 * W] = dq.astype(dp_o.dtype)

    def pcol(j):
        return pl.BlockSpec((tl, W), lambda i: (i + ob, j))

    return pl.pallas_call(
        body, name=name, grid=(nt,),
        in_specs=[pcol(j) for j in range(6)]
        + [pl.BlockSpec((tl, W), lambda i: (i, 0)), pl.BlockSpec((tl, W), lambda i: (i, 1)), seq, seq,
           _edge_block(h1s, tl, nt, 1), seq, taps, taps, vec, wd_spec, vec, vec, vec, vec, vec],
        out_specs=[pl.BlockSpec((tl, 6 * W), lambda i: (i, 0)), wd_spec, vec, vec, vec, taps, taps, vec, vec],
        out_shape=[jax.ShapeDtypeStruct((rows, 6 * W), BF16), jax.ShapeDtypeStruct(wd.shape, F32)]
        + [jax.ShapeDtypeStruct((1, W), F32)] * 3
        + [jax.ShapeDtypeStruct((SUBLANES, W), F32)] * 2 + [jax.ShapeDtypeStruct((1, W), F32)] * 2,
        scratch_shapes=[pltpu.VMEM((tl, W), F32)] * 3 + [pltpu.VMEM((tl, 2 * W), F32), pltpu.VMEM((1, W), F32)],
        compiler_params=_params(("arbitrary",)),
    )(*([P] * 6), dcat, dcat, h0s, h1s, h1s, dxb0, wca, wcb, bcb, wd, ba, bx, lam, h_init, c_init)


def _loss_head(out, x, tgt, gt, fg, tr):
    L, D = x.shape

    def body(o_ref, x_ref, t_ref, gt_ref, fg_ref, dn_o, do_o, dfg_o, dgt_o, loss_o):
        i = pl.program_id(0)

        @pl.when(i == 0)
        def _():
            dfg_o[...] = jnp.zeros_like(dfg_o)
            dgt_o[...] = jnp.zeros_like(dgt_o)
            loss_o[...] = jnp.zeros_like(loss_o)

        o = o_ref[...]
        gt_v = gt_ref[...]
        fg_v = fg_ref[...]
        n = x_ref[...] + gt_v * o
        r = lax.rsqrt(jnp.mean(n * n, axis=-1, keepdims=True) + EPS)
        nr = n * r
        e = nr * fg_v - t_ref[...]
        loss_o[...] += 0.5 * jnp.sum(jnp.mean(e * e, axis=-1, keepdims=True))
        dy = e * (1.0 / D)
        dfg_o[...] += jnp.sum(dy * nr, axis=0, keepdims=True)
        qv = dy * fg_v
        dn = r * (qv - nr * jnp.mean(qv * nr, axis=-1, keepdims=True))
        dgt_o[...] += jnp.sum(dn * o, axis=0, keepdims=True)
        dn_o[...] = dn
        do_o[...] = (dn * gt_v).astype(do_o.dtype)

    blk = pl.BlockSpec((tr, D), lambda i: (i, 0))
    vec = pl.BlockSpec((1, D), lambda i: (0, 0))
    return pl.pallas_call(
        body, name="loss_head", grid=(L // tr,), in_specs=[blk, blk, blk, vec, vec],
        out_specs=[blk, blk, vec, vec, pl.BlockSpec((SUBLANES, LANES), lambda i: (0, 0))],
        out_shape=[jax.ShapeDtypeStruct((L, D), F32), jax.ShapeDtypeStruct((L, D), BF16),
                   jax.ShapeDtypeStruct((1, D), F32), jax.ShapeDtypeStruct((1, D), F32),
                   jax.ShapeDtypeStruct((SUBLANES, LANES), F32)],
        compiler_params=_params(("arbitrary",)),
    )(out, x, tgt, gt, fg)


def _norm_bwd(dhl, x, dn, g, sc, tr, name):
    L, D = x.shape
    with_x = dn is not None

    def body(*refs):
        if with_x:
            d_ref, x_ref, dn_ref, g_ref, sc_ref, gx_o, dsh_o, dsc_o, dg_o = refs
        else:
            d_ref, x_ref, g_ref, sc_ref, dsh_o, dsc_o, dg_o = refs
        i = pl.program_id(0)

        @pl.when(i == 0)
        def _():
            dsh_o[...] = jnp.zeros_like(dsh_o)
            dsc_o[...] = jnp.zeros_like(dsc_o)
            dg_o[...] = jnp.zeros_like(dg_o)

        d = d_ref[...]
        xv = x_ref[...]
        g_v = g_ref[...]
        r = lax.rsqrt(jnp.mean(xv * xv, axis=-1, keepdims=True) + EPS)
        xr = xv * r
        dsh_o[...] += jnp.sum(d, axis=0, keepdims=True)
        dsc_o[...] += jnp.sum(d * (xr * g_v), axis=0, keepdims=True)
        dxn = d * (1.0 + sc_ref[...])
        dg_o[...] += jnp.sum(dxn * xr, axis=0, keepdims=True)
        if with_x:
            qv = dxn * g_v
            gx_o[...] = r * (qv - xr * jnp.mean(qv * xr, axis=-1, keepdims=True)) + dn_ref[...]

    blk = pl.BlockSpec((tr, D), lambda i: (i, 0))
    vec = pl.BlockSpec((1, D), lambda i: (0, 0))
    vshape = jax.ShapeDtypeStruct((1, D), F32)
    res = pl.pallas_call(
        body, name=name, grid=(L // tr,),
        in_specs=[blk, blk] + ([blk] if with_x else []) + [vec, vec],
        out_specs=([blk] if with_x else []) + [vec, vec, vec],
        out_shape=([jax.ShapeDtypeStruct((L, D), F32)] if with_x else []) + [vshape] * 3,
        compiler_params=_params(("arbitrary",)),
    )(*([dhl, x] + ([dn] if with_x else []) + [g, sc]))
    return res if with_x else [None] + list(res)


def _pack_blockdiag(wa, wx, gs):
    H, hd, _ = wa.shape
    hp = gs // hd
    ng = H // hp
    eye = jnp.eye(hp, dtype=wa.dtype)

    def bd(w):
        return jnp.einsum("gpij,pq->gpiqj", w.reshape(ng, hp, hd, hd), eye).reshape(ng, gs, gs)

    return jnp.concatenate([bd(wa), bd(wx)], axis=-1).astype(BF16)


def _unpack_blockdiag(dwd, H, hd, gs):
    hp = gs // hd
    ng = H // hp
    eye = jnp.eye(hp, dtype=dwd.dtype)

    def diag(dm):
        return jnp.einsum("gpiqj,pq->gpij", dm.reshape(ng, hp, hd, hp, hd), eye).reshape(H, hd, hd)

    return diag(dwd[:, :, :gs]), diag(dwd[:, :, gs:])


def kernel(x, c, ctx, c_ctx, norm_g, w_ada, b_ada, w_in, w_conv_a, w_conv_b, b_conv_b, lru_wa, lru_ba, lru_wx, lru_bx, lru_lambda, w_out, final_g, loss_target, m_c_ctx, m_norm_g, m_w_ada, m_b_ada, m_w_in, m_w_conv_a, m_w_conv_b, m_b_conv_b, m_lru_wa, m_lru_ba, m_lru_wx, m_lru_bx, m_lru_lambda, m_w_out, m_final_g, v_c_ctx, v_norm_g, v_w_ada, v_b_ada, v_w_in, v_w_conv_a, v_w_conv_b, v_b_conv_b, v_lru_wa, v_lru_ba, v_lru_wx, v_lru_bx, v_lru_lambda, v_w_out, v_final_g):
    xi, yi, ci = _pos()
    me = 4 * xi + 2 * yi + ci
    q = 2 * xi + yi
    first_core = (ci == 0).astype(F32)

    L, D = x.shape[1], x.shape[2]
    T = ctx.shape[1]
    W = D // 2
    Wq = W // 4
    H, hd = lru_wa.shape[2], lru_wa.shape[3]
    gs = min(LRU_GROUP, W)
    nq = w_ada.shape[2]
    tl = min(256, T, L)
    tr = min(256, T, L)
    x2, ctx2, tgt2 = x[0], ctx[0], loss_target[0]

    def place(shard, full_cols):
        z = jnp.zeros((shard.shape[0], full_cols), F32)
        return lax.dynamic_update_slice(z, shard * first_core, (0, q * shard.shape[1]))

    c_rows = lax.dynamic_update_slice(jnp.zeros((8, D), F32), c, (me, 0))
    small_in = [c_rows, place(w_conv_a[0], W), place(w_conv_b[0], W), place(lru_ba[0], W), place(lru_bx[0], W),
                place(lru_lambda[0], W)]
    small_shapes = [a.shape for a in small_in]
    gathered = _allreduce8(_pack(small_in, 8 * SUBLANES), "gather_small")
    c_all, wca, wcb, ba_all, bx_all, lam_all = _unpack(gathered, small_shapes)

    s_rows = jnp.concatenate([c_all, c_ctx[None, :], jnp.zeros((7, D), F32)], axis=0)
    mod_part = _matmul(s_rows, w_ada[0], a_act="silu", bias=lax.dynamic_slice(b_ada, (0, q * nq), (1, nq)),
                       tm=16, tn=nq, tk=512, name="ada_fwd")
    mod_all = _allreduce8(_pack([place(mod_part, 4 * nq)], 8 * SUBLANES), "gather_mod")
    mod_all = _unpack(mod_all, [(16, 4 * nq)])[0]
    mod_l = lax.dynamic_slice(mod_all, (me, 0), (1, 3 * D))
    mod_c = mod_all[8:9]
    sh_l, sc_l, gt_l = mod_l[:, :D], mod_l[:, D:2 * D], mod_l[:, 2 * D:]
    sh_c, sc_c = mod_c[:, :D], mod_c[:, D:2 * D]

    pad_taps = lambda w: jnp.pad(w, ((0, SUBLANES - w.shape[0]), (0, 0)))
    wts = {
        "wca": pad_taps(wca), "wcb": pad_taps(wcb), "bcb": b_conv_b,
        "wd": [_pack_blockdiag(lru_wa[0, d], lru_wx[0, d], gs) for d in range(2)],
        "ba": [ba_all[d:d + 1] for d in range(2)], "bx": [bx_all[d:d + 1] for d in range(2)],
        "lam": [lam_all[d:d + 1] for d in range(2)],
    }

    hl = _norm_in(x2, ctx2, norm_g, sc_l, sh_l, sc_c, sh_c, tr)
    p_lat, win_full, wout_full = _in_proj_gather(hl, w_in[0].astype(BF16), w_out[0].astype(BF16),
                                                 jnp.reshape(q, (1,)).astype(jnp.int32), rows=L, tm=min(1024, L))
    p_ctx = _matmul(hl, win_full, a_rows=T, a_off=L, tm=T, tn=1536, tk=D, name="in_proj_ctx")
    zero_w = jnp.zeros((1, W), F32)
    c0s = _mix_fwd(p_ctx, 0, zero_w, wts, rows=T, row_off=0, row_w=T, tl=tl, name="ctx_fwd0")[0]
    c1s, _ = _mix_fwd(p_ctx, 1, zero_w, wts, rows=T, row_off=0, row_w=T, tl=tl, h_other=c0s, name="ctx_fwd1")
    h0_init, h1_init = c0s[T - 1:T], c1s[0:1]
    h0s = _mix_fwd(p_lat, 0, h0_init, wts, rows=L, row_off=0, row_w=GRID_W, tl=tl, name="mix_fwd0")[0]
    h1s, cat = _mix_fwd(p_lat, 1, h1_init, wts, rows=L, row_off=0, row_w=GRID_W, tl=tl, h_other=h0s, name="mix_fwd1")
    out = _matmul(cat, wout_full, tm=512, tn=D, tk=2 * W, name="out_proj")
    dn, dout, dfg, dgt, loss_blk = _loss_head(out, x2, tgt2, gt_l, final_g[None, :], tr)

    dcat = _matmul(dout, wout_full, tb=True, tm=512, tn=2 * W, tk=D, name="out_proj_bwd")
    gw_out = _matmul(cat, dout, ta=True, tm=1024, tn=D, tk=1024, out_dtype=BF16, name="w_out_grad")
    dxb0, dwd0, dba0, dbx0, dlam0, ch0 = _mix_bwd0(p_lat, dcat, h0s, h0_init, zero_w, wts, rows=L, row_off=0,
                                                   row_w=GRID_W, tl=tl, name="mix_bwd0")
    dp, dwd1, dba1, dbx1, dlam1, dwca, dwcb, dbcb, ch1 = _mix_bwd1(
        p_lat, dcat, h0s, h1s, dxb0, h1_init, zero_w, wts, rows=L, row_off=0, row_w=GRID_W, tl=tl, name="mix_bwd1")
    zero_cat = jnp.zeros((T, 2 * W), F32)
    cxb0, cwd0, cba0, cbx0, clam0, _ = _mix_bwd0(p_ctx, zero_cat, c0s, zero_w, ch0, wts, rows=T, row_off=0,
                                                 row_w=T, tl=tl, name="ctx_bwd0")
    dp_c, cwd1, cba1, cbx1, clam1, cwca, cwcb, cbcb, _ = _mix_bwd1(
        p_ctx, zero_cat, c0s, c1s, cxb0, zero_w, ch1, wts, rows=T, row_off=0, row_w=T, tl=tl, name="ctx_bwd1")

    gw_in_ctx = _matmul(hl, dp_c, ta=True, a_rows=T, a_off=L, tm=1024, tn=1536, tk=T, name="w_in_grad_ctx")
    gw_in = _matmul(hl, dp, ta=True, a_rows=L, tm=1024, tn=1536, tk=1024, init=gw_in_ctx, out_dtype=BF16,
                    name="w_in_grad")
    rs_axes = [1, 0]
    pair_sums = _rs_pair_sums([gw_in, gw_out], rs_axes)
    dhl, rs_slots = _matmul(dp, win_full, tb=True, tm=1024, tn=D, tk=1024, name="in_proj_bwd",
                            side=_rs_chips_side(pair_sums))
    dhc = _matmul(dp_c, win_full, tb=True, tm=T, tn=D, tk=512, name="in_proj_bwd_ctx")
    gx, dsh_l, dsc_l, dng_l = _norm_bwd(dhl, x2, dn, norm_g, sc_l, tr, "norm_bwd")
    _, dsh_c, dsc_c, dng_c = _norm_bwd(dhc, ctx2, None, norm_g, sc_c, tr, "norm_bwd_ctx")

    g_in_shard, g_out_shard = _rs_finish(rs_slots, rs_axes)

    dwa0, dwx0 = _unpack_blockdiag(dwd0 + cwd0, H, hd, gs)
    dwa1, dwx1 = _unpack_blockdiag(dwd1 + cwd1, H, hd, gs)
    zeros_d = jnp.zeros((1, D), F32)
    dmod_l = jnp.concatenate([dsh_l, dsc_l, dgt], axis=1)
    dmod_c = jnp.concatenate([dsh_c, dsc_c, zeros_d], axis=1)
    small_g = [
        lax.dynamic_update_slice(jnp.zeros((8, 3 * D), F32), dmod_l, (me, 0)), dmod_c,
        dfg, dng_l + dng_c, (dwca + cwca)[:3], (dwcb + cwcb)[:4], dbcb + cbcb,
        jnp.stack([dwa0, dwa1]), jnp.stack([dwx0, dwx1]),
        jnp.concatenate([dba0 + cba0, dba1 + cba1], axis=0), jnp.concatenate([dbx0 + cbx0, dbx1 + cbx1], axis=0),
        jnp.concatenate([dlam0 + clam0, dlam1 + clam1], axis=0),
    ]
    g_shapes = [a.shape for a in small_g]
    (g_rows, g_modc, g_fg, g_ng, g_wca, g_wcb, g_bcb, g_wa, g_wx, g_ba, g_bx, g_lam) = _unpack(
        _allreduce8(_pack(small_g, 8 * SUBLANES), "reduce_small"), g_shapes)

    g_mod = jnp.concatenate([g_rows, g_modc, jnp.zeros((7, 3 * D), F32)], axis=0)
    g_mod_q = lax.dynamic_slice(g_mod, (0, q * nq), (16, nq))
    g_w_ada = _matmul(s_rows, g_mod_q, ta=True, a_act="silu", tm=1024, tn=nq, tk=16, name="w_ada_grad")
    g_b_ada = jnp.sum(g_mod[:9], axis=0, keepdims=True)
    gc_part = _matmul(jnp.pad(lax.dynamic_slice(g_modc, (0, q * nq), (1, nq)), ((0, 7), (0, 0))), w_ada[0], tb=True,
                      dsilu_mul=c_ctx[None, :], tm=8, tn=D, tk=512, name="c_ctx_grad")
    g_c_ctx = _unpack(_allreduce8(_pack([gc_part[0:1] * first_core], 8 * SUBLANES), "reduce_c_ctx"), [(D,)])[0]

    def shard_cols(a, width):
        return lax.dynamic_slice(a, (0, q * width), (a.shape[0], width))

    grads = {
        "c_ctx": g_c_ctx, "norm_g": g_ng, "b_ada": g_b_ada,
        "w_conv_a": shard_cols(g_wca, Wq)[None], "w_conv_b": shard_cols(g_wcb, Wq)[None], "b_conv_b": g_bcb,
        "lru_wa": g_wa[None], "lru_ba": shard_cols(g_ba, Wq)[None], "lru_wx": g_wx[None],
        "lru_bx": shard_cols(g_bx, Wq)[None], "lru_lambda": shard_cols(g_lam, Wq)[None], "final_g": g_fg[0],
    }
    small_names = list(grads)
    given = dict(c_ctx=(c_ctx, m_c_ctx, v_c_ctx), norm_g=(norm_g, m_norm_g, v_norm_g), b_ada=(b_ada, m_b_ada, v_b_ada),
                 w_conv_a=(w_conv_a, m_w_conv_a, v_w_conv_a), w_conv_b=(w_conv_b, m_w_conv_b, v_w_conv_b),
                 b_conv_b=(b_conv_b, m_b_conv_b, v_b_conv_b), lru_wa=(lru_wa, m_lru_wa, v_lru_wa),
                 lru_ba=(lru_ba, m_lru_ba, v_lru_ba), lru_wx=(lru_wx, m_lru_wx, v_lru_wx),
                 lru_bx=(lru_bx, m_lru_bx, v_lru_bx), lru_lambda=(lru_lambda, m_lru_lambda, v_lru_lambda),
                 final_g=(final_g, m_final_g, v_final_g))
    shapes = [given[n][0].shape for n in small_names]
    packed = [_pack([given[n][j] for n in small_names], 2 * SUBLANES) for j in range(3)]
    packed_g = _pack([grads[n] for n in small_names], 2 * SUBLANES)
    sd, sm, sv = _adam(packed[0], packed_g, packed[1], packed[2], "adam_small")
    delta_s = dict(zip(small_names, _unpack(sd, shapes)))
    newm_s = dict(zip(small_names, _unpack(sm, shapes)))
    newv_s = dict(zip(small_names, _unpack(sv, shapes)))
    grads = {n: grads[n].reshape(given[n][0].shape) for n in small_names}

    big = {"w_ada": (w_ada, g_w_ada, m_w_ada, v_w_ada), "w_in": (w_in, g_in_shard, m_w_in, v_w_in),
           "w_out": (w_out, g_out_shard, m_w_out, v_w_out)}
    delta_b, newm_b, newv_b = {}, {}, {}
    for n, (w, g, m, v) in big.items():
        grads[n] = g[None]
        d_, m_, v_ = _adam(w[0], g, m[0], v[0], "adam_" + n)
        delta_b[n], newm_b[n], newv_b[n] = d_[None], m_[None], v_[None]

    loss = lax.psum(loss_blk[0, 0], AXES)
    order = ["c_ctx", "norm_g", "w_ada", "b_ada", "w_in", "w_conv_a", "w_conv_b", "b_conv_b", "lru_wa", "lru_ba",
             "lru_wx", "lru_bx", "lru_lambda", "w_out", "final_g"]
    delta = {**delta_s, **delta_b}
    newm = {**newm_s, **newm_b}
    newv = {**newv_s, **newv_b}
    return (loss, gx[None], *[grads[n] for n in order], *[delta[n] for n in order], *[newm[n] for n in order],
            *[newv[n] for n in order])
```

```python
import functools

import jax
import jax.numpy as jnp
from jax import lax
from jax.experimental import pallas as pl
from jax.experimental.pallas import tpu as pltpu

F32 = jnp.float32
BF16 = jnp.bfloat16
MESH_ID = pl.DeviceIdType.MESH
AXES = ("x", "y", "c")

EPS = 1e-6
LRU_C = 8.0
GRID_W = 64
ADAM_LR = 0.001
ADAM_B1 = 0.9
ADAM_B2 = 0.999
ADAM_EPS = 1e-08
ADAM_WD = 0.01
ADAM_STEP = 10

LANES = 128
SUBLANES = 8
PACK_COLS = 1024
VMEM_LIMIT = 56 * 2**20
LRU_GROUP = 256


def _params(sem=None):
    return pltpu.CompilerParams(vmem_limit_bytes=VMEM_LIMIT, dimension_semantics=sem)


def _pick(dim, pref, quantum=LANES):
    if dim <= pref:
        return dim
    best = None
    for t in range(quantum, pref + 1, quantum):
        if dim % t == 0:
            best = t
    assert best is not None, (dim, pref)
    return best


def _pos():
    return lax.axis_index("x"), lax.axis_index("y"), lax.axis_index("c")


def _flip(v, bit):
    return 1 - v if bit else v


def _sigmoid(v):
    return jax.nn.sigmoid(v)


def _silu(v):
    return v * _sigmoid(v)


def _dsilu(v):
    s = _sigmoid(v)
    return s * (1.0 + v * (1.0 - s))


def _neg_expm1(z):
    series = -z * (1.0 + z * (0.5 + z * (1.0 / 6.0 + z * (1.0 / 24.0))))
    return jnp.where(z > -0.03, series, 1.0 - jnp.exp(z))


def _softplus(z):
    return jnp.maximum(z, 0.0) + jnp.log1p(jnp.exp(-jnp.abs(z)))


def _matmul(a, b, *, ta=False, tb=False, tm=512, tn=512, tk=512, out_dtype=F32, name,
            a_rows=None, a_off=0, a_act=None, init=None, bias=None, dsilu_mul=None, side=None):
    rows_a = a.shape[0] if a_rows is None else a_rows
    if ta:
        K, M = rows_a, a.shape[1]
    else:
        M, K = rows_a, a.shape[1]
    N = b.shape[0] if tb else b.shape[1]
    tm, tn, tk = _pick(M, tm, SUBLANES), _pick(N, tn), _pick(K, tk)
    t_rows = tk if ta else tm
    assert a_off % t_rows == 0
    nk = K // tk
    gi, gj = M // tm, N // tn
    off_blocks = a_off // t_rows
    dims = (((0 if ta else 1,), (1 if tb else 0,)), ((), ()))
    extras = [e for e in (init, bias, dsilu_mul) if e is not None]
    n_sin = len(side["ins"]) if side else 0
    n_sout = len(side["outs"]) if side else 0

    def body(a_ref, b_ref, *rest):
        rest = list(rest)
        init_ref = rest.pop(0) if init is not None else None
        bias_ref = rest.pop(0) if bias is not None else None
        dsm_ref = rest.pop(0) if dsilu_mul is not None else None
        side_in = [rest.pop(0) for _ in range(n_sin)]
        o_ref = rest.pop(0)
        side_out = [rest.pop(0) for _ in range(n_sout)]
        acc_ref = rest.pop(0) if nk > 1 else None
        side_scr = rest
        i, j, k = pl.program_id(0), pl.program_id(1), pl.program_id(2)

        if side:
            @pl.when((i == 0) & (j == 0) & (k == 0))
            def _():
                side["start"](side_in, side_out, side_scr)

        av = a_ref[...]
        if a_act == "silu":
            av = _silu(av)
        prod = lax.dot_general(av, b_ref[...], dims, preferred_element_type=F32)

        def finish(r):
            if bias_ref is not None:
                r = r + bias_ref[...]
            if dsm_ref is not None:
                r = r * _dsilu(dsm_ref[...])
            o_ref[...] = r.astype(o_ref.dtype)

        if nk == 1:
            finish(prod if init_ref is None else prod + init_ref[...])
        else:
            @pl.when(k == 0)
            def _():
                acc_ref[...] = prod if init_ref is None else prod + init_ref[...]

            @pl.when(k > 0)
            def _():
                acc_ref[...] += prod

            @pl.when(k == nk - 1)
            def _():
                finish(acc_ref[...])

        if side:
            @pl.when((i == gi - 1) & (j == gj - 1) & (k == nk - 1))
            def _():
                side["finish"](side_in, side_out, side_scr)

    if ta:
        a_spec = pl.BlockSpec((tk, tm), lambda i, j, k: (k + off_blocks, i))
    else:
        a_spec = pl.BlockSpec((tm, tk), lambda i, j, k: (i + off_blocks, k))
    if tb:
        b_spec = pl.BlockSpec((tn, tk), lambda i, j, k: (j, k))
    else:
        b_spec = pl.BlockSpec((tk, tn), lambda i, j, k: (k, j))
    in_specs = [a_spec, b_spec]
    if init is not None:
        in_specs.append(pl.BlockSpec((tm, tn), lambda i, j, k: (i, j)))
    if bias is not None:
        in_specs.append(pl.BlockSpec((1, tn), lambda i, j, k: (0, j)))
    if dsilu_mul is not None:
        in_specs.append(pl.BlockSpec((1, tn), lambda i, j, k: (0, j)))
    hbm = pl.BlockSpec(memory_space=pl.ANY)
    res = pl.pallas_call(
        body, name=name, grid=(gi, gj, nk),
        in_specs=in_specs + [hbm] * n_sin,
        out_specs=[pl.BlockSpec((tm, tn), lambda i, j, k: (i, j))] + [hbm] * n_sout,
        out_shape=[jax.ShapeDtypeStruct((M, N), out_dtype)] + (list(side["outs"]) if side else []),
        scratch_shapes=([pltpu.VMEM((tm, tn), F32)] if nk > 1 else []) + (list(side["scratch"]) if side else []),
        compiler_params=_params(("arbitrary",) * 3 if side else ("parallel", "parallel", "arbitrary")),
    )(a, b, *extras, *(side["ins"] if side else []))
    return (res[0], res[1:]) if side else res[0]


def _elementwise(fn, ins, outs, *, rows, cols, name, tr=256):
    tr = _pick(rows, tr, 2 * SUBLANES)
    n_in = len(ins)

    def body(*refs):
        vals = fn(*[r[...] for r in refs[:n_in]])
        if not isinstance(vals, (tuple, list)):
            vals = (vals,)
        for r, v in zip(refs[n_in:], vals, strict=True):
            r[...] = v.astype(r.dtype)

    def spec(off):
        assert off % tr == 0
        ob = off // tr
        return pl.BlockSpec((tr, cols), lambda i: (i + ob, 0))

    res = pl.pallas_call(
        body, name=name, grid=(rows // tr,),
        in_specs=[spec(off) for _, off in ins],
        out_specs=[spec(0) for _ in outs],
        out_shape=[jax.ShapeDtypeStruct((rows, cols), dt) for dt in outs],
        compiler_params=_params(("parallel",)),
    )(*[a for a, _ in ins])
    return res


def _adam_math(w, g, m, v):
    m = ADAM_B1 * m + (1.0 - ADAM_B1) * g
    v = ADAM_B2 * v + (1.0 - ADAM_B2) * (g * g)
    m_hat = m / (1.0 - ADAM_B1 ** ADAM_STEP)
    v_hat = v / (1.0 - ADAM_B2 ** ADAM_STEP)
    delta = -ADAM_LR * (m_hat / (jnp.sqrt(v_hat) + ADAM_EPS) + ADAM_WD * w)
    return delta, m, v


def _adam(w, g, m, v, name):
    rows, cols = w.shape
    return _elementwise(_adam_math, [(w, 0), (g, 0), (m, 0), (v, 0)], [F32, F32, F32],
                        rows=rows, cols=cols, name=name)


def _pack(arrs, row_quantum):
    flat = jnp.concatenate([a.reshape(-1).astype(F32) for a in arrs])
    n = flat.shape[0]
    q = row_quantum * PACK_COLS
    total = -(-n // q) * q
    flat = jnp.pad(flat, (0, total - n))
    return flat.reshape(total // PACK_COLS, PACK_COLS)


def _unpack(buf, shapes):
    flat = buf.reshape(-1)
    out, off = [], 0
    for s in shapes:
        n = 1
        for d in s:
            n *= d
        out.append(flat[off:off + n].reshape(s))
        off += n
    return out


def _allreduce8(buf, name):
    R, C = buf.shape
    assert R % (8 * SUBLANES) == 0
    m = R // 8

    def body(x_ref, o_ref, recv, red, s1, r1, s2, r2):
        x, y, c = _pos()
        me = 4 * x + 2 * y + c

        def peer(k):
            px, py, pc = _flip(x, (k >> 2) & 1), _flip(y, (k >> 1) & 1), _flip(c, k & 1)
            return (px, py, pc), 4 * px + 2 * py + pc

        def rows(ref, idx):
            return ref.at[pl.ds(pl.multiple_of(idx * m, SUBLANES), m), :]

        def scatter(k):
            dev, p = peer(k)
            return pltpu.make_async_remote_copy(src_ref=rows(x_ref, p), dst_ref=recv.at[k], send_sem=s1.at[k],
                                                recv_sem=r1.at[k], device_id=dev, device_id_type=MESH_ID)

        def share(k):
            dev, p = peer(k)
            return pltpu.make_async_remote_copy(src_ref=red, dst_ref=rows(o_ref, me), send_sem=s2.at[k],
                                                recv_sem=r2.at[k], device_id=dev, device_id_type=MESH_ID)

        def shared_from(k):
            dev, p = peer(k)
            return pltpu.make_async_remote_copy(src_ref=red, dst_ref=rows(o_ref, p), send_sem=s2.at[k],
                                                recv_sem=r2.at[k], device_id=dev, device_id_type=MESH_ID)

        for k in range(1, 8):
            scatter(k).start()
        acc = rows(x_ref, me)[...]
        for k in range(1, 8):
            scatter(k).wait_recv()
            acc = acc + recv[k]
        red[...] = acc
        rows(o_ref, me)[...] = acc
        for k in range(1, 8):
            share(k).start()
        for k in range(1, 8):
            shared_from(k).wait_recv()
        for k in range(1, 8):
            scatter(k).wait_send()
            share(k).wait_send()

    return pl.pallas_call(
        body, name=name,
        in_specs=[pl.BlockSpec(memory_space=pltpu.VMEM)],
        out_specs=pl.BlockSpec(memory_space=pltpu.VMEM),
        out_shape=jax.ShapeDtypeStruct((R, C), F32),
        scratch_shapes=[pltpu.VMEM((8, m, C), F32), pltpu.VMEM((m, C), F32),
                        pltpu.SemaphoreType.DMA((8,)), pltpu.SemaphoreType.DMA((8,)),
                        pltpu.SemaphoreType.DMA((8,)), pltpu.SemaphoreType.DMA((8,))],
        compiler_params=_params(),
    )(buf)


def _bounce(src, dst, buf, sem):
    cin = pltpu.make_async_copy(src, buf, sem)
    cin.start()
    cin.wait()
    cout = pltpu.make_async_copy(buf, dst, sem)
    cout.start()
    cout.wait()


def _chunk(ref, axis, idx, size):
    start = idx * size
    if axis == 0:
        return ref.at[pl.ds(start, size), :]
    return ref.at[:, pl.ds(start, size)]


def _in_proj_gather(hl, win, wout, q_arr, *, rows, tm):
    D, nq = win.shape
    dq, D2 = wout.shape
    ni = rows // tm
    ops = ((0, 1, nq, D // 2), (1, 0, dq, dq // 2))

    def body(q_ref, a_ref, win_ref, wout_ref, p_ref, gin_ref, gout_ref, b_scr, buf_out, lsem, ssem, rsem, fsem, gsem):
        j, i = pl.program_id(0), pl.program_id(1)
        x, y, c = _pos()
        q = 2 * x + y
        srcs = (win_ref, wout_ref)
        dsts = (gin_ref, gout_ref)

        def shard_window(o, chip):
            _, axis, size, _ = ops[o]
            return _chunk(dsts[o], axis, chip, size)

        def half(ref, o, core):
            return ref.at[pl.ds(core * ops[o][3], ops[o][3]), :]

        def half_window(o, chip, core):
            _, axis, size, hs = ops[o]
            if axis == 1:
                return dsts[o].at[pl.ds(core * hs, hs), pl.ds(chip * size, size)]
            return dsts[o].at[pl.ds(chip * size + core * hs, hs), :]

        def chip_of(k):
            px, py = _flip(x, (k >> 1) & 1), _flip(y, k & 1)
            return px, py, 2 * px + py

        def send(o, k):
            px, py, _ = chip_of(k)
            return pltpu.make_async_remote_copy(
                src_ref=half(srcs[o], o, c), dst_ref=half_window(o, q, c), send_sem=ssem.at[o, k],
                recv_sem=rsem.at[o, k], device_id=(px, py, c), device_id_type=MESH_ID)

        def arrive(o, k):
            px, py, pq = chip_of(k)
            landed = half_window(o, pq, c)
            pltpu.make_async_remote_copy(src_ref=landed, dst_ref=landed, send_sem=ssem.at[o, k], recv_sem=rsem.at[o, k],
                                         device_id=(px, py, c), device_id_type=MESH_ID).wait_recv()
            fw = pltpu.make_async_remote_copy(src_ref=landed, dst_ref=landed, send_sem=fsem.at[o, k],
                                              recv_sem=gsem.at[o, k], device_id=(x, y, 1 - c), device_id_type=MESH_ID)
            fw.start()
            theirs = half_window(o, pq, 1 - c)
            pltpu.make_async_remote_copy(src_ref=theirs, dst_ref=theirs, send_sem=fsem.at[o, k], recv_sem=gsem.at[o, k],
                                         device_id=(x, y, 1 - c), device_id_type=MESH_ID).wait_recv()
            fw.wait_send()

        def load_b(src):
            cp = pltpu.make_async_copy(src, b_scr, lsem.at[0])
            cp.start()
            cp.wait()

        def relay(o, core):
            if core == 0:
                landed, target = half_window(o, chip_of(2)[2], 0), (x, 1 - y, 0)
            else:
                landed, target = half_window(o, chip_of(1)[2], 1), (1 - x, y, 1)
            return pltpu.make_async_remote_copy(src_ref=landed, dst_ref=landed, send_sem=ssem.at[o, 3],
                                                recv_sem=rsem.at[o, 3], device_id=target, device_id_type=MESH_ID)

        def on_core(core, fn):
            @pl.when(c == core)
            def _():
                fn()

        @pl.when((j == 0) & (i == 0))
        def _():
            for o in range(2):
                for k in (2, 1):
                    send(o, k).start()
            load_b(win_ref)
            own = pltpu.make_async_copy(b_scr, shard_window(0, q), lsem.at[0])
            own.start()
            own.wait()
            _bounce(wout_ref, shard_window(1, q), buf_out, lsem.at[1])

        @pl.when((j == 1) & (i == 0))
        def _():
            arrive(0, 2)
            on_core(0, lambda: relay(0, 0).start())
            load_b(shard_window(0, chip_of(2)[2]))

        @pl.when((j == 2) & (i == 0))
        def _():
            arrive(0, 1)
            on_core(1, lambda: relay(0, 1).start())
            load_b(shard_window(0, chip_of(1)[2]))

        @pl.when((j == 3) & (i == 0))
        def _():
            arrive(0, 3)
            load_b(shard_window(0, chip_of(3)[2]))
            arrive(1, 2)
            on_core(0, lambda: relay(1, 0).start())
            arrive(1, 1)
            on_core(1, lambda: relay(1, 1).start())

        p_ref[...] = jnp.dot(a_ref[...], b_scr[...], preferred_element_type=F32)

        @pl.when((j == 3) & (i == ni - 1))
        def _():
            arrive(1, 3)
            for o in range(2):
                for k in (2, 1):
                    send(o, k).wait_send()
                for core in range(2):
                    on_core(core, lambda o=o, core=core: relay(o, core).wait_send())

    hbm = pl.BlockSpec(memory_space=pl.ANY)
    grid_spec = pltpu.PrefetchScalarGridSpec(
        num_scalar_prefetch=1, grid=(4, ni),
        in_specs=[pl.BlockSpec((tm, D), lambda j, i, qr: (i, 0)), hbm, hbm],
        out_specs=[pl.BlockSpec((tm, nq), lambda j, i, qr: (i, jnp.bitwise_xor(qr[0], ((j & 1) << 1) | (j >> 1)))),
                   hbm, hbm],
        scratch_shapes=[pltpu.VMEM(win.shape, win.dtype), pltpu.VMEM(wout.shape, wout.dtype), pltpu.SemaphoreType.DMA((2,))]
        + [pltpu.SemaphoreType.DMA((2, 4)) for _ in range(4)])
    return pl.pallas_call(
        body, name="in_proj_gather", grid_spec=grid_spec,
        out_shape=[jax.ShapeDtypeStruct((rows, 4 * nq), F32), jax.ShapeDtypeStruct((D, 4 * nq), win.dtype),
                   jax.ShapeDtypeStruct((4 * dq, D2), wout.dtype)],
        compiler_params=_params(("arbitrary", "arbitrary")),
    )(q_arr, hl, win, wout)


def _rs_to_sibling(gs, axes):
    n = len(gs)
    shapes = []
    for g, ax in zip(gs, axes):
        s = list(g.shape)
        s[ax] //= 8
        shapes.append(tuple(s))

    def body(*refs):
        g_refs, mine, landed = refs[:n], refs[n:2 * n], refs[2 * n:3 * n]
        bufs = refs[3 * n:4 * n]
        lsem, ssem, rsem = refs[4 * n:]
        x, y, c = _pos()
        cps = []
        for o in range(n):
            size = shapes[o][axes[o]]
            for j in range(4):
                rc = pltpu.make_async_remote_copy(
                    src_ref=_chunk(g_refs[o], axes[o], 2 * j + 1 - c, size), dst_ref=landed[o].at[j],
                    send_sem=ssem.at[o, j], recv_sem=rsem.at[o, j], device_id=(x, y, 1 - c), device_id_type=MESH_ID)
                rc.start()
                cps.append(rc)
        for o in range(n):
            size = shapes[o][axes[o]]
            for j in range(4):
                _bounce(_chunk(g_refs[o], axes[o], 2 * j + c, size), mine[o].at[j], bufs[o], lsem.at[o])
        for rc in cps:
            rc.wait()

    hbm = pl.BlockSpec(memory_space=pl.ANY)
    outs = [jax.ShapeDtypeStruct((4,) + s, g.dtype) for s, g in zip(shapes, gs)]
    res = pl.pallas_call(
        body, name="rs_to_sibling", in_specs=[hbm] * n, out_specs=[hbm] * (2 * n), out_shape=outs + outs,
        scratch_shapes=[pltpu.VMEM(s, g.dtype) for s, g in zip(shapes, gs)]
        + [pltpu.SemaphoreType.DMA((n,)), pltpu.SemaphoreType.DMA((n, 4)), pltpu.SemaphoreType.DMA((n, 4))],
        compiler_params=_params(),
    )(*gs)
    return res[:n], res[n:]


def _rs_chips_side(parts):
    n = len(parts)

    def copies(p_refs, slots, scr):
        ssem, rsem = scr[n + 1], scr[n + 2]
        x, y, c = _pos()
        cps = []
        for o in range(n):
            for k in range(1, 4):
                px, py = _flip(x, (k >> 1) & 1), _flip(y, k & 1)
                cps.append(pltpu.make_async_remote_copy(
                    src_ref=p_refs[o].at[2 * px + py], dst_ref=slots[o].at[k], send_sem=ssem.at[o, k],
                    recv_sem=rsem.at[o, k], device_id=(px, py, c), device_id_type=MESH_ID))
        return cps

    def start(p_refs, slots, scr):
        for cp in copies(p_refs, slots, scr):
            cp.start()

    def finish(p_refs, slots, scr):
        x, y, _ = _pos()
        q = 2 * x + y
        for o in range(n):
            _bounce(p_refs[o].at[q], slots[o].at[0], scr[o], scr[n].at[o])
        for cp in copies(p_refs, slots, scr):
            cp.wait()

    return dict(
        ins=list(parts), outs=[jax.ShapeDtypeStruct(p.shape, p.dtype) for p in parts],
        scratch=[pltpu.VMEM(p.shape[1:], p.dtype) for p in parts]
        + [pltpu.SemaphoreType.DMA((n,)), pltpu.SemaphoreType.DMA((n, 4)), pltpu.SemaphoreType.DMA((n, 4))],
        start=start, finish=finish)


def _rs_share(rs, axes):
    n = len(rs)
    shapes = []
    for r, ax in zip(rs, axes):
        s = list(r.shape)
        s[ax] *= 2
        shapes.append(tuple(s))

    def body(*refs):
        r_refs, outs = refs[:n], refs[n:2 * n]
        bufs = refs[2 * n:3 * n]
        lsem, ssem, rsem = refs[3 * n:]
        x, y, c = _pos()
        cps = []
        for o in range(n):
            size = r_refs[o].shape[axes[o]]
            window = _chunk(outs[o], axes[o], c, size)
            rc = pltpu.make_async_remote_copy(src_ref=r_refs[o], dst_ref=window, send_sem=ssem.at[o], recv_sem=rsem.at[o],
                                              device_id=(x, y, 1 - c), device_id_type=MESH_ID)
            rc.start()
            cps.append(rc)
        for o in range(n):
            size = r_refs[o].shape[axes[o]]
            _bounce(r_refs[o], _chunk(outs[o], axes[o], c, size), bufs[o], lsem.at[o])
        for cp in cps:
            cp.wait()

    hbm = pl.BlockSpec(memory_space=pl.ANY)
    return pl.pallas_call(
        body, name="rs_share", in_specs=[hbm] * n, out_specs=[hbm] * n,
        out_shape=[jax.ShapeDtypeStruct(s, r.dtype) for s, r in zip(shapes, rs)],
        scratch_shapes=[pltpu.VMEM(r.shape, r.dtype) for r in rs] + [pltpu.SemaphoreType.DMA((n,)) for _ in range(3)],
        compiler_params=_params(),
    )(*rs)


def _rs_pair_sums(gs, axes):
    mine, landed = _rs_to_sibling(gs, axes)
    pair_sums = []
    for o, (mi, la) in enumerate(zip(mine, landed)):
        rows, cols = mi.shape[0] * mi.shape[1], mi.shape[2]
        s = _elementwise(lambda a, b: a.astype(F32) + b.astype(F32), [(mi.reshape(rows, cols), 0), (la.reshape(rows, cols), 0)],
                         [BF16], rows=rows, cols=cols, name=f"rs_pair_sum{o}")[0]
        pair_sums.append(s.reshape(mi.shape))
    return pair_sums


def _rs_finish(slots, axes):
    reduced = []
    for o, sl in enumerate(slots):
        rows, cols = sl.shape[1], sl.shape[2]
        flat = sl.reshape(4 * rows, cols)
        r = _elementwise(lambda a, b, c, d: (a.astype(F32) + b.astype(F32)) + (c.astype(F32) + d.astype(F32)),
                         [(flat, k * rows) for k in range(4)], [F32], rows=rows, cols=cols, name=f"rs_chip_sum{o}")[0]
        reduced.append(r)
    return _rs_share(reduced, axes)


def _norm_in(x, ctx, g, sc_l, sh_l, sc_c, sh_c, tr):
    L, D = x.shape
    T = ctx.shape[0]
    nx, nc = L // tr, T // tr

    def body(x_ref, c_ref, g_ref, scl, shl, scc, shc, o_ref):
        i = pl.program_id(0)

        def run(src, sc, sh):
            v = src[...]
            r = lax.rsqrt(jnp.mean(v * v, axis=-1, keepdims=True) + EPS)
            o_ref[...] = ((v * r * g_ref[...]) * (1.0 + sc[...]) + sh[...]).astype(o_ref.dtype)

        @pl.when(i < nx)
        def _():
            run(x_ref, scl, shl)

        @pl.when(i >= nx)
        def _():
            run(c_ref, scc, shc)

    vec = pl.BlockSpec((1, D), lambda i: (0, 0))
    return pl.pallas_call(
        body, name="norm_in", grid=(nx + nc,),
        in_specs=[pl.BlockSpec((tr, D), lambda i: (jnp.minimum(i, nx - 1), 0)),
                  pl.BlockSpec((tr, D), lambda i: (jnp.maximum(i - nx, 0), 0)), vec, vec, vec, vec, vec],
        out_specs=pl.BlockSpec((tr, D), lambda i: (i, 0)),
        out_shape=jax.ShapeDtypeStruct((L + T, D), BF16),
        compiler_params=_params(("arbitrary",)),
    )(x, ctx, g, sc_l, sh_l, sc_c, sh_c)


def _tmod(tl, row_w):
    assert row_w & (row_w - 1) == 0
    return lax.broadcasted_iota(jnp.int32, (tl, 1), 0) & (row_w - 1)


def _shift(z, k, tmod, row_w):
    tl = z.shape[0]
    rolled = pltpu.roll(z, k % tl, 0)
    mask = (tmod >= k) if k > 0 else (tmod < row_w + k)
    return jnp.where(mask, rolled, 0.0)


def _conv(z, w_ref, taps, left, tmod, row_w):
    out = None
    for j in range(taps):
        k = left - j
        term = (z if k == 0 else _shift(z, k, tmod, row_w)) * w_ref[j:j + 1, :]
        out = term if out is None else out + term
    return out


def _conv_bwd(dz, z, w_ref, taps, left, tmod, row_w):
    din = None
    dws = []
    for j in range(taps):
        k = left - j
        term = (dz if k == 0 else _shift(dz, -k, tmod, row_w)) * w_ref[j:j + 1, :]
        din = term if din is None else din + term
        zs = z if k == 0 else _shift(z, k, tmod, row_w)
        dws.append(jnp.sum(dz * zs, axis=0, keepdims=True))
    return din, dws


def _coeffs(xb, wd_ref, ba_ref, bx_ref, lam_ref, pre_scr, ng, gs):
    W = xb.shape[1]
    xb16 = xb.astype(BF16)
    for g in range(ng):
        pg = jnp.dot(xb16[:, g * gs:(g + 1) * gs], wd_ref[g], preferred_element_type=F32)
        pre_scr[:, g * gs:(g + 1) * gs] = pg[:, :gs]
        pre_scr[:, W + g * gs:W + (g + 1) * gs] = pg[:, gs:]
    r = _sigmoid(pre_scr[:, :W] + ba_ref[...])
    ig = _sigmoid(pre_scr[:, W:] + bx_ref[...])
    sp = _softplus(-lam_ref[...])
    la = (-LRU_C) * r * sp
    a = jnp.exp(la)
    m = jnp.sqrt(_neg_expm1(2.0 * la))
    return r, ig, sp, a, m


def _row_loop(tl, rev, step, init):
    nchunk = tl // SUBLANES

    def chunk(j, carry):
        jj = (nchunk - 1 - j) if rev else j
        c0 = pl.multiple_of(jj * SUBLANES, SUBLANES)
        for r in (range(SUBLANES - 1, -1, -1) if rev else range(SUBLANES)):
            carry = step(c0 + r, carry)
        return carry

    return lax.fori_loop(0, nchunk, chunk, init)


def _mix_fwd(P, d, h_init, wts, *, rows, row_off, row_w, tl, h_other=None, name):
    W = P.shape[1] // 6
    nt = rows // tl
    ob = row_off // tl
    rev = d == 1
    gs = min(LRU_GROUP, W)
    ng = W // gs
    wca, wcb, bcb = wts["wca"], wts["wcb"], wts["bcb"]
    wd, ba, bx, lam = wts["wd"][d], wts["ba"][d], wts["bx"][d], wts["lam"][d]

    def tile(i):
        return (nt - 1 - i) if rev else i

    def pcol(j):
        return pl.BlockSpec((tl, W), lambda i: (tile(i) + ob, j))

    vec = pl.BlockSpec((1, W), lambda i: (0, 0))
    taps = pl.BlockSpec((SUBLANES, W), lambda i: (0, 0))
    wd_spec = pl.BlockSpec(wd.shape, lambda i: (0, 0, 0))
    seq = pl.BlockSpec((tl, W), lambda i: (tile(i), 0))

    def body(*refs):
        if rev:
            (bl, cl, ul, gl, vl, ql, ho, wca_r, wcb_r, bcb_r, wd_r, ba_r, bx_r, lam_r, hin, hseq, cat,
             a_scr, b_scr, pre_scr, carry) = refs
        else:
            vl, wcb_r, bcb_r, wd_r, ba_r, bx_r, lam_r, hin, hseq, a_scr, b_scr, pre_scr, carry = refs
        i = pl.program_id(0)

        @pl.when(i == 0)
        def _():
            carry[...] = hin[...]

        tmod = _tmod(tl, row_w)
        xb = _conv(vl[...], wcb_r, 4, 2, tmod, row_w) + bcb_r[...]
        r, ig, sp, a, m = _coeffs(xb, wd_r, ba_r, bx_r, lam_r, pre_scr, ng, gs)
        a_scr[...] = a
        b_scr[...] = m * (ig * xb)

        def step(t, h):
            h = a_scr[pl.ds(t, 1), :] * h + b_scr[pl.ds(t, 1), :]
            hseq[pl.ds(t, 1), :] = h
            return h

        carry[...] = _row_loop(tl, rev, step, carry[...])

        if rev:
            ylru = ho[...] + hseq[...]
            yb = ylru * _silu(ql[...])
            ya = bl[...] * _conv(cl[...] * ul[...], wca_r, 3, 1, tmod, row_w) * _silu(gl[...])
            cat[:, :W] = ya.astype(cat.dtype)
            cat[:, W:] = yb.astype(cat.dtype)

    scratch = [pltpu.VMEM((tl, W), F32), pltpu.VMEM((tl, W), F32), pltpu.VMEM((tl, 2 * W), F32), pltpu.VMEM((1, W), F32)]
    if rev:
        in_specs = [pcol(j) for j in range(6)] + [seq, taps, taps, vec, wd_spec, vec, vec, vec, vec]
        args = [P] * 6 + [h_other, wca, wcb, bcb, wd, ba, bx, lam, h_init]
        out_specs = [seq, pl.BlockSpec((tl, 2 * W), lambda i: (tile(i), 0))]
        out_shape = [jax.ShapeDtypeStruct((rows, W), F32), jax.ShapeDtypeStruct((rows, 2 * W), BF16)]
    else:
        in_specs = [pcol(4), taps, vec, wd_spec, vec, vec, vec, vec]
        args = [P, wcb, bcb, wd, ba, bx, lam, h_init]
        out_specs = [seq]
        out_shape = [jax.ShapeDtypeStruct((rows, W), F32)]
    return pl.pallas_call(
        body, name=name, grid=(nt,), in_specs=in_specs, out_specs=out_specs, out_shape=out_shape,
        scratch_shapes=scratch, compiler_params=_params(("arbitrary",)),
    )(*args)


def _lru_bwd_tile(d, xb, dyl, hseq_v, hedge, c_scr, wd_r, ba_r, bx_r, lam_r, a_scr, d_scr, g_scr, pre_scr,
                  acc, first, tl, ng, gs):
    W = xb.shape[1]
    rev = d == 0
    r, ig, sp, a, m = _coeffs(xb, wd_r, ba_r, bx_r, lam_r, pre_scr, ng, gs)
    a_scr[...] = a
    d_scr[...] = dyl

    def step(t, c):
        g = d_scr[pl.ds(t, 1), :] + c
        g_scr[pl.ds(t, 1), :] = g
        return a_scr[pl.ds(t, 1), :] * g

    c_scr[...] = _row_loop(tl, rev, step, c_scr[...])
    g = g_scr[...]
    row = lax.broadcasted_iota(jnp.int32, (tl, 1), 0)
    if d == 0:
        hprev = jnp.where(row == 0, hedge, pltpu.roll(hseq_v, 1, 0))
    else:
        hprev = jnp.where(row == tl - 1, hedge, pltpu.roll(hseq_v, tl - 1, 0))
    u = ig * xb
    d_u = g * m
    d_la = (g * hprev) * a - (g * u) * (a * a) / m
    d_ig = d_u * xb
    dxb = d_u * ig
    d_pr = d_la * ((-LRU_C) * sp) * (r * (1.0 - r))
    d_pi = d_ig * (ig * (1.0 - ig))
    dsp = jnp.sum(d_la * ((-LRU_C) * r), axis=0, keepdims=True)
    dlam = dsp * (-_sigmoid(-lam_r[...]))
    dba = jnp.sum(d_pr, axis=0, keepdims=True)
    dbx = jnp.sum(d_pi, axis=0, keepdims=True)
    dwd_ref, dba_ref, dbx_ref, dlam_ref = acc

    @pl.when(first)
    def _():
        dwd_ref[...] = jnp.zeros_like(dwd_ref)
        dba_ref[...] = jnp.zeros_like(dba_ref)
        dbx_ref[...] = jnp.zeros_like(dbx_ref)
        dlam_ref[...] = jnp.zeros_like(dlam_ref)

    dba_ref[...] += dba
    dbx_ref[...] += dbx
    dlam_ref[...] += dlam
    xb16 = xb.astype(BF16)
    dpr16 = d_pr.astype(BF16)
    dpi16 = d_pi.astype(BF16)
    parts = []
    for gi in range(ng):
        sl = slice(gi * gs, (gi + 1) * gs)
        dp = jnp.concatenate([dpr16[:, sl], dpi16[:, sl]], axis=1)
        parts.append(lax.dot_general(dp, wd_r[gi], (((1,), (1,)), ((), ())), preferred_element_type=F32))
        dwd_ref[gi] += lax.dot_general(xb16[:, sl], dp, (((0,), (0,)), ((), ())), preferred_element_type=F32)
    return dxb + (parts[0] if ng == 1 else jnp.concatenate(parts, axis=1))


def _edge_block(h, tl, nt, d):
    W = h.shape[1]
    per = tl // SUBLANES
    if d == 0:
        return pl.BlockSpec((SUBLANES, W), lambda i: (jnp.maximum((nt - 1 - i) * per - 1, 0), 0))
    return pl.BlockSpec((SUBLANES, W), lambda i: (jnp.minimum((i + 1) * per, nt * per - 1), 0))


def _mix_bwd0(P, dcat, h0s, h_init, c_init, wts, *, rows, row_off, row_w, tl, name):
    W = P.shape[1] // 6
    nt = rows // tl
    ob = row_off // tl
    gs = min(LRU_GROUP, W)
    ng = W // gs
    wcb, bcb = wts["wcb"], wts["bcb"]
    wd, ba, bx, lam = wts["wd"][0], wts["ba"][0], wts["bx"][0], wts["lam"][0]

    def tile(i):
        return nt - 1 - i

    vec = pl.BlockSpec((1, W), lambda i: (0, 0))
    taps = pl.BlockSpec((SUBLANES, W), lambda i: (0, 0))
    wd_spec = pl.BlockSpec(wd.shape, lambda i: (0, 0, 0))
    seq = pl.BlockSpec((tl, W), lambda i: (tile(i), 0))

    def body(vl, ql, dyb, hs, hedge8, wcb_r, bcb_r, wd_r, ba_r, bx_r, lam_r, hin, cin,
             dxb_o, dwd_o, dba_o, dbx_o, dlam_o, cfin, a_scr, d_scr, g_scr, pre_scr, c_scr):
        i = pl.program_id(0)

        @pl.when(i == 0)
        def _():
            c_scr[...] = cin[...]

        tmod = _tmod(tl, row_w)
        xb = _conv(vl[...], wcb_r, 4, 2, tmod, row_w) + bcb_r[...]
        dyl = dyb[...] * _silu(ql[...])
        hedge = jnp.where(i == nt - 1, hin[...], hedge8[SUBLANES - 1:SUBLANES, :])
        dxb_o[...] = _lru_bwd_tile(0, xb, dyl, hs[...], hedge, c_scr, wd_r, ba_r, bx_r, lam_r, a_scr, d_scr, g_scr,
                                   pre_scr, (dwd_o, dba_o, dbx_o, dlam_o), i == 0, tl, ng, gs)
        cfin[...] = c_scr[...]

    return pl.pallas_call(
        body, name=name, grid=(nt,),
        in_specs=[pl.BlockSpec((tl, W), lambda i: (tile(i) + ob, 4)), pl.BlockSpec((tl, W), lambda i: (tile(i) + ob, 5)),
                  pl.BlockSpec((tl, W), lambda i: (tile(i), 1)), seq, _edge_block(h0s, tl, nt, 0),
                  taps, vec, wd_spec, vec, vec, vec, vec, vec],
        out_specs=[seq, wd_spec, vec, vec, vec, vec],
        out_shape=[jax.ShapeDtypeStruct((rows, W), F32), jax.ShapeDtypeStruct(wd.shape, F32)]
        + [jax.ShapeDtypeStruct((1, W), F32)] * 4,
        scratch_shapes=[pltpu.VMEM((tl, W), F32)] * 3 + [pltpu.VMEM((tl, 2 * W), F32), pltpu.VMEM((1, W), F32)],
        compiler_params=_params(("arbitrary",)),
    )(P, P, dcat, h0s, h0s, wcb, bcb, wd, ba, bx, lam, h_init, c_init)


def _mix_bwd1(P, dcat, h0s, h1s, dxb0, h_init, c_init, wts, *, rows, row_off, row_w, tl, name):
    W = P.shape[1] // 6
    nt = rows // tl
    ob = row_off // tl
    gs = min(LRU_GROUP, W)
    ng = W // gs
    wca, wcb, bcb = wts["wca"], wts["wcb"], wts["bcb"]
    wd, ba, bx, lam = wts["wd"][1], wts["ba"][1], wts["bx"][1], wts["lam"][1]

    vec = pl.BlockSpec((1, W), lambda i: (0, 0))
    taps = pl.BlockSpec((SUBLANES, W), lambda i: (0, 0))
    wd_spec = pl.BlockSpec(wd.shape, lambda i: (0, 0, 0))
    seq = pl.BlockSpec((tl, W), lambda i: (i, 0))

    def body(bl, cl, ul, gl, vl, ql, dya, dyb, h0, h1, hedge8, dx0, wca_r, wcb_r, bcb_r, wd_r, ba_r, bx_r, lam_r, hin, cin,
             dp_o, dwd_o, dba_o, dbx_o, dlam_o, dwca_o, dwcb_o, dbcb_o, cfin, a_scr, d_scr, g_scr, pre_scr, c_scr):
        i = pl.program_id(0)

        @pl.when(i == 0)
        def _():
            c_scr[...] = cin[...]
            dwca_o[...] = jnp.zeros_like(dwca_o)
            dwcb_o[...] = jnp.zeros_like(dwcb_o)
            dbcb_o[...] = jnp.zeros_like(dbcb_o)

        tmod = _tmod(tl, row_w)
        v = vl[...]
        q = ql[...]
        xb = _conv(v, wcb_r, 4, 2, tmod, row_w) + bcb_r[...]
        sq = _sigmoid(q)
        dyl = dyb[...] * (q * sq)
        hedge = jnp.where(i == nt - 1, hin[...], hedge8[0:1, :])
        dxb = dx0[...] + _lru_bwd_tile(1, xb, dyl, h1[...], hedge, c_scr, wd_r, ba_r, bx_r, lam_r, a_scr, d_scr, g_scr,
                                       pre_scr, (dwd_o, dba_o, dbx_o, dlam_o), i == 0, tl, ng, gs)
        cfin[...] = c_scr[...]
        dv, dwb = _conv_bwd(dxb, v, wcb_r, 4, 2, tmod, row_w)
        for j in range(4):
            dwcb_o[j:j + 1, :] += dwb[j]
        dbcb_o[...] += jnp.sum(dxb, axis=0, keepdims=True)
        dq = dyb[...] * (h0[...] + h1[...]) * (sq * (1.0 + q * (1.0 - sq)))
        b_, c_, u_, g_ = bl[...], cl[...], ul[...], gl[...]
        z = c_ * u_
        cz = _conv(z, wca_r, 3, 1, tmod, row_w)
        sgm = _sigmoid(g_)
        sg = g_ * sgm
        da = dya[...]
        dz, dwa = _conv_bwd(da * b_ * sg, z, wca_r, 3, 1, tmod, row_w)
        for j in range(3):
            dwca_o[j:j + 1, :] += dwa[j]
        dp_o[:, 0 * W:1 * W] = (da * cz * sg).astype(dp_o.dtype)
        dp_o[:, 1 * W:2 * W] = (dz * u_).astype(dp_o.dtype)
        dp_o[:, 2 * W:3 * W] = (dz * c_).astype(dp_o.dtype)
        dp_o[:, 3 * W:4 * W] = (da * b_ * cz * (sgm * (1.0 + g_ * (1.0 - sgm)))).astype(dp_o.dtype)
        dp_o[:, 4 * W:5 * W] = dv.astype(dp_o.dtype)
        dp_o[:, 5 * W:6 * W] = dq.astype(dp_o.dtype)

    def pcol(j):
        return pl.BlockSpec((tl, W), lambda i: (i + ob, j))

    return pl.pallas_call(
        body, name=name, grid=(nt,),
        in_specs=[pcol(j) for j in range(6)]
        + [pl.BlockSpec((tl, W), lambda i: (i, 0)), pl.BlockSpec((tl, W), lambda i: (i, 1)), seq, seq,
           _edge_block(h1s, tl, nt, 1), seq, taps, taps, vec, wd_spec, vec, vec, vec, vec, vec],
        out_specs=[pl.BlockSpec((tl, 6 * W), lambda i: (i, 0)), wd_spec, vec, vec, vec, taps, taps, vec, vec],
        out_shape=[jax.ShapeDtypeStruct((rows, 6 * W), BF16), jax.ShapeDtypeStruct(wd.shape, F32)]
        + [jax.ShapeDtypeStruct((1, W), F32)] * 3
        + [jax.ShapeDtypeStruct((SUBLANES, W), F32)] * 2 + [jax.ShapeDtypeStruct((1, W), F32)] * 2,
        scratch_shapes=[pltpu.VMEM((tl, W), F32)] * 3 + [pltpu.VMEM((tl, 2 * W), F32), pltpu.VMEM((1, W), F32)],
        compiler_params=_params(("arbitrary",)),
    )(*([P] * 6), dcat, dcat, h0s, h1s, h1s, dxb0, wca, wcb, bcb, wd, ba, bx, lam, h_init, c_init)


def _loss_head(out, x, tgt, gt, fg, tr):
    L, D = x.shape

    def body(o_ref, x_ref, t_ref, gt_ref, fg_ref, dn_o, do_o, dfg_o, dgt_o, loss_o):
        i = pl.program_id(0)

        @pl.when(i == 0)
        def _():
            dfg_o[...] = jnp.zeros_like(dfg_o)
            dgt_o[...] = jnp.zeros_like(dgt_o)
            loss_o[...] = jnp.zeros_like(loss_o)

        o = o_ref[...]
        gt_v = gt_ref[...]
        fg_v = fg_ref[...]
        n = x_ref[...] + gt_v * o
        r = lax.rsqrt(jnp.mean(n * n, axis=-1, keepdims=True) + EPS)
        nr = n * r
        e = nr * fg_v - t_ref[...]
        loss_o[...] += 0.5 * jnp.sum(jnp.mean(e * e, axis=-1, keepdims=True))
        dy = e * (1.0 / D)
        dfg_o[...] += jnp.sum(dy * nr, axis=0, keepdims=True)
        qv = dy * fg_v
        dn = r * (qv - nr * jnp.mean(qv * nr, axis=-1, keepdims=True))
        dgt_o[...] += jnp.sum(dn * o, axis=0, keepdims=True)
        dn_o[...] = dn
        do_o[...] = (dn * gt_v).astype(do_o.dtype)

    blk = pl.BlockSpec((tr, D), lambda i: (i, 0))
    vec = pl.BlockSpec((1, D), lambda i: (0, 0))
    return pl.pallas_call(
        body, name="loss_head", grid=(L // tr,), in_specs=[blk, blk, blk, vec, vec],
        out_specs=[blk, blk, vec, vec, pl.BlockSpec((SUBLANES, LANES), lambda i: (0, 0))],
        out_shape=[jax.ShapeDtypeStruct((L, D), F32), jax.ShapeDtypeStruct((L, D), BF16),
                   jax.ShapeDtypeStruct((1, D), F32), jax.ShapeDtypeStruct((1, D), F32),
                   jax.ShapeDtypeStruct((SUBLANES, LANES), F32)],
        compiler_params=_params(("arbitrary",)),
    )(out, x, tgt, gt, fg)


def _norm_bwd(dhl, x, dn, g, sc, tr, name):
    L, D = x.shape
    with_x = dn is not None

    def body(*refs):
        if with_x:
            d_ref, x_ref, dn_ref, g_ref, sc_ref, gx_o, dsh_o, dsc_o, dg_o = refs
        else:
            d_ref, x_ref, g_ref, sc_ref, dsh_o, dsc_o, dg_o = refs
        i = pl.program_id(0)

        @pl.when(i == 0)
        def _():
            dsh_o[...] = jnp.zeros_like(dsh_o)
            dsc_o[...] = jnp.zeros_like(dsc_o)
            dg_o[...] = jnp.zeros_like(dg_o)

        d = d_ref[...]
        xv = x_ref[...]
        g_v = g_ref[...]
        r = lax.rsqrt(jnp.mean(xv * xv, axis=-1, keepdims=True) + EPS)
        xr = xv * r
        dsh_o[...] += jnp.sum(d, axis=0, keepdims=True)
        dsc_o[...] += jnp.sum(d * (xr * g_v), axis=0, keepdims=True)
        dxn = d * (1.0 + sc_ref[...])
        dg_o[...] += jnp.sum(dxn * xr, axis=0, keepdims=True)
        if with_x:
            qv = dxn * g_v
            gx_o[...] = r * (qv - xr * jnp.mean(qv * xr, axis=-1, keepdims=True)) + dn_ref[...]

    blk = pl.BlockSpec((tr, D), lambda i: (i, 0))
    vec = pl.BlockSpec((1, D), lambda i: (0, 0))
    vshape = jax.ShapeDtypeStruct((1, D), F32)
    res = pl.pallas_call(
        body, name=name, grid=(L // tr,),
        in_specs=[blk, blk] + ([blk] if with_x else []) + [vec, vec],
        out_specs=([blk] if with_x else []) + [vec, vec, vec],
        out_shape=([jax.ShapeDtypeStruct((L, D), F32)] if with_x else []) + [vshape] * 3,
        compiler_params=_params(("arbitrary",)),
    )(*([dhl, x] + ([dn] if with_x else []) + [g, sc]))
    return res if with_x else [None] + list(res)


def _pack_blockdiag(wa, wx, gs):
    H, hd, _ = wa.shape
    hp = gs // hd
    ng = H // hp
    eye = jnp.eye(hp, dtype=wa.dtype)

    def bd(w):
        return jnp.einsum("gpij,pq->gpiqj", w.reshape(ng, hp, hd, hd), eye).reshape(ng, gs, gs)

    return jnp.concatenate([bd(wa), bd(wx)], axis=-1).astype(BF16)


def _unpack_blockdiag(dwd, H, hd, gs):
    hp = gs // hd
    ng = H // hp
    eye = jnp.eye(hp, dtype=dwd.dtype)

    def diag(dm):
        return jnp.einsum("gpiqj,pq->gpij", dm.reshape(ng, hp, hd, hp, hd), eye).reshape(H, hd, hd)

    return diag(dwd[:, :, :gs]), diag(dwd[:, :, gs:])


def kernel(x, c, ctx, c_ctx, norm_g, w_ada, b_ada, w_in, w_conv_a, w_conv_b, b_conv_b, lru_wa, lru_ba, lru_wx, lru_bx, lru_lambda, w_out, final_g, loss_target, m_c_ctx, m_norm_g, m_w_ada, m_b_ada, m_w_in, m_w_conv_a, m_w_conv_b, m_b_conv_b, m_lru_wa, m_lru_ba, m_lru_wx, m_lru_bx, m_lru_lambda, m_w_out, m_final_g, v_c_ctx, v_norm_g, v_w_ada, v_b_ada, v_w_in, v_w_conv_a, v_w_conv_b, v_b_conv_b, v_lru_wa, v_lru_ba, v_lru_wx, v_lru_bx, v_lru_lambda, v_w_out, v_final_g):
    xi, yi, ci = _pos()
    me = 4 * xi + 2 * yi + ci
    q = 2 * xi + yi
    first_core = (ci == 0).astype(F32)

    L, D = x.shape[1], x.shape[2]
    T = ctx.shape[1]
    W = D // 2
    Wq = W // 4
    H, hd = lru_wa.shape[2], lru_wa.shape[3]
    gs = min(LRU_GROUP, W)
    nq = w_ada.shape[2]
    tl = min(256, T, L)
    tr = min(256, T, L)
    x2, ctx2, tgt2 = x[0], ctx[0], loss_target[0]

    def place(shard, full_cols):
        z = jnp.zeros((shard.shape[0], full_cols), F32)
        return lax.dynamic_update_slice(z, shard * first_core, (0, q * shard.shape[1]))

    c_rows = lax.dynamic_update_slice(jnp.zeros((8, D), F32), c, (me, 0))
    small_in = [c_rows, place(w_conv_a[0], W), place(w_conv_b[0], W), place(lru_ba[0], W), place(lru_bx[0], W),
                place(lru_lambda[0], W)]
    small_shapes = [a.shape for a in small_in]
    gathered = _allreduce8(_pack(small_in, 8 * SUBLANES), "gather_small")
    c_all, wca, wcb, ba_all, bx_all, lam_all = _unpack(gathered, small_shapes)

    s_rows = jnp.concatenate([c_all, c_ctx[None, :], jnp.zeros((7, D), F32)], axis=0)
    mod_part = _matmul(s_rows, w_ada[0], a_act="silu", bias=lax.dynamic_slice(b_ada, (0, q * nq), (1, nq)),
                       tm=16, tn=nq, tk=512, name="ada_fwd")
    mod_all = _allreduce8(_pack([place(mod_part, 4 * nq)], 8 * SUBLANES), "gather_mod")
    mod_all = _unpack(mod_all, [(16, 4 * nq)])[0]
    mod_l = lax.dynamic_slice(mod_all, (me, 0), (1, 3 * D))
    mod_c = mod_all[8:9]
    sh_l, sc_l, gt_l = mod_l[:, :D], mod_l[:, D:2 * D], mod_l[:, 2 * D:]
    sh_c, sc_c = mod_c[:, :D], mod_c[:, D:2 * D]

    pad_taps = lambda w: jnp.pad(w, ((0, SUBLANES - w.shape[0]), (0, 0)))
    wts = {
        "wca": pad_taps(wca), "wcb": pad_taps(wcb), "bcb": b_conv_b,
        "wd": [_pack_blockdiag(lru_wa[0, d], lru_wx[0, d], gs) for d in range(2)],
        "ba": [ba_all[d:d + 1] for d in range(2)], "bx": [bx_all[d:d + 1] for d in range(2)],
        "lam": [lam_all[d:d + 1] for d in range(2)],
    }

    hl = _norm_in(x2, ctx2, norm_g, sc_l, sh_l, sc_c, sh_c, tr)
    p_lat, win_full, wout_full = _in_proj_gather(hl, w_in[0].astype(BF16), w_out[0].astype(BF16),
                                                 jnp.reshape(q, (1,)).astype(jnp.int32), rows=L, tm=min(1024, L))
    p_ctx = _matmul(hl, win_full, a_rows=T, a_off=L, tm=T, tn=1536, tk=D, name="in_proj_ctx")
    zero_w = jnp.zeros((1, W), F32)
    c0s = _mix_fwd(p_ctx, 0, zero_w, wts, rows=T, row_off=0, row_w=T, tl=tl, name="ctx_fwd0")[0]
    c1s, _ = _mix_fwd(p_ctx, 1, zero_w, wts, rows=T, row_off=0, row_w=T, tl=tl, h_other=c0s, name="ctx_fwd1")
    h0_init, h1_init = c0s[T - 1:T], c1s[0:1]
    h0s = _mix_fwd(p_lat, 0, h0_init, wts, rows=L, row_off=0, row_w=GRID_W, tl=tl, name="mix_fwd0")[0]
    h1s, cat = _mix_fwd(p_lat, 1, h1_init, wts, rows=L, row_off=0, row_w=GRID_W, tl=tl, h_other=h0s, name="mix_fwd1")
    out = _matmul(cat, wout_full, tm=512, tn=D, tk=2 * W, name="out_proj")
    dn, dout, dfg, dgt, loss_blk = _loss_head(out, x2, tgt2, gt_l, final_g[None, :], tr)

    dcat = _matmul(dout, wout_full, tb=True, tm=512, tn=2 * W, tk=D, name="out_proj_bwd")
    gw_out = _matmul(cat, dout, ta=True, tm=1024, tn=D, tk=1024, out_dtype=BF16, name="w_out_grad")
    dxb0, dwd0, dba0, dbx0, dlam0, ch0 = _mix_bwd0(p_lat, dcat, h0s, h0_init, zero_w, wts, rows=L, row_off=0,
                                                   row_w=GRID_W, tl=tl, name="mix_bwd0")
    dp, dwd1, dba1, dbx1, dlam1, dwca, dwcb, dbcb, ch1 = _mix_bwd1(
        p_lat, dcat, h0s, h1s, dxb0, h1_init, zero_w, wts, rows=L, row_off=0, row_w=GRID_W, tl=tl, name="mix_bwd1")
    zero_cat = jnp.zeros((T, 2 * W), F32)
    cxb0, cwd0, cba0, cbx0, clam0, _ = _mix_bwd0(p_ctx, zero_cat, c0s, zero_w, ch0, wts, rows=T, row_off=0,
                                                 row_w=T, tl=tl, name="ctx_bwd0")
    dp_c, cwd1, cba1, cbx1, clam1, cwca, cwcb, cbcb, _ = _mix_bwd1(
        p_ctx, zero_cat, c0s, c1s, cxb0, zero_w, ch1, wts, rows=T, row_off=0, row_w=T, tl=tl, name="ctx_bwd1")

    gw_in_ctx = _matmul(hl, dp_c, ta=True, a_rows=T, a_off=L, tm=1024, tn=1536, tk=T, name="w_in_grad_ctx")
    gw_in = _matmul(hl, dp, ta=True, a_rows=L, tm=1024, tn=1536, tk=1024, init=gw_in_ctx, out_dtype=BF16,
                    name="w_in_grad")
    rs_axes = [1, 0]
    pair_sums = _rs_pair_sums([gw_in, gw_out], rs_axes)
    dhl, rs_slots = _matmul(dp, win_full, tb=True, tm=1024, tn=D, tk=1024, name="in_proj_bwd",
                            side=_rs_chips_side(pair_sums))
    dhc = _matmul(dp_c, win_full, tb=True, tm=T, tn=D, tk=512, name="in_proj_bwd_ctx")
    gx, dsh_l, dsc_l, dng_l = _norm_bwd(dhl, x2, dn, norm_g, sc_l, tr, "norm_bwd")
    _, dsh_c, dsc_c, dng_c = _norm_bwd(dhc, ctx2, None, norm_g, sc_c, tr, "norm_bwd_ctx")

    g_in_shard, g_out_shard = _rs_finish(rs_slots, rs_axes)

    dwa0, dwx0 = _unpack_blockdiag(dwd0 + cwd0, H, hd, gs)
    dwa1, dwx1 = _unpack_blockdiag(dwd1 + cwd1, H, hd, gs)
    zeros_d = jnp.zeros((1, D), F32)
    dmod_l = jnp.concatenate([dsh_l, dsc_l, dgt], axis=1)
    dmod_c = jnp.concatenate([dsh_c, dsc_c, zeros_d], axis=1)
    small_g = [
        lax.dynamic_update_slice(jnp.zeros((8, 3 * D), F32), dmod_l, (me, 0)), dmod_c,
        dfg, dng_l + dng_c, (dwca + cwca)[:3], (dwcb + cwcb)[:4], dbcb + cbcb,
        jnp.stack([dwa0, dwa1]), jnp.stack([dwx0, dwx1]),
        jnp.concatenate([dba0 + cba0, dba1 + cba1], axis=0), jnp.concatenate([dbx0 + cbx0, dbx1 + cbx1], axis=0),
        jnp.concatenate([dlam0 + clam0, dlam1 + clam1], axis=0),
    ]
    g_shapes = [a.shape for a in small_g]
    (g_rows, g_modc, g_fg, g_ng, g_wca, g_wcb, g_bcb, g_wa, g_wx, g_ba, g_bx, g_lam) = _unpack(
        _allreduce8(_pack(small_g, 8 * SUBLANES), "reduce_small"), g_shapes)

    g_mod = jnp.concatenate([g_rows, g_modc, jnp.zeros((7, 3 * D), F32)], axis=0)
    g_mod_q = lax.dynamic_slice(g_mod, (0, q * nq), (16, nq))
    g_w_ada = _matmul(s_rows, g_mod_q, ta=True, a_act="silu", tm=1024, tn=nq, tk=16, name="w_ada_grad")
    g_b_ada = jnp.sum(g_mod[:9], axis=0, keepdims=True)
    gc_part = _matmul(jnp.pad(lax.dynamic_slice(g_modc, (0, q * nq), (1, nq)), ((0, 7), (0, 0))), w_ada[0], tb=True,
                      dsilu_mul=c_ctx[None, :], tm=8, tn=D, tk=512, name="c_ctx_grad")
    g_c_ctx = _unpack(_allreduce8(_pack([gc_part[0:1] * first_core], 8 * SUBLANES), "reduce_c_ctx"), [(D,)])[0]

    def shard_cols(a, width):
        return lax.dynamic_slice(a, (0, q * width), (a.shape[0], width))

    grads = {
        "c_ctx": g_c_ctx, "norm_g": g_ng, "b_ada": g_b_ada,
        "w_conv_a": shard_cols(g_wca, Wq)[None], "w_conv_b": shard_cols(g_wcb, Wq)[None], "b_conv_b": g_bcb,
        "lru_wa": g_wa[None], "lru_ba": shard_cols(g_ba, Wq)[None], "lru_wx": g_wx[None],
        "lru_bx": shard_cols(g_bx, Wq)[None], "lru_lambda": shard_cols(g_lam, Wq)[None], "final_g": g_fg[0],
    }
    small_names = list(grads)
    given = dict(c_ctx=(c_ctx, m_c_ctx, v_c_ctx), norm_g=(norm_g, m_norm_g, v_norm_g), b_ada=(b_ada, m_b_ada, v_b_ada),
                 w_conv_a=(w_conv_a, m_w_conv_a, v_w_conv_a), w_conv_b=(w_conv_b, m_w_conv_b, v_w_conv_b),
                 b_conv_b=(b_conv_b, m_b_conv_b, v_b_conv_b), lru_wa=(lru_wa, m_lru_wa, v_lru_wa),
                 lru_ba=(lru_ba, m_lru_ba, v_lru_ba), lru_wx=(lru_wx, m_lru_wx, v_lru_wx),
                 lru_bx=(lru_bx, m_lru_bx, v_lru_bx), lru_lambda=(lru_lambda, m_lru_lambda, v_lru_lambda),
                 final_g=(final_g, m_final_g, v_final_g))
    shapes = [given[n][0].shape for n in small_names]
    packed = [_pack([given[n][j] for n in small_names], 2 * SUBLANES) for j in range(3)]
    packed_g = _pack([grads[n] for n in small_names], 2 * SUBLANES)
    sd, sm, sv = _adam(packed[0], packed_g, packed[1], packed[2], "adam_small")
    delta_s = dict(zip(small_names, _unpack(sd, shapes)))
    newm_s = dict(zip(small_names, _unpack(sm, shapes)))
    newv_s = dict(zip(small_names, _unpack(sv, shapes)))
    grads = {n: grads[n].reshape(given[n][0].shape) for n in small_names}

    big = {"w_ada": (w_ada, g_w_ada, m_w_ada, v_w_ada), "w_in": (w_in, g_in_shard, m_w_in, v_w_in),
           "w_out": (w_out, g_out_shard, m_w_out, v_w_out)}
    delta_b, newm_b, newv_b = {}, {}, {}
    for n, (w, g, m, v) in big.items():
        grads[n] = g[None]
        d_, m_, v_ = _adam(w[0], g, m[0], v[0], "adam_" + n)
        delta_b[n], newm_b[n], newv_b[n] = d_[None], m_[None], v_[None]

    loss = lax.psum(loss_blk[0, 0], AXES)
    order = ["c_ctx", "norm_g", "w_ada", "b_ada", "w_in", "w_conv_a", "w_conv_b", "b_conv_b", "lru_wa", "lru_ba",
             "lru_wx", "lru_bx", "lru_lambda", "w_out", "final_g"]
    delta = {**delta_s, **delta_b}
    newm = {**newm_s, **newm_b}
    newv = {**newv_s, **newv_b}
    return (loss, gx[None], *[grads[n] for n in order], *[delta[n] for n in order], *[newm[n] for n in order],
            *[newv[n] for n in order])
```

```python
import functools

import jax
import jax.numpy as jnp
from jax import lax
from jax.experimental import pallas as pl
from jax.experimental.pallas import tpu as pltpu

F32 = jnp.float32
BF16 = jnp.bfloat16
MESH_ID = pl.DeviceIdType.MESH
AXES = ("x", "y", "c")

EPS = 1e-6
LRU_C = 8.0
GRID_W = 64
ADAM_LR = 0.001
ADAM_B1 = 0.9
ADAM_B2 = 0.999
ADAM_EPS = 1e-08
ADAM_WD = 0.01
ADAM_STEP = 10

LANES = 128
SUBLANES = 8
PACK_COLS = 1024
VMEM_LIMIT = 56 * 2**20
LRU_GROUP = 256


def _params(sem=None):
    return pltpu.CompilerParams(vmem_limit_bytes=VMEM_LIMIT, dimension_semantics=sem)


def _pick(dim, pref, quantum=LANES):
    if dim <= pref:
        return dim
    best = None
    for t in range(quantum, pref + 1, quantum):
        if dim % t == 0:
            best = t
    assert best is not None, (dim, pref)
    return best


def _pos():
    return lax.axis_index("x"), lax.axis_index("y"), lax.axis_index("c")


def _flip(v, bit):
    return 1 - v if bit else v


def _sigmoid(v):
    return 0.5 * jnp.tanh(0.5 * v) + 0.5


def _silu(v):
    return v * _sigmoid(v)


def _dsilu(v):
    s = _sigmoid(v)
    return s * (1.0 + v * (1.0 - s))


def _gates(pre_r, pre_i, sp):
    r = _sigmoid(pre_r)
    ig = _sigmoid(pre_i)
    e = LRU_C * r * sp
    w = jnp.tanh(e)
    return r, ig, jnp.exp(-e), 2.0 * w / (1.0 + w)


def _softplus(z):
    return jnp.maximum(z, 0.0) + jnp.log1p(jnp.exp(-jnp.abs(z)))


def _matmul(a, b, *, ta=False, tb=False, tm=512, tn=512, tk=512, out_dtype=F32, name,
            a_rows=None, a_off=0, a_act=None, init=None, bias=None, dsilu_mul=None, side=None):
    rows_a = a.shape[0] if a_rows is None else a_rows
    if ta:
        K, M = rows_a, a.shape[1]
    else:
        M, K = rows_a, a.shape[1]
    N = b.shape[0] if tb else b.shape[1]
    tm, tn, tk = _pick(M, tm, SUBLANES), _pick(N, tn), _pick(K, tk)
    t_rows = tk if ta else tm
    assert a_off % t_rows == 0
    nk = K // tk
    gi, gj = M // tm, N // tn
    off_blocks = a_off // t_rows
    dims = (((0 if ta else 1,), (1 if tb else 0,)), ((), ()))
    extras = [e for e in (init, bias, dsilu_mul) if e is not None]
    n_sin = len(side["ins"]) if side else 0
    n_sout = len(side["outs"]) if side else 0

    def body(a_ref, b_ref, *rest):
        rest = list(rest)
        init_ref = rest.pop(0) if init is not None else None
        bias_ref = rest.pop(0) if bias is not None else None
        dsm_ref = rest.pop(0) if dsilu_mul is not None else None
        side_in = [rest.pop(0) for _ in range(n_sin)]
        o_ref = rest.pop(0)
        side_out = [rest.pop(0) for _ in range(n_sout)]
        acc_ref = rest.pop(0) if nk > 1 else None
        side_scr = rest
        i, j, k = pl.program_id(0), pl.program_id(1), pl.program_id(2)

        if side:
            @pl.when((i == 0) & (j == 0) & (k == 0))
            def _():
                side["start"](side_in, side_out, side_scr)

        av = a_ref[...]
        if a_act == "silu":
            av = _silu(av)
        prod = lax.dot_general(av, b_ref[...], dims, preferred_element_type=F32)

        def finish(r):
            if bias_ref is not None:
                r = r + bias_ref[...]
            if dsm_ref is not None:
                r = r * _dsilu(dsm_ref[...])
            o_ref[...] = r.astype(o_ref.dtype)

        if nk == 1:
            finish(prod if init_ref is None else prod + init_ref[...])
        else:
            @pl.when(k == 0)
            def _():
                acc_ref[...] = prod if init_ref is None else prod + init_ref[...]

            @pl.when(k > 0)
            def _():
                acc_ref[...] += prod

            @pl.when(k == nk - 1)
            def _():
                finish(acc_ref[...])

        if side:
            @pl.when((i == gi - 1) & (j == gj - 1) & (k == nk - 1))
            def _():
                side["finish"](side_in, side_out, side_scr)

    if ta:
        a_spec = pl.BlockSpec((tk, tm), lambda i, j, k: (k + off_blocks, i))
    else:
        a_spec = pl.BlockSpec((tm, tk), lambda i, j, k: (i + off_blocks, k))
    if tb:
        b_spec = pl.BlockSpec((tn, tk), lambda i, j, k: (j, k))
    else:
        b_spec = pl.BlockSpec((tk, tn), lambda i, j, k: (k, j))
    in_specs = [a_spec, b_spec]
    if init is not None:
        in_specs.append(pl.BlockSpec((tm, tn), lambda i, j, k: (i, j)))
    if bias is not None:
        in_specs.append(pl.BlockSpec((1, tn), lambda i, j, k: (0, j)))
    if dsilu_mul is not None:
        in_specs.append(pl.BlockSpec((1, tn), lambda i, j, k: (0, j)))
    hbm = pl.BlockSpec(memory_space=pl.ANY)
    res = pl.pallas_call(
        body, name=name, grid=(gi, gj, nk),
        in_specs=in_specs + [hbm] * n_sin,
        out_specs=[pl.BlockSpec((tm, tn), lambda i, j, k: (i, j))] + [hbm] * n_sout,
        out_shape=[jax.ShapeDtypeStruct((M, N), out_dtype)] + (list(side["outs"]) if side else []),
        scratch_shapes=([pltpu.VMEM((tm, tn), F32)] if nk > 1 else []) + (list(side["scratch"]) if side else []),
        compiler_params=_params(("arbitrary",) * 3 if side else ("parallel", "parallel", "arbitrary")),
    )(a, b, *extras, *(side["ins"] if side else []))
    return (res[0], res[1:]) if side else res[0]


def _elementwise(fn, ins, outs, *, rows, cols, name, tr=256):
    tr = _pick(rows, tr, 2 * SUBLANES)
    n_in = len(ins)

    def body(*refs):
        vals = fn(*[r[...] for r in refs[:n_in]])
        if not isinstance(vals, (tuple, list)):
            vals = (vals,)
        for r, v in zip(refs[n_in:], vals, strict=True):
            r[...] = v.astype(r.dtype)

    def spec(off):
        assert off % tr == 0
        ob = off // tr
        return pl.BlockSpec((tr, cols), lambda i: (i + ob, 0))

    res = pl.pallas_call(
        body, name=name, grid=(rows // tr,),
        in_specs=[spec(off) for _, off in ins],
        out_specs=[spec(0) for _ in outs],
        out_shape=[jax.ShapeDtypeStruct((rows, cols), dt) for dt in outs],
        compiler_params=_params(("parallel",)),
    )(*[a for a, _ in ins])
    return res


def _adam_math(w, g, m, v):
    m = ADAM_B1 * m + (1.0 - ADAM_B1) * g
    v = ADAM_B2 * v + (1.0 - ADAM_B2) * (g * g)
    m_hat = m / (1.0 - ADAM_B1 ** ADAM_STEP)
    v_hat = v / (1.0 - ADAM_B2 ** ADAM_STEP)
    delta = -ADAM_LR * (m_hat / (jnp.sqrt(v_hat) + ADAM_EPS) + ADAM_WD * w)
    return delta, m, v


def _adam(w, g, m, v, name):
    rows, cols = w.shape
    return _elementwise(_adam_math, [(w, 0), (g, 0), (m, 0), (v, 0)], [F32, F32, F32],
                        rows=rows, cols=cols, name=name)


def _pack(arrs, row_quantum):
    flat = jnp.concatenate([a.reshape(-1).astype(F32) for a in arrs])
    n = flat.shape[0]
    q = row_quantum * PACK_COLS
    total = -(-n // q) * q
    flat = jnp.pad(flat, (0, total - n))
    return flat.reshape(total // PACK_COLS, PACK_COLS)


def _unpack(buf, shapes):
    flat = buf.reshape(-1)
    out, off = [], 0
    for s in shapes:
        n = 1
        for d in s:
            n *= d
        out.append(flat[off:off + n].reshape(s))
        off += n
    return out


def _allreduce8(buf, name):
    R, C = buf.shape
    assert R % (8 * SUBLANES) == 0
    m = R // 8

    def body(x_ref, o_ref, recv, red, s1, r1, s2, r2):
        x, y, c = _pos()
        me = 4 * x + 2 * y + c

        def peer(k):
            px, py, pc = _flip(x, (k >> 2) & 1), _flip(y, (k >> 1) & 1), _flip(c, k & 1)
            return (px, py, pc), 4 * px + 2 * py + pc

        def rows(ref, idx):
            return ref.at[pl.ds(pl.multiple_of(idx * m, SUBLANES), m), :]

        def scatter(k):
            dev, p = peer(k)
            return pltpu.make_async_remote_copy(src_ref=rows(x_ref, p), dst_ref=recv.at[k], send_sem=s1.at[k],
                                                recv_sem=r1.at[k], device_id=dev, device_id_type=MESH_ID)

        def share(k):
            dev, p = peer(k)
            return pltpu.make_async_remote_copy(src_ref=red, dst_ref=rows(o_ref, me), send_sem=s2.at[k],
                                                recv_sem=r2.at[k], device_id=dev, device_id_type=MESH_ID)

        def shared_from(k):
            dev, p = peer(k)
            return pltpu.make_async_remote_copy(src_ref=red, dst_ref=rows(o_ref, p), send_sem=s2.at[k],
                                                recv_sem=r2.at[k], device_id=dev, device_id_type=MESH_ID)

        for k in range(1, 8):
            scatter(k).start()
        acc = rows(x_ref, me)[...]
        for k in range(1, 8):
            scatter(k).wait_recv()
            acc = acc + recv[k]
        red[...] = acc
        rows(o_ref, me)[...] = acc
        for k in range(1, 8):
            share(k).start()
        for k in range(1, 8):
            shared_from(k).wait_recv()
        for k in range(1, 8):
            scatter(k).wait_send()
            share(k).wait_send()

    return pl.pallas_call(
        body, name=name,
        in_specs=[pl.BlockSpec(memory_space=pltpu.VMEM)],
        out_specs=pl.BlockSpec(memory_space=pltpu.VMEM),
        out_shape=jax.ShapeDtypeStruct((R, C), F32),
        scratch_shapes=[pltpu.VMEM((8, m, C), F32), pltpu.VMEM((m, C), F32),
                        pltpu.SemaphoreType.DMA((8,)), pltpu.SemaphoreType.DMA((8,)),
                        pltpu.SemaphoreType.DMA((8,)), pltpu.SemaphoreType.DMA((8,))],
        compiler_params=_params(),
    )(buf)


def _bounce(src, dst, buf, sem):
    cin = pltpu.make_async_copy(src, buf, sem)
    cin.start()
    cin.wait()
    cout = pltpu.make_async_copy(buf, dst, sem)
    cout.start()
    cout.wait()


def _chunk(ref, axis, idx, size):
    start = idx * size
    if axis == 0:
        return ref.at[pl.ds(start, size), :]
    return ref.at[:, pl.ds(start, size)]


def _in_proj_gather(hl, win, wout, q_arr, *, rows, tm):
    D, nq = win.shape
    dq, D2 = wout.shape
    ni = rows // tm
    ops = ((0, 1, nq, D // 2), (1, 0, dq, dq // 2))

    def body(q_ref, a_ref, win_ref, wout_ref, p_ref, gin_ref, gout_ref, b_scr, buf_out, lsem, ssem, rsem, fsem, gsem):
        j, i = pl.program_id(0), pl.program_id(1)
        x, y, c = _pos()
        q = 2 * x + y
        srcs = (win_ref, wout_ref)
        dsts = (gin_ref, gout_ref)

        def shard_window(o, chip):
            _, axis, size, _ = ops[o]
            return _chunk(dsts[o], axis, chip, size)

        def half(ref, o, core):
            return ref.at[pl.ds(core * ops[o][3], ops[o][3]), :]

        def half_window(o, chip, core):
            _, axis, size, hs = ops[o]
            if axis == 1:
                return dsts[o].at[pl.ds(core * hs, hs), pl.ds(chip * size, size)]
            return dsts[o].at[pl.ds(chip * size + core * hs, hs), :]

        def chip_of(k):
            px, py = _flip(x, (k >> 1) & 1), _flip(y, k & 1)
            return px, py, 2 * px + py

        def send(o, k):
            px, py, _ = chip_of(k)
            return pltpu.make_async_remote_copy(
                src_ref=half(srcs[o], o, c), dst_ref=half_window(o, q, c), send_sem=ssem.at[o, k],
                recv_sem=rsem.at[o, k], device_id=(px, py, c), device_id_type=MESH_ID)

        def arrive(o, k):
            px, py, pq = chip_of(k)
            landed = half_window(o, pq, c)
            pltpu.make_async_remote_copy(src_ref=landed, dst_ref=landed, send_sem=ssem.at[o, k], recv_sem=rsem.at[o, k],
                                         device_id=(px, py, c), device_id_type=MESH_ID).wait_recv()
            fw = pltpu.make_async_remote_copy(src_ref=landed, dst_ref=landed, send_sem=fsem.at[o, k],
                                              recv_sem=gsem.at[o, k], device_id=(x, y, 1 - c), device_id_type=MESH_ID)
            fw.start()
            theirs = half_window(o, pq, 1 - c)
            pltpu.make_async_remote_copy(src_ref=theirs, dst_ref=theirs, send_sem=fsem.at[o, k], recv_sem=gsem.at[o, k],
                                         device_id=(x, y, 1 - c), device_id_type=MESH_ID).wait_recv()
            fw.wait_send()

        def load_b(src):
            cp = pltpu.make_async_copy(src, b_scr, lsem.at[0])
            cp.start()
            cp.wait()

        def relay(o, core):
            if core == 0:
                landed, target = half_window(o, chip_of(2)[2], 0), (x, 1 - y, 0)
            else:
                landed, target = half_window(o, chip_of(1)[2], 1), (1 - x, y, 1)
            return pltpu.make_async_remote_copy(src_ref=landed, dst_ref=landed, send_sem=ssem.at[o, 3],
                                                recv_sem=rsem.at[o, 3], device_id=target, device_id_type=MESH_ID)

        def on_core(core, fn):
            @pl.when(c == core)
            def _():
                fn()

        @pl.when((j == 0) & (i == 0))
        def _():
            for o in range(2):
                for k in (2, 1):
                    send(o, k).start()
            load_b(win_ref)
            own = pltpu.make_async_copy(b_scr, shard_window(0, q), lsem.at[0])
            own.start()
            own.wait()
            _bounce(wout_ref, shard_window(1, q), buf_out, lsem.at[1])

        @pl.when((j == 1) & (i == 0))
        def _():
            arrive(0, 2)
            on_core(0, lambda: relay(0, 0).start())
            load_b(shard_window(0, chip_of(2)[2]))

        @pl.when((j == 2) & (i == 0))
        def _():
            arrive(0, 1)
            on_core(1, lambda: relay(0, 1).start())
            load_b(shard_window(0, chip_of(1)[2]))

        @pl.when((j == 3) & (i == 0))
        def _():
            arrive(0, 3)
            load_b(shard_window(0, chip_of(3)[2]))
            arrive(1, 2)
            on_core(0, lambda: relay(1, 0).start())
            arrive(1, 1)
            on_core(1, lambda: relay(1, 1).start())

        p_ref[...] = jnp.dot(a_ref[...], b_scr[...], preferred_element_type=F32)

        @pl.when((j == 3) & (i == ni - 1))
        def _():
            arrive(1, 3)
            for o in range(2):
                for k in (2, 1):
                    send(o, k).wait_send()
                for core in range(2):
                    on_core(core, lambda o=o, core=core: relay(o, core).wait_send())

    hbm = pl.BlockSpec(memory_space=pl.ANY)
    grid_spec = pltpu.PrefetchScalarGridSpec(
        num_scalar_prefetch=1, grid=(4, ni),
        in_specs=[pl.BlockSpec((tm, D), lambda j, i, qr: (i, 0)), hbm, hbm],
        out_specs=[pl.BlockSpec((tm, nq), lambda j, i, qr: (i, jnp.bitwise_xor(qr[0], ((j & 1) << 1) | (j >> 1)))),
                   hbm, hbm],
        scratch_shapes=[pltpu.VMEM(win.shape, win.dtype), pltpu.VMEM(wout.shape, wout.dtype), pltpu.SemaphoreType.DMA((2,))]
        + [pltpu.SemaphoreType.DMA((2, 4)) for _ in range(4)])
    return pl.pallas_call(
        body, name="in_proj_gather", grid_spec=grid_spec,
        out_shape=[jax.ShapeDtypeStruct((rows, 4 * nq), F32), jax.ShapeDtypeStruct((D, 4 * nq), win.dtype),
                   jax.ShapeDtypeStruct((4 * dq, D2), wout.dtype)],
        compiler_params=_params(("arbitrary", "arbitrary")),
    )(q_arr, hl, win, wout)


def _rs_to_sibling(gs, axes):
    n = len(gs)
    shapes = []
    for g, ax in zip(gs, axes):
        s = list(g.shape)
        s[ax] //= 8
        shapes.append(tuple(s))

    def body(*refs):
        g_refs, mine, landed = refs[:n], refs[n:2 * n], refs[2 * n:3 * n]
        bufs = refs[3 * n:4 * n]
        lsem, ssem, rsem = refs[4 * n:]
        x, y, c = _pos()
        cps = []
        for o in range(n):
            size = shapes[o][axes[o]]
            for j in range(4):
                rc = pltpu.make_async_remote_copy(
                    src_ref=_chunk(g_refs[o], axes[o], 2 * j + 1 - c, size), dst_ref=landed[o].at[j],
                    send_sem=ssem.at[o, j], recv_sem=rsem.at[o, j], device_id=(x, y, 1 - c), device_id_type=MESH_ID)
                rc.start()
                cps.append(rc)
        for o in range(n):
            size = shapes[o][axes[o]]
            for j in range(4):
                _bounce(_chunk(g_refs[o], axes[o], 2 * j + c, size), mine[o].at[j], bufs[o], lsem.at[o])
        for rc in cps:
            rc.wait()

    hbm = pl.BlockSpec(memory_space=pl.ANY)
    outs = [jax.ShapeDtypeStruct((4,) + s, g.dtype) for s, g in zip(shapes, gs)]
    res = pl.pallas_call(
        body, name="rs_to_sibling", in_specs=[hbm] * n, out_specs=[hbm] * (2 * n), out_shape=outs + outs,
        scratch_shapes=[pltpu.VMEM(s, g.dtype) for s, g in zip(shapes, gs)]
        + [pltpu.SemaphoreType.DMA((n,)), pltpu.SemaphoreType.DMA((n, 4)), pltpu.SemaphoreType.DMA((n, 4))],
        compiler_params=_params(),
    )(*gs)
    return res[:n], res[n:]


def _rs_chips_side(parts):
    n = len(parts)

    def copies(p_refs, slots, scr):
        ssem, rsem = scr[n + 1], scr[n + 2]
        x, y, c = _pos()
        cps = []
        for o in range(n):
            for k in range(1, 4):
                px, py = _flip(x, (k >> 1) & 1), _flip(y, k & 1)
                cps.append(pltpu.make_async_remote_copy(
                    src_ref=p_refs[o].at[2 * px + py], dst_ref=slots[o].at[k], send_sem=ssem.at[o, k],
                    recv_sem=rsem.at[o, k], device_id=(px, py, c), device_id_type=MESH_ID))
        return cps

    def start(p_refs, slots, scr):
        for cp in copies(p_refs, slots, scr):
            cp.start()

    def finish(p_refs, slots, scr):
        x, y, _ = _pos()
        q = 2 * x + y
        for o in range(n):
            _bounce(p_refs[o].at[q], slots[o].at[0], scr[o], scr[n].at[o])
        for cp in copies(p_refs, slots, scr):
            cp.wait()

    return dict(
        ins=list(parts), outs=[jax.ShapeDtypeStruct(p.shape, p.dtype) for p in parts],
        scratch=[pltpu.VMEM(p.shape[1:], p.dtype) for p in parts]
        + [pltpu.SemaphoreType.DMA((n,)), pltpu.SemaphoreType.DMA((n, 4)), pltpu.SemaphoreType.DMA((n, 4))],
        start=start, finish=finish)


def _rs_share(rs, axes):
    n = len(rs)
    shapes = []
    for r, ax in zip(rs, axes):
        s = list(r.shape)
        s[ax] *= 2
        shapes.append(tuple(s))

    def body(*refs):
        r_refs, outs = refs[:n], refs[n:2 * n]
        bufs = refs[2 * n:3 * n]
        lsem, ssem, rsem = refs[3 * n:]
        x, y, c = _pos()
        cps = []
        for o in range(n):
            size = r_refs[o].shape[axes[o]]
            window = _chunk(outs[o], axes[o], c, size)
            rc = pltpu.make_async_remote_copy(src_ref=r_refs[o], dst_ref=window, send_sem=ssem.at[o], recv_sem=rsem.at[o],
                                              device_id=(x, y, 1 - c), device_id_type=MESH_ID)
            rc.start()
            cps.append(rc)
        for o in range(n):
            size = r_refs[o].shape[axes[o]]
            _bounce(r_refs[o], _chunk(outs[o], axes[o], c, size), bufs[o], lsem.at[o])
        for cp in cps:
            cp.wait()

    hbm = pl.BlockSpec(memory_space=pl.ANY)
    return pl.pallas_call(
        body, name="rs_share", in_specs=[hbm] * n, out_specs=[hbm] * n,
        out_shape=[jax.ShapeDtypeStruct(s, r.dtype) for s, r in zip(shapes, rs)],
        scratch_shapes=[pltpu.VMEM(r.shape, r.dtype) for r in rs] + [pltpu.SemaphoreType.DMA((n,)) for _ in range(3)],
        compiler_params=_params(),
    )(*rs)


def _rs_pair_sums(gs, axes):
    mine, landed = _rs_to_sibling(gs, axes)
    pair_sums = []
    for o, (mi, la) in enumerate(zip(mine, landed)):
        rows, cols = mi.shape[0] * mi.shape[1], mi.shape[2]
        s = _elementwise(lambda a, b: a.astype(F32) + b.astype(F32), [(mi.reshape(rows, cols), 0), (la.reshape(rows, cols), 0)],
                         [BF16], rows=rows, cols=cols, name=f"rs_pair_sum{o}")[0]
        pair_sums.append(s.reshape(mi.shape))
    return pair_sums


def _rs_finish(slots, axes):
    reduced = []
    for o, sl in enumerate(slots):
        rows, cols = sl.shape[1], sl.shape[2]
        flat = sl.reshape(4 * rows, cols)
        r = _elementwise(lambda a, b, c, d: (a.astype(F32) + b.astype(F32)) + (c.astype(F32) + d.astype(F32)),
                         [(flat, k * rows) for k in range(4)], [F32], rows=rows, cols=cols, name=f"rs_chip_sum{o}")[0]
        reduced.append(r)
    return _rs_share(reduced, axes)


def _norm_in(x, ctx, g, sc_l, sh_l, sc_c, sh_c, tr):
    L, D = x.shape
    T = ctx.shape[0]
    nx, nc = L // tr, T // tr

    def body(x_ref, c_ref, g_ref, scl, shl, scc, shc, o_ref):
        i = pl.program_id(0)

        def run(src, sc, sh):
            v = src[...]
            r = lax.rsqrt(jnp.mean(v * v, axis=-1, keepdims=True) + EPS)
            o_ref[...] = ((v * r * g_ref[...]) * (1.0 + sc[...]) + sh[...]).astype(o_ref.dtype)

        @pl.when(i < nx)
        def _():
            run(x_ref, scl, shl)

        @pl.when(i >= nx)
        def _():
            run(c_ref, scc, shc)

    vec = pl.BlockSpec((1, D), lambda i: (0, 0))
    return pl.pallas_call(
        body, name="norm_in", grid=(nx + nc,),
        in_specs=[pl.BlockSpec((tr, D), lambda i: (jnp.minimum(i, nx - 1), 0)),
                  pl.BlockSpec((tr, D), lambda i: (jnp.maximum(i - nx, 0), 0)), vec, vec, vec, vec, vec],
        out_specs=pl.BlockSpec((tr, D), lambda i: (i, 0)),
        out_shape=jax.ShapeDtypeStruct((L + T, D), BF16),
        compiler_params=_params(("arbitrary",)),
    )(x, ctx, g, sc_l, sh_l, sc_c, sh_c)


def _tmod(tl, row_w):
    assert row_w & (row_w - 1) == 0
    return lax.broadcasted_iota(jnp.int32, (tl, 1), 0) & (row_w - 1)


def _shift(z, k, tmod, row_w):
    tl = z.shape[0]
    rolled = pltpu.roll(z, k % tl, 0)
    mask = (tmod >= k) if k > 0 else (tmod < row_w + k)
    return jnp.where(mask, rolled, 0.0)


def _conv(z, w_ref, taps, left, tmod, row_w, lanes=slice(None)):
    out = None
    for j in range(taps):
        k = left - j
        term = (z if k == 0 else _shift(z, k, tmod, row_w)) * w_ref[j:j + 1, lanes]
        out = term if out is None else out + term
    return out


def _conv_bwd(dz, z, w_ref, taps, left, tmod, row_w, lanes=slice(None)):
    din = None
    dws = []
    for j in range(taps):
        k = left - j
        term = (dz if k == 0 else _shift(dz, -k, tmod, row_w)) * w_ref[j:j + 1, lanes]
        din = term if din is None else din + term
        zs = z if k == 0 else _shift(z, k, tmod, row_w)
        dws.append(jnp.sum(dz * zs, axis=0, keepdims=True))
    return din, dws


def _gate_matmul(xb16_ref, wd_ref, pre_scr, W, ng, gs):
    for g in range(ng):
        pg = jnp.dot(xb16_ref[:, g * gs:(g + 1) * gs], wd_ref[g], preferred_element_type=F32)
        pre_scr[:, g * gs:(g + 1) * gs] = pg[:, :gs]
        pre_scr[:, W + g * gs:W + (g + 1) * gs] = pg[:, gs:]


def _sub_loop(tl, sub_r, W, fn):
    def chunk(ci, carry):
        r0 = pl.multiple_of(ci * sub_r, sub_r)
        for lb in range(W // LANES):
            fn(r0, lb * LANES)
        return carry

    lax.fori_loop(0, tl // sub_r, chunk, 0)


def _row_loop(tl, rev, step, init):
    nchunk = tl // SUBLANES

    def chunk(j, carry):
        jj = (nchunk - 1 - j) if rev else j
        c0 = pl.multiple_of(jj * SUBLANES, SUBLANES)
        for r in (range(SUBLANES - 1, -1, -1) if rev else range(SUBLANES)):
            carry = step(c0 + r, carry)
        return carry

    return lax.fori_loop(0, nchunk, chunk, init)


def _mix_fwd(P, d, h_init, wts, *, rows, row_off, row_w, tl, h_other=None, name):
    W = P.shape[1] // 6
    nt = rows // tl
    ob = row_off // tl
    rev = d == 1
    gs = min(LRU_GROUP, W)
    ng = W // gs
    wca, wcb, bcb = wts["wca"], wts["wcb"], wts["bcb"]
    wd, ba, bx, lam = wts["wd"][d], wts["ba"][d], wts["bx"][d], wts["lam"][d]

    def tile(i):
        return (nt - 1 - i) if rev else i

    def pcol(j):
        return pl.BlockSpec((tl, W), lambda i: (tile(i) + ob, j))

    vec = pl.BlockSpec((1, W), lambda i: (0, 0))
    taps = pl.BlockSpec((SUBLANES, W), lambda i: (0, 0))
    wd_spec = pl.BlockSpec(wd.shape, lambda i: (0, 0, 0))
    seq = pl.BlockSpec((tl, W), lambda i: (tile(i), 0))

    sub_r = min(row_w, tl)
    assert tl % sub_r == 0

    def body(*refs):
        if rev:
            (bl, cl, ul, gl, vl, ql, ho, wca_r, wcb_r, bcb_r, wd_r, ba_r, bx_r, lam_r, hin, hseq, cat,
             a_scr, b_scr, pre_scr, carry, xb16_scr, sp_scr) = refs
        else:
            (vl, wcb_r, bcb_r, wd_r, ba_r, bx_r, lam_r, hin, hseq, a_scr, b_scr, pre_scr, carry, xb16_scr,
             sp_scr) = refs
        i = pl.program_id(0)

        @pl.when(i == 0)
        def _():
            carry[...] = hin[...]

        sp_scr[...] = _softplus(-lam_r[...])
        tmod = _tmod(sub_r, sub_r)

        def conv_in(r0, l0):
            rs, ls = pl.ds(r0, sub_r), pl.ds(l0, LANES)
            xb = _conv(vl[rs, ls], wcb_r, 4, 2, tmod, sub_r, ls) + bcb_r[:, ls]
            b_scr[rs, ls] = xb
            xb16_scr[rs, ls] = xb.astype(BF16)

        def gates(r0, l0):
            rs, ls = pl.ds(r0, sub_r), pl.ds(l0, LANES)
            _, ig, a, m2 = _gates(pre_scr[rs, ls] + ba_r[:, ls], pre_scr[rs, pl.ds(W + l0, LANES)] + bx_r[:, ls],
                                  sp_scr[:, ls])
            a_scr[rs, ls] = a
            b_scr[rs, ls] = jnp.sqrt(m2) * (ig * b_scr[rs, ls])

        _sub_loop(tl, sub_r, W, conv_in)
        _gate_matmul(xb16_scr, wd_r, pre_scr, W, ng, gs)
        _sub_loop(tl, sub_r, W, gates)

        def step(t, h):
            h = a_scr[pl.ds(t, 1), :] * h + b_scr[pl.ds(t, 1), :]
            hseq[pl.ds(t, 1), :] = h
            return h

        carry[...] = _row_loop(tl, rev, step, carry[...])

        if rev:
            def mix_out(r0, l0):
                rs, ls = pl.ds(r0, sub_r), pl.ds(l0, LANES)
                yb = (ho[rs, ls] + hseq[rs, ls]) * _silu(ql[rs, ls])
                ya = bl[rs, ls] * _conv(cl[rs, ls] * ul[rs, ls], wca_r, 3, 1, tmod, sub_r, ls) * _silu(gl[rs, ls])
                cat[rs, ls] = ya.astype(cat.dtype)
                cat[rs, pl.ds(W + l0, LANES)] = yb.astype(cat.dtype)

            _sub_loop(tl, sub_r, W, mix_out)

    scratch = [pltpu.VMEM((tl, W), F32), pltpu.VMEM((tl, W), F32), pltpu.VMEM((tl, 2 * W), F32), pltpu.VMEM((1, W), F32),
               pltpu.VMEM((tl, W), BF16), pltpu.VMEM((1, W), F32)]
    if rev:
        in_specs = [pcol(j) for j in range(6)] + [seq, taps, taps, vec, wd_spec, vec, vec, vec, vec]
        args = [P] * 6 + [h_other, wca, wcb, bcb, wd, ba, bx, lam, h_init]
        out_specs = [seq, pl.BlockSpec((tl, 2 * W), lambda i: (tile(i), 0))]
        out_shape = [jax.ShapeDtypeStruct((rows, W), F32), jax.ShapeDtypeStruct((rows, 2 * W), BF16)]
    else:
        in_specs = [pcol(4), taps, vec, wd_spec, vec, vec, vec, vec]
        args = [P, wcb, bcb, wd, ba, bx, lam, h_init]
        out_specs = [seq]
        out_shape = [jax.ShapeDtypeStruct((rows, W), F32)]
    return pl.pallas_call(
        body, name=name, grid=(nt,), in_specs=in_specs, out_specs=out_specs, out_shape=out_shape,
        scratch_shapes=scratch, compiler_params=_params(("arbitrary",)),
    )(*args)


_BWD_SCRATCH = ("xb", "a", "dyl", "g", "r", "ig", "m2", "pre", "xb16", "dp16", "sp", "dlf", "edge", "c")


def _bwd_scratch(tl, W):
    f32 = {n: pltpu.VMEM((tl, W), F32) for n in ("xb", "a", "dyl", "g", "r", "ig", "m2")}
    other = {"pre": pltpu.VMEM((tl, 2 * W), F32), "xb16": pltpu.VMEM((tl, W), BF16), "dp16": pltpu.VMEM((tl, 2 * W), BF16),
             "sp": pltpu.VMEM((1, W), F32), "dlf": pltpu.VMEM((1, W), F32), "edge": pltpu.VMEM((1, W), F32),
             "c": pltpu.VMEM((1, W), F32)}
    return [{**f32, **other}[n] for n in _BWD_SCRATCH]


def _lru_bwd_tile(d, v_ref, dy_fn, hs_ref, w_refs, scr, acc, first, tl, sub_r, ng, gs):
    wcb_r, bcb_r, wd_r, ba_r, bx_r, lam_r = w_refs
    dwd_ref, dba_ref, dbx_ref, dlam_ref = acc
    W = v_ref.shape[1]
    assert gs % LANES == 0
    rev = d == 0
    tmod = _tmod(sub_r, sub_r)
    lam = lam_r[...]
    scr["sp"][...] = _softplus(-lam)
    scr["dlf"][...] = -_sigmoid(-lam)

    @pl.when(first)
    def _():
        dwd_ref[...] = jnp.zeros_like(dwd_ref)
        dba_ref[...] = jnp.zeros_like(dba_ref)
        dbx_ref[...] = jnp.zeros_like(dbx_ref)
        dlam_ref[...] = jnp.zeros_like(dlam_ref)

    def conv_in(r0, l0):
        rs, ls = pl.ds(r0, sub_r), pl.ds(l0, LANES)
        xb = _conv(v_ref[rs, ls], wcb_r, 4, 2, tmod, sub_r, ls) + bcb_r[:, ls]
        scr["xb"][rs, ls] = xb
        scr["xb16"][rs, ls] = xb.astype(BF16)
        scr["dyl"][rs, ls] = dy_fn(rs, ls)

    def gates(r0, l0):
        rs, ls = pl.ds(r0, sub_r), pl.ds(l0, LANES)
        r, ig, a, m2 = _gates(scr["pre"][rs, ls] + ba_r[:, ls], scr["pre"][rs, pl.ds(W + l0, LANES)] + bx_r[:, ls],
                              scr["sp"][:, ls])
        scr["a"][rs, ls] = a
        scr["r"][rs, ls] = r
        scr["ig"][rs, ls] = ig
        scr["m2"][rs, ls] = m2

    _sub_loop(tl, sub_r, W, conv_in)
    _gate_matmul(scr["xb16"], wd_r, scr["pre"], W, ng, gs)
    _sub_loop(tl, sub_r, W, gates)

    def step(t, c):
        g = scr["dyl"][pl.ds(t, 1), :] + c
        scr["g"][pl.ds(t, 1), :] = g
        return scr["a"][pl.ds(t, 1), :] * g

    scr["c"][...] = _row_loop(tl, rev, step, scr["c"][...])
    row = lax.broadcasted_iota(jnp.int32, (sub_r, 1), 0)

    def grads(r0, l0):
        rs, ls = pl.ds(r0, sub_r), pl.ds(l0, LANES)
        g, a, m2 = scr["g"][rs, ls], scr["a"][rs, ls], scr["m2"][rs, ls]
        r, ig, xb = scr["r"][rs, ls], scr["ig"][rs, ls], scr["xb"][rs, ls]
        h = hs_ref[rs, ls]
        if d == 0:
            e0 = pl.multiple_of(jnp.maximum(r0 - SUBLANES, 0), SUBLANES)
            edge = jnp.where(r0 == 0, scr["edge"][:, ls], hs_ref[pl.ds(e0, SUBLANES), ls][SUBLANES - 1:, :])
            hprev = jnp.where(row == 0, edge, pltpu.roll(h, 1, 0))
        else:
            e0 = pl.multiple_of(jnp.minimum(r0 + sub_r, tl - SUBLANES), SUBLANES)
            edge = jnp.where(r0 == tl - sub_r, scr["edge"][:, ls], hs_ref[pl.ds(e0, SUBLANES), ls][:1, :])
            hprev = jnp.where(row == sub_r - 1, edge, pltpu.roll(h, sub_r - 1, 0))
        rsq = lax.rsqrt(m2)
        gm = g * (m2 * rsq)
        d_la = (g * hprev) * a - (g * (ig * xb)) * ((1.0 - m2) * rsq)
        d_pr = d_la * ((-LRU_C) * scr["sp"][:, ls]) * (r * (1.0 - r))
        d_pi = (gm * xb) * (ig * (1.0 - ig))
        scr["dyl"][rs, ls] = gm * ig
        dlam_ref[:, ls] += jnp.sum(d_la * ((-LRU_C) * r), axis=0, keepdims=True) * scr["dlf"][:, ls]
        dba_ref[:, ls] += jnp.sum(d_pr, axis=0, keepdims=True)
        dbx_ref[:, ls] += jnp.sum(d_pi, axis=0, keepdims=True)
        gi, off = divmod(l0, gs)
        scr["dp16"][rs, pl.ds(gi * 2 * gs + off, LANES)] = d_pr.astype(BF16)
        scr["dp16"][rs, pl.ds(gi * 2 * gs + gs + off, LANES)] = d_pi.astype(BF16)

    _sub_loop(tl, sub_r, W, grads)
    for gi in range(ng):
        dp = scr["dp16"][:, gi * 2 * gs:(gi + 1) * 2 * gs]
        scr["g"][:, gi * gs:(gi + 1) * gs] = lax.dot_general(dp, wd_r[gi], (((1,), (1,)), ((), ())),
                                                             preferred_element_type=F32)
        dwd_ref[gi] += lax.dot_general(scr["xb16"][:, gi * gs:(gi + 1) * gs], dp, (((0,), (0,)), ((), ())),
                                       preferred_element_type=F32)


def _edge_block(h, tl, nt, d):
    W = h.shape[1]
    per = tl // SUBLANES
    if d == 0:
        return pl.BlockSpec((SUBLANES, W), lambda i: (jnp.maximum((nt - 1 - i) * per - 1, 0), 0))
    return pl.BlockSpec((SUBLANES, W), lambda i: (jnp.minimum((i + 1) * per, nt * per - 1), 0))


def _mix_bwd0(P, dcat, h0s, h_init, c_init, wts, *, rows, row_off, row_w, tl, name):
    W = P.shape[1] // 6
    nt = rows // tl
    ob = row_off // tl
    gs = min(LRU_GROUP, W)
    ng = W // gs
    wcb, bcb = wts["wcb"], wts["bcb"]
    wd, ba, bx, lam = wts["wd"][0], wts["ba"][0], wts["bx"][0], wts["lam"][0]

    def tile(i):
        return nt - 1 - i

    vec = pl.BlockSpec((1, W), lambda i: (0, 0))
    taps = pl.BlockSpec((SUBLANES, W), lambda i: (0, 0))
    wd_spec = pl.BlockSpec(wd.shape, lambda i: (0, 0, 0))
    seq = pl.BlockSpec((tl, W), lambda i: (tile(i), 0))

    sub_r = min(row_w, tl)
    assert tl % sub_r == 0

    def body(vl, ql, dyb, hs, hedge8, wcb_r, bcb_r, wd_r, ba_r, bx_r, lam_r, hin, cin,
             dxb_o, dwd_o, dba_o, dbx_o, dlam_o, cfin, *scratch):
        scr = dict(zip(_BWD_SCRATCH, scratch, strict=True))
        i = pl.program_id(0)

        @pl.when(i == 0)
        def _():
            scr["c"][...] = cin[...]

        scr["edge"][...] = jnp.where(i == nt - 1, hin[...], hedge8[SUBLANES - 1:SUBLANES, :])
        _lru_bwd_tile(0, vl, lambda rs, ls: dyb[rs, ls] * _silu(ql[rs, ls]), hs, (wcb_r, bcb_r, wd_r, ba_r, bx_r, lam_r),
                      scr, (dwd_o, dba_o, dbx_o, dlam_o), i == 0, tl, sub_r, ng, gs)
        dxb_o[...] = scr["dyl"][...] + scr["g"][...]
        cfin[...] = scr["c"][...]

    return pl.pallas_call(
        body, name=name, grid=(nt,),
        in_specs=[pl.BlockSpec((tl, W), lambda i: (tile(i) + ob, 4)), pl.BlockSpec((tl, W), lambda i: (tile(i) + ob, 5)),
                  pl.BlockSpec((tl, W), lambda i: (tile(i), 1)), seq, _edge_block(h0s, tl, nt, 0),
                  taps, vec, wd_spec, vec, vec, vec, vec, vec],
        out_specs=[seq, wd_spec, vec, vec, vec, vec],
        out_shape=[jax.ShapeDtypeStruct((rows, W), F32), jax.ShapeDtypeStruct(wd.shape, F32)]
        + [jax.ShapeDtypeStruct((1, W), F32)] * 4,
        scratch_shapes=_bwd_scratch(tl, W),
        compiler_params=_params(("arbitrary",)),
    )(P, P, dcat, h0s, h0s, wcb, bcb, wd, ba, bx, lam, h_init, c_init)


def _mix_bwd1(P, dcat, h0s, h1s, dxb0, h_init, c_init, wts, *, rows, row_off, row_w, tl, name):
    W = P.shape[1] // 6
    nt = rows // tl
    ob = row_off // tl
    gs = min(LRU_GROUP, W)
    ng = W // gs
    wca, wcb, bcb = wts["wca"], wts["wcb"], wts["bcb"]
    wd, ba, bx, lam = wts["wd"][1], wts["ba"][1], wts["bx"][1], wts["lam"][1]

    vec = pl.BlockSpec((1, W), lambda i: (0, 0))
    taps = pl.BlockSpec((SUBLANES, W), lambda i: (0, 0))
    wd_spec = pl.BlockSpec(wd.shape, lambda i: (0, 0, 0))
    seq = pl.BlockSpec((tl, W), lambda i: (i, 0))

    sub_r = min(row_w, tl)
    assert tl % sub_r == 0

    def body(bl, cl, ul, gl, vl, ql, dya, dyb, h0, h1, hedge8, dx0, wca_r, wcb_r, bcb_r, wd_r, ba_r, bx_r, lam_r, hin, cin,
             dp_o, dwd_o, dba_o, dbx_o, dlam_o, dwca_o, dwcb_o, dbcb_o, cfin, *scratch):
        scr = dict(zip(_BWD_SCRATCH, scratch, strict=True))
        i = pl.program_id(0)

        @pl.when(i == 0)
        def _():
            scr["c"][...] = cin[...]
            dwca_o[...] = jnp.zeros_like(dwca_o)
            dwcb_o[...] = jnp.zeros_like(dwcb_o)
            dbcb_o[...] = jnp.zeros_like(dbcb_o)

        scr["edge"][...] = jnp.where(i == nt - 1, hin[...], hedge8[0:1, :])
        _lru_bwd_tile(1, vl, lambda rs, ls: dyb[rs, ls] * _silu(ql[rs, ls]), h1, (wcb_r, bcb_r, wd_r, ba_r, bx_r, lam_r),
                      scr, (dwd_o, dba_o, dbx_o, dlam_o), i == 0, tl, sub_r, ng, gs)
        cfin[...] = scr["c"][...]
        tmod = _tmod(sub_r, sub_r)

        def rest(r0, l0):
            rs, ls = pl.ds(r0, sub_r), pl.ds(l0, LANES)
            dxb = dx0[rs, ls] + scr["dyl"][rs, ls] + scr["g"][rs, ls]
            dv, dwb = _conv_bwd(dxb, vl[rs, ls], wcb_r, 4, 2, tmod, sub_r, ls)
            for j in range(4):
                dwcb_o[j:j + 1, ls] += dwb[j]
            dbcb_o[:, ls] += jnp.sum(dxb, axis=0, keepdims=True)
            q = ql[rs, ls]
            sq = _sigmoid(q)
            dq = dyb[rs, ls] * (h0[rs, ls] + h1[rs, ls]) * (sq * (1.0 + q * (1.0 - sq)))
            b_, c_, u_, g_ = bl[rs, ls], cl[rs, ls], ul[rs, ls], gl[rs, ls]
            z = c_ * u_
            cz = _conv(z, wca_r, 3, 1, tmod, sub_r, ls)
            sgm = _sigmoid(g_)
            sg = g_ * sgm
            da = dya[rs, ls]
            dz, dwa = _conv_bwd(da * b_ * sg, z, wca_r, 3, 1, tmod, sub_r, ls)
            for j in range(3):
                dwca_o[j:j + 1, ls] += dwa[j]
            parts = (da * cz * sg, dz * u_, dz * c_, da * b_ * cz * (sgm * (1.0 + g_ * (1.0 - sgm))), dv, dq)
            for k, val in enumerate(parts):
                dp_o[rs, pl.ds(k * W + l0, LANES)] = val.astype(dp_o.dtype)

        _sub_loop(tl, sub_r, W, rest)

    def pcol(j):
        return pl.BlockSpec((tl, W), lambda i: (i + ob, j))

    return pl.pallas_call(
        body, name=name, grid=(nt,),
        in_specs=[pcol(j) for j in range(6)]
        + [pl.BlockSpec((tl, W), lambda i: (i, 0)), pl.BlockSpec((tl, W), lambda i: (i, 1)), seq, seq,
           _edge_block(h1s, tl, nt, 1), seq, taps, taps, vec, wd_spec, vec, vec, vec, vec, vec],
        out_specs=[pl.BlockSpec((tl, 6 * W), lambda i: (i, 0)), wd_spec, vec, vec, vec, taps, taps, vec, vec],
        out_shape=[jax.ShapeDtypeStruct((rows, 6 * W), BF16), jax.ShapeDtypeStruct(wd.shape, F32)]
        + [jax.ShapeDtypeStruct((1, W), F32)] * 3
        + [jax.ShapeDtypeStruct((SUBLANES, W), F32)] * 2 + [jax.ShapeDtypeStruct((1, W), F32)] * 2,
        scratch_shapes=_bwd_scratch(tl, W),
        compiler_params=_params(("arbitrary",)),
    )(*([P] * 6), dcat, dcat, h0s, h1s, h1s, dxb0, wca, wcb, bcb, wd, ba, bx, lam, h_init, c_init)


def _loss_head(out, x, tgt, gt, fg, tr):
    L, D = x.shape

    def body(o_ref, x_ref, t_ref, gt_ref, fg_ref, dn_o, do_o, dfg_o, dgt_o, loss_o):
        i = pl.program_id(0)

        @pl.when(i == 0)
        def _():
            dfg_o[...] = jnp.zeros_like(dfg_o)
            dgt_o[...] = jnp.zeros_like(dgt_o)
            loss_o[...] = jnp.zeros_like(loss_o)

        o = o_ref[...]
        gt_v = gt_ref[...]
        fg_v = fg_ref[...]
        n = x_ref[...] + gt_v * o
        r = lax.rsqrt(jnp.mean(n * n, axis=-1, keepdims=True) + EPS)
        nr = n * r
        e = nr * fg_v - t_ref[...]
        loss_o[...] += 0.5 * jnp.sum(jnp.mean(e * e, axis=-1, keepdims=True))
        dy = e * (1.0 / D)
        dfg_o[...] += jnp.sum(dy * nr, axis=0, keepdims=True)
        qv = dy * fg_v
        dn = r * (qv - nr * jnp.mean(qv * nr, axis=-1, keepdims=True))
        dgt_o[...] += jnp.sum(dn * o, axis=0, keepdims=True)
        dn_o[...] = dn
        do_o[...] = (dn * gt_v).astype(do_o.dtype)

    blk = pl.BlockSpec((tr, D), lambda i: (i, 0))
    vec = pl.BlockSpec((1, D), lambda i: (0, 0))
    return pl.pallas_call(
        body, name="loss_head", grid=(L // tr,), in_specs=[blk, blk, blk, vec, vec],
        out_specs=[blk, blk, vec, vec, pl.BlockSpec((SUBLANES, LANES), lambda i: (0, 0))],
        out_shape=[jax.ShapeDtypeStruct((L, D), F32), jax.ShapeDtypeStruct((L, D), BF16),
                   jax.ShapeDtypeStruct((1, D), F32), jax.ShapeDtypeStruct((1, D), F32),
                   jax.ShapeDtypeStruct((SUBLANES, LANES), F32)],
        compiler_params=_params(("arbitrary",)),
    )(out, x, tgt, gt, fg)


def _norm_bwd(dhl, x, dn, g, sc, tr, name):
    L, D = x.shape
    with_x = dn is not None

    def body(*refs):
        if with_x:
            d_ref, x_ref, dn_ref, g_ref, sc_ref, gx_o, dsh_o, dsc_o, dg_o = refs
        else:
            d_ref, x_ref, g_ref, sc_ref, dsh_o, dsc_o, dg_o = refs
        i = pl.program_id(0)

        @pl.when(i == 0)
        def _():
            dsh_o[...] = jnp.zeros_like(dsh_o)
            dsc_o[...] = jnp.zeros_like(dsc_o)
            dg_o[...] = jnp.zeros_like(dg_o)

        d = d_ref[...]
        xv = x_ref[...]
        g_v = g_ref[...]
        r = lax.rsqrt(jnp.mean(xv * xv, axis=-1, keepdims=True) + EPS)
        xr = xv * r
        dsh_o[...] += jnp.sum(d, axis=0, keepdims=True)
        dsc_o[...] += jnp.sum(d * (xr * g_v), axis=0, keepdims=True)
        dxn = d * (1.0 + sc_ref[...])
        dg_o[...] += jnp.sum(dxn * xr, axis=0, keepdims=True)
        if with_x:
            qv = dxn * g_v
            gx_o[...] = r * (qv - xr * jnp.mean(qv * xr, axis=-1, keepdims=True)) + dn_ref[...]

    blk = pl.BlockSpec((tr, D), lambda i: (i, 0))
    vec = pl.BlockSpec((1, D), lambda i: (0, 0))
    vshape = jax.ShapeDtypeStruct((1, D), F32)
    res = pl.pallas_call(
        body, name=name, grid=(L // tr,),
        in_specs=[blk, blk] + ([blk] if with_x else []) + [vec, vec],
        out_specs=([blk] if with_x else []) + [vec, vec, vec],
        out_shape=([jax.ShapeDtypeStruct((L, D), F32)] if with_x else []) + [vshape] * 3,
        compiler_params=_params(("arbitrary",)),
    )(*([dhl, x] + ([dn] if with_x else []) + [g, sc]))
    return res if with_x else [None] + list(res)


def _pack_blockdiag(wa, wx, gs):
    H, hd, _ = wa.shape
    hp = gs // hd
    ng = H // hp
    eye = jnp.eye(hp, dtype=wa.dtype)

    def bd(w):
        return jnp.einsum("gpij,pq->gpiqj", w.reshape(ng, hp, hd, hd), eye).reshape(ng, gs, gs)

    return jnp.concatenate([bd(wa), bd(wx)], axis=-1).astype(BF16)


def _unpack_blockdiag(dwd, H, hd, gs):
    hp = gs // hd
    ng = H // hp
    eye = jnp.eye(hp, dtype=dwd.dtype)

    def diag(dm):
        return jnp.einsum("gpiqj,pq->gpij", dm.reshape(ng, hp, hd, hp, hd), eye).reshape(H, hd, hd)

    return diag(dwd[:, :, :gs]), diag(dwd[:, :, gs:])


def kernel(x, c, ctx, c_ctx, norm_g, w_ada, b_ada, w_in, w_conv_a, w_conv_b, b_conv_b, lru_wa, lru_ba, lru_wx, lru_bx, lru_lambda, w_out, final_g, loss_target, m_c_ctx, m_norm_g, m_w_ada, m_b_ada, m_w_in, m_w_conv_a, m_w_conv_b, m_b_conv_b, m_lru_wa, m_lru_ba, m_lru_wx, m_lru_bx, m_lru_lambda, m_w_out, m_final_g, v_c_ctx, v_norm_g, v_w_ada, v_b_ada, v_w_in, v_w_conv_a, v_w_conv_b, v_b_conv_b, v_lru_wa, v_lru_ba, v_lru_wx, v_lru_bx, v_lru_lambda, v_w_out, v_final_g):
    xi, yi, ci = _pos()
    me = 4 * xi + 2 * yi + ci
    q = 2 * xi + yi
    first_core = (ci == 0).astype(F32)

    L, D = x.shape[1], x.shape[2]
    T = ctx.shape[1]
    W = D // 2
    Wq = W // 4
    H, hd = lru_wa.shape[2], lru_wa.shape[3]
    gs = min(LRU_GROUP, W)
    nq = w_ada.shape[2]
    tl = min(256, T, L)
    tr = min(256, T, L)
    x2, ctx2, tgt2 = x[0], ctx[0], loss_target[0]

    def place(shard, full_cols):
        z = jnp.zeros((shard.shape[0], full_cols), F32)
        return lax.dynamic_update_slice(z, shard * first_core, (0, q * shard.shape[1]))

    c_rows = lax.dynamic_update_slice(jnp.zeros((8, D), F32), c, (me, 0))
    small_in = [c_rows, place(w_conv_a[0], W), place(w_conv_b[0], W), place(lru_ba[0], W), place(lru_bx[0], W),
                place(lru_lambda[0], W)]
    small_shapes = [a.shape for a in small_in]
    gathered = _allreduce8(_pack(small_in, 8 * SUBLANES), "gather_small")
    c_all, wca, wcb, ba_all, bx_all, lam_all = _unpack(gathered, small_shapes)

    s_rows = jnp.concatenate([c_all, c_ctx[None, :], jnp.zeros((7, D), F32)], axis=0)
    mod_part = _matmul(s_rows, w_ada[0], a_act="silu", bias=lax.dynamic_slice(b_ada, (0, q * nq), (1, nq)),
                       tm=16, tn=nq, tk=512, name="ada_fwd")
    mod_all = _allreduce8(_pack([place(mod_part, 4 * nq)], 8 * SUBLANES), "gather_mod")
    mod_all = _unpack(mod_all, [(16, 4 * nq)])[0]
    mod_l = lax.dynamic_slice(mod_all, (me, 0), (1, 3 * D))
    mod_c = mod_all[8:9]
    sh_l, sc_l, gt_l = mod_l[:, :D], mod_l[:, D:2 * D], mod_l[:, 2 * D:]
    sh_c, sc_c = mod_c[:, :D], mod_c[:, D:2 * D]

    pad_taps = lambda w: jnp.pad(w, ((0, SUBLANES - w.shape[0]), (0, 0)))
    wts = {
        "wca": pad_taps(wca), "wcb": pad_taps(wcb), "bcb": b_conv_b,
        "wd": [_pack_blockdiag(lru_wa[0, d], lru_wx[0, d], gs) for d in range(2)],
        "ba": [ba_all[d:d + 1] for d in range(2)], "bx": [bx_all[d:d + 1] for d in range(2)],
        "lam": [lam_all[d:d + 1] for d in range(2)],
    }

    hl = _norm_in(x2, ctx2, norm_g, sc_l, sh_l, sc_c, sh_c, tr)
    p_lat, win_full, wout_full = _in_proj_gather(hl, w_in[0].astype(BF16), w_out[0].astype(BF16),
                                                 jnp.reshape(q, (1,)).astype(jnp.int32), rows=L, tm=min(1024, L))
    p_ctx = _matmul(hl, win_full, a_rows=T, a_off=L, tm=T, tn=1536, tk=D, name="in_proj_ctx")
    zero_w = jnp.zeros((1, W), F32)
    c0s = _mix_fwd(p_ctx, 0, zero_w, wts, rows=T, row_off=0, row_w=T, tl=tl, name="ctx_fwd0")[0]
    c1s, _ = _mix_fwd(p_ctx, 1, zero_w, wts, rows=T, row_off=0, row_w=T, tl=tl, h_other=c0s, name="ctx_fwd1")
    h0_init, h1_init = c0s[T - 1:T], c1s[0:1]
    h0s = _mix_fwd(p_lat, 0, h0_init, wts, rows=L, row_off=0, row_w=GRID_W, tl=tl, name="mix_fwd0")[0]
    h1s, cat = _mix_fwd(p_lat, 1, h1_init, wts, rows=L, row_off=0, row_w=GRID_W, tl=tl, h_other=h0s, name="mix_fwd1")
    out = _matmul(cat, wout_full, tm=512, tn=D, tk=2 * W, name="out_proj")
    dn, dout, dfg, dgt, loss_blk = _loss_head(out, x2, tgt2, gt_l, final_g[None, :], tr)

    dcat = _matmul(dout, wout_full, tb=True, tm=512, tn=2 * W, tk=D, name="out_proj_bwd")
    gw_out = _matmul(cat, dout, ta=True, tm=1024, tn=D, tk=1024, out_dtype=BF16, name="w_out_grad")
    dxb0, dwd0, dba0, dbx0, dlam0, ch0 = _mix_bwd0(p_lat, dcat, h0s, h0_init, zero_w, wts, rows=L, row_off=0,
                                                   row_w=GRID_W, tl=tl, name="mix_bwd0")
    dp, dwd1, dba1, dbx1, dlam1, dwca, dwcb, dbcb, ch1 = _mix_bwd1(
        p_lat, dcat, h0s, h1s, dxb0, h1_init, zero_w, wts, rows=L, row_off=0, row_w=GRID_W, tl=tl, name="mix_bwd1")
    zero_cat = jnp.zeros((T, 2 * W), F32)
    cxb0, cwd0, cba0, cbx0, clam0, _ = _mix_bwd0(p_ctx, zero_cat, c0s, zero_w, ch0, wts, rows=T, row_off=0,
                                                 row_w=T, tl=tl, name="ctx_bwd0")
    dp_c, cwd1, cba1, cbx1, clam1, cwca, cwcb, cbcb, _ = _mix_bwd1(
        p_ctx, zero_cat, c0s, c1s, cxb0, zero_w, ch1, wts, rows=T, row_off=0, row_w=T, tl=tl, name="ctx_bwd1")

    gw_in_ctx = _matmul(hl, dp_c, ta=True, a_rows=T, a_off=L, tm=1024, tn=1536, tk=T, name="w_in_grad_ctx")
    gw_in = _matmul(hl, dp, ta=True, a_rows=L, tm=1024, tn=1536, tk=1024, init=gw_in_ctx, out_dtype=BF16,
                    name="w_in_grad")
    rs_axes = [1, 0]
    pair_sums = _rs_pair_sums([gw_in, gw_out], rs_axes)
    dhl, rs_slots = _matmul(dp, win_full, tb=True, tm=1024, tn=D, tk=1024, name="in_proj_bwd",
                            side=_rs_chips_side(pair_sums))
    dhc = _matmul(dp_c, win_full, tb=True, tm=T, tn=D, tk=512, name="in_proj_bwd_ctx")
    gx, dsh_l, dsc_l, dng_l = _norm_bwd(dhl, x2, dn, norm_g, sc_l, tr, "norm_bwd")
    _, dsh_c, dsc_c, dng_c = _norm_bwd(dhc, ctx2, None, norm_g, sc_c, tr, "norm_bwd_ctx")

    g_in_shard, g_out_shard = _rs_finish(rs_slots, rs_axes)

    dwa0, dwx0 = _unpack_blockdiag(dwd0 + cwd0, H, hd, gs)
    dwa1, dwx1 = _unpack_blockdiag(dwd1 + cwd1, H, hd, gs)
    zeros_d = jnp.zeros((1, D), F32)
    dmod_l = jnp.concatenate([dsh_l, dsc_l, dgt], axis=1)
    dmod_c = jnp.concatenate([dsh_c, dsc_c, zeros_d], axis=1)
    small_g = [
        lax.dynamic_update_slice(jnp.zeros((8, 3 * D), F32), dmod_l, (me, 0)), dmod_c,
        dfg, dng_l + dng_c, (dwca + cwca)[:3], (dwcb + cwcb)[:4], dbcb + cbcb,
        jnp.stack([dwa0, dwa1]), jnp.stack([dwx0, dwx1]),
        jnp.concatenate([dba0 + cba0, dba1 + cba1], axis=0), jnp.concatenate([dbx0 + cbx0, dbx1 + cbx1], axis=0),
        jnp.concatenate([dlam0 + clam0, dlam1 + clam1], axis=0),
    ]
    g_shapes = [a.shape for a in small_g]
    (g_rows, g_modc, g_fg, g_ng, g_wca, g_wcb, g_bcb, g_wa, g_wx, g_ba, g_bx, g_lam) = _unpack(
        _allreduce8(_pack(small_g, 8 * SUBLANES), "reduce_small"), g_shapes)

    g_mod = jnp.concatenate([g_rows, g_modc, jnp.zeros((7, 3 * D), F32)], axis=0)
    g_mod_q = lax.dynamic_slice(g_mod, (0, q * nq), (16, nq))
    g_w_ada = _matmul(s_rows, g_mod_q, ta=True, a_act="silu", tm=1024, tn=nq, tk=16, name="w_ada_grad")
    g_b_ada = jnp.sum(g_mod[:9], axis=0, keepdims=True)
    gc_part = _matmul(jnp.pad(lax.dynamic_slice(g_modc, (0, q * nq), (1, nq)), ((0, 7), (0, 0))), w_ada[0], tb=True,
                      dsilu_mul=c_ctx[None, :], tm=8, tn=D, tk=512, name="c_ctx_grad")
    g_c_ctx = _unpack(_allreduce8(_pack([gc_part[0:1] * first_core], 8 * SUBLANES), "reduce_c_ctx"), [(D,)])[0]

    def shard_cols(a, width):
        return lax.dynamic_slice(a, (0, q * width), (a.shape[0], width))

    grads = {
        "c_ctx": g_c_ctx, "norm_g": g_ng, "b_ada": g_b_ada,
        "w_conv_a": shard_cols(g_wca, Wq)[None], "w_conv_b": shard_cols(g_wcb, Wq)[None], "b_conv_b": g_bcb,
        "lru_wa": g_wa[None], "lru_ba": shard_cols(g_ba, Wq)[None], "lru_wx": g_wx[None],
        "lru_bx": shard_cols(g_bx, Wq)[None], "lru_lambda": shard_cols(g_lam, Wq)[None], "final_g": g_fg[0],
    }
    small_names = list(grads)
    given = dict(c_ctx=(c_ctx, m_c_ctx, v_c_ctx), norm_g=(norm_g, m_norm_g, v_norm_g), b_ada=(b_ada, m_b_ada, v_b_ada),
                 w_conv_a=(w_conv_a, m_w_conv_a, v_w_conv_a), w_conv_b=(w_conv_b, m_w_conv_b, v_w_conv_b),
                 b_conv_b=(b_conv_b, m_b_conv_b, v_b_conv_b), lru_wa=(lru_wa, m_lru_wa, v_lru_wa),
                 lru_ba=(lru_ba, m_lru_ba, v_lru_ba), lru_wx=(lru_wx, m_lru_wx, v_lru_wx),
                 lru_bx=(lru_bx, m_lru_bx, v_lru_bx), lru_lambda=(lru_lambda, m_lru_lambda, v_lru_lambda),
                 final_g=(final_g, m_final_g, v_final_g))
    shapes = [given[n][0].shape for n in small_names]
    packed = [_pack([given[n][j] for n in small_names], 2 * SUBLANES) for j in range(3)]
    packed_g = _pack([grads[n] for n in small_names], 2 * SUBLANES)
    sd, sm, sv = _adam(packed[0], packed_g, packed[1], packed[2], "adam_small")
    delta_s = dict(zip(small_names, _unpack(sd, shapes)))
    newm_s = dict(zip(small_names, _unpack(sm, shapes)))
    newv_s = dict(zip(small_names, _unpack(sv, shapes)))
    grads = {n: grads[n].reshape(given[n][0].shape) for n in small_names}

    big = {"w_ada": (w_ada, g_w_ada, m_w_ada, v_w_ada), "w_in": (w_in, g_in_shard, m_w_in, v_w_in),
           "w_out": (w_out, g_out_shard, m_w_out, v_w_out)}
    delta_b, newm_b, newv_b = {}, {}, {}
    for n, (w, g, m, v) in big.items():
        grads[n] = g[None]
        d_, m_, v_ = _adam(w[0], g, m[0], v[0], "adam_" + n)
        delta_b[n], newm_b[n], newv_b[n] = d_[None], m_[None], v_[None]

    loss = lax.psum(loss_blk[0, 0], AXES)
    order = ["c_ctx", "norm_g", "w_ada", "b_ada", "w_in", "w_conv_a", "w_conv_b", "b_conv_b", "lru_wa", "lru_ba",
             "lru_wx", "lru_bx", "lru_lambda", "w_out", "final_g"]
    delta = {**delta_s, **delta_b}
    newm = {**newm_s, **newm_b}
    newv = {**newv_s, **newv_b}
    return (loss, gx[None], *[grads[n] for n in order], *[delta[n] for n in order], *[newm[n] for n in order],
            *[newv[n] for n in order])
```

```python
import functools

import jax
import jax.numpy as jnp
from jax import lax
from jax.experimental import pallas as pl
from jax.experimental.pallas import tpu as pltpu

F32 = jnp.float32
BF16 = jnp.bfloat16
MESH_ID = pl.DeviceIdType.MESH
AXES = ("x", "y", "c")

EPS = 1e-6
LRU_C = 8.0
GRID_W = 64
ADAM_LR = 0.001
ADAM_B1 = 0.9
ADAM_B2 = 0.999
ADAM_EPS = 1e-08
ADAM_WD = 0.01
ADAM_STEP = 10

LANES = 128
SUBLANES = 8
PACK_COLS = 1024
VMEM_LIMIT = 56 * 2**20
LRU_GROUP = 256


def _params(sem=None):
    return pltpu.CompilerParams(vmem_limit_bytes=VMEM_LIMIT, dimension_semantics=sem)


def _pick(dim, pref, quantum=LANES):
    if dim <= pref:
        return dim
    best = None
    for t in range(quantum, pref + 1, quantum):
        if dim % t == 0:
            best = t
    assert best is not None, (dim, pref)
    return best


def _pos():
    return lax.axis_index("x"), lax.axis_index("y"), lax.axis_index("c")


def _flip(v, bit):
    return 1 - v if bit else v


def _sigmoid(v):
    return 0.5 * jnp.tanh(0.5 * v) + 0.5


def _silu(v):
    return v * _sigmoid(v)


def _dsilu(v):
    s = _sigmoid(v)
    return s * (1.0 + v * (1.0 - s))


def _gates(pre_r, pre_i, sp):
    r = _sigmoid(pre_r)
    ig = _sigmoid(pre_i)
    e = LRU_C * r * sp
    w = jnp.tanh(e)
    return r, ig, jnp.exp(-e), (2.0 * w) * pl.reciprocal(1.0 + w, approx=True)


def _softplus(z):
    return jnp.maximum(z, 0.0) + jnp.log1p(jnp.exp(-jnp.abs(z)))


def _matmul(a, b, *, ta=False, tb=False, tm=512, tn=512, tk=512, out_dtype=F32, name,
            a_rows=None, a_off=0, a_act=None, init=None, bias=None, dsilu_mul=None, side=None):
    rows_a = a.shape[0] if a_rows is None else a_rows
    if ta:
        K, M = rows_a, a.shape[1]
    else:
        M, K = rows_a, a.shape[1]
    N = b.shape[0] if tb else b.shape[1]
    tm, tn, tk = _pick(M, tm, SUBLANES), _pick(N, tn), _pick(K, tk)
    t_rows = tk if ta else tm
    assert a_off % t_rows == 0
    nk = K // tk
    gi, gj = M // tm, N // tn
    off_blocks = a_off // t_rows
    dims = (((0 if ta else 1,), (1 if tb else 0,)), ((), ()))
    extras = [e for e in (init, bias, dsilu_mul) if e is not None]
    n_sin = len(side["ins"]) if side else 0
    n_sout = len(side["outs"]) if side else 0

    def body(a_ref, b_ref, *rest):
        rest = list(rest)
        init_ref = rest.pop(0) if init is not None else None
        bias_ref = rest.pop(0) if bias is not None else None
        dsm_ref = rest.pop(0) if dsilu_mul is not None else None
        side_in = [rest.pop(0) for _ in range(n_sin)]
        o_ref = rest.pop(0)
        side_out = [rest.pop(0) for _ in range(n_sout)]
        acc_ref = rest.pop(0) if nk > 1 else None
        side_scr = rest
        i, j, k = pl.program_id(0), pl.program_id(1), pl.program_id(2)

        if side:
            @pl.when((i == 0) & (j == 0) & (k == 0))
            def _():
                side["start"](side_in, side_out, side_scr)

        av = a_ref[...]
        if a_act == "silu":
            av = _silu(av)
        prod = lax.dot_general(av, b_ref[...], dims, preferred_element_type=F32)

        def finish(r):
            if bias_ref is not None:
                r = r + bias_ref[...]
            if dsm_ref is not None:
                r = r * _dsilu(dsm_ref[...])
            o_ref[...] = r.astype(o_ref.dtype)

        if nk == 1:
            finish(prod if init_ref is None else prod + init_ref[...])
        else:
            @pl.when(k == 0)
            def _():
                acc_ref[...] = prod if init_ref is None else prod + init_ref[...]

            @pl.when(k > 0)
            def _():
                acc_ref[...] += prod

            @pl.when(k == nk - 1)
            def _():
                finish(acc_ref[...])

        if side:
            @pl.when((i == gi - 1) & (j == gj - 1) & (k == nk - 1))
            def _():
                side["finish"](side_in, side_out, side_scr)

    if ta:
        a_spec = pl.BlockSpec((tk, tm), lambda i, j, k: (k + off_blocks, i))
    else:
        a_spec = pl.BlockSpec((tm, tk), lambda i, j, k: (i + off_blocks, k))
    if tb:
        b_spec = pl.BlockSpec((tn, tk), lambda i, j, k: (j, k))
    else:
        b_spec = pl.BlockSpec((tk, tn), lambda i, j, k: (k, j))
    in_specs = [a_spec, b_spec]
    if init is not None:
        in_specs.append(pl.BlockSpec((tm, tn), lambda i, j, k: (i, j)))
    if bias is not None:
        in_specs.append(pl.BlockSpec((1, tn), lambda i, j, k: (0, j)))
    if dsilu_mul is not None:
        in_specs.append(pl.BlockSpec((1, tn), lambda i, j, k: (0, j)))
    hbm = pl.BlockSpec(memory_space=pl.ANY)
    res = pl.pallas_call(
        body, name=name, grid=(gi, gj, nk),
        in_specs=in_specs + [hbm] * n_sin,
        out_specs=[pl.BlockSpec((tm, tn), lambda i, j, k: (i, j))] + [hbm] * n_sout,
        out_shape=[jax.ShapeDtypeStruct((M, N), out_dtype)] + (list(side["outs"]) if side else []),
        scratch_shapes=([pltpu.VMEM((tm, tn), F32)] if nk > 1 else []) + (list(side["scratch"]) if side else []),
        compiler_params=_params(("arbitrary",) * 3 if side else ("parallel", "parallel", "arbitrary")),
    )(a, b, *extras, *(side["ins"] if side else []))
    return (res[0], res[1:]) if side else res[0]


def _matmul_rows(a, b, epi, *, tb=False, tm, tk, name, side=None):
    M, K = a.shape
    N = b.shape[0] if tb else b.shape[1]
    tm, tk = _pick(M, tm, SUBLANES), _pick(K, tk)
    gi, nk = M // tm, K // tk
    n_in, n_vec, n_out, n_sum = len(epi["ins"]), len(epi["vecs"]), len(epi["outs"]), epi["n_sum"]
    n_sin = len(side["ins"]) if side else 0
    n_sout = len(side["outs"]) if side else 0
    dims = (((1,), (1 if tb else 0,)), ((), ()))

    def body(a_ref, b_ref, *rest):
        rest = list(rest)
        in_refs = [rest.pop(0) for _ in range(n_in)]
        vec_refs = [rest.pop(0) for _ in range(n_vec)]
        side_in = [rest.pop(0) for _ in range(n_sin)]
        out_refs = [rest.pop(0) for _ in range(n_out)]
        sum_refs = [rest.pop(0) for _ in range(n_sum)]
        side_out = [rest.pop(0) for _ in range(n_sout)]
        acc_ref, part_ref = rest.pop(0), rest.pop(0)
        side_scr = rest
        i, k = pl.program_id(0), pl.program_id(1)

        @pl.when((i == 0) & (k == 0))
        def _():
            part_ref[...] = jnp.zeros_like(part_ref)
            if side:
                side["start"](side_in, side_out, side_scr)

        prod = lax.dot_general(a_ref[...], b_ref[...], dims, preferred_element_type=F32)
        if nk == 1:
            acc_ref[...] = prod
        else:
            @pl.when(k == 0)
            def _():
                acc_ref[...] = prod

            @pl.when(k > 0)
            def _():
                acc_ref[...] += prod

        @pl.when(k == nk - 1)
        def _():
            def group(gidx, carry):
                rs = pl.ds(pl.multiple_of(gidx * SUBLANES, SUBLANES), SUBLANES)
                outs, terms = epi["fn"](acc_ref[rs, :], [r[rs, :] for r in in_refs], [v[...] for v in vec_refs])
                for o_ref, val in zip(out_refs, outs, strict=True):
                    o_ref[rs, :] = val.astype(o_ref.dtype)
                for s, term in enumerate(terms):
                    part_ref[s] += term
                return carry

            lax.fori_loop(0, tm // SUBLANES, group, 0)

        @pl.when((i == gi - 1) & (k == nk - 1))
        def _():
            for s in range(n_sum):
                sum_refs[s][...] = jnp.sum(part_ref[s], axis=0, keepdims=True)
            if side:
                side["finish"](side_in, side_out, side_scr)

    once = dict(pipeline_mode=pl.Buffered(1)) if nk == 1 else {}
    b_spec = pl.BlockSpec((N, tk), lambda i, k: (0, k), **once) if tb else pl.BlockSpec((tk, N), lambda i, k: (k, 0), **once)
    rows = pl.BlockSpec((tm, N), lambda i, k: (i, 0))
    vec = pl.BlockSpec((1, N), lambda i, k: (0, 0))
    hbm = pl.BlockSpec(memory_space=pl.ANY)
    res = pl.pallas_call(
        body, name=name, grid=(gi, nk),
        in_specs=[pl.BlockSpec((tm, tk), lambda i, k: (i, k)), b_spec] + [rows] * n_in + [vec] * n_vec + [hbm] * n_sin,
        out_specs=[rows] * n_out + [vec] * n_sum + [hbm] * n_sout,
        out_shape=[jax.ShapeDtypeStruct((M, N), dt) for dt in epi["outs"]] + [jax.ShapeDtypeStruct((1, N), F32)] * n_sum
        + (list(side["outs"]) if side else []),
        scratch_shapes=[pltpu.VMEM((tm, N), F32), pltpu.VMEM((n_sum, SUBLANES, N), F32)]
        + (list(side["scratch"]) if side else []),
        compiler_params=_params(("arbitrary", "arbitrary")),
    )(a, b, *epi["ins"], *epi["vecs"], *(side["ins"] if side else []))
    outs, sums = res[:n_out], res[n_out:n_out + n_sum]
    return (outs, sums, res[n_out + n_sum:]) if side else (outs, sums)


def _elementwise(fn, ins, outs, *, rows, cols, name, tr=256):
    tr = _pick(rows, tr, 2 * SUBLANES)
    n_in = len(ins)

    def body(*refs):
        vals = fn(*[r[...] for r in refs[:n_in]])
        if not isinstance(vals, (tuple, list)):
            vals = (vals,)
        for r, v in zip(refs[n_in:], vals, strict=True):
            r[...] = v.astype(r.dtype)

    def spec(off):
        assert off % tr == 0
        ob = off // tr
        return pl.BlockSpec((tr, cols), lambda i: (i + ob, 0))

    res = pl.pallas_call(
        body, name=name, grid=(rows // tr,),
        in_specs=[spec(off) for _, off in ins],
        out_specs=[spec(0) for _ in outs],
        out_shape=[jax.ShapeDtypeStruct((rows, cols), dt) for dt in outs],
        compiler_params=_params(("parallel",)),
    )(*[a for a, _ in ins])
    return res


def _adam_math(w, g, m, v):
    m = ADAM_B1 * m + (1.0 - ADAM_B1) * g
    v = ADAM_B2 * v + (1.0 - ADAM_B2) * (g * g)
    m_hat = m / (1.0 - ADAM_B1 ** ADAM_STEP)
    v_hat = v / (1.0 - ADAM_B2 ** ADAM_STEP)
    delta = -ADAM_LR * (m_hat / (jnp.sqrt(v_hat) + ADAM_EPS) + ADAM_WD * w)
    return delta, m, v


def _adam(w, g, m, v, name):
    rows, cols = w.shape
    return _elementwise(_adam_math, [(w, 0), (g, 0), (m, 0), (v, 0)], [F32, F32, F32],
                        rows=rows, cols=cols, name=name)


def _pack(arrs, row_quantum):
    flat = jnp.concatenate([a.reshape(-1).astype(F32) for a in arrs])
    n = flat.shape[0]
    q = row_quantum * PACK_COLS
    total = -(-n // q) * q
    flat = jnp.pad(flat, (0, total - n))
    return flat.reshape(total // PACK_COLS, PACK_COLS)


def _unpack(buf, shapes):
    flat = buf.reshape(-1)
    out, off = [], 0
    for s in shapes:
        n = 1
        for d in s:
            n *= d
        out.append(flat[off:off + n].reshape(s))
        off += n
    return out


def _allreduce8(buf, name):
    R, C = buf.shape
    assert R % (8 * SUBLANES) == 0
    m = R // 8

    def body(x_ref, o_ref, recv, red, s1, r1, s2, r2):
        x, y, c = _pos()
        me = 4 * x + 2 * y + c

        def peer(k):
            px, py, pc = _flip(x, (k >> 2) & 1), _flip(y, (k >> 1) & 1), _flip(c, k & 1)
            return (px, py, pc), 4 * px + 2 * py + pc

        def rows(ref, idx):
            return ref.at[pl.ds(pl.multiple_of(idx * m, SUBLANES), m), :]

        def scatter(k):
            dev, p = peer(k)
            return pltpu.make_async_remote_copy(src_ref=rows(x_ref, p), dst_ref=recv.at[k], send_sem=s1.at[k],
                                                recv_sem=r1.at[k], device_id=dev, device_id_type=MESH_ID)

        def share(k):
            dev, p = peer(k)
            return pltpu.make_async_remote_copy(src_ref=red, dst_ref=rows(o_ref, me), send_sem=s2.at[k],
                                                recv_sem=r2.at[k], device_id=dev, device_id_type=MESH_ID)

        def shared_from(k):
            dev, p = peer(k)
            return pltpu.make_async_remote_copy(src_ref=red, dst_ref=rows(o_ref, p), send_sem=s2.at[k],
                                                recv_sem=r2.at[k], device_id=dev, device_id_type=MESH_ID)

        for k in range(1, 8):
            scatter(k).start()
        acc = rows(x_ref, me)[...]
        for k in range(1, 8):
            scatter(k).wait_recv()
            acc = acc + recv[k]
        red[...] = acc
        rows(o_ref, me)[...] = acc
        for k in range(1, 8):
            share(k).start()
        for k in range(1, 8):
            shared_from(k).wait_recv()
        for k in range(1, 8):
            scatter(k).wait_send()
            share(k).wait_send()

    return pl.pallas_call(
        body, name=name,
        in_specs=[pl.BlockSpec(memory_space=pltpu.VMEM)],
        out_specs=pl.BlockSpec(memory_space=pltpu.VMEM),
        out_shape=jax.ShapeDtypeStruct((R, C), F32),
        scratch_shapes=[pltpu.VMEM((8, m, C), F32), pltpu.VMEM((m, C), F32),
                        pltpu.SemaphoreType.DMA((8,)), pltpu.SemaphoreType.DMA((8,)),
                        pltpu.SemaphoreType.DMA((8,)), pltpu.SemaphoreType.DMA((8,))],
        compiler_params=_params(),
    )(buf)


def _bounce(src, dst, buf, sem):
    cin = pltpu.make_async_copy(src, buf, sem)
    cin.start()
    cin.wait()
    cout = pltpu.make_async_copy(buf, dst, sem)
    cout.start()
    cout.wait()


def _chunk(ref, axis, idx, size):
    start = idx * size
    if axis == 0:
        return ref.at[pl.ds(start, size), :]
    return ref.at[:, pl.ds(start, size)]


def _in_proj_gather(hl, win, wout, q_arr, *, rows, tm):
    D, nq = win.shape
    dq, D2 = wout.shape
    ni = rows // tm
    ops = ((0, 1, nq, D // 2), (1, 0, dq, dq // 2))

    def body(q_ref, a_ref, win_ref, wout_ref, p_ref, gin_ref, gout_ref, b_scr, buf_out, lsem, ssem, rsem, fsem, gsem):
        j, i = pl.program_id(0), pl.program_id(1)
        x, y, c = _pos()
        q = 2 * x + y
        srcs = (win_ref, wout_ref)
        dsts = (gin_ref, gout_ref)

        def shard_window(o, chip):
            _, axis, size, _ = ops[o]
            return _chunk(dsts[o], axis, chip, size)

        def half(ref, o, core):
            return ref.at[pl.ds(core * ops[o][3], ops[o][3]), :]

        def half_window(o, chip, core):
            _, axis, size, hs = ops[o]
            if axis == 1:
                return dsts[o].at[pl.ds(core * hs, hs), pl.ds(chip * size, size)]
            return dsts[o].at[pl.ds(chip * size + core * hs, hs), :]

        def chip_of(k):
            px, py = _flip(x, (k >> 1) & 1), _flip(y, k & 1)
            return px, py, 2 * px + py

        def send(o, k):
            px, py, _ = chip_of(k)
            return pltpu.make_async_remote_copy(
                src_ref=half(srcs[o], o, c), dst_ref=half_window(o, q, c), send_sem=ssem.at[o, k],
                recv_sem=rsem.at[o, k], device_id=(px, py, c), device_id_type=MESH_ID)

        def arrive(o, k):
            px, py, pq = chip_of(k)
            landed = half_window(o, pq, c)
            pltpu.make_async_remote_copy(src_ref=landed, dst_ref=landed, send_sem=ssem.at[o, k], recv_sem=rsem.at[o, k],
                                         device_id=(px, py, c), device_id_type=MESH_ID).wait_recv()
            fw = pltpu.make_async_remote_copy(src_ref=landed, dst_ref=landed, send_sem=fsem.at[o, k],
                                              recv_sem=gsem.at[o, k], device_id=(x, y, 1 - c), device_id_type=MESH_ID)
            fw.start()
            theirs = half_window(o, pq, 1 - c)
            pltpu.make_async_remote_copy(src_ref=theirs, dst_ref=theirs, send_sem=fsem.at[o, k], recv_sem=gsem.at[o, k],
                                         device_id=(x, y, 1 - c), device_id_type=MESH_ID).wait_recv()
            fw.wait_send()

        def load_b(src):
            cp = pltpu.make_async_copy(src, b_scr, lsem.at[0])
            cp.start()
            cp.wait()

        def relay(o, core):
            if core == 0:
                landed, target = half_window(o, chip_of(2)[2], 0), (x, 1 - y, 0)
            else:
                landed, target = half_window(o, chip_of(1)[2], 1), (1 - x, y, 1)
            return pltpu.make_async_remote_copy(src_ref=landed, dst_ref=landed, send_sem=ssem.at[o, 3],
                                                recv_sem=rsem.at[o, 3], device_id=target, device_id_type=MESH_ID)

        def on_core(core, fn):
            @pl.when(c == core)
            def _():
                fn()

        @pl.when((j == 0) & (i == 0))
        def _():
            for o in range(2):
                for k in (2, 1):
                    send(o, k).start()
            load_b(win_ref)
            own = pltpu.make_async_copy(b_scr, shard_window(0, q), lsem.at[0])
            own.start()
            own.wait()
            _bounce(wout_ref, shard_window(1, q), buf_out, lsem.at[1])

        @pl.when((j == 1) & (i == 0))
        def _():
            arrive(0, 2)
            on_core(0, lambda: relay(0, 0).start())
            load_b(shard_window(0, chip_of(2)[2]))

        @pl.when((j == 2) & (i == 0))
        def _():
            arrive(0, 1)
            on_core(1, lambda: relay(0, 1).start())
            load_b(shard_window(0, chip_of(1)[2]))

        @pl.when((j == 3) & (i == 0))
        def _():
            arrive(0, 3)
            load_b(shard_window(0, chip_of(3)[2]))
            arrive(1, 2)
            on_core(0, lambda: relay(1, 0).start())
            arrive(1, 1)
            on_core(1, lambda: relay(1, 1).start())

        p_ref[...] = jnp.dot(a_ref[...], b_scr[...], preferred_element_type=F32)

        @pl.when((j == 3) & (i == ni - 1))
        def _():
            arrive(1, 3)
            for o in range(2):
                for k in (2, 1):
                    send(o, k).wait_send()
                for core in range(2):
                    on_core(core, lambda o=o, core=core: relay(o, core).wait_send())

    hbm = pl.BlockSpec(memory_space=pl.ANY)
    grid_spec = pltpu.PrefetchScalarGridSpec(
        num_scalar_prefetch=1, grid=(4, ni),
        in_specs=[pl.BlockSpec((tm, D), lambda j, i, qr: (i, 0)), hbm, hbm],
        out_specs=[pl.BlockSpec((tm, nq), lambda j, i, qr: (i, jnp.bitwise_xor(qr[0], ((j & 1) << 1) | (j >> 1)))),
                   hbm, hbm],
        scratch_shapes=[pltpu.VMEM(win.shape, win.dtype), pltpu.VMEM(wout.shape, wout.dtype), pltpu.SemaphoreType.DMA((2,))]
        + [pltpu.SemaphoreType.DMA((2, 4)) for _ in range(4)])
    return pl.pallas_call(
        body, name="in_proj_gather", grid_spec=grid_spec,
        out_shape=[jax.ShapeDtypeStruct((rows, 4 * nq), F32), jax.ShapeDtypeStruct((D, 4 * nq), win.dtype),
                   jax.ShapeDtypeStruct((4 * dq, D2), wout.dtype)],
        compiler_params=_params(("arbitrary", "arbitrary")),
    )(q_arr, hl, win, wout)


def _rs_to_sibling(gs, axes):
    n = len(gs)
    shapes = []
    for g, ax in zip(gs, axes):
        s = list(g.shape)
        s[ax] //= 8
        shapes.append(tuple(s))

    def body(*refs):
        g_refs, mine, landed = refs[:n], refs[n:2 * n], refs[2 * n:3 * n]
        bufs = refs[3 * n:4 * n]
        lsem, ssem, rsem = refs[4 * n:]
        x, y, c = _pos()
        cps = []
        for o in range(n):
            size = shapes[o][axes[o]]
            for j in range(4):
                rc = pltpu.make_async_remote_copy(
                    src_ref=_chunk(g_refs[o], axes[o], 2 * j + 1 - c, size), dst_ref=landed[o].at[j],
                    send_sem=ssem.at[o, j], recv_sem=rsem.at[o, j], device_id=(x, y, 1 - c), device_id_type=MESH_ID)
                rc.start()
                cps.append(rc)
        for o in range(n):
            size = shapes[o][axes[o]]
            for j in range(4):
                _bounce(_chunk(g_refs[o], axes[o], 2 * j + c, size), mine[o].at[j], bufs[o], lsem.at[o])
        for rc in cps:
            rc.wait()

    hbm = pl.BlockSpec(memory_space=pl.ANY)
    outs = [jax.ShapeDtypeStruct((4,) + s, g.dtype) for s, g in zip(shapes, gs)]
    res = pl.pallas_call(
        body, name="rs_to_sibling", in_specs=[hbm] * n, out_specs=[hbm] * (2 * n), out_shape=outs + outs,
        scratch_shapes=[pltpu.VMEM(s, g.dtype) for s, g in zip(shapes, gs)]
        + [pltpu.SemaphoreType.DMA((n,)), pltpu.SemaphoreType.DMA((n, 4)), pltpu.SemaphoreType.DMA((n, 4))],
        compiler_params=_params(),
    )(*gs)
    return res[:n], res[n:]


def _rs_chips_side(parts):
    n = len(parts)

    def copies(p_refs, slots, scr):
        ssem, rsem = scr[n + 1], scr[n + 2]
        x, y, c = _pos()
        cps = []
        for o in range(n):
            for k in range(1, 4):
                px, py = _flip(x, (k >> 1) & 1), _flip(y, k & 1)
                cps.append(pltpu.make_async_remote_copy(
                    src_ref=p_refs[o].at[2 * px + py], dst_ref=slots[o].at[k], send_sem=ssem.at[o, k],
                    recv_sem=rsem.at[o, k], device_id=(px, py, c), device_id_type=MESH_ID))
        return cps

    def start(p_refs, slots, scr):
        for cp in copies(p_refs, slots, scr):
            cp.start()

    def finish(p_refs, slots, scr):
        x, y, _ = _pos()
        q = 2 * x + y
        for o in range(n):
            _bounce(p_refs[o].at[q], slots[o].at[0], scr[o], scr[n].at[o])
        for cp in copies(p_refs, slots, scr):
            cp.wait()

    return dict(
        ins=list(parts), outs=[jax.ShapeDtypeStruct(p.shape, p.dtype) for p in parts],
        scratch=[pltpu.VMEM(p.shape[1:], p.dtype) for p in parts]
        + [pltpu.SemaphoreType.DMA((n,)), pltpu.SemaphoreType.DMA((n, 4)), pltpu.SemaphoreType.DMA((n, 4))],
        start=start, finish=finish)


def _rs_share(rs, axes):
    n = len(rs)
    shapes = []
    for r, ax in zip(rs, axes):
        s = list(r.shape)
        s[ax] *= 2
        shapes.append(tuple(s))

    def body(*refs):
        r_refs, outs = refs[:n], refs[n:2 * n]
        bufs = refs[2 * n:3 * n]
        lsem, ssem, rsem = refs[3 * n:]
        x, y, c = _pos()
        cps = []
        for o in range(n):
            size = r_refs[o].shape[axes[o]]
            window = _chunk(outs[o], axes[o], c, size)
            rc = pltpu.make_async_remote_copy(src_ref=r_refs[o], dst_ref=window, send_sem=ssem.at[o], recv_sem=rsem.at[o],
                                              device_id=(x, y, 1 - c), device_id_type=MESH_ID)
            rc.start()
            cps.append(rc)
        for o in range(n):
            size = r_refs[o].shape[axes[o]]
            _bounce(r_refs[o], _chunk(outs[o], axes[o], c, size), bufs[o], lsem.at[o])
        for cp in cps:
            cp.wait()

    hbm = pl.BlockSpec(memory_space=pl.ANY)
    return pl.pallas_call(
        body, name="rs_share", in_specs=[hbm] * n, out_specs=[hbm] * n,
        out_shape=[jax.ShapeDtypeStruct(s, r.dtype) for s, r in zip(shapes, rs)],
        scratch_shapes=[pltpu.VMEM(r.shape, r.dtype) for r in rs] + [pltpu.SemaphoreType.DMA((n,)) for _ in range(3)],
        compiler_params=_params(),
    )(*rs)


def _rs_pair_sums(gs, axes):
    mine, landed = _rs_to_sibling(gs, axes)
    pair_sums = []
    for o, (mi, la) in enumerate(zip(mine, landed)):
        rows, cols = mi.shape[0] * mi.shape[1], mi.shape[2]
        s = _elementwise(lambda a, b: a.astype(F32) + b.astype(F32), [(mi.reshape(rows, cols), 0), (la.reshape(rows, cols), 0)],
                         [BF16], rows=rows, cols=cols, name=f"rs_pair_sum{o}")[0]
        pair_sums.append(s.reshape(mi.shape))
    return pair_sums


def _rs_finish(slots, axes):
    reduced = []
    for o, sl in enumerate(slots):
        rows, cols = sl.shape[1], sl.shape[2]
        flat = sl.reshape(4 * rows, cols)
        r = _elementwise(lambda a, b, c, d: (a.astype(F32) + b.astype(F32)) + (c.astype(F32) + d.astype(F32)),
                         [(flat, k * rows) for k in range(4)], [F32], rows=rows, cols=cols, name=f"rs_chip_sum{o}")[0]
        reduced.append(r)
    return _rs_share(reduced, axes)


def _norm_in(x, ctx, g, sc_l, sh_l, sc_c, sh_c, tr):
    L, D = x.shape
    T = ctx.shape[0]
    nx, nc = L // tr, T // tr

    def body(x_ref, c_ref, g_ref, scl, shl, scc, shc, o_ref):
        i = pl.program_id(0)

        def run(src, sc, sh):
            v = src[...]
            r = lax.rsqrt(jnp.mean(v * v, axis=-1, keepdims=True) + EPS)
            o_ref[...] = ((v * r * g_ref[...]) * (1.0 + sc[...]) + sh[...]).astype(o_ref.dtype)

        @pl.when(i < nx)
        def _():
            run(x_ref, scl, shl)

        @pl.when(i >= nx)
        def _():
            run(c_ref, scc, shc)

    vec = pl.BlockSpec((1, D), lambda i: (0, 0))
    return pl.pallas_call(
        body, name="norm_in", grid=(nx + nc,),
        in_specs=[pl.BlockSpec((tr, D), lambda i: (jnp.minimum(i, nx - 1), 0)),
                  pl.BlockSpec((tr, D), lambda i: (jnp.maximum(i - nx, 0), 0)), vec, vec, vec, vec, vec],
        out_specs=pl.BlockSpec((tr, D), lambda i: (i, 0)),
        out_shape=jax.ShapeDtypeStruct((L + T, D), BF16),
        compiler_params=_params(("arbitrary",)),
    )(x, ctx, g, sc_l, sh_l, sc_c, sh_c)


def _tmod(tl, row_w):
    assert row_w & (row_w - 1) == 0
    return lax.broadcasted_iota(jnp.int32, (tl, 1), 0) & (row_w - 1)


def _shift(z, k, tmod, row_w):
    tl = z.shape[0]
    rolled = pltpu.roll(z, k % tl, 0)
    mask = (tmod >= k) if k > 0 else (tmod < row_w + k)
    return jnp.where(mask, rolled, 0.0)


def _conv(z, w_ref, taps, left, tmod, row_w, lanes=slice(None)):
    out = None
    for j in range(taps):
        k = left - j
        term = (z if k == 0 else _shift(z, k, tmod, row_w)) * w_ref[j:j + 1, lanes]
        out = term if out is None else out + term
    return out


def _conv_bwd(dz, z, w_ref, taps, left, tmod, row_w, lanes=slice(None)):
    din = None
    dws = []
    for j in range(taps):
        k = left - j
        shifted = dz if k == 0 else _shift(dz, -k, tmod, row_w)
        term = shifted * w_ref[j:j + 1, lanes]
        din = term if din is None else din + term
        dws.append(jnp.sum(shifted * z, axis=0, keepdims=True))
    return din, dws


def _gate_matmul(xb16_ref, wd_ref, pre_scr, W, ng, gs):
    for g in range(ng):
        pg = jnp.dot(xb16_ref[:, g * gs:(g + 1) * gs], wd_ref[g], preferred_element_type=F32)
        pre_scr[:, g * gs:(g + 1) * gs] = pg[:, :gs]
        pre_scr[:, W + g * gs:W + (g + 1) * gs] = pg[:, gs:]


def _sub_loop(tl, sub_r, W, fn):
    def chunk(ci, carry):
        r0 = pl.multiple_of(ci * sub_r, sub_r)
        for lb in range(W // LANES):
            fn(r0, lb * LANES)
        return carry

    lax.fori_loop(0, tl // sub_r, chunk, 0)


def _row_loop(tl, rev, step, init):
    nchunk = tl // SUBLANES

    def chunk(j, carry):
        jj = (nchunk - 1 - j) if rev else j
        c0 = pl.multiple_of(jj * SUBLANES, SUBLANES)
        for r in (range(SUBLANES - 1, -1, -1) if rev else range(SUBLANES)):
            carry = step(c0 + r, carry)
        return carry

    return lax.fori_loop(0, nchunk, chunk, init)


def _mix_fwd(P, d, h_init, wts, *, rows, row_off, row_w, tl, h_other=None, name):
    W = P.shape[1] // 6
    nt = rows // tl
    ob = row_off // tl
    rev = d == 1
    gs = min(LRU_GROUP, W)
    ng = W // gs
    wca, wcb, bcb = wts["wca"], wts["wcb"], wts["bcb"]
    wd, ba, bx, lam = wts["wd"][d], wts["ba"][d], wts["bx"][d], wts["lam"][d]

    def tile(i):
        return (nt - 1 - i) if rev else i

    def pcol(j):
        return pl.BlockSpec((tl, W), lambda i: (tile(i) + ob, j))

    vec = pl.BlockSpec((1, W), lambda i: (0, 0))
    taps = pl.BlockSpec((SUBLANES, W), lambda i: (0, 0))
    wd_spec = pl.BlockSpec(wd.shape, lambda i: (0, 0, 0))
    seq = pl.BlockSpec((tl, W), lambda i: (tile(i), 0))

    sub_r = min(row_w, tl)
    assert tl % sub_r == 0

    def body(*refs):
        if rev:
            (bl, cl, ul, gl, vl, ql, ho, wca_r, wcb_r, bcb_r, wd_r, ba_r, bx_r, lam_r, hin, hseq, cat,
             a_scr, b_scr, pre_scr, carry, xb16_scr, sp_scr) = refs
        else:
            (vl, wcb_r, bcb_r, wd_r, ba_r, bx_r, lam_r, hin, hseq, a_scr, b_scr, pre_scr, carry, xb16_scr,
             sp_scr) = refs
        i = pl.program_id(0)

        @pl.when(i == 0)
        def _():
            carry[...] = hin[...]

        sp_scr[...] = _softplus(-lam_r[...])
        tmod = _tmod(sub_r, sub_r)

        def conv_in(r0, l0):
            rs, ls = pl.ds(r0, sub_r), pl.ds(l0, LANES)
            xb = _conv(vl[rs, ls], wcb_r, 4, 2, tmod, sub_r, ls) + bcb_r[:, ls]
            b_scr[rs, ls] = xb
            xb16_scr[rs, ls] = xb.astype(BF16)

        def gates(r0, l0):
            rs, ls = pl.ds(r0, sub_r), pl.ds(l0, LANES)
            _, ig, a, m2 = _gates(pre_scr[rs, ls] + ba_r[:, ls], pre_scr[rs, pl.ds(W + l0, LANES)] + bx_r[:, ls],
                                  sp_scr[:, ls])
            a_scr[rs, ls] = a
            m = jnp.where(m2 > 0.0, m2 * lax.rsqrt(m2), 0.0)
            b_scr[rs, ls] = m * (ig * b_scr[rs, ls])

        _sub_loop(tl, sub_r, W, conv_in)
        _gate_matmul(xb16_scr, wd_r, pre_scr, W, ng, gs)
        _sub_loop(tl, sub_r, W, gates)

        def step(t, h):
            h = a_scr[pl.ds(t, 1), :] * h + b_scr[pl.ds(t, 1), :]
            hseq[pl.ds(t, 1), :] = h
            return h

        carry[...] = _row_loop(tl, rev, step, carry[...])

        if rev:
            def mix_out(r0, l0):
                rs, ls = pl.ds(r0, sub_r), pl.ds(l0, LANES)
                yb = (ho[rs, ls] + hseq[rs, ls]) * _silu(ql[rs, ls])
                ya = bl[rs, ls] * _conv(cl[rs, ls] * ul[rs, ls], wca_r, 3, 1, tmod, sub_r, ls) * _silu(gl[rs, ls])
                cat[rs, ls] = ya.astype(cat.dtype)
                cat[rs, pl.ds(W + l0, LANES)] = yb.astype(cat.dtype)

            _sub_loop(tl, sub_r, W, mix_out)

    scratch = [pltpu.VMEM((tl, W), F32), pltpu.VMEM((tl, W), F32), pltpu.VMEM((tl, 2 * W), F32), pltpu.VMEM((1, W), F32),
               pltpu.VMEM((tl, W), BF16), pltpu.VMEM((1, W), F32)]
    if rev:
        in_specs = [pcol(j) for j in range(6)] + [seq, taps, taps, vec, wd_spec, vec, vec, vec, vec]
        args = [P] * 6 + [h_other, wca, wcb, bcb, wd, ba, bx, lam, h_init]
        out_specs = [seq, pl.BlockSpec((tl, 2 * W), lambda i: (tile(i), 0))]
        out_shape = [jax.ShapeDtypeStruct((rows, W), F32), jax.ShapeDtypeStruct((rows, 2 * W), BF16)]
    else:
        in_specs = [pcol(4), taps, vec, wd_spec, vec, vec, vec, vec]
        args = [P, wcb, bcb, wd, ba, bx, lam, h_init]
        out_specs = [seq]
        out_shape = [jax.ShapeDtypeStruct((rows, W), F32)]
    return pl.pallas_call(
        body, name=name, grid=(nt,), in_specs=in_specs, out_specs=out_specs, out_shape=out_shape,
        scratch_shapes=scratch, compiler_params=_params(("arbitrary",)),
    )(*args)


_BWD_SCRATCH = ("xb", "a", "dyl", "g", "r", "ig", "m2", "pre", "xb16", "dp16", "sp", "dlf", "edge", "c")


def _bwd_scratch(tl, W):
    f32 = {n: pltpu.VMEM((tl, W), F32) for n in ("xb", "a", "dyl", "g", "r", "ig", "m2")}
    other = {"pre": pltpu.VMEM((tl, 2 * W), F32), "xb16": pltpu.VMEM((tl, W), BF16), "dp16": pltpu.VMEM((tl, 2 * W), BF16),
             "sp": pltpu.VMEM((1, W), F32), "dlf": pltpu.VMEM((1, W), F32), "edge": pltpu.VMEM((1, W), F32),
             "c": pltpu.VMEM((1, W), F32)}
    return [{**f32, **other}[n] for n in _BWD_SCRATCH]


def _lru_bwd_tile(d, v_ref, dy_fn, hs_ref, w_refs, scr, acc, first, tl, sub_r, ng, gs):
    wcb_r, bcb_r, wd_r, ba_r, bx_r, lam_r = w_refs
    dwd_ref, dba_ref, dbx_ref, dlam_ref = acc
    W = v_ref.shape[1]
    assert gs % LANES == 0
    rev = d == 0
    tmod = _tmod(sub_r, sub_r)
    lam = lam_r[...]
    scr["sp"][...] = _softplus(-lam)
    scr["dlf"][...] = -_sigmoid(-lam)

    @pl.when(first)
    def _():
        dwd_ref[...] = jnp.zeros_like(dwd_ref)
        dba_ref[...] = jnp.zeros_like(dba_ref)
        dbx_ref[...] = jnp.zeros_like(dbx_ref)
        dlam_ref[...] = jnp.zeros_like(dlam_ref)

    def conv_in(r0, l0):
        rs, ls = pl.ds(r0, sub_r), pl.ds(l0, LANES)
        xb = _conv(v_ref[rs, ls], wcb_r, 4, 2, tmod, sub_r, ls) + bcb_r[:, ls]
        scr["xb"][rs, ls] = xb
        scr["xb16"][rs, ls] = xb.astype(BF16)
        scr["dyl"][rs, ls] = dy_fn(rs, ls)

    def gates(r0, l0):
        rs, ls = pl.ds(r0, sub_r), pl.ds(l0, LANES)
        r, ig, a, m2 = _gates(scr["pre"][rs, ls] + ba_r[:, ls], scr["pre"][rs, pl.ds(W + l0, LANES)] + bx_r[:, ls],
                              scr["sp"][:, ls])
        scr["a"][rs, ls] = a
        scr["r"][rs, ls] = r
        scr["ig"][rs, ls] = ig
        scr["m2"][rs, ls] = m2

    _sub_loop(tl, sub_r, W, conv_in)
    _gate_matmul(scr["xb16"], wd_r, scr["pre"], W, ng, gs)
    _sub_loop(tl, sub_r, W, gates)

    def step(t, c):
        g = scr["dyl"][pl.ds(t, 1), :] + c
        scr["g"][pl.ds(t, 1), :] = g
        return scr["a"][pl.ds(t, 1), :] * g

    scr["c"][...] = _row_loop(tl, rev, step, scr["c"][...])
    row = lax.broadcasted_iota(jnp.int32, (sub_r, 1), 0)

    def grads(r0, l0):
        rs, ls = pl.ds(r0, sub_r), pl.ds(l0, LANES)
        g, a, m2 = scr["g"][rs, ls], scr["a"][rs, ls], scr["m2"][rs, ls]
        r, ig, xb = scr["r"][rs, ls], scr["ig"][rs, ls], scr["xb"][rs, ls]
        h = hs_ref[rs, ls]
        if d == 0:
            e0 = pl.multiple_of(jnp.maximum(r0 - SUBLANES, 0), SUBLANES)
            edge = jnp.where(r0 == 0, scr["edge"][:, ls], hs_ref[pl.ds(e0, SUBLANES), ls][SUBLANES - 1:, :])
            hprev = jnp.where(row == 0, edge, pltpu.roll(h, 1, 0))
        else:
            e0 = pl.multiple_of(jnp.minimum(r0 + sub_r, tl - SUBLANES), SUBLANES)
            edge = jnp.where(r0 == tl - sub_r, scr["edge"][:, ls], hs_ref[pl.ds(e0, SUBLANES), ls][:1, :])
            hprev = jnp.where(row == sub_r - 1, edge, pltpu.roll(h, sub_r - 1, 0))
        rsq = lax.rsqrt(m2)
        gm = g * (m2 * rsq)
        d_la = (g * hprev) * a - (g * (ig * xb)) * ((1.0 - m2) * rsq)
        d_pr = d_la * ((-LRU_C) * scr["sp"][:, ls]) * (r * (1.0 - r))
        d_pi = (gm * xb) * (ig * (1.0 - ig))
        scr["dyl"][rs, ls] = gm * ig
        dlam_ref[:, ls] += jnp.sum(d_la * ((-LRU_C) * r), axis=0, keepdims=True) * scr["dlf"][:, ls]
        dba_ref[:, ls] += jnp.sum(d_pr, axis=0, keepdims=True)
        dbx_ref[:, ls] += jnp.sum(d_pi, axis=0, keepdims=True)
        gi, off = divmod(l0, gs)
        scr["dp16"][rs, pl.ds(gi * 2 * gs + off, LANES)] = d_pr.astype(BF16)
        scr["dp16"][rs, pl.ds(gi * 2 * gs + gs + off, LANES)] = d_pi.astype(BF16)

    _sub_loop(tl, sub_r, W, grads)
    for gi in range(ng):
        dp = scr["dp16"][:, gi * 2 * gs:(gi + 1) * 2 * gs]
        scr["g"][:, gi * gs:(gi + 1) * gs] = lax.dot_general(dp, wd_r[gi], (((1,), (1,)), ((), ())),
                                                             preferred_element_type=F32)
        dwd_ref[gi] += lax.dot_general(scr["xb16"][:, gi * gs:(gi + 1) * gs], dp, (((0,), (0,)), ((), ())),
                                       preferred_element_type=F32)


def _edge_block(h, tl, nt, d):
    W = h.shape[1]
    per = tl // SUBLANES
    if d == 0:
        return pl.BlockSpec((SUBLANES, W), lambda i: (jnp.maximum((nt - 1 - i) * per - 1, 0), 0))
    return pl.BlockSpec((SUBLANES, W), lambda i: (jnp.minimum((i + 1) * per, nt * per - 1), 0))


def _mix_bwd0(P, dcat, h0s, h_init, c_init, wts, *, rows, row_off, row_w, tl, name):
    W = P.shape[1] // 6
    nt = rows // tl
    ob = row_off // tl
    gs = min(LRU_GROUP, W)
    ng = W // gs
    wcb, bcb = wts["wcb"], wts["bcb"]
    wd, ba, bx, lam = wts["wd"][0], wts["ba"][0], wts["bx"][0], wts["lam"][0]

    def tile(i):
        return nt - 1 - i

    vec = pl.BlockSpec((1, W), lambda i: (0, 0))
    taps = pl.BlockSpec((SUBLANES, W), lambda i: (0, 0))
    wd_spec = pl.BlockSpec(wd.shape, lambda i: (0, 0, 0))
    seq = pl.BlockSpec((tl, W), lambda i: (tile(i), 0))

    sub_r = min(row_w, tl)
    assert tl % sub_r == 0

    def body(vl, ql, dyb, hs, hedge8, wcb_r, bcb_r, wd_r, ba_r, bx_r, lam_r, hin, cin,
             dxb_o, dwd_o, dba_o, dbx_o, dlam_o, cfin, *scratch):
        scr = dict(zip(_BWD_SCRATCH, scratch, strict=True))
        i = pl.program_id(0)

        @pl.when(i == 0)
        def _():
            scr["c"][...] = cin[...]

        scr["edge"][...] = jnp.where(i == nt - 1, hin[...], hedge8[SUBLANES - 1:SUBLANES, :])
        _lru_bwd_tile(0, vl, lambda rs, ls: dyb[rs, ls] * _silu(ql[rs, ls]), hs, (wcb_r, bcb_r, wd_r, ba_r, bx_r, lam_r),
                      scr, (dwd_o, dba_o, dbx_o, dlam_o), i == 0, tl, sub_r, ng, gs)
        dxb_o[...] = scr["dyl"][...] + scr["g"][...]
        cfin[...] = scr["c"][...]

    return pl.pallas_call(
        body, name=name, grid=(nt,),
        in_specs=[pl.BlockSpec((tl, W), lambda i: (tile(i) + ob, 4)), pl.BlockSpec((tl, W), lambda i: (tile(i) + ob, 5)),
                  pl.BlockSpec((tl, W), lambda i: (tile(i), 1)), seq, _edge_block(h0s, tl, nt, 0),
                  taps, vec, wd_spec, vec, vec, vec, vec, vec],
        out_specs=[seq, wd_spec, vec, vec, vec, vec],
        out_shape=[jax.ShapeDtypeStruct((rows, W), F32), jax.ShapeDtypeStruct(wd.shape, F32)]
        + [jax.ShapeDtypeStruct((1, W), F32)] * 4,
        scratch_shapes=_bwd_scratch(tl, W),
        compiler_params=_params(("arbitrary",)),
    )(P, P, dcat, h0s, h0s, wcb, bcb, wd, ba, bx, lam, h_init, c_init)


def _mix_bwd1(P, dcat, h0s, h1s, dxb0, h_init, c_init, wts, *, rows, row_off, row_w, tl, name):
    W = P.shape[1] // 6
    nt = rows // tl
    ob = row_off // tl
    gs = min(LRU_GROUP, W)
    ng = W // gs
    wca, wcb, bcb = wts["wca"], wts["wcb"], wts["bcb"]
    wd, ba, bx, lam = wts["wd"][1], wts["ba"][1], wts["bx"][1], wts["lam"][1]

    vec = pl.BlockSpec((1, W), lambda i: (0, 0))
    taps = pl.BlockSpec((SUBLANES, W), lambda i: (0, 0))
    wd_spec = pl.BlockSpec(wd.shape, lambda i: (0, 0, 0))
    seq = pl.BlockSpec((tl, W), lambda i: (i, 0))

    sub_r = min(row_w, tl)
    assert tl % sub_r == 0

    def body(bl, cl, ul, gl, vl, ql, dya, dyb, h0, h1, hedge8, dx0, wca_r, wcb_r, bcb_r, wd_r, ba_r, bx_r, lam_r, hin, cin,
             dp_o, dwd_o, dba_o, dbx_o, dlam_o, dwca_o, dwcb_o, dbcb_o, cfin, *scratch):
        scr = dict(zip(_BWD_SCRATCH, scratch, strict=True))
        i = pl.program_id(0)

        @pl.when(i == 0)
        def _():
            scr["c"][...] = cin[...]
            dwca_o[...] = jnp.zeros_like(dwca_o)
            dwcb_o[...] = jnp.zeros_like(dwcb_o)
            dbcb_o[...] = jnp.zeros_like(dbcb_o)

        scr["edge"][...] = jnp.where(i == nt - 1, hin[...], hedge8[0:1, :])
        _lru_bwd_tile(1, vl, lambda rs, ls: dyb[rs, ls] * _silu(ql[rs, ls]), h1, (wcb_r, bcb_r, wd_r, ba_r, bx_r, lam_r),
                      scr, (dwd_o, dba_o, dbx_o, dlam_o), i == 0, tl, sub_r, ng, gs)
        cfin[...] = scr["c"][...]
        tmod = _tmod(sub_r, sub_r)

        def rest(r0, l0):
            rs, ls = pl.ds(r0, sub_r), pl.ds(l0, LANES)
            dxb = dx0[rs, ls] + scr["dyl"][rs, ls] + scr["g"][rs, ls]
            dv, dwb = _conv_bwd(dxb, vl[rs, ls], wcb_r, 4, 2, tmod, sub_r, ls)
            for j in range(4):
                dwcb_o[j:j + 1, ls] += dwb[j]
            dbcb_o[:, ls] += jnp.sum(dxb, axis=0, keepdims=True)
            q = ql[rs, ls]
            sq = _sigmoid(q)
            dq = dyb[rs, ls] * (h0[rs, ls] + h1[rs, ls]) * (sq * (1.0 + q * (1.0 - sq)))
            b_, c_, u_, g_ = bl[rs, ls], cl[rs, ls], ul[rs, ls], gl[rs, ls]
            z = c_ * u_
            cz = _conv(z, wca_r, 3, 1, tmod, sub_r, ls)
            sgm = _sigmoid(g_)
            sg = g_ * sgm
            da = dya[rs, ls]
            dz, dwa = _conv_bwd(da * b_ * sg, z, wca_r, 3, 1, tmod, sub_r, ls)
            for j in range(3):
                dwca_o[j:j + 1, ls] += dwa[j]
            parts = (da * cz * sg, dz * u_, dz * c_, da * b_ * cz * (sgm * (1.0 + g_ * (1.0 - sgm))), dv, dq)
            for k, val in enumerate(parts):
                dp_o[rs, pl.ds(k * W + l0, LANES)] = val.astype(dp_o.dtype)

        _sub_loop(tl, sub_r, W, rest)

    def pcol(j):
        return pl.BlockSpec((tl, W), lambda i: (i + ob, j))

    return pl.pallas_call(
        body, name=name, grid=(nt,),
        in_specs=[pcol(j) for j in range(6)]
        + [pl.BlockSpec((tl, W), lambda i: (i, 0)), pl.BlockSpec((tl, W), lambda i: (i, 1)), seq, seq,
           _edge_block(h1s, tl, nt, 1), seq, taps, taps, vec, wd_spec, vec, vec, vec, vec, vec],
        out_specs=[pl.BlockSpec((tl, 6 * W), lambda i: (i, 0)), wd_spec, vec, vec, vec, taps, taps, vec, vec],
        out_shape=[jax.ShapeDtypeStruct((rows, 6 * W), BF16), jax.ShapeDtypeStruct(wd.shape, F32)]
        + [jax.ShapeDtypeStruct((1, W), F32)] * 3
        + [jax.ShapeDtypeStruct((SUBLANES, W), F32)] * 2 + [jax.ShapeDtypeStruct((1, W), F32)] * 2,
        scratch_shapes=_bwd_scratch(tl, W),
        compiler_params=_params(("arbitrary",)),
    )(*([P] * 6), dcat, dcat, h0s, h1s, h1s, dxb0, wca, wcb, bcb, wd, ba, bx, lam, h_init, c_init)


def _loss_rows(o, ins, vecs):
    (x, tgt), (gt_v, fg_v) = ins, vecs
    n = x + gt_v * o
    r = lax.rsqrt(jnp.mean(n * n, axis=-1, keepdims=True) + EPS)
    nr = n * r
    e = nr * fg_v - tgt
    dy = e * (1.0 / o.shape[-1])
    qv = dy * fg_v
    dn = r * (qv - nr * jnp.mean(qv * nr, axis=-1, keepdims=True))
    return [dn, dn * gt_v], [dy * nr, dn * o, e * e]


def _norm_bwd_rows(d, ins, vecs):
    (x, dn), (g_v, sc_v) = ins, vecs
    r = lax.rsqrt(jnp.mean(x * x, axis=-1, keepdims=True) + EPS)
    xr = x * r
    dxn = d * (1.0 + sc_v)
    qv = dxn * g_v
    gx = r * (qv - xr * jnp.mean(qv * xr, axis=-1, keepdims=True)) + dn
    return [gx], [d, d * (xr * g_v), dxn * xr]


def _norm_bwd_ctx(dhc, x, g, sc, tr):
    L, D = x.shape

    def body(d_ref, x_ref, g_ref, sc_ref, dsh_o, dsc_o, dg_o):
        i = pl.program_id(0)

        @pl.when(i == 0)
        def _():
            dsh_o[...] = jnp.zeros_like(dsh_o)
            dsc_o[...] = jnp.zeros_like(dsc_o)
            dg_o[...] = jnp.zeros_like(dg_o)

        d = d_ref[...]
        xv = x_ref[...]
        g_v = g_ref[...]
        r = lax.rsqrt(jnp.mean(xv * xv, axis=-1, keepdims=True) + EPS)
        xr = xv * r
        dsh_o[...] += jnp.sum(d, axis=0, keepdims=True)
        dsc_o[...] += jnp.sum(d * (xr * g_v), axis=0, keepdims=True)
        dxn = d * (1.0 + sc_ref[...])
        dg_o[...] += jnp.sum(dxn * xr, axis=0, keepdims=True)

    blk = pl.BlockSpec((tr, D), lambda i: (i, 0))
    vec = pl.BlockSpec((1, D), lambda i: (0, 0))
    return pl.pallas_call(
        body, name="norm_bwd_ctx", grid=(L // tr,), in_specs=[blk, blk, vec, vec], out_specs=[vec, vec, vec],
        out_shape=[jax.ShapeDtypeStruct((1, D), F32)] * 3, compiler_params=_params(("arbitrary",)),
    )(dhc, x, g, sc)


def _pack_blockdiag(wa, wx, gs):
    H, hd, _ = wa.shape
    hp = gs // hd
    ng = H // hp
    eye = jnp.eye(hp, dtype=wa.dtype)

    def bd(w):
        return jnp.einsum("gpij,pq->gpiqj", w.reshape(ng, hp, hd, hd), eye).reshape(ng, gs, gs)

    return jnp.concatenate([bd(wa), bd(wx)], axis=-1).astype(BF16)


def _unpack_blockdiag(dwd, H, hd, gs):
    hp = gs // hd
    ng = H // hp
    eye = jnp.eye(hp, dtype=dwd.dtype)

    def diag(dm):
        return jnp.einsum("gpiqj,pq->gpij", dm.reshape(ng, hp, hd, hp, hd), eye).reshape(H, hd, hd)

    return diag(dwd[:, :, :gs]), diag(dwd[:, :, gs:])


def kernel(x, c, ctx, c_ctx, norm_g, w_ada, b_ada, w_in, w_conv_a, w_conv_b, b_conv_b, lru_wa, lru_ba, lru_wx, lru_bx, lru_lambda, w_out, final_g, loss_target, m_c_ctx, m_norm_g, m_w_ada, m_b_ada, m_w_in, m_w_conv_a, m_w_conv_b, m_b_conv_b, m_lru_wa, m_lru_ba, m_lru_wx, m_lru_bx, m_lru_lambda, m_w_out, m_final_g, v_c_ctx, v_norm_g, v_w_ada, v_b_ada, v_w_in, v_w_conv_a, v_w_conv_b, v_b_conv_b, v_lru_wa, v_lru_ba, v_lru_wx, v_lru_bx, v_lru_lambda, v_w_out, v_final_g):
    xi, yi, ci = _pos()
    me = 4 * xi + 2 * yi + ci
    q = 2 * xi + yi
    first_core = (ci == 0).astype(F32)

    L, D = x.shape[1], x.shape[2]
    T = ctx.shape[1]
    W = D // 2
    Wq = W // 4
    H, hd = lru_wa.shape[2], lru_wa.shape[3]
    gs = min(LRU_GROUP, W)
    nq = w_ada.shape[2]
    tl = min(256, T, L)
    tr = min(256, T, L)
    x2, ctx2, tgt2 = x[0], ctx[0], loss_target[0]

    def place(shard, full_cols):
        z = jnp.zeros((shard.shape[0], full_cols), F32)
        return lax.dynamic_update_slice(z, shard * first_core, (0, q * shard.shape[1]))

    c_rows = lax.dynamic_update_slice(jnp.zeros((8, D), F32), c, (me, 0))
    small_in = [c_rows, place(w_conv_a[0], W), place(w_conv_b[0], W), place(lru_ba[0], W), place(lru_bx[0], W),
                place(lru_lambda[0], W)]
    small_shapes = [a.shape for a in small_in]
    gathered = _allreduce8(_pack(small_in, 8 * SUBLANES), "gather_small")
    c_all, wca, wcb, ba_all, bx_all, lam_all = _unpack(gathered, small_shapes)

    s_rows = jnp.concatenate([c_all, c_ctx[None, :], jnp.zeros((7, D), F32)], axis=0)
    mod_part = _matmul(s_rows, w_ada[0], a_act="silu", bias=lax.dynamic_slice(b_ada, (0, q * nq), (1, nq)),
                       tm=16, tn=nq, tk=512, name="ada_fwd")
    mod_all = _allreduce8(_pack([place(mod_part, 4 * nq)], 8 * SUBLANES), "gather_mod")
    mod_all = _unpack(mod_all, [(16, 4 * nq)])[0]
    mod_l = lax.dynamic_slice(mod_all, (me, 0), (1, 3 * D))
    mod_c = mod_all[8:9]
    sh_l, sc_l, gt_l = mod_l[:, :D], mod_l[:, D:2 * D], mod_l[:, 2 * D:]
    sh_c, sc_c = mod_c[:, :D], mod_c[:, D:2 * D]

    pad_taps = lambda w: jnp.pad(w, ((0, SUBLANES - w.shape[0]), (0, 0)))
    wts = {
        "wca": pad_taps(wca), "wcb": pad_taps(wcb), "bcb": b_conv_b,
        "wd": [_pack_blockdiag(lru_wa[0, d], lru_wx[0, d], gs) for d in range(2)],
        "ba": [ba_all[d:d + 1] for d in range(2)], "bx": [bx_all[d:d + 1] for d in range(2)],
        "lam": [lam_all[d:d + 1] for d in range(2)],
    }

    hl = _norm_in(x2, ctx2, norm_g, sc_l, sh_l, sc_c, sh_c, tr)
    p_lat, win_full, wout_full = _in_proj_gather(hl, w_in[0].astype(BF16), w_out[0].astype(BF16),
                                                 jnp.reshape(q, (1,)).astype(jnp.int32), rows=L, tm=min(1024, L))
    p_ctx = _matmul(hl, win_full, a_rows=T, a_off=L, tm=T, tn=1536, tk=D, name="in_proj_ctx")
    zero_w = jnp.zeros((1, W), F32)
    c0s = _mix_fwd(p_ctx, 0, zero_w, wts, rows=T, row_off=0, row_w=T, tl=tl, name="ctx_fwd0")[0]
    c1s, _ = _mix_fwd(p_ctx, 1, zero_w, wts, rows=T, row_off=0, row_w=T, tl=tl, h_other=c0s, name="ctx_fwd1")
    h0_init, h1_init = c0s[T - 1:T], c1s[0:1]
    h0s = _mix_fwd(p_lat, 0, h0_init, wts, rows=L, row_off=0, row_w=GRID_W, tl=tl, name="mix_fwd0")[0]
    h1s, cat = _mix_fwd(p_lat, 1, h1_init, wts, rows=L, row_off=0, row_w=GRID_W, tl=tl, h_other=h0s, name="mix_fwd1")
    (dn, dout), (dfg, dgt, sq_err) = _matmul_rows(
        cat, wout_full, dict(ins=[x2, tgt2], vecs=[gt_l, final_g[None, :]], outs=[F32, BF16], n_sum=3, fn=_loss_rows),
        tm=512, tk=2 * W, name="out_proj_loss")

    dcat = _matmul(dout, wout_full, tb=True, tm=512, tn=2 * W, tk=D, name="out_proj_bwd")
    gw_out = _matmul(cat, dout, ta=True, tm=1024, tn=D, tk=2048, out_dtype=BF16, name="w_out_grad")
    dxb0, dwd0, dba0, dbx0, dlam0, ch0 = _mix_bwd0(p_lat, dcat, h0s, h0_init, zero_w, wts, rows=L, row_off=0,
                                                   row_w=GRID_W, tl=tl, name="mix_bwd0")
    dp, dwd1, dba1, dbx1, dlam1, dwca, dwcb, dbcb, ch1 = _mix_bwd1(
        p_lat, dcat, h0s, h1s, dxb0, h1_init, zero_w, wts, rows=L, row_off=0, row_w=GRID_W, tl=tl, name="mix_bwd1")
    zero_cat = jnp.zeros((T, 2 * W), F32)
    cxb0, cwd0, cba0, cbx0, clam0, _ = _mix_bwd0(p_ctx, zero_cat, c0s, zero_w, ch0, wts, rows=T, row_off=0,
                                                 row_w=T, tl=tl, name="ctx_bwd0")
    dp_c, cwd1, cba1, cbx1, clam1, cwca, cwcb, cbcb, _ = _mix_bwd1(
        p_ctx, zero_cat, c0s, c1s, cxb0, zero_w, ch1, wts, rows=T, row_off=0, row_w=T, tl=tl, name="ctx_bwd1")

    gw_in_ctx = _matmul(hl, dp_c, ta=True, a_rows=T, a_off=L, tm=1024, tn=1536, tk=T, name="w_in_grad_ctx")
    gw_in = _matmul(hl, dp, ta=True, a_rows=L, tm=1024, tn=1536, tk=2048, init=gw_in_ctx, out_dtype=BF16,
                    name="w_in_grad")
    rs_axes = [1, 0]
    pair_sums = _rs_pair_sums([gw_in, gw_out], rs_axes)
    (gx,), (dsh_l, dsc_l, dng_l), rs_slots = _matmul_rows(
        dp, win_full, dict(ins=[x2, dn], vecs=[norm_g, sc_l], outs=[F32], n_sum=3, fn=_norm_bwd_rows), tb=True,
        tm=512, tk=1536, name="in_proj_bwd", side=_rs_chips_side(pair_sums))
    dhc = _matmul(dp_c, win_full, tb=True, tm=T, tn=D, tk=512, name="in_proj_bwd_ctx")
    dsh_c, dsc_c, dng_c = _norm_bwd_ctx(dhc, ctx2, norm_g, sc_c, tr)

    g_in_shard, g_out_shard = _rs_finish(rs_slots, rs_axes)

    dwa0, dwx0 = _unpack_blockdiag(dwd0 + cwd0, H, hd, gs)
    dwa1, dwx1 = _unpack_blockdiag(dwd1 + cwd1, H, hd, gs)
    zeros_d = jnp.zeros((1, D), F32)
    dmod_l = jnp.concatenate([dsh_l, dsc_l, dgt], axis=1)
    dmod_c = jnp.concatenate([dsh_c, dsc_c, zeros_d], axis=1)
    small_g = [
        lax.dynamic_update_slice(jnp.zeros((8, 3 * D), F32), dmod_l, (me, 0)), dmod_c,
        dfg, dng_l + dng_c, (dwca + cwca)[:3], (dwcb + cwcb)[:4], dbcb + cbcb,
        jnp.stack([dwa0, dwa1]), jnp.stack([dwx0, dwx1]),
        jnp.concatenate([dba0 + cba0, dba1 + cba1], axis=0), jnp.concatenate([dbx0 + cbx0, dbx1 + cbx1], axis=0),
        jnp.concatenate([dlam0 + clam0, dlam1 + clam1], axis=0),
    ]
    g_shapes = [a.shape for a in small_g]
    (g_rows, g_modc, g_fg, g_ng, g_wca, g_wcb, g_bcb, g_wa, g_wx, g_ba, g_bx, g_lam) = _unpack(
        _allreduce8(_pack(small_g, 8 * SUBLANES), "reduce_small"), g_shapes)

    g_mod = jnp.concatenate([g_rows, g_modc, jnp.zeros((7, 3 * D), F32)], axis=0)
    g_mod_q = lax.dynamic_slice(g_mod, (0, q * nq), (16, nq))
    g_w_ada = _matmul(s_rows, g_mod_q, ta=True, a_act="silu", tm=1024, tn=nq, tk=16, name="w_ada_grad")
    g_b_ada = jnp.sum(g_mod[:9], axis=0, keepdims=True)
    gc_part = _matmul(jnp.pad(lax.dynamic_slice(g_modc, (0, q * nq), (1, nq)), ((0, 7), (0, 0))), w_ada[0], tb=True,
                      dsilu_mul=c_ctx[None, :], tm=8, tn=D, tk=512, name="c_ctx_grad")
    g_c_ctx = _unpack(_allreduce8(_pack([gc_part[0:1] * first_core], 8 * SUBLANES), "reduce_c_ctx"), [(D,)])[0]

    def shard_cols(a, width):
        return lax.dynamic_slice(a, (0, q * width), (a.shape[0], width))

    grads = {
        "c_ctx": g_c_ctx, "norm_g": g_ng, "b_ada": g_b_ada,
        "w_conv_a": shard_cols(g_wca, Wq)[None], "w_conv_b": shard_cols(g_wcb, Wq)[None], "b_conv_b": g_bcb,
        "lru_wa": g_wa[None], "lru_ba": shard_cols(g_ba, Wq)[None], "lru_wx": g_wx[None],
        "lru_bx": shard_cols(g_bx, Wq)[None], "lru_lambda": shard_cols(g_lam, Wq)[None], "final_g": g_fg[0],
    }
    small_names = list(grads)
    given = dict(c_ctx=(c_ctx, m_c_ctx, v_c_ctx), norm_g=(norm_g, m_norm_g, v_norm_g), b_ada=(b_ada, m_b_ada, v_b_ada),
                 w_conv_a=(w_conv_a, m_w_conv_a, v_w_conv_a), w_conv_b=(w_conv_b, m_w_conv_b, v_w_conv_b),
                 b_conv_b=(b_conv_b, m_b_conv_b, v_b_conv_b), lru_wa=(lru_wa, m_lru_wa, v_lru_wa),
                 lru_ba=(lru_ba, m_lru_ba, v_lru_ba), lru_wx=(lru_wx, m_lru_wx, v_lru_wx),
                 lru_bx=(lru_bx, m_lru_bx, v_lru_bx), lru_lambda=(lru_lambda, m_lru_lambda, v_lru_lambda),
                 final_g=(final_g, m_final_g, v_final_g))
    shapes = [given[n][0].shape for n in small_names]
    packed = [_pack([given[n][j] for n in small_names], 2 * SUBLANES) for j in range(3)]
    packed_g = _pack([grads[n] for n in small_names], 2 * SUBLANES)
    sd, sm, sv = _adam(packed[0], packed_g, packed[1], packed[2], "adam_small")
    delta_s = dict(zip(small_names, _unpack(sd, shapes)))
    newm_s = dict(zip(small_names, _unpack(sm, shapes)))
    newv_s = dict(zip(small_names, _unpack(sv, shapes)))
    grads = {n: grads[n].reshape(given[n][0].shape) for n in small_names}

    big = {"w_ada": (w_ada, g_w_ada, m_w_ada, v_w_ada), "w_in": (w_in, g_in_shard, m_w_in, v_w_in),
           "w_out": (w_out, g_out_shard, m_w_out, v_w_out)}
    delta_b, newm_b, newv_b = {}, {}, {}
    for n, (w, g, m, v) in big.items():
        grads[n] = g[None]
        d_, m_, v_ = _adam(w[0], g, m[0], v[0], "adam_" + n)
        delta_b[n], newm_b[n], newv_b[n] = d_[None], m_[None], v_[None]

    loss = lax.psum((0.5 / D) * jnp.sum(sq_err), AXES)
    order = ["c_ctx", "norm_g", "w_ada", "b_ada", "w_in", "w_conv_a", "w_conv_b", "b_conv_b", "lru_wa", "lru_ba",
             "lru_wx", "lru_bx", "lru_lambda", "w_out", "final_g"]
    delta = {**delta_s, **delta_b}
    newm = {**newm_s, **newm_b}
    newv = {**newv_s, **newv_b}
    return (loss, gx[None], *[grads[n] for n in order], *[delta[n] for n in order], *[newm[n] for n in order],
            *[newv[n] for n in order])
```

```python
import functools

import jax
import jax.numpy as jnp
from jax import lax
from jax.experimental import pallas as pl
from jax.experimental.pallas import tpu as pltpu

F32 = jnp.float32
BF16 = jnp.bfloat16
MESH_ID = pl.DeviceIdType.MESH
AXES = ("x", "y", "c")

EPS = 1e-6
LRU_C = 8.0
GRID_W = 64
ADAM_LR = 0.001
ADAM_B1 = 0.9
ADAM_B2 = 0.999
ADAM_EPS = 1e-08
ADAM_WD = 0.01
ADAM_STEP = 10

LANES = 128
SUBLANES = 8
PACK_COLS = 1024
VMEM_LIMIT = 56 * 2**20
LRU_GROUP = 256


def _params(sem=None):
    return pltpu.CompilerParams(vmem_limit_bytes=VMEM_LIMIT, dimension_semantics=sem)


def _pick(dim, pref, quantum=LANES):
    if dim <= pref:
        return dim
    best = None
    for t in range(quantum, pref + 1, quantum):
        if dim % t == 0:
            best = t
    assert best is not None, (dim, pref)
    return best


def _pos():
    return lax.axis_index("x"), lax.axis_index("y"), lax.axis_index("c")


def _flip(v, bit):
    return 1 - v if bit else v


def _sigmoid(v):
    return 0.5 * jnp.tanh(0.5 * v) + 0.5


def _silu(v):
    return v * _sigmoid(v)


def _dsilu(v):
    s = _sigmoid(v)
    return s * (1.0 + v * (1.0 - s))


def _gates(pre_r, pre_i, sp):
    r = _sigmoid(pre_r)
    ig = _sigmoid(pre_i)
    e = LRU_C * r * sp
    w = jnp.tanh(e)
    return r, ig, jnp.exp(-e), (2.0 * w) * pl.reciprocal(1.0 + w, approx=True)


def _softplus(z):
    return jnp.maximum(z, 0.0) + jnp.log1p(jnp.exp(-jnp.abs(z)))


def _matmul(a, b, *, ta=False, tb=False, tm=512, tn=512, tk=512, out_dtype=F32, name,
            a_rows=None, a_off=0, a_act=None, init=None, bias=None, dsilu_mul=None, side=None):
    rows_a = a.shape[0] if a_rows is None else a_rows
    if ta:
        K, M = rows_a, a.shape[1]
    else:
        M, K = rows_a, a.shape[1]
    N = b.shape[0] if tb else b.shape[1]
    tm, tn, tk = _pick(M, tm, SUBLANES), _pick(N, tn), _pick(K, tk)
    t_rows = tk if ta else tm
    assert a_off % t_rows == 0
    nk = K // tk
    gi, gj = M // tm, N // tn
    off_blocks = a_off // t_rows
    dims = (((0 if ta else 1,), (1 if tb else 0,)), ((), ()))
    extras = [e for e in (init, bias, dsilu_mul) if e is not None]
    n_sin = len(side["ins"]) if side else 0
    n_sout = len(side["outs"]) if side else 0

    def body(a_ref, b_ref, *rest):
        rest = list(rest)
        init_ref = rest.pop(0) if init is not None else None
        bias_ref = rest.pop(0) if bias is not None else None
        dsm_ref = rest.pop(0) if dsilu_mul is not None else None
        side_in = [rest.pop(0) for _ in range(n_sin)]
        o_ref = rest.pop(0)
        side_out = [rest.pop(0) for _ in range(n_sout)]
        acc_ref = rest.pop(0) if nk > 1 else None
        side_scr = rest
        i, j, k = pl.program_id(0), pl.program_id(1), pl.program_id(2)

        if side:
            @pl.when((i == 0) & (j == 0) & (k == 0))
            def _():
                side["start"](side_in, side_out, side_scr)

        av = a_ref[...]
        if a_act == "silu":
            av = _silu(av)
        prod = lax.dot_general(av, b_ref[...], dims, preferred_element_type=F32)

        def finish(r):
            if bias_ref is not None:
                r = r + bias_ref[...]
            if dsm_ref is not None:
                r = r * _dsilu(dsm_ref[...])
            o_ref[...] = r.astype(o_ref.dtype)

        if nk == 1:
            finish(prod if init_ref is None else prod + init_ref[...])
        else:
            @pl.when(k == 0)
            def _():
                acc_ref[...] = prod if init_ref is None else prod + init_ref[...]

            @pl.when(k > 0)
            def _():
                acc_ref[...] += prod

            @pl.when(k == nk - 1)
            def _():
                finish(acc_ref[...])

        if side:
            @pl.when((i == gi - 1) & (j == gj - 1) & (k == nk - 1))
            def _():
                side["finish"](side_in, side_out, side_scr)

    if ta:
        a_spec = pl.BlockSpec((tk, tm), lambda i, j, k: (k + off_blocks, i))
    else:
        a_spec = pl.BlockSpec((tm, tk), lambda i, j, k: (i + off_blocks, k))
    if tb:
        b_spec = pl.BlockSpec((tn, tk), lambda i, j, k: (j, k))
    else:
        b_spec = pl.BlockSpec((tk, tn), lambda i, j, k: (k, j))
    in_specs = [a_spec, b_spec]
    if init is not None:
        in_specs.append(pl.BlockSpec((tm, tn), lambda i, j, k: (i, j)))
    if bias is not None:
        in_specs.append(pl.BlockSpec((1, tn), lambda i, j, k: (0, j)))
    if dsilu_mul is not None:
        in_specs.append(pl.BlockSpec((1, tn), lambda i, j, k: (0, j)))
    hbm = pl.BlockSpec(memory_space=pl.ANY)
    res = pl.pallas_call(
        body, name=name, grid=(gi, gj, nk),
        in_specs=in_specs + [hbm] * n_sin,
        out_specs=[pl.BlockSpec((tm, tn), lambda i, j, k: (i, j))] + [hbm] * n_sout,
        out_shape=[jax.ShapeDtypeStruct((M, N), out_dtype)] + (list(side["outs"]) if side else []),
        scratch_shapes=([pltpu.VMEM((tm, tn), F32)] if nk > 1 else []) + (list(side["scratch"]) if side else []),
        compiler_params=_params(("arbitrary",) * 3 if side else ("parallel", "parallel", "arbitrary")),
    )(a, b, *extras, *(side["ins"] if side else []))
    return (res[0], res[1:]) if side else res[0]


def _elementwise(fn, ins, outs, *, rows, cols, name, tr=256):
    tr = _pick(rows, tr, 2 * SUBLANES)
    n_in = len(ins)

    def body(*refs):
        vals = fn(*[r[...] for r in refs[:n_in]])
        if not isinstance(vals, (tuple, list)):
            vals = (vals,)
        for r, v in zip(refs[n_in:], vals, strict=True):
            r[...] = v.astype(r.dtype)

    def spec(off):
        assert off % tr == 0
        ob = off // tr
        return pl.BlockSpec((tr, cols), lambda i: (i + ob, 0))

    res = pl.pallas_call(
        body, name=name, grid=(rows // tr,),
        in_specs=[spec(off) for _, off in ins],
        out_specs=[spec(0) for _ in outs],
        out_shape=[jax.ShapeDtypeStruct((rows, cols), dt) for dt in outs],
        compiler_params=_params(("parallel",)),
    )(*[a for a, _ in ins])
    return res


def _adam_math(w, g, m, v):
    m = ADAM_B1 * m + (1.0 - ADAM_B1) * g
    v = ADAM_B2 * v + (1.0 - ADAM_B2) * (g * g)
    m_hat = m / (1.0 - ADAM_B1 ** ADAM_STEP)
    v_hat = v / (1.0 - ADAM_B2 ** ADAM_STEP)
    delta = -ADAM_LR * (m_hat / (jnp.sqrt(v_hat) + ADAM_EPS) + ADAM_WD * w)
    return delta, m, v


def _adam(w, g, m, v, name):
    rows, cols = w.shape
    return _elementwise(_adam_math, [(w, 0), (g, 0), (m, 0), (v, 0)], [F32, F32, F32],
                        rows=rows, cols=cols, name=name)


def _pack(arrs, row_quantum):
    flat = jnp.concatenate([a.reshape(-1).astype(F32) for a in arrs])
    n = flat.shape[0]
    q = row_quantum * PACK_COLS
    total = -(-n // q) * q
    flat = jnp.pad(flat, (0, total - n))
    return flat.reshape(total // PACK_COLS, PACK_COLS)


def _unpack(buf, shapes):
    flat = buf.reshape(-1)
    out, off = [], 0
    for s in shapes:
        n = 1
        for d in s:
            n *= d
        out.append(flat[off:off + n].reshape(s))
        off += n
    return out


def _allreduce8(buf, name):
    R, C = buf.shape
    assert R % (8 * SUBLANES) == 0
    m = R // 8

    def body(x_ref, o_ref, recv, red, s1, r1, s2, r2):
        x, y, c = _pos()
        me = 4 * x + 2 * y + c

        def peer(k):
            px, py, pc = _flip(x, (k >> 2) & 1), _flip(y, (k >> 1) & 1), _flip(c, k & 1)
            return (px, py, pc), 4 * px + 2 * py + pc

        def rows(ref, idx):
            return ref.at[pl.ds(pl.multiple_of(idx * m, SUBLANES), m), :]

        def scatter(k):
            dev, p = peer(k)
            return pltpu.make_async_remote_copy(src_ref=rows(x_ref, p), dst_ref=recv.at[k], send_sem=s1.at[k],
                                                recv_sem=r1.at[k], device_id=dev, device_id_type=MESH_ID)

        def share(k):
            dev, p = peer(k)
            return pltpu.make_async_remote_copy(src_ref=red, dst_ref=rows(o_ref, me), send_sem=s2.at[k],
                                                recv_sem=r2.at[k], device_id=dev, device_id_type=MESH_ID)

        def shared_from(k):
            dev, p = peer(k)
            return pltpu.make_async_remote_copy(src_ref=red, dst_ref=rows(o_ref, p), send_sem=s2.at[k],
                                                recv_sem=r2.at[k], device_id=dev, device_id_type=MESH_ID)

        for k in range(1, 8):
            scatter(k).start()
        acc = rows(x_ref, me)[...]
        for k in range(1, 8):
            scatter(k).wait_recv()
            acc = acc + recv[k]
        red[...] = acc
        rows(o_ref, me)[...] = acc
        for k in range(1, 8):
            share(k).start()
        for k in range(1, 8):
            shared_from(k).wait_recv()
        for k in range(1, 8):
            scatter(k).wait_send()
            share(k).wait_send()

    return pl.pallas_call(
        body, name=name,
        in_specs=[pl.BlockSpec(memory_space=pltpu.VMEM)],
        out_specs=pl.BlockSpec(memory_space=pltpu.VMEM),
        out_shape=jax.ShapeDtypeStruct((R, C), F32),
        scratch_shapes=[pltpu.VMEM((8, m, C), F32), pltpu.VMEM((m, C), F32),
                        pltpu.SemaphoreType.DMA((8,)), pltpu.SemaphoreType.DMA((8,)),
                        pltpu.SemaphoreType.DMA((8,)), pltpu.SemaphoreType.DMA((8,))],
        compiler_params=_params(),
    )(buf)


def _bounce(src, dst, buf, sem):
    cin = pltpu.make_async_copy(src, buf, sem)
    cin.start()
    cin.wait()
    cout = pltpu.make_async_copy(buf, dst, sem)
    cout.start()
    cout.wait()


def _chunk(ref, axis, idx, size):
    start = idx * size
    if axis == 0:
        return ref.at[pl.ds(start, size), :]
    return ref.at[:, pl.ds(start, size)]


def _in_proj_gather(hl, win, wout, q_arr, *, rows, tm):
    D, nq = win.shape
    dq, D2 = wout.shape
    ni = rows // tm
    ops = ((0, 1, nq, D // 2), (1, 0, dq, dq // 2))

    def body(q_ref, a_ref, win_ref, wout_ref, p_ref, gin_ref, gout_ref, b_scr, buf_out, lsem, ssem, rsem, fsem, gsem):
        j, i = pl.program_id(0), pl.program_id(1)
        x, y, c = _pos()
        q = 2 * x + y
        srcs = (win_ref, wout_ref)
        dsts = (gin_ref, gout_ref)

        def shard_window(o, chip):
            _, axis, size, _ = ops[o]
            return _chunk(dsts[o], axis, chip, size)

        def half(ref, o, core):
            return ref.at[pl.ds(core * ops[o][3], ops[o][3]), :]

        def half_window(o, chip, core):
            _, axis, size, hs = ops[o]
            if axis == 1:
                return dsts[o].at[pl.ds(core * hs, hs), pl.ds(chip * size, size)]
            return dsts[o].at[pl.ds(chip * size + core * hs, hs), :]

        def chip_of(k):
            px, py = _flip(x, (k >> 1) & 1), _flip(y, k & 1)
            return px, py, 2 * px + py

        def send(o, k):
            px, py, _ = chip_of(k)
            return pltpu.make_async_remote_copy(
                src_ref=half(srcs[o], o, c), dst_ref=half_window(o, q, c), send_sem=ssem.at[o, k],
                recv_sem=rsem.at[o, k], device_id=(px, py, c), device_id_type=MESH_ID)

        def chip_recv(o, k):
            px, py, pq = chip_of(k)
            landed = half_window(o, pq, c)
            pltpu.make_async_remote_copy(src_ref=landed, dst_ref=landed, send_sem=ssem.at[o, k], recv_sem=rsem.at[o, k],
                                         device_id=(px, py, c), device_id_type=MESH_ID).wait_recv()

        def to_sibling(o, k):
            landed = half_window(o, chip_of(k)[2], c)
            return pltpu.make_async_remote_copy(src_ref=landed, dst_ref=landed, send_sem=fsem.at[o, k],
                                                recv_sem=gsem.at[o, k], device_id=(x, y, 1 - c), device_id_type=MESH_ID)

        def from_sibling(o, k):
            theirs = half_window(o, chip_of(k)[2], 1 - c)
            pltpu.make_async_remote_copy(src_ref=theirs, dst_ref=theirs, send_sem=fsem.at[o, k], recv_sem=gsem.at[o, k],
                                         device_id=(x, y, 1 - c), device_id_type=MESH_ID).wait_recv()

        def relay(o, core):
            if core == 0:
                landed, target = half_window(o, chip_of(2)[2], 0), (x, 1 - y, 0)
            else:
                landed, target = half_window(o, chip_of(1)[2], 1), (1 - x, y, 1)
            return pltpu.make_async_remote_copy(src_ref=landed, dst_ref=landed, send_sem=ssem.at[o, 3],
                                                recv_sem=rsem.at[o, 3], device_id=target, device_id_type=MESH_ID)

        def on_core(core, fn):
            @pl.when(c == core)
            def _():
                fn()

        def land(o, k):
            chip_recv(o, k)
            if k == 2:
                on_core(0, lambda: relay(o, 0).start())
            if k == 1:
                on_core(1, lambda: relay(o, 1).start())
            to_sibling(o, k).start()

        def settle(o, k):
            from_sibling(o, k)
            to_sibling(o, k).wait_send()

        def b_load(k, slot):
            src = win_ref if k == 0 else shard_window(0, chip_of(k)[2])
            return pltpu.make_async_copy(src, b_scr.at[slot], lsem.at[0])

        def own_store():
            return pltpu.make_async_copy(b_scr.at[0], shard_window(0, q), lsem.at[2])

        order = (0, 2, 1, 3)
        early = max(ni - 2, 0)

        @pl.when((j == 0) & (i == 0))
        def _():
            for o in range(2):
                for k in (2, 1):
                    send(o, k).start()
            first = b_load(0, 0)
            first.start()
            first.wait()
            own_store().start()
            _bounce(wout_ref, shard_window(1, q), buf_out, lsem.at[1])

        for jj in range(3):
            nxt = order[jj + 1]

            @pl.when((j == jj) & (i == early))
            def _(nxt=nxt):
                land(0, nxt)

            @pl.when((j == jj) & (i == ni - 1))
            def _(jj=jj, nxt=nxt):
                settle(0, nxt)
                if jj == 1:
                    own_store().wait()
                b_load(nxt, (jj + 1) % 2).start()

            @pl.when((j == jj + 1) & (i == 0))
            def _(jj=jj, nxt=nxt):
                b_load(nxt, (jj + 1) % 2).wait()

        @pl.when((j == 3) & (i == 0))
        def _():
            land(1, 2)
            land(1, 1)

        p_ref[...] = jnp.dot(a_ref[...], b_scr[j % 2], preferred_element_type=F32)

        @pl.when((j == 3) & (i == ni - 1))
        def _():
            settle(1, 2)
            settle(1, 1)
            land(1, 3)
            settle(1, 3)
            for o in range(2):
                for k in (2, 1):
                    send(o, k).wait_send()
                for core in range(2):
                    on_core(core, lambda o=o, core=core: relay(o, core).wait_send())

    hbm = pl.BlockSpec(memory_space=pl.ANY)
    grid_spec = pltpu.PrefetchScalarGridSpec(
        num_scalar_prefetch=1, grid=(4, ni),
        in_specs=[pl.BlockSpec((tm, D), lambda j, i, qr: (i, 0)), hbm, hbm],
        out_specs=[pl.BlockSpec((tm, nq), lambda j, i, qr: (i, jnp.bitwise_xor(qr[0], ((j & 1) << 1) | (j >> 1)))),
                   hbm, hbm],
        scratch_shapes=[pltpu.VMEM((2,) + win.shape, win.dtype), pltpu.VMEM(wout.shape, wout.dtype), pltpu.SemaphoreType.DMA((3,))]
        + [pltpu.SemaphoreType.DMA((2, 4)) for _ in range(4)])
    return pl.pallas_call(
        body, name="in_proj_gather", grid_spec=grid_spec,
        out_shape=[jax.ShapeDtypeStruct((rows, 4 * nq), F32), jax.ShapeDtypeStruct((D, 4 * nq), win.dtype),
                   jax.ShapeDtypeStruct((4 * dq, D2), wout.dtype)],
        compiler_params=_params(("arbitrary", "arbitrary")),
    )(q_arr, hl, win, wout)


def _rs_to_sibling(gs, axes):
    n = len(gs)
    shapes = []
    for g, ax in zip(gs, axes):
        s = list(g.shape)
        s[ax] //= 8
        shapes.append(tuple(s))

    def body(*refs):
        g_refs, mine, landed = refs[:n], refs[n:2 * n], refs[2 * n:3 * n]
        bufs = refs[3 * n:4 * n]
        lsem, ssem, rsem = refs[4 * n:]
        x, y, c = _pos()
        cps = []
        for o in range(n):
            size = shapes[o][axes[o]]
            for j in range(4):
                rc = pltpu.make_async_remote_copy(
                    src_ref=_chunk(g_refs[o], axes[o], 2 * j + 1 - c, size), dst_ref=landed[o].at[j],
                    send_sem=ssem.at[o, j], recv_sem=rsem.at[o, j], device_id=(x, y, 1 - c), device_id_type=MESH_ID)
                rc.start()
                cps.append(rc)
        for o in range(n):
            size = shapes[o][axes[o]]
            for j in range(4):
                _bounce(_chunk(g_refs[o], axes[o], 2 * j + c, size), mine[o].at[j], bufs[o], lsem.at[o])
        for rc in cps:
            rc.wait()

    hbm = pl.BlockSpec(memory_space=pl.ANY)
    outs = [jax.ShapeDtypeStruct((4,) + s, g.dtype) for s, g in zip(shapes, gs)]
    res = pl.pallas_call(
        body, name="rs_to_sibling", in_specs=[hbm] * n, out_specs=[hbm] * (2 * n), out_shape=outs + outs,
        scratch_shapes=[pltpu.VMEM(s, g.dtype) for s, g in zip(shapes, gs)]
        + [pltpu.SemaphoreType.DMA((n,)), pltpu.SemaphoreType.DMA((n, 4)), pltpu.SemaphoreType.DMA((n, 4))],
        compiler_params=_params(),
    )(*gs)
    return res[:n], res[n:]


def _rs_chips_side(parts):
    n = len(parts)

    def copies(p_refs, slots, scr):
        ssem, rsem = scr[n + 1], scr[n + 2]
        x, y, c = _pos()
        cps = []
        for o in range(n):
            for k in range(1, 4):
                px, py = _flip(x, (k >> 1) & 1), _flip(y, k & 1)
                cps.append(pltpu.make_async_remote_copy(
                    src_ref=p_refs[o].at[2 * px + py], dst_ref=slots[o].at[k], send_sem=ssem.at[o, k],
                    recv_sem=rsem.at[o, k], device_id=(px, py, c), device_id_type=MESH_ID))
        return cps

    def start(p_refs, slots, scr):
        for cp in copies(p_refs, slots, scr):
            cp.start()

    def finish(p_refs, slots, scr):
        x, y, _ = _pos()
        q = 2 * x + y
        for o in range(n):
            _bounce(p_refs[o].at[q], slots[o].at[0], scr[o], scr[n].at[o])
        for cp in copies(p_refs, slots, scr):
            cp.wait()

    return dict(
        ins=list(parts), outs=[jax.ShapeDtypeStruct(p.shape, p.dtype) for p in parts],
        scratch=[pltpu.VMEM(p.shape[1:], p.dtype) for p in parts]
        + [pltpu.SemaphoreType.DMA((n,)), pltpu.SemaphoreType.DMA((n, 4)), pltpu.SemaphoreType.DMA((n, 4))],
        start=start, finish=finish)


def _rs_share(rs, axes):
    n = len(rs)
    shapes = []
    for r, ax in zip(rs, axes):
        s = list(r.shape)
        s[ax] *= 2
        shapes.append(tuple(s))

    def body(*refs):
        r_refs, outs = refs[:n], refs[n:2 * n]
        bufs = refs[2 * n:3 * n]
        lsem, ssem, rsem = refs[3 * n:]
        x, y, c = _pos()
        cps = []
        for o in range(n):
            size = r_refs[o].shape[axes[o]]
            window = _chunk(outs[o], axes[o], c, size)
            rc = pltpu.make_async_remote_copy(src_ref=r_refs[o], dst_ref=window, send_sem=ssem.at[o], recv_sem=rsem.at[o],
                                              device_id=(x, y, 1 - c), device_id_type=MESH_ID)
            rc.start()
            cps.append(rc)
        for o in range(n):
            size = r_refs[o].shape[axes[o]]
            _bounce(r_refs[o], _chunk(outs[o], axes[o], c, size), bufs[o], lsem.at[o])
        for cp in cps:
            cp.wait()

    hbm = pl.BlockSpec(memory_space=pl.ANY)
    return pl.pallas_call(
        body, name="rs_share", in_specs=[hbm] * n, out_specs=[hbm] * n,
        out_shape=[jax.ShapeDtypeStruct(s, r.dtype) for s, r in zip(shapes, rs)],
        scratch_shapes=[pltpu.VMEM(r.shape, r.dtype) for r in rs] + [pltpu.SemaphoreType.DMA((n,)) for _ in range(3)],
        compiler_params=_params(),
    )(*rs)


def _rs_pair_sums(gs, axes):
    mine, landed = _rs_to_sibling(gs, axes)
    pair_sums = []
    for o, (mi, la) in enumerate(zip(mine, landed)):
        rows, cols = mi.shape[0] * mi.shape[1], mi.shape[2]
        s = _elementwise(lambda a, b: a.astype(F32) + b.astype(F32), [(mi.reshape(rows, cols), 0), (la.reshape(rows, cols), 0)],
                         [BF16], rows=rows, cols=cols, name=f"rs_pair_sum{o}")[0]
        pair_sums.append(s.reshape(mi.shape))
    return pair_sums


def _rs_finish(slots, axes):
    reduced = []
    for o, sl in enumerate(slots):
        rows, cols = sl.shape[1], sl.shape[2]
        flat = sl.reshape(4 * rows, cols)
        r = _elementwise(lambda a, b, c, d: (a.astype(F32) + b.astype(F32)) + (c.astype(F32) + d.astype(F32)),
                         [(flat, k * rows) for k in range(4)], [F32], rows=rows, cols=cols, name=f"rs_chip_sum{o}")[0]
        reduced.append(r)
    return _rs_share(reduced, axes)


def _norm_in(x, ctx, g, sc_l, sh_l, sc_c, sh_c, tr):
    L, D = x.shape
    T = ctx.shape[0]
    nx, nc = L // tr, T // tr

    def body(x_ref, c_ref, g_ref, scl, shl, scc, shc, o_ref):
        i = pl.program_id(0)

        def run(src, sc, sh):
            v = src[...]
            r = lax.rsqrt(jnp.mean(v * v, axis=-1, keepdims=True) + EPS)
            o_ref[...] = ((v * r * g_ref[...]) * (1.0 + sc[...]) + sh[...]).astype(o_ref.dtype)

        @pl.when(i < nx)
        def _():
            run(x_ref, scl, shl)

        @pl.when(i >= nx)
        def _():
            run(c_ref, scc, shc)

    vec = pl.BlockSpec((1, D), lambda i: (0, 0))
    return pl.pallas_call(
        body, name="norm_in", grid=(nx + nc,),
        in_specs=[pl.BlockSpec((tr, D), lambda i: (jnp.minimum(i, nx - 1), 0)),
                  pl.BlockSpec((tr, D), lambda i: (jnp.maximum(i - nx, 0), 0)), vec, vec, vec, vec, vec],
        out_specs=pl.BlockSpec((tr, D), lambda i: (i, 0)),
        out_shape=jax.ShapeDtypeStruct((L + T, D), BF16),
        compiler_params=_params(("arbitrary",)),
    )(x, ctx, g, sc_l, sh_l, sc_c, sh_c)


def _tmod(tl, row_w):
    assert row_w & (row_w - 1) == 0
    return lax.broadcasted_iota(jnp.int32, (tl, 1), 0) & (row_w - 1)


def _shift(z, k, tmod, row_w):
    tl = z.shape[0]
    rolled = pltpu.roll(z, k % tl, 0)
    mask = (tmod >= k) if k > 0 else (tmod < row_w + k)
    return jnp.where(mask, rolled, 0.0)


def _conv(z, w_ref, taps, left, tmod, row_w, lanes=slice(None)):
    out = None
    for j in range(taps):
        k = left - j
        term = (z if k == 0 else _shift(z, k, tmod, row_w)) * w_ref[j:j + 1, lanes]
        out = term if out is None else out + term
    return out


def _conv_bwd(dz, z, w_ref, taps, left, tmod, row_w, lanes=slice(None)):
    din = None
    dws = []
    for j in range(taps):
        k = left - j
        shifted = dz if k == 0 else _shift(dz, -k, tmod, row_w)
        term = shifted * w_ref[j:j + 1, lanes]
        din = term if din is None else din + term
        dws.append(jnp.sum(shifted * z, axis=0, keepdims=True))
    return din, dws


def _gate_matmul(xb16_ref, wd_ref, pre_scr, W, ng, gs):
    for g in range(ng):
        pg = jnp.dot(xb16_ref[:, g * gs:(g + 1) * gs], wd_ref[g], preferred_element_type=F32)
        pre_scr[:, g * gs:(g + 1) * gs] = pg[:, :gs]
        pre_scr[:, W + g * gs:W + (g + 1) * gs] = pg[:, gs:]


def _sub_loop(tl, sub_r, W, fn):
    def chunk(ci, carry):
        r0 = pl.multiple_of(ci * sub_r, sub_r)
        for lb in range(W // LANES):
            fn(r0, lb * LANES)
        return carry

    lax.fori_loop(0, tl // sub_r, chunk, 0)


def _row_loop(tl, rev, step, init):
    nchunk = tl // SUBLANES

    def chunk(j, carry):
        jj = (nchunk - 1 - j) if rev else j
        c0 = pl.multiple_of(jj * SUBLANES, SUBLANES)
        for r in (range(SUBLANES - 1, -1, -1) if rev else range(SUBLANES)):
            carry = step(c0 + r, carry)
        return carry

    return lax.fori_loop(0, nchunk, chunk, init)


def _mix_fwd(P, d, h_init, wts, *, rows, row_off, row_w, tl, h_other=None, name):
    W = P.shape[1] // 6
    nt = rows // tl
    ob = row_off // tl
    rev = d == 1
    gs = min(LRU_GROUP, W)
    ng = W // gs
    wca, wcb, bcb = wts["wca"], wts["wcb"], wts["bcb"]
    wd, ba, bx, lam = wts["wd"][d], wts["ba"][d], wts["bx"][d], wts["lam"][d]

    def tile(i):
        return (nt - 1 - i) if rev else i

    def pcol(j):
        return pl.BlockSpec((tl, W), lambda i: (tile(i) + ob, j))

    vec = pl.BlockSpec((1, W), lambda i: (0, 0))
    taps = pl.BlockSpec((SUBLANES, W), lambda i: (0, 0))
    wd_spec = pl.BlockSpec(wd.shape, lambda i: (0, 0, 0))
    seq = pl.BlockSpec((tl, W), lambda i: (tile(i), 0))

    sub_r = min(row_w, tl)
    assert tl % sub_r == 0

    def body(*refs):
        if rev:
            (bl, cl, ul, gl, vl, ql, ho, wca_r, wcb_r, bcb_r, wd_r, ba_r, bx_r, lam_r, hin, hseq, cat,
             a_scr, b_scr, pre_scr, carry, xb16_scr, sp_scr) = refs
        else:
            (vl, wcb_r, bcb_r, wd_r, ba_r, bx_r, lam_r, hin, hseq, a_scr, b_scr, pre_scr, carry, xb16_scr,
             sp_scr) = refs
        i = pl.program_id(0)

        @pl.when(i == 0)
        def _():
            carry[...] = hin[...]

        sp_scr[...] = _softplus(-lam_r[...])
        tmod = _tmod(sub_r, sub_r)

        def conv_in(r0, l0):
            rs, ls = pl.ds(r0, sub_r), pl.ds(l0, LANES)
            xb = _conv(vl[rs, ls], wcb_r, 4, 2, tmod, sub_r, ls) + bcb_r[:, ls]
            b_scr[rs, ls] = xb
            xb16_scr[rs, ls] = xb.astype(BF16)

        def gates(r0, l0):
            rs, ls = pl.ds(r0, sub_r), pl.ds(l0, LANES)
            _, ig, a, m2 = _gates(pre_scr[rs, ls] + ba_r[:, ls], pre_scr[rs, pl.ds(W + l0, LANES)] + bx_r[:, ls],
                                  sp_scr[:, ls])
            a_scr[rs, ls] = a
            m = jnp.where(m2 > 0.0, m2 * lax.rsqrt(m2), 0.0)
            b_scr[rs, ls] = m * (ig * b_scr[rs, ls])

        _sub_loop(tl, sub_r, W, conv_in)
        _gate_matmul(xb16_scr, wd_r, pre_scr, W, ng, gs)
        _sub_loop(tl, sub_r, W, gates)

        def step(t, h):
            h = a_scr[pl.ds(t, 1), :] * h + b_scr[pl.ds(t, 1), :]
            hseq[pl.ds(t, 1), :] = h
            return h

        carry[...] = _row_loop(tl, rev, step, carry[...])

        if rev:
            def mix_out(r0, l0):
                rs, ls = pl.ds(r0, sub_r), pl.ds(l0, LANES)
                yb = (ho[rs, ls] + hseq[rs, ls]) * _silu(ql[rs, ls])
                ya = bl[rs, ls] * _conv(cl[rs, ls] * ul[rs, ls], wca_r, 3, 1, tmod, sub_r, ls) * _silu(gl[rs, ls])
                cat[rs, ls] = ya.astype(cat.dtype)
                cat[rs, pl.ds(W + l0, LANES)] = yb.astype(cat.dtype)

            _sub_loop(tl, sub_r, W, mix_out)

    scratch = [pltpu.VMEM((tl, W), F32), pltpu.VMEM((tl, W), F32), pltpu.VMEM((tl, 2 * W), F32), pltpu.VMEM((1, W), F32),
               pltpu.VMEM((tl, W), BF16), pltpu.VMEM((1, W), F32)]
    if rev:
        in_specs = [pcol(j) for j in range(6)] + [seq, taps, taps, vec, wd_spec, vec, vec, vec, vec]
        args = [P] * 6 + [h_other, wca, wcb, bcb, wd, ba, bx, lam, h_init]
        out_specs = [seq, pl.BlockSpec((tl, 2 * W), lambda i: (tile(i), 0))]
        out_shape = [jax.ShapeDtypeStruct((rows, W), F32), jax.ShapeDtypeStruct((rows, 2 * W), BF16)]
    else:
        in_specs = [pcol(4), taps, vec, wd_spec, vec, vec, vec, vec]
        args = [P, wcb, bcb, wd, ba, bx, lam, h_init]
        out_specs = [seq]
        out_shape = [jax.ShapeDtypeStruct((rows, W), F32)]
    return pl.pallas_call(
        body, name=name, grid=(nt,), in_specs=in_specs, out_specs=out_specs, out_shape=out_shape,
        scratch_shapes=scratch, compiler_params=_params(("arbitrary",)),
    )(*args)


_BWD_SCRATCH = ("xb", "a", "dyl", "g", "r", "ig", "m2", "pre", "xb16", "dp16", "sp", "dlf", "edge", "c")


def _bwd_scratch(tl, W):
    f32 = {n: pltpu.VMEM((tl, W), F32) for n in ("xb", "a", "dyl", "g", "r", "ig", "m2")}
    other = {"pre": pltpu.VMEM((tl, 2 * W), F32), "xb16": pltpu.VMEM((tl, W), BF16), "dp16": pltpu.VMEM((tl, 2 * W), BF16),
             "sp": pltpu.VMEM((1, W), F32), "dlf": pltpu.VMEM((1, W), F32), "edge": pltpu.VMEM((1, W), F32),
             "c": pltpu.VMEM((1, W), F32)}
    return [{**f32, **other}[n] for n in _BWD_SCRATCH]


def _lru_bwd_tile(d, v_ref, dy_fn, hs_ref, w_refs, scr, acc, first, tl, sub_r, ng, gs):
    wcb_r, bcb_r, wd_r, ba_r, bx_r, lam_r = w_refs
    dwd_ref, dba_ref, dbx_ref, dlam_ref = acc
    W = v_ref.shape[1]
    assert gs % LANES == 0
    rev = d == 0
    tmod = _tmod(sub_r, sub_r)
    lam = lam_r[...]
    scr["sp"][...] = _softplus(-lam)
    scr["dlf"][...] = -_sigmoid(-lam)

    @pl.when(first)
    def _():
        dwd_ref[...] = jnp.zeros_like(dwd_ref)
        dba_ref[...] = jnp.zeros_like(dba_ref)
        dbx_ref[...] = jnp.zeros_like(dbx_ref)
        dlam_ref[...] = jnp.zeros_like(dlam_ref)

    def conv_in(r0, l0):
        rs, ls = pl.ds(r0, sub_r), pl.ds(l0, LANES)
        xb = _conv(v_ref[rs, ls], wcb_r, 4, 2, tmod, sub_r, ls) + bcb_r[:, ls]
        scr["xb"][rs, ls] = xb
        scr["xb16"][rs, ls] = xb.astype(BF16)
        scr["dyl"][rs, ls] = dy_fn(rs, ls)

    def gates(r0, l0):
        rs, ls = pl.ds(r0, sub_r), pl.ds(l0, LANES)
        r, ig, a, m2 = _gates(scr["pre"][rs, ls] + ba_r[:, ls], scr["pre"][rs, pl.ds(W + l0, LANES)] + bx_r[:, ls],
                              scr["sp"][:, ls])
        scr["a"][rs, ls] = a
        scr["r"][rs, ls] = r
        scr["ig"][rs, ls] = ig
        scr["m2"][rs, ls] = m2

    _sub_loop(tl, sub_r, W, conv_in)
    _gate_matmul(scr["xb16"], wd_r, scr["pre"], W, ng, gs)
    _sub_loop(tl, sub_r, W, gates)

    def step(t, c):
        g = scr["dyl"][pl.ds(t, 1), :] + c
        scr["g"][pl.ds(t, 1), :] = g
        return scr["a"][pl.ds(t, 1), :] * g

    scr["c"][...] = _row_loop(tl, rev, step, scr["c"][...])
    row = lax.broadcasted_iota(jnp.int32, (sub_r, 1), 0)

    def grads(r0, l0):
        rs, ls = pl.ds(r0, sub_r), pl.ds(l0, LANES)
        g, a, m2 = scr["g"][rs, ls], scr["a"][rs, ls], scr["m2"][rs, ls]
        r, ig, xb = scr["r"][rs, ls], scr["ig"][rs, ls], scr["xb"][rs, ls]
        h = hs_ref[rs, ls]
        if d == 0:
            e0 = pl.multiple_of(jnp.maximum(r0 - SUBLANES, 0), SUBLANES)
            edge = jnp.where(r0 == 0, scr["edge"][:, ls], hs_ref[pl.ds(e0, SUBLANES), ls][SUBLANES - 1:, :])
            hprev = jnp.where(row == 0, edge, pltpu.roll(h, 1, 0))
        else:
            e0 = pl.multiple_of(jnp.minimum(r0 + sub_r, tl - SUBLANES), SUBLANES)
            edge = jnp.where(r0 == tl - sub_r, scr["edge"][:, ls], hs_ref[pl.ds(e0, SUBLANES), ls][:1, :])
            hprev = jnp.where(row == sub_r - 1, edge, pltpu.roll(h, sub_r - 1, 0))
        rsq = lax.rsqrt(m2)
        gm = g * (m2 * rsq)
        d_la = (g * hprev) * a - (g * (ig * xb)) * ((1.0 - m2) * rsq)
        d_pr = d_la * ((-LRU_C) * scr["sp"][:, ls]) * (r * (1.0 - r))
        d_pi = (gm * xb) * (ig * (1.0 - ig))
        scr["dyl"][rs, ls] = gm * ig
        dlam_ref[:, ls] += jnp.sum(d_la * ((-LRU_C) * r), axis=0, keepdims=True) * scr["dlf"][:, ls]
        dba_ref[:, ls] += jnp.sum(d_pr, axis=0, keepdims=True)
        dbx_ref[:, ls] += jnp.sum(d_pi, axis=0, keepdims=True)
        gi, off = divmod(l0, gs)
        scr["dp16"][rs, pl.ds(gi * 2 * gs + off, LANES)] = d_pr.astype(BF16)
        scr["dp16"][rs, pl.ds(gi * 2 * gs + gs + off, LANES)] = d_pi.astype(BF16)

    _sub_loop(tl, sub_r, W, grads)
    for gi in range(ng):
        dp = scr["dp16"][:, gi * 2 * gs:(gi + 1) * 2 * gs]
        scr["g"][:, gi * gs:(gi + 1) * gs] = lax.dot_general(dp, wd_r[gi], (((1,), (1,)), ((), ())),
                                                             preferred_element_type=F32)
        dwd_ref[gi] += lax.dot_general(scr["xb16"][:, gi * gs:(gi + 1) * gs], dp, (((0,), (0,)), ((), ())),
                                       preferred_element_type=F32)


def _edge_block(h, tl, nt, d):
    W = h.shape[1]
    per = tl // SUBLANES
    if d == 0:
        return pl.BlockSpec((SUBLANES, W), lambda i: (jnp.maximum((nt - 1 - i) * per - 1, 0), 0))
    return pl.BlockSpec((SUBLANES, W), lambda i: (jnp.minimum((i + 1) * per, nt * per - 1), 0))


def _mix_bwd0(P, dcat, h0s, h_init, c_init, wts, *, rows, row_off, row_w, tl, name):
    W = P.shape[1] // 6
    nt = rows // tl
    ob = row_off // tl
    gs = min(LRU_GROUP, W)
    ng = W // gs
    wcb, bcb = wts["wcb"], wts["bcb"]
    wd, ba, bx, lam = wts["wd"][0], wts["ba"][0], wts["bx"][0], wts["lam"][0]

    def tile(i):
        return nt - 1 - i

    vec = pl.BlockSpec((1, W), lambda i: (0, 0))
    taps = pl.BlockSpec((SUBLANES, W), lambda i: (0, 0))
    wd_spec = pl.BlockSpec(wd.shape, lambda i: (0, 0, 0))
    seq = pl.BlockSpec((tl, W), lambda i: (tile(i), 0))

    sub_r = min(row_w, tl)
    assert tl % sub_r == 0

    def body(vl, ql, dyb, hs, hedge8, wcb_r, bcb_r, wd_r, ba_r, bx_r, lam_r, hin, cin,
             dxb_o, dwd_o, dba_o, dbx_o, dlam_o, cfin, *scratch):
        scr = dict(zip(_BWD_SCRATCH, scratch, strict=True))
        i = pl.program_id(0)

        @pl.when(i == 0)
        def _():
            scr["c"][...] = cin[...]

        scr["edge"][...] = jnp.where(i == nt - 1, hin[...], hedge8[SUBLANES - 1:SUBLANES, :])
        _lru_bwd_tile(0, vl, lambda rs, ls: dyb[rs, ls] * _silu(ql[rs, ls]), hs, (wcb_r, bcb_r, wd_r, ba_r, bx_r, lam_r),
                      scr, (dwd_o, dba_o, dbx_o, dlam_o), i == 0, tl, sub_r, ng, gs)
        dxb_o[...] = scr["dyl"][...] + scr["g"][...]
        cfin[...] = scr["c"][...]

    return pl.pallas_call(
        body, name=name, grid=(nt,),
        in_specs=[pl.BlockSpec((tl, W), lambda i: (tile(i) + ob, 4)), pl.BlockSpec((tl, W), lambda i: (tile(i) + ob, 5)),
                  pl.BlockSpec((tl, W), lambda i: (tile(i), 1)), seq, _edge_block(h0s, tl, nt, 0),
                  taps, vec, wd_spec, vec, vec, vec, vec, vec],
        out_specs=[seq, wd_spec, vec, vec, vec, vec],
        out_shape=[jax.ShapeDtypeStruct((rows, W), F32), jax.ShapeDtypeStruct(wd.shape, F32)]
        + [jax.ShapeDtypeStruct((1, W), F32)] * 4,
        scratch_shapes=_bwd_scratch(tl, W),
        compiler_params=_params(("arbitrary",)),
    )(P, P, dcat, h0s, h0s, wcb, bcb, wd, ba, bx, lam, h_init, c_init)


def _mix_bwd1(P, dcat, h0s, h1s, dxb0, h_init, c_init, wts, *, rows, row_off, row_w, tl, name):
    W = P.shape[1] // 6
    nt = rows // tl
    ob = row_off // tl
    gs = min(LRU_GROUP, W)
    ng = W // gs
    wca, wcb, bcb = wts["wca"], wts["wcb"], wts["bcb"]
    wd, ba, bx, lam = wts["wd"][1], wts["ba"][1], wts["bx"][1], wts["lam"][1]

    vec = pl.BlockSpec((1, W), lambda i: (0, 0))
    taps = pl.BlockSpec((SUBLANES, W), lambda i: (0, 0))
    wd_spec = pl.BlockSpec(wd.shape, lambda i: (0, 0, 0))
    seq = pl.BlockSpec((tl, W), lambda i: (i, 0))

    sub_r = min(row_w, tl)
    assert tl % sub_r == 0

    def body(bl, cl, ul, gl, vl, ql, dya, dyb, h0, h1, hedge8, dx0, wca_r, wcb_r, bcb_r, wd_r, ba_r, bx_r, lam_r, hin, cin,
             dp_o, dwd_o, dba_o, dbx_o, dlam_o, dwca_o, dwcb_o, dbcb_o, cfin, *scratch):
        scr = dict(zip(_BWD_SCRATCH, scratch, strict=True))
        i = pl.program_id(0)

        @pl.when(i == 0)
        def _():
            scr["c"][...] = cin[...]
            dwca_o[...] = jnp.zeros_like(dwca_o)
            dwcb_o[...] = jnp.zeros_like(dwcb_o)
            dbcb_o[...] = jnp.zeros_like(dbcb_o)

        scr["edge"][...] = jnp.where(i == nt - 1, hin[...], hedge8[0:1, :])
        _lru_bwd_tile(1, vl, lambda rs, ls: dyb[rs, ls] * _silu(ql[rs, ls]), h1, (wcb_r, bcb_r, wd_r, ba_r, bx_r, lam_r),
                      scr, (dwd_o, dba_o, dbx_o, dlam_o), i == 0, tl, sub_r, ng, gs)
        cfin[...] = scr["c"][...]
        tmod = _tmod(sub_r, sub_r)

        def rest(r0, l0):
            rs, ls = pl.ds(r0, sub_r), pl.ds(l0, LANES)
            dxb = dx0[rs, ls] + scr["dyl"][rs, ls] + scr["g"][rs, ls]
            dv, dwb = _conv_bwd(dxb, vl[rs, ls], wcb_r, 4, 2, tmod, sub_r, ls)
            for j in range(4):
                dwcb_o[j:j + 1, ls] += dwb[j]
            dbcb_o[:, ls] += jnp.sum(dxb, axis=0, keepdims=True)
            q = ql[rs, ls]
            sq = _sigmoid(q)
            dq = dyb[rs, ls] * (h0[rs, ls] + h1[rs, ls]) * (sq * (1.0 + q * (1.0 - sq)))
            b_, c_, u_, g_ = bl[rs, ls], cl[rs, ls], ul[rs, ls], gl[rs, ls]
            z = c_ * u_
            cz = _conv(z, wca_r, 3, 1, tmod, sub_r, ls)
            sgm = _sigmoid(g_)
            sg = g_ * sgm
            da = dya[rs, ls]
            dz, dwa = _conv_bwd(da * b_ * sg, z, wca_r, 3, 1, tmod, sub_r, ls)
            for j in range(3):
                dwca_o[j:j + 1, ls] += dwa[j]
            parts = (da * cz * sg, dz * u_, dz * c_, da * b_ * cz * (sgm * (1.0 + g_ * (1.0 - sgm))), dv, dq)
            for k, val in enumerate(parts):
                dp_o[rs, pl.ds(k * W + l0, LANES)] = val.astype(dp_o.dtype)

        _sub_loop(tl, sub_r, W, rest)

    def pcol(j):
        return pl.BlockSpec((tl, W), lambda i: (i + ob, j))

    return pl.pallas_call(
        body, name=name, grid=(nt,),
        in_specs=[pcol(j) for j in range(6)]
        + [pl.BlockSpec((tl, W), lambda i: (i, 0)), pl.BlockSpec((tl, W), lambda i: (i, 1)), seq, seq,
           _edge_block(h1s, tl, nt, 1), seq, taps, taps, vec, wd_spec, vec, vec, vec, vec, vec],
        out_specs=[pl.BlockSpec((tl, 6 * W), lambda i: (i, 0)), wd_spec, vec, vec, vec, taps, taps, vec, vec],
        out_shape=[jax.ShapeDtypeStruct((rows, 6 * W), BF16), jax.ShapeDtypeStruct(wd.shape, F32)]
        + [jax.ShapeDtypeStruct((1, W), F32)] * 3
        + [jax.ShapeDtypeStruct((SUBLANES, W), F32)] * 2 + [jax.ShapeDtypeStruct((1, W), F32)] * 2,
        scratch_shapes=_bwd_scratch(tl, W),
        compiler_params=_params(("arbitrary",)),
    )(*([P] * 6), dcat, dcat, h0s, h1s, h1s, dxb0, wca, wcb, bcb, wd, ba, bx, lam, h_init, c_init)


def _loss_head(out, x, tgt, gt, fg, tr):
    L, D = x.shape

    def body(o_ref, x_ref, t_ref, gt_ref, fg_ref, dn_o, do_o, dfg_o, dgt_o, loss_o):
        i = pl.program_id(0)

        @pl.when(i == 0)
        def _():
            dfg_o[...] = jnp.zeros_like(dfg_o)
            dgt_o[...] = jnp.zeros_like(dgt_o)
            loss_o[...] = jnp.zeros_like(loss_o)

        o = o_ref[...]
        gt_v = gt_ref[...]
        fg_v = fg_ref[...]
        n = x_ref[...] + gt_v * o
        r = lax.rsqrt(jnp.mean(n * n, axis=-1, keepdims=True) + EPS)
        nr = n * r
        e = nr * fg_v - t_ref[...]
        loss_o[...] += 0.5 * jnp.sum(jnp.mean(e * e, axis=-1, keepdims=True))
        dy = e * (1.0 / D)
        dfg_o[...] += jnp.sum(dy * nr, axis=0, keepdims=True)
        qv = dy * fg_v
        dn = r * (qv - nr * jnp.mean(qv * nr, axis=-1, keepdims=True))
        dgt_o[...] += jnp.sum(dn * o, axis=0, keepdims=True)
        dn_o[...] = dn
        do_o[...] = (dn * gt_v).astype(do_o.dtype)

    blk = pl.BlockSpec((tr, D), lambda i: (i, 0))
    vec = pl.BlockSpec((1, D), lambda i: (0, 0))
    return pl.pallas_call(
        body, name="loss_head", grid=(L // tr,), in_specs=[blk, blk, blk, vec, vec],
        out_specs=[blk, blk, vec, vec, pl.BlockSpec((SUBLANES, LANES), lambda i: (0, 0))],
        out_shape=[jax.ShapeDtypeStruct((L, D), F32), jax.ShapeDtypeStruct((L, D), BF16),
                   jax.ShapeDtypeStruct((1, D), F32), jax.ShapeDtypeStruct((1, D), F32),
                   jax.ShapeDtypeStruct((SUBLANES, LANES), F32)],
        compiler_params=_params(("arbitrary",)),
    )(out, x, tgt, gt, fg)


def _norm_bwd(dhl, x, dn, g, sc, tr, name):
    L, D = x.shape
    with_x = dn is not None

    def body(*refs):
        if with_x:
            d_ref, x_ref, dn_ref, g_ref, sc_ref, gx_o, dsh_o, dsc_o, dg_o = refs
        else:
            d_ref, x_ref, g_ref, sc_ref, dsh_o, dsc_o, dg_o = refs
        i = pl.program_id(0)

        @pl.when(i == 0)
        def _():
            dsh_o[...] = jnp.zeros_like(dsh_o)
            dsc_o[...] = jnp.zeros_like(dsc_o)
            dg_o[...] = jnp.zeros_like(dg_o)

        d = d_ref[...]
        xv = x_ref[...]
        g_v = g_ref[...]
        r = lax.rsqrt(jnp.mean(xv * xv, axis=-1, keepdims=True) + EPS)
        xr = xv * r
        dsh_o[...] += jnp.sum(d, axis=0, keepdims=True)
        dsc_o[...] += jnp.sum(d * (xr * g_v), axis=0, keepdims=True)
        dxn = d * (1.0 + sc_ref[...])
        dg_o[...] += jnp.sum(dxn * xr, axis=0, keepdims=True)
        if with_x:
            qv = dxn * g_v
            gx_o[...] = r * (qv - xr * jnp.mean(qv * xr, axis=-1, keepdims=True)) + dn_ref[...]

    blk = pl.BlockSpec((tr, D), lambda i: (i, 0))
    vec = pl.BlockSpec((1, D), lambda i: (0, 0))
    vshape = jax.ShapeDtypeStruct((1, D), F32)
    res = pl.pallas_call(
        body, name=name, grid=(L // tr,),
        in_specs=[blk, blk] + ([blk] if with_x else []) + [vec, vec],
        out_specs=([blk] if with_x else []) + [vec, vec, vec],
        out_shape=([jax.ShapeDtypeStruct((L, D), F32)] if with_x else []) + [vshape] * 3,
        compiler_params=_params(("arbitrary",)),
    )(*([dhl, x] + ([dn] if with_x else []) + [g, sc]))
    return res if with_x else [None] + list(res)


def _pack_blockdiag(wa, wx, gs):
    H, hd, _ = wa.shape
    hp = gs // hd
    ng = H // hp
    eye = jnp.eye(hp, dtype=wa.dtype)

    def bd(w):
        return jnp.einsum("gpij,pq->gpiqj", w.reshape(ng, hp, hd, hd), eye).reshape(ng, gs, gs)

    return jnp.concatenate([bd(wa), bd(wx)], axis=-1).astype(BF16)


def _unpack_blockdiag(dwd, H, hd, gs):
    hp = gs // hd
    ng = H // hp
    eye = jnp.eye(hp, dtype=dwd.dtype)

    def diag(dm):
        return jnp.einsum("gpiqj,pq->gpij", dm.reshape(ng, hp, hd, hp, hd), eye).reshape(H, hd, hd)

    return diag(dwd[:, :, :gs]), diag(dwd[:, :, gs:])


def kernel(x, c, ctx, c_ctx, norm_g, w_ada, b_ada, w_in, w_conv_a, w_conv_b, b_conv_b, lru_wa, lru_ba, lru_wx, lru_bx, lru_lambda, w_out, final_g, loss_target, m_c_ctx, m_norm_g, m_w_ada, m_b_ada, m_w_in, m_w_conv_a, m_w_conv_b, m_b_conv_b, m_lru_wa, m_lru_ba, m_lru_wx, m_lru_bx, m_lru_lambda, m_w_out, m_final_g, v_c_ctx, v_norm_g, v_w_ada, v_b_ada, v_w_in, v_w_conv_a, v_w_conv_b, v_b_conv_b, v_lru_wa, v_lru_ba, v_lru_wx, v_lru_bx, v_lru_lambda, v_w_out, v_final_g):
    xi, yi, ci = _pos()
    me = 4 * xi + 2 * yi + ci
    q = 2 * xi + yi
    first_core = (ci == 0).astype(F32)

    L, D = x.shape[1], x.shape[2]
    T = ctx.shape[1]
    W = D // 2
    Wq = W // 4
    H, hd = lru_wa.shape[2], lru_wa.shape[3]
    gs = min(LRU_GROUP, W)
    nq = w_ada.shape[2]
    tl = min(256, T, L)
    tr = min(256, T, L)
    x2, ctx2, tgt2 = x[0], ctx[0], loss_target[0]

    def place(shard, full_cols):
        z = jnp.zeros((shard.shape[0], full_cols), F32)
        return lax.dynamic_update_slice(z, shard * first_core, (0, q * shard.shape[1]))

    c_rows = lax.dynamic_update_slice(jnp.zeros((8, D), F32), c, (me, 0))
    small_in = [c_rows, place(w_conv_a[0], W), place(w_conv_b[0], W), place(lru_ba[0], W), place(lru_bx[0], W),
                place(lru_lambda[0], W)]
    small_shapes = [a.shape for a in small_in]
    gathered = _allreduce8(_pack(small_in, 8 * SUBLANES), "gather_small")
    c_all, wca, wcb, ba_all, bx_all, lam_all = _unpack(gathered, small_shapes)

    s_rows = jnp.concatenate([c_all, c_ctx[None, :], jnp.zeros((7, D), F32)], axis=0)
    mod_part = _matmul(s_rows, w_ada[0], a_act="silu", bias=lax.dynamic_slice(b_ada, (0, q * nq), (1, nq)),
                       tm=16, tn=nq, tk=512, name="ada_fwd")
    mod_all = _allreduce8(_pack([place(mod_part, 4 * nq)], 8 * SUBLANES), "gather_mod")
    mod_all = _unpack(mod_all, [(16, 4 * nq)])[0]
    mod_l = lax.dynamic_slice(mod_all, (me, 0), (1, 3 * D))
    mod_c = mod_all[8:9]
    sh_l, sc_l, gt_l = mod_l[:, :D], mod_l[:, D:2 * D], mod_l[:, 2 * D:]
    sh_c, sc_c = mod_c[:, :D], mod_c[:, D:2 * D]

    pad_taps = lambda w: jnp.pad(w, ((0, SUBLANES - w.shape[0]), (0, 0)))
    wts = {
        "wca": pad_taps(wca), "wcb": pad_taps(wcb), "bcb": b_conv_b,
        "wd": [_pack_blockdiag(lru_wa[0, d], lru_wx[0, d], gs) for d in range(2)],
        "ba": [ba_all[d:d + 1] for d in range(2)], "bx": [bx_all[d:d + 1] for d in range(2)],
        "lam": [lam_all[d:d + 1] for d in range(2)],
    }

    hl = _norm_in(x2, ctx2, norm_g, sc_l, sh_l, sc_c, sh_c, tr)
    p_lat, win_full, wout_full = _in_proj_gather(hl, w_in[0].astype(BF16), w_out[0].astype(BF16),
                                                 jnp.reshape(q, (1,)).astype(jnp.int32), rows=L, tm=min(1024, L))
    p_ctx = _matmul(hl, win_full, a_rows=T, a_off=L, tm=T, tn=1536, tk=D, name="in_proj_ctx")
    zero_w = jnp.zeros((1, W), F32)
    c0s = _mix_fwd(p_ctx, 0, zero_w, wts, rows=T, row_off=0, row_w=T, tl=tl, name="ctx_fwd0")[0]
    c1s, _ = _mix_fwd(p_ctx, 1, zero_w, wts, rows=T, row_off=0, row_w=T, tl=tl, h_other=c0s, name="ctx_fwd1")
    h0_init, h1_init = c0s[T - 1:T], c1s[0:1]
    h0s = _mix_fwd(p_lat, 0, h0_init, wts, rows=L, row_off=0, row_w=GRID_W, tl=tl, name="mix_fwd0")[0]
    h1s, cat = _mix_fwd(p_lat, 1, h1_init, wts, rows=L, row_off=0, row_w=GRID_W, tl=tl, h_other=h0s, name="mix_fwd1")
    out = _matmul(cat, wout_full, tm=512, tn=D, tk=2 * W, name="out_proj")
    dn, dout, dfg, dgt, loss_blk = _loss_head(out, x2, tgt2, gt_l, final_g[None, :], tr)

    dcat = _matmul(dout, wout_full, tb=True, tm=512, tn=2 * W, tk=D, name="out_proj_bwd")
    gw_out = _matmul(cat, dout, ta=True, tm=1024, tn=D, tk=2048, out_dtype=BF16, name="w_out_grad")
    dxb0, dwd0, dba0, dbx0, dlam0, ch0 = _mix_bwd0(p_lat, dcat, h0s, h0_init, zero_w, wts, rows=L, row_off=0,
                                                   row_w=GRID_W, tl=tl, name="mix_bwd0")
    dp, dwd1, dba1, dbx1, dlam1, dwca, dwcb, dbcb, ch1 = _mix_bwd1(
        p_lat, dcat, h0s, h1s, dxb0, h1_init, zero_w, wts, rows=L, row_off=0, row_w=GRID_W, tl=tl, name="mix_bwd1")
    zero_cat = jnp.zeros((T, 2 * W), F32)
    cxb0, cwd0, cba0, cbx0, clam0, _ = _mix_bwd0(p_ctx, zero_cat, c0s, zero_w, ch0, wts, rows=T, row_off=0,
                                                 row_w=T, tl=tl, name="ctx_bwd0")
    dp_c, cwd1, cba1, cbx1, clam1, cwca, cwcb, cbcb, _ = _mix_bwd1(
        p_ctx, zero_cat, c0s, c1s, cxb0, zero_w, ch1, wts, rows=T, row_off=0, row_w=T, tl=tl, name="ctx_bwd1")

    gw_in_ctx = _matmul(hl, dp_c, ta=True, a_rows=T, a_off=L, tm=1024, tn=1536, tk=T, name="w_in_grad_ctx")
    gw_in = _matmul(hl, dp, ta=True, a_rows=L, tm=1024, tn=1536, tk=2048, init=gw_in_ctx, out_dtype=BF16,
                    name="w_in_grad")
    rs_axes = [1, 0]
    pair_sums = _rs_pair_sums([gw_in, gw_out], rs_axes)
    dhl, rs_slots = _matmul(dp, win_full, tb=True, tm=512, tn=D, tk=3072, name="in_proj_bwd",
                            side=_rs_chips_side(pair_sums))
    dhc = _matmul(dp_c, win_full, tb=True, tm=T, tn=D, tk=512, name="in_proj_bwd_ctx")
    gx, dsh_l, dsc_l, dng_l = _norm_bwd(dhl, x2, dn, norm_g, sc_l, tr, "norm_bwd")
    _, dsh_c, dsc_c, dng_c = _norm_bwd(dhc, ctx2, None, norm_g, sc_c, tr, "norm_bwd_ctx")

    g_in_shard, g_out_shard = _rs_finish(rs_slots, rs_axes)

    dwa0, dwx0 = _unpack_blockdiag(dwd0 + cwd0, H, hd, gs)
    dwa1, dwx1 = _unpack_blockdiag(dwd1 + cwd1, H, hd, gs)
    zeros_d = jnp.zeros((1, D), F32)
    dmod_l = jnp.concatenate([dsh_l, dsc_l, dgt], axis=1)
    dmod_c = jnp.concatenate([dsh_c, dsc_c, zeros_d], axis=1)
    small_g = [
        lax.dynamic_update_slice(jnp.zeros((8, 3 * D), F32), dmod_l, (me, 0)), dmod_c,
        dfg, dng_l + dng_c, (dwca + cwca)[:3], (dwcb + cwcb)[:4], dbcb + cbcb,
        jnp.stack([dwa0, dwa1]), jnp.stack([dwx0, dwx1]),
        jnp.concatenate([dba0 + cba0, dba1 + cba1], axis=0), jnp.concatenate([dbx0 + cbx0, dbx1 + cbx1], axis=0),
        jnp.concatenate([dlam0 + clam0, dlam1 + clam1], axis=0),
    ]
    g_shapes = [a.shape for a in small_g]
    (g_rows, g_modc, g_fg, g_ng, g_wca, g_wcb, g_bcb, g_wa, g_wx, g_ba, g_bx, g_lam) = _unpack(
        _allreduce8(_pack(small_g, 8 * SUBLANES), "reduce_small"), g_shapes)

    g_mod = jnp.concatenate([g_rows, g_modc, jnp.zeros((7, 3 * D), F32)], axis=0)
    g_mod_q = lax.dynamic_slice(g_mod, (0, q * nq), (16, nq))
    g_w_ada = _matmul(s_rows, g_mod_q, ta=True, a_act="silu", tm=1024, tn=nq, tk=16, name="w_ada_grad")
    g_b_ada = jnp.sum(g_mod[:9], axis=0, keepdims=True)
    gc_part = _matmul(jnp.pad(lax.dynamic_slice(g_modc, (0, q * nq), (1, nq)), ((0, 7), (0, 0))), w_ada[0], tb=True,
                      dsilu_mul=c_ctx[None, :], tm=8, tn=D, tk=512, name="c_ctx_grad")
    g_c_ctx = _unpack(_allreduce8(_pack([gc_part[0:1] * first_core], 8 * SUBLANES), "reduce_c_ctx"), [(D,)])[0]

    def shard_cols(a, width):
        return lax.dynamic_slice(a, (0, q * width), (a.shape[0], width))

    grads = {
        "c_ctx": g_c_ctx, "norm_g": g_ng, "b_ada": g_b_ada,
        "w_conv_a": shard_cols(g_wca, Wq)[None], "w_conv_b": shard_cols(g_wcb, Wq)[None], "b_conv_b": g_bcb,
        "lru_wa": g_wa[None], "lru_ba": shard_cols(g_ba, Wq)[None], "lru_wx": g_wx[None],
        "lru_bx": shard_cols(g_bx, Wq)[None], "lru_lambda": shard_cols(g_lam, Wq)[None], "final_g": g_fg[0],
    }
    small_names = list(grads)
    given = dict(c_ctx=(c_ctx, m_c_ctx, v_c_ctx), norm_g=(norm_g, m_norm_g, v_norm_g), b_ada=(b_ada, m_b_ada, v_b_ada),
                 w_conv_a=(w_conv_a, m_w_conv_a, v_w_conv_a), w_conv_b=(w_conv_b, m_w_conv_b, v_w_conv_b),
                 b_conv_b=(b_conv_b, m_b_conv_b, v_b_conv_b), lru_wa=(lru_wa, m_lru_wa, v_lru_wa),
                 lru_ba=(lru_ba, m_lru_ba, v_lru_ba), lru_wx=(lru_wx, m_lru_wx, v_lru_wx),
                 lru_bx=(lru_bx, m_lru_bx, v_lru_bx), lru_lambda=(lru_lambda, m_lru_lambda, v_lru_lambda),
                 final_g=(final_g, m_final_g, v_final_g))
    shapes = [given[n][0].shape for n in small_names]
    packed = [_pack([given[n][j] for n in small_names], 2 * SUBLANES) for j in range(3)]
    packed_g = _pack([grads[n] for n in small_names], 2 * SUBLANES)
    sd, sm, sv = _adam(packed[0], packed_g, packed[1], packed[2], "adam_small")
    delta_s = dict(zip(small_names, _unpack(sd, shapes)))
    newm_s = dict(zip(small_names, _unpack(sm, shapes)))
    newv_s = dict(zip(small_names, _unpack(sv, shapes)))
    grads = {n: grads[n].reshape(given[n][0].shape) for n in small_names}

    big = {"w_ada": (w_ada, g_w_ada, m_w_ada, v_w_ada), "w_in": (w_in, g_in_shard, m_w_in, v_w_in),
           "w_out": (w_out, g_out_shard, m_w_out, v_w_out)}
    delta_b, newm_b, newv_b = {}, {}, {}
    for n, (w, g, m, v) in big.items():
        grads[n] = g[None]
        d_, m_, v_ = _adam(w[0], g, m[0], v[0], "adam_" + n)
        delta_b[n], newm_b[n], newv_b[n] = d_[None], m_[None], v_[None]

    loss = lax.psum(loss_blk[0, 0], AXES)
    order = ["c_ctx", "norm_g", "w_ada", "b_ada", "w_in", "w_conv_a", "w_conv_b", "b_conv_b", "lru_wa", "lru_ba",
             "lru_wx", "lru_bx", "lru_lambda", "w_out", "final_g"]
    delta = {**delta_s, **delta_b}
    newm = {**newm_s, **newm_b}
    newv = {**newv_s, **newv_b}
    return (loss, gx[None], *[grads[n] for n in order], *[delta[n] for n in order], *[newm[n] for n in order],
            *[newv[n] for n in order])
```

```python
import functools

import jax
import jax.numpy as jnp
from jax import lax
from jax.experimental import pallas as pl
from jax.experimental.pallas import tpu as pltpu

F32 = jnp.float32
BF16 = jnp.bfloat16
MESH_ID = pl.DeviceIdType.MESH
AXES = ("x", "y", "c")

EPS = 1e-6
LRU_C = 8.0
GRID_W = 64
ADAM_LR = 0.001
ADAM_B1 = 0.9
ADAM_B2 = 0.999
ADAM_EPS = 1e-08
ADAM_WD = 0.01
ADAM_STEP = 10

LANES = 128
SUBLANES = 8
PACK_COLS = 1024
VMEM_LIMIT = 56 * 2**20
LRU_GROUP = 256


def _params(sem=None):
    return pltpu.CompilerParams(vmem_limit_bytes=VMEM_LIMIT, dimension_semantics=sem)


def _pick(dim, pref, quantum=LANES):
    if dim <= pref:
        return dim
    best = None
    for t in range(quantum, pref + 1, quantum):
        if dim % t == 0:
            best = t
    assert best is not None, (dim, pref)
    return best


def _pos():
    return lax.axis_index("x"), lax.axis_index("y"), lax.axis_index("c")


def _flip(v, bit):
    return 1 - v if bit else v


def _sigmoid(v):
    return 0.5 * jnp.tanh(0.5 * v) + 0.5


def _silu(v):
    return v * _sigmoid(v)


def _dsilu(v):
    s = _sigmoid(v)
    return s * (1.0 + v * (1.0 - s))


def _gates(pre_r, pre_i, sp):
    r = _sigmoid(pre_r)
    ig = _sigmoid(pre_i)
    e = LRU_C * r * sp
    w = jnp.tanh(e)
    return r, ig, jnp.exp(-e), (2.0 * w) * pl.reciprocal(1.0 + w, approx=True)


def _softplus(z):
    return jnp.maximum(z, 0.0) + jnp.log1p(jnp.exp(-jnp.abs(z)))


def _matmul(a, b, *, ta=False, tb=False, tm=512, tn=512, tk=512, out_dtype=F32, name,
            a_rows=None, a_off=0, a_act=None, init=None, bias=None, dsilu_mul=None, side=None):
    rows_a = a.shape[0] if a_rows is None else a_rows
    if ta:
        K, M = rows_a, a.shape[1]
    else:
        M, K = rows_a, a.shape[1]
    N = b.shape[0] if tb else b.shape[1]
    tm, tn, tk = _pick(M, tm, SUBLANES), _pick(N, tn), _pick(K, tk)
    t_rows = tk if ta else tm
    assert a_off % t_rows == 0
    nk = K // tk
    gi, gj = M // tm, N // tn
    off_blocks = a_off // t_rows
    dims = (((0 if ta else 1,), (1 if tb else 0,)), ((), ()))
    extras = [e for e in (init, bias, dsilu_mul) if e is not None]
    n_sin = len(side["ins"]) if side else 0
    n_sout = len(side["outs"]) if side else 0

    def body(a_ref, b_ref, *rest):
        rest = list(rest)
        init_ref = rest.pop(0) if init is not None else None
        bias_ref = rest.pop(0) if bias is not None else None
        dsm_ref = rest.pop(0) if dsilu_mul is not None else None
        side_in = [rest.pop(0) for _ in range(n_sin)]
        o_ref = rest.pop(0)
        side_out = [rest.pop(0) for _ in range(n_sout)]
        acc_ref = rest.pop(0) if nk > 1 else None
        side_scr = rest
        i, j, k = pl.program_id(0), pl.program_id(1), pl.program_id(2)

        if side:
            @pl.when((i == 0) & (j == 0) & (k == 0))
            def _():
                side["start"](side_in, side_out, side_scr)

        av = a_ref[...]
        if a_act == "silu":
            av = _silu(av)
        prod = lax.dot_general(av, b_ref[...], dims, preferred_element_type=F32)

        def finish(r):
            if bias_ref is not None:
                r = r + bias_ref[...]
            if dsm_ref is not None:
                r = r * _dsilu(dsm_ref[...])
            o_ref[...] = r.astype(o_ref.dtype)

        if nk == 1:
            finish(prod if init_ref is None else prod + init_ref[...])
        else:
            @pl.when(k == 0)
            def _():
                acc_ref[...] = prod if init_ref is None else prod + init_ref[...]

            @pl.when(k > 0)
            def _():
                acc_ref[...] += prod

            @pl.when(k == nk - 1)
            def _():
                finish(acc_ref[...])

        if side:
            @pl.when((i == gi - 1) & (j == gj - 1) & (k == nk - 1))
            def _():
                side["finish"](side_in, side_out, side_scr)

    if ta:
        a_spec = pl.BlockSpec((tk, tm), lambda i, j, k: (k + off_blocks, i))
    else:
        a_spec = pl.BlockSpec((tm, tk), lambda i, j, k: (i + off_blocks, k))
    if tb:
        b_spec = pl.BlockSpec((tn, tk), lambda i, j, k: (j, k))
    else:
        b_spec = pl.BlockSpec((tk, tn), lambda i, j, k: (k, j))
    in_specs = [a_spec, b_spec]
    if init is not None:
        in_specs.append(pl.BlockSpec((tm, tn), lambda i, j, k: (i, j)))
    if bias is not None:
        in_specs.append(pl.BlockSpec((1, tn), lambda i, j, k: (0, j)))
    if dsilu_mul is not None:
        in_specs.append(pl.BlockSpec((1, tn), lambda i, j, k: (0, j)))
    hbm = pl.BlockSpec(memory_space=pl.ANY)
    res = pl.pallas_call(
        body, name=name, grid=(gi, gj, nk),
        in_specs=in_specs + [hbm] * n_sin,
        out_specs=[pl.BlockSpec((tm, tn), lambda i, j, k: (i, j))] + [hbm] * n_sout,
        out_shape=[jax.ShapeDtypeStruct((M, N), out_dtype)] + (list(side["outs"]) if side else []),
        scratch_shapes=([pltpu.VMEM((tm, tn), F32)] if nk > 1 else []) + (list(side["scratch"]) if side else []),
        compiler_params=_params(("arbitrary",) * 3 if side else ("parallel", "parallel", "arbitrary")),
    )(a, b, *extras, *(side["ins"] if side else []))
    return (res[0], res[1:]) if side else res[0]


def _elementwise(fn, ins, outs, *, rows, cols, name, tr=256):
    tr = _pick(rows, tr, 2 * SUBLANES)
    n_in = len(ins)

    def body(*refs):
        vals = fn(*[r[...] for r in refs[:n_in]])
        if not isinstance(vals, (tuple, list)):
            vals = (vals,)
        for r, v in zip(refs[n_in:], vals, strict=True):
            r[...] = v.astype(r.dtype)

    def spec(off):
        assert off % tr == 0
        ob = off // tr
        return pl.BlockSpec((tr, cols), lambda i: (i + ob, 0))

    res = pl.pallas_call(
        body, name=name, grid=(rows // tr,),
        in_specs=[spec(off) for _, off in ins],
        out_specs=[spec(0) for _ in outs],
        out_shape=[jax.ShapeDtypeStruct((rows, cols), dt) for dt in outs],
        compiler_params=_params(("parallel",)),
    )(*[a for a, _ in ins])
    return res


def _adam_math(w, g, m, v):
    m = ADAM_B1 * m + (1.0 - ADAM_B1) * g
    v = ADAM_B2 * v + (1.0 - ADAM_B2) * (g * g)
    m_hat = m / (1.0 - ADAM_B1 ** ADAM_STEP)
    v_hat = v / (1.0 - ADAM_B2 ** ADAM_STEP)
    delta = -ADAM_LR * (m_hat / (jnp.sqrt(v_hat) + ADAM_EPS) + ADAM_WD * w)
    return delta, m, v


def _adam(w, g, m, v, name):
    rows, cols = w.shape
    return _elementwise(_adam_math, [(w, 0), (g, 0), (m, 0), (v, 0)], [F32, F32, F32],
                        rows=rows, cols=cols, name=name)


def _pack(arrs, row_quantum):
    flat = jnp.concatenate([a.reshape(-1).astype(F32) for a in arrs])
    n = flat.shape[0]
    q = row_quantum * PACK_COLS
    total = -(-n // q) * q
    flat = jnp.pad(flat, (0, total - n))
    return flat.reshape(total // PACK_COLS, PACK_COLS)


def _unpack(buf, shapes):
    flat = buf.reshape(-1)
    out, off = [], 0
    for s in shapes:
        n = 1
        for d in s:
            n *= d
        out.append(flat[off:off + n].reshape(s))
        off += n
    return out


def _allreduce8(buf, name):
    R, C = buf.shape
    assert R % (8 * SUBLANES) == 0
    m = R // 8

    def body(x_ref, o_ref, recv, red, s1, r1, s2, r2):
        x, y, c = _pos()
        me = 4 * x + 2 * y + c

        def peer(k):
            px, py, pc = _flip(x, (k >> 2) & 1), _flip(y, (k >> 1) & 1), _flip(c, k & 1)
            return (px, py, pc), 4 * px + 2 * py + pc

        def rows(ref, idx):
            return ref.at[pl.ds(pl.multiple_of(idx * m, SUBLANES), m), :]

        def scatter(k):
            dev, p = peer(k)
            return pltpu.make_async_remote_copy(src_ref=rows(x_ref, p), dst_ref=recv.at[k], send_sem=s1.at[k],
                                                recv_sem=r1.at[k], device_id=dev, device_id_type=MESH_ID)

        def share(k):
            dev, p = peer(k)
            return pltpu.make_async_remote_copy(src_ref=red, dst_ref=rows(o_ref, me), send_sem=s2.at[k],
                                                recv_sem=r2.at[k], device_id=dev, device_id_type=MESH_ID)

        def shared_from(k):
            dev, p = peer(k)
            return pltpu.make_async_remote_copy(src_ref=red, dst_ref=rows(o_ref, p), send_sem=s2.at[k],
                                                recv_sem=r2.at[k], device_id=dev, device_id_type=MESH_ID)

        for k in range(1, 8):
            scatter(k).start()
        acc = rows(x_ref, me)[...]
        for k in range(1, 8):
            scatter(k).wait_recv()
            acc = acc + recv[k]
        red[...] = acc
        rows(o_ref, me)[...] = acc
        for k in range(1, 8):
            share(k).start()
        for k in range(1, 8):
            shared_from(k).wait_recv()
        for k in range(1, 8):
            scatter(k).wait_send()
            share(k).wait_send()

    return pl.pallas_call(
        body, name=name,
        in_specs=[pl.BlockSpec(memory_space=pltpu.VMEM)],
        out_specs=pl.BlockSpec(memory_space=pltpu.VMEM),
        out_shape=jax.ShapeDtypeStruct((R, C), F32),
        scratch_shapes=[pltpu.VMEM((8, m, C), F32), pltpu.VMEM((m, C), F32),
                        pltpu.SemaphoreType.DMA((8,)), pltpu.SemaphoreType.DMA((8,)),
                        pltpu.SemaphoreType.DMA((8,)), pltpu.SemaphoreType.DMA((8,))],
        compiler_params=_params(),
    )(buf)


def _bounce(src, dst, buf, sem):
    cin = pltpu.make_async_copy(src, buf, sem)
    cin.start()
    cin.wait()
    cout = pltpu.make_async_copy(buf, dst, sem)
    cout.start()
    cout.wait()


def _chunk(ref, axis, idx, size):
    start = idx * size
    if axis == 0:
        return ref.at[pl.ds(start, size), :]
    return ref.at[:, pl.ds(start, size)]


def _in_proj_gather(hl, win, wout, q_arr, *, rows, tm):
    D, nq = win.shape
    dq, D2 = wout.shape
    ni = rows // tm
    ops = ((0, 1, nq, D // 2), (1, 0, dq, dq // 2))

    def body(q_ref, a_ref, win_ref, wout_ref, p_ref, gin_ref, gout_ref, b_scr, buf_out, lsem, ssem, rsem, fsem, gsem):
        j, i = pl.program_id(0), pl.program_id(1)
        x, y, c = _pos()
        q = 2 * x + y
        srcs = (win_ref, wout_ref)
        dsts = (gin_ref, gout_ref)

        def shard_window(o, chip):
            _, axis, size, _ = ops[o]
            return _chunk(dsts[o], axis, chip, size)

        def half(ref, o, core):
            return ref.at[pl.ds(core * ops[o][3], ops[o][3]), :]

        def half_window(o, chip, core):
            _, axis, size, hs = ops[o]
            if axis == 1:
                return dsts[o].at[pl.ds(core * hs, hs), pl.ds(chip * size, size)]
            return dsts[o].at[pl.ds(chip * size + core * hs, hs), :]

        def chip_of(k):
            px, py = _flip(x, (k >> 1) & 1), _flip(y, k & 1)
            return px, py, 2 * px + py

        def send(o, k):
            px, py, _ = chip_of(k)
            return pltpu.make_async_remote_copy(
                src_ref=half(srcs[o], o, c), dst_ref=half_window(o, q, c), send_sem=ssem.at[o, k],
                recv_sem=rsem.at[o, k], device_id=(px, py, c), device_id_type=MESH_ID)

        def chip_recv(o, k):
            px, py, pq = chip_of(k)
            landed = half_window(o, pq, c)
            pltpu.make_async_remote_copy(src_ref=landed, dst_ref=landed, send_sem=ssem.at[o, k], recv_sem=rsem.at[o, k],
                                         device_id=(px, py, c), device_id_type=MESH_ID).wait_recv()

        def to_sibling(o, k):
            landed = half_window(o, chip_of(k)[2], c)
            return pltpu.make_async_remote_copy(src_ref=landed, dst_ref=landed, send_sem=fsem.at[o, k],
                                                recv_sem=gsem.at[o, k], device_id=(x, y, 1 - c), device_id_type=MESH_ID)

        def from_sibling(o, k):
            theirs = half_window(o, chip_of(k)[2], 1 - c)
            pltpu.make_async_remote_copy(src_ref=theirs, dst_ref=theirs, send_sem=fsem.at[o, k], recv_sem=gsem.at[o, k],
                                         device_id=(x, y, 1 - c), device_id_type=MESH_ID).wait_recv()

        def relay(o, core):
            if core == 0:
                landed, target = half_window(o, chip_of(2)[2], 0), (x, 1 - y, 0)
            else:
                landed, target = half_window(o, chip_of(1)[2], 1), (1 - x, y, 1)
            return pltpu.make_async_remote_copy(src_ref=landed, dst_ref=landed, send_sem=ssem.at[o, 3],
                                                recv_sem=rsem.at[o, 3], device_id=target, device_id_type=MESH_ID)

        def on_core(core, fn):
            @pl.when(c == core)
            def _():
                fn()

        def land(o, k):
            chip_recv(o, k)
            if k == 2:
                on_core(0, lambda: relay(o, 0).start())
            if k == 1:
                on_core(1, lambda: relay(o, 1).start())
            to_sibling(o, k).start()

        def settle(o, k):
            from_sibling(o, k)
            to_sibling(o, k).wait_send()

        def b_load(k, slot):
            src = win_ref if k == 0 else shard_window(0, chip_of(k)[2])
            return pltpu.make_async_copy(src, b_scr.at[slot], lsem.at[0])

        def own_store():
            return pltpu.make_async_copy(b_scr.at[0], shard_window(0, q), lsem.at[2])

        order = (0, 2, 1, 3)
        early = max(ni - 2, 0)

        @pl.when((j == 0) & (i == 0))
        def _():
            for o in range(2):
                for k in (2, 1):
                    send(o, k).start()
            first = b_load(0, 0)
            first.start()
            first.wait()
            own_store().start()
            _bounce(wout_ref, shard_window(1, q), buf_out, lsem.at[1])

        for jj in range(3):
            nxt = order[jj + 1]

            @pl.when((j == jj) & (i == early))
            def _(nxt=nxt):
                land(0, nxt)

            @pl.when((j == jj) & (i == ni - 1))
            def _(jj=jj, nxt=nxt):
                settle(0, nxt)
                if jj == 1:
                    own_store().wait()
                b_load(nxt, (jj + 1) % 2).start()

            @pl.when((j == jj + 1) & (i == 0))
            def _(jj=jj, nxt=nxt):
                b_load(nxt, (jj + 1) % 2).wait()

        @pl.when((j == 3) & (i == 0))
        def _():
            land(1, 2)
            land(1, 1)

        p_ref[...] = jnp.dot(a_ref[...], b_scr[j % 2], preferred_element_type=F32)

        @pl.when((j == 3) & (i == ni - 1))
        def _():
            settle(1, 2)
            settle(1, 1)
            land(1, 3)
            settle(1, 3)
            for o in range(2):
                for k in (2, 1):
                    send(o, k).wait_send()
                for core in range(2):
                    on_core(core, lambda o=o, core=core: relay(o, core).wait_send())

    hbm = pl.BlockSpec(memory_space=pl.ANY)
    grid_spec = pltpu.PrefetchScalarGridSpec(
        num_scalar_prefetch=1, grid=(4, ni),
        in_specs=[pl.BlockSpec((tm, D), lambda j, i, qr: (i, 0)), hbm, hbm],
        out_specs=[pl.BlockSpec((tm, nq), lambda j, i, qr: (i, jnp.bitwise_xor(qr[0], ((j & 1) << 1) | (j >> 1)))),
                   hbm, hbm],
        scratch_shapes=[pltpu.VMEM((2,) + win.shape, win.dtype), pltpu.VMEM(wout.shape, wout.dtype), pltpu.SemaphoreType.DMA((3,))]
        + [pltpu.SemaphoreType.DMA((2, 4)) for _ in range(4)])
    return pl.pallas_call(
        body, name="in_proj_gather", grid_spec=grid_spec,
        out_shape=[jax.ShapeDtypeStruct((rows, 4 * nq), F32), jax.ShapeDtypeStruct((D, 4 * nq), win.dtype),
                   jax.ShapeDtypeStruct((4 * dq, D2), wout.dtype)],
        compiler_params=_params(("arbitrary", "arbitrary")),
    )(q_arr, hl, win, wout)


def _rs_to_sibling(gs, axes):
    n = len(gs)
    shapes = []
    for g, ax in zip(gs, axes):
        s = list(g.shape)
        s[ax] //= 8
        shapes.append(tuple(s))

    def body(*refs):
        g_refs, mine, landed = refs[:n], refs[n:2 * n], refs[2 * n:3 * n]
        bufs = refs[3 * n:4 * n]
        lsem, ssem, rsem = refs[4 * n:]
        x, y, c = _pos()
        cps = []
        for o in range(n):
            size = shapes[o][axes[o]]
            for j in range(4):
                rc = pltpu.make_async_remote_copy(
                    src_ref=_chunk(g_refs[o], axes[o], 2 * j + 1 - c, size), dst_ref=landed[o].at[j],
                    send_sem=ssem.at[o, j], recv_sem=rsem.at[o, j], device_id=(x, y, 1 - c), device_id_type=MESH_ID)
                rc.start()
                cps.append(rc)
        for o in range(n):
            size = shapes[o][axes[o]]
            for j in range(4):
                _bounce(_chunk(g_refs[o], axes[o], 2 * j + c, size), mine[o].at[j], bufs[o], lsem.at[o])
        for rc in cps:
            rc.wait()

    hbm = pl.BlockSpec(memory_space=pl.ANY)
    outs = [jax.ShapeDtypeStruct((4,) + s, g.dtype) for s, g in zip(shapes, gs)]
    res = pl.pallas_call(
        body, name="rs_to_sibling", in_specs=[hbm] * n, out_specs=[hbm] * (2 * n), out_shape=outs + outs,
        scratch_shapes=[pltpu.VMEM(s, g.dtype) for s, g in zip(shapes, gs)]
        + [pltpu.SemaphoreType.DMA((n,)), pltpu.SemaphoreType.DMA((n, 4)), pltpu.SemaphoreType.DMA((n, 4))],
        compiler_params=_params(),
    )(*gs)
    return res[:n], res[n:]


def _rs_chips_side(parts):
    n = len(parts)

    def copies(p_refs, slots, scr):
        ssem, rsem = scr[n + 1], scr[n + 2]
        x, y, c = _pos()
        cps = []
        for o in range(n):
            for k in range(1, 4):
                px, py = _flip(x, (k >> 1) & 1), _flip(y, k & 1)
                cps.append(pltpu.make_async_remote_copy(
                    src_ref=p_refs[o].at[2 * px + py], dst_ref=slots[o].at[k], send_sem=ssem.at[o, k],
                    recv_sem=rsem.at[o, k], device_id=(px, py, c), device_id_type=MESH_ID))
        return cps

    def start(p_refs, slots, scr):
        for cp in copies(p_refs, slots, scr):
            cp.start()

    def finish(p_refs, slots, scr):
        x, y, _ = _pos()
        q = 2 * x + y
        for o in range(n):
            _bounce(p_refs[o].at[q], slots[o].at[0], scr[o], scr[n].at[o])
        for cp in copies(p_refs, slots, scr):
            cp.wait()

    return dict(
        ins=list(parts), outs=[jax.ShapeDtypeStruct(p.shape, p.dtype) for p in parts],
        scratch=[pltpu.VMEM(p.shape[1:], p.dtype) for p in parts]
        + [pltpu.SemaphoreType.DMA((n,)), pltpu.SemaphoreType.DMA((n, 4)), pltpu.SemaphoreType.DMA((n, 4))],
        start=start, finish=finish)


def _rs_share(rs, axes):
    n = len(rs)
    shapes = []
    for r, ax in zip(rs, axes):
        s = list(r.shape)
        s[ax] *= 2
        shapes.append(tuple(s))

    def body(*refs):
        r_refs, outs = refs[:n], refs[n:2 * n]
        bufs = refs[2 * n:3 * n]
        lsem, ssem, rsem = refs[3 * n:]
        x, y, c = _pos()
        cps = []
        for o in range(n):
            size = r_refs[o].shape[axes[o]]
            window = _chunk(outs[o], axes[o], c, size)
            rc = pltpu.make_async_remote_copy(src_ref=r_refs[o], dst_ref=window, send_sem=ssem.at[o], recv_sem=rsem.at[o],
                                              device_id=(x, y, 1 - c), device_id_type=MESH_ID)
            rc.start()
            cps.append(rc)
        for o in range(n):
            size = r_refs[o].shape[axes[o]]
            _bounce(r_refs[o], _chunk(outs[o], axes[o], c, size), bufs[o], lsem.at[o])
        for cp in cps:
            cp.wait()

    hbm = pl.BlockSpec(memory_space=pl.ANY)
    return pl.pallas_call(
        body, name="rs_share", in_specs=[hbm] * n, out_specs=[hbm] * n,
        out_shape=[jax.ShapeDtypeStruct(s, r.dtype) for s, r in zip(shapes, rs)],
        scratch_shapes=[pltpu.VMEM(r.shape, r.dtype) for r in rs] + [pltpu.SemaphoreType.DMA((n,)) for _ in range(3)],
        compiler_params=_params(),
    )(*rs)


def _rs_pair_sums(gs, axes):
    mine, landed = _rs_to_sibling(gs, axes)
    pair_sums = []
    for o, (mi, la) in enumerate(zip(mine, landed)):
        rows, cols = mi.shape[0] * mi.shape[1], mi.shape[2]
        s = _elementwise(lambda a, b: a.astype(F32) + b.astype(F32), [(mi.reshape(rows, cols), 0), (la.reshape(rows, cols), 0)],
                         [BF16], rows=rows, cols=cols, name=f"rs_pair_sum{o}")[0]
        pair_sums.append(s.reshape(mi.shape))
    return pair_sums


def _rs_finish(slots, axes):
    reduced = []
    for o, sl in enumerate(slots):
        rows, cols = sl.shape[1], sl.shape[2]
        flat = sl.reshape(4 * rows, cols)
        r = _elementwise(lambda a, b, c, d: (a.astype(F32) + b.astype(F32)) + (c.astype(F32) + d.astype(F32)),
                         [(flat, k * rows) for k in range(4)], [F32], rows=rows, cols=cols, name=f"rs_chip_sum{o}")[0]
        reduced.append(r)
    return _rs_share(reduced, axes)


def _norm_in(x, ctx, g, sc_l, sh_l, sc_c, sh_c, tr):
    L, D = x.shape
    T = ctx.shape[0]
    nx, nc = L // tr, T // tr

    def body(x_ref, c_ref, g_ref, scl, shl, scc, shc, o_ref):
        i = pl.program_id(0)

        def run(src, sc, sh):
            v = src[...]
            r = lax.rsqrt(jnp.mean(v * v, axis=-1, keepdims=True) + EPS)
            o_ref[...] = ((v * r * g_ref[...]) * (1.0 + sc[...]) + sh[...]).astype(o_ref.dtype)

        @pl.when(i < nx)
        def _():
            run(x_ref, scl, shl)

        @pl.when(i >= nx)
        def _():
            run(c_ref, scc, shc)

    vec = pl.BlockSpec((1, D), lambda i: (0, 0))
    return pl.pallas_call(
        body, name="norm_in", grid=(nx + nc,),
        in_specs=[pl.BlockSpec((tr, D), lambda i: (jnp.minimum(i, nx - 1), 0)),
                  pl.BlockSpec((tr, D), lambda i: (jnp.maximum(i - nx, 0), 0)), vec, vec, vec, vec, vec],
        out_specs=pl.BlockSpec((tr, D), lambda i: (i, 0)),
        out_shape=jax.ShapeDtypeStruct((L + T, D), BF16),
        compiler_params=_params(("arbitrary",)),
    )(x, ctx, g, sc_l, sh_l, sc_c, sh_c)


def _tmod(tl, row_w):
    assert row_w & (row_w - 1) == 0
    return lax.broadcasted_iota(jnp.int32, (tl, 1), 0) & (row_w - 1)


def _shift(z, k, tmod, row_w):
    tl = z.shape[0]
    rolled = pltpu.roll(z, k % tl, 0)
    mask = (tmod >= k) if k > 0 else (tmod < row_w + k)
    return jnp.where(mask, rolled, 0.0)


def _conv(z, w_ref, taps, left, tmod, row_w, lanes=slice(None)):
    out = None
    for j in range(taps):
        k = left - j
        term = (z if k == 0 else _shift(z, k, tmod, row_w)) * w_ref[j:j + 1, lanes]
        out = term if out is None else out + term
    return out


def _conv_bwd(dz, z, w_ref, taps, left, tmod, row_w, lanes=slice(None)):
    din = None
    dws = []
    for j in range(taps):
        k = left - j
        shifted = dz if k == 0 else _shift(dz, -k, tmod, row_w)
        term = shifted * w_ref[j:j + 1, lanes]
        din = term if din is None else din + term
        dws.append(jnp.sum(shifted * z, axis=0, keepdims=True))
    return din, dws


def _gate_matmul(xb16_ref, wd_ref, pre_scr, W, ng, gs):
    for g in range(ng):
        pg = jnp.dot(xb16_ref[:, g * gs:(g + 1) * gs], wd_ref[g], preferred_element_type=F32)
        pre_scr[:, g * gs:(g + 1) * gs] = pg[:, :gs]
        pre_scr[:, W + g * gs:W + (g + 1) * gs] = pg[:, gs:]


def _sub_loop(tl, sub_r, W, fn):
    def chunk(ci, carry):
        r0 = pl.multiple_of(ci * sub_r, sub_r)
        for lb in range(W // LANES):
            fn(r0, lb * LANES)
        return carry

    lax.fori_loop(0, tl // sub_r, chunk, 0)


def _row_loop(tl, rev, step, init):
    nchunk = tl // SUBLANES

    def chunk(j, carry):
        jj = (nchunk - 1 - j) if rev else j
        c0 = pl.multiple_of(jj * SUBLANES, SUBLANES)
        for r in (range(SUBLANES - 1, -1, -1) if rev else range(SUBLANES)):
            carry = step(c0 + r, carry)
        return carry

    return lax.fori_loop(0, nchunk, chunk, init)


def _mix_fwd(P, d, h_init, wts, *, rows, row_off, row_w, tl, saved0=None, name):
    W = P.shape[1] // 6
    nt = rows // tl
    ob = row_off // tl
    rev = d == 1
    gs = min(LRU_GROUP, W)
    ng = W // gs
    wca, wcb, bcb = wts["wca"], wts["wcb"], wts["bcb"]
    wd, ba, bx, lam = wts["wd"][d], wts["ba"][d], wts["bx"][d], wts["lam"][d]

    def tile(i):
        return (nt - 1 - i) if rev else i

    def pcol(j):
        return pl.BlockSpec((tl, W), lambda i: (tile(i) + ob, j))

    vec = pl.BlockSpec((1, W), lambda i: (0, 0))
    taps = pl.BlockSpec((SUBLANES, W), lambda i: (0, 0))
    wd_spec = pl.BlockSpec(wd.shape, lambda i: (0, 0, 0))
    seq = pl.BlockSpec((tl, W), lambda i: (tile(i), 0))

    sub_r = min(row_w, tl)
    assert tl % sub_r == 0

    def body(*refs):
        if rev:
            (bl, cl, ul, gl, ql, ho, xb_r, xb16_r, wca_r, wd_r, ba_r, bx_r, lam_r, hin, hseq, cat, a_o, r_o, ig_o, m2_o,
             b_scr, pre_scr, carry, sp_scr) = refs
        else:
            (vl, wcb_r, bcb_r, wd_r, ba_r, bx_r, lam_r, hin, hseq, xb_r, xb16_r, a_o, r_o, ig_o, m2_o,
             b_scr, pre_scr, carry, sp_scr) = refs
        i = pl.program_id(0)

        @pl.when(i == 0)
        def _():
            carry[...] = hin[...]

        sp_scr[...] = _softplus(-lam_r[...])
        tmod = _tmod(sub_r, sub_r)

        def conv_in(r0, l0):
            rs, ls = pl.ds(r0, sub_r), pl.ds(l0, LANES)
            xb = _conv(vl[rs, ls], wcb_r, 4, 2, tmod, sub_r, ls) + bcb_r[:, ls]
            xb_r[rs, ls] = xb
            xb16_r[rs, ls] = xb.astype(BF16)

        def gates(r0, l0):
            rs, ls = pl.ds(r0, sub_r), pl.ds(l0, LANES)
            r, ig, a, m2 = _gates(pre_scr[rs, ls] + ba_r[:, ls], pre_scr[rs, pl.ds(W + l0, LANES)] + bx_r[:, ls],
                                  sp_scr[:, ls])
            a_o[rs, ls] = a
            r_o[rs, ls] = r
            ig_o[rs, ls] = ig
            m2_o[rs, ls] = m2
            m = jnp.where(m2 > 0.0, m2 * lax.rsqrt(m2), 0.0)
            b_scr[rs, ls] = m * (ig * xb_r[rs, ls])

        if not rev:
            _sub_loop(tl, sub_r, W, conv_in)
        _gate_matmul(xb16_r, wd_r, pre_scr, W, ng, gs)
        _sub_loop(tl, sub_r, W, gates)

        def step(t, h):
            h = a_o[pl.ds(t, 1), :] * h + b_scr[pl.ds(t, 1), :]
            hseq[pl.ds(t, 1), :] = h
            return h

        carry[...] = _row_loop(tl, rev, step, carry[...])

        if rev:
            def mix_out(r0, l0):
                rs, ls = pl.ds(r0, sub_r), pl.ds(l0, LANES)
                yb = (ho[rs, ls] + hseq[rs, ls]) * _silu(ql[rs, ls])
                ya = bl[rs, ls] * _conv(cl[rs, ls] * ul[rs, ls], wca_r, 3, 1, tmod, sub_r, ls) * _silu(gl[rs, ls])
                cat[rs, ls] = ya.astype(cat.dtype)
                cat[rs, pl.ds(W + l0, LANES)] = yb.astype(cat.dtype)

            _sub_loop(tl, sub_r, W, mix_out)

    scratch = [pltpu.VMEM((tl, W), F32), pltpu.VMEM((tl, 2 * W), F32), pltpu.VMEM((1, W), F32), pltpu.VMEM((1, W), F32)]
    f32_seq = jax.ShapeDtypeStruct((rows, W), F32)
    if rev:
        in_specs = [pcol(j) for j in (0, 1, 2, 3, 5)] + [seq, seq, seq, taps, wd_spec, vec, vec, vec, vec]
        args = [P] * 5 + [saved0["h"], saved0["xb"], saved0["xb16"], wca, wd, ba, bx, lam, h_init]
        out_specs = [seq, pl.BlockSpec((tl, 2 * W), lambda i: (tile(i), 0))] + [seq] * 4
        out_shape = [f32_seq, jax.ShapeDtypeStruct((rows, 2 * W), BF16)] + [f32_seq] * 4
    else:
        in_specs = [pcol(4), taps, vec, wd_spec, vec, vec, vec, vec]
        args = [P, wcb, bcb, wd, ba, bx, lam, h_init]
        out_specs = [seq] * 7
        out_shape = [f32_seq, f32_seq, jax.ShapeDtypeStruct((rows, W), BF16)] + [f32_seq] * 4
    res = pl.pallas_call(
        body, name=name, grid=(nt,), in_specs=in_specs, out_specs=out_specs, out_shape=out_shape,
        scratch_shapes=scratch, compiler_params=_params(("arbitrary",)),
    )(*args)
    gates = dict(zip(("a", "r", "ig", "m2"), res[-4:]))
    if rev:
        return res[0], res[1], gates
    return dict(h=res[0], xb=res[1], xb16=res[2]), gates


_BWD_SCRATCH = ("dyl", "g", "dp16", "sp", "dlf", "edge", "c")
_FWD_SAVED = ("xb", "xb16", "a", "r", "ig", "m2")


def _bwd_scratch(tl, W):
    shapes = {"dyl": pltpu.VMEM((tl, W), F32), "g": pltpu.VMEM((tl, W), F32), "dp16": pltpu.VMEM((tl, 2 * W), BF16),
              "sp": pltpu.VMEM((1, W), F32), "dlf": pltpu.VMEM((1, W), F32), "edge": pltpu.VMEM((1, W), F32),
              "c": pltpu.VMEM((1, W), F32)}
    return [shapes[n] for n in _BWD_SCRATCH]


def _lru_bwd_tile(d, dy_fn, hs_ref, wd_r, lam_r, scr, acc, first, tl, sub_r, ng, gs):
    dwd_ref, dba_ref, dbx_ref, dlam_ref = acc
    W = hs_ref.shape[1]
    assert gs % LANES == 0
    rev = d == 0
    lam = lam_r[...]
    scr["sp"][...] = _softplus(-lam)
    scr["dlf"][...] = -_sigmoid(-lam)

    @pl.when(first)
    def _():
        dwd_ref[...] = jnp.zeros_like(dwd_ref)
        dba_ref[...] = jnp.zeros_like(dba_ref)
        dbx_ref[...] = jnp.zeros_like(dbx_ref)
        dlam_ref[...] = jnp.zeros_like(dlam_ref)

    def state_grad(r0, l0):
        rs, ls = pl.ds(r0, sub_r), pl.ds(l0, LANES)
        scr["dyl"][rs, ls] = dy_fn(rs, ls)

    _sub_loop(tl, sub_r, W, state_grad)

    def step(t, c):
        g = scr["dyl"][pl.ds(t, 1), :] + c
        scr["g"][pl.ds(t, 1), :] = g
        return scr["a"][pl.ds(t, 1), :] * g

    scr["c"][...] = _row_loop(tl, rev, step, scr["c"][...])
    row = lax.broadcasted_iota(jnp.int32, (sub_r, 1), 0)

    def grads(r0, l0):
        rs, ls = pl.ds(r0, sub_r), pl.ds(l0, LANES)
        g, a, m2 = scr["g"][rs, ls], scr["a"][rs, ls], scr["m2"][rs, ls]
        r, ig, xb = scr["r"][rs, ls], scr["ig"][rs, ls], scr["xb"][rs, ls]
        h = hs_ref[rs, ls]
        if d == 0:
            e0 = pl.multiple_of(jnp.maximum(r0 - SUBLANES, 0), SUBLANES)
            edge = jnp.where(r0 == 0, scr["edge"][:, ls], hs_ref[pl.ds(e0, SUBLANES), ls][SUBLANES - 1:, :])
            hprev = jnp.where(row == 0, edge, pltpu.roll(h, 1, 0))
        else:
            e0 = pl.multiple_of(jnp.minimum(r0 + sub_r, tl - SUBLANES), SUBLANES)
            edge = jnp.where(r0 == tl - sub_r, scr["edge"][:, ls], hs_ref[pl.ds(e0, SUBLANES), ls][:1, :])
            hprev = jnp.where(row == sub_r - 1, edge, pltpu.roll(h, sub_r - 1, 0))
        rsq = lax.rsqrt(m2)
        gm = g * (m2 * rsq)
        d_la = (g * hprev) * a - (g * (ig * xb)) * ((1.0 - m2) * rsq)
        d_pr = d_la * ((-LRU_C) * scr["sp"][:, ls]) * (r * (1.0 - r))
        d_pi = (gm * xb) * (ig * (1.0 - ig))
        scr["dyl"][rs, ls] = gm * ig
        dlam_ref[:, ls] += jnp.sum(d_la * ((-LRU_C) * r), axis=0, keepdims=True) * scr["dlf"][:, ls]
        dba_ref[:, ls] += jnp.sum(d_pr, axis=0, keepdims=True)
        dbx_ref[:, ls] += jnp.sum(d_pi, axis=0, keepdims=True)
        gi, off = divmod(l0, gs)
        scr["dp16"][rs, pl.ds(gi * 2 * gs + off, LANES)] = d_pr.astype(BF16)
        scr["dp16"][rs, pl.ds(gi * 2 * gs + gs + off, LANES)] = d_pi.astype(BF16)

    _sub_loop(tl, sub_r, W, grads)
    for gi in range(ng):
        dp = scr["dp16"][:, gi * 2 * gs:(gi + 1) * 2 * gs]
        scr["g"][:, gi * gs:(gi + 1) * gs] = lax.dot_general(dp, wd_r[gi], (((1,), (1,)), ((), ())),
                                                             preferred_element_type=F32)
        dwd_ref[gi] += lax.dot_general(scr["xb16"][:, gi * gs:(gi + 1) * gs], dp, (((0,), (0,)), ((), ())),
                                       preferred_element_type=F32)


def _edge_block(h, tl, nt, d):
    W = h.shape[1]
    per = tl // SUBLANES
    if d == 0:
        return pl.BlockSpec((SUBLANES, W), lambda i: (jnp.maximum((nt - 1 - i) * per - 1, 0), 0))
    return pl.BlockSpec((SUBLANES, W), lambda i: (jnp.minimum((i + 1) * per, nt * per - 1), 0))


def _mix_bwd0(P, dcat, saved0, gates0, h_init, c_init, wts, *, rows, row_off, row_w, tl, name):
    W = P.shape[1] // 6
    nt = rows // tl
    ob = row_off // tl
    gs = min(LRU_GROUP, W)
    ng = W // gs
    wd, lam = wts["wd"][0], wts["lam"][0]
    h0s = saved0["h"]
    kept = [saved0["xb"], saved0["xb16"]] + [gates0[n] for n in ("a", "r", "ig", "m2")]

    def tile(i):
        return nt - 1 - i

    vec = pl.BlockSpec((1, W), lambda i: (0, 0))
    wd_spec = pl.BlockSpec(wd.shape, lambda i: (0, 0, 0))
    seq = pl.BlockSpec((tl, W), lambda i: (tile(i), 0))

    sub_r = min(row_w, tl)
    assert tl % sub_r == 0

    def body(ql, dyb, hs, hedge8, xb_r, xb16_r, a_r, r_r, ig_r, m2_r, wd_r, lam_r, hin, cin,
             dxb_o, dwd_o, dba_o, dbx_o, dlam_o, cfin, *scratch):
        scr = dict(zip(_BWD_SCRATCH, scratch, strict=True))
        scr.update(zip(_FWD_SAVED, (xb_r, xb16_r, a_r, r_r, ig_r, m2_r), strict=True))
        i = pl.program_id(0)

        @pl.when(i == 0)
        def _():
            scr["c"][...] = cin[...]

        scr["edge"][...] = jnp.where(i == nt - 1, hin[...], hedge8[SUBLANES - 1:SUBLANES, :])
        _lru_bwd_tile(0, lambda rs, ls: dyb[rs, ls] * _silu(ql[rs, ls]), hs, wd_r, lam_r, scr,
                      (dwd_o, dba_o, dbx_o, dlam_o), i == 0, tl, sub_r, ng, gs)
        dxb_o[...] = scr["dyl"][...] + scr["g"][...]
        cfin[...] = scr["c"][...]

    return pl.pallas_call(
        body, name=name, grid=(nt,),
        in_specs=[pl.BlockSpec((tl, W), lambda i: (tile(i) + ob, 5)), pl.BlockSpec((tl, W), lambda i: (tile(i), 1)), seq,
                  _edge_block(h0s, tl, nt, 0)] + [seq] * 6 + [wd_spec, vec, vec, vec],
        out_specs=[seq, wd_spec, vec, vec, vec, vec],
        out_shape=[jax.ShapeDtypeStruct((rows, W), F32), jax.ShapeDtypeStruct(wd.shape, F32)]
        + [jax.ShapeDtypeStruct((1, W), F32)] * 4,
        scratch_shapes=_bwd_scratch(tl, W),
        compiler_params=_params(("arbitrary",)),
    )(P, dcat, h0s, h0s, *kept, wd, lam, h_init, c_init)


def _mix_bwd1(P, dcat, saved0, h1s, gates1, dxb0, h_init, c_init, wts, *, rows, row_off, row_w, tl, name):
    W = P.shape[1] // 6
    nt = rows // tl
    ob = row_off // tl
    gs = min(LRU_GROUP, W)
    ng = W // gs
    wca, wcb = wts["wca"], wts["wcb"]
    wd, lam = wts["wd"][1], wts["lam"][1]
    h0s = saved0["h"]
    kept = [saved0["xb"], saved0["xb16"]] + [gates1[n] for n in ("a", "r", "ig", "m2")]

    vec = pl.BlockSpec((1, W), lambda i: (0, 0))
    taps = pl.BlockSpec((SUBLANES, W), lambda i: (0, 0))
    wd_spec = pl.BlockSpec(wd.shape, lambda i: (0, 0, 0))
    seq = pl.BlockSpec((tl, W), lambda i: (i, 0))

    sub_r = min(row_w, tl)
    assert tl % sub_r == 0

    def body(bl, cl, ul, gl, vl, ql, dya, dyb, h0, h1, hedge8, dx0, xb_r, xb16_r, a_r, r_r, ig_r, m2_r, wca_r, wcb_r,
             wd_r, lam_r, hin, cin, dp_o, dwd_o, dba_o, dbx_o, dlam_o, dwca_o, dwcb_o, dbcb_o, cfin, *scratch):
        scr = dict(zip(_BWD_SCRATCH, scratch, strict=True))
        scr.update(zip(_FWD_SAVED, (xb_r, xb16_r, a_r, r_r, ig_r, m2_r), strict=True))
        i = pl.program_id(0)

        @pl.when(i == 0)
        def _():
            scr["c"][...] = cin[...]
            dwca_o[...] = jnp.zeros_like(dwca_o)
            dwcb_o[...] = jnp.zeros_like(dwcb_o)
            dbcb_o[...] = jnp.zeros_like(dbcb_o)

        scr["edge"][...] = jnp.where(i == nt - 1, hin[...], hedge8[0:1, :])
        _lru_bwd_tile(1, lambda rs, ls: dyb[rs, ls] * _silu(ql[rs, ls]), h1, wd_r, lam_r, scr,
                      (dwd_o, dba_o, dbx_o, dlam_o), i == 0, tl, sub_r, ng, gs)
        cfin[...] = scr["c"][...]
        tmod = _tmod(sub_r, sub_r)

        def rest(r0, l0):
            rs, ls = pl.ds(r0, sub_r), pl.ds(l0, LANES)
            dxb = dx0[rs, ls] + scr["dyl"][rs, ls] + scr["g"][rs, ls]
            dv, dwb = _conv_bwd(dxb, vl[rs, ls], wcb_r, 4, 2, tmod, sub_r, ls)
            for j in range(4):
                dwcb_o[j:j + 1, ls] += dwb[j]
            dbcb_o[:, ls] += jnp.sum(dxb, axis=0, keepdims=True)
            q = ql[rs, ls]
            sq = _sigmoid(q)
            dq = dyb[rs, ls] * (h0[rs, ls] + h1[rs, ls]) * (sq * (1.0 + q * (1.0 - sq)))
            b_, c_, u_, g_ = bl[rs, ls], cl[rs, ls], ul[rs, ls], gl[rs, ls]
            z = c_ * u_
            cz = _conv(z, wca_r, 3, 1, tmod, sub_r, ls)
            sgm = _sigmoid(g_)
            sg = g_ * sgm
            da = dya[rs, ls]
            dz, dwa = _conv_bwd(da * b_ * sg, z, wca_r, 3, 1, tmod, sub_r, ls)
            for j in range(3):
                dwca_o[j:j + 1, ls] += dwa[j]
            parts = (da * cz * sg, dz * u_, dz * c_, da * b_ * cz * (sgm * (1.0 + g_ * (1.0 - sgm))), dv, dq)
            for k, val in enumerate(parts):
                dp_o[rs, pl.ds(k * W + l0, LANES)] = val.astype(dp_o.dtype)

        _sub_loop(tl, sub_r, W, rest)

    def pcol(j):
        return pl.BlockSpec((tl, W), lambda i: (i + ob, j))

    return pl.pallas_call(
        body, name=name, grid=(nt,),
        in_specs=[pcol(j) for j in range(6)]
        + [pl.BlockSpec((tl, W), lambda i: (i, 0)), pl.BlockSpec((tl, W), lambda i: (i, 1)), seq, seq,
           _edge_block(h1s, tl, nt, 1), seq] + [seq] * 6 + [taps, taps, wd_spec, vec, vec, vec],
        out_specs=[pl.BlockSpec((tl, 6 * W), lambda i: (i, 0)), wd_spec, vec, vec, vec, taps, taps, vec, vec],
        out_shape=[jax.ShapeDtypeStruct((rows, 6 * W), BF16), jax.ShapeDtypeStruct(wd.shape, F32)]
        + [jax.ShapeDtypeStruct((1, W), F32)] * 3
        + [jax.ShapeDtypeStruct((SUBLANES, W), F32)] * 2 + [jax.ShapeDtypeStruct((1, W), F32)] * 2,
        scratch_shapes=_bwd_scratch(tl, W),
        compiler_params=_params(("arbitrary",)),
    )(*([P] * 6), dcat, dcat, h0s, h1s, h1s, dxb0, *kept, wca, wcb, wd, lam, h_init, c_init)


def _loss_head(out, x, tgt, gt, fg, tr):
    L, D = x.shape

    def body(o_ref, x_ref, t_ref, gt_ref, fg_ref, dn_o, do_o, dfg_o, dgt_o, loss_o):
        i = pl.program_id(0)

        @pl.when(i == 0)
        def _():
            dfg_o[...] = jnp.zeros_like(dfg_o)
            dgt_o[...] = jnp.zeros_like(dgt_o)
            loss_o[...] = jnp.zeros_like(loss_o)

        o = o_ref[...].astype(F32)
        gt_v = gt_ref[...]
        fg_v = fg_ref[...]
        n = x_ref[...] + gt_v * o
        r = lax.rsqrt(jnp.mean(n * n, axis=-1, keepdims=True) + EPS)
        nr = n * r
        e = nr * fg_v - t_ref[...]
        loss_o[...] += 0.5 * jnp.sum(jnp.mean(e * e, axis=-1, keepdims=True))
        dy = e * (1.0 / D)
        dfg_o[...] += jnp.sum(dy * nr, axis=0, keepdims=True)
        qv = dy * fg_v
        dn = r * (qv - nr * jnp.mean(qv * nr, axis=-1, keepdims=True))
        dgt_o[...] += jnp.sum(dn * o, axis=0, keepdims=True)
        dn_o[...] = dn.astype(dn_o.dtype)
        do_o[...] = (dn * gt_v).astype(do_o.dtype)

    blk = pl.BlockSpec((tr, D), lambda i: (i, 0))
    vec = pl.BlockSpec((1, D), lambda i: (0, 0))
    return pl.pallas_call(
        body, name="loss_head", grid=(L // tr,), in_specs=[blk, blk, blk, vec, vec],
        out_specs=[blk, blk, vec, vec, pl.BlockSpec((SUBLANES, LANES), lambda i: (0, 0))],
        out_shape=[jax.ShapeDtypeStruct((L, D), BF16), jax.ShapeDtypeStruct((L, D), BF16),
                   jax.ShapeDtypeStruct((1, D), F32), jax.ShapeDtypeStruct((1, D), F32),
                   jax.ShapeDtypeStruct((SUBLANES, LANES), F32)],
        compiler_params=_params(("arbitrary",)),
    )(out, x, tgt, gt, fg)


def _norm_bwd(dhl, x, dn, g, sc, tr, name):
    L, D = x.shape
    with_x = dn is not None

    def body(*refs):
        if with_x:
            d_ref, x_ref, dn_ref, g_ref, sc_ref, gx_o, dsh_o, dsc_o, dg_o = refs
        else:
            d_ref, x_ref, g_ref, sc_ref, dsh_o, dsc_o, dg_o = refs
        i = pl.program_id(0)

        @pl.when(i == 0)
        def _():
            dsh_o[...] = jnp.zeros_like(dsh_o)
            dsc_o[...] = jnp.zeros_like(dsc_o)
            dg_o[...] = jnp.zeros_like(dg_o)

        d = d_ref[...].astype(F32)
        xv = x_ref[...]
        g_v = g_ref[...]
        r = lax.rsqrt(jnp.mean(xv * xv, axis=-1, keepdims=True) + EPS)
        xr = xv * r
        dsh_o[...] += jnp.sum(d, axis=0, keepdims=True)
        dsc_o[...] += jnp.sum(d * (xr * g_v), axis=0, keepdims=True)
        dxn = d * (1.0 + sc_ref[...])
        dg_o[...] += jnp.sum(dxn * xr, axis=0, keepdims=True)
        if with_x:
            qv = dxn * g_v
            gx_o[...] = r * (qv - xr * jnp.mean(qv * xr, axis=-1, keepdims=True)) + dn_ref[...].astype(F32)

    blk = pl.BlockSpec((tr, D), lambda i: (i, 0))
    vec = pl.BlockSpec((1, D), lambda i: (0, 0))
    vshape = jax.ShapeDtypeStruct((1, D), F32)
    res = pl.pallas_call(
        body, name=name, grid=(L // tr,),
        in_specs=[blk, blk] + ([blk] if with_x else []) + [vec, vec],
        out_specs=([blk] if with_x else []) + [vec, vec, vec],
        out_shape=([jax.ShapeDtypeStruct((L, D), F32)] if with_x else []) + [vshape] * 3,
        compiler_params=_params(("arbitrary",)),
    )(*([dhl, x] + ([dn] if with_x else []) + [g, sc]))
    return res if with_x else [None] + list(res)


def _pack_blockdiag(wa, wx, gs):
    H, hd, _ = wa.shape
    hp = gs // hd
    ng = H // hp
    eye = jnp.eye(hp, dtype=wa.dtype)

    def bd(w):
        return jnp.einsum("gpij,pq->gpiqj", w.reshape(ng, hp, hd, hd), eye).reshape(ng, gs, gs)

    return jnp.concatenate([bd(wa), bd(wx)], axis=-1).astype(BF16)


def _unpack_blockdiag(dwd, H, hd, gs):
    hp = gs // hd
    ng = H // hp
    eye = jnp.eye(hp, dtype=dwd.dtype)

    def diag(dm):
        return jnp.einsum("gpiqj,pq->gpij", dm.reshape(ng, hp, hd, hp, hd), eye).reshape(H, hd, hd)

    return diag(dwd[:, :, :gs]), diag(dwd[:, :, gs:])


def kernel(x, c, ctx, c_ctx, norm_g, w_ada, b_ada, w_in, w_conv_a, w_conv_b, b_conv_b, lru_wa, lru_ba, lru_wx, lru_bx, lru_lambda, w_out, final_g, loss_target, m_c_ctx, m_norm_g, m_w_ada, m_b_ada, m_w_in, m_w_conv_a, m_w_conv_b, m_b_conv_b, m_lru_wa, m_lru_ba, m_lru_wx, m_lru_bx, m_lru_lambda, m_w_out, m_final_g, v_c_ctx, v_norm_g, v_w_ada, v_b_ada, v_w_in, v_w_conv_a, v_w_conv_b, v_b_conv_b, v_lru_wa, v_lru_ba, v_lru_wx, v_lru_bx, v_lru_lambda, v_w_out, v_final_g):
    xi, yi, ci = _pos()
    me = 4 * xi + 2 * yi + ci
    q = 2 * xi + yi
    first_core = (ci == 0).astype(F32)

    L, D = x.shape[1], x.shape[2]
    T = ctx.shape[1]
    W = D // 2
    Wq = W // 4
    H, hd = lru_wa.shape[2], lru_wa.shape[3]
    gs = min(LRU_GROUP, W)
    nq = w_ada.shape[2]
    tl = min(256, T, L)
    tr = min(256, T, L)
    x2, ctx2, tgt2 = x[0], ctx[0], loss_target[0]

    def place(shard, full_cols):
        z = jnp.zeros((shard.shape[0], full_cols), F32)
        return lax.dynamic_update_slice(z, shard * first_core, (0, q * shard.shape[1]))

    c_rows = lax.dynamic_update_slice(jnp.zeros((8, D), F32), c, (me, 0))
    small_in = [c_rows, place(w_conv_a[0], W), place(w_conv_b[0], W), place(lru_ba[0], W), place(lru_bx[0], W),
                place(lru_lambda[0], W)]
    small_shapes = [a.shape for a in small_in]
    gathered = _allreduce8(_pack(small_in, 8 * SUBLANES), "gather_small")
    c_all, wca, wcb, ba_all, bx_all, lam_all = _unpack(gathered, small_shapes)

    s_rows = jnp.concatenate([c_all, c_ctx[None, :], jnp.zeros((7, D), F32)], axis=0)
    mod_part = _matmul(s_rows, w_ada[0], a_act="silu", bias=lax.dynamic_slice(b_ada, (0, q * nq), (1, nq)),
                       tm=16, tn=nq, tk=512, name="ada_fwd")
    mod_all = _allreduce8(_pack([place(mod_part, 4 * nq)], 8 * SUBLANES), "gather_mod")
    mod_all = _unpack(mod_all, [(16, 4 * nq)])[0]
    mod_l = lax.dynamic_slice(mod_all, (me, 0), (1, 3 * D))
    mod_c = mod_all[8:9]
    sh_l, sc_l, gt_l = mod_l[:, :D], mod_l[:, D:2 * D], mod_l[:, 2 * D:]
    sh_c, sc_c = mod_c[:, :D], mod_c[:, D:2 * D]

    pad_taps = lambda w: jnp.pad(w, ((0, SUBLANES - w.shape[0]), (0, 0)))
    wts = {
        "wca": pad_taps(wca), "wcb": pad_taps(wcb), "bcb": b_conv_b,
        "wd": [_pack_blockdiag(lru_wa[0, d], lru_wx[0, d], gs) for d in range(2)],
        "ba": [ba_all[d:d + 1] for d in range(2)], "bx": [bx_all[d:d + 1] for d in range(2)],
        "lam": [lam_all[d:d + 1] for d in range(2)],
    }

    hl = _norm_in(x2, ctx2, norm_g, sc_l, sh_l, sc_c, sh_c, tr)
    p_lat, win_full, wout_full = _in_proj_gather(hl, w_in[0].astype(BF16), w_out[0].astype(BF16),
                                                 jnp.reshape(q, (1,)).astype(jnp.int32), rows=L, tm=min(1024, L))
    p_ctx = _matmul(hl, win_full, a_rows=T, a_off=L, tm=T, tn=1536, tk=D, name="in_proj_ctx")
    zero_w = jnp.zeros((1, W), F32)
    ctx0, cgates0 = _mix_fwd(p_ctx, 0, zero_w, wts, rows=T, row_off=0, row_w=T, tl=tl, name="ctx_fwd0")
    c1s, _, cgates1 = _mix_fwd(p_ctx, 1, zero_w, wts, rows=T, row_off=0, row_w=T, tl=tl, saved0=ctx0, name="ctx_fwd1")
    h0_init, h1_init = ctx0["h"][T - 1:T], c1s[0:1]
    lat0, gates0 = _mix_fwd(p_lat, 0, h0_init, wts, rows=L, row_off=0, row_w=GRID_W, tl=tl, name="mix_fwd0")
    h1s, cat, gates1 = _mix_fwd(p_lat, 1, h1_init, wts, rows=L, row_off=0, row_w=GRID_W, tl=tl, saved0=lat0,
                                name="mix_fwd1")
    out = _matmul(cat, wout_full, tm=512, tn=D, tk=2 * W, out_dtype=BF16, name="out_proj")
    dn, dout, dfg, dgt, loss_blk = _loss_head(out, x2, tgt2, gt_l, final_g[None, :], tr)

    dcat = _matmul(dout, wout_full, tb=True, tm=512, tn=2 * W, tk=D, name="out_proj_bwd")
    gw_out = _matmul(cat, dout, ta=True, tm=1024, tn=D, tk=2048, out_dtype=BF16, name="w_out_grad")
    dxb0, dwd0, dba0, dbx0, dlam0, ch0 = _mix_bwd0(p_lat, dcat, lat0, gates0, h0_init, zero_w, wts, rows=L, row_off=0,
                                                   row_w=GRID_W, tl=tl, name="mix_bwd0")
    dp, dwd1, dba1, dbx1, dlam1, dwca, dwcb, dbcb, ch1 = _mix_bwd1(
        p_lat, dcat, lat0, h1s, gates1, dxb0, h1_init, zero_w, wts, rows=L, row_off=0, row_w=GRID_W, tl=tl,
        name="mix_bwd1")
    zero_cat = jnp.zeros((T, 2 * W), F32)
    cxb0, cwd0, cba0, cbx0, clam0, _ = _mix_bwd0(p_ctx, zero_cat, ctx0, cgates0, zero_w, ch0, wts, rows=T, row_off=0,
                                                 row_w=T, tl=tl, name="ctx_bwd0")
    dp_c, cwd1, cba1, cbx1, clam1, cwca, cwcb, cbcb, _ = _mix_bwd1(
        p_ctx, zero_cat, ctx0, c1s, cgates1, cxb0, zero_w, ch1, wts, rows=T, row_off=0, row_w=T, tl=tl, name="ctx_bwd1")

    gw_in_ctx = _matmul(hl, dp_c, ta=True, a_rows=T, a_off=L, tm=1024, tn=1536, tk=T, name="w_in_grad_ctx")
    gw_in = _matmul(hl, dp, ta=True, a_rows=L, tm=1024, tn=1536, tk=2048, init=gw_in_ctx, out_dtype=BF16,
                    name="w_in_grad")
    rs_axes = [1, 0]
    pair_sums = _rs_pair_sums([gw_in, gw_out], rs_axes)
    dhl, rs_slots = _matmul(dp, win_full, tb=True, tm=512, tn=D, tk=3072, out_dtype=BF16, name="in_proj_bwd",
                            side=_rs_chips_side(pair_sums))
    dhc = _matmul(dp_c, win_full, tb=True, tm=T, tn=D, tk=512, name="in_proj_bwd_ctx")
    gx, dsh_l, dsc_l, dng_l = _norm_bwd(dhl, x2, dn, norm_g, sc_l, tr, "norm_bwd")
    _, dsh_c, dsc_c, dng_c = _norm_bwd(dhc, ctx2, None, norm_g, sc_c, tr, "norm_bwd_ctx")

    g_in_shard, g_out_shard = _rs_finish(rs_slots, rs_axes)

    dwa0, dwx0 = _unpack_blockdiag(dwd0 + cwd0, H, hd, gs)
    dwa1, dwx1 = _unpack_blockdiag(dwd1 + cwd1, H, hd, gs)
    zeros_d = jnp.zeros((1, D), F32)
    dmod_l = jnp.concatenate([dsh_l, dsc_l, dgt], axis=1)
    dmod_c = jnp.concatenate([dsh_c, dsc_c, zeros_d], axis=1)
    small_g = [
        lax.dynamic_update_slice(jnp.zeros((8, 3 * D), F32), dmod_l, (me, 0)), dmod_c,
        dfg, dng_l + dng_c, (dwca + cwca)[:3], (dwcb + cwcb)[:4], dbcb + cbcb,
        jnp.stack([dwa0, dwa1]), jnp.stack([dwx0, dwx1]),
        jnp.concatenate([dba0 + cba0, dba1 + cba1], axis=0), jnp.concatenate([dbx0 + cbx0, dbx1 + cbx1], axis=0),
        jnp.concatenate([dlam0 + clam0, dlam1 + clam1], axis=0),
    ]
    g_shapes = [a.shape for a in small_g]
    (g_rows, g_modc, g_fg, g_ng, g_wca, g_wcb, g_bcb, g_wa, g_wx, g_ba, g_bx, g_lam) = _unpack(
        _allreduce8(_pack(small_g, 8 * SUBLANES), "reduce_small"), g_shapes)

    g_mod = jnp.concatenate([g_rows, g_modc, jnp.zeros((7, 3 * D), F32)], axis=0)
    g_mod_q = lax.dynamic_slice(g_mod, (0, q * nq), (16, nq))
    g_w_ada = _matmul(s_rows, g_mod_q, ta=True, a_act="silu", tm=1024, tn=nq, tk=16, name="w_ada_grad")
    g_b_ada = jnp.sum(g_mod[:9], axis=0, keepdims=True)
    gc_part = _matmul(jnp.pad(lax.dynamic_slice(g_modc, (0, q * nq), (1, nq)), ((0, 7), (0, 0))), w_ada[0], tb=True,
                      dsilu_mul=c_ctx[None, :], tm=8, tn=D, tk=512, name="c_ctx_grad")
    g_c_ctx = _unpack(_allreduce8(_pack([gc_part[0:1] * first_core], 8 * SUBLANES), "reduce_c_ctx"), [(D,)])[0]

    def shard_cols(a, width):
        return lax.dynamic_slice(a, (0, q * width), (a.shape[0], width))

    grads = {
        "c_ctx": g_c_ctx, "norm_g": g_ng, "b_ada": g_b_ada,
        "w_conv_a": shard_cols(g_wca, Wq)[None], "w_conv_b": shard_cols(g_wcb, Wq)[None], "b_conv_b": g_bcb,
        "lru_wa": g_wa[None], "lru_ba": shard_cols(g_ba, Wq)[None], "lru_wx": g_wx[None],
        "lru_bx": shard_cols(g_bx, Wq)[None], "lru_lambda": shard_cols(g_lam, Wq)[None], "final_g": g_fg[0],
    }
    small_names = list(grads)
    given = dict(c_ctx=(c_ctx, m_c_ctx, v_c_ctx), norm_g=(norm_g, m_norm_g, v_norm_g), b_ada=(b_ada, m_b_ada, v_b_ada),
                 w_conv_a=(w_conv_a, m_w_conv_a, v_w_conv_a), w_conv_b=(w_conv_b, m_w_conv_b, v_w_conv_b),
                 b_conv_b=(b_conv_b, m_b_conv_b, v_b_conv_b), lru_wa=(lru_wa, m_lru_wa, v_lru_wa),
                 lru_ba=(lru_ba, m_lru_ba, v_lru_ba), lru_wx=(lru_wx, m_lru_wx, v_lru_wx),
                 lru_bx=(lru_bx, m_lru_bx, v_lru_bx), lru_lambda=(lru_lambda, m_lru_lambda, v_lru_lambda),
                 final_g=(final_g, m_final_g, v_final_g))
    shapes = [given[n][0].shape for n in small_names]
    packed = [_pack([given[n][j] for n in small_names], 2 * SUBLANES) for j in range(3)]
    packed_g = _pack([grads[n] for n in small_names], 2 * SUBLANES)
    sd, sm, sv = _adam(packed[0], packed_g, packed[1], packed[2], "adam_small")
    delta_s = dict(zip(small_names, _unpack(sd, shapes)))
    newm_s = dict(zip(small_names, _unpack(sm, shapes)))
    newv_s = dict(zip(small_names, _unpack(sv, shapes)))
    grads = {n: grads[n].reshape(given[n][0].shape) for n in small_names}

    big = {"w_ada": (w_ada, g_w_ada, m_w_ada, v_w_ada), "w_in": (w_in, g_in_shard, m_w_in, v_w_in),
           "w_out": (w_out, g_out_shard, m_w_out, v_w_out)}
    delta_b, newm_b, newv_b = {}, {}, {}
    for n, (w, g, m, v) in big.items():
        grads[n] = g[None]
        d_, m_, v_ = _adam(w[0], g, m[0], v[0], "adam_" + n)
        delta_b[n], newm_b[n], newv_b[n] = d_[None], m_[None], v_[None]

    loss = lax.psum(loss_blk[0, 0], AXES)
    order = ["c_ctx", "norm_g", "w_ada", "b_ada", "w_in", "w_conv_a", "w_conv_b", "b_conv_b", "lru_wa", "lru_ba",
             "lru_wx", "lru_bx", "lru_lambda", "w_out", "final_g"]
    delta = {**delta_s, **delta_b}
    newm = {**newm_s, **newm_b}
    newv = {**newv_s, **newv_b}
    return (loss, gx[None], *[grads[n] for n in order], *[delta[n] for n in order], *[newm[n] for n in order],
            *[newv[n] for n in order])
```

```python
import functools

import jax
import jax.numpy as jnp
from jax import lax
from jax.experimental import pallas as pl
from jax.experimental.pallas import tpu as pltpu

F32 = jnp.float32
BF16 = jnp.bfloat16
MESH_ID = pl.DeviceIdType.MESH
AXES = ("x", "y", "c")

EPS = 1e-6
LRU_C = 8.0
GRID_W = 64
ADAM_LR = 0.001
ADAM_B1 = 0.9
ADAM_B2 = 0.999
ADAM_EPS = 1e-08
ADAM_WD = 0.01
ADAM_STEP = 10

LANES = 128
SUBLANES = 8
PACK_COLS = 1024
VMEM_LIMIT = 56 * 2**20
LRU_GROUP = 256


def _params(sem=None):
    return pltpu.CompilerParams(vmem_limit_bytes=VMEM_LIMIT, dimension_semantics=sem)


def _pick(dim, pref, quantum=LANES):
    if dim <= pref:
        return dim
    best = None
    for t in range(quantum, pref + 1, quantum):
        if dim % t == 0:
            best = t
    assert best is not None, (dim, pref)
    return best


def _pos():
    return lax.axis_index("x"), lax.axis_index("y"), lax.axis_index("c")


def _flip(v, bit):
    return 1 - v if bit else v


def _sigmoid(v):
    return 0.5 * jnp.tanh(0.5 * v) + 0.5


def _silu(v):
    return v * _sigmoid(v)


def _dsilu(v):
    s = _sigmoid(v)
    return s * (1.0 + v * (1.0 - s))


def _gates(pre_r, pre_i, sp):
    r = _sigmoid(pre_r)
    ig = _sigmoid(pre_i)
    e = LRU_C * r * sp
    w = jnp.tanh(e)
    return r, ig, jnp.exp(-e), (2.0 * w) * pl.reciprocal(1.0 + w, approx=True)


def _softplus(z):
    return jnp.maximum(z, 0.0) + jnp.log1p(jnp.exp(-jnp.abs(z)))


def _matmul(a, b, *, ta=False, tb=False, tm=512, tn=512, tk=512, out_dtype=F32, name,
            a_rows=None, a_off=0, a_act=None, init=None, bias=None, dsilu_mul=None, side=None):
    rows_a = a.shape[0] if a_rows is None else a_rows
    if ta:
        K, M = rows_a, a.shape[1]
    else:
        M, K = rows_a, a.shape[1]
    N = b.shape[0] if tb else b.shape[1]
    tm, tn, tk = _pick(M, tm, SUBLANES), _pick(N, tn), _pick(K, tk)
    t_rows = tk if ta else tm
    assert a_off % t_rows == 0
    nk = K // tk
    gi, gj = M // tm, N // tn
    off_blocks = a_off // t_rows
    dims = (((0 if ta else 1,), (1 if tb else 0,)), ((), ()))
    extras = [e for e in (init, bias, dsilu_mul) if e is not None]
    n_sin = len(side["ins"]) if side else 0
    n_sout = len(side["outs"]) if side else 0

    def body(a_ref, b_ref, *rest):
        rest = list(rest)
        init_ref = rest.pop(0) if init is not None else None
        bias_ref = rest.pop(0) if bias is not None else None
        dsm_ref = rest.pop(0) if dsilu_mul is not None else None
        side_in = [rest.pop(0) for _ in range(n_sin)]
        o_ref = rest.pop(0)
        side_out = [rest.pop(0) for _ in range(n_sout)]
        acc_ref = rest.pop(0) if nk > 1 else None
        side_scr = rest
        i, j, k = pl.program_id(0), pl.program_id(1), pl.program_id(2)

        if side:
            @pl.when((i == 0) & (j == 0) & (k == 0))
            def _():
                side["start"](side_in, side_out, side_scr)

        av = a_ref[...]
        if a_act == "silu":
            av = _silu(av)
        prod = lax.dot_general(av, b_ref[...], dims, preferred_element_type=F32)

        def finish(r):
            if bias_ref is not None:
                r = r + bias_ref[...]
            if dsm_ref is not None:
                r = r * _dsilu(dsm_ref[...])
            o_ref[...] = r.astype(o_ref.dtype)

        if nk == 1:
            finish(prod if init_ref is None else prod + init_ref[...])
        else:
            @pl.when(k == 0)
            def _():
                acc_ref[...] = prod if init_ref is None else prod + init_ref[...]

            @pl.when(k > 0)
            def _():
                acc_ref[...] += prod

            @pl.when(k == nk - 1)
            def _():
                finish(acc_ref[...])

        if side:
            @pl.when((i == gi - 1) & (j == gj - 1) & (k == nk - 1))
            def _():
                side["finish"](side_in, side_out, side_scr)

    if ta:
        a_spec = pl.BlockSpec((tk, tm), lambda i, j, k: (k + off_blocks, i))
    else:
        a_spec = pl.BlockSpec((tm, tk), lambda i, j, k: (i + off_blocks, k))
    if tb:
        b_spec = pl.BlockSpec((tn, tk), lambda i, j, k: (j, k))
    else:
        b_spec = pl.BlockSpec((tk, tn), lambda i, j, k: (k, j))
    in_specs = [a_spec, b_spec]
    if init is not None:
        in_specs.append(pl.BlockSpec((tm, tn), lambda i, j, k: (i, j)))
    if bias is not None:
        in_specs.append(pl.BlockSpec((1, tn), lambda i, j, k: (0, j)))
    if dsilu_mul is not None:
        in_specs.append(pl.BlockSpec((1, tn), lambda i, j, k: (0, j)))
    hbm = pl.BlockSpec(memory_space=pl.ANY)
    res = pl.pallas_call(
        body, name=name, grid=(gi, gj, nk),
        in_specs=in_specs + [hbm] * n_sin,
        out_specs=[pl.BlockSpec((tm, tn), lambda i, j, k: (i, j))] + [hbm] * n_sout,
        out_shape=[jax.ShapeDtypeStruct((M, N), out_dtype)] + (list(side["outs"]) if side else []),
        scratch_shapes=([pltpu.VMEM((tm, tn), F32)] if nk > 1 else []) + (list(side["scratch"]) if side else []),
        compiler_params=_params(("arbitrary",) * 3 if side else ("parallel", "parallel", "arbitrary")),
    )(a, b, *extras, *(side["ins"] if side else []))
    return (res[0], res[1:]) if side else res[0]


def _elementwise(fn, ins, outs, *, rows, cols, name, tr=256):
    tr = _pick(rows, tr, 2 * SUBLANES)
    n_in = len(ins)

    def body(*refs):
        vals = fn(*[r[...] for r in refs[:n_in]])
        if not isinstance(vals, (tuple, list)):
            vals = (vals,)
        for r, v in zip(refs[n_in:], vals, strict=True):
            r[...] = v.astype(r.dtype)

    def spec(off):
        assert off % tr == 0
        ob = off // tr
        return pl.BlockSpec((tr, cols), lambda i: (i + ob, 0))

    res = pl.pallas_call(
        body, name=name, grid=(rows // tr,),
        in_specs=[spec(off) for _, off in ins],
        out_specs=[spec(0) for _ in outs],
        out_shape=[jax.ShapeDtypeStruct((rows, cols), dt) for dt in outs],
        compiler_params=_params(("parallel",)),
    )(*[a for a, _ in ins])
    return res


def _adam_math(w, g, m, v):
    m = ADAM_B1 * m + (1.0 - ADAM_B1) * g
    v = ADAM_B2 * v + (1.0 - ADAM_B2) * (g * g)
    m_hat = m / (1.0 - ADAM_B1 ** ADAM_STEP)
    v_hat = v / (1.0 - ADAM_B2 ** ADAM_STEP)
    delta = -ADAM_LR * (m_hat / (jnp.sqrt(v_hat) + ADAM_EPS) + ADAM_WD * w)
    return delta, m, v


def _adam(w, g, m, v, name):
    rows, cols = w.shape
    return _elementwise(_adam_math, [(w, 0), (g, 0), (m, 0), (v, 0)], [F32, F32, F32],
                        rows=rows, cols=cols, name=name)


def _pack(arrs, row_quantum):
    flat = jnp.concatenate([a.reshape(-1).astype(F32) for a in arrs])
    n = flat.shape[0]
    q = row_quantum * PACK_COLS
    total = -(-n // q) * q
    flat = jnp.pad(flat, (0, total - n))
    return flat.reshape(total // PACK_COLS, PACK_COLS)


def _unpack(buf, shapes):
    flat = buf.reshape(-1)
    out, off = [], 0
    for s in shapes:
        n = 1
        for d in s:
            n *= d
        out.append(flat[off:off + n].reshape(s))
        off += n
    return out


def _allreduce8(buf, name):
    R, C = buf.shape
    assert R % (8 * SUBLANES) == 0
    m = R // 8

    def body(x_ref, o_ref, recv, red, s1, r1, s2, r2):
        x, y, c = _pos()
        me = 4 * x + 2 * y + c

        def peer(k):
            px, py, pc = _flip(x, (k >> 2) & 1), _flip(y, (k >> 1) & 1), _flip(c, k & 1)
            return (px, py, pc), 4 * px + 2 * py + pc

        def rows(ref, idx):
            return ref.at[pl.ds(pl.multiple_of(idx * m, SUBLANES), m), :]

        def scatter(k):
            dev, p = peer(k)
            return pltpu.make_async_remote_copy(src_ref=rows(x_ref, p), dst_ref=recv.at[k], send_sem=s1.at[k],
                                                recv_sem=r1.at[k], device_id=dev, device_id_type=MESH_ID)

        def share(k):
            dev, p = peer(k)
            return pltpu.make_async_remote_copy(src_ref=red, dst_ref=rows(o_ref, me), send_sem=s2.at[k],
                                                recv_sem=r2.at[k], device_id=dev, device_id_type=MESH_ID)

        def shared_from(k):
            dev, p = peer(k)
            return pltpu.make_async_remote_copy(src_ref=red, dst_ref=rows(o_ref, p), send_sem=s2.at[k],
                                                recv_sem=r2.at[k], device_id=dev, device_id_type=MESH_ID)

        for k in range(1, 8):
            scatter(k).start()
        acc = rows(x_ref, me)[...]
        for k in range(1, 8):
            scatter(k).wait_recv()
            acc = acc + recv[k]
        red[...] = acc
        rows(o_ref, me)[...] = acc
        for k in range(1, 8):
            share(k).start()
        for k in range(1, 8):
            shared_from(k).wait_recv()
        for k in range(1, 8):
            scatter(k).wait_send()
            share(k).wait_send()

    return pl.pallas_call(
        body, name=name,
        in_specs=[pl.BlockSpec(memory_space=pltpu.VMEM)],
        out_specs=pl.BlockSpec(memory_space=pltpu.VMEM),
        out_shape=jax.ShapeDtypeStruct((R, C), F32),
        scratch_shapes=[pltpu.VMEM((8, m, C), F32), pltpu.VMEM((m, C), F32),
                        pltpu.SemaphoreType.DMA((8,)), pltpu.SemaphoreType.DMA((8,)),
                        pltpu.SemaphoreType.DMA((8,)), pltpu.SemaphoreType.DMA((8,))],
        compiler_params=_params(),
    )(buf)


def _bounce(src, dst, buf, sem):
    cin = pltpu.make_async_copy(src, buf, sem)
    cin.start()
    cin.wait()
    cout = pltpu.make_async_copy(buf, dst, sem)
    cout.start()
    cout.wait()


def _chunk(ref, axis, idx, size):
    start = idx * size
    if axis == 0:
        return ref.at[pl.ds(start, size), :]
    return ref.at[:, pl.ds(start, size)]


def _in_proj_gather(hl, win, wout, q_arr, *, rows, tm):
    D, nq = win.shape
    dq, D2 = wout.shape
    ni = rows // tm
    ops = ((0, 1, nq, D // 2), (1, 0, dq, dq // 2))

    def body(q_ref, a_ref, win_ref, wout_ref, p_ref, gin_ref, gout_ref, b_scr, buf_out, lsem, ssem, rsem, fsem, gsem):
        j, i = pl.program_id(0), pl.program_id(1)
        x, y, c = _pos()
        q = 2 * x + y
        srcs = (win_ref, wout_ref)
        dsts = (gin_ref, gout_ref)

        def shard_window(o, chip):
            _, axis, size, _ = ops[o]
            return _chunk(dsts[o], axis, chip, size)

        def half(ref, o, core):
            return ref.at[pl.ds(core * ops[o][3], ops[o][3]), :]

        def half_window(o, chip, core):
            _, axis, size, hs = ops[o]
            if axis == 1:
                return dsts[o].at[pl.ds(core * hs, hs), pl.ds(chip * size, size)]
            return dsts[o].at[pl.ds(chip * size + core * hs, hs), :]

        def chip_of(k):
            px, py = _flip(x, (k >> 1) & 1), _flip(y, k & 1)
            return px, py, 2 * px + py

        def send(o, k):
            px, py, _ = chip_of(k)
            return pltpu.make_async_remote_copy(
                src_ref=half(srcs[o], o, c), dst_ref=half_window(o, q, c), send_sem=ssem.at[o, k],
                recv_sem=rsem.at[o, k], device_id=(px, py, c), device_id_type=MESH_ID)

        def chip_recv(o, k):
            px, py, pq = chip_of(k)
            landed = half_window(o, pq, c)
            pltpu.make_async_remote_copy(src_ref=landed, dst_ref=landed, send_sem=ssem.at[o, k], recv_sem=rsem.at[o, k],
                                         device_id=(px, py, c), device_id_type=MESH_ID).wait_recv()

        def to_sibling(o, k):
            landed = half_window(o, chip_of(k)[2], c)
            return pltpu.make_async_remote_copy(src_ref=landed, dst_ref=landed, send_sem=fsem.at[o, k],
                                                recv_sem=gsem.at[o, k], device_id=(x, y, 1 - c), device_id_type=MESH_ID)

        def from_sibling(o, k):
            theirs = half_window(o, chip_of(k)[2], 1 - c)
            pltpu.make_async_remote_copy(src_ref=theirs, dst_ref=theirs, send_sem=fsem.at[o, k], recv_sem=gsem.at[o, k],
                                         device_id=(x, y, 1 - c), device_id_type=MESH_ID).wait_recv()

        def relay(o, core):
            if core == 0:
                landed, target = half_window(o, chip_of(2)[2], 0), (x, 1 - y, 0)
            else:
                landed, target = half_window(o, chip_of(1)[2], 1), (1 - x, y, 1)
            return pltpu.make_async_remote_copy(src_ref=landed, dst_ref=landed, send_sem=ssem.at[o, 3],
                                                recv_sem=rsem.at[o, 3], device_id=target, device_id_type=MESH_ID)

        def on_core(core, fn):
            @pl.when(c == core)
            def _():
                fn()

        def land(o, k):
            chip_recv(o, k)
            if k == 2:
                on_core(0, lambda: relay(o, 0).start())
            if k == 1:
                on_core(1, lambda: relay(o, 1).start())
            to_sibling(o, k).start()

        def settle(o, k):
            from_sibling(o, k)
            to_sibling(o, k).wait_send()

        def b_load(k, slot):
            src = win_ref if k == 0 else shard_window(0, chip_of(k)[2])
            return pltpu.make_async_copy(src, b_scr.at[slot], lsem.at[0])

        def own_store():
            return pltpu.make_async_copy(b_scr.at[0], shard_window(0, q), lsem.at[2])

        order = (0, 2, 1, 3)
        early = max(ni - 2, 0)

        @pl.when((j == 0) & (i == 0))
        def _():
            for o in range(2):
                for k in (2, 1):
                    send(o, k).start()
            first = b_load(0, 0)
            first.start()
            first.wait()
            own_store().start()
            _bounce(wout_ref, shard_window(1, q), buf_out, lsem.at[1])

        for jj in range(3):
            nxt = order[jj + 1]

            @pl.when((j == jj) & (i == early))
            def _(nxt=nxt):
                land(0, nxt)

            @pl.when((j == jj) & (i == ni - 1))
            def _(jj=jj, nxt=nxt):
                settle(0, nxt)
                if jj == 1:
                    own_store().wait()
                b_load(nxt, (jj + 1) % 2).start()

            @pl.when((j == jj + 1) & (i == 0))
            def _(jj=jj, nxt=nxt):
                b_load(nxt, (jj + 1) % 2).wait()

        @pl.when((j == 3) & (i == 0))
        def _():
            land(1, 2)
            land(1, 1)

        p_ref[...] = jnp.dot(a_ref[...], b_scr[j % 2], preferred_element_type=F32).astype(p_ref.dtype)

        @pl.when((j == 3) & (i == ni - 1))
        def _():
            settle(1, 2)
            settle(1, 1)
            land(1, 3)
            settle(1, 3)
            for o in range(2):
                for k in (2, 1):
                    send(o, k).wait_send()
                for core in range(2):
                    on_core(core, lambda o=o, core=core: relay(o, core).wait_send())

    hbm = pl.BlockSpec(memory_space=pl.ANY)
    grid_spec = pltpu.PrefetchScalarGridSpec(
        num_scalar_prefetch=1, grid=(4, ni),
        in_specs=[pl.BlockSpec((tm, D), lambda j, i, qr: (i, 0)), hbm, hbm],
        out_specs=[pl.BlockSpec((tm, nq), lambda j, i, qr: (i, jnp.bitwise_xor(qr[0], ((j & 1) << 1) | (j >> 1)))),
                   hbm, hbm],
        scratch_shapes=[pltpu.VMEM((2,) + win.shape, win.dtype), pltpu.VMEM(wout.shape, wout.dtype), pltpu.SemaphoreType.DMA((3,))]
        + [pltpu.SemaphoreType.DMA((2, 4)) for _ in range(4)])
    return pl.pallas_call(
        body, name="in_proj_gather", grid_spec=grid_spec,
        out_shape=[jax.ShapeDtypeStruct((rows, 4 * nq), BF16), jax.ShapeDtypeStruct((D, 4 * nq), win.dtype),
                   jax.ShapeDtypeStruct((4 * dq, D2), wout.dtype)],
        compiler_params=_params(("arbitrary", "arbitrary")),
    )(q_arr, hl, win, wout)


def _rs_to_sibling(gs, axes):
    n = len(gs)
    shapes = []
    for g, ax in zip(gs, axes):
        s = list(g.shape)
        s[ax] //= 8
        shapes.append(tuple(s))

    def body(*refs):
        g_refs, mine, landed = refs[:n], refs[n:2 * n], refs[2 * n:3 * n]
        bufs = refs[3 * n:4 * n]
        lsem, ssem, rsem = refs[4 * n:]
        x, y, c = _pos()
        cps = []
        for o in range(n):
            size = shapes[o][axes[o]]
            for j in range(4):
                rc = pltpu.make_async_remote_copy(
                    src_ref=_chunk(g_refs[o], axes[o], 2 * j + 1 - c, size), dst_ref=landed[o].at[j],
                    send_sem=ssem.at[o, j], recv_sem=rsem.at[o, j], device_id=(x, y, 1 - c), device_id_type=MESH_ID)
                rc.start()
                cps.append(rc)
        for o in range(n):
            size = shapes[o][axes[o]]
            for j in range(4):
                _bounce(_chunk(g_refs[o], axes[o], 2 * j + c, size), mine[o].at[j], bufs[o], lsem.at[o])
        for rc in cps:
            rc.wait()

    hbm = pl.BlockSpec(memory_space=pl.ANY)
    outs = [jax.ShapeDtypeStruct((4,) + s, g.dtype) for s, g in zip(shapes, gs)]
    res = pl.pallas_call(
        body, name="rs_to_sibling", in_specs=[hbm] * n, out_specs=[hbm] * (2 * n), out_shape=outs + outs,
        scratch_shapes=[pltpu.VMEM(s, g.dtype) for s, g in zip(shapes, gs)]
        + [pltpu.SemaphoreType.DMA((n,)), pltpu.SemaphoreType.DMA((n, 4)), pltpu.SemaphoreType.DMA((n, 4))],
        compiler_params=_params(),
    )(*gs)
    return res[:n], res[n:]


def _rs_chips_side(parts):
    n = len(parts)

    def copies(p_refs, slots, scr):
        ssem, rsem = scr[n + 1], scr[n + 2]
        x, y, c = _pos()
        cps = []
        for o in range(n):
            for k in range(1, 4):
                px, py = _flip(x, (k >> 1) & 1), _flip(y, k & 1)
                cps.append(pltpu.make_async_remote_copy(
                    src_ref=p_refs[o].at[2 * px + py], dst_ref=slots[o].at[k], send_sem=ssem.at[o, k],
                    recv_sem=rsem.at[o, k], device_id=(px, py, c), device_id_type=MESH_ID))
        return cps

    def start(p_refs, slots, scr):
        for cp in copies(p_refs, slots, scr):
            cp.start()

    def finish(p_refs, slots, scr):
        x, y, _ = _pos()
        q = 2 * x + y
        for o in range(n):
            _bounce(p_refs[o].at[q], slots[o].at[0], scr[o], scr[n].at[o])
        for cp in copies(p_refs, slots, scr):
            cp.wait()

    return dict(
        ins=list(parts), outs=[jax.ShapeDtypeStruct(p.shape, p.dtype) for p in parts],
        scratch=[pltpu.VMEM(p.shape[1:], p.dtype) for p in parts]
        + [pltpu.SemaphoreType.DMA((n,)), pltpu.SemaphoreType.DMA((n, 4)), pltpu.SemaphoreType.DMA((n, 4))],
        start=start, finish=finish)


def _rs_share(rs, axes):
    n = len(rs)
    shapes = []
    for r, ax in zip(rs, axes):
        s = list(r.shape)
        s[ax] *= 2
        shapes.append(tuple(s))

    def body(*refs):
        r_refs, outs = refs[:n], refs[n:2 * n]
        bufs = refs[2 * n:3 * n]
        lsem, ssem, rsem = refs[3 * n:]
        x, y, c = _pos()
        cps = []
        for o in range(n):
            size = r_refs[o].shape[axes[o]]
            window = _chunk(outs[o], axes[o], c, size)
            rc = pltpu.make_async_remote_copy(src_ref=r_refs[o], dst_ref=window, send_sem=ssem.at[o], recv_sem=rsem.at[o],
                                              device_id=(x, y, 1 - c), device_id_type=MESH_ID)
            rc.start()
            cps.append(rc)
        for o in range(n):
            size = r_refs[o].shape[axes[o]]
            _bounce(r_refs[o], _chunk(outs[o], axes[o], c, size), bufs[o], lsem.at[o])
        for cp in cps:
            cp.wait()

    hbm = pl.BlockSpec(memory_space=pl.ANY)
    return pl.pallas_call(
        body, name="rs_share", in_specs=[hbm] * n, out_specs=[hbm] * n,
        out_shape=[jax.ShapeDtypeStruct(s, r.dtype) for s, r in zip(shapes, rs)],
        scratch_shapes=[pltpu.VMEM(r.shape, r.dtype) for r in rs] + [pltpu.SemaphoreType.DMA((n,)) for _ in range(3)],
        compiler_params=_params(),
    )(*rs)


def _rs_pair_sums(gs, axes):
    mine, landed = _rs_to_sibling(gs, axes)
    pair_sums = []
    for o, (mi, la) in enumerate(zip(mine, landed)):
        rows, cols = mi.shape[0] * mi.shape[1], mi.shape[2]
        s = _elementwise(lambda a, b: a.astype(F32) + b.astype(F32), [(mi.reshape(rows, cols), 0), (la.reshape(rows, cols), 0)],
                         [BF16], rows=rows, cols=cols, name=f"rs_pair_sum{o}")[0]
        pair_sums.append(s.reshape(mi.shape))
    return pair_sums


def _rs_finish(slots, axes):
    reduced = []
    for o, sl in enumerate(slots):
        rows, cols = sl.shape[1], sl.shape[2]
        flat = sl.reshape(4 * rows, cols)
        r = _elementwise(lambda a, b, c, d: (a.astype(F32) + b.astype(F32)) + (c.astype(F32) + d.astype(F32)),
                         [(flat, k * rows) for k in range(4)], [F32], rows=rows, cols=cols, name=f"rs_chip_sum{o}")[0]
        reduced.append(r)
    return _rs_share(reduced, axes)


def _norm_in(x, ctx, g, sc_l, sh_l, sc_c, sh_c, tr):
    L, D = x.shape
    T = ctx.shape[0]
    nx, nc = L // tr, T // tr

    def body(x_ref, c_ref, g_ref, scl, shl, scc, shc, o_ref):
        i = pl.program_id(0)

        def run(src, sc, sh):
            v = src[...]
            r = lax.rsqrt(jnp.mean(v * v, axis=-1, keepdims=True) + EPS)
            o_ref[...] = ((v * r * g_ref[...]) * (1.0 + sc[...]) + sh[...]).astype(o_ref.dtype)

        @pl.when(i < nx)
        def _():
            run(x_ref, scl, shl)

        @pl.when(i >= nx)
        def _():
            run(c_ref, scc, shc)

    vec = pl.BlockSpec((1, D), lambda i: (0, 0))
    return pl.pallas_call(
        body, name="norm_in", grid=(nx + nc,),
        in_specs=[pl.BlockSpec((tr, D), lambda i: (jnp.minimum(i, nx - 1), 0)),
                  pl.BlockSpec((tr, D), lambda i: (jnp.maximum(i - nx, 0), 0)), vec, vec, vec, vec, vec],
        out_specs=pl.BlockSpec((tr, D), lambda i: (i, 0)),
        out_shape=jax.ShapeDtypeStruct((L + T, D), BF16),
        compiler_params=_params(("arbitrary",)),
    )(x, ctx, g, sc_l, sh_l, sc_c, sh_c)


def _tmod(tl, row_w):
    assert row_w & (row_w - 1) == 0
    return lax.broadcasted_iota(jnp.int32, (tl, 1), 0) & (row_w - 1)


def _shift(z, k, tmod, row_w):
    tl = z.shape[0]
    rolled = pltpu.roll(z, k % tl, 0)
    mask = (tmod >= k) if k > 0 else (tmod < row_w + k)
    return jnp.where(mask, rolled, 0.0)


def _conv(z, w_ref, taps, left, tmod, row_w, lanes=slice(None)):
    out = None
    for j in range(taps):
        k = left - j
        term = (z if k == 0 else _shift(z, k, tmod, row_w)) * w_ref[j:j + 1, lanes]
        out = term if out is None else out + term
    return out


def _conv_bwd(dz, z, w_ref, taps, left, tmod, row_w, lanes=slice(None)):
    din = None
    dws = []
    for j in range(taps):
        k = left - j
        shifted = dz if k == 0 else _shift(dz, -k, tmod, row_w)
        term = shifted * w_ref[j:j + 1, lanes]
        din = term if din is None else din + term
        dws.append(jnp.sum(shifted * z, axis=0, keepdims=True))
    return din, dws


def _gate_matmul(xb16_ref, wd_ref, pre_scr, W, ng, gs):
    for g in range(ng):
        pg = jnp.dot(xb16_ref[:, g * gs:(g + 1) * gs], wd_ref[g], preferred_element_type=F32)
        pre_scr[:, g * gs:(g + 1) * gs] = pg[:, :gs]
        pre_scr[:, W + g * gs:W + (g + 1) * gs] = pg[:, gs:]


def _f32(ref, rows, lanes):
    return ref[rows, lanes].astype(F32)


def _sub_loop(tl, sub_r, W, fn):
    def chunk(ci, carry):
        r0 = pl.multiple_of(ci * sub_r, sub_r)
        for lb in range(W // LANES):
            fn(r0, lb * LANES)
        return carry

    lax.fori_loop(0, tl // sub_r, chunk, 0)


def _row_loop(tl, rev, step, init):
    nchunk = tl // SUBLANES

    def chunk(j, carry):
        jj = (nchunk - 1 - j) if rev else j
        c0 = pl.multiple_of(jj * SUBLANES, SUBLANES)
        for r in (range(SUBLANES - 1, -1, -1) if rev else range(SUBLANES)):
            carry = step(c0 + r, carry)
        return carry

    return lax.fori_loop(0, nchunk, chunk, init)


def _mix_fwd(P, d, h_init, wts, *, rows, row_off, row_w, tl, saved0=None, name):
    W = P.shape[1] // 6
    nt = rows // tl
    ob = row_off // tl
    rev = d == 1
    gs = min(LRU_GROUP, W)
    ng = W // gs
    wca, wcb, bcb = wts["wca"], wts["wcb"], wts["bcb"]
    wd, ba, bx, lam = wts["wd"][d], wts["ba"][d], wts["bx"][d], wts["lam"][d]

    def tile(i):
        return (nt - 1 - i) if rev else i

    def pcol(j):
        return pl.BlockSpec((tl, W), lambda i: (tile(i) + ob, j))

    vec = pl.BlockSpec((1, W), lambda i: (0, 0))
    taps = pl.BlockSpec((SUBLANES, W), lambda i: (0, 0))
    wd_spec = pl.BlockSpec(wd.shape, lambda i: (0, 0, 0))
    seq = pl.BlockSpec((tl, W), lambda i: (tile(i), 0))

    sub_r = min(row_w, tl)
    assert tl % sub_r == 0

    def body(*refs):
        if rev:
            (bl, cl, ul, gl, ql, ho, xb_r, xb16_r, wca_r, wd_r, ba_r, bx_r, lam_r, hin, hseq, cat, a_o, r_o, ig_o, m2_o,
             b_scr, pre_scr, carry, sp_scr) = refs
        else:
            (vl, wcb_r, bcb_r, wd_r, ba_r, bx_r, lam_r, hin, hseq, xb_r, xb16_r, a_o, r_o, ig_o, m2_o,
             b_scr, pre_scr, carry, sp_scr) = refs
        i = pl.program_id(0)

        @pl.when(i == 0)
        def _():
            carry[...] = hin[...]

        sp_scr[...] = _softplus(-lam_r[...])
        tmod = _tmod(sub_r, sub_r)

        def conv_in(r0, l0):
            rs, ls = pl.ds(r0, sub_r), pl.ds(l0, LANES)
            xb = _conv(_f32(vl, rs, ls), wcb_r, 4, 2, tmod, sub_r, ls) + bcb_r[:, ls]
            xb_r[rs, ls] = xb
            xb16_r[rs, ls] = xb.astype(BF16)

        def gates(r0, l0):
            rs, ls = pl.ds(r0, sub_r), pl.ds(l0, LANES)
            r, ig, a, m2 = _gates(pre_scr[rs, ls] + ba_r[:, ls], pre_scr[rs, pl.ds(W + l0, LANES)] + bx_r[:, ls],
                                  sp_scr[:, ls])
            a_o[rs, ls] = a
            r_o[rs, ls] = r.astype(r_o.dtype)
            ig_o[rs, ls] = ig.astype(ig_o.dtype)
            m2_o[rs, ls] = m2.astype(m2_o.dtype)
            m = jnp.where(m2 > 0.0, m2 * lax.rsqrt(m2), 0.0)
            b_scr[rs, ls] = m * (ig * xb_r[rs, ls])

        if not rev:
            _sub_loop(tl, sub_r, W, conv_in)
        _gate_matmul(xb16_r, wd_r, pre_scr, W, ng, gs)
        _sub_loop(tl, sub_r, W, gates)

        def step(t, h):
            h = a_o[pl.ds(t, 1), :] * h + b_scr[pl.ds(t, 1), :]
            hseq[pl.ds(t, 1), :] = h
            return h

        carry[...] = _row_loop(tl, rev, step, carry[...])

        if rev:
            def mix_out(r0, l0):
                rs, ls = pl.ds(r0, sub_r), pl.ds(l0, LANES)
                yb = (ho[rs, ls] + hseq[rs, ls]) * _silu(_f32(ql, rs, ls))
                ya = (_f32(bl, rs, ls) * _conv(_f32(cl, rs, ls) * _f32(ul, rs, ls), wca_r, 3, 1, tmod, sub_r, ls)
                      * _silu(_f32(gl, rs, ls)))
                cat[rs, ls] = ya.astype(cat.dtype)
                cat[rs, pl.ds(W + l0, LANES)] = yb.astype(cat.dtype)

            _sub_loop(tl, sub_r, W, mix_out)

    scratch = [pltpu.VMEM((tl, W), F32), pltpu.VMEM((tl, 2 * W), F32), pltpu.VMEM((1, W), F32), pltpu.VMEM((1, W), F32)]
    f32_seq = jax.ShapeDtypeStruct((rows, W), F32)
    kept_gates = [f32_seq] + [jax.ShapeDtypeStruct((rows, W), BF16)] * 3
    if rev:
        in_specs = [pcol(j) for j in (0, 1, 2, 3, 5)] + [seq, seq, seq, taps, wd_spec, vec, vec, vec, vec]
        args = [P] * 5 + [saved0["h"], saved0["xb"], saved0["xb16"], wca, wd, ba, bx, lam, h_init]
        out_specs = [seq, pl.BlockSpec((tl, 2 * W), lambda i: (tile(i), 0))] + [seq] * 4
        out_shape = [f32_seq, jax.ShapeDtypeStruct((rows, 2 * W), BF16)] + kept_gates
    else:
        in_specs = [pcol(4), taps, vec, wd_spec, vec, vec, vec, vec]
        args = [P, wcb, bcb, wd, ba, bx, lam, h_init]
        out_specs = [seq] * 7
        out_shape = [f32_seq, f32_seq, jax.ShapeDtypeStruct((rows, W), BF16)] + kept_gates
    res = pl.pallas_call(
        body, name=name, grid=(nt,), in_specs=in_specs, out_specs=out_specs, out_shape=out_shape,
        scratch_shapes=scratch, compiler_params=_params(("arbitrary",)),
    )(*args)
    gates = dict(zip(("a", "r", "ig", "m2"), res[-4:]))
    if rev:
        return res[0], res[1], gates
    return dict(h=res[0], xb=res[1], xb16=res[2]), gates


_BWD_SCRATCH = ("dyl", "g", "dp16", "sp", "dlf", "edge", "c")
_FWD_SAVED = ("xb", "xb16", "a", "r", "ig", "m2")


def _bwd_scratch(tl, W):
    shapes = {"dyl": pltpu.VMEM((tl, W), F32), "g": pltpu.VMEM((tl, W), F32), "dp16": pltpu.VMEM((tl, 2 * W), BF16),
              "sp": pltpu.VMEM((1, W), F32), "dlf": pltpu.VMEM((1, W), F32), "edge": pltpu.VMEM((1, W), F32),
              "c": pltpu.VMEM((1, W), F32)}
    return [shapes[n] for n in _BWD_SCRATCH]


def _lru_bwd_tile(d, dy_fn, hs_ref, wd_r, lam_r, scr, acc, first, tl, sub_r, ng, gs):
    dwd_ref, dba_ref, dbx_ref, dlam_ref = acc
    W = hs_ref.shape[1]
    assert gs % LANES == 0
    rev = d == 0
    lam = lam_r[...]
    scr["sp"][...] = _softplus(-lam)
    scr["dlf"][...] = -_sigmoid(-lam)

    @pl.when(first)
    def _():
        dwd_ref[...] = jnp.zeros_like(dwd_ref)
        dba_ref[...] = jnp.zeros_like(dba_ref)
        dbx_ref[...] = jnp.zeros_like(dbx_ref)
        dlam_ref[...] = jnp.zeros_like(dlam_ref)

    def state_grad(r0, l0):
        rs, ls = pl.ds(r0, sub_r), pl.ds(l0, LANES)
        scr["dyl"][rs, ls] = dy_fn(rs, ls)

    _sub_loop(tl, sub_r, W, state_grad)

    def step(t, c):
        g = scr["dyl"][pl.ds(t, 1), :] + c
        scr["g"][pl.ds(t, 1), :] = g
        return scr["a"][pl.ds(t, 1), :] * g

    scr["c"][...] = _row_loop(tl, rev, step, scr["c"][...])
    row = lax.broadcasted_iota(jnp.int32, (sub_r, 1), 0)

    def grads(r0, l0):
        rs, ls = pl.ds(r0, sub_r), pl.ds(l0, LANES)
        g, a, m2 = scr["g"][rs, ls], scr["a"][rs, ls], _f32(scr["m2"], rs, ls)
        r, ig, xb = _f32(scr["r"], rs, ls), _f32(scr["ig"], rs, ls), scr["xb"][rs, ls]
        h = hs_ref[rs, ls]
        if d == 0:
            e0 = pl.multiple_of(jnp.maximum(r0 - SUBLANES, 0), SUBLANES)
            edge = jnp.where(r0 == 0, scr["edge"][:, ls], hs_ref[pl.ds(e0, SUBLANES), ls][SUBLANES - 1:, :])
            hprev = jnp.where(row == 0, edge, pltpu.roll(h, 1, 0))
        else:
            e0 = pl.multiple_of(jnp.minimum(r0 + sub_r, tl - SUBLANES), SUBLANES)
            edge = jnp.where(r0 == tl - sub_r, scr["edge"][:, ls], hs_ref[pl.ds(e0, SUBLANES), ls][:1, :])
            hprev = jnp.where(row == sub_r - 1, edge, pltpu.roll(h, sub_r - 1, 0))
        rsq = lax.rsqrt(m2)
        gm = g * (m2 * rsq)
        d_la = (g * hprev) * a - (g * (ig * xb)) * ((1.0 - m2) * rsq)
        d_pr = d_la * ((-LRU_C) * scr["sp"][:, ls]) * (r * (1.0 - r))
        d_pi = (gm * xb) * (ig * (1.0 - ig))
        scr["dyl"][rs, ls] = gm * ig
        dlam_ref[:, ls] += jnp.sum(d_la * ((-LRU_C) * r), axis=0, keepdims=True) * scr["dlf"][:, ls]
        dba_ref[:, ls] += jnp.sum(d_pr, axis=0, keepdims=True)
        dbx_ref[:, ls] += jnp.sum(d_pi, axis=0, keepdims=True)
        gi, off = divmod(l0, gs)
        scr["dp16"][rs, pl.ds(gi * 2 * gs + off, LANES)] = d_pr.astype(BF16)
        scr["dp16"][rs, pl.ds(gi * 2 * gs + gs + off, LANES)] = d_pi.astype(BF16)

    _sub_loop(tl, sub_r, W, grads)
    for gi in range(ng):
        dp = scr["dp16"][:, gi * 2 * gs:(gi + 1) * 2 * gs]
        scr["g"][:, gi * gs:(gi + 1) * gs] = lax.dot_general(dp, wd_r[gi], (((1,), (1,)), ((), ())),
                                                             preferred_element_type=F32)
        dwd_ref[gi] += lax.dot_general(scr["xb16"][:, gi * gs:(gi + 1) * gs], dp, (((0,), (0,)), ((), ())),
                                       preferred_element_type=F32)


def _edge_block(h, tl, nt, d):
    W = h.shape[1]
    per = tl // SUBLANES
    if d == 0:
        return pl.BlockSpec((SUBLANES, W), lambda i: (jnp.maximum((nt - 1 - i) * per - 1, 0), 0))
    return pl.BlockSpec((SUBLANES, W), lambda i: (jnp.minimum((i + 1) * per, nt * per - 1), 0))


def _mix_bwd0(P, dcat, saved0, gates0, h_init, c_init, wts, *, rows, row_off, row_w, tl, name):
    W = P.shape[1] // 6
    nt = rows // tl
    ob = row_off // tl
    gs = min(LRU_GROUP, W)
    ng = W // gs
    wd, lam = wts["wd"][0], wts["lam"][0]
    h0s = saved0["h"]
    kept = [saved0["xb"], saved0["xb16"]] + [gates0[n] for n in ("a", "r", "ig", "m2")]

    def tile(i):
        return nt - 1 - i

    vec = pl.BlockSpec((1, W), lambda i: (0, 0))
    wd_spec = pl.BlockSpec(wd.shape, lambda i: (0, 0, 0))
    seq = pl.BlockSpec((tl, W), lambda i: (tile(i), 0))

    sub_r = min(row_w, tl)
    assert tl % sub_r == 0

    def body(ql, dyb, hs, hedge8, xb_r, xb16_r, a_r, r_r, ig_r, m2_r, wd_r, lam_r, hin, cin,
             dxb_o, dwd_o, dba_o, dbx_o, dlam_o, cfin, *scratch):
        scr = dict(zip(_BWD_SCRATCH, scratch, strict=True))
        scr.update(zip(_FWD_SAVED, (xb_r, xb16_r, a_r, r_r, ig_r, m2_r), strict=True))
        i = pl.program_id(0)

        @pl.when(i == 0)
        def _():
            scr["c"][...] = cin[...]

        scr["edge"][...] = jnp.where(i == nt - 1, hin[...], hedge8[SUBLANES - 1:SUBLANES, :])
        _lru_bwd_tile(0, lambda rs, ls: _f32(dyb, rs, ls) * _silu(_f32(ql, rs, ls)), hs, wd_r, lam_r, scr,
                      (dwd_o, dba_o, dbx_o, dlam_o), i == 0, tl, sub_r, ng, gs)
        dxb_o[...] = scr["dyl"][...] + scr["g"][...]
        cfin[...] = scr["c"][...]

    return pl.pallas_call(
        body, name=name, grid=(nt,),
        in_specs=[pl.BlockSpec((tl, W), lambda i: (tile(i) + ob, 5)), pl.BlockSpec((tl, W), lambda i: (tile(i), 1)), seq,
                  _edge_block(h0s, tl, nt, 0)] + [seq] * 6 + [wd_spec, vec, vec, vec],
        out_specs=[seq, wd_spec, vec, vec, vec, vec],
        out_shape=[jax.ShapeDtypeStruct((rows, W), F32), jax.ShapeDtypeStruct(wd.shape, F32)]
        + [jax.ShapeDtypeStruct((1, W), F32)] * 4,
        scratch_shapes=_bwd_scratch(tl, W),
        compiler_params=_params(("arbitrary",)),
    )(P, dcat, h0s, h0s, *kept, wd, lam, h_init, c_init)


def _mix_bwd1(P, dcat, saved0, h1s, gates1, dxb0, h_init, c_init, wts, *, rows, row_off, row_w, tl, name):
    W = P.shape[1] // 6
    nt = rows // tl
    ob = row_off // tl
    gs = min(LRU_GROUP, W)
    ng = W // gs
    wca, wcb = wts["wca"], wts["wcb"]
    wd, lam = wts["wd"][1], wts["lam"][1]
    h0s = saved0["h"]
    kept = [saved0["xb"], saved0["xb16"]] + [gates1[n] for n in ("a", "r", "ig", "m2")]

    vec = pl.BlockSpec((1, W), lambda i: (0, 0))
    taps = pl.BlockSpec((SUBLANES, W), lambda i: (0, 0))
    wd_spec = pl.BlockSpec(wd.shape, lambda i: (0, 0, 0))
    seq = pl.BlockSpec((tl, W), lambda i: (i, 0))

    sub_r = min(row_w, tl)
    assert tl % sub_r == 0

    def body(bl, cl, ul, gl, vl, ql, dya, dyb, h0, h1, hedge8, dx0, xb_r, xb16_r, a_r, r_r, ig_r, m2_r, wca_r, wcb_r,
             wd_r, lam_r, hin, cin, dp_o, dwd_o, dba_o, dbx_o, dlam_o, dwca_o, dwcb_o, dbcb_o, cfin, *scratch):
        scr = dict(zip(_BWD_SCRATCH, scratch, strict=True))
        scr.update(zip(_FWD_SAVED, (xb_r, xb16_r, a_r, r_r, ig_r, m2_r), strict=True))
        i = pl.program_id(0)

        @pl.when(i == 0)
        def _():
            scr["c"][...] = cin[...]
            dwca_o[...] = jnp.zeros_like(dwca_o)
            dwcb_o[...] = jnp.zeros_like(dwcb_o)
            dbcb_o[...] = jnp.zeros_like(dbcb_o)

        scr["edge"][...] = jnp.where(i == nt - 1, hin[...], hedge8[0:1, :])
        _lru_bwd_tile(1, lambda rs, ls: _f32(dyb, rs, ls) * _silu(_f32(ql, rs, ls)), h1, wd_r, lam_r, scr,
                      (dwd_o, dba_o, dbx_o, dlam_o), i == 0, tl, sub_r, ng, gs)
        cfin[...] = scr["c"][...]
        tmod = _tmod(sub_r, sub_r)

        def rest(r0, l0):
            rs, ls = pl.ds(r0, sub_r), pl.ds(l0, LANES)
            dxb = dx0[rs, ls] + scr["dyl"][rs, ls] + scr["g"][rs, ls]
            dv, dwb = _conv_bwd(dxb, _f32(vl, rs, ls), wcb_r, 4, 2, tmod, sub_r, ls)
            for j in range(4):
                dwcb_o[j:j + 1, ls] += dwb[j]
            dbcb_o[:, ls] += jnp.sum(dxb, axis=0, keepdims=True)
            q = _f32(ql, rs, ls)
            sq = _sigmoid(q)
            dq = _f32(dyb, rs, ls) * (h0[rs, ls] + h1[rs, ls]) * (sq * (1.0 + q * (1.0 - sq)))
            b_, c_, u_, g_ = _f32(bl, rs, ls), _f32(cl, rs, ls), _f32(ul, rs, ls), _f32(gl, rs, ls)
            z = c_ * u_
            cz = _conv(z, wca_r, 3, 1, tmod, sub_r, ls)
            sgm = _sigmoid(g_)
            sg = g_ * sgm
            da = _f32(dya, rs, ls)
            dz, dwa = _conv_bwd(da * b_ * sg, z, wca_r, 3, 1, tmod, sub_r, ls)
            for j in range(3):
                dwca_o[j:j + 1, ls] += dwa[j]
            parts = (da * cz * sg, dz * u_, dz * c_, da * b_ * cz * (sgm * (1.0 + g_ * (1.0 - sgm))), dv, dq)
            for k, val in enumerate(parts):
                dp_o[rs, pl.ds(k * W + l0, LANES)] = val.astype(dp_o.dtype)

        _sub_loop(tl, sub_r, W, rest)

    def pcol(j):
        return pl.BlockSpec((tl, W), lambda i: (i + ob, j))

    return pl.pallas_call(
        body, name=name, grid=(nt,),
        in_specs=[pcol(j) for j in range(6)]
        + [pl.BlockSpec((tl, W), lambda i: (i, 0)), pl.BlockSpec((tl, W), lambda i: (i, 1)), seq, seq,
           _edge_block(h1s, tl, nt, 1), seq] + [seq] * 6 + [taps, taps, wd_spec, vec, vec, vec],
        out_specs=[pl.BlockSpec((tl, 6 * W), lambda i: (i, 0)), wd_spec, vec, vec, vec, taps, taps, vec, vec],
        out_shape=[jax.ShapeDtypeStruct((rows, 6 * W), BF16), jax.ShapeDtypeStruct(wd.shape, F32)]
        + [jax.ShapeDtypeStruct((1, W), F32)] * 3
        + [jax.ShapeDtypeStruct((SUBLANES, W), F32)] * 2 + [jax.ShapeDtypeStruct((1, W), F32)] * 2,
        scratch_shapes=_bwd_scratch(tl, W),
        compiler_params=_params(("arbitrary",)),
    )(*([P] * 6), dcat, dcat, h0s, h1s, h1s, dxb0, *kept, wca, wcb, wd, lam, h_init, c_init)


def _loss_head(out, x, tgt, gt, fg, tr):
    L, D = x.shape

    def body(o_ref, x_ref, t_ref, gt_ref, fg_ref, dn_o, do_o, dfg_o, dgt_o, loss_o):
        i = pl.program_id(0)

        @pl.when(i == 0)
        def _():
            dfg_o[...] = jnp.zeros_like(dfg_o)
            dgt_o[...] = jnp.zeros_like(dgt_o)
            loss_o[...] = jnp.zeros_like(loss_o)

        o = o_ref[...].astype(F32)
        gt_v = gt_ref[...]
        fg_v = fg_ref[...]
        n = x_ref[...] + gt_v * o
        r = lax.rsqrt(jnp.mean(n * n, axis=-1, keepdims=True) + EPS)
        nr = n * r
        e = nr * fg_v - t_ref[...]
        loss_o[...] += 0.5 * jnp.sum(jnp.mean(e * e, axis=-1, keepdims=True))
        dy = e * (1.0 / D)
        dfg_o[...] += jnp.sum(dy * nr, axis=0, keepdims=True)
        qv = dy * fg_v
        dn = r * (qv - nr * jnp.mean(qv * nr, axis=-1, keepdims=True))
        dgt_o[...] += jnp.sum(dn * o, axis=0, keepdims=True)
        dn_o[...] = dn.astype(dn_o.dtype)
        do_o[...] = (dn * gt_v).astype(do_o.dtype)

    blk = pl.BlockSpec((tr, D), lambda i: (i, 0))
    vec = pl.BlockSpec((1, D), lambda i: (0, 0))
    return pl.pallas_call(
        body, name="loss_head", grid=(L // tr,), in_specs=[blk, blk, blk, vec, vec],
        out_specs=[blk, blk, vec, vec, pl.BlockSpec((SUBLANES, LANES), lambda i: (0, 0))],
        out_shape=[jax.ShapeDtypeStruct((L, D), BF16), jax.ShapeDtypeStruct((L, D), BF16),
                   jax.ShapeDtypeStruct((1, D), F32), jax.ShapeDtypeStruct((1, D), F32),
                   jax.ShapeDtypeStruct((SUBLANES, LANES), F32)],
        compiler_params=_params(("arbitrary",)),
    )(out, x, tgt, gt, fg)


def _norm_bwd(dhl, x, dn, g, sc, tr, name):
    L, D = x.shape
    with_x = dn is not None

    def body(*refs):
        if with_x:
            d_ref, x_ref, dn_ref, g_ref, sc_ref, gx_o, dsh_o, dsc_o, dg_o = refs
        else:
            d_ref, x_ref, g_ref, sc_ref, dsh_o, dsc_o, dg_o = refs
        i = pl.program_id(0)

        @pl.when(i == 0)
        def _():
            dsh_o[...] = jnp.zeros_like(dsh_o)
            dsc_o[...] = jnp.zeros_like(dsc_o)
            dg_o[...] = jnp.zeros_like(dg_o)

        d = d_ref[...].astype(F32)
        xv = x_ref[...]
        g_v = g_ref[...]
        r = lax.rsqrt(jnp.mean(xv * xv, axis=-1, keepdims=True) + EPS)
        xr = xv * r
        dsh_o[...] += jnp.sum(d, axis=0, keepdims=True)
        dsc_o[...] += jnp.sum(d * (xr * g_v), axis=0, keepdims=True)
        dxn = d * (1.0 + sc_ref[...])
        dg_o[...] += jnp.sum(dxn * xr, axis=0, keepdims=True)
        if with_x:
            qv = dxn * g_v
            gx_o[...] = r * (qv - xr * jnp.mean(qv * xr, axis=-1, keepdims=True)) + dn_ref[...].astype(F32)

    blk = pl.BlockSpec((tr, D), lambda i: (i, 0))
    vec = pl.BlockSpec((1, D), lambda i: (0, 0))
    vshape = jax.ShapeDtypeStruct((1, D), F32)
    res = pl.pallas_call(
        body, name=name, grid=(L // tr,),
        in_specs=[blk, blk] + ([blk] if with_x else []) + [vec, vec],
        out_specs=([blk] if with_x else []) + [vec, vec, vec],
        out_shape=([jax.ShapeDtypeStruct((L, D), F32)] if with_x else []) + [vshape] * 3,
        compiler_params=_params(("arbitrary",)),
    )(*([dhl, x] + ([dn] if with_x else []) + [g, sc]))
    return res if with_x else [None] + list(res)


def _pack_blockdiag(wa, wx, gs):
    H, hd, _ = wa.shape
    hp = gs // hd
    ng = H // hp
    eye = jnp.eye(hp, dtype=wa.dtype)

    def bd(w):
        return jnp.einsum("gpij,pq->gpiqj", w.reshape(ng, hp, hd, hd), eye).reshape(ng, gs, gs)

    return jnp.concatenate([bd(wa), bd(wx)], axis=-1).astype(BF16)


def _unpack_blockdiag(dwd, H, hd, gs):
    hp = gs // hd
    ng = H // hp
    eye = jnp.eye(hp, dtype=dwd.dtype)

    def diag(dm):
        return jnp.einsum("gpiqj,pq->gpij", dm.reshape(ng, hp, hd, hp, hd), eye).reshape(H, hd, hd)

    return diag(dwd[:, :, :gs]), diag(dwd[:, :, gs:])


def kernel(x, c, ctx, c_ctx, norm_g, w_ada, b_ada, w_in, w_conv_a, w_conv_b, b_conv_b, lru_wa, lru_ba, lru_wx, lru_bx, lru_lambda, w_out, final_g, loss_target, m_c_ctx, m_norm_g, m_w_ada, m_b_ada, m_w_in, m_w_conv_a, m_w_conv_b, m_b_conv_b, m_lru_wa, m_lru_ba, m_lru_wx, m_lru_bx, m_lru_lambda, m_w_out, m_final_g, v_c_ctx, v_norm_g, v_w_ada, v_b_ada, v_w_in, v_w_conv_a, v_w_conv_b, v_b_conv_b, v_lru_wa, v_lru_ba, v_lru_wx, v_lru_bx, v_lru_lambda, v_w_out, v_final_g):
    xi, yi, ci = _pos()
    me = 4 * xi + 2 * yi + ci
    q = 2 * xi + yi
    first_core = (ci == 0).astype(F32)

    L, D = x.shape[1], x.shape[2]
    T = ctx.shape[1]
    W = D // 2
    Wq = W // 4
    H, hd = lru_wa.shape[2], lru_wa.shape[3]
    gs = min(LRU_GROUP, W)
    nq = w_ada.shape[2]
    tl = min(256, T, L)
    tr = min(256, T, L)
    x2, ctx2, tgt2 = x[0], ctx[0], loss_target[0]

    def place(shard, full_cols):
        z = jnp.zeros((shard.shape[0], full_cols), F32)
        return lax.dynamic_update_slice(z, shard * first_core, (0, q * shard.shape[1]))

    c_rows = lax.dynamic_update_slice(jnp.zeros((8, D), F32), c, (me, 0))
    small_in = [c_rows, place(w_conv_a[0], W), place(w_conv_b[0], W), place(lru_ba[0], W), place(lru_bx[0], W),
                place(lru_lambda[0], W)]
    small_shapes = [a.shape for a in small_in]
    gathered = _allreduce8(_pack(small_in, 8 * SUBLANES), "gather_small")
    c_all, wca, wcb, ba_all, bx_all, lam_all = _unpack(gathered, small_shapes)

    s_rows = jnp.concatenate([c_all, c_ctx[None, :], jnp.zeros((7, D), F32)], axis=0)
    mod_part = _matmul(s_rows, w_ada[0], a_act="silu", bias=lax.dynamic_slice(b_ada, (0, q * nq), (1, nq)),
                       tm=16, tn=nq, tk=512, name="ada_fwd")
    mod_all = _allreduce8(_pack([place(mod_part, 4 * nq)], 8 * SUBLANES), "gather_mod")
    mod_all = _unpack(mod_all, [(16, 4 * nq)])[0]
    mod_l = lax.dynamic_slice(mod_all, (me, 0), (1, 3 * D))
    mod_c = mod_all[8:9]
    sh_l, sc_l, gt_l = mod_l[:, :D], mod_l[:, D:2 * D], mod_l[:, 2 * D:]
    sh_c, sc_c = mod_c[:, :D], mod_c[:, D:2 * D]

    pad_taps = lambda w: jnp.pad(w, ((0, SUBLANES - w.shape[0]), (0, 0)))
    wts = {
        "wca": pad_taps(wca), "wcb": pad_taps(wcb), "bcb": b_conv_b,
        "wd": [_pack_blockdiag(lru_wa[0, d], lru_wx[0, d], gs) for d in range(2)],
        "ba": [ba_all[d:d + 1] for d in range(2)], "bx": [bx_all[d:d + 1] for d in range(2)],
        "lam": [lam_all[d:d + 1] for d in range(2)],
    }

    hl = _norm_in(x2, ctx2, norm_g, sc_l, sh_l, sc_c, sh_c, tr)
    p_lat, win_full, wout_full = _in_proj_gather(hl, w_in[0].astype(BF16), w_out[0].astype(BF16),
                                                 jnp.reshape(q, (1,)).astype(jnp.int32), rows=L, tm=min(1024, L))
    p_ctx = _matmul(hl, win_full, a_rows=T, a_off=L, tm=T, tn=1536, tk=D, out_dtype=BF16, name="in_proj_ctx")
    zero_w = jnp.zeros((1, W), F32)
    ctx0, cgates0 = _mix_fwd(p_ctx, 0, zero_w, wts, rows=T, row_off=0, row_w=T, tl=tl, name="ctx_fwd0")
    c1s, _, cgates1 = _mix_fwd(p_ctx, 1, zero_w, wts, rows=T, row_off=0, row_w=T, tl=tl, saved0=ctx0, name="ctx_fwd1")
    h0_init, h1_init = ctx0["h"][T - 1:T], c1s[0:1]
    lat0, gates0 = _mix_fwd(p_lat, 0, h0_init, wts, rows=L, row_off=0, row_w=GRID_W, tl=tl, name="mix_fwd0")
    h1s, cat, gates1 = _mix_fwd(p_lat, 1, h1_init, wts, rows=L, row_off=0, row_w=GRID_W, tl=tl, saved0=lat0,
                                name="mix_fwd1")
    out = _matmul(cat, wout_full, tm=512, tn=D, tk=2 * W, out_dtype=BF16, name="out_proj")
    dn, dout, dfg, dgt, loss_blk = _loss_head(out, x2, tgt2, gt_l, final_g[None, :], tr)

    dcat = _matmul(dout, wout_full, tb=True, tm=512, tn=2 * W, tk=D, out_dtype=BF16, name="out_proj_bwd")
    gw_out = _matmul(cat, dout, ta=True, tm=1024, tn=D, tk=2048, out_dtype=BF16, name="w_out_grad")
    dxb0, dwd0, dba0, dbx0, dlam0, ch0 = _mix_bwd0(p_lat, dcat, lat0, gates0, h0_init, zero_w, wts, rows=L, row_off=0,
                                                   row_w=GRID_W, tl=tl, name="mix_bwd0")
    dp, dwd1, dba1, dbx1, dlam1, dwca, dwcb, dbcb, ch1 = _mix_bwd1(
        p_lat, dcat, lat0, h1s, gates1, dxb0, h1_init, zero_w, wts, rows=L, row_off=0, row_w=GRID_W, tl=tl,
        name="mix_bwd1")
    zero_cat = jnp.zeros((T, 2 * W), BF16)
    cxb0, cwd0, cba0, cbx0, clam0, _ = _mix_bwd0(p_ctx, zero_cat, ctx0, cgates0, zero_w, ch0, wts, rows=T, row_off=0,
                                                 row_w=T, tl=tl, name="ctx_bwd0")
    dp_c, cwd1, cba1, cbx1, clam1, cwca, cwcb, cbcb, _ = _mix_bwd1(
        p_ctx, zero_cat, ctx0, c1s, cgates1, cxb0, zero_w, ch1, wts, rows=T, row_off=0, row_w=T, tl=tl, name="ctx_bwd1")

    gw_in_ctx = _matmul(hl, dp_c, ta=True, a_rows=T, a_off=L, tm=1024, tn=1536, tk=T, name="w_in_grad_ctx")
    gw_in = _matmul(hl, dp, ta=True, a_rows=L, tm=1024, tn=1536, tk=2048, init=gw_in_ctx, out_dtype=BF16,
                    name="w_in_grad")
    rs_axes = [1, 0]
    pair_sums = _rs_pair_sums([gw_in, gw_out], rs_axes)
    dhl, rs_slots = _matmul(dp, win_full, tb=True, tm=512, tn=D, tk=3072, out_dtype=BF16, name="in_proj_bwd",
                            side=_rs_chips_side(pair_sums))
    dhc = _matmul(dp_c, win_full, tb=True, tm=T, tn=D, tk=512, name="in_proj_bwd_ctx")
    gx, dsh_l, dsc_l, dng_l = _norm_bwd(dhl, x2, dn, norm_g, sc_l, tr, "norm_bwd")
    _, dsh_c, dsc_c, dng_c = _norm_bwd(dhc, ctx2, None, norm_g, sc_c, tr, "norm_bwd_ctx")

    g_in_shard, g_out_shard = _rs_finish(rs_slots, rs_axes)

    dwa0, dwx0 = _unpack_blockdiag(dwd0 + cwd0, H, hd, gs)
    dwa1, dwx1 = _unpack_blockdiag(dwd1 + cwd1, H, hd, gs)
    zeros_d = jnp.zeros((1, D), F32)
    dmod_l = jnp.concatenate([dsh_l, dsc_l, dgt], axis=1)
    dmod_c = jnp.concatenate([dsh_c, dsc_c, zeros_d], axis=1)
    small_g = [
        lax.dynamic_update_slice(jnp.zeros((8, 3 * D), F32), dmod_l, (me, 0)), dmod_c,
        dfg, dng_l + dng_c, (dwca + cwca)[:3], (dwcb + cwcb)[:4], dbcb + cbcb,
        jnp.stack([dwa0, dwa1]), jnp.stack([dwx0, dwx1]),
        jnp.concatenate([dba0 + cba0, dba1 + cba1], axis=0), jnp.concatenate([dbx0 + cbx0, dbx1 + cbx1], axis=0),
        jnp.concatenate([dlam0 + clam0, dlam1 + clam1], axis=0),
    ]
    g_shapes = [a.shape for a in small_g]
    (g_rows, g_modc, g_fg, g_ng, g_wca, g_wcb, g_bcb, g_wa, g_wx, g_ba, g_bx, g_lam) = _unpack(
        _allreduce8(_pack(small_g, 8 * SUBLANES), "reduce_small"), g_shapes)

    g_mod = jnp.concatenate([g_rows, g_modc, jnp.zeros((7, 3 * D), F32)], axis=0)
    g_mod_q = lax.dynamic_slice(g_mod, (0, q * nq), (16, nq))
    g_w_ada = _matmul(s_rows, g_mod_q, ta=True, a_act="silu", tm=1024, tn=nq, tk=16, name="w_ada_grad")
    g_b_ada = jnp.sum(g_mod[:9], axis=0, keepdims=True)
    gc_part = _matmul(jnp.pad(lax.dynamic_slice(g_modc, (0, q * nq), (1, nq)), ((0, 7), (0, 0))), w_ada[0], tb=True,
                      dsilu_mul=c_ctx[None, :], tm=8, tn=D, tk=512, name="c_ctx_grad")
    g_c_ctx = _unpack(_allreduce8(_pack([gc_part[0:1] * first_core], 8 * SUBLANES), "reduce_c_ctx"), [(D,)])[0]

    def shard_cols(a, width):
        return lax.dynamic_slice(a, (0, q * width), (a.shape[0], width))

    grads = {
        "c_ctx": g_c_ctx, "norm_g": g_ng, "b_ada": g_b_ada,
        "w_conv_a": shard_cols(g_wca, Wq)[None], "w_conv_b": shard_cols(g_wcb, Wq)[None], "b_conv_b": g_bcb,
        "lru_wa": g_wa[None], "lru_ba": shard_cols(g_ba, Wq)[None], "lru_wx": g_wx[None],
        "lru_bx": shard_cols(g_bx, Wq)[None], "lru_lambda": shard_cols(g_lam, Wq)[None], "final_g": g_fg[0],
    }
    small_names = list(grads)
    given = dict(c_ctx=(c_ctx, m_c_ctx, v_c_ctx), norm_g=(norm_g, m_norm_g, v_norm_g), b_ada=(b_ada, m_b_ada, v_b_ada),
                 w_conv_a=(w_conv_a, m_w_conv_a, v_w_conv_a), w_conv_b=(w_conv_b, m_w_conv_b, v_w_conv_b),
                 b_conv_b=(b_conv_b, m_b_conv_b, v_b_conv_b), lru_wa=(lru_wa, m_lru_wa, v_lru_wa),
                 lru_ba=(lru_ba, m_lru_ba, v_lru_ba), lru_wx=(lru_wx, m_lru_wx, v_lru_wx),
                 lru_bx=(lru_bx, m_lru_bx, v_lru_bx), lru_lambda=(lru_lambda, m_lru_lambda, v_lru_lambda),
                 final_g=(final_g, m_final_g, v_final_g))
    shapes = [given[n][0].shape for n in small_names]
    packed = [_pack([given[n][j] for n in small_names], 2 * SUBLANES) for j in range(3)]
    packed_g = _pack([grads[n] for n in small_names], 2 * SUBLANES)
    sd, sm, sv = _adam(packed[0], packed_g, packed[1], packed[2], "adam_small")
    delta_s = dict(zip(small_names, _unpack(sd, shapes)))
    newm_s = dict(zip(small_names, _unpack(sm, shapes)))
    newv_s = dict(zip(small_names, _unpack(sv, shapes)))
    grads = {n: grads[n].reshape(given[n][0].shape) for n in small_names}

    big = {"w_ada": (w_ada, g_w_ada, m_w_ada, v_w_ada), "w_in": (w_in, g_in_shard, m_w_in, v_w_in),
           "w_out": (w_out, g_out_shard, m_w_out, v_w_out)}
    delta_b, newm_b, newv_b = {}, {}, {}
    for n, (w, g, m, v) in big.items():
        grads[n] = g[None]
        d_, m_, v_ = _adam(w[0], g, m[0], v[0], "adam_" + n)
        delta_b[n], newm_b[n], newv_b[n] = d_[None], m_[None], v_[None]

    loss = lax.psum(loss_blk[0, 0], AXES)
    order = ["c_ctx", "norm_g", "w_ada", "b_ada", "w_in", "w_conv_a", "w_conv_b", "b_conv_b", "lru_wa", "lru_ba",
             "lru_wx", "lru_bx", "lru_lambda", "w_out", "final_g"]
    delta = {**delta_s, **delta_b}
    newm = {**newm_s, **newm_b}
    newv = {**newv_s, **newv_b}
    return (loss, gx[None], *[grads[n] for n in order], *[delta[n] for n in order], *[newm[n] for n in order],
            *[newv[n] for n in order])
```

```python
import functools

import jax
import jax.numpy as jnp
from jax import lax
from jax.experimental import pallas as pl
from jax.experimental.pallas import tpu as pltpu

F32 = jnp.float32
BF16 = jnp.bfloat16
MESH_ID = pl.DeviceIdType.MESH
AXES = ("x", "y", "c")

EPS = 1e-6
LRU_C = 8.0
GRID_W = 64
ADAM_LR = 0.001
ADAM_B1 = 0.9
ADAM_B2 = 0.999
ADAM_EPS = 1e-08
ADAM_WD = 0.01
ADAM_STEP = 10

LANES = 128
SUBLANES = 8
PACK_COLS = 1024
VMEM_LIMIT = 56 * 2**20
LRU_GROUP = 256


def _params(sem=None):
    return pltpu.CompilerParams(vmem_limit_bytes=VMEM_LIMIT, dimension_semantics=sem)


def _pick(dim, pref, quantum=LANES):
    if dim <= pref:
        return dim
    best = None
    for t in range(quantum, pref + 1, quantum):
        if dim % t == 0:
            best = t
    assert best is not None, (dim, pref)
    return best


def _pos():
    return lax.axis_index("x"), lax.axis_index("y"), lax.axis_index("c")


def _flip(v, bit):
    return 1 - v if bit else v


def _sigmoid(v):
    return 0.5 * jnp.tanh(0.5 * v) + 0.5


def _silu(v):
    return v * _sigmoid(v)


def _dsilu(v):
    s = _sigmoid(v)
    return s * (1.0 + v * (1.0 - s))


def _gates(pre_r, pre_i, sp):
    r = _sigmoid(pre_r)
    ig = _sigmoid(pre_i)
    e = LRU_C * r * sp
    w = jnp.tanh(e)
    return r, ig, jnp.exp(-e), (2.0 * w) * pl.reciprocal(1.0 + w, approx=True)


def _softplus(z):
    return jnp.maximum(z, 0.0) + jnp.log1p(jnp.exp(-jnp.abs(z)))


def _matmul(a, b, *, ta=False, tb=False, tm=512, tn=512, tk=512, out_dtype=F32, name,
            a_rows=None, a_off=0, a_act=None, init=None, bias=None, dsilu_mul=None, side=None):
    rows_a = a.shape[0] if a_rows is None else a_rows
    if ta:
        K, M = rows_a, a.shape[1]
    else:
        M, K = rows_a, a.shape[1]
    N = b.shape[0] if tb else b.shape[1]
    tm, tn, tk = _pick(M, tm, SUBLANES), _pick(N, tn), _pick(K, tk)
    t_rows = tk if ta else tm
    assert a_off % t_rows == 0
    nk = K // tk
    gi, gj = M // tm, N // tn
    off_blocks = a_off // t_rows
    dims = (((0 if ta else 1,), (1 if tb else 0,)), ((), ()))
    extras = [e for e in (init, bias, dsilu_mul) if e is not None]
    n_sin = len(side["ins"]) if side else 0
    n_sout = len(side["outs"]) if side else 0

    def body(a_ref, b_ref, *rest):
        rest = list(rest)
        init_ref = rest.pop(0) if init is not None else None
        bias_ref = rest.pop(0) if bias is not None else None
        dsm_ref = rest.pop(0) if dsilu_mul is not None else None
        side_in = [rest.pop(0) for _ in range(n_sin)]
        o_ref = rest.pop(0)
        side_out = [rest.pop(0) for _ in range(n_sout)]
        acc_ref = rest.pop(0) if nk > 1 else None
        side_scr = rest
        i, j, k = pl.program_id(0), pl.program_id(1), pl.program_id(2)

        if side:
            @pl.when((i == 0) & (j == 0) & (k == 0))
            def _():
                side["start"](side_in, side_out, side_scr)

        av = a_ref[...]
        if a_act == "silu":
            av = _silu(av)
        prod = lax.dot_general(av, b_ref[...], dims, preferred_element_type=F32)

        def finish(r):
            if bias_ref is not None:
                r = r + bias_ref[...]
            if dsm_ref is not None:
                r = r * _dsilu(dsm_ref[...])
            o_ref[...] = r.astype(o_ref.dtype)

        if nk == 1:
            finish(prod if init_ref is None else prod + init_ref[...])
        else:
            @pl.when(k == 0)
            def _():
                acc_ref[...] = prod if init_ref is None else prod + init_ref[...]

            @pl.when(k > 0)
            def _():
                acc_ref[...] += prod

            @pl.when(k == nk - 1)
            def _():
                finish(acc_ref[...])

        if side:
            @pl.when((i == gi - 1) & (j == gj - 1) & (k == nk - 1))
            def _():
                side["finish"](side_in, side_out, side_scr)

    if ta:
        a_spec = pl.BlockSpec((tk, tm), lambda i, j, k: (k + off_blocks, i))
    else:
        a_spec = pl.BlockSpec((tm, tk), lambda i, j, k: (i + off_blocks, k))
    if tb:
        b_spec = pl.BlockSpec((tn, tk), lambda i, j, k: (j, k))
    else:
        b_spec = pl.BlockSpec((tk, tn), lambda i, j, k: (k, j))
    in_specs = [a_spec, b_spec]
    if init is not None:
        in_specs.append(pl.BlockSpec((tm, tn), lambda i, j, k: (i, j)))
    if bias is not None:
        in_specs.append(pl.BlockSpec((1, tn), lambda i, j, k: (0, j)))
    if dsilu_mul is not None:
        in_specs.append(pl.BlockSpec((1, tn), lambda i, j, k: (0, j)))
    hbm = pl.BlockSpec(memory_space=pl.ANY)
    res = pl.pallas_call(
        body, name=name, grid=(gi, gj, nk),
        in_specs=in_specs + [hbm] * n_sin,
        out_specs=[pl.BlockSpec((tm, tn), lambda i, j, k: (i, j))] + [hbm] * n_sout,
        out_shape=[jax.ShapeDtypeStruct((M, N), out_dtype)] + (list(side["outs"]) if side else []),
        scratch_shapes=([pltpu.VMEM((tm, tn), F32)] if nk > 1 else []) + (list(side["scratch"]) if side else []),
        compiler_params=_params(("arbitrary",) * 3 if side else ("parallel", "parallel", "arbitrary")),
    )(a, b, *extras, *(side["ins"] if side else []))
    return (res[0], res[1:]) if side else res[0]


def _elementwise(fn, ins, outs, *, rows, cols, name, tr=256):
    tr = _pick(rows, tr, 2 * SUBLANES)
    n_in = len(ins)

    def body(*refs):
        vals = fn(*[r[...] for r in refs[:n_in]])
        if not isinstance(vals, (tuple, list)):
            vals = (vals,)
        for r, v in zip(refs[n_in:], vals, strict=True):
            r[...] = v.astype(r.dtype)

    def spec(off):
        assert off % tr == 0
        ob = off // tr
        return pl.BlockSpec((tr, cols), lambda i: (i + ob, 0))

    res = pl.pallas_call(
        body, name=name, grid=(rows // tr,),
        in_specs=[spec(off) for _, off in ins],
        out_specs=[spec(0) for _ in outs],
        out_shape=[jax.ShapeDtypeStruct((rows, cols), dt) for dt in outs],
        compiler_params=_params(("parallel",)),
    )(*[a for a, _ in ins])
    return res


def _adam_math(w, g, m, v):
    m = ADAM_B1 * m + (1.0 - ADAM_B1) * g
    v = ADAM_B2 * v + (1.0 - ADAM_B2) * (g * g)
    m_hat = m / (1.0 - ADAM_B1 ** ADAM_STEP)
    v_hat = v / (1.0 - ADAM_B2 ** ADAM_STEP)
    delta = -ADAM_LR * (m_hat / (jnp.sqrt(v_hat) + ADAM_EPS) + ADAM_WD * w)
    return delta, m, v


def _adam(w, g, m, v, name):
    rows, cols = w.shape
    return _elementwise(_adam_math, [(w, 0), (g, 0), (m, 0), (v, 0)], [F32, F32, F32],
                        rows=rows, cols=cols, name=name)


def _pack(arrs, row_quantum):
    flat = jnp.concatenate([a.reshape(-1).astype(F32) for a in arrs])
    n = flat.shape[0]
    q = row_quantum * PACK_COLS
    total = -(-n // q) * q
    flat = jnp.pad(flat, (0, total - n))
    return flat.reshape(total // PACK_COLS, PACK_COLS)


def _unpack(buf, shapes):
    flat = buf.reshape(-1)
    out, off = [], 0
    for s in shapes:
        n = 1
        for d in s:
            n *= d
        out.append(flat[off:off + n].reshape(s))
        off += n
    return out


def _allreduce8(buf, name):
    R, C = buf.shape
    assert R % (8 * SUBLANES) == 0
    m = R // 8

    def body(x_ref, o_ref, recv, red, s1, r1, s2, r2):
        x, y, c = _pos()
        me = 4 * x + 2 * y + c

        def peer(k):
            px, py, pc = _flip(x, (k >> 2) & 1), _flip(y, (k >> 1) & 1), _flip(c, k & 1)
            return (px, py, pc), 4 * px + 2 * py + pc

        def rows(ref, idx):
            return ref.at[pl.ds(pl.multiple_of(idx * m, SUBLANES), m), :]

        def scatter(k):
            dev, p = peer(k)
            return pltpu.make_async_remote_copy(src_ref=rows(x_ref, p), dst_ref=recv.at[k], send_sem=s1.at[k],
                                                recv_sem=r1.at[k], device_id=dev, device_id_type=MESH_ID)

        def share(k):
            dev, p = peer(k)
            return pltpu.make_async_remote_copy(src_ref=red, dst_ref=rows(o_ref, me), send_sem=s2.at[k],
                                                recv_sem=r2.at[k], device_id=dev, device_id_type=MESH_ID)

        def shared_from(k):
            dev, p = peer(k)
            return pltpu.make_async_remote_copy(src_ref=red, dst_ref=rows(o_ref, p), send_sem=s2.at[k],
                                                recv_sem=r2.at[k], device_id=dev, device_id_type=MESH_ID)

        for k in range(1, 8):
            scatter(k).start()
        acc = rows(x_ref, me)[...]
        for k in range(1, 8):
            scatter(k).wait_recv()
            acc = acc + recv[k]
        red[...] = acc
        rows(o_ref, me)[...] = acc
        for k in range(1, 8):
            share(k).start()
        for k in range(1, 8):
            shared_from(k).wait_recv()
        for k in range(1, 8):
            scatter(k).wait_send()
            share(k).wait_send()

    return pl.pallas_call(
        body, name=name,
        in_specs=[pl.BlockSpec(memory_space=pltpu.VMEM)],
        out_specs=pl.BlockSpec(memory_space=pltpu.VMEM),
        out_shape=jax.ShapeDtypeStruct((R, C), F32),
        scratch_shapes=[pltpu.VMEM((8, m, C), F32), pltpu.VMEM((m, C), F32),
                        pltpu.SemaphoreType.DMA((8,)), pltpu.SemaphoreType.DMA((8,)),
                        pltpu.SemaphoreType.DMA((8,)), pltpu.SemaphoreType.DMA((8,))],
        compiler_params=_params(),
    )(buf)


def _bounce(src, dst, buf, sem):
    cin = pltpu.make_async_copy(src, buf, sem)
    cin.start()
    cin.wait()
    cout = pltpu.make_async_copy(buf, dst, sem)
    cout.start()
    cout.wait()


def _chunk(ref, axis, idx, size):
    start = idx * size
    if axis == 0:
        return ref.at[pl.ds(start, size), :]
    return ref.at[:, pl.ds(start, size)]


def _in_proj_gather(hl, win, wout, q_arr, *, rows, tm):
    D, nq = win.shape
    dq, D2 = wout.shape
    ni = rows // tm
    ops = ((0, 1, nq, D // 2), (1, 0, dq, dq // 2))

    def body(q_ref, a_ref, win_ref, wout_ref, p_ref, gin_ref, gout_ref, b_scr, buf_out, lsem, ssem, rsem, fsem, gsem):
        j, i = pl.program_id(0), pl.program_id(1)
        x, y, c = _pos()
        q = 2 * x + y
        srcs = (win_ref, wout_ref)
        dsts = (gin_ref, gout_ref)

        def shard_window(o, chip):
            _, axis, size, _ = ops[o]
            return _chunk(dsts[o], axis, chip, size)

        def half(ref, o, core):
            return ref.at[pl.ds(core * ops[o][3], ops[o][3]), :]

        def half_window(o, chip, core):
            _, axis, size, hs = ops[o]
            if axis == 1:
                return dsts[o].at[pl.ds(core * hs, hs), pl.ds(chip * size, size)]
            return dsts[o].at[pl.ds(chip * size + core * hs, hs), :]

        def chip_of(k):
            px, py = _flip(x, (k >> 1) & 1), _flip(y, k & 1)
            return px, py, 2 * px + py

        def send(o, k):
            px, py, _ = chip_of(k)
            return pltpu.make_async_remote_copy(
                src_ref=half(srcs[o], o, c), dst_ref=half_window(o, q, c), send_sem=ssem.at[o, k],
                recv_sem=rsem.at[o, k], device_id=(px, py, c), device_id_type=MESH_ID)

        def chip_recv(o, k):
            px, py, pq = chip_of(k)
            landed = half_window(o, pq, c)
            pltpu.make_async_remote_copy(src_ref=landed, dst_ref=landed, send_sem=ssem.at[o, k], recv_sem=rsem.at[o, k],
                                         device_id=(px, py, c), device_id_type=MESH_ID).wait_recv()

        def to_sibling(o, k):
            landed = half_window(o, chip_of(k)[2], c)
            return pltpu.make_async_remote_copy(src_ref=landed, dst_ref=landed, send_sem=fsem.at[o, k],
                                                recv_sem=gsem.at[o, k], device_id=(x, y, 1 - c), device_id_type=MESH_ID)

        def from_sibling(o, k):
            theirs = half_window(o, chip_of(k)[2], 1 - c)
            pltpu.make_async_remote_copy(src_ref=theirs, dst_ref=theirs, send_sem=fsem.at[o, k], recv_sem=gsem.at[o, k],
                                         device_id=(x, y, 1 - c), device_id_type=MESH_ID).wait_recv()

        def relay(o, core):
            if core == 0:
                landed, target = half_window(o, chip_of(2)[2], 0), (x, 1 - y, 0)
            else:
                landed, target = half_window(o, chip_of(1)[2], 1), (1 - x, y, 1)
            return pltpu.make_async_remote_copy(src_ref=landed, dst_ref=landed, send_sem=ssem.at[o, 3],
                                                recv_sem=rsem.at[o, 3], device_id=target, device_id_type=MESH_ID)

        def on_core(core, fn):
            @pl.when(c == core)
            def _():
                fn()

        def land(o, k):
            chip_recv(o, k)
            if k == 2:
                on_core(0, lambda: relay(o, 0).start())
            if k == 1:
                on_core(1, lambda: relay(o, 1).start())
            to_sibling(o, k).start()

        def settle(o, k):
            from_sibling(o, k)
            to_sibling(o, k).wait_send()

        def b_load(k, slot):
            src = win_ref if k == 0 else shard_window(0, chip_of(k)[2])
            return pltpu.make_async_copy(src, b_scr.at[slot], lsem.at[0])

        def own_store():
            return pltpu.make_async_copy(b_scr.at[0], shard_window(0, q), lsem.at[2])

        order = (0, 2, 1, 3)
        early = max(ni - 2, 0)

        @pl.when((j == 0) & (i == 0))
        def _():
            for o in range(2):
                for k in (2, 1):
                    send(o, k).start()
            first = b_load(0, 0)
            first.start()
            first.wait()
            own_store().start()
            _bounce(wout_ref, shard_window(1, q), buf_out, lsem.at[1])

        for jj in range(3):
            nxt = order[jj + 1]

            @pl.when((j == jj) & (i == early))
            def _(nxt=nxt):
                land(0, nxt)

            @pl.when((j == jj) & (i == ni - 1))
            def _(jj=jj, nxt=nxt):
                settle(0, nxt)
                if jj == 1:
                    own_store().wait()
                b_load(nxt, (jj + 1) % 2).start()

            @pl.when((j == jj + 1) & (i == 0))
            def _(jj=jj, nxt=nxt):
                b_load(nxt, (jj + 1) % 2).wait()

        @pl.when((j == 3) & (i == 0))
        def _():
            land(1, 2)
            land(1, 1)

        p_ref[...] = jnp.dot(a_ref[...], b_scr[j % 2], preferred_element_type=F32).astype(p_ref.dtype)

        @pl.when((j == 3) & (i == ni - 1))
        def _():
            settle(1, 2)
            settle(1, 1)
            land(1, 3)
            settle(1, 3)
            for o in range(2):
                for k in (2, 1):
                    send(o, k).wait_send()
                for core in range(2):
                    on_core(core, lambda o=o, core=core: relay(o, core).wait_send())

    hbm = pl.BlockSpec(memory_space=pl.ANY)
    grid_spec = pltpu.PrefetchScalarGridSpec(
        num_scalar_prefetch=1, grid=(4, ni),
        in_specs=[pl.BlockSpec((tm, D), lambda j, i, qr: (i, 0)), hbm, hbm],
        out_specs=[pl.BlockSpec((tm, nq), lambda j, i, qr: (i, jnp.bitwise_xor(qr[0], ((j & 1) << 1) | (j >> 1)))),
                   hbm, hbm],
        scratch_shapes=[pltpu.VMEM((2,) + win.shape, win.dtype), pltpu.VMEM(wout.shape, wout.dtype), pltpu.SemaphoreType.DMA((3,))]
        + [pltpu.SemaphoreType.DMA((2, 4)) for _ in range(4)])
    return pl.pallas_call(
        body, name="in_proj_gather", grid_spec=grid_spec,
        out_shape=[jax.ShapeDtypeStruct((rows, 4 * nq), BF16), jax.ShapeDtypeStruct((D, 4 * nq), win.dtype),
                   jax.ShapeDtypeStruct((4 * dq, D2), wout.dtype)],
        compiler_params=_params(("arbitrary", "arbitrary")),
    )(q_arr, hl, win, wout)


def _rs_to_sibling(gs, axes):
    n = len(gs)
    shapes = []
    for g, ax in zip(gs, axes):
        s = list(g.shape)
        s[ax] //= 8
        shapes.append(tuple(s))

    def body(*refs):
        g_refs, mine, landed = refs[:n], refs[n:2 * n], refs[2 * n:3 * n]
        bufs = refs[3 * n:4 * n]
        lsem, ssem, rsem = refs[4 * n:]
        x, y, c = _pos()
        cps = []
        for o in range(n):
            size = shapes[o][axes[o]]
            for j in range(4):
                rc = pltpu.make_async_remote_copy(
                    src_ref=_chunk(g_refs[o], axes[o], 2 * j + 1 - c, size), dst_ref=landed[o].at[j],
                    send_sem=ssem.at[o, j], recv_sem=rsem.at[o, j], device_id=(x, y, 1 - c), device_id_type=MESH_ID)
                rc.start()
                cps.append(rc)
        for o in range(n):
            size = shapes[o][axes[o]]
            for j in range(4):
                _bounce(_chunk(g_refs[o], axes[o], 2 * j + c, size), mine[o].at[j], bufs[o], lsem.at[o])
        for rc in cps:
            rc.wait()

    hbm = pl.BlockSpec(memory_space=pl.ANY)
    outs = [jax.ShapeDtypeStruct((4,) + s, g.dtype) for s, g in zip(shapes, gs)]
    res = pl.pallas_call(
        body, name="rs_to_sibling", in_specs=[hbm] * n, out_specs=[hbm] * (2 * n), out_shape=outs + outs,
        scratch_shapes=[pltpu.VMEM(s, g.dtype) for s, g in zip(shapes, gs)]
        + [pltpu.SemaphoreType.DMA((n,)), pltpu.SemaphoreType.DMA((n, 4)), pltpu.SemaphoreType.DMA((n, 4))],
        compiler_params=_params(),
    )(*gs)
    return res[:n], res[n:]


def _rs_chips_side(parts):
    n = len(parts)

    def copies(p_refs, slots, scr):
        ssem, rsem = scr[n + 1], scr[n + 2]
        x, y, c = _pos()
        cps = []
        for o in range(n):
            for k in range(1, 4):
                px, py = _flip(x, (k >> 1) & 1), _flip(y, k & 1)
                cps.append(pltpu.make_async_remote_copy(
                    src_ref=p_refs[o].at[2 * px + py], dst_ref=slots[o].at[k], send_sem=ssem.at[o, k],
                    recv_sem=rsem.at[o, k], device_id=(px, py, c), device_id_type=MESH_ID))
        return cps

    def start(p_refs, slots, scr):
        for cp in copies(p_refs, slots, scr):
            cp.start()

    def finish(p_refs, slots, scr):
        x, y, _ = _pos()
        q = 2 * x + y
        for o in range(n):
            _bounce(p_refs[o].at[q], slots[o].at[0], scr[o], scr[n].at[o])
        for cp in copies(p_refs, slots, scr):
            cp.wait()

    return dict(
        ins=list(parts), outs=[jax.ShapeDtypeStruct(p.shape, p.dtype) for p in parts],
        scratch=[pltpu.VMEM(p.shape[1:], p.dtype) for p in parts]
        + [pltpu.SemaphoreType.DMA((n,)), pltpu.SemaphoreType.DMA((n, 4)), pltpu.SemaphoreType.DMA((n, 4))],
        start=start, finish=finish)


def _rs_share(rs, axes):
    n = len(rs)
    shapes = []
    for r, ax in zip(rs, axes):
        s = list(r.shape)
        s[ax] *= 2
        shapes.append(tuple(s))

    def body(*refs):
        r_refs, outs = refs[:n], refs[n:2 * n]
        bufs = refs[2 * n:3 * n]
        lsem, ssem, rsem = refs[3 * n:]
        x, y, c = _pos()
        cps = []
        for o in range(n):
            size = r_refs[o].shape[axes[o]]
            window = _chunk(outs[o], axes[o], c, size)
            rc = pltpu.make_async_remote_copy(src_ref=r_refs[o], dst_ref=window, send_sem=ssem.at[o], recv_sem=rsem.at[o],
                                              device_id=(x, y, 1 - c), device_id_type=MESH_ID)
            rc.start()
            cps.append(rc)
        for o in range(n):
            size = r_refs[o].shape[axes[o]]
            _bounce(r_refs[o], _chunk(outs[o], axes[o], c, size), bufs[o], lsem.at[o])
        for cp in cps:
            cp.wait()

    hbm = pl.BlockSpec(memory_space=pl.ANY)
    return pl.pallas_call(
        body, name="rs_share", in_specs=[hbm] * n, out_specs=[hbm] * n,
        out_shape=[jax.ShapeDtypeStruct(s, r.dtype) for s, r in zip(shapes, rs)],
        scratch_shapes=[pltpu.VMEM(r.shape, r.dtype) for r in rs] + [pltpu.SemaphoreType.DMA((n,)) for _ in range(3)],
        compiler_params=_params(),
    )(*rs)


def _rs_pair_sums(gs, axes):
    mine, landed = _rs_to_sibling(gs, axes)
    pair_sums = []
    for o, (mi, la) in enumerate(zip(mine, landed)):
        rows, cols = mi.shape[0] * mi.shape[1], mi.shape[2]
        s = _elementwise(lambda a, b: a.astype(F32) + b.astype(F32), [(mi.reshape(rows, cols), 0), (la.reshape(rows, cols), 0)],
                         [BF16], rows=rows, cols=cols, name=f"rs_pair_sum{o}")[0]
        pair_sums.append(s.reshape(mi.shape))
    return pair_sums


def _rs_finish(slots, axes):
    reduced = []
    for o, sl in enumerate(slots):
        rows, cols = sl.shape[1], sl.shape[2]
        flat = sl.reshape(4 * rows, cols)
        r = _elementwise(lambda a, b, c, d: (a.astype(F32) + b.astype(F32)) + (c.astype(F32) + d.astype(F32)),
                         [(flat, k * rows) for k in range(4)], [F32], rows=rows, cols=cols, name=f"rs_chip_sum{o}")[0]
        reduced.append(r)
    return _rs_share(reduced, axes)


ROW_CHUNK = 2 * SUBLANES


def _row_chunks(tr, fn):
    def body(ci, carry):
        fn(pl.ds(pl.multiple_of(ci * ROW_CHUNK, ROW_CHUNK), ROW_CHUNK))
        return carry

    lax.fori_loop(0, tr // ROW_CHUNK, body, 0, unroll=2)


def _fold(v):
    return v[:SUBLANES] + v[SUBLANES:]


def _norm_in(x, ctx, g, sc_l, sh_l, sc_c, sh_c, tr):
    L, D = x.shape
    T = ctx.shape[0]
    nx, nc = L // tr, T // tr

    def body(x_ref, c_ref, g_ref, scl, shl, scc, shc, o_ref):
        i = pl.program_id(0)

        def run(src, sc, sh):
            def chunk(rs):
                v = src[rs, :]
                r = lax.rsqrt(jnp.mean(v * v, axis=-1, keepdims=True) + EPS)
                o_ref[rs, :] = ((v * r * g_ref[...]) * (1.0 + sc[...]) + sh[...]).astype(o_ref.dtype)

            _row_chunks(tr, chunk)

        @pl.when(i < nx)
        def _():
            run(x_ref, scl, shl)

        @pl.when(i >= nx)
        def _():
            run(c_ref, scc, shc)

    vec = pl.BlockSpec((1, D), lambda i: (0, 0))
    return pl.pallas_call(
        body, name="norm_in", grid=(nx + nc,),
        in_specs=[pl.BlockSpec((tr, D), lambda i: (jnp.minimum(i, nx - 1), 0)),
                  pl.BlockSpec((tr, D), lambda i: (jnp.maximum(i - nx, 0), 0)), vec, vec, vec, vec, vec],
        out_specs=pl.BlockSpec((tr, D), lambda i: (i, 0)),
        out_shape=jax.ShapeDtypeStruct((L + T, D), BF16),
        compiler_params=_params(("arbitrary",)),
    )(x, ctx, g, sc_l, sh_l, sc_c, sh_c)


def _tmod(tl, row_w):
    assert row_w & (row_w - 1) == 0
    return lax.broadcasted_iota(jnp.int32, (tl, 1), 0) & (row_w - 1)


def _shift(z, k, tmod, row_w):
    tl = z.shape[0]
    rolled = pltpu.roll(z, k % tl, 0)
    mask = (tmod >= k) if k > 0 else (tmod < row_w + k)
    return jnp.where(mask, rolled, 0.0)


def _conv(z, w_ref, taps, left, tmod, row_w, lanes=slice(None)):
    out = None
    for j in range(taps):
        k = left - j
        term = (z if k == 0 else _shift(z, k, tmod, row_w)) * w_ref[j:j + 1, lanes]
        out = term if out is None else out + term
    return out


def _conv_bwd(dz, z, w_ref, taps, left, tmod, row_w, lanes=slice(None)):
    din = None
    dws = []
    for j in range(taps):
        k = left - j
        shifted = dz if k == 0 else _shift(dz, -k, tmod, row_w)
        term = shifted * w_ref[j:j + 1, lanes]
        din = term if din is None else din + term
        dws.append(jnp.sum(shifted * z, axis=0, keepdims=True))
    return din, dws


def _gate_matmul(xb16_ref, wd_ref, pre_scr, W, ng, gs):
    for g in range(ng):
        pg = jnp.dot(xb16_ref[:, g * gs:(g + 1) * gs], wd_ref[g], preferred_element_type=F32)
        pre_scr[:, g * gs:(g + 1) * gs] = pg[:, :gs]
        pre_scr[:, W + g * gs:W + (g + 1) * gs] = pg[:, gs:]


def _f32(ref, rows, lanes):
    return ref[rows, lanes].astype(F32)


def _sub_loop(tl, sub_r, W, fn):
    def chunk(ci, carry):
        r0 = pl.multiple_of(ci * sub_r, sub_r)
        for lb in range(W // LANES):
            fn(r0, lb * LANES)
        return carry

    lax.fori_loop(0, tl // sub_r, chunk, 0)


def _row_loop(tl, rev, step, init):
    nchunk = tl // SUBLANES

    def chunk(j, carry):
        jj = (nchunk - 1 - j) if rev else j
        c0 = pl.multiple_of(jj * SUBLANES, SUBLANES)
        for r in (range(SUBLANES - 1, -1, -1) if rev else range(SUBLANES)):
            carry = step(c0 + r, carry)
        return carry

    return lax.fori_loop(0, nchunk, chunk, init)


def _mix_fwd(P, d, h_init, wts, *, rows, row_off, row_w, tl, saved0=None, name):
    W = P.shape[1] // 6
    nt = rows // tl
    ob = row_off // tl
    rev = d == 1
    gs = min(LRU_GROUP, W)
    ng = W // gs
    wca, wcb, bcb = wts["wca"], wts["wcb"], wts["bcb"]
    wd, ba, bx, lam = wts["wd"][d], wts["ba"][d], wts["bx"][d], wts["lam"][d]

    def tile(i):
        return (nt - 1 - i) if rev else i

    def pcol(j):
        return pl.BlockSpec((tl, W), lambda i: (tile(i) + ob, j))

    vec = pl.BlockSpec((1, W), lambda i: (0, 0))
    taps = pl.BlockSpec((SUBLANES, W), lambda i: (0, 0))
    wd_spec = pl.BlockSpec(wd.shape, lambda i: (0, 0, 0))
    seq = pl.BlockSpec((tl, W), lambda i: (tile(i), 0))

    sub_r = min(row_w, tl)
    assert tl % sub_r == 0

    def body(*refs):
        if rev:
            (bl, cl, ul, gl, ql, ho, xb_r, xb16_r, wca_r, wd_r, ba_r, bx_r, lam_r, hin, hseq, cat, a_o, r_o, ig_o, m2_o,
             b_scr, pre_scr, carry, sp_scr) = refs
        else:
            (vl, wcb_r, bcb_r, wd_r, ba_r, bx_r, lam_r, hin, hseq, xb_r, xb16_r, a_o, r_o, ig_o, m2_o,
             b_scr, pre_scr, carry, sp_scr) = refs
        i = pl.program_id(0)

        @pl.when(i == 0)
        def _():
            carry[...] = hin[...]

        sp_scr[...] = _softplus(-lam_r[...])
        tmod = _tmod(sub_r, sub_r)

        def conv_in(r0, l0):
            rs, ls = pl.ds(r0, sub_r), pl.ds(l0, LANES)
            xb = _conv(_f32(vl, rs, ls), wcb_r, 4, 2, tmod, sub_r, ls) + bcb_r[:, ls]
            xb_r[rs, ls] = xb
            xb16_r[rs, ls] = xb.astype(BF16)

        def gates(r0, l0):
            rs, ls = pl.ds(r0, sub_r), pl.ds(l0, LANES)
            r, ig, a, m2 = _gates(pre_scr[rs, ls] + ba_r[:, ls], pre_scr[rs, pl.ds(W + l0, LANES)] + bx_r[:, ls],
                                  sp_scr[:, ls])
            a_o[rs, ls] = a
            r_o[rs, ls] = r.astype(r_o.dtype)
            ig_o[rs, ls] = ig.astype(ig_o.dtype)
            m2_o[rs, ls] = m2.astype(m2_o.dtype)
            m = jnp.where(m2 > 0.0, m2 * lax.rsqrt(m2), 0.0)
            b_scr[rs, ls] = m * (ig * xb_r[rs, ls])

        if not rev:
            _sub_loop(tl, sub_r, W, conv_in)
        _gate_matmul(xb16_r, wd_r, pre_scr, W, ng, gs)
        _sub_loop(tl, sub_r, W, gates)

        def step(t, h):
            h = a_o[pl.ds(t, 1), :] * h + b_scr[pl.ds(t, 1), :]
            hseq[pl.ds(t, 1), :] = h
            return h

        carry[...] = _row_loop(tl, rev, step, carry[...])

        if rev:
            def mix_out(r0, l0):
                rs, ls = pl.ds(r0, sub_r), pl.ds(l0, LANES)
                yb = (ho[rs, ls] + hseq[rs, ls]) * _silu(_f32(ql, rs, ls))
                ya = (_f32(bl, rs, ls) * _conv(_f32(cl, rs, ls) * _f32(ul, rs, ls), wca_r, 3, 1, tmod, sub_r, ls)
                      * _silu(_f32(gl, rs, ls)))
                cat[rs, ls] = ya.astype(cat.dtype)
                cat[rs, pl.ds(W + l0, LANES)] = yb.astype(cat.dtype)

            _sub_loop(tl, sub_r, W, mix_out)

    scratch = [pltpu.VMEM((tl, W), F32), pltpu.VMEM((tl, 2 * W), F32), pltpu.VMEM((1, W), F32), pltpu.VMEM((1, W), F32)]
    f32_seq = jax.ShapeDtypeStruct((rows, W), F32)
    kept_gates = [f32_seq] + [jax.ShapeDtypeStruct((rows, W), BF16)] * 3
    if rev:
        in_specs = [pcol(j) for j in (0, 1, 2, 3, 5)] + [seq, seq, seq, taps, wd_spec, vec, vec, vec, vec]
        args = [P] * 5 + [saved0["h"], saved0["xb"], saved0["xb16"], wca, wd, ba, bx, lam, h_init]
        out_specs = [seq, pl.BlockSpec((tl, 2 * W), lambda i: (tile(i), 0))] + [seq] * 4
        out_shape = [f32_seq, jax.ShapeDtypeStruct((rows, 2 * W), BF16)] + kept_gates
    else:
        in_specs = [pcol(4), taps, vec, wd_spec, vec, vec, vec, vec]
        args = [P, wcb, bcb, wd, ba, bx, lam, h_init]
        out_specs = [seq] * 7
        out_shape = [f32_seq, f32_seq, jax.ShapeDtypeStruct((rows, W), BF16)] + kept_gates
    res = pl.pallas_call(
        body, name=name, grid=(nt,), in_specs=in_specs, out_specs=out_specs, out_shape=out_shape,
        scratch_shapes=scratch, compiler_params=_params(("arbitrary",)),
    )(*args)
    gates = dict(zip(("a", "r", "ig", "m2"), res[-4:]))
    if rev:
        return res[0], res[1], gates
    return dict(h=res[0], xb=res[1], xb16=res[2]), gates


_BWD_SCRATCH = ("dyl", "g", "dp16", "sp", "dlf", "edge", "c")
_FWD_SAVED = ("xb", "xb16", "a", "r", "ig", "m2")


def _bwd_scratch(tl, W):
    shapes = {"dyl": pltpu.VMEM((tl, W), F32), "g": pltpu.VMEM((tl, W), F32), "dp16": pltpu.VMEM((tl, 2 * W), BF16),
              "sp": pltpu.VMEM((1, W), F32), "dlf": pltpu.VMEM((1, W), F32), "edge": pltpu.VMEM((1, W), F32),
              "c": pltpu.VMEM((1, W), F32)}
    return [shapes[n] for n in _BWD_SCRATCH]


def _lru_bwd_tile(d, dy_fn, hs_ref, wd_r, lam_r, scr, acc, first, tl, sub_r, ng, gs):
    dwd_ref, dba_ref, dbx_ref, dlam_ref = acc
    W = hs_ref.shape[1]
    assert gs % LANES == 0
    rev = d == 0
    lam = lam_r[...]
    scr["sp"][...] = _softplus(-lam)
    scr["dlf"][...] = -_sigmoid(-lam)

    @pl.when(first)
    def _():
        dwd_ref[...] = jnp.zeros_like(dwd_ref)
        dba_ref[...] = jnp.zeros_like(dba_ref)
        dbx_ref[...] = jnp.zeros_like(dbx_ref)
        dlam_ref[...] = jnp.zeros_like(dlam_ref)

    def state_grad(r0, l0):
        rs, ls = pl.ds(r0, sub_r), pl.ds(l0, LANES)
        scr["dyl"][rs, ls] = dy_fn(rs, ls)

    _sub_loop(tl, sub_r, W, state_grad)

    def step(t, c):
        g = scr["dyl"][pl.ds(t, 1), :] + c
        scr["g"][pl.ds(t, 1), :] = g
        return scr["a"][pl.ds(t, 1), :] * g

    scr["c"][...] = _row_loop(tl, rev, step, scr["c"][...])
    row = lax.broadcasted_iota(jnp.int32, (sub_r, 1), 0)

    def grads(r0, l0):
        rs, ls = pl.ds(r0, sub_r), pl.ds(l0, LANES)
        g, a, m2 = scr["g"][rs, ls], scr["a"][rs, ls], _f32(scr["m2"], rs, ls)
        r, ig, xb = _f32(scr["r"], rs, ls), _f32(scr["ig"], rs, ls), scr["xb"][rs, ls]
        h = hs_ref[rs, ls]
        if d == 0:
            e0 = pl.multiple_of(jnp.maximum(r0 - SUBLANES, 0), SUBLANES)
            edge = jnp.where(r0 == 0, scr["edge"][:, ls], hs_ref[pl.ds(e0, SUBLANES), ls][SUBLANES - 1:, :])
            hprev = jnp.where(row == 0, edge, pltpu.roll(h, 1, 0))
        else:
            e0 = pl.multiple_of(jnp.minimum(r0 + sub_r, tl - SUBLANES), SUBLANES)
            edge = jnp.where(r0 == tl - sub_r, scr["edge"][:, ls], hs_ref[pl.ds(e0, SUBLANES), ls][:1, :])
            hprev = jnp.where(row == sub_r - 1, edge, pltpu.roll(h, sub_r - 1, 0))
        rsq = lax.rsqrt(m2)
        gm = g * (m2 * rsq)
        d_la = (g * hprev) * a - (g * (ig * xb)) * ((1.0 - m2) * rsq)
        d_pr = d_la * ((-LRU_C) * scr["sp"][:, ls]) * (r * (1.0 - r))
        d_pi = (gm * xb) * (ig * (1.0 - ig))
        scr["dyl"][rs, ls] = gm * ig
        dlam_ref[:, ls] += jnp.sum(d_la * ((-LRU_C) * r), axis=0, keepdims=True) * scr["dlf"][:, ls]
        dba_ref[:, ls] += jnp.sum(d_pr, axis=0, keepdims=True)
        dbx_ref[:, ls] += jnp.sum(d_pi, axis=0, keepdims=True)
        gi, off = divmod(l0, gs)
        scr["dp16"][rs, pl.ds(gi * 2 * gs + off, LANES)] = d_pr.astype(BF16)
        scr["dp16"][rs, pl.ds(gi * 2 * gs + gs + off, LANES)] = d_pi.astype(BF16)

    _sub_loop(tl, sub_r, W, grads)
    for gi in range(ng):
        dp = scr["dp16"][:, gi * 2 * gs:(gi + 1) * 2 * gs]
        scr["g"][:, gi * gs:(gi + 1) * gs] = lax.dot_general(dp, wd_r[gi], (((1,), (1,)), ((), ())),
                                                             preferred_element_type=F32)
        dwd_ref[gi] += lax.dot_general(scr["xb16"][:, gi * gs:(gi + 1) * gs], dp, (((0,), (0,)), ((), ())),
                                       preferred_element_type=F32)


def _edge_block(h, tl, nt, d):
    W = h.shape[1]
    per = tl // SUBLANES
    if d == 0:
        return pl.BlockSpec((SUBLANES, W), lambda i: (jnp.maximum((nt - 1 - i) * per - 1, 0), 0))
    return pl.BlockSpec((SUBLANES, W), lambda i: (jnp.minimum((i + 1) * per, nt * per - 1), 0))


def _mix_bwd0(P, dcat, saved0, gates0, h_init, c_init, wts, *, rows, row_off, row_w, tl, name):
    W = P.shape[1] // 6
    nt = rows // tl
    ob = row_off // tl
    gs = min(LRU_GROUP, W)
    ng = W // gs
    wd, lam = wts["wd"][0], wts["lam"][0]
    h0s = saved0["h"]
    kept = [saved0["xb"], saved0["xb16"]] + [gates0[n] for n in ("a", "r", "ig", "m2")]

    def tile(i):
        return nt - 1 - i

    vec = pl.BlockSpec((1, W), lambda i: (0, 0))
    wd_spec = pl.BlockSpec(wd.shape, lambda i: (0, 0, 0))
    seq = pl.BlockSpec((tl, W), lambda i: (tile(i), 0))

    sub_r = min(row_w, tl)
    assert tl % sub_r == 0

    def body(ql, dyb, hs, hedge8, xb_r, xb16_r, a_r, r_r, ig_r, m2_r, wd_r, lam_r, hin, cin,
             dxb_o, dwd_o, dba_o, dbx_o, dlam_o, cfin, *scratch):
        scr = dict(zip(_BWD_SCRATCH, scratch, strict=True))
        scr.update(zip(_FWD_SAVED, (xb_r, xb16_r, a_r, r_r, ig_r, m2_r), strict=True))
        i = pl.program_id(0)

        @pl.when(i == 0)
        def _():
            scr["c"][...] = cin[...]

        scr["edge"][...] = jnp.where(i == nt - 1, hin[...], hedge8[SUBLANES - 1:SUBLANES, :])
        _lru_bwd_tile(0, lambda rs, ls: _f32(dyb, rs, ls) * _silu(_f32(ql, rs, ls)), hs, wd_r, lam_r, scr,
                      (dwd_o, dba_o, dbx_o, dlam_o), i == 0, tl, sub_r, ng, gs)
        dxb_o[...] = scr["dyl"][...] + scr["g"][...]
        cfin[...] = scr["c"][...]

    return pl.pallas_call(
        body, name=name, grid=(nt,),
        in_specs=[pl.BlockSpec((tl, W), lambda i: (tile(i) + ob, 5)), pl.BlockSpec((tl, W), lambda i: (tile(i), 1)), seq,
                  _edge_block(h0s, tl, nt, 0)] + [seq] * 6 + [wd_spec, vec, vec, vec],
        out_specs=[seq, wd_spec, vec, vec, vec, vec],
        out_shape=[jax.ShapeDtypeStruct((rows, W), F32), jax.ShapeDtypeStruct(wd.shape, F32)]
        + [jax.ShapeDtypeStruct((1, W), F32)] * 4,
        scratch_shapes=_bwd_scratch(tl, W),
        compiler_params=_params(("arbitrary",)),
    )(P, dcat, h0s, h0s, *kept, wd, lam, h_init, c_init)


def _mix_bwd1(P, dcat, saved0, h1s, gates1, dxb0, h_init, c_init, wts, *, rows, row_off, row_w, tl, name):
    W = P.shape[1] // 6
    nt = rows // tl
    ob = row_off // tl
    gs = min(LRU_GROUP, W)
    ng = W // gs
    wca, wcb = wts["wca"], wts["wcb"]
    wd, lam = wts["wd"][1], wts["lam"][1]
    h0s = saved0["h"]
    kept = [saved0["xb"], saved0["xb16"]] + [gates1[n] for n in ("a", "r", "ig", "m2")]

    vec = pl.BlockSpec((1, W), lambda i: (0, 0))
    taps = pl.BlockSpec((SUBLANES, W), lambda i: (0, 0))
    wd_spec = pl.BlockSpec(wd.shape, lambda i: (0, 0, 0))
    seq = pl.BlockSpec((tl, W), lambda i: (i, 0))

    sub_r = min(row_w, tl)
    assert tl % sub_r == 0

    def body(bl, cl, ul, gl, vl, ql, dya, dyb, h0, h1, hedge8, dx0, xb_r, xb16_r, a_r, r_r, ig_r, m2_r, wca_r, wcb_r,
             wd_r, lam_r, hin, cin, dp_o, dwd_o, dba_o, dbx_o, dlam_o, dwca_o, dwcb_o, dbcb_o, cfin, *scratch):
        scr = dict(zip(_BWD_SCRATCH, scratch, strict=True))
        scr.update(zip(_FWD_SAVED, (xb_r, xb16_r, a_r, r_r, ig_r, m2_r), strict=True))
        i = pl.program_id(0)

        @pl.when(i == 0)
        def _():
            scr["c"][...] = cin[...]
            dwca_o[...] = jnp.zeros_like(dwca_o)
            dwcb_o[...] = jnp.zeros_like(dwcb_o)
            dbcb_o[...] = jnp.zeros_like(dbcb_o)

        scr["edge"][...] = jnp.where(i == nt - 1, hin[...], hedge8[0:1, :])
        _lru_bwd_tile(1, lambda rs, ls: _f32(dyb, rs, ls) * _silu(_f32(ql, rs, ls)), h1, wd_r, lam_r, scr,
                      (dwd_o, dba_o, dbx_o, dlam_o), i == 0, tl, sub_r, ng, gs)
        cfin[...] = scr["c"][...]
        tmod = _tmod(sub_r, sub_r)

        def rest(r0, l0):
            rs, ls = pl.ds(r0, sub_r), pl.ds(l0, LANES)
            dxb = dx0[rs, ls] + scr["dyl"][rs, ls] + scr["g"][rs, ls]
            dv, dwb = _conv_bwd(dxb, _f32(vl, rs, ls), wcb_r, 4, 2, tmod, sub_r, ls)
            for j in range(4):
                dwcb_o[j:j + 1, ls] += dwb[j]
            dbcb_o[:, ls] += jnp.sum(dxb, axis=0, keepdims=True)
            q = _f32(ql, rs, ls)
            sq = _sigmoid(q)
            dq = _f32(dyb, rs, ls) * (h0[rs, ls] + h1[rs, ls]) * (sq * (1.0 + q * (1.0 - sq)))
            b_, c_, u_, g_ = _f32(bl, rs, ls), _f32(cl, rs, ls), _f32(ul, rs, ls), _f32(gl, rs, ls)
            z = c_ * u_
            cz = _conv(z, wca_r, 3, 1, tmod, sub_r, ls)
            sgm = _sigmoid(g_)
            sg = g_ * sgm
            da = _f32(dya, rs, ls)
            dz, dwa = _conv_bwd(da * b_ * sg, z, wca_r, 3, 1, tmod, sub_r, ls)
            for j in range(3):
                dwca_o[j:j + 1, ls] += dwa[j]
            parts = (da * cz * sg, dz * u_, dz * c_, da * b_ * cz * (sgm * (1.0 + g_ * (1.0 - sgm))), dv, dq)
            for k, val in enumerate(parts):
                dp_o[rs, pl.ds(k * W + l0, LANES)] = val.astype(dp_o.dtype)

        _sub_loop(tl, sub_r, W, rest)

    def pcol(j):
        return pl.BlockSpec((tl, W), lambda i: (i + ob, j))

    return pl.pallas_call(
        body, name=name, grid=(nt,),
        in_specs=[pcol(j) for j in range(6)]
        + [pl.BlockSpec((tl, W), lambda i: (i, 0)), pl.BlockSpec((tl, W), lambda i: (i, 1)), seq, seq,
           _edge_block(h1s, tl, nt, 1), seq] + [seq] * 6 + [taps, taps, wd_spec, vec, vec, vec],
        out_specs=[pl.BlockSpec((tl, 6 * W), lambda i: (i, 0)), wd_spec, vec, vec, vec, taps, taps, vec, vec],
        out_shape=[jax.ShapeDtypeStruct((rows, 6 * W), BF16), jax.ShapeDtypeStruct(wd.shape, F32)]
        + [jax.ShapeDtypeStruct((1, W), F32)] * 3
        + [jax.ShapeDtypeStruct((SUBLANES, W), F32)] * 2 + [jax.ShapeDtypeStruct((1, W), F32)] * 2,
        scratch_shapes=_bwd_scratch(tl, W),
        compiler_params=_params(("arbitrary",)),
    )(*([P] * 6), dcat, dcat, h0s, h1s, h1s, dxb0, *kept, wca, wcb, wd, lam, h_init, c_init)


def _loss_head(out, x, tgt, gt, fg, tr):
    L, D = x.shape

    nsteps = L // tr

    def body(o_ref, x_ref, t_ref, gt_ref, fg_ref, dn_o, do_o, dfg_o, dgt_o, loss_o, part):
        i = pl.program_id(0)

        @pl.when(i == 0)
        def _():
            part[...] = jnp.zeros_like(part)

        def chunk(rs):
            o = o_ref[rs, :].astype(F32)
            gt_v = gt_ref[...]
            fg_v = fg_ref[...]
            n = x_ref[rs, :] + gt_v * o
            r = lax.rsqrt(jnp.mean(n * n, axis=-1, keepdims=True) + EPS)
            nr = n * r
            e = nr * fg_v - t_ref[rs, :]
            dy = e * (1.0 / D)
            qv = dy * fg_v
            dn = r * (qv - nr * jnp.mean(qv * nr, axis=-1, keepdims=True))
            part[0] += _fold(dy * nr)
            part[1] += _fold(dn * o)
            part[2] += _fold(e * e)
            dn_o[rs, :] = dn.astype(dn_o.dtype)
            do_o[rs, :] = (dn * gt_v).astype(do_o.dtype)

        _row_chunks(tr, chunk)

        @pl.when(i == nsteps - 1)
        def _():
            dfg_o[...] = jnp.sum(part[0], axis=0, keepdims=True)
            dgt_o[...] = jnp.sum(part[1], axis=0, keepdims=True)
            loss_o[...] = jnp.full(loss_o.shape, (0.5 / D) * jnp.sum(part[2]), F32)

    blk = pl.BlockSpec((tr, D), lambda i: (i, 0))
    vec = pl.BlockSpec((1, D), lambda i: (0, 0))
    return pl.pallas_call(
        body, name="loss_head", grid=(nsteps,), in_specs=[blk, blk, blk, vec, vec],
        out_specs=[blk, blk, vec, vec, pl.BlockSpec((SUBLANES, LANES), lambda i: (0, 0))],
        out_shape=[jax.ShapeDtypeStruct((L, D), BF16), jax.ShapeDtypeStruct((L, D), BF16),
                   jax.ShapeDtypeStruct((1, D), F32), jax.ShapeDtypeStruct((1, D), F32),
                   jax.ShapeDtypeStruct((SUBLANES, LANES), F32)],
        scratch_shapes=[pltpu.VMEM((3, SUBLANES, D), F32)],
        compiler_params=_params(("arbitrary",)),
    )(out, x, tgt, gt, fg)


def _norm_bwd(dhl, x, dn, g, sc, tr, name):
    L, D = x.shape
    with_x = dn is not None

    nsteps = L // tr

    def body(*refs):
        if with_x:
            d_ref, x_ref, dn_ref, g_ref, sc_ref, gx_o, dsh_o, dsc_o, dg_o, part = refs
        else:
            d_ref, x_ref, g_ref, sc_ref, dsh_o, dsc_o, dg_o, part = refs
        i = pl.program_id(0)

        @pl.when(i == 0)
        def _():
            part[...] = jnp.zeros_like(part)

        def chunk(rs):
            d = d_ref[rs, :].astype(F32)
            xv = x_ref[rs, :]
            g_v = g_ref[...]
            r = lax.rsqrt(jnp.mean(xv * xv, axis=-1, keepdims=True) + EPS)
            xr = xv * r
            dxn = d * (1.0 + sc_ref[...])
            part[0] += _fold(d)
            part[1] += _fold(d * (xr * g_v))
            part[2] += _fold(dxn * xr)
            if with_x:
                qv = dxn * g_v
                gx_o[rs, :] = r * (qv - xr * jnp.mean(qv * xr, axis=-1, keepdims=True)) + dn_ref[rs, :].astype(F32)

        _row_chunks(tr, chunk)

        @pl.when(i == nsteps - 1)
        def _():
            for k, o_ref in enumerate((dsh_o, dsc_o, dg_o)):
                o_ref[...] = jnp.sum(part[k], axis=0, keepdims=True)

    blk = pl.BlockSpec((tr, D), lambda i: (i, 0))
    vec = pl.BlockSpec((1, D), lambda i: (0, 0))
    vshape = jax.ShapeDtypeStruct((1, D), F32)
    res = pl.pallas_call(
        body, name=name, grid=(nsteps,),
        in_specs=[blk, blk] + ([blk] if with_x else []) + [vec, vec],
        out_specs=([blk] if with_x else []) + [vec, vec, vec],
        out_shape=([jax.ShapeDtypeStruct((L, D), F32)] if with_x else []) + [vshape] * 3,
        scratch_shapes=[pltpu.VMEM((3, SUBLANES, D), F32)],
        compiler_params=_params(("arbitrary",)),
    )(*([dhl, x] + ([dn] if with_x else []) + [g, sc]))
    return res if with_x else [None] + list(res)


def _pack_blockdiag(wa, wx, gs):
    H, hd, _ = wa.shape
    hp = gs // hd
    ng = H // hp
    eye = jnp.eye(hp, dtype=wa.dtype)

    def bd(w):
        return jnp.einsum("gpij,pq->gpiqj", w.reshape(ng, hp, hd, hd), eye).reshape(ng, gs, gs)

    return jnp.concatenate([bd(wa), bd(wx)], axis=-1).astype(BF16)


def _unpack_blockdiag(dwd, H, hd, gs):
    hp = gs // hd
    ng = H // hp
    eye = jnp.eye(hp, dtype=dwd.dtype)

    def diag(dm):
        return jnp.einsum("gpiqj,pq->gpij", dm.reshape(ng, hp, hd, hp, hd), eye).reshape(H, hd, hd)

    return diag(dwd[:, :, :gs]), diag(dwd[:, :, gs:])


def kernel(x, c, ctx, c_ctx, norm_g, w_ada, b_ada, w_in, w_conv_a, w_conv_b, b_conv_b, lru_wa, lru_ba, lru_wx, lru_bx, lru_lambda, w_out, final_g, loss_target, m_c_ctx, m_norm_g, m_w_ada, m_b_ada, m_w_in, m_w_conv_a, m_w_conv_b, m_b_conv_b, m_lru_wa, m_lru_ba, m_lru_wx, m_lru_bx, m_lru_lambda, m_w_out, m_final_g, v_c_ctx, v_norm_g, v_w_ada, v_b_ada, v_w_in, v_w_conv_a, v_w_conv_b, v_b_conv_b, v_lru_wa, v_lru_ba, v_lru_wx, v_lru_bx, v_lru_lambda, v_w_out, v_final_g):
    xi, yi, ci = _pos()
    me = 4 * xi + 2 * yi + ci
    q = 2 * xi + yi
    first_core = (ci == 0).astype(F32)

    L, D = x.shape[1], x.shape[2]
    T = ctx.shape[1]
    W = D // 2
    Wq = W // 4
    H, hd = lru_wa.shape[2], lru_wa.shape[3]
    gs = min(LRU_GROUP, W)
    nq = w_ada.shape[2]
    tl = min(256, T, L)
    tr = min(256, T, L)
    x2, ctx2, tgt2 = x[0], ctx[0], loss_target[0]

    def place(shard, full_cols):
        z = jnp.zeros((shard.shape[0], full_cols), F32)
        return lax.dynamic_update_slice(z, shard * first_core, (0, q * shard.shape[1]))

    c_rows = lax.dynamic_update_slice(jnp.zeros((8, D), F32), c, (me, 0))
    small_in = [c_rows, place(w_conv_a[0], W), place(w_conv_b[0], W), place(lru_ba[0], W), place(lru_bx[0], W),
                place(lru_lambda[0], W)]
    small_shapes = [a.shape for a in small_in]
    gathered = _allreduce8(_pack(small_in, 8 * SUBLANES), "gather_small")
    c_all, wca, wcb, ba_all, bx_all, lam_all = _unpack(gathered, small_shapes)

    s_rows = jnp.concatenate([c_all, c_ctx[None, :], jnp.zeros((7, D), F32)], axis=0)
    mod_part = _matmul(s_rows, w_ada[0], a_act="silu", bias=lax.dynamic_slice(b_ada, (0, q * nq), (1, nq)),
                       tm=16, tn=nq, tk=512, name="ada_fwd")
    mod_all = _allreduce8(_pack([place(mod_part, 4 * nq)], 8 * SUBLANES), "gather_mod")
    mod_all = _unpack(mod_all, [(16, 4 * nq)])[0]
    mod_l = lax.dynamic_slice(mod_all, (me, 0), (1, 3 * D))
    mod_c = mod_all[8:9]
    sh_l, sc_l, gt_l = mod_l[:, :D], mod_l[:, D:2 * D], mod_l[:, 2 * D:]
    sh_c, sc_c = mod_c[:, :D], mod_c[:, D:2 * D]

    pad_taps = lambda w: jnp.pad(w, ((0, SUBLANES - w.shape[0]), (0, 0)))
    wts = {
        "wca": pad_taps(wca), "wcb": pad_taps(wcb), "bcb": b_conv_b,
        "wd": [_pack_blockdiag(lru_wa[0, d], lru_wx[0, d], gs) for d in range(2)],
        "ba": [ba_all[d:d + 1] for d in range(2)], "bx": [bx_all[d:d + 1] for d in range(2)],
        "lam": [lam_all[d:d + 1] for d in range(2)],
    }

    hl = _norm_in(x2, ctx2, norm_g, sc_l, sh_l, sc_c, sh_c, tr)
    p_lat, win_full, wout_full = _in_proj_gather(hl, w_in[0].astype(BF16), w_out[0].astype(BF16),
                                                 jnp.reshape(q, (1,)).astype(jnp.int32), rows=L, tm=min(1024, L))
    p_ctx = _matmul(hl, win_full, a_rows=T, a_off=L, tm=T, tn=1536, tk=D, out_dtype=BF16, name="in_proj_ctx")
    zero_w = jnp.zeros((1, W), F32)
    ctx0, cgates0 = _mix_fwd(p_ctx, 0, zero_w, wts, rows=T, row_off=0, row_w=T, tl=tl, name="ctx_fwd0")
    c1s, _, cgates1 = _mix_fwd(p_ctx, 1, zero_w, wts, rows=T, row_off=0, row_w=T, tl=tl, saved0=ctx0, name="ctx_fwd1")
    h0_init, h1_init = ctx0["h"][T - 1:T], c1s[0:1]
    lat0, gates0 = _mix_fwd(p_lat, 0, h0_init, wts, rows=L, row_off=0, row_w=GRID_W, tl=tl, name="mix_fwd0")
    h1s, cat, gates1 = _mix_fwd(p_lat, 1, h1_init, wts, rows=L, row_off=0, row_w=GRID_W, tl=tl, saved0=lat0,
                                name="mix_fwd1")
    out = _matmul(cat, wout_full, tm=512, tn=D, tk=2 * W, out_dtype=BF16, name="out_proj")
    dn, dout, dfg, dgt, loss_blk = _loss_head(out, x2, tgt2, gt_l, final_g[None, :], tr)

    dcat = _matmul(dout, wout_full, tb=True, tm=512, tn=2 * W, tk=D, out_dtype=BF16, name="out_proj_bwd")
    gw_out = _matmul(cat, dout, ta=True, tm=1024, tn=D, tk=2048, out_dtype=BF16, name="w_out_grad")
    dxb0, dwd0, dba0, dbx0, dlam0, ch0 = _mix_bwd0(p_lat, dcat, lat0, gates0, h0_init, zero_w, wts, rows=L, row_off=0,
                                                   row_w=GRID_W, tl=tl, name="mix_bwd0")
    dp, dwd1, dba1, dbx1, dlam1, dwca, dwcb, dbcb, ch1 = _mix_bwd1(
        p_lat, dcat, lat0, h1s, gates1, dxb0, h1_init, zero_w, wts, rows=L, row_off=0, row_w=GRID_W, tl=tl,
        name="mix_bwd1")
    zero_cat = jnp.zeros((T, 2 * W), BF16)
    cxb0, cwd0, cba0, cbx0, clam0, _ = _mix_bwd0(p_ctx, zero_cat, ctx0, cgates0, zero_w, ch0, wts, rows=T, row_off=0,
                                                 row_w=T, tl=tl, name="ctx_bwd0")
    dp_c, cwd1, cba1, cbx1, clam1, cwca, cwcb, cbcb, _ = _mix_bwd1(
        p_ctx, zero_cat, ctx0, c1s, cgates1, cxb0, zero_w, ch1, wts, rows=T, row_off=0, row_w=T, tl=tl, name="ctx_bwd1")

    gw_in_ctx = _matmul(hl, dp_c, ta=True, a_rows=T, a_off=L, tm=1024, tn=1536, tk=T, name="w_in_grad_ctx")
    gw_in = _matmul(hl, dp, ta=True, a_rows=L, tm=1024, tn=1536, tk=2048, init=gw_in_ctx, out_dtype=BF16,
                    name="w_in_grad")
    rs_axes = [1, 0]
    pair_sums = _rs_pair_sums([gw_in, gw_out], rs_axes)
    dhl, rs_slots = _matmul(dp, win_full, tb=True, tm=512, tn=D, tk=3072, out_dtype=BF16, name="in_proj_bwd",
                            side=_rs_chips_side(pair_sums))
    dhc = _matmul(dp_c, win_full, tb=True, tm=T, tn=D, tk=512, name="in_proj_bwd_ctx")
    gx, dsh_l, dsc_l, dng_l = _norm_bwd(dhl, x2, dn, norm_g, sc_l, tr, "norm_bwd")
    _, dsh_c, dsc_c, dng_c = _norm_bwd(dhc, ctx2, None, norm_g, sc_c, tr, "norm_bwd_ctx")

    g_in_shard, g_out_shard = _rs_finish(rs_slots, rs_axes)

    dwa0, dwx0 = _unpack_blockdiag(dwd0 + cwd0, H, hd, gs)
    dwa1, dwx1 = _unpack_blockdiag(dwd1 + cwd1, H, hd, gs)
    zeros_d = jnp.zeros((1, D), F32)
    dmod_l = jnp.concatenate([dsh_l, dsc_l, dgt], axis=1)
    dmod_c = jnp.concatenate([dsh_c, dsc_c, zeros_d], axis=1)
    small_g = [
        lax.dynamic_update_slice(jnp.zeros((8, 3 * D), F32), dmod_l, (me, 0)), dmod_c,
        dfg, dng_l + dng_c, (dwca + cwca)[:3], (dwcb + cwcb)[:4], dbcb + cbcb,
        jnp.stack([dwa0, dwa1]), jnp.stack([dwx0, dwx1]),
        jnp.concatenate([dba0 + cba0, dba1 + cba1], axis=0), jnp.concatenate([dbx0 + cbx0, dbx1 + cbx1], axis=0),
        jnp.concatenate([dlam0 + clam0, dlam1 + clam1], axis=0),
    ]
    g_shapes = [a.shape for a in small_g]
    (g_rows, g_modc, g_fg, g_ng, g_wca, g_wcb, g_bcb, g_wa, g_wx, g_ba, g_bx, g_lam) = _unpack(
        _allreduce8(_pack(small_g, 8 * SUBLANES), "reduce_small"), g_shapes)

    g_mod = jnp.concatenate([g_rows, g_modc, jnp.zeros((7, 3 * D), F32)], axis=0)
    g_mod_q = lax.dynamic_slice(g_mod, (0, q * nq), (16, nq))
    g_w_ada = _matmul(s_rows, g_mod_q, ta=True, a_act="silu", tm=1024, tn=nq, tk=16, name="w_ada_grad")
    g_b_ada = jnp.sum(g_mod[:9], axis=0, keepdims=True)
    gc_part = _matmul(jnp.pad(lax.dynamic_slice(g_modc, (0, q * nq), (1, nq)), ((0, 7), (0, 0))), w_ada[0], tb=True,
                      dsilu_mul=c_ctx[None, :], tm=8, tn=D, tk=512, name="c_ctx_grad")
    g_c_ctx = _unpack(_allreduce8(_pack([gc_part[0:1] * first_core], 8 * SUBLANES), "reduce_c_ctx"), [(D,)])[0]

    def shard_cols(a, width):
        return lax.dynamic_slice(a, (0, q * width), (a.shape[0], width))

    grads = {
        "c_ctx": g_c_ctx, "norm_g": g_ng, "b_ada": g_b_ada,
        "w_conv_a": shard_cols(g_wca, Wq)[None], "w_conv_b": shard_cols(g_wcb, Wq)[None], "b_conv_b": g_bcb,
        "lru_wa": g_wa[None], "lru_ba": shard_cols(g_ba, Wq)[None], "lru_wx": g_wx[None],
        "lru_bx": shard_cols(g_bx, Wq)[None], "lru_lambda": shard_cols(g_lam, Wq)[None], "final_g": g_fg[0],
    }
    small_names = list(grads)
    given = dict(c_ctx=(c_ctx, m_c_ctx, v_c_ctx), norm_g=(norm_g, m_norm_g, v_norm_g), b_ada=(b_ada, m_b_ada, v_b_ada),
                 w_conv_a=(w_conv_a, m_w_conv_a, v_w_conv_a), w_conv_b=(w_conv_b, m_w_conv_b, v_w_conv_b),
                 b_conv_b=(b_conv_b, m_b_conv_b, v_b_conv_b), lru_wa=(lru_wa, m_lru_wa, v_lru_wa),
                 lru_ba=(lru_ba, m_lru_ba, v_lru_ba), lru_wx=(lru_wx, m_lru_wx, v_lru_wx),
                 lru_bx=(lru_bx, m_lru_bx, v_lru_bx), lru_lambda=(lru_lambda, m_lru_lambda, v_lru_lambda),
                 final_g=(final_g, m_final_g, v_final_g))
    shapes = [given[n][0].shape for n in small_names]
    packed = [_pack([given[n][j] for n in small_names], 2 * SUBLANES) for j in range(3)]
    packed_g = _pack([grads[n] for n in small_names], 2 * SUBLANES)
    sd, sm, sv = _adam(packed[0], packed_g, packed[1], packed[2], "adam_small")
    delta_s = dict(zip(small_names, _unpack(sd, shapes)))
    newm_s = dict(zip(small_names, _unpack(sm, shapes)))
    newv_s = dict(zip(small_names, _unpack(sv, shapes)))
    grads = {n: grads[n].reshape(given[n][0].shape) for n in small_names}

    big = {"w_ada": (w_ada, g_w_ada, m_w_ada, v_w_ada), "w_in": (w_in, g_in_shard, m_w_in, v_w_in),
           "w_out": (w_out, g_out_shard, m_w_out, v_w_out)}
    delta_b, newm_b, newv_b = {}, {}, {}
    for n, (w, g, m, v) in big.items():
        grads[n] = g[None]
        d_, m_, v_ = _adam(w[0], g, m[0], v[0], "adam_" + n)
        delta_b[n], newm_b[n], newv_b[n] = d_[None], m_[None], v_[None]

    loss = lax.psum(loss_blk[0, 0], AXES)
    order = ["c_ctx", "norm_g", "w_ada", "b_ada", "w_in", "w_conv_a", "w_conv_b", "b_conv_b", "lru_wa", "lru_ba",
             "lru_wx", "lru_bx", "lru_lambda", "w_out", "final_g"]
    delta = {**delta_s, **delta_b}
    newm = {**newm_s, **newm_b}
    newv = {**newv_s, **newv_b}
    return (loss, gx[None], *[grads[n] for n in order], *[delta[n] for n in order], *[newm[n] for n in order],
            *[newv[n] for n in order])
```

```python
import functools

import jax
import jax.numpy as jnp
from jax import lax
from jax.experimental import pallas as pl
from jax.experimental.pallas import tpu as pltpu

F32 = jnp.float32
BF16 = jnp.bfloat16
MESH_ID = pl.DeviceIdType.MESH
AXES = ("x", "y", "c")

EPS = 1e-6
LRU_C = 8.0
GRID_W = 64
ADAM_LR = 0.001
ADAM_B1 = 0.9
ADAM_B2 = 0.999
ADAM_EPS = 1e-08
ADAM_WD = 0.01
ADAM_STEP = 10

LANES = 128
SUBLANES = 8
PACK_COLS = 1024
VMEM_LIMIT = 56 * 2**20
LRU_GROUP = 256


def _params(sem=None):
    return pltpu.CompilerParams(vmem_limit_bytes=VMEM_LIMIT, dimension_semantics=sem)


def _pick(dim, pref, quantum=LANES):
    if dim <= pref:
        return dim
    best = None
    for t in range(quantum, pref + 1, quantum):
        if dim % t == 0:
            best = t
    assert best is not None, (dim, pref)
    return best


def _pos():
    return lax.axis_index("x"), lax.axis_index("y"), lax.axis_index("c")


def _flip(v, bit):
    return 1 - v if bit else v


def _sigmoid(v):
    return 0.5 * jnp.tanh(0.5 * v) + 0.5


def _silu(v):
    return v * _sigmoid(v)


def _dsilu(v):
    s = _sigmoid(v)
    return s * (1.0 + v * (1.0 - s))


def _gates(pre_r, pre_i, sp):
    r = _sigmoid(pre_r)
    ig = _sigmoid(pre_i)
    e = LRU_C * r * sp
    w = jnp.tanh(e)
    return r, ig, jnp.exp(-e), (2.0 * w) * pl.reciprocal(1.0 + w, approx=True)


def _softplus(z):
    return jnp.maximum(z, 0.0) + jnp.log1p(jnp.exp(-jnp.abs(z)))


def _matmul(a, b, *, ta=False, tb=False, tm=512, tn=512, tk=512, out_dtype=F32, name,
            a_rows=None, a_off=0, a_act=None, init=None, bias=None, dsilu_mul=None, side=None):
    rows_a = a.shape[0] if a_rows is None else a_rows
    if ta:
        K, M = rows_a, a.shape[1]
    else:
        M, K = rows_a, a.shape[1]
    N = b.shape[0] if tb else b.shape[1]
    tm, tn, tk = _pick(M, tm, SUBLANES), _pick(N, tn), _pick(K, tk)
    t_rows = tk if ta else tm
    assert a_off % t_rows == 0
    nk = K // tk
    gi, gj = M // tm, N // tn
    off_blocks = a_off // t_rows
    dims = (((0 if ta else 1,), (1 if tb else 0,)), ((), ()))
    extras = [e for e in (init, bias, dsilu_mul) if e is not None]
    n_sin = len(side["ins"]) if side else 0
    n_sout = len(side["outs"]) if side else 0

    def body(a_ref, b_ref, *rest):
        rest = list(rest)
        init_ref = rest.pop(0) if init is not None else None
        bias_ref = rest.pop(0) if bias is not None else None
        dsm_ref = rest.pop(0) if dsilu_mul is not None else None
        side_in = [rest.pop(0) for _ in range(n_sin)]
        o_ref = rest.pop(0)
        side_out = [rest.pop(0) for _ in range(n_sout)]
        acc_ref = rest.pop(0) if nk > 1 else None
        side_scr = rest
        i, j, k = pl.program_id(0), pl.program_id(1), pl.program_id(2)

        if side:
            @pl.when((i == 0) & (j == 0) & (k == 0))
            def _():
                side["start"](side_in, side_out, side_scr)

        av = a_ref[...]
        if a_act == "silu":
            av = _silu(av)
        prod = lax.dot_general(av, b_ref[...], dims, preferred_element_type=F32)

        def finish(r):
            if bias_ref is not None:
                r = r + bias_ref[...]
            if dsm_ref is not None:
                r = r * _dsilu(dsm_ref[...])
            o_ref[...] = r.astype(o_ref.dtype)

        if nk == 1:
            finish(prod if init_ref is None else prod + init_ref[...])
        else:
            @pl.when(k == 0)
            def _():
                acc_ref[...] = prod if init_ref is None else prod + init_ref[...]

            @pl.when(k > 0)
            def _():
                acc_ref[...] += prod

            @pl.when(k == nk - 1)
            def _():
                finish(acc_ref[...])

        if side:
            @pl.when((i == gi - 1) & (j == gj - 1) & (k == nk - 1))
            def _():
                side["finish"](side_in, side_out, side_scr)

    if ta:
        a_spec = pl.BlockSpec((tk, tm), lambda i, j, k: (k + off_blocks, i))
    else:
        a_spec = pl.BlockSpec((tm, tk), lambda i, j, k: (i + off_blocks, k))
    if tb:
        b_spec = pl.BlockSpec((tn, tk), lambda i, j, k: (j, k))
    else:
        b_spec = pl.BlockSpec((tk, tn), lambda i, j, k: (k, j))
    in_specs = [a_spec, b_spec]
    if init is not None:
        in_specs.append(pl.BlockSpec((tm, tn), lambda i, j, k: (i, j)))
    if bias is not None:
        in_specs.append(pl.BlockSpec((1, tn), lambda i, j, k: (0, j)))
    if dsilu_mul is not None:
        in_specs.append(pl.BlockSpec((1, tn), lambda i, j, k: (0, j)))
    hbm = pl.BlockSpec(memory_space=pl.ANY)
    res = pl.pallas_call(
        body, name=name, grid=(gi, gj, nk),
        in_specs=in_specs + [hbm] * n_sin,
        out_specs=[pl.BlockSpec((tm, tn), lambda i, j, k: (i, j))] + [hbm] * n_sout,
        out_shape=[jax.ShapeDtypeStruct((M, N), out_dtype)] + (list(side["outs"]) if side else []),
        scratch_shapes=([pltpu.VMEM((tm, tn), F32)] if nk > 1 else []) + (list(side["scratch"]) if side else []),
        compiler_params=_params(("arbitrary",) * 3 if side else ("parallel", "parallel", "arbitrary")),
    )(a, b, *extras, *(side["ins"] if side else []))
    return (res[0], res[1:]) if side else res[0]


def _elementwise(fn, ins, outs, *, rows, cols, name, tr=256):
    tr = _pick(rows, tr, 2 * SUBLANES)
    n_in = len(ins)

    def body(*refs):
        vals = fn(*[r[...] for r in refs[:n_in]])
        if not isinstance(vals, (tuple, list)):
            vals = (vals,)
        for r, v in zip(refs[n_in:], vals, strict=True):
            r[...] = v.astype(r.dtype)

    def spec(off):
        assert off % tr == 0
        ob = off // tr
        return pl.BlockSpec((tr, cols), lambda i: (i + ob, 0))

    res = pl.pallas_call(
        body, name=name, grid=(rows // tr,),
        in_specs=[spec(off) for _, off in ins],
        out_specs=[spec(0) for _ in outs],
        out_shape=[jax.ShapeDtypeStruct((rows, cols), dt) for dt in outs],
        compiler_params=_params(("parallel",)),
    )(*[a for a, _ in ins])
    return res


def _adam_math(w, g, m, v):
    m = ADAM_B1 * m + (1.0 - ADAM_B1) * g
    v = ADAM_B2 * v + (1.0 - ADAM_B2) * (g * g)
    m_hat = m / (1.0 - ADAM_B1 ** ADAM_STEP)
    v_hat = v / (1.0 - ADAM_B2 ** ADAM_STEP)
    delta = -ADAM_LR * (m_hat / (jnp.sqrt(v_hat) + ADAM_EPS) + ADAM_WD * w)
    return delta, m, v


def _adam(w, g, m, v, name):
    rows, cols = w.shape
    return _elementwise(_adam_math, [(w, 0), (g, 0), (m, 0), (v, 0)], [F32, F32, F32],
                        rows=rows, cols=cols, name=name)


def _pack(arrs, row_quantum):
    flat = jnp.concatenate([a.reshape(-1).astype(F32) for a in arrs])
    n = flat.shape[0]
    q = row_quantum * PACK_COLS
    total = -(-n // q) * q
    flat = jnp.pad(flat, (0, total - n))
    return flat.reshape(total // PACK_COLS, PACK_COLS)


def _unpack(buf, shapes):
    flat = buf.reshape(-1)
    out, off = [], 0
    for s in shapes:
        n = 1
        for d in s:
            n *= d
        out.append(flat[off:off + n].reshape(s))
        off += n
    return out


def _allreduce8(buf, name):
    R, C = buf.shape
    assert R % (8 * SUBLANES) == 0
    m = R // 8

    def body(x_ref, o_ref, recv, red, s1, r1, s2, r2):
        x, y, c = _pos()
        me = 4 * x + 2 * y + c

        def peer(k):
            px, py, pc = _flip(x, (k >> 2) & 1), _flip(y, (k >> 1) & 1), _flip(c, k & 1)
            return (px, py, pc), 4 * px + 2 * py + pc

        def rows(ref, idx):
            return ref.at[pl.ds(pl.multiple_of(idx * m, SUBLANES), m), :]

        def scatter(k):
            dev, p = peer(k)
            return pltpu.make_async_remote_copy(src_ref=rows(x_ref, p), dst_ref=recv.at[k], send_sem=s1.at[k],
                                                recv_sem=r1.at[k], device_id=dev, device_id_type=MESH_ID)

        def share(k):
            dev, p = peer(k)
            return pltpu.make_async_remote_copy(src_ref=red, dst_ref=rows(o_ref, me), send_sem=s2.at[k],
                                                recv_sem=r2.at[k], device_id=dev, device_id_type=MESH_ID)

        def shared_from(k):
            dev, p = peer(k)
            return pltpu.make_async_remote_copy(src_ref=red, dst_ref=rows(o_ref, p), send_sem=s2.at[k],
                                                recv_sem=r2.at[k], device_id=dev, device_id_type=MESH_ID)

        for k in range(1, 8):
            scatter(k).start()
        acc = rows(x_ref, me)[...]
        for k in range(1, 8):
            scatter(k).wait_recv()
            acc = acc + recv[k]
        red[...] = acc
        rows(o_ref, me)[...] = acc
        for k in range(1, 8):
            share(k).start()
        for k in range(1, 8):
            shared_from(k).wait_recv()
        for k in range(1, 8):
            scatter(k).wait_send()
            share(k).wait_send()

    return pl.pallas_call(
        body, name=name,
        in_specs=[pl.BlockSpec(memory_space=pltpu.VMEM)],
        out_specs=pl.BlockSpec(memory_space=pltpu.VMEM),
        out_shape=jax.ShapeDtypeStruct((R, C), F32),
        scratch_shapes=[pltpu.VMEM((8, m, C), F32), pltpu.VMEM((m, C), F32),
                        pltpu.SemaphoreType.DMA((8,)), pltpu.SemaphoreType.DMA((8,)),
                        pltpu.SemaphoreType.DMA((8,)), pltpu.SemaphoreType.DMA((8,))],
        compiler_params=_params(),
    )(buf)


def _bounce(src, dst, buf, sem):
    cin = pltpu.make_async_copy(src, buf, sem)
    cin.start()
    cin.wait()
    cout = pltpu.make_async_copy(buf, dst, sem)
    cout.start()
    cout.wait()


def _chunk(ref, axis, idx, size):
    start = idx * size
    if axis == 0:
        return ref.at[pl.ds(start, size), :]
    return ref.at[:, pl.ds(start, size)]


def _in_proj_gather(hl, win, wout, q_arr, *, rows, tm):
    D, nq = win.shape
    dq, D2 = wout.shape
    ni = rows // tm
    ops = ((0, 1, nq, D // 2), (1, 0, dq, dq // 2))

    def body(q_ref, a_ref, win_ref, wout_ref, p_ref, gin_ref, gout_ref, b_scr, buf_out, lsem, ssem, rsem, fsem, gsem):
        j, i = pl.program_id(0), pl.program_id(1)
        x, y, c = _pos()
        q = 2 * x + y
        srcs = (win_ref, wout_ref)
        dsts = (gin_ref, gout_ref)

        def shard_window(o, chip):
            _, axis, size, _ = ops[o]
            return _chunk(dsts[o], axis, chip, size)

        def half(ref, o, core):
            return ref.at[pl.ds(core * ops[o][3], ops[o][3]), :]

        def half_window(o, chip, core):
            _, axis, size, hs = ops[o]
            if axis == 1:
                return dsts[o].at[pl.ds(core * hs, hs), pl.ds(chip * size, size)]
            return dsts[o].at[pl.ds(chip * size + core * hs, hs), :]

        def chip_of(k):
            px, py = _flip(x, (k >> 1) & 1), _flip(y, k & 1)
            return px, py, 2 * px + py

        def send(o, k):
            px, py, _ = chip_of(k)
            return pltpu.make_async_remote_copy(
                src_ref=half(srcs[o], o, c), dst_ref=half_window(o, q, c), send_sem=ssem.at[o, k],
                recv_sem=rsem.at[o, k], device_id=(px, py, c), device_id_type=MESH_ID)

        def chip_recv(o, k):
            px, py, pq = chip_of(k)
            landed = half_window(o, pq, c)
            pltpu.make_async_remote_copy(src_ref=landed, dst_ref=landed, send_sem=ssem.at[o, k], recv_sem=rsem.at[o, k],
                                         device_id=(px, py, c), device_id_type=MESH_ID).wait_recv()

        def to_sibling(o, k):
            landed = half_window(o, chip_of(k)[2], c)
            return pltpu.make_async_remote_copy(src_ref=landed, dst_ref=landed, send_sem=fsem.at[o, k],
                                                recv_sem=gsem.at[o, k], device_id=(x, y, 1 - c), device_id_type=MESH_ID)

        def from_sibling(o, k):
            theirs = half_window(o, chip_of(k)[2], 1 - c)
            pltpu.make_async_remote_copy(src_ref=theirs, dst_ref=theirs, send_sem=fsem.at[o, k], recv_sem=gsem.at[o, k],
                                         device_id=(x, y, 1 - c), device_id_type=MESH_ID).wait_recv()

        def relay(o, core):
            if core == 0:
                landed, target = half_window(o, chip_of(2)[2], 0), (x, 1 - y, 0)
            else:
                landed, target = half_window(o, chip_of(1)[2], 1), (1 - x, y, 1)
            return pltpu.make_async_remote_copy(src_ref=landed, dst_ref=landed, send_sem=ssem.at[o, 3],
                                                recv_sem=rsem.at[o, 3], device_id=target, device_id_type=MESH_ID)

        def on_core(core, fn):
            @pl.when(c == core)
            def _():
                fn()

        def land(o, k):
            chip_recv(o, k)
            if k == 2:
                on_core(0, lambda: relay(o, 0).start())
            if k == 1:
                on_core(1, lambda: relay(o, 1).start())
            to_sibling(o, k).start()

        def settle(o, k):
            from_sibling(o, k)
            to_sibling(o, k).wait_send()

        def b_load(k, slot):
            src = win_ref if k == 0 else shard_window(0, chip_of(k)[2])
            return pltpu.make_async_copy(src, b_scr.at[slot], lsem.at[0])

        def own_store():
            return pltpu.make_async_copy(b_scr.at[0], shard_window(0, q), lsem.at[2])

        order = (0, 2, 1, 3)
        early = max(ni - 2, 0)

        @pl.when((j == 0) & (i == 0))
        def _():
            for o in range(2):
                for k in (2, 1):
                    send(o, k).start()
            first = b_load(0, 0)
            first.start()
            first.wait()
            own_store().start()
            _bounce(wout_ref, shard_window(1, q), buf_out, lsem.at[1])

        for jj in range(3):
            nxt = order[jj + 1]

            @pl.when((j == jj) & (i == early))
            def _(nxt=nxt):
                land(0, nxt)

            @pl.when((j == jj) & (i == ni - 1))
            def _(jj=jj, nxt=nxt):
                settle(0, nxt)
                if jj == 1:
                    own_store().wait()
                b_load(nxt, (jj + 1) % 2).start()

            @pl.when((j == jj + 1) & (i == 0))
            def _(jj=jj, nxt=nxt):
                b_load(nxt, (jj + 1) % 2).wait()

        @pl.when((j == 3) & (i == 0))
        def _():
            land(1, 2)
            land(1, 1)

        p_ref[...] = jnp.dot(a_ref[...], b_scr[j % 2], preferred_element_type=F32).astype(p_ref.dtype)

        @pl.when((j == 3) & (i == ni - 1))
        def _():
            settle(1, 2)
            settle(1, 1)
            land(1, 3)
            settle(1, 3)
            for o in range(2):
                for k in (2, 1):
                    send(o, k).wait_send()
                for core in range(2):
                    on_core(core, lambda o=o, core=core: relay(o, core).wait_send())

    hbm = pl.BlockSpec(memory_space=pl.ANY)
    grid_spec = pltpu.PrefetchScalarGridSpec(
        num_scalar_prefetch=1, grid=(4, ni),
        in_specs=[pl.BlockSpec((tm, D), lambda j, i, qr: (i, 0)), hbm, hbm],
        out_specs=[pl.BlockSpec((tm, nq), lambda j, i, qr: (i, jnp.bitwise_xor(qr[0], ((j & 1) << 1) | (j >> 1)))),
                   hbm, hbm],
        scratch_shapes=[pltpu.VMEM((2,) + win.shape, win.dtype), pltpu.VMEM(wout.shape, wout.dtype), pltpu.SemaphoreType.DMA((3,))]
        + [pltpu.SemaphoreType.DMA((2, 4)) for _ in range(4)])
    return pl.pallas_call(
        body, name="in_proj_gather", grid_spec=grid_spec,
        out_shape=[jax.ShapeDtypeStruct((rows, 4 * nq), BF16), jax.ShapeDtypeStruct((D, 4 * nq), win.dtype),
                   jax.ShapeDtypeStruct((4 * dq, D2), wout.dtype)],
        compiler_params=_params(("arbitrary", "arbitrary")),
    )(q_arr, hl, win, wout)


def _rs_to_sibling(gs, axes):
    n = len(gs)
    shapes = []
    for g, ax in zip(gs, axes):
        s = list(g.shape)
        s[ax] //= 8
        shapes.append(tuple(s))

    def body(*refs):
        g_refs, mine, landed = refs[:n], refs[n:2 * n], refs[2 * n:3 * n]
        bufs = refs[3 * n:4 * n]
        lsem, ssem, rsem = refs[4 * n:]
        x, y, c = _pos()
        cps = []
        for o in range(n):
            size = shapes[o][axes[o]]
            for j in range(4):
                rc = pltpu.make_async_remote_copy(
                    src_ref=_chunk(g_refs[o], axes[o], 2 * j + 1 - c, size), dst_ref=landed[o].at[j],
                    send_sem=ssem.at[o, j], recv_sem=rsem.at[o, j], device_id=(x, y, 1 - c), device_id_type=MESH_ID)
                rc.start()
                cps.append(rc)
        for o in range(n):
            size = shapes[o][axes[o]]
            for j in range(4):
                _bounce(_chunk(g_refs[o], axes[o], 2 * j + c, size), mine[o].at[j], bufs[o], lsem.at[o])
        for rc in cps:
            rc.wait()

    hbm = pl.BlockSpec(memory_space=pl.ANY)
    outs = [jax.ShapeDtypeStruct((4,) + s, g.dtype) for s, g in zip(shapes, gs)]
    res = pl.pallas_call(
        body, name="rs_to_sibling", in_specs=[hbm] * n, out_specs=[hbm] * (2 * n), out_shape=outs + outs,
        scratch_shapes=[pltpu.VMEM(s, g.dtype) for s, g in zip(shapes, gs)]
        + [pltpu.SemaphoreType.DMA((n,)), pltpu.SemaphoreType.DMA((n, 4)), pltpu.SemaphoreType.DMA((n, 4))],
        compiler_params=_params(),
    )(*gs)
    return res[:n], res[n:]


def _rs_chips_side(parts):
    n = len(parts)

    def copies(p_refs, slots, scr):
        ssem, rsem = scr[n + 1], scr[n + 2]
        x, y, c = _pos()
        cps = []
        for o in range(n):
            for k in range(1, 4):
                px, py = _flip(x, (k >> 1) & 1), _flip(y, k & 1)
                cps.append(pltpu.make_async_remote_copy(
                    src_ref=p_refs[o].at[2 * px + py], dst_ref=slots[o].at[k], send_sem=ssem.at[o, k],
                    recv_sem=rsem.at[o, k], device_id=(px, py, c), device_id_type=MESH_ID))
        return cps

    def start(p_refs, slots, scr):
        for cp in copies(p_refs, slots, scr):
            cp.start()

    def finish(p_refs, slots, scr):
        x, y, _ = _pos()
        q = 2 * x + y
        for o in range(n):
            _bounce(p_refs[o].at[q], slots[o].at[0], scr[o], scr[n].at[o])
        for cp in copies(p_refs, slots, scr):
            cp.wait()

    return dict(
        ins=list(parts), outs=[jax.ShapeDtypeStruct(p.shape, p.dtype) for p in parts],
        scratch=[pltpu.VMEM(p.shape[1:], p.dtype) for p in parts]
        + [pltpu.SemaphoreType.DMA((n,)), pltpu.SemaphoreType.DMA((n, 4)), pltpu.SemaphoreType.DMA((n, 4))],
        start=start, finish=finish)


def _rs_share(rs, axes):
    n = len(rs)
    shapes = []
    for r, ax in zip(rs, axes):
        s = list(r.shape)
        s[ax] *= 2
        shapes.append(tuple(s))

    def body(*refs):
        r_refs, outs = refs[:n], refs[n:2 * n]
        bufs = refs[2 * n:3 * n]
        lsem, ssem, rsem = refs[3 * n:]
        x, y, c = _pos()
        cps = []
        for o in range(n):
            size = r_refs[o].shape[axes[o]]
            window = _chunk(outs[o], axes[o], c, size)
            rc = pltpu.make_async_remote_copy(src_ref=r_refs[o], dst_ref=window, send_sem=ssem.at[o], recv_sem=rsem.at[o],
                                              device_id=(x, y, 1 - c), device_id_type=MESH_ID)
            rc.start()
            cps.append(rc)
        for o in range(n):
            size = r_refs[o].shape[axes[o]]
            _bounce(r_refs[o], _chunk(outs[o], axes[o], c, size), bufs[o], lsem.at[o])
        for cp in cps:
            cp.wait()

    hbm = pl.BlockSpec(memory_space=pl.ANY)
    return pl.pallas_call(
        body, name="rs_share", in_specs=[hbm] * n, out_specs=[hbm] * n,
        out_shape=[jax.ShapeDtypeStruct(s, r.dtype) for s, r in zip(shapes, rs)],
        scratch_shapes=[pltpu.VMEM(r.shape, r.dtype) for r in rs] + [pltpu.SemaphoreType.DMA((n,)) for _ in range(3)],
        compiler_params=_params(),
    )(*rs)


def _rs_pair_sums(gs, axes):
    mine, landed = _rs_to_sibling(gs, axes)
    pair_sums = []
    for o, (mi, la) in enumerate(zip(mine, landed)):
        rows, cols = mi.shape[0] * mi.shape[1], mi.shape[2]
        s = _elementwise(lambda a, b: a.astype(F32) + b.astype(F32), [(mi.reshape(rows, cols), 0), (la.reshape(rows, cols), 0)],
                         [BF16], rows=rows, cols=cols, name=f"rs_pair_sum{o}")[0]
        pair_sums.append(s.reshape(mi.shape))
    return pair_sums


def _rs_finish(slots, axes):
    reduced = []
    for o, sl in enumerate(slots):
        rows, cols = sl.shape[1], sl.shape[2]
        flat = sl.reshape(4 * rows, cols)
        r = _elementwise(lambda a, b, c, d: (a.astype(F32) + b.astype(F32)) + (c.astype(F32) + d.astype(F32)),
                         [(flat, k * rows) for k in range(4)], [F32], rows=rows, cols=cols, name=f"rs_chip_sum{o}")[0]
        reduced.append(r)
    return _rs_share(reduced, axes)


def _norm_in(x, ctx, g, sc_l, sh_l, sc_c, sh_c, tr):
    L, D = x.shape
    T = ctx.shape[0]
    nx, nc = L // tr, T // tr

    def body(x_ref, c_ref, g_ref, scl, shl, scc, shc, o_ref):
        i = pl.program_id(0)

        def run(src, sc, sh):
            v = src[...]
            r = lax.rsqrt(jnp.mean(v * v, axis=-1, keepdims=True) + EPS)
            o_ref[...] = ((v * r * g_ref[...]) * (1.0 + sc[...]) + sh[...]).astype(o_ref.dtype)

        @pl.when(i < nx)
        def _():
            run(x_ref, scl, shl)

        @pl.when(i >= nx)
        def _():
            run(c_ref, scc, shc)

    vec = pl.BlockSpec((1, D), lambda i: (0, 0))
    return pl.pallas_call(
        body, name="norm_in", grid=(nx + nc,),
        in_specs=[pl.BlockSpec((tr, D), lambda i: (jnp.minimum(i, nx - 1), 0)),
                  pl.BlockSpec((tr, D), lambda i: (jnp.maximum(i - nx, 0), 0)), vec, vec, vec, vec, vec],
        out_specs=pl.BlockSpec((tr, D), lambda i: (i, 0)),
        out_shape=jax.ShapeDtypeStruct((L + T, D), BF16),
        compiler_params=_params(("arbitrary",)),
    )(x, ctx, g, sc_l, sh_l, sc_c, sh_c)


def _tmod(tl, row_w):
    assert row_w & (row_w - 1) == 0
    return lax.broadcasted_iota(jnp.int32, (tl, 1), 0) & (row_w - 1)


def _shift(z, k, tmod, row_w):
    tl = z.shape[0]
    rolled = pltpu.roll(z, k % tl, 0)
    mask = (tmod >= k) if k > 0 else (tmod < row_w + k)
    return jnp.where(mask, rolled, 0.0)


def _conv(z, w_ref, taps, left, tmod, row_w, lanes=slice(None)):
    out = None
    for j in range(taps):
        k = left - j
        term = (z if k == 0 else _shift(z, k, tmod, row_w)) * w_ref[j:j + 1, lanes]
        out = term if out is None else out + term
    return out


def _conv_bwd(dz, z, w_ref, taps, left, tmod, row_w, lanes=slice(None)):
    din = None
    dws = []
    for j in range(taps):
        k = left - j
        shifted = dz if k == 0 else _shift(dz, -k, tmod, row_w)
        term = shifted * w_ref[j:j + 1, lanes]
        din = term if din is None else din + term
        dws.append(jnp.sum(shifted * z, axis=0, keepdims=True))
    return din, dws


def _gate_matmul(xb16_ref, wd_ref, pre_scr, W, ng, gs):
    for g in range(ng):
        pg = jnp.dot(xb16_ref[:, g * gs:(g + 1) * gs], wd_ref[g], preferred_element_type=F32)
        pre_scr[:, g * gs:(g + 1) * gs] = pg[:, :gs]
        pre_scr[:, W + g * gs:W + (g + 1) * gs] = pg[:, gs:]


def _f32(ref, rows, lanes):
    return ref[rows, lanes].astype(F32)


def _sub_loop(tl, sub_r, W, fn):
    def chunk(ci, carry):
        r0 = pl.multiple_of(ci * sub_r, sub_r)
        for lb in range(W // LANES):
            fn(r0, lb * LANES)
        return carry

    lax.fori_loop(0, tl // sub_r, chunk, 0)


def _row_loop(tl, rev, step, init):
    nchunk = tl // SUBLANES

    def chunk(j, carry):
        jj = (nchunk - 1 - j) if rev else j
        c0 = pl.multiple_of(jj * SUBLANES, SUBLANES)
        for r in (range(SUBLANES - 1, -1, -1) if rev else range(SUBLANES)):
            carry = step(c0 + r, carry)
        return carry

    return lax.fori_loop(0, nchunk, chunk, init)


def _mix_fwd(P, d, h_init, wts, *, rows, row_off, row_w, tl, saved0=None, name):
    W = P.shape[1] // 6
    nt = rows // tl
    ob = row_off // tl
    rev = d == 1
    gs = min(LRU_GROUP, W)
    ng = W // gs
    wca, wcb, bcb = wts["wca"], wts["wcb"], wts["bcb"]
    wd, ba, bx, lam = wts["wd"][d], wts["ba"][d], wts["bx"][d], wts["lam"][d]

    def tile(i):
        return (nt - 1 - i) if rev else i

    def pcol(j):
        return pl.BlockSpec((tl, W), lambda i: (tile(i) + ob, j))

    vec = pl.BlockSpec((1, W), lambda i: (0, 0))
    taps = pl.BlockSpec((SUBLANES, W), lambda i: (0, 0))
    wd_spec = pl.BlockSpec(wd.shape, lambda i: (0, 0, 0))
    seq = pl.BlockSpec((tl, W), lambda i: (tile(i), 0))

    sub_r = min(row_w, tl)
    assert tl % sub_r == 0

    def body(*refs):
        if rev:
            (bl, cl, ul, gl, ql, ho, xb_r, xb16_r, wca_r, wd_r, ba_r, bx_r, lam_r, hin, hseq, cat, a_o, r_o, ig_o, m2_o,
             b_scr, pre_scr, carry, sp_scr) = refs
        else:
            (vl, wcb_r, bcb_r, wd_r, ba_r, bx_r, lam_r, hin, hseq, xb_r, xb16_r, a_o, r_o, ig_o, m2_o,
             b_scr, pre_scr, carry, sp_scr) = refs
        i = pl.program_id(0)

        @pl.when(i == 0)
        def _():
            carry[...] = hin[...]

        sp_scr[...] = _softplus(-lam_r[...])
        tmod = _tmod(sub_r, sub_r)

        def conv_in(r0, l0):
            rs, ls = pl.ds(r0, sub_r), pl.ds(l0, LANES)
            xb = _conv(_f32(vl, rs, ls), wcb_r, 4, 2, tmod, sub_r, ls) + bcb_r[:, ls]
            xb_r[rs, ls] = xb
            xb16_r[rs, ls] = xb.astype(BF16)

        def gates(r0, l0):
            rs, ls = pl.ds(r0, sub_r), pl.ds(l0, LANES)
            r, ig, a, m2 = _gates(pre_scr[rs, ls] + ba_r[:, ls], pre_scr[rs, pl.ds(W + l0, LANES)] + bx_r[:, ls],
                                  sp_scr[:, ls])
            a_o[rs, ls] = a
            r_o[rs, ls] = r.astype(r_o.dtype)
            ig_o[rs, ls] = ig.astype(ig_o.dtype)
            m2_o[rs, ls] = m2.astype(m2_o.dtype)
            m = jnp.where(m2 > 0.0, m2 * lax.rsqrt(m2), 0.0)
            b_scr[rs, ls] = m * (ig * xb_r[rs, ls])

        if not rev:
            _sub_loop(tl, sub_r, W, conv_in)
        _gate_matmul(xb16_r, wd_r, pre_scr, W, ng, gs)
        _sub_loop(tl, sub_r, W, gates)

        def step(t, h):
            h = a_o[pl.ds(t, 1), :] * h + b_scr[pl.ds(t, 1), :]
            hseq[pl.ds(t, 1), :] = h
            return h

        carry[...] = _row_loop(tl, rev, step, carry[...])

        if rev:
            def mix_out(r0, l0):
                rs, ls = pl.ds(r0, sub_r), pl.ds(l0, LANES)
                yb = (ho[rs, ls] + hseq[rs, ls]) * _silu(_f32(ql, rs, ls))
                ya = (_f32(bl, rs, ls) * _conv(_f32(cl, rs, ls) * _f32(ul, rs, ls), wca_r, 3, 1, tmod, sub_r, ls)
                      * _silu(_f32(gl, rs, ls)))
                cat[rs, ls] = ya.astype(cat.dtype)
                cat[rs, pl.ds(W + l0, LANES)] = yb.astype(cat.dtype)

            _sub_loop(tl, sub_r, W, mix_out)

    scratch = [pltpu.VMEM((tl, W), F32), pltpu.VMEM((tl, 2 * W), F32), pltpu.VMEM((1, W), F32), pltpu.VMEM((1, W), F32)]
    f32_seq = jax.ShapeDtypeStruct((rows, W), F32)
    kept_gates = [f32_seq] + [jax.ShapeDtypeStruct((rows, W), BF16)] * 3
    if rev:
        in_specs = [pcol(j) for j in (0, 1, 2, 3, 5)] + [seq, seq, seq, taps, wd_spec, vec, vec, vec, vec]
        args = [P] * 5 + [saved0["h"], saved0["xb"], saved0["xb16"], wca, wd, ba, bx, lam, h_init]
        out_specs = [seq, pl.BlockSpec((tl, 2 * W), lambda i: (tile(i), 0))] + [seq] * 4
        out_shape = [f32_seq, jax.ShapeDtypeStruct((rows, 2 * W), BF16)] + kept_gates
    else:
        in_specs = [pcol(4), taps, vec, wd_spec, vec, vec, vec, vec]
        args = [P, wcb, bcb, wd, ba, bx, lam, h_init]
        out_specs = [seq] * 7
        out_shape = [f32_seq, f32_seq, jax.ShapeDtypeStruct((rows, W), BF16)] + kept_gates
    res = pl.pallas_call(
        body, name=name, grid=(nt,), in_specs=in_specs, out_specs=out_specs, out_shape=out_shape,
        scratch_shapes=scratch, compiler_params=_params(("arbitrary",)),
    )(*args)
    gates = dict(zip(("a", "r", "ig", "m2"), res[-4:]))
    if rev:
        return res[0], res[1], gates
    return dict(h=res[0], xb=res[1], xb16=res[2]), gates


_BWD_SCRATCH = ("dyl", "g", "dp16", "sp", "dlf", "edge", "c")
_FWD_SAVED = ("xb", "xb16", "a", "r", "ig", "m2")


def _bwd_scratch(tl, W):
    shapes = {"dyl": pltpu.VMEM((tl, W), F32), "g": pltpu.VMEM((tl, W), F32), "dp16": pltpu.VMEM((tl, 2 * W), BF16),
              "sp": pltpu.VMEM((1, W), F32), "dlf": pltpu.VMEM((1, W), F32), "edge": pltpu.VMEM((1, W), F32),
              "c": pltpu.VMEM((1, W), F32)}
    return [shapes[n] for n in _BWD_SCRATCH]


def _lru_bwd_tile(d, dy_fn, hs_ref, wd_r, lam_r, scr, acc, first, tl, sub_r, ng, gs):
    dwd_ref, dba_ref, dbx_ref, dlam_ref = acc
    W = hs_ref.shape[1]
    assert gs % LANES == 0
    rev = d == 0
    lam = lam_r[...]
    scr["sp"][...] = _softplus(-lam)
    scr["dlf"][...] = -_sigmoid(-lam)

    @pl.when(first)
    def _():
        dwd_ref[...] = jnp.zeros_like(dwd_ref)
        dba_ref[...] = jnp.zeros_like(dba_ref)
        dbx_ref[...] = jnp.zeros_like(dbx_ref)
        dlam_ref[...] = jnp.zeros_like(dlam_ref)

    def state_grad(r0, l0):
        rs, ls = pl.ds(r0, sub_r), pl.ds(l0, LANES)
        scr["dyl"][rs, ls] = dy_fn(rs, ls)

    _sub_loop(tl, sub_r, W, state_grad)

    def step(t, c):
        g = scr["dyl"][pl.ds(t, 1), :] + c
        scr["g"][pl.ds(t, 1), :] = g
        return scr["a"][pl.ds(t, 1), :] * g

    scr["c"][...] = _row_loop(tl, rev, step, scr["c"][...])
    row = lax.broadcasted_iota(jnp.int32, (sub_r, 1), 0)

    def grads(r0, l0):
        rs, ls = pl.ds(r0, sub_r), pl.ds(l0, LANES)
        g, a, m2 = scr["g"][rs, ls], scr["a"][rs, ls], _f32(scr["m2"], rs, ls)
        r, ig, xb = _f32(scr["r"], rs, ls), _f32(scr["ig"], rs, ls), scr["xb"][rs, ls]
        h = hs_ref[rs, ls]
        if d == 0:
            e0 = pl.multiple_of(jnp.maximum(r0 - SUBLANES, 0), SUBLANES)
            edge = jnp.where(r0 == 0, scr["edge"][:, ls], hs_ref[pl.ds(e0, SUBLANES), ls][SUBLANES - 1:, :])
            hprev = jnp.where(row == 0, edge, pltpu.roll(h, 1, 0))
        else:
            e0 = pl.multiple_of(jnp.minimum(r0 + sub_r, tl - SUBLANES), SUBLANES)
            edge = jnp.where(r0 == tl - sub_r, scr["edge"][:, ls], hs_ref[pl.ds(e0, SUBLANES), ls][:1, :])
            hprev = jnp.where(row == sub_r - 1, edge, pltpu.roll(h, sub_r - 1, 0))
        rsq = lax.rsqrt(m2)
        gm = g * (m2 * rsq)
        d_la = (g * hprev) * a - (g * (ig * xb)) * ((1.0 - m2) * rsq)
        d_pr = d_la * ((-LRU_C) * scr["sp"][:, ls]) * (r * (1.0 - r))
        d_pi = (gm * xb) * (ig * (1.0 - ig))
        scr["dyl"][rs, ls] = gm * ig
        dlam_ref[:, ls] += jnp.sum(d_la * ((-LRU_C) * r), axis=0, keepdims=True) * scr["dlf"][:, ls]
        dba_ref[:, ls] += jnp.sum(d_pr, axis=0, keepdims=True)
        dbx_ref[:, ls] += jnp.sum(d_pi, axis=0, keepdims=True)
        gi, off = divmod(l0, gs)
        scr["dp16"][rs, pl.ds(gi * 2 * gs + off, LANES)] = d_pr.astype(BF16)
        scr["dp16"][rs, pl.ds(gi * 2 * gs + gs + off, LANES)] = d_pi.astype(BF16)

    _sub_loop(tl, sub_r, W, grads)
    for gi in range(ng):
        dp = scr["dp16"][:, gi * 2 * gs:(gi + 1) * 2 * gs]
        scr["g"][:, gi * gs:(gi + 1) * gs] = lax.dot_general(dp, wd_r[gi], (((1,), (1,)), ((), ())),
                                                             preferred_element_type=F32)
        dwd_ref[gi] += lax.dot_general(scr["xb16"][:, gi * gs:(gi + 1) * gs], dp, (((0,), (0,)), ((), ())),
                                       preferred_element_type=F32)


def _edge_block(h, tl, nt, d):
    W = h.shape[1]
    per = tl // SUBLANES
    if d == 0:
        return pl.BlockSpec((SUBLANES, W), lambda i: (jnp.maximum((nt - 1 - i) * per - 1, 0), 0))
    return pl.BlockSpec((SUBLANES, W), lambda i: (jnp.minimum((i + 1) * per, nt * per - 1), 0))


def _mix_bwd0(P, dcat, saved0, gates0, h_init, c_init, wts, *, rows, row_off, row_w, tl, name):
    W = P.shape[1] // 6
    nt = rows // tl
    ob = row_off // tl
    gs = min(LRU_GROUP, W)
    ng = W // gs
    wd, lam = wts["wd"][0], wts["lam"][0]
    h0s = saved0["h"]
    kept = [saved0["xb"], saved0["xb16"]] + [gates0[n] for n in ("a", "r", "ig", "m2")]

    def tile(i):
        return nt - 1 - i

    vec = pl.BlockSpec((1, W), lambda i: (0, 0))
    wd_spec = pl.BlockSpec(wd.shape, lambda i: (0, 0, 0))
    seq = pl.BlockSpec((tl, W), lambda i: (tile(i), 0))

    sub_r = min(row_w, tl)
    assert tl % sub_r == 0

    def body(ql, dyb, hs, hedge8, xb_r, xb16_r, a_r, r_r, ig_r, m2_r, wd_r, lam_r, hin, cin,
             dxb_o, dwd_o, dba_o, dbx_o, dlam_o, cfin, *scratch):
        scr = dict(zip(_BWD_SCRATCH, scratch, strict=True))
        scr.update(zip(_FWD_SAVED, (xb_r, xb16_r, a_r, r_r, ig_r, m2_r), strict=True))
        i = pl.program_id(0)

        @pl.when(i == 0)
        def _():
            scr["c"][...] = cin[...]

        scr["edge"][...] = jnp.where(i == nt - 1, hin[...], hedge8[SUBLANES - 1:SUBLANES, :])
        _lru_bwd_tile(0, lambda rs, ls: _f32(dyb, rs, ls) * _silu(_f32(ql, rs, ls)), hs, wd_r, lam_r, scr,
                      (dwd_o, dba_o, dbx_o, dlam_o), i == 0, tl, sub_r, ng, gs)
        dxb_o[...] = scr["dyl"][...] + scr["g"][...]
        cfin[...] = scr["c"][...]

    return pl.pallas_call(
        body, name=name, grid=(nt,),
        in_specs=[pl.BlockSpec((tl, W), lambda i: (tile(i) + ob, 5)), pl.BlockSpec((tl, W), lambda i: (tile(i), 1)), seq,
                  _edge_block(h0s, tl, nt, 0)] + [seq] * 6 + [wd_spec, vec, vec, vec],
        out_specs=[seq, wd_spec, vec, vec, vec, vec],
        out_shape=[jax.ShapeDtypeStruct((rows, W), F32), jax.ShapeDtypeStruct(wd.shape, F32)]
        + [jax.ShapeDtypeStruct((1, W), F32)] * 4,
        scratch_shapes=_bwd_scratch(tl, W),
        compiler_params=_params(("arbitrary",)),
    )(P, dcat, h0s, h0s, *kept, wd, lam, h_init, c_init)


def _mix_bwd1(P, dcat, saved0, h1s, gates1, dxb0, h_init, c_init, wts, *, rows, row_off, row_w, tl, name,
              dp_rows=None, dp_off=0, dp_into=None):
    dp_rows = rows if dp_rows is None else dp_rows
    dpb = dp_off // tl
    W = P.shape[1] // 6
    nt = rows // tl
    ob = row_off // tl
    gs = min(LRU_GROUP, W)
    ng = W // gs
    wca, wcb = wts["wca"], wts["wcb"]
    wd, lam = wts["wd"][1], wts["lam"][1]
    h0s = saved0["h"]
    kept = [saved0["xb"], saved0["xb16"]] + [gates1[n] for n in ("a", "r", "ig", "m2")]

    vec = pl.BlockSpec((1, W), lambda i: (0, 0))
    taps = pl.BlockSpec((SUBLANES, W), lambda i: (0, 0))
    wd_spec = pl.BlockSpec(wd.shape, lambda i: (0, 0, 0))
    seq = pl.BlockSpec((tl, W), lambda i: (i, 0))

    sub_r = min(row_w, tl)
    assert tl % sub_r == 0

    def body(*refs):
        if dp_into is not None:
            refs = refs[1:]
        (bl, cl, ul, gl, vl, ql, dya, dyb, h0, h1, hedge8, dx0, xb_r, xb16_r, a_r, r_r, ig_r, m2_r, wca_r, wcb_r,
         wd_r, lam_r, hin, cin, dp_o, dwd_o, dba_o, dbx_o, dlam_o, dwca_o, dwcb_o, dbcb_o, cfin, *scratch) = refs
        scr = dict(zip(_BWD_SCRATCH, scratch, strict=True))
        scr.update(zip(_FWD_SAVED, (xb_r, xb16_r, a_r, r_r, ig_r, m2_r), strict=True))
        i = pl.program_id(0)

        @pl.when(i == 0)
        def _():
            scr["c"][...] = cin[...]
            dwca_o[...] = jnp.zeros_like(dwca_o)
            dwcb_o[...] = jnp.zeros_like(dwcb_o)
            dbcb_o[...] = jnp.zeros_like(dbcb_o)

        scr["edge"][...] = jnp.where(i == nt - 1, hin[...], hedge8[0:1, :])
        _lru_bwd_tile(1, lambda rs, ls: _f32(dyb, rs, ls) * _silu(_f32(ql, rs, ls)), h1, wd_r, lam_r, scr,
                      (dwd_o, dba_o, dbx_o, dlam_o), i == 0, tl, sub_r, ng, gs)
        cfin[...] = scr["c"][...]
        tmod = _tmod(sub_r, sub_r)

        def rest(r0, l0):
            rs, ls = pl.ds(r0, sub_r), pl.ds(l0, LANES)
            dxb = dx0[rs, ls] + scr["dyl"][rs, ls] + scr["g"][rs, ls]
            dv, dwb = _conv_bwd(dxb, _f32(vl, rs, ls), wcb_r, 4, 2, tmod, sub_r, ls)
            for j in range(4):
                dwcb_o[j:j + 1, ls] += dwb[j]
            dbcb_o[:, ls] += jnp.sum(dxb, axis=0, keepdims=True)
            q = _f32(ql, rs, ls)
            sq = _sigmoid(q)
            dq = _f32(dyb, rs, ls) * (h0[rs, ls] + h1[rs, ls]) * (sq * (1.0 + q * (1.0 - sq)))
            b_, c_, u_, g_ = _f32(bl, rs, ls), _f32(cl, rs, ls), _f32(ul, rs, ls), _f32(gl, rs, ls)
            z = c_ * u_
            cz = _conv(z, wca_r, 3, 1, tmod, sub_r, ls)
            sgm = _sigmoid(g_)
            sg = g_ * sgm
            da = _f32(dya, rs, ls)
            dz, dwa = _conv_bwd(da * b_ * sg, z, wca_r, 3, 1, tmod, sub_r, ls)
            for j in range(3):
                dwca_o[j:j + 1, ls] += dwa[j]
            parts = (da * cz * sg, dz * u_, dz * c_, da * b_ * cz * (sgm * (1.0 + g_ * (1.0 - sgm))), dv, dq)
            for k, val in enumerate(parts):
                dp_o[rs, pl.ds(k * W + l0, LANES)] = val.astype(dp_o.dtype)

        _sub_loop(tl, sub_r, W, rest)

    def pcol(j):
        return pl.BlockSpec((tl, W), lambda i: (i + ob, j))

    prev = [] if dp_into is None else [dp_into]
    return pl.pallas_call(
        body, name=name, grid=(nt,), input_output_aliases={} if dp_into is None else {0: 0},
        in_specs=[pl.BlockSpec(memory_space=pl.ANY)] * len(prev) + [pcol(j) for j in range(6)]
        + [pl.BlockSpec((tl, W), lambda i: (i, 0)), pl.BlockSpec((tl, W), lambda i: (i, 1)), seq, seq,
           _edge_block(h1s, tl, nt, 1), seq] + [seq] * 6 + [taps, taps, wd_spec, vec, vec, vec],
        out_specs=[pl.BlockSpec((tl, 6 * W), lambda i: (i + dpb, 0)), wd_spec, vec, vec, vec, taps, taps, vec, vec],
        out_shape=[jax.ShapeDtypeStruct((dp_rows, 6 * W), BF16), jax.ShapeDtypeStruct(wd.shape, F32)]
        + [jax.ShapeDtypeStruct((1, W), F32)] * 3
        + [jax.ShapeDtypeStruct((SUBLANES, W), F32)] * 2 + [jax.ShapeDtypeStruct((1, W), F32)] * 2,
        scratch_shapes=_bwd_scratch(tl, W),
        compiler_params=_params(("arbitrary",)),
    )(*prev, *([P] * 6), dcat, dcat, h0s, h1s, h1s, dxb0, *kept, wca, wcb, wd, lam, h_init, c_init)


def _loss_head(out, x, tgt, gt, fg, tr):
    L, D = x.shape

    def body(o_ref, x_ref, t_ref, gt_ref, fg_ref, dn_o, do_o, dfg_o, dgt_o, loss_o):
        i = pl.program_id(0)

        @pl.when(i == 0)
        def _():
            dfg_o[...] = jnp.zeros_like(dfg_o)
            dgt_o[...] = jnp.zeros_like(dgt_o)
            loss_o[...] = jnp.zeros_like(loss_o)

        o = o_ref[...].astype(F32)
        gt_v = gt_ref[...]
        fg_v = fg_ref[...]
        n = x_ref[...] + gt_v * o
        r = lax.rsqrt(jnp.mean(n * n, axis=-1, keepdims=True) + EPS)
        nr = n * r
        e = nr * fg_v - t_ref[...]
        loss_o[...] += 0.5 * jnp.sum(jnp.mean(e * e, axis=-1, keepdims=True))
        dy = e * (1.0 / D)
        dfg_o[...] += jnp.sum(dy * nr, axis=0, keepdims=True)
        qv = dy * fg_v
        dn = r * (qv - nr * jnp.mean(qv * nr, axis=-1, keepdims=True))
        dgt_o[...] += jnp.sum(dn * o, axis=0, keepdims=True)
        dn_o[...] = dn.astype(dn_o.dtype)
        do_o[...] = (dn * gt_v).astype(do_o.dtype)

    blk = pl.BlockSpec((tr, D), lambda i: (i, 0))
    vec = pl.BlockSpec((1, D), lambda i: (0, 0))
    return pl.pallas_call(
        body, name="loss_head", grid=(L // tr,), in_specs=[blk, blk, blk, vec, vec],
        out_specs=[blk, blk, vec, vec, pl.BlockSpec((SUBLANES, LANES), lambda i: (0, 0))],
        out_shape=[jax.ShapeDtypeStruct((L, D), BF16), jax.ShapeDtypeStruct((L, D), BF16),
                   jax.ShapeDtypeStruct((1, D), F32), jax.ShapeDtypeStruct((1, D), F32),
                   jax.ShapeDtypeStruct((SUBLANES, LANES), F32)],
        compiler_params=_params(("arbitrary",)),
    )(out, x, tgt, gt, fg)


def _norm_bwd(dhl, x, dn, g, sc, tr, name):
    L, D = x.shape
    with_x = dn is not None

    def body(*refs):
        if with_x:
            d_ref, x_ref, dn_ref, g_ref, sc_ref, gx_o, dsh_o, dsc_o, dg_o = refs
        else:
            d_ref, x_ref, g_ref, sc_ref, dsh_o, dsc_o, dg_o = refs
        i = pl.program_id(0)

        @pl.when(i == 0)
        def _():
            dsh_o[...] = jnp.zeros_like(dsh_o)
            dsc_o[...] = jnp.zeros_like(dsc_o)
            dg_o[...] = jnp.zeros_like(dg_o)

        d = d_ref[...].astype(F32)
        xv = x_ref[...]
        g_v = g_ref[...]
        r = lax.rsqrt(jnp.mean(xv * xv, axis=-1, keepdims=True) + EPS)
        xr = xv * r
        dsh_o[...] += jnp.sum(d, axis=0, keepdims=True)
        dsc_o[...] += jnp.sum(d * (xr * g_v), axis=0, keepdims=True)
        dxn = d * (1.0 + sc_ref[...])
        dg_o[...] += jnp.sum(dxn * xr, axis=0, keepdims=True)
        if with_x:
            qv = dxn * g_v
            gx_o[...] = r * (qv - xr * jnp.mean(qv * xr, axis=-1, keepdims=True)) + dn_ref[...].astype(F32)

    blk = pl.BlockSpec((tr, D), lambda i: (i, 0))
    vec = pl.BlockSpec((1, D), lambda i: (0, 0))
    vshape = jax.ShapeDtypeStruct((1, D), F32)
    res = pl.pallas_call(
        body, name=name, grid=(L // tr,),
        in_specs=[blk, blk] + ([blk] if with_x else []) + [vec, vec],
        out_specs=([blk] if with_x else []) + [vec, vec, vec],
        out_shape=([jax.ShapeDtypeStruct((L, D), F32)] if with_x else []) + [vshape] * 3,
        compiler_params=_params(("arbitrary",)),
    )(*([dhl, x] + ([dn] if with_x else []) + [g, sc]))
    return res if with_x else [None] + list(res)


def _pack_blockdiag(wa, wx, gs):
    H, hd, _ = wa.shape
    hp = gs // hd
    ng = H // hp
    eye = jnp.eye(hp, dtype=wa.dtype)

    def bd(w):
        return jnp.einsum("gpij,pq->gpiqj", w.reshape(ng, hp, hd, hd), eye).reshape(ng, gs, gs)

    return jnp.concatenate([bd(wa), bd(wx)], axis=-1).astype(BF16)


def _unpack_blockdiag(dwd, H, hd, gs):
    hp = gs // hd
    ng = H // hp
    eye = jnp.eye(hp, dtype=dwd.dtype)

    def diag(dm):
        return jnp.einsum("gpiqj,pq->gpij", dm.reshape(ng, hp, hd, hp, hd), eye).reshape(H, hd, hd)

    return diag(dwd[:, :, :gs]), diag(dwd[:, :, gs:])


def kernel(x, c, ctx, c_ctx, norm_g, w_ada, b_ada, w_in, w_conv_a, w_conv_b, b_conv_b, lru_wa, lru_ba, lru_wx, lru_bx, lru_lambda, w_out, final_g, loss_target, m_c_ctx, m_norm_g, m_w_ada, m_b_ada, m_w_in, m_w_conv_a, m_w_conv_b, m_b_conv_b, m_lru_wa, m_lru_ba, m_lru_wx, m_lru_bx, m_lru_lambda, m_w_out, m_final_g, v_c_ctx, v_norm_g, v_w_ada, v_b_ada, v_w_in, v_w_conv_a, v_w_conv_b, v_b_conv_b, v_lru_wa, v_lru_ba, v_lru_wx, v_lru_bx, v_lru_lambda, v_w_out, v_final_g):
    xi, yi, ci = _pos()
    me = 4 * xi + 2 * yi + ci
    q = 2 * xi + yi
    first_core = (ci == 0).astype(F32)

    L, D = x.shape[1], x.shape[2]
    T = ctx.shape[1]
    W = D // 2
    Wq = W // 4
    H, hd = lru_wa.shape[2], lru_wa.shape[3]
    gs = min(LRU_GROUP, W)
    nq = w_ada.shape[2]
    tl = min(256, T, L)
    tr = min(256, T, L)
    x2, ctx2, tgt2 = x[0], ctx[0], loss_target[0]

    def place(shard, full_cols):
        z = jnp.zeros((shard.shape[0], full_cols), F32)
        return lax.dynamic_update_slice(z, shard * first_core, (0, q * shard.shape[1]))

    c_rows = lax.dynamic_update_slice(jnp.zeros((8, D), F32), c, (me, 0))
    small_in = [c_rows, place(w_conv_a[0], W), place(w_conv_b[0], W), place(lru_ba[0], W), place(lru_bx[0], W),
                place(lru_lambda[0], W)]
    small_shapes = [a.shape for a in small_in]
    gathered = _allreduce8(_pack(small_in, 8 * SUBLANES), "gather_small")
    c_all, wca, wcb, ba_all, bx_all, lam_all = _unpack(gathered, small_shapes)

    s_rows = jnp.concatenate([c_all, c_ctx[None, :], jnp.zeros((7, D), F32)], axis=0)
    mod_part = _matmul(s_rows, w_ada[0], a_act="silu", bias=lax.dynamic_slice(b_ada, (0, q * nq), (1, nq)),
                       tm=16, tn=nq, tk=512, name="ada_fwd")
    mod_all = _allreduce8(_pack([place(mod_part, 4 * nq)], 8 * SUBLANES), "gather_mod")
    mod_all = _unpack(mod_all, [(16, 4 * nq)])[0]
    mod_l = lax.dynamic_slice(mod_all, (me, 0), (1, 3 * D))
    mod_c = mod_all[8:9]
    sh_l, sc_l, gt_l = mod_l[:, :D], mod_l[:, D:2 * D], mod_l[:, 2 * D:]
    sh_c, sc_c = mod_c[:, :D], mod_c[:, D:2 * D]

    pad_taps = lambda w: jnp.pad(w, ((0, SUBLANES - w.shape[0]), (0, 0)))
    wts = {
        "wca": pad_taps(wca), "wcb": pad_taps(wcb), "bcb": b_conv_b,
        "wd": [_pack_blockdiag(lru_wa[0, d], lru_wx[0, d], gs) for d in range(2)],
        "ba": [ba_all[d:d + 1] for d in range(2)], "bx": [bx_all[d:d + 1] for d in range(2)],
        "lam": [lam_all[d:d + 1] for d in range(2)],
    }

    hl = _norm_in(x2, ctx2, norm_g, sc_l, sh_l, sc_c, sh_c, tr)
    p_lat, win_full, wout_full = _in_proj_gather(hl, w_in[0].astype(BF16), w_out[0].astype(BF16),
                                                 jnp.reshape(q, (1,)).astype(jnp.int32), rows=L, tm=min(1024, L))
    p_ctx = _matmul(hl, win_full, a_rows=T, a_off=L, tm=T, tn=1536, tk=D, out_dtype=BF16, name="in_proj_ctx")
    zero_w = jnp.zeros((1, W), F32)
    ctx0, cgates0 = _mix_fwd(p_ctx, 0, zero_w, wts, rows=T, row_off=0, row_w=T, tl=tl, name="ctx_fwd0")
    c1s, _, cgates1 = _mix_fwd(p_ctx, 1, zero_w, wts, rows=T, row_off=0, row_w=T, tl=tl, saved0=ctx0, name="ctx_fwd1")
    h0_init, h1_init = ctx0["h"][T - 1:T], c1s[0:1]
    lat0, gates0 = _mix_fwd(p_lat, 0, h0_init, wts, rows=L, row_off=0, row_w=GRID_W, tl=tl, name="mix_fwd0")
    h1s, cat, gates1 = _mix_fwd(p_lat, 1, h1_init, wts, rows=L, row_off=0, row_w=GRID_W, tl=tl, saved0=lat0,
                                name="mix_fwd1")
    out = _matmul(cat, wout_full, tm=512, tn=D, tk=2 * W, out_dtype=BF16, name="out_proj")
    dn, dout, dfg, dgt, loss_blk = _loss_head(out, x2, tgt2, gt_l, final_g[None, :], tr)

    dcat = _matmul(dout, wout_full, tb=True, tm=512, tn=2 * W, tk=D, out_dtype=BF16, name="out_proj_bwd")
    gw_out = _matmul(cat, dout, ta=True, tm=1024, tn=D, tk=2048, out_dtype=BF16, name="w_out_grad")
    dxb0, dwd0, dba0, dbx0, dlam0, ch0 = _mix_bwd0(p_lat, dcat, lat0, gates0, h0_init, zero_w, wts, rows=L, row_off=0,
                                                   row_w=GRID_W, tl=tl, name="mix_bwd0")
    dp_lat, dwd1, dba1, dbx1, dlam1, dwca, dwcb, dbcb, ch1 = _mix_bwd1(
        p_lat, dcat, lat0, h1s, gates1, dxb0, h1_init, zero_w, wts, rows=L, row_off=0, row_w=GRID_W, tl=tl,
        name="mix_bwd1", dp_rows=L + T)
    zero_cat = jnp.zeros((T, 2 * W), BF16)
    cxb0, cwd0, cba0, cbx0, clam0, _ = _mix_bwd0(p_ctx, zero_cat, ctx0, cgates0, zero_w, ch0, wts, rows=T, row_off=0,
                                                 row_w=T, tl=tl, name="ctx_bwd0")
    dp, cwd1, cba1, cbx1, clam1, cwca, cwcb, cbcb, _ = _mix_bwd1(
        p_ctx, zero_cat, ctx0, c1s, cgates1, cxb0, zero_w, ch1, wts, rows=T, row_off=0, row_w=T, tl=tl, name="ctx_bwd1",
        dp_rows=L + T, dp_off=L, dp_into=dp_lat)

    gw_in = _matmul(hl, dp, ta=True, tm=1024, tn=1536, tk=2816, out_dtype=BF16, name="w_in_grad")
    rs_axes = [1, 0]
    pair_sums = _rs_pair_sums([gw_in, gw_out], rs_axes)
    dhl, rs_slots = _matmul(dp, win_full, tb=True, a_rows=L, tm=512, tn=D, tk=3072, out_dtype=BF16, name="in_proj_bwd",
                            side=_rs_chips_side(pair_sums))
    dhc = _matmul(dp, win_full, tb=True, a_rows=T, a_off=L, tm=T, tn=D, tk=3072, name="in_proj_bwd_ctx")
    gx, dsh_l, dsc_l, dng_l = _norm_bwd(dhl, x2, dn, norm_g, sc_l, tr, "norm_bwd")
    _, dsh_c, dsc_c, dng_c = _norm_bwd(dhc, ctx2, None, norm_g, sc_c, tr, "norm_bwd_ctx")

    g_in_shard, g_out_shard = _rs_finish(rs_slots, rs_axes)

    dwa0, dwx0 = _unpack_blockdiag(dwd0 + cwd0, H, hd, gs)
    dwa1, dwx1 = _unpack_blockdiag(dwd1 + cwd1, H, hd, gs)
    zeros_d = jnp.zeros((1, D), F32)
    dmod_l = jnp.concatenate([dsh_l, dsc_l, dgt], axis=1)
    dmod_c = jnp.concatenate([dsh_c, dsc_c, zeros_d], axis=1)
    small_g = [
        lax.dynamic_update_slice(jnp.zeros((8, 3 * D), F32), dmod_l, (me, 0)), dmod_c,
        dfg, dng_l + dng_c, (dwca + cwca)[:3], (dwcb + cwcb)[:4], dbcb + cbcb,
        jnp.stack([dwa0, dwa1]), jnp.stack([dwx0, dwx1]),
        jnp.concatenate([dba0 + cba0, dba1 + cba1], axis=0), jnp.concatenate([dbx0 + cbx0, dbx1 + cbx1], axis=0),
        jnp.concatenate([dlam0 + clam0, dlam1 + clam1], axis=0), loss_blk[0:1, 0:1],
    ]
    g_shapes = [a.shape for a in small_g]
    (g_rows, g_modc, g_fg, g_ng, g_wca, g_wcb, g_bcb, g_wa, g_wx, g_ba, g_bx, g_lam, loss_sum) = _unpack(
        _allreduce8(_pack(small_g, 8 * SUBLANES), "reduce_small"), g_shapes)

    g_mod = jnp.concatenate([g_rows, g_modc, jnp.zeros((7, 3 * D), F32)], axis=0)
    g_mod_q = lax.dynamic_slice(g_mod, (0, q * nq), (16, nq))
    g_w_ada = _matmul(s_rows, g_mod_q, ta=True, a_act="silu", tm=1024, tn=nq, tk=16, name="w_ada_grad")
    g_b_ada = jnp.sum(g_mod[:9], axis=0, keepdims=True)
    gc_part = _matmul(jnp.pad(lax.dynamic_slice(g_modc, (0, q * nq), (1, nq)), ((0, 7), (0, 0))), w_ada[0], tb=True,
                      dsilu_mul=c_ctx[None, :], tm=8, tn=D, tk=512, name="c_ctx_grad")
    g_c_ctx = _unpack(_allreduce8(_pack([gc_part[0:1] * first_core], 8 * SUBLANES), "reduce_c_ctx"), [(D,)])[0]

    def shard_cols(a, width):
        return lax.dynamic_slice(a, (0, q * width), (a.shape[0], width))

    grads = {
        "c_ctx": g_c_ctx, "norm_g": g_ng, "b_ada": g_b_ada,
        "w_conv_a": shard_cols(g_wca, Wq)[None], "w_conv_b": shard_cols(g_wcb, Wq)[None], "b_conv_b": g_bcb,
        "lru_wa": g_wa[None], "lru_ba": shard_cols(g_ba, Wq)[None], "lru_wx": g_wx[None],
        "lru_bx": shard_cols(g_bx, Wq)[None], "lru_lambda": shard_cols(g_lam, Wq)[None], "final_g": g_fg[0],
    }
    small_names = list(grads)
    given = dict(c_ctx=(c_ctx, m_c_ctx, v_c_ctx), norm_g=(norm_g, m_norm_g, v_norm_g), b_ada=(b_ada, m_b_ada, v_b_ada),
                 w_conv_a=(w_conv_a, m_w_conv_a, v_w_conv_a), w_conv_b=(w_conv_b, m_w_conv_b, v_w_conv_b),
                 b_conv_b=(b_conv_b, m_b_conv_b, v_b_conv_b), lru_wa=(lru_wa, m_lru_wa, v_lru_wa),
                 lru_ba=(lru_ba, m_lru_ba, v_lru_ba), lru_wx=(lru_wx, m_lru_wx, v_lru_wx),
                 lru_bx=(lru_bx, m_lru_bx, v_lru_bx), lru_lambda=(lru_lambda, m_lru_lambda, v_lru_lambda),
                 final_g=(final_g, m_final_g, v_final_g))
    shapes = [given[n][0].shape for n in small_names]
    packed = [_pack([given[n][j] for n in small_names], 2 * SUBLANES) for j in range(3)]
    packed_g = _pack([grads[n] for n in small_names], 2 * SUBLANES)
    sd, sm, sv = _adam(packed[0], packed_g, packed[1], packed[2], "adam_small")
    delta_s = dict(zip(small_names, _unpack(sd, shapes)))
    newm_s = dict(zip(small_names, _unpack(sm, shapes)))
    newv_s = dict(zip(small_names, _unpack(sv, shapes)))
    grads = {n: grads[n].reshape(given[n][0].shape) for n in small_names}

    big = {"w_ada": (w_ada, g_w_ada, m_w_ada, v_w_ada), "w_in": (w_in, g_in_shard, m_w_in, v_w_in),
           "w_out": (w_out, g_out_shard, m_w_out, v_w_out)}
    delta_b, newm_b, newv_b = {}, {}, {}
    for n, (w, g, m, v) in big.items():
        grads[n] = g[None]
        d_, m_, v_ = _adam(w[0], g, m[0], v[0], "adam_" + n)
        delta_b[n], newm_b[n], newv_b[n] = d_[None], m_[None], v_[None]

    loss = loss_sum[0, 0]
    order = ["c_ctx", "norm_g", "w_ada", "b_ada", "w_in", "w_conv_a", "w_conv_b", "b_conv_b", "lru_wa", "lru_ba",
             "lru_wx", "lru_bx", "lru_lambda", "w_out", "final_g"]
    delta = {**delta_s, **delta_b}
    newm = {**newm_s, **newm_b}
    newv = {**newv_s, **newv_b}
    return (loss, gx[None], *[grads[n] for n in order], *[delta[n] for n in order], *[newm[n] for n in order],
            *[newv[n] for n in order])
```

```python
import functools

import jax
import jax.numpy as jnp
from jax import lax
from jax.experimental import pallas as pl
from jax.experimental.pallas import tpu as pltpu

F32 = jnp.float32
BF16 = jnp.bfloat16
MESH_ID = pl.DeviceIdType.MESH

EPS = 1e-6
LRU_C = 8.0
GRID_W = 64
ADAM_LR = 0.001
ADAM_B1 = 0.9
ADAM_B2 = 0.999
ADAM_EPS = 1e-08
ADAM_WD = 0.01
ADAM_STEP = 10

LANES = 128
SUBLANES = 8
PACK_COLS = 1024
VMEM_LIMIT = 56 * 2**20
LRU_GROUP = 256


def _params(sem=None):
    return pltpu.CompilerParams(vmem_limit_bytes=VMEM_LIMIT, dimension_semantics=sem)


def _pick(dim, pref, quantum=LANES):
    if dim <= pref:
        return dim
    best = None
    for t in range(quantum, pref + 1, quantum):
        if dim % t == 0:
            best = t
    assert best is not None, (dim, pref)
    return best


def _pos():
    return lax.axis_index("x"), lax.axis_index("y"), lax.axis_index("c")


def _flip(v, bit):
    return 1 - v if bit else v


def _sigmoid(v):
    return 0.5 * jnp.tanh(0.5 * v) + 0.5


def _silu(v):
    return v * _sigmoid(v)


def _dsilu(v):
    s = _sigmoid(v)
    return s * (1.0 + v * (1.0 - s))


def _gates(pre_r, pre_i, sp):
    r = _sigmoid(pre_r)
    ig = _sigmoid(pre_i)
    e = LRU_C * r * sp
    w = jnp.tanh(e)
    return r, ig, jnp.exp(-e), (2.0 * w) * pl.reciprocal(1.0 + w, approx=True)


def _softplus(z):
    return jnp.maximum(z, 0.0) + jnp.log1p(jnp.exp(-jnp.abs(z)))


def _matmul(a, b, *, ta=False, tb=False, tm=512, tn=512, tk=512, out_dtype=F32, name,
            a_rows=None, a_off=0, a_act=None, bias=None, dsilu_mul=None, side=None):
    rows_a = a.shape[0] if a_rows is None else a_rows
    if ta:
        K, M = rows_a, a.shape[1]
    else:
        M, K = rows_a, a.shape[1]
    N = b.shape[0] if tb else b.shape[1]
    tm, tn, tk = _pick(M, tm, SUBLANES), _pick(N, tn), _pick(K, tk)
    t_rows = tk if ta else tm
    assert a_off % t_rows == 0
    nk = K // tk
    gi, gj = M // tm, N // tn
    off_blocks = a_off // t_rows
    dims = (((0 if ta else 1,), (1 if tb else 0,)), ((), ()))
    extras = [e for e in (bias, dsilu_mul) if e is not None]
    n_sin = len(side["ins"]) if side else 0
    n_sout = len(side["outs"]) if side else 0

    def body(a_ref, b_ref, *rest):
        rest = list(rest)
        bias_ref = rest.pop(0) if bias is not None else None
        dsm_ref = rest.pop(0) if dsilu_mul is not None else None
        side_in = [rest.pop(0) for _ in range(n_sin)]
        o_ref = rest.pop(0)
        side_out = [rest.pop(0) for _ in range(n_sout)]
        acc_ref = rest.pop(0) if nk > 1 else None
        side_scr = rest
        i, j, k = pl.program_id(0), pl.program_id(1), pl.program_id(2)

        if side:
            @pl.when((i == 0) & (j == 0) & (k == 0))
            def _():
                side["start"](side_in, side_out, side_scr)

        av = a_ref[...]
        if a_act == "silu":
            av = _silu(av)
        prod = lax.dot_general(av, b_ref[...], dims, preferred_element_type=F32)

        def finish(r):
            if bias_ref is not None:
                r = r + bias_ref[...]
            if dsm_ref is not None:
                r = r * _dsilu(dsm_ref[...])
            o_ref[...] = r.astype(o_ref.dtype)

        if nk == 1:
            finish(prod)
        else:
            @pl.when(k == 0)
            def _():
                acc_ref[...] = prod

            @pl.when(k > 0)
            def _():
                acc_ref[...] += prod

            @pl.when(k == nk - 1)
            def _():
                finish(acc_ref[...])

        if side:
            @pl.when((i == gi - 1) & (j == gj - 1) & (k == nk - 1))
            def _():
                side["finish"](side_in, side_out, side_scr)

    if ta:
        a_spec = pl.BlockSpec((tk, tm), lambda i, j, k: (k + off_blocks, i))
    else:
        a_spec = pl.BlockSpec((tm, tk), lambda i, j, k: (i + off_blocks, k))
    if tb:
        b_spec = pl.BlockSpec((tn, tk), lambda i, j, k: (j, k))
    else:
        b_spec = pl.BlockSpec((tk, tn), lambda i, j, k: (k, j))
    in_specs = [a_spec, b_spec]
    if bias is not None:
        in_specs.append(pl.BlockSpec((1, tn), lambda i, j, k: (0, j)))
    if dsilu_mul is not None:
        in_specs.append(pl.BlockSpec((1, tn), lambda i, j, k: (0, j)))
    hbm = pl.BlockSpec(memory_space=pl.ANY)
    res = pl.pallas_call(
        body, name=name, grid=(gi, gj, nk),
        in_specs=in_specs + [hbm] * n_sin,
        out_specs=[pl.BlockSpec((tm, tn), lambda i, j, k: (i, j))] + [hbm] * n_sout,
        out_shape=[jax.ShapeDtypeStruct((M, N), out_dtype)] + (list(side["outs"]) if side else []),
        scratch_shapes=([pltpu.VMEM((tm, tn), F32)] if nk > 1 else []) + (list(side["scratch"]) if side else []),
        compiler_params=_params(("arbitrary",) * 3 if side else ("parallel", "parallel", "arbitrary")),
    )(a, b, *extras, *(side["ins"] if side else []))
    return (res[0], res[1:]) if side else res[0]


def _elementwise(fn, ins, outs, *, rows, cols, name, tr=256):
    tr = _pick(rows, tr, 2 * SUBLANES)
    n_in = len(ins)

    def body(*refs):
        vals = fn(*[r[...] for r in refs[:n_in]])
        if not isinstance(vals, (tuple, list)):
            vals = (vals,)
        for r, v in zip(refs[n_in:], vals, strict=True):
            r[...] = v.astype(r.dtype)

    def spec(off):
        assert off % tr == 0
        ob = off // tr
        return pl.BlockSpec((tr, cols), lambda i: (i + ob, 0))

    res = pl.pallas_call(
        body, name=name, grid=(rows // tr,),
        in_specs=[spec(off) for _, off in ins],
        out_specs=[spec(0) for _ in outs],
        out_shape=[jax.ShapeDtypeStruct((rows, cols), dt) for dt in outs],
        compiler_params=_params(("parallel",)),
    )(*[a for a, _ in ins])
    return res


def _adam_math(w, g, m, v):
    m = ADAM_B1 * m + (1.0 - ADAM_B1) * g
    v = ADAM_B2 * v + (1.0 - ADAM_B2) * (g * g)
    m_hat = m / (1.0 - ADAM_B1 ** ADAM_STEP)
    v_hat = v / (1.0 - ADAM_B2 ** ADAM_STEP)
    delta = -ADAM_LR * (m_hat / (jnp.sqrt(v_hat) + ADAM_EPS) + ADAM_WD * w)
    return delta, m, v


def _adam(w, g, m, v, name, echo_g=False):
    rows, cols = w.shape
    fn = (lambda w_, g_, m_, v_: _adam_math(w_, g_, m_, v_) + (g_,)) if echo_g else _adam_math
    return _elementwise(fn, [(w, 0), (g, 0), (m, 0), (v, 0)], [F32] * (4 if echo_g else 3),
                        rows=rows, cols=cols, name=name)


def _adam_many(quads, name):
    n = len(quads)

    def body(*refs):
        ins, outs = refs[:4 * n], refs[4 * n:]
        for t in range(n):
            w, g, m, v = (r[...] for r in ins[4 * t:4 * t + 4])
            for o_ref, val in zip(outs[3 * t:3 * t + 3], _adam_math(w, g, m, v), strict=True):
                o_ref[...] = val

    res = pl.pallas_call(
        body, name=name,
        out_shape=[jax.ShapeDtypeStruct(q[0].shape, F32) for q in quads for _ in range(3)],
        compiler_params=_params(),
    )(*[a for q in quads for a in q])
    return [res[3 * t:3 * t + 3] for t in range(n)]


def _pack(arrs, row_quantum):
    flat = jnp.concatenate([a.reshape(-1).astype(F32) for a in arrs])
    n = flat.shape[0]
    q = row_quantum * PACK_COLS
    total = -(-n // q) * q
    flat = jnp.pad(flat, (0, total - n))
    return flat.reshape(total // PACK_COLS, PACK_COLS)


def _unpack(buf, shapes):
    flat = buf.reshape(-1)
    out, off = [], 0
    for s in shapes:
        n = 1
        for d in s:
            n *= d
        out.append(flat[off:off + n].reshape(s))
        off += n
    return out


def _allreduce8(buf, name):
    R, C = buf.shape
    assert R % (8 * SUBLANES) == 0
    m = R // 8

    def body(x_ref, o_ref, recv, red, s1, r1, s2, r2):
        x, y, c = _pos()
        me = 4 * x + 2 * y + c

        def peer(k):
            px, py, pc = _flip(x, (k >> 2) & 1), _flip(y, (k >> 1) & 1), _flip(c, k & 1)
            return (px, py, pc), 4 * px + 2 * py + pc

        def rows(ref, idx):
            return ref.at[pl.ds(pl.multiple_of(idx * m, SUBLANES), m), :]

        def scatter(k):
            dev, p = peer(k)
            return pltpu.make_async_remote_copy(src_ref=rows(x_ref, p), dst_ref=recv.at[k], send_sem=s1.at[k],
                                                recv_sem=r1.at[k], device_id=dev, device_id_type=MESH_ID)

        def share(k):
            dev, p = peer(k)
            return pltpu.make_async_remote_copy(src_ref=red, dst_ref=rows(o_ref, me), send_sem=s2.at[k],
                                                recv_sem=r2.at[k], device_id=dev, device_id_type=MESH_ID)

        def shared_from(k):
            dev, p = peer(k)
            return pltpu.make_async_remote_copy(src_ref=red, dst_ref=rows(o_ref, p), send_sem=s2.at[k],
                                                recv_sem=r2.at[k], device_id=dev, device_id_type=MESH_ID)

        for k in range(1, 8):
            scatter(k).start()
        acc = rows(x_ref, me)[...]
        for k in range(1, 8):
            scatter(k).wait_recv()
            acc = acc + recv[k]
        red[...] = acc
        rows(o_ref, me)[...] = acc
        for k in range(1, 8):
            share(k).start()
        for k in range(1, 8):
            shared_from(k).wait_recv()
        for k in range(1, 8):
            scatter(k).wait_send()
            share(k).wait_send()

    return pl.pallas_call(
        body, name=name,
        in_specs=[pl.BlockSpec(memory_space=pltpu.VMEM)],
        out_specs=pl.BlockSpec(memory_space=pltpu.VMEM),
        out_shape=jax.ShapeDtypeStruct((R, C), F32),
        scratch_shapes=[pltpu.VMEM((8, m, C), F32), pltpu.VMEM((m, C), F32),
                        pltpu.SemaphoreType.DMA((8,)), pltpu.SemaphoreType.DMA((8,)),
                        pltpu.SemaphoreType.DMA((8,)), pltpu.SemaphoreType.DMA((8,))],
        compiler_params=_params(),
    )(buf)


def _bounce(src, dst, buf, sem):
    cin = pltpu.make_async_copy(src, buf, sem)
    cin.start()
    cin.wait()
    cout = pltpu.make_async_copy(buf, dst, sem)
    cout.start()
    cout.wait()


def _chunk(ref, axis, idx, size):
    start = idx * size
    if axis == 0:
        return ref.at[pl.ds(start, size), :]
    return ref.at[:, pl.ds(start, size)]


def _in_proj_gather(hl, win, wout, q_arr, *, rows, tm):
    D, nq = win.shape
    dq, D2 = wout.shape
    ni = rows // tm
    ops = ((0, 1, nq, D // 2), (1, 0, dq, dq // 2))

    def body(q_ref, a_ref, win_ref, wout_ref, p_ref, gin_ref, gout_ref, b_scr, buf_out, lsem, ssem, rsem, fsem, gsem):
        j, i = pl.program_id(0), pl.program_id(1)
        x, y, c = _pos()
        q = 2 * x + y
        srcs = (win_ref, wout_ref)
        dsts = (gin_ref, gout_ref)

        def shard_window(o, chip):
            _, axis, size, _ = ops[o]
            return _chunk(dsts[o], axis, chip, size)

        def half(ref, o, core):
            return ref.at[pl.ds(core * ops[o][3], ops[o][3]), :]

        def half_window(o, chip, core):
            _, axis, size, hs = ops[o]
            if axis == 1:
                return dsts[o].at[pl.ds(core * hs, hs), pl.ds(chip * size, size)]
            return dsts[o].at[pl.ds(chip * size + core * hs, hs), :]

        def chip_of(k):
            px, py = _flip(x, (k >> 1) & 1), _flip(y, k & 1)
            return px, py, 2 * px + py

        def send(o, k):
            px, py, _ = chip_of(k)
            return pltpu.make_async_remote_copy(
                src_ref=half(srcs[o], o, c), dst_ref=half_window(o, q, c), send_sem=ssem.at[o, k],
                recv_sem=rsem.at[o, k], device_id=(px, py, c), device_id_type=MESH_ID)

        def chip_recv(o, k):
            px, py, pq = chip_of(k)
            landed = half_window(o, pq, c)
            pltpu.make_async_remote_copy(src_ref=landed, dst_ref=landed, send_sem=ssem.at[o, k], recv_sem=rsem.at[o, k],
                                         device_id=(px, py, c), device_id_type=MESH_ID).wait_recv()

        def to_sibling(o, k):
            landed = half_window(o, chip_of(k)[2], c)
            return pltpu.make_async_remote_copy(src_ref=landed, dst_ref=landed, send_sem=fsem.at[o, k],
                                                recv_sem=gsem.at[o, k], device_id=(x, y, 1 - c), device_id_type=MESH_ID)

        def from_sibling(o, k):
            theirs = half_window(o, chip_of(k)[2], 1 - c)
            pltpu.make_async_remote_copy(src_ref=theirs, dst_ref=theirs, send_sem=fsem.at[o, k], recv_sem=gsem.at[o, k],
                                         device_id=(x, y, 1 - c), device_id_type=MESH_ID).wait_recv()

        def relay(o, core):
            if core == 0:
                landed, target = half_window(o, chip_of(2)[2], 0), (x, 1 - y, 0)
            else:
                landed, target = half_window(o, chip_of(1)[2], 1), (1 - x, y, 1)
            return pltpu.make_async_remote_copy(src_ref=landed, dst_ref=landed, send_sem=ssem.at[o, 3],
                                                recv_sem=rsem.at[o, 3], device_id=target, device_id_type=MESH_ID)

        def on_core(core, fn):
            @pl.when(c == core)
            def _():
                fn()

        def land(o, k):
            chip_recv(o, k)
            if k == 2:
                on_core(0, lambda: relay(o, 0).start())
            if k == 1:
                on_core(1, lambda: relay(o, 1).start())
            to_sibling(o, k).start()

        def settle(o, k):
            from_sibling(o, k)
            to_sibling(o, k).wait_send()

        def b_load(k, slot):
            src = win_ref if k == 0 else shard_window(0, chip_of(k)[2])
            return pltpu.make_async_copy(src, b_scr.at[slot], lsem.at[0])

        def own_store():
            return pltpu.make_async_copy(b_scr.at[0], shard_window(0, q), lsem.at[2])

        order = (0, 2, 1, 3)
        early = max(ni - 2, 0)

        @pl.when((j == 0) & (i == 0))
        def _():
            for o in range(2):
                for k in (2, 1):
                    send(o, k).start()
            first = b_load(0, 0)
            first.start()
            first.wait()
            own_store().start()
            _bounce(wout_ref, shard_window(1, q), buf_out, lsem.at[1])

        for jj in range(3):
            nxt = order[jj + 1]

            @pl.when((j == jj) & (i == early))
            def _(nxt=nxt):
                land(0, nxt)

            @pl.when((j == jj) & (i == ni - 1))
            def _(jj=jj, nxt=nxt):
                settle(0, nxt)
                if jj == 1:
                    own_store().wait()
                b_load(nxt, (jj + 1) % 2).start()

            @pl.when((j == jj + 1) & (i == 0))
            def _(jj=jj, nxt=nxt):
                b_load(nxt, (jj + 1) % 2).wait()

        @pl.when((j == 3) & (i == 0))
        def _():
            land(1, 2)
            land(1, 1)

        p_ref[...] = jnp.dot(a_ref[...], b_scr[j % 2], preferred_element_type=F32).astype(p_ref.dtype)

        @pl.when((j == 3) & (i == ni - 1))
        def _():
            settle(1, 2)
            settle(1, 1)
            land(1, 3)
            settle(1, 3)
            for o in range(2):
                for k in (2, 1):
                    send(o, k).wait_send()
                for core in range(2):
                    on_core(core, lambda o=o, core=core: relay(o, core).wait_send())

    hbm = pl.BlockSpec(memory_space=pl.ANY)
    grid_spec = pltpu.PrefetchScalarGridSpec(
        num_scalar_prefetch=1, grid=(4, ni),
        in_specs=[pl.BlockSpec((tm, D), lambda j, i, qr: (i, 0)), hbm, hbm],
        out_specs=[pl.BlockSpec((tm, nq), lambda j, i, qr: (i, jnp.bitwise_xor(qr[0], ((j & 1) << 1) | (j >> 1)))),
                   hbm, hbm],
        scratch_shapes=[pltpu.VMEM((2,) + win.shape, win.dtype), pltpu.VMEM(wout.shape, wout.dtype), pltpu.SemaphoreType.DMA((3,))]
        + [pltpu.SemaphoreType.DMA((2, 4)) for _ in range(4)])
    return pl.pallas_call(
        body, name="in_proj_gather", grid_spec=grid_spec,
        out_shape=[jax.ShapeDtypeStruct((rows, 4 * nq), BF16), jax.ShapeDtypeStruct((D, 4 * nq), win.dtype),
                   jax.ShapeDtypeStruct((4 * dq, D2), wout.dtype)],
        compiler_params=_params(("arbitrary", "arbitrary")),
    )(q_arr, hl, win, wout)


def _rs_to_sibling(gs, axes):
    n = len(gs)
    shapes = []
    for g, ax in zip(gs, axes):
        s = list(g.shape)
        s[ax] //= 8
        shapes.append(tuple(s))

    def body(*refs):
        g_refs, mine, landed = refs[:n], refs[n:2 * n], refs[2 * n:3 * n]
        bufs = refs[3 * n:4 * n]
        lsem, ssem, rsem = refs[4 * n:]
        x, y, c = _pos()
        cps = []
        for o in range(n):
            size = shapes[o][axes[o]]
            for j in range(4):
                rc = pltpu.make_async_remote_copy(
                    src_ref=_chunk(g_refs[o], axes[o], 2 * j + 1 - c, size), dst_ref=landed[o].at[j],
                    send_sem=ssem.at[o, j], recv_sem=rsem.at[o, j], device_id=(x, y, 1 - c), device_id_type=MESH_ID)
                rc.start()
                cps.append(rc)
        for o in range(n):
            size = shapes[o][axes[o]]
            for j in range(4):
                _bounce(_chunk(g_refs[o], axes[o], 2 * j + c, size), mine[o].at[j], bufs[o], lsem.at[o])
        for rc in cps:
            rc.wait()

    hbm = pl.BlockSpec(memory_space=pl.ANY)
    outs = [jax.ShapeDtypeStruct((4,) + s, g.dtype) for s, g in zip(shapes, gs)]
    res = pl.pallas_call(
        body, name="rs_to_sibling", in_specs=[hbm] * n, out_specs=[hbm] * (2 * n), out_shape=outs + outs,
        scratch_shapes=[pltpu.VMEM(s, g.dtype) for s, g in zip(shapes, gs)]
        + [pltpu.SemaphoreType.DMA((n,)), pltpu.SemaphoreType.DMA((n, 4)), pltpu.SemaphoreType.DMA((n, 4))],
        compiler_params=_params(),
    )(*gs)
    return res[:n], res[n:]


def _rs_chips_side(parts):
    n = len(parts)

    def copies(p_refs, slots, scr):
        ssem, rsem = scr[n + 1], scr[n + 2]
        x, y, c = _pos()
        cps = []
        for o in range(n):
            for k in range(1, 4):
                px, py = _flip(x, (k >> 1) & 1), _flip(y, k & 1)
                cps.append(pltpu.make_async_remote_copy(
                    src_ref=p_refs[o].at[2 * px + py], dst_ref=slots[o].at[k], send_sem=ssem.at[o, k],
                    recv_sem=rsem.at[o, k], device_id=(px, py, c), device_id_type=MESH_ID))
        return cps

    def start(p_refs, slots, scr):
        for cp in copies(p_refs, slots, scr):
            cp.start()

    def finish(p_refs, slots, scr):
        x, y, _ = _pos()
        q = 2 * x + y
        for o in range(n):
            _bounce(p_refs[o].at[q], slots[o].at[0], scr[o], scr[n].at[o])
        for cp in copies(p_refs, slots, scr):
            cp.wait()

    return dict(
        ins=list(parts), outs=[jax.ShapeDtypeStruct(p.shape, p.dtype) for p in parts],
        scratch=[pltpu.VMEM(p.shape[1:], p.dtype) for p in parts]
        + [pltpu.SemaphoreType.DMA((n,)), pltpu.SemaphoreType.DMA((n, 4)), pltpu.SemaphoreType.DMA((n, 4))],
        start=start, finish=finish)


def _rs_share(rs, axes):
    n = len(rs)
    shapes = []
    for r, ax in zip(rs, axes):
        s = list(r.shape)
        s[ax] *= 2
        shapes.append(tuple(s))

    def body(*refs):
        r_refs, outs = refs[:n], refs[n:2 * n]
        bufs = refs[2 * n:3 * n]
        lsem, ssem, rsem = refs[3 * n:]
        x, y, c = _pos()
        cps = []
        for o in range(n):
            size = r_refs[o].shape[axes[o]]
            window = _chunk(outs[o], axes[o], c, size)
            rc = pltpu.make_async_remote_copy(src_ref=r_refs[o], dst_ref=window, send_sem=ssem.at[o], recv_sem=rsem.at[o],
                                              device_id=(x, y, 1 - c), device_id_type=MESH_ID)
            rc.start()
            cps.append(rc)
        for o in range(n):
            size = r_refs[o].shape[axes[o]]
            _bounce(r_refs[o], _chunk(outs[o], axes[o], c, size), bufs[o], lsem.at[o])
        for cp in cps:
            cp.wait()

    hbm = pl.BlockSpec(memory_space=pl.ANY)
    return pl.pallas_call(
        body, name="rs_share", in_specs=[hbm] * n, out_specs=[hbm] * n,
        out_shape=[jax.ShapeDtypeStruct(s, r.dtype) for s, r in zip(shapes, rs)],
        scratch_shapes=[pltpu.VMEM(r.shape, r.dtype) for r in rs] + [pltpu.SemaphoreType.DMA((n,)) for _ in range(3)],
        compiler_params=_params(),
    )(*rs)


def _rs_pair_sums(gs, axes):
    mine, landed = _rs_to_sibling(gs, axes)
    pair_sums = []
    for o, (mi, la) in enumerate(zip(mine, landed)):
        rows, cols = mi.shape[0] * mi.shape[1], mi.shape[2]
        s = _elementwise(lambda a, b: a.astype(F32) + b.astype(F32), [(mi.reshape(rows, cols), 0), (la.reshape(rows, cols), 0)],
                         [BF16], rows=rows, cols=cols, name=f"rs_pair_sum{o}")[0]
        pair_sums.append(s.reshape(mi.shape))
    return pair_sums


def _rs_finish(slots, axes):
    reduced = []
    for o, sl in enumerate(slots):
        rows, cols = sl.shape[1], sl.shape[2]
        flat = sl.reshape(4 * rows, cols)
        r = _elementwise(lambda a, b, c, d: (a.astype(F32) + b.astype(F32)) + (c.astype(F32) + d.astype(F32)),
                         [(flat, k * rows) for k in range(4)], [F32], rows=rows, cols=cols, name=f"rs_chip_sum{o}")[0]
        reduced.append(r)
    return _rs_share(reduced, axes)


def _norm_in(x, ctx, g, sc_l, sh_l, sc_c, sh_c, tr):
    L, D = x.shape
    T = ctx.shape[0]
    nx, nc = L // tr, T // tr

    def body(x_ref, c_ref, g_ref, scl, shl, scc, shc, o_ref):
        i = pl.program_id(0)

        def run(src, sc, sh):
            v = src[...]
            r = lax.rsqrt(jnp.mean(v * v, axis=-1, keepdims=True) + EPS)
            o_ref[...] = ((v * r * g_ref[...]) * (1.0 + sc[...]) + sh[...]).astype(o_ref.dtype)

        @pl.when(i < nx)
        def _():
            run(x_ref, scl, shl)

        @pl.when(i >= nx)
        def _():
            run(c_ref, scc, shc)

    vec = pl.BlockSpec((1, D), lambda i: (0, 0))
    return pl.pallas_call(
        body, name="norm_in", grid=(nx + nc,),
        in_specs=[pl.BlockSpec((tr, D), lambda i: (jnp.minimum(i, nx - 1), 0)),
                  pl.BlockSpec((tr, D), lambda i: (jnp.maximum(i - nx, 0), 0)), vec, vec, vec, vec, vec],
        out_specs=pl.BlockSpec((tr, D), lambda i: (i, 0)),
        out_shape=jax.ShapeDtypeStruct((L + T, D), BF16),
        compiler_params=_params(("arbitrary",)),
    )(x, ctx, g, sc_l, sh_l, sc_c, sh_c)


def _tmod(tl, row_w):
    assert row_w & (row_w - 1) == 0
    return lax.broadcasted_iota(jnp.int32, (tl, 1), 0) & (row_w - 1)


def _shift(z, k, tmod, row_w):
    tl = z.shape[0]
    rolled = pltpu.roll(z, k % tl, 0)
    mask = (tmod >= k) if k > 0 else (tmod < row_w + k)
    return jnp.where(mask, rolled, 0.0)


def _conv(z, w_ref, taps, left, tmod, row_w, lanes=slice(None)):
    out = None
    for j in range(taps):
        k = left - j
        term = (z if k == 0 else _shift(z, k, tmod, row_w)) * w_ref[j:j + 1, lanes]
        out = term if out is None else out + term
    return out


def _conv_bwd(dz, z, w_ref, taps, left, tmod, row_w, lanes=slice(None)):
    din = None
    dws = []
    for j in range(taps):
        k = left - j
        shifted = dz if k == 0 else _shift(dz, -k, tmod, row_w)
        term = shifted * w_ref[j:j + 1, lanes]
        din = term if din is None else din + term
        dws.append(jnp.sum(shifted * z, axis=0, keepdims=True))
    return din, dws


def _gate_matmul(xb16_ref, wd_ref, pre_scr, W, ng, gs):
    for g in range(ng):
        pg = jnp.dot(xb16_ref[:, g * gs:(g + 1) * gs], wd_ref[g], preferred_element_type=F32)
        pre_scr[:, g * gs:(g + 1) * gs] = pg[:, :gs]
        pre_scr[:, W + g * gs:W + (g + 1) * gs] = pg[:, gs:]


def _f32(ref, rows, lanes):
    return ref[rows, lanes].astype(F32)


def _sub_loop(tl, sub_r, W, fn):
    def chunk(ci, carry):
        r0 = pl.multiple_of(ci * sub_r, sub_r)
        for lb in range(W // LANES):
            fn(r0, lb * LANES)
        return carry

    lax.fori_loop(0, tl // sub_r, chunk, 0)


def _row_loop(tl, rev, step, init):
    nchunk = tl // SUBLANES

    def chunk(j, carry):
        jj = (nchunk - 1 - j) if rev else j
        c0 = pl.multiple_of(jj * SUBLANES, SUBLANES)
        for r in (range(SUBLANES - 1, -1, -1) if rev else range(SUBLANES)):
            carry = step(c0 + r, carry)
        return carry

    return lax.fori_loop(0, nchunk, chunk, init)


def _mix_fwd(P, d, h_init, wts, *, rows, row_off, row_w, tl, saved0=None, name):
    W = P.shape[1] // 6
    nt = rows // tl
    ob = row_off // tl
    rev = d == 1
    gs = min(LRU_GROUP, W)
    ng = W // gs
    wca, wcb, bcb = wts["wca"], wts["wcb"], wts["bcb"]
    wd, ba, bx, lam = wts["wd"][d], wts["ba"][d], wts["bx"][d], wts["lam"][d]

    def tile(i):
        return (nt - 1 - i) if rev else i

    def pcol(j):
        return pl.BlockSpec((tl, W), lambda i: (tile(i) + ob, j))

    vec = pl.BlockSpec((1, W), lambda i: (0, 0))
    taps = pl.BlockSpec((SUBLANES, W), lambda i: (0, 0))
    wd_spec = pl.BlockSpec(wd.shape, lambda i: (0, 0, 0))
    seq = pl.BlockSpec((tl, W), lambda i: (tile(i), 0))

    sub_r = min(row_w, tl)
    assert tl % sub_r == 0

    def body(*refs):
        if rev:
            (bl, cl, ul, gl, ql, ho, xb_r, xb16_r, wca_r, wd_r, ba_r, bx_r, lam_r, hin, hseq, cat, a_o, r_o, ig_o, m2_o,
             b_scr, pre_scr, carry, sp_scr) = refs
        else:
            (vl, wcb_r, bcb_r, wd_r, ba_r, bx_r, lam_r, hin, hseq, xb_r, xb16_r, a_o, r_o, ig_o, m2_o,
             b_scr, pre_scr, carry, sp_scr) = refs
        i = pl.program_id(0)

        @pl.when(i == 0)
        def _():
            carry[...] = hin[...]

        sp_scr[...] = _softplus(-lam_r[...])
        tmod = _tmod(sub_r, sub_r)

        def conv_in(r0, l0):
            rs, ls = pl.ds(r0, sub_r), pl.ds(l0, LANES)
            xb = _conv(_f32(vl, rs, ls), wcb_r, 4, 2, tmod, sub_r, ls) + bcb_r[:, ls]
            xb_r[rs, ls] = xb
            xb16_r[rs, ls] = xb.astype(BF16)

        def gates(r0, l0):
            rs, ls = pl.ds(r0, sub_r), pl.ds(l0, LANES)
            r, ig, a, m2 = _gates(pre_scr[rs, ls] + ba_r[:, ls], pre_scr[rs, pl.ds(W + l0, LANES)] + bx_r[:, ls],
                                  sp_scr[:, ls])
            a_o[rs, ls] = a
            r_o[rs, ls] = r.astype(r_o.dtype)
            ig_o[rs, ls] = ig.astype(ig_o.dtype)
            m2_o[rs, ls] = m2.astype(m2_o.dtype)
            m = jnp.where(m2 > 0.0, m2 * lax.rsqrt(m2), 0.0)
            b_scr[rs, ls] = m * (ig * xb_r[rs, ls])

        if not rev:
            _sub_loop(tl, sub_r, W, conv_in)
        _gate_matmul(xb16_r, wd_r, pre_scr, W, ng, gs)
        _sub_loop(tl, sub_r, W, gates)

        def step(t, h):
            h = a_o[pl.ds(t, 1), :] * h + b_scr[pl.ds(t, 1), :]
            hseq[pl.ds(t, 1), :] = h
            return h

        carry[...] = _row_loop(tl, rev, step, carry[...])

        if rev:
            def mix_out(r0, l0):
                rs, ls = pl.ds(r0, sub_r), pl.ds(l0, LANES)
                yb = (ho[rs, ls] + hseq[rs, ls]) * _silu(_f32(ql, rs, ls))
                ya = (_f32(bl, rs, ls) * _conv(_f32(cl, rs, ls) * _f32(ul, rs, ls), wca_r, 3, 1, tmod, sub_r, ls)
                      * _silu(_f32(gl, rs, ls)))
                cat[rs, ls] = ya.astype(cat.dtype)
                cat[rs, pl.ds(W + l0, LANES)] = yb.astype(cat.dtype)

            _sub_loop(tl, sub_r, W, mix_out)

    scratch = [pltpu.VMEM((tl, W), F32), pltpu.VMEM((tl, 2 * W), F32), pltpu.VMEM((1, W), F32), pltpu.VMEM((1, W), F32)]
    f32_seq = jax.ShapeDtypeStruct((rows, W), F32)
    kept_gates = [f32_seq] + [jax.ShapeDtypeStruct((rows, W), BF16)] * 3
    if rev:
        in_specs = [pcol(j) for j in (0, 1, 2, 3, 5)] + [seq, seq, seq, taps, wd_spec, vec, vec, vec, vec]
        args = [P] * 5 + [saved0["h"], saved0["xb"], saved0["xb16"], wca, wd, ba, bx, lam, h_init]
        out_specs = [seq, pl.BlockSpec((tl, 2 * W), lambda i: (tile(i), 0))] + [seq] * 4
        out_shape = [f32_seq, jax.ShapeDtypeStruct((rows, 2 * W), BF16)] + kept_gates
    else:
        in_specs = [pcol(4), taps, vec, wd_spec, vec, vec, vec, vec]
        args = [P, wcb, bcb, wd, ba, bx, lam, h_init]
        out_specs = [seq] * 7
        out_shape = [f32_seq, f32_seq, jax.ShapeDtypeStruct((rows, W), BF16)] + kept_gates
    res = pl.pallas_call(
        body, name=name, grid=(nt,), in_specs=in_specs, out_specs=out_specs, out_shape=out_shape,
        scratch_shapes=scratch, compiler_params=_params(("arbitrary",)),
    )(*args)
    gates = dict(zip(("a", "r", "ig", "m2"), res[-4:]))
    if rev:
        return res[0], res[1], gates
    return dict(h=res[0], xb=res[1], xb16=res[2]), gates


_BWD_SCRATCH = ("dyl", "g", "dp16", "sp", "dlf", "edge", "c")
_FWD_SAVED = ("xb", "xb16", "a", "r", "ig", "m2")


def _bwd_scratch(tl, W):
    shapes = {"dyl": pltpu.VMEM((tl, W), F32), "g": pltpu.VMEM((tl, W), F32), "dp16": pltpu.VMEM((tl, 2 * W), BF16),
              "sp": pltpu.VMEM((1, W), F32), "dlf": pltpu.VMEM((1, W), F32), "edge": pltpu.VMEM((1, W), F32),
              "c": pltpu.VMEM((1, W), F32)}
    return [shapes[n] for n in _BWD_SCRATCH]


def _lru_bwd_tile(d, dy_fn, hs_ref, wd_r, lam_r, scr, acc, first, tl, sub_r, ng, gs):
    dwd_ref, dba_ref, dbx_ref, dlam_ref = acc
    W = hs_ref.shape[1]
    assert gs % LANES == 0
    rev = d == 0
    lam = lam_r[...]
    scr["sp"][...] = _softplus(-lam)
    scr["dlf"][...] = -_sigmoid(-lam)

    @pl.when(first)
    def _():
        dwd_ref[...] = jnp.zeros_like(dwd_ref)
        dba_ref[...] = jnp.zeros_like(dba_ref)
        dbx_ref[...] = jnp.zeros_like(dbx_ref)
        dlam_ref[...] = jnp.zeros_like(dlam_ref)

    def state_grad(r0, l0):
        rs, ls = pl.ds(r0, sub_r), pl.ds(l0, LANES)
        scr["dyl"][rs, ls] = dy_fn(rs, ls)

    _sub_loop(tl, sub_r, W, state_grad)

    def step(t, c):
        g = scr["dyl"][pl.ds(t, 1), :] + c
        scr["g"][pl.ds(t, 1), :] = g
        return scr["a"][pl.ds(t, 1), :] * g

    scr["c"][...] = _row_loop(tl, rev, step, scr["c"][...])
    row = lax.broadcasted_iota(jnp.int32, (sub_r, 1), 0)

    def grads(r0, l0):
        rs, ls = pl.ds(r0, sub_r), pl.ds(l0, LANES)
        g, a, m2 = scr["g"][rs, ls], scr["a"][rs, ls], _f32(scr["m2"], rs, ls)
        r, ig, xb = _f32(scr["r"], rs, ls), _f32(scr["ig"], rs, ls), scr["xb"][rs, ls]
        h = hs_ref[rs, ls]
        if d == 0:
            e0 = pl.multiple_of(jnp.maximum(r0 - SUBLANES, 0), SUBLANES)
            edge = jnp.where(r0 == 0, scr["edge"][:, ls], hs_ref[pl.ds(e0, SUBLANES), ls][SUBLANES - 1:, :])
            hprev = jnp.where(row == 0, edge, pltpu.roll(h, 1, 0))
        else:
            e0 = pl.multiple_of(jnp.minimum(r0 + sub_r, tl - SUBLANES), SUBLANES)
            edge = jnp.where(r0 == tl - sub_r, scr["edge"][:, ls], hs_ref[pl.ds(e0, SUBLANES), ls][:1, :])
            hprev = jnp.where(row == sub_r - 1, edge, pltpu.roll(h, sub_r - 1, 0))
        rsq = lax.rsqrt(m2)
        gm = g * (m2 * rsq)
        d_la = (g * hprev) * a - (g * (ig * xb)) * ((1.0 - m2) * rsq)
        d_pr = d_la * ((-LRU_C) * scr["sp"][:, ls]) * (r * (1.0 - r))
        d_pi = (gm * xb) * (ig * (1.0 - ig))
        scr["dyl"][rs, ls] = gm * ig
        dlam_ref[:, ls] += jnp.sum(d_la * ((-LRU_C) * r), axis=0, keepdims=True) * scr["dlf"][:, ls]
        dba_ref[:, ls] += jnp.sum(d_pr, axis=0, keepdims=True)
        dbx_ref[:, ls] += jnp.sum(d_pi, axis=0, keepdims=True)
        gi, off = divmod(l0, gs)
        scr["dp16"][rs, pl.ds(gi * 2 * gs + off, LANES)] = d_pr.astype(BF16)
        scr["dp16"][rs, pl.ds(gi * 2 * gs + gs + off, LANES)] = d_pi.astype(BF16)

    _sub_loop(tl, sub_r, W, grads)
    for gi in range(ng):
        dp = scr["dp16"][:, gi * 2 * gs:(gi + 1) * 2 * gs]
        scr["g"][:, gi * gs:(gi + 1) * gs] = lax.dot_general(dp, wd_r[gi], (((1,), (1,)), ((), ())),
                                                             preferred_element_type=F32)
        dwd_ref[gi] += lax.dot_general(scr["xb16"][:, gi * gs:(gi + 1) * gs], dp, (((0,), (0,)), ((), ())),
                                       preferred_element_type=F32)


def _edge_block(h, tl, nt, d):
    W = h.shape[1]
    per = tl // SUBLANES
    if d == 0:
        return pl.BlockSpec((SUBLANES, W), lambda i: (jnp.maximum((nt - 1 - i) * per - 1, 0), 0))
    return pl.BlockSpec((SUBLANES, W), lambda i: (jnp.minimum((i + 1) * per, nt * per - 1), 0))


def _mix_bwd0(P, dcat, saved0, gates0, h_init, c_init, wts, *, rows, row_off, row_w, tl, name):
    W = P.shape[1] // 6
    nt = rows // tl
    ob = row_off // tl
    gs = min(LRU_GROUP, W)
    ng = W // gs
    wd, lam = wts["wd"][0], wts["lam"][0]
    h0s = saved0["h"]
    kept = [saved0["xb"], saved0["xb16"]] + [gates0[n] for n in ("a", "r", "ig", "m2")]

    def tile(i):
        return nt - 1 - i

    vec = pl.BlockSpec((1, W), lambda i: (0, 0))
    wd_spec = pl.BlockSpec(wd.shape, lambda i: (0, 0, 0))
    seq = pl.BlockSpec((tl, W), lambda i: (tile(i), 0))

    sub_r = min(row_w, tl)
    assert tl % sub_r == 0

    def body(ql, dyb, hs, hedge8, xb_r, xb16_r, a_r, r_r, ig_r, m2_r, wd_r, lam_r, hin, cin,
             dxb_o, dwd_o, dba_o, dbx_o, dlam_o, cfin, *scratch):
        scr = dict(zip(_BWD_SCRATCH, scratch, strict=True))
        scr.update(zip(_FWD_SAVED, (xb_r, xb16_r, a_r, r_r, ig_r, m2_r), strict=True))
        i = pl.program_id(0)

        @pl.when(i == 0)
        def _():
            scr["c"][...] = cin[...]

        scr["edge"][...] = jnp.where(i == nt - 1, hin[...], hedge8[SUBLANES - 1:SUBLANES, :])
        _lru_bwd_tile(0, lambda rs, ls: _f32(dyb, rs, ls) * _silu(_f32(ql, rs, ls)), hs, wd_r, lam_r, scr,
                      (dwd_o, dba_o, dbx_o, dlam_o), i == 0, tl, sub_r, ng, gs)
        dxb_o[...] = scr["dyl"][...] + scr["g"][...]
        cfin[...] = scr["c"][...]

    return pl.pallas_call(
        body, name=name, grid=(nt,),
        in_specs=[pl.BlockSpec((tl, W), lambda i: (tile(i) + ob, 5)), pl.BlockSpec((tl, W), lambda i: (tile(i), 1)), seq,
                  _edge_block(h0s, tl, nt, 0)] + [seq] * 6 + [wd_spec, vec, vec, vec],
        out_specs=[seq, wd_spec, vec, vec, vec, vec],
        out_shape=[jax.ShapeDtypeStruct((rows, W), F32), jax.ShapeDtypeStruct(wd.shape, F32)]
        + [jax.ShapeDtypeStruct((1, W), F32)] * 4,
        scratch_shapes=_bwd_scratch(tl, W),
        compiler_params=_params(("arbitrary",)),
    )(P, dcat, h0s, h0s, *kept, wd, lam, h_init, c_init)


def _mix_bwd1(P, dcat, saved0, h1s, gates1, dxb0, h_init, c_init, wts, *, rows, row_off, row_w, tl, name,
              dp_rows=None, dp_off=0, dp_into=None):
    dp_rows = rows if dp_rows is None else dp_rows
    dpb = dp_off // tl
    W = P.shape[1] // 6
    nt = rows // tl
    ob = row_off // tl
    gs = min(LRU_GROUP, W)
    ng = W // gs
    wca, wcb = wts["wca"], wts["wcb"]
    wd, lam = wts["wd"][1], wts["lam"][1]
    h0s = saved0["h"]
    kept = [saved0["xb"], saved0["xb16"]] + [gates1[n] for n in ("a", "r", "ig", "m2")]

    vec = pl.BlockSpec((1, W), lambda i: (0, 0))
    taps = pl.BlockSpec((SUBLANES, W), lambda i: (0, 0))
    wd_spec = pl.BlockSpec(wd.shape, lambda i: (0, 0, 0))
    seq = pl.BlockSpec((tl, W), lambda i: (i, 0))

    sub_r = min(row_w, tl)
    assert tl % sub_r == 0

    def body(*refs):
        if dp_into is not None:
            refs = refs[1:]
        (bl, cl, ul, gl, vl, ql, dya, dyb, h0, h1, hedge8, dx0, xb_r, xb16_r, a_r, r_r, ig_r, m2_r, wca_r, wcb_r,
         wd_r, lam_r, hin, cin, dp_o, dwd_o, dba_o, dbx_o, dlam_o, dwca_o, dwcb_o, dbcb_o, cfin, *scratch) = refs
        scr = dict(zip(_BWD_SCRATCH, scratch, strict=True))
        scr.update(zip(_FWD_SAVED, (xb_r, xb16_r, a_r, r_r, ig_r, m2_r), strict=True))
        i = pl.program_id(0)

        @pl.when(i == 0)
        def _():
            scr["c"][...] = cin[...]
            dwca_o[...] = jnp.zeros_like(dwca_o)
            dwcb_o[...] = jnp.zeros_like(dwcb_o)
            dbcb_o[...] = jnp.zeros_like(dbcb_o)

        scr["edge"][...] = jnp.where(i == nt - 1, hin[...], hedge8[0:1, :])
        _lru_bwd_tile(1, lambda rs, ls: _f32(dyb, rs, ls) * _silu(_f32(ql, rs, ls)), h1, wd_r, lam_r, scr,
                      (dwd_o, dba_o, dbx_o, dlam_o), i == 0, tl, sub_r, ng, gs)
        cfin[...] = scr["c"][...]
        tmod = _tmod(sub_r, sub_r)

        def rest(r0, l0):
            rs, ls = pl.ds(r0, sub_r), pl.ds(l0, LANES)
            dxb = dx0[rs, ls] + scr["dyl"][rs, ls] + scr["g"][rs, ls]
            dv, dwb = _conv_bwd(dxb, _f32(vl, rs, ls), wcb_r, 4, 2, tmod, sub_r, ls)
            for j in range(4):
                dwcb_o[j:j + 1, ls] += dwb[j]
            dbcb_o[:, ls] += jnp.sum(dxb, axis=0, keepdims=True)
            q = _f32(ql, rs, ls)
            sq = _sigmoid(q)
            dq = _f32(dyb, rs, ls) * (h0[rs, ls] + h1[rs, ls]) * (sq * (1.0 + q * (1.0 - sq)))
            b_, c_, u_, g_ = _f32(bl, rs, ls), _f32(cl, rs, ls), _f32(ul, rs, ls), _f32(gl, rs, ls)
            z = c_ * u_
            cz = _conv(z, wca_r, 3, 1, tmod, sub_r, ls)
            sgm = _sigmoid(g_)
            sg = g_ * sgm
            da = _f32(dya, rs, ls)
            dz, dwa = _conv_bwd(da * b_ * sg, z, wca_r, 3, 1, tmod, sub_r, ls)
            for j in range(3):
                dwca_o[j:j + 1, ls] += dwa[j]
            parts = (da * cz * sg, dz * u_, dz * c_, da * b_ * cz * (sgm * (1.0 + g_ * (1.0 - sgm))), dv, dq)
            for k, val in enumerate(parts):
                dp_o[rs, pl.ds(k * W + l0, LANES)] = val.astype(dp_o.dtype)

        _sub_loop(tl, sub_r, W, rest)

    def pcol(j):
        return pl.BlockSpec((tl, W), lambda i: (i + ob, j))

    prev = [] if dp_into is None else [dp_into]
    return pl.pallas_call(
        body, name=name, grid=(nt,), input_output_aliases={} if dp_into is None else {0: 0},
        in_specs=[pl.BlockSpec(memory_space=pl.ANY)] * len(prev) + [pcol(j) for j in range(6)]
        + [pl.BlockSpec((tl, W), lambda i: (i, 0)), pl.BlockSpec((tl, W), lambda i: (i, 1)), seq, seq,
           _edge_block(h1s, tl, nt, 1), seq] + [seq] * 6 + [taps, taps, wd_spec, vec, vec, vec],
        out_specs=[pl.BlockSpec((tl, 6 * W), lambda i: (i + dpb, 0)), wd_spec, vec, vec, vec, taps, taps, vec, vec],
        out_shape=[jax.ShapeDtypeStruct((dp_rows, 6 * W), BF16), jax.ShapeDtypeStruct(wd.shape, F32)]
        + [jax.ShapeDtypeStruct((1, W), F32)] * 3
        + [jax.ShapeDtypeStruct((SUBLANES, W), F32)] * 2 + [jax.ShapeDtypeStruct((1, W), F32)] * 2,
        scratch_shapes=_bwd_scratch(tl, W),
        compiler_params=_params(("arbitrary",)),
    )(*prev, *([P] * 6), dcat, dcat, h0s, h1s, h1s, dxb0, *kept, wca, wcb, wd, lam, h_init, c_init)


def _loss_head(out, x, tgt, gt, fg, tr):
    L, D = x.shape

    def body(o_ref, x_ref, t_ref, gt_ref, fg_ref, dn_o, do_o, dfg_o, dgt_o, loss_o):
        i = pl.program_id(0)

        @pl.when(i == 0)
        def _():
            dfg_o[...] = jnp.zeros_like(dfg_o)
            dgt_o[...] = jnp.zeros_like(dgt_o)
            loss_o[...] = jnp.zeros_like(loss_o)

        o = o_ref[...].astype(F32)
        gt_v = gt_ref[...]
        fg_v = fg_ref[...]
        n = x_ref[...] + gt_v * o
        r = lax.rsqrt(jnp.mean(n * n, axis=-1, keepdims=True) + EPS)
        nr = n * r
        e = nr * fg_v - t_ref[...]
        loss_o[...] += 0.5 * jnp.sum(jnp.mean(e * e, axis=-1, keepdims=True))
        dy = e * (1.0 / D)
        dfg_o[...] += jnp.sum(dy * nr, axis=0, keepdims=True)
        qv = dy * fg_v
        dn = r * (qv - nr * jnp.mean(qv * nr, axis=-1, keepdims=True))
        dgt_o[...] += jnp.sum(dn * o, axis=0, keepdims=True)
        dn_o[...] = dn.astype(dn_o.dtype)
        do_o[...] = (dn * gt_v).astype(do_o.dtype)

    blk = pl.BlockSpec((tr, D), lambda i: (i, 0))
    vec = pl.BlockSpec((1, D), lambda i: (0, 0))
    return pl.pallas_call(
        body, name="loss_head", grid=(L // tr,), in_specs=[blk, blk, blk, vec, vec],
        out_specs=[blk, blk, vec, vec, pl.BlockSpec((SUBLANES, LANES), lambda i: (0, 0))],
        out_shape=[jax.ShapeDtypeStruct((L, D), BF16), jax.ShapeDtypeStruct((L, D), BF16),
                   jax.ShapeDtypeStruct((1, D), F32), jax.ShapeDtypeStruct((1, D), F32),
                   jax.ShapeDtypeStruct((SUBLANES, LANES), F32)],
        compiler_params=_params(("arbitrary",)),
    )(out, x, tgt, gt, fg)


def _norm_bwd(dhl, x, dn, g, sc, tr, name):
    L, D = x.shape
    with_x = dn is not None

    def body(*refs):
        if with_x:
            d_ref, x_ref, dn_ref, g_ref, sc_ref, gx_o, dsh_o, dsc_o, dg_o = refs
        else:
            d_ref, x_ref, g_ref, sc_ref, dsh_o, dsc_o, dg_o = refs
        i = pl.program_id(0)

        @pl.when(i == 0)
        def _():
            dsh_o[...] = jnp.zeros_like(dsh_o)
            dsc_o[...] = jnp.zeros_like(dsc_o)
            dg_o[...] = jnp.zeros_like(dg_o)

        d = d_ref[...].astype(F32)
        xv = x_ref[...]
        g_v = g_ref[...]
        r = lax.rsqrt(jnp.mean(xv * xv, axis=-1, keepdims=True) + EPS)
        xr = xv * r
        dsh_o[...] += jnp.sum(d, axis=0, keepdims=True)
        dsc_o[...] += jnp.sum(d * (xr * g_v), axis=0, keepdims=True)
        dxn = d * (1.0 + sc_ref[...])
        dg_o[...] += jnp.sum(dxn * xr, axis=0, keepdims=True)
        if with_x:
            qv = dxn * g_v
            gx_o[...] = r * (qv - xr * jnp.mean(qv * xr, axis=-1, keepdims=True)) + dn_ref[...].astype(F32)

    blk = pl.BlockSpec((tr, D), lambda i: (i, 0))
    vec = pl.BlockSpec((1, D), lambda i: (0, 0))
    vshape = jax.ShapeDtypeStruct((1, D), F32)
    res = pl.pallas_call(
        body, name=name, grid=(L // tr,),
        in_specs=[blk, blk] + ([blk] if with_x else []) + [vec, vec],
        out_specs=([blk] if with_x else []) + [vec, vec, vec],
        out_shape=([jax.ShapeDtypeStruct((L, D), F32)] if with_x else []) + [vshape] * 3,
        compiler_params=_params(("arbitrary",)),
    )(*([dhl, x] + ([dn] if with_x else []) + [g, sc]))
    return res if with_x else [None] + list(res)


def _pack_blockdiag(wa, wx, gs):
    H, hd, _ = wa.shape
    hp = gs // hd
    ng = H // hp
    eye = jnp.eye(hp, dtype=wa.dtype)

    def bd(w):
        return jnp.einsum("gpij,pq->gpiqj", w.reshape(ng, hp, hd, hd), eye).reshape(ng, gs, gs)

    return jnp.concatenate([bd(wa), bd(wx)], axis=-1).astype(BF16)


def _unpack_blockdiag(dwd, H, hd, gs):
    hp = gs // hd
    ng = H // hp
    eye = jnp.eye(hp, dtype=dwd.dtype)

    def diag(dm):
        return jnp.einsum("gpiqj,pq->gpij", dm.reshape(ng, hp, hd, hp, hd), eye).reshape(H, hd, hd)

    return diag(dwd[:, :, :gs]), diag(dwd[:, :, gs:])


def kernel(x, c, ctx, c_ctx, norm_g, w_ada, b_ada, w_in, w_conv_a, w_conv_b, b_conv_b, lru_wa, lru_ba, lru_wx, lru_bx, lru_lambda, w_out, final_g, loss_target, m_c_ctx, m_norm_g, m_w_ada, m_b_ada, m_w_in, m_w_conv_a, m_w_conv_b, m_b_conv_b, m_lru_wa, m_lru_ba, m_lru_wx, m_lru_bx, m_lru_lambda, m_w_out, m_final_g, v_c_ctx, v_norm_g, v_w_ada, v_b_ada, v_w_in, v_w_conv_a, v_w_conv_b, v_b_conv_b, v_lru_wa, v_lru_ba, v_lru_wx, v_lru_bx, v_lru_lambda, v_w_out, v_final_g):
    xi, yi, ci = _pos()
    me = 4 * xi + 2 * yi + ci
    q = 2 * xi + yi
    first_core = (ci == 0).astype(F32)

    L, D = x.shape[1], x.shape[2]
    T = ctx.shape[1]
    W = D // 2
    Wq = W // 4
    H, hd = lru_wa.shape[2], lru_wa.shape[3]
    gs = min(LRU_GROUP, W)
    nq = w_ada.shape[2]
    tl = min(256, T, L)
    tr = min(256, T, L)
    x2, ctx2, tgt2 = x[0], ctx[0], loss_target[0]

    def place(shard, full_cols):
        z = jnp.zeros((shard.shape[0], full_cols), F32)
        return lax.dynamic_update_slice(z, shard * first_core, (0, q * shard.shape[1]))

    c_rows = lax.dynamic_update_slice(jnp.zeros((8, D), F32), c, (me, 0))
    small_in = [c_rows, place(w_conv_a[0], W), place(w_conv_b[0], W), place(lru_ba[0], W), place(lru_bx[0], W),
                place(lru_lambda[0], W)]
    small_shapes = [a.shape for a in small_in]
    gathered = _allreduce8(_pack(small_in, 8 * SUBLANES), "gather_small")
    c_all, wca, wcb, ba_all, bx_all, lam_all = _unpack(gathered, small_shapes)

    s_rows = jnp.concatenate([c_all, c_ctx[None, :], jnp.zeros((7, D), F32)], axis=0)
    mod_part = _matmul(s_rows, w_ada[0], a_act="silu", bias=lax.dynamic_slice(b_ada, (0, q * nq), (1, nq)),
                       tm=16, tn=nq, tk=512, name="ada_fwd")
    mod_all = _allreduce8(_pack([place(mod_part, 4 * nq)], 8 * SUBLANES), "gather_mod")
    mod_all = _unpack(mod_all, [(16, 4 * nq)])[0]
    mod_l = lax.dynamic_slice(mod_all, (me, 0), (1, 3 * D))
    mod_c = mod_all[8:9]
    sh_l, sc_l, gt_l = mod_l[:, :D], mod_l[:, D:2 * D], mod_l[:, 2 * D:]
    sh_c, sc_c = mod_c[:, :D], mod_c[:, D:2 * D]

    pad_taps = lambda w: jnp.pad(w, ((0, SUBLANES - w.shape[0]), (0, 0)))
    wts = {
        "wca": pad_taps(wca), "wcb": pad_taps(wcb), "bcb": b_conv_b,
        "wd": [_pack_blockdiag(lru_wa[0, d], lru_wx[0, d], gs) for d in range(2)],
        "ba": [ba_all[d:d + 1] for d in range(2)], "bx": [bx_all[d:d + 1] for d in range(2)],
        "lam": [lam_all[d:d + 1] for d in range(2)],
    }

    hl = _norm_in(x2, ctx2, norm_g, sc_l, sh_l, sc_c, sh_c, tr)
    p_lat, win_full, wout_full = _in_proj_gather(hl, w_in[0].astype(BF16), w_out[0].astype(BF16),
                                                 jnp.reshape(q, (1,)).astype(jnp.int32), rows=L, tm=min(1024, L))
    p_ctx = _matmul(hl, win_full, a_rows=T, a_off=L, tm=T, tn=1536, tk=D, out_dtype=BF16, name="in_proj_ctx")
    zero_w = jnp.zeros((1, W), F32)
    ctx0, cgates0 = _mix_fwd(p_ctx, 0, zero_w, wts, rows=T, row_off=0, row_w=T, tl=tl, name="ctx_fwd0")
    c1s, _, cgates1 = _mix_fwd(p_ctx, 1, zero_w, wts, rows=T, row_off=0, row_w=T, tl=tl, saved0=ctx0, name="ctx_fwd1")
    h0_init, h1_init = ctx0["h"][T - 1:T], c1s[0:1]
    lat0, gates0 = _mix_fwd(p_lat, 0, h0_init, wts, rows=L, row_off=0, row_w=GRID_W, tl=tl, name="mix_fwd0")
    h1s, cat, gates1 = _mix_fwd(p_lat, 1, h1_init, wts, rows=L, row_off=0, row_w=GRID_W, tl=tl, saved0=lat0,
                                name="mix_fwd1")
    out = _matmul(cat, wout_full, tm=512, tn=D, tk=2 * W, out_dtype=BF16, name="out_proj")
    dn, dout, dfg, dgt, loss_blk = _loss_head(out, x2, tgt2, gt_l, final_g[None, :], tr)

    dcat = _matmul(dout, wout_full, tb=True, tm=512, tn=2 * W, tk=D, out_dtype=BF16, name="out_proj_bwd")
    gw_out = _matmul(cat, dout, ta=True, tm=1024, tn=D, tk=2048, out_dtype=BF16, name="w_out_grad")
    dxb0, dwd0, dba0, dbx0, dlam0, ch0 = _mix_bwd0(p_lat, dcat, lat0, gates0, h0_init, zero_w, wts, rows=L, row_off=0,
                                                   row_w=GRID_W, tl=tl, name="mix_bwd0")
    dp_lat, dwd1, dba1, dbx1, dlam1, dwca, dwcb, dbcb, ch1 = _mix_bwd1(
        p_lat, dcat, lat0, h1s, gates1, dxb0, h1_init, zero_w, wts, rows=L, row_off=0, row_w=GRID_W, tl=tl,
        name="mix_bwd1", dp_rows=L + T)
    zero_cat = jnp.zeros((T, 2 * W), BF16)
    cxb0, cwd0, cba0, cbx0, clam0, _ = _mix_bwd0(p_ctx, zero_cat, ctx0, cgates0, zero_w, ch0, wts, rows=T, row_off=0,
                                                 row_w=T, tl=tl, name="ctx_bwd0")
    dp, cwd1, cba1, cbx1, clam1, cwca, cwcb, cbcb, _ = _mix_bwd1(
        p_ctx, zero_cat, ctx0, c1s, cgates1, cxb0, zero_w, ch1, wts, rows=T, row_off=0, row_w=T, tl=tl, name="ctx_bwd1",
        dp_rows=L + T, dp_off=L, dp_into=dp_lat)

    gw_in = _matmul(hl, dp, ta=True, tm=1024, tn=1536, tk=2816, out_dtype=BF16, name="w_in_grad")
    rs_axes = [1, 0]
    pair_sums = _rs_pair_sums([gw_in, gw_out], rs_axes)
    dhl, rs_slots = _matmul(dp, win_full, tb=True, a_rows=L, tm=512, tn=D, tk=3072, out_dtype=BF16, name="in_proj_bwd",
                            side=_rs_chips_side(pair_sums))
    dhc = _matmul(dp, win_full, tb=True, a_rows=T, a_off=L, tm=T, tn=D, tk=3072, name="in_proj_bwd_ctx")
    gx, dsh_l, dsc_l, dng_l = _norm_bwd(dhl, x2, dn, norm_g, sc_l, tr, "norm_bwd")
    _, dsh_c, dsc_c, dng_c = _norm_bwd(dhc, ctx2, None, norm_g, sc_c, tr, "norm_bwd_ctx")

    g_in_shard, g_out_shard = _rs_finish(rs_slots, rs_axes)

    dwa0, dwx0 = _unpack_blockdiag(dwd0 + cwd0, H, hd, gs)
    dwa1, dwx1 = _unpack_blockdiag(dwd1 + cwd1, H, hd, gs)
    zeros_d = jnp.zeros((1, D), F32)
    dmod_l = jnp.concatenate([dsh_l, dsc_l, dgt], axis=1)
    dmod_c = jnp.concatenate([dsh_c, dsc_c, zeros_d], axis=1)
    small_g = [
        lax.dynamic_update_slice(jnp.zeros((8, 3 * D), F32), dmod_l, (me, 0)), dmod_c,
        dfg, dng_l + dng_c, (dwca + cwca)[:3], (dwcb + cwcb)[:4], dbcb + cbcb,
        jnp.stack([dwa0, dwa1]), jnp.stack([dwx0, dwx1]),
        jnp.concatenate([dba0 + cba0, dba1 + cba1], axis=0), jnp.concatenate([dbx0 + cbx0, dbx1 + cbx1], axis=0),
        jnp.concatenate([dlam0 + clam0, dlam1 + clam1], axis=0), loss_blk[0:1, 0:1],
    ]
    g_shapes = [a.shape for a in small_g]
    (g_rows, g_modc, g_fg, g_ng, g_wca, g_wcb, g_bcb, g_wa, g_wx, g_ba, g_bx, g_lam, loss_sum) = _unpack(
        _allreduce8(_pack(small_g, 8 * SUBLANES), "reduce_small"), g_shapes)

    g_mod = jnp.concatenate([g_rows, g_modc, jnp.zeros((7, 3 * D), F32)], axis=0)
    g_mod_q = lax.dynamic_slice(g_mod, (0, q * nq), (16, nq))
    g_w_ada = _matmul(s_rows, g_mod_q, ta=True, a_act="silu", tm=1024, tn=nq, tk=16, name="w_ada_grad")
    g_b_ada = jnp.sum(g_mod[:9], axis=0, keepdims=True)
    gc_part = _matmul(jnp.pad(lax.dynamic_slice(g_modc, (0, q * nq), (1, nq)), ((0, 7), (0, 0))), w_ada[0], tb=True,
                      dsilu_mul=c_ctx[None, :], tm=8, tn=D, tk=512, name="c_ctx_grad")
    g_c_ctx = _unpack(_allreduce8(_pack([gc_part[0:1] * first_core], 8 * SUBLANES), "reduce_c_ctx"), [(D,)])[0]

    def shard_cols(a, width):
        return lax.dynamic_slice(a, (0, q * width), (a.shape[0], width))

    grads = {
        "c_ctx": g_c_ctx, "norm_g": g_ng, "b_ada": g_b_ada,
        "w_conv_a": shard_cols(g_wca, Wq)[None], "w_conv_b": shard_cols(g_wcb, Wq)[None], "b_conv_b": g_bcb,
        "lru_wa": g_wa[None], "lru_ba": shard_cols(g_ba, Wq)[None], "lru_wx": g_wx[None],
        "lru_bx": shard_cols(g_bx, Wq)[None], "lru_lambda": shard_cols(g_lam, Wq)[None], "final_g": g_fg[0],
    }
    small_names = list(grads)
    given = dict(c_ctx=(c_ctx, m_c_ctx, v_c_ctx), norm_g=(norm_g, m_norm_g, v_norm_g), b_ada=(b_ada, m_b_ada, v_b_ada),
                 w_conv_a=(w_conv_a, m_w_conv_a, v_w_conv_a), w_conv_b=(w_conv_b, m_w_conv_b, v_w_conv_b),
                 b_conv_b=(b_conv_b, m_b_conv_b, v_b_conv_b), lru_wa=(lru_wa, m_lru_wa, v_lru_wa),
                 lru_ba=(lru_ba, m_lru_ba, v_lru_ba), lru_wx=(lru_wx, m_lru_wx, v_lru_wx),
                 lru_bx=(lru_bx, m_lru_bx, v_lru_bx), lru_lambda=(lru_lambda, m_lru_lambda, v_lru_lambda),
                 final_g=(final_g, m_final_g, v_final_g))
    def rows2d(a):
        return a.reshape(-1, a.shape[-1])

    grads = {n: grads[n].reshape(given[n][0].shape) for n in small_names}
    quads = [tuple(rows2d(a) for a in (given[n][0], grads[n], given[n][1], given[n][2])) for n in small_names]
    updated = _adam_many(quads, "adam_small")
    delta_s, newm_s, newv_s = ({n: u[j].reshape(given[n][0].shape) for n, u in zip(small_names, updated)} for j in range(3))

    big = {"w_ada": (w_ada, g_w_ada, m_w_ada, v_w_ada), "w_in": (w_in, g_in_shard, m_w_in, v_w_in),
           "w_out": (w_out, g_out_shard, m_w_out, v_w_out)}
    delta_b, newm_b, newv_b = {}, {}, {}
    for n, (w, g, m, v) in big.items():
        d_, m_, v_, *echo = _adam(w[0], g, m[0], v[0], "adam_" + n, echo_g=n != "w_ada")
        grads[n] = (echo[0] if echo else g)[None]
        delta_b[n], newm_b[n], newv_b[n] = d_[None], m_[None], v_[None]

    loss = loss_sum[0, 0]
    order = ["c_ctx", "norm_g", "w_ada", "b_ada", "w_in", "w_conv_a", "w_conv_b", "b_conv_b", "lru_wa", "lru_ba",
             "lru_wx", "lru_bx", "lru_lambda", "w_out", "final_g"]
    delta = {**delta_s, **delta_b}
    newm = {**newm_s, **newm_b}
    newv = {**newv_s, **newv_b}
    return (loss, gx[None], *[grads[n] for n in order], *[delta[n] for n in order], *[newm[n] for n in order],
            *[newv[n] for n in order])
```

```python
import functools

import jax
import jax.numpy as jnp
from jax import lax
from jax.experimental import pallas as pl
from jax.experimental.pallas import tpu as pltpu

F32 = jnp.float32
BF16 = jnp.bfloat16
MESH_ID = pl.DeviceIdType.MESH

EPS = 1e-6
LRU_C = 8.0
GRID_W = 64
ADAM_LR = 0.001
ADAM_B1 = 0.9
ADAM_B2 = 0.999
ADAM_EPS = 1e-08
ADAM_WD = 0.01
ADAM_STEP = 10

LANES = 128
SUBLANES = 8
PACK_COLS = 1024
VMEM_LIMIT = 56 * 2**20
LRU_GROUP = 256


def _params(sem=None):
    return pltpu.CompilerParams(vmem_limit_bytes=VMEM_LIMIT, dimension_semantics=sem)


def _pick(dim, pref, quantum=LANES):
    if dim <= pref:
        return dim
    best = None
    for t in range(quantum, pref + 1, quantum):
        if dim % t == 0:
            best = t
    assert best is not None, (dim, pref)
    return best


def _pos():
    return lax.axis_index("x"), lax.axis_index("y"), lax.axis_index("c")


def _flip(v, bit):
    return 1 - v if bit else v


def _sigmoid(v):
    return 0.5 * jnp.tanh(0.5 * v) + 0.5


def _silu(v):
    return v * _sigmoid(v)


def _dsilu(v):
    s = _sigmoid(v)
    return s * (1.0 + v * (1.0 - s))


def _gates(pre_r, pre_i, sp):
    r = _sigmoid(pre_r)
    ig = _sigmoid(pre_i)
    e = LRU_C * r * sp
    w = jnp.tanh(e)
    return r, ig, jnp.exp(-e), (2.0 * w) * pl.reciprocal(1.0 + w, approx=True)


def _softplus(z):
    return jnp.maximum(z, 0.0) + jnp.log1p(jnp.exp(-jnp.abs(z)))


def _matmul(a, b, *, ta=False, tb=False, tm=512, tn=512, tk=512, out_dtype=F32, name,
            a_rows=None, a_off=0, a_act=None, bias=None, dsilu_mul=None, side=None):
    rows_a = a.shape[0] if a_rows is None else a_rows
    if ta:
        K, M = rows_a, a.shape[1]
    else:
        M, K = rows_a, a.shape[1]
    N = b.shape[0] if tb else b.shape[1]
    tm, tn, tk = _pick(M, tm, SUBLANES), _pick(N, tn), _pick(K, tk)
    t_rows = tk if ta else tm
    assert a_off % t_rows == 0
    nk = K // tk
    gi, gj = M // tm, N // tn
    off_blocks = a_off // t_rows
    dims = (((0 if ta else 1,), (1 if tb else 0,)), ((), ()))
    extras = [e for e in (bias, dsilu_mul) if e is not None]
    n_sin = len(side["ins"]) if side else 0
    n_sout = len(side["outs"]) if side else 0

    def body(a_ref, b_ref, *rest):
        rest = list(rest)
        bias_ref = rest.pop(0) if bias is not None else None
        dsm_ref = rest.pop(0) if dsilu_mul is not None else None
        side_in = [rest.pop(0) for _ in range(n_sin)]
        o_ref = rest.pop(0)
        side_out = [rest.pop(0) for _ in range(n_sout)]
        acc_ref = rest.pop(0) if nk > 1 else None
        side_scr = rest
        i, j, k = pl.program_id(0), pl.program_id(1), pl.program_id(2)

        if side:
            @pl.when((i == 0) & (j == 0) & (k == 0))
            def _():
                side["start"](side_in, side_out, side_scr)

        av = a_ref[...]
        if a_act == "silu":
            av = _silu(av)
        prod = lax.dot_general(av, b_ref[...], dims, preferred_element_type=F32)

        def finish(r):
            if bias_ref is not None:
                r = r + bias_ref[...]
            if dsm_ref is not None:
                r = r * _dsilu(dsm_ref[...])
            o_ref[...] = r.astype(o_ref.dtype)

        if nk == 1:
            finish(prod)
        else:
            @pl.when(k == 0)
            def _():
                acc_ref[...] = prod

            @pl.when(k > 0)
            def _():
                acc_ref[...] += prod

            @pl.when(k == nk - 1)
            def _():
                finish(acc_ref[...])

        if side:
            @pl.when((i == gi - 1) & (j == gj - 1) & (k == nk - 1))
            def _():
                side["finish"](side_in, side_out, side_scr)

    if ta:
        a_spec = pl.BlockSpec((tk, tm), lambda i, j, k: (k + off_blocks, i))
    else:
        a_spec = pl.BlockSpec((tm, tk), lambda i, j, k: (i + off_blocks, k))
    if tb:
        b_spec = pl.BlockSpec((tn, tk), lambda i, j, k: (j, k))
    else:
        b_spec = pl.BlockSpec((tk, tn), lambda i, j, k: (k, j))
    in_specs = [a_spec, b_spec]
    if bias is not None:
        in_specs.append(pl.BlockSpec((1, tn), lambda i, j, k: (0, j)))
    if dsilu_mul is not None:
        in_specs.append(pl.BlockSpec((1, tn), lambda i, j, k: (0, j)))
    hbm = pl.BlockSpec(memory_space=pl.ANY)
    res = pl.pallas_call(
        body, name=name, grid=(gi, gj, nk),
        in_specs=in_specs + [hbm] * n_sin,
        out_specs=[pl.BlockSpec((tm, tn), lambda i, j, k: (i, j))] + [hbm] * n_sout,
        out_shape=[jax.ShapeDtypeStruct((M, N), out_dtype)] + (list(side["outs"]) if side else []),
        scratch_shapes=([pltpu.VMEM((tm, tn), F32)] if nk > 1 else []) + (list(side["scratch"]) if side else []),
        compiler_params=_params(("arbitrary",) * 3 if side else ("parallel", "parallel", "arbitrary")),
    )(a, b, *extras, *(side["ins"] if side else []))
    return (res[0], res[1:]) if side else res[0]


def _elementwise(fn, ins, outs, *, rows, cols, name, tr=256):
    tr = _pick(rows, tr, 2 * SUBLANES)
    n_in = len(ins)

    def body(*refs):
        vals = fn(*[r[...] for r in refs[:n_in]])
        if not isinstance(vals, (tuple, list)):
            vals = (vals,)
        for r, v in zip(refs[n_in:], vals, strict=True):
            r[...] = v.astype(r.dtype)

    def spec(off):
        assert off % tr == 0
        ob = off // tr
        return pl.BlockSpec((tr, cols), lambda i: (i + ob, 0))

    res = pl.pallas_call(
        body, name=name, grid=(rows // tr,),
        in_specs=[spec(off) for _, off in ins],
        out_specs=[spec(0) for _ in outs],
        out_shape=[jax.ShapeDtypeStruct((rows, cols), dt) for dt in outs],
        compiler_params=_params(("parallel",)),
    )(*[a for a, _ in ins])
    return res


def _adam_math(w, g, m, v):
    m = ADAM_B1 * m + (1.0 - ADAM_B1) * g
    v = ADAM_B2 * v + (1.0 - ADAM_B2) * (g * g)
    m_hat = m / (1.0 - ADAM_B1 ** ADAM_STEP)
    v_hat = v / (1.0 - ADAM_B2 ** ADAM_STEP)
    delta = -ADAM_LR * (m_hat / (jnp.sqrt(v_hat) + ADAM_EPS) + ADAM_WD * w)
    return delta, m, v


def _adam(w, g, m, v, name, echo_g=False):
    rows, cols = w.shape
    fn = (lambda w_, g_, m_, v_: _adam_math(w_, g_, m_, v_) + (g_,)) if echo_g else _adam_math
    return _elementwise(fn, [(w, 0), (g, 0), (m, 0), (v, 0)], [F32] * (4 if echo_g else 3),
                        rows=rows, cols=cols, name=name)


def _adam_many(quads, name):
    n = len(quads)

    def body(*refs):
        ins, outs = refs[:4 * n], refs[4 * n:]
        for t in range(n):
            w, g, m, v = (r[...] for r in ins[4 * t:4 * t + 4])
            for o_ref, val in zip(outs[3 * t:3 * t + 3], _adam_math(w, g, m, v), strict=True):
                o_ref[...] = val

    res = pl.pallas_call(
        body, name=name,
        out_shape=[jax.ShapeDtypeStruct(q[0].shape, F32) for q in quads for _ in range(3)],
        compiler_params=_params(),
    )(*[a for q in quads for a in q])
    return [res[3 * t:3 * t + 3] for t in range(n)]


def _pack(arrs, row_quantum):
    flat = jnp.concatenate([a.reshape(-1).astype(F32) for a in arrs])
    n = flat.shape[0]
    q = row_quantum * PACK_COLS
    total = -(-n // q) * q
    flat = jnp.pad(flat, (0, total - n))
    return flat.reshape(total // PACK_COLS, PACK_COLS)


def _unpack(buf, shapes):
    flat = buf.reshape(-1)
    out, off = [], 0
    for s in shapes:
        n = 1
        for d in s:
            n *= d
        out.append(flat[off:off + n].reshape(s))
        off += n
    return out


def _allreduce8(buf, name):
    R, C = buf.shape
    assert R % (8 * SUBLANES) == 0
    m = R // 8

    def body(x_ref, o_ref, recv, red, s1, r1, s2, r2):
        x, y, c = _pos()
        me = 4 * x + 2 * y + c

        def peer(k):
            px, py, pc = _flip(x, (k >> 2) & 1), _flip(y, (k >> 1) & 1), _flip(c, k & 1)
            return (px, py, pc), 4 * px + 2 * py + pc

        def rows(ref, idx):
            return ref.at[pl.ds(pl.multiple_of(idx * m, SUBLANES), m), :]

        def scatter(k):
            dev, p = peer(k)
            return pltpu.make_async_remote_copy(src_ref=rows(x_ref, p), dst_ref=recv.at[k], send_sem=s1.at[k],
                                                recv_sem=r1.at[k], device_id=dev, device_id_type=MESH_ID)

        def share(k):
            dev, p = peer(k)
            return pltpu.make_async_remote_copy(src_ref=red, dst_ref=rows(o_ref, me), send_sem=s2.at[k],
                                                recv_sem=r2.at[k], device_id=dev, device_id_type=MESH_ID)

        def shared_from(k):
            dev, p = peer(k)
            return pltpu.make_async_remote_copy(src_ref=red, dst_ref=rows(o_ref, p), send_sem=s2.at[k],
                                                recv_sem=r2.at[k], device_id=dev, device_id_type=MESH_ID)

        for k in range(1, 8):
            scatter(k).start()
        acc = rows(x_ref, me)[...]
        for k in range(1, 8):
            scatter(k).wait_recv()
            acc = acc + recv[k]
        red[...] = acc
        rows(o_ref, me)[...] = acc
        for k in range(1, 8):
            share(k).start()
        for k in range(1, 8):
            shared_from(k).wait_recv()
        for k in range(1, 8):
            scatter(k).wait_send()
            share(k).wait_send()

    return pl.pallas_call(
        body, name=name,
        in_specs=[pl.BlockSpec(memory_space=pltpu.VMEM)],
        out_specs=pl.BlockSpec(memory_space=pltpu.VMEM),
        out_shape=jax.ShapeDtypeStruct((R, C), F32),
        scratch_shapes=[pltpu.VMEM((8, m, C), F32), pltpu.VMEM((m, C), F32),
                        pltpu.SemaphoreType.DMA((8,)), pltpu.SemaphoreType.DMA((8,)),
                        pltpu.SemaphoreType.DMA((8,)), pltpu.SemaphoreType.DMA((8,))],
        compiler_params=_params(),
    )(buf)


def _allreduce8_two_level(buf, name):
    R, C = buf.shape
    assert R % (8 * SUBLANES) == 0
    m, hr = R // 8, R // 2

    def body(x_ref, o_ref, got0, half, got1, red, ssem, rsem):
        x, y, c = _pos()
        q = 2 * x + y
        sibling = (x, y, 1 - c)

        def half_rows(ref, core):
            return ref.at[pl.ds(pl.multiple_of(core * hr, SUBLANES), hr), :]

        def chunk(ref, core, chip):
            return ref.at[pl.ds(pl.multiple_of(core * hr + chip * m, SUBLANES), m), :]

        def chip_of(k):
            px, py = _flip(x, (k >> 1) & 1), _flip(y, k & 1)
            return (px, py, c), 2 * px + py

        def copy(src, dst, phase, k, dev):
            return pltpu.make_async_remote_copy(src_ref=src, dst_ref=dst, send_sem=ssem.at[phase, k],
                                                recv_sem=rsem.at[phase, k], device_id=dev, device_id_type=MESH_ID)

        swap = copy(half_rows(x_ref, 1 - c), got0, 0, 0, sibling)
        swap.start()
        swap.wait()
        half[...] = half_rows(x_ref, c)[...] + got0[...]

        def scatter(k):
            dev, p = chip_of(k)
            return copy(half.at[pl.ds(pl.multiple_of(p * m, SUBLANES), m), :], got1.at[k], 1, k, dev)

        for k in range(1, 4):
            scatter(k).start()
        acc = half[pl.ds(pl.multiple_of(q * m, SUBLANES), m), :]
        for k in range(1, 4):
            scatter(k).wait_recv()
            acc = acc + got1[k]
        red[...] = acc
        chunk(o_ref, c, q)[...] = acc

        def share(k, landing_chip):
            return copy(red, chunk(o_ref, c, landing_chip), 2, k, chip_of(k)[0])

        for k in range(1, 4):
            share(k, q).start()
        for k in range(1, 4):
            share(k, chip_of(k)[1]).wait_recv()
        back = copy(half_rows(o_ref, c), half_rows(o_ref, c), 3, 0, sibling)
        back.start()
        copy(half_rows(o_ref, 1 - c), half_rows(o_ref, 1 - c), 3, 0, sibling).wait_recv()
        back.wait_send()
        for k in range(1, 4):
            scatter(k).wait_send()
            share(k, q).wait_send()

    return pl.pallas_call(
        body, name=name,
        in_specs=[pl.BlockSpec(memory_space=pltpu.VMEM)],
        out_specs=pl.BlockSpec(memory_space=pltpu.VMEM),
        out_shape=jax.ShapeDtypeStruct((R, C), F32),
        scratch_shapes=[pltpu.VMEM((hr, C), F32), pltpu.VMEM((hr, C), F32), pltpu.VMEM((4, m, C), F32), pltpu.VMEM((m, C), F32),
                        pltpu.SemaphoreType.DMA((4, 4)), pltpu.SemaphoreType.DMA((4, 4))],
        compiler_params=_params(),
    )(buf)


def _bounce(src, dst, buf, sem):
    cin = pltpu.make_async_copy(src, buf, sem)
    cin.start()
    cin.wait()
    cout = pltpu.make_async_copy(buf, dst, sem)
    cout.start()
    cout.wait()


def _chunk(ref, axis, idx, size):
    start = idx * size
    if axis == 0:
        return ref.at[pl.ds(start, size), :]
    return ref.at[:, pl.ds(start, size)]


def _in_proj_gather(hl, win, wout, q_arr, *, rows, tm):
    D, nq = win.shape
    dq, D2 = wout.shape
    ni = rows // tm
    ops = ((0, 1, nq, D // 2), (1, 0, dq, dq // 2))

    def body(q_ref, a_ref, win_ref, wout_ref, p_ref, gin_ref, gout_ref, b_scr, buf_out, lsem, ssem, rsem, fsem, gsem):
        j, i = pl.program_id(0), pl.program_id(1)
        x, y, c = _pos()
        q = 2 * x + y
        srcs = (win_ref, wout_ref)
        dsts = (gin_ref, gout_ref)

        def shard_window(o, chip):
            _, axis, size, _ = ops[o]
            return _chunk(dsts[o], axis, chip, size)

        def half(ref, o, core):
            return ref.at[pl.ds(core * ops[o][3], ops[o][3]), :]

        def half_window(o, chip, core):
            _, axis, size, hs = ops[o]
            if axis == 1:
                return dsts[o].at[pl.ds(core * hs, hs), pl.ds(chip * size, size)]
            return dsts[o].at[pl.ds(chip * size + core * hs, hs), :]

        def chip_of(k):
            px, py = _flip(x, (k >> 1) & 1), _flip(y, k & 1)
            return px, py, 2 * px + py

        def send(o, k):
            px, py, _ = chip_of(k)
            return pltpu.make_async_remote_copy(
                src_ref=half(srcs[o], o, c), dst_ref=half_window(o, q, c), send_sem=ssem.at[o, k],
                recv_sem=rsem.at[o, k], device_id=(px, py, c), device_id_type=MESH_ID)

        def chip_recv(o, k):
            px, py, pq = chip_of(k)
            landed = half_window(o, pq, c)
            pltpu.make_async_remote_copy(src_ref=landed, dst_ref=landed, send_sem=ssem.at[o, k], recv_sem=rsem.at[o, k],
                                         device_id=(px, py, c), device_id_type=MESH_ID).wait_recv()

        def to_sibling(o, k):
            landed = half_window(o, chip_of(k)[2], c)
            return pltpu.make_async_remote_copy(src_ref=landed, dst_ref=landed, send_sem=fsem.at[o, k],
                                                recv_sem=gsem.at[o, k], device_id=(x, y, 1 - c), device_id_type=MESH_ID)

        def from_sibling(o, k):
            theirs = half_window(o, chip_of(k)[2], 1 - c)
            pltpu.make_async_remote_copy(src_ref=theirs, dst_ref=theirs, send_sem=fsem.at[o, k], recv_sem=gsem.at[o, k],
                                         device_id=(x, y, 1 - c), device_id_type=MESH_ID).wait_recv()

        def relay(o, core):
            if core == 0:
                landed, target = half_window(o, chip_of(2)[2], 0), (x, 1 - y, 0)
            else:
                landed, target = half_window(o, chip_of(1)[2], 1), (1 - x, y, 1)
            return pltpu.make_async_remote_copy(src_ref=landed, dst_ref=landed, send_sem=ssem.at[o, 3],
                                                recv_sem=rsem.at[o, 3], device_id=target, device_id_type=MESH_ID)

        def on_core(core, fn):
            @pl.when(c == core)
            def _():
                fn()

        def land(o, k):
            chip_recv(o, k)
            if k == 2:
                on_core(0, lambda: relay(o, 0).start())
            if k == 1:
                on_core(1, lambda: relay(o, 1).start())
            to_sibling(o, k).start()

        def settle(o, k):
            from_sibling(o, k)
            to_sibling(o, k).wait_send()

        def b_load(k, slot):
            src = win_ref if k == 0 else shard_window(0, chip_of(k)[2])
            return pltpu.make_async_copy(src, b_scr.at[slot], lsem.at[0])

        def own_store():
            return pltpu.make_async_copy(b_scr.at[0], shard_window(0, q), lsem.at[2])

        order = (0, 2, 1, 3)
        early = max(ni - 2, 0)

        @pl.when((j == 0) & (i == 0))
        def _():
            for o in range(2):
                for k in (2, 1):
                    send(o, k).start()
            first = b_load(0, 0)
            first.start()
            first.wait()
            own_store().start()
            _bounce(wout_ref, shard_window(1, q), buf_out, lsem.at[1])

        for jj in range(3):
            nxt = order[jj + 1]

            @pl.when((j == jj) & (i == early))
            def _(nxt=nxt):
                land(0, nxt)

            @pl.when((j == jj) & (i == ni - 1))
            def _(jj=jj, nxt=nxt):
                settle(0, nxt)
                if jj == 1:
                    own_store().wait()
                b_load(nxt, (jj + 1) % 2).start()

            @pl.when((j == jj + 1) & (i == 0))
            def _(jj=jj, nxt=nxt):
                b_load(nxt, (jj + 1) % 2).wait()

        @pl.when((j == 3) & (i == 0))
        def _():
            land(1, 2)
            land(1, 1)

        p_ref[...] = jnp.dot(a_ref[...], b_scr[j % 2], preferred_element_type=F32).astype(p_ref.dtype)

        @pl.when((j == 3) & (i == ni - 1))
        def _():
            settle(1, 2)
            settle(1, 1)
            land(1, 3)
            settle(1, 3)
            for o in range(2):
                for k in (2, 1):
                    send(o, k).wait_send()
                for core in range(2):
                    on_core(core, lambda o=o, core=core: relay(o, core).wait_send())

    hbm = pl.BlockSpec(memory_space=pl.ANY)
    grid_spec = pltpu.PrefetchScalarGridSpec(
        num_scalar_prefetch=1, grid=(4, ni),
        in_specs=[pl.BlockSpec((tm, D), lambda j, i, qr: (i, 0)), hbm, hbm],
        out_specs=[pl.BlockSpec((tm, nq), lambda j, i, qr: (i, jnp.bitwise_xor(qr[0], ((j & 1) << 1) | (j >> 1)))),
                   hbm, hbm],
        scratch_shapes=[pltpu.VMEM((2,) + win.shape, win.dtype), pltpu.VMEM(wout.shape, wout.dtype), pltpu.SemaphoreType.DMA((3,))]
        + [pltpu.SemaphoreType.DMA((2, 4)) for _ in range(4)])
    return pl.pallas_call(
        body, name="in_proj_gather", grid_spec=grid_spec,
        out_shape=[jax.ShapeDtypeStruct((rows, 4 * nq), BF16), jax.ShapeDtypeStruct((D, 4 * nq), win.dtype),
                   jax.ShapeDtypeStruct((4 * dq, D2), wout.dtype)],
        compiler_params=_params(("arbitrary", "arbitrary")),
    )(q_arr, hl, win, wout)


def _rs_to_sibling(gs, axes):
    n = len(gs)
    shapes = []
    for g, ax in zip(gs, axes):
        s = list(g.shape)
        s[ax] //= 8
        shapes.append(tuple(s))

    def body(*refs):
        g_refs, mine, landed = refs[:n], refs[n:2 * n], refs[2 * n:3 * n]
        bufs = refs[3 * n:4 * n]
        lsem, ssem, rsem = refs[4 * n:]
        x, y, c = _pos()
        cps = []
        for o in range(n):
            size = shapes[o][axes[o]]
            for j in range(4):
                rc = pltpu.make_async_remote_copy(
                    src_ref=_chunk(g_refs[o], axes[o], 2 * j + 1 - c, size), dst_ref=landed[o].at[j],
                    send_sem=ssem.at[o, j], recv_sem=rsem.at[o, j], device_id=(x, y, 1 - c), device_id_type=MESH_ID)
                rc.start()
                cps.append(rc)
        for o in range(n):
            size = shapes[o][axes[o]]
            for j in range(4):
                _bounce(_chunk(g_refs[o], axes[o], 2 * j + c, size), mine[o].at[j], bufs[o], lsem.at[o])
        for rc in cps:
            rc.wait()

    hbm = pl.BlockSpec(memory_space=pl.ANY)
    outs = [jax.ShapeDtypeStruct((4,) + s, g.dtype) for s, g in zip(shapes, gs)]
    res = pl.pallas_call(
        body, name="rs_to_sibling", in_specs=[hbm] * n, out_specs=[hbm] * (2 * n), out_shape=outs + outs,
        scratch_shapes=[pltpu.VMEM(s, g.dtype) for s, g in zip(shapes, gs)]
        + [pltpu.SemaphoreType.DMA((n,)), pltpu.SemaphoreType.DMA((n, 4)), pltpu.SemaphoreType.DMA((n, 4))],
        compiler_params=_params(),
    )(*gs)
    return res[:n], res[n:]


def _rs_chips_side(parts):
    n = len(parts)

    def copies(p_refs, slots, scr):
        ssem, rsem = scr[n + 1], scr[n + 2]
        x, y, c = _pos()
        cps = []
        for o in range(n):
            for k in range(1, 4):
                px, py = _flip(x, (k >> 1) & 1), _flip(y, k & 1)
                cps.append(pltpu.make_async_remote_copy(
                    src_ref=p_refs[o].at[2 * px + py], dst_ref=slots[o].at[k], send_sem=ssem.at[o, k],
                    recv_sem=rsem.at[o, k], device_id=(px, py, c), device_id_type=MESH_ID))
        return cps

    def start(p_refs, slots, scr):
        for cp in copies(p_refs, slots, scr):
            cp.start()

    def finish(p_refs, slots, scr):
        x, y, _ = _pos()
        q = 2 * x + y
        for o in range(n):
            _bounce(p_refs[o].at[q], slots[o].at[0], scr[o], scr[n].at[o])
        for cp in copies(p_refs, slots, scr):
            cp.wait()

    return dict(
        ins=list(parts), outs=[jax.ShapeDtypeStruct(p.shape, p.dtype) for p in parts],
        scratch=[pltpu.VMEM(p.shape[1:], p.dtype) for p in parts]
        + [pltpu.SemaphoreType.DMA((n,)), pltpu.SemaphoreType.DMA((n, 4)), pltpu.SemaphoreType.DMA((n, 4))],
        start=start, finish=finish)


def _rs_share(rs, axes):
    n = len(rs)
    shapes = []
    for r, ax in zip(rs, axes):
        s = list(r.shape)
        s[ax] *= 2
        shapes.append(tuple(s))

    def body(*refs):
        r_refs, outs = refs[:n], refs[n:2 * n]
        bufs = refs[2 * n:3 * n]
        lsem, ssem, rsem = refs[3 * n:]
        x, y, c = _pos()
        cps = []
        for o in range(n):
            size = r_refs[o].shape[axes[o]]
            window = _chunk(outs[o], axes[o], c, size)
            rc = pltpu.make_async_remote_copy(src_ref=r_refs[o], dst_ref=window, send_sem=ssem.at[o], recv_sem=rsem.at[o],
                                              device_id=(x, y, 1 - c), device_id_type=MESH_ID)
            rc.start()
            cps.append(rc)
        for o in range(n):
            size = r_refs[o].shape[axes[o]]
            _bounce(r_refs[o], _chunk(outs[o], axes[o], c, size), bufs[o], lsem.at[o])
        for cp in cps:
            cp.wait()

    hbm = pl.BlockSpec(memory_space=pl.ANY)
    return pl.pallas_call(
        body, name="rs_share", in_specs=[hbm] * n, out_specs=[hbm] * n,
        out_shape=[jax.ShapeDtypeStruct(s, r.dtype) for s, r in zip(shapes, rs)],
        scratch_shapes=[pltpu.VMEM(r.shape, r.dtype) for r in rs] + [pltpu.SemaphoreType.DMA((n,)) for _ in range(3)],
        compiler_params=_params(),
    )(*rs)


def _rs_pair_sums(gs, axes):
    mine, landed = _rs_to_sibling(gs, axes)
    pair_sums = []
    for o, (mi, la) in enumerate(zip(mine, landed)):
        rows, cols = mi.shape[0] * mi.shape[1], mi.shape[2]
        s = _elementwise(lambda a, b: a.astype(F32) + b.astype(F32), [(mi.reshape(rows, cols), 0), (la.reshape(rows, cols), 0)],
                         [BF16], rows=rows, cols=cols, name=f"rs_pair_sum{o}")[0]
        pair_sums.append(s.reshape(mi.shape))
    return pair_sums


def _rs_finish(slots, axes):
    reduced = []
    for o, sl in enumerate(slots):
        rows, cols = sl.shape[1], sl.shape[2]
        flat = sl.reshape(4 * rows, cols)
        r = _elementwise(lambda a, b, c, d: (a.astype(F32) + b.astype(F32)) + (c.astype(F32) + d.astype(F32)),
                         [(flat, k * rows) for k in range(4)], [F32], rows=rows, cols=cols, name=f"rs_chip_sum{o}")[0]
        reduced.append(r)
    return _rs_share(reduced, axes)


def _norm_in(x, ctx, g, sc_l, sh_l, sc_c, sh_c, tr):
    L, D = x.shape
    T = ctx.shape[0]
    nx, nc = L // tr, T // tr

    def body(x_ref, c_ref, g_ref, scl, shl, scc, shc, o_ref):
        i = pl.program_id(0)

        def run(src, sc, sh):
            v = src[...]
            r = lax.rsqrt(jnp.mean(v * v, axis=-1, keepdims=True) + EPS)
            o_ref[...] = ((v * r * g_ref[...]) * (1.0 + sc[...]) + sh[...]).astype(o_ref.dtype)

        @pl.when(i < nx)
        def _():
            run(x_ref, scl, shl)

        @pl.when(i >= nx)
        def _():
            run(c_ref, scc, shc)

    vec = pl.BlockSpec((1, D), lambda i: (0, 0))
    return pl.pallas_call(
        body, name="norm_in", grid=(nx + nc,),
        in_specs=[pl.BlockSpec((tr, D), lambda i: (jnp.minimum(i, nx - 1), 0)),
                  pl.BlockSpec((tr, D), lambda i: (jnp.maximum(i - nx, 0), 0)), vec, vec, vec, vec, vec],
        out_specs=pl.BlockSpec((tr, D), lambda i: (i, 0)),
        out_shape=jax.ShapeDtypeStruct((L + T, D), BF16),
        compiler_params=_params(("arbitrary",)),
    )(x, ctx, g, sc_l, sh_l, sc_c, sh_c)


def _tmod(tl, row_w):
    assert row_w & (row_w - 1) == 0
    return lax.broadcasted_iota(jnp.int32, (tl, 1), 0) & (row_w - 1)


def _shift(z, k, tmod, row_w):
    tl = z.shape[0]
    rolled = pltpu.roll(z, k % tl, 0)
    mask = (tmod >= k) if k > 0 else (tmod < row_w + k)
    return jnp.where(mask, rolled, 0.0)


def _conv(z, w_ref, taps, left, tmod, row_w, lanes=slice(None)):
    out = None
    for j in range(taps):
        k = left - j
        term = (z if k == 0 else _shift(z, k, tmod, row_w)) * w_ref[j:j + 1, lanes]
        out = term if out is None else out + term
    return out


def _conv_bwd(dz, z, w_ref, taps, left, tmod, row_w, lanes=slice(None)):
    din = None
    dws = []
    for j in range(taps):
        k = left - j
        shifted = dz if k == 0 else _shift(dz, -k, tmod, row_w)
        term = shifted * w_ref[j:j + 1, lanes]
        din = term if din is None else din + term
        dws.append(jnp.sum(shifted * z, axis=0, keepdims=True))
    return din, dws


def _gate_matmul(xb16_ref, wd_ref, pre_scr, W, ng, gs):
    for g in range(ng):
        pg = jnp.dot(xb16_ref[:, g * gs:(g + 1) * gs], wd_ref[g], preferred_element_type=F32)
        pre_scr[:, g * gs:(g + 1) * gs] = pg[:, :gs]
        pre_scr[:, W + g * gs:W + (g + 1) * gs] = pg[:, gs:]


def _f32(ref, rows, lanes):
    return ref[rows, lanes].astype(F32)


def _sub_loop(tl, sub_r, W, fn):
    def chunk(ci, carry):
        r0 = pl.multiple_of(ci * sub_r, sub_r)
        for lb in range(W // LANES):
            fn(r0, lb * LANES)
        return carry

    lax.fori_loop(0, tl // sub_r, chunk, 0)


def _row_loop(tl, rev, step, init):
    nchunk = tl // SUBLANES

    def chunk(j, carry):
        jj = (nchunk - 1 - j) if rev else j
        c0 = pl.multiple_of(jj * SUBLANES, SUBLANES)
        for r in (range(SUBLANES - 1, -1, -1) if rev else range(SUBLANES)):
            carry = step(c0 + r, carry)
        return carry

    return lax.fori_loop(0, nchunk, chunk, init)


def _mix_fwd(P, d, h_init, wts, *, rows, row_off, row_w, tl, saved0=None, name):
    W = P.shape[1] // 6
    nt = rows // tl
    ob = row_off // tl
    rev = d == 1
    gs = min(LRU_GROUP, W)
    ng = W // gs
    wca, wcb, bcb = wts["wca"], wts["wcb"], wts["bcb"]
    wd, ba, bx, lam = wts["wd"][d], wts["ba"][d], wts["bx"][d], wts["lam"][d]

    def tile(i):
        return (nt - 1 - i) if rev else i

    def pcol(j):
        return pl.BlockSpec((tl, W), lambda i: (tile(i) + ob, j))

    vec = pl.BlockSpec((1, W), lambda i: (0, 0))
    taps = pl.BlockSpec((SUBLANES, W), lambda i: (0, 0))
    wd_spec = pl.BlockSpec(wd.shape, lambda i: (0, 0, 0))
    seq = pl.BlockSpec((tl, W), lambda i: (tile(i), 0))

    sub_r = min(row_w, tl)
    assert tl % sub_r == 0

    def body(*refs):
        if rev:
            (bl, cl, ul, gl, ql, ho, xb_r, xb16_r, wca_r, wd_r, ba_r, bx_r, lam_r, hin, hseq, cat, a_o, r_o, ig_o, m2_o,
             b_scr, pre_scr, carry, sp_scr) = refs
        else:
            (vl, wcb_r, bcb_r, wd_r, ba_r, bx_r, lam_r, hin, hseq, xb_r, xb16_r, a_o, r_o, ig_o, m2_o,
             b_scr, pre_scr, carry, sp_scr) = refs
        i = pl.program_id(0)

        @pl.when(i == 0)
        def _():
            carry[...] = hin[...]

        sp_scr[...] = _softplus(-lam_r[...])
        tmod = _tmod(sub_r, sub_r)

        def conv_in(r0, l0):
            rs, ls = pl.ds(r0, sub_r), pl.ds(l0, LANES)
            xb = _conv(_f32(vl, rs, ls), wcb_r, 4, 2, tmod, sub_r, ls) + bcb_r[:, ls]
            xb_r[rs, ls] = xb
            xb16_r[rs, ls] = xb.astype(BF16)

        def gates(r0, l0):
            rs, ls = pl.ds(r0, sub_r), pl.ds(l0, LANES)
            r, ig, a, m2 = _gates(pre_scr[rs, ls] + ba_r[:, ls], pre_scr[rs, pl.ds(W + l0, LANES)] + bx_r[:, ls],
                                  sp_scr[:, ls])
            a_o[rs, ls] = a
            r_o[rs, ls] = r.astype(r_o.dtype)
            ig_o[rs, ls] = ig.astype(ig_o.dtype)
            m2_o[rs, ls] = m2.astype(m2_o.dtype)
            m = jnp.where(m2 > 0.0, m2 * lax.rsqrt(m2), 0.0)
            b_scr[rs, ls] = m * (ig * xb_r[rs, ls])

        if not rev:
            _sub_loop(tl, sub_r, W, conv_in)
        _gate_matmul(xb16_r, wd_r, pre_scr, W, ng, gs)
        _sub_loop(tl, sub_r, W, gates)

        def step(t, h):
            h = a_o[pl.ds(t, 1), :] * h + b_scr[pl.ds(t, 1), :]
            hseq[pl.ds(t, 1), :] = h
            return h

        carry[...] = _row_loop(tl, rev, step, carry[...])

        if rev:
            def mix_out(r0, l0):
                rs, ls = pl.ds(r0, sub_r), pl.ds(l0, LANES)
                yb = (ho[rs, ls] + hseq[rs, ls]) * _silu(_f32(ql, rs, ls))
                ya = (_f32(bl, rs, ls) * _conv(_f32(cl, rs, ls) * _f32(ul, rs, ls), wca_r, 3, 1, tmod, sub_r, ls)
                      * _silu(_f32(gl, rs, ls)))
                cat[rs, ls] = ya.astype(cat.dtype)
                cat[rs, pl.ds(W + l0, LANES)] = yb.astype(cat.dtype)

            _sub_loop(tl, sub_r, W, mix_out)

    scratch = [pltpu.VMEM((tl, W), F32), pltpu.VMEM((tl, 2 * W), F32), pltpu.VMEM((1, W), F32), pltpu.VMEM((1, W), F32)]
    f32_seq = jax.ShapeDtypeStruct((rows, W), F32)
    kept_gates = [f32_seq] + [jax.ShapeDtypeStruct((rows, W), BF16)] * 3
    if rev:
        in_specs = [pcol(j) for j in (0, 1, 2, 3, 5)] + [seq, seq, seq, taps, wd_spec, vec, vec, vec, vec]
        args = [P] * 5 + [saved0["h"], saved0["xb"], saved0["xb16"], wca, wd, ba, bx, lam, h_init]
        out_specs = [seq, pl.BlockSpec((tl, 2 * W), lambda i: (tile(i), 0))] + [seq] * 4
        out_shape = [f32_seq, jax.ShapeDtypeStruct((rows, 2 * W), BF16)] + kept_gates
    else:
        in_specs = [pcol(4), taps, vec, wd_spec, vec, vec, vec, vec]
        args = [P, wcb, bcb, wd, ba, bx, lam, h_init]
        out_specs = [seq] * 7
        out_shape = [f32_seq, f32_seq, jax.ShapeDtypeStruct((rows, W), BF16)] + kept_gates
    res = pl.pallas_call(
        body, name=name, grid=(nt,), in_specs=in_specs, out_specs=out_specs, out_shape=out_shape,
        scratch_shapes=scratch, compiler_params=_params(("arbitrary",)),
    )(*args)
    gates = dict(zip(("a", "r", "ig", "m2"), res[-4:]))
    if rev:
        return res[0], res[1], gates
    return dict(h=res[0], xb=res[1], xb16=res[2]), gates


_BWD_SCRATCH = ("dyl", "g", "dp16", "sp", "dlf", "edge", "c")
_FWD_SAVED = ("xb", "xb16", "a", "r", "ig", "m2")


def _bwd_scratch(tl, W):
    shapes = {"dyl": pltpu.VMEM((tl, W), F32), "g": pltpu.VMEM((tl, W), F32), "dp16": pltpu.VMEM((tl, 2 * W), BF16),
              "sp": pltpu.VMEM((1, W), F32), "dlf": pltpu.VMEM((1, W), F32), "edge": pltpu.VMEM((1, W), F32),
              "c": pltpu.VMEM((1, W), F32)}
    return [shapes[n] for n in _BWD_SCRATCH]


def _lru_bwd_tile(d, dy_fn, hs_ref, wd_r, lam_r, scr, acc, first, tl, sub_r, ng, gs):
    dwd_ref, dba_ref, dbx_ref, dlam_ref = acc
    W = hs_ref.shape[1]
    assert gs % LANES == 0
    rev = d == 0
    lam = lam_r[...]
    scr["sp"][...] = _softplus(-lam)
    scr["dlf"][...] = -_sigmoid(-lam)

    @pl.when(first)
    def _():
        dwd_ref[...] = jnp.zeros_like(dwd_ref)
        dba_ref[...] = jnp.zeros_like(dba_ref)
        dbx_ref[...] = jnp.zeros_like(dbx_ref)
        dlam_ref[...] = jnp.zeros_like(dlam_ref)

    def state_grad(r0, l0):
        rs, ls = pl.ds(r0, sub_r), pl.ds(l0, LANES)
        scr["dyl"][rs, ls] = dy_fn(rs, ls)

    _sub_loop(tl, sub_r, W, state_grad)

    def step(t, c):
        g = scr["dyl"][pl.ds(t, 1), :] + c
        scr["g"][pl.ds(t, 1), :] = g
        return scr["a"][pl.ds(t, 1), :] * g

    scr["c"][...] = _row_loop(tl, rev, step, scr["c"][...])
    row = lax.broadcasted_iota(jnp.int32, (sub_r, 1), 0)

    def grads(r0, l0):
        rs, ls = pl.ds(r0, sub_r), pl.ds(l0, LANES)
        g, a, m2 = scr["g"][rs, ls], scr["a"][rs, ls], _f32(scr["m2"], rs, ls)
        r, ig, xb = _f32(scr["r"], rs, ls), _f32(scr["ig"], rs, ls), scr["xb"][rs, ls]
        h = hs_ref[rs, ls]
        if d == 0:
            e0 = pl.multiple_of(jnp.maximum(r0 - SUBLANES, 0), SUBLANES)
            edge = jnp.where(r0 == 0, scr["edge"][:, ls], hs_ref[pl.ds(e0, SUBLANES), ls][SUBLANES - 1:, :])
            hprev = jnp.where(row == 0, edge, pltpu.roll(h, 1, 0))
        else:
            e0 = pl.multiple_of(jnp.minimum(r0 + sub_r, tl - SUBLANES), SUBLANES)
            edge = jnp.where(r0 == tl - sub_r, scr["edge"][:, ls], hs_ref[pl.ds(e0, SUBLANES), ls][:1, :])
            hprev = jnp.where(row == sub_r - 1, edge, pltpu.roll(h, sub_r - 1, 0))
        rsq = lax.rsqrt(m2)
        gm = g * (m2 * rsq)
        d_la = (g * hprev) * a - (g * (ig * xb)) * ((1.0 - m2) * rsq)
        d_pr = d_la * ((-LRU_C) * scr["sp"][:, ls]) * (r * (1.0 - r))
        d_pi = (gm * xb) * (ig * (1.0 - ig))
        scr["dyl"][rs, ls] = gm * ig
        dlam_ref[:, ls] += jnp.sum(d_la * ((-LRU_C) * r), axis=0, keepdims=True) * scr["dlf"][:, ls]
        dba_ref[:, ls] += jnp.sum(d_pr, axis=0, keepdims=True)
        dbx_ref[:, ls] += jnp.sum(d_pi, axis=0, keepdims=True)
        gi, off = divmod(l0, gs)
        scr["dp16"][rs, pl.ds(gi * 2 * gs + off, LANES)] = d_pr.astype(BF16)
        scr["dp16"][rs, pl.ds(gi * 2 * gs + gs + off, LANES)] = d_pi.astype(BF16)

    _sub_loop(tl, sub_r, W, grads)
    for gi in range(ng):
        dp = scr["dp16"][:, gi * 2 * gs:(gi + 1) * 2 * gs]
        scr["g"][:, gi * gs:(gi + 1) * gs] = lax.dot_general(dp, wd_r[gi], (((1,), (1,)), ((), ())),
                                                             preferred_element_type=F32)
        dwd_ref[gi] += lax.dot_general(scr["xb16"][:, gi * gs:(gi + 1) * gs], dp, (((0,), (0,)), ((), ())),
                                       preferred_element_type=F32)


def _edge_block(h, tl, nt, d):
    W = h.shape[1]
    per = tl // SUBLANES
    if d == 0:
        return pl.BlockSpec((SUBLANES, W), lambda i: (jnp.maximum((nt - 1 - i) * per - 1, 0), 0))
    return pl.BlockSpec((SUBLANES, W), lambda i: (jnp.minimum((i + 1) * per, nt * per - 1), 0))


def _mix_bwd0(P, dcat, saved0, gates0, h_init, c_init, wts, *, rows, row_off, row_w, tl, name):
    W = P.shape[1] // 6
    nt = rows // tl
    ob = row_off // tl
    gs = min(LRU_GROUP, W)
    ng = W // gs
    wd, lam = wts["wd"][0], wts["lam"][0]
    h0s = saved0["h"]
    kept = [saved0["xb"], saved0["xb16"]] + [gates0[n] for n in ("a", "r", "ig", "m2")]

    def tile(i):
        return nt - 1 - i

    vec = pl.BlockSpec((1, W), lambda i: (0, 0))
    wd_spec = pl.BlockSpec(wd.shape, lambda i: (0, 0, 0))
    seq = pl.BlockSpec((tl, W), lambda i: (tile(i), 0))

    sub_r = min(row_w, tl)
    assert tl % sub_r == 0

    def body(ql, dyb, hs, hedge8, xb_r, xb16_r, a_r, r_r, ig_r, m2_r, wd_r, lam_r, hin, cin,
             dxb_o, dwd_o, dba_o, dbx_o, dlam_o, cfin, *scratch):
        scr = dict(zip(_BWD_SCRATCH, scratch, strict=True))
        scr.update(zip(_FWD_SAVED, (xb_r, xb16_r, a_r, r_r, ig_r, m2_r), strict=True))
        i = pl.program_id(0)

        @pl.when(i == 0)
        def _():
            scr["c"][...] = cin[...]

        scr["edge"][...] = jnp.where(i == nt - 1, hin[...], hedge8[SUBLANES - 1:SUBLANES, :])
        _lru_bwd_tile(0, lambda rs, ls: _f32(dyb, rs, ls) * _silu(_f32(ql, rs, ls)), hs, wd_r, lam_r, scr,
                      (dwd_o, dba_o, dbx_o, dlam_o), i == 0, tl, sub_r, ng, gs)
        dxb_o[...] = scr["dyl"][...] + scr["g"][...]
        cfin[...] = scr["c"][...]

    return pl.pallas_call(
        body, name=name, grid=(nt,),
        in_specs=[pl.BlockSpec((tl, W), lambda i: (tile(i) + ob, 5)), pl.BlockSpec((tl, W), lambda i: (tile(i), 1)), seq,
                  _edge_block(h0s, tl, nt, 0)] + [seq] * 6 + [wd_spec, vec, vec, vec],
        out_specs=[seq, wd_spec, vec, vec, vec, vec],
        out_shape=[jax.ShapeDtypeStruct((rows, W), F32), jax.ShapeDtypeStruct(wd.shape, F32)]
        + [jax.ShapeDtypeStruct((1, W), F32)] * 4,
        scratch_shapes=_bwd_scratch(tl, W),
        compiler_params=_params(("arbitrary",)),
    )(P, dcat, h0s, h0s, *kept, wd, lam, h_init, c_init)


def _mix_bwd1(P, dcat, saved0, h1s, gates1, dxb0, h_init, c_init, wts, *, rows, row_off, row_w, tl, name,
              dp_rows=None, dp_off=0, dp_into=None):
    dp_rows = rows if dp_rows is None else dp_rows
    dpb = dp_off // tl
    W = P.shape[1] // 6
    nt = rows // tl
    ob = row_off // tl
    gs = min(LRU_GROUP, W)
    ng = W // gs
    wca, wcb = wts["wca"], wts["wcb"]
    wd, lam = wts["wd"][1], wts["lam"][1]
    h0s = saved0["h"]
    kept = [saved0["xb"], saved0["xb16"]] + [gates1[n] for n in ("a", "r", "ig", "m2")]

    vec = pl.BlockSpec((1, W), lambda i: (0, 0))
    taps = pl.BlockSpec((SUBLANES, W), lambda i: (0, 0))
    wd_spec = pl.BlockSpec(wd.shape, lambda i: (0, 0, 0))
    seq = pl.BlockSpec((tl, W), lambda i: (i, 0))

    sub_r = min(row_w, tl)
    assert tl % sub_r == 0

    def body(*refs):
        if dp_into is not None:
            refs = refs[1:]
        (bl, cl, ul, gl, vl, ql, dya, dyb, h0, h1, hedge8, dx0, xb_r, xb16_r, a_r, r_r, ig_r, m2_r, wca_r, wcb_r,
         wd_r, lam_r, hin, cin, dp_o, dwd_o, dba_o, dbx_o, dlam_o, dwca_o, dwcb_o, dbcb_o, cfin, *scratch) = refs
        scr = dict(zip(_BWD_SCRATCH, scratch, strict=True))
        scr.update(zip(_FWD_SAVED, (xb_r, xb16_r, a_r, r_r, ig_r, m2_r), strict=True))
        i = pl.program_id(0)

        @pl.when(i == 0)
        def _():
            scr["c"][...] = cin[...]
            dwca_o[...] = jnp.zeros_like(dwca_o)
            dwcb_o[...] = jnp.zeros_like(dwcb_o)
            dbcb_o[...] = jnp.zeros_like(dbcb_o)

        scr["edge"][...] = jnp.where(i == nt - 1, hin[...], hedge8[0:1, :])
        _lru_bwd_tile(1, lambda rs, ls: _f32(dyb, rs, ls) * _silu(_f32(ql, rs, ls)), h1, wd_r, lam_r, scr,
                      (dwd_o, dba_o, dbx_o, dlam_o), i == 0, tl, sub_r, ng, gs)
        cfin[...] = scr["c"][...]
        tmod = _tmod(sub_r, sub_r)

        def rest(r0, l0):
            rs, ls = pl.ds(r0, sub_r), pl.ds(l0, LANES)
            dxb = dx0[rs, ls] + scr["dyl"][rs, ls] + scr["g"][rs, ls]
            dv, dwb = _conv_bwd(dxb, _f32(vl, rs, ls), wcb_r, 4, 2, tmod, sub_r, ls)
            for j in range(4):
                dwcb_o[j:j + 1, ls] += dwb[j]
            dbcb_o[:, ls] += jnp.sum(dxb, axis=0, keepdims=True)
            q = _f32(ql, rs, ls)
            sq = _sigmoid(q)
            dq = _f32(dyb, rs, ls) * (h0[rs, ls] + h1[rs, ls]) * (sq * (1.0 + q * (1.0 - sq)))
            b_, c_, u_, g_ = _f32(bl, rs, ls), _f32(cl, rs, ls), _f32(ul, rs, ls), _f32(gl, rs, ls)
            z = c_ * u_
            cz = _conv(z, wca_r, 3, 1, tmod, sub_r, ls)
            sgm = _sigmoid(g_)
            sg = g_ * sgm
            da = _f32(dya, rs, ls)
            dz, dwa = _conv_bwd(da * b_ * sg, z, wca_r, 3, 1, tmod, sub_r, ls)
            for j in range(3):
                dwca_o[j:j + 1, ls] += dwa[j]
            parts = (da * cz * sg, dz * u_, dz * c_, da * b_ * cz * (sgm * (1.0 + g_ * (1.0 - sgm))), dv, dq)
            for k, val in enumerate(parts):
                dp_o[rs, pl.ds(k * W + l0, LANES)] = val.astype(dp_o.dtype)

        _sub_loop(tl, sub_r, W, rest)

    def pcol(j):
        return pl.BlockSpec((tl, W), lambda i: (i + ob, j))

    prev = [] if dp_into is None else [dp_into]
    return pl.pallas_call(
        body, name=name, grid=(nt,), input_output_aliases={} if dp_into is None else {0: 0},
        in_specs=[pl.BlockSpec(memory_space=pl.ANY)] * len(prev) + [pcol(j) for j in range(6)]
        + [pl.BlockSpec((tl, W), lambda i: (i, 0)), pl.BlockSpec((tl, W), lambda i: (i, 1)), seq, seq,
           _edge_block(h1s, tl, nt, 1), seq] + [seq] * 6 + [taps, taps, wd_spec, vec, vec, vec],
        out_specs=[pl.BlockSpec((tl, 6 * W), lambda i: (i + dpb, 0)), wd_spec, vec, vec, vec, taps, taps, vec, vec],
        out_shape=[jax.ShapeDtypeStruct((dp_rows, 6 * W), BF16), jax.ShapeDtypeStruct(wd.shape, F32)]
        + [jax.ShapeDtypeStruct((1, W), F32)] * 3
        + [jax.ShapeDtypeStruct((SUBLANES, W), F32)] * 2 + [jax.ShapeDtypeStruct((1, W), F32)] * 2,
        scratch_shapes=_bwd_scratch(tl, W),
        compiler_params=_params(("arbitrary",)),
    )(*prev, *([P] * 6), dcat, dcat, h0s, h1s, h1s, dxb0, *kept, wca, wcb, wd, lam, h_init, c_init)


def _loss_head(out, x, tgt, gt, fg, tr):
    L, D = x.shape

    def body(o_ref, x_ref, t_ref, gt_ref, fg_ref, dn_o, do_o, dfg_o, dgt_o, loss_o):
        i = pl.program_id(0)

        @pl.when(i == 0)
        def _():
            dfg_o[...] = jnp.zeros_like(dfg_o)
            dgt_o[...] = jnp.zeros_like(dgt_o)
            loss_o[...] = jnp.zeros_like(loss_o)

        o = o_ref[...].astype(F32)
        gt_v = gt_ref[...]
        fg_v = fg_ref[...]
        n = x_ref[...] + gt_v * o
        r = lax.rsqrt(jnp.mean(n * n, axis=-1, keepdims=True) + EPS)
        nr = n * r
        e = nr * fg_v - t_ref[...]
        loss_o[...] += 0.5 * jnp.sum(jnp.mean(e * e, axis=-1, keepdims=True))
        dy = e * (1.0 / D)
        dfg_o[...] += jnp.sum(dy * nr, axis=0, keepdims=True)
        qv = dy * fg_v
        dn = r * (qv - nr * jnp.mean(qv * nr, axis=-1, keepdims=True))
        dgt_o[...] += jnp.sum(dn * o, axis=0, keepdims=True)
        dn_o[...] = dn.astype(dn_o.dtype)
        do_o[...] = (dn * gt_v).astype(do_o.dtype)

    blk = pl.BlockSpec((tr, D), lambda i: (i, 0))
    vec = pl.BlockSpec((1, D), lambda i: (0, 0))
    return pl.pallas_call(
        body, name="loss_head", grid=(L // tr,), in_specs=[blk, blk, blk, vec, vec],
        out_specs=[blk, blk, vec, vec, pl.BlockSpec((SUBLANES, LANES), lambda i: (0, 0))],
        out_shape=[jax.ShapeDtypeStruct((L, D), BF16), jax.ShapeDtypeStruct((L, D), BF16),
                   jax.ShapeDtypeStruct((1, D), F32), jax.ShapeDtypeStruct((1, D), F32),
                   jax.ShapeDtypeStruct((SUBLANES, LANES), F32)],
        compiler_params=_params(("arbitrary",)),
    )(out, x, tgt, gt, fg)


def _norm_bwd(dhl, x, dn, g, sc, tr, name):
    L, D = x.shape
    with_x = dn is not None

    def body(*refs):
        if with_x:
            d_ref, x_ref, dn_ref, g_ref, sc_ref, gx_o, dsh_o, dsc_o, dg_o = refs
        else:
            d_ref, x_ref, g_ref, sc_ref, dsh_o, dsc_o, dg_o = refs
        i = pl.program_id(0)

        @pl.when(i == 0)
        def _():
            dsh_o[...] = jnp.zeros_like(dsh_o)
            dsc_o[...] = jnp.zeros_like(dsc_o)
            dg_o[...] = jnp.zeros_like(dg_o)

        d = d_ref[...].astype(F32)
        xv = x_ref[...]
        g_v = g_ref[...]
        r = lax.rsqrt(jnp.mean(xv * xv, axis=-1, keepdims=True) + EPS)
        xr = xv * r
        dsh_o[...] += jnp.sum(d, axis=0, keepdims=True)
        dsc_o[...] += jnp.sum(d * (xr * g_v), axis=0, keepdims=True)
        dxn = d * (1.0 + sc_ref[...])
        dg_o[...] += jnp.sum(dxn * xr, axis=0, keepdims=True)
        if with_x:
            qv = dxn * g_v
            gx_o[...] = r * (qv - xr * jnp.mean(qv * xr, axis=-1, keepdims=True)) + dn_ref[...].astype(F32)

    blk = pl.BlockSpec((tr, D), lambda i: (i, 0))
    vec = pl.BlockSpec((1, D), lambda i: (0, 0))
    vshape = jax.ShapeDtypeStruct((1, D), F32)
    res = pl.pallas_call(
        body, name=name, grid=(L // tr,),
        in_specs=[blk, blk] + ([blk] if with_x else []) + [vec, vec],
        out_specs=([blk] if with_x else []) + [vec, vec, vec],
        out_shape=([jax.ShapeDtypeStruct((L, D), F32)] if with_x else []) + [vshape] * 3,
        compiler_params=_params(("arbitrary",)),
    )(*([dhl, x] + ([dn] if with_x else []) + [g, sc]))
    return res if with_x else [None] + list(res)


def _pack_blockdiag(wa, wx, gs):
    H, hd, _ = wa.shape
    hp = gs // hd
    ng = H // hp
    eye = jnp.eye(hp, dtype=wa.dtype)

    def bd(w):
        return jnp.einsum("gpij,pq->gpiqj", w.reshape(ng, hp, hd, hd), eye).reshape(ng, gs, gs)

    return jnp.concatenate([bd(wa), bd(wx)], axis=-1).astype(BF16)


def _unpack_blockdiag(dwd, H, hd, gs):
    hp = gs // hd
    ng = H // hp
    eye = jnp.eye(hp, dtype=dwd.dtype)

    def diag(dm):
        return jnp.einsum("gpiqj,pq->gpij", dm.reshape(ng, hp, hd, hp, hd), eye).reshape(H, hd, hd)

    return diag(dwd[:, :, :gs]), diag(dwd[:, :, gs:])


def kernel(x, c, ctx, c_ctx, norm_g, w_ada, b_ada, w_in, w_conv_a, w_conv_b, b_conv_b, lru_wa, lru_ba, lru_wx, lru_bx, lru_lambda, w_out, final_g, loss_target, m_c_ctx, m_norm_g, m_w_ada, m_b_ada, m_w_in, m_w_conv_a, m_w_conv_b, m_b_conv_b, m_lru_wa, m_lru_ba, m_lru_wx, m_lru_bx, m_lru_lambda, m_w_out, m_final_g, v_c_ctx, v_norm_g, v_w_ada, v_b_ada, v_w_in, v_w_conv_a, v_w_conv_b, v_b_conv_b, v_lru_wa, v_lru_ba, v_lru_wx, v_lru_bx, v_lru_lambda, v_w_out, v_final_g):
    xi, yi, ci = _pos()
    me = 4 * xi + 2 * yi + ci
    q = 2 * xi + yi
    first_core = (ci == 0).astype(F32)

    L, D = x.shape[1], x.shape[2]
    T = ctx.shape[1]
    W = D // 2
    Wq = W // 4
    H, hd = lru_wa.shape[2], lru_wa.shape[3]
    gs = min(LRU_GROUP, W)
    nq = w_ada.shape[2]
    tl = min(256, T, L)
    tr = min(256, T, L)
    x2, ctx2, tgt2 = x[0], ctx[0], loss_target[0]

    def place(shard, full_cols):
        z = jnp.zeros((shard.shape[0], full_cols), F32)
        return lax.dynamic_update_slice(z, shard * first_core, (0, q * shard.shape[1]))

    c_rows = lax.dynamic_update_slice(jnp.zeros((8, D), F32), c, (me, 0))
    small_in = [c_rows, place(w_conv_a[0], W), place(w_conv_b[0], W), place(lru_ba[0], W), place(lru_bx[0], W),
                place(lru_lambda[0], W)]
    small_shapes = [a.shape for a in small_in]
    gathered = _allreduce8(_pack(small_in, 8 * SUBLANES), "gather_small")
    c_all, wca, wcb, ba_all, bx_all, lam_all = _unpack(gathered, small_shapes)

    s_rows = jnp.concatenate([c_all, c_ctx[None, :], jnp.zeros((7, D), F32)], axis=0)
    mod_part = _matmul(s_rows, w_ada[0], a_act="silu", bias=lax.dynamic_slice(b_ada, (0, q * nq), (1, nq)),
                       tm=16, tn=nq, tk=512, name="ada_fwd")
    mod_all = _allreduce8(_pack([place(mod_part, 4 * nq)], 8 * SUBLANES), "gather_mod")
    mod_all = _unpack(mod_all, [(16, 4 * nq)])[0]
    mod_l = lax.dynamic_slice(mod_all, (me, 0), (1, 3 * D))
    mod_c = mod_all[8:9]
    sh_l, sc_l, gt_l = mod_l[:, :D], mod_l[:, D:2 * D], mod_l[:, 2 * D:]
    sh_c, sc_c = mod_c[:, :D], mod_c[:, D:2 * D]

    pad_taps = lambda w: jnp.pad(w, ((0, SUBLANES - w.shape[0]), (0, 0)))
    wts = {
        "wca": pad_taps(wca), "wcb": pad_taps(wcb), "bcb": b_conv_b,
        "wd": [_pack_blockdiag(lru_wa[0, d], lru_wx[0, d], gs) for d in range(2)],
        "ba": [ba_all[d:d + 1] for d in range(2)], "bx": [bx_all[d:d + 1] for d in range(2)],
        "lam": [lam_all[d:d + 1] for d in range(2)],
    }

    hl = _norm_in(x2, ctx2, norm_g, sc_l, sh_l, sc_c, sh_c, tr)
    p_lat, win_full, wout_full = _in_proj_gather(hl, w_in[0].astype(BF16), w_out[0].astype(BF16),
                                                 jnp.reshape(q, (1,)).astype(jnp.int32), rows=L, tm=min(1024, L))
    p_ctx = _matmul(hl, win_full, a_rows=T, a_off=L, tm=T, tn=1536, tk=D, out_dtype=BF16, name="in_proj_ctx")
    zero_w = jnp.zeros((1, W), F32)
    ctx0, cgates0 = _mix_fwd(p_ctx, 0, zero_w, wts, rows=T, row_off=0, row_w=T, tl=tl, name="ctx_fwd0")
    c1s, _, cgates1 = _mix_fwd(p_ctx, 1, zero_w, wts, rows=T, row_off=0, row_w=T, tl=tl, saved0=ctx0, name="ctx_fwd1")
    h0_init, h1_init = ctx0["h"][T - 1:T], c1s[0:1]
    lat0, gates0 = _mix_fwd(p_lat, 0, h0_init, wts, rows=L, row_off=0, row_w=GRID_W, tl=tl, name="mix_fwd0")
    h1s, cat, gates1 = _mix_fwd(p_lat, 1, h1_init, wts, rows=L, row_off=0, row_w=GRID_W, tl=tl, saved0=lat0,
                                name="mix_fwd1")
    out = _matmul(cat, wout_full, tm=512, tn=D, tk=2 * W, out_dtype=BF16, name="out_proj")
    dn, dout, dfg, dgt, loss_blk = _loss_head(out, x2, tgt2, gt_l, final_g[None, :], tr)

    dcat = _matmul(dout, wout_full, tb=True, tm=512, tn=2 * W, tk=D, out_dtype=BF16, name="out_proj_bwd")
    gw_out = _matmul(cat, dout, ta=True, tm=1024, tn=D, tk=2048, out_dtype=BF16, name="w_out_grad")
    dxb0, dwd0, dba0, dbx0, dlam0, ch0 = _mix_bwd0(p_lat, dcat, lat0, gates0, h0_init, zero_w, wts, rows=L, row_off=0,
                                                   row_w=GRID_W, tl=tl, name="mix_bwd0")
    dp_lat, dwd1, dba1, dbx1, dlam1, dwca, dwcb, dbcb, ch1 = _mix_bwd1(
        p_lat, dcat, lat0, h1s, gates1, dxb0, h1_init, zero_w, wts, rows=L, row_off=0, row_w=GRID_W, tl=tl,
        name="mix_bwd1", dp_rows=L + T)
    zero_cat = jnp.zeros((T, 2 * W), BF16)
    cxb0, cwd0, cba0, cbx0, clam0, _ = _mix_bwd0(p_ctx, zero_cat, ctx0, cgates0, zero_w, ch0, wts, rows=T, row_off=0,
                                                 row_w=T, tl=tl, name="ctx_bwd0")
    dp, cwd1, cba1, cbx1, clam1, cwca, cwcb, cbcb, _ = _mix_bwd1(
        p_ctx, zero_cat, ctx0, c1s, cgates1, cxb0, zero_w, ch1, wts, rows=T, row_off=0, row_w=T, tl=tl, name="ctx_bwd1",
        dp_rows=L + T, dp_off=L, dp_into=dp_lat)

    gw_in = _matmul(hl, dp, ta=True, tm=1024, tn=1536, tk=2816, out_dtype=BF16, name="w_in_grad")
    rs_axes = [1, 0]
    pair_sums = _rs_pair_sums([gw_in, gw_out], rs_axes)
    dhl, rs_slots = _matmul(dp, win_full, tb=True, a_rows=L, tm=512, tn=D, tk=3072, out_dtype=BF16, name="in_proj_bwd",
                            side=_rs_chips_side(pair_sums))
    dhc = _matmul(dp, win_full, tb=True, a_rows=T, a_off=L, tm=T, tn=D, tk=3072, name="in_proj_bwd_ctx")
    gx, dsh_l, dsc_l, dng_l = _norm_bwd(dhl, x2, dn, norm_g, sc_l, tr, "norm_bwd")
    _, dsh_c, dsc_c, dng_c = _norm_bwd(dhc, ctx2, None, norm_g, sc_c, tr, "norm_bwd_ctx")

    g_in_shard, g_out_shard = _rs_finish(rs_slots, rs_axes)

    dwa0, dwx0 = _unpack_blockdiag(dwd0 + cwd0, H, hd, gs)
    dwa1, dwx1 = _unpack_blockdiag(dwd1 + cwd1, H, hd, gs)
    zeros_d = jnp.zeros((1, D), F32)
    dmod_l = jnp.concatenate([dsh_l, dsc_l, dgt], axis=1)
    dmod_c = jnp.concatenate([dsh_c, dsc_c, zeros_d], axis=1)
    small_g = [
        lax.dynamic_update_slice(jnp.zeros((8, 3 * D), F32), dmod_l, (me, 0)), dmod_c,
        dfg, dng_l + dng_c, (dwca + cwca)[:3], (dwcb + cwcb)[:4], dbcb + cbcb,
        jnp.stack([dwa0, dwa1]), jnp.stack([dwx0, dwx1]),
        jnp.concatenate([dba0 + cba0, dba1 + cba1], axis=0), jnp.concatenate([dbx0 + cbx0, dbx1 + cbx1], axis=0),
        jnp.concatenate([dlam0 + clam0, dlam1 + clam1], axis=0), loss_blk[0:1, 0:1],
    ]
    g_shapes = [a.shape for a in small_g]
    (g_rows, g_modc, g_fg, g_ng, g_wca, g_wcb, g_bcb, g_wa, g_wx, g_ba, g_bx, g_lam, loss_sum) = _unpack(
        _allreduce8_two_level(_pack(small_g, 8 * SUBLANES), "reduce_small"), g_shapes)

    g_mod = jnp.concatenate([g_rows, g_modc, jnp.zeros((7, 3 * D), F32)], axis=0)
    g_mod_q = lax.dynamic_slice(g_mod, (0, q * nq), (16, nq))
    g_w_ada = _matmul(s_rows, g_mod_q, ta=True, a_act="silu", tm=1024, tn=nq, tk=16, name="w_ada_grad")
    g_b_ada = jnp.sum(g_mod[:9], axis=0, keepdims=True)
    gc_part = _matmul(jnp.pad(lax.dynamic_slice(g_modc, (0, q * nq), (1, nq)), ((0, 7), (0, 0))), w_ada[0], tb=True,
                      dsilu_mul=c_ctx[None, :], tm=8, tn=D, tk=512, name="c_ctx_grad")
    g_c_ctx = _unpack(_allreduce8(_pack([gc_part[0:1] * first_core], 8 * SUBLANES), "reduce_c_ctx"), [(D,)])[0]

    def shard_cols(a, width):
        return lax.dynamic_slice(a, (0, q * width), (a.shape[0], width))

    grads = {
        "c_ctx": g_c_ctx, "norm_g": g_ng, "b_ada": g_b_ada,
        "w_conv_a": shard_cols(g_wca, Wq)[None], "w_conv_b": shard_cols(g_wcb, Wq)[None], "b_conv_b": g_bcb,
        "lru_wa": g_wa[None], "lru_ba": shard_cols(g_ba, Wq)[None], "lru_wx": g_wx[None],
        "lru_bx": shard_cols(g_bx, Wq)[None], "lru_lambda": shard_cols(g_lam, Wq)[None], "final_g": g_fg[0],
    }
    small_names = list(grads)
    given = dict(c_ctx=(c_ctx, m_c_ctx, v_c_ctx), norm_g=(norm_g, m_norm_g, v_norm_g), b_ada=(b_ada, m_b_ada, v_b_ada),
                 w_conv_a=(w_conv_a, m_w_conv_a, v_w_conv_a), w_conv_b=(w_conv_b, m_w_conv_b, v_w_conv_b),
                 b_conv_b=(b_conv_b, m_b_conv_b, v_b_conv_b), lru_wa=(lru_wa, m_lru_wa, v_lru_wa),
                 lru_ba=(lru_ba, m_lru_ba, v_lru_ba), lru_wx=(lru_wx, m_lru_wx, v_lru_wx),
                 lru_bx=(lru_bx, m_lru_bx, v_lru_bx), lru_lambda=(lru_lambda, m_lru_lambda, v_lru_lambda),
                 final_g=(final_g, m_final_g, v_final_g))
    def rows2d(a):
        return a.reshape(-1, a.shape[-1])

    grads = {n: grads[n].reshape(given[n][0].shape) for n in small_names}
    quads = [tuple(rows2d(a) for a in (given[n][0], grads[n], given[n][1], given[n][2])) for n in small_names]
    updated = _adam_many(quads, "adam_small")
    delta_s, newm_s, newv_s = ({n: u[j].reshape(given[n][0].shape) for n, u in zip(small_names, updated)} for j in range(3))

    big = {"w_ada": (w_ada, g_w_ada, m_w_ada, v_w_ada), "w_in": (w_in, g_in_shard, m_w_in, v_w_in),
           "w_out": (w_out, g_out_shard, m_w_out, v_w_out)}
    delta_b, newm_b, newv_b = {}, {}, {}
    for n, (w, g, m, v) in big.items():
        d_, m_, v_, *echo = _adam(w[0], g, m[0], v[0], "adam_" + n, echo_g=n != "w_ada")
        grads[n] = (echo[0] if echo else g)[None]
        delta_b[n], newm_b[n], newv_b[n] = d_[None], m_[None], v_[None]

    loss = loss_sum[0, 0]
    order = ["c_ctx", "norm_g", "w_ada", "b_ada", "w_in", "w_conv_a", "w_conv_b", "b_conv_b", "lru_wa", "lru_ba",
             "lru_wx", "lru_bx", "lru_lambda", "w_out", "final_g"]
    delta = {**delta_s, **delta_b}
    newm = {**newm_s, **newm_b}
    newv = {**newv_s, **newv_b}
    return (loss, gx[None], *[grads[n] for n in order], *[delta[n] for n in order], *[newm[n] for n in order],
            *[newv[n] for n in order])
```

```python
import functools

import jax
import jax.numpy as jnp
from jax import lax
from jax.experimental import pallas as pl
from jax.experimental.pallas import tpu as pltpu

F32 = jnp.float32
BF16 = jnp.bfloat16
MESH_ID = pl.DeviceIdType.MESH

EPS = 1e-6
LRU_C = 8.0
GRID_W = 64
ADAM_LR = 0.001
ADAM_B1 = 0.9
ADAM_B2 = 0.999
ADAM_EPS = 1e-08
ADAM_WD = 0.01
ADAM_STEP = 10

LANES = 128
SUBLANES = 8
PACK_COLS = 1024
VMEM_LIMIT = 56 * 2**20
LRU_GROUP = 256


def _params(sem=None):
    return pltpu.CompilerParams(vmem_limit_bytes=VMEM_LIMIT, dimension_semantics=sem)


def _pick(dim, pref, quantum=LANES):
    if dim <= pref:
        return dim
    best = None
    for t in range(quantum, pref + 1, quantum):
        if dim % t == 0:
            best = t
    assert best is not None, (dim, pref)
    return best


def _pos():
    return lax.axis_index("x"), lax.axis_index("y"), lax.axis_index("c")


def _flip(v, bit):
    return 1 - v if bit else v


def _sigmoid(v):
    return 0.5 * jnp.tanh(0.5 * v) + 0.5


def _silu(v):
    return v * _sigmoid(v)


def _dsilu(v):
    s = _sigmoid(v)
    return s * (1.0 + v * (1.0 - s))


def _gates(pre_r, pre_i, sp):
    r = _sigmoid(pre_r)
    ig = _sigmoid(pre_i)
    e = LRU_C * r * sp
    w = jnp.tanh(e)
    return r, ig, jnp.exp(-e), (2.0 * w) * pl.reciprocal(1.0 + w, approx=True)


def _softplus(z):
    return jnp.maximum(z, 0.0) + jnp.log1p(jnp.exp(-jnp.abs(z)))


def _matmul(a, b, *, ta=False, tb=False, tm=512, tn=512, tk=512, out_dtype=F32, name,
            a_rows=None, a_off=0, a_act=None, bias=None, dsilu_mul=None, side=None):
    rows_a = a.shape[0] if a_rows is None else a_rows
    if ta:
        K, M = rows_a, a.shape[1]
    else:
        M, K = rows_a, a.shape[1]
    N = b.shape[0] if tb else b.shape[1]
    tm, tn, tk = _pick(M, tm, SUBLANES), _pick(N, tn), _pick(K, tk)
    t_rows = tk if ta else tm
    assert a_off % t_rows == 0
    nk = K // tk
    gi, gj = M // tm, N // tn
    off_blocks = a_off // t_rows
    dims = (((0 if ta else 1,), (1 if tb else 0,)), ((), ()))
    extras = [e for e in (bias, dsilu_mul) if e is not None]
    n_sin = len(side["ins"]) if side else 0
    n_sout = len(side["outs"]) if side else 0

    def body(a_ref, b_ref, *rest):
        rest = list(rest)
        bias_ref = rest.pop(0) if bias is not None else None
        dsm_ref = rest.pop(0) if dsilu_mul is not None else None
        side_in = [rest.pop(0) for _ in range(n_sin)]
        o_ref = rest.pop(0)
        side_out = [rest.pop(0) for _ in range(n_sout)]
        acc_ref = rest.pop(0) if nk > 1 else None
        side_scr = rest
        i, j, k = pl.program_id(0), pl.program_id(1), pl.program_id(2)

        if side:
            @pl.when((i == 0) & (j == 0) & (k == 0))
            def _():
                side["start"](side_in, side_out, side_scr)

        av = a_ref[...]
        if a_act == "silu":
            av = _silu(av)
        prod = lax.dot_general(av, b_ref[...], dims, preferred_element_type=F32)

        def finish(r):
            if bias_ref is not None:
                r = r + bias_ref[...]
            if dsm_ref is not None:
                r = r * _dsilu(dsm_ref[...])
            o_ref[...] = r.astype(o_ref.dtype)

        if nk == 1:
            finish(prod)
        else:
            @pl.when(k == 0)
            def _():
                acc_ref[...] = prod

            @pl.when(k > 0)
            def _():
                acc_ref[...] += prod

            @pl.when(k == nk - 1)
            def _():
                finish(acc_ref[...])

        if side:
            @pl.when((i == gi - 1) & (j == gj - 1) & (k == nk - 1))
            def _():
                side["finish"](side_in, side_out, side_scr)

    if ta:
        a_spec = pl.BlockSpec((tk, tm), lambda i, j, k: (k + off_blocks, i))
    else:
        a_spec = pl.BlockSpec((tm, tk), lambda i, j, k: (i + off_blocks, k))
    if tb:
        b_spec = pl.BlockSpec((tn, tk), lambda i, j, k: (j, k))
    else:
        b_spec = pl.BlockSpec((tk, tn), lambda i, j, k: (k, j))
    in_specs = [a_spec, b_spec]
    if bias is not None:
        in_specs.append(pl.BlockSpec((1, tn), lambda i, j, k: (0, j)))
    if dsilu_mul is not None:
        in_specs.append(pl.BlockSpec((1, tn), lambda i, j, k: (0, j)))
    hbm = pl.BlockSpec(memory_space=pl.ANY)
    res = pl.pallas_call(
        body, name=name, grid=(gi, gj, nk),
        in_specs=in_specs + [hbm] * n_sin,
        out_specs=[pl.BlockSpec((tm, tn), lambda i, j, k: (i, j))] + [hbm] * n_sout,
        out_shape=[jax.ShapeDtypeStruct((M, N), out_dtype)] + (list(side["outs"]) if side else []),
        scratch_shapes=([pltpu.VMEM((tm, tn), F32)] if nk > 1 else []) + (list(side["scratch"]) if side else []),
        compiler_params=_params(("arbitrary",) * 3 if side else ("parallel", "parallel", "arbitrary")),
    )(a, b, *extras, *(side["ins"] if side else []))
    return (res[0], res[1:]) if side else res[0]


def _elementwise(fn, ins, outs, *, rows, cols, name, tr=256):
    tr = _pick(rows, tr, 2 * SUBLANES)
    n_in = len(ins)

    def body(*refs):
        vals = fn(*[r[...] for r in refs[:n_in]])
        if not isinstance(vals, (tuple, list)):
            vals = (vals,)
        for r, v in zip(refs[n_in:], vals, strict=True):
            r[...] = v.astype(r.dtype)

    def spec(off):
        assert off % tr == 0
        ob = off // tr
        return pl.BlockSpec((tr, cols), lambda i: (i + ob, 0))

    res = pl.pallas_call(
        body, name=name, grid=(rows // tr,),
        in_specs=[spec(off) for _, off in ins],
        out_specs=[spec(0) for _ in outs],
        out_shape=[jax.ShapeDtypeStruct((rows, cols), dt) for dt in outs],
        compiler_params=_params(("parallel",)),
    )(*[a for a, _ in ins])
    return res


def _adam_math(w, g, m, v):
    m = ADAM_B1 * m + (1.0 - ADAM_B1) * g
    v = ADAM_B2 * v + (1.0 - ADAM_B2) * (g * g)
    m_hat = m / (1.0 - ADAM_B1 ** ADAM_STEP)
    v_hat = v / (1.0 - ADAM_B2 ** ADAM_STEP)
    delta = -ADAM_LR * (m_hat / (jnp.sqrt(v_hat) + ADAM_EPS) + ADAM_WD * w)
    return delta, m, v


def _adam(w, g, m, v, name, echo_g=False):
    rows, cols = w.shape
    fn = (lambda w_, g_, m_, v_: _adam_math(w_, g_, m_, v_) + (g_,)) if echo_g else _adam_math
    return _elementwise(fn, [(w, 0), (g, 0), (m, 0), (v, 0)], [F32] * (4 if echo_g else 3),
                        rows=rows, cols=cols, name=name)


def _adam_many(quads, name):
    n = len(quads)

    def body(*refs):
        ins, outs = refs[:4 * n], refs[4 * n:]
        for t in range(n):
            w, g, m, v = (r[...] for r in ins[4 * t:4 * t + 4])
            for o_ref, val in zip(outs[3 * t:3 * t + 3], _adam_math(w, g, m, v), strict=True):
                o_ref[...] = val

    res = pl.pallas_call(
        body, name=name,
        out_shape=[jax.ShapeDtypeStruct(q[0].shape, F32) for q in quads for _ in range(3)],
        compiler_params=_params(),
    )(*[a for q in quads for a in q])
    return [res[3 * t:3 * t + 3] for t in range(n)]


def _pack(arrs, row_quantum):
    flat = jnp.concatenate([a.reshape(-1).astype(F32) for a in arrs])
    n = flat.shape[0]
    q = row_quantum * PACK_COLS
    total = -(-n // q) * q
    flat = jnp.pad(flat, (0, total - n))
    return flat.reshape(total // PACK_COLS, PACK_COLS)


def _unpack(buf, shapes):
    flat = buf.reshape(-1)
    out, off = [], 0
    for s in shapes:
        n = 1
        for d in s:
            n *= d
        out.append(flat[off:off + n].reshape(s))
        off += n
    return out


def _allreduce8(buf, name):
    R, C = buf.shape
    assert R % (8 * SUBLANES) == 0
    m = R // 8

    def body(x_ref, o_ref, recv, red, s1, r1, s2, r2):
        x, y, c = _pos()
        me = 4 * x + 2 * y + c

        def peer(k):
            px, py, pc = _flip(x, (k >> 2) & 1), _flip(y, (k >> 1) & 1), _flip(c, k & 1)
            return (px, py, pc), 4 * px + 2 * py + pc

        def rows(ref, idx):
            return ref.at[pl.ds(pl.multiple_of(idx * m, SUBLANES), m), :]

        def scatter(k):
            dev, p = peer(k)
            return pltpu.make_async_remote_copy(src_ref=rows(x_ref, p), dst_ref=recv.at[k], send_sem=s1.at[k],
                                                recv_sem=r1.at[k], device_id=dev, device_id_type=MESH_ID)

        def share(k):
            dev, p = peer(k)
            return pltpu.make_async_remote_copy(src_ref=red, dst_ref=rows(o_ref, me), send_sem=s2.at[k],
                                                recv_sem=r2.at[k], device_id=dev, device_id_type=MESH_ID)

        def shared_from(k):
            dev, p = peer(k)
            return pltpu.make_async_remote_copy(src_ref=red, dst_ref=rows(o_ref, p), send_sem=s2.at[k],
                                                recv_sem=r2.at[k], device_id=dev, device_id_type=MESH_ID)

        for k in range(1, 8):
            scatter(k).start()
        acc = rows(x_ref, me)[...]
        for k in range(1, 8):
            scatter(k).wait_recv()
            acc = acc + recv[k]
        red[...] = acc
        rows(o_ref, me)[...] = acc
        for k in range(1, 8):
            share(k).start()
        for k in range(1, 8):
            shared_from(k).wait_recv()
        for k in range(1, 8):
            scatter(k).wait_send()
            share(k).wait_send()

    return pl.pallas_call(
        body, name=name,
        in_specs=[pl.BlockSpec(memory_space=pltpu.VMEM)],
        out_specs=pl.BlockSpec(memory_space=pltpu.VMEM),
        out_shape=jax.ShapeDtypeStruct((R, C), F32),
        scratch_shapes=[pltpu.VMEM((8, m, C), F32), pltpu.VMEM((m, C), F32),
                        pltpu.SemaphoreType.DMA((8,)), pltpu.SemaphoreType.DMA((8,)),
                        pltpu.SemaphoreType.DMA((8,)), pltpu.SemaphoreType.DMA((8,))],
        compiler_params=_params(),
    )(buf)


def _allreduce8_two_level(buf, name):
    R, C = buf.shape
    assert R % (8 * SUBLANES) == 0
    m, hr = R // 8, R // 2

    def body(x_ref, o_ref, got0, half, got1, red, ssem, rsem):
        x, y, c = _pos()
        q = 2 * x + y
        sibling = (x, y, 1 - c)

        def half_rows(ref, core):
            return ref.at[pl.ds(pl.multiple_of(core * hr, SUBLANES), hr), :]

        def chunk(ref, core, chip):
            return ref.at[pl.ds(pl.multiple_of(core * hr + chip * m, SUBLANES), m), :]

        def chip_of(k):
            px, py = _flip(x, (k >> 1) & 1), _flip(y, k & 1)
            return (px, py, c), 2 * px + py

        def copy(src, dst, phase, k, dev):
            return pltpu.make_async_remote_copy(src_ref=src, dst_ref=dst, send_sem=ssem.at[phase, k],
                                                recv_sem=rsem.at[phase, k], device_id=dev, device_id_type=MESH_ID)

        swap = copy(half_rows(x_ref, 1 - c), got0, 0, 0, sibling)
        swap.start()
        swap.wait()
        half[...] = half_rows(x_ref, c)[...] + got0[...]

        def scatter(k):
            dev, p = chip_of(k)
            return copy(half.at[pl.ds(pl.multiple_of(p * m, SUBLANES), m), :], got1.at[k], 1, k, dev)

        for k in range(1, 4):
            scatter(k).start()
        acc = half[pl.ds(pl.multiple_of(q * m, SUBLANES), m), :]
        for k in range(1, 4):
            scatter(k).wait_recv()
            acc = acc + got1[k]
        red[...] = acc
        chunk(o_ref, c, q)[...] = acc

        def share(k, landing_chip):
            return copy(red, chunk(o_ref, c, landing_chip), 2, k, chip_of(k)[0])

        for k in range(1, 4):
            share(k, q).start()
        for k in range(1, 4):
            share(k, chip_of(k)[1]).wait_recv()
        back = copy(half_rows(o_ref, c), half_rows(o_ref, c), 3, 0, sibling)
        back.start()
        copy(half_rows(o_ref, 1 - c), half_rows(o_ref, 1 - c), 3, 0, sibling).wait_recv()
        back.wait_send()
        for k in range(1, 4):
            scatter(k).wait_send()
            share(k, q).wait_send()

    return pl.pallas_call(
        body, name=name,
        in_specs=[pl.BlockSpec(memory_space=pltpu.VMEM)],
        out_specs=pl.BlockSpec(memory_space=pltpu.VMEM),
        out_shape=jax.ShapeDtypeStruct((R, C), F32),
        scratch_shapes=[pltpu.VMEM((hr, C), F32), pltpu.VMEM((hr, C), F32), pltpu.VMEM((4, m, C), F32), pltpu.VMEM((m, C), F32),
                        pltpu.SemaphoreType.DMA((4, 4)), pltpu.SemaphoreType.DMA((4, 4))],
        compiler_params=_params(),
    )(buf)


def _bounce(src, dst, buf, sem):
    cin = pltpu.make_async_copy(src, buf, sem)
    cin.start()
    cin.wait()
    cout = pltpu.make_async_copy(buf, dst, sem)
    cout.start()
    cout.wait()


def _chunk(ref, axis, idx, size):
    start = idx * size
    if axis == 0:
        return ref.at[pl.ds(start, size), :]
    return ref.at[:, pl.ds(start, size)]


def _in_proj_gather(hl, win, wout, q_arr, *, rows, tm):
    D, nq = win.shape
    dq, D2 = wout.shape
    ni = rows // tm
    ops = ((0, 1, nq, D // 2), (1, 0, dq, dq // 2))

    def body(q_ref, a_ref, win_ref, wout_ref, p_ref, gin_ref, gout_ref, b_scr, buf_out, lsem, ssem, rsem, fsem, gsem):
        j, i = pl.program_id(0), pl.program_id(1)
        x, y, c = _pos()
        q = 2 * x + y
        srcs = (win_ref, wout_ref)
        dsts = (gin_ref, gout_ref)

        def shard_window(o, chip):
            _, axis, size, _ = ops[o]
            return _chunk(dsts[o], axis, chip, size)

        def half(ref, o, core):
            return ref.at[pl.ds(core * ops[o][3], ops[o][3]), :]

        def half_window(o, chip, core):
            _, axis, size, hs = ops[o]
            if axis == 1:
                return dsts[o].at[pl.ds(core * hs, hs), pl.ds(chip * size, size)]
            return dsts[o].at[pl.ds(chip * size + core * hs, hs), :]

        def chip_of(k):
            px, py = _flip(x, (k >> 1) & 1), _flip(y, k & 1)
            return px, py, 2 * px + py

        def send(o, k):
            px, py, _ = chip_of(k)
            return pltpu.make_async_remote_copy(
                src_ref=half(srcs[o], o, c), dst_ref=half_window(o, q, c), send_sem=ssem.at[o, k],
                recv_sem=rsem.at[o, k], device_id=(px, py, c), device_id_type=MESH_ID)

        def chip_recv(o, k):
            px, py, pq = chip_of(k)
            landed = half_window(o, pq, c)
            pltpu.make_async_remote_copy(src_ref=landed, dst_ref=landed, send_sem=ssem.at[o, k], recv_sem=rsem.at[o, k],
                                         device_id=(px, py, c), device_id_type=MESH_ID).wait_recv()

        def to_sibling(o, k):
            landed = half_window(o, chip_of(k)[2], c)
            return pltpu.make_async_remote_copy(src_ref=landed, dst_ref=landed, send_sem=fsem.at[o, k],
                                                recv_sem=gsem.at[o, k], device_id=(x, y, 1 - c), device_id_type=MESH_ID)

        def from_sibling(o, k):
            theirs = half_window(o, chip_of(k)[2], 1 - c)
            pltpu.make_async_remote_copy(src_ref=theirs, dst_ref=theirs, send_sem=fsem.at[o, k], recv_sem=gsem.at[o, k],
                                         device_id=(x, y, 1 - c), device_id_type=MESH_ID).wait_recv()

        def relay(o, core):
            if core == 0:
                landed, target = half_window(o, chip_of(2)[2], 0), (x, 1 - y, 0)
            else:
                landed, target = half_window(o, chip_of(1)[2], 1), (1 - x, y, 1)
            return pltpu.make_async_remote_copy(src_ref=landed, dst_ref=landed, send_sem=ssem.at[o, 3],
                                                recv_sem=rsem.at[o, 3], device_id=target, device_id_type=MESH_ID)

        def on_core(core, fn):
            @pl.when(c == core)
            def _():
                fn()

        def land(o, k):
            chip_recv(o, k)
            if k == 2:
                on_core(0, lambda: relay(o, 0).start())
            if k == 1:
                on_core(1, lambda: relay(o, 1).start())
            to_sibling(o, k).start()

        def settle(o, k):
            from_sibling(o, k)
            to_sibling(o, k).wait_send()

        def b_load(k, slot):
            src = win_ref if k == 0 else shard_window(0, chip_of(k)[2])
            return pltpu.make_async_copy(src, b_scr.at[slot], lsem.at[0])

        def own_store():
            return pltpu.make_async_copy(b_scr.at[0], shard_window(0, q), lsem.at[2])

        order = (0, 2, 1, 3)
        early = max(ni - 2, 0)

        @pl.when((j == 0) & (i == 0))
        def _():
            for o in range(2):
                for k in (2, 1):
                    send(o, k).start()
            first = b_load(0, 0)
            first.start()
            first.wait()
            own_store().start()
            _bounce(wout_ref, shard_window(1, q), buf_out, lsem.at[1])

        for jj in range(3):
            nxt = order[jj + 1]

            @pl.when((j == jj) & (i == early))
            def _(nxt=nxt):
                land(0, nxt)

            @pl.when((j == jj) & (i == ni - 1))
            def _(jj=jj, nxt=nxt):
                settle(0, nxt)
                if jj == 1:
                    own_store().wait()
                b_load(nxt, (jj + 1) % 2).start()

            @pl.when((j == jj + 1) & (i == 0))
            def _(jj=jj, nxt=nxt):
                b_load(nxt, (jj + 1) % 2).wait()

        @pl.when((j == 3) & (i == 0))
        def _():
            land(1, 2)
            land(1, 1)

        p_ref[...] = jnp.dot(a_ref[...], b_scr[j % 2], preferred_element_type=F32).astype(p_ref.dtype)

        @pl.when((j == 3) & (i == ni - 1))
        def _():
            settle(1, 2)
            settle(1, 1)
            land(1, 3)
            settle(1, 3)
            for o in range(2):
                for k in (2, 1):
                    send(o, k).wait_send()
                for core in range(2):
                    on_core(core, lambda o=o, core=core: relay(o, core).wait_send())

    hbm = pl.BlockSpec(memory_space=pl.ANY)
    grid_spec = pltpu.PrefetchScalarGridSpec(
        num_scalar_prefetch=1, grid=(4, ni),
        in_specs=[pl.BlockSpec((tm, D), lambda j, i, qr: (i, 0)), hbm, hbm],
        out_specs=[pl.BlockSpec((tm, nq), lambda j, i, qr: (i, jnp.bitwise_xor(qr[0], ((j & 1) << 1) | (j >> 1)))),
                   hbm, hbm],
        scratch_shapes=[pltpu.VMEM((2,) + win.shape, win.dtype), pltpu.VMEM(wout.shape, wout.dtype), pltpu.SemaphoreType.DMA((3,))]
        + [pltpu.SemaphoreType.DMA((2, 4)) for _ in range(4)])
    return pl.pallas_call(
        body, name="in_proj_gather", grid_spec=grid_spec,
        out_shape=[jax.ShapeDtypeStruct((rows, 4 * nq), BF16), jax.ShapeDtypeStruct((D, 4 * nq), win.dtype),
                   jax.ShapeDtypeStruct((4 * dq, D2), wout.dtype)],
        compiler_params=_params(("arbitrary", "arbitrary")),
    )(q_arr, hl, win, wout)


def _rs_to_sibling(gs, axes):
    n = len(gs)
    shapes = []
    for g, ax in zip(gs, axes):
        s = list(g.shape)
        s[ax] //= 8
        shapes.append(tuple(s))

    def body(*refs):
        g_refs, mine, landed = refs[:n], refs[n:2 * n], refs[2 * n:3 * n]
        bufs = refs[3 * n:4 * n]
        lsem, ssem, rsem = refs[4 * n:]
        x, y, c = _pos()
        cps = []
        for o in range(n):
            size = shapes[o][axes[o]]
            for j in range(4):
                rc = pltpu.make_async_remote_copy(
                    src_ref=_chunk(g_refs[o], axes[o], 2 * j + 1 - c, size), dst_ref=landed[o].at[j],
                    send_sem=ssem.at[o, j], recv_sem=rsem.at[o, j], device_id=(x, y, 1 - c), device_id_type=MESH_ID)
                rc.start()
                cps.append(rc)
        for o in range(n):
            size = shapes[o][axes[o]]
            for j in range(4):
                _bounce(_chunk(g_refs[o], axes[o], 2 * j + c, size), mine[o].at[j], bufs[o], lsem.at[o])
        for rc in cps:
            rc.wait()

    hbm = pl.BlockSpec(memory_space=pl.ANY)
    outs = [jax.ShapeDtypeStruct((4,) + s, g.dtype) for s, g in zip(shapes, gs)]
    res = pl.pallas_call(
        body, name="rs_to_sibling", in_specs=[hbm] * n, out_specs=[hbm] * (2 * n), out_shape=outs + outs,
        scratch_shapes=[pltpu.VMEM(s, g.dtype) for s, g in zip(shapes, gs)]
        + [pltpu.SemaphoreType.DMA((n,)), pltpu.SemaphoreType.DMA((n, 4)), pltpu.SemaphoreType.DMA((n, 4))],
        compiler_params=_params(),
    )(*gs)
    return res[:n], res[n:]


def _gather_block_side(block):
    r, n_cols = block.shape

    def copies(ins, outs, scr):
        ssem, rsem = scr[1], scr[2]
        x, y, c = _pos()
        return [pltpu.make_async_remote_copy(
            src_ref=ins[0], dst_ref=outs[0].at[k], send_sem=ssem.at[k], recv_sem=rsem.at[k],
            device_id=(_flip(x, (k >> 2) & 1), _flip(y, (k >> 1) & 1), _flip(c, k & 1)), device_id_type=MESH_ID)
            for k in range(1, 8)]

    def start(ins, outs, scr):
        for cp in copies(ins, outs, scr):
            cp.start()

    def finish(ins, outs, scr):
        _bounce(ins[0], outs[0].at[0], scr[0], scr[3].at[0])
        for cp in copies(ins, outs, scr):
            cp.wait()

    return dict(ins=[block], outs=[jax.ShapeDtypeStruct((8, r, n_cols), block.dtype)],
                scratch=[pltpu.VMEM((r, n_cols), block.dtype), pltpu.SemaphoreType.DMA((8,)), pltpu.SemaphoreType.DMA((8,)),
                         pltpu.SemaphoreType.DMA((1,))],
                start=start, finish=finish)


def _join_sides(a, b):
    na_i, na_o, na_s = len(a["ins"]), len(a["outs"]), len(a["scratch"])

    def run(which):
        def fn(ins, outs, scr):
            a[which](ins[:na_i], outs[:na_o], scr[:na_s])
            b[which](ins[na_i:], outs[na_o:], scr[na_s:])
        return fn

    return dict(ins=a["ins"] + b["ins"], outs=a["outs"] + b["outs"], scratch=a["scratch"] + b["scratch"],
                start=run("start"), finish=run("finish"))


def _rs_chips_side(parts):
    n = len(parts)

    def copies(p_refs, slots, scr):
        ssem, rsem = scr[n + 1], scr[n + 2]
        x, y, c = _pos()
        cps = []
        for o in range(n):
            for k in range(1, 4):
                px, py = _flip(x, (k >> 1) & 1), _flip(y, k & 1)
                cps.append(pltpu.make_async_remote_copy(
                    src_ref=p_refs[o].at[2 * px + py], dst_ref=slots[o].at[k], send_sem=ssem.at[o, k],
                    recv_sem=rsem.at[o, k], device_id=(px, py, c), device_id_type=MESH_ID))
        return cps

    def start(p_refs, slots, scr):
        for cp in copies(p_refs, slots, scr):
            cp.start()

    def finish(p_refs, slots, scr):
        x, y, _ = _pos()
        q = 2 * x + y
        for o in range(n):
            _bounce(p_refs[o].at[q], slots[o].at[0], scr[o], scr[n].at[o])
        for cp in copies(p_refs, slots, scr):
            cp.wait()

    return dict(
        ins=list(parts), outs=[jax.ShapeDtypeStruct(p.shape, p.dtype) for p in parts],
        scratch=[pltpu.VMEM(p.shape[1:], p.dtype) for p in parts]
        + [pltpu.SemaphoreType.DMA((n,)), pltpu.SemaphoreType.DMA((n, 4)), pltpu.SemaphoreType.DMA((n, 4))],
        start=start, finish=finish)


def _rs_share(rs, axes):
    n = len(rs)
    shapes = []
    for r, ax in zip(rs, axes):
        s = list(r.shape)
        s[ax] *= 2
        shapes.append(tuple(s))

    def body(*refs):
        r_refs, outs = refs[:n], refs[n:2 * n]
        bufs = refs[2 * n:3 * n]
        lsem, ssem, rsem = refs[3 * n:]
        x, y, c = _pos()
        cps = []
        for o in range(n):
            size = r_refs[o].shape[axes[o]]
            window = _chunk(outs[o], axes[o], c, size)
            rc = pltpu.make_async_remote_copy(src_ref=r_refs[o], dst_ref=window, send_sem=ssem.at[o], recv_sem=rsem.at[o],
                                              device_id=(x, y, 1 - c), device_id_type=MESH_ID)
            rc.start()
            cps.append(rc)
        for o in range(n):
            size = r_refs[o].shape[axes[o]]
            _bounce(r_refs[o], _chunk(outs[o], axes[o], c, size), bufs[o], lsem.at[o])
        for cp in cps:
            cp.wait()

    hbm = pl.BlockSpec(memory_space=pl.ANY)
    return pl.pallas_call(
        body, name="rs_share", in_specs=[hbm] * n, out_specs=[hbm] * n,
        out_shape=[jax.ShapeDtypeStruct(s, r.dtype) for s, r in zip(shapes, rs)],
        scratch_shapes=[pltpu.VMEM(r.shape, r.dtype) for r in rs] + [pltpu.SemaphoreType.DMA((n,)) for _ in range(3)],
        compiler_params=_params(),
    )(*rs)


def _rs_pair_sums(gs, axes):
    mine, landed = _rs_to_sibling(gs, axes)
    pair_sums = []
    for o, (mi, la) in enumerate(zip(mine, landed)):
        rows, cols = mi.shape[0] * mi.shape[1], mi.shape[2]
        s = _elementwise(lambda a, b: a.astype(F32) + b.astype(F32), [(mi.reshape(rows, cols), 0), (la.reshape(rows, cols), 0)],
                         [BF16], rows=rows, cols=cols, name=f"rs_pair_sum{o}")[0]
        pair_sums.append(s.reshape(mi.shape))
    return pair_sums


def _rs_finish(slots, axes):
    reduced = []
    for o, sl in enumerate(slots):
        rows, cols = sl.shape[1], sl.shape[2]
        flat = sl.reshape(4 * rows, cols)
        r = _elementwise(lambda a, b, c, d: (a.astype(F32) + b.astype(F32)) + (c.astype(F32) + d.astype(F32)),
                         [(flat, k * rows) for k in range(4)], [F32], rows=rows, cols=cols, name=f"rs_chip_sum{o}")[0]
        reduced.append(r)
    return _rs_share(reduced, axes)


def _norm_in(x, ctx, g, sc_l, sh_l, sc_c, sh_c, tr):
    L, D = x.shape
    T = ctx.shape[0]
    nx, nc = L // tr, T // tr

    def body(x_ref, c_ref, g_ref, scl, shl, scc, shc, o_ref):
        i = pl.program_id(0)

        def run(src, sc, sh):
            v = src[...]
            r = lax.rsqrt(jnp.mean(v * v, axis=-1, keepdims=True) + EPS)
            o_ref[...] = ((v * r * g_ref[...]) * (1.0 + sc[...]) + sh[...]).astype(o_ref.dtype)

        @pl.when(i < nx)
        def _():
            run(x_ref, scl, shl)

        @pl.when(i >= nx)
        def _():
            run(c_ref, scc, shc)

    vec = pl.BlockSpec((1, D), lambda i: (0, 0))
    return pl.pallas_call(
        body, name="norm_in", grid=(nx + nc,),
        in_specs=[pl.BlockSpec((tr, D), lambda i: (jnp.minimum(i, nx - 1), 0)),
                  pl.BlockSpec((tr, D), lambda i: (jnp.maximum(i - nx, 0), 0)), vec, vec, vec, vec, vec],
        out_specs=pl.BlockSpec((tr, D), lambda i: (i, 0)),
        out_shape=jax.ShapeDtypeStruct((L + T, D), BF16),
        compiler_params=_params(("arbitrary",)),
    )(x, ctx, g, sc_l, sh_l, sc_c, sh_c)


def _tmod(tl, row_w):
    assert row_w & (row_w - 1) == 0
    return lax.broadcasted_iota(jnp.int32, (tl, 1), 0) & (row_w - 1)


def _shift(z, k, tmod, row_w):
    tl = z.shape[0]
    rolled = pltpu.roll(z, k % tl, 0)
    mask = (tmod >= k) if k > 0 else (tmod < row_w + k)
    return jnp.where(mask, rolled, 0.0)


def _conv(z, w_ref, taps, left, tmod, row_w, lanes=slice(None)):
    out = None
    for j in range(taps):
        k = left - j
        term = (z if k == 0 else _shift(z, k, tmod, row_w)) * w_ref[j:j + 1, lanes]
        out = term if out is None else out + term
    return out


def _conv_bwd(dz, z, w_ref, taps, left, tmod, row_w, lanes=slice(None)):
    din = None
    dws = []
    for j in range(taps):
        k = left - j
        shifted = dz if k == 0 else _shift(dz, -k, tmod, row_w)
        term = shifted * w_ref[j:j + 1, lanes]
        din = term if din is None else din + term
        dws.append(jnp.sum(shifted * z, axis=0, keepdims=True))
    return din, dws


def _gate_matmul(xb16_ref, wd_ref, pre_scr, W, ng, gs):
    for g in range(ng):
        pg = jnp.dot(xb16_ref[:, g * gs:(g + 1) * gs], wd_ref[g], preferred_element_type=F32)
        pre_scr[:, g * gs:(g + 1) * gs] = pg[:, :gs]
        pre_scr[:, W + g * gs:W + (g + 1) * gs] = pg[:, gs:]


def _f32(ref, rows, lanes):
    return ref[rows, lanes].astype(F32)


def _sub_loop(tl, sub_r, W, fn):
    def chunk(ci, carry):
        r0 = pl.multiple_of(ci * sub_r, sub_r)
        for lb in range(W // LANES):
            fn(r0, lb * LANES)
        return carry

    lax.fori_loop(0, tl // sub_r, chunk, 0)


def _row_loop(tl, rev, step, init):
    nchunk = tl // SUBLANES

    def chunk(j, carry):
        jj = (nchunk - 1 - j) if rev else j
        c0 = pl.multiple_of(jj * SUBLANES, SUBLANES)
        for r in (range(SUBLANES - 1, -1, -1) if rev else range(SUBLANES)):
            carry = step(c0 + r, carry)
        return carry

    return lax.fori_loop(0, nchunk, chunk, init)


def _mix_fwd(P, d, h_init, wts, *, rows, row_off, row_w, tl, saved0=None, name):
    W = P.shape[1] // 6
    nt = rows // tl
    ob = row_off // tl
    rev = d == 1
    gs = min(LRU_GROUP, W)
    ng = W // gs
    wca, wcb, bcb = wts["wca"], wts["wcb"], wts["bcb"]
    wd, ba, bx, lam = wts["wd"][d], wts["ba"][d], wts["bx"][d], wts["lam"][d]

    def tile(i):
        return (nt - 1 - i) if rev else i

    def pcol(j):
        return pl.BlockSpec((tl, W), lambda i: (tile(i) + ob, j))

    vec = pl.BlockSpec((1, W), lambda i: (0, 0))
    taps = pl.BlockSpec((SUBLANES, W), lambda i: (0, 0))
    wd_spec = pl.BlockSpec(wd.shape, lambda i: (0, 0, 0))
    seq = pl.BlockSpec((tl, W), lambda i: (tile(i), 0))

    sub_r = min(row_w, tl)
    assert tl % sub_r == 0

    def body(*refs):
        if rev:
            (bl, cl, ul, gl, ql, ho, xb_r, xb16_r, wca_r, wd_r, ba_r, bx_r, lam_r, hin, hseq, cat, a_o, r_o, ig_o, m2_o,
             b_scr, pre_scr, carry, sp_scr) = refs
        else:
            (vl, wcb_r, bcb_r, wd_r, ba_r, bx_r, lam_r, hin, hseq, xb_r, xb16_r, a_o, r_o, ig_o, m2_o,
             b_scr, pre_scr, carry, sp_scr) = refs
        i = pl.program_id(0)

        @pl.when(i == 0)
        def _():
            carry[...] = hin[...]

        sp_scr[...] = _softplus(-lam_r[...])
        tmod = _tmod(sub_r, sub_r)

        def conv_in(r0, l0):
            rs, ls = pl.ds(r0, sub_r), pl.ds(l0, LANES)
            xb = _conv(_f32(vl, rs, ls), wcb_r, 4, 2, tmod, sub_r, ls) + bcb_r[:, ls]
            xb_r[rs, ls] = xb
            xb16_r[rs, ls] = xb.astype(BF16)

        def gates(r0, l0):
            rs, ls = pl.ds(r0, sub_r), pl.ds(l0, LANES)
            r, ig, a, m2 = _gates(pre_scr[rs, ls] + ba_r[:, ls], pre_scr[rs, pl.ds(W + l0, LANES)] + bx_r[:, ls],
                                  sp_scr[:, ls])
            a_o[rs, ls] = a
            r_o[rs, ls] = r.astype(r_o.dtype)
            ig_o[rs, ls] = ig.astype(ig_o.dtype)
            m2_o[rs, ls] = m2.astype(m2_o.dtype)
            m = jnp.where(m2 > 0.0, m2 * lax.rsqrt(m2), 0.0)
            b_scr[rs, ls] = m * (ig * xb_r[rs, ls])

        if not rev:
            _sub_loop(tl, sub_r, W, conv_in)
        _gate_matmul(xb16_r, wd_r, pre_scr, W, ng, gs)
        _sub_loop(tl, sub_r, W, gates)

        def step(t, h):
            h = a_o[pl.ds(t, 1), :] * h + b_scr[pl.ds(t, 1), :]
            hseq[pl.ds(t, 1), :] = h
            return h

        carry[...] = _row_loop(tl, rev, step, carry[...])

        if rev:
            def mix_out(r0, l0):
                rs, ls = pl.ds(r0, sub_r), pl.ds(l0, LANES)
                yb = (ho[rs, ls] + hseq[rs, ls]) * _silu(_f32(ql, rs, ls))
                ya = (_f32(bl, rs, ls) * _conv(_f32(cl, rs, ls) * _f32(ul, rs, ls), wca_r, 3, 1, tmod, sub_r, ls)
                      * _silu(_f32(gl, rs, ls)))
                cat[rs, ls] = ya.astype(cat.dtype)
                cat[rs, pl.ds(W + l0, LANES)] = yb.astype(cat.dtype)

            _sub_loop(tl, sub_r, W, mix_out)

    scratch = [pltpu.VMEM((tl, W), F32), pltpu.VMEM((tl, 2 * W), F32), pltpu.VMEM((1, W), F32), pltpu.VMEM((1, W), F32)]
    f32_seq = jax.ShapeDtypeStruct((rows, W), F32)
    kept_gates = [f32_seq] + [jax.ShapeDtypeStruct((rows, W), BF16)] * 3
    if rev:
        in_specs = [pcol(j) for j in (0, 1, 2, 3, 5)] + [seq, seq, seq, taps, wd_spec, vec, vec, vec, vec]
        args = [P] * 5 + [saved0["h"], saved0["xb"], saved0["xb16"], wca, wd, ba, bx, lam, h_init]
        out_specs = [seq, pl.BlockSpec((tl, 2 * W), lambda i: (tile(i), 0))] + [seq] * 4
        out_shape = [f32_seq, jax.ShapeDtypeStruct((rows, 2 * W), BF16)] + kept_gates
    else:
        in_specs = [pcol(4), taps, vec, wd_spec, vec, vec, vec, vec]
        args = [P, wcb, bcb, wd, ba, bx, lam, h_init]
        out_specs = [seq] * 7
        out_shape = [f32_seq, f32_seq, jax.ShapeDtypeStruct((rows, W), BF16)] + kept_gates
    res = pl.pallas_call(
        body, name=name, grid=(nt,), in_specs=in_specs, out_specs=out_specs, out_shape=out_shape,
        scratch_shapes=scratch, compiler_params=_params(("arbitrary",)),
    )(*args)
    gates = dict(zip(("a", "r", "ig", "m2"), res[-4:]))
    if rev:
        return res[0], res[1], gates
    return dict(h=res[0], xb=res[1], xb16=res[2]), gates


_BWD_SCRATCH = ("dyl", "g", "dp16", "sp", "dlf", "edge", "c")
_FWD_SAVED = ("xb", "xb16", "a", "r", "ig", "m2")


def _bwd_scratch(tl, W):
    shapes = {"dyl": pltpu.VMEM((tl, W), F32), "g": pltpu.VMEM((tl, W), F32), "dp16": pltpu.VMEM((tl, 2 * W), BF16),
              "sp": pltpu.VMEM((1, W), F32), "dlf": pltpu.VMEM((1, W), F32), "edge": pltpu.VMEM((1, W), F32),
              "c": pltpu.VMEM((1, W), F32)}
    return [shapes[n] for n in _BWD_SCRATCH]


def _lru_bwd_tile(d, dy_fn, hs_ref, wd_r, lam_r, scr, acc, first, tl, sub_r, ng, gs):
    dwd_ref, dba_ref, dbx_ref, dlam_ref = acc
    W = hs_ref.shape[1]
    assert gs % LANES == 0
    rev = d == 0
    lam = lam_r[...]
    scr["sp"][...] = _softplus(-lam)
    scr["dlf"][...] = -_sigmoid(-lam)

    @pl.when(first)
    def _():
        dwd_ref[...] = jnp.zeros_like(dwd_ref)
        dba_ref[...] = jnp.zeros_like(dba_ref)
        dbx_ref[...] = jnp.zeros_like(dbx_ref)
        dlam_ref[...] = jnp.zeros_like(dlam_ref)

    def state_grad(r0, l0):
        rs, ls = pl.ds(r0, sub_r), pl.ds(l0, LANES)
        scr["dyl"][rs, ls] = dy_fn(rs, ls)

    _sub_loop(tl, sub_r, W, state_grad)

    def step(t, c):
        g = scr["dyl"][pl.ds(t, 1), :] + c
        scr["g"][pl.ds(t, 1), :] = g
        return scr["a"][pl.ds(t, 1), :] * g

    scr["c"][...] = _row_loop(tl, rev, step, scr["c"][...])
    row = lax.broadcasted_iota(jnp.int32, (sub_r, 1), 0)

    def grads(r0, l0):
        rs, ls = pl.ds(r0, sub_r), pl.ds(l0, LANES)
        g, a, m2 = scr["g"][rs, ls], scr["a"][rs, ls], _f32(scr["m2"], rs, ls)
        r, ig, xb = _f32(scr["r"], rs, ls), _f32(scr["ig"], rs, ls), scr["xb"][rs, ls]
        h = hs_ref[rs, ls]
        if d == 0:
            e0 = pl.multiple_of(jnp.maximum(r0 - SUBLANES, 0), SUBLANES)
            edge = jnp.where(r0 == 0, scr["edge"][:, ls], hs_ref[pl.ds(e0, SUBLANES), ls][SUBLANES - 1:, :])
            hprev = jnp.where(row == 0, edge, pltpu.roll(h, 1, 0))
        else:
            e0 = pl.multiple_of(jnp.minimum(r0 + sub_r, tl - SUBLANES), SUBLANES)
            edge = jnp.where(r0 == tl - sub_r, scr["edge"][:, ls], hs_ref[pl.ds(e0, SUBLANES), ls][:1, :])
            hprev = jnp.where(row == sub_r - 1, edge, pltpu.roll(h, sub_r - 1, 0))
        rsq = lax.rsqrt(m2)
        gm = g * (m2 * rsq)
        d_la = (g * hprev) * a - (g * (ig * xb)) * ((1.0 - m2) * rsq)
        d_pr = d_la * ((-LRU_C) * scr["sp"][:, ls]) * (r * (1.0 - r))
        d_pi = (gm * xb) * (ig * (1.0 - ig))
        scr["dyl"][rs, ls] = gm * ig
        dlam_ref[:, ls] += jnp.sum(d_la * ((-LRU_C) * r), axis=0, keepdims=True) * scr["dlf"][:, ls]
        dba_ref[:, ls] += jnp.sum(d_pr, axis=0, keepdims=True)
        dbx_ref[:, ls] += jnp.sum(d_pi, axis=0, keepdims=True)
        gi, off = divmod(l0, gs)
        scr["dp16"][rs, pl.ds(gi * 2 * gs + off, LANES)] = d_pr.astype(BF16)
        scr["dp16"][rs, pl.ds(gi * 2 * gs + gs + off, LANES)] = d_pi.astype(BF16)

    _sub_loop(tl, sub_r, W, grads)
    for gi in range(ng):
        dp = scr["dp16"][:, gi * 2 * gs:(gi + 1) * 2 * gs]
        scr["g"][:, gi * gs:(gi + 1) * gs] = lax.dot_general(dp, wd_r[gi], (((1,), (1,)), ((), ())),
                                                             preferred_element_type=F32)
        dwd_ref[gi] += lax.dot_general(scr["xb16"][:, gi * gs:(gi + 1) * gs], dp, (((0,), (0,)), ((), ())),
                                       preferred_element_type=F32)


def _edge_block(h, tl, nt, d):
    W = h.shape[1]
    per = tl // SUBLANES
    if d == 0:
        return pl.BlockSpec((SUBLANES, W), lambda i: (jnp.maximum((nt - 1 - i) * per - 1, 0), 0))
    return pl.BlockSpec((SUBLANES, W), lambda i: (jnp.minimum((i + 1) * per, nt * per - 1), 0))


def _mix_bwd0(P, dcat, saved0, gates0, h_init, c_init, wts, *, rows, row_off, row_w, tl, name):
    W = P.shape[1] // 6
    nt = rows // tl
    ob = row_off // tl
    gs = min(LRU_GROUP, W)
    ng = W // gs
    wd, lam = wts["wd"][0], wts["lam"][0]
    h0s = saved0["h"]
    kept = [saved0["xb"], saved0["xb16"]] + [gates0[n] for n in ("a", "r", "ig", "m2")]

    def tile(i):
        return nt - 1 - i

    vec = pl.BlockSpec((1, W), lambda i: (0, 0))
    wd_spec = pl.BlockSpec(wd.shape, lambda i: (0, 0, 0))
    seq = pl.BlockSpec((tl, W), lambda i: (tile(i), 0))

    sub_r = min(row_w, tl)
    assert tl % sub_r == 0

    def body(ql, dyb, hs, hedge8, xb_r, xb16_r, a_r, r_r, ig_r, m2_r, wd_r, lam_r, hin, cin,
             dxb_o, dwd_o, dba_o, dbx_o, dlam_o, cfin, *scratch):
        scr = dict(zip(_BWD_SCRATCH, scratch, strict=True))
        scr.update(zip(_FWD_SAVED, (xb_r, xb16_r, a_r, r_r, ig_r, m2_r), strict=True))
        i = pl.program_id(0)

        @pl.when(i == 0)
        def _():
            scr["c"][...] = cin[...]

        scr["edge"][...] = jnp.where(i == nt - 1, hin[...], hedge8[SUBLANES - 1:SUBLANES, :])
        _lru_bwd_tile(0, lambda rs, ls: _f32(dyb, rs, ls) * _silu(_f32(ql, rs, ls)), hs, wd_r, lam_r, scr,
                      (dwd_o, dba_o, dbx_o, dlam_o), i == 0, tl, sub_r, ng, gs)
        dxb_o[...] = scr["dyl"][...] + scr["g"][...]
        cfin[...] = scr["c"][...]

    return pl.pallas_call(
        body, name=name, grid=(nt,),
        in_specs=[pl.BlockSpec((tl, W), lambda i: (tile(i) + ob, 5)), pl.BlockSpec((tl, W), lambda i: (tile(i), 1)), seq,
                  _edge_block(h0s, tl, nt, 0)] + [seq] * 6 + [wd_spec, vec, vec, vec],
        out_specs=[seq, wd_spec, vec, vec, vec, vec],
        out_shape=[jax.ShapeDtypeStruct((rows, W), F32), jax.ShapeDtypeStruct(wd.shape, F32)]
        + [jax.ShapeDtypeStruct((1, W), F32)] * 4,
        scratch_shapes=_bwd_scratch(tl, W),
        compiler_params=_params(("arbitrary",)),
    )(P, dcat, h0s, h0s, *kept, wd, lam, h_init, c_init)


def _mix_bwd1(P, dcat, saved0, h1s, gates1, dxb0, h_init, c_init, wts, *, rows, row_off, row_w, tl, name,
              dp_rows=None, dp_off=0, dp_into=None):
    dp_rows = rows if dp_rows is None else dp_rows
    dpb = dp_off // tl
    W = P.shape[1] // 6
    nt = rows // tl
    ob = row_off // tl
    gs = min(LRU_GROUP, W)
    ng = W // gs
    wca, wcb = wts["wca"], wts["wcb"]
    wd, lam = wts["wd"][1], wts["lam"][1]
    h0s = saved0["h"]
    kept = [saved0["xb"], saved0["xb16"]] + [gates1[n] for n in ("a", "r", "ig", "m2")]

    vec = pl.BlockSpec((1, W), lambda i: (0, 0))
    taps = pl.BlockSpec((SUBLANES, W), lambda i: (0, 0))
    wd_spec = pl.BlockSpec(wd.shape, lambda i: (0, 0, 0))
    seq = pl.BlockSpec((tl, W), lambda i: (i, 0))

    sub_r = min(row_w, tl)
    assert tl % sub_r == 0

    def body(*refs):
        if dp_into is not None:
            refs = refs[1:]
        (bl, cl, ul, gl, vl, ql, dya, dyb, h0, h1, hedge8, dx0, xb_r, xb16_r, a_r, r_r, ig_r, m2_r, wca_r, wcb_r,
         wd_r, lam_r, hin, cin, dp_o, dwd_o, dba_o, dbx_o, dlam_o, dwca_o, dwcb_o, dbcb_o, cfin, *scratch) = refs
        scr = dict(zip(_BWD_SCRATCH, scratch, strict=True))
        scr.update(zip(_FWD_SAVED, (xb_r, xb16_r, a_r, r_r, ig_r, m2_r), strict=True))
        i = pl.program_id(0)

        @pl.when(i == 0)
        def _():
            scr["c"][...] = cin[...]
            dwca_o[...] = jnp.zeros_like(dwca_o)
            dwcb_o[...] = jnp.zeros_like(dwcb_o)
            dbcb_o[...] = jnp.zeros_like(dbcb_o)

        scr["edge"][...] = jnp.where(i == nt - 1, hin[...], hedge8[0:1, :])
        _lru_bwd_tile(1, lambda rs, ls: _f32(dyb, rs, ls) * _silu(_f32(ql, rs, ls)), h1, wd_r, lam_r, scr,
                      (dwd_o, dba_o, dbx_o, dlam_o), i == 0, tl, sub_r, ng, gs)
        cfin[...] = scr["c"][...]
        tmod = _tmod(sub_r, sub_r)

        def rest(r0, l0):
            rs, ls = pl.ds(r0, sub_r), pl.ds(l0, LANES)
            dxb = dx0[rs, ls] + scr["dyl"][rs, ls] + scr["g"][rs, ls]
            dv, dwb = _conv_bwd(dxb, _f32(vl, rs, ls), wcb_r, 4, 2, tmod, sub_r, ls)
            for j in range(4):
                dwcb_o[j:j + 1, ls] += dwb[j]
            dbcb_o[:, ls] += jnp.sum(dxb, axis=0, keepdims=True)
            q = _f32(ql, rs, ls)
            sq = _sigmoid(q)
            dq = _f32(dyb, rs, ls) * (h0[rs, ls] + h1[rs, ls]) * (sq * (1.0 + q * (1.0 - sq)))
            b_, c_, u_, g_ = _f32(bl, rs, ls), _f32(cl, rs, ls), _f32(ul, rs, ls), _f32(gl, rs, ls)
            z = c_ * u_
            cz = _conv(z, wca_r, 3, 1, tmod, sub_r, ls)
            sgm = _sigmoid(g_)
            sg = g_ * sgm
            da = _f32(dya, rs, ls)
            dz, dwa = _conv_bwd(da * b_ * sg, z, wca_r, 3, 1, tmod, sub_r, ls)
            for j in range(3):
                dwca_o[j:j + 1, ls] += dwa[j]
            parts = (da * cz * sg, dz * u_, dz * c_, da * b_ * cz * (sgm * (1.0 + g_ * (1.0 - sgm))), dv, dq)
            for k, val in enumerate(parts):
                dp_o[rs, pl.ds(k * W + l0, LANES)] = val.astype(dp_o.dtype)

        _sub_loop(tl, sub_r, W, rest)

    def pcol(j):
        return pl.BlockSpec((tl, W), lambda i: (i + ob, j))

    prev = [] if dp_into is None else [dp_into]
    return pl.pallas_call(
        body, name=name, grid=(nt,), input_output_aliases={} if dp_into is None else {0: 0},
        in_specs=[pl.BlockSpec(memory_space=pl.ANY)] * len(prev) + [pcol(j) for j in range(6)]
        + [pl.BlockSpec((tl, W), lambda i: (i, 0)), pl.BlockSpec((tl, W), lambda i: (i, 1)), seq, seq,
           _edge_block(h1s, tl, nt, 1), seq] + [seq] * 6 + [taps, taps, wd_spec, vec, vec, vec],
        out_specs=[pl.BlockSpec((tl, 6 * W), lambda i: (i + dpb, 0)), wd_spec, vec, vec, vec, taps, taps, vec, vec],
        out_shape=[jax.ShapeDtypeStruct((dp_rows, 6 * W), BF16), jax.ShapeDtypeStruct(wd.shape, F32)]
        + [jax.ShapeDtypeStruct((1, W), F32)] * 3
        + [jax.ShapeDtypeStruct((SUBLANES, W), F32)] * 2 + [jax.ShapeDtypeStruct((1, W), F32)] * 2,
        scratch_shapes=_bwd_scratch(tl, W),
        compiler_params=_params(("arbitrary",)),
    )(*prev, *([P] * 6), dcat, dcat, h0s, h1s, h1s, dxb0, *kept, wca, wcb, wd, lam, h_init, c_init)


def _loss_head(out, x, tgt, gt, fg, tr):
    L, D = x.shape

    def body(o_ref, x_ref, t_ref, gt_ref, fg_ref, dn_o, do_o, dfg_o, dgt_o, loss_o):
        i = pl.program_id(0)

        @pl.when(i == 0)
        def _():
            dfg_o[...] = jnp.zeros_like(dfg_o)
            dgt_o[...] = jnp.zeros_like(dgt_o)
            loss_o[...] = jnp.zeros_like(loss_o)

        o = o_ref[...].astype(F32)
        gt_v = gt_ref[...]
        fg_v = fg_ref[...]
        n = x_ref[...] + gt_v * o
        r = lax.rsqrt(jnp.mean(n * n, axis=-1, keepdims=True) + EPS)
        nr = n * r
        e = nr * fg_v - t_ref[...]
        loss_o[...] += 0.5 * jnp.sum(jnp.mean(e * e, axis=-1, keepdims=True))
        dy = e * (1.0 / D)
        dfg_o[...] += jnp.sum(dy * nr, axis=0, keepdims=True)
        qv = dy * fg_v
        dn = r * (qv - nr * jnp.mean(qv * nr, axis=-1, keepdims=True))
        dgt_o[...] += jnp.sum(dn * o, axis=0, keepdims=True)
        dn_o[...] = dn.astype(dn_o.dtype)
        do_o[...] = (dn * gt_v).astype(do_o.dtype)

    blk = pl.BlockSpec((tr, D), lambda i: (i, 0))
    vec = pl.BlockSpec((1, D), lambda i: (0, 0))
    return pl.pallas_call(
        body, name="loss_head", grid=(L // tr,), in_specs=[blk, blk, blk, vec, vec],
        out_specs=[blk, blk, vec, vec, pl.BlockSpec((SUBLANES, LANES), lambda i: (0, 0))],
        out_shape=[jax.ShapeDtypeStruct((L, D), BF16), jax.ShapeDtypeStruct((L, D), BF16),
                   jax.ShapeDtypeStruct((1, D), F32), jax.ShapeDtypeStruct((1, D), F32),
                   jax.ShapeDtypeStruct((SUBLANES, LANES), F32)],
        compiler_params=_params(("arbitrary",)),
    )(out, x, tgt, gt, fg)


def _norm_bwd(dhl, x, dn, g, sc, tr, name):
    L, D = x.shape
    with_x = dn is not None

    def body(*refs):
        if with_x:
            d_ref, x_ref, dn_ref, g_ref, sc_ref, gx_o, dsh_o, dsc_o, dg_o = refs
        else:
            d_ref, x_ref, g_ref, sc_ref, dsh_o, dsc_o, dg_o = refs
        i = pl.program_id(0)

        @pl.when(i == 0)
        def _():
            dsh_o[...] = jnp.zeros_like(dsh_o)
            dsc_o[...] = jnp.zeros_like(dsc_o)
            dg_o[...] = jnp.zeros_like(dg_o)

        d = d_ref[...].astype(F32)
        xv = x_ref[...]
        g_v = g_ref[...]
        r = lax.rsqrt(jnp.mean(xv * xv, axis=-1, keepdims=True) + EPS)
        xr = xv * r
        dsh_o[...] += jnp.sum(d, axis=0, keepdims=True)
        dsc_o[...] += jnp.sum(d * (xr * g_v), axis=0, keepdims=True)
        dxn = d * (1.0 + sc_ref[...])
        dg_o[...] += jnp.sum(dxn * xr, axis=0, keepdims=True)
        if with_x:
            qv = dxn * g_v
            gx_o[...] = r * (qv - xr * jnp.mean(qv * xr, axis=-1, keepdims=True)) + dn_ref[...].astype(F32)

    blk = pl.BlockSpec((tr, D), lambda i: (i, 0))
    vec = pl.BlockSpec((1, D), lambda i: (0, 0))
    vshape = jax.ShapeDtypeStruct((1, D), F32)
    res = pl.pallas_call(
        body, name=name, grid=(L // tr,),
        in_specs=[blk, blk] + ([blk] if with_x else []) + [vec, vec],
        out_specs=([blk] if with_x else []) + [vec, vec, vec],
        out_shape=([jax.ShapeDtypeStruct((L, D), F32)] if with_x else []) + [vshape] * 3,
        compiler_params=_params(("arbitrary",)),
    )(*([dhl, x] + ([dn] if with_x else []) + [g, sc]))
    return res if with_x else [None] + list(res)


def _pack_blockdiag(wa, wx, gs):
    H, hd, _ = wa.shape
    hp = gs // hd
    ng = H // hp
    eye = jnp.eye(hp, dtype=wa.dtype)

    def bd(w):
        return jnp.einsum("gpij,pq->gpiqj", w.reshape(ng, hp, hd, hd), eye).reshape(ng, gs, gs)

    return jnp.concatenate([bd(wa), bd(wx)], axis=-1).astype(BF16)


def _unpack_blockdiag(dwd, H, hd, gs):
    hp = gs // hd
    ng = H // hp
    eye = jnp.eye(hp, dtype=dwd.dtype)

    def diag(dm):
        return jnp.einsum("gpiqj,pq->gpij", dm.reshape(ng, hp, hd, hp, hd), eye).reshape(H, hd, hd)

    return diag(dwd[:, :, :gs]), diag(dwd[:, :, gs:])


def kernel(x, c, ctx, c_ctx, norm_g, w_ada, b_ada, w_in, w_conv_a, w_conv_b, b_conv_b, lru_wa, lru_ba, lru_wx, lru_bx, lru_lambda, w_out, final_g, loss_target, m_c_ctx, m_norm_g, m_w_ada, m_b_ada, m_w_in, m_w_conv_a, m_w_conv_b, m_b_conv_b, m_lru_wa, m_lru_ba, m_lru_wx, m_lru_bx, m_lru_lambda, m_w_out, m_final_g, v_c_ctx, v_norm_g, v_w_ada, v_b_ada, v_w_in, v_w_conv_a, v_w_conv_b, v_b_conv_b, v_lru_wa, v_lru_ba, v_lru_wx, v_lru_bx, v_lru_lambda, v_w_out, v_final_g):
    xi, yi, ci = _pos()
    me = 4 * xi + 2 * yi + ci
    q = 2 * xi + yi
    first_core = (ci == 0).astype(F32)

    L, D = x.shape[1], x.shape[2]
    T = ctx.shape[1]
    W = D // 2
    Wq = W // 4
    H, hd = lru_wa.shape[2], lru_wa.shape[3]
    gs = min(LRU_GROUP, W)
    nq = w_ada.shape[2]
    tl = min(256, T, L)
    tr = min(256, T, L)
    x2, ctx2, tgt2 = x[0], ctx[0], loss_target[0]

    def place(shard, full_cols):
        z = jnp.zeros((shard.shape[0], full_cols), F32)
        return lax.dynamic_update_slice(z, shard * first_core, (0, q * shard.shape[1]))

    c_rows = lax.dynamic_update_slice(jnp.zeros((8, D), F32), c, (me, 0))
    small_in = [c_rows, place(w_conv_a[0], W), place(w_conv_b[0], W), place(lru_ba[0], W), place(lru_bx[0], W),
                place(lru_lambda[0], W)]
    small_shapes = [a.shape for a in small_in]
    gathered = _allreduce8(_pack(small_in, 8 * SUBLANES), "gather_small")
    c_all, wca, wcb, ba_all, bx_all, lam_all = _unpack(gathered, small_shapes)

    s_rows = jnp.concatenate([c_all, c_ctx[None, :], jnp.zeros((7, D), F32)], axis=0)
    mod_part = _matmul(s_rows, w_ada[0], a_act="silu", bias=lax.dynamic_slice(b_ada, (0, q * nq), (1, nq)),
                       tm=16, tn=nq, tk=512, name="ada_fwd")
    mod_all = _allreduce8(_pack([place(mod_part, 4 * nq)], 8 * SUBLANES), "gather_mod")
    mod_all = _unpack(mod_all, [(16, 4 * nq)])[0]
    mod_l = lax.dynamic_slice(mod_all, (me, 0), (1, 3 * D))
    mod_c = mod_all[8:9]
    sh_l, sc_l, gt_l = mod_l[:, :D], mod_l[:, D:2 * D], mod_l[:, 2 * D:]
    sh_c, sc_c = mod_c[:, :D], mod_c[:, D:2 * D]

    pad_taps = lambda w: jnp.pad(w, ((0, SUBLANES - w.shape[0]), (0, 0)))
    wts = {
        "wca": pad_taps(wca), "wcb": pad_taps(wcb), "bcb": b_conv_b,
        "wd": [_pack_blockdiag(lru_wa[0, d], lru_wx[0, d], gs) for d in range(2)],
        "ba": [ba_all[d:d + 1] for d in range(2)], "bx": [bx_all[d:d + 1] for d in range(2)],
        "lam": [lam_all[d:d + 1] for d in range(2)],
    }

    hl = _norm_in(x2, ctx2, norm_g, sc_l, sh_l, sc_c, sh_c, tr)
    p_lat, win_full, wout_full = _in_proj_gather(hl, w_in[0].astype(BF16), w_out[0].astype(BF16),
                                                 jnp.reshape(q, (1,)).astype(jnp.int32), rows=L, tm=min(1024, L))
    p_ctx = _matmul(hl, win_full, a_rows=T, a_off=L, tm=T, tn=1536, tk=D, out_dtype=BF16, name="in_proj_ctx")
    zero_w = jnp.zeros((1, W), F32)
    ctx0, cgates0 = _mix_fwd(p_ctx, 0, zero_w, wts, rows=T, row_off=0, row_w=T, tl=tl, name="ctx_fwd0")
    c1s, _, cgates1 = _mix_fwd(p_ctx, 1, zero_w, wts, rows=T, row_off=0, row_w=T, tl=tl, saved0=ctx0, name="ctx_fwd1")
    h0_init, h1_init = ctx0["h"][T - 1:T], c1s[0:1]
    lat0, gates0 = _mix_fwd(p_lat, 0, h0_init, wts, rows=L, row_off=0, row_w=GRID_W, tl=tl, name="mix_fwd0")
    h1s, cat, gates1 = _mix_fwd(p_lat, 1, h1_init, wts, rows=L, row_off=0, row_w=GRID_W, tl=tl, saved0=lat0,
                                name="mix_fwd1")
    out = _matmul(cat, wout_full, tm=512, tn=D, tk=2 * W, out_dtype=BF16, name="out_proj")
    dn, dout, dfg, dgt, loss_blk = _loss_head(out, x2, tgt2, gt_l, final_g[None, :], tr)

    dcat = _matmul(dout, wout_full, tb=True, tm=512, tn=2 * W, tk=D, out_dtype=BF16, name="out_proj_bwd")
    gw_out = _matmul(cat, dout, ta=True, tm=1024, tn=D, tk=2048, out_dtype=BF16, name="w_out_grad")
    dxb0, dwd0, dba0, dbx0, dlam0, ch0 = _mix_bwd0(p_lat, dcat, lat0, gates0, h0_init, zero_w, wts, rows=L, row_off=0,
                                                   row_w=GRID_W, tl=tl, name="mix_bwd0")
    dp_lat, dwd1, dba1, dbx1, dlam1, dwca, dwcb, dbcb, ch1 = _mix_bwd1(
        p_lat, dcat, lat0, h1s, gates1, dxb0, h1_init, zero_w, wts, rows=L, row_off=0, row_w=GRID_W, tl=tl,
        name="mix_bwd1", dp_rows=L + T)
    zero_cat = jnp.zeros((T, 2 * W), BF16)
    cxb0, cwd0, cba0, cbx0, clam0, _ = _mix_bwd0(p_ctx, zero_cat, ctx0, cgates0, zero_w, ch0, wts, rows=T, row_off=0,
                                                 row_w=T, tl=tl, name="ctx_bwd0")
    dp, cwd1, cba1, cbx1, clam1, cwca, cwcb, cbcb, _ = _mix_bwd1(
        p_ctx, zero_cat, ctx0, c1s, cgates1, cxb0, zero_w, ch1, wts, rows=T, row_off=0, row_w=T, tl=tl, name="ctx_bwd1",
        dp_rows=L + T, dp_off=L, dp_into=dp_lat)

    gw_in = _matmul(hl, dp, ta=True, tm=1024, tn=1536, tk=2816, out_dtype=BF16, name="w_in_grad")
    rs_axes = [1, 0]
    pair_sums = _rs_pair_sums([gw_in, gw_out], rs_axes)
    dhc = _matmul(dp, win_full, tb=True, a_rows=T, a_off=L, tm=T, tn=D, tk=3072, name="in_proj_bwd_ctx")
    _, dsh_c, dsc_c, dng_c = _norm_bwd(dhc, ctx2, None, norm_g, sc_c, tr, "norm_bwd_ctx")
    zeros_d = jnp.zeros((1, D), F32)
    dmod_c = jnp.concatenate([dsh_c, dsc_c, zeros_d], axis=1)
    dhl, (*rs_slots, dmod_c_all) = _matmul(
        dp, win_full, tb=True, a_rows=L, tm=512, tn=D, tk=3072, out_dtype=BF16, name="in_proj_bwd",
        side=_join_sides(_rs_chips_side(pair_sums), _gather_block_side(jnp.pad(dmod_c, ((0, SUBLANES - 1), (0, 0))))))
    gx, dsh_l, dsc_l, dng_l = _norm_bwd(dhl, x2, dn, norm_g, sc_l, tr, "norm_bwd")
    gc_rows = _matmul(lax.dynamic_slice(dmod_c_all.reshape(8 * SUBLANES, 3 * D), (0, q * nq), (8 * SUBLANES, nq)), w_ada[0],
                      tb=True, dsilu_mul=c_ctx[None, :], tm=8 * SUBLANES, tn=D, tk=512, name="c_ctx_grad")
    gc_part = jnp.sum(gc_rows, axis=0, keepdims=True) * first_core

    g_in_shard, g_out_shard = _rs_finish(rs_slots, rs_axes)

    dwa0, dwx0 = _unpack_blockdiag(dwd0 + cwd0, H, hd, gs)
    dwa1, dwx1 = _unpack_blockdiag(dwd1 + cwd1, H, hd, gs)
    dmod_l = jnp.concatenate([dsh_l, dsc_l, dgt], axis=1)
    small_g = [
        lax.dynamic_update_slice(jnp.zeros((8, 3 * D), F32), dmod_l, (me, 0)), dmod_c,
        dfg, dng_l + dng_c, (dwca + cwca)[:3], (dwcb + cwcb)[:4], dbcb + cbcb,
        jnp.stack([dwa0, dwa1]), jnp.stack([dwx0, dwx1]),
        jnp.concatenate([dba0 + cba0, dba1 + cba1], axis=0), jnp.concatenate([dbx0 + cbx0, dbx1 + cbx1], axis=0),
        jnp.concatenate([dlam0 + clam0, dlam1 + clam1], axis=0), loss_blk[0:1, 0:1], gc_part,
    ]
    g_shapes = [a.shape for a in small_g]
    (g_rows, g_modc, g_fg, g_ng, g_wca, g_wcb, g_bcb, g_wa, g_wx, g_ba, g_bx, g_lam, loss_sum, g_c_ctx) = _unpack(
        _allreduce8_two_level(_pack(small_g, 8 * SUBLANES), "reduce_small"), g_shapes)

    g_mod = jnp.concatenate([g_rows, g_modc, jnp.zeros((7, 3 * D), F32)], axis=0)
    g_mod_q = lax.dynamic_slice(g_mod, (0, q * nq), (16, nq))
    g_w_ada = _matmul(s_rows, g_mod_q, ta=True, a_act="silu", tm=1024, tn=nq, tk=16, name="w_ada_grad")
    g_b_ada = jnp.sum(g_mod[:9], axis=0, keepdims=True)

    def shard_cols(a, width):
        return lax.dynamic_slice(a, (0, q * width), (a.shape[0], width))

    grads = {
        "c_ctx": g_c_ctx, "norm_g": g_ng, "b_ada": g_b_ada,
        "w_conv_a": shard_cols(g_wca, Wq)[None], "w_conv_b": shard_cols(g_wcb, Wq)[None], "b_conv_b": g_bcb,
        "lru_wa": g_wa[None], "lru_ba": shard_cols(g_ba, Wq)[None], "lru_wx": g_wx[None],
        "lru_bx": shard_cols(g_bx, Wq)[None], "lru_lambda": shard_cols(g_lam, Wq)[None], "final_g": g_fg[0],
    }
    small_names = list(grads)
    given = dict(c_ctx=(c_ctx, m_c_ctx, v_c_ctx), norm_g=(norm_g, m_norm_g, v_norm_g), b_ada=(b_ada, m_b_ada, v_b_ada),
                 w_conv_a=(w_conv_a, m_w_conv_a, v_w_conv_a), w_conv_b=(w_conv_b, m_w_conv_b, v_w_conv_b),
                 b_conv_b=(b_conv_b, m_b_conv_b, v_b_conv_b), lru_wa=(lru_wa, m_lru_wa, v_lru_wa),
                 lru_ba=(lru_ba, m_lru_ba, v_lru_ba), lru_wx=(lru_wx, m_lru_wx, v_lru_wx),
                 lru_bx=(lru_bx, m_lru_bx, v_lru_bx), lru_lambda=(lru_lambda, m_lru_lambda, v_lru_lambda),
                 final_g=(final_g, m_final_g, v_final_g))
    def rows2d(a):
        return a.reshape(-1, a.shape[-1])

    grads = {n: grads[n].reshape(given[n][0].shape) for n in small_names}
    quads = [tuple(rows2d(a) for a in (given[n][0], grads[n], given[n][1], given[n][2])) for n in small_names]
    updated = _adam_many(quads, "adam_small")
    delta_s, newm_s, newv_s = ({n: u[j].reshape(given[n][0].shape) for n, u in zip(small_names, updated)} for j in range(3))

    big = {"w_ada": (w_ada, g_w_ada, m_w_ada, v_w_ada), "w_in": (w_in, g_in_shard, m_w_in, v_w_in),
           "w_out": (w_out, g_out_shard, m_w_out, v_w_out)}
    delta_b, newm_b, newv_b = {}, {}, {}
    for n, (w, g, m, v) in big.items():
        d_, m_, v_, *echo = _adam(w[0], g, m[0], v[0], "adam_" + n, echo_g=n != "w_ada")
        grads[n] = (echo[0] if echo else g)[None]
        delta_b[n], newm_b[n], newv_b[n] = d_[None], m_[None], v_[None]

    loss = loss_sum[0, 0]
    order = ["c_ctx", "norm_g", "w_ada", "b_ada", "w_in", "w_conv_a", "w_conv_b", "b_conv_b", "lru_wa", "lru_ba",
             "lru_wx", "lru_bx", "lru_lambda", "w_out", "final_g"]
    delta = {**delta_s, **delta_b}
    newm = {**newm_s, **newm_b}
    newv = {**newv_s, **newv_b}
    return (loss, gx[None], *[grads[n] for n in order], *[delta[n] for n in order], *[newm[n] for n in order],
            *[newv[n] for n in order])
```

```python
import functools

import jax
import jax.numpy as jnp
from jax import lax
from jax.experimental import pallas as pl
from jax.experimental.pallas import tpu as pltpu

F32 = jnp.float32
BF16 = jnp.bfloat16
MESH_ID = pl.DeviceIdType.MESH

EPS = 1e-6
LRU_C = 8.0
GRID_W = 64
ADAM_LR = 0.001
ADAM_B1 = 0.9
ADAM_B2 = 0.999
ADAM_EPS = 1e-08
ADAM_WD = 0.01
ADAM_STEP = 10

LANES = 128
SUBLANES = 8
PACK_COLS = 1024
VMEM_LIMIT = 56 * 2**20
LRU_GROUP = 256


def _params(sem=None):
    return pltpu.CompilerParams(vmem_limit_bytes=VMEM_LIMIT, dimension_semantics=sem)


def _pick(dim, pref, quantum=LANES):
    if dim <= pref:
        return dim
    best = None
    for t in range(quantum, pref + 1, quantum):
        if dim % t == 0:
            best = t
    assert best is not None, (dim, pref)
    return best


def _pos():
    return lax.axis_index("x"), lax.axis_index("y"), lax.axis_index("c")


def _flip(v, bit):
    return 1 - v if bit else v


def _sigmoid(v):
    return 0.5 * jnp.tanh(0.5 * v) + 0.5


def _silu(v):
    return v * _sigmoid(v)


def _dsilu(v):
    s = _sigmoid(v)
    return s * (1.0 + v * (1.0 - s))


def _gates(pre_r, pre_i, sp):
    r = _sigmoid(pre_r)
    ig = _sigmoid(pre_i)
    e = LRU_C * r * sp
    w = jnp.tanh(e)
    return r, ig, jnp.exp(-e), (2.0 * w) * pl.reciprocal(1.0 + w, approx=True)


def _softplus(z):
    return jnp.maximum(z, 0.0) + jnp.log1p(jnp.exp(-jnp.abs(z)))


def _matmul(a, b, *, ta=False, tb=False, tm=512, tn=512, tk=512, out_dtype=F32, name,
            a_rows=None, a_off=0, a_act=None, bias=None, dsilu_mul=None, side=None):
    rows_a = a.shape[0] if a_rows is None else a_rows
    if ta:
        K, M = rows_a, a.shape[1]
    else:
        M, K = rows_a, a.shape[1]
    N = b.shape[0] if tb else b.shape[1]
    tm, tn, tk = _pick(M, tm, SUBLANES), _pick(N, tn), _pick(K, tk)
    t_rows = tk if ta else tm
    assert a_off % t_rows == 0
    nk = K // tk
    gi, gj = M // tm, N // tn
    off_blocks = a_off // t_rows
    dims = (((0 if ta else 1,), (1 if tb else 0,)), ((), ()))
    extras = [e for e in (bias, dsilu_mul) if e is not None]
    n_sin = len(side["ins"]) if side else 0
    n_sout = len(side["outs"]) if side else 0

    def body(a_ref, b_ref, *rest):
        rest = list(rest)
        bias_ref = rest.pop(0) if bias is not None else None
        dsm_ref = rest.pop(0) if dsilu_mul is not None else None
        side_in = [rest.pop(0) for _ in range(n_sin)]
        o_ref = rest.pop(0)
        side_out = [rest.pop(0) for _ in range(n_sout)]
        acc_ref = rest.pop(0) if nk > 1 else None
        side_scr = rest
        i, j, k = pl.program_id(0), pl.program_id(1), pl.program_id(2)

        if side:
            @pl.when((i == 0) & (j == 0) & (k == 0))
            def _():
                side["start"](side_in, side_out, side_scr)

        av = a_ref[...]
        if a_act == "silu":
            av = _silu(av)
        prod = lax.dot_general(av, b_ref[...], dims, preferred_element_type=F32)

        def finish(r):
            if bias_ref is not None:
                r = r + bias_ref[...]
            if dsm_ref is not None:
                r = r * _dsilu(dsm_ref[...])
            o_ref[...] = r.astype(o_ref.dtype)

        if nk == 1:
            finish(prod)
        else:
            @pl.when(k == 0)
            def _():
                acc_ref[...] = prod

            @pl.when(k > 0)
            def _():
                acc_ref[...] += prod

            @pl.when(k == nk - 1)
            def _():
                finish(acc_ref[...])

        if side:
            @pl.when((i == gi - 1) & (j == gj - 1) & (k == nk - 1))
            def _():
                side["finish"](side_in, side_out, side_scr)

    if ta:
        a_spec = pl.BlockSpec((tk, tm), lambda i, j, k: (k + off_blocks, i))
    else:
        a_spec = pl.BlockSpec((tm, tk), lambda i, j, k: (i + off_blocks, k))
    if tb:
        b_spec = pl.BlockSpec((tn, tk), lambda i, j, k: (j, k))
    else:
        b_spec = pl.BlockSpec((tk, tn), lambda i, j, k: (k, j))
    in_specs = [a_spec, b_spec]
    if bias is not None:
        in_specs.append(pl.BlockSpec((1, tn), lambda i, j, k: (0, j)))
    if dsilu_mul is not None:
        in_specs.append(pl.BlockSpec((1, tn), lambda i, j, k: (0, j)))
    hbm = pl.BlockSpec(memory_space=pl.ANY)
    res = pl.pallas_call(
        body, name=name, grid=(gi, gj, nk),
        in_specs=in_specs + [hbm] * n_sin,
        out_specs=[pl.BlockSpec((tm, tn), lambda i, j, k: (i, j))] + [hbm] * n_sout,
        out_shape=[jax.ShapeDtypeStruct((M, N), out_dtype)] + (list(side["outs"]) if side else []),
        scratch_shapes=([pltpu.VMEM((tm, tn), F32)] if nk > 1 else []) + (list(side["scratch"]) if side else []),
        compiler_params=_params(("arbitrary",) * 3 if side else ("parallel", "parallel", "arbitrary")),
    )(a, b, *extras, *(side["ins"] if side else []))
    return (res[0], res[1:]) if side else res[0]


def _elementwise(fn, ins, outs, *, rows, cols, name, tr=256):
    tr = _pick(rows, tr, 2 * SUBLANES)
    n_in = len(ins)

    def body(*refs):
        vals = fn(*[r[...] for r in refs[:n_in]])
        if not isinstance(vals, (tuple, list)):
            vals = (vals,)
        for r, v in zip(refs[n_in:], vals, strict=True):
            r[...] = v.astype(r.dtype)

    def spec(off):
        assert off % tr == 0
        ob = off // tr
        return pl.BlockSpec((tr, cols), lambda i: (i + ob, 0))

    res = pl.pallas_call(
        body, name=name, grid=(rows // tr,),
        in_specs=[spec(off) for _, off in ins],
        out_specs=[spec(0) for _ in outs],
        out_shape=[jax.ShapeDtypeStruct((rows, cols), dt) for dt in outs],
        compiler_params=_params(("parallel",)),
    )(*[a for a, _ in ins])
    return res


def _adam_math(w, g, m, v):
    m = ADAM_B1 * m + (1.0 - ADAM_B1) * g
    v = ADAM_B2 * v + (1.0 - ADAM_B2) * (g * g)
    m_hat = m / (1.0 - ADAM_B1 ** ADAM_STEP)
    v_hat = v / (1.0 - ADAM_B2 ** ADAM_STEP)
    delta = -ADAM_LR * (m_hat / (jnp.sqrt(v_hat) + ADAM_EPS) + ADAM_WD * w)
    return delta, m, v


def _adam(w, g, m, v, name, echo_g=False):
    rows, cols = w.shape
    fn = (lambda w_, g_, m_, v_: _adam_math(w_, g_, m_, v_) + (g_,)) if echo_g else _adam_math
    return _elementwise(fn, [(w, 0), (g, 0), (m, 0), (v, 0)], [F32] * (4 if echo_g else 3),
                        rows=rows, cols=cols, name=name)


def _adam_many(quads, name):
    n = len(quads)

    def body(*refs):
        ins, outs = refs[:4 * n], refs[4 * n:]
        for t in range(n):
            w, g, m, v = (r[...] for r in ins[4 * t:4 * t + 4])
            for o_ref, val in zip(outs[3 * t:3 * t + 3], _adam_math(w, g, m, v), strict=True):
                o_ref[...] = val

    res = pl.pallas_call(
        body, name=name,
        out_shape=[jax.ShapeDtypeStruct(q[0].shape, F32) for q in quads for _ in range(3)],
        compiler_params=_params(),
    )(*[a for q in quads for a in q])
    return [res[3 * t:3 * t + 3] for t in range(n)]


def _pack(arrs, row_quantum):
    flat = jnp.concatenate([a.reshape(-1).astype(F32) for a in arrs])
    n = flat.shape[0]
    q = row_quantum * PACK_COLS
    total = -(-n // q) * q
    flat = jnp.pad(flat, (0, total - n))
    return flat.reshape(total // PACK_COLS, PACK_COLS)


def _unpack(buf, shapes):
    flat = buf.reshape(-1)
    out, off = [], 0
    for s in shapes:
        n = 1
        for d in s:
            n *= d
        out.append(flat[off:off + n].reshape(s))
        off += n
    return out


def _allreduce8(buf, name):
    R, C = buf.shape
    assert R % (8 * SUBLANES) == 0
    m = R // 8

    def body(x_ref, o_ref, recv, red, s1, r1, s2, r2):
        x, y, c = _pos()
        me = 4 * x + 2 * y + c

        def peer(k):
            px, py, pc = _flip(x, (k >> 2) & 1), _flip(y, (k >> 1) & 1), _flip(c, k & 1)
            return (px, py, pc), 4 * px + 2 * py + pc

        def rows(ref, idx):
            return ref.at[pl.ds(pl.multiple_of(idx * m, SUBLANES), m), :]

        def scatter(k):
            dev, p = peer(k)
            return pltpu.make_async_remote_copy(src_ref=rows(x_ref, p), dst_ref=recv.at[k], send_sem=s1.at[k],
                                                recv_sem=r1.at[k], device_id=dev, device_id_type=MESH_ID)

        def share(k):
            dev, p = peer(k)
            return pltpu.make_async_remote_copy(src_ref=red, dst_ref=rows(o_ref, me), send_sem=s2.at[k],
                                                recv_sem=r2.at[k], device_id=dev, device_id_type=MESH_ID)

        def shared_from(k):
            dev, p = peer(k)
            return pltpu.make_async_remote_copy(src_ref=red, dst_ref=rows(o_ref, p), send_sem=s2.at[k],
                                                recv_sem=r2.at[k], device_id=dev, device_id_type=MESH_ID)

        for k in range(1, 8):
            scatter(k).start()
        acc = rows(x_ref, me)[...]
        for k in range(1, 8):
            scatter(k).wait_recv()
            acc = acc + recv[k]
        red[...] = acc
        rows(o_ref, me)[...] = acc
        for k in range(1, 8):
            share(k).start()
        for k in range(1, 8):
            shared_from(k).wait_recv()
        for k in range(1, 8):
            scatter(k).wait_send()
            share(k).wait_send()

    return pl.pallas_call(
        body, name=name,
        in_specs=[pl.BlockSpec(memory_space=pltpu.VMEM)],
        out_specs=pl.BlockSpec(memory_space=pltpu.VMEM),
        out_shape=jax.ShapeDtypeStruct((R, C), F32),
        scratch_shapes=[pltpu.VMEM((8, m, C), F32), pltpu.VMEM((m, C), F32),
                        pltpu.SemaphoreType.DMA((8,)), pltpu.SemaphoreType.DMA((8,)),
                        pltpu.SemaphoreType.DMA((8,)), pltpu.SemaphoreType.DMA((8,))],
        compiler_params=_params(),
    )(buf)


def _allreduce8_two_level(buf, name):
    R, C = buf.shape
    assert R % (8 * SUBLANES) == 0
    m, hr = R // 8, R // 2

    def body(x_ref, o_ref, got0, half, got1, red, ssem, rsem):
        x, y, c = _pos()
        q = 2 * x + y
        sibling = (x, y, 1 - c)

        def half_rows(ref, core):
            return ref.at[pl.ds(pl.multiple_of(core * hr, SUBLANES), hr), :]

        def chunk(ref, core, chip):
            return ref.at[pl.ds(pl.multiple_of(core * hr + chip * m, SUBLANES), m), :]

        def chip_of(k):
            px, py = _flip(x, (k >> 1) & 1), _flip(y, k & 1)
            return (px, py, c), 2 * px + py

        def copy(src, dst, phase, k, dev):
            return pltpu.make_async_remote_copy(src_ref=src, dst_ref=dst, send_sem=ssem.at[phase, k],
                                                recv_sem=rsem.at[phase, k], device_id=dev, device_id_type=MESH_ID)

        swap = copy(half_rows(x_ref, 1 - c), got0, 0, 0, sibling)
        swap.start()
        swap.wait()
        half[...] = half_rows(x_ref, c)[...] + got0[...]

        def scatter(k):
            dev, p = chip_of(k)
            return copy(half.at[pl.ds(pl.multiple_of(p * m, SUBLANES), m), :], got1.at[k], 1, k, dev)

        for k in range(1, 4):
            scatter(k).start()
        acc = half[pl.ds(pl.multiple_of(q * m, SUBLANES), m), :]
        for k in range(1, 4):
            scatter(k).wait_recv()
            acc = acc + got1[k]
        red[...] = acc
        chunk(o_ref, c, q)[...] = acc

        def share(k, landing_chip):
            return copy(red, chunk(o_ref, c, landing_chip), 2, k, chip_of(k)[0])

        for k in range(1, 4):
            share(k, q).start()
        for k in range(1, 4):
            share(k, chip_of(k)[1]).wait_recv()
        back = copy(half_rows(o_ref, c), half_rows(o_ref, c), 3, 0, sibling)
        back.start()
        copy(half_rows(o_ref, 1 - c), half_rows(o_ref, 1 - c), 3, 0, sibling).wait_recv()
        back.wait_send()
        for k in range(1, 4):
            scatter(k).wait_send()
            share(k, q).wait_send()

    return pl.pallas_call(
        body, name=name,
        in_specs=[pl.BlockSpec(memory_space=pltpu.VMEM)],
        out_specs=pl.BlockSpec(memory_space=pltpu.VMEM),
        out_shape=jax.ShapeDtypeStruct((R, C), F32),
        scratch_shapes=[pltpu.VMEM((hr, C), F32), pltpu.VMEM((hr, C), F32), pltpu.VMEM((4, m, C), F32), pltpu.VMEM((m, C), F32),
                        pltpu.SemaphoreType.DMA((4, 4)), pltpu.SemaphoreType.DMA((4, 4))],
        compiler_params=_params(),
    )(buf)


def _bounce(src, dst, buf, sem):
    cin = pltpu.make_async_copy(src, buf, sem)
    cin.start()
    cin.wait()
    cout = pltpu.make_async_copy(buf, dst, sem)
    cout.start()
    cout.wait()


def _chunk(ref, axis, idx, size):
    start = idx * size
    if axis == 0:
        return ref.at[pl.ds(start, size), :]
    return ref.at[:, pl.ds(start, size)]


def _in_proj_gather(hl, win, wout, q_arr, *, rows, tm):
    D, nq = win.shape
    dq, D2 = wout.shape
    ni = rows // tm
    ops = ((0, 1, nq, D // 2), (1, 0, dq, dq // 2))

    def body(q_ref, a_ref, win_ref, wout_ref, p_ref, gin_ref, gout_ref, b_scr, buf_out, lsem, ssem, rsem, fsem, gsem):
        j, i = pl.program_id(0), pl.program_id(1)
        x, y, c = _pos()
        q = 2 * x + y
        srcs = (win_ref, wout_ref)
        dsts = (gin_ref, gout_ref)

        def shard_window(o, chip):
            _, axis, size, _ = ops[o]
            return _chunk(dsts[o], axis, chip, size)

        def half(ref, o, core):
            return ref.at[pl.ds(core * ops[o][3], ops[o][3]), :]

        def half_window(o, chip, core):
            _, axis, size, hs = ops[o]
            if axis == 1:
                return dsts[o].at[pl.ds(core * hs, hs), pl.ds(chip * size, size)]
            return dsts[o].at[pl.ds(chip * size + core * hs, hs), :]

        def chip_of(k):
            px, py = _flip(x, (k >> 1) & 1), _flip(y, k & 1)
            return px, py, 2 * px + py

        def send(o, k):
            px, py, _ = chip_of(k)
            return pltpu.make_async_remote_copy(
                src_ref=half(srcs[o], o, c), dst_ref=half_window(o, q, c), send_sem=ssem.at[o, k],
                recv_sem=rsem.at[o, k], device_id=(px, py, c), device_id_type=MESH_ID)

        def chip_recv(o, k):
            px, py, pq = chip_of(k)
            landed = half_window(o, pq, c)
            pltpu.make_async_remote_copy(src_ref=landed, dst_ref=landed, send_sem=ssem.at[o, k], recv_sem=rsem.at[o, k],
                                         device_id=(px, py, c), device_id_type=MESH_ID).wait_recv()

        def to_sibling(o, k):
            landed = half_window(o, chip_of(k)[2], c)
            return pltpu.make_async_remote_copy(src_ref=landed, dst_ref=landed, send_sem=fsem.at[o, k],
                                                recv_sem=gsem.at[o, k], device_id=(x, y, 1 - c), device_id_type=MESH_ID)

        def from_sibling(o, k):
            theirs = half_window(o, chip_of(k)[2], 1 - c)
            pltpu.make_async_remote_copy(src_ref=theirs, dst_ref=theirs, send_sem=fsem.at[o, k], recv_sem=gsem.at[o, k],
                                         device_id=(x, y, 1 - c), device_id_type=MESH_ID).wait_recv()

        def relay(o, core):
            if core == 0:
                landed, target = half_window(o, chip_of(2)[2], 0), (x, 1 - y, 0)
            else:
                landed, target = half_window(o, chip_of(1)[2], 1), (1 - x, y, 1)
            return pltpu.make_async_remote_copy(src_ref=landed, dst_ref=landed, send_sem=ssem.at[o, 3],
                                                recv_sem=rsem.at[o, 3], device_id=target, device_id_type=MESH_ID)

        def on_core(core, fn):
            @pl.when(c == core)
            def _():
                fn()

        def land(o, k):
            chip_recv(o, k)
            if k == 2:
                on_core(0, lambda: relay(o, 0).start())
            if k == 1:
                on_core(1, lambda: relay(o, 1).start())
            to_sibling(o, k).start()

        def settle(o, k):
            from_sibling(o, k)
            to_sibling(o, k).wait_send()

        def b_load(k, slot):
            src = win_ref if k == 0 else shard_window(0, chip_of(k)[2])
            return pltpu.make_async_copy(src, b_scr.at[slot], lsem.at[0])

        def own_store():
            return pltpu.make_async_copy(b_scr.at[0], shard_window(0, q), lsem.at[2])

        order = (0, 2, 1, 3)
        early = max(ni - 2, 0)

        @pl.when((j == 0) & (i == 0))
        def _():
            for o in range(2):
                for k in (2, 1):
                    send(o, k).start()
            first = b_load(0, 0)
            first.start()
            first.wait()
            own_store().start()
            _bounce(wout_ref, shard_window(1, q), buf_out, lsem.at[1])

        for jj in range(3):
            nxt = order[jj + 1]

            @pl.when((j == jj) & (i == early))
            def _(nxt=nxt):
                land(0, nxt)

            @pl.when((j == jj) & (i == ni - 1))
            def _(jj=jj, nxt=nxt):
                settle(0, nxt)
                if jj == 1:
                    own_store().wait()
                b_load(nxt, (jj + 1) % 2).start()

            @pl.when((j == jj + 1) & (i == 0))
            def _(jj=jj, nxt=nxt):
                b_load(nxt, (jj + 1) % 2).wait()

        @pl.when((j == 3) & (i == 0))
        def _():
            land(1, 2)
            land(1, 1)

        p_ref[...] = jnp.dot(a_ref[...], b_scr[j % 2], preferred_element_type=F32).astype(p_ref.dtype)

        @pl.when((j == 3) & (i == ni - 1))
        def _():
            settle(1, 2)
            settle(1, 1)
            land(1, 3)
            settle(1, 3)
            for o in range(2):
                for k in (2, 1):
                    send(o, k).wait_send()
                for core in range(2):
                    on_core(core, lambda o=o, core=core: relay(o, core).wait_send())

    hbm = pl.BlockSpec(memory_space=pl.ANY)
    grid_spec = pltpu.PrefetchScalarGridSpec(
        num_scalar_prefetch=1, grid=(4, ni),
        in_specs=[pl.BlockSpec((tm, D), lambda j, i, qr: (i, 0)), hbm, hbm],
        out_specs=[pl.BlockSpec((tm, nq), lambda j, i, qr: (i, jnp.bitwise_xor(qr[0], ((j & 1) << 1) | (j >> 1)))),
                   hbm, hbm],
        scratch_shapes=[pltpu.VMEM((2,) + win.shape, win.dtype), pltpu.VMEM(wout.shape, wout.dtype), pltpu.SemaphoreType.DMA((3,))]
        + [pltpu.SemaphoreType.DMA((2, 4)) for _ in range(4)])
    return pl.pallas_call(
        body, name="in_proj_gather", grid_spec=grid_spec,
        out_shape=[jax.ShapeDtypeStruct((rows, 4 * nq), BF16), jax.ShapeDtypeStruct((D, 4 * nq), win.dtype),
                   jax.ShapeDtypeStruct((4 * dq, D2), wout.dtype)],
        compiler_params=_params(("arbitrary", "arbitrary")),
    )(q_arr, hl, win, wout)


def _rs_to_sibling(gs, axes):
    n = len(gs)
    shapes = []
    for g, ax in zip(gs, axes):
        s = list(g.shape)
        s[ax] //= 8
        shapes.append(tuple(s))

    def body(*refs):
        g_refs, mine, landed = refs[:n], refs[n:2 * n], refs[2 * n:3 * n]
        bufs = refs[3 * n:4 * n]
        lsem, ssem, rsem = refs[4 * n:]
        x, y, c = _pos()
        cps = []
        for o in range(n):
            size = shapes[o][axes[o]]
            for j in range(4):
                rc = pltpu.make_async_remote_copy(
                    src_ref=_chunk(g_refs[o], axes[o], 2 * j + 1 - c, size), dst_ref=landed[o].at[j],
                    send_sem=ssem.at[o, j], recv_sem=rsem.at[o, j], device_id=(x, y, 1 - c), device_id_type=MESH_ID)
                rc.start()
                cps.append(rc)
        for o in range(n):
            size = shapes[o][axes[o]]
            for j in range(4):
                _bounce(_chunk(g_refs[o], axes[o], 2 * j + c, size), mine[o].at[j], bufs[o], lsem.at[o])
        for rc in cps:
            rc.wait()

    hbm = pl.BlockSpec(memory_space=pl.ANY)
    outs = [jax.ShapeDtypeStruct((4,) + s, g.dtype) for s, g in zip(shapes, gs)]
    res = pl.pallas_call(
        body, name="rs_to_sibling", in_specs=[hbm] * n, out_specs=[hbm] * (2 * n), out_shape=outs + outs,
        scratch_shapes=[pltpu.VMEM(s, g.dtype) for s, g in zip(shapes, gs)]
        + [pltpu.SemaphoreType.DMA((n,)), pltpu.SemaphoreType.DMA((n, 4)), pltpu.SemaphoreType.DMA((n, 4))],
        compiler_params=_params(),
    )(*gs)
    return res[:n], res[n:]


def _gather_block_side(block):
    r, n_cols = block.shape

    def copies(ins, outs, scr):
        ssem, rsem = scr[1], scr[2]
        x, y, c = _pos()
        return [pltpu.make_async_remote_copy(
            src_ref=ins[0], dst_ref=outs[0].at[k], send_sem=ssem.at[k], recv_sem=rsem.at[k],
            device_id=(_flip(x, (k >> 2) & 1), _flip(y, (k >> 1) & 1), _flip(c, k & 1)), device_id_type=MESH_ID)
            for k in range(1, 8)]

    def start(ins, outs, scr):
        for cp in copies(ins, outs, scr):
            cp.start()

    def finish(ins, outs, scr):
        _bounce(ins[0], outs[0].at[0], scr[0], scr[3].at[0])
        for cp in copies(ins, outs, scr):
            cp.wait()

    return dict(ins=[block], outs=[jax.ShapeDtypeStruct((8, r, n_cols), block.dtype)],
                scratch=[pltpu.VMEM((r, n_cols), block.dtype), pltpu.SemaphoreType.DMA((8,)), pltpu.SemaphoreType.DMA((8,)),
                         pltpu.SemaphoreType.DMA((1,))],
                start=start, finish=finish)


def _join_sides(a, b):
    na_i, na_o, na_s = len(a["ins"]), len(a["outs"]), len(a["scratch"])

    def run(which):
        def fn(ins, outs, scr):
            a[which](ins[:na_i], outs[:na_o], scr[:na_s])
            b[which](ins[na_i:], outs[na_o:], scr[na_s:])
        return fn

    return dict(ins=a["ins"] + b["ins"], outs=a["outs"] + b["outs"], scratch=a["scratch"] + b["scratch"],
                start=run("start"), finish=run("finish"))


def _rs_chips_side(parts):
    n = len(parts)

    def copies(p_refs, slots, scr):
        ssem, rsem = scr[n + 1], scr[n + 2]
        x, y, c = _pos()
        cps = []
        for o in range(n):
            for k in range(1, 4):
                px, py = _flip(x, (k >> 1) & 1), _flip(y, k & 1)
                cps.append(pltpu.make_async_remote_copy(
                    src_ref=p_refs[o].at[2 * px + py], dst_ref=slots[o].at[k], send_sem=ssem.at[o, k],
                    recv_sem=rsem.at[o, k], device_id=(px, py, c), device_id_type=MESH_ID))
        return cps

    def start(p_refs, slots, scr):
        for cp in copies(p_refs, slots, scr):
            cp.start()

    def finish(p_refs, slots, scr):
        x, y, _ = _pos()
        q = 2 * x + y
        for o in range(n):
            _bounce(p_refs[o].at[q], slots[o].at[0], scr[o], scr[n].at[o])
        for cp in copies(p_refs, slots, scr):
            cp.wait()

    return dict(
        ins=list(parts), outs=[jax.ShapeDtypeStruct(p.shape, p.dtype) for p in parts],
        scratch=[pltpu.VMEM(p.shape[1:], p.dtype) for p in parts]
        + [pltpu.SemaphoreType.DMA((n,)), pltpu.SemaphoreType.DMA((n, 4)), pltpu.SemaphoreType.DMA((n, 4))],
        start=start, finish=finish)


def _rs_share(rs, axes):
    n = len(rs)
    shapes = []
    for r, ax in zip(rs, axes):
        s = list(r.shape)
        s[ax] *= 2
        shapes.append(tuple(s))

    def body(*refs):
        r_refs, outs = refs[:n], refs[n:2 * n]
        bufs = refs[2 * n:3 * n]
        lsem, ssem, rsem = refs[3 * n:]
        x, y, c = _pos()
        cps = []
        for o in range(n):
            size = r_refs[o].shape[axes[o]]
            window = _chunk(outs[o], axes[o], c, size)
            rc = pltpu.make_async_remote_copy(src_ref=r_refs[o], dst_ref=window, send_sem=ssem.at[o], recv_sem=rsem.at[o],
                                              device_id=(x, y, 1 - c), device_id_type=MESH_ID)
            rc.start()
            cps.append(rc)
        for o in range(n):
            size = r_refs[o].shape[axes[o]]
            _bounce(r_refs[o], _chunk(outs[o], axes[o], c, size), bufs[o], lsem.at[o])
        for cp in cps:
            cp.wait()

    hbm = pl.BlockSpec(memory_space=pl.ANY)
    return pl.pallas_call(
        body, name="rs_share", in_specs=[hbm] * n, out_specs=[hbm] * n,
        out_shape=[jax.ShapeDtypeStruct(s, r.dtype) for s, r in zip(shapes, rs)],
        scratch_shapes=[pltpu.VMEM(r.shape, r.dtype) for r in rs] + [pltpu.SemaphoreType.DMA((n,)) for _ in range(3)],
        compiler_params=_params(),
    )(*rs)


def _rs_pair_sums(gs, axes):
    mine, landed = _rs_to_sibling(gs, axes)
    pair_sums = []
    for o, (mi, la) in enumerate(zip(mine, landed)):
        rows, cols = mi.shape[0] * mi.shape[1], mi.shape[2]
        s = _elementwise(lambda a, b: a.astype(F32) + b.astype(F32), [(mi.reshape(rows, cols), 0), (la.reshape(rows, cols), 0)],
                         [BF16], rows=rows, cols=cols, name=f"rs_pair_sum{o}")[0]
        pair_sums.append(s.reshape(mi.shape))
    return pair_sums


def _rs_finish(slots, axes):
    reduced = []
    for o, sl in enumerate(slots):
        rows, cols = sl.shape[1], sl.shape[2]
        flat = sl.reshape(4 * rows, cols)
        r = _elementwise(lambda a, b, c, d: (a.astype(F32) + b.astype(F32)) + (c.astype(F32) + d.astype(F32)),
                         [(flat, k * rows) for k in range(4)], [F32], rows=rows, cols=cols, name=f"rs_chip_sum{o}")[0]
        reduced.append(r)
    return _rs_share(reduced, axes)


def _norm_in(x, ctx, g, sc_l, sh_l, sc_c, sh_c, tr):
    L, D = x.shape
    T = ctx.shape[0]
    nx, nc = L // tr, T // tr

    def body(x_ref, c_ref, g_ref, scl, shl, scc, shc, o_ref):
        i = pl.program_id(0)

        def run(src, sc, sh):
            v = src[...]
            r = lax.rsqrt(jnp.mean(v * v, axis=-1, keepdims=True) + EPS)
            o_ref[...] = ((v * r * g_ref[...]) * (1.0 + sc[...]) + sh[...]).astype(o_ref.dtype)

        @pl.when(i < nx)
        def _():
            run(x_ref, scl, shl)

        @pl.when(i >= nx)
        def _():
            run(c_ref, scc, shc)

    vec = pl.BlockSpec((1, D), lambda i: (0, 0))
    return pl.pallas_call(
        body, name="norm_in", grid=(nx + nc,),
        in_specs=[pl.BlockSpec((tr, D), lambda i: (jnp.minimum(i, nx - 1), 0)),
                  pl.BlockSpec((tr, D), lambda i: (jnp.maximum(i - nx, 0), 0)), vec, vec, vec, vec, vec],
        out_specs=pl.BlockSpec((tr, D), lambda i: (i, 0)),
        out_shape=jax.ShapeDtypeStruct((L + T, D), BF16),
        compiler_params=_params(("arbitrary",)),
    )(x, ctx, g, sc_l, sh_l, sc_c, sh_c)


def _tmod(tl, row_w):
    assert row_w & (row_w - 1) == 0
    return lax.broadcasted_iota(jnp.int32, (tl, 1), 0) & (row_w - 1)


def _shift(z, k, tmod, row_w):
    tl = z.shape[0]
    rolled = pltpu.roll(z, k % tl, 0)
    mask = (tmod >= k) if k > 0 else (tmod < row_w + k)
    return jnp.where(mask, rolled, 0.0)


def _conv(z, w_ref, taps, left, tmod, row_w, lanes=slice(None)):
    out = None
    for j in range(taps):
        k = left - j
        term = (z if k == 0 else _shift(z, k, tmod, row_w)) * w_ref[j:j + 1, lanes]
        out = term if out is None else out + term
    return out


def _conv_bwd(dz, z, w_ref, taps, left, tmod, row_w, lanes=slice(None)):
    din = None
    dws = []
    for j in range(taps):
        k = left - j
        shifted = dz if k == 0 else _shift(dz, -k, tmod, row_w)
        term = shifted * w_ref[j:j + 1, lanes]
        din = term if din is None else din + term
        dws.append(jnp.sum(shifted * z, axis=0, keepdims=True))
    return din, dws


def _gate_matmul(xb16_ref, wd_ref, pre_scr, W, ng, gs):
    for g in range(ng):
        pg = jnp.dot(xb16_ref[:, g * gs:(g + 1) * gs], wd_ref[g], preferred_element_type=F32)
        pre_scr[:, g * gs:(g + 1) * gs] = pg[:, :gs]
        pre_scr[:, W + g * gs:W + (g + 1) * gs] = pg[:, gs:]


def _f32(ref, rows, lanes):
    return ref[rows, lanes].astype(F32)


def _sub_loop(tl, sub_r, W, fn):
    def chunk(ci, carry):
        r0 = pl.multiple_of(ci * sub_r, sub_r)
        for lb in range(W // LANES):
            fn(r0, lb * LANES)
        return carry

    lax.fori_loop(0, tl // sub_r, chunk, 0)


def _row_loop(tl, rev, step, init):
    nchunk = tl // SUBLANES

    def chunk(j, carry):
        jj = (nchunk - 1 - j) if rev else j
        c0 = pl.multiple_of(jj * SUBLANES, SUBLANES)
        for r in (range(SUBLANES - 1, -1, -1) if rev else range(SUBLANES)):
            carry = step(c0 + r, carry)
        return carry

    return lax.fori_loop(0, nchunk, chunk, init)


def _mix_fwd(P, d, h_init, wts, *, rows, row_off, row_w, tl, saved0=None, name):
    W = P.shape[1] // 6
    nt = rows // tl
    ob = row_off // tl
    rev = d == 1
    gs = min(LRU_GROUP, W)
    ng = W // gs
    wca, wcb, bcb = wts["wca"], wts["wcb"], wts["bcb"]
    wd, ba, bx, lam = wts["wd"][d], wts["ba"][d], wts["bx"][d], wts["lam"][d]

    def tile(i):
        return (nt - 1 - i) if rev else i

    def pcol(j):
        return pl.BlockSpec((tl, W), lambda i: (tile(i) + ob, j))

    vec = pl.BlockSpec((1, W), lambda i: (0, 0))
    taps = pl.BlockSpec((SUBLANES, W), lambda i: (0, 0))
    wd_spec = pl.BlockSpec(wd.shape, lambda i: (0, 0, 0))
    seq = pl.BlockSpec((tl, W), lambda i: (tile(i), 0))

    sub_r = min(row_w, tl)
    assert tl % sub_r == 0

    def body(*refs):
        if rev:
            (bl, cl, ul, gl, ql, ho, xb_r, xb16_r, wca_r, wd_r, ba_r, bx_r, lam_r, hin, hseq, cat, a_o, r_o, ig_o, m2_o,
             b_scr, pre_scr, carry, sp_scr) = refs
        else:
            (vl, wcb_r, bcb_r, wd_r, ba_r, bx_r, lam_r, hin, hseq, xb_r, xb16_r, a_o, r_o, ig_o, m2_o,
             b_scr, pre_scr, carry, sp_scr) = refs
        i = pl.program_id(0)

        @pl.when(i == 0)
        def _():
            carry[...] = hin[...]

        sp_scr[...] = _softplus(-lam_r[...])
        tmod = _tmod(sub_r, sub_r)

        def conv_in(r0, l0):
            rs, ls = pl.ds(r0, sub_r), pl.ds(l0, LANES)
            xb = _conv(_f32(vl, rs, ls), wcb_r, 4, 2, tmod, sub_r, ls) + bcb_r[:, ls]
            xb_r[rs, ls] = xb
            xb16_r[rs, ls] = xb.astype(BF16)

        def gates(r0, l0):
            rs, ls = pl.ds(r0, sub_r), pl.ds(l0, LANES)
            r, ig, a, m2 = _gates(pre_scr[rs, ls] + ba_r[:, ls], pre_scr[rs, pl.ds(W + l0, LANES)] + bx_r[:, ls],
                                  sp_scr[:, ls])
            a_o[rs, ls] = a
            r_o[rs, ls] = r.astype(r_o.dtype)
            ig_o[rs, ls] = ig.astype(ig_o.dtype)
            m2_o[rs, ls] = m2.astype(m2_o.dtype)
            m = jnp.where(m2 > 0.0, m2 * lax.rsqrt(m2), 0.0)
            b_scr[rs, ls] = m * (ig * xb_r[rs, ls])

        if not rev:
            _sub_loop(tl, sub_r, W, conv_in)
        _gate_matmul(xb16_r, wd_r, pre_scr, W, ng, gs)
        _sub_loop(tl, sub_r, W, gates)

        def step(t, h):
            h = a_o[pl.ds(t, 1), :] * h + b_scr[pl.ds(t, 1), :]
            hseq[pl.ds(t, 1), :] = h
            return h

        carry[...] = _row_loop(tl, rev, step, carry[...])

        if rev:
            def mix_out(r0, l0):
                rs, ls = pl.ds(r0, sub_r), pl.ds(l0, LANES)
                yb = (ho[rs, ls] + hseq[rs, ls]) * _silu(_f32(ql, rs, ls))
                ya = (_f32(bl, rs, ls) * _conv(_f32(cl, rs, ls) * _f32(ul, rs, ls), wca_r, 3, 1, tmod, sub_r, ls)
                      * _silu(_f32(gl, rs, ls)))
                cat[rs, ls] = ya.astype(cat.dtype)
                cat[rs, pl.ds(W + l0, LANES)] = yb.astype(cat.dtype)

            _sub_loop(tl, sub_r, W, mix_out)

    scratch = [pltpu.VMEM((tl, W), F32), pltpu.VMEM((tl, 2 * W), F32), pltpu.VMEM((1, W), F32), pltpu.VMEM((1, W), F32)]
    f32_seq = jax.ShapeDtypeStruct((rows, W), F32)
    kept_gates = [f32_seq] + [jax.ShapeDtypeStruct((rows, W), BF16)] * 3
    if rev:
        in_specs = [pcol(j) for j in (0, 1, 2, 3, 5)] + [seq, seq, seq, taps, wd_spec, vec, vec, vec, vec]
        args = [P] * 5 + [saved0["h"], saved0["xb"], saved0["xb16"], wca, wd, ba, bx, lam, h_init]
        out_specs = [seq, pl.BlockSpec((tl, 2 * W), lambda i: (tile(i), 0))] + [seq] * 4
        out_shape = [f32_seq, jax.ShapeDtypeStruct((rows, 2 * W), BF16)] + kept_gates
    else:
        in_specs = [pcol(4), taps, vec, wd_spec, vec, vec, vec, vec]
        args = [P, wcb, bcb, wd, ba, bx, lam, h_init]
        out_specs = [seq] * 7
        out_shape = [f32_seq, f32_seq, jax.ShapeDtypeStruct((rows, W), BF16)] + kept_gates
    res = pl.pallas_call(
        body, name=name, grid=(nt,), in_specs=in_specs, out_specs=out_specs, out_shape=out_shape,
        scratch_shapes=scratch, compiler_params=_params(("arbitrary",)),
    )(*args)
    gates = dict(zip(("a", "r", "ig", "m2"), res[-4:]))
    if rev:
        return res[0], res[1], gates
    return dict(h=res[0], xb=res[1], xb16=res[2]), gates


_BWD_SCRATCH = ("dyl", "g", "dp16", "sp", "dlf", "edge", "c")
_FWD_SAVED = ("xb", "xb16", "a", "r", "ig", "m2")


def _bwd_scratch(tl, W):
    shapes = {"dyl": pltpu.VMEM((tl, W), F32), "g": pltpu.VMEM((tl, W), F32), "dp16": pltpu.VMEM((tl, 2 * W), BF16),
              "sp": pltpu.VMEM((1, W), F32), "dlf": pltpu.VMEM((1, W), F32), "edge": pltpu.VMEM((1, W), F32),
              "c": pltpu.VMEM((1, W), F32)}
    return [shapes[n] for n in _BWD_SCRATCH]


def _lru_bwd_tile(d, dy_fn, hs_ref, wd_r, lam_r, scr, acc, first, tl, sub_r, ng, gs):
    dwd_ref, dba_ref, dbx_ref, dlam_ref = acc
    W = hs_ref.shape[1]
    assert gs % LANES == 0
    rev = d == 0
    lam = lam_r[...]
    scr["sp"][...] = _softplus(-lam)
    scr["dlf"][...] = -_sigmoid(-lam)

    @pl.when(first)
    def _():
        dwd_ref[...] = jnp.zeros_like(dwd_ref)
        dba_ref[...] = jnp.zeros_like(dba_ref)
        dbx_ref[...] = jnp.zeros_like(dbx_ref)
        dlam_ref[...] = jnp.zeros_like(dlam_ref)

    def state_grad(r0, l0):
        rs, ls = pl.ds(r0, sub_r), pl.ds(l0, LANES)
        scr["dyl"][rs, ls] = dy_fn(rs, ls)

    _sub_loop(tl, sub_r, W, state_grad)

    def step(t, c):
        g = scr["dyl"][pl.ds(t, 1), :] + c
        scr["g"][pl.ds(t, 1), :] = g
        return scr["a"][pl.ds(t, 1), :] * g

    scr["c"][...] = _row_loop(tl, rev, step, scr["c"][...])
    row = lax.broadcasted_iota(jnp.int32, (sub_r, 1), 0)

    def grads(r0, l0):
        rs, ls = pl.ds(r0, sub_r), pl.ds(l0, LANES)
        g, a, m2 = scr["g"][rs, ls], scr["a"][rs, ls], _f32(scr["m2"], rs, ls)
        r, ig, xb = _f32(scr["r"], rs, ls), _f32(scr["ig"], rs, ls), scr["xb"][rs, ls]
        h = hs_ref[rs, ls]
        if d == 0:
            e0 = pl.multiple_of(jnp.maximum(r0 - SUBLANES, 0), SUBLANES)
            edge = jnp.where(r0 == 0, scr["edge"][:, ls], hs_ref[pl.ds(e0, SUBLANES), ls][SUBLANES - 1:, :])
            hprev = jnp.where(row == 0, edge, pltpu.roll(h, 1, 0))
        else:
            e0 = pl.multiple_of(jnp.minimum(r0 + sub_r, tl - SUBLANES), SUBLANES)
            edge = jnp.where(r0 == tl - sub_r, scr["edge"][:, ls], hs_ref[pl.ds(e0, SUBLANES), ls][:1, :])
            hprev = jnp.where(row == sub_r - 1, edge, pltpu.roll(h, sub_r - 1, 0))
        rsq = lax.rsqrt(m2)
        gm = g * (m2 * rsq)
        d_la = (g * hprev) * a - (g * (ig * xb)) * ((1.0 - m2) * rsq)
        d_pr = d_la * ((-LRU_C) * scr["sp"][:, ls]) * (r * (1.0 - r))
        d_pi = (gm * xb) * (ig * (1.0 - ig))
        scr["dyl"][rs, ls] = gm * ig
        dlam_ref[:, ls] += jnp.sum(d_la * ((-LRU_C) * r), axis=0, keepdims=True) * scr["dlf"][:, ls]
        dba_ref[:, ls] += jnp.sum(d_pr, axis=0, keepdims=True)
        dbx_ref[:, ls] += jnp.sum(d_pi, axis=0, keepdims=True)
        gi, off = divmod(l0, gs)
        scr["dp16"][rs, pl.ds(gi * 2 * gs + off, LANES)] = d_pr.astype(BF16)
        scr["dp16"][rs, pl.ds(gi * 2 * gs + gs + off, LANES)] = d_pi.astype(BF16)

    _sub_loop(tl, sub_r, W, grads)
    for gi in range(ng):
        dp = scr["dp16"][:, gi * 2 * gs:(gi + 1) * 2 * gs]
        scr["g"][:, gi * gs:(gi + 1) * gs] = lax.dot_general(dp, wd_r[gi], (((1,), (1,)), ((), ())),
                                                             preferred_element_type=F32)
        dwd_ref[gi] += lax.dot_general(scr["xb16"][:, gi * gs:(gi + 1) * gs], dp, (((0,), (0,)), ((), ())),
                                       preferred_element_type=F32)


def _edge_block(h, tl, nt, d):
    W = h.shape[1]
    per = tl // SUBLANES
    if d == 0:
        return pl.BlockSpec((SUBLANES, W), lambda i: (jnp.maximum((nt - 1 - i) * per - 1, 0), 0))
    return pl.BlockSpec((SUBLANES, W), lambda i: (jnp.minimum((i + 1) * per, nt * per - 1), 0))


def _mix_bwd0(P, dcat, saved0, gates0, h_init, c_init, wts, *, rows, row_off, row_w, tl, name):
    W = P.shape[1] // 6
    nt = rows // tl
    ob = row_off // tl
    gs = min(LRU_GROUP, W)
    ng = W // gs
    wd, lam = wts["wd"][0], wts["lam"][0]
    h0s = saved0["h"]
    kept = [saved0["xb"], saved0["xb16"]] + [gates0[n] for n in ("a", "r", "ig", "m2")]

    def tile(i):
        return nt - 1 - i

    vec = pl.BlockSpec((1, W), lambda i: (0, 0))
    wd_spec = pl.BlockSpec(wd.shape, lambda i: (0, 0, 0))
    seq = pl.BlockSpec((tl, W), lambda i: (tile(i), 0))

    sub_r = min(row_w, tl)
    assert tl % sub_r == 0

    def body(ql, dyb, hs, hedge8, xb_r, xb16_r, a_r, r_r, ig_r, m2_r, wd_r, lam_r, hin, cin,
             dxb_o, dwd_o, dba_o, dbx_o, dlam_o, cfin, *scratch):
        scr = dict(zip(_BWD_SCRATCH, scratch, strict=True))
        scr.update(zip(_FWD_SAVED, (xb_r, xb16_r, a_r, r_r, ig_r, m2_r), strict=True))
        i = pl.program_id(0)

        @pl.when(i == 0)
        def _():
            scr["c"][...] = cin[...]

        scr["edge"][...] = jnp.where(i == nt - 1, hin[...], hedge8[SUBLANES - 1:SUBLANES, :])
        _lru_bwd_tile(0, lambda rs, ls: _f32(dyb, rs, ls) * _silu(_f32(ql, rs, ls)), hs, wd_r, lam_r, scr,
                      (dwd_o, dba_o, dbx_o, dlam_o), i == 0, tl, sub_r, ng, gs)
        dxb_o[...] = scr["dyl"][...] + scr["g"][...]
        cfin[...] = scr["c"][...]

    return pl.pallas_call(
        body, name=name, grid=(nt,),
        in_specs=[pl.BlockSpec((tl, W), lambda i: (tile(i) + ob, 5)), pl.BlockSpec((tl, W), lambda i: (tile(i), 1)), seq,
                  _edge_block(h0s, tl, nt, 0)] + [seq] * 6 + [wd_spec, vec, vec, vec],
        out_specs=[seq, wd_spec, vec, vec, vec, vec],
        out_shape=[jax.ShapeDtypeStruct((rows, W), F32), jax.ShapeDtypeStruct(wd.shape, F32)]
        + [jax.ShapeDtypeStruct((1, W), F32)] * 4,
        scratch_shapes=_bwd_scratch(tl, W),
        compiler_params=_params(("arbitrary",)),
    )(P, dcat, h0s, h0s, *kept, wd, lam, h_init, c_init)


def _mix_bwd1(P, dcat, saved0, h1s, gates1, dxb0, h_init, c_init, wts, *, rows, row_off, row_w, tl, name,
              dp_rows=None, dp_off=0, dp_into=None):
    dp_rows = rows if dp_rows is None else dp_rows
    dpb = dp_off // tl
    W = P.shape[1] // 6
    nt = rows // tl
    ob = row_off // tl
    gs = min(LRU_GROUP, W)
    ng = W // gs
    wca, wcb = wts["wca"], wts["wcb"]
    wd, lam = wts["wd"][1], wts["lam"][1]
    h0s = saved0["h"]
    kept = [saved0["xb"], saved0["xb16"]] + [gates1[n] for n in ("a", "r", "ig", "m2")]

    vec = pl.BlockSpec((1, W), lambda i: (0, 0))
    taps = pl.BlockSpec((SUBLANES, W), lambda i: (0, 0))
    wd_spec = pl.BlockSpec(wd.shape, lambda i: (0, 0, 0))
    seq = pl.BlockSpec((tl, W), lambda i: (i, 0))

    sub_r = min(row_w, tl)
    assert tl % sub_r == 0

    def body(*refs):
        if dp_into is not None:
            refs = refs[1:]
        (bl, cl, ul, gl, vl, ql, dya, dyb, h0, h1, hedge8, dx0, xb_r, xb16_r, a_r, r_r, ig_r, m2_r, wca_r, wcb_r,
         wd_r, lam_r, hin, cin, dp_o, dwd_o, dba_o, dbx_o, dlam_o, dwca_o, dwcb_o, dbcb_o, cfin, *scratch) = refs
        scr = dict(zip(_BWD_SCRATCH, scratch, strict=True))
        scr.update(zip(_FWD_SAVED, (xb_r, xb16_r, a_r, r_r, ig_r, m2_r), strict=True))
        i = pl.program_id(0)

        @pl.when(i == 0)
        def _():
            scr["c"][...] = cin[...]
            dwca_o[...] = jnp.zeros_like(dwca_o)
            dwcb_o[...] = jnp.zeros_like(dwcb_o)
            dbcb_o[...] = jnp.zeros_like(dbcb_o)

        scr["edge"][...] = jnp.where(i == nt - 1, hin[...], hedge8[0:1, :])
        _lru_bwd_tile(1, lambda rs, ls: _f32(dyb, rs, ls) * _silu(_f32(ql, rs, ls)), h1, wd_r, lam_r, scr,
                      (dwd_o, dba_o, dbx_o, dlam_o), i == 0, tl, sub_r, ng, gs)
        cfin[...] = scr["c"][...]
        tmod = _tmod(sub_r, sub_r)

        def lru_side(r0, l0):
            rs, ls = pl.ds(r0, sub_r), pl.ds(l0, LANES)
            dxb = dx0[rs, ls] + scr["dyl"][rs, ls] + scr["g"][rs, ls]
            dv, dwb = _conv_bwd(dxb, _f32(vl, rs, ls), wcb_r, 4, 2, tmod, sub_r, ls)
            for j in range(4):
                dwcb_o[j:j + 1, ls] += dwb[j]
            dbcb_o[:, ls] += jnp.sum(dxb, axis=0, keepdims=True)
            q = _f32(ql, rs, ls)
            sq = _sigmoid(q)
            dq = _f32(dyb, rs, ls) * (h0[rs, ls] + h1[rs, ls]) * (sq * (1.0 + q * (1.0 - sq)))
            dp_o[rs, pl.ds(4 * W + l0, LANES)] = dv.astype(dp_o.dtype)
            dp_o[rs, pl.ds(5 * W + l0, LANES)] = dq.astype(dp_o.dtype)

        def conv_side(r0, l0):
            rs, ls = pl.ds(r0, sub_r), pl.ds(l0, LANES)
            b_, c_, u_, g_ = _f32(bl, rs, ls), _f32(cl, rs, ls), _f32(ul, rs, ls), _f32(gl, rs, ls)
            z = c_ * u_
            cz = _conv(z, wca_r, 3, 1, tmod, sub_r, ls)
            sgm = _sigmoid(g_)
            sg = g_ * sgm
            da = _f32(dya, rs, ls)
            dz, dwa = _conv_bwd(da * b_ * sg, z, wca_r, 3, 1, tmod, sub_r, ls)
            for j in range(3):
                dwca_o[j:j + 1, ls] += dwa[j]
            parts = (da * cz * sg, dz * u_, dz * c_, da * b_ * cz * (sgm * (1.0 + g_ * (1.0 - sgm))))
            for k, val in enumerate(parts):
                dp_o[rs, pl.ds(k * W + l0, LANES)] = val.astype(dp_o.dtype)

        _sub_loop(tl, sub_r, W, lru_side)
        _sub_loop(tl, sub_r, W, conv_side)

    def pcol(j):
        return pl.BlockSpec((tl, W), lambda i: (i + ob, j))

    prev = [] if dp_into is None else [dp_into]
    return pl.pallas_call(
        body, name=name, grid=(nt,), input_output_aliases={} if dp_into is None else {0: 0},
        in_specs=[pl.BlockSpec(memory_space=pl.ANY)] * len(prev) + [pcol(j) for j in range(6)]
        + [pl.BlockSpec((tl, W), lambda i: (i, 0)), pl.BlockSpec((tl, W), lambda i: (i, 1)), seq, seq,
           _edge_block(h1s, tl, nt, 1), seq] + [seq] * 6 + [taps, taps, wd_spec, vec, vec, vec],
        out_specs=[pl.BlockSpec((tl, 6 * W), lambda i: (i + dpb, 0)), wd_spec, vec, vec, vec, taps, taps, vec, vec],
        out_shape=[jax.ShapeDtypeStruct((dp_rows, 6 * W), BF16), jax.ShapeDtypeStruct(wd.shape, F32)]
        + [jax.ShapeDtypeStruct((1, W), F32)] * 3
        + [jax.ShapeDtypeStruct((SUBLANES, W), F32)] * 2 + [jax.ShapeDtypeStruct((1, W), F32)] * 2,
        scratch_shapes=_bwd_scratch(tl, W),
        compiler_params=_params(("arbitrary",)),
    )(*prev, *([P] * 6), dcat, dcat, h0s, h1s, h1s, dxb0, *kept, wca, wcb, wd, lam, h_init, c_init)


def _loss_head(out, x, tgt, gt, fg, tr):
    L, D = x.shape

    def body(o_ref, x_ref, t_ref, gt_ref, fg_ref, dn_o, do_o, dfg_o, dgt_o, loss_o):
        i = pl.program_id(0)

        @pl.when(i == 0)
        def _():
            dfg_o[...] = jnp.zeros_like(dfg_o)
            dgt_o[...] = jnp.zeros_like(dgt_o)
            loss_o[...] = jnp.zeros_like(loss_o)

        o = o_ref[...].astype(F32)
        gt_v = gt_ref[...]
        fg_v = fg_ref[...]
        n = x_ref[...] + gt_v * o
        r = lax.rsqrt(jnp.mean(n * n, axis=-1, keepdims=True) + EPS)
        nr = n * r
        e = nr * fg_v - t_ref[...]
        loss_o[...] += 0.5 * jnp.sum(jnp.mean(e * e, axis=-1, keepdims=True))
        dy = e * (1.0 / D)
        dfg_o[...] += jnp.sum(dy * nr, axis=0, keepdims=True)
        qv = dy * fg_v
        dn = r * (qv - nr * jnp.mean(qv * nr, axis=-1, keepdims=True))
        dgt_o[...] += jnp.sum(dn * o, axis=0, keepdims=True)
        dn_o[...] = dn.astype(dn_o.dtype)
        do_o[...] = (dn * gt_v).astype(do_o.dtype)

    blk = pl.BlockSpec((tr, D), lambda i: (i, 0))
    vec = pl.BlockSpec((1, D), lambda i: (0, 0))
    return pl.pallas_call(
        body, name="loss_head", grid=(L // tr,), in_specs=[blk, blk, blk, vec, vec],
        out_specs=[blk, blk, vec, vec, pl.BlockSpec((SUBLANES, LANES), lambda i: (0, 0))],
        out_shape=[jax.ShapeDtypeStruct((L, D), BF16), jax.ShapeDtypeStruct((L, D), BF16),
                   jax.ShapeDtypeStruct((1, D), F32), jax.ShapeDtypeStruct((1, D), F32),
                   jax.ShapeDtypeStruct((SUBLANES, LANES), F32)],
        compiler_params=_params(("arbitrary",)),
    )(out, x, tgt, gt, fg)


def _norm_bwd(dhl, x, dn, g, sc, tr, name):
    L, D = x.shape
    with_x = dn is not None

    def body(*refs):
        if with_x:
            d_ref, x_ref, dn_ref, g_ref, sc_ref, gx_o, dsh_o, dsc_o, dg_o = refs
        else:
            d_ref, x_ref, g_ref, sc_ref, dsh_o, dsc_o, dg_o = refs
        i = pl.program_id(0)

        @pl.when(i == 0)
        def _():
            dsh_o[...] = jnp.zeros_like(dsh_o)
            dsc_o[...] = jnp.zeros_like(dsc_o)
            dg_o[...] = jnp.zeros_like(dg_o)

        d = d_ref[...].astype(F32)
        xv = x_ref[...]
        g_v = g_ref[...]
        r = lax.rsqrt(jnp.mean(xv * xv, axis=-1, keepdims=True) + EPS)
        xr = xv * r
        dsh_o[...] += jnp.sum(d, axis=0, keepdims=True)
        dsc_o[...] += jnp.sum(d * (xr * g_v), axis=0, keepdims=True)
        dxn = d * (1.0 + sc_ref[...])
        dg_o[...] += jnp.sum(dxn * xr, axis=0, keepdims=True)
        if with_x:
            qv = dxn * g_v
            gx_o[...] = r * (qv - xr * jnp.mean(qv * xr, axis=-1, keepdims=True)) + dn_ref[...].astype(F32)

    blk = pl.BlockSpec((tr, D), lambda i: (i, 0))
    vec = pl.BlockSpec((1, D), lambda i: (0, 0))
    vshape = jax.ShapeDtypeStruct((1, D), F32)
    res = pl.pallas_call(
        body, name=name, grid=(L // tr,),
        in_specs=[blk, blk] + ([blk] if with_x else []) + [vec, vec],
        out_specs=([blk] if with_x else []) + [vec, vec, vec],
        out_shape=([jax.ShapeDtypeStruct((L, D), F32)] if with_x else []) + [vshape] * 3,
        compiler_params=_params(("arbitrary",)),
    )(*([dhl, x] + ([dn] if with_x else []) + [g, sc]))
    return res if with_x else [None] + list(res)


def _pack_blockdiag(wa, wx, gs):
    H, hd, _ = wa.shape
    hp = gs // hd
    ng = H // hp
    eye = jnp.eye(hp, dtype=wa.dtype)

    def bd(w):
        return jnp.einsum("gpij,pq->gpiqj", w.reshape(ng, hp, hd, hd), eye).reshape(ng, gs, gs)

    return jnp.concatenate([bd(wa), bd(wx)], axis=-1).astype(BF16)


def _unpack_blockdiag(dwd, H, hd, gs):
    hp = gs // hd
    ng = H // hp
    eye = jnp.eye(hp, dtype=dwd.dtype)

    def diag(dm):
        return jnp.einsum("gpiqj,pq->gpij", dm.reshape(ng, hp, hd, hp, hd), eye).reshape(H, hd, hd)

    return diag(dwd[:, :, :gs]), diag(dwd[:, :, gs:])


def kernel(x, c, ctx, c_ctx, norm_g, w_ada, b_ada, w_in, w_conv_a, w_conv_b, b_conv_b, lru_wa, lru_ba, lru_wx, lru_bx, lru_lambda, w_out, final_g, loss_target, m_c_ctx, m_norm_g, m_w_ada, m_b_ada, m_w_in, m_w_conv_a, m_w_conv_b, m_b_conv_b, m_lru_wa, m_lru_ba, m_lru_wx, m_lru_bx, m_lru_lambda, m_w_out, m_final_g, v_c_ctx, v_norm_g, v_w_ada, v_b_ada, v_w_in, v_w_conv_a, v_w_conv_b, v_b_conv_b, v_lru_wa, v_lru_ba, v_lru_wx, v_lru_bx, v_lru_lambda, v_w_out, v_final_g):
    xi, yi, ci = _pos()
    me = 4 * xi + 2 * yi + ci
    q = 2 * xi + yi
    first_core = (ci == 0).astype(F32)

    L, D = x.shape[1], x.shape[2]
    T = ctx.shape[1]
    W = D // 2
    Wq = W // 4
    H, hd = lru_wa.shape[2], lru_wa.shape[3]
    gs = min(LRU_GROUP, W)
    nq = w_ada.shape[2]
    tl = min(256, T, L)
    tr = min(256, T, L)
    x2, ctx2, tgt2 = x[0], ctx[0], loss_target[0]

    def place(shard, full_cols):
        z = jnp.zeros((shard.shape[0], full_cols), F32)
        return lax.dynamic_update_slice(z, shard * first_core, (0, q * shard.shape[1]))

    c_rows = lax.dynamic_update_slice(jnp.zeros((8, D), F32), c, (me, 0))
    small_in = [c_rows, place(w_conv_a[0], W), place(w_conv_b[0], W), place(lru_ba[0], W), place(lru_bx[0], W),
                place(lru_lambda[0], W)]
    small_shapes = [a.shape for a in small_in]
    gathered = _allreduce8(_pack(small_in, 8 * SUBLANES), "gather_small")
    c_all, wca, wcb, ba_all, bx_all, lam_all = _unpack(gathered, small_shapes)

    s_rows = jnp.concatenate([c_all, c_ctx[None, :], jnp.zeros((7, D), F32)], axis=0)
    mod_part = _matmul(s_rows, w_ada[0], a_act="silu", bias=lax.dynamic_slice(b_ada, (0, q * nq), (1, nq)),
                       tm=16, tn=nq, tk=512, name="ada_fwd")
    mod_all = _allreduce8(_pack([place(mod_part, 4 * nq)], 8 * SUBLANES), "gather_mod")
    mod_all = _unpack(mod_all, [(16, 4 * nq)])[0]
    mod_l = lax.dynamic_slice(mod_all, (me, 0), (1, 3 * D))
    mod_c = mod_all[8:9]
    sh_l, sc_l, gt_l = mod_l[:, :D], mod_l[:, D:2 * D], mod_l[:, 2 * D:]
    sh_c, sc_c = mod_c[:, :D], mod_c[:, D:2 * D]

    pad_taps = lambda w: jnp.pad(w, ((0, SUBLANES - w.shape[0]), (0, 0)))
    wts = {
        "wca": pad_taps(wca), "wcb": pad_taps(wcb), "bcb": b_conv_b,
        "wd": [_pack_blockdiag(lru_wa[0, d], lru_wx[0, d], gs) for d in range(2)],
        "ba": [ba_all[d:d + 1] for d in range(2)], "bx": [bx_all[d:d + 1] for d in range(2)],
        "lam": [lam_all[d:d + 1] for d in range(2)],
    }

    hl = _norm_in(x2, ctx2, norm_g, sc_l, sh_l, sc_c, sh_c, tr)
    p_lat, win_full, wout_full = _in_proj_gather(hl, w_in[0].astype(BF16), w_out[0].astype(BF16),
                                                 jnp.reshape(q, (1,)).astype(jnp.int32), rows=L, tm=min(1024, L))
    p_ctx = _matmul(hl, win_full, a_rows=T, a_off=L, tm=T, tn=1536, tk=D, out_dtype=BF16, name="in_proj_ctx")
    zero_w = jnp.zeros((1, W), F32)
    ctx0, cgates0 = _mix_fwd(p_ctx, 0, zero_w, wts, rows=T, row_off=0, row_w=T, tl=tl, name="ctx_fwd0")
    c1s, _, cgates1 = _mix_fwd(p_ctx, 1, zero_w, wts, rows=T, row_off=0, row_w=T, tl=tl, saved0=ctx0, name="ctx_fwd1")
    h0_init, h1_init = ctx0["h"][T - 1:T], c1s[0:1]
    tl_fwd = 2 * tl if L % (2 * tl) == 0 else tl
    lat0, gates0 = _mix_fwd(p_lat, 0, h0_init, wts, rows=L, row_off=0, row_w=GRID_W, tl=tl_fwd, name="mix_fwd0")
    h1s, cat, gates1 = _mix_fwd(p_lat, 1, h1_init, wts, rows=L, row_off=0, row_w=GRID_W, tl=tl_fwd, saved0=lat0,
                                name="mix_fwd1")
    out = _matmul(cat, wout_full, tm=512, tn=D, tk=2 * W, out_dtype=BF16, name="out_proj")
    dn, dout, dfg, dgt, loss_blk = _loss_head(out, x2, tgt2, gt_l, final_g[None, :], tr)

    dcat = _matmul(dout, wout_full, tb=True, tm=512, tn=2 * W, tk=D, out_dtype=BF16, name="out_proj_bwd")
    gw_out = _matmul(cat, dout, ta=True, tm=1024, tn=D, tk=2048, out_dtype=BF16, name="w_out_grad")
    dxb0, dwd0, dba0, dbx0, dlam0, ch0 = _mix_bwd0(p_lat, dcat, lat0, gates0, h0_init, zero_w, wts, rows=L, row_off=0,
                                                   row_w=GRID_W, tl=tl, name="mix_bwd0")
    dp_lat, dwd1, dba1, dbx1, dlam1, dwca, dwcb, dbcb, ch1 = _mix_bwd1(
        p_lat, dcat, lat0, h1s, gates1, dxb0, h1_init, zero_w, wts, rows=L, row_off=0, row_w=GRID_W, tl=tl,
        name="mix_bwd1", dp_rows=L + T)
    zero_cat = jnp.zeros((T, 2 * W), BF16)
    cxb0, cwd0, cba0, cbx0, clam0, _ = _mix_bwd0(p_ctx, zero_cat, ctx0, cgates0, zero_w, ch0, wts, rows=T, row_off=0,
                                                 row_w=T, tl=tl, name="ctx_bwd0")
    dp, cwd1, cba1, cbx1, clam1, cwca, cwcb, cbcb, _ = _mix_bwd1(
        p_ctx, zero_cat, ctx0, c1s, cgates1, cxb0, zero_w, ch1, wts, rows=T, row_off=0, row_w=T, tl=tl, name="ctx_bwd1",
        dp_rows=L + T, dp_off=L, dp_into=dp_lat)

    gw_in = _matmul(hl, dp, ta=True, tm=1024, tn=1536, tk=2816, out_dtype=BF16, name="w_in_grad")
    rs_axes = [1, 0]
    pair_sums = _rs_pair_sums([gw_in, gw_out], rs_axes)
    dhc = _matmul(dp, win_full, tb=True, a_rows=T, a_off=L, tm=T, tn=D, tk=3072, name="in_proj_bwd_ctx")
    _, dsh_c, dsc_c, dng_c = _norm_bwd(dhc, ctx2, None, norm_g, sc_c, tr, "norm_bwd_ctx")
    zeros_d = jnp.zeros((1, D), F32)
    dmod_c = jnp.concatenate([dsh_c, dsc_c, zeros_d], axis=1)
    dhl, (*rs_slots, dmod_c_all) = _matmul(
        dp, win_full, tb=True, a_rows=L, tm=512, tn=D, tk=3072, out_dtype=BF16, name="in_proj_bwd",
        side=_join_sides(_rs_chips_side(pair_sums), _gather_block_side(jnp.pad(dmod_c, ((0, SUBLANES - 1), (0, 0))))))
    gx, dsh_l, dsc_l, dng_l = _norm_bwd(dhl, x2, dn, norm_g, sc_l, tr, "norm_bwd")
    gc_rows = _matmul(lax.dynamic_slice(dmod_c_all.reshape(8 * SUBLANES, 3 * D), (0, q * nq), (8 * SUBLANES, nq)), w_ada[0],
                      tb=True, dsilu_mul=c_ctx[None, :], tm=8 * SUBLANES, tn=D, tk=512, name="c_ctx_grad")
    gc_part = jnp.sum(gc_rows, axis=0, keepdims=True) * first_core

    g_in_shard, g_out_shard = _rs_finish(rs_slots, rs_axes)

    dwa0, dwx0 = _unpack_blockdiag(dwd0 + cwd0, H, hd, gs)
    dwa1, dwx1 = _unpack_blockdiag(dwd1 + cwd1, H, hd, gs)
    dmod_l = jnp.concatenate([dsh_l, dsc_l, dgt], axis=1)
    small_g = [
        lax.dynamic_update_slice(jnp.zeros((8, 3 * D), F32), dmod_l, (me, 0)), dmod_c,
        dfg, dng_l + dng_c, (dwca + cwca)[:3], (dwcb + cwcb)[:4], dbcb + cbcb,
        jnp.stack([dwa0, dwa1]), jnp.stack([dwx0, dwx1]),
        jnp.concatenate([dba0 + cba0, dba1 + cba1], axis=0), jnp.concatenate([dbx0 + cbx0, dbx1 + cbx1], axis=0),
        jnp.concatenate([dlam0 + clam0, dlam1 + clam1], axis=0), loss_blk[0:1, 0:1], gc_part,
    ]
    g_shapes = [a.shape for a in small_g]
    (g_rows, g_modc, g_fg, g_ng, g_wca, g_wcb, g_bcb, g_wa, g_wx, g_ba, g_bx, g_lam, loss_sum, g_c_ctx) = _unpack(
        _allreduce8_two_level(_pack(small_g, 8 * SUBLANES), "reduce_small"), g_shapes)

    g_mod = jnp.concatenate([g_rows, g_modc, jnp.zeros((7, 3 * D), F32)], axis=0)
    g_mod_q = lax.dynamic_slice(g_mod, (0, q * nq), (16, nq))
    g_w_ada = _matmul(s_rows, g_mod_q, ta=True, a_act="silu", tm=1024, tn=nq, tk=16, name="w_ada_grad")
    g_b_ada = jnp.sum(g_mod[:9], axis=0, keepdims=True)

    def shard_cols(a, width):
        return lax.dynamic_slice(a, (0, q * width), (a.shape[0], width))

    grads = {
        "c_ctx": g_c_ctx, "norm_g": g_ng, "b_ada": g_b_ada,
        "w_conv_a": shard_cols(g_wca, Wq)[None], "w_conv_b": shard_cols(g_wcb, Wq)[None], "b_conv_b": g_bcb,
        "lru_wa": g_wa[None], "lru_ba": shard_cols(g_ba, Wq)[None], "lru_wx": g_wx[None],
        "lru_bx": shard_cols(g_bx, Wq)[None], "lru_lambda": shard_cols(g_lam, Wq)[None], "final_g": g_fg[0],
    }
    small_names = list(grads)
    given = dict(c_ctx=(c_ctx, m_c_ctx, v_c_ctx), norm_g=(norm_g, m_norm_g, v_norm_g), b_ada=(b_ada, m_b_ada, v_b_ada),
                 w_conv_a=(w_conv_a, m_w_conv_a, v_w_conv_a), w_conv_b=(w_conv_b, m_w_conv_b, v_w_conv_b),
                 b_conv_b=(b_conv_b, m_b_conv_b, v_b_conv_b), lru_wa=(lru_wa, m_lru_wa, v_lru_wa),
                 lru_ba=(lru_ba, m_lru_ba, v_lru_ba), lru_wx=(lru_wx, m_lru_wx, v_lru_wx),
                 lru_bx=(lru_bx, m_lru_bx, v_lru_bx), lru_lambda=(lru_lambda, m_lru_lambda, v_lru_lambda),
                 final_g=(final_g, m_final_g, v_final_g))
    def rows2d(a):
        return a.reshape(-1, a.shape[-1])

    grads = {n: grads[n].reshape(given[n][0].shape) for n in small_names}
    quads = [tuple(rows2d(a) for a in (given[n][0], grads[n], given[n][1], given[n][2])) for n in small_names]
    updated = _adam_many(quads, "adam_small")
    delta_s, newm_s, newv_s = ({n: u[j].reshape(given[n][0].shape) for n, u in zip(small_names, updated)} for j in range(3))

    big = {"w_ada": (w_ada, g_w_ada, m_w_ada, v_w_ada), "w_in": (w_in, g_in_shard, m_w_in, v_w_in),
           "w_out": (w_out, g_out_shard, m_w_out, v_w_out)}
    delta_b, newm_b, newv_b = {}, {}, {}
    for n, (w, g, m, v) in big.items():
        d_, m_, v_, *echo = _adam(w[0], g, m[0], v[0], "adam_" + n, echo_g=n != "w_ada")
        grads[n] = (echo[0] if echo else g)[None]
        delta_b[n], newm_b[n], newv_b[n] = d_[None], m_[None], v_[None]

    loss = loss_sum[0, 0]
    order = ["c_ctx", "norm_g", "w_ada", "b_ada", "w_in", "w_conv_a", "w_conv_b", "b_conv_b", "lru_wa", "lru_ba",
             "lru_wx", "lru_bx", "lru_lambda", "w_out", "final_g"]
    delta = {**delta_s, **delta_b}
    newm = {**newm_s, **newm_b}
    newv = {**newv_s, **newv_b}
    return (loss, gx[None], *[grads[n] for n in order], *[delta[n] for n in order], *[newm[n] for n in order],
            *[newv[n] for n in order])
```

```python
import functools

import jax
import jax.numpy as jnp
from jax import lax
from jax.experimental import pallas as pl
from jax.experimental.pallas import tpu as pltpu

F32 = jnp.float32
BF16 = jnp.bfloat16
MESH_ID = pl.DeviceIdType.MESH

EPS = 1e-6
LRU_C = 8.0
GRID_W = 64
ADAM_LR = 0.001
ADAM_B1 = 0.9
ADAM_B2 = 0.999
ADAM_EPS = 1e-08
ADAM_WD = 0.01
ADAM_STEP = 10

LANES = 128
SUBLANES = 8
PACK_COLS = 1024
VMEM_LIMIT = 56 * 2**20
LRU_GROUP = 256


def _params(sem=None):
    return pltpu.CompilerParams(vmem_limit_bytes=VMEM_LIMIT, dimension_semantics=sem)


def _pick(dim, pref, quantum=LANES):
    if dim <= pref:
        return dim
    best = None
    for t in range(quantum, pref + 1, quantum):
        if dim % t == 0:
            best = t
    assert best is not None, (dim, pref)
    return best


def _pos():
    return lax.axis_index("x"), lax.axis_index("y"), lax.axis_index("c")


def _flip(v, bit):
    return 1 - v if bit else v


def _sigmoid(v):
    return 0.5 * jnp.tanh(0.5 * v) + 0.5


def _silu(v):
    return v * _sigmoid(v)


def _dsilu(v):
    s = _sigmoid(v)
    return s * (1.0 + v * (1.0 - s))


def _gates(pre_r, pre_i, sp):
    r = _sigmoid(pre_r)
    ig = _sigmoid(pre_i)
    e = LRU_C * r * sp
    w = jnp.tanh(e)
    return r, ig, jnp.exp(-e), (2.0 * w) * pl.reciprocal(1.0 + w, approx=True)


def _softplus(z):
    return jnp.maximum(z, 0.0) + jnp.log1p(jnp.exp(-jnp.abs(z)))


def _matmul(a, b, *, ta=False, tb=False, tm=512, tn=512, tk=512, out_dtype=F32, name,
            a_rows=None, a_off=0, a_act=None, bias=None, dsilu_mul=None, side=None):
    rows_a = a.shape[0] if a_rows is None else a_rows
    if ta:
        K, M = rows_a, a.shape[1]
    else:
        M, K = rows_a, a.shape[1]
    N = b.shape[0] if tb else b.shape[1]
    tm, tn, tk = _pick(M, tm, SUBLANES), _pick(N, tn), _pick(K, tk)
    t_rows = tk if ta else tm
    assert a_off % t_rows == 0
    nk = K // tk
    gi, gj = M // tm, N // tn
    off_blocks = a_off // t_rows
    dims = (((0 if ta else 1,), (1 if tb else 0,)), ((), ()))
    extras = [e for e in (bias, dsilu_mul) if e is not None]
    n_sin = len(side["ins"]) if side else 0
    n_sout = len(side["outs"]) if side else 0

    def body(a_ref, b_ref, *rest):
        rest = list(rest)
        bias_ref = rest.pop(0) if bias is not None else None
        dsm_ref = rest.pop(0) if dsilu_mul is not None else None
        side_in = [rest.pop(0) for _ in range(n_sin)]
        o_ref = rest.pop(0)
        side_out = [rest.pop(0) for _ in range(n_sout)]
        acc_ref = rest.pop(0) if nk > 1 else None
        side_scr = rest
        i, j, k = pl.program_id(0), pl.program_id(1), pl.program_id(2)

        if side:
            @pl.when((i == 0) & (j == 0) & (k == 0))
            def _():
                side["start"](side_in, side_out, side_scr)

        av = a_ref[...]
        if a_act == "silu":
            av = _silu(av)
        prod = lax.dot_general(av, b_ref[...], dims, preferred_element_type=F32)

        def finish(r):
            if bias_ref is not None:
                r = r + bias_ref[...]
            if dsm_ref is not None:
                r = r * _dsilu(dsm_ref[...])
            o_ref[...] = r.astype(o_ref.dtype)

        if nk == 1:
            finish(prod)
        else:
            @pl.when(k == 0)
            def _():
                acc_ref[...] = prod

            @pl.when(k > 0)
            def _():
                acc_ref[...] += prod

            @pl.when(k == nk - 1)
            def _():
                finish(acc_ref[...])

        if side:
            @pl.when((i == gi - 1) & (j == gj - 1) & (k == nk - 1))
            def _():
                side["finish"](side_in, side_out, side_scr)

    if ta:
        a_spec = pl.BlockSpec((tk, tm), lambda i, j, k: (k + off_blocks, i))
    else:
        a_spec = pl.BlockSpec((tm, tk), lambda i, j, k: (i + off_blocks, k))
    if tb:
        b_spec = pl.BlockSpec((tn, tk), lambda i, j, k: (j, k))
    else:
        b_spec = pl.BlockSpec((tk, tn), lambda i, j, k: (k, j))
    in_specs = [a_spec, b_spec]
    if bias is not None:
        in_specs.append(pl.BlockSpec((1, tn), lambda i, j, k: (0, j)))
    if dsilu_mul is not None:
        in_specs.append(pl.BlockSpec((1, tn), lambda i, j, k: (0, j)))
    hbm = pl.BlockSpec(memory_space=pl.ANY)
    res = pl.pallas_call(
        body, name=name, grid=(gi, gj, nk),
        in_specs=in_specs + [hbm] * n_sin,
        out_specs=[pl.BlockSpec((tm, tn), lambda i, j, k: (i, j))] + [hbm] * n_sout,
        out_shape=[jax.ShapeDtypeStruct((M, N), out_dtype)] + (list(side["outs"]) if side else []),
        scratch_shapes=([pltpu.VMEM((tm, tn), F32)] if nk > 1 else []) + (list(side["scratch"]) if side else []),
        compiler_params=_params(("arbitrary",) * 3 if side else ("parallel", "parallel", "arbitrary")),
    )(a, b, *extras, *(side["ins"] if side else []))
    return (res[0], res[1:]) if side else res[0]


def _elementwise(fn, ins, outs, *, rows, cols, name, tr=256):
    tr = _pick(rows, tr, 2 * SUBLANES)
    n_in = len(ins)

    def body(*refs):
        vals = fn(*[r[...] for r in refs[:n_in]])
        if not isinstance(vals, (tuple, list)):
            vals = (vals,)
        for r, v in zip(refs[n_in:], vals, strict=True):
            r[...] = v.astype(r.dtype)

    def spec(off):
        assert off % tr == 0
        ob = off // tr
        return pl.BlockSpec((tr, cols), lambda i: (i + ob, 0))

    res = pl.pallas_call(
        body, name=name, grid=(rows // tr,),
        in_specs=[spec(off) for _, off in ins],
        out_specs=[spec(0) for _ in outs],
        out_shape=[jax.ShapeDtypeStruct((rows, cols), dt) for dt in outs],
        compiler_params=_params(("parallel",)),
    )(*[a for a, _ in ins])
    return res


def _adam_math(w, g, m, v):
    m = ADAM_B1 * m + (1.0 - ADAM_B1) * g
    v = ADAM_B2 * v + (1.0 - ADAM_B2) * (g * g)
    m_hat = m / (1.0 - ADAM_B1 ** ADAM_STEP)
    v_hat = v / (1.0 - ADAM_B2 ** ADAM_STEP)
    delta = -ADAM_LR * (m_hat / (jnp.sqrt(v_hat) + ADAM_EPS) + ADAM_WD * w)
    return delta, m, v


def _adam(w, g, m, v, name, echo_g=False):
    rows, cols = w.shape
    fn = (lambda w_, g_, m_, v_: _adam_math(w_, g_, m_, v_) + (g_,)) if echo_g else _adam_math
    return _elementwise(fn, [(w, 0), (g, 0), (m, 0), (v, 0)], [F32] * (4 if echo_g else 3),
                        rows=rows, cols=cols, name=name)


def _adam_many(quads, name):
    n = len(quads)

    def body(*refs):
        ins, outs = refs[:4 * n], refs[4 * n:]
        for t in range(n):
            w, g, m, v = (r[...] for r in ins[4 * t:4 * t + 4])
            for o_ref, val in zip(outs[3 * t:3 * t + 3], _adam_math(w, g, m, v), strict=True):
                o_ref[...] = val

    res = pl.pallas_call(
        body, name=name,
        out_shape=[jax.ShapeDtypeStruct(q[0].shape, F32) for q in quads for _ in range(3)],
        compiler_params=_params(),
    )(*[a for q in quads for a in q])
    return [res[3 * t:3 * t + 3] for t in range(n)]


def _pack(arrs, row_quantum):
    flat = jnp.concatenate([a.reshape(-1).astype(F32) for a in arrs])
    n = flat.shape[0]
    q = row_quantum * PACK_COLS
    total = -(-n // q) * q
    flat = jnp.pad(flat, (0, total - n))
    return flat.reshape(total // PACK_COLS, PACK_COLS)


def _unpack(buf, shapes):
    flat = buf.reshape(-1)
    out, off = [], 0
    for s in shapes:
        n = 1
        for d in s:
            n *= d
        out.append(flat[off:off + n].reshape(s))
        off += n
    return out


def _allreduce8(buf, name):
    R, C = buf.shape
    assert R % (8 * SUBLANES) == 0
    m = R // 8

    def body(x_ref, o_ref, recv, red, s1, r1, s2, r2):
        x, y, c = _pos()
        me = 4 * x + 2 * y + c

        def peer(k):
            px, py, pc = _flip(x, (k >> 2) & 1), _flip(y, (k >> 1) & 1), _flip(c, k & 1)
            return (px, py, pc), 4 * px + 2 * py + pc

        def rows(ref, idx):
            return ref.at[pl.ds(pl.multiple_of(idx * m, SUBLANES), m), :]

        def scatter(k):
            dev, p = peer(k)
            return pltpu.make_async_remote_copy(src_ref=rows(x_ref, p), dst_ref=recv.at[k], send_sem=s1.at[k],
                                                recv_sem=r1.at[k], device_id=dev, device_id_type=MESH_ID)

        def share(k):
            dev, p = peer(k)
            return pltpu.make_async_remote_copy(src_ref=red, dst_ref=rows(o_ref, me), send_sem=s2.at[k],
                                                recv_sem=r2.at[k], device_id=dev, device_id_type=MESH_ID)

        def shared_from(k):
            dev, p = peer(k)
            return pltpu.make_async_remote_copy(src_ref=red, dst_ref=rows(o_ref, p), send_sem=s2.at[k],
                                                recv_sem=r2.at[k], device_id=dev, device_id_type=MESH_ID)

        for k in range(1, 8):
            scatter(k).start()
        acc = rows(x_ref, me)[...]
        for k in range(1, 8):
            scatter(k).wait_recv()
            acc = acc + recv[k]
        red[...] = acc
        rows(o_ref, me)[...] = acc
        for k in range(1, 8):
            share(k).start()
        for k in range(1, 8):
            shared_from(k).wait_recv()
        for k in range(1, 8):
            scatter(k).wait_send()
            share(k).wait_send()

    return pl.pallas_call(
        body, name=name,
        in_specs=[pl.BlockSpec(memory_space=pltpu.VMEM)],
        out_specs=pl.BlockSpec(memory_space=pltpu.VMEM),
        out_shape=jax.ShapeDtypeStruct((R, C), F32),
        scratch_shapes=[pltpu.VMEM((8, m, C), F32), pltpu.VMEM((m, C), F32),
                        pltpu.SemaphoreType.DMA((8,)), pltpu.SemaphoreType.DMA((8,)),
                        pltpu.SemaphoreType.DMA((8,)), pltpu.SemaphoreType.DMA((8,))],
        compiler_params=_params(),
    )(buf)


def _allreduce8_two_level(buf, name):
    R, C = buf.shape
    assert R % (8 * SUBLANES) == 0
    m, hr = R // 8, R // 2

    def body(x_ref, o_ref, got0, half, got1, red, ssem, rsem):
        x, y, c = _pos()
        q = 2 * x + y
        sibling = (x, y, 1 - c)

        def half_rows(ref, core):
            return ref.at[pl.ds(pl.multiple_of(core * hr, SUBLANES), hr), :]

        def chunk(ref, core, chip):
            return ref.at[pl.ds(pl.multiple_of(core * hr + chip * m, SUBLANES), m), :]

        def chip_of(k):
            px, py = _flip(x, (k >> 1) & 1), _flip(y, k & 1)
            return (px, py, c), 2 * px + py

        def copy(src, dst, phase, k, dev):
            return pltpu.make_async_remote_copy(src_ref=src, dst_ref=dst, send_sem=ssem.at[phase, k],
                                                recv_sem=rsem.at[phase, k], device_id=dev, device_id_type=MESH_ID)

        swap = copy(half_rows(x_ref, 1 - c), got0, 0, 0, sibling)
        swap.start()
        swap.wait()
        half[...] = half_rows(x_ref, c)[...] + got0[...]

        def scatter(k):
            dev, p = chip_of(k)
            return copy(half.at[pl.ds(pl.multiple_of(p * m, SUBLANES), m), :], got1.at[k], 1, k, dev)

        for k in range(1, 4):
            scatter(k).start()
        acc = half[pl.ds(pl.multiple_of(q * m, SUBLANES), m), :]
        for k in range(1, 4):
            scatter(k).wait_recv()
            acc = acc + got1[k]
        red[...] = acc
        chunk(o_ref, c, q)[...] = acc

        def share(k, landing_chip):
            return copy(red, chunk(o_ref, c, landing_chip), 2, k, chip_of(k)[0])

        for k in range(1, 4):
            share(k, q).start()
        for k in range(1, 4):
            share(k, chip_of(k)[1]).wait_recv()
        back = copy(half_rows(o_ref, c), half_rows(o_ref, c), 3, 0, sibling)
        back.start()
        copy(half_rows(o_ref, 1 - c), half_rows(o_ref, 1 - c), 3, 0, sibling).wait_recv()
        back.wait_send()
        for k in range(1, 4):
            scatter(k).wait_send()
            share(k, q).wait_send()

    return pl.pallas_call(
        body, name=name,
        in_specs=[pl.BlockSpec(memory_space=pltpu.VMEM)],
        out_specs=pl.BlockSpec(memory_space=pltpu.VMEM),
        out_shape=jax.ShapeDtypeStruct((R, C), F32),
        scratch_shapes=[pltpu.VMEM((hr, C), F32), pltpu.VMEM((hr, C), F32), pltpu.VMEM((4, m, C), F32), pltpu.VMEM((m, C), F32),
                        pltpu.SemaphoreType.DMA((4, 4)), pltpu.SemaphoreType.DMA((4, 4))],
        compiler_params=_params(),
    )(buf)


def _bounce(src, dst, buf, sem):
    cin = pltpu.make_async_copy(src, buf, sem)
    cin.start()
    cin.wait()
    cout = pltpu.make_async_copy(buf, dst, sem)
    cout.start()
    cout.wait()


def _chunk(ref, axis, idx, size):
    start = idx * size
    if axis == 0:
        return ref.at[pl.ds(start, size), :]
    return ref.at[:, pl.ds(start, size)]


def _in_proj_gather(hl, win, wout, q_arr, *, rows, tm):
    D, nq = win.shape
    dq, D2 = wout.shape
    ni = rows // tm
    ops = ((0, 1, nq, D // 2), (1, 0, dq, dq // 2))

    def body(q_ref, a_ref, win_ref, wout_ref, p_ref, gin_ref, gout_ref, b_scr, buf_out, lsem, ssem, rsem, fsem, gsem):
        j, i = pl.program_id(0), pl.program_id(1)
        x, y, c = _pos()
        q = 2 * x + y
        srcs = (win_ref, wout_ref)
        dsts = (gin_ref, gout_ref)

        def shard_window(o, chip):
            _, axis, size, _ = ops[o]
            return _chunk(dsts[o], axis, chip, size)

        def half(ref, o, core):
            return ref.at[pl.ds(core * ops[o][3], ops[o][3]), :]

        def half_window(o, chip, core):
            _, axis, size, hs = ops[o]
            if axis == 1:
                return dsts[o].at[pl.ds(core * hs, hs), pl.ds(chip * size, size)]
            return dsts[o].at[pl.ds(chip * size + core * hs, hs), :]

        def chip_of(k):
            px, py = _flip(x, (k >> 1) & 1), _flip(y, k & 1)
            return px, py, 2 * px + py

        def send(o, k):
            px, py, _ = chip_of(k)
            return pltpu.make_async_remote_copy(
                src_ref=half(srcs[o], o, c), dst_ref=half_window(o, q, c), send_sem=ssem.at[o, k],
                recv_sem=rsem.at[o, k], device_id=(px, py, c), device_id_type=MESH_ID)

        def chip_recv(o, k):
            px, py, pq = chip_of(k)
            landed = half_window(o, pq, c)
            pltpu.make_async_remote_copy(src_ref=landed, dst_ref=landed, send_sem=ssem.at[o, k], recv_sem=rsem.at[o, k],
                                         device_id=(px, py, c), device_id_type=MESH_ID).wait_recv()

        def to_sibling(o, k):
            landed = half_window(o, chip_of(k)[2], c)
            return pltpu.make_async_remote_copy(src_ref=landed, dst_ref=landed, send_sem=fsem.at[o, k],
                                                recv_sem=gsem.at[o, k], device_id=(x, y, 1 - c), device_id_type=MESH_ID)

        def from_sibling(o, k):
            theirs = half_window(o, chip_of(k)[2], 1 - c)
            pltpu.make_async_remote_copy(src_ref=theirs, dst_ref=theirs, send_sem=fsem.at[o, k], recv_sem=gsem.at[o, k],
                                         device_id=(x, y, 1 - c), device_id_type=MESH_ID).wait_recv()

        def relay(o, core):
            if core == 0:
                landed, target = half_window(o, chip_of(2)[2], 0), (x, 1 - y, 0)
            else:
                landed, target = half_window(o, chip_of(1)[2], 1), (1 - x, y, 1)
            return pltpu.make_async_remote_copy(src_ref=landed, dst_ref=landed, send_sem=ssem.at[o, 3],
                                                recv_sem=rsem.at[o, 3], device_id=target, device_id_type=MESH_ID)

        def on_core(core, fn):
            @pl.when(c == core)
            def _():
                fn()

        def land(o, k):
            chip_recv(o, k)
            if k == 2:
                on_core(0, lambda: relay(o, 0).start())
            if k == 1:
                on_core(1, lambda: relay(o, 1).start())
            to_sibling(o, k).start()

        def settle(o, k):
            from_sibling(o, k)
            to_sibling(o, k).wait_send()

        def b_load(k, slot):
            src = win_ref if k == 0 else shard_window(0, chip_of(k)[2])
            return pltpu.make_async_copy(src, b_scr.at[slot], lsem.at[0])

        def own_store():
            return pltpu.make_async_copy(b_scr.at[0], shard_window(0, q), lsem.at[2])

        order = (0, 2, 1, 3)
        early = max(ni - 2, 0)

        @pl.when((j == 0) & (i == 0))
        def _():
            for o in range(2):
                for k in (2, 1):
                    send(o, k).start()
            first = b_load(0, 0)
            first.start()
            first.wait()
            own_store().start()
            _bounce(wout_ref, shard_window(1, q), buf_out, lsem.at[1])

        for jj in range(3):
            nxt = order[jj + 1]

            @pl.when((j == jj) & (i == early))
            def _(nxt=nxt):
                land(0, nxt)

            @pl.when((j == jj) & (i == ni - 1))
            def _(jj=jj, nxt=nxt):
                settle(0, nxt)
                if jj == 1:
                    own_store().wait()
                b_load(nxt, (jj + 1) % 2).start()

            @pl.when((j == jj + 1) & (i == 0))
            def _(jj=jj, nxt=nxt):
                b_load(nxt, (jj + 1) % 2).wait()

        @pl.when((j == 3) & (i == 0))
        def _():
            land(1, 2)
            land(1, 1)

        p_ref[...] = jnp.dot(a_ref[...], b_scr[j % 2], preferred_element_type=F32).astype(p_ref.dtype)

        @pl.when((j == 3) & (i == ni - 1))
        def _():
            settle(1, 2)
            settle(1, 1)
            land(1, 3)
            settle(1, 3)
            for o in range(2):
                for k in (2, 1):
                    send(o, k).wait_send()
                for core in range(2):
                    on_core(core, lambda o=o, core=core: relay(o, core).wait_send())

    hbm = pl.BlockSpec(memory_space=pl.ANY)
    grid_spec = pltpu.PrefetchScalarGridSpec(
        num_scalar_prefetch=1, grid=(4, ni),
        in_specs=[pl.BlockSpec((tm, D), lambda j, i, qr: (i, 0)), hbm, hbm],
        out_specs=[pl.BlockSpec((tm, nq), lambda j, i, qr: (i, jnp.bitwise_xor(qr[0], ((j & 1) << 1) | (j >> 1)))),
                   hbm, hbm],
        scratch_shapes=[pltpu.VMEM((2,) + win.shape, win.dtype), pltpu.VMEM(wout.shape, wout.dtype), pltpu.SemaphoreType.DMA((3,))]
        + [pltpu.SemaphoreType.DMA((2, 4)) for _ in range(4)])
    return pl.pallas_call(
        body, name="in_proj_gather", grid_spec=grid_spec,
        out_shape=[jax.ShapeDtypeStruct((rows, 4 * nq), BF16), jax.ShapeDtypeStruct((D, 4 * nq), win.dtype),
                   jax.ShapeDtypeStruct((4 * dq, D2), wout.dtype)],
        compiler_params=_params(("arbitrary", "arbitrary")),
    )(q_arr, hl, win, wout)


def _rs_to_sibling(gs, axes):
    n = len(gs)
    shapes = []
    for g, ax in zip(gs, axes):
        s = list(g.shape)
        s[ax] //= 8
        shapes.append(tuple(s))

    def body(*refs):
        g_refs, mine, landed = refs[:n], refs[n:2 * n], refs[2 * n:3 * n]
        bufs = refs[3 * n:4 * n]
        lsem, ssem, rsem = refs[4 * n:]
        x, y, c = _pos()
        cps = []
        for o in range(n):
            size = shapes[o][axes[o]]
            for j in range(4):
                rc = pltpu.make_async_remote_copy(
                    src_ref=_chunk(g_refs[o], axes[o], 2 * j + 1 - c, size), dst_ref=landed[o].at[j],
                    send_sem=ssem.at[o, j], recv_sem=rsem.at[o, j], device_id=(x, y, 1 - c), device_id_type=MESH_ID)
                rc.start()
                cps.append(rc)
        for o in range(n):
            size = shapes[o][axes[o]]
            for j in range(4):
                _bounce(_chunk(g_refs[o], axes[o], 2 * j + c, size), mine[o].at[j], bufs[o], lsem.at[o])
        for rc in cps:
            rc.wait()

    hbm = pl.BlockSpec(memory_space=pl.ANY)
    outs = [jax.ShapeDtypeStruct((4,) + s, g.dtype) for s, g in zip(shapes, gs)]
    res = pl.pallas_call(
        body, name="rs_to_sibling", in_specs=[hbm] * n, out_specs=[hbm] * (2 * n), out_shape=outs + outs,
        scratch_shapes=[pltpu.VMEM(s, g.dtype) for s, g in zip(shapes, gs)]
        + [pltpu.SemaphoreType.DMA((n,)), pltpu.SemaphoreType.DMA((n, 4)), pltpu.SemaphoreType.DMA((n, 4))],
        compiler_params=_params(),
    )(*gs)
    return res[:n], res[n:]


def _gather_block_side(block):
    r, n_cols = block.shape

    def copies(ins, outs, scr):
        ssem, rsem = scr[1], scr[2]
        x, y, c = _pos()
        return [pltpu.make_async_remote_copy(
            src_ref=ins[0], dst_ref=outs[0].at[k], send_sem=ssem.at[k], recv_sem=rsem.at[k],
            device_id=(_flip(x, (k >> 2) & 1), _flip(y, (k >> 1) & 1), _flip(c, k & 1)), device_id_type=MESH_ID)
            for k in range(1, 8)]

    def start(ins, outs, scr):
        for cp in copies(ins, outs, scr):
            cp.start()

    def finish(ins, outs, scr):
        _bounce(ins[0], outs[0].at[0], scr[0], scr[3].at[0])
        for cp in copies(ins, outs, scr):
            cp.wait()

    return dict(ins=[block], outs=[jax.ShapeDtypeStruct((8, r, n_cols), block.dtype)],
                scratch=[pltpu.VMEM((r, n_cols), block.dtype), pltpu.SemaphoreType.DMA((8,)), pltpu.SemaphoreType.DMA((8,)),
                         pltpu.SemaphoreType.DMA((1,))],
                start=start, finish=finish)


def _join_sides(a, b):
    na_i, na_o, na_s = len(a["ins"]), len(a["outs"]), len(a["scratch"])

    def run(which):
        def fn(ins, outs, scr):
            a[which](ins[:na_i], outs[:na_o], scr[:na_s])
            b[which](ins[na_i:], outs[na_o:], scr[na_s:])
        return fn

    return dict(ins=a["ins"] + b["ins"], outs=a["outs"] + b["outs"], scratch=a["scratch"] + b["scratch"],
                start=run("start"), finish=run("finish"))


def _rs_chips_side(parts):
    n = len(parts)

    def copies(p_refs, slots, scr):
        ssem, rsem = scr[n + 1], scr[n + 2]
        x, y, c = _pos()
        cps = []
        for o in range(n):
            for k in range(1, 4):
                px, py = _flip(x, (k >> 1) & 1), _flip(y, k & 1)
                cps.append(pltpu.make_async_remote_copy(
                    src_ref=p_refs[o].at[2 * px + py], dst_ref=slots[o].at[k], send_sem=ssem.at[o, k],
                    recv_sem=rsem.at[o, k], device_id=(px, py, c), device_id_type=MESH_ID))
        return cps

    def start(p_refs, slots, scr):
        for cp in copies(p_refs, slots, scr):
            cp.start()

    def finish(p_refs, slots, scr):
        x, y, _ = _pos()
        q = 2 * x + y
        for o in range(n):
            _bounce(p_refs[o].at[q], slots[o].at[0], scr[o], scr[n].at[o])
        for cp in copies(p_refs, slots, scr):
            cp.wait()

    return dict(
        ins=list(parts), outs=[jax.ShapeDtypeStruct(p.shape, p.dtype) for p in parts],
        scratch=[pltpu.VMEM(p.shape[1:], p.dtype) for p in parts]
        + [pltpu.SemaphoreType.DMA((n,)), pltpu.SemaphoreType.DMA((n, 4)), pltpu.SemaphoreType.DMA((n, 4))],
        start=start, finish=finish)


def _rs_share(rs, axes):
    n = len(rs)
    shapes = []
    for r, ax in zip(rs, axes):
        s = list(r.shape)
        s[ax] *= 2
        shapes.append(tuple(s))

    def body(*refs):
        r_refs, outs = refs[:n], refs[n:2 * n]
        bufs = refs[2 * n:3 * n]
        lsem, ssem, rsem = refs[3 * n:]
        x, y, c = _pos()
        cps = []
        for o in range(n):
            size = r_refs[o].shape[axes[o]]
            window = _chunk(outs[o], axes[o], c, size)
            rc = pltpu.make_async_remote_copy(src_ref=r_refs[o], dst_ref=window, send_sem=ssem.at[o], recv_sem=rsem.at[o],
                                              device_id=(x, y, 1 - c), device_id_type=MESH_ID)
            rc.start()
            cps.append(rc)
        for o in range(n):
            size = r_refs[o].shape[axes[o]]
            _bounce(r_refs[o], _chunk(outs[o], axes[o], c, size), bufs[o], lsem.at[o])
        for cp in cps:
            cp.wait()

    hbm = pl.BlockSpec(memory_space=pl.ANY)
    return pl.pallas_call(
        body, name="rs_share", in_specs=[hbm] * n, out_specs=[hbm] * n,
        out_shape=[jax.ShapeDtypeStruct(s, r.dtype) for s, r in zip(shapes, rs)],
        scratch_shapes=[pltpu.VMEM(r.shape, r.dtype) for r in rs] + [pltpu.SemaphoreType.DMA((n,)) for _ in range(3)],
        compiler_params=_params(),
    )(*rs)


def _rs_pair_sums(gs, axes):
    mine, landed = _rs_to_sibling(gs, axes)
    pair_sums = []
    for o, (mi, la) in enumerate(zip(mine, landed)):
        rows, cols = mi.shape[0] * mi.shape[1], mi.shape[2]
        s = _elementwise(lambda a, b: a.astype(F32) + b.astype(F32), [(mi.reshape(rows, cols), 0), (la.reshape(rows, cols), 0)],
                         [BF16], rows=rows, cols=cols, name=f"rs_pair_sum{o}")[0]
        pair_sums.append(s.reshape(mi.shape))
    return pair_sums


def _rs_finish(slots, axes):
    reduced = []
    for o, sl in enumerate(slots):
        rows, cols = sl.shape[1], sl.shape[2]
        flat = sl.reshape(4 * rows, cols)
        r = _elementwise(lambda a, b, c, d: (a.astype(F32) + b.astype(F32)) + (c.astype(F32) + d.astype(F32)),
                         [(flat, k * rows) for k in range(4)], [F32], rows=rows, cols=cols, name=f"rs_chip_sum{o}")[0]
        reduced.append(r)
    return _rs_share(reduced, axes)


def _norm_in(x, ctx, g, sc_l, sh_l, sc_c, sh_c, tr):
    L, D = x.shape
    T = ctx.shape[0]
    nx, nc = L // tr, T // tr

    def body(x_ref, c_ref, g_ref, scl, shl, scc, shc, o_ref):
        i = pl.program_id(0)

        def run(src, sc, sh):
            v = src[...]
            r = lax.rsqrt(jnp.mean(v * v, axis=-1, keepdims=True) + EPS)
            o_ref[...] = ((v * r * g_ref[...]) * (1.0 + sc[...]) + sh[...]).astype(o_ref.dtype)

        @pl.when(i < nx)
        def _():
            run(x_ref, scl, shl)

        @pl.when(i >= nx)
        def _():
            run(c_ref, scc, shc)

    vec = pl.BlockSpec((1, D), lambda i: (0, 0))
    return pl.pallas_call(
        body, name="norm_in", grid=(nx + nc,),
        in_specs=[pl.BlockSpec((tr, D), lambda i: (jnp.minimum(i, nx - 1), 0)),
                  pl.BlockSpec((tr, D), lambda i: (jnp.maximum(i - nx, 0), 0)), vec, vec, vec, vec, vec],
        out_specs=pl.BlockSpec((tr, D), lambda i: (i, 0)),
        out_shape=jax.ShapeDtypeStruct((L + T, D), BF16),
        compiler_params=_params(("arbitrary",)),
    )(x, ctx, g, sc_l, sh_l, sc_c, sh_c)


def _tmod(tl, row_w):
    assert row_w & (row_w - 1) == 0
    return lax.broadcasted_iota(jnp.int32, (tl, 1), 0) & (row_w - 1)


def _shift(z, k, tmod, row_w):
    tl = z.shape[0]
    rolled = pltpu.roll(z, k % tl, 0)
    mask = (tmod >= k) if k > 0 else (tmod < row_w + k)
    return jnp.where(mask, rolled, 0.0)


def _conv(z, w_ref, taps, left, tmod, row_w, lanes=slice(None)):
    out = None
    for j in range(taps):
        k = left - j
        term = (z if k == 0 else _shift(z, k, tmod, row_w)) * w_ref[j:j + 1, lanes]
        out = term if out is None else out + term
    return out


def _conv_bwd(dz, z, w_ref, taps, left, tmod, row_w, lanes=slice(None)):
    din = None
    dws = []
    for j in range(taps):
        k = left - j
        shifted = dz if k == 0 else _shift(dz, -k, tmod, row_w)
        term = shifted * w_ref[j:j + 1, lanes]
        din = term if din is None else din + term
        dws.append(jnp.sum(shifted * z, axis=0, keepdims=True))
    return din, dws


def _gate_matmul(xb16_ref, wd_ref, pre_scr, W, ng, gs):
    for g in range(ng):
        pg = jnp.dot(xb16_ref[:, g * gs:(g + 1) * gs], wd_ref[g], preferred_element_type=F32)
        pre_scr[:, g * gs:(g + 1) * gs] = pg[:, :gs]
        pre_scr[:, W + g * gs:W + (g + 1) * gs] = pg[:, gs:]


def _f32(ref, rows, lanes):
    return ref[rows, lanes].astype(F32)


def _sub_loop(tl, sub_r, W, fn):
    def chunk(ci, carry):
        r0 = pl.multiple_of(ci * sub_r, sub_r)
        for lb in range(W // LANES):
            fn(r0, lb * LANES)
        return carry

    lax.fori_loop(0, tl // sub_r, chunk, 0)


def _row_loop(tl, rev, step, init):
    nchunk = tl // SUBLANES

    def chunk(j, carry):
        jj = (nchunk - 1 - j) if rev else j
        c0 = pl.multiple_of(jj * SUBLANES, SUBLANES)
        for r in (range(SUBLANES - 1, -1, -1) if rev else range(SUBLANES)):
            carry = step(c0 + r, carry)
        return carry

    return lax.fori_loop(0, nchunk, chunk, init)


def _mix_fwd(P, d, h_init, wts, *, rows, row_off, row_w, tl, saved0=None, name):
    W = P.shape[1] // 6
    nt = rows // tl
    ob = row_off // tl
    rev = d == 1
    gs = min(LRU_GROUP, W)
    ng = W // gs
    wca, wcb, bcb = wts["wca"], wts["wcb"], wts["bcb"]
    wd, ba, bx, lam = wts["wd"][d], wts["ba"][d], wts["bx"][d], wts["lam"][d]

    def tile(i):
        return (nt - 1 - i) if rev else i

    def pcol(j):
        return pl.BlockSpec((tl, W), lambda i: (tile(i) + ob, j))

    vec = pl.BlockSpec((1, W), lambda i: (0, 0))
    taps = pl.BlockSpec((SUBLANES, W), lambda i: (0, 0))
    wd_spec = pl.BlockSpec(wd.shape, lambda i: (0, 0, 0))
    seq = pl.BlockSpec((tl, W), lambda i: (tile(i), 0))

    sub_r = min(row_w, tl)
    assert tl % sub_r == 0

    def body(*refs):
        if rev:
            (bl, cl, ul, gl, ql, ho, xb_r, xb16_r, wca_r, wd_r, ba_r, bx_r, lam_r, hin, hseq, cat, a_o, r_o, ig_o, m2_o,
             b_scr, pre_scr, carry, sp_scr) = refs
        else:
            (vl, wcb_r, bcb_r, wd_r, ba_r, bx_r, lam_r, hin, hseq, xb_r, xb16_r, a_o, r_o, ig_o, m2_o,
             b_scr, pre_scr, carry, sp_scr) = refs
        i = pl.program_id(0)

        @pl.when(i == 0)
        def _():
            carry[...] = hin[...]

        sp_scr[...] = _softplus(-lam_r[...])
        tmod = _tmod(sub_r, sub_r)

        def conv_in(r0, l0):
            rs, ls = pl.ds(r0, sub_r), pl.ds(l0, LANES)
            xb = _conv(_f32(vl, rs, ls), wcb_r, 4, 2, tmod, sub_r, ls) + bcb_r[:, ls]
            xb_r[rs, ls] = xb
            xb16_r[rs, ls] = xb.astype(BF16)

        def gates(r0, l0):
            rs, ls = pl.ds(r0, sub_r), pl.ds(l0, LANES)
            r, ig, a, m2 = _gates(pre_scr[rs, ls] + ba_r[:, ls], pre_scr[rs, pl.ds(W + l0, LANES)] + bx_r[:, ls],
                                  sp_scr[:, ls])
            a_o[rs, ls] = a
            r_o[rs, ls] = r.astype(r_o.dtype)
            ig_o[rs, ls] = ig.astype(ig_o.dtype)
            m2_o[rs, ls] = m2.astype(m2_o.dtype)
            m = jnp.where(m2 > 0.0, m2 * lax.rsqrt(m2), 0.0)
            b_scr[rs, ls] = m * (ig * xb_r[rs, ls])

        if not rev:
            _sub_loop(tl, sub_r, W, conv_in)
        _gate_matmul(xb16_r, wd_r, pre_scr, W, ng, gs)
        _sub_loop(tl, sub_r, W, gates)

        def step(t, h):
            h = a_o[pl.ds(t, 1), :] * h + b_scr[pl.ds(t, 1), :]
            hseq[pl.ds(t, 1), :] = h
            return h

        carry[...] = _row_loop(tl, rev, step, carry[...])

        if rev:
            def mix_out(r0, l0):
                rs, ls = pl.ds(r0, sub_r), pl.ds(l0, LANES)
                yb = (ho[rs, ls] + hseq[rs, ls]) * _silu(_f32(ql, rs, ls))
                ya = (_f32(bl, rs, ls) * _conv(_f32(cl, rs, ls) * _f32(ul, rs, ls), wca_r, 3, 1, tmod, sub_r, ls)
                      * _silu(_f32(gl, rs, ls)))
                cat[rs, ls] = ya.astype(cat.dtype)
                cat[rs, pl.ds(W + l0, LANES)] = yb.astype(cat.dtype)

            _sub_loop(tl, sub_r, W, mix_out)

    scratch = [pltpu.VMEM((tl, W), F32), pltpu.VMEM((tl, 2 * W), F32), pltpu.VMEM((1, W), F32), pltpu.VMEM((1, W), F32)]
    f32_seq = jax.ShapeDtypeStruct((rows, W), F32)
    kept_gates = [f32_seq] + [jax.ShapeDtypeStruct((rows, W), BF16)] * 3
    if rev:
        in_specs = [pcol(j) for j in (0, 1, 2, 3, 5)] + [seq, seq, seq, taps, wd_spec, vec, vec, vec, vec]
        args = [P] * 5 + [saved0["h"], saved0["xb"], saved0["xb16"], wca, wd, ba, bx, lam, h_init]
        out_specs = [seq, pl.BlockSpec((tl, 2 * W), lambda i: (tile(i), 0))] + [seq] * 4
        out_shape = [f32_seq, jax.ShapeDtypeStruct((rows, 2 * W), BF16)] + kept_gates
    else:
        in_specs = [pcol(4), taps, vec, wd_spec, vec, vec, vec, vec]
        args = [P, wcb, bcb, wd, ba, bx, lam, h_init]
        out_specs = [seq] * 7
        out_shape = [f32_seq, f32_seq, jax.ShapeDtypeStruct((rows, W), BF16)] + kept_gates
    res = pl.pallas_call(
        body, name=name, grid=(nt,), in_specs=in_specs, out_specs=out_specs, out_shape=out_shape,
        scratch_shapes=scratch, compiler_params=_params(("arbitrary",)),
    )(*args)
    gates = dict(zip(("a", "r", "ig", "m2"), res[-4:]))
    if rev:
        return res[0], res[1], gates
    return dict(h=res[0], xb=res[1], xb16=res[2]), gates


_BWD_SCRATCH = ("dyl", "g", "dp16", "sp", "dlf", "edge", "c")
_FWD_SAVED = ("xb", "xb16", "a", "r", "ig", "m2")


def _bwd_scratch(tl, W):
    shapes = {"dyl": pltpu.VMEM((tl, W), F32), "g": pltpu.VMEM((tl, W), F32), "dp16": pltpu.VMEM((tl, 2 * W), BF16),
              "sp": pltpu.VMEM((1, W), F32), "dlf": pltpu.VMEM((1, W), F32), "edge": pltpu.VMEM((1, W), F32),
              "c": pltpu.VMEM((1, W), F32)}
    return [shapes[n] for n in _BWD_SCRATCH]


def _lru_bwd_tile(d, dy_fn, hs_ref, wd_r, lam_r, scr, acc, first, tl, sub_r, ng, gs):
    dwd_ref, dba_ref, dbx_ref, dlam_ref = acc
    W = hs_ref.shape[1]
    assert gs % LANES == 0
    rev = d == 0
    lam = lam_r[...]
    scr["sp"][...] = _softplus(-lam)
    scr["dlf"][...] = -_sigmoid(-lam)

    @pl.when(first)
    def _():
        dwd_ref[...] = jnp.zeros_like(dwd_ref)
        dba_ref[...] = jnp.zeros_like(dba_ref)
        dbx_ref[...] = jnp.zeros_like(dbx_ref)
        dlam_ref[...] = jnp.zeros_like(dlam_ref)

    def state_grad(r0, l0):
        rs, ls = pl.ds(r0, sub_r), pl.ds(l0, LANES)
        scr["dyl"][rs, ls] = dy_fn(rs, ls)

    _sub_loop(tl, sub_r, W, state_grad)

    def step(t, c):
        g = scr["dyl"][pl.ds(t, 1), :] + c
        scr["g"][pl.ds(t, 1), :] = g
        return scr["a"][pl.ds(t, 1), :] * g

    scr["c"][...] = _row_loop(tl, rev, step, scr["c"][...])
    row = lax.broadcasted_iota(jnp.int32, (sub_r, 1), 0)

    def grads(r0, l0):
        rs, ls = pl.ds(r0, sub_r), pl.ds(l0, LANES)
        g, a, m2 = scr["g"][rs, ls], scr["a"][rs, ls], _f32(scr["m2"], rs, ls)
        r, ig, xb = _f32(scr["r"], rs, ls), _f32(scr["ig"], rs, ls), scr["xb"][rs, ls]
        h = hs_ref[rs, ls]
        if d == 0:
            e0 = pl.multiple_of(jnp.maximum(r0 - SUBLANES, 0), SUBLANES)
            edge = jnp.where(r0 == 0, scr["edge"][:, ls], hs_ref[pl.ds(e0, SUBLANES), ls][SUBLANES - 1:, :])
            hprev = jnp.where(row == 0, edge, pltpu.roll(h, 1, 0))
        else:
            e0 = pl.multiple_of(jnp.minimum(r0 + sub_r, tl - SUBLANES), SUBLANES)
            edge = jnp.where(r0 == tl - sub_r, scr["edge"][:, ls], hs_ref[pl.ds(e0, SUBLANES), ls][:1, :])
            hprev = jnp.where(row == sub_r - 1, edge, pltpu.roll(h, sub_r - 1, 0))
        rsq = lax.rsqrt(m2)
        gm = g * (m2 * rsq)
        d_la = (g * hprev) * a - (g * (ig * xb)) * ((1.0 - m2) * rsq)
        d_pr = d_la * ((-LRU_C) * scr["sp"][:, ls]) * (r * (1.0 - r))
        d_pi = (gm * xb) * (ig * (1.0 - ig))
        scr["dyl"][rs, ls] = gm * ig
        dlam_ref[:, ls] += jnp.sum(d_la * ((-LRU_C) * r), axis=0, keepdims=True) * scr["dlf"][:, ls]
        dba_ref[:, ls] += jnp.sum(d_pr, axis=0, keepdims=True)
        dbx_ref[:, ls] += jnp.sum(d_pi, axis=0, keepdims=True)
        gi, off = divmod(l0, gs)
        scr["dp16"][rs, pl.ds(gi * 2 * gs + off, LANES)] = d_pr.astype(BF16)
        scr["dp16"][rs, pl.ds(gi * 2 * gs + gs + off, LANES)] = d_pi.astype(BF16)

    _sub_loop(tl, sub_r, W, grads)
    for gi in range(ng):
        dp = scr["dp16"][:, gi * 2 * gs:(gi + 1) * 2 * gs]
        scr["g"][:, gi * gs:(gi + 1) * gs] = lax.dot_general(dp, wd_r[gi], (((1,), (1,)), ((), ())),
                                                             preferred_element_type=F32)
        dwd_ref[gi] += lax.dot_general(scr["xb16"][:, gi * gs:(gi + 1) * gs], dp, (((0,), (0,)), ((), ())),
                                       preferred_element_type=F32)


def _edge_block(h, tl, nt, d):
    W = h.shape[1]
    per = tl // SUBLANES
    if d == 0:
        return pl.BlockSpec((SUBLANES, W), lambda i: (jnp.maximum((nt - 1 - i) * per - 1, 0), 0))
    return pl.BlockSpec((SUBLANES, W), lambda i: (jnp.minimum((i + 1) * per, nt * per - 1), 0))


def _mix_bwd0(P, dcat, saved0, gates0, h_init, c_init, wts, *, rows, row_off, row_w, tl, name):
    W = P.shape[1] // 6
    nt = rows // tl
    ob = row_off // tl
    gs = min(LRU_GROUP, W)
    ng = W // gs
    wd, lam = wts["wd"][0], wts["lam"][0]
    h0s = saved0["h"]
    kept = [saved0["xb"], saved0["xb16"]] + [gates0[n] for n in ("a", "r", "ig", "m2")]

    def tile(i):
        return nt - 1 - i

    vec = pl.BlockSpec((1, W), lambda i: (0, 0))
    wd_spec = pl.BlockSpec(wd.shape, lambda i: (0, 0, 0))
    seq = pl.BlockSpec((tl, W), lambda i: (tile(i), 0))

    sub_r = min(row_w, tl)
    assert tl % sub_r == 0

    def body(ql, dyb, hs, hedge8, xb_r, xb16_r, a_r, r_r, ig_r, m2_r, wd_r, lam_r, hin, cin,
             dxb_o, dwd_o, dba_o, dbx_o, dlam_o, cfin, *scratch):
        scr = dict(zip(_BWD_SCRATCH, scratch, strict=True))
        scr.update(zip(_FWD_SAVED, (xb_r, xb16_r, a_r, r_r, ig_r, m2_r), strict=True))
        i = pl.program_id(0)

        @pl.when(i == 0)
        def _():
            scr["c"][...] = cin[...]

        scr["edge"][...] = jnp.where(i == nt - 1, hin[...], hedge8[SUBLANES - 1:SUBLANES, :])
        _lru_bwd_tile(0, lambda rs, ls: _f32(dyb, rs, ls) * _silu(_f32(ql, rs, ls)), hs, wd_r, lam_r, scr,
                      (dwd_o, dba_o, dbx_o, dlam_o), i == 0, tl, sub_r, ng, gs)
        dxb_o[...] = scr["dyl"][...] + scr["g"][...]
        cfin[...] = scr["c"][...]

    return pl.pallas_call(
        body, name=name, grid=(nt,),
        in_specs=[pl.BlockSpec((tl, W), lambda i: (tile(i) + ob, 5)), pl.BlockSpec((tl, W), lambda i: (tile(i), 1)), seq,
                  _edge_block(h0s, tl, nt, 0)] + [seq] * 6 + [wd_spec, vec, vec, vec],
        out_specs=[seq, wd_spec, vec, vec, vec, vec],
        out_shape=[jax.ShapeDtypeStruct((rows, W), F32), jax.ShapeDtypeStruct(wd.shape, F32)]
        + [jax.ShapeDtypeStruct((1, W), F32)] * 4,
        scratch_shapes=_bwd_scratch(tl, W),
        compiler_params=_params(("arbitrary",)),
    )(P, dcat, h0s, h0s, *kept, wd, lam, h_init, c_init)


def _mix_bwd1(P, dcat, saved0, h1s, gates1, dxb0, h_init, c_init, wts, *, rows, row_off, row_w, tl, name,
              dp_rows=None, dp_off=0, dp_into=None):
    dp_rows = rows if dp_rows is None else dp_rows
    dpb = dp_off // tl
    W = P.shape[1] // 6
    nt = rows // tl
    ob = row_off // tl
    gs = min(LRU_GROUP, W)
    ng = W // gs
    wca, wcb = wts["wca"], wts["wcb"]
    wd, lam = wts["wd"][1], wts["lam"][1]
    h0s = saved0["h"]
    kept = [saved0["xb"], saved0["xb16"]] + [gates1[n] for n in ("a", "r", "ig", "m2")]

    vec = pl.BlockSpec((1, W), lambda i: (0, 0))
    taps = pl.BlockSpec((SUBLANES, W), lambda i: (0, 0))
    wd_spec = pl.BlockSpec(wd.shape, lambda i: (0, 0, 0))
    seq = pl.BlockSpec((tl, W), lambda i: (i, 0))

    sub_r = min(row_w, tl)
    assert tl % sub_r == 0

    def body(*refs):
        if dp_into is not None:
            refs = refs[1:]
        (bl, cl, ul, gl, vl, ql, dya, dyb, h0, h1, hedge8, dx0, xb_r, xb16_r, a_r, r_r, ig_r, m2_r, wca_r, wcb_r,
         wd_r, lam_r, hin, cin, dp_o, dwd_o, dba_o, dbx_o, dlam_o, dwca_o, dwcb_o, dbcb_o, cfin, *scratch) = refs
        scr = dict(zip(_BWD_SCRATCH, scratch, strict=True))
        scr.update(zip(_FWD_SAVED, (xb_r, xb16_r, a_r, r_r, ig_r, m2_r), strict=True))
        i = pl.program_id(0)

        @pl.when(i == 0)
        def _():
            scr["c"][...] = cin[...]
            dwca_o[...] = jnp.zeros_like(dwca_o)
            dwcb_o[...] = jnp.zeros_like(dwcb_o)
            dbcb_o[...] = jnp.zeros_like(dbcb_o)

        scr["edge"][...] = jnp.where(i == nt - 1, hin[...], hedge8[0:1, :])
        _lru_bwd_tile(1, lambda rs, ls: _f32(dyb, rs, ls) * _silu(_f32(ql, rs, ls)), h1, wd_r, lam_r, scr,
                      (dwd_o, dba_o, dbx_o, dlam_o), i == 0, tl, sub_r, ng, gs)
        cfin[...] = scr["c"][...]
        tmod = _tmod(sub_r, sub_r)

        def rest(r0, l0):
            rs, ls = pl.ds(r0, sub_r), pl.ds(l0, LANES)
            dxb = dx0[rs, ls] + scr["dyl"][rs, ls] + scr["g"][rs, ls]
            dv, dwb = _conv_bwd(dxb, _f32(vl, rs, ls), wcb_r, 4, 2, tmod, sub_r, ls)
            for j in range(4):
                dwcb_o[j:j + 1, ls] += dwb[j]
            dbcb_o[:, ls] += jnp.sum(dxb, axis=0, keepdims=True)
            q = _f32(ql, rs, ls)
            sq = _sigmoid(q)
            dq = _f32(dyb, rs, ls) * (h0[rs, ls] + h1[rs, ls]) * (sq * (1.0 + q * (1.0 - sq)))
            b_, c_, u_, g_ = _f32(bl, rs, ls), _f32(cl, rs, ls), _f32(ul, rs, ls), _f32(gl, rs, ls)
            z = c_ * u_
            cz = _conv(z, wca_r, 3, 1, tmod, sub_r, ls)
            sgm = _sigmoid(g_)
            sg = g_ * sgm
            da = _f32(dya, rs, ls)
            dz, dwa = _conv_bwd(da * b_ * sg, z, wca_r, 3, 1, tmod, sub_r, ls)
            for j in range(3):
                dwca_o[j:j + 1, ls] += dwa[j]
            parts = (da * cz * sg, dz * u_, dz * c_, da * b_ * cz * (sgm * (1.0 + g_ * (1.0 - sgm))), dv, dq)
            for k, val in enumerate(parts):
                dp_o[rs, pl.ds(k * W + l0, LANES)] = val.astype(dp_o.dtype)

        _sub_loop(tl, sub_r, W, rest)

    def pcol(j):
        return pl.BlockSpec((tl, W), lambda i: (i + ob, j))

    prev = [] if dp_into is None else [dp_into]
    return pl.pallas_call(
        body, name=name, grid=(nt,), input_output_aliases={} if dp_into is None else {0: 0},
        in_specs=[pl.BlockSpec(memory_space=pl.ANY)] * len(prev) + [pcol(j) for j in range(6)]
        + [pl.BlockSpec((tl, W), lambda i: (i, 0)), pl.BlockSpec((tl, W), lambda i: (i, 1)), seq, seq,
           _edge_block(h1s, tl, nt, 1), seq] + [seq] * 6 + [taps, taps, wd_spec, vec, vec, vec],
        out_specs=[pl.BlockSpec((tl, 6 * W), lambda i: (i + dpb, 0)), wd_spec, vec, vec, vec, taps, taps, vec, vec],
        out_shape=[jax.ShapeDtypeStruct((dp_rows, 6 * W), BF16), jax.ShapeDtypeStruct(wd.shape, F32)]
        + [jax.ShapeDtypeStruct((1, W), F32)] * 3
        + [jax.ShapeDtypeStruct((SUBLANES, W), F32)] * 2 + [jax.ShapeDtypeStruct((1, W), F32)] * 2,
        scratch_shapes=_bwd_scratch(tl, W),
        compiler_params=_params(("arbitrary",)),
    )(*prev, *([P] * 6), dcat, dcat, h0s, h1s, h1s, dxb0, *kept, wca, wcb, wd, lam, h_init, c_init)


def _loss_head(out, x, tgt, gt, fg, tr):
    L, D = x.shape

    def body(o_ref, x_ref, t_ref, gt_ref, fg_ref, dn_o, do_o, dfg_o, dgt_o, loss_o):
        i = pl.program_id(0)

        @pl.when(i == 0)
        def _():
            dfg_o[...] = jnp.zeros_like(dfg_o)
            dgt_o[...] = jnp.zeros_like(dgt_o)
            loss_o[...] = jnp.zeros_like(loss_o)

        o = o_ref[...].astype(F32)
        gt_v = gt_ref[...]
        fg_v = fg_ref[...]
        n = x_ref[...] + gt_v * o
        r = lax.rsqrt(jnp.mean(n * n, axis=-1, keepdims=True) + EPS)
        nr = n * r
        e = nr * fg_v - t_ref[...]
        loss_o[...] += 0.5 * jnp.sum(jnp.mean(e * e, axis=-1, keepdims=True))
        dy = e * (1.0 / D)
        dfg_o[...] += jnp.sum(dy * nr, axis=0, keepdims=True)
        qv = dy * fg_v
        dn = r * (qv - nr * jnp.mean(qv * nr, axis=-1, keepdims=True))
        dgt_o[...] += jnp.sum(dn * o, axis=0, keepdims=True)
        dn_o[...] = dn.astype(dn_o.dtype)
        do_o[...] = (dn * gt_v).astype(do_o.dtype)

    blk = pl.BlockSpec((tr, D), lambda i: (i, 0))
    vec = pl.BlockSpec((1, D), lambda i: (0, 0))
    return pl.pallas_call(
        body, name="loss_head", grid=(L // tr,), in_specs=[blk, blk, blk, vec, vec],
        out_specs=[blk, blk, vec, vec, pl.BlockSpec((SUBLANES, LANES), lambda i: (0, 0))],
        out_shape=[jax.ShapeDtypeStruct((L, D), BF16), jax.ShapeDtypeStruct((L, D), BF16),
                   jax.ShapeDtypeStruct((1, D), F32), jax.ShapeDtypeStruct((1, D), F32),
                   jax.ShapeDtypeStruct((SUBLANES, LANES), F32)],
        compiler_params=_params(("arbitrary",)),
    )(out, x, tgt, gt, fg)


def _norm_bwd(dhl, x, dn, g, sc, tr, name):
    L, D = x.shape
    with_x = dn is not None

    def body(*refs):
        if with_x:
            d_ref, x_ref, dn_ref, g_ref, sc_ref, gx_o, dsh_o, dsc_o, dg_o = refs
        else:
            d_ref, x_ref, g_ref, sc_ref, dsh_o, dsc_o, dg_o = refs
        i = pl.program_id(0)

        @pl.when(i == 0)
        def _():
            dsh_o[...] = jnp.zeros_like(dsh_o)
            dsc_o[...] = jnp.zeros_like(dsc_o)
            dg_o[...] = jnp.zeros_like(dg_o)

        d = d_ref[...].astype(F32)
        xv = x_ref[...]
        g_v = g_ref[...]
        r = lax.rsqrt(jnp.mean(xv * xv, axis=-1, keepdims=True) + EPS)
        xr = xv * r
        dsh_o[...] += jnp.sum(d, axis=0, keepdims=True)
        dsc_o[...] += jnp.sum(d * (xr * g_v), axis=0, keepdims=True)
        dxn = d * (1.0 + sc_ref[...])
        dg_o[...] += jnp.sum(dxn * xr, axis=0, keepdims=True)
        if with_x:
            qv = dxn * g_v
            gx_o[...] = r * (qv - xr * jnp.mean(qv * xr, axis=-1, keepdims=True)) + dn_ref[...].astype(F32)

    blk = pl.BlockSpec((tr, D), lambda i: (i, 0))
    vec = pl.BlockSpec((1, D), lambda i: (0, 0))
    vshape = jax.ShapeDtypeStruct((1, D), F32)
    res = pl.pallas_call(
        body, name=name, grid=(L // tr,),
        in_specs=[blk, blk] + ([blk] if with_x else []) + [vec, vec],
        out_specs=([blk] if with_x else []) + [vec, vec, vec],
        out_shape=([jax.ShapeDtypeStruct((L, D), F32)] if with_x else []) + [vshape] * 3,
        compiler_params=_params(("arbitrary",)),
    )(*([dhl, x] + ([dn] if with_x else []) + [g, sc]))
    return res if with_x else [None] + list(res)


def _pack_blockdiag(wa, wx, gs):
    H, hd, _ = wa.shape
    hp = gs // hd
    ng = H // hp
    eye = jnp.eye(hp, dtype=wa.dtype)

    def bd(w):
        return jnp.einsum("gpij,pq->gpiqj", w.reshape(ng, hp, hd, hd), eye).reshape(ng, gs, gs)

    return jnp.concatenate([bd(wa), bd(wx)], axis=-1).astype(BF16)


def _unpack_blockdiag(dwd, H, hd, gs):
    hp = gs // hd
    ng = H // hp
    eye = jnp.eye(hp, dtype=dwd.dtype)

    def diag(dm):
        return jnp.einsum("gpiqj,pq->gpij", dm.reshape(ng, hp, hd, hp, hd), eye).reshape(H, hd, hd)

    return diag(dwd[:, :, :gs]), diag(dwd[:, :, gs:])


def kernel(x, c, ctx, c_ctx, norm_g, w_ada, b_ada, w_in, w_conv_a, w_conv_b, b_conv_b, lru_wa, lru_ba, lru_wx, lru_bx, lru_lambda, w_out, final_g, loss_target, m_c_ctx, m_norm_g, m_w_ada, m_b_ada, m_w_in, m_w_conv_a, m_w_conv_b, m_b_conv_b, m_lru_wa, m_lru_ba, m_lru_wx, m_lru_bx, m_lru_lambda, m_w_out, m_final_g, v_c_ctx, v_norm_g, v_w_ada, v_b_ada, v_w_in, v_w_conv_a, v_w_conv_b, v_b_conv_b, v_lru_wa, v_lru_ba, v_lru_wx, v_lru_bx, v_lru_lambda, v_w_out, v_final_g):
    xi, yi, ci = _pos()
    me = 4 * xi + 2 * yi + ci
    q = 2 * xi + yi
    first_core = (ci == 0).astype(F32)

    L, D = x.shape[1], x.shape[2]
    T = ctx.shape[1]
    W = D // 2
    Wq = W // 4
    H, hd = lru_wa.shape[2], lru_wa.shape[3]
    gs = min(LRU_GROUP, W)
    nq = w_ada.shape[2]
    tl = min(256, T, L)
    tr = min(256, T, L)
    x2, ctx2, tgt2 = x[0], ctx[0], loss_target[0]

    def place(shard, full_cols):
        z = jnp.zeros((shard.shape[0], full_cols), F32)
        return lax.dynamic_update_slice(z, shard * first_core, (0, q * shard.shape[1]))

    c_rows = lax.dynamic_update_slice(jnp.zeros((8, D), F32), c, (me, 0))
    small_in = [c_rows, place(w_conv_a[0], W), place(w_conv_b[0], W), place(lru_ba[0], W), place(lru_bx[0], W),
                place(lru_lambda[0], W)]
    small_shapes = [a.shape for a in small_in]
    gathered = _allreduce8(_pack(small_in, 8 * SUBLANES), "gather_small")
    c_all, wca, wcb, ba_all, bx_all, lam_all = _unpack(gathered, small_shapes)

    s_rows = jnp.concatenate([c_all, c_ctx[None, :], jnp.zeros((7, D), F32)], axis=0)
    mod_part = _matmul(s_rows, w_ada[0], a_act="silu", bias=lax.dynamic_slice(b_ada, (0, q * nq), (1, nq)),
                       tm=16, tn=nq, tk=512, name="ada_fwd")
    mod_all = _allreduce8(_pack([place(mod_part[:9], 4 * nq)], 8 * SUBLANES), "gather_mod")
    mod_all = _unpack(mod_all, [(9, 4 * nq)])[0]
    mod_l = lax.dynamic_slice(mod_all, (me, 0), (1, 3 * D))
    mod_c = mod_all[8:9]
    sh_l, sc_l, gt_l = mod_l[:, :D], mod_l[:, D:2 * D], mod_l[:, 2 * D:]
    sh_c, sc_c = mod_c[:, :D], mod_c[:, D:2 * D]

    pad_taps = lambda w: jnp.pad(w, ((0, SUBLANES - w.shape[0]), (0, 0)))
    wts = {
        "wca": pad_taps(wca), "wcb": pad_taps(wcb), "bcb": b_conv_b,
        "wd": [_pack_blockdiag(lru_wa[0, d], lru_wx[0, d], gs) for d in range(2)],
        "ba": [ba_all[d:d + 1] for d in range(2)], "bx": [bx_all[d:d + 1] for d in range(2)],
        "lam": [lam_all[d:d + 1] for d in range(2)],
    }

    hl = _norm_in(x2, ctx2, norm_g, sc_l, sh_l, sc_c, sh_c, tr)
    p_lat, win_full, wout_full = _in_proj_gather(hl, w_in[0].astype(BF16), w_out[0].astype(BF16),
                                                 jnp.reshape(q, (1,)).astype(jnp.int32), rows=L, tm=min(1024, L))
    p_ctx = _matmul(hl, win_full, a_rows=T, a_off=L, tm=T, tn=1536, tk=D, out_dtype=BF16, name="in_proj_ctx")
    zero_w = jnp.zeros((1, W), F32)
    ctx0, cgates0 = _mix_fwd(p_ctx, 0, zero_w, wts, rows=T, row_off=0, row_w=T, tl=tl, name="ctx_fwd0")
    c1s, _, cgates1 = _mix_fwd(p_ctx, 1, zero_w, wts, rows=T, row_off=0, row_w=T, tl=tl, saved0=ctx0, name="ctx_fwd1")
    h0_init, h1_init = ctx0["h"][T - 1:T], c1s[0:1]
    tl_fwd = 2 * tl if L % (2 * tl) == 0 else tl
    lat0, gates0 = _mix_fwd(p_lat, 0, h0_init, wts, rows=L, row_off=0, row_w=GRID_W, tl=tl_fwd, name="mix_fwd0")
    h1s, cat, gates1 = _mix_fwd(p_lat, 1, h1_init, wts, rows=L, row_off=0, row_w=GRID_W, tl=tl_fwd, saved0=lat0,
                                name="mix_fwd1")
    out = _matmul(cat, wout_full, tm=512, tn=D, tk=2 * W, out_dtype=BF16, name="out_proj")
    tr_lat = 2 * tr if L % (2 * tr) == 0 else tr
    dn, dout, dfg, dgt, loss_blk = _loss_head(out, x2, tgt2, gt_l, final_g[None, :], tr_lat)

    dcat = _matmul(dout, wout_full, tb=True, tm=512, tn=2 * W, tk=D, out_dtype=BF16, name="out_proj_bwd")
    gw_out = _matmul(cat, dout, ta=True, tm=1024, tn=D, tk=2048, out_dtype=BF16, name="w_out_grad")
    dxb0, dwd0, dba0, dbx0, dlam0, ch0 = _mix_bwd0(p_lat, dcat, lat0, gates0, h0_init, zero_w, wts, rows=L, row_off=0,
                                                   row_w=GRID_W, tl=tl, name="mix_bwd0")
    dp_lat, dwd1, dba1, dbx1, dlam1, dwca, dwcb, dbcb, ch1 = _mix_bwd1(
        p_lat, dcat, lat0, h1s, gates1, dxb0, h1_init, zero_w, wts, rows=L, row_off=0, row_w=GRID_W, tl=tl,
        name="mix_bwd1", dp_rows=L + T)
    zero_cat = jnp.zeros((T, 2 * W), BF16)
    cxb0, cwd0, cba0, cbx0, clam0, _ = _mix_bwd0(p_ctx, zero_cat, ctx0, cgates0, zero_w, ch0, wts, rows=T, row_off=0,
                                                 row_w=T, tl=tl, name="ctx_bwd0")
    dp, cwd1, cba1, cbx1, clam1, cwca, cwcb, cbcb, _ = _mix_bwd1(
        p_ctx, zero_cat, ctx0, c1s, cgates1, cxb0, zero_w, ch1, wts, rows=T, row_off=0, row_w=T, tl=tl, name="ctx_bwd1",
        dp_rows=L + T, dp_off=L, dp_into=dp_lat)

    gw_in = _matmul(hl, dp, ta=True, tm=1024, tn=1536, tk=2816, out_dtype=BF16, name="w_in_grad")
    rs_axes = [1, 0]
    pair_sums = _rs_pair_sums([gw_in, gw_out], rs_axes)
    dhc = _matmul(dp, win_full, tb=True, a_rows=T, a_off=L, tm=T, tn=D, tk=3072, name="in_proj_bwd_ctx")
    _, dsh_c, dsc_c, dng_c = _norm_bwd(dhc, ctx2, None, norm_g, sc_c, tr, "norm_bwd_ctx")
    zeros_d = jnp.zeros((1, D), F32)
    dmod_c = jnp.concatenate([dsh_c, dsc_c, zeros_d], axis=1)
    dhl, (*rs_slots, dmod_c_all) = _matmul(
        dp, win_full, tb=True, a_rows=L, tm=512, tn=D, tk=3072, out_dtype=BF16, name="in_proj_bwd",
        side=_join_sides(_rs_chips_side(pair_sums), _gather_block_side(jnp.pad(dmod_c, ((0, SUBLANES - 1), (0, 0))))))
    gx, dsh_l, dsc_l, dng_l = _norm_bwd(dhl, x2, dn, norm_g, sc_l, tr_lat, "norm_bwd")
    gc_rows = _matmul(lax.dynamic_slice(dmod_c_all.reshape(8 * SUBLANES, 3 * D), (0, q * nq), (8 * SUBLANES, nq)), w_ada[0],
                      tb=True, dsilu_mul=c_ctx[None, :], tm=8 * SUBLANES, tn=D, tk=512, name="c_ctx_grad")
    gc_part = jnp.sum(gc_rows, axis=0, keepdims=True) * first_core

    g_in_shard, g_out_shard = _rs_finish(rs_slots, rs_axes)

    dwa0, dwx0 = _unpack_blockdiag(dwd0 + cwd0, H, hd, gs)
    dwa1, dwx1 = _unpack_blockdiag(dwd1 + cwd1, H, hd, gs)
    dmod_l = jnp.concatenate([dsh_l, dsc_l, dgt], axis=1)
    small_g = [
        lax.dynamic_update_slice(jnp.zeros((8, 3 * D), F32), dmod_l, (me, 0)), dmod_c,
        dfg, dng_l + dng_c, (dwca + cwca)[:3], (dwcb + cwcb)[:4], dbcb + cbcb,
        jnp.stack([dwa0, dwa1]), jnp.stack([dwx0, dwx1]),
        jnp.concatenate([dba0 + cba0, dba1 + cba1], axis=0), jnp.concatenate([dbx0 + cbx0, dbx1 + cbx1], axis=0),
        jnp.concatenate([dlam0 + clam0, dlam1 + clam1], axis=0), loss_blk[0:1, 0:1], gc_part,
    ]
    g_shapes = [a.shape for a in small_g]
    (g_rows, g_modc, g_fg, g_ng, g_wca, g_wcb, g_bcb, g_wa, g_wx, g_ba, g_bx, g_lam, loss_sum, g_c_ctx) = _unpack(
        _allreduce8_two_level(_pack(small_g, 8 * SUBLANES), "reduce_small"), g_shapes)

    g_mod = jnp.concatenate([g_rows, g_modc, jnp.zeros((7, 3 * D), F32)], axis=0)
    g_mod_q = lax.dynamic_slice(g_mod, (0, q * nq), (16, nq))
    g_w_ada = _matmul(s_rows, g_mod_q, ta=True, a_act="silu", tm=1024, tn=nq, tk=16, name="w_ada_grad")
    g_b_ada = jnp.sum(g_mod[:9], axis=0, keepdims=True)

    def shard_cols(a, width):
        return lax.dynamic_slice(a, (0, q * width), (a.shape[0], width))

    grads = {
        "c_ctx": g_c_ctx, "norm_g": g_ng, "b_ada": g_b_ada,
        "w_conv_a": shard_cols(g_wca, Wq)[None], "w_conv_b": shard_cols(g_wcb, Wq)[None], "b_conv_b": g_bcb,
        "lru_wa": g_wa[None], "lru_ba": shard_cols(g_ba, Wq)[None], "lru_wx": g_wx[None],
        "lru_bx": shard_cols(g_bx, Wq)[None], "lru_lambda": shard_cols(g_lam, Wq)[None], "final_g": g_fg[0],
    }
    small_names = list(grads)
    given = dict(c_ctx=(c_ctx, m_c_ctx, v_c_ctx), norm_g=(norm_g, m_norm_g, v_norm_g), b_ada=(b_ada, m_b_ada, v_b_ada),
                 w_conv_a=(w_conv_a, m_w_conv_a, v_w_conv_a), w_conv_b=(w_conv_b, m_w_conv_b, v_w_conv_b),
                 b_conv_b=(b_conv_b, m_b_conv_b, v_b_conv_b), lru_wa=(lru_wa, m_lru_wa, v_lru_wa),
                 lru_ba=(lru_ba, m_lru_ba, v_lru_ba), lru_wx=(lru_wx, m_lru_wx, v_lru_wx),
                 lru_bx=(lru_bx, m_lru_bx, v_lru_bx), lru_lambda=(lru_lambda, m_lru_lambda, v_lru_lambda),
                 final_g=(final_g, m_final_g, v_final_g))
    def rows2d(a):
        return a.reshape(-1, a.shape[-1])

    grads = {n: grads[n].reshape(given[n][0].shape) for n in small_names}
    quads = [tuple(rows2d(a) for a in (given[n][0], grads[n], given[n][1], given[n][2])) for n in small_names]
    updated = _adam_many(quads, "adam_small")
    delta_s, newm_s, newv_s = ({n: u[j].reshape(given[n][0].shape) for n, u in zip(small_names, updated)} for j in range(3))

    big = {"w_ada": (w_ada, g_w_ada, m_w_ada, v_w_ada), "w_in": (w_in, g_in_shard, m_w_in, v_w_in),
           "w_out": (w_out, g_out_shard, m_w_out, v_w_out)}
    delta_b, newm_b, newv_b = {}, {}, {}
    for n, (w, g, m, v) in big.items():
        d_, m_, v_, *echo = _adam(w[0], g, m[0], v[0], "adam_" + n, echo_g=n != "w_ada")
        grads[n] = (echo[0] if echo else g)[None]
        delta_b[n], newm_b[n], newv_b[n] = d_[None], m_[None], v_[None]

    loss = loss_sum[0, 0]
    order = ["c_ctx", "norm_g", "w_ada", "b_ada", "w_in", "w_conv_a", "w_conv_b", "b_conv_b", "lru_wa", "lru_ba",
             "lru_wx", "lru_bx", "lru_lambda", "w_out", "final_g"]
    delta = {**delta_s, **delta_b}
    newm = {**newm_s, **newm_b}
    newv = {**newv_s, **newv_b}
    return (loss, gx[None], *[grads[n] for n in order], *[delta[n] for n in order], *[newm[n] for n in order],
            *[newv[n] for n in order])
```

```python
import functools

import jax
import jax.numpy as jnp
from jax import lax
from jax.experimental import pallas as pl
from jax.experimental.pallas import tpu as pltpu

F32 = jnp.float32
BF16 = jnp.bfloat16
MESH_ID = pl.DeviceIdType.MESH

EPS = 1e-6
LRU_C = 8.0
GRID_W = 64
ADAM_LR = 0.001
ADAM_B1 = 0.9
ADAM_B2 = 0.999
ADAM_EPS = 1e-08
ADAM_WD = 0.01
ADAM_STEP = 10

LANES = 128
SUBLANES = 8
PACK_COLS = 1024
VMEM_LIMIT = 56 * 2**20
LRU_GROUP = 256


def _params(sem=None):
    return pltpu.CompilerParams(vmem_limit_bytes=VMEM_LIMIT, dimension_semantics=sem)


def _pick(dim, pref, quantum=LANES):
    if dim <= pref:
        return dim
    best = None
    for t in range(quantum, pref + 1, quantum):
        if dim % t == 0:
            best = t
    assert best is not None, (dim, pref)
    return best


def _pos():
    return lax.axis_index("x"), lax.axis_index("y"), lax.axis_index("c")


def _flip(v, bit):
    return 1 - v if bit else v


def _sigmoid(v):
    return 0.5 * jnp.tanh(0.5 * v) + 0.5


def _silu(v):
    return v * _sigmoid(v)


def _dsilu(v):
    s = _sigmoid(v)
    return s * (1.0 + v * (1.0 - s))


def _gates(pre_r, pre_i, sp):
    r = _sigmoid(pre_r)
    ig = _sigmoid(pre_i)
    e = LRU_C * r * sp
    w = jnp.tanh(e)
    return r, ig, jnp.exp(-e), (2.0 * w) * pl.reciprocal(1.0 + w, approx=True)


def _softplus(z):
    return jnp.maximum(z, 0.0) + jnp.log1p(jnp.exp(-jnp.abs(z)))


def _matmul(a, b, *, ta=False, tb=False, tm=512, tn=512, tk=512, out_dtype=F32, name,
            a_rows=None, a_off=0, a_act=None, bias=None, dsilu_mul=None, side=None):
    rows_a = a.shape[0] if a_rows is None else a_rows
    if ta:
        K, M = rows_a, a.shape[1]
    else:
        M, K = rows_a, a.shape[1]
    N = b.shape[0] if tb else b.shape[1]
    tm, tn, tk = _pick(M, tm, SUBLANES), _pick(N, tn), _pick(K, tk)
    t_rows = tk if ta else tm
    assert a_off % t_rows == 0
    nk = K // tk
    gi, gj = M // tm, N // tn
    off_blocks = a_off // t_rows
    dims = (((0 if ta else 1,), (1 if tb else 0,)), ((), ()))
    extras = [e for e in (bias, dsilu_mul) if e is not None]
    n_sin = len(side["ins"]) if side else 0
    n_sout = len(side["outs"]) if side else 0

    def body(a_ref, b_ref, *rest):
        rest = list(rest)
        bias_ref = rest.pop(0) if bias is not None else None
        dsm_ref = rest.pop(0) if dsilu_mul is not None else None
        side_in = [rest.pop(0) for _ in range(n_sin)]
        o_ref = rest.pop(0)
        side_out = [rest.pop(0) for _ in range(n_sout)]
        acc_ref = rest.pop(0) if nk > 1 else None
        side_scr = rest
        i, j, k = pl.program_id(0), pl.program_id(1), pl.program_id(2)

        if side:
            @pl.when((i == 0) & (j == 0) & (k == 0))
            def _():
                side["start"](side_in, side_out, side_scr)

        av = a_ref[...]
        if a_act == "silu":
            av = _silu(av)
        prod = lax.dot_general(av, b_ref[...], dims, preferred_element_type=F32)

        def finish(r):
            if bias_ref is not None:
                r = r + bias_ref[...]
            if dsm_ref is not None:
                r = r * _dsilu(dsm_ref[...])
            o_ref[...] = r.astype(o_ref.dtype)

        if nk == 1:
            finish(prod)
        else:
            @pl.when(k == 0)
            def _():
                acc_ref[...] = prod

            @pl.when(k > 0)
            def _():
                acc_ref[...] += prod

            @pl.when(k == nk - 1)
            def _():
                finish(acc_ref[...])

        if side:
            @pl.when((i == gi - 1) & (j == gj - 1) & (k == nk - 1))
            def _():
                side["finish"](side_in, side_out, side_scr)

    if ta:
        a_spec = pl.BlockSpec((tk, tm), lambda i, j, k: (k + off_blocks, i))
    else:
        a_spec = pl.BlockSpec((tm, tk), lambda i, j, k: (i + off_blocks, k))
    if tb:
        b_spec = pl.BlockSpec((tn, tk), lambda i, j, k: (j, k))
    else:
        b_spec = pl.BlockSpec((tk, tn), lambda i, j, k: (k, j))
    in_specs = [a_spec, b_spec]
    if bias is not None:
        in_specs.append(pl.BlockSpec((1, tn), lambda i, j, k: (0, j)))
    if dsilu_mul is not None:
        in_specs.append(pl.BlockSpec((1, tn), lambda i, j, k: (0, j)))
    hbm = pl.BlockSpec(memory_space=pl.ANY)
    res = pl.pallas_call(
        body, name=name, grid=(gi, gj, nk),
        in_specs=in_specs + [hbm] * n_sin,
        out_specs=[pl.BlockSpec((tm, tn), lambda i, j, k: (i, j))] + [hbm] * n_sout,
        out_shape=[jax.ShapeDtypeStruct((M, N), out_dtype)] + (list(side["outs"]) if side else []),
        scratch_shapes=([pltpu.VMEM((tm, tn), F32)] if nk > 1 else []) + (list(side["scratch"]) if side else []),
        compiler_params=_params(("arbitrary",) * 3 if side else ("parallel", "parallel", "arbitrary")),
    )(a, b, *extras, *(side["ins"] if side else []))
    return (res[0], res[1:]) if side else res[0]


def _elementwise(fn, ins, outs, *, rows, cols, name, tr=256):
    tr = _pick(rows, tr, 2 * SUBLANES)
    n_in = len(ins)

    def body(*refs):
        vals = fn(*[r[...] for r in refs[:n_in]])
        if not isinstance(vals, (tuple, list)):
            vals = (vals,)
        for r, v in zip(refs[n_in:], vals, strict=True):
            r[...] = v.astype(r.dtype)

    def spec(off):
        assert off % tr == 0
        ob = off // tr
        return pl.BlockSpec((tr, cols), lambda i: (i + ob, 0))

    res = pl.pallas_call(
        body, name=name, grid=(rows // tr,),
        in_specs=[spec(off) for _, off in ins],
        out_specs=[spec(0) for _ in outs],
        out_shape=[jax.ShapeDtypeStruct((rows, cols), dt) for dt in outs],
        compiler_params=_params(("parallel",)),
    )(*[a for a, _ in ins])
    return res


def _adam_math(w, g, m, v):
    m = ADAM_B1 * m + (1.0 - ADAM_B1) * g
    v = ADAM_B2 * v + (1.0 - ADAM_B2) * (g * g)
    m_hat = m / (1.0 - ADAM_B1 ** ADAM_STEP)
    v_hat = v / (1.0 - ADAM_B2 ** ADAM_STEP)
    delta = -ADAM_LR * (m_hat / (jnp.sqrt(v_hat) + ADAM_EPS) + ADAM_WD * w)
    return delta, m, v


def _adam(w, g, m, v, name, echo_g=False):
    rows, cols = w.shape
    fn = (lambda w_, g_, m_, v_: _adam_math(w_, g_, m_, v_) + (g_,)) if echo_g else _adam_math
    return _elementwise(fn, [(w, 0), (g, 0), (m, 0), (v, 0)], [F32] * (4 if echo_g else 3),
                        rows=rows, cols=cols, name=name)


def _adam_many(quads, name):
    n = len(quads)

    def body(*refs):
        ins, outs = refs[:4 * n], refs[4 * n:]
        for t in range(n):
            w, g, m, v = (r[...] for r in ins[4 * t:4 * t + 4])
            for o_ref, val in zip(outs[3 * t:3 * t + 3], _adam_math(w, g, m, v), strict=True):
                o_ref[...] = val

    res = pl.pallas_call(
        body, name=name,
        out_shape=[jax.ShapeDtypeStruct(q[0].shape, F32) for q in quads for _ in range(3)],
        compiler_params=_params(),
    )(*[a for q in quads for a in q])
    return [res[3 * t:3 * t + 3] for t in range(n)]


def _pack(arrs, row_quantum):
    flat = jnp.concatenate([a.reshape(-1).astype(F32) for a in arrs])
    n = flat.shape[0]
    q = row_quantum * PACK_COLS
    total = -(-n // q) * q
    flat = jnp.pad(flat, (0, total - n))
    return flat.reshape(total // PACK_COLS, PACK_COLS)


def _unpack(buf, shapes):
    flat = buf.reshape(-1)
    out, off = [], 0
    for s in shapes:
        n = 1
        for d in s:
            n *= d
        out.append(flat[off:off + n].reshape(s))
        off += n
    return out


def _allreduce8(buf, name):
    R, C = buf.shape
    assert R % (8 * SUBLANES) == 0
    m = R // 8

    def body(x_ref, o_ref, recv, red, s1, r1, s2, r2):
        x, y, c = _pos()
        me = 4 * x + 2 * y + c

        def peer(k):
            px, py, pc = _flip(x, (k >> 2) & 1), _flip(y, (k >> 1) & 1), _flip(c, k & 1)
            return (px, py, pc), 4 * px + 2 * py + pc

        def rows(ref, idx):
            return ref.at[pl.ds(pl.multiple_of(idx * m, SUBLANES), m), :]

        def scatter(k):
            dev, p = peer(k)
            return pltpu.make_async_remote_copy(src_ref=rows(x_ref, p), dst_ref=recv.at[k], send_sem=s1.at[k],
                                                recv_sem=r1.at[k], device_id=dev, device_id_type=MESH_ID)

        def share(k):
            dev, p = peer(k)
            return pltpu.make_async_remote_copy(src_ref=red, dst_ref=rows(o_ref, me), send_sem=s2.at[k],
                                                recv_sem=r2.at[k], device_id=dev, device_id_type=MESH_ID)

        def shared_from(k):
            dev, p = peer(k)
            return pltpu.make_async_remote_copy(src_ref=red, dst_ref=rows(o_ref, p), send_sem=s2.at[k],
                                                recv_sem=r2.at[k], device_id=dev, device_id_type=MESH_ID)

        for k in range(1, 8):
            scatter(k).start()
        acc = rows(x_ref, me)[...]
        for k in range(1, 8):
            scatter(k).wait_recv()
            acc = acc + recv[k]
        red[...] = acc
        rows(o_ref, me)[...] = acc
        for k in range(1, 8):
            share(k).start()
        for k in range(1, 8):
            shared_from(k).wait_recv()
        for k in range(1, 8):
            scatter(k).wait_send()
            share(k).wait_send()

    return pl.pallas_call(
        body, name=name,
        in_specs=[pl.BlockSpec(memory_space=pltpu.VMEM)],
        out_specs=pl.BlockSpec(memory_space=pltpu.VMEM),
        out_shape=jax.ShapeDtypeStruct((R, C), F32),
        scratch_shapes=[pltpu.VMEM((8, m, C), F32), pltpu.VMEM((m, C), F32),
                        pltpu.SemaphoreType.DMA((8,)), pltpu.SemaphoreType.DMA((8,)),
                        pltpu.SemaphoreType.DMA((8,)), pltpu.SemaphoreType.DMA((8,))],
        compiler_params=_params(),
    )(buf)


def _allreduce8_two_level(buf, name):
    R, C = buf.shape
    assert R % (8 * SUBLANES) == 0
    m, hr = R // 8, R // 2

    def body(x_ref, o_ref, got0, half, got1, red, ssem, rsem):
        x, y, c = _pos()
        q = 2 * x + y
        sibling = (x, y, 1 - c)

        def half_rows(ref, core):
            return ref.at[pl.ds(pl.multiple_of(core * hr, SUBLANES), hr), :]

        def chunk(ref, core, chip):
            return ref.at[pl.ds(pl.multiple_of(core * hr + chip * m, SUBLANES), m), :]

        def chip_of(k):
            px, py = _flip(x, (k >> 1) & 1), _flip(y, k & 1)
            return (px, py, c), 2 * px + py

        def copy(src, dst, phase, k, dev):
            return pltpu.make_async_remote_copy(src_ref=src, dst_ref=dst, send_sem=ssem.at[phase, k],
                                                recv_sem=rsem.at[phase, k], device_id=dev, device_id_type=MESH_ID)

        swap = copy(half_rows(x_ref, 1 - c), got0, 0, 0, sibling)
        swap.start()
        swap.wait()
        half[...] = half_rows(x_ref, c)[...] + got0[...]

        def scatter(k):
            dev, p = chip_of(k)
            return copy(half.at[pl.ds(pl.multiple_of(p * m, SUBLANES), m), :], got1.at[k], 1, k, dev)

        for k in range(1, 4):
            scatter(k).start()
        acc = half[pl.ds(pl.multiple_of(q * m, SUBLANES), m), :]
        for k in range(1, 4):
            scatter(k).wait_recv()
            acc = acc + got1[k]
        red[...] = acc
        chunk(o_ref, c, q)[...] = acc

        def share(k, landing_chip):
            return copy(red, chunk(o_ref, c, landing_chip), 2, k, chip_of(k)[0])

        for k in range(1, 4):
            share(k, q).start()
        for k in range(1, 4):
            share(k, chip_of(k)[1]).wait_recv()
        back = copy(half_rows(o_ref, c), half_rows(o_ref, c), 3, 0, sibling)
        back.start()
        copy(half_rows(o_ref, 1 - c), half_rows(o_ref, 1 - c), 3, 0, sibling).wait_recv()
        back.wait_send()
        for k in range(1, 4):
            scatter(k).wait_send()
            share(k, q).wait_send()

    return pl.pallas_call(
        body, name=name,
        in_specs=[pl.BlockSpec(memory_space=pltpu.VMEM)],
        out_specs=pl.BlockSpec(memory_space=pltpu.VMEM),
        out_shape=jax.ShapeDtypeStruct((R, C), F32),
        scratch_shapes=[pltpu.VMEM((hr, C), F32), pltpu.VMEM((hr, C), F32), pltpu.VMEM((4, m, C), F32), pltpu.VMEM((m, C), F32),
                        pltpu.SemaphoreType.DMA((4, 4)), pltpu.SemaphoreType.DMA((4, 4))],
        compiler_params=_params(),
    )(buf)


def _bounce(src, dst, buf, sem):
    cin = pltpu.make_async_copy(src, buf, sem)
    cin.start()
    cin.wait()
    cout = pltpu.make_async_copy(buf, dst, sem)
    cout.start()
    cout.wait()


def _chunk(ref, axis, idx, size):
    start = idx * size
    if axis == 0:
        return ref.at[pl.ds(start, size), :]
    return ref.at[:, pl.ds(start, size)]


def _in_proj_gather(hl, win, wout, q_arr, *, rows, tm):
    D, nq = win.shape
    dq, D2 = wout.shape
    ni = rows // tm
    ops = ((0, 1, nq, D // 2), (1, 0, dq, dq // 2))

    def body(q_ref, a_ref, win_ref, wout_ref, p_ref, gin_ref, gout_ref, b_scr, buf_out, lsem, ssem, rsem, fsem, gsem):
        j, i = pl.program_id(0), pl.program_id(1)
        x, y, c = _pos()
        q = 2 * x + y
        srcs = (win_ref, wout_ref)
        dsts = (gin_ref, gout_ref)

        def shard_window(o, chip):
            _, axis, size, _ = ops[o]
            return _chunk(dsts[o], axis, chip, size)

        def half(ref, o, core):
            return ref.at[pl.ds(core * ops[o][3], ops[o][3]), :]

        def half_window(o, chip, core):
            _, axis, size, hs = ops[o]
            if axis == 1:
                return dsts[o].at[pl.ds(core * hs, hs), pl.ds(chip * size, size)]
            return dsts[o].at[pl.ds(chip * size + core * hs, hs), :]

        def chip_of(k):
            px, py = _flip(x, (k >> 1) & 1), _flip(y, k & 1)
            return px, py, 2 * px + py

        def send(o, k):
            px, py, _ = chip_of(k)
            return pltpu.make_async_remote_copy(
                src_ref=half(srcs[o], o, c), dst_ref=half_window(o, q, c), send_sem=ssem.at[o, k],
                recv_sem=rsem.at[o, k], device_id=(px, py, c), device_id_type=MESH_ID)

        def chip_recv(o, k):
            px, py, pq = chip_of(k)
            landed = half_window(o, pq, c)
            pltpu.make_async_remote_copy(src_ref=landed, dst_ref=landed, send_sem=ssem.at[o, k], recv_sem=rsem.at[o, k],
                                         device_id=(px, py, c), device_id_type=MESH_ID).wait_recv()

        def to_sibling(o, k):
            landed = half_window(o, chip_of(k)[2], c)
            return pltpu.make_async_remote_copy(src_ref=landed, dst_ref=landed, send_sem=fsem.at[o, k],
                                                recv_sem=gsem.at[o, k], device_id=(x, y, 1 - c), device_id_type=MESH_ID)

        def from_sibling(o, k):
            theirs = half_window(o, chip_of(k)[2], 1 - c)
            pltpu.make_async_remote_copy(src_ref=theirs, dst_ref=theirs, send_sem=fsem.at[o, k], recv_sem=gsem.at[o, k],
                                         device_id=(x, y, 1 - c), device_id_type=MESH_ID).wait_recv()

        def relay(o, core):
            if core == 0:
                landed, target = half_window(o, chip_of(2)[2], 0), (x, 1 - y, 0)
            else:
                landed, target = half_window(o, chip_of(1)[2], 1), (1 - x, y, 1)
            return pltpu.make_async_remote_copy(src_ref=landed, dst_ref=landed, send_sem=ssem.at[o, 3],
                                                recv_sem=rsem.at[o, 3], device_id=target, device_id_type=MESH_ID)

        def on_core(core, fn):
            @pl.when(c == core)
            def _():
                fn()

        def land(o, k):
            chip_recv(o, k)
            if k == 2:
                on_core(0, lambda: relay(o, 0).start())
            if k == 1:
                on_core(1, lambda: relay(o, 1).start())
            to_sibling(o, k).start()

        def settle(o, k):
            from_sibling(o, k)
            to_sibling(o, k).wait_send()

        def b_load(k, slot):
            src = win_ref if k == 0 else shard_window(0, chip_of(k)[2])
            return pltpu.make_async_copy(src, b_scr.at[slot], lsem.at[0])

        def own_store():
            return pltpu.make_async_copy(b_scr.at[0], shard_window(0, q), lsem.at[2])

        order = (0, 2, 1, 3)
        early = max(ni - 2, 0)

        @pl.when((j == 0) & (i == 0))
        def _():
            for o in range(2):
                for k in (2, 1):
                    send(o, k).start()
            first = b_load(0, 0)
            first.start()
            first.wait()
            own_store().start()
            _bounce(wout_ref, shard_window(1, q), buf_out, lsem.at[1])

        for jj in range(3):
            nxt = order[jj + 1]

            @pl.when((j == jj) & (i == early))
            def _(nxt=nxt):
                land(0, nxt)

            @pl.when((j == jj) & (i == ni - 1))
            def _(jj=jj, nxt=nxt):
                settle(0, nxt)
                if jj == 1:
                    own_store().wait()
                b_load(nxt, (jj + 1) % 2).start()

            @pl.when((j == jj + 1) & (i == 0))
            def _(jj=jj, nxt=nxt):
                b_load(nxt, (jj + 1) % 2).wait()

        @pl.when((j == 3) & (i == 0))
        def _():
            land(1, 2)
            land(1, 1)

        p_ref[...] = jnp.dot(a_ref[...], b_scr[j % 2], preferred_element_type=F32).astype(p_ref.dtype)

        @pl.when((j == 3) & (i == ni - 1))
        def _():
            settle(1, 2)
            settle(1, 1)
            land(1, 3)
            settle(1, 3)
            for o in range(2):
                for k in (2, 1):
                    send(o, k).wait_send()
                for core in range(2):
                    on_core(core, lambda o=o, core=core: relay(o, core).wait_send())

    hbm = pl.BlockSpec(memory_space=pl.ANY)
    grid_spec = pltpu.PrefetchScalarGridSpec(
        num_scalar_prefetch=1, grid=(4, ni),
        in_specs=[pl.BlockSpec((tm, D), lambda j, i, qr: (i, 0)), hbm, hbm],
        out_specs=[pl.BlockSpec((tm, nq), lambda j, i, qr: (i, jnp.bitwise_xor(qr[0], ((j & 1) << 1) | (j >> 1)))),
                   hbm, hbm],
        scratch_shapes=[pltpu.VMEM((2,) + win.shape, win.dtype), pltpu.VMEM(wout.shape, wout.dtype), pltpu.SemaphoreType.DMA((3,))]
        + [pltpu.SemaphoreType.DMA((2, 4)) for _ in range(4)])
    return pl.pallas_call(
        body, name="in_proj_gather", grid_spec=grid_spec,
        out_shape=[jax.ShapeDtypeStruct((rows, 4 * nq), BF16), jax.ShapeDtypeStruct((D, 4 * nq), win.dtype),
                   jax.ShapeDtypeStruct((4 * dq, D2), wout.dtype)],
        compiler_params=_params(("arbitrary", "arbitrary")),
    )(q_arr, hl, win, wout)


def _rs_to_sibling(gs, axes):
    n = len(gs)
    shapes = []
    for g, ax in zip(gs, axes):
        s = list(g.shape)
        s[ax] //= 8
        shapes.append(tuple(s))

    def body(*refs):
        g_refs, mine, landed = refs[:n], refs[n:2 * n], refs[2 * n:3 * n]
        bufs = refs[3 * n:4 * n]
        lsem, ssem, rsem = refs[4 * n:]
        x, y, c = _pos()
        cps = []
        for o in range(n):
            size = shapes[o][axes[o]]
            for j in range(4):
                rc = pltpu.make_async_remote_copy(
                    src_ref=_chunk(g_refs[o], axes[o], 2 * j + 1 - c, size), dst_ref=landed[o].at[j],
                    send_sem=ssem.at[o, j], recv_sem=rsem.at[o, j], device_id=(x, y, 1 - c), device_id_type=MESH_ID)
                rc.start()
                cps.append(rc)
        for o in range(n):
            size = shapes[o][axes[o]]
            for j in range(4):
                _bounce(_chunk(g_refs[o], axes[o], 2 * j + c, size), mine[o].at[j], bufs[o], lsem.at[o])
        for rc in cps:
            rc.wait()

    hbm = pl.BlockSpec(memory_space=pl.ANY)
    outs = [jax.ShapeDtypeStruct((4,) + s, g.dtype) for s, g in zip(shapes, gs)]
    res = pl.pallas_call(
        body, name="rs_to_sibling", in_specs=[hbm] * n, out_specs=[hbm] * (2 * n), out_shape=outs + outs,
        scratch_shapes=[pltpu.VMEM(s, g.dtype) for s, g in zip(shapes, gs)]
        + [pltpu.SemaphoreType.DMA((n,)), pltpu.SemaphoreType.DMA((n, 4)), pltpu.SemaphoreType.DMA((n, 4))],
        compiler_params=_params(),
    )(*gs)
    return res[:n], res[n:]


def _gather_block_side(block):
    r, n_cols = block.shape

    def copies(ins, outs, scr):
        ssem, rsem = scr[1], scr[2]
        x, y, c = _pos()
        return [pltpu.make_async_remote_copy(
            src_ref=ins[0], dst_ref=outs[0].at[k], send_sem=ssem.at[k], recv_sem=rsem.at[k],
            device_id=(_flip(x, (k >> 2) & 1), _flip(y, (k >> 1) & 1), _flip(c, k & 1)), device_id_type=MESH_ID)
            for k in range(1, 8)]

    def start(ins, outs, scr):
        for cp in copies(ins, outs, scr):
            cp.start()

    def finish(ins, outs, scr):
        _bounce(ins[0], outs[0].at[0], scr[0], scr[3].at[0])
        for cp in copies(ins, outs, scr):
            cp.wait()

    return dict(ins=[block], outs=[jax.ShapeDtypeStruct((8, r, n_cols), block.dtype)],
                scratch=[pltpu.VMEM((r, n_cols), block.dtype), pltpu.SemaphoreType.DMA((8,)), pltpu.SemaphoreType.DMA((8,)),
                         pltpu.SemaphoreType.DMA((1,))],
                start=start, finish=finish)


def _join_sides(a, b):
    na_i, na_o, na_s = len(a["ins"]), len(a["outs"]), len(a["scratch"])

    def run(which):
        def fn(ins, outs, scr):
            a[which](ins[:na_i], outs[:na_o], scr[:na_s])
            b[which](ins[na_i:], outs[na_o:], scr[na_s:])
        return fn

    return dict(ins=a["ins"] + b["ins"], outs=a["outs"] + b["outs"], scratch=a["scratch"] + b["scratch"],
                start=run("start"), finish=run("finish"))


def _rs_chips_side(parts):
    n = len(parts)

    def copies(p_refs, slots, scr):
        ssem, rsem = scr[n + 1], scr[n + 2]
        x, y, c = _pos()
        cps = []
        for o in range(n):
            for k in range(1, 4):
                px, py = _flip(x, (k >> 1) & 1), _flip(y, k & 1)
                cps.append(pltpu.make_async_remote_copy(
                    src_ref=p_refs[o].at[2 * px + py], dst_ref=slots[o].at[k], send_sem=ssem.at[o, k],
                    recv_sem=rsem.at[o, k], device_id=(px, py, c), device_id_type=MESH_ID))
        return cps

    def start(p_refs, slots, scr):
        for cp in copies(p_refs, slots, scr):
            cp.start()

    def finish(p_refs, slots, scr):
        x, y, _ = _pos()
        q = 2 * x + y
        for o in range(n):
            _bounce(p_refs[o].at[q], slots[o].at[0], scr[o], scr[n].at[o])
        for cp in copies(p_refs, slots, scr):
            cp.wait()

    return dict(
        ins=list(parts), outs=[jax.ShapeDtypeStruct(p.shape, p.dtype) for p in parts],
        scratch=[pltpu.VMEM(p.shape[1:], p.dtype) for p in parts]
        + [pltpu.SemaphoreType.DMA((n,)), pltpu.SemaphoreType.DMA((n, 4)), pltpu.SemaphoreType.DMA((n, 4))],
        start=start, finish=finish)


def _rs_share(rs, axes):
    n = len(rs)
    shapes = []
    for r, ax in zip(rs, axes):
        s = list(r.shape)
        s[ax] *= 2
        shapes.append(tuple(s))

    def body(*refs):
        r_refs, outs = refs[:n], refs[n:2 * n]
        bufs = refs[2 * n:3 * n]
        lsem, ssem, rsem = refs[3 * n:]
        x, y, c = _pos()
        cps = []
        for o in range(n):
            size = r_refs[o].shape[axes[o]]
            window = _chunk(outs[o], axes[o], c, size)
            rc = pltpu.make_async_remote_copy(src_ref=r_refs[o], dst_ref=window, send_sem=ssem.at[o], recv_sem=rsem.at[o],
                                              device_id=(x, y, 1 - c), device_id_type=MESH_ID)
            rc.start()
            cps.append(rc)
        for o in range(n):
            size = r_refs[o].shape[axes[o]]
            _bounce(r_refs[o], _chunk(outs[o], axes[o], c, size), bufs[o], lsem.at[o])
        for cp in cps:
            cp.wait()

    hbm = pl.BlockSpec(memory_space=pl.ANY)
    return pl.pallas_call(
        body, name="rs_share", in_specs=[hbm] * n, out_specs=[hbm] * n,
        out_shape=[jax.ShapeDtypeStruct(s, r.dtype) for s, r in zip(shapes, rs)],
        scratch_shapes=[pltpu.VMEM(r.shape, r.dtype) for r in rs] + [pltpu.SemaphoreType.DMA((n,)) for _ in range(3)],
        compiler_params=_params(),
    )(*rs)


def _rs_pair_sums(gs, axes):
    mine, landed = _rs_to_sibling(gs, axes)
    pair_sums = []
    for o, (mi, la) in enumerate(zip(mine, landed)):
        rows, cols = mi.shape[0] * mi.shape[1], mi.shape[2]
        s = _elementwise(lambda a, b: a.astype(F32) + b.astype(F32), [(mi.reshape(rows, cols), 0), (la.reshape(rows, cols), 0)],
                         [BF16], rows=rows, cols=cols, name=f"rs_pair_sum{o}")[0]
        pair_sums.append(s.reshape(mi.shape))
    return pair_sums


def _rs_finish(slots, axes):
    reduced = []
    for o, sl in enumerate(slots):
        rows, cols = sl.shape[1], sl.shape[2]
        flat = sl.reshape(4 * rows, cols)
        r = _elementwise(lambda a, b, c, d: (a.astype(F32) + b.astype(F32)) + (c.astype(F32) + d.astype(F32)),
                         [(flat, k * rows) for k in range(4)], [F32], rows=rows, cols=cols, name=f"rs_chip_sum{o}")[0]
        reduced.append(r)
    return _rs_share(reduced, axes)


def _norm_in(x, ctx, g, sc_l, sh_l, sc_c, sh_c, tr):
    L, D = x.shape
    T = ctx.shape[0]
    nx, nc = L // tr, T // tr

    def body(x_ref, c_ref, g_ref, scl, shl, scc, shc, o_ref):
        i = pl.program_id(0)

        def run(src, sc, sh):
            v = src[...]
            r = lax.rsqrt(jnp.mean(v * v, axis=-1, keepdims=True) + EPS)
            o_ref[...] = ((v * r * g_ref[...]) * (1.0 + sc[...]) + sh[...]).astype(o_ref.dtype)

        @pl.when(i < nx)
        def _():
            run(x_ref, scl, shl)

        @pl.when(i >= nx)
        def _():
            run(c_ref, scc, shc)

    vec = pl.BlockSpec((1, D), lambda i: (0, 0))
    return pl.pallas_call(
        body, name="norm_in", grid=(nx + nc,),
        in_specs=[pl.BlockSpec((tr, D), lambda i: (jnp.minimum(i, nx - 1), 0)),
                  pl.BlockSpec((tr, D), lambda i: (jnp.maximum(i - nx, 0), 0)), vec, vec, vec, vec, vec],
        out_specs=pl.BlockSpec((tr, D), lambda i: (i, 0)),
        out_shape=jax.ShapeDtypeStruct((L + T, D), BF16),
        compiler_params=_params(("arbitrary",)),
    )(x, ctx, g, sc_l, sh_l, sc_c, sh_c)


def _tmod(tl, row_w):
    assert row_w & (row_w - 1) == 0
    return lax.broadcasted_iota(jnp.int32, (tl, 1), 0) & (row_w - 1)


def _shift(z, k, tmod, row_w):
    tl = z.shape[0]
    rolled = pltpu.roll(z, k % tl, 0)
    mask = (tmod >= k) if k > 0 else (tmod < row_w + k)
    return jnp.where(mask, rolled, 0.0)


def _conv(z, w_ref, taps, left, tmod, row_w, lanes=slice(None)):
    out = None
    for j in range(taps):
        k = left - j
        term = (z if k == 0 else _shift(z, k, tmod, row_w)) * w_ref[j:j + 1, lanes]
        out = term if out is None else out + term
    return out


def _conv_bwd(dz, z, w_ref, taps, left, tmod, row_w, lanes=slice(None)):
    din = None
    dws = []
    for j in range(taps):
        k = left - j
        shifted = dz if k == 0 else _shift(dz, -k, tmod, row_w)
        term = shifted * w_ref[j:j + 1, lanes]
        din = term if din is None else din + term
        dws.append(jnp.sum(shifted * z, axis=0, keepdims=True))
    return din, dws


def _gate_matmul(xb16_ref, wd_ref, pre_scr, W, ng, gs):
    for g in range(ng):
        pg = jnp.dot(xb16_ref[:, g * gs:(g + 1) * gs], wd_ref[g], preferred_element_type=F32)
        pre_scr[:, g * gs:(g + 1) * gs] = pg[:, :gs]
        pre_scr[:, W + g * gs:W + (g + 1) * gs] = pg[:, gs:]


def _f32(ref, rows, lanes):
    return ref[rows, lanes].astype(F32)


def _sub_loop(tl, sub_r, W, fn):
    def chunk(ci, carry):
        r0 = pl.multiple_of(ci * sub_r, sub_r)
        for lb in range(W // LANES):
            fn(r0, lb * LANES)
        return carry

    lax.fori_loop(0, tl // sub_r, chunk, 0)


def _row_loop(tl, rev, step, init):
    nchunk = tl // SUBLANES

    def chunk(j, carry):
        jj = (nchunk - 1 - j) if rev else j
        c0 = pl.multiple_of(jj * SUBLANES, SUBLANES)
        for r in (range(SUBLANES - 1, -1, -1) if rev else range(SUBLANES)):
            carry = step(c0 + r, carry)
        return carry

    return lax.fori_loop(0, nchunk, chunk, init)


def _mix_fwd(P, d, h_init, wts, *, rows, row_off, row_w, tl, saved0=None, name):
    W = P.shape[1] // 6
    nt = rows // tl
    ob = row_off // tl
    rev = d == 1
    gs = min(LRU_GROUP, W)
    ng = W // gs
    wca, wcb, bcb = wts["wca"], wts["wcb"], wts["bcb"]
    wd, ba, bx, lam = wts["wd"][d], wts["ba"][d], wts["bx"][d], wts["lam"][d]

    def tile(i):
        return (nt - 1 - i) if rev else i

    def pcol(j):
        return pl.BlockSpec((tl, W), lambda i: (tile(i) + ob, j))

    vec = pl.BlockSpec((1, W), lambda i: (0, 0))
    taps = pl.BlockSpec((SUBLANES, W), lambda i: (0, 0))
    wd_spec = pl.BlockSpec(wd.shape, lambda i: (0, 0, 0))
    seq = pl.BlockSpec((tl, W), lambda i: (tile(i), 0))

    sub_r = min(row_w, tl)
    assert tl % sub_r == 0

    def body(*refs):
        if rev:
            (bl, cl, ul, gl, ql, ho, xb_r, xb16_r, wca_r, wd_r, ba_r, bx_r, lam_r, hin, hseq, cat, a_o, r_o, ig_o, m2_o,
             b_scr, pre_scr, carry, sp_scr) = refs
        else:
            (vl, wcb_r, bcb_r, wd_r, ba_r, bx_r, lam_r, hin, hseq, xb_r, xb16_r, a_o, r_o, ig_o, m2_o,
             b_scr, pre_scr, carry, sp_scr) = refs
        i = pl.program_id(0)

        @pl.when(i == 0)
        def _():
            carry[...] = hin[...]

        sp_scr[...] = _softplus(-lam_r[...])
        tmod = _tmod(sub_r, sub_r)

        def conv_in(r0, l0):
            rs, ls = pl.ds(r0, sub_r), pl.ds(l0, LANES)
            xb = _conv(_f32(vl, rs, ls), wcb_r, 4, 2, tmod, sub_r, ls) + bcb_r[:, ls]
            xb_r[rs, ls] = xb
            xb16_r[rs, ls] = xb.astype(BF16)

        def gates(r0, l0):
            rs, ls = pl.ds(r0, sub_r), pl.ds(l0, LANES)
            r, ig, a, m2 = _gates(pre_scr[rs, ls] + ba_r[:, ls], pre_scr[rs, pl.ds(W + l0, LANES)] + bx_r[:, ls],
                                  sp_scr[:, ls])
            a_o[rs, ls] = a
            r_o[rs, ls] = r.astype(r_o.dtype)
            ig_o[rs, ls] = ig.astype(ig_o.dtype)
            m2_o[rs, ls] = m2.astype(m2_o.dtype)
            m = jnp.where(m2 > 0.0, m2 * lax.rsqrt(m2), 0.0)
            b_scr[rs, ls] = m * (ig * xb_r[rs, ls])

        if not rev:
            _sub_loop(tl, sub_r, W, conv_in)
        _gate_matmul(xb16_r, wd_r, pre_scr, W, ng, gs)
        _sub_loop(tl, sub_r, W, gates)

        def step(t, h):
            h = a_o[pl.ds(t, 1), :] * h + b_scr[pl.ds(t, 1), :]
            hseq[pl.ds(t, 1), :] = h
            return h

        carry[...] = _row_loop(tl, rev, step, carry[...])

        if rev:
            def mix_out(r0, l0):
                rs, ls = pl.ds(r0, sub_r), pl.ds(l0, LANES)
                yb = (ho[rs, ls] + hseq[rs, ls]) * _silu(_f32(ql, rs, ls))
                ya = (_f32(bl, rs, ls) * _conv(_f32(cl, rs, ls) * _f32(ul, rs, ls), wca_r, 3, 1, tmod, sub_r, ls)
                      * _silu(_f32(gl, rs, ls)))
                cat[rs, ls] = ya.astype(cat.dtype)
                cat[rs, pl.ds(W + l0, LANES)] = yb.astype(cat.dtype)

            _sub_loop(tl, sub_r, W, mix_out)

    scratch = [pltpu.VMEM((tl, W), F32), pltpu.VMEM((tl, 2 * W), F32), pltpu.VMEM((1, W), F32), pltpu.VMEM((1, W), F32)]
    f32_seq = jax.ShapeDtypeStruct((rows, W), F32)
    kept_gates = [f32_seq] + [jax.ShapeDtypeStruct((rows, W), BF16)] * 3
    if rev:
        in_specs = [pcol(j) for j in (0, 1, 2, 3, 5)] + [seq, seq, seq, taps, wd_spec, vec, vec, vec, vec]
        args = [P] * 5 + [saved0["h"], saved0["xb"], saved0["xb16"], wca, wd, ba, bx, lam, h_init]
        out_specs = [seq, pl.BlockSpec((tl, 2 * W), lambda i: (tile(i), 0))] + [seq] * 4
        out_shape = [f32_seq, jax.ShapeDtypeStruct((rows, 2 * W), BF16)] + kept_gates
    else:
        in_specs = [pcol(4), taps, vec, wd_spec, vec, vec, vec, vec]
        args = [P, wcb, bcb, wd, ba, bx, lam, h_init]
        out_specs = [seq] * 7
        out_shape = [f32_seq, f32_seq, jax.ShapeDtypeStruct((rows, W), BF16)] + kept_gates
    res = pl.pallas_call(
        body, name=name, grid=(nt,), in_specs=in_specs, out_specs=out_specs, out_shape=out_shape,
        scratch_shapes=scratch, compiler_params=_params(("arbitrary",)),
    )(*args)
    gates = dict(zip(("a", "r", "ig", "m2"), res[-4:]))
    if rev:
        return res[0], res[1], gates
    return dict(h=res[0], xb=res[1], xb16=res[2]), gates


_BWD_SCRATCH = ("dyl", "g", "dp16", "sp", "dlf", "edge", "c")
_FWD_SAVED = ("xb", "xb16", "a", "r", "ig", "m2")


def _bwd_scratch(tl, W):
    shapes = {"dyl": pltpu.VMEM((tl, W), F32), "g": pltpu.VMEM((tl, W), F32), "dp16": pltpu.VMEM((tl, 2 * W), BF16),
              "sp": pltpu.VMEM((1, W), F32), "dlf": pltpu.VMEM((1, W), F32), "edge": pltpu.VMEM((1, W), F32),
              "c": pltpu.VMEM((1, W), F32)}
    return [shapes[n] for n in _BWD_SCRATCH]


def _lru_bwd_tile(d, dy_fn, hs_ref, wd_r, lam_r, scr, acc, first, tl, sub_r, ng, gs):
    dwd_ref, dba_ref, dbx_ref, dlam_ref = acc
    W = hs_ref.shape[1]
    assert gs % LANES == 0
    rev = d == 0
    lam = lam_r[...]
    scr["sp"][...] = _softplus(-lam)
    scr["dlf"][...] = -_sigmoid(-lam)

    @pl.when(first)
    def _():
        dwd_ref[...] = jnp.zeros_like(dwd_ref)
        dba_ref[...] = jnp.zeros_like(dba_ref)
        dbx_ref[...] = jnp.zeros_like(dbx_ref)
        dlam_ref[...] = jnp.zeros_like(dlam_ref)

    def state_grad(r0, l0):
        rs, ls = pl.ds(r0, sub_r), pl.ds(l0, LANES)
        scr["dyl"][rs, ls] = dy_fn(rs, ls)

    _sub_loop(tl, sub_r, W, state_grad)

    def step(t, c):
        g = scr["dyl"][pl.ds(t, 1), :] + c
        scr["g"][pl.ds(t, 1), :] = g
        return scr["a"][pl.ds(t, 1), :] * g

    scr["c"][...] = _row_loop(tl, rev, step, scr["c"][...])
    row = lax.broadcasted_iota(jnp.int32, (sub_r, 1), 0)

    def grads(r0, l0):
        rs, ls = pl.ds(r0, sub_r), pl.ds(l0, LANES)
        g, a, m2 = scr["g"][rs, ls], scr["a"][rs, ls], _f32(scr["m2"], rs, ls)
        r, ig, xb = _f32(scr["r"], rs, ls), _f32(scr["ig"], rs, ls), scr["xb"][rs, ls]
        h = hs_ref[rs, ls]
        if d == 0:
            e0 = pl.multiple_of(jnp.maximum(r0 - SUBLANES, 0), SUBLANES)
            edge = jnp.where(r0 == 0, scr["edge"][:, ls], hs_ref[pl.ds(e0, SUBLANES), ls][SUBLANES - 1:, :])
            hprev = jnp.where(row == 0, edge, pltpu.roll(h, 1, 0))
        else:
            e0 = pl.multiple_of(jnp.minimum(r0 + sub_r, tl - SUBLANES), SUBLANES)
            edge = jnp.where(r0 == tl - sub_r, scr["edge"][:, ls], hs_ref[pl.ds(e0, SUBLANES), ls][:1, :])
            hprev = jnp.where(row == sub_r - 1, edge, pltpu.roll(h, sub_r - 1, 0))
        rsq = lax.rsqrt(m2)
        gm = g * (m2 * rsq)
        d_la = (g * hprev) * a - (g * (ig * xb)) * ((1.0 - m2) * rsq)
        d_pr = d_la * ((-LRU_C) * scr["sp"][:, ls]) * (r * (1.0 - r))
        d_pi = (gm * xb) * (ig * (1.0 - ig))
        scr["dyl"][rs, ls] = gm * ig
        dlam_ref[:, ls] += jnp.sum(d_la * ((-LRU_C) * r), axis=0, keepdims=True) * scr["dlf"][:, ls]
        dba_ref[:, ls] += jnp.sum(d_pr, axis=0, keepdims=True)
        dbx_ref[:, ls] += jnp.sum(d_pi, axis=0, keepdims=True)
        gi, off = divmod(l0, gs)
        scr["dp16"][rs, pl.ds(gi * 2 * gs + off, LANES)] = d_pr.astype(BF16)
        scr["dp16"][rs, pl.ds(gi * 2 * gs + gs + off, LANES)] = d_pi.astype(BF16)

    _sub_loop(tl, sub_r, W, grads)
    for gi in range(ng):
        dp = scr["dp16"][:, gi * 2 * gs:(gi + 1) * 2 * gs]
        scr["g"][:, gi * gs:(gi + 1) * gs] = lax.dot_general(dp, wd_r[gi], (((1,), (1,)), ((), ())),
                                                             preferred_element_type=F32)
        dwd_ref[gi] += lax.dot_general(scr["xb16"][:, gi * gs:(gi + 1) * gs], dp, (((0,), (0,)), ((), ())),
                                       preferred_element_type=F32)


def _edge_block(h, tl, nt, d):
    W = h.shape[1]
    per = tl // SUBLANES
    if d == 0:
        return pl.BlockSpec((SUBLANES, W), lambda i: (jnp.maximum((nt - 1 - i) * per - 1, 0), 0))
    return pl.BlockSpec((SUBLANES, W), lambda i: (jnp.minimum((i + 1) * per, nt * per - 1), 0))


def _mix_bwd0(P, dcat, saved0, gates0, h_init, c_init, wts, *, rows, row_off, row_w, tl, name):
    W = P.shape[1] // 6
    nt = rows // tl
    ob = row_off // tl
    gs = min(LRU_GROUP, W)
    ng = W // gs
    wd, lam = wts["wd"][0], wts["lam"][0]
    h0s = saved0["h"]
    kept = [saved0["xb"], saved0["xb16"]] + [gates0[n] for n in ("a", "r", "ig", "m2")]

    def tile(i):
        return nt - 1 - i

    vec = pl.BlockSpec((1, W), lambda i: (0, 0))
    wd_spec = pl.BlockSpec(wd.shape, lambda i: (0, 0, 0))
    seq = pl.BlockSpec((tl, W), lambda i: (tile(i), 0))

    sub_r = min(row_w, tl)
    assert tl % sub_r == 0

    def body(ql, dyb, hs, hedge8, xb_r, xb16_r, a_r, r_r, ig_r, m2_r, wd_r, lam_r, hin, cin,
             dxb_o, dwd_o, dba_o, dbx_o, dlam_o, cfin, *scratch):
        scr = dict(zip(_BWD_SCRATCH, scratch, strict=True))
        scr.update(zip(_FWD_SAVED, (xb_r, xb16_r, a_r, r_r, ig_r, m2_r), strict=True))
        i = pl.program_id(0)

        @pl.when(i == 0)
        def _():
            scr["c"][...] = cin[...]

        scr["edge"][...] = jnp.where(i == nt - 1, hin[...], hedge8[SUBLANES - 1:SUBLANES, :])
        _lru_bwd_tile(0, lambda rs, ls: _f32(dyb, rs, ls) * _silu(_f32(ql, rs, ls)), hs, wd_r, lam_r, scr,
                      (dwd_o, dba_o, dbx_o, dlam_o), i == 0, tl, sub_r, ng, gs)
        dxb_o[...] = scr["dyl"][...] + scr["g"][...]
        cfin[...] = scr["c"][...]

    return pl.pallas_call(
        body, name=name, grid=(nt,),
        in_specs=[pl.BlockSpec((tl, W), lambda i: (tile(i) + ob, 5)), pl.BlockSpec((tl, W), lambda i: (tile(i), 1)), seq,
                  _edge_block(h0s, tl, nt, 0)] + [seq] * 6 + [wd_spec, vec, vec, vec],
        out_specs=[seq, wd_spec, vec, vec, vec, vec],
        out_shape=[jax.ShapeDtypeStruct((rows, W), F32), jax.ShapeDtypeStruct(wd.shape, F32)]
        + [jax.ShapeDtypeStruct((1, W), F32)] * 4,
        scratch_shapes=_bwd_scratch(tl, W),
        compiler_params=_params(("arbitrary",)),
    )(P, dcat, h0s, h0s, *kept, wd, lam, h_init, c_init)


def _mix_bwd1(P, dcat, saved0, h1s, gates1, dxb0, h_init, c_init, wts, *, rows, row_off, row_w, tl, name,
              dp_rows=None, dp_off=0, dp_into=None):
    dp_rows = rows if dp_rows is None else dp_rows
    dpb = dp_off // tl
    W = P.shape[1] // 6
    nt = rows // tl
    ob = row_off // tl
    gs = min(LRU_GROUP, W)
    ng = W // gs
    wca, wcb = wts["wca"], wts["wcb"]
    wd, lam = wts["wd"][1], wts["lam"][1]
    h0s = saved0["h"]
    kept = [saved0["xb"], saved0["xb16"]] + [gates1[n] for n in ("a", "r", "ig", "m2")]

    vec = pl.BlockSpec((1, W), lambda i: (0, 0))
    taps = pl.BlockSpec((SUBLANES, W), lambda i: (0, 0))
    wd_spec = pl.BlockSpec(wd.shape, lambda i: (0, 0, 0))
    seq = pl.BlockSpec((tl, W), lambda i: (i, 0))

    sub_r = min(row_w, tl)
    assert tl % sub_r == 0

    def body(*refs):
        if dp_into is not None:
            refs = refs[1:]
        (bl, cl, ul, gl, vl, ql, dya, dyb, h0, h1, hedge8, dx0, xb_r, xb16_r, a_r, r_r, ig_r, m2_r, wca_r, wcb_r,
         wd_r, lam_r, hin, cin, dp_o, dwd_o, dba_o, dbx_o, dlam_o, dwca_o, dwcb_o, dbcb_o, cfin, *scratch) = refs
        scr = dict(zip(_BWD_SCRATCH, scratch, strict=True))
        scr.update(zip(_FWD_SAVED, (xb_r, xb16_r, a_r, r_r, ig_r, m2_r), strict=True))
        i = pl.program_id(0)

        @pl.when(i == 0)
        def _():
            scr["c"][...] = cin[...]
            dwca_o[...] = jnp.zeros_like(dwca_o)
            dwcb_o[...] = jnp.zeros_like(dwcb_o)
            dbcb_o[...] = jnp.zeros_like(dbcb_o)

        scr["edge"][...] = jnp.where(i == nt - 1, hin[...], hedge8[0:1, :])
        _lru_bwd_tile(1, lambda rs, ls: _f32(dyb, rs, ls) * _silu(_f32(ql, rs, ls)), h1, wd_r, lam_r, scr,
                      (dwd_o, dba_o, dbx_o, dlam_o), i == 0, tl, sub_r, ng, gs)
        cfin[...] = scr["c"][...]
        tmod = _tmod(sub_r, sub_r)

        def rest(r0, l0):
            rs, ls = pl.ds(r0, sub_r), pl.ds(l0, LANES)
            dxb = dx0[rs, ls] + scr["dyl"][rs, ls] + scr["g"][rs, ls]
            dv, dwb = _conv_bwd(dxb, _f32(vl, rs, ls), wcb_r, 4, 2, tmod, sub_r, ls)
            for j in range(4):
                dwcb_o[j:j + 1, ls] += dwb[j]
            dbcb_o[:, ls] += jnp.sum(dxb, axis=0, keepdims=True)
            q = _f32(ql, rs, ls)
            sq = _sigmoid(q)
            dq = _f32(dyb, rs, ls) * (h0[rs, ls] + h1[rs, ls]) * (sq * (1.0 + q * (1.0 - sq)))
            b_, c_, u_, g_ = _f32(bl, rs, ls), _f32(cl, rs, ls), _f32(ul, rs, ls), _f32(gl, rs, ls)
            z = c_ * u_
            cz = _conv(z, wca_r, 3, 1, tmod, sub_r, ls)
            sgm = _sigmoid(g_)
            sg = g_ * sgm
            da = _f32(dya, rs, ls)
            dz, dwa = _conv_bwd(da * b_ * sg, z, wca_r, 3, 1, tmod, sub_r, ls)
            for j in range(3):
                dwca_o[j:j + 1, ls] += dwa[j]
            parts = (da * cz * sg, dz * u_, dz * c_, da * b_ * cz * (sgm * (1.0 + g_ * (1.0 - sgm))), dv, dq)
            for k, val in enumerate(parts):
                dp_o[rs, pl.ds(k * W + l0, LANES)] = val.astype(dp_o.dtype)

        _sub_loop(tl, sub_r, W, rest)

    def pcol(j):
        return pl.BlockSpec((tl, W), lambda i: (i + ob, j))

    prev = [] if dp_into is None else [dp_into]
    return pl.pallas_call(
        body, name=name, grid=(nt,), input_output_aliases={} if dp_into is None else {0: 0},
        in_specs=[pl.BlockSpec(memory_space=pl.ANY)] * len(prev) + [pcol(j) for j in range(6)]
        + [pl.BlockSpec((tl, W), lambda i: (i, 0)), pl.BlockSpec((tl, W), lambda i: (i, 1)), seq, seq,
           _edge_block(h1s, tl, nt, 1), seq] + [seq] * 6 + [taps, taps, wd_spec, vec, vec, vec],
        out_specs=[pl.BlockSpec((tl, 6 * W), lambda i: (i + dpb, 0)), wd_spec, vec, vec, vec, taps, taps, vec, vec],
        out_shape=[jax.ShapeDtypeStruct((dp_rows, 6 * W), BF16), jax.ShapeDtypeStruct(wd.shape, F32)]
        + [jax.ShapeDtypeStruct((1, W), F32)] * 3
        + [jax.ShapeDtypeStruct((SUBLANES, W), F32)] * 2 + [jax.ShapeDtypeStruct((1, W), F32)] * 2,
        scratch_shapes=_bwd_scratch(tl, W),
        compiler_params=_params(("arbitrary",)),
    )(*prev, *([P] * 6), dcat, dcat, h0s, h1s, h1s, dxb0, *kept, wca, wcb, wd, lam, h_init, c_init)


def _loss_head(out, x, tgt, gt, fg, tr):
    L, D = x.shape

    def body(o_ref, x_ref, t_ref, gt_ref, fg_ref, dn_o, do_o, dfg_o, dgt_o, loss_o):
        i = pl.program_id(0)

        @pl.when(i == 0)
        def _():
            dfg_o[...] = jnp.zeros_like(dfg_o)
            dgt_o[...] = jnp.zeros_like(dgt_o)
            loss_o[...] = jnp.zeros_like(loss_o)

        o = o_ref[...].astype(F32)
        gt_v = gt_ref[...]
        fg_v = fg_ref[...]
        n = x_ref[...] + gt_v * o
        r = lax.rsqrt(jnp.mean(n * n, axis=-1, keepdims=True) + EPS)
        nr = n * r
        e = nr * fg_v - t_ref[...]
        loss_o[...] += 0.5 * jnp.sum(jnp.mean(e * e, axis=-1, keepdims=True))
        dy = e * (1.0 / D)
        dfg_o[...] += jnp.sum(dy * nr, axis=0, keepdims=True)
        qv = dy * fg_v
        dn = r * (qv - nr * jnp.mean(qv * nr, axis=-1, keepdims=True))
        dgt_o[...] += jnp.sum(dn * o, axis=0, keepdims=True)
        dn_o[...] = dn.astype(dn_o.dtype)
        do_o[...] = (dn * gt_v).astype(do_o.dtype)

    blk = pl.BlockSpec((tr, D), lambda i: (i, 0))
    vec = pl.BlockSpec((1, D), lambda i: (0, 0))
    return pl.pallas_call(
        body, name="loss_head", grid=(L // tr,), in_specs=[blk, blk, blk, vec, vec],
        out_specs=[blk, blk, vec, vec, pl.BlockSpec((SUBLANES, LANES), lambda i: (0, 0))],
        out_shape=[jax.ShapeDtypeStruct((L, D), BF16), jax.ShapeDtypeStruct((L, D), BF16),
                   jax.ShapeDtypeStruct((1, D), F32), jax.ShapeDtypeStruct((1, D), F32),
                   jax.ShapeDtypeStruct((SUBLANES, LANES), F32)],
        compiler_params=_params(("arbitrary",)),
    )(out, x, tgt, gt, fg)


def _norm_bwd(dhl, x, dn, g, sc, tr, name):
    L, D = x.shape
    with_x = dn is not None

    def body(*refs):
        if with_x:
            d_ref, x_ref, dn_ref, g_ref, sc_ref, gx_o, dsh_o, dsc_o, dg_o = refs
        else:
            d_ref, x_ref, g_ref, sc_ref, dsh_o, dsc_o, dg_o = refs
        i = pl.program_id(0)

        @pl.when(i == 0)
        def _():
            dsh_o[...] = jnp.zeros_like(dsh_o)
            dsc_o[...] = jnp.zeros_like(dsc_o)
            dg_o[...] = jnp.zeros_like(dg_o)

        d = d_ref[...].astype(F32)
        xv = x_ref[...]
        g_v = g_ref[...]
        r = lax.rsqrt(jnp.mean(xv * xv, axis=-1, keepdims=True) + EPS)
        xr = xv * r
        dsh_o[...] += jnp.sum(d, axis=0, keepdims=True)
        dsc_o[...] += jnp.sum(d * (xr * g_v), axis=0, keepdims=True)
        dxn = d * (1.0 + sc_ref[...])
        dg_o[...] += jnp.sum(dxn * xr, axis=0, keepdims=True)
        if with_x:
            qv = dxn * g_v
            gx_o[...] = r * (qv - xr * jnp.mean(qv * xr, axis=-1, keepdims=True)) + dn_ref[...].astype(F32)

    blk = pl.BlockSpec((tr, D), lambda i: (i, 0))
    vec = pl.BlockSpec((1, D), lambda i: (0, 0))
    vshape = jax.ShapeDtypeStruct((1, D), F32)
    res = pl.pallas_call(
        body, name=name, grid=(L // tr,),
        in_specs=[blk, blk] + ([blk] if with_x else []) + [vec, vec],
        out_specs=([blk] if with_x else []) + [vec, vec, vec],
        out_shape=([jax.ShapeDtypeStruct((L, D), F32)] if with_x else []) + [vshape] * 3,
        compiler_params=_params(("arbitrary",)),
    )(*([dhl, x] + ([dn] if with_x else []) + [g, sc]))
    return res if with_x else [None] + list(res)


def _pack_blockdiag(wa, wx, gs):
    H, hd, _ = wa.shape
    hp = gs // hd
    ng = H // hp
    eye = jnp.eye(hp, dtype=wa.dtype)

    def bd(w):
        return jnp.einsum("gpij,pq->gpiqj", w.reshape(ng, hp, hd, hd), eye).reshape(ng, gs, gs)

    return jnp.concatenate([bd(wa), bd(wx)], axis=-1).astype(BF16)


def _unpack_blockdiag(dwd, H, hd, gs):
    hp = gs // hd
    ng = H // hp
    eye = jnp.eye(hp, dtype=dwd.dtype)

    def diag(dm):
        return jnp.einsum("gpiqj,pq->gpij", dm.reshape(ng, hp, hd, hp, hd), eye).reshape(H, hd, hd)

    return diag(dwd[:, :, :gs]), diag(dwd[:, :, gs:])


def kernel(x, c, ctx, c_ctx, norm_g, w_ada, b_ada, w_in, w_conv_a, w_conv_b, b_conv_b, lru_wa, lru_ba, lru_wx, lru_bx, lru_lambda, w_out, final_g, loss_target, m_c_ctx, m_norm_g, m_w_ada, m_b_ada, m_w_in, m_w_conv_a, m_w_conv_b, m_b_conv_b, m_lru_wa, m_lru_ba, m_lru_wx, m_lru_bx, m_lru_lambda, m_w_out, m_final_g, v_c_ctx, v_norm_g, v_w_ada, v_b_ada, v_w_in, v_w_conv_a, v_w_conv_b, v_b_conv_b, v_lru_wa, v_lru_ba, v_lru_wx, v_lru_bx, v_lru_lambda, v_w_out, v_final_g):
    xi, yi, ci = _pos()
    me = 4 * xi + 2 * yi + ci
    q = 2 * xi + yi
    first_core = (ci == 0).astype(F32)

    L, D = x.shape[1], x.shape[2]
    T = ctx.shape[1]
    W = D // 2
    Wq = W // 4
    H, hd = lru_wa.shape[2], lru_wa.shape[3]
    gs = min(LRU_GROUP, W)
    nq = w_ada.shape[2]
    tl = min(256, T, L)
    tr = min(256, T, L)
    x2, ctx2, tgt2 = x[0], ctx[0], loss_target[0]

    def place(shard, full_cols):
        z = jnp.zeros((shard.shape[0], full_cols), F32)
        return lax.dynamic_update_slice(z, shard * first_core, (0, q * shard.shape[1]))

    c_rows = lax.dynamic_update_slice(jnp.zeros((8, D), F32), c, (me, 0))
    small_in = [c_rows, place(w_conv_a[0], W), place(w_conv_b[0], W), place(lru_ba[0], W), place(lru_bx[0], W),
                place(lru_lambda[0], W)]
    small_shapes = [a.shape for a in small_in]
    gathered = _allreduce8(_pack(small_in, 8 * SUBLANES), "gather_small")
    c_all, wca, wcb, ba_all, bx_all, lam_all = _unpack(gathered, small_shapes)

    s_rows = jnp.concatenate([c_all, c_ctx[None, :], jnp.zeros((7, D), F32)], axis=0)
    mod_part = _matmul(s_rows, w_ada[0], a_act="silu", bias=lax.dynamic_slice(b_ada, (0, q * nq), (1, nq)),
                       tm=16, tn=nq, tk=512, name="ada_fwd")
    mod_all = _allreduce8(_pack([place(mod_part[:9], 4 * nq)], 8 * SUBLANES), "gather_mod")
    mod_all = _unpack(mod_all, [(9, 4 * nq)])[0]
    mod_l = lax.dynamic_slice(mod_all, (me, 0), (1, 3 * D))
    mod_c = mod_all[8:9]
    sh_l, sc_l, gt_l = mod_l[:, :D], mod_l[:, D:2 * D], mod_l[:, 2 * D:]
    sh_c, sc_c = mod_c[:, :D], mod_c[:, D:2 * D]

    pad_taps = lambda w: jnp.pad(w, ((0, SUBLANES - w.shape[0]), (0, 0)))
    wts = {
        "wca": pad_taps(wca), "wcb": pad_taps(wcb), "bcb": b_conv_b,
        "wd": [_pack_blockdiag(lru_wa[0, d], lru_wx[0, d], gs) for d in range(2)],
        "ba": [ba_all[d:d + 1] for d in range(2)], "bx": [bx_all[d:d + 1] for d in range(2)],
        "lam": [lam_all[d:d + 1] for d in range(2)],
    }

    hl = _norm_in(x2, ctx2, norm_g, sc_l, sh_l, sc_c, sh_c, tr)
    p_lat, win_full, wout_full = _in_proj_gather(hl, w_in[0].astype(BF16), w_out[0].astype(BF16),
                                                 jnp.reshape(q, (1,)).astype(jnp.int32), rows=L, tm=min(1024, L))
    p_ctx = _matmul(hl, win_full, a_rows=T, a_off=L, tm=T, tn=1536, tk=D, out_dtype=BF16, name="in_proj_ctx")
    zero_w = jnp.zeros((1, W), F32)
    ctx0, cgates0 = _mix_fwd(p_ctx, 0, zero_w, wts, rows=T, row_off=0, row_w=T, tl=tl, name="ctx_fwd0")
    c1s, _, cgates1 = _mix_fwd(p_ctx, 1, zero_w, wts, rows=T, row_off=0, row_w=T, tl=tl, saved0=ctx0, name="ctx_fwd1")
    h0_init, h1_init = ctx0["h"][T - 1:T], c1s[0:1]
    tl_fwd = 2 * tl if L % (2 * tl) == 0 else tl
    lat0, gates0 = _mix_fwd(p_lat, 0, h0_init, wts, rows=L, row_off=0, row_w=GRID_W, tl=tl_fwd, name="mix_fwd0")
    h1s, cat, gates1 = _mix_fwd(p_lat, 1, h1_init, wts, rows=L, row_off=0, row_w=GRID_W, tl=tl_fwd, saved0=lat0,
                                name="mix_fwd1")
    out = _matmul(cat, wout_full, tm=512, tn=D, tk=2 * W, out_dtype=BF16, name="out_proj")
    tr_lat = 2 * tr if L % (2 * tr) == 0 else tr
    dn, dout, dfg, dgt, loss_blk = _loss_head(out, x2, tgt2, gt_l, final_g[None, :], tr_lat)

    dcat = _matmul(dout, wout_full, tb=True, tm=512, tn=2 * W, tk=D, out_dtype=BF16, name="out_proj_bwd")
    gw_out = _matmul(cat, dout, ta=True, tm=1024, tn=D, tk=2048, out_dtype=BF16, name="w_out_grad")
    dxb0, dwd0, dba0, dbx0, dlam0, ch0 = _mix_bwd0(p_lat, dcat, lat0, gates0, h0_init, zero_w, wts, rows=L, row_off=0,
                                                   row_w=GRID_W, tl=tl_fwd, name="mix_bwd0")
    dp_lat, dwd1, dba1, dbx1, dlam1, dwca, dwcb, dbcb, ch1 = _mix_bwd1(
        p_lat, dcat, lat0, h1s, gates1, dxb0, h1_init, zero_w, wts, rows=L, row_off=0, row_w=GRID_W, tl=tl,
        name="mix_bwd1", dp_rows=L + T)
    zero_cat = jnp.zeros((T, 2 * W), BF16)
    cxb0, cwd0, cba0, cbx0, clam0, _ = _mix_bwd0(p_ctx, zero_cat, ctx0, cgates0, zero_w, ch0, wts, rows=T, row_off=0,
                                                 row_w=T, tl=tl, name="ctx_bwd0")
    dp, cwd1, cba1, cbx1, clam1, cwca, cwcb, cbcb, _ = _mix_bwd1(
        p_ctx, zero_cat, ctx0, c1s, cgates1, cxb0, zero_w, ch1, wts, rows=T, row_off=0, row_w=T, tl=tl, name="ctx_bwd1",
        dp_rows=L + T, dp_off=L, dp_into=dp_lat)

    gw_in = _matmul(hl, dp, ta=True, tm=1024, tn=1536, tk=2816, out_dtype=BF16, name="w_in_grad")
    rs_axes = [1, 0]
    pair_sums = _rs_pair_sums([gw_in, gw_out], rs_axes)
    dhc = _matmul(dp, win_full, tb=True, a_rows=T, a_off=L, tm=T, tn=D, tk=3072, name="in_proj_bwd_ctx")
    _, dsh_c, dsc_c, dng_c = _norm_bwd(dhc, ctx2, None, norm_g, sc_c, tr, "norm_bwd_ctx")
    zeros_d = jnp.zeros((1, D), F32)
    dmod_c = jnp.concatenate([dsh_c, dsc_c, zeros_d], axis=1)
    dhl, (*rs_slots, dmod_c_all) = _matmul(
        dp, win_full, tb=True, a_rows=L, tm=512, tn=D, tk=3072, out_dtype=BF16, name="in_proj_bwd",
        side=_join_sides(_rs_chips_side(pair_sums), _gather_block_side(jnp.pad(dmod_c, ((0, SUBLANES - 1), (0, 0))))))
    gx, dsh_l, dsc_l, dng_l = _norm_bwd(dhl, x2, dn, norm_g, sc_l, tr_lat, "norm_bwd")
    gc_rows = _matmul(lax.dynamic_slice(dmod_c_all.reshape(8 * SUBLANES, 3 * D), (0, q * nq), (8 * SUBLANES, nq)), w_ada[0],
                      tb=True, dsilu_mul=c_ctx[None, :], tm=8 * SUBLANES, tn=D, tk=512, name="c_ctx_grad")
    gc_part = jnp.sum(gc_rows, axis=0, keepdims=True) * first_core

    g_in_shard, g_out_shard = _rs_finish(rs_slots, rs_axes)

    dwa0, dwx0 = _unpack_blockdiag(dwd0 + cwd0, H, hd, gs)
    dwa1, dwx1 = _unpack_blockdiag(dwd1 + cwd1, H, hd, gs)
    dmod_l = jnp.concatenate([dsh_l, dsc_l, dgt], axis=1)
    small_g = [
        lax.dynamic_update_slice(jnp.zeros((8, 3 * D), F32), dmod_l, (me, 0)), dmod_c,
        dfg, dng_l + dng_c, (dwca + cwca)[:3], (dwcb + cwcb)[:4], dbcb + cbcb,
        jnp.stack([dwa0, dwa1]), jnp.stack([dwx0, dwx1]),
        jnp.concatenate([dba0 + cba0, dba1 + cba1], axis=0), jnp.concatenate([dbx0 + cbx0, dbx1 + cbx1], axis=0),
        jnp.concatenate([dlam0 + clam0, dlam1 + clam1], axis=0), loss_blk[0:1, 0:1], gc_part,
    ]
    g_shapes = [a.shape for a in small_g]
    (g_rows, g_modc, g_fg, g_ng, g_wca, g_wcb, g_bcb, g_wa, g_wx, g_ba, g_bx, g_lam, loss_sum, g_c_ctx) = _unpack(
        _allreduce8_two_level(_pack(small_g, 8 * SUBLANES), "reduce_small"), g_shapes)

    g_mod = jnp.concatenate([g_rows, g_modc, jnp.zeros((7, 3 * D), F32)], axis=0)
    g_mod_q = lax.dynamic_slice(g_mod, (0, q * nq), (16, nq))
    g_w_ada = _matmul(s_rows, g_mod_q, ta=True, a_act="silu", tm=1024, tn=nq, tk=16, name="w_ada_grad")
    g_b_ada = jnp.sum(g_mod[:9], axis=0, keepdims=True)

    def shard_cols(a, width):
        return lax.dynamic_slice(a, (0, q * width), (a.shape[0], width))

    grads = {
        "c_ctx": g_c_ctx, "norm_g": g_ng, "b_ada": g_b_ada,
        "w_conv_a": shard_cols(g_wca, Wq)[None], "w_conv_b": shard_cols(g_wcb, Wq)[None], "b_conv_b": g_bcb,
        "lru_wa": g_wa[None], "lru_ba": shard_cols(g_ba, Wq)[None], "lru_wx": g_wx[None],
        "lru_bx": shard_cols(g_bx, Wq)[None], "lru_lambda": shard_cols(g_lam, Wq)[None], "final_g": g_fg[0],
    }
    small_names = list(grads)
    given = dict(c_ctx=(c_ctx, m_c_ctx, v_c_ctx), norm_g=(norm_g, m_norm_g, v_norm_g), b_ada=(b_ada, m_b_ada, v_b_ada),
                 w_conv_a=(w_conv_a, m_w_conv_a, v_w_conv_a), w_conv_b=(w_conv_b, m_w_conv_b, v_w_conv_b),
                 b_conv_b=(b_conv_b, m_b_conv_b, v_b_conv_b), lru_wa=(lru_wa, m_lru_wa, v_lru_wa),
                 lru_ba=(lru_ba, m_lru_ba, v_lru_ba), lru_wx=(lru_wx, m_lru_wx, v_lru_wx),
                 lru_bx=(lru_bx, m_lru_bx, v_lru_bx), lru_lambda=(lru_lambda, m_lru_lambda, v_lru_lambda),
                 final_g=(final_g, m_final_g, v_final_g))
    def rows2d(a):
        return a.reshape(-1, a.shape[-1])

    grads = {n: grads[n].reshape(given[n][0].shape) for n in small_names}
    quads = [tuple(rows2d(a) for a in (given[n][0], grads[n], given[n][1], given[n][2])) for n in small_names]
    updated = _adam_many(quads, "adam_small")
    delta_s, newm_s, newv_s = ({n: u[j].reshape(given[n][0].shape) for n, u in zip(small_names, updated)} for j in range(3))

    big = {"w_ada": (w_ada, g_w_ada, m_w_ada, v_w_ada), "w_in": (w_in, g_in_shard, m_w_in, v_w_in),
           "w_out": (w_out, g_out_shard, m_w_out, v_w_out)}
    delta_b, newm_b, newv_b = {}, {}, {}
    for n, (w, g, m, v) in big.items():
        d_, m_, v_, *echo = _adam(w[0], g, m[0], v[0], "adam_" + n, echo_g=n != "w_ada")
        grads[n] = (echo[0] if echo else g)[None]
        delta_b[n], newm_b[n], newv_b[n] = d_[None], m_[None], v_[None]

    loss = loss_sum[0, 0]
    order = ["c_ctx", "norm_g", "w_ada", "b_ada", "w_in", "w_conv_a", "w_conv_b", "b_conv_b", "lru_wa", "lru_ba",
             "lru_wx", "lru_bx", "lru_lambda", "w_out", "final_g"]
    delta = {**delta_s, **delta_b}
    newm = {**newm_s, **newm_b}
    newv = {**newv_s, **newv_b}
    return (loss, gx[None], *[grads[n] for n in order], *[delta[n] for n in order], *[newm[n] for n in order],
            *[newv[n] for n in order])
```

```python
import jax
import jax.numpy as jnp
from jax import lax
from jax.experimental import pallas as pl
from jax.experimental.pallas import tpu as pltpu

F32 = jnp.float32
BF16 = jnp.bfloat16
MESH_ID = pl.DeviceIdType.MESH

EPS = 1e-6
LRU_C = 8.0
GRID_W = 64
ADAM_LR = 0.001
ADAM_B1 = 0.9
ADAM_B2 = 0.999
ADAM_EPS = 1e-08
ADAM_WD = 0.01
ADAM_STEP = 10

LANES = 128
SUBLANES = 8
PACK_COLS = 1024
VMEM_LIMIT = 56 * 2**20
LRU_GROUP = 256


def _params(sem=None):
    return pltpu.CompilerParams(vmem_limit_bytes=VMEM_LIMIT, dimension_semantics=sem)


def _pick(dim, pref, quantum=LANES):
    if dim <= pref:
        return dim
    best = None
    for t in range(quantum, pref + 1, quantum):
        if dim % t == 0:
            best = t
    assert best is not None, (dim, pref)
    return best


def _pos():
    return lax.axis_index("x"), lax.axis_index("y"), lax.axis_index("c")


def _flip(v, bit):
    return 1 - v if bit else v


def _sigmoid(v):
    return 0.5 * jnp.tanh(0.5 * v) + 0.5


def _silu(v):
    return v * _sigmoid(v)


def _dsilu(v):
    s = _sigmoid(v)
    return s * (1.0 + v * (1.0 - s))


def _gates(pre_r, pre_i, sp):
    r = _sigmoid(pre_r)
    ig = _sigmoid(pre_i)
    e = LRU_C * r * sp
    w = jnp.tanh(e)
    return r, ig, jnp.exp(-e), (2.0 * w) * pl.reciprocal(1.0 + w, approx=True)


def _softplus(z):
    return jnp.maximum(z, 0.0) + jnp.log1p(jnp.exp(-jnp.abs(z)))


def _matmul(a, b, *, ta=False, tb=False, tm=512, tn=512, tk=512, out_dtype=F32, name,
            a_rows=None, a_off=0, a_act=None, bias=None, dsilu_mul=None, side=None):
    rows_a = a.shape[0] if a_rows is None else a_rows
    if ta:
        K, M = rows_a, a.shape[1]
    else:
        M, K = rows_a, a.shape[1]
    N = b.shape[0] if tb else b.shape[1]
    tm, tn, tk = _pick(M, tm, SUBLANES), _pick(N, tn), _pick(K, tk)
    t_rows = tk if ta else tm
    assert a_off % t_rows == 0
    nk = K // tk
    gi, gj = M // tm, N // tn
    off_blocks = a_off // t_rows
    dims = (((0 if ta else 1,), (1 if tb else 0,)), ((), ()))
    extras = [e for e in (bias, dsilu_mul) if e is not None]
    n_sin = len(side["ins"]) if side else 0
    n_sout = len(side["outs"]) if side else 0

    def body(a_ref, b_ref, *rest):
        rest = list(rest)
        bias_ref = rest.pop(0) if bias is not None else None
        dsm_ref = rest.pop(0) if dsilu_mul is not None else None
        side_in = [rest.pop(0) for _ in range(n_sin)]
        o_ref = rest.pop(0)
        side_out = [rest.pop(0) for _ in range(n_sout)]
        acc_ref = rest.pop(0) if nk > 1 else None
        side_scr = rest
        i, j, k = pl.program_id(0), pl.program_id(1), pl.program_id(2)

        if side:
            @pl.when((i == 0) & (j == 0) & (k == 0))
            def _():
                side["start"](side_in, side_out, side_scr)

        av = a_ref[...]
        if a_act == "silu":
            av = _silu(av)
        prod = lax.dot_general(av, b_ref[...], dims, preferred_element_type=F32)

        def finish(r):
            if bias_ref is not None:
                r = r + bias_ref[...]
            if dsm_ref is not None:
                r = r * _dsilu(dsm_ref[...])
            o_ref[...] = r.astype(o_ref.dtype)

        if nk == 1:
            finish(prod)
        else:
            @pl.when(k == 0)
            def _():
                acc_ref[...] = prod

            @pl.when(k > 0)
            def _():
                acc_ref[...] += prod

            @pl.when(k == nk - 1)
            def _():
                finish(acc_ref[...])

        if side:
            @pl.when((i == gi - 1) & (j == gj - 1) & (k == nk - 1))
            def _():
                side["finish"](side_in, side_out, side_scr)

    if ta:
        a_spec = pl.BlockSpec((tk, tm), lambda i, j, k: (k + off_blocks, i))
    else:
        a_spec = pl.BlockSpec((tm, tk), lambda i, j, k: (i + off_blocks, k))
    once = dict(pipeline_mode=pl.Buffered(1)) if (gj == 1 and nk == 1) else {}
    if tb:
        b_spec = pl.BlockSpec((tn, tk), lambda i, j, k: (j, k), **once)
    else:
        b_spec = pl.BlockSpec((tk, tn), lambda i, j, k: (k, j), **once)
    in_specs = [a_spec, b_spec]
    if bias is not None:
        in_specs.append(pl.BlockSpec((1, tn), lambda i, j, k: (0, j)))
    if dsilu_mul is not None:
        in_specs.append(pl.BlockSpec((1, tn), lambda i, j, k: (0, j)))
    hbm = pl.BlockSpec(memory_space=pl.ANY)
    res = pl.pallas_call(
        body, name=name, grid=(gi, gj, nk),
        in_specs=in_specs + [hbm] * n_sin,
        out_specs=[pl.BlockSpec((tm, tn), lambda i, j, k: (i, j))] + [hbm] * n_sout,
        out_shape=[jax.ShapeDtypeStruct((M, N), out_dtype)] + (list(side["outs"]) if side else []),
        scratch_shapes=([pltpu.VMEM((tm, tn), F32)] if nk > 1 else []) + (list(side["scratch"]) if side else []),
        compiler_params=_params(("arbitrary",) * 3 if side else ("parallel", "parallel", "arbitrary")),
    )(a, b, *extras, *(side["ins"] if side else []))
    return (res[0], res[1:]) if side else res[0]


def _elementwise(fn, ins, outs, *, rows, cols, name, tr=256):
    tr = _pick(rows, tr, 2 * SUBLANES)
    n_in = len(ins)

    def body(*refs):
        vals = fn(*[r[...] for r in refs[:n_in]])
        if not isinstance(vals, (tuple, list)):
            vals = (vals,)
        for r, v in zip(refs[n_in:], vals, strict=True):
            r[...] = v.astype(r.dtype)

    def spec(off):
        assert off % tr == 0
        ob = off // tr
        return pl.BlockSpec((tr, cols), lambda i: (i + ob, 0))

    res = pl.pallas_call(
        body, name=name, grid=(rows // tr,),
        in_specs=[spec(off) for _, off in ins],
        out_specs=[spec(0) for _ in outs],
        out_shape=[jax.ShapeDtypeStruct((rows, cols), dt) for dt in outs],
        compiler_params=_params(("parallel",)),
    )(*[a for a, _ in ins])
    return res


def _adam_math(w, g, m, v):
    m = ADAM_B1 * m + (1.0 - ADAM_B1) * g
    v = ADAM_B2 * v + (1.0 - ADAM_B2) * (g * g)
    m_hat = m / (1.0 - ADAM_B1 ** ADAM_STEP)
    v_hat = v / (1.0 - ADAM_B2 ** ADAM_STEP)
    delta = -ADAM_LR * (m_hat / (jnp.sqrt(v_hat) + ADAM_EPS) + ADAM_WD * w)
    return delta, m, v


def _adam(w, g, m, v, name, echo_g=False):
    rows, cols = w.shape
    fn = (lambda w_, g_, m_, v_: _adam_math(w_, g_, m_, v_) + (g_,)) if echo_g else _adam_math
    return _elementwise(fn, [(w, 0), (g, 0), (m, 0), (v, 0)], [F32] * (4 if echo_g else 3),
                        rows=rows, cols=cols, name=name)


def _adam_many(quads, name):
    n = len(quads)

    def body(*refs):
        ins, outs = refs[:4 * n], refs[4 * n:]
        for t in range(n):
            w, g, m, v = (r[...] for r in ins[4 * t:4 * t + 4])
            for o_ref, val in zip(outs[3 * t:3 * t + 3], _adam_math(w, g, m, v), strict=True):
                o_ref[...] = val

    res = pl.pallas_call(
        body, name=name,
        out_shape=[jax.ShapeDtypeStruct(q[0].shape, F32) for q in quads for _ in range(3)],
        compiler_params=_params(),
    )(*[a for q in quads for a in q])
    return [res[3 * t:3 * t + 3] for t in range(n)]


def _pack(arrs, row_quantum):
    flat = jnp.concatenate([a.reshape(-1).astype(F32) for a in arrs])
    n = flat.shape[0]
    q = row_quantum * PACK_COLS
    total = -(-n // q) * q
    flat = jnp.pad(flat, (0, total - n))
    return flat.reshape(total // PACK_COLS, PACK_COLS)


def _unpack(buf, shapes):
    flat = buf.reshape(-1)
    out, off = [], 0
    for s in shapes:
        n = 1
        for d in s:
            n *= d
        out.append(flat[off:off + n].reshape(s))
        off += n
    return out


def _allreduce8(buf, name):
    R, C = buf.shape
    assert R % (8 * SUBLANES) == 0
    m = R // 8

    def body(x_ref, o_ref, recv, red, s1, r1, s2, r2):
        x, y, c = _pos()
        me = 4 * x + 2 * y + c

        def peer(k):
            px, py, pc = _flip(x, (k >> 2) & 1), _flip(y, (k >> 1) & 1), _flip(c, k & 1)
            return (px, py, pc), 4 * px + 2 * py + pc

        def rows(ref, idx):
            return ref.at[pl.ds(pl.multiple_of(idx * m, SUBLANES), m), :]

        def scatter(k):
            dev, p = peer(k)
            return pltpu.make_async_remote_copy(src_ref=rows(x_ref, p), dst_ref=recv.at[k], send_sem=s1.at[k],
                                                recv_sem=r1.at[k], device_id=dev, device_id_type=MESH_ID)

        def share(k):
            dev, p = peer(k)
            return pltpu.make_async_remote_copy(src_ref=red, dst_ref=rows(o_ref, me), send_sem=s2.at[k],
                                                recv_sem=r2.at[k], device_id=dev, device_id_type=MESH_ID)

        def shared_from(k):
            dev, p = peer(k)
            return pltpu.make_async_remote_copy(src_ref=red, dst_ref=rows(o_ref, p), send_sem=s2.at[k],
                                                recv_sem=r2.at[k], device_id=dev, device_id_type=MESH_ID)

        for k in range(1, 8):
            scatter(k).start()
        acc = rows(x_ref, me)[...]
        for k in range(1, 8):
            scatter(k).wait_recv()
            acc = acc + recv[k]
        red[...] = acc
        rows(o_ref, me)[...] = acc
        for k in range(1, 8):
            share(k).start()
        for k in range(1, 8):
            shared_from(k).wait_recv()
        for k in range(1, 8):
            scatter(k).wait_send()
            share(k).wait_send()

    return pl.pallas_call(
        body, name=name,
        in_specs=[pl.BlockSpec(memory_space=pltpu.VMEM)],
        out_specs=pl.BlockSpec(memory_space=pltpu.VMEM),
        out_shape=jax.ShapeDtypeStruct((R, C), F32),
        scratch_shapes=[pltpu.VMEM((8, m, C), F32), pltpu.VMEM((m, C), F32),
                        pltpu.SemaphoreType.DMA((8,)), pltpu.SemaphoreType.DMA((8,)),
                        pltpu.SemaphoreType.DMA((8,)), pltpu.SemaphoreType.DMA((8,))],
        compiler_params=_params(),
    )(buf)


def _allreduce8_two_level(buf, name):
    R, C = buf.shape
    assert R % (8 * SUBLANES) == 0
    m, hr = R // 8, R // 2

    def body(x_ref, o_ref, got0, half, got1, red, ssem, rsem):
        x, y, c = _pos()
        q = 2 * x + y
        sibling = (x, y, 1 - c)

        def half_rows(ref, core):
            return ref.at[pl.ds(pl.multiple_of(core * hr, SUBLANES), hr), :]

        def chunk(ref, core, chip):
            return ref.at[pl.ds(pl.multiple_of(core * hr + chip * m, SUBLANES), m), :]

        def chip_of(k):
            px, py = _flip(x, (k >> 1) & 1), _flip(y, k & 1)
            return (px, py, c), 2 * px + py

        def copy(src, dst, phase, k, dev):
            return pltpu.make_async_remote_copy(src_ref=src, dst_ref=dst, send_sem=ssem.at[phase, k],
                                                recv_sem=rsem.at[phase, k], device_id=dev, device_id_type=MESH_ID)

        swap = copy(half_rows(x_ref, 1 - c), got0, 0, 0, sibling)
        swap.start()
        swap.wait()
        half[...] = half_rows(x_ref, c)[...] + got0[...]

        def scatter(k):
            dev, p = chip_of(k)
            return copy(half.at[pl.ds(pl.multiple_of(p * m, SUBLANES), m), :], got1.at[k], 1, k, dev)

        for k in range(1, 4):
            scatter(k).start()
        acc = half[pl.ds(pl.multiple_of(q * m, SUBLANES), m), :]
        for k in range(1, 4):
            scatter(k).wait_recv()
            acc = acc + got1[k]
        red[...] = acc
        chunk(o_ref, c, q)[...] = acc

        def share(k, landing_chip):
            return copy(red, chunk(o_ref, c, landing_chip), 2, k, chip_of(k)[0])

        for k in range(1, 4):
            share(k, q).start()
        for k in range(1, 4):
            share(k, chip_of(k)[1]).wait_recv()
        back = copy(half_rows(o_ref, c), half_rows(o_ref, c), 3, 0, sibling)
        back.start()
        copy(half_rows(o_ref, 1 - c), half_rows(o_ref, 1 - c), 3, 0, sibling).wait_recv()
        back.wait_send()
        for k in range(1, 4):
            scatter(k).wait_send()
            share(k, q).wait_send()

    return pl.pallas_call(
        body, name=name,
        in_specs=[pl.BlockSpec(memory_space=pltpu.VMEM)],
        out_specs=pl.BlockSpec(memory_space=pltpu.VMEM),
        out_shape=jax.ShapeDtypeStruct((R, C), F32),
        scratch_shapes=[pltpu.VMEM((hr, C), F32), pltpu.VMEM((hr, C), F32), pltpu.VMEM((4, m, C), F32), pltpu.VMEM((m, C), F32),
                        pltpu.SemaphoreType.DMA((4, 4)), pltpu.SemaphoreType.DMA((4, 4))],
        compiler_params=_params(),
    )(buf)


def _bounce(src, dst, buf, sem):
    cin = pltpu.make_async_copy(src, buf, sem)
    cin.start()
    cin.wait()
    cout = pltpu.make_async_copy(buf, dst, sem)
    cout.start()
    cout.wait()


def _chunk(ref, axis, idx, size):
    start = idx * size
    if axis == 0:
        return ref.at[pl.ds(start, size), :]
    return ref.at[:, pl.ds(start, size)]


def _in_proj_gather(hl, win, wout, q_arr, *, rows, tm):
    D, nq = win.shape
    dq, D2 = wout.shape
    ni = rows // tm
    ops = ((0, 1, nq, D // 2), (1, 0, dq, dq // 2))

    def body(q_ref, a_ref, win_ref, wout_ref, p_ref, gin_ref, gout_ref, b_scr, buf_out, lsem, ssem, rsem, fsem, gsem):
        j, i = pl.program_id(0), pl.program_id(1)
        x, y, c = _pos()
        q = 2 * x + y
        srcs = (win_ref, wout_ref)
        dsts = (gin_ref, gout_ref)

        def shard_window(o, chip):
            _, axis, size, _ = ops[o]
            return _chunk(dsts[o], axis, chip, size)

        def half(ref, o, core):
            return ref.at[pl.ds(core * ops[o][3], ops[o][3]), :]

        def half_window(o, chip, core):
            _, axis, size, hs = ops[o]
            if axis == 1:
                return dsts[o].at[pl.ds(core * hs, hs), pl.ds(chip * size, size)]
            return dsts[o].at[pl.ds(chip * size + core * hs, hs), :]

        def chip_of(k):
            px, py = _flip(x, (k >> 1) & 1), _flip(y, k & 1)
            return px, py, 2 * px + py

        def send(o, k):
            px, py, _ = chip_of(k)
            return pltpu.make_async_remote_copy(
                src_ref=half(srcs[o], o, c), dst_ref=half_window(o, q, c), send_sem=ssem.at[o, k],
                recv_sem=rsem.at[o, k], device_id=(px, py, c), device_id_type=MESH_ID)

        def chip_recv(o, k):
            px, py, pq = chip_of(k)
            landed = half_window(o, pq, c)
            pltpu.make_async_remote_copy(src_ref=landed, dst_ref=landed, send_sem=ssem.at[o, k], recv_sem=rsem.at[o, k],
                                         device_id=(px, py, c), device_id_type=MESH_ID).wait_recv()

        def to_sibling(o, k):
            landed = half_window(o, chip_of(k)[2], c)
            return pltpu.make_async_remote_copy(src_ref=landed, dst_ref=landed, send_sem=fsem.at[o, k],
                                                recv_sem=gsem.at[o, k], device_id=(x, y, 1 - c), device_id_type=MESH_ID)

        def from_sibling(o, k):
            theirs = half_window(o, chip_of(k)[2], 1 - c)
            pltpu.make_async_remote_copy(src_ref=theirs, dst_ref=theirs, send_sem=fsem.at[o, k], recv_sem=gsem.at[o, k],
                                         device_id=(x, y, 1 - c), device_id_type=MESH_ID).wait_recv()

        def relay(o, core):
            if core == 0:
                landed, target = half_window(o, chip_of(2)[2], 0), (x, 1 - y, 0)
            else:
                landed, target = half_window(o, chip_of(1)[2], 1), (1 - x, y, 1)
            return pltpu.make_async_remote_copy(src_ref=landed, dst_ref=landed, send_sem=ssem.at[o, 3],
                                                recv_sem=rsem.at[o, 3], device_id=target, device_id_type=MESH_ID)

        def on_core(core, fn):
            @pl.when(c == core)
            def _():
                fn()

        def land(o, k):
            chip_recv(o, k)
            if k == 2:
                on_core(0, lambda: relay(o, 0).start())
            if k == 1:
                on_core(1, lambda: relay(o, 1).start())
            to_sibling(o, k).start()

        def settle(o, k):
            from_sibling(o, k)
            to_sibling(o, k).wait_send()

        def b_load(k, slot):
            src = win_ref if k == 0 else shard_window(0, chip_of(k)[2])
            return pltpu.make_async_copy(src, b_scr.at[slot], lsem.at[0])

        def own_store():
            return pltpu.make_async_copy(b_scr.at[0], shard_window(0, q), lsem.at[2])

        order = (0, 2, 1, 3)
        early = max(ni - 2, 0)

        @pl.when((j == 0) & (i == 0))
        def _():
            for o in range(2):
                for k in (2, 1):
                    send(o, k).start()
            first = b_load(0, 0)
            first.start()
            first.wait()
            own_store().start()
            _bounce(wout_ref, shard_window(1, q), buf_out, lsem.at[1])

        for jj in range(3):
            nxt = order[jj + 1]

            @pl.when((j == jj) & (i == early))
            def _(nxt=nxt):
                land(0, nxt)

            @pl.when((j == jj) & (i == ni - 1))
            def _(jj=jj, nxt=nxt):
                settle(0, nxt)
                if jj == 1:
                    own_store().wait()
                b_load(nxt, (jj + 1) % 2).start()

            @pl.when((j == jj + 1) & (i == 0))
            def _(jj=jj, nxt=nxt):
                b_load(nxt, (jj + 1) % 2).wait()

        @pl.when((j == 3) & (i == 0))
        def _():
            land(1, 2)
            land(1, 1)

        p_ref[...] = jnp.dot(a_ref[...], b_scr[j % 2], preferred_element_type=F32).astype(p_ref.dtype)

        @pl.when((j == 3) & (i == ni - 1))
        def _():
            settle(1, 2)
            settle(1, 1)
            land(1, 3)
            settle(1, 3)
            for o in range(2):
                for k in (2, 1):
                    send(o, k).wait_send()
                for core in range(2):
                    on_core(core, lambda o=o, core=core: relay(o, core).wait_send())

    hbm = pl.BlockSpec(memory_space=pl.ANY)
    grid_spec = pltpu.PrefetchScalarGridSpec(
        num_scalar_prefetch=1, grid=(4, ni),
        in_specs=[pl.BlockSpec((tm, D), lambda j, i, qr: (i, 0)), hbm, hbm],
        out_specs=[pl.BlockSpec((tm, nq), lambda j, i, qr: (i, jnp.bitwise_xor(qr[0], ((j & 1) << 1) | (j >> 1)))),
                   hbm, hbm],
        scratch_shapes=[pltpu.VMEM((2,) + win.shape, win.dtype), pltpu.VMEM(wout.shape, wout.dtype), pltpu.SemaphoreType.DMA((3,))]
        + [pltpu.SemaphoreType.DMA((2, 4)) for _ in range(4)])
    return pl.pallas_call(
        body, name="in_proj_gather", grid_spec=grid_spec,
        out_shape=[jax.ShapeDtypeStruct((rows, 4 * nq), BF16), jax.ShapeDtypeStruct((D, 4 * nq), win.dtype),
                   jax.ShapeDtypeStruct((4 * dq, D2), wout.dtype)],
        compiler_params=_params(("arbitrary", "arbitrary")),
    )(q_arr, hl, win, wout)


def _rs_to_sibling(gs, axes):
    n = len(gs)
    shapes = []
    for g, ax in zip(gs, axes):
        s = list(g.shape)
        s[ax] //= 8
        shapes.append(tuple(s))

    def body(*refs):
        g_refs, mine, landed = refs[:n], refs[n:2 * n], refs[2 * n:3 * n]
        bufs = refs[3 * n:4 * n]
        lsem, ssem, rsem = refs[4 * n:]
        x, y, c = _pos()
        cps = []
        for o in range(n):
            size = shapes[o][axes[o]]
            for j in range(4):
                rc = pltpu.make_async_remote_copy(
                    src_ref=_chunk(g_refs[o], axes[o], 2 * j + 1 - c, size), dst_ref=landed[o].at[j],
                    send_sem=ssem.at[o, j], recv_sem=rsem.at[o, j], device_id=(x, y, 1 - c), device_id_type=MESH_ID)
                rc.start()
                cps.append(rc)
        for o in range(n):
            size = shapes[o][axes[o]]
            for j in range(4):
                _bounce(_chunk(g_refs[o], axes[o], 2 * j + c, size), mine[o].at[j], bufs[o], lsem.at[o])
        for rc in cps:
            rc.wait()

    hbm = pl.BlockSpec(memory_space=pl.ANY)
    outs = [jax.ShapeDtypeStruct((4,) + s, g.dtype) for s, g in zip(shapes, gs)]
    res = pl.pallas_call(
        body, name="rs_to_sibling", in_specs=[hbm] * n, out_specs=[hbm] * (2 * n), out_shape=outs + outs,
        scratch_shapes=[pltpu.VMEM(s, g.dtype) for s, g in zip(shapes, gs)]
        + [pltpu.SemaphoreType.DMA((n,)), pltpu.SemaphoreType.DMA((n, 4)), pltpu.SemaphoreType.DMA((n, 4))],
        compiler_params=_params(),
    )(*gs)
    return res[:n], res[n:]


def _gather_block_side(block):
    r, n_cols = block.shape

    def copies(ins, outs, scr):
        ssem, rsem = scr[1], scr[2]
        x, y, c = _pos()
        return [pltpu.make_async_remote_copy(
            src_ref=ins[0], dst_ref=outs[0].at[k], send_sem=ssem.at[k], recv_sem=rsem.at[k],
            device_id=(_flip(x, (k >> 2) & 1), _flip(y, (k >> 1) & 1), _flip(c, k & 1)), device_id_type=MESH_ID)
            for k in range(1, 8)]

    def start(ins, outs, scr):
        for cp in copies(ins, outs, scr):
            cp.start()

    def finish(ins, outs, scr):
        _bounce(ins[0], outs[0].at[0], scr[0], scr[3].at[0])
        for cp in copies(ins, outs, scr):
            cp.wait()

    return dict(ins=[block], outs=[jax.ShapeDtypeStruct((8, r, n_cols), block.dtype)],
                scratch=[pltpu.VMEM((r, n_cols), block.dtype), pltpu.SemaphoreType.DMA((8,)), pltpu.SemaphoreType.DMA((8,)),
                         pltpu.SemaphoreType.DMA((1,))],
                start=start, finish=finish)


def _join_sides(a, b):
    na_i, na_o, na_s = len(a["ins"]), len(a["outs"]), len(a["scratch"])

    def run(which):
        def fn(ins, outs, scr):
            a[which](ins[:na_i], outs[:na_o], scr[:na_s])
            b[which](ins[na_i:], outs[na_o:], scr[na_s:])
        return fn

    return dict(ins=a["ins"] + b["ins"], outs=a["outs"] + b["outs"], scratch=a["scratch"] + b["scratch"],
                start=run("start"), finish=run("finish"))


def _rs_chips_side(parts):
    n = len(parts)

    def copies(p_refs, slots, scr):
        ssem, rsem = scr[n + 1], scr[n + 2]
        x, y, c = _pos()
        cps = []
        for o in range(n):
            for k in range(1, 4):
                px, py = _flip(x, (k >> 1) & 1), _flip(y, k & 1)
                cps.append(pltpu.make_async_remote_copy(
                    src_ref=p_refs[o].at[2 * px + py], dst_ref=slots[o].at[k], send_sem=ssem.at[o, k],
                    recv_sem=rsem.at[o, k], device_id=(px, py, c), device_id_type=MESH_ID))
        return cps

    def start(p_refs, slots, scr):
        for cp in copies(p_refs, slots, scr):
            cp.start()

    def finish(p_refs, slots, scr):
        x, y, _ = _pos()
        q = 2 * x + y
        for o in range(n):
            _bounce(p_refs[o].at[q], slots[o].at[0], scr[o], scr[n].at[o])
        for cp in copies(p_refs, slots, scr):
            cp.wait()

    return dict(
        ins=list(parts), outs=[jax.ShapeDtypeStruct(p.shape, p.dtype) for p in parts],
        scratch=[pltpu.VMEM(p.shape[1:], p.dtype) for p in parts]
        + [pltpu.SemaphoreType.DMA((n,)), pltpu.SemaphoreType.DMA((n, 4)), pltpu.SemaphoreType.DMA((n, 4))],
        start=start, finish=finish)


def _rs_share(rs, axes):
    n = len(rs)
    shapes = []
    for r, ax in zip(rs, axes):
        s = list(r.shape)
        s[ax] *= 2
        shapes.append(tuple(s))

    def body(*refs):
        r_refs, outs = refs[:n], refs[n:2 * n]
        bufs = refs[2 * n:3 * n]
        lsem, ssem, rsem = refs[3 * n:]
        x, y, c = _pos()
        cps = []
        for o in range(n):
            size = r_refs[o].shape[axes[o]]
            window = _chunk(outs[o], axes[o], c, size)
            rc = pltpu.make_async_remote_copy(src_ref=r_refs[o], dst_ref=window, send_sem=ssem.at[o], recv_sem=rsem.at[o],
                                              device_id=(x, y, 1 - c), device_id_type=MESH_ID)
            rc.start()
            cps.append(rc)
        for o in range(n):
            size = r_refs[o].shape[axes[o]]
            _bounce(r_refs[o], _chunk(outs[o], axes[o], c, size), bufs[o], lsem.at[o])
        for cp in cps:
            cp.wait()

    hbm = pl.BlockSpec(memory_space=pl.ANY)
    return pl.pallas_call(
        body, name="rs_share", in_specs=[hbm] * n, out_specs=[hbm] * n,
        out_shape=[jax.ShapeDtypeStruct(s, r.dtype) for s, r in zip(shapes, rs)],
        scratch_shapes=[pltpu.VMEM(r.shape, r.dtype) for r in rs] + [pltpu.SemaphoreType.DMA((n,)) for _ in range(3)],
        compiler_params=_params(),
    )(*rs)


def _rs_pair_sums(gs, axes):
    mine, landed = _rs_to_sibling(gs, axes)
    pair_sums = []
    for o, (mi, la) in enumerate(zip(mine, landed)):
        rows, cols = mi.shape[0] * mi.shape[1], mi.shape[2]
        s = _elementwise(lambda a, b: a.astype(F32) + b.astype(F32), [(mi.reshape(rows, cols), 0), (la.reshape(rows, cols), 0)],
                         [BF16], rows=rows, cols=cols, name=f"rs_pair_sum{o}")[0]
        pair_sums.append(s.reshape(mi.shape))
    return pair_sums


def _rs_finish(slots, axes):
    reduced = []
    for o, sl in enumerate(slots):
        rows, cols = sl.shape[1], sl.shape[2]
        flat = sl.reshape(4 * rows, cols)
        r = _elementwise(lambda a, b, c, d: (a.astype(F32) + b.astype(F32)) + (c.astype(F32) + d.astype(F32)),
                         [(flat, k * rows) for k in range(4)], [F32], rows=rows, cols=cols, name=f"rs_chip_sum{o}")[0]
        reduced.append(r)
    return _rs_share(reduced, axes)


def _norm_in(x, ctx, g, sc_l, sh_l, sc_c, sh_c, tr):
    L, D = x.shape
    T = ctx.shape[0]
    nx, nc = L // tr, T // tr

    def body(x_ref, c_ref, g_ref, scl, shl, scc, shc, o_ref):
        i = pl.program_id(0)

        def run(src, sc, sh):
            v = src[...]
            r = lax.rsqrt(jnp.mean(v * v, axis=-1, keepdims=True) + EPS)
            o_ref[...] = ((v * r * g_ref[...]) * (1.0 + sc[...]) + sh[...]).astype(o_ref.dtype)

        @pl.when(i < nx)
        def _():
            run(x_ref, scl, shl)

        @pl.when(i >= nx)
        def _():
            run(c_ref, scc, shc)

    vec = pl.BlockSpec((1, D), lambda i: (0, 0))
    return pl.pallas_call(
        body, name="norm_in", grid=(nx + nc,),
        in_specs=[pl.BlockSpec((tr, D), lambda i: (jnp.minimum(i, nx - 1), 0)),
                  pl.BlockSpec((tr, D), lambda i: (jnp.maximum(i - nx, 0), 0)), vec, vec, vec, vec, vec],
        out_specs=pl.BlockSpec((tr, D), lambda i: (i, 0)),
        out_shape=jax.ShapeDtypeStruct((L + T, D), BF16),
        compiler_params=_params(("arbitrary",)),
    )(x, ctx, g, sc_l, sh_l, sc_c, sh_c)


def _tmod(tl, row_w):
    assert row_w & (row_w - 1) == 0
    return lax.broadcasted_iota(jnp.int32, (tl, 1), 0) & (row_w - 1)


def _shift(z, k, tmod, row_w):
    tl = z.shape[0]
    rolled = pltpu.roll(z, k % tl, 0)
    mask = (tmod >= k) if k > 0 else (tmod < row_w + k)
    return jnp.where(mask, rolled, 0.0)


def _conv(z, w_ref, taps, left, tmod, row_w, lanes=slice(None)):
    out = None
    for j in range(taps):
        k = left - j
        term = (z if k == 0 else _shift(z, k, tmod, row_w)) * w_ref[j:j + 1, lanes]
        out = term if out is None else out + term
    return out


def _conv_bwd(dz, z, w_ref, taps, left, tmod, row_w, lanes=slice(None)):
    din = None
    dws = []
    for j in range(taps):
        k = left - j
        shifted = dz if k == 0 else _shift(dz, -k, tmod, row_w)
        term = shifted * w_ref[j:j + 1, lanes]
        din = term if din is None else din + term
        dws.append(jnp.sum(shifted * z, axis=0, keepdims=True))
    return din, dws


def _gate_matmul(xb16_ref, wd_ref, pre_scr, W, ng, gs):
    for g in range(ng):
        pg = jnp.dot(xb16_ref[:, g * gs:(g + 1) * gs], wd_ref[g], preferred_element_type=F32)
        pre_scr[:, g * gs:(g + 1) * gs] = pg[:, :gs]
        pre_scr[:, W + g * gs:W + (g + 1) * gs] = pg[:, gs:]


def _f32(ref, rows, lanes):
    return ref[rows, lanes].astype(F32)


def _sub_loop(tl, sub_r, W, fn):
    def chunk(ci, carry):
        r0 = pl.multiple_of(ci * sub_r, sub_r)
        for lb in range(W // LANES):
            fn(r0, lb * LANES)
        return carry

    lax.fori_loop(0, tl // sub_r, chunk, 0)


def _row_loop(tl, rev, step, init):
    nchunk = tl // SUBLANES

    def chunk(j, carry):
        jj = (nchunk - 1 - j) if rev else j
        c0 = pl.multiple_of(jj * SUBLANES, SUBLANES)
        for r in (range(SUBLANES - 1, -1, -1) if rev else range(SUBLANES)):
            carry = step(c0 + r, carry)
        return carry

    return lax.fori_loop(0, nchunk, chunk, init)


def _mix_fwd(P, d, h_init, wts, *, rows, row_off, row_w, tl, saved0=None, name):
    W = P.shape[1] // 6
    nt = rows // tl
    ob = row_off // tl
    rev = d == 1
    gs = min(LRU_GROUP, W)
    ng = W // gs
    wca, wcb, bcb = wts["wca"], wts["wcb"], wts["bcb"]
    wd, ba, bx, lam = wts["wd"][d], wts["ba"][d], wts["bx"][d], wts["lam"][d]

    def tile(i):
        return (nt - 1 - i) if rev else i

    def pcol(j):
        return pl.BlockSpec((tl, W), lambda i: (tile(i) + ob, j))

    vec = pl.BlockSpec((1, W), lambda i: (0, 0))
    taps = pl.BlockSpec((SUBLANES, W), lambda i: (0, 0))
    wd_spec = pl.BlockSpec(wd.shape, lambda i: (0, 0, 0))
    seq = pl.BlockSpec((tl, W), lambda i: (tile(i), 0))

    sub_r = min(row_w, tl)
    assert tl % sub_r == 0

    def body(*refs):
        if rev:
            (bl, cl, ul, gl, ql, ho, xb_r, xb16_r, wca_r, wd_r, ba_r, bx_r, lam_r, hin, hseq, cat, a_o, r_o, ig_o, m2_o,
             b_scr, pre_scr, carry, sp_scr) = refs
        else:
            (vl, wcb_r, bcb_r, wd_r, ba_r, bx_r, lam_r, hin, hseq, xb_r, xb16_r, a_o, r_o, ig_o, m2_o,
             b_scr, pre_scr, carry, sp_scr) = refs
        i = pl.program_id(0)

        @pl.when(i == 0)
        def _():
            carry[...] = hin[...]

        sp_scr[...] = _softplus(-lam_r[...])
        tmod = _tmod(sub_r, sub_r)

        def conv_in(r0, l0):
            rs, ls = pl.ds(r0, sub_r), pl.ds(l0, LANES)
            xb = _conv(_f32(vl, rs, ls), wcb_r, 4, 2, tmod, sub_r, ls) + bcb_r[:, ls]
            xb_r[rs, ls] = xb
            xb16_r[rs, ls] = xb.astype(BF16)

        def gates(r0, l0):
            rs, ls = pl.ds(r0, sub_r), pl.ds(l0, LANES)
            r, ig, a, m2 = _gates(pre_scr[rs, ls] + ba_r[:, ls], pre_scr[rs, pl.ds(W + l0, LANES)] + bx_r[:, ls],
                                  sp_scr[:, ls])
            a_o[rs, ls] = a
            r_o[rs, ls] = r.astype(r_o.dtype)
            ig_o[rs, ls] = ig.astype(ig_o.dtype)
            m2_o[rs, ls] = m2.astype(m2_o.dtype)
            m = jnp.where(m2 > 0.0, m2 * lax.rsqrt(m2), 0.0)
            b_scr[rs, ls] = m * (ig * xb_r[rs, ls])

        if not rev:
            _sub_loop(tl, sub_r, W, conv_in)
        _gate_matmul(xb16_r, wd_r, pre_scr, W, ng, gs)
        _sub_loop(tl, sub_r, W, gates)

        def step(t, h):
            h = a_o[pl.ds(t, 1), :] * h + b_scr[pl.ds(t, 1), :]
            hseq[pl.ds(t, 1), :] = h
            return h

        carry[...] = _row_loop(tl, rev, step, carry[...])

        if rev:
            def mix_out(r0, l0):
                rs, ls = pl.ds(r0, sub_r), pl.ds(l0, LANES)
                yb = (ho[rs, ls] + hseq[rs, ls]) * _silu(_f32(ql, rs, ls))
                ya = (_f32(bl, rs, ls) * _conv(_f32(cl, rs, ls) * _f32(ul, rs, ls), wca_r, 3, 1, tmod, sub_r, ls)
                      * _silu(_f32(gl, rs, ls)))
                cat[rs, ls] = ya.astype(cat.dtype)
                cat[rs, pl.ds(W + l0, LANES)] = yb.astype(cat.dtype)

            _sub_loop(tl, sub_r, W, mix_out)

    scratch = [pltpu.VMEM((tl, W), F32), pltpu.VMEM((tl, 2 * W), F32), pltpu.VMEM((1, W), F32), pltpu.VMEM((1, W), F32)]
    f32_seq = jax.ShapeDtypeStruct((rows, W), F32)
    kept_gates = [f32_seq] + [jax.ShapeDtypeStruct((rows, W), BF16)] * 3
    if rev:
        in_specs = [pcol(j) for j in (0, 1, 2, 3, 5)] + [seq, seq, seq, taps, wd_spec, vec, vec, vec, vec]
        args = [P] * 5 + [saved0["h"], saved0["xb"], saved0["xb16"], wca, wd, ba, bx, lam, h_init]
        out_specs = [seq, pl.BlockSpec((tl, 2 * W), lambda i: (tile(i), 0))] + [seq] * 4
        out_shape = [f32_seq, jax.ShapeDtypeStruct((rows, 2 * W), BF16)] + kept_gates
    else:
        in_specs = [pcol(4), taps, vec, wd_spec, vec, vec, vec, vec]
        args = [P, wcb, bcb, wd, ba, bx, lam, h_init]
        out_specs = [seq] * 7
        out_shape = [f32_seq, f32_seq, jax.ShapeDtypeStruct((rows, W), BF16)] + kept_gates
    res = pl.pallas_call(
        body, name=name, grid=(nt,), in_specs=in_specs, out_specs=out_specs, out_shape=out_shape,
        scratch_shapes=scratch, compiler_params=_params(("arbitrary",)),
    )(*args)
    gates = dict(zip(("a", "r", "ig", "m2"), res[-4:]))
    if rev:
        return res[0], res[1], gates
    return dict(h=res[0], xb=res[1], xb16=res[2]), gates


_BWD_SCRATCH = ("dyl", "g", "dp16", "sp", "dlf", "edge", "c")
_FWD_SAVED = ("xb", "xb16", "a", "r", "ig", "m2")


def _bwd_scratch(tl, W):
    shapes = {"dyl": pltpu.VMEM((tl, W), F32), "g": pltpu.VMEM((tl, W), F32), "dp16": pltpu.VMEM((tl, 2 * W), BF16),
              "sp": pltpu.VMEM((1, W), F32), "dlf": pltpu.VMEM((1, W), F32), "edge": pltpu.VMEM((1, W), F32),
              "c": pltpu.VMEM((1, W), F32)}
    return [shapes[n] for n in _BWD_SCRATCH]


def _lru_bwd_tile(d, dy_fn, hs_ref, wd_r, lam_r, scr, acc, first, tl, sub_r, ng, gs):
    dwd_ref, dba_ref, dbx_ref, dlam_ref = acc
    W = hs_ref.shape[1]
    assert gs % LANES == 0
    rev = d == 0
    lam = lam_r[...]
    scr["sp"][...] = _softplus(-lam)
    scr["dlf"][...] = -_sigmoid(-lam)

    @pl.when(first)
    def _():
        dwd_ref[...] = jnp.zeros_like(dwd_ref)
        dba_ref[...] = jnp.zeros_like(dba_ref)
        dbx_ref[...] = jnp.zeros_like(dbx_ref)
        dlam_ref[...] = jnp.zeros_like(dlam_ref)

    def state_grad(r0, l0):
        rs, ls = pl.ds(r0, sub_r), pl.ds(l0, LANES)
        scr["dyl"][rs, ls] = dy_fn(rs, ls)

    _sub_loop(tl, sub_r, W, state_grad)

    def step(t, c):
        g = scr["dyl"][pl.ds(t, 1), :] + c
        scr["g"][pl.ds(t, 1), :] = g
        return scr["a"][pl.ds(t, 1), :] * g

    scr["c"][...] = _row_loop(tl, rev, step, scr["c"][...])
    row = lax.broadcasted_iota(jnp.int32, (sub_r, 1), 0)

    def grads(r0, l0):
        rs, ls = pl.ds(r0, sub_r), pl.ds(l0, LANES)
        g, a, m2 = scr["g"][rs, ls], scr["a"][rs, ls], _f32(scr["m2"], rs, ls)
        r, ig, xb = _f32(scr["r"], rs, ls), _f32(scr["ig"], rs, ls), scr["xb"][rs, ls]
        h = hs_ref[rs, ls]
        if d == 0:
            e0 = pl.multiple_of(jnp.maximum(r0 - SUBLANES, 0), SUBLANES)
            edge = jnp.where(r0 == 0, scr["edge"][:, ls], hs_ref[pl.ds(e0, SUBLANES), ls][SUBLANES - 1:, :])
            hprev = jnp.where(row == 0, edge, pltpu.roll(h, 1, 0))
        else:
            e0 = pl.multiple_of(jnp.minimum(r0 + sub_r, tl - SUBLANES), SUBLANES)
            edge = jnp.where(r0 == tl - sub_r, scr["edge"][:, ls], hs_ref[pl.ds(e0, SUBLANES), ls][:1, :])
            hprev = jnp.where(row == sub_r - 1, edge, pltpu.roll(h, sub_r - 1, 0))
        rsq = lax.rsqrt(m2)
        gm = g * (m2 * rsq)
        d_la = (g * hprev) * a - (g * (ig * xb)) * ((1.0 - m2) * rsq)
        d_pr = d_la * ((-LRU_C) * scr["sp"][:, ls]) * (r * (1.0 - r))
        d_pi = (gm * xb) * (ig * (1.0 - ig))
        scr["dyl"][rs, ls] = gm * ig
        dlam_ref[:, ls] += jnp.sum(d_la * ((-LRU_C) * r), axis=0, keepdims=True) * scr["dlf"][:, ls]
        dba_ref[:, ls] += jnp.sum(d_pr, axis=0, keepdims=True)
        dbx_ref[:, ls] += jnp.sum(d_pi, axis=0, keepdims=True)
        gi, off = divmod(l0, gs)
        scr["dp16"][rs, pl.ds(gi * 2 * gs + off, LANES)] = d_pr.astype(BF16)
        scr["dp16"][rs, pl.ds(gi * 2 * gs + gs + off, LANES)] = d_pi.astype(BF16)

    _sub_loop(tl, sub_r, W, grads)
    for gi in range(ng):
        dp = scr["dp16"][:, gi * 2 * gs:(gi + 1) * 2 * gs]
        scr["g"][:, gi * gs:(gi + 1) * gs] = lax.dot_general(dp, wd_r[gi], (((1,), (1,)), ((), ())),
                                                             preferred_element_type=F32)
        dwd_ref[gi] += lax.dot_general(scr["xb16"][:, gi * gs:(gi + 1) * gs], dp, (((0,), (0,)), ((), ())),
                                       preferred_element_type=F32)


def _edge_block(h, tl, nt, d):
    W = h.shape[1]
    per = tl // SUBLANES
    if d == 0:
        return pl.BlockSpec((SUBLANES, W), lambda i: (jnp.maximum((nt - 1 - i) * per - 1, 0), 0))
    return pl.BlockSpec((SUBLANES, W), lambda i: (jnp.minimum((i + 1) * per, nt * per - 1), 0))


def _mix_bwd0(P, dcat, saved0, gates0, h_init, c_init, wts, *, rows, row_off, row_w, tl, name):
    W = P.shape[1] // 6
    nt = rows // tl
    ob = row_off // tl
    gs = min(LRU_GROUP, W)
    ng = W // gs
    wd, lam = wts["wd"][0], wts["lam"][0]
    h0s = saved0["h"]
    kept = [saved0["xb"], saved0["xb16"]] + [gates0[n] for n in ("a", "r", "ig", "m2")]

    def tile(i):
        return nt - 1 - i

    vec = pl.BlockSpec((1, W), lambda i: (0, 0))
    wd_spec = pl.BlockSpec(wd.shape, lambda i: (0, 0, 0))
    seq = pl.BlockSpec((tl, W), lambda i: (tile(i), 0))

    sub_r = min(row_w, tl)
    assert tl % sub_r == 0

    def body(ql, dyb, hs, hedge8, xb_r, xb16_r, a_r, r_r, ig_r, m2_r, wd_r, lam_r, hin, cin,
             dxb_o, dwd_o, dba_o, dbx_o, dlam_o, cfin, *scratch):
        scr = dict(zip(_BWD_SCRATCH, scratch, strict=True))
        scr.update(zip(_FWD_SAVED, (xb_r, xb16_r, a_r, r_r, ig_r, m2_r), strict=True))
        i = pl.program_id(0)

        @pl.when(i == 0)
        def _():
            scr["c"][...] = cin[...]

        scr["edge"][...] = jnp.where(i == nt - 1, hin[...], hedge8[SUBLANES - 1:SUBLANES, :])
        _lru_bwd_tile(0, lambda rs, ls: _f32(dyb, rs, ls) * _silu(_f32(ql, rs, ls)), hs, wd_r, lam_r, scr,
                      (dwd_o, dba_o, dbx_o, dlam_o), i == 0, tl, sub_r, ng, gs)
        dxb_o[...] = scr["dyl"][...] + scr["g"][...]
        cfin[...] = scr["c"][...]

    return pl.pallas_call(
        body, name=name, grid=(nt,),
        in_specs=[pl.BlockSpec((tl, W), lambda i: (tile(i) + ob, 5)), pl.BlockSpec((tl, W), lambda i: (tile(i), 1)), seq,
                  _edge_block(h0s, tl, nt, 0)] + [seq] * 6 + [wd_spec, vec, vec, vec],
        out_specs=[seq, wd_spec, vec, vec, vec, vec],
        out_shape=[jax.ShapeDtypeStruct((rows, W), F32), jax.ShapeDtypeStruct(wd.shape, F32)]
        + [jax.ShapeDtypeStruct((1, W), F32)] * 4,
        scratch_shapes=_bwd_scratch(tl, W),
        compiler_params=_params(("arbitrary",)),
    )(P, dcat, h0s, h0s, *kept, wd, lam, h_init, c_init)


def _mix_bwd1(P, dcat, saved0, h1s, gates1, dxb0, h_init, c_init, wts, *, rows, row_off, row_w, tl, name,
              dp_rows=None, dp_off=0, dp_into=None):
    dp_rows = rows if dp_rows is None else dp_rows
    dpb = dp_off // tl
    W = P.shape[1] // 6
    nt = rows // tl
    ob = row_off // tl
    gs = min(LRU_GROUP, W)
    ng = W // gs
    wca, wcb = wts["wca"], wts["wcb"]
    wd, lam = wts["wd"][1], wts["lam"][1]
    h0s = saved0["h"]
    kept = [saved0["xb"], saved0["xb16"]] + [gates1[n] for n in ("a", "r", "ig", "m2")]

    vec = pl.BlockSpec((1, W), lambda i: (0, 0))
    taps = pl.BlockSpec((SUBLANES, W), lambda i: (0, 0))
    wd_spec = pl.BlockSpec(wd.shape, lambda i: (0, 0, 0))
    seq = pl.BlockSpec((tl, W), lambda i: (i, 0))

    sub_r = min(row_w, tl)
    assert tl % sub_r == 0

    def body(*refs):
        if dp_into is not None:
            refs = refs[1:]
        (bl, cl, ul, gl, vl, ql, dya, dyb, h0, h1, hedge8, dx0, xb_r, xb16_r, a_r, r_r, ig_r, m2_r, wca_r, wcb_r,
         wd_r, lam_r, hin, cin, dp_o, dwd_o, dba_o, dbx_o, dlam_o, dwca_o, dwcb_o, dbcb_o, cfin, *scratch) = refs
        scr = dict(zip(_BWD_SCRATCH, scratch, strict=True))
        scr.update(zip(_FWD_SAVED, (xb_r, xb16_r, a_r, r_r, ig_r, m2_r), strict=True))
        i = pl.program_id(0)

        @pl.when(i == 0)
        def _():
            scr["c"][...] = cin[...]
            dwca_o[...] = jnp.zeros_like(dwca_o)
            dwcb_o[...] = jnp.zeros_like(dwcb_o)
            dbcb_o[...] = jnp.zeros_like(dbcb_o)

        scr["edge"][...] = jnp.where(i == nt - 1, hin[...], hedge8[0:1, :])
        _lru_bwd_tile(1, lambda rs, ls: _f32(dyb, rs, ls) * _silu(_f32(ql, rs, ls)), h1, wd_r, lam_r, scr,
                      (dwd_o, dba_o, dbx_o, dlam_o), i == 0, tl, sub_r, ng, gs)
        cfin[...] = scr["c"][...]
        tmod = _tmod(sub_r, sub_r)

        def rest(r0, l0):
            rs, ls = pl.ds(r0, sub_r), pl.ds(l0, LANES)
            dxb = dx0[rs, ls] + scr["dyl"][rs, ls] + scr["g"][rs, ls]
            dv, dwb = _conv_bwd(dxb, _f32(vl, rs, ls), wcb_r, 4, 2, tmod, sub_r, ls)
            for j in range(4):
                dwcb_o[j:j + 1, ls] += dwb[j]
            dbcb_o[:, ls] += jnp.sum(dxb, axis=0, keepdims=True)
            q = _f32(ql, rs, ls)
            sq = _sigmoid(q)
            dq = _f32(dyb, rs, ls) * (h0[rs, ls] + h1[rs, ls]) * (sq * (1.0 + q * (1.0 - sq)))
            b_, c_, u_, g_ = _f32(bl, rs, ls), _f32(cl, rs, ls), _f32(ul, rs, ls), _f32(gl, rs, ls)
            z = c_ * u_
            cz = _conv(z, wca_r, 3, 1, tmod, sub_r, ls)
            sgm = _sigmoid(g_)
            sg = g_ * sgm
            da = _f32(dya, rs, ls)
            dz, dwa = _conv_bwd(da * b_ * sg, z, wca_r, 3, 1, tmod, sub_r, ls)
            for j in range(3):
                dwca_o[j:j + 1, ls] += dwa[j]
            parts = (da * cz * sg, dz * u_, dz * c_, da * b_ * cz * (sgm * (1.0 + g_ * (1.0 - sgm))), dv, dq)
            for k, val in enumerate(parts):
                dp_o[rs, pl.ds(k * W + l0, LANES)] = val.astype(dp_o.dtype)

        _sub_loop(tl, sub_r, W, rest)

    def pcol(j):
        return pl.BlockSpec((tl, W), lambda i: (i + ob, j))

    prev = [] if dp_into is None else [dp_into]
    return pl.pallas_call(
        body, name=name, grid=(nt,), input_output_aliases={} if dp_into is None else {0: 0},
        in_specs=[pl.BlockSpec(memory_space=pl.ANY)] * len(prev) + [pcol(j) for j in range(6)]
        + [pl.BlockSpec((tl, W), lambda i: (i, 0)), pl.BlockSpec((tl, W), lambda i: (i, 1)), seq, seq,
           _edge_block(h1s, tl, nt, 1), seq] + [seq] * 6 + [taps, taps, wd_spec, vec, vec, vec],
        out_specs=[pl.BlockSpec((tl, 6 * W), lambda i: (i + dpb, 0)), wd_spec, vec, vec, vec, taps, taps, vec, vec],
        out_shape=[jax.ShapeDtypeStruct((dp_rows, 6 * W), BF16), jax.ShapeDtypeStruct(wd.shape, F32)]
        + [jax.ShapeDtypeStruct((1, W), F32)] * 3
        + [jax.ShapeDtypeStruct((SUBLANES, W), F32)] * 2 + [jax.ShapeDtypeStruct((1, W), F32)] * 2,
        scratch_shapes=_bwd_scratch(tl, W),
        compiler_params=_params(("arbitrary",)),
    )(*prev, *([P] * 6), dcat, dcat, h0s, h1s, h1s, dxb0, *kept, wca, wcb, wd, lam, h_init, c_init)


def _loss_head(out, x, tgt, gt, fg, tr):
    L, D = x.shape

    def body(o_ref, x_ref, t_ref, gt_ref, fg_ref, dn_o, do_o, dfg_o, dgt_o, loss_o):
        i = pl.program_id(0)

        @pl.when(i == 0)
        def _():
            dfg_o[...] = jnp.zeros_like(dfg_o)
            dgt_o[...] = jnp.zeros_like(dgt_o)
            loss_o[...] = jnp.zeros_like(loss_o)

        o = o_ref[...].astype(F32)
        gt_v = gt_ref[...]
        fg_v = fg_ref[...]
        n = x_ref[...] + gt_v * o
        r = lax.rsqrt(jnp.mean(n * n, axis=-1, keepdims=True) + EPS)
        nr = n * r
        e = nr * fg_v - t_ref[...]
        loss_o[...] += 0.5 * jnp.sum(jnp.mean(e * e, axis=-1, keepdims=True))
        dy = e * (1.0 / D)
        dfg_o[...] += jnp.sum(dy * nr, axis=0, keepdims=True)
        qv = dy * fg_v
        dn = r * (qv - nr * jnp.mean(qv * nr, axis=-1, keepdims=True))
        dgt_o[...] += jnp.sum(dn * o, axis=0, keepdims=True)
        dn_o[...] = dn.astype(dn_o.dtype)
        do_o[...] = (dn * gt_v).astype(do_o.dtype)

    blk = pl.BlockSpec((tr, D), lambda i: (i, 0))
    vec = pl.BlockSpec((1, D), lambda i: (0, 0))
    return pl.pallas_call(
        body, name="loss_head", grid=(L // tr,), in_specs=[blk, blk, blk, vec, vec],
        out_specs=[blk, blk, vec, vec, pl.BlockSpec((SUBLANES, LANES), lambda i: (0, 0))],
        out_shape=[jax.ShapeDtypeStruct((L, D), BF16), jax.ShapeDtypeStruct((L, D), BF16),
                   jax.ShapeDtypeStruct((1, D), F32), jax.ShapeDtypeStruct((1, D), F32),
                   jax.ShapeDtypeStruct((SUBLANES, LANES), F32)],
        compiler_params=_params(("arbitrary",)),
    )(out, x, tgt, gt, fg)


def _norm_bwd(dhl, x, dn, g, sc, tr, name):
    L, D = x.shape
    with_x = dn is not None

    def body(*refs):
        if with_x:
            d_ref, x_ref, dn_ref, g_ref, sc_ref, gx_o, dsh_o, dsc_o, dg_o = refs
        else:
            d_ref, x_ref, g_ref, sc_ref, dsh_o, dsc_o, dg_o = refs
        i = pl.program_id(0)

        @pl.when(i == 0)
        def _():
            dsh_o[...] = jnp.zeros_like(dsh_o)
            dsc_o[...] = jnp.zeros_like(dsc_o)
            dg_o[...] = jnp.zeros_like(dg_o)

        d = d_ref[...].astype(F32)
        xv = x_ref[...]
        g_v = g_ref[...]
        r = lax.rsqrt(jnp.mean(xv * xv, axis=-1, keepdims=True) + EPS)
        xr = xv * r
        dsh_o[...] += jnp.sum(d, axis=0, keepdims=True)
        dsc_o[...] += jnp.sum(d * (xr * g_v), axis=0, keepdims=True)
        dxn = d * (1.0 + sc_ref[...])
        dg_o[...] += jnp.sum(dxn * xr, axis=0, keepdims=True)
        if with_x:
            qv = dxn * g_v
            gx_o[...] = r * (qv - xr * jnp.mean(qv * xr, axis=-1, keepdims=True)) + dn_ref[...].astype(F32)

    blk = pl.BlockSpec((tr, D), lambda i: (i, 0))
    vec = pl.BlockSpec((1, D), lambda i: (0, 0))
    vshape = jax.ShapeDtypeStruct((1, D), F32)
    res = pl.pallas_call(
        body, name=name, grid=(L // tr,),
        in_specs=[blk, blk] + ([blk] if with_x else []) + [vec, vec],
        out_specs=([blk] if with_x else []) + [vec, vec, vec],
        out_shape=([jax.ShapeDtypeStruct((L, D), F32)] if with_x else []) + [vshape] * 3,
        compiler_params=_params(("arbitrary",)),
    )(*([dhl, x] + ([dn] if with_x else []) + [g, sc]))
    return res if with_x else [None] + list(res)


def _pack_blockdiag(wa, wx, gs):
    H, hd, _ = wa.shape
    hp = gs // hd
    ng = H // hp
    eye = jnp.eye(hp, dtype=wa.dtype)

    def bd(w):
        return jnp.einsum("gpij,pq->gpiqj", w.reshape(ng, hp, hd, hd), eye).reshape(ng, gs, gs)

    return jnp.concatenate([bd(wa), bd(wx)], axis=-1).astype(BF16)


def _unpack_blockdiag(dwd, H, hd, gs):
    hp = gs // hd
    ng = H // hp
    eye = jnp.eye(hp, dtype=dwd.dtype)

    def diag(dm):
        return jnp.einsum("gpiqj,pq->gpij", dm.reshape(ng, hp, hd, hp, hd), eye).reshape(H, hd, hd)

    return diag(dwd[:, :, :gs]), diag(dwd[:, :, gs:])


def kernel(x, c, ctx, c_ctx, norm_g, w_ada, b_ada, w_in, w_conv_a, w_conv_b, b_conv_b, lru_wa, lru_ba, lru_wx, lru_bx, lru_lambda, w_out, final_g, loss_target, m_c_ctx, m_norm_g, m_w_ada, m_b_ada, m_w_in, m_w_conv_a, m_w_conv_b, m_b_conv_b, m_lru_wa, m_lru_ba, m_lru_wx, m_lru_bx, m_lru_lambda, m_w_out, m_final_g, v_c_ctx, v_norm_g, v_w_ada, v_b_ada, v_w_in, v_w_conv_a, v_w_conv_b, v_b_conv_b, v_lru_wa, v_lru_ba, v_lru_wx, v_lru_bx, v_lru_lambda, v_w_out, v_final_g):
    xi, yi, ci = _pos()
    me = 4 * xi + 2 * yi + ci
    q = 2 * xi + yi
    first_core = (ci == 0).astype(F32)

    L, D = x.shape[1], x.shape[2]
    T = ctx.shape[1]
    W = D // 2
    Wq = W // 4
    H, hd = lru_wa.shape[2], lru_wa.shape[3]
    gs = min(LRU_GROUP, W)
    nq = w_ada.shape[2]
    tl = min(256, T, L)
    tr = min(256, T, L)
    x2, ctx2, tgt2 = x[0], ctx[0], loss_target[0]

    def place(shard, full_cols):
        z = jnp.zeros((shard.shape[0], full_cols), F32)
        return lax.dynamic_update_slice(z, shard * first_core, (0, q * shard.shape[1]))

    c_rows = lax.dynamic_update_slice(jnp.zeros((8, D), F32), c, (me, 0))
    small_in = [c_rows, place(w_conv_a[0], W), place(w_conv_b[0], W), place(lru_ba[0], W), place(lru_bx[0], W),
                place(lru_lambda[0], W)]
    small_shapes = [a.shape for a in small_in]
    gathered = _allreduce8(_pack(small_in, 8 * SUBLANES), "gather_small")
    c_all, wca, wcb, ba_all, bx_all, lam_all = _unpack(gathered, small_shapes)

    s_rows = jnp.concatenate([c_all, c_ctx[None, :], jnp.zeros((7, D), F32)], axis=0)
    mod_part = _matmul(s_rows, w_ada[0], a_act="silu", bias=lax.dynamic_slice(b_ada, (0, q * nq), (1, nq)),
                       tm=16, tn=nq, tk=512, name="ada_fwd")
    mod_all = _allreduce8(_pack([place(mod_part[:9], 4 * nq)], 8 * SUBLANES), "gather_mod")
    mod_all = _unpack(mod_all, [(9, 4 * nq)])[0]
    mod_l = lax.dynamic_slice(mod_all, (me, 0), (1, 3 * D))
    mod_c = mod_all[8:9]
    sh_l, sc_l, gt_l = mod_l[:, :D], mod_l[:, D:2 * D], mod_l[:, 2 * D:]
    sh_c, sc_c = mod_c[:, :D], mod_c[:, D:2 * D]

    pad_taps = lambda w: jnp.pad(w, ((0, SUBLANES - w.shape[0]), (0, 0)))
    wts = {
        "wca": pad_taps(wca), "wcb": pad_taps(wcb), "bcb": b_conv_b,
        "wd": [_pack_blockdiag(lru_wa[0, d], lru_wx[0, d], gs) for d in range(2)],
        "ba": [ba_all[d:d + 1] for d in range(2)], "bx": [bx_all[d:d + 1] for d in range(2)],
        "lam": [lam_all[d:d + 1] for d in range(2)],
    }

    hl = _norm_in(x2, ctx2, norm_g, sc_l, sh_l, sc_c, sh_c, tr)
    p_lat, win_full, wout_full = _in_proj_gather(hl, w_in[0].astype(BF16), w_out[0].astype(BF16),
                                                 jnp.reshape(q, (1,)).astype(jnp.int32), rows=L, tm=min(1024, L))
    p_ctx = _matmul(hl, win_full, a_rows=T, a_off=L, tm=T, tn=1536, tk=D, out_dtype=BF16, name="in_proj_ctx")
    zero_w = jnp.zeros((1, W), F32)
    ctx0, cgates0 = _mix_fwd(p_ctx, 0, zero_w, wts, rows=T, row_off=0, row_w=T, tl=tl, name="ctx_fwd0")
    c1s, _, cgates1 = _mix_fwd(p_ctx, 1, zero_w, wts, rows=T, row_off=0, row_w=T, tl=tl, saved0=ctx0, name="ctx_fwd1")
    h0_init, h1_init = ctx0["h"][T - 1:T], c1s[0:1]
    tl_tall = 2 * tl if L % (2 * tl) == 0 else tl
    lat0, gates0 = _mix_fwd(p_lat, 0, h0_init, wts, rows=L, row_off=0, row_w=GRID_W, tl=tl_tall, name="mix_fwd0")
    h1s, cat, gates1 = _mix_fwd(p_lat, 1, h1_init, wts, rows=L, row_off=0, row_w=GRID_W, tl=tl_tall, saved0=lat0,
                                name="mix_fwd1")
    out = _matmul(cat, wout_full, tm=512, tn=D, tk=2 * W, out_dtype=BF16, name="out_proj")
    tr_lat = 2 * tr if L % (2 * tr) == 0 else tr
    dn, dout, dfg, dgt, loss_blk = _loss_head(out, x2, tgt2, gt_l, final_g[None, :], tr_lat)

    dcat = _matmul(dout, wout_full, tb=True, tm=512, tn=2 * W, tk=D, out_dtype=BF16, name="out_proj_bwd")
    gw_out = _matmul(cat, dout, ta=True, tm=1024, tn=D, tk=2048, out_dtype=BF16, name="w_out_grad")
    dxb0, dwd0, dba0, dbx0, dlam0, ch0 = _mix_bwd0(p_lat, dcat, lat0, gates0, h0_init, zero_w, wts, rows=L, row_off=0,
                                                   row_w=GRID_W, tl=tl_tall, name="mix_bwd0")
    dp_lat, dwd1, dba1, dbx1, dlam1, dwca, dwcb, dbcb, ch1 = _mix_bwd1(
        p_lat, dcat, lat0, h1s, gates1, dxb0, h1_init, zero_w, wts, rows=L, row_off=0, row_w=GRID_W, tl=tl,
        name="mix_bwd1", dp_rows=L + T)
    zero_cat = jnp.zeros((T, 2 * W), BF16)
    cxb0, cwd0, cba0, cbx0, clam0, _ = _mix_bwd0(p_ctx, zero_cat, ctx0, cgates0, zero_w, ch0, wts, rows=T, row_off=0,
                                                 row_w=T, tl=tl, name="ctx_bwd0")
    dp, cwd1, cba1, cbx1, clam1, cwca, cwcb, cbcb, _ = _mix_bwd1(
        p_ctx, zero_cat, ctx0, c1s, cgates1, cxb0, zero_w, ch1, wts, rows=T, row_off=0, row_w=T, tl=tl, name="ctx_bwd1",
        dp_rows=L + T, dp_off=L, dp_into=dp_lat)

    gw_in = _matmul(hl, dp, ta=True, tm=1024, tn=1536, tk=2816, out_dtype=BF16, name="w_in_grad")
    rs_axes = [1, 0]
    pair_sums = _rs_pair_sums([gw_in, gw_out], rs_axes)
    dhc = _matmul(dp, win_full, tb=True, a_rows=T, a_off=L, tm=T, tn=D, tk=3072, name="in_proj_bwd_ctx")
    _, dsh_c, dsc_c, dng_c = _norm_bwd(dhc, ctx2, None, norm_g, sc_c, tr, "norm_bwd_ctx")
    zeros_d = jnp.zeros((1, D), F32)
    dmod_c = jnp.concatenate([dsh_c, dsc_c, zeros_d], axis=1)
    dhl, (*rs_slots, dmod_c_all) = _matmul(
        dp, win_full, tb=True, a_rows=L, tm=512, tn=D, tk=6 * W, out_dtype=BF16, name="in_proj_bwd",
        side=_join_sides(_rs_chips_side(pair_sums), _gather_block_side(jnp.pad(dmod_c, ((0, SUBLANES - 1), (0, 0))))))
    gx, dsh_l, dsc_l, dng_l = _norm_bwd(dhl, x2, dn, norm_g, sc_l, tr_lat, "norm_bwd")
    gc_rows = _matmul(lax.dynamic_slice(dmod_c_all.reshape(8 * SUBLANES, 3 * D), (0, q * nq), (8 * SUBLANES, nq)), w_ada[0],
                      tb=True, dsilu_mul=c_ctx[None, :], tm=8 * SUBLANES, tn=D, tk=512, name="c_ctx_grad")
    gc_part = jnp.sum(gc_rows, axis=0, keepdims=True) * first_core

    g_in_shard, g_out_shard = _rs_finish(rs_slots, rs_axes)

    dwa0, dwx0 = _unpack_blockdiag(dwd0 + cwd0, H, hd, gs)
    dwa1, dwx1 = _unpack_blockdiag(dwd1 + cwd1, H, hd, gs)
    dmod_l = jnp.concatenate([dsh_l, dsc_l, dgt], axis=1)
    small_g = [
        lax.dynamic_update_slice(jnp.zeros((8, 3 * D), F32), dmod_l, (me, 0)), dmod_c,
        dfg, dng_l + dng_c, (dwca + cwca)[:3], (dwcb + cwcb)[:4], dbcb + cbcb,
        jnp.stack([dwa0, dwa1]), jnp.stack([dwx0, dwx1]),
        jnp.concatenate([dba0 + cba0, dba1 + cba1], axis=0), jnp.concatenate([dbx0 + cbx0, dbx1 + cbx1], axis=0),
        jnp.concatenate([dlam0 + clam0, dlam1 + clam1], axis=0), loss_blk[0:1, 0:1], gc_part,
    ]
    g_shapes = [a.shape for a in small_g]
    (g_rows, g_modc, g_fg, g_ng, g_wca, g_wcb, g_bcb, g_wa, g_wx, g_ba, g_bx, g_lam, loss_sum, g_c_ctx) = _unpack(
        _allreduce8_two_level(_pack(small_g, 8 * SUBLANES), "reduce_small"), g_shapes)

    g_mod = jnp.concatenate([g_rows, g_modc, jnp.zeros((7, 3 * D), F32)], axis=0)
    g_mod_q = lax.dynamic_slice(g_mod, (0, q * nq), (16, nq))
    g_w_ada = _matmul(s_rows, g_mod_q, ta=True, a_act="silu", tm=1024, tn=nq, tk=16, name="w_ada_grad")
    g_b_ada = jnp.sum(g_mod[:9], axis=0, keepdims=True)

    def shard_cols(a, width):
        return lax.dynamic_slice(a, (0, q * width), (a.shape[0], width))

    grads = {
        "c_ctx": g_c_ctx, "norm_g": g_ng, "b_ada": g_b_ada,
        "w_conv_a": shard_cols(g_wca, Wq)[None], "w_conv_b": shard_cols(g_wcb, Wq)[None], "b_conv_b": g_bcb,
        "lru_wa": g_wa[None], "lru_ba": shard_cols(g_ba, Wq)[None], "lru_wx": g_wx[None],
        "lru_bx": shard_cols(g_bx, Wq)[None], "lru_lambda": shard_cols(g_lam, Wq)[None], "final_g": g_fg[0],
    }
    small_names = list(grads)
    given = dict(c_ctx=(c_ctx, m_c_ctx, v_c_ctx), norm_g=(norm_g, m_norm_g, v_norm_g), b_ada=(b_ada, m_b_ada, v_b_ada),
                 w_conv_a=(w_conv_a, m_w_conv_a, v_w_conv_a), w_conv_b=(w_conv_b, m_w_conv_b, v_w_conv_b),
                 b_conv_b=(b_conv_b, m_b_conv_b, v_b_conv_b), lru_wa=(lru_wa, m_lru_wa, v_lru_wa),
                 lru_ba=(lru_ba, m_lru_ba, v_lru_ba), lru_wx=(lru_wx, m_lru_wx, v_lru_wx),
                 lru_bx=(lru_bx, m_lru_bx, v_lru_bx), lru_lambda=(lru_lambda, m_lru_lambda, v_lru_lambda),
                 final_g=(final_g, m_final_g, v_final_g))
    def rows2d(a):
        return a.reshape(-1, a.shape[-1])

    grads = {n: grads[n].reshape(given[n][0].shape) for n in small_names}
    quads = [tuple(rows2d(a) for a in (given[n][0], grads[n], given[n][1], given[n][2])) for n in small_names]
    updated = _adam_many(quads, "adam_small")
    delta_s, newm_s, newv_s = ({n: u[j].reshape(given[n][0].shape) for n, u in zip(small_names, updated)} for j in range(3))

    big = {"w_ada": (w_ada, g_w_ada, m_w_ada, v_w_ada), "w_in": (w_in, g_in_shard, m_w_in, v_w_in),
           "w_out": (w_out, g_out_shard, m_w_out, v_w_out)}
    delta_b, newm_b, newv_b = {}, {}, {}
    for n, (w, g, m, v) in big.items():
        d_, m_, v_, *echo = _adam(w[0], g, m[0], v[0], "adam_" + n, echo_g=n != "w_ada")
        grads[n] = (echo[0] if echo else g)[None]
        delta_b[n], newm_b[n], newv_b[n] = d_[None], m_[None], v_[None]

    loss = loss_sum[0, 0]
    order = ["c_ctx", "norm_g", "w_ada", "b_ada", "w_in", "w_conv_a", "w_conv_b", "b_conv_b", "lru_wa", "lru_ba",
             "lru_wx", "lru_bx", "lru_lambda", "w_out", "final_g"]
    delta = {**delta_s, **delta_b}
    newm = {**newm_s, **newm_b}
    newv = {**newv_s, **newv_b}
    return (loss, gx[None], *[grads[n] for n in order], *[delta[n] for n in order], *[newm[n] for n in order],
            *[newv[n] for n in order])
```

```python
import jax
import jax.numpy as jnp
from jax import lax
from jax.experimental import pallas as pl
from jax.experimental.pallas import tpu as pltpu

F32 = jnp.float32
BF16 = jnp.bfloat16
MESH_ID = pl.DeviceIdType.MESH

EPS = 1e-6
LRU_C = 8.0
GRID_W = 64
ADAM_LR = 0.001
ADAM_B1 = 0.9
ADAM_B2 = 0.999
ADAM_EPS = 1e-08
ADAM_WD = 0.01
ADAM_STEP = 10

LANES = 128
SUBLANES = 8
PACK_COLS = 1024
VMEM_LIMIT = 56 * 2**20
LRU_GROUP = 256


def _params(sem=None):
    return pltpu.CompilerParams(vmem_limit_bytes=VMEM_LIMIT, dimension_semantics=sem)


def _pick(dim, pref, quantum=LANES):
    if dim <= pref:
        return dim
    best = None
    for t in range(quantum, pref + 1, quantum):
        if dim % t == 0:
            best = t
    assert best is not None, (dim, pref)
    return best


def _pos():
    return lax.axis_index("x"), lax.axis_index("y"), lax.axis_index("c")


def _flip(v, bit):
    return 1 - v if bit else v


def _sigmoid(v):
    return 0.5 * jnp.tanh(0.5 * v) + 0.5


def _silu(v):
    return v * _sigmoid(v)


def _dsilu(v):
    s = _sigmoid(v)
    return s * (1.0 + v * (1.0 - s))


def _gates(pre_r, pre_i, sp):
    r = _sigmoid(pre_r)
    ig = _sigmoid(pre_i)
    e = LRU_C * r * sp
    w = jnp.tanh(e)
    return r, ig, jnp.exp(-e), (2.0 * w) * pl.reciprocal(1.0 + w, approx=True)


def _softplus(z):
    return jnp.maximum(z, 0.0) + jnp.log1p(jnp.exp(-jnp.abs(z)))


def _matmul(a, b, *, ta=False, tb=False, tm=512, tn=512, tk=512, out_dtype=F32, name,
            a_rows=None, a_off=0, a_act=None, bias=None, dsilu_mul=None, side=None):
    rows_a = a.shape[0] if a_rows is None else a_rows
    if ta:
        K, M = rows_a, a.shape[1]
    else:
        M, K = rows_a, a.shape[1]
    N = b.shape[0] if tb else b.shape[1]
    tm, tn, tk = _pick(M, tm, SUBLANES), _pick(N, tn), _pick(K, tk)
    t_rows = tk if ta else tm
    assert a_off % t_rows == 0
    nk = K // tk
    gi, gj = M // tm, N // tn
    off_blocks = a_off // t_rows
    dims = (((0 if ta else 1,), (1 if tb else 0,)), ((), ()))
    extras = [e for e in (bias, dsilu_mul) if e is not None]
    n_sin = len(side["ins"]) if side else 0
    n_sout = len(side["outs"]) if side else 0

    def body(a_ref, b_ref, *rest):
        rest = list(rest)
        bias_ref = rest.pop(0) if bias is not None else None
        dsm_ref = rest.pop(0) if dsilu_mul is not None else None
        side_in = [rest.pop(0) for _ in range(n_sin)]
        o_ref = rest.pop(0)
        side_out = [rest.pop(0) for _ in range(n_sout)]
        acc_ref = rest.pop(0) if nk > 1 else None
        side_scr = rest
        i, j, k = pl.program_id(0), pl.program_id(1), pl.program_id(2)

        if side:
            @pl.when((i == 0) & (j == 0) & (k == 0))
            def _():
                side["start"](side_in, side_out, side_scr)

        av = a_ref[...]
        if a_act == "silu":
            av = _silu(av)
        prod = lax.dot_general(av, b_ref[...], dims, preferred_element_type=F32)

        def finish(r):
            if bias_ref is not None:
                r = r + bias_ref[...]
            if dsm_ref is not None:
                r = r * _dsilu(dsm_ref[...])
            o_ref[...] = r.astype(o_ref.dtype)

        if nk == 1:
            finish(prod)
        else:
            @pl.when(k == 0)
            def _():
                acc_ref[...] = prod

            @pl.when(k > 0)
            def _():
                acc_ref[...] += prod

            @pl.when(k == nk - 1)
            def _():
                finish(acc_ref[...])

        if side:
            @pl.when((i == gi - 1) & (j == gj - 1) & (k == nk - 1))
            def _():
                side["finish"](side_in, side_out, side_scr)

    if ta:
        a_spec = pl.BlockSpec((tk, tm), lambda i, j, k: (k + off_blocks, i))
    else:
        a_spec = pl.BlockSpec((tm, tk), lambda i, j, k: (i + off_blocks, k))
    once = dict(pipeline_mode=pl.Buffered(1)) if (gj == 1 and nk == 1) else {}
    if tb:
        b_spec = pl.BlockSpec((tn, tk), lambda i, j, k: (j, k), **once)
    else:
        b_spec = pl.BlockSpec((tk, tn), lambda i, j, k: (k, j), **once)
    in_specs = [a_spec, b_spec]
    if bias is not None:
        in_specs.append(pl.BlockSpec((1, tn), lambda i, j, k: (0, j)))
    if dsilu_mul is not None:
        in_specs.append(pl.BlockSpec((1, tn), lambda i, j, k: (0, j)))
    hbm = pl.BlockSpec(memory_space=pl.ANY)
    res = pl.pallas_call(
        body, name=name, grid=(gi, gj, nk),
        in_specs=in_specs + [hbm] * n_sin,
        out_specs=[pl.BlockSpec((tm, tn), lambda i, j, k: (i, j))] + [hbm] * n_sout,
        out_shape=[jax.ShapeDtypeStruct((M, N), out_dtype)] + (list(side["outs"]) if side else []),
        scratch_shapes=([pltpu.VMEM((tm, tn), F32)] if nk > 1 else []) + (list(side["scratch"]) if side else []),
        compiler_params=_params(("arbitrary",) * 3 if side else ("parallel", "parallel", "arbitrary")),
    )(a, b, *extras, *(side["ins"] if side else []))
    return (res[0], res[1:]) if side else res[0]


def _elementwise(fn, ins, outs, *, rows, cols, name, tr=256):
    tr = _pick(rows, tr, 2 * SUBLANES)
    n_in = len(ins)

    def body(*refs):
        vals = fn(*[r[...] for r in refs[:n_in]])
        if not isinstance(vals, (tuple, list)):
            vals = (vals,)
        for r, v in zip(refs[n_in:], vals, strict=True):
            r[...] = v.astype(r.dtype)

    def spec(off):
        assert off % tr == 0
        ob = off // tr
        return pl.BlockSpec((tr, cols), lambda i: (i + ob, 0))

    res = pl.pallas_call(
        body, name=name, grid=(rows // tr,),
        in_specs=[spec(off) for _, off in ins],
        out_specs=[spec(0) for _ in outs],
        out_shape=[jax.ShapeDtypeStruct((rows, cols), dt) for dt in outs],
        compiler_params=_params(("parallel",)),
    )(*[a for a, _ in ins])
    return res


def _adam_math(w, g, m, v):
    m = ADAM_B1 * m + (1.0 - ADAM_B1) * g
    v = ADAM_B2 * v + (1.0 - ADAM_B2) * (g * g)
    m_hat = m / (1.0 - ADAM_B1 ** ADAM_STEP)
    v_hat = v / (1.0 - ADAM_B2 ** ADAM_STEP)
    delta = -ADAM_LR * (m_hat / (jnp.sqrt(v_hat) + ADAM_EPS) + ADAM_WD * w)
    return delta, m, v


def _adam(w, g, m, v, name, echo_g=False):
    rows, cols = w.shape
    fn = (lambda w_, g_, m_, v_: _adam_math(w_, g_, m_, v_) + (g_,)) if echo_g else _adam_math
    return _elementwise(fn, [(w, 0), (g, 0), (m, 0), (v, 0)], [F32] * (4 if echo_g else 3),
                        rows=rows, cols=cols, name=name)


def _adam_many(quads, name):
    n = len(quads)

    def body(*refs):
        ins, outs = refs[:4 * n], refs[4 * n:]
        for t in range(n):
            w, g, m, v = (r[...] for r in ins[4 * t:4 * t + 4])
            for o_ref, val in zip(outs[3 * t:3 * t + 3], _adam_math(w, g, m, v), strict=True):
                o_ref[...] = val

    res = pl.pallas_call(
        body, name=name,
        out_shape=[jax.ShapeDtypeStruct(q[0].shape, F32) for q in quads for _ in range(3)],
        compiler_params=_params(),
    )(*[a for q in quads for a in q])
    return [res[3 * t:3 * t + 3] for t in range(n)]


def _pack(arrs, row_quantum):
    flat = jnp.concatenate([a.reshape(-1).astype(F32) for a in arrs])
    n = flat.shape[0]
    q = row_quantum * PACK_COLS
    total = -(-n // q) * q
    flat = jnp.pad(flat, (0, total - n))
    return flat.reshape(total // PACK_COLS, PACK_COLS)


def _unpack(buf, shapes):
    flat = buf.reshape(-1)
    out, off = [], 0
    for s in shapes:
        n = 1
        for d in s:
            n *= d
        out.append(flat[off:off + n].reshape(s))
        off += n
    return out


def _allreduce8(buf, name):
    R, C = buf.shape
    assert R % (8 * SUBLANES) == 0
    m = R // 8

    def body(x_ref, o_ref, recv, red, s1, r1, s2, r2):
        x, y, c = _pos()
        me = 4 * x + 2 * y + c

        def peer(k):
            px, py, pc = _flip(x, (k >> 2) & 1), _flip(y, (k >> 1) & 1), _flip(c, k & 1)
            return (px, py, pc), 4 * px + 2 * py + pc

        def rows(ref, idx):
            return ref.at[pl.ds(pl.multiple_of(idx * m, SUBLANES), m), :]

        def scatter(k):
            dev, p = peer(k)
            return pltpu.make_async_remote_copy(src_ref=rows(x_ref, p), dst_ref=recv.at[k], send_sem=s1.at[k],
                                                recv_sem=r1.at[k], device_id=dev, device_id_type=MESH_ID)

        def share(k):
            dev, p = peer(k)
            return pltpu.make_async_remote_copy(src_ref=red, dst_ref=rows(o_ref, me), send_sem=s2.at[k],
                                                recv_sem=r2.at[k], device_id=dev, device_id_type=MESH_ID)

        def shared_from(k):
            dev, p = peer(k)
            return pltpu.make_async_remote_copy(src_ref=red, dst_ref=rows(o_ref, p), send_sem=s2.at[k],
                                                recv_sem=r2.at[k], device_id=dev, device_id_type=MESH_ID)

        for k in range(1, 8):
            scatter(k).start()
        acc = rows(x_ref, me)[...]
        for k in range(1, 8):
            scatter(k).wait_recv()
            acc = acc + recv[k]
        red[...] = acc
        rows(o_ref, me)[...] = acc
        for k in range(1, 8):
            share(k).start()
        for k in range(1, 8):
            shared_from(k).wait_recv()
        for k in range(1, 8):
            scatter(k).wait_send()
            share(k).wait_send()

    return pl.pallas_call(
        body, name=name,
        in_specs=[pl.BlockSpec(memory_space=pltpu.VMEM)],
        out_specs=pl.BlockSpec(memory_space=pltpu.VMEM),
        out_shape=jax.ShapeDtypeStruct((R, C), F32),
        scratch_shapes=[pltpu.VMEM((8, m, C), F32), pltpu.VMEM((m, C), F32),
                        pltpu.SemaphoreType.DMA((8,)), pltpu.SemaphoreType.DMA((8,)),
                        pltpu.SemaphoreType.DMA((8,)), pltpu.SemaphoreType.DMA((8,))],
        compiler_params=_params(),
    )(buf)


def _allreduce8_two_level(buf, name):
    R, C = buf.shape
    assert R % (8 * SUBLANES) == 0
    m, hr = R // 8, R // 2

    def body(x_ref, o_ref, got0, half, got1, red, ssem, rsem):
        x, y, c = _pos()
        q = 2 * x + y
        sibling = (x, y, 1 - c)

        def half_rows(ref, core):
            return ref.at[pl.ds(pl.multiple_of(core * hr, SUBLANES), hr), :]

        def chunk(ref, core, chip):
            return ref.at[pl.ds(pl.multiple_of(core * hr + chip * m, SUBLANES), m), :]

        def chip_of(k):
            px, py = _flip(x, (k >> 1) & 1), _flip(y, k & 1)
            return (px, py, c), 2 * px + py

        def copy(src, dst, phase, k, dev):
            return pltpu.make_async_remote_copy(src_ref=src, dst_ref=dst, send_sem=ssem.at[phase, k],
                                                recv_sem=rsem.at[phase, k], device_id=dev, device_id_type=MESH_ID)

        swap = copy(half_rows(x_ref, 1 - c), got0, 0, 0, sibling)
        swap.start()
        swap.wait()
        half[...] = half_rows(x_ref, c)[...] + got0[...]

        def scatter(k):
            dev, p = chip_of(k)
            return copy(half.at[pl.ds(pl.multiple_of(p * m, SUBLANES), m), :], got1.at[k], 1, k, dev)

        for k in range(1, 4):
            scatter(k).start()
        acc = half[pl.ds(pl.multiple_of(q * m, SUBLANES), m), :]
        for k in range(1, 4):
            scatter(k).wait_recv()
            acc = acc + got1[k]
        red[...] = acc
        chunk(o_ref, c, q)[...] = acc

        def share(k, landing_chip):
            return copy(red, chunk(o_ref, c, landing_chip), 2, k, chip_of(k)[0])

        for k in range(1, 4):
            share(k, q).start()
        for k in range(1, 4):
            share(k, chip_of(k)[1]).wait_recv()
        back = copy(half_rows(o_ref, c), half_rows(o_ref, c), 3, 0, sibling)
        back.start()
        copy(half_rows(o_ref, 1 - c), half_rows(o_ref, 1 - c), 3, 0, sibling).wait_recv()
        back.wait_send()
        for k in range(1, 4):
            scatter(k).wait_send()
            share(k, q).wait_send()

    return pl.pallas_call(
        body, name=name,
        in_specs=[pl.BlockSpec(memory_space=pltpu.VMEM)],
        out_specs=pl.BlockSpec(memory_space=pltpu.VMEM),
        out_shape=jax.ShapeDtypeStruct((R, C), F32),
        scratch_shapes=[pltpu.VMEM((hr, C), F32), pltpu.VMEM((hr, C), F32), pltpu.VMEM((4, m, C), F32), pltpu.VMEM((m, C), F32),
                        pltpu.SemaphoreType.DMA((4, 4)), pltpu.SemaphoreType.DMA((4, 4))],
        compiler_params=_params(),
    )(buf)


def _bounce(src, dst, buf, sem):
    cin = pltpu.make_async_copy(src, buf, sem)
    cin.start()
    cin.wait()
    cout = pltpu.make_async_copy(buf, dst, sem)
    cout.start()
    cout.wait()


def _chunk(ref, axis, idx, size):
    start = idx * size
    if axis == 0:
        return ref.at[pl.ds(start, size), :]
    return ref.at[:, pl.ds(start, size)]


def _in_proj_gather(hl, win, wout, q_arr, *, rows, tm):
    D, nq = win.shape
    dq, D2 = wout.shape
    ni = rows // tm
    ops = ((0, 1, nq, D // 2), (1, 0, dq, dq // 2))

    def body(q_ref, a_ref, win_ref, wout_ref, p_ref, gin_ref, gout_ref, b_scr, buf_out, lsem, ssem, rsem, fsem, gsem):
        j, i = pl.program_id(0), pl.program_id(1)
        x, y, c = _pos()
        q = 2 * x + y
        srcs = (win_ref, wout_ref)
        dsts = (gin_ref, gout_ref)

        def shard_window(o, chip):
            _, axis, size, _ = ops[o]
            return _chunk(dsts[o], axis, chip, size)

        def half(ref, o, core):
            return ref.at[pl.ds(core * ops[o][3], ops[o][3]), :]

        def half_window(o, chip, core):
            _, axis, size, hs = ops[o]
            if axis == 1:
                return dsts[o].at[pl.ds(core * hs, hs), pl.ds(chip * size, size)]
            return dsts[o].at[pl.ds(chip * size + core * hs, hs), :]

        def chip_of(k):
            px, py = _flip(x, (k >> 1) & 1), _flip(y, k & 1)
            return px, py, 2 * px + py

        def send(o, k):
            px, py, _ = chip_of(k)
            return pltpu.make_async_remote_copy(
                src_ref=half(srcs[o], o, c), dst_ref=half_window(o, q, c), send_sem=ssem.at[o, k],
                recv_sem=rsem.at[o, k], device_id=(px, py, c), device_id_type=MESH_ID)

        def chip_recv(o, k):
            px, py, pq = chip_of(k)
            landed = half_window(o, pq, c)
            pltpu.make_async_remote_copy(src_ref=landed, dst_ref=landed, send_sem=ssem.at[o, k], recv_sem=rsem.at[o, k],
                                         device_id=(px, py, c), device_id_type=MESH_ID).wait_recv()

        def to_sibling(o, k):
            landed = half_window(o, chip_of(k)[2], c)
            return pltpu.make_async_remote_copy(src_ref=landed, dst_ref=landed, send_sem=fsem.at[o, k],
                                                recv_sem=gsem.at[o, k], device_id=(x, y, 1 - c), device_id_type=MESH_ID)

        def from_sibling(o, k):
            theirs = half_window(o, chip_of(k)[2], 1 - c)
            pltpu.make_async_remote_copy(src_ref=theirs, dst_ref=theirs, send_sem=fsem.at[o, k], recv_sem=gsem.at[o, k],
                                         device_id=(x, y, 1 - c), device_id_type=MESH_ID).wait_recv()

        def relay(o, core):
            if core == 0:
                landed, target = half_window(o, chip_of(2)[2], 0), (x, 1 - y, 0)
            else:
                landed, target = half_window(o, chip_of(1)[2], 1), (1 - x, y, 1)
            return pltpu.make_async_remote_copy(src_ref=landed, dst_ref=landed, send_sem=ssem.at[o, 3],
                                                recv_sem=rsem.at[o, 3], device_id=target, device_id_type=MESH_ID)

        def on_core(core, fn):
            @pl.when(c == core)
            def _():
                fn()

        def land(o, k):
            chip_recv(o, k)
            if k == 2:
                on_core(0, lambda: relay(o, 0).start())
            if k == 1:
                on_core(1, lambda: relay(o, 1).start())
            to_sibling(o, k).start()

        def settle(o, k):
            from_sibling(o, k)
            to_sibling(o, k).wait_send()

        def b_load(k, slot):
            src = win_ref if k == 0 else shard_window(0, chip_of(k)[2])
            return pltpu.make_async_copy(src, b_scr.at[slot], lsem.at[0])

        def own_store():
            return pltpu.make_async_copy(b_scr.at[0], shard_window(0, q), lsem.at[2])

        order = (0, 2, 1, 3)
        early = max(ni - 2, 0)

        @pl.when((j == 0) & (i == 0))
        def _():
            for o in range(2):
                for k in (2, 1):
                    send(o, k).start()
            first = b_load(0, 0)
            first.start()
            first.wait()
            own_store().start()
            _bounce(wout_ref, shard_window(1, q), buf_out, lsem.at[1])

        for jj in range(3):
            nxt = order[jj + 1]

            @pl.when((j == jj) & (i == early))
            def _(nxt=nxt):
                land(0, nxt)

            @pl.when((j == jj) & (i == ni - 1))
            def _(jj=jj, nxt=nxt):
                settle(0, nxt)
                if jj == 1:
                    own_store().wait()
                b_load(nxt, (jj + 1) % 2).start()

            @pl.when((j == jj + 1) & (i == 0))
            def _(jj=jj, nxt=nxt):
                b_load(nxt, (jj + 1) % 2).wait()

        @pl.when((j == 3) & (i == 0))
        def _():
            land(1, 2)
            land(1, 1)

        p_ref[...] = jnp.dot(a_ref[...], b_scr[j % 2], preferred_element_type=F32).astype(p_ref.dtype)

        @pl.when((j == 3) & (i == ni - 1))
        def _():
            settle(1, 2)
            settle(1, 1)
            land(1, 3)
            settle(1, 3)
            for o in range(2):
                for k in (2, 1):
                    send(o, k).wait_send()
                for core in range(2):
                    on_core(core, lambda o=o, core=core: relay(o, core).wait_send())

    hbm = pl.BlockSpec(memory_space=pl.ANY)
    grid_spec = pltpu.PrefetchScalarGridSpec(
        num_scalar_prefetch=1, grid=(4, ni),
        in_specs=[pl.BlockSpec((tm, D), lambda j, i, qr: (i, 0)), hbm, hbm],
        out_specs=[pl.BlockSpec((tm, nq), lambda j, i, qr: (i, jnp.bitwise_xor(qr[0], ((j & 1) << 1) | (j >> 1)))),
                   hbm, hbm],
        scratch_shapes=[pltpu.VMEM((2,) + win.shape, win.dtype), pltpu.VMEM(wout.shape, wout.dtype), pltpu.SemaphoreType.DMA((3,))]
        + [pltpu.SemaphoreType.DMA((2, 4)) for _ in range(4)])
    return pl.pallas_call(
        body, name="in_proj_gather", grid_spec=grid_spec,
        out_shape=[jax.ShapeDtypeStruct((rows, 4 * nq), BF16), jax.ShapeDtypeStruct((D, 4 * nq), win.dtype),
                   jax.ShapeDtypeStruct((4 * dq, D2), wout.dtype)],
        compiler_params=_params(("arbitrary", "arbitrary")),
    )(q_arr, hl, win, wout)


def _rs_to_sibling(gs, axes):
    n = len(gs)
    shapes = []
    for g, ax in zip(gs, axes):
        s = list(g.shape)
        s[ax] //= 8
        shapes.append(tuple(s))

    def body(*refs):
        g_refs, mine, landed = refs[:n], refs[n:2 * n], refs[2 * n:3 * n]
        bufs = refs[3 * n:4 * n]
        lsem, ssem, rsem = refs[4 * n:]
        x, y, c = _pos()
        cps = []
        for o in range(n):
            size = shapes[o][axes[o]]
            for j in range(4):
                rc = pltpu.make_async_remote_copy(
                    src_ref=_chunk(g_refs[o], axes[o], 2 * j + 1 - c, size), dst_ref=landed[o].at[j],
                    send_sem=ssem.at[o, j], recv_sem=rsem.at[o, j], device_id=(x, y, 1 - c), device_id_type=MESH_ID)
                rc.start()
                cps.append(rc)
        for o in range(n):
            size = shapes[o][axes[o]]
            for j in range(4):
                _bounce(_chunk(g_refs[o], axes[o], 2 * j + c, size), mine[o].at[j], bufs[o], lsem.at[o])
        for rc in cps:
            rc.wait()

    hbm = pl.BlockSpec(memory_space=pl.ANY)
    outs = [jax.ShapeDtypeStruct((4,) + s, g.dtype) for s, g in zip(shapes, gs)]
    res = pl.pallas_call(
        body, name="rs_to_sibling", in_specs=[hbm] * n, out_specs=[hbm] * (2 * n), out_shape=outs + outs,
        scratch_shapes=[pltpu.VMEM(s, g.dtype) for s, g in zip(shapes, gs)]
        + [pltpu.SemaphoreType.DMA((n,)), pltpu.SemaphoreType.DMA((n, 4)), pltpu.SemaphoreType.DMA((n, 4))],
        compiler_params=_params(),
    )(*gs)
    return res[:n], res[n:]


def _gather_block_side(block):
    r, n_cols = block.shape

    def copies(ins, outs, scr):
        ssem, rsem = scr[1], scr[2]
        x, y, c = _pos()
        return [pltpu.make_async_remote_copy(
            src_ref=ins[0], dst_ref=outs[0].at[k], send_sem=ssem.at[k], recv_sem=rsem.at[k],
            device_id=(_flip(x, (k >> 2) & 1), _flip(y, (k >> 1) & 1), _flip(c, k & 1)), device_id_type=MESH_ID)
            for k in range(1, 8)]

    def start(ins, outs, scr):
        for cp in copies(ins, outs, scr):
            cp.start()

    def finish(ins, outs, scr):
        _bounce(ins[0], outs[0].at[0], scr[0], scr[3].at[0])
        for cp in copies(ins, outs, scr):
            cp.wait()

    return dict(ins=[block], outs=[jax.ShapeDtypeStruct((8, r, n_cols), block.dtype)],
                scratch=[pltpu.VMEM((r, n_cols), block.dtype), pltpu.SemaphoreType.DMA((8,)), pltpu.SemaphoreType.DMA((8,)),
                         pltpu.SemaphoreType.DMA((1,))],
                start=start, finish=finish)


def _join_sides(a, b):
    na_i, na_o, na_s = len(a["ins"]), len(a["outs"]), len(a["scratch"])

    def run(which):
        def fn(ins, outs, scr):
            a[which](ins[:na_i], outs[:na_o], scr[:na_s])
            b[which](ins[na_i:], outs[na_o:], scr[na_s:])
        return fn

    return dict(ins=a["ins"] + b["ins"], outs=a["outs"] + b["outs"], scratch=a["scratch"] + b["scratch"],
                start=run("start"), finish=run("finish"))


def _rs_chips_side(parts):
    n = len(parts)

    def copies(p_refs, slots, scr):
        ssem, rsem = scr[n + 1], scr[n + 2]
        x, y, c = _pos()
        cps = []
        for o in range(n):
            for k in range(1, 4):
                px, py = _flip(x, (k >> 1) & 1), _flip(y, k & 1)
                cps.append(pltpu.make_async_remote_copy(
                    src_ref=p_refs[o].at[2 * px + py], dst_ref=slots[o].at[k], send_sem=ssem.at[o, k],
                    recv_sem=rsem.at[o, k], device_id=(px, py, c), device_id_type=MESH_ID))
        return cps

    def start(p_refs, slots, scr):
        for cp in copies(p_refs, slots, scr):
            cp.start()

    def finish(p_refs, slots, scr):
        x, y, _ = _pos()
        q = 2 * x + y
        for o in range(n):
            _bounce(p_refs[o].at[q], slots[o].at[0], scr[o], scr[n].at[o])
        for cp in copies(p_refs, slots, scr):
            cp.wait()

    return dict(
        ins=list(parts), outs=[jax.ShapeDtypeStruct(p.shape, p.dtype) for p in parts],
        scratch=[pltpu.VMEM(p.shape[1:], p.dtype) for p in parts]
        + [pltpu.SemaphoreType.DMA((n,)), pltpu.SemaphoreType.DMA((n, 4)), pltpu.SemaphoreType.DMA((n, 4))],
        start=start, finish=finish)


def _rs_share(rs, axes):
    n = len(rs)
    shapes = []
    for r, ax in zip(rs, axes):
        s = list(r.shape)
        s[ax] *= 2
        shapes.append(tuple(s))

    def body(*refs):
        r_refs, outs = refs[:n], refs[n:2 * n]
        bufs = refs[2 * n:3 * n]
        lsem, ssem, rsem = refs[3 * n:]
        x, y, c = _pos()
        cps = []
        for o in range(n):
            size = r_refs[o].shape[axes[o]]
            window = _chunk(outs[o], axes[o], c, size)
            rc = pltpu.make_async_remote_copy(src_ref=r_refs[o], dst_ref=window, send_sem=ssem.at[o], recv_sem=rsem.at[o],
                                              device_id=(x, y, 1 - c), device_id_type=MESH_ID)
            rc.start()
            cps.append(rc)
        for o in range(n):
            size = r_refs[o].shape[axes[o]]
            _bounce(r_refs[o], _chunk(outs[o], axes[o], c, size), bufs[o], lsem.at[o])
        for cp in cps:
            cp.wait()

    hbm = pl.BlockSpec(memory_space=pl.ANY)
    return pl.pallas_call(
        body, name="rs_share", in_specs=[hbm] * n, out_specs=[hbm] * n,
        out_shape=[jax.ShapeDtypeStruct(s, r.dtype) for s, r in zip(shapes, rs)],
        scratch_shapes=[pltpu.VMEM(r.shape, r.dtype) for r in rs] + [pltpu.SemaphoreType.DMA((n,)) for _ in range(3)],
        compiler_params=_params(),
    )(*rs)


def _rs_pair_sums(gs, axes):
    mine, landed = _rs_to_sibling(gs, axes)
    pair_sums = []
    for o, (mi, la) in enumerate(zip(mine, landed)):
        rows, cols = mi.shape[0] * mi.shape[1], mi.shape[2]
        s = _elementwise(lambda a, b: a.astype(F32) + b.astype(F32), [(mi.reshape(rows, cols), 0), (la.reshape(rows, cols), 0)],
                         [BF16], rows=rows, cols=cols, name=f"rs_pair_sum{o}")[0]
        pair_sums.append(s.reshape(mi.shape))
    return pair_sums


def _rs_finish(slots, axes):
    reduced = []
    for o, sl in enumerate(slots):
        rows, cols = sl.shape[1], sl.shape[2]
        flat = sl.reshape(4 * rows, cols)
        r = _elementwise(lambda a, b, c, d: (a.astype(F32) + b.astype(F32)) + (c.astype(F32) + d.astype(F32)),
                         [(flat, k * rows) for k in range(4)], [F32], rows=rows, cols=cols, name=f"rs_chip_sum{o}")[0]
        reduced.append(r)
    return _rs_share(reduced, axes)


def _norm_in(x, ctx, g, sc_l, sh_l, sc_c, sh_c, tr):
    L, D = x.shape
    T = ctx.shape[0]
    nx, nc = L // tr, T // tr

    def body(x_ref, c_ref, g_ref, scl, shl, scc, shc, o_ref):
        i = pl.program_id(0)

        def run(src, sc, sh):
            v = src[...]
            r = lax.rsqrt(jnp.mean(v * v, axis=-1, keepdims=True) + EPS)
            o_ref[...] = ((v * r * g_ref[...]) * (1.0 + sc[...]) + sh[...]).astype(o_ref.dtype)

        @pl.when(i < nx)
        def _():
            run(x_ref, scl, shl)

        @pl.when(i >= nx)
        def _():
            run(c_ref, scc, shc)

    vec = pl.BlockSpec((1, D), lambda i: (0, 0))
    return pl.pallas_call(
        body, name="norm_in", grid=(nx + nc,),
        in_specs=[pl.BlockSpec((tr, D), lambda i: (jnp.minimum(i, nx - 1), 0)),
                  pl.BlockSpec((tr, D), lambda i: (jnp.maximum(i - nx, 0), 0)), vec, vec, vec, vec, vec],
        out_specs=pl.BlockSpec((tr, D), lambda i: (i, 0)),
        out_shape=jax.ShapeDtypeStruct((L + T, D), BF16),
        compiler_params=_params(("arbitrary",)),
    )(x, ctx, g, sc_l, sh_l, sc_c, sh_c)


def _tmod(tl, row_w):
    assert row_w & (row_w - 1) == 0
    return lax.broadcasted_iota(jnp.int32, (tl, 1), 0) & (row_w - 1)


def _shift(z, k, tmod, row_w):
    tl = z.shape[0]
    rolled = pltpu.roll(z, k % tl, 0)
    mask = (tmod >= k) if k > 0 else (tmod < row_w + k)
    return jnp.where(mask, rolled, 0.0)


def _conv(z, w_ref, taps, left, tmod, row_w, lanes=slice(None)):
    out = None
    for j in range(taps):
        k = left - j
        term = (z if k == 0 else _shift(z, k, tmod, row_w)) * w_ref[j:j + 1, lanes]
        out = term if out is None else out + term
    return out


def _conv_bwd(dz, z, w_ref, taps, left, tmod, row_w, lanes=slice(None)):
    din = None
    dws = []
    for j in range(taps):
        k = left - j
        shifted = dz if k == 0 else _shift(dz, -k, tmod, row_w)
        term = shifted * w_ref[j:j + 1, lanes]
        din = term if din is None else din + term
        dws.append(jnp.sum(shifted * z, axis=0, keepdims=True))
    return din, dws


def _gate_matmul(xb16_ref, wd_ref, pre_scr, W, ng, gs):
    for g in range(ng):
        pg = jnp.dot(xb16_ref[:, g * gs:(g + 1) * gs], wd_ref[g], preferred_element_type=F32)
        pre_scr[:, g * gs:(g + 1) * gs] = pg[:, :gs]
        pre_scr[:, W + g * gs:W + (g + 1) * gs] = pg[:, gs:]


def _f32(ref, rows, lanes):
    return ref[rows, lanes].astype(F32)


def _sub_loop(tl, sub_r, W, fn):
    def chunk(ci, carry):
        r0 = pl.multiple_of(ci * sub_r, sub_r)
        for lb in range(W // LANES):
            fn(r0, lb * LANES)
        return carry

    lax.fori_loop(0, tl // sub_r, chunk, 0)


def _row_loop(tl, rev, step, init):
    nchunk = tl // SUBLANES

    def chunk(j, carry):
        jj = (nchunk - 1 - j) if rev else j
        c0 = pl.multiple_of(jj * SUBLANES, SUBLANES)
        for r in (range(SUBLANES - 1, -1, -1) if rev else range(SUBLANES)):
            carry = step(c0 + r, carry)
        return carry

    return lax.fori_loop(0, nchunk, chunk, init)


def _mix_fwd(P, d, h_init, wts, *, rows, row_off, row_w, tl, saved0=None, name):
    W = P.shape[1] // 6
    nt = rows // tl
    ob = row_off // tl
    rev = d == 1
    gs = min(LRU_GROUP, W)
    ng = W // gs
    wca, wcb, bcb = wts["wca"], wts["wcb"], wts["bcb"]
    wd, ba, bx, lam = wts["wd"][d], wts["ba"][d], wts["bx"][d], wts["lam"][d]

    def tile(i):
        return (nt - 1 - i) if rev else i

    def pcol(j):
        return pl.BlockSpec((tl, W), lambda i: (tile(i) + ob, j))

    vec = pl.BlockSpec((1, W), lambda i: (0, 0))
    taps = pl.BlockSpec((SUBLANES, W), lambda i: (0, 0))
    wd_spec = pl.BlockSpec(wd.shape, lambda i: (0, 0, 0))
    seq = pl.BlockSpec((tl, W), lambda i: (tile(i), 0))

    sub_r = min(row_w, tl)
    assert tl % sub_r == 0

    def body(*refs):
        if rev:
            (bl, cl, ul, gl, ql, ho, xb_r, xb16_r, wca_r, wd_r, ba_r, bx_r, lam_r, hin, hseq, cat, a_o, r_o, ig_o, m2_o,
             b_scr, pre_scr, carry, sp_scr) = refs
        else:
            (vl, wcb_r, bcb_r, wd_r, ba_r, bx_r, lam_r, hin, hseq, xb_r, xb16_r, a_o, r_o, ig_o, m2_o,
             b_scr, pre_scr, carry, sp_scr) = refs
        i = pl.program_id(0)

        @pl.when(i == 0)
        def _():
            carry[...] = hin[...]

        sp_scr[...] = _softplus(-lam_r[...])
        tmod = _tmod(sub_r, sub_r)

        def conv_in(r0, l0):
            rs, ls = pl.ds(r0, sub_r), pl.ds(l0, LANES)
            xb = _conv(_f32(vl, rs, ls), wcb_r, 4, 2, tmod, sub_r, ls) + bcb_r[:, ls]
            xb_r[rs, ls] = xb
            xb16_r[rs, ls] = xb.astype(BF16)

        def gates(r0, l0):
            rs, ls = pl.ds(r0, sub_r), pl.ds(l0, LANES)
            r, ig, a, m2 = _gates(pre_scr[rs, ls] + ba_r[:, ls], pre_scr[rs, pl.ds(W + l0, LANES)] + bx_r[:, ls],
                                  sp_scr[:, ls])
            a_o[rs, ls] = a
            r_o[rs, ls] = r.astype(r_o.dtype)
            ig_o[rs, ls] = ig.astype(ig_o.dtype)
            m2_o[rs, ls] = m2.astype(m2_o.dtype)
            m = jnp.where(m2 > 0.0, m2 * lax.rsqrt(m2), 0.0)
            b_scr[rs, ls] = m * (ig * xb_r[rs, ls])

        if not rev:
            _sub_loop(tl, sub_r, W, conv_in)
        _gate_matmul(xb16_r, wd_r, pre_scr, W, ng, gs)
        _sub_loop(tl, sub_r, W, gates)

        def step(t, h):
            h = a_o[pl.ds(t, 1), :] * h + b_scr[pl.ds(t, 1), :]
            hseq[pl.ds(t, 1), :] = h
            return h

        carry[...] = _row_loop(tl, rev, step, carry[...])

        if rev:
            def mix_out(r0, l0):
                rs, ls = pl.ds(r0, sub_r), pl.ds(l0, LANES)
                yb = (ho[rs, ls] + hseq[rs, ls]) * _silu(_f32(ql, rs, ls))
                ya = (_f32(bl, rs, ls) * _conv(_f32(cl, rs, ls) * _f32(ul, rs, ls), wca_r, 3, 1, tmod, sub_r, ls)
                      * _silu(_f32(gl, rs, ls)))
                cat[rs, ls] = ya.astype(cat.dtype)
                cat[rs, pl.ds(W + l0, LANES)] = yb.astype(cat.dtype)

            _sub_loop(tl, sub_r, W, mix_out)

    scratch = [pltpu.VMEM((tl, W), F32), pltpu.VMEM((tl, 2 * W), F32), pltpu.VMEM((1, W), F32), pltpu.VMEM((1, W), F32)]
    f32_seq = jax.ShapeDtypeStruct((rows, W), F32)
    kept_gates = [f32_seq] + [jax.ShapeDtypeStruct((rows, W), BF16)] * 3
    if rev:
        in_specs = [pcol(j) for j in (0, 1, 2, 3, 5)] + [seq, seq, seq, taps, wd_spec, vec, vec, vec, vec]
        args = [P] * 5 + [saved0["h"], saved0["xb"], saved0["xb16"], wca, wd, ba, bx, lam, h_init]
        out_specs = [seq, pl.BlockSpec((tl, 2 * W), lambda i: (tile(i), 0))] + [seq] * 4
        out_shape = [f32_seq, jax.ShapeDtypeStruct((rows, 2 * W), BF16)] + kept_gates
    else:
        in_specs = [pcol(4), taps, vec, wd_spec, vec, vec, vec, vec]
        args = [P, wcb, bcb, wd, ba, bx, lam, h_init]
        out_specs = [seq] * 7
        out_shape = [f32_seq, f32_seq, jax.ShapeDtypeStruct((rows, W), BF16)] + kept_gates
    res = pl.pallas_call(
        body, name=name, grid=(nt,), in_specs=in_specs, out_specs=out_specs, out_shape=out_shape,
        scratch_shapes=scratch, compiler_params=_params(("arbitrary",)),
    )(*args)
    gates = dict(zip(("a", "r", "ig", "m2"), res[-4:]))
    if rev:
        return res[0], res[1], gates
    return dict(h=res[0], xb=res[1], xb16=res[2]), gates


_BWD_SCRATCH = ("dyl", "g", "dp16", "sp", "dlf", "edge", "c")
_FWD_SAVED = ("xb", "xb16", "a", "r", "ig", "m2")


def _bwd_scratch(tl, W):
    shapes = {"dyl": pltpu.VMEM((tl, W), F32), "g": pltpu.VMEM((tl, W), F32), "dp16": pltpu.VMEM((tl, 2 * W), BF16),
              "sp": pltpu.VMEM((1, W), F32), "dlf": pltpu.VMEM((1, W), F32), "edge": pltpu.VMEM((1, W), F32),
              "c": pltpu.VMEM((1, W), F32)}
    return [shapes[n] for n in _BWD_SCRATCH]


def _lru_bwd_tile(d, dy_fn, hs_ref, wd_r, lam_r, scr, acc, first, tl, sub_r, ng, gs):
    dwd_ref, dba_ref, dbx_ref, dlam_ref = acc
    W = hs_ref.shape[1]
    assert gs % LANES == 0
    rev = d == 0
    lam = lam_r[...]
    scr["sp"][...] = _softplus(-lam)
    scr["dlf"][...] = -_sigmoid(-lam)

    @pl.when(first)
    def _():
        dwd_ref[...] = jnp.zeros_like(dwd_ref)
        dba_ref[...] = jnp.zeros_like(dba_ref)
        dbx_ref[...] = jnp.zeros_like(dbx_ref)
        dlam_ref[...] = jnp.zeros_like(dlam_ref)

    def state_grad(r0, l0):
        rs, ls = pl.ds(r0, sub_r), pl.ds(l0, LANES)
        scr["dyl"][rs, ls] = dy_fn(rs, ls)

    _sub_loop(tl, sub_r, W, state_grad)

    def step(t, c):
        g = scr["dyl"][pl.ds(t, 1), :] + c
        scr["g"][pl.ds(t, 1), :] = g
        return scr["a"][pl.ds(t, 1), :] * g

    scr["c"][...] = _row_loop(tl, rev, step, scr["c"][...])
    row = lax.broadcasted_iota(jnp.int32, (sub_r, 1), 0)

    def grads(r0, l0):
        rs, ls = pl.ds(r0, sub_r), pl.ds(l0, LANES)
        g, a, m2 = scr["g"][rs, ls], scr["a"][rs, ls], _f32(scr["m2"], rs, ls)
        r, ig, xb = _f32(scr["r"], rs, ls), _f32(scr["ig"], rs, ls), scr["xb"][rs, ls]
        h = hs_ref[rs, ls]
        if d == 0:
            e0 = pl.multiple_of(jnp.maximum(r0 - SUBLANES, 0), SUBLANES)
            edge = jnp.where(r0 == 0, scr["edge"][:, ls], hs_ref[pl.ds(e0, SUBLANES), ls][SUBLANES - 1:, :])
            hprev = jnp.where(row == 0, edge, pltpu.roll(h, 1, 0))
        else:
            e0 = pl.multiple_of(jnp.minimum(r0 + sub_r, tl - SUBLANES), SUBLANES)
            edge = jnp.where(r0 == tl - sub_r, scr["edge"][:, ls], hs_ref[pl.ds(e0, SUBLANES), ls][:1, :])
            hprev = jnp.where(row == sub_r - 1, edge, pltpu.roll(h, sub_r - 1, 0))
        rsq = lax.rsqrt(m2)
        gm = g * (m2 * rsq)
        d_la = (g * hprev) * a - (g * (ig * xb)) * ((1.0 - m2) * rsq)
        d_pr = d_la * ((-LRU_C) * scr["sp"][:, ls]) * (r * (1.0 - r))
        d_pi = (gm * xb) * (ig * (1.0 - ig))
        scr["dyl"][rs, ls] = gm * ig
        dlam_ref[:, ls] += jnp.sum(d_la * ((-LRU_C) * r), axis=0, keepdims=True) * scr["dlf"][:, ls]
        dba_ref[:, ls] += jnp.sum(d_pr, axis=0, keepdims=True)
        dbx_ref[:, ls] += jnp.sum(d_pi, axis=0, keepdims=True)
        gi, off = divmod(l0, gs)
        scr["dp16"][rs, pl.ds(gi * 2 * gs + off, LANES)] = d_pr.astype(BF16)
        scr["dp16"][rs, pl.ds(gi * 2 * gs + gs + off, LANES)] = d_pi.astype(BF16)

    _sub_loop(tl, sub_r, W, grads)
    for gi in range(ng):
        dp = scr["dp16"][:, gi * 2 * gs:(gi + 1) * 2 * gs]
        scr["g"][:, gi * gs:(gi + 1) * gs] = lax.dot_general(dp, wd_r[gi], (((1,), (1,)), ((), ())),
                                                             preferred_element_type=F32)
        dwd_ref[gi] += lax.dot_general(scr["xb16"][:, gi * gs:(gi + 1) * gs], dp, (((0,), (0,)), ((), ())),
                                       preferred_element_type=F32)


def _edge_block(h, tl, nt, d):
    W = h.shape[1]
    per = tl // SUBLANES
    if d == 0:
        return pl.BlockSpec((SUBLANES, W), lambda i: (jnp.maximum((nt - 1 - i) * per - 1, 0), 0))
    return pl.BlockSpec((SUBLANES, W), lambda i: (jnp.minimum((i + 1) * per, nt * per - 1), 0))


def _mix_bwd0(P, dcat, saved0, gates0, h_init, c_init, wts, *, rows, row_off, row_w, tl, name):
    W = P.shape[1] // 6
    nt = rows // tl
    ob = row_off // tl
    gs = min(LRU_GROUP, W)
    ng = W // gs
    wd, lam = wts["wd"][0], wts["lam"][0]
    h0s = saved0["h"]
    kept = [saved0["xb"], saved0["xb16"]] + [gates0[n] for n in ("a", "r", "ig", "m2")]

    def tile(i):
        return nt - 1 - i

    vec = pl.BlockSpec((1, W), lambda i: (0, 0))
    wd_spec = pl.BlockSpec(wd.shape, lambda i: (0, 0, 0))
    seq = pl.BlockSpec((tl, W), lambda i: (tile(i), 0))

    sub_r = min(row_w, tl)
    assert tl % sub_r == 0

    def body(ql, dyb, hs, hedge8, xb_r, xb16_r, a_r, r_r, ig_r, m2_r, wd_r, lam_r, hin, cin,
             dxb_o, dwd_o, dba_o, dbx_o, dlam_o, cfin, *scratch):
        scr = dict(zip(_BWD_SCRATCH, scratch, strict=True))
        scr.update(zip(_FWD_SAVED, (xb_r, xb16_r, a_r, r_r, ig_r, m2_r), strict=True))
        i = pl.program_id(0)

        @pl.when(i == 0)
        def _():
            scr["c"][...] = cin[...]

        scr["edge"][...] = jnp.where(i == nt - 1, hin[...], hedge8[SUBLANES - 1:SUBLANES, :])
        _lru_bwd_tile(0, lambda rs, ls: _f32(dyb, rs, ls) * _silu(_f32(ql, rs, ls)), hs, wd_r, lam_r, scr,
                      (dwd_o, dba_o, dbx_o, dlam_o), i == 0, tl, sub_r, ng, gs)
        dxb_o[...] = scr["dyl"][...] + scr["g"][...]
        cfin[...] = scr["c"][...]

    return pl.pallas_call(
        body, name=name, grid=(nt,),
        in_specs=[pl.BlockSpec((tl, W), lambda i: (tile(i) + ob, 5)), pl.BlockSpec((tl, W), lambda i: (tile(i), 1)), seq,
                  _edge_block(h0s, tl, nt, 0)] + [seq] * 6 + [wd_spec, vec, vec, vec],
        out_specs=[seq, wd_spec, vec, vec, vec, vec],
        out_shape=[jax.ShapeDtypeStruct((rows, W), F32), jax.ShapeDtypeStruct(wd.shape, F32)]
        + [jax.ShapeDtypeStruct((1, W), F32)] * 4,
        scratch_shapes=_bwd_scratch(tl, W),
        compiler_params=_params(("arbitrary",)),
    )(P, dcat, h0s, h0s, *kept, wd, lam, h_init, c_init)


def _mix_bwd1(P, dcat, saved0, h1s, gates1, dxb0, h_init, c_init, wts, *, rows, row_off, row_w, tl, name,
              dp_rows=None, dp_off=0, dp_into=None):
    dp_rows = rows if dp_rows is None else dp_rows
    dpb = dp_off // tl
    W = P.shape[1] // 6
    nt = rows // tl
    ob = row_off // tl
    gs = min(LRU_GROUP, W)
    ng = W // gs
    wca, wcb = wts["wca"], wts["wcb"]
    wd, lam = wts["wd"][1], wts["lam"][1]
    h0s = saved0["h"]
    kept = [saved0["xb"], saved0["xb16"]] + [gates1[n] for n in ("a", "r", "ig", "m2")]

    vec = pl.BlockSpec((1, W), lambda i: (0, 0))
    taps = pl.BlockSpec((SUBLANES, W), lambda i: (0, 0))
    wd_spec = pl.BlockSpec(wd.shape, lambda i: (0, 0, 0))
    seq = pl.BlockSpec((tl, W), lambda i: (i, 0))

    sub_r = min(row_w, tl)
    assert tl % sub_r == 0

    def body(*refs):
        if dp_into is not None:
            refs = refs[1:]
        (bl, cl, ul, gl, vl, ql, dya, dyb, h0, h1, hedge8, dx0, xb_r, xb16_r, a_r, r_r, ig_r, m2_r, wca_r, wcb_r,
         wd_r, lam_r, hin, cin, dp_o, dwd_o, dba_o, dbx_o, dlam_o, dwca_o, dwcb_o, dbcb_o, cfin, *scratch) = refs
        scr = dict(zip(_BWD_SCRATCH, scratch, strict=True))
        scr.update(zip(_FWD_SAVED, (xb_r, xb16_r, a_r, r_r, ig_r, m2_r), strict=True))
        i = pl.program_id(0)

        @pl.when(i == 0)
        def _():
            scr["c"][...] = cin[...]
            dwca_o[...] = jnp.zeros_like(dwca_o)
            dwcb_o[...] = jnp.zeros_like(dwcb_o)
            dbcb_o[...] = jnp.zeros_like(dbcb_o)

        scr["edge"][...] = jnp.where(i == nt - 1, hin[...], hedge8[0:1, :])
        _lru_bwd_tile(1, lambda rs, ls: _f32(dyb, rs, ls) * _silu(_f32(ql, rs, ls)), h1, wd_r, lam_r, scr,
                      (dwd_o, dba_o, dbx_o, dlam_o), i == 0, tl, sub_r, ng, gs)
        cfin[...] = scr["c"][...]
        tmod = _tmod(sub_r, sub_r)

        def rest(r0, l0):
            rs, ls = pl.ds(r0, sub_r), pl.ds(l0, LANES)
            dxb = dx0[rs, ls] + scr["dyl"][rs, ls] + scr["g"][rs, ls]
            dv, dwb = _conv_bwd(dxb, _f32(vl, rs, ls), wcb_r, 4, 2, tmod, sub_r, ls)
            for j in range(4):
                dwcb_o[j:j + 1, ls] += dwb[j]
            dbcb_o[:, ls] += jnp.sum(dxb, axis=0, keepdims=True)
            q = _f32(ql, rs, ls)
            sq = _sigmoid(q)
            dq = _f32(dyb, rs, ls) * (h0[rs, ls] + h1[rs, ls]) * (sq * (1.0 + q * (1.0 - sq)))
            b_, c_, u_, g_ = _f32(bl, rs, ls), _f32(cl, rs, ls), _f32(ul, rs, ls), _f32(gl, rs, ls)
            z = c_ * u_
            cz = _conv(z, wca_r, 3, 1, tmod, sub_r, ls)
            sgm = _sigmoid(g_)
            sg = g_ * sgm
            da = _f32(dya, rs, ls)
            dz, dwa = _conv_bwd(da * b_ * sg, z, wca_r, 3, 1, tmod, sub_r, ls)
            for j in range(3):
                dwca_o[j:j + 1, ls] += dwa[j]
            parts = (da * cz * sg, dz * u_, dz * c_, da * b_ * cz * (sgm * (1.0 + g_ * (1.0 - sgm))), dv, dq)
            for k, val in enumerate(parts):
                dp_o[rs, pl.ds(k * W + l0, LANES)] = val.astype(dp_o.dtype)

        _sub_loop(tl, sub_r, W, rest)

    def pcol(j):
        return pl.BlockSpec((tl, W), lambda i: (i + ob, j))

    prev = [] if dp_into is None else [dp_into]
    return pl.pallas_call(
        body, name=name, grid=(nt,), input_output_aliases={} if dp_into is None else {0: 0},
        in_specs=[pl.BlockSpec(memory_space=pl.ANY)] * len(prev) + [pcol(j) for j in range(6)]
        + [pl.BlockSpec((tl, W), lambda i: (i, 0)), pl.BlockSpec((tl, W), lambda i: (i, 1)), seq, seq,
           _edge_block(h1s, tl, nt, 1), seq] + [seq] * 6 + [taps, taps, wd_spec, vec, vec, vec],
        out_specs=[pl.BlockSpec((tl, 6 * W), lambda i: (i + dpb, 0)), wd_spec, vec, vec, vec, taps, taps, vec, vec],
        out_shape=[jax.ShapeDtypeStruct((dp_rows, 6 * W), BF16), jax.ShapeDtypeStruct(wd.shape, F32)]
        + [jax.ShapeDtypeStruct((1, W), F32)] * 3
        + [jax.ShapeDtypeStruct((SUBLANES, W), F32)] * 2 + [jax.ShapeDtypeStruct((1, W), F32)] * 2,
        scratch_shapes=_bwd_scratch(tl, W),
        compiler_params=_params(("arbitrary",)),
    )(*prev, *([P] * 6), dcat, dcat, h0s, h1s, h1s, dxb0, *kept, wca, wcb, wd, lam, h_init, c_init)


def _loss_head(out, x, tgt, gt, fg, tr):
    L, D = x.shape

    def body(o_ref, x_ref, t_ref, gt_ref, fg_ref, dn_o, do_o, dfg_o, dgt_o, loss_o):
        i = pl.program_id(0)

        @pl.when(i == 0)
        def _():
            dfg_o[...] = jnp.zeros_like(dfg_o)
            dgt_o[...] = jnp.zeros_like(dgt_o)
            loss_o[...] = jnp.zeros_like(loss_o)

        o = o_ref[...].astype(F32)
        gt_v = gt_ref[...]
        fg_v = fg_ref[...]
        n = x_ref[...] + gt_v * o
        r = lax.rsqrt(jnp.mean(n * n, axis=-1, keepdims=True) + EPS)
        nr = n * r
        e = nr * fg_v - t_ref[...]
        loss_o[...] += 0.5 * jnp.sum(jnp.mean(e * e, axis=-1, keepdims=True))
        dy = e * (1.0 / D)
        dfg_o[...] += jnp.sum(dy * nr, axis=0, keepdims=True)
        qv = dy * fg_v
        dn = r * (qv - nr * jnp.mean(qv * nr, axis=-1, keepdims=True))
        dgt_o[...] += jnp.sum(dn * o, axis=0, keepdims=True)
        dn_o[...] = dn.astype(dn_o.dtype)
        do_o[...] = (dn * gt_v).astype(do_o.dtype)

    blk = pl.BlockSpec((tr, D), lambda i: (i, 0))
    vec = pl.BlockSpec((1, D), lambda i: (0, 0))
    return pl.pallas_call(
        body, name="loss_head", grid=(L // tr,), in_specs=[blk, blk, blk, vec, vec],
        out_specs=[blk, blk, vec, vec, pl.BlockSpec((SUBLANES, LANES), lambda i: (0, 0))],
        out_shape=[jax.ShapeDtypeStruct((L, D), BF16), jax.ShapeDtypeStruct((L, D), BF16),
                   jax.ShapeDtypeStruct((1, D), F32), jax.ShapeDtypeStruct((1, D), F32),
                   jax.ShapeDtypeStruct((SUBLANES, LANES), F32)],
        compiler_params=_params(("arbitrary",)),
    )(out, x, tgt, gt, fg)


def _norm_bwd(dhl, x, dn, g, sc, tr, name):
    L, D = x.shape
    with_x = dn is not None

    def body(*refs):
        if with_x:
            d_ref, x_ref, dn_ref, g_ref, sc_ref, gx_o, dsh_o, dsc_o, dg_o = refs
        else:
            d_ref, x_ref, g_ref, sc_ref, dsh_o, dsc_o, dg_o = refs
        i = pl.program_id(0)

        @pl.when(i == 0)
        def _():
            dsh_o[...] = jnp.zeros_like(dsh_o)
            dsc_o[...] = jnp.zeros_like(dsc_o)
            dg_o[...] = jnp.zeros_like(dg_o)

        d = d_ref[...].astype(F32)
        xv = x_ref[...]
        g_v = g_ref[...]
        r = lax.rsqrt(jnp.mean(xv * xv, axis=-1, keepdims=True) + EPS)
        xr = xv * r
        dsh_o[...] += jnp.sum(d, axis=0, keepdims=True)
        dsc_o[...] += jnp.sum(d * (xr * g_v), axis=0, keepdims=True)
        dxn = d * (1.0 + sc_ref[...])
        dg_o[...] += jnp.sum(dxn * xr, axis=0, keepdims=True)
        if with_x:
            qv = dxn * g_v
            gx_o[...] = r * (qv - xr * jnp.mean(qv * xr, axis=-1, keepdims=True)) + dn_ref[...].astype(F32)

    blk = pl.BlockSpec((tr, D), lambda i: (i, 0))
    vec = pl.BlockSpec((1, D), lambda i: (0, 0))
    vshape = jax.ShapeDtypeStruct((1, D), F32)
    res = pl.pallas_call(
        body, name=name, grid=(L // tr,),
        in_specs=[blk, blk] + ([blk] if with_x else []) + [vec, vec],
        out_specs=([blk] if with_x else []) + [vec, vec, vec],
        out_shape=([jax.ShapeDtypeStruct((L, D), F32)] if with_x else []) + [vshape] * 3,
        compiler_params=_params(("arbitrary",)),
    )(*([dhl, x] + ([dn] if with_x else []) + [g, sc]))
    return res if with_x else [None] + list(res)


def _pack_blockdiag(wa, wx, gs):
    H, hd, _ = wa.shape
    hp = gs // hd
    ng = H // hp
    eye = jnp.eye(hp, dtype=wa.dtype)

    def bd(w):
        return jnp.einsum("gpij,pq->gpiqj", w.reshape(ng, hp, hd, hd), eye).reshape(ng, gs, gs)

    return jnp.concatenate([bd(wa), bd(wx)], axis=-1).astype(BF16)


def _unpack_blockdiag(dwd, H, hd, gs):
    hp = gs // hd
    ng = H // hp
    eye = jnp.eye(hp, dtype=dwd.dtype)

    def diag(dm):
        return jnp.einsum("gpiqj,pq->gpij", dm.reshape(ng, hp, hd, hp, hd), eye).reshape(H, hd, hd)

    return diag(dwd[:, :, :gs]), diag(dwd[:, :, gs:])


def kernel(x, c, ctx, c_ctx, norm_g, w_ada, b_ada, w_in, w_conv_a, w_conv_b, b_conv_b, lru_wa, lru_ba, lru_wx, lru_bx, lru_lambda, w_out, final_g, loss_target, m_c_ctx, m_norm_g, m_w_ada, m_b_ada, m_w_in, m_w_conv_a, m_w_conv_b, m_b_conv_b, m_lru_wa, m_lru_ba, m_lru_wx, m_lru_bx, m_lru_lambda, m_w_out, m_final_g, v_c_ctx, v_norm_g, v_w_ada, v_b_ada, v_w_in, v_w_conv_a, v_w_conv_b, v_b_conv_b, v_lru_wa, v_lru_ba, v_lru_wx, v_lru_bx, v_lru_lambda, v_w_out, v_final_g):
    xi, yi, ci = _pos()
    me = 4 * xi + 2 * yi + ci
    q = 2 * xi + yi
    first_core = (ci == 0).astype(F32)

    L, D = x.shape[1], x.shape[2]
    T = ctx.shape[1]
    W = D // 2
    Wq = W // 4
    H, hd = lru_wa.shape[2], lru_wa.shape[3]
    gs = min(LRU_GROUP, W)
    nq = w_ada.shape[2]
    tl = min(256, T, L)
    tr = min(256, T, L)
    x2, ctx2, tgt2 = x[0], ctx[0], loss_target[0]

    def place(shard, full_cols):
        z = jnp.zeros((shard.shape[0], full_cols), F32)
        return lax.dynamic_update_slice(z, shard * first_core, (0, q * shard.shape[1]))

    c_rows = lax.dynamic_update_slice(jnp.zeros((8, D), F32), c, (me, 0))
    small_in = [c_rows, place(w_conv_a[0], W), place(w_conv_b[0], W), place(lru_ba[0], W), place(lru_bx[0], W),
                place(lru_lambda[0], W)]
    small_shapes = [a.shape for a in small_in]
    gathered = _allreduce8(_pack(small_in, 8 * SUBLANES), "gather_small")
    c_all, wca, wcb, ba_all, bx_all, lam_all = _unpack(gathered, small_shapes)

    s_rows = jnp.concatenate([c_all, c_ctx[None, :], jnp.zeros((7, D), F32)], axis=0)
    mod_part = _matmul(s_rows, w_ada[0], a_act="silu", bias=lax.dynamic_slice(b_ada, (0, q * nq), (1, nq)),
                       tm=16, tn=nq, tk=512, name="ada_fwd")
    mod_all = _allreduce8(_pack([place(mod_part[:9], 4 * nq)], 8 * SUBLANES), "gather_mod")
    mod_all = _unpack(mod_all, [(9, 4 * nq)])[0]
    mod_l = lax.dynamic_slice(mod_all, (me, 0), (1, 3 * D))
    mod_c = mod_all[8:9]
    sh_l, sc_l, gt_l = mod_l[:, :D], mod_l[:, D:2 * D], mod_l[:, 2 * D:]
    sh_c, sc_c = mod_c[:, :D], mod_c[:, D:2 * D]

    pad_taps = lambda w: jnp.pad(w, ((0, SUBLANES - w.shape[0]), (0, 0)))
    wts = {
        "wca": pad_taps(wca), "wcb": pad_taps(wcb), "bcb": b_conv_b,
        "wd": [_pack_blockdiag(lru_wa[0, d], lru_wx[0, d], gs) for d in range(2)],
        "ba": [ba_all[d:d + 1] for d in range(2)], "bx": [bx_all[d:d + 1] for d in range(2)],
        "lam": [lam_all[d:d + 1] for d in range(2)],
    }

    hl = _norm_in(x2, ctx2, norm_g, sc_l, sh_l, sc_c, sh_c, tr)
    p_lat, win_full, wout_full = _in_proj_gather(hl, w_in[0].astype(BF16), w_out[0].astype(BF16),
                                                 jnp.reshape(q, (1,)).astype(jnp.int32), rows=L, tm=min(1024, L))
    p_ctx = _matmul(hl, win_full, a_rows=T, a_off=L, tm=T, tn=1536, tk=D, out_dtype=BF16, name="in_proj_ctx")
    zero_w = jnp.zeros((1, W), F32)
    ctx0, cgates0 = _mix_fwd(p_ctx, 0, zero_w, wts, rows=T, row_off=0, row_w=T, tl=tl, name="ctx_fwd0")
    c1s, _, cgates1 = _mix_fwd(p_ctx, 1, zero_w, wts, rows=T, row_off=0, row_w=T, tl=tl, saved0=ctx0, name="ctx_fwd1")
    h0_init, h1_init = ctx0["h"][T - 1:T], c1s[0:1]
    tl_tall = 2 * tl if L % (2 * tl) == 0 else tl
    lat0, gates0 = _mix_fwd(p_lat, 0, h0_init, wts, rows=L, row_off=0, row_w=GRID_W, tl=tl_tall, name="mix_fwd0")
    h1s, cat, gates1 = _mix_fwd(p_lat, 1, h1_init, wts, rows=L, row_off=0, row_w=GRID_W, tl=tl_tall, saved0=lat0,
                                name="mix_fwd1")
    out = _matmul(cat, wout_full, tm=1024, tn=D, tk=2 * W, out_dtype=BF16, name="out_proj")
    tr_lat = 2 * tr if L % (2 * tr) == 0 else tr
    dn, dout, dfg, dgt, loss_blk = _loss_head(out, x2, tgt2, gt_l, final_g[None, :], tr_lat)

    dcat = _matmul(dout, wout_full, tb=True, tm=1024, tn=2 * W, tk=D, out_dtype=BF16, name="out_proj_bwd")
    gw_out = _matmul(cat, dout, ta=True, tm=1024, tn=D, tk=2048, out_dtype=BF16, name="w_out_grad")
    dxb0, dwd0, dba0, dbx0, dlam0, ch0 = _mix_bwd0(p_lat, dcat, lat0, gates0, h0_init, zero_w, wts, rows=L, row_off=0,
                                                   row_w=GRID_W, tl=tl_tall, name="mix_bwd0")
    dp_lat, dwd1, dba1, dbx1, dlam1, dwca, dwcb, dbcb, ch1 = _mix_bwd1(
        p_lat, dcat, lat0, h1s, gates1, dxb0, h1_init, zero_w, wts, rows=L, row_off=0, row_w=GRID_W, tl=tl,
        name="mix_bwd1", dp_rows=L + T)
    zero_cat = jnp.zeros((T, 2 * W), BF16)
    cxb0, cwd0, cba0, cbx0, clam0, _ = _mix_bwd0(p_ctx, zero_cat, ctx0, cgates0, zero_w, ch0, wts, rows=T, row_off=0,
                                                 row_w=T, tl=tl, name="ctx_bwd0")
    dp, cwd1, cba1, cbx1, clam1, cwca, cwcb, cbcb, _ = _mix_bwd1(
        p_ctx, zero_cat, ctx0, c1s, cgates1, cxb0, zero_w, ch1, wts, rows=T, row_off=0, row_w=T, tl=tl, name="ctx_bwd1",
        dp_rows=L + T, dp_off=L, dp_into=dp_lat)

    gw_in = _matmul(hl, dp, ta=True, tm=1024, tn=1536, tk=2816, out_dtype=BF16, name="w_in_grad")
    rs_axes = [1, 0]
    pair_sums = _rs_pair_sums([gw_in, gw_out], rs_axes)
    dhc = _matmul(dp, win_full, tb=True, a_rows=T, a_off=L, tm=T, tn=D, tk=3072, name="in_proj_bwd_ctx")
    _, dsh_c, dsc_c, dng_c = _norm_bwd(dhc, ctx2, None, norm_g, sc_c, tr, "norm_bwd_ctx")
    zeros_d = jnp.zeros((1, D), F32)
    dmod_c = jnp.concatenate([dsh_c, dsc_c, zeros_d], axis=1)
    dhl, (*rs_slots, dmod_c_all) = _matmul(
        dp, win_full, tb=True, a_rows=L, tm=512, tn=D, tk=6 * W, out_dtype=BF16, name="in_proj_bwd",
        side=_join_sides(_rs_chips_side(pair_sums), _gather_block_side(jnp.pad(dmod_c, ((0, SUBLANES - 1), (0, 0))))))
    gx, dsh_l, dsc_l, dng_l = _norm_bwd(dhl, x2, dn, norm_g, sc_l, tr_lat, "norm_bwd")
    gc_rows = _matmul(lax.dynamic_slice(dmod_c_all.reshape(8 * SUBLANES, 3 * D), (0, q * nq), (8 * SUBLANES, nq)), w_ada[0],
                      tb=True, dsilu_mul=c_ctx[None, :], tm=8 * SUBLANES, tn=D, tk=512, name="c_ctx_grad")
    gc_part = jnp.sum(gc_rows, axis=0, keepdims=True) * first_core

    g_in_shard, g_out_shard = _rs_finish(rs_slots, rs_axes)

    dwa0, dwx0 = _unpack_blockdiag(dwd0 + cwd0, H, hd, gs)
    dwa1, dwx1 = _unpack_blockdiag(dwd1 + cwd1, H, hd, gs)
    dmod_l = jnp.concatenate([dsh_l, dsc_l, dgt], axis=1)
    small_g = [
        lax.dynamic_update_slice(jnp.zeros((8, 3 * D), F32), dmod_l, (me, 0)), dmod_c,
        dfg, dng_l + dng_c, (dwca + cwca)[:3], (dwcb + cwcb)[:4], dbcb + cbcb,
        jnp.stack([dwa0, dwa1]), jnp.stack([dwx0, dwx1]),
        jnp.concatenate([dba0 + cba0, dba1 + cba1], axis=0), jnp.concatenate([dbx0 + cbx0, dbx1 + cbx1], axis=0),
        jnp.concatenate([dlam0 + clam0, dlam1 + clam1], axis=0), loss_blk[0:1, 0:1], gc_part,
    ]
    g_shapes = [a.shape for a in small_g]
    (g_rows, g_modc, g_fg, g_ng, g_wca, g_wcb, g_bcb, g_wa, g_wx, g_ba, g_bx, g_lam, loss_sum, g_c_ctx) = _unpack(
        _allreduce8_two_level(_pack(small_g, 8 * SUBLANES), "reduce_small"), g_shapes)

    g_mod = jnp.concatenate([g_rows, g_modc, jnp.zeros((7, 3 * D), F32)], axis=0)
    g_mod_q = lax.dynamic_slice(g_mod, (0, q * nq), (16, nq))
    g_w_ada = _matmul(s_rows, g_mod_q, ta=True, a_act="silu", tm=1024, tn=nq, tk=16, name="w_ada_grad")
    g_b_ada = jnp.sum(g_mod[:9], axis=0, keepdims=True)

    def shard_cols(a, width):
        return lax.dynamic_slice(a, (0, q * width), (a.shape[0], width))

    grads = {
        "c_ctx": g_c_ctx, "norm_g": g_ng, "b_ada": g_b_ada,
        "w_conv_a": shard_cols(g_wca, Wq)[None], "w_conv_b": shard_cols(g_wcb, Wq)[None], "b_conv_b": g_bcb,
        "lru_wa": g_wa[None], "lru_ba": shard_cols(g_ba, Wq)[None], "lru_wx": g_wx[None],
        "lru_bx": shard_cols(g_bx, Wq)[None], "lru_lambda": shard_cols(g_lam, Wq)[None], "final_g": g_fg[0],
    }
    small_names = list(grads)
    given = dict(c_ctx=(c_ctx, m_c_ctx, v_c_ctx), norm_g=(norm_g, m_norm_g, v_norm_g), b_ada=(b_ada, m_b_ada, v_b_ada),
                 w_conv_a=(w_conv_a, m_w_conv_a, v_w_conv_a), w_conv_b=(w_conv_b, m_w_conv_b, v_w_conv_b),
                 b_conv_b=(b_conv_b, m_b_conv_b, v_b_conv_b), lru_wa=(lru_wa, m_lru_wa, v_lru_wa),
                 lru_ba=(lru_ba, m_lru_ba, v_lru_ba), lru_wx=(lru_wx, m_lru_wx, v_lru_wx),
                 lru_bx=(lru_bx, m_lru_bx, v_lru_bx), lru_lambda=(lru_lambda, m_lru_lambda, v_lru_lambda),
                 final_g=(final_g, m_final_g, v_final_g))
    def rows2d(a):
        return a.reshape(-1, a.shape[-1])

    grads = {n: grads[n].reshape(given[n][0].shape) for n in small_names}
    quads = [tuple(rows2d(a) for a in (given[n][0], grads[n], given[n][1], given[n][2])) for n in small_names]
    updated = _adam_many(quads, "adam_small")
    delta_s, newm_s, newv_s = ({n: u[j].reshape(given[n][0].shape) for n, u in zip(small_names, updated)} for j in range(3))

    big = {"w_ada": (w_ada, g_w_ada, m_w_ada, v_w_ada), "w_in": (w_in, g_in_shard, m_w_in, v_w_in),
           "w_out": (w_out, g_out_shard, m_w_out, v_w_out)}
    delta_b, newm_b, newv_b = {}, {}, {}
    for n, (w, g, m, v) in big.items():
        d_, m_, v_, *echo = _adam(w[0], g, m[0], v[0], "adam_" + n, echo_g=n != "w_ada")
        grads[n] = (echo[0] if echo else g)[None]
        delta_b[n], newm_b[n], newv_b[n] = d_[None], m_[None], v_[None]

    loss = loss_sum[0, 0]
    order = ["c_ctx", "norm_g", "w_ada", "b_ada", "w_in", "w_conv_a", "w_conv_b", "b_conv_b", "lru_wa", "lru_ba",
             "lru_wx", "lru_bx", "lru_lambda", "w_out", "final_g"]
    delta = {**delta_s, **delta_b}
    newm = {**newm_s, **newm_b}
    newv = {**newv_s, **newv_b}
    return (loss, gx[None], *[grads[n] for n in order], *[delta[n] for n in order], *[newm[n] for n in order],
            *[newv[n] for n in order])
```

```python
import jax
import jax.numpy as jnp
from jax import lax
from jax.experimental import pallas as pl
from jax.experimental.pallas import tpu as pltpu

F32 = jnp.float32
BF16 = jnp.bfloat16
MESH_ID = pl.DeviceIdType.MESH

EPS = 1e-6
LRU_C = 8.0
GRID_W = 64
ADAM_LR = 0.001
ADAM_B1 = 0.9
ADAM_B2 = 0.999
ADAM_EPS = 1e-08
ADAM_WD = 0.01
ADAM_STEP = 10

LANES = 128
SUBLANES = 8
PACK_COLS = 1024
VMEM_LIMIT = 56 * 2**20
LRU_GROUP = 256


def _params(sem=None):
    return pltpu.CompilerParams(vmem_limit_bytes=VMEM_LIMIT, dimension_semantics=sem)


def _pick(dim, pref, quantum=LANES):
    if dim <= pref:
        return dim
    best = None
    for t in range(quantum, pref + 1, quantum):
        if dim % t == 0:
            best = t
    assert best is not None, (dim, pref)
    return best


def _pos():
    return lax.axis_index("x"), lax.axis_index("y"), lax.axis_index("c")


def _flip(v, bit):
    return 1 - v if bit else v


def _sigmoid(v):
    return 0.5 * jnp.tanh(0.5 * v) + 0.5


def _silu(v):
    return v * _sigmoid(v)


def _dsilu(v):
    s = _sigmoid(v)
    return s * (1.0 + v * (1.0 - s))


def _gates(pre_r, pre_i, sp):
    r = _sigmoid(pre_r)
    ig = _sigmoid(pre_i)
    e = LRU_C * r * sp
    w = jnp.tanh(e)
    return r, ig, jnp.exp(-e), (2.0 * w) * pl.reciprocal(1.0 + w, approx=True)


def _softplus(z):
    return jnp.maximum(z, 0.0) + jnp.log1p(jnp.exp(-jnp.abs(z)))


def _matmul(a, b, *, ta=False, tb=False, tm=512, tn=512, tk=512, out_dtype=F32, name,
            a_rows=None, a_off=0, a_act=None, bias=None, dsilu_mul=None, side=None):
    rows_a = a.shape[0] if a_rows is None else a_rows
    if ta:
        K, M = rows_a, a.shape[1]
    else:
        M, K = rows_a, a.shape[1]
    N = b.shape[0] if tb else b.shape[1]
    tm, tn, tk = _pick(M, tm, SUBLANES), _pick(N, tn), _pick(K, tk)
    t_rows = tk if ta else tm
    assert a_off % t_rows == 0
    nk = K // tk
    gi, gj = M // tm, N // tn
    off_blocks = a_off // t_rows
    dims = (((0 if ta else 1,), (1 if tb else 0,)), ((), ()))
    extras = [e for e in (bias, dsilu_mul) if e is not None]
    n_sin = len(side["ins"]) if side else 0
    n_sout = len(side["outs"]) if side else 0

    def body(a_ref, b_ref, *rest):
        rest = list(rest)
        bias_ref = rest.pop(0) if bias is not None else None
        dsm_ref = rest.pop(0) if dsilu_mul is not None else None
        side_in = [rest.pop(0) for _ in range(n_sin)]
        o_ref = rest.pop(0)
        side_out = [rest.pop(0) for _ in range(n_sout)]
        acc_ref = rest.pop(0) if nk > 1 else None
        side_scr = rest
        i, j, k = pl.program_id(0), pl.program_id(1), pl.program_id(2)

        if side:
            @pl.when((i == 0) & (j == 0) & (k == 0))
            def _():
                side["start"](side_in, side_out, side_scr)

        av = a_ref[...]
        if a_act == "silu":
            av = _silu(av)
        prod = lax.dot_general(av, b_ref[...], dims, preferred_element_type=F32)

        def finish(r):
            if bias_ref is not None:
                r = r + bias_ref[...]
            if dsm_ref is not None:
                r = r * _dsilu(dsm_ref[...])
            o_ref[...] = r.astype(o_ref.dtype)

        if nk == 1:
            finish(prod)
        else:
            @pl.when(k == 0)
            def _():
                acc_ref[...] = prod

            @pl.when(k > 0)
            def _():
                acc_ref[...] += prod

            @pl.when(k == nk - 1)
            def _():
                finish(acc_ref[...])

        if side:
            @pl.when((i == gi - 1) & (j == gj - 1) & (k == nk - 1))
            def _():
                side["finish"](side_in, side_out, side_scr)

    if ta:
        a_spec = pl.BlockSpec((tk, tm), lambda i, j, k: (k + off_blocks, i))
    else:
        a_spec = pl.BlockSpec((tm, tk), lambda i, j, k: (i + off_blocks, k))
    once = dict(pipeline_mode=pl.Buffered(1)) if (gj == 1 and nk == 1) else {}
    if tb:
        b_spec = pl.BlockSpec((tn, tk), lambda i, j, k: (j, k), **once)
    else:
        b_spec = pl.BlockSpec((tk, tn), lambda i, j, k: (k, j), **once)
    in_specs = [a_spec, b_spec]
    if bias is not None:
        in_specs.append(pl.BlockSpec((1, tn), lambda i, j, k: (0, j)))
    if dsilu_mul is not None:
        in_specs.append(pl.BlockSpec((1, tn), lambda i, j, k: (0, j)))
    hbm = pl.BlockSpec(memory_space=pl.ANY)
    res = pl.pallas_call(
        body, name=name, grid=(gi, gj, nk),
        in_specs=in_specs + [hbm] * n_sin,
        out_specs=[pl.BlockSpec((tm, tn), lambda i, j, k: (i, j))] + [hbm] * n_sout,
        out_shape=[jax.ShapeDtypeStruct((M, N), out_dtype)] + (list(side["outs"]) if side else []),
        scratch_shapes=([pltpu.VMEM((tm, tn), F32)] if nk > 1 else []) + (list(side["scratch"]) if side else []),
        compiler_params=_params(("arbitrary",) * 3 if side else ("parallel", "parallel", "arbitrary")),
    )(a, b, *extras, *(side["ins"] if side else []))
    return (res[0], res[1:]) if side else res[0]


def _elementwise(fn, ins, outs, *, rows, cols, name, tr=256):
    tr = _pick(rows, tr, 2 * SUBLANES)
    n_in = len(ins)

    def body(*refs):
        vals = fn(*[r[...] for r in refs[:n_in]])
        if not isinstance(vals, (tuple, list)):
            vals = (vals,)
        for r, v in zip(refs[n_in:], vals, strict=True):
            r[...] = v.astype(r.dtype)

    def spec(off):
        assert off % tr == 0
        ob = off // tr
        return pl.BlockSpec((tr, cols), lambda i: (i + ob, 0))

    res = pl.pallas_call(
        body, name=name, grid=(rows // tr,),
        in_specs=[spec(off) for _, off in ins],
        out_specs=[spec(0) for _ in outs],
        out_shape=[jax.ShapeDtypeStruct((rows, cols), dt) for dt in outs],
        compiler_params=_params(("parallel",)),
    )(*[a for a, _ in ins])
    return res


def _adam_math(w, g, m, v):
    m = ADAM_B1 * m + (1.0 - ADAM_B1) * g
    v = ADAM_B2 * v + (1.0 - ADAM_B2) * (g * g)
    m_hat = m / (1.0 - ADAM_B1 ** ADAM_STEP)
    v_hat = v / (1.0 - ADAM_B2 ** ADAM_STEP)
    delta = -ADAM_LR * (m_hat / (jnp.sqrt(v_hat) + ADAM_EPS) + ADAM_WD * w)
    return delta, m, v


def _adam(w, g, m, v, name, echo_g=False):
    rows, cols = w.shape
    fn = (lambda w_, g_, m_, v_: _adam_math(w_, g_, m_, v_) + (g_,)) if echo_g else _adam_math
    return _elementwise(fn, [(w, 0), (g, 0), (m, 0), (v, 0)], [F32] * (4 if echo_g else 3),
                        rows=rows, cols=cols, name=name)


def _adam_many(quads, name):
    n = len(quads)

    def body(*refs):
        ins, outs = refs[:4 * n], refs[4 * n:]
        for t in range(n):
            w, g, m, v = (r[...] for r in ins[4 * t:4 * t + 4])
            for o_ref, val in zip(outs[3 * t:3 * t + 3], _adam_math(w, g, m, v), strict=True):
                o_ref[...] = val

    res = pl.pallas_call(
        body, name=name,
        out_shape=[jax.ShapeDtypeStruct(q[0].shape, F32) for q in quads for _ in range(3)],
        compiler_params=_params(),
    )(*[a for q in quads for a in q])
    return [res[3 * t:3 * t + 3] for t in range(n)]


def _pack(arrs, row_quantum):
    flat = jnp.concatenate([a.reshape(-1).astype(F32) for a in arrs])
    n = flat.shape[0]
    q = row_quantum * PACK_COLS
    total = -(-n // q) * q
    flat = jnp.pad(flat, (0, total - n))
    return flat.reshape(total // PACK_COLS, PACK_COLS)


def _unpack(buf, shapes):
    flat = buf.reshape(-1)
    out, off = [], 0
    for s in shapes:
        n = 1
        for d in s:
            n *= d
        out.append(flat[off:off + n].reshape(s))
        off += n
    return out


def _allreduce8(buf, name):
    R, C = buf.shape
    assert R % (8 * SUBLANES) == 0
    m = R // 8

    def body(x_ref, o_ref, recv, red, s1, r1, s2, r2):
        x, y, c = _pos()
        me = 4 * x + 2 * y + c

        def peer(k):
            px, py, pc = _flip(x, (k >> 2) & 1), _flip(y, (k >> 1) & 1), _flip(c, k & 1)
            return (px, py, pc), 4 * px + 2 * py + pc

        def rows(ref, idx):
            return ref.at[pl.ds(pl.multiple_of(idx * m, SUBLANES), m), :]

        def scatter(k):
            dev, p = peer(k)
            return pltpu.make_async_remote_copy(src_ref=rows(x_ref, p), dst_ref=recv.at[k], send_sem=s1.at[k],
                                                recv_sem=r1.at[k], device_id=dev, device_id_type=MESH_ID)

        def share(k):
            dev, p = peer(k)
            return pltpu.make_async_remote_copy(src_ref=red, dst_ref=rows(o_ref, me), send_sem=s2.at[k],
                                                recv_sem=r2.at[k], device_id=dev, device_id_type=MESH_ID)

        def shared_from(k):
            dev, p = peer(k)
            return pltpu.make_async_remote_copy(src_ref=red, dst_ref=rows(o_ref, p), send_sem=s2.at[k],
                                                recv_sem=r2.at[k], device_id=dev, device_id_type=MESH_ID)

        for k in range(1, 8):
            scatter(k).start()
        acc = rows(x_ref, me)[...]
        for k in range(1, 8):
            scatter(k).wait_recv()
            acc = acc + recv[k]
        red[...] = acc
        rows(o_ref, me)[...] = acc
        for k in range(1, 8):
            share(k).start()
        for k in range(1, 8):
            shared_from(k).wait_recv()
        for k in range(1, 8):
            scatter(k).wait_send()
            share(k).wait_send()

    return pl.pallas_call(
        body, name=name,
        in_specs=[pl.BlockSpec(memory_space=pltpu.VMEM)],
        out_specs=pl.BlockSpec(memory_space=pltpu.VMEM),
        out_shape=jax.ShapeDtypeStruct((R, C), F32),
        scratch_shapes=[pltpu.VMEM((8, m, C), F32), pltpu.VMEM((m, C), F32),
                        pltpu.SemaphoreType.DMA((8,)), pltpu.SemaphoreType.DMA((8,)),
                        pltpu.SemaphoreType.DMA((8,)), pltpu.SemaphoreType.DMA((8,))],
        compiler_params=_params(),
    )(buf)


def _allreduce8_two_level(buf, name):
    R, C = buf.shape
    assert R % (8 * SUBLANES) == 0
    m, hr = R // 8, R // 2

    def body(x_ref, o_ref, got0, half, got1, red, ssem, rsem):
        x, y, c = _pos()
        q = 2 * x + y
        sibling = (x, y, 1 - c)

        def half_rows(ref, core):
            return ref.at[pl.ds(pl.multiple_of(core * hr, SUBLANES), hr), :]

        def chunk(ref, core, chip):
            return ref.at[pl.ds(pl.multiple_of(core * hr + chip * m, SUBLANES), m), :]

        def chip_of(k):
            px, py = _flip(x, (k >> 1) & 1), _flip(y, k & 1)
            return (px, py, c), 2 * px + py

        def copy(src, dst, phase, k, dev):
            return pltpu.make_async_remote_copy(src_ref=src, dst_ref=dst, send_sem=ssem.at[phase, k],
                                                recv_sem=rsem.at[phase, k], device_id=dev, device_id_type=MESH_ID)

        swap = copy(half_rows(x_ref, 1 - c), got0, 0, 0, sibling)
        swap.start()
        swap.wait()
        half[...] = half_rows(x_ref, c)[...] + got0[...]

        def scatter(k):
            dev, p = chip_of(k)
            return copy(half.at[pl.ds(pl.multiple_of(p * m, SUBLANES), m), :], got1.at[k], 1, k, dev)

        for k in range(1, 4):
            scatter(k).start()
        acc = half[pl.ds(pl.multiple_of(q * m, SUBLANES), m), :]
        for k in range(1, 4):
            scatter(k).wait_recv()
            acc = acc + got1[k]
        red[...] = acc
        chunk(o_ref, c, q)[...] = acc

        def share(k, landing_chip):
            return copy(red, chunk(o_ref, c, landing_chip), 2, k, chip_of(k)[0])

        for k in range(1, 4):
            share(k, q).start()
        for k in range(1, 4):
            share(k, chip_of(k)[1]).wait_recv()
        back = copy(half_rows(o_ref, c), half_rows(o_ref, c), 3, 0, sibling)
        back.start()
        copy(half_rows(o_ref, 1 - c), half_rows(o_ref, 1 - c), 3, 0, sibling).wait_recv()
        back.wait_send()
        for k in range(1, 4):
            scatter(k).wait_send()
            share(k, q).wait_send()

    return pl.pallas_call(
        body, name=name,
        in_specs=[pl.BlockSpec(memory_space=pltpu.VMEM)],
        out_specs=pl.BlockSpec(memory_space=pltpu.VMEM),
        out_shape=jax.ShapeDtypeStruct((R, C), F32),
        scratch_shapes=[pltpu.VMEM((hr, C), F32), pltpu.VMEM((hr, C), F32), pltpu.VMEM((4, m, C), F32), pltpu.VMEM((m, C), F32),
                        pltpu.SemaphoreType.DMA((4, 4)), pltpu.SemaphoreType.DMA((4, 4))],
        compiler_params=_params(),
    )(buf)


def _bounce(src, dst, buf, sem):
    cin = pltpu.make_async_copy(src, buf, sem)
    cin.start()
    cin.wait()
    cout = pltpu.make_async_copy(buf, dst, sem)
    cout.start()
    cout.wait()


def _chunk(ref, axis, idx, size):
    start = idx * size
    if axis == 0:
        return ref.at[pl.ds(start, size), :]
    return ref.at[:, pl.ds(start, size)]


def _in_proj_gather(hl, win, wout, q_arr, *, rows, tm):
    D, nq = win.shape
    dq, D2 = wout.shape
    ni = rows // tm
    ops = ((0, 1, nq, D // 2), (1, 0, dq, dq // 2))

    def body(q_ref, a_ref, win_ref, wout_ref, p_ref, gin_ref, gout_ref, b_scr, buf_out, lsem, ssem, rsem, fsem, gsem):
        j, i = pl.program_id(0), pl.program_id(1)
        x, y, c = _pos()
        q = 2 * x + y
        srcs = (win_ref, wout_ref)
        dsts = (gin_ref, gout_ref)

        def shard_window(o, chip):
            _, axis, size, _ = ops[o]
            return _chunk(dsts[o], axis, chip, size)

        def half(ref, o, core):
            return ref.at[pl.ds(core * ops[o][3], ops[o][3]), :]

        def half_window(o, chip, core):
            _, axis, size, hs = ops[o]
            if axis == 1:
                return dsts[o].at[pl.ds(core * hs, hs), pl.ds(chip * size, size)]
            return dsts[o].at[pl.ds(chip * size + core * hs, hs), :]

        def chip_of(k):
            px, py = _flip(x, (k >> 1) & 1), _flip(y, k & 1)
            return px, py, 2 * px + py

        def send(o, k):
            px, py, _ = chip_of(k)
            return pltpu.make_async_remote_copy(
                src_ref=half(srcs[o], o, c), dst_ref=half_window(o, q, c), send_sem=ssem.at[o, k],
                recv_sem=rsem.at[o, k], device_id=(px, py, c), device_id_type=MESH_ID)

        def chip_recv(o, k):
            px, py, pq = chip_of(k)
            landed = half_window(o, pq, c)
            pltpu.make_async_remote_copy(src_ref=landed, dst_ref=landed, send_sem=ssem.at[o, k], recv_sem=rsem.at[o, k],
                                         device_id=(px, py, c), device_id_type=MESH_ID).wait_recv()

        def to_sibling(o, k):
            landed = half_window(o, chip_of(k)[2], c)
            return pltpu.make_async_remote_copy(src_ref=landed, dst_ref=landed, send_sem=fsem.at[o, k],
                                                recv_sem=gsem.at[o, k], device_id=(x, y, 1 - c), device_id_type=MESH_ID)

        def from_sibling(o, k):
            theirs = half_window(o, chip_of(k)[2], 1 - c)
            pltpu.make_async_remote_copy(src_ref=theirs, dst_ref=theirs, send_sem=fsem.at[o, k], recv_sem=gsem.at[o, k],
                                         device_id=(x, y, 1 - c), device_id_type=MESH_ID).wait_recv()

        def relay(o, core):
            if core == 0:
                landed, target = half_window(o, chip_of(2)[2], 0), (x, 1 - y, 0)
            else:
                landed, target = half_window(o, chip_of(1)[2], 1), (1 - x, y, 1)
            return pltpu.make_async_remote_copy(src_ref=landed, dst_ref=landed, send_sem=ssem.at[o, 3],
                                                recv_sem=rsem.at[o, 3], device_id=target, device_id_type=MESH_ID)

        def on_core(core, fn):
            @pl.when(c == core)
            def _():
                fn()

        def land(o, k):
            chip_recv(o, k)
            if k == 2:
                on_core(0, lambda: relay(o, 0).start())
            if k == 1:
                on_core(1, lambda: relay(o, 1).start())
            to_sibling(o, k).start()

        def settle(o, k):
            from_sibling(o, k)
            to_sibling(o, k).wait_send()

        def b_load(k, slot):
            src = win_ref if k == 0 else shard_window(0, chip_of(k)[2])
            return pltpu.make_async_copy(src, b_scr.at[slot], lsem.at[0])

        def own_store():
            return pltpu.make_async_copy(b_scr.at[0], shard_window(0, q), lsem.at[2])

        order = (0, 2, 1, 3)
        early = max(ni - 2, 0)

        @pl.when((j == 0) & (i == 0))
        def _():
            for o in range(2):
                for k in (2, 1):
                    send(o, k).start()
            first = b_load(0, 0)
            first.start()
            first.wait()
            own_store().start()
            _bounce(wout_ref, shard_window(1, q), buf_out, lsem.at[1])

        for jj in range(3):
            nxt = order[jj + 1]

            @pl.when((j == jj) & (i == early))
            def _(nxt=nxt):
                land(0, nxt)

            @pl.when((j == jj) & (i == ni - 1))
            def _(jj=jj, nxt=nxt):
                settle(0, nxt)
                if jj == 1:
                    own_store().wait()
                b_load(nxt, (jj + 1) % 2).start()

            @pl.when((j == jj + 1) & (i == 0))
            def _(jj=jj, nxt=nxt):
                b_load(nxt, (jj + 1) % 2).wait()

        @pl.when((j == 3) & (i == 0))
        def _():
            land(1, 2)
            land(1, 1)

        p_ref[...] = jnp.dot(a_ref[...], b_scr[j % 2], preferred_element_type=F32).astype(p_ref.dtype)

        @pl.when((j == 3) & (i == ni - 1))
        def _():
            settle(1, 2)
            settle(1, 1)
            land(1, 3)
            settle(1, 3)
            for o in range(2):
                for k in (2, 1):
                    send(o, k).wait_send()
                for core in range(2):
                    on_core(core, lambda o=o, core=core: relay(o, core).wait_send())

    hbm = pl.BlockSpec(memory_space=pl.ANY)
    grid_spec = pltpu.PrefetchScalarGridSpec(
        num_scalar_prefetch=1, grid=(4, ni),
        in_specs=[pl.BlockSpec((tm, D), lambda j, i, qr: (i, 0)), hbm, hbm],
        out_specs=[pl.BlockSpec((tm, nq), lambda j, i, qr: (i, jnp.bitwise_xor(qr[0], ((j & 1) << 1) | (j >> 1)))),
                   hbm, hbm],
        scratch_shapes=[pltpu.VMEM((2,) + win.shape, win.dtype), pltpu.VMEM(wout.shape, wout.dtype), pltpu.SemaphoreType.DMA((3,))]
        + [pltpu.SemaphoreType.DMA((2, 4)) for _ in range(4)])
    return pl.pallas_call(
        body, name="in_proj_gather", grid_spec=grid_spec,
        out_shape=[jax.ShapeDtypeStruct((rows, 4 * nq), BF16), jax.ShapeDtypeStruct((D, 4 * nq), win.dtype),
                   jax.ShapeDtypeStruct((4 * dq, D2), wout.dtype)],
        compiler_params=_params(("arbitrary", "arbitrary")),
    )(q_arr, hl, win, wout)


def _rs_to_sibling(gs, axes):
    n = len(gs)
    shapes = []
    for g, ax in zip(gs, axes):
        s = list(g.shape)
        s[ax] //= 8
        shapes.append(tuple(s))

    def body(*refs):
        g_refs, mine, landed = refs[:n], refs[n:2 * n], refs[2 * n:3 * n]
        bufs = refs[3 * n:4 * n]
        lsem, ssem, rsem = refs[4 * n:]
        x, y, c = _pos()
        cps = []
        for o in range(n):
            size = shapes[o][axes[o]]
            for j in range(4):
                rc = pltpu.make_async_remote_copy(
                    src_ref=_chunk(g_refs[o], axes[o], 2 * j + 1 - c, size), dst_ref=landed[o].at[j],
                    send_sem=ssem.at[o, j], recv_sem=rsem.at[o, j], device_id=(x, y, 1 - c), device_id_type=MESH_ID)
                rc.start()
                cps.append(rc)
        for o in range(n):
            size = shapes[o][axes[o]]
            for j in range(4):
                _bounce(_chunk(g_refs[o], axes[o], 2 * j + c, size), mine[o].at[j], bufs[o], lsem.at[o])
        for rc in cps:
            rc.wait()

    hbm = pl.BlockSpec(memory_space=pl.ANY)
    outs = [jax.ShapeDtypeStruct((4,) + s, g.dtype) for s, g in zip(shapes, gs)]
    res = pl.pallas_call(
        body, name="rs_to_sibling", in_specs=[hbm] * n, out_specs=[hbm] * (2 * n), out_shape=outs + outs,
        scratch_shapes=[pltpu.VMEM(s, g.dtype) for s, g in zip(shapes, gs)]
        + [pltpu.SemaphoreType.DMA((n,)), pltpu.SemaphoreType.DMA((n, 4)), pltpu.SemaphoreType.DMA((n, 4))],
        compiler_params=_params(),
    )(*gs)
    return res[:n], res[n:]


def _gather_block_side(block):
    r, n_cols = block.shape

    def copies(ins, outs, scr):
        ssem, rsem = scr[1], scr[2]
        x, y, c = _pos()
        return [pltpu.make_async_remote_copy(
            src_ref=ins[0], dst_ref=outs[0].at[k], send_sem=ssem.at[k], recv_sem=rsem.at[k],
            device_id=(_flip(x, (k >> 2) & 1), _flip(y, (k >> 1) & 1), _flip(c, k & 1)), device_id_type=MESH_ID)
            for k in range(1, 8)]

    def start(ins, outs, scr):
        for cp in copies(ins, outs, scr):
            cp.start()

    def finish(ins, outs, scr):
        _bounce(ins[0], outs[0].at[0], scr[0], scr[3].at[0])
        for cp in copies(ins, outs, scr):
            cp.wait()

    return dict(ins=[block], outs=[jax.ShapeDtypeStruct((8, r, n_cols), block.dtype)],
                scratch=[pltpu.VMEM((r, n_cols), block.dtype), pltpu.SemaphoreType.DMA((8,)), pltpu.SemaphoreType.DMA((8,)),
                         pltpu.SemaphoreType.DMA((1,))],
                start=start, finish=finish)


def _join_sides(a, b):
    na_i, na_o, na_s = len(a["ins"]), len(a["outs"]), len(a["scratch"])

    def run(which):
        def fn(ins, outs, scr):
            a[which](ins[:na_i], outs[:na_o], scr[:na_s])
            b[which](ins[na_i:], outs[na_o:], scr[na_s:])
        return fn

    return dict(ins=a["ins"] + b["ins"], outs=a["outs"] + b["outs"], scratch=a["scratch"] + b["scratch"],
                start=run("start"), finish=run("finish"))


def _rs_chips_side(parts):
    n = len(parts)

    def copies(p_refs, slots, scr):
        ssem, rsem = scr[n + 1], scr[n + 2]
        x, y, c = _pos()
        cps = []
        for o in range(n):
            for k in range(1, 4):
                px, py = _flip(x, (k >> 1) & 1), _flip(y, k & 1)
                cps.append(pltpu.make_async_remote_copy(
                    src_ref=p_refs[o].at[2 * px + py], dst_ref=slots[o].at[k], send_sem=ssem.at[o, k],
                    recv_sem=rsem.at[o, k], device_id=(px, py, c), device_id_type=MESH_ID))
        return cps

    def start(p_refs, slots, scr):
        for cp in copies(p_refs, slots, scr):
            cp.start()

    def finish(p_refs, slots, scr):
        x, y, _ = _pos()
        q = 2 * x + y
        for o in range(n):
            _bounce(p_refs[o].at[q], slots[o].at[0], scr[o], scr[n].at[o])
        for cp in copies(p_refs, slots, scr):
            cp.wait()

    return dict(
        ins=list(parts), outs=[jax.ShapeDtypeStruct(p.shape, p.dtype) for p in parts],
        scratch=[pltpu.VMEM(p.shape[1:], p.dtype) for p in parts]
        + [pltpu.SemaphoreType.DMA((n,)), pltpu.SemaphoreType.DMA((n, 4)), pltpu.SemaphoreType.DMA((n, 4))],
        start=start, finish=finish)


def _rs_share(rs, axes):
    n = len(rs)
    shapes = []
    for r, ax in zip(rs, axes):
        s = list(r.shape)
        s[ax] *= 2
        shapes.append(tuple(s))

    def body(*refs):
        r_refs, outs = refs[:n], refs[n:2 * n]
        bufs = refs[2 * n:3 * n]
        lsem, ssem, rsem = refs[3 * n:]
        x, y, c = _pos()
        cps = []
        for o in range(n):
            size = r_refs[o].shape[axes[o]]
            window = _chunk(outs[o], axes[o], c, size)
            rc = pltpu.make_async_remote_copy(src_ref=r_refs[o], dst_ref=window, send_sem=ssem.at[o], recv_sem=rsem.at[o],
                                              device_id=(x, y, 1 - c), device_id_type=MESH_ID)
            rc.start()
            cps.append(rc)
        for o in range(n):
            size = r_refs[o].shape[axes[o]]
            _bounce(r_refs[o], _chunk(outs[o], axes[o], c, size), bufs[o], lsem.at[o])
        for cp in cps:
            cp.wait()

    hbm = pl.BlockSpec(memory_space=pl.ANY)
    return pl.pallas_call(
        body, name="rs_share", in_specs=[hbm] * n, out_specs=[hbm] * n,
        out_shape=[jax.ShapeDtypeStruct(s, r.dtype) for s, r in zip(shapes, rs)],
        scratch_shapes=[pltpu.VMEM(r.shape, r.dtype) for r in rs] + [pltpu.SemaphoreType.DMA((n,)) for _ in range(3)],
        compiler_params=_params(),
    )(*rs)


def _rs_pair_sums(gs, axes):
    mine, landed = _rs_to_sibling(gs, axes)
    pair_sums = []
    for o, (mi, la) in enumerate(zip(mine, landed)):
        rows, cols = mi.shape[0] * mi.shape[1], mi.shape[2]
        s = _elementwise(lambda a, b: a.astype(F32) + b.astype(F32), [(mi.reshape(rows, cols), 0), (la.reshape(rows, cols), 0)],
                         [BF16], rows=rows, cols=cols, name=f"rs_pair_sum{o}")[0]
        pair_sums.append(s.reshape(mi.shape))
    return pair_sums


def _rs_finish(slots, axes):
    reduced = []
    for o, sl in enumerate(slots):
        rows, cols = sl.shape[1], sl.shape[2]
        flat = sl.reshape(4 * rows, cols)
        r = _elementwise(lambda a, b, c, d: (a.astype(F32) + b.astype(F32)) + (c.astype(F32) + d.astype(F32)),
                         [(flat, k * rows) for k in range(4)], [F32], rows=rows, cols=cols, name=f"rs_chip_sum{o}")[0]
        reduced.append(r)
    return _rs_share(reduced, axes)


def _norm_in(x, ctx, g, sc_l, sh_l, sc_c, sh_c, tr):
    L, D = x.shape
    T = ctx.shape[0]
    nx, nc = L // tr, T // tr

    def body(x_ref, c_ref, g_ref, scl, shl, scc, shc, o_ref):
        i = pl.program_id(0)

        def run(src, sc, sh):
            v = src[...]
            r = lax.rsqrt(jnp.mean(v * v, axis=-1, keepdims=True) + EPS)
            o_ref[...] = ((v * r * g_ref[...]) * (1.0 + sc[...]) + sh[...]).astype(o_ref.dtype)

        @pl.when(i < nx)
        def _():
            run(x_ref, scl, shl)

        @pl.when(i >= nx)
        def _():
            run(c_ref, scc, shc)

    vec = pl.BlockSpec((1, D), lambda i: (0, 0))
    return pl.pallas_call(
        body, name="norm_in", grid=(nx + nc,),
        in_specs=[pl.BlockSpec((tr, D), lambda i: (jnp.minimum(i, nx - 1), 0)),
                  pl.BlockSpec((tr, D), lambda i: (jnp.maximum(i - nx, 0), 0)), vec, vec, vec, vec, vec],
        out_specs=pl.BlockSpec((tr, D), lambda i: (i, 0)),
        out_shape=jax.ShapeDtypeStruct((L + T, D), BF16),
        compiler_params=_params(("arbitrary",)),
    )(x, ctx, g, sc_l, sh_l, sc_c, sh_c)


def _tmod(tl, row_w):
    assert row_w & (row_w - 1) == 0
    return lax.broadcasted_iota(jnp.int32, (tl, 1), 0) & (row_w - 1)


def _shift(z, k, tmod, row_w):
    tl = z.shape[0]
    rolled = pltpu.roll(z, k % tl, 0)
    mask = (tmod >= k) if k > 0 else (tmod < row_w + k)
    return jnp.where(mask, rolled, 0.0)


def _conv(z, w_ref, taps, left, tmod, row_w, lanes=slice(None)):
    out = None
    for j in range(taps):
        k = left - j
        term = (z if k == 0 else _shift(z, k, tmod, row_w)) * w_ref[j:j + 1, lanes]
        out = term if out is None else out + term
    return out


def _conv_bwd(dz, z, w_ref, taps, left, tmod, row_w, lanes=slice(None)):
    din = None
    dws = []
    for j in range(taps):
        k = left - j
        shifted = dz if k == 0 else _shift(dz, -k, tmod, row_w)
        term = shifted * w_ref[j:j + 1, lanes]
        din = term if din is None else din + term
        dws.append(jnp.sum(shifted * z, axis=0, keepdims=True))
    return din, dws


def _gate_matmul(xb16_ref, wd_ref, pre_scr, W, ng, gs):
    for g in range(ng):
        pg = jnp.dot(xb16_ref[:, g * gs:(g + 1) * gs], wd_ref[g], preferred_element_type=F32)
        pre_scr[:, g * gs:(g + 1) * gs] = pg[:, :gs]
        pre_scr[:, W + g * gs:W + (g + 1) * gs] = pg[:, gs:]


def _f32(ref, rows, lanes):
    return ref[rows, lanes].astype(F32)


def _sub_loop(tl, sub_r, W, fn):
    def chunk(ci, carry):
        r0 = pl.multiple_of(ci * sub_r, sub_r)
        for lb in range(W // LANES):
            fn(r0, lb * LANES)
        return carry

    lax.fori_loop(0, tl // sub_r, chunk, 0)


def _row_loop(tl, rev, step, init):
    nchunk = tl // SUBLANES

    def chunk(j, carry):
        jj = (nchunk - 1 - j) if rev else j
        c0 = pl.multiple_of(jj * SUBLANES, SUBLANES)
        for r in (range(SUBLANES - 1, -1, -1) if rev else range(SUBLANES)):
            carry = step(c0 + r, carry)
        return carry

    return lax.fori_loop(0, nchunk, chunk, init)


def _mix_fwd(P, d, h_init, wts, *, rows, row_off, row_w, tl, saved0=None, name):
    W = P.shape[1] // 6
    nt = rows // tl
    ob = row_off // tl
    rev = d == 1
    gs = min(LRU_GROUP, W)
    ng = W // gs
    wca, wcb, bcb = wts["wca"], wts["wcb"], wts["bcb"]
    wd, ba, bx, lam = wts["wd"][d], wts["ba"][d], wts["bx"][d], wts["lam"][d]

    def tile(i):
        return (nt - 1 - i) if rev else i

    def pcol(j):
        return pl.BlockSpec((tl, W), lambda i: (tile(i) + ob, j))

    vec = pl.BlockSpec((1, W), lambda i: (0, 0))
    taps = pl.BlockSpec((SUBLANES, W), lambda i: (0, 0))
    wd_spec = pl.BlockSpec(wd.shape, lambda i: (0, 0, 0))
    seq = pl.BlockSpec((tl, W), lambda i: (tile(i), 0))

    sub_r = min(row_w, tl)
    assert tl % sub_r == 0

    def body(*refs):
        if rev:
            (bl, cl, ul, gl, ql, ho, xb_r, xb16_r, wca_r, wd_r, ba_r, bx_r, lam_r, hin, hseq, cat, a_o, r_o, ig_o, m2_o,
             b_scr, pre_scr, carry, sp_scr) = refs
        else:
            (vl, wcb_r, bcb_r, wd_r, ba_r, bx_r, lam_r, hin, hseq, xb_r, xb16_r, a_o, r_o, ig_o, m2_o,
             b_scr, pre_scr, carry, sp_scr) = refs
        i = pl.program_id(0)

        @pl.when(i == 0)
        def _():
            carry[...] = hin[...]

        sp_scr[...] = _softplus(-lam_r[...])
        tmod = _tmod(sub_r, sub_r)

        def conv_in(r0, l0):
            rs, ls = pl.ds(r0, sub_r), pl.ds(l0, LANES)
            xb = _conv(_f32(vl, rs, ls), wcb_r, 4, 2, tmod, sub_r, ls) + bcb_r[:, ls]
            xb_r[rs, ls] = xb
            xb16_r[rs, ls] = xb.astype(BF16)

        def gates(r0, l0):
            rs, ls = pl.ds(r0, sub_r), pl.ds(l0, LANES)
            r, ig, a, m2 = _gates(pre_scr[rs, ls] + ba_r[:, ls], pre_scr[rs, pl.ds(W + l0, LANES)] + bx_r[:, ls],
                                  sp_scr[:, ls])
            a_o[rs, ls] = a
            r_o[rs, ls] = r.astype(r_o.dtype)
            ig_o[rs, ls] = ig.astype(ig_o.dtype)
            m2_o[rs, ls] = m2.astype(m2_o.dtype)
            m = jnp.where(m2 > 0.0, m2 * lax.rsqrt(m2), 0.0)
            b_scr[rs, ls] = m * (ig * xb_r[rs, ls])

        if not rev:
            _sub_loop(tl, sub_r, W, conv_in)
        _gate_matmul(xb16_r, wd_r, pre_scr, W, ng, gs)
        _sub_loop(tl, sub_r, W, gates)

        def step(t, h):
            h = a_o[pl.ds(t, 1), :] * h + b_scr[pl.ds(t, 1), :]
            hseq[pl.ds(t, 1), :] = h
            return h

        carry[...] = _row_loop(tl, rev, step, carry[...])

        if rev:
            def mix_out(r0, l0):
                rs, ls = pl.ds(r0, sub_r), pl.ds(l0, LANES)
                yb = (ho[rs, ls] + hseq[rs, ls]) * _silu(_f32(ql, rs, ls))
                ya = (_f32(bl, rs, ls) * _conv(_f32(cl, rs, ls) * _f32(ul, rs, ls), wca_r, 3, 1, tmod, sub_r, ls)
                      * _silu(_f32(gl, rs, ls)))
                cat[rs, ls] = ya.astype(cat.dtype)
                cat[rs, pl.ds(W + l0, LANES)] = yb.astype(cat.dtype)

            _sub_loop(tl, sub_r, W, mix_out)

    scratch = [pltpu.VMEM((tl, W), F32), pltpu.VMEM((tl, 2 * W), F32), pltpu.VMEM((1, W), F32), pltpu.VMEM((1, W), F32)]
    f32_seq = jax.ShapeDtypeStruct((rows, W), F32)
    kept_gates = [f32_seq] + [jax.ShapeDtypeStruct((rows, W), BF16)] * 3
    if rev:
        in_specs = [pcol(j) for j in (0, 1, 2, 3, 5)] + [seq, seq, seq, taps, wd_spec, vec, vec, vec, vec]
        args = [P] * 5 + [saved0["h"], saved0["xb"], saved0["xb16"], wca, wd, ba, bx, lam, h_init]
        out_specs = [seq, pl.BlockSpec((tl, 2 * W), lambda i: (tile(i), 0))] + [seq] * 4
        out_shape = [f32_seq, jax.ShapeDtypeStruct((rows, 2 * W), BF16)] + kept_gates
    else:
        in_specs = [pcol(4), taps, vec, wd_spec, vec, vec, vec, vec]
        args = [P, wcb, bcb, wd, ba, bx, lam, h_init]
        out_specs = [seq] * 7
        out_shape = [f32_seq, f32_seq, jax.ShapeDtypeStruct((rows, W), BF16)] + kept_gates
    res = pl.pallas_call(
        body, name=name, grid=(nt,), in_specs=in_specs, out_specs=out_specs, out_shape=out_shape,
        scratch_shapes=scratch, compiler_params=_params(("arbitrary",)),
    )(*args)
    gates = dict(zip(("a", "r", "ig", "m2"), res[-4:]))
    if rev:
        return res[0], res[1], gates
    return dict(h=res[0], xb=res[1], xb16=res[2]), gates


_BWD_SCRATCH = ("dyl", "g", "dp16", "sp", "dlf", "edge", "c")
_FWD_SAVED = ("xb", "xb16", "a", "r", "ig", "m2")


def _bwd_scratch(tl, W):
    shapes = {"dyl": pltpu.VMEM((tl, W), F32), "g": pltpu.VMEM((tl, W), F32), "dp16": pltpu.VMEM((tl, 2 * W), BF16),
              "sp": pltpu.VMEM((1, W), F32), "dlf": pltpu.VMEM((1, W), F32), "edge": pltpu.VMEM((1, W), F32),
              "c": pltpu.VMEM((1, W), F32)}
    return [shapes[n] for n in _BWD_SCRATCH]


def _lru_bwd_tile(d, dy_fn, hs_ref, wd_r, lam_r, scr, acc, first, tl, sub_r, ng, gs):
    dwd_ref, dba_ref, dbx_ref, dlam_ref = acc
    W = hs_ref.shape[1]
    assert gs % LANES == 0
    rev = d == 0
    lam = lam_r[...]
    scr["sp"][...] = _softplus(-lam)
    scr["dlf"][...] = -_sigmoid(-lam)

    @pl.when(first)
    def _():
        dwd_ref[...] = jnp.zeros_like(dwd_ref)
        dba_ref[...] = jnp.zeros_like(dba_ref)
        dbx_ref[...] = jnp.zeros_like(dbx_ref)
        dlam_ref[...] = jnp.zeros_like(dlam_ref)

    def state_grad(r0, l0):
        rs, ls = pl.ds(r0, sub_r), pl.ds(l0, LANES)
        scr["dyl"][rs, ls] = dy_fn(rs, ls)

    _sub_loop(tl, sub_r, W, state_grad)

    def step(t, c):
        g = scr["dyl"][pl.ds(t, 1), :] + c
        scr["g"][pl.ds(t, 1), :] = g
        return scr["a"][pl.ds(t, 1), :] * g

    scr["c"][...] = _row_loop(tl, rev, step, scr["c"][...])
    row = lax.broadcasted_iota(jnp.int32, (sub_r, 1), 0)

    def grads(r0, l0):
        rs, ls = pl.ds(r0, sub_r), pl.ds(l0, LANES)
        g, a, m2 = scr["g"][rs, ls], scr["a"][rs, ls], _f32(scr["m2"], rs, ls)
        r, ig, xb = _f32(scr["r"], rs, ls), _f32(scr["ig"], rs, ls), scr["xb"][rs, ls]
        h = hs_ref[rs, ls]
        if d == 0:
            e0 = pl.multiple_of(jnp.maximum(r0 - SUBLANES, 0), SUBLANES)
            edge = jnp.where(r0 == 0, scr["edge"][:, ls], hs_ref[pl.ds(e0, SUBLANES), ls][SUBLANES - 1:, :])
            hprev = jnp.where(row == 0, edge, pltpu.roll(h, 1, 0))
        else:
            e0 = pl.multiple_of(jnp.minimum(r0 + sub_r, tl - SUBLANES), SUBLANES)
            edge = jnp.where(r0 == tl - sub_r, scr["edge"][:, ls], hs_ref[pl.ds(e0, SUBLANES), ls][:1, :])
            hprev = jnp.where(row == sub_r - 1, edge, pltpu.roll(h, sub_r - 1, 0))
        rsq = lax.rsqrt(m2)
        gm = g * (m2 * rsq)
        d_la = (g * hprev) * a - (g * (ig * xb)) * ((1.0 - m2) * rsq)
        d_pr = d_la * ((-LRU_C) * scr["sp"][:, ls]) * (r * (1.0 - r))
        d_pi = (gm * xb) * (ig * (1.0 - ig))
        scr["dyl"][rs, ls] = gm * ig
        dlam_ref[:, ls] += jnp.sum(d_la * ((-LRU_C) * r), axis=0, keepdims=True) * scr["dlf"][:, ls]
        dba_ref[:, ls] += jnp.sum(d_pr, axis=0, keepdims=True)
        dbx_ref[:, ls] += jnp.sum(d_pi, axis=0, keepdims=True)
        gi, off = divmod(l0, gs)
        scr["dp16"][rs, pl.ds(gi * 2 * gs + off, LANES)] = d_pr.astype(BF16)
        scr["dp16"][rs, pl.ds(gi * 2 * gs + gs + off, LANES)] = d_pi.astype(BF16)

    _sub_loop(tl, sub_r, W, grads)
    for gi in range(ng):
        dp = scr["dp16"][:, gi * 2 * gs:(gi + 1) * 2 * gs]
        scr["g"][:, gi * gs:(gi + 1) * gs] = lax.dot_general(dp, wd_r[gi], (((1,), (1,)), ((), ())),
                                                             preferred_element_type=F32)
        dwd_ref[gi] += lax.dot_general(scr["xb16"][:, gi * gs:(gi + 1) * gs], dp, (((0,), (0,)), ((), ())),
                                       preferred_element_type=F32)


def _edge_block(h, tl, nt, d):
    W = h.shape[1]
    per = tl // SUBLANES
    if d == 0:
        return pl.BlockSpec((SUBLANES, W), lambda i: (jnp.maximum((nt - 1 - i) * per - 1, 0), 0))
    return pl.BlockSpec((SUBLANES, W), lambda i: (jnp.minimum((i + 1) * per, nt * per - 1), 0))


def _mix_bwd0(P, dcat, saved0, gates0, h_init, c_init, wts, *, rows, row_off, row_w, tl, name):
    W = P.shape[1] // 6
    nt = rows // tl
    ob = row_off // tl
    gs = min(LRU_GROUP, W)
    ng = W // gs
    wd, lam = wts["wd"][0], wts["lam"][0]
    h0s = saved0["h"]
    kept = [saved0["xb"], saved0["xb16"]] + [gates0[n] for n in ("a", "r", "ig", "m2")]

    def tile(i):
        return nt - 1 - i

    vec = pl.BlockSpec((1, W), lambda i: (0, 0))
    wd_spec = pl.BlockSpec(wd.shape, lambda i: (0, 0, 0))
    seq = pl.BlockSpec((tl, W), lambda i: (tile(i), 0))

    sub_r = min(row_w, tl)
    assert tl % sub_r == 0

    def body(ql, dyb, hs, hedge8, xb_r, xb16_r, a_r, r_r, ig_r, m2_r, wd_r, lam_r, hin, cin,
             dxb_o, dwd_o, dba_o, dbx_o, dlam_o, cfin, *scratch):
        scr = dict(zip(_BWD_SCRATCH, scratch, strict=True))
        scr.update(zip(_FWD_SAVED, (xb_r, xb16_r, a_r, r_r, ig_r, m2_r), strict=True))
        i = pl.program_id(0)

        @pl.when(i == 0)
        def _():
            scr["c"][...] = cin[...]

        scr["edge"][...] = jnp.where(i == nt - 1, hin[...], hedge8[SUBLANES - 1:SUBLANES, :])
        _lru_bwd_tile(0, lambda rs, ls: _f32(dyb, rs, ls) * _silu(_f32(ql, rs, ls)), hs, wd_r, lam_r, scr,
                      (dwd_o, dba_o, dbx_o, dlam_o), i == 0, tl, sub_r, ng, gs)
        dxb_o[...] = scr["dyl"][...] + scr["g"][...]
        cfin[...] = scr["c"][...]

    return pl.pallas_call(
        body, name=name, grid=(nt,),
        in_specs=[pl.BlockSpec((tl, W), lambda i: (tile(i) + ob, 5)), pl.BlockSpec((tl, W), lambda i: (tile(i), 1)), seq,
                  _edge_block(h0s, tl, nt, 0)] + [seq] * 6 + [wd_spec, vec, vec, vec],
        out_specs=[seq, wd_spec, vec, vec, vec, vec],
        out_shape=[jax.ShapeDtypeStruct((rows, W), F32), jax.ShapeDtypeStruct(wd.shape, F32)]
        + [jax.ShapeDtypeStruct((1, W), F32)] * 4,
        scratch_shapes=_bwd_scratch(tl, W),
        compiler_params=_params(("arbitrary",)),
    )(P, dcat, h0s, h0s, *kept, wd, lam, h_init, c_init)


def _mix_bwd1(P, dcat, saved0, h1s, gates1, dxb0, h_init, c_init, wts, *, rows, row_off, row_w, tl, name,
              dp_rows=None, dp_off=0, dp_into=None):
    dp_rows = rows if dp_rows is None else dp_rows
    dpb = dp_off // tl
    W = P.shape[1] // 6
    nt = rows // tl
    ob = row_off // tl
    gs = min(LRU_GROUP, W)
    ng = W // gs
    wca, wcb = wts["wca"], wts["wcb"]
    wd, lam = wts["wd"][1], wts["lam"][1]
    h0s = saved0["h"]
    kept = [saved0["xb"], saved0["xb16"]] + [gates1[n] for n in ("a", "r", "ig", "m2")]

    vec = pl.BlockSpec((1, W), lambda i: (0, 0))
    taps = pl.BlockSpec((SUBLANES, W), lambda i: (0, 0))
    wd_spec = pl.BlockSpec(wd.shape, lambda i: (0, 0, 0))
    seq = pl.BlockSpec((tl, W), lambda i: (i, 0))

    sub_r = min(row_w, tl)
    assert tl % sub_r == 0

    def body(*refs):
        if dp_into is not None:
            refs = refs[1:]
        (bl, cl, ul, gl, vl, ql, dya, dyb, h0, h1, hedge8, dx0, xb_r, xb16_r, a_r, r_r, ig_r, m2_r, wca_r, wcb_r,
         wd_r, lam_r, hin, cin, dp_o, dwd_o, dba_o, dbx_o, dlam_o, dwca_o, dwcb_o, dbcb_o, cfin, *scratch) = refs
        scr = dict(zip(_BWD_SCRATCH, scratch, strict=True))
        scr.update(zip(_FWD_SAVED, (xb_r, xb16_r, a_r, r_r, ig_r, m2_r), strict=True))
        i = pl.program_id(0)

        @pl.when(i == 0)
        def _():
            scr["c"][...] = cin[...]
            dwca_o[...] = jnp.zeros_like(dwca_o)
            dwcb_o[...] = jnp.zeros_like(dwcb_o)
            dbcb_o[...] = jnp.zeros_like(dbcb_o)

        scr["edge"][...] = jnp.where(i == nt - 1, hin[...], hedge8[0:1, :])
        _lru_bwd_tile(1, lambda rs, ls: _f32(dyb, rs, ls) * _silu(_f32(ql, rs, ls)), h1, wd_r, lam_r, scr,
                      (dwd_o, dba_o, dbx_o, dlam_o), i == 0, tl, sub_r, ng, gs)
        cfin[...] = scr["c"][...]
        tmod = _tmod(sub_r, sub_r)

        def rest(r0, l0):
            rs, ls = pl.ds(r0, sub_r), pl.ds(l0, LANES)
            dxb = dx0[rs, ls] + scr["dyl"][rs, ls] + scr["g"][rs, ls]
            dv, dwb = _conv_bwd(dxb, _f32(vl, rs, ls), wcb_r, 4, 2, tmod, sub_r, ls)
            for j in range(4):
                dwcb_o[j:j + 1, ls] += dwb[j]
            dbcb_o[:, ls] += jnp.sum(dxb, axis=0, keepdims=True)
            q = _f32(ql, rs, ls)
            sq = _sigmoid(q)
            dq = _f32(dyb, rs, ls) * (h0[rs, ls] + h1[rs, ls]) * (sq * (1.0 + q * (1.0 - sq)))
            b_, c_, u_, g_ = _f32(bl, rs, ls), _f32(cl, rs, ls), _f32(ul, rs, ls), _f32(gl, rs, ls)
            z = c_ * u_
            cz = _conv(z, wca_r, 3, 1, tmod, sub_r, ls)
            sgm = _sigmoid(g_)
            sg = g_ * sgm
            da = _f32(dya, rs, ls)
            dz, dwa = _conv_bwd(da * b_ * sg, z, wca_r, 3, 1, tmod, sub_r, ls)
            for j in range(3):
                dwca_o[j:j + 1, ls] += dwa[j]
            parts = (da * cz * sg, dz * u_, dz * c_, da * b_ * cz * (sgm * (1.0 + g_ * (1.0 - sgm))), dv, dq)
            for k, val in enumerate(parts):
                dp_o[rs, pl.ds(k * W + l0, LANES)] = val.astype(dp_o.dtype)

        _sub_loop(tl, sub_r, W, rest)

    def pcol(j):
        return pl.BlockSpec((tl, W), lambda i: (i + ob, j))

    prev = [] if dp_into is None else [dp_into]
    return pl.pallas_call(
        body, name=name, grid=(nt,), input_output_aliases={} if dp_into is None else {0: 0},
        in_specs=[pl.BlockSpec(memory_space=pl.ANY)] * len(prev) + [pcol(j) for j in range(6)]
        + [pl.BlockSpec((tl, W), lambda i: (i, 0)), pl.BlockSpec((tl, W), lambda i: (i, 1)), seq, seq,
           _edge_block(h1s, tl, nt, 1), seq] + [seq] * 6 + [taps, taps, wd_spec, vec, vec, vec],
        out_specs=[pl.BlockSpec((tl, 6 * W), lambda i: (i + dpb, 0)), wd_spec, vec, vec, vec, taps, taps, vec, vec],
        out_shape=[jax.ShapeDtypeStruct((dp_rows, 6 * W), BF16), jax.ShapeDtypeStruct(wd.shape, F32)]
        + [jax.ShapeDtypeStruct((1, W), F32)] * 3
        + [jax.ShapeDtypeStruct((SUBLANES, W), F32)] * 2 + [jax.ShapeDtypeStruct((1, W), F32)] * 2,
        scratch_shapes=_bwd_scratch(tl, W),
        compiler_params=_params(("arbitrary",)),
    )(*prev, *([P] * 6), dcat, dcat, h0s, h1s, h1s, dxb0, *kept, wca, wcb, wd, lam, h_init, c_init)


def _loss_head(out, x, tgt, gt, fg, tr):
    L, D = x.shape

    def body(o_ref, x_ref, t_ref, gt_ref, fg_ref, dn_o, do_o, dfg_o, dgt_o, loss_o):
        i = pl.program_id(0)

        @pl.when(i == 0)
        def _():
            dfg_o[...] = jnp.zeros_like(dfg_o)
            dgt_o[...] = jnp.zeros_like(dgt_o)
            loss_o[...] = jnp.zeros_like(loss_o)

        o = o_ref[...].astype(F32)
        gt_v = gt_ref[...]
        fg_v = fg_ref[...]
        n = x_ref[...] + gt_v * o
        r = lax.rsqrt(jnp.mean(n * n, axis=-1, keepdims=True) + EPS)
        nr = n * r
        e = nr * fg_v - t_ref[...]
        loss_o[...] += 0.5 * jnp.sum(jnp.mean(e * e, axis=-1, keepdims=True))
        dy = e * (1.0 / D)
        dfg_o[...] += jnp.sum(dy * nr, axis=0, keepdims=True)
        qv = dy * fg_v
        dn = r * (qv - nr * jnp.mean(qv * nr, axis=-1, keepdims=True))
        dgt_o[...] += jnp.sum(dn * o, axis=0, keepdims=True)
        dn_o[...] = dn.astype(dn_o.dtype)
        do_o[...] = (dn * gt_v).astype(do_o.dtype)

    blk = pl.BlockSpec((tr, D), lambda i: (i, 0))
    vec = pl.BlockSpec((1, D), lambda i: (0, 0))
    return pl.pallas_call(
        body, name="loss_head", grid=(L // tr,), in_specs=[blk, blk, blk, vec, vec],
        out_specs=[blk, blk, vec, vec, pl.BlockSpec((SUBLANES, LANES), lambda i: (0, 0))],
        out_shape=[jax.ShapeDtypeStruct((L, D), BF16), jax.ShapeDtypeStruct((L, D), BF16),
                   jax.ShapeDtypeStruct((1, D), F32), jax.ShapeDtypeStruct((1, D), F32),
                   jax.ShapeDtypeStruct((SUBLANES, LANES), F32)],
        compiler_params=_params(("arbitrary",)),
    )(out, x, tgt, gt, fg)


def _norm_bwd(dhl, x, dn, g, sc, tr, name):
    L, D = x.shape
    with_x = dn is not None

    def body(*refs):
        if with_x:
            d_ref, x_ref, dn_ref, g_ref, sc_ref, gx_o, dsh_o, dsc_o, dg_o = refs
        else:
            d_ref, x_ref, g_ref, sc_ref, dsh_o, dsc_o, dg_o = refs
        i = pl.program_id(0)

        @pl.when(i == 0)
        def _():
            dsh_o[...] = jnp.zeros_like(dsh_o)
            dsc_o[...] = jnp.zeros_like(dsc_o)
            dg_o[...] = jnp.zeros_like(dg_o)

        d = d_ref[...].astype(F32)
        xv = x_ref[...]
        g_v = g_ref[...]
        r = lax.rsqrt(jnp.mean(xv * xv, axis=-1, keepdims=True) + EPS)
        xr = xv * r
        dsh_o[...] += jnp.sum(d, axis=0, keepdims=True)
        dsc_o[...] += jnp.sum(d * (xr * g_v), axis=0, keepdims=True)
        dxn = d * (1.0 + sc_ref[...])
        dg_o[...] += jnp.sum(dxn * xr, axis=0, keepdims=True)
        if with_x:
            qv = dxn * g_v
            gx_o[...] = r * (qv - xr * jnp.mean(qv * xr, axis=-1, keepdims=True)) + dn_ref[...].astype(F32)

    blk = pl.BlockSpec((tr, D), lambda i: (i, 0))
    vec = pl.BlockSpec((1, D), lambda i: (0, 0))
    vshape = jax.ShapeDtypeStruct((1, D), F32)
    res = pl.pallas_call(
        body, name=name, grid=(L // tr,),
        in_specs=[blk, blk] + ([blk] if with_x else []) + [vec, vec],
        out_specs=([blk] if with_x else []) + [vec, vec, vec],
        out_shape=([jax.ShapeDtypeStruct((L, D), F32)] if with_x else []) + [vshape] * 3,
        compiler_params=_params(("arbitrary",)),
    )(*([dhl, x] + ([dn] if with_x else []) + [g, sc]))
    return res if with_x else [None] + list(res)


def _pack_blockdiag(wa, wx, gs):
    H, hd, _ = wa.shape
    hp = gs // hd
    ng = H // hp
    on_diag = _diag_mask(gs, hd)

    def bd(w):
        return jnp.where(on_diag, jnp.tile(w.reshape(ng, gs, hd), (1, 1, hp)), 0.0)

    return jnp.concatenate([bd(wa), bd(wx)], axis=-1).astype(BF16)


def _diag_mask(gs, hd):
    idx = jnp.arange(gs) // hd
    return idx[:, None] == idx[None, :]


def _unpack_blockdiag(dwd, H, hd, gs):
    hp = gs // hd
    on_diag = _diag_mask(gs, hd)

    def diag(dm):
        kept = jnp.where(on_diag, dm, 0.0)
        return sum(kept[:, :, p * hd:(p + 1) * hd] for p in range(hp)).reshape(H, hd, hd)

    return diag(dwd[:, :, :gs]), diag(dwd[:, :, gs:])


def kernel(x, c, ctx, c_ctx, norm_g, w_ada, b_ada, w_in, w_conv_a, w_conv_b, b_conv_b, lru_wa, lru_ba, lru_wx, lru_bx, lru_lambda, w_out, final_g, loss_target, m_c_ctx, m_norm_g, m_w_ada, m_b_ada, m_w_in, m_w_conv_a, m_w_conv_b, m_b_conv_b, m_lru_wa, m_lru_ba, m_lru_wx, m_lru_bx, m_lru_lambda, m_w_out, m_final_g, v_c_ctx, v_norm_g, v_w_ada, v_b_ada, v_w_in, v_w_conv_a, v_w_conv_b, v_b_conv_b, v_lru_wa, v_lru_ba, v_lru_wx, v_lru_bx, v_lru_lambda, v_w_out, v_final_g):
    xi, yi, ci = _pos()
    me = 4 * xi + 2 * yi + ci
    q = 2 * xi + yi
    first_core = (ci == 0).astype(F32)

    L, D = x.shape[1], x.shape[2]
    T = ctx.shape[1]
    W = D // 2
    Wq = W // 4
    H, hd = lru_wa.shape[2], lru_wa.shape[3]
    gs = min(LRU_GROUP, W)
    nq = w_ada.shape[2]
    tl = min(256, T, L)
    tr = min(256, T, L)
    x2, ctx2, tgt2 = x[0], ctx[0], loss_target[0]

    def place(shard, full_cols):
        z = jnp.zeros((shard.shape[0], full_cols), F32)
        return lax.dynamic_update_slice(z, shard * first_core, (0, q * shard.shape[1]))

    c_rows = lax.dynamic_update_slice(jnp.zeros((8, D), F32), c, (me, 0))
    small_in = [c_rows, place(w_conv_a[0], W), place(w_conv_b[0], W), place(lru_ba[0], W), place(lru_bx[0], W),
                place(lru_lambda[0], W)]
    small_shapes = [a.shape for a in small_in]
    gathered = _allreduce8(_pack(small_in, 8 * SUBLANES), "gather_small")
    c_all, wca, wcb, ba_all, bx_all, lam_all = _unpack(gathered, small_shapes)

    s_rows = jnp.concatenate([c_all, c_ctx[None, :], jnp.zeros((7, D), F32)], axis=0)
    mod_part = _matmul(s_rows, w_ada[0], a_act="silu", bias=lax.dynamic_slice(b_ada, (0, q * nq), (1, nq)),
                       tm=16, tn=nq, tk=512, name="ada_fwd")
    mod_all = _allreduce8(_pack([place(mod_part[:9], 4 * nq)], 8 * SUBLANES), "gather_mod")
    mod_all = _unpack(mod_all, [(9, 4 * nq)])[0]
    mod_l = lax.dynamic_slice(mod_all, (me, 0), (1, 3 * D))
    mod_c = mod_all[8:9]
    sh_l, sc_l, gt_l = mod_l[:, :D], mod_l[:, D:2 * D], mod_l[:, 2 * D:]
    sh_c, sc_c = mod_c[:, :D], mod_c[:, D:2 * D]

    pad_taps = lambda w: jnp.pad(w, ((0, SUBLANES - w.shape[0]), (0, 0)))
    wts = {
        "wca": pad_taps(wca), "wcb": pad_taps(wcb), "bcb": b_conv_b,
        "wd": [_pack_blockdiag(lru_wa[0, d], lru_wx[0, d], gs) for d in range(2)],
        "ba": [ba_all[d:d + 1] for d in range(2)], "bx": [bx_all[d:d + 1] for d in range(2)],
        "lam": [lam_all[d:d + 1] for d in range(2)],
    }

    hl = _norm_in(x2, ctx2, norm_g, sc_l, sh_l, sc_c, sh_c, tr)
    p_lat, win_full, wout_full = _in_proj_gather(hl, w_in[0].astype(BF16), w_out[0].astype(BF16),
                                                 jnp.reshape(q, (1,)).astype(jnp.int32), rows=L, tm=min(1024, L))
    p_ctx = _matmul(hl, win_full, a_rows=T, a_off=L, tm=T, tn=1536, tk=D, out_dtype=BF16, name="in_proj_ctx")
    zero_w = jnp.zeros((1, W), F32)
    ctx0, cgates0 = _mix_fwd(p_ctx, 0, zero_w, wts, rows=T, row_off=0, row_w=T, tl=tl, name="ctx_fwd0")
    c1s, _, cgates1 = _mix_fwd(p_ctx, 1, zero_w, wts, rows=T, row_off=0, row_w=T, tl=tl, saved0=ctx0, name="ctx_fwd1")
    h0_init, h1_init = ctx0["h"][T - 1:T], c1s[0:1]
    tl_tall = 2 * tl if L % (2 * tl) == 0 else tl
    lat0, gates0 = _mix_fwd(p_lat, 0, h0_init, wts, rows=L, row_off=0, row_w=GRID_W, tl=tl_tall, name="mix_fwd0")
    h1s, cat, gates1 = _mix_fwd(p_lat, 1, h1_init, wts, rows=L, row_off=0, row_w=GRID_W, tl=tl_tall, saved0=lat0,
                                name="mix_fwd1")
    out = _matmul(cat, wout_full, tm=1024, tn=D, tk=2 * W, out_dtype=BF16, name="out_proj")
    tr_lat = 2 * tr if L % (2 * tr) == 0 else tr
    dn, dout, dfg, dgt, loss_blk = _loss_head(out, x2, tgt2, gt_l, final_g[None, :], tr_lat)

    dcat = _matmul(dout, wout_full, tb=True, tm=1024, tn=2 * W, tk=D, out_dtype=BF16, name="out_proj_bwd")
    gw_out = _matmul(cat, dout, ta=True, tm=1024, tn=D, tk=2048, out_dtype=BF16, name="w_out_grad")
    dxb0, dwd0, dba0, dbx0, dlam0, ch0 = _mix_bwd0(p_lat, dcat, lat0, gates0, h0_init, zero_w, wts, rows=L, row_off=0,
                                                   row_w=GRID_W, tl=tl_tall, name="mix_bwd0")
    dp_lat, dwd1, dba1, dbx1, dlam1, dwca, dwcb, dbcb, ch1 = _mix_bwd1(
        p_lat, dcat, lat0, h1s, gates1, dxb0, h1_init, zero_w, wts, rows=L, row_off=0, row_w=GRID_W, tl=tl,
        name="mix_bwd1", dp_rows=L + T)
    zero_cat = jnp.zeros((T, 2 * W), BF16)
    cxb0, cwd0, cba0, cbx0, clam0, _ = _mix_bwd0(p_ctx, zero_cat, ctx0, cgates0, zero_w, ch0, wts, rows=T, row_off=0,
                                                 row_w=T, tl=tl, name="ctx_bwd0")
    dp, cwd1, cba1, cbx1, clam1, cwca, cwcb, cbcb, _ = _mix_bwd1(
        p_ctx, zero_cat, ctx0, c1s, cgates1, cxb0, zero_w, ch1, wts, rows=T, row_off=0, row_w=T, tl=tl, name="ctx_bwd1",
        dp_rows=L + T, dp_off=L, dp_into=dp_lat)

    gw_in = _matmul(hl, dp, ta=True, tm=1024, tn=1536, tk=2816, out_dtype=BF16, name="w_in_grad")
    rs_axes = [1, 0]
    pair_sums = _rs_pair_sums([gw_in, gw_out], rs_axes)
    dhc = _matmul(dp, win_full, tb=True, a_rows=T, a_off=L, tm=T, tn=D, tk=3072, name="in_proj_bwd_ctx")
    _, dsh_c, dsc_c, dng_c = _norm_bwd(dhc, ctx2, None, norm_g, sc_c, tr, "norm_bwd_ctx")
    zeros_d = jnp.zeros((1, D), F32)
    dmod_c = jnp.concatenate([dsh_c, dsc_c, zeros_d], axis=1)
    dhl, (*rs_slots, dmod_c_all) = _matmul(
        dp, win_full, tb=True, a_rows=L, tm=512, tn=D, tk=6 * W, out_dtype=BF16, name="in_proj_bwd",
        side=_join_sides(_rs_chips_side(pair_sums), _gather_block_side(jnp.pad(dmod_c, ((0, SUBLANES - 1), (0, 0))))))
    gx, dsh_l, dsc_l, dng_l = _norm_bwd(dhl, x2, dn, norm_g, sc_l, tr_lat, "norm_bwd")
    gc_rows = _matmul(lax.dynamic_slice(dmod_c_all.reshape(8 * SUBLANES, 3 * D), (0, q * nq), (8 * SUBLANES, nq)), w_ada[0],
                      tb=True, dsilu_mul=c_ctx[None, :], tm=8 * SUBLANES, tn=D, tk=512, name="c_ctx_grad")
    gc_part = jnp.sum(gc_rows, axis=0, keepdims=True) * first_core

    g_in_shard, g_out_shard = _rs_finish(rs_slots, rs_axes)

    dwa0, dwx0 = _unpack_blockdiag(dwd0 + cwd0, H, hd, gs)
    dwa1, dwx1 = _unpack_blockdiag(dwd1 + cwd1, H, hd, gs)
    dmod_l = jnp.concatenate([dsh_l, dsc_l, dgt], axis=1)
    small_g = [
        lax.dynamic_update_slice(jnp.zeros((8, 3 * D), F32), dmod_l, (me, 0)), dmod_c,
        dfg, dng_l + dng_c, (dwca + cwca)[:3], (dwcb + cwcb)[:4], dbcb + cbcb,
        jnp.stack([dwa0, dwa1]), jnp.stack([dwx0, dwx1]),
        jnp.concatenate([dba0 + cba0, dba1 + cba1], axis=0), jnp.concatenate([dbx0 + cbx0, dbx1 + cbx1], axis=0),
        jnp.concatenate([dlam0 + clam0, dlam1 + clam1], axis=0), loss_blk[0:1, 0:1], gc_part,
    ]
    g_shapes = [a.shape for a in small_g]
    (g_rows, g_modc, g_fg, g_ng, g_wca, g_wcb, g_bcb, g_wa, g_wx, g_ba, g_bx, g_lam, loss_sum, g_c_ctx) = _unpack(
        _allreduce8_two_level(_pack(small_g, 8 * SUBLANES), "reduce_small"), g_shapes)

    g_mod = jnp.concatenate([g_rows, g_modc, jnp.zeros((7, 3 * D), F32)], axis=0)
    g_mod_q = lax.dynamic_slice(g_mod, (0, q * nq), (16, nq))
    g_w_ada = _matmul(s_rows, g_mod_q, ta=True, a_act="silu", tm=1024, tn=nq, tk=16, name="w_ada_grad")
    g_b_ada = jnp.sum(g_mod[:9], axis=0, keepdims=True)

    def shard_cols(a, width):
        return lax.dynamic_slice(a, (0, q * width), (a.shape[0], width))

    grads = {
        "c_ctx": g_c_ctx, "norm_g": g_ng, "b_ada": g_b_ada,
        "w_conv_a": shard_cols(g_wca, Wq)[None], "w_conv_b": shard_cols(g_wcb, Wq)[None], "b_conv_b": g_bcb,
        "lru_wa": g_wa[None], "lru_ba": shard_cols(g_ba, Wq)[None], "lru_wx": g_wx[None],
        "lru_bx": shard_cols(g_bx, Wq)[None], "lru_lambda": shard_cols(g_lam, Wq)[None], "final_g": g_fg[0],
    }
    small_names = list(grads)
    given = dict(c_ctx=(c_ctx, m_c_ctx, v_c_ctx), norm_g=(norm_g, m_norm_g, v_norm_g), b_ada=(b_ada, m_b_ada, v_b_ada),
                 w_conv_a=(w_conv_a, m_w_conv_a, v_w_conv_a), w_conv_b=(w_conv_b, m_w_conv_b, v_w_conv_b),
                 b_conv_b=(b_conv_b, m_b_conv_b, v_b_conv_b), lru_wa=(lru_wa, m_lru_wa, v_lru_wa),
                 lru_ba=(lru_ba, m_lru_ba, v_lru_ba), lru_wx=(lru_wx, m_lru_wx, v_lru_wx),
                 lru_bx=(lru_bx, m_lru_bx, v_lru_bx), lru_lambda=(lru_lambda, m_lru_lambda, v_lru_lambda),
                 final_g=(final_g, m_final_g, v_final_g))
    def rows2d(a):
        return a.reshape(-1, a.shape[-1])

    grads = {n: grads[n].reshape(given[n][0].shape) for n in small_names}
    quads = [tuple(rows2d(a) for a in (given[n][0], grads[n], given[n][1], given[n][2])) for n in small_names]
    updated = _adam_many(quads, "adam_small")
    delta_s, newm_s, newv_s = ({n: u[j].reshape(given[n][0].shape) for n, u in zip(small_names, updated)} for j in range(3))

    big = {"w_ada": (w_ada, g_w_ada, m_w_ada, v_w_ada), "w_in": (w_in, g_in_shard, m_w_in, v_w_in),
           "w_out": (w_out, g_out_shard, m_w_out, v_w_out)}
    delta_b, newm_b, newv_b = {}, {}, {}
    for n, (w, g, m, v) in big.items():
        d_, m_, v_, *echo = _adam(w[0], g, m[0], v[0], "adam_" + n, echo_g=n != "w_ada")
        grads[n] = (echo[0] if echo else g)[None]
        delta_b[n], newm_b[n], newv_b[n] = d_[None], m_[None], v_[None]

    loss = loss_sum[0, 0]
    order = ["c_ctx", "norm_g", "w_ada", "b_ada", "w_in", "w_conv_a", "w_conv_b", "b_conv_b", "lru_wa", "lru_ba",
             "lru_wx", "lru_bx", "lru_lambda", "w_out", "final_g"]
    delta = {**delta_s, **delta_b}
    newm = {**newm_s, **newm_b}
    newv = {**newv_s, **newv_b}
    return (loss, gx[None], *[grads[n] for n in order], *[delta[n] for n in order], *[newm[n] for n in order],
            *[newv[n] for n in order])
```

```python
import jax
import jax.numpy as jnp
from jax import lax
from jax.experimental import pallas as pl
from jax.experimental.pallas import tpu as pltpu

F32 = jnp.float32
BF16 = jnp.bfloat16
MESH_ID = pl.DeviceIdType.MESH

EPS = 1e-6
LRU_C = 8.0
GRID_W = 64
ADAM_LR = 0.001
ADAM_B1 = 0.9
ADAM_B2 = 0.999
ADAM_EPS = 1e-08
ADAM_WD = 0.01
ADAM_STEP = 10

LANES = 128
SUBLANES = 8
PACK_COLS = 1024
VMEM_LIMIT = 56 * 2**20
LRU_GROUP = 256


def _params(sem=None):
    return pltpu.CompilerParams(vmem_limit_bytes=VMEM_LIMIT, dimension_semantics=sem)


def _pick(dim, pref, quantum=LANES):
    if dim <= pref:
        return dim
    best = None
    for t in range(quantum, pref + 1, quantum):
        if dim % t == 0:
            best = t
    assert best is not None, (dim, pref)
    return best


def _pos():
    return lax.axis_index("x"), lax.axis_index("y"), lax.axis_index("c")


def _flip(v, bit):
    return 1 - v if bit else v


def _sigmoid(v):
    return 0.5 * jnp.tanh(0.5 * v) + 0.5


def _silu(v):
    return v * _sigmoid(v)


def _dsilu(v):
    s = _sigmoid(v)
    return s * (1.0 + v * (1.0 - s))


def _gates(pre_r, pre_i, sp):
    r = _sigmoid(pre_r)
    ig = _sigmoid(pre_i)
    e = LRU_C * r * sp
    w = jnp.tanh(e)
    return r, ig, jnp.exp(-e), (2.0 * w) * pl.reciprocal(1.0 + w, approx=True)


def _softplus(z):
    return jnp.maximum(z, 0.0) + jnp.log1p(jnp.exp(-jnp.abs(z)))


def _matmul(a, b, *, ta=False, tb=False, tm=512, tn=512, tk=512, out_dtype=F32, name,
            a_rows=None, a_off=0, n_range=None, k_range=None, a_act=None, bias=None, dsilu_mul=None, side=None):
    rows_a = a.shape[0] if a_rows is None else a_rows
    if ta:
        K, M = rows_a, a.shape[1]
    else:
        M, K = rows_a, a.shape[1]
    N = b.shape[0] if tb else b.shape[1]
    n_off, k_off = 0, 0
    if n_range is not None:
        n_off, N = n_range
    if k_range is not None:
        assert not ta
        k_off, K = k_range
    tm, tn, tk = _pick(M, tm, SUBLANES), _pick(N, tn), _pick(K, tk)
    t_rows = tk if ta else tm
    assert a_off % t_rows == 0 and n_off % tn == 0 and k_off % tk == 0
    nk = K // tk
    gi, gj = M // tm, N // tn
    off_blocks, nb, kb = a_off // t_rows, n_off // tn, k_off // tk
    dims = (((0 if ta else 1,), (1 if tb else 0,)), ((), ()))
    extras = [e for e in (bias, dsilu_mul) if e is not None]
    n_sin = len(side["ins"]) if side else 0
    n_sout = len(side["outs"]) if side else 0

    def body(a_ref, b_ref, *rest):
        rest = list(rest)
        bias_ref = rest.pop(0) if bias is not None else None
        dsm_ref = rest.pop(0) if dsilu_mul is not None else None
        side_in = [rest.pop(0) for _ in range(n_sin)]
        o_ref = rest.pop(0)
        side_out = [rest.pop(0) for _ in range(n_sout)]
        acc_ref = rest.pop(0) if nk > 1 else None
        side_scr = rest
        i, j, k = pl.program_id(0), pl.program_id(1), pl.program_id(2)

        if side:
            @pl.when((i == 0) & (j == 0) & (k == 0))
            def _():
                side["start"](side_in, side_out, side_scr)

        av = a_ref[...]
        if a_act == "silu":
            av = _silu(av)
        prod = lax.dot_general(av, b_ref[...], dims, preferred_element_type=F32)

        def finish(r):
            if bias_ref is not None:
                r = r + bias_ref[...]
            if dsm_ref is not None:
                r = r * _dsilu(dsm_ref[...])
            o_ref[...] = r.astype(o_ref.dtype)

        if nk == 1:
            finish(prod)
        else:
            @pl.when(k == 0)
            def _():
                acc_ref[...] = prod

            @pl.when(k > 0)
            def _():
                acc_ref[...] += prod

            @pl.when(k == nk - 1)
            def _():
                finish(acc_ref[...])

        if side:
            @pl.when((i == gi - 1) & (j == gj - 1) & (k == nk - 1))
            def _():
                side["finish"](side_in, side_out, side_scr)

    if ta:
        a_spec = pl.BlockSpec((tk, tm), lambda i, j, k: (k + off_blocks, i))
    else:
        a_spec = pl.BlockSpec((tm, tk), lambda i, j, k: (i + off_blocks, k + kb))
    once = dict(pipeline_mode=pl.Buffered(1)) if (gj == 1 and nk == 1) else {}
    if tb:
        b_spec = pl.BlockSpec((tn, tk), lambda i, j, k: (j + nb, k + kb), **once)
    else:
        b_spec = pl.BlockSpec((tk, tn), lambda i, j, k: (k + kb, j + nb), **once)
    in_specs = [a_spec, b_spec]
    if bias is not None:
        in_specs.append(pl.BlockSpec((1, tn), lambda i, j, k: (0, j)))
    if dsilu_mul is not None:
        in_specs.append(pl.BlockSpec((1, tn), lambda i, j, k: (0, j)))
    hbm = pl.BlockSpec(memory_space=pl.ANY)
    res = pl.pallas_call(
        body, name=name, grid=(gi, gj, nk),
        in_specs=in_specs + [hbm] * n_sin,
        out_specs=[pl.BlockSpec((tm, tn), lambda i, j, k: (i, j))] + [hbm] * n_sout,
        out_shape=[jax.ShapeDtypeStruct((M, N), out_dtype)] + (list(side["outs"]) if side else []),
        scratch_shapes=([pltpu.VMEM((tm, tn), F32)] if nk > 1 else []) + (list(side["scratch"]) if side else []),
        compiler_params=_params(("arbitrary",) * 3 if side else ("parallel", "parallel", "arbitrary")),
    )(a, b, *extras, *(side["ins"] if side else []))
    return (res[0], res[1:]) if side else res[0]


def _elementwise(fn, ins, outs, *, rows, cols, name, tr=256):
    tr = _pick(rows, tr, 2 * SUBLANES)
    n_in = len(ins)

    def body(*refs):
        vals = fn(*[r[...] for r in refs[:n_in]])
        if not isinstance(vals, (tuple, list)):
            vals = (vals,)
        for r, v in zip(refs[n_in:], vals, strict=True):
            r[...] = v.astype(r.dtype)

    def spec(off):
        assert off % tr == 0
        ob = off // tr
        return pl.BlockSpec((tr, cols), lambda i: (i + ob, 0))

    res = pl.pallas_call(
        body, name=name, grid=(rows // tr,),
        in_specs=[spec(off) for _, off in ins],
        out_specs=[spec(0) for _ in outs],
        out_shape=[jax.ShapeDtypeStruct((rows, cols), dt) for dt in outs],
        compiler_params=_params(("parallel",)),
    )(*[a for a, _ in ins])
    return res


def _adam_math(w, g, m, v):
    m = ADAM_B1 * m + (1.0 - ADAM_B1) * g
    v = ADAM_B2 * v + (1.0 - ADAM_B2) * (g * g)
    m_hat = m / (1.0 - ADAM_B1 ** ADAM_STEP)
    v_hat = v / (1.0 - ADAM_B2 ** ADAM_STEP)
    delta = -ADAM_LR * (m_hat / (jnp.sqrt(v_hat) + ADAM_EPS) + ADAM_WD * w)
    return delta, m, v


def _adam(w, g, m, v, name, echo_g=False):
    rows, cols = w.shape
    fn = (lambda w_, g_, m_, v_: _adam_math(w_, g_, m_, v_) + (g_,)) if echo_g else _adam_math
    return _elementwise(fn, [(w, 0), (g, 0), (m, 0), (v, 0)], [F32] * (4 if echo_g else 3),
                        rows=rows, cols=cols, name=name)


def _adam_many(quads, name):
    n = len(quads)

    def body(*refs):
        ins, outs = refs[:4 * n], refs[4 * n:]
        for t in range(n):
            w, g, m, v = (r[...] for r in ins[4 * t:4 * t + 4])
            for o_ref, val in zip(outs[3 * t:3 * t + 3], _adam_math(w, g, m, v), strict=True):
                o_ref[...] = val

    res = pl.pallas_call(
        body, name=name,
        out_shape=[jax.ShapeDtypeStruct(q[0].shape, F32) for q in quads for _ in range(3)],
        compiler_params=_params(),
    )(*[a for q in quads for a in q])
    return [res[3 * t:3 * t + 3] for t in range(n)]


def _pack(arrs, row_quantum):
    flat = jnp.concatenate([a.reshape(-1).astype(F32) for a in arrs])
    n = flat.shape[0]
    q = row_quantum * PACK_COLS
    total = -(-n // q) * q
    flat = jnp.pad(flat, (0, total - n))
    return flat.reshape(total // PACK_COLS, PACK_COLS)


def _unpack(buf, shapes):
    flat = buf.reshape(-1)
    out, off = [], 0
    for s in shapes:
        n = 1
        for d in s:
            n *= d
        out.append(flat[off:off + n].reshape(s))
        off += n
    return out


def _allreduce8(buf, name):
    R, C = buf.shape
    assert R % (8 * SUBLANES) == 0
    m = R // 8

    def body(x_ref, o_ref, recv, red, s1, r1, s2, r2):
        x, y, c = _pos()
        me = 4 * x + 2 * y + c

        def peer(k):
            px, py, pc = _flip(x, (k >> 2) & 1), _flip(y, (k >> 1) & 1), _flip(c, k & 1)
            return (px, py, pc), 4 * px + 2 * py + pc

        def rows(ref, idx):
            return ref.at[pl.ds(pl.multiple_of(idx * m, SUBLANES), m), :]

        def scatter(k):
            dev, p = peer(k)
            return pltpu.make_async_remote_copy(src_ref=rows(x_ref, p), dst_ref=recv.at[k], send_sem=s1.at[k],
                                                recv_sem=r1.at[k], device_id=dev, device_id_type=MESH_ID)

        def share(k):
            dev, p = peer(k)
            return pltpu.make_async_remote_copy(src_ref=red, dst_ref=rows(o_ref, me), send_sem=s2.at[k],
                                                recv_sem=r2.at[k], device_id=dev, device_id_type=MESH_ID)

        def shared_from(k):
            dev, p = peer(k)
            return pltpu.make_async_remote_copy(src_ref=red, dst_ref=rows(o_ref, p), send_sem=s2.at[k],
                                                recv_sem=r2.at[k], device_id=dev, device_id_type=MESH_ID)

        for k in range(1, 8):
            scatter(k).start()
        acc = rows(x_ref, me)[...]
        for k in range(1, 8):
            scatter(k).wait_recv()
            acc = acc + recv[k]
        red[...] = acc
        rows(o_ref, me)[...] = acc
        for k in range(1, 8):
            share(k).start()
        for k in range(1, 8):
            shared_from(k).wait_recv()
        for k in range(1, 8):
            scatter(k).wait_send()
            share(k).wait_send()

    return pl.pallas_call(
        body, name=name,
        in_specs=[pl.BlockSpec(memory_space=pltpu.VMEM)],
        out_specs=pl.BlockSpec(memory_space=pltpu.VMEM),
        out_shape=jax.ShapeDtypeStruct((R, C), F32),
        scratch_shapes=[pltpu.VMEM((8, m, C), F32), pltpu.VMEM((m, C), F32),
                        pltpu.SemaphoreType.DMA((8,)), pltpu.SemaphoreType.DMA((8,)),
                        pltpu.SemaphoreType.DMA((8,)), pltpu.SemaphoreType.DMA((8,))],
        compiler_params=_params(),
    )(buf)


def _allreduce8_two_level(buf, name):
    R, C = buf.shape
    assert R % (8 * SUBLANES) == 0
    m, hr = R // 8, R // 2

    def body(x_ref, o_ref, got0, half, got1, red, ssem, rsem):
        x, y, c = _pos()
        q = 2 * x + y
        sibling = (x, y, 1 - c)

        def half_rows(ref, core):
            return ref.at[pl.ds(pl.multiple_of(core * hr, SUBLANES), hr), :]

        def chunk(ref, core, chip):
            return ref.at[pl.ds(pl.multiple_of(core * hr + chip * m, SUBLANES), m), :]

        def chip_of(k):
            px, py = _flip(x, (k >> 1) & 1), _flip(y, k & 1)
            return (px, py, c), 2 * px + py

        def copy(src, dst, phase, k, dev):
            return pltpu.make_async_remote_copy(src_ref=src, dst_ref=dst, send_sem=ssem.at[phase, k],
                                                recv_sem=rsem.at[phase, k], device_id=dev, device_id_type=MESH_ID)

        swap = copy(half_rows(x_ref, 1 - c), got0, 0, 0, sibling)
        swap.start()
        swap.wait()
        half[...] = half_rows(x_ref, c)[...] + got0[...]

        def scatter(k):
            dev, p = chip_of(k)
            return copy(half.at[pl.ds(pl.multiple_of(p * m, SUBLANES), m), :], got1.at[k], 1, k, dev)

        for k in range(1, 4):
            scatter(k).start()
        acc = half[pl.ds(pl.multiple_of(q * m, SUBLANES), m), :]
        for k in range(1, 4):
            scatter(k).wait_recv()
            acc = acc + got1[k]
        red[...] = acc
        chunk(o_ref, c, q)[...] = acc

        def share(k, landing_chip):
            return copy(red, chunk(o_ref, c, landing_chip), 2, k, chip_of(k)[0])

        for k in range(1, 4):
            share(k, q).start()
        for k in range(1, 4):
            share(k, chip_of(k)[1]).wait_recv()
        back = copy(half_rows(o_ref, c), half_rows(o_ref, c), 3, 0, sibling)
        back.start()
        copy(half_rows(o_ref, 1 - c), half_rows(o_ref, 1 - c), 3, 0, sibling).wait_recv()
        back.wait_send()
        for k in range(1, 4):
            scatter(k).wait_send()
            share(k, q).wait_send()

    return pl.pallas_call(
        body, name=name,
        in_specs=[pl.BlockSpec(memory_space=pltpu.VMEM)],
        out_specs=pl.BlockSpec(memory_space=pltpu.VMEM),
        out_shape=jax.ShapeDtypeStruct((R, C), F32),
        scratch_shapes=[pltpu.VMEM((hr, C), F32), pltpu.VMEM((hr, C), F32), pltpu.VMEM((4, m, C), F32), pltpu.VMEM((m, C), F32),
                        pltpu.SemaphoreType.DMA((4, 4)), pltpu.SemaphoreType.DMA((4, 4))],
        compiler_params=_params(),
    )(buf)


def _bounce(src, dst, buf, sem):
    cin = pltpu.make_async_copy(src, buf, sem)
    cin.start()
    cin.wait()
    cout = pltpu.make_async_copy(buf, dst, sem)
    cout.start()
    cout.wait()


def _chunk(ref, axis, idx, size):
    start = idx * size
    if axis == 0:
        return ref.at[pl.ds(start, size), :]
    return ref.at[:, pl.ds(start, size)]


def _in_proj_gather(hl, win, wout, q_arr, *, rows, tm):
    D, nq = win.shape
    dq, D2 = wout.shape
    ni = rows // tm
    ops = ((0, 1, nq, D // 2), (1, 0, dq, dq // 2))

    def body(q_ref, a_ref, win_ref, wout_ref, p_ref, gin_ref, gout_ref, b_scr, buf_out, lsem, ssem, rsem, fsem, gsem):
        j, i = pl.program_id(0), pl.program_id(1)
        x, y, c = _pos()
        q = 2 * x + y
        srcs = (win_ref, wout_ref)
        dsts = (gin_ref, gout_ref)

        def shard_window(o, chip):
            _, axis, size, _ = ops[o]
            return _chunk(dsts[o], axis, chip, size)

        def half(ref, o, core):
            return ref.at[pl.ds(core * ops[o][3], ops[o][3]), :]

        def half_window(o, chip, core):
            _, axis, size, hs = ops[o]
            if axis == 1:
                return dsts[o].at[pl.ds(core * hs, hs), pl.ds(chip * size, size)]
            return dsts[o].at[pl.ds(chip * size + core * hs, hs), :]

        def chip_of(k):
            px, py = _flip(x, (k >> 1) & 1), _flip(y, k & 1)
            return px, py, 2 * px + py

        def send(o, k):
            px, py, _ = chip_of(k)
            return pltpu.make_async_remote_copy(
                src_ref=half(srcs[o], o, c), dst_ref=half_window(o, q, c), send_sem=ssem.at[o, k],
                recv_sem=rsem.at[o, k], device_id=(px, py, c), device_id_type=MESH_ID)

        def chip_recv(o, k):
            px, py, pq = chip_of(k)
            landed = half_window(o, pq, c)
            pltpu.make_async_remote_copy(src_ref=landed, dst_ref=landed, send_sem=ssem.at[o, k], recv_sem=rsem.at[o, k],
                                         device_id=(px, py, c), device_id_type=MESH_ID).wait_recv()

        def to_sibling(o, k):
            landed = half_window(o, chip_of(k)[2], c)
            return pltpu.make_async_remote_copy(src_ref=landed, dst_ref=landed, send_sem=fsem.at[o, k],
                                                recv_sem=gsem.at[o, k], device_id=(x, y, 1 - c), device_id_type=MESH_ID)

        def from_sibling(o, k):
            theirs = half_window(o, chip_of(k)[2], 1 - c)
            pltpu.make_async_remote_copy(src_ref=theirs, dst_ref=theirs, send_sem=fsem.at[o, k], recv_sem=gsem.at[o, k],
                                         device_id=(x, y, 1 - c), device_id_type=MESH_ID).wait_recv()

        def relay(o, core):
            if core == 0:
                landed, target = half_window(o, chip_of(2)[2], 0), (x, 1 - y, 0)
            else:
                landed, target = half_window(o, chip_of(1)[2], 1), (1 - x, y, 1)
            return pltpu.make_async_remote_copy(src_ref=landed, dst_ref=landed, send_sem=ssem.at[o, 3],
                                                recv_sem=rsem.at[o, 3], device_id=target, device_id_type=MESH_ID)

        def on_core(core, fn):
            @pl.when(c == core)
            def _():
                fn()

        def land(o, k):
            chip_recv(o, k)
            if k == 2:
                on_core(0, lambda: relay(o, 0).start())
            if k == 1:
                on_core(1, lambda: relay(o, 1).start())
            to_sibling(o, k).start()

        def settle(o, k):
            from_sibling(o, k)
            to_sibling(o, k).wait_send()

        def b_load(k, slot):
            src = win_ref if k == 0 else shard_window(0, chip_of(k)[2])
            return pltpu.make_async_copy(src, b_scr.at[slot], lsem.at[0])

        def own_store():
            return pltpu.make_async_copy(b_scr.at[0], shard_window(0, q), lsem.at[2])

        order = (0, 2, 1, 3)
        early = max(ni - 2, 0)

        @pl.when((j == 0) & (i == 0))
        def _():
            for o in range(2):
                for k in (2, 1):
                    send(o, k).start()
            first = b_load(0, 0)
            first.start()
            first.wait()
            own_store().start()
            _bounce(wout_ref, shard_window(1, q), buf_out, lsem.at[1])

        for jj in range(3):
            nxt = order[jj + 1]

            @pl.when((j == jj) & (i == early))
            def _(nxt=nxt):
                land(0, nxt)

            @pl.when((j == jj) & (i == ni - 1))
            def _(jj=jj, nxt=nxt):
                settle(0, nxt)
                if jj == 1:
                    own_store().wait()
                b_load(nxt, (jj + 1) % 2).start()

            @pl.when((j == jj + 1) & (i == 0))
            def _(jj=jj, nxt=nxt):
                b_load(nxt, (jj + 1) % 2).wait()

        @pl.when((j == 3) & (i == 0))
        def _():
            land(1, 2)
            land(1, 1)

        p_ref[...] = jnp.dot(a_ref[...], b_scr[j % 2], preferred_element_type=F32).astype(p_ref.dtype)

        @pl.when((j == 3) & (i == ni - 1))
        def _():
            settle(1, 2)
            settle(1, 1)
            land(1, 3)
            settle(1, 3)
            for o in range(2):
                for k in (2, 1):
                    send(o, k).wait_send()
                for core in range(2):
                    on_core(core, lambda o=o, core=core: relay(o, core).wait_send())

    hbm = pl.BlockSpec(memory_space=pl.ANY)
    grid_spec = pltpu.PrefetchScalarGridSpec(
        num_scalar_prefetch=1, grid=(4, ni),
        in_specs=[pl.BlockSpec((tm, D), lambda j, i, qr: (i, 0)), hbm, hbm],
        out_specs=[pl.BlockSpec((tm, nq), lambda j, i, qr: (i, jnp.bitwise_xor(qr[0], ((j & 1) << 1) | (j >> 1)))),
                   hbm, hbm],
        scratch_shapes=[pltpu.VMEM((2,) + win.shape, win.dtype), pltpu.VMEM(wout.shape, wout.dtype), pltpu.SemaphoreType.DMA((3,))]
        + [pltpu.SemaphoreType.DMA((2, 4)) for _ in range(4)])
    return pl.pallas_call(
        body, name="in_proj_gather", grid_spec=grid_spec,
        out_shape=[jax.ShapeDtypeStruct((rows, 4 * nq), BF16), jax.ShapeDtypeStruct((D, 4 * nq), win.dtype),
                   jax.ShapeDtypeStruct((4 * dq, D2), wout.dtype)],
        compiler_params=_params(("arbitrary", "arbitrary")),
    )(q_arr, hl, win, wout)


def _rs_to_sibling(gs, axes):
    n = len(gs)
    shapes = []
    for g, ax in zip(gs, axes):
        s = list(g.shape)
        s[ax] //= 8
        shapes.append(tuple(s))

    def body(*refs):
        g_refs, mine, landed = refs[:n], refs[n:2 * n], refs[2 * n:3 * n]
        bufs = refs[3 * n:4 * n]
        lsem, ssem, rsem = refs[4 * n:]
        x, y, c = _pos()
        cps = []
        for o in range(n):
            size = shapes[o][axes[o]]
            for j in range(4):
                rc = pltpu.make_async_remote_copy(
                    src_ref=_chunk(g_refs[o], axes[o], 2 * j + 1 - c, size), dst_ref=landed[o].at[j],
                    send_sem=ssem.at[o, j], recv_sem=rsem.at[o, j], device_id=(x, y, 1 - c), device_id_type=MESH_ID)
                rc.start()
                cps.append(rc)
        for o in range(n):
            size = shapes[o][axes[o]]
            for j in range(4):
                _bounce(_chunk(g_refs[o], axes[o], 2 * j + c, size), mine[o].at[j], bufs[o], lsem.at[o])
        for rc in cps:
            rc.wait()

    hbm = pl.BlockSpec(memory_space=pl.ANY)
    outs = [jax.ShapeDtypeStruct((4,) + s, g.dtype) for s, g in zip(shapes, gs)]
    res = pl.pallas_call(
        body, name="rs_to_sibling", in_specs=[hbm] * n, out_specs=[hbm] * (2 * n), out_shape=outs + outs,
        scratch_shapes=[pltpu.VMEM(s, g.dtype) for s, g in zip(shapes, gs)]
        + [pltpu.SemaphoreType.DMA((n,)), pltpu.SemaphoreType.DMA((n, 4)), pltpu.SemaphoreType.DMA((n, 4))],
        compiler_params=_params(),
    )(*gs)
    return res[:n], res[n:]


def _gather_block_side(block):
    r, n_cols = block.shape

    def copies(ins, outs, scr):
        ssem, rsem = scr[1], scr[2]
        x, y, c = _pos()
        return [pltpu.make_async_remote_copy(
            src_ref=ins[0], dst_ref=outs[0].at[k], send_sem=ssem.at[k], recv_sem=rsem.at[k],
            device_id=(_flip(x, (k >> 2) & 1), _flip(y, (k >> 1) & 1), _flip(c, k & 1)), device_id_type=MESH_ID)
            for k in range(1, 8)]

    def start(ins, outs, scr):
        for cp in copies(ins, outs, scr):
            cp.start()

    def finish(ins, outs, scr):
        _bounce(ins[0], outs[0].at[0], scr[0], scr[3].at[0])
        for cp in copies(ins, outs, scr):
            cp.wait()

    return dict(ins=[block], outs=[jax.ShapeDtypeStruct((8, r, n_cols), block.dtype)],
                scratch=[pltpu.VMEM((r, n_cols), block.dtype), pltpu.SemaphoreType.DMA((8,)), pltpu.SemaphoreType.DMA((8,)),
                         pltpu.SemaphoreType.DMA((1,))],
                start=start, finish=finish)


def _join_sides(a, b):
    na_i, na_o, na_s = len(a["ins"]), len(a["outs"]), len(a["scratch"])

    def run(which):
        def fn(ins, outs, scr):
            a[which](ins[:na_i], outs[:na_o], scr[:na_s])
            b[which](ins[na_i:], outs[na_o:], scr[na_s:])
        return fn

    return dict(ins=a["ins"] + b["ins"], outs=a["outs"] + b["outs"], scratch=a["scratch"] + b["scratch"],
                start=run("start"), finish=run("finish"))


def _rs_chips_side(parts):
    n = len(parts)

    def copies(p_refs, slots, scr):
        ssem, rsem = scr[n + 1], scr[n + 2]
        x, y, c = _pos()
        cps = []
        for o in range(n):
            for k in range(1, 4):
                px, py = _flip(x, (k >> 1) & 1), _flip(y, k & 1)
                cps.append(pltpu.make_async_remote_copy(
                    src_ref=p_refs[o].at[2 * px + py], dst_ref=slots[o].at[k], send_sem=ssem.at[o, k],
                    recv_sem=rsem.at[o, k], device_id=(px, py, c), device_id_type=MESH_ID))
        return cps

    def start(p_refs, slots, scr):
        for cp in copies(p_refs, slots, scr):
            cp.start()

    def finish(p_refs, slots, scr):
        x, y, _ = _pos()
        q = 2 * x + y
        for o in range(n):
            _bounce(p_refs[o].at[q], slots[o].at[0], scr[o], scr[n].at[o])
        for cp in copies(p_refs, slots, scr):
            cp.wait()

    return dict(
        ins=list(parts), outs=[jax.ShapeDtypeStruct(p.shape, p.dtype) for p in parts],
        scratch=[pltpu.VMEM(p.shape[1:], p.dtype) for p in parts]
        + [pltpu.SemaphoreType.DMA((n,)), pltpu.SemaphoreType.DMA((n, 4)), pltpu.SemaphoreType.DMA((n, 4))],
        start=start, finish=finish)


def _rs_share(rs, axes):
    n = len(rs)
    shapes = []
    for r, ax in zip(rs, axes):
        s = list(r.shape)
        s[ax] *= 2
        shapes.append(tuple(s))

    def body(*refs):
        r_refs, outs = refs[:n], refs[n:2 * n]
        bufs = refs[2 * n:3 * n]
        lsem, ssem, rsem = refs[3 * n:]
        x, y, c = _pos()
        cps = []
        for o in range(n):
            size = r_refs[o].shape[axes[o]]
            window = _chunk(outs[o], axes[o], c, size)
            rc = pltpu.make_async_remote_copy(src_ref=r_refs[o], dst_ref=window, send_sem=ssem.at[o], recv_sem=rsem.at[o],
                                              device_id=(x, y, 1 - c), device_id_type=MESH_ID)
            rc.start()
            cps.append(rc)
        for o in range(n):
            size = r_refs[o].shape[axes[o]]
            _bounce(r_refs[o], _chunk(outs[o], axes[o], c, size), bufs[o], lsem.at[o])
        for cp in cps:
            cp.wait()

    hbm = pl.BlockSpec(memory_space=pl.ANY)
    return pl.pallas_call(
        body, name="rs_share", in_specs=[hbm] * n, out_specs=[hbm] * n,
        out_shape=[jax.ShapeDtypeStruct(s, r.dtype) for s, r in zip(shapes, rs)],
        scratch_shapes=[pltpu.VMEM(r.shape, r.dtype) for r in rs] + [pltpu.SemaphoreType.DMA((n,)) for _ in range(3)],
        compiler_params=_params(),
    )(*rs)


def _rs_pair_sums(gs, axes):
    mine, landed = _rs_to_sibling(gs, axes)
    pair_sums = []
    for o, (mi, la) in enumerate(zip(mine, landed)):
        rows, cols = mi.shape[0] * mi.shape[1], mi.shape[2]
        s = _elementwise(lambda a, b: a.astype(F32) + b.astype(F32), [(mi.reshape(rows, cols), 0), (la.reshape(rows, cols), 0)],
                         [BF16], rows=rows, cols=cols, name=f"rs_pair_sum{o}")[0]
        pair_sums.append(s.reshape(mi.shape))
    return pair_sums


def _rs_finish(slots, axes):
    reduced = []
    for o, sl in enumerate(slots):
        rows, cols = sl.shape[1], sl.shape[2]
        flat = sl.reshape(4 * rows, cols)
        r = _elementwise(lambda a, b, c, d: (a.astype(F32) + b.astype(F32)) + (c.astype(F32) + d.astype(F32)),
                         [(flat, k * rows) for k in range(4)], [F32], rows=rows, cols=cols, name=f"rs_chip_sum{o}")[0]
        reduced.append(r)
    return _rs_share(reduced, axes)


def _norm_in(x, ctx, g, sc_l, sh_l, sc_c, sh_c, tr):
    L, D = x.shape
    T = ctx.shape[0]
    nx, nc = L // tr, T // tr

    def body(x_ref, c_ref, g_ref, scl, shl, scc, shc, o_ref):
        i = pl.program_id(0)

        def run(src, sc, sh):
            v = src[...]
            r = lax.rsqrt(jnp.mean(v * v, axis=-1, keepdims=True) + EPS)
            o_ref[...] = ((v * r * g_ref[...]) * (1.0 + sc[...]) + sh[...]).astype(o_ref.dtype)

        @pl.when(i < nx)
        def _():
            run(x_ref, scl, shl)

        @pl.when(i >= nx)
        def _():
            run(c_ref, scc, shc)

    vec = pl.BlockSpec((1, D), lambda i: (0, 0))
    return pl.pallas_call(
        body, name="norm_in", grid=(nx + nc,),
        in_specs=[pl.BlockSpec((tr, D), lambda i: (jnp.minimum(i, nx - 1), 0)),
                  pl.BlockSpec((tr, D), lambda i: (jnp.maximum(i - nx, 0), 0)), vec, vec, vec, vec, vec],
        out_specs=pl.BlockSpec((tr, D), lambda i: (i, 0)),
        out_shape=jax.ShapeDtypeStruct((L + T, D), BF16),
        compiler_params=_params(("arbitrary",)),
    )(x, ctx, g, sc_l, sh_l, sc_c, sh_c)


def _tmod(tl, row_w):
    assert row_w & (row_w - 1) == 0
    return lax.broadcasted_iota(jnp.int32, (tl, 1), 0) & (row_w - 1)


def _shift(z, k, tmod, row_w):
    tl = z.shape[0]
    rolled = pltpu.roll(z, k % tl, 0)
    mask = (tmod >= k) if k > 0 else (tmod < row_w + k)
    return jnp.where(mask, rolled, 0.0)


def _conv(z, w_ref, taps, left, tmod, row_w, lanes=slice(None)):
    out = None
    for j in range(taps):
        k = left - j
        term = (z if k == 0 else _shift(z, k, tmod, row_w)) * w_ref[j:j + 1, lanes]
        out = term if out is None else out + term
    return out


def _conv_bwd(dz, z, w_ref, taps, left, tmod, row_w, lanes=slice(None)):
    din = None
    dws = []
    for j in range(taps):
        k = left - j
        shifted = dz if k == 0 else _shift(dz, -k, tmod, row_w)
        term = shifted * w_ref[j:j + 1, lanes]
        din = term if din is None else din + term
        dws.append(jnp.sum(shifted * z, axis=0, keepdims=True))
    return din, dws


def _gate_matmul(xb16_ref, wd_ref, pre_scr, W, ng, gs):
    for g in range(ng):
        pg = jnp.dot(xb16_ref[:, g * gs:(g + 1) * gs], wd_ref[g], preferred_element_type=F32)
        pre_scr[:, g * gs:(g + 1) * gs] = pg[:, :gs]
        pre_scr[:, W + g * gs:W + (g + 1) * gs] = pg[:, gs:]


def _f32(ref, rows, lanes):
    return ref[rows, lanes].astype(F32)


def _sub_loop(tl, sub_r, W, fn):
    def chunk(ci, carry):
        r0 = pl.multiple_of(ci * sub_r, sub_r)
        for lb in range(W // LANES):
            fn(r0, lb * LANES)
        return carry

    lax.fori_loop(0, tl // sub_r, chunk, 0)


def _row_loop(tl, rev, step, init):
    nchunk = tl // SUBLANES

    def chunk(j, carry):
        jj = (nchunk - 1 - j) if rev else j
        c0 = pl.multiple_of(jj * SUBLANES, SUBLANES)
        for r in (range(SUBLANES - 1, -1, -1) if rev else range(SUBLANES)):
            carry = step(c0 + r, carry)
        return carry

    return lax.fori_loop(0, nchunk, chunk, init)


def _mix_fwd(P, d, h_init, wts, *, rows, row_off, row_w, tl, saved0=None, name):
    W = P.shape[1] // 6
    nt = rows // tl
    ob = row_off // tl
    rev = d == 1
    gs = min(LRU_GROUP, W)
    ng = W // gs
    wca, wcb, bcb = wts["wca"], wts["wcb"], wts["bcb"]
    wd, ba, bx, lam = wts["wd"][d], wts["ba"][d], wts["bx"][d], wts["lam"][d]

    def tile(i):
        return (nt - 1 - i) if rev else i

    def pcol(j):
        return pl.BlockSpec((tl, W), lambda i: (tile(i) + ob, j))

    vec = pl.BlockSpec((1, W), lambda i: (0, 0))
    taps = pl.BlockSpec((SUBLANES, W), lambda i: (0, 0))
    wd_spec = pl.BlockSpec(wd.shape, lambda i: (0, 0, 0))
    seq = pl.BlockSpec((tl, W), lambda i: (tile(i), 0))

    sub_r = min(row_w, tl)
    assert tl % sub_r == 0

    def body(*refs):
        if rev:
            (bl, cl, ul, gl, ql, ho, xb_r, xb16_r, wca_r, wd_r, ba_r, bx_r, lam_r, hin, hseq, cat, a_o, r_o, ig_o, m2_o,
             b_scr, pre_scr, carry, sp_scr) = refs
        else:
            (vl, wcb_r, bcb_r, wd_r, ba_r, bx_r, lam_r, hin, hseq, xb_r, xb16_r, a_o, r_o, ig_o, m2_o,
             b_scr, pre_scr, carry, sp_scr) = refs
        i = pl.program_id(0)

        @pl.when(i == 0)
        def _():
            carry[...] = hin[...]

        sp_scr[...] = _softplus(-lam_r[...])
        tmod = _tmod(sub_r, sub_r)

        def conv_in(r0, l0):
            rs, ls = pl.ds(r0, sub_r), pl.ds(l0, LANES)
            xb = _conv(_f32(vl, rs, ls), wcb_r, 4, 2, tmod, sub_r, ls) + bcb_r[:, ls]
            xb_r[rs, ls] = xb
            xb16_r[rs, ls] = xb.astype(BF16)

        def gates(r0, l0):
            rs, ls = pl.ds(r0, sub_r), pl.ds(l0, LANES)
            r, ig, a, m2 = _gates(pre_scr[rs, ls] + ba_r[:, ls], pre_scr[rs, pl.ds(W + l0, LANES)] + bx_r[:, ls],
                                  sp_scr[:, ls])
            a_o[rs, ls] = a
            r_o[rs, ls] = r.astype(r_o.dtype)
            ig_o[rs, ls] = ig.astype(ig_o.dtype)
            m2_o[rs, ls] = m2.astype(m2_o.dtype)
            m = jnp.where(m2 > 0.0, m2 * lax.rsqrt(m2), 0.0)
            b_scr[rs, ls] = m * (ig * xb_r[rs, ls])

        if not rev:
            _sub_loop(tl, sub_r, W, conv_in)
        _gate_matmul(xb16_r, wd_r, pre_scr, W, ng, gs)
        _sub_loop(tl, sub_r, W, gates)

        def step(t, h):
            h = a_o[pl.ds(t, 1), :] * h + b_scr[pl.ds(t, 1), :]
            hseq[pl.ds(t, 1), :] = h
            return h

        carry[...] = _row_loop(tl, rev, step, carry[...])

        if rev:
            def mix_out(r0, l0):
                rs, ls = pl.ds(r0, sub_r), pl.ds(l0, LANES)
                yb = (ho[rs, ls] + hseq[rs, ls]) * _silu(_f32(ql, rs, ls))
                ya = (_f32(bl, rs, ls) * _conv(_f32(cl, rs, ls) * _f32(ul, rs, ls), wca_r, 3, 1, tmod, sub_r, ls)
                      * _silu(_f32(gl, rs, ls)))
                cat[rs, ls] = ya.astype(cat.dtype)
                cat[rs, pl.ds(W + l0, LANES)] = yb.astype(cat.dtype)

            _sub_loop(tl, sub_r, W, mix_out)

    scratch = [pltpu.VMEM((tl, W), F32), pltpu.VMEM((tl, 2 * W), F32), pltpu.VMEM((1, W), F32), pltpu.VMEM((1, W), F32)]
    f32_seq = jax.ShapeDtypeStruct((rows, W), F32)
    kept_gates = [f32_seq] + [jax.ShapeDtypeStruct((rows, W), BF16)] * 3
    if rev:
        in_specs = [pcol(j) for j in (0, 1, 2, 3, 5)] + [seq, seq, seq, taps, wd_spec, vec, vec, vec, vec]
        args = [P] * 5 + [saved0["h"], saved0["xb"], saved0["xb16"], wca, wd, ba, bx, lam, h_init]
        out_specs = [seq, pl.BlockSpec((tl, 2 * W), lambda i: (tile(i), 0))] + [seq] * 4
        out_shape = [f32_seq, jax.ShapeDtypeStruct((rows, 2 * W), BF16)] + kept_gates
    else:
        in_specs = [pcol(4), taps, vec, wd_spec, vec, vec, vec, vec]
        args = [P, wcb, bcb, wd, ba, bx, lam, h_init]
        out_specs = [seq] * 7
        out_shape = [f32_seq, f32_seq, jax.ShapeDtypeStruct((rows, W), BF16)] + kept_gates
    res = pl.pallas_call(
        body, name=name, grid=(nt,), in_specs=in_specs, out_specs=out_specs, out_shape=out_shape,
        scratch_shapes=scratch, compiler_params=_params(("arbitrary",)),
    )(*args)
    gates = dict(zip(("a", "r", "ig", "m2"), res[-4:]))
    if rev:
        return res[0], res[1], gates
    return dict(h=res[0], xb=res[1], xb16=res[2]), gates


_BWD_SCRATCH = ("dyl", "g", "dp16", "sp", "dlf", "edge", "c")
_FWD_SAVED = ("xb", "xb16", "a", "r", "ig", "m2")


def _bwd_scratch(tl, W):
    shapes = {"dyl": pltpu.VMEM((tl, W), F32), "g": pltpu.VMEM((tl, W), F32), "dp16": pltpu.VMEM((tl, 2 * W), BF16),
              "sp": pltpu.VMEM((1, W), F32), "dlf": pltpu.VMEM((1, W), F32), "edge": pltpu.VMEM((1, W), F32),
              "c": pltpu.VMEM((1, W), F32)}
    return [shapes[n] for n in _BWD_SCRATCH]


def _lru_bwd_tile(d, dy_fn, hs_ref, wd_r, lam_r, scr, acc, first, tl, sub_r, ng, gs):
    dwd_ref, dba_ref, dbx_ref, dlam_ref = acc
    W = hs_ref.shape[1]
    assert gs % LANES == 0
    rev = d == 0
    lam = lam_r[...]
    scr["sp"][...] = _softplus(-lam)
    scr["dlf"][...] = -_sigmoid(-lam)

    @pl.when(first)
    def _():
        dwd_ref[...] = jnp.zeros_like(dwd_ref)
        dba_ref[...] = jnp.zeros_like(dba_ref)
        dbx_ref[...] = jnp.zeros_like(dbx_ref)
        dlam_ref[...] = jnp.zeros_like(dlam_ref)

    def state_grad(r0, l0):
        rs, ls = pl.ds(r0, sub_r), pl.ds(l0, LANES)
        scr["dyl"][rs, ls] = dy_fn(rs, ls)

    _sub_loop(tl, sub_r, W, state_grad)

    def step(t, c):
        g = scr["dyl"][pl.ds(t, 1), :] + c
        scr["g"][pl.ds(t, 1), :] = g
        return scr["a"][pl.ds(t, 1), :] * g

    scr["c"][...] = _row_loop(tl, rev, step, scr["c"][...])
    row = lax.broadcasted_iota(jnp.int32, (sub_r, 1), 0)

    def grads(r0, l0):
        rs, ls = pl.ds(r0, sub_r), pl.ds(l0, LANES)
        g, a, m2 = scr["g"][rs, ls], scr["a"][rs, ls], _f32(scr["m2"], rs, ls)
        r, ig, xb = _f32(scr["r"], rs, ls), _f32(scr["ig"], rs, ls), scr["xb"][rs, ls]
        h = hs_ref[rs, ls]
        if d == 0:
            e0 = pl.multiple_of(jnp.maximum(r0 - SUBLANES, 0), SUBLANES)
            edge = jnp.where(r0 == 0, scr["edge"][:, ls], hs_ref[pl.ds(e0, SUBLANES), ls][SUBLANES - 1:, :])
            hprev = jnp.where(row == 0, edge, pltpu.roll(h, 1, 0))
        else:
            e0 = pl.multiple_of(jnp.minimum(r0 + sub_r, tl - SUBLANES), SUBLANES)
            edge = jnp.where(r0 == tl - sub_r, scr["edge"][:, ls], hs_ref[pl.ds(e0, SUBLANES), ls][:1, :])
            hprev = jnp.where(row == sub_r - 1, edge, pltpu.roll(h, sub_r - 1, 0))
        rsq = lax.rsqrt(m2)
        gm = g * (m2 * rsq)
        d_la = (g * hprev) * a - (g * (ig * xb)) * ((1.0 - m2) * rsq)
        d_pr = d_la * ((-LRU_C) * scr["sp"][:, ls]) * (r * (1.0 - r))
        d_pi = (gm * xb) * (ig * (1.0 - ig))
        scr["dyl"][rs, ls] = gm * ig
        dlam_ref[:, ls] += jnp.sum(d_la * ((-LRU_C) * r), axis=0, keepdims=True) * scr["dlf"][:, ls]
        dba_ref[:, ls] += jnp.sum(d_pr, axis=0, keepdims=True)
        dbx_ref[:, ls] += jnp.sum(d_pi, axis=0, keepdims=True)
        gi, off = divmod(l0, gs)
        scr["dp16"][rs, pl.ds(gi * 2 * gs + off, LANES)] = d_pr.astype(BF16)
        scr["dp16"][rs, pl.ds(gi * 2 * gs + gs + off, LANES)] = d_pi.astype(BF16)

    _sub_loop(tl, sub_r, W, grads)
    for gi in range(ng):
        dp = scr["dp16"][:, gi * 2 * gs:(gi + 1) * 2 * gs]
        scr["g"][:, gi * gs:(gi + 1) * gs] = lax.dot_general(dp, wd_r[gi], (((1,), (1,)), ((), ())),
                                                             preferred_element_type=F32)
        dwd_ref[gi] += lax.dot_general(scr["xb16"][:, gi * gs:(gi + 1) * gs], dp, (((0,), (0,)), ((), ())),
                                       preferred_element_type=F32)


def _edge_block(h, tl, nt, d):
    W = h.shape[1]
    per = tl // SUBLANES
    if d == 0:
        return pl.BlockSpec((SUBLANES, W), lambda i: (jnp.maximum((nt - 1 - i) * per - 1, 0), 0))
    return pl.BlockSpec((SUBLANES, W), lambda i: (jnp.minimum((i + 1) * per, nt * per - 1), 0))


def _mix_bwd0(P, dcat, saved0, gates0, h_init, c_init, wts, *, rows, row_off, row_w, tl, name):
    W = P.shape[1] // 6
    nt = rows // tl
    ob = row_off // tl
    gs = min(LRU_GROUP, W)
    ng = W // gs
    wd, lam = wts["wd"][0], wts["lam"][0]
    h0s = saved0["h"]
    kept = [saved0["xb"], saved0["xb16"]] + [gates0[n] for n in ("a", "r", "ig", "m2")]

    def tile(i):
        return nt - 1 - i

    vec = pl.BlockSpec((1, W), lambda i: (0, 0))
    wd_spec = pl.BlockSpec(wd.shape, lambda i: (0, 0, 0))
    seq = pl.BlockSpec((tl, W), lambda i: (tile(i), 0))

    sub_r = min(row_w, tl)
    assert tl % sub_r == 0

    def body(ql, dyb, hs, hedge8, xb_r, xb16_r, a_r, r_r, ig_r, m2_r, wd_r, lam_r, hin, cin,
             dxb_o, dwd_o, dba_o, dbx_o, dlam_o, cfin, *scratch):
        scr = dict(zip(_BWD_SCRATCH, scratch, strict=True))
        scr.update(zip(_FWD_SAVED, (xb_r, xb16_r, a_r, r_r, ig_r, m2_r), strict=True))
        i = pl.program_id(0)

        @pl.when(i == 0)
        def _():
            scr["c"][...] = cin[...]

        scr["edge"][...] = jnp.where(i == nt - 1, hin[...], hedge8[SUBLANES - 1:SUBLANES, :])
        _lru_bwd_tile(0, lambda rs, ls: _f32(dyb, rs, ls) * _silu(_f32(ql, rs, ls)), hs, wd_r, lam_r, scr,
                      (dwd_o, dba_o, dbx_o, dlam_o), i == 0, tl, sub_r, ng, gs)
        dxb_o[...] = scr["dyl"][...] + scr["g"][...]
        cfin[...] = scr["c"][...]

    return pl.pallas_call(
        body, name=name, grid=(nt,),
        in_specs=[pl.BlockSpec((tl, W), lambda i: (tile(i) + ob, 5)), pl.BlockSpec((tl, W), lambda i: (tile(i), 1)), seq,
                  _edge_block(h0s, tl, nt, 0)] + [seq] * 6 + [wd_spec, vec, vec, vec],
        out_specs=[seq, wd_spec, vec, vec, vec, vec],
        out_shape=[jax.ShapeDtypeStruct((rows, W), F32), jax.ShapeDtypeStruct(wd.shape, F32)]
        + [jax.ShapeDtypeStruct((1, W), F32)] * 4,
        scratch_shapes=_bwd_scratch(tl, W),
        compiler_params=_params(("arbitrary",)),
    )(P, dcat, h0s, h0s, *kept, wd, lam, h_init, c_init)


def _mix_bwd1(P, dcat, saved0, h1s, gates1, dxb0, h_init, c_init, wts, *, rows, row_off, row_w, tl, name,
              dp_rows=None, dp_off=0, dp_into=None):
    dp_rows = rows if dp_rows is None else dp_rows
    dpb = dp_off // tl
    W = P.shape[1] // 6
    nt = rows // tl
    ob = row_off // tl
    gs = min(LRU_GROUP, W)
    ng = W // gs
    wca, wcb = wts["wca"], wts["wcb"]
    wd, lam = wts["wd"][1], wts["lam"][1]
    h0s = saved0["h"]
    kept = [saved0["xb"], saved0["xb16"]] + [gates1[n] for n in ("a", "r", "ig", "m2")]

    vec = pl.BlockSpec((1, W), lambda i: (0, 0))
    taps = pl.BlockSpec((SUBLANES, W), lambda i: (0, 0))
    wd_spec = pl.BlockSpec(wd.shape, lambda i: (0, 0, 0))
    seq = pl.BlockSpec((tl, W), lambda i: (i, 0))

    sub_r = min(row_w, tl)
    assert tl % sub_r == 0

    def body(*refs):
        if dp_into is not None:
            refs = refs[1:]
        (bl, cl, ul, gl, vl, ql, dya, dyb, h0, h1, hedge8, dx0, xb_r, xb16_r, a_r, r_r, ig_r, m2_r, wca_r, wcb_r,
         wd_r, lam_r, hin, cin, dp_o, dwd_o, dba_o, dbx_o, dlam_o, dwca_o, dwcb_o, dbcb_o, cfin, *scratch) = refs
        scr = dict(zip(_BWD_SCRATCH, scratch, strict=True))
        scr.update(zip(_FWD_SAVED, (xb_r, xb16_r, a_r, r_r, ig_r, m2_r), strict=True))
        i = pl.program_id(0)

        @pl.when(i == 0)
        def _():
            scr["c"][...] = cin[...]
            dwca_o[...] = jnp.zeros_like(dwca_o)
            dwcb_o[...] = jnp.zeros_like(dwcb_o)
            dbcb_o[...] = jnp.zeros_like(dbcb_o)

        scr["edge"][...] = jnp.where(i == nt - 1, hin[...], hedge8[0:1, :])
        _lru_bwd_tile(1, lambda rs, ls: _f32(dyb, rs, ls) * _silu(_f32(ql, rs, ls)), h1, wd_r, lam_r, scr,
                      (dwd_o, dba_o, dbx_o, dlam_o), i == 0, tl, sub_r, ng, gs)
        cfin[...] = scr["c"][...]
        tmod = _tmod(sub_r, sub_r)

        def rest(r0, l0):
            rs, ls = pl.ds(r0, sub_r), pl.ds(l0, LANES)
            dxb = dx0[rs, ls] + scr["dyl"][rs, ls] + scr["g"][rs, ls]
            dv, dwb = _conv_bwd(dxb, _f32(vl, rs, ls), wcb_r, 4, 2, tmod, sub_r, ls)
            for j in range(4):
                dwcb_o[j:j + 1, ls] += dwb[j]
            dbcb_o[:, ls] += jnp.sum(dxb, axis=0, keepdims=True)
            q = _f32(ql, rs, ls)
            sq = _sigmoid(q)
            dq = _f32(dyb, rs, ls) * (h0[rs, ls] + h1[rs, ls]) * (sq * (1.0 + q * (1.0 - sq)))
            b_, c_, u_, g_ = _f32(bl, rs, ls), _f32(cl, rs, ls), _f32(ul, rs, ls), _f32(gl, rs, ls)
            z = c_ * u_
            cz = _conv(z, wca_r, 3, 1, tmod, sub_r, ls)
            sgm = _sigmoid(g_)
            sg = g_ * sgm
            da = _f32(dya, rs, ls)
            dz, dwa = _conv_bwd(da * b_ * sg, z, wca_r, 3, 1, tmod, sub_r, ls)
            for j in range(3):
                dwca_o[j:j + 1, ls] += dwa[j]
            parts = (da * cz * sg, dz * u_, dz * c_, da * b_ * cz * (sgm * (1.0 + g_ * (1.0 - sgm))), dv, dq)
            for k, val in enumerate(parts):
                dp_o[rs, pl.ds(k * W + l0, LANES)] = val.astype(dp_o.dtype)

        _sub_loop(tl, sub_r, W, rest)

    def pcol(j):
        return pl.BlockSpec((tl, W), lambda i: (i + ob, j))

    prev = [] if dp_into is None else [dp_into]
    return pl.pallas_call(
        body, name=name, grid=(nt,), input_output_aliases={} if dp_into is None else {0: 0},
        in_specs=[pl.BlockSpec(memory_space=pl.ANY)] * len(prev) + [pcol(j) for j in range(6)]
        + [pl.BlockSpec((tl, W), lambda i: (i, 0)), pl.BlockSpec((tl, W), lambda i: (i, 1)), seq, seq,
           _edge_block(h1s, tl, nt, 1), seq] + [seq] * 6 + [taps, taps, wd_spec, vec, vec, vec],
        out_specs=[pl.BlockSpec((tl, 6 * W), lambda i: (i + dpb, 0)), wd_spec, vec, vec, vec, taps, taps, vec, vec],
        out_shape=[jax.ShapeDtypeStruct((dp_rows, 6 * W), BF16), jax.ShapeDtypeStruct(wd.shape, F32)]
        + [jax.ShapeDtypeStruct((1, W), F32)] * 3
        + [jax.ShapeDtypeStruct((SUBLANES, W), F32)] * 2 + [jax.ShapeDtypeStruct((1, W), F32)] * 2,
        scratch_shapes=_bwd_scratch(tl, W),
        compiler_params=_params(("arbitrary",)),
    )(*prev, *([P] * 6), dcat, dcat, h0s, h1s, h1s, dxb0, *kept, wca, wcb, wd, lam, h_init, c_init)


def _loss_head(out, x, tgt, gt, fg, tr):
    L, D = x.shape

    def body(o_ref, x_ref, t_ref, gt_ref, fg_ref, dn_o, do_o, dfg_o, dgt_o, loss_o):
        i = pl.program_id(0)

        @pl.when(i == 0)
        def _():
            dfg_o[...] = jnp.zeros_like(dfg_o)
            dgt_o[...] = jnp.zeros_like(dgt_o)
            loss_o[...] = jnp.zeros_like(loss_o)

        o = o_ref[...].astype(F32)
        gt_v = gt_ref[...]
        fg_v = fg_ref[...]
        n = x_ref[...] + gt_v * o
        r = lax.rsqrt(jnp.mean(n * n, axis=-1, keepdims=True) + EPS)
        nr = n * r
        e = nr * fg_v - t_ref[...]
        loss_o[...] += 0.5 * jnp.sum(jnp.mean(e * e, axis=-1, keepdims=True))
        dy = e * (1.0 / D)
        dfg_o[...] += jnp.sum(dy * nr, axis=0, keepdims=True)
        qv = dy * fg_v
        dn = r * (qv - nr * jnp.mean(qv * nr, axis=-1, keepdims=True))
        dgt_o[...] += jnp.sum(dn * o, axis=0, keepdims=True)
        dn_o[...] = dn.astype(dn_o.dtype)
        do_o[...] = (dn * gt_v).astype(do_o.dtype)

    blk = pl.BlockSpec((tr, D), lambda i: (i, 0))
    vec = pl.BlockSpec((1, D), lambda i: (0, 0))
    return pl.pallas_call(
        body, name="loss_head", grid=(L // tr,), in_specs=[blk, blk, blk, vec, vec],
        out_specs=[blk, blk, vec, vec, pl.BlockSpec((SUBLANES, LANES), lambda i: (0, 0))],
        out_shape=[jax.ShapeDtypeStruct((L, D), BF16), jax.ShapeDtypeStruct((L, D), BF16),
                   jax.ShapeDtypeStruct((1, D), F32), jax.ShapeDtypeStruct((1, D), F32),
                   jax.ShapeDtypeStruct((SUBLANES, LANES), F32)],
        compiler_params=_params(("arbitrary",)),
    )(out, x, tgt, gt, fg)


def _norm_bwd(dhl, x, dn, g, sc, tr, name):
    L, D = x.shape
    with_x = dn is not None

    def body(*refs):
        if with_x:
            d_ref, x_ref, dn_ref, g_ref, sc_ref, gx_o, dsh_o, dsc_o, dg_o = refs
        else:
            d_ref, x_ref, g_ref, sc_ref, dsh_o, dsc_o, dg_o = refs
        i = pl.program_id(0)

        @pl.when(i == 0)
        def _():
            dsh_o[...] = jnp.zeros_like(dsh_o)
            dsc_o[...] = jnp.zeros_like(dsc_o)
            dg_o[...] = jnp.zeros_like(dg_o)

        d = d_ref[...].astype(F32)
        xv = x_ref[...]
        g_v = g_ref[...]
        r = lax.rsqrt(jnp.mean(xv * xv, axis=-1, keepdims=True) + EPS)
        xr = xv * r
        dsh_o[...] += jnp.sum(d, axis=0, keepdims=True)
        dsc_o[...] += jnp.sum(d * (xr * g_v), axis=0, keepdims=True)
        dxn = d * (1.0 + sc_ref[...])
        dg_o[...] += jnp.sum(dxn * xr, axis=0, keepdims=True)
        if with_x:
            qv = dxn * g_v
            gx_o[...] = r * (qv - xr * jnp.mean(qv * xr, axis=-1, keepdims=True)) + dn_ref[...].astype(F32)

    blk = pl.BlockSpec((tr, D), lambda i: (i, 0))
    vec = pl.BlockSpec((1, D), lambda i: (0, 0))
    vshape = jax.ShapeDtypeStruct((1, D), F32)
    res = pl.pallas_call(
        body, name=name, grid=(L // tr,),
        in_specs=[blk, blk] + ([blk] if with_x else []) + [vec, vec],
        out_specs=([blk] if with_x else []) + [vec, vec, vec],
        out_shape=([jax.ShapeDtypeStruct((L, D), F32)] if with_x else []) + [vshape] * 3,
        compiler_params=_params(("arbitrary",)),
    )(*([dhl, x] + ([dn] if with_x else []) + [g, sc]))
    return res if with_x else [None] + list(res)


def _pack_blockdiag(wa, wx, gs):
    H, hd, _ = wa.shape
    hp = gs // hd
    ng = H // hp
    on_diag = _diag_mask(gs, hd)

    def bd(w):
        return jnp.where(on_diag, jnp.tile(w.reshape(ng, gs, hd), (1, 1, hp)), 0.0)

    return jnp.concatenate([bd(wa), bd(wx)], axis=-1).astype(BF16)


def _diag_mask(gs, hd):
    idx = jnp.arange(gs) // hd
    return idx[:, None] == idx[None, :]


def _unpack_blockdiag(dwd, H, hd, gs):
    hp = gs // hd
    on_diag = _diag_mask(gs, hd)

    def diag(dm):
        kept = jnp.where(on_diag, dm, 0.0)
        return sum(kept[:, :, p * hd:(p + 1) * hd] for p in range(hp)).reshape(H, hd, hd)

    return diag(dwd[:, :, :gs]), diag(dwd[:, :, gs:])


def kernel(x, c, ctx, c_ctx, norm_g, w_ada, b_ada, w_in, w_conv_a, w_conv_b, b_conv_b, lru_wa, lru_ba, lru_wx, lru_bx, lru_lambda, w_out, final_g, loss_target, m_c_ctx, m_norm_g, m_w_ada, m_b_ada, m_w_in, m_w_conv_a, m_w_conv_b, m_b_conv_b, m_lru_wa, m_lru_ba, m_lru_wx, m_lru_bx, m_lru_lambda, m_w_out, m_final_g, v_c_ctx, v_norm_g, v_w_ada, v_b_ada, v_w_in, v_w_conv_a, v_w_conv_b, v_b_conv_b, v_lru_wa, v_lru_ba, v_lru_wx, v_lru_bx, v_lru_lambda, v_w_out, v_final_g):
    xi, yi, ci = _pos()
    me = 4 * xi + 2 * yi + ci
    q = 2 * xi + yi
    first_core = (ci == 0).astype(F32)

    L, D = x.shape[1], x.shape[2]
    T = ctx.shape[1]
    W = D // 2
    Wq = W // 4
    H, hd = lru_wa.shape[2], lru_wa.shape[3]
    gs = min(LRU_GROUP, W)
    nq = w_ada.shape[2]
    tl = min(256, T, L)
    tr = min(256, T, L)
    x2, ctx2, tgt2 = x[0], ctx[0], loss_target[0]

    def place(shard, full_cols):
        z = jnp.zeros((shard.shape[0], full_cols), F32)
        return lax.dynamic_update_slice(z, shard * first_core, (0, q * shard.shape[1]))

    c_rows = lax.dynamic_update_slice(jnp.zeros((8, D), F32), c, (me, 0))
    small_in = [c_rows, place(w_conv_a[0], W), place(w_conv_b[0], W), place(lru_ba[0], W), place(lru_bx[0], W),
                place(lru_lambda[0], W)]
    small_shapes = [a.shape for a in small_in]
    gathered = _allreduce8(_pack(small_in, 8 * SUBLANES), "gather_small")
    c_all, wca, wcb, ba_all, bx_all, lam_all = _unpack(gathered, small_shapes)

    s_rows = jnp.concatenate([c_all, c_ctx[None, :], jnp.zeros((7, D), F32)], axis=0)
    mod_part = _matmul(s_rows, w_ada[0], a_act="silu", bias=lax.dynamic_slice(b_ada, (0, q * nq), (1, nq)),
                       tm=16, tn=nq, tk=512, name="ada_fwd")
    mod_all = _allreduce8(_pack([place(mod_part[:9], 4 * nq)], 8 * SUBLANES), "gather_mod")
    mod_all = _unpack(mod_all, [(9, 4 * nq)])[0]
    mod_l = lax.dynamic_slice(mod_all, (me, 0), (1, 3 * D))
    mod_c = mod_all[8:9]
    sh_l, sc_l, gt_l = mod_l[:, :D], mod_l[:, D:2 * D], mod_l[:, 2 * D:]
    sh_c, sc_c = mod_c[:, :D], mod_c[:, D:2 * D]

    pad_taps = lambda w: jnp.pad(w, ((0, SUBLANES - w.shape[0]), (0, 0)))
    wts = {
        "wca": pad_taps(wca), "wcb": pad_taps(wcb), "bcb": b_conv_b,
        "wd": [_pack_blockdiag(lru_wa[0, d], lru_wx[0, d], gs) for d in range(2)],
        "ba": [ba_all[d:d + 1] for d in range(2)], "bx": [bx_all[d:d + 1] for d in range(2)],
        "lam": [lam_all[d:d + 1] for d in range(2)],
    }

    hl = _norm_in(x2, ctx2, norm_g, sc_l, sh_l, sc_c, sh_c, tr)
    p_lat, win_full, wout_full = _in_proj_gather(hl, w_in[0].astype(BF16), w_out[0].astype(BF16),
                                                 jnp.reshape(q, (1,)).astype(jnp.int32), rows=L, tm=min(1024, L))
    v_ctx = _matmul(hl, win_full, a_rows=T, a_off=L, n_range=(4 * W, W), tm=T, tn=W, tk=D, out_dtype=BF16,
                    name="in_proj_ctx")
    p_ctx = jnp.pad(v_ctx, ((0, 0), (4 * W, W)))
    zero_w = jnp.zeros((1, W), F32)
    ctx0, cgates0 = _mix_fwd(p_ctx, 0, zero_w, wts, rows=T, row_off=0, row_w=T, tl=tl, name="ctx_fwd0")
    c1s, _, cgates1 = _mix_fwd(p_ctx, 1, zero_w, wts, rows=T, row_off=0, row_w=T, tl=tl, saved0=ctx0, name="ctx_fwd1")
    h0_init, h1_init = ctx0["h"][T - 1:T], c1s[0:1]
    tl_tall = 2 * tl if L % (2 * tl) == 0 else tl
    lat0, gates0 = _mix_fwd(p_lat, 0, h0_init, wts, rows=L, row_off=0, row_w=GRID_W, tl=tl_tall, name="mix_fwd0")
    h1s, cat, gates1 = _mix_fwd(p_lat, 1, h1_init, wts, rows=L, row_off=0, row_w=GRID_W, tl=tl_tall, saved0=lat0,
                                name="mix_fwd1")
    out = _matmul(cat, wout_full, tm=1024, tn=D, tk=2 * W, out_dtype=BF16, name="out_proj")
    tr_lat = 2 * tr if L % (2 * tr) == 0 else tr
    dn, dout, dfg, dgt, loss_blk = _loss_head(out, x2, tgt2, gt_l, final_g[None, :], tr_lat)

    dcat = _matmul(dout, wout_full, tb=True, tm=1024, tn=2 * W, tk=D, out_dtype=BF16, name="out_proj_bwd")
    gw_out = _matmul(cat, dout, ta=True, tm=1024, tn=D, tk=2048, out_dtype=BF16, name="w_out_grad")
    dxb0, dwd0, dba0, dbx0, dlam0, ch0 = _mix_bwd0(p_lat, dcat, lat0, gates0, h0_init, zero_w, wts, rows=L, row_off=0,
                                                   row_w=GRID_W, tl=tl_tall, name="mix_bwd0")
    dp_lat, dwd1, dba1, dbx1, dlam1, dwca, dwcb, dbcb, ch1 = _mix_bwd1(
        p_lat, dcat, lat0, h1s, gates1, dxb0, h1_init, zero_w, wts, rows=L, row_off=0, row_w=GRID_W, tl=tl,
        name="mix_bwd1", dp_rows=L + T)
    zero_cat = jnp.zeros((T, 2 * W), BF16)
    cxb0, cwd0, cba0, cbx0, clam0, _ = _mix_bwd0(p_ctx, zero_cat, ctx0, cgates0, zero_w, ch0, wts, rows=T, row_off=0,
                                                 row_w=T, tl=tl, name="ctx_bwd0")
    dp, cwd1, cba1, cbx1, clam1, cwca, cwcb, cbcb, _ = _mix_bwd1(
        p_ctx, zero_cat, ctx0, c1s, cgates1, cxb0, zero_w, ch1, wts, rows=T, row_off=0, row_w=T, tl=tl, name="ctx_bwd1",
        dp_rows=L + T, dp_off=L, dp_into=dp_lat)

    gw_in = _matmul(hl, dp, ta=True, tm=1024, tn=1536, tk=2816, out_dtype=BF16, name="w_in_grad")
    rs_axes = [1, 0]
    pair_sums = _rs_pair_sums([gw_in, gw_out], rs_axes)
    dhc = _matmul(dp, win_full, tb=True, a_rows=T, a_off=L, k_range=(4 * W, W), tm=T, tn=D, tk=W,
                  name="in_proj_bwd_ctx")
    _, dsh_c, dsc_c, dng_c = _norm_bwd(dhc, ctx2, None, norm_g, sc_c, tr, "norm_bwd_ctx")
    zeros_d = jnp.zeros((1, D), F32)
    dmod_c = jnp.concatenate([dsh_c, dsc_c, zeros_d], axis=1)
    dhl, (*rs_slots, dmod_c_all) = _matmul(
        dp, win_full, tb=True, a_rows=L, tm=512, tn=D, tk=6 * W, out_dtype=BF16, name="in_proj_bwd",
        side=_join_sides(_rs_chips_side(pair_sums), _gather_block_side(jnp.pad(dmod_c, ((0, SUBLANES - 1), (0, 0))))))
    gx, dsh_l, dsc_l, dng_l = _norm_bwd(dhl, x2, dn, norm_g, sc_l, tr_lat, "norm_bwd")
    gc_rows = _matmul(lax.dynamic_slice(dmod_c_all.reshape(8 * SUBLANES, 3 * D), (0, q * nq), (8 * SUBLANES, nq)), w_ada[0],
                      tb=True, dsilu_mul=c_ctx[None, :], tm=8 * SUBLANES, tn=D, tk=512, name="c_ctx_grad")
    gc_part = jnp.sum(gc_rows, axis=0, keepdims=True) * first_core

    g_in_shard, g_out_shard = _rs_finish(rs_slots, rs_axes)

    dwa0, dwx0 = _unpack_blockdiag(dwd0 + cwd0, H, hd, gs)
    dwa1, dwx1 = _unpack_blockdiag(dwd1 + cwd1, H, hd, gs)
    dmod_l = jnp.concatenate([dsh_l, dsc_l, dgt], axis=1)
    small_g = [
        lax.dynamic_update_slice(jnp.zeros((8, 3 * D), F32), dmod_l, (me, 0)), dmod_c,
        dfg, dng_l + dng_c, (dwca + cwca)[:3], (dwcb + cwcb)[:4], dbcb + cbcb,
        jnp.stack([dwa0, dwa1]), jnp.stack([dwx0, dwx1]),
        jnp.concatenate([dba0 + cba0, dba1 + cba1], axis=0), jnp.concatenate([dbx0 + cbx0, dbx1 + cbx1], axis=0),
        jnp.concatenate([dlam0 + clam0, dlam1 + clam1], axis=0), loss_blk[0:1, 0:1], gc_part,
    ]
    g_shapes = [a.shape for a in small_g]
    (g_rows, g_modc, g_fg, g_ng, g_wca, g_wcb, g_bcb, g_wa, g_wx, g_ba, g_bx, g_lam, loss_sum, g_c_ctx) = _unpack(
        _allreduce8_two_level(_pack(small_g, 8 * SUBLANES), "reduce_small"), g_shapes)

    g_mod = jnp.concatenate([g_rows, g_modc, jnp.zeros((7, 3 * D), F32)], axis=0)
    g_mod_q = lax.dynamic_slice(g_mod, (0, q * nq), (16, nq))
    g_w_ada = _matmul(s_rows, g_mod_q, ta=True, a_act="silu", tm=1024, tn=nq, tk=16, name="w_ada_grad")
    g_b_ada = jnp.sum(g_mod[:9], axis=0, keepdims=True)

    def shard_cols(a, width):
        return lax.dynamic_slice(a, (0, q * width), (a.shape[0], width))

    grads = {
        "c_ctx": g_c_ctx, "norm_g": g_ng, "b_ada": g_b_ada,
        "w_conv_a": shard_cols(g_wca, Wq)[None], "w_conv_b": shard_cols(g_wcb, Wq)[None], "b_conv_b": g_bcb,
        "lru_wa": g_wa[None], "lru_ba": shard_cols(g_ba, Wq)[None], "lru_wx": g_wx[None],
        "lru_bx": shard_cols(g_bx, Wq)[None], "lru_lambda": shard_cols(g_lam, Wq)[None], "final_g": g_fg[0],
    }
    small_names = list(grads)
    given = dict(c_ctx=(c_ctx, m_c_ctx, v_c_ctx), norm_g=(norm_g, m_norm_g, v_norm_g), b_ada=(b_ada, m_b_ada, v_b_ada),
                 w_conv_a=(w_conv_a, m_w_conv_a, v_w_conv_a), w_conv_b=(w_conv_b, m_w_conv_b, v_w_conv_b),
                 b_conv_b=(b_conv_b, m_b_conv_b, v_b_conv_b), lru_wa=(lru_wa, m_lru_wa, v_lru_wa),
                 lru_ba=(lru_ba, m_lru_ba, v_lru_ba), lru_wx=(lru_wx, m_lru_wx, v_lru_wx),
                 lru_bx=(lru_bx, m_lru_bx, v_lru_bx), lru_lambda=(lru_lambda, m_lru_lambda, v_lru_lambda),
                 final_g=(final_g, m_final_g, v_final_g))
    def rows2d(a):
        return a.reshape(-1, a.shape[-1])

    grads = {n: grads[n].reshape(given[n][0].shape) for n in small_names}
    quads = [tuple(rows2d(a) for a in (given[n][0], grads[n], given[n][1], given[n][2])) for n in small_names]
    updated = _adam_many(quads, "adam_small")
    delta_s, newm_s, newv_s = ({n: u[j].reshape(given[n][0].shape) for n, u in zip(small_names, updated)} for j in range(3))

    big = {"w_ada": (w_ada, g_w_ada, m_w_ada, v_w_ada), "w_in": (w_in, g_in_shard, m_w_in, v_w_in),
           "w_out": (w_out, g_out_shard, m_w_out, v_w_out)}
    delta_b, newm_b, newv_b = {}, {}, {}
    for n, (w, g, m, v) in big.items():
        d_, m_, v_, *echo = _adam(w[0], g, m[0], v[0], "adam_" + n, echo_g=n != "w_ada")
        grads[n] = (echo[0] if echo else g)[None]
        delta_b[n], newm_b[n], newv_b[n] = d_[None], m_[None], v_[None]

    loss = loss_sum[0, 0]
    order = ["c_ctx", "norm_g", "w_ada", "b_ada", "w_in", "w_conv_a", "w_conv_b", "b_conv_b", "lru_wa", "lru_ba",
             "lru_wx", "lru_bx", "lru_lambda", "w_out", "final_g"]
    delta = {**delta_s, **delta_b}
    newm = {**newm_s, **newm_b}
    newv = {**newv_s, **newv_b}
    return (loss, gx[None], *[grads[n] for n in order], *[delta[n] for n in order], *[newm[n] for n in order],
            *[newv[n] for n in order])
```

```python
import jax
import jax.numpy as jnp
from jax import lax
from jax.experimental import pallas as pl
from jax.experimental.pallas import tpu as pltpu

F32 = jnp.float32
BF16 = jnp.bfloat16
MESH_ID = pl.DeviceIdType.MESH

EPS = 1e-6
LRU_C = 8.0
GRID_W = 64
ADAM_LR = 0.001
ADAM_B1 = 0.9
ADAM_B2 = 0.999
ADAM_EPS = 1e-08
ADAM_WD = 0.01
ADAM_STEP = 10

LANES = 128
SUBLANES = 8
PACK_COLS = 1024
VMEM_LIMIT = 56 * 2**20
LRU_GROUP = 256


def _params(sem=None):
    return pltpu.CompilerParams(vmem_limit_bytes=VMEM_LIMIT, dimension_semantics=sem)


def _pick(dim, pref, quantum=LANES):
    if dim <= pref:
        return dim
    best = None
    for t in range(quantum, pref + 1, quantum):
        if dim % t == 0:
            best = t
    assert best is not None, (dim, pref)
    return best


def _pos():
    return lax.axis_index("x"), lax.axis_index("y"), lax.axis_index("c")


def _flip(v, bit):
    return 1 - v if bit else v


def _sigmoid(v):
    return 0.5 * jnp.tanh(0.5 * v) + 0.5


def _silu(v):
    return v * _sigmoid(v)


def _dsilu(v):
    s = _sigmoid(v)
    return s * (1.0 + v * (1.0 - s))


def _gates(pre_r, pre_i, sp):
    r = _sigmoid(pre_r)
    ig = _sigmoid(pre_i)
    e = LRU_C * r * sp
    w = jnp.tanh(e)
    return r, ig, jnp.exp(-e), (2.0 * w) * pl.reciprocal(1.0 + w, approx=True)


def _softplus(z):
    return jnp.maximum(z, 0.0) + jnp.log1p(jnp.exp(-jnp.abs(z)))


def _matmul(a, b, *, ta=False, tb=False, tm=512, tn=512, tk=512, out_dtype=F32, name,
            a_rows=None, a_off=0, n_range=None, k_range=None, a_act=None, bias=None, dsilu_mul=None, side=None):
    rows_a = a.shape[0] if a_rows is None else a_rows
    if ta:
        K, M = rows_a, a.shape[1]
    else:
        M, K = rows_a, a.shape[1]
    N = b.shape[0] if tb else b.shape[1]
    n_off, k_off = 0, 0
    if n_range is not None:
        n_off, N = n_range
    if k_range is not None:
        assert not ta
        k_off, K = k_range
    tm, tn, tk = _pick(M, tm, SUBLANES), _pick(N, tn), _pick(K, tk)
    t_rows = tk if ta else tm
    assert a_off % t_rows == 0 and n_off % tn == 0 and k_off % tk == 0
    nk = K // tk
    gi, gj = M // tm, N // tn
    off_blocks, nb, kb = a_off // t_rows, n_off // tn, k_off // tk
    dims = (((0 if ta else 1,), (1 if tb else 0,)), ((), ()))
    extras = [e for e in (bias, dsilu_mul) if e is not None]
    n_sin = len(side["ins"]) if side else 0
    n_sout = len(side["outs"]) if side else 0

    def body(a_ref, b_ref, *rest):
        rest = list(rest)
        bias_ref = rest.pop(0) if bias is not None else None
        dsm_ref = rest.pop(0) if dsilu_mul is not None else None
        side_in = [rest.pop(0) for _ in range(n_sin)]
        o_ref = rest.pop(0)
        side_out = [rest.pop(0) for _ in range(n_sout)]
        acc_ref = rest.pop(0) if nk > 1 else None
        side_scr = rest
        i, j, k = pl.program_id(0), pl.program_id(1), pl.program_id(2)

        if side:
            @pl.when((i == 0) & (j == 0) & (k == 0))
            def _():
                side["start"](side_in, side_out, side_scr)

        av = a_ref[...]
        if a_act == "silu":
            av = _silu(av)
        prod = lax.dot_general(av, b_ref[...], dims, preferred_element_type=F32)

        def finish(r):
            if bias_ref is not None:
                r = r + bias_ref[...]
            if dsm_ref is not None:
                r = r * _dsilu(dsm_ref[...])
            o_ref[...] = r.astype(o_ref.dtype)

        if nk == 1:
            finish(prod)
        else:
            @pl.when(k == 0)
            def _():
                acc_ref[...] = prod

            @pl.when(k > 0)
            def _():
                acc_ref[...] += prod

            @pl.when(k == nk - 1)
            def _():
                finish(acc_ref[...])

        if side:
            @pl.when((i == gi - 1) & (j == gj - 1) & (k == nk - 1))
            def _():
                side["finish"](side_in, side_out, side_scr)

    if ta:
        a_spec = pl.BlockSpec((tk, tm), lambda i, j, k: (k + off_blocks, i))
    else:
        a_spec = pl.BlockSpec((tm, tk), lambda i, j, k: (i + off_blocks, k + kb))
    once = dict(pipeline_mode=pl.Buffered(1)) if (gj == 1 and nk == 1) else {}
    if tb:
        b_spec = pl.BlockSpec((tn, tk), lambda i, j, k: (j + nb, k + kb), **once)
    else:
        b_spec = pl.BlockSpec((tk, tn), lambda i, j, k: (k + kb, j + nb), **once)
    in_specs = [a_spec, b_spec]
    if bias is not None:
        in_specs.append(pl.BlockSpec((1, tn), lambda i, j, k: (0, j)))
    if dsilu_mul is not None:
        in_specs.append(pl.BlockSpec((1, tn), lambda i, j, k: (0, j)))
    hbm = pl.BlockSpec(memory_space=pl.ANY)
    res = pl.pallas_call(
        body, name=name, grid=(gi, gj, nk),
        in_specs=in_specs + [hbm] * n_sin,
        out_specs=[pl.BlockSpec((tm, tn), lambda i, j, k: (i, j))] + [hbm] * n_sout,
        out_shape=[jax.ShapeDtypeStruct((M, N), out_dtype)] + (list(side["outs"]) if side else []),
        scratch_shapes=([pltpu.VMEM((tm, tn), F32)] if nk > 1 else []) + (list(side["scratch"]) if side else []),
        compiler_params=_params(("arbitrary",) * 3 if side else ("parallel", "parallel", "arbitrary")),
    )(a, b, *extras, *(side["ins"] if side else []))
    return (res[0], res[1:]) if side else res[0]


def _elementwise(fn, ins, outs, *, rows, cols, name, tr=256):
    tr = _pick(rows, tr, 2 * SUBLANES)
    n_in = len(ins)

    def body(*refs):
        vals = fn(*[r[...] for r in refs[:n_in]])
        if not isinstance(vals, (tuple, list)):
            vals = (vals,)
        for r, v in zip(refs[n_in:], vals, strict=True):
            r[...] = v.astype(r.dtype)

    def spec(off):
        assert off % tr == 0
        ob = off // tr
        return pl.BlockSpec((tr, cols), lambda i: (i + ob, 0))

    res = pl.pallas_call(
        body, name=name, grid=(rows // tr,),
        in_specs=[spec(off) for _, off in ins],
        out_specs=[spec(0) for _ in outs],
        out_shape=[jax.ShapeDtypeStruct((rows, cols), dt) for dt in outs],
        compiler_params=_params(("parallel",)),
    )(*[a for a, _ in ins])
    return res


def _adam_math(w, g, m, v):
    m = ADAM_B1 * m + (1.0 - ADAM_B1) * g
    v = ADAM_B2 * v + (1.0 - ADAM_B2) * (g * g)
    m_hat = m / (1.0 - ADAM_B1 ** ADAM_STEP)
    v_hat = v / (1.0 - ADAM_B2 ** ADAM_STEP)
    delta = -ADAM_LR * (m_hat / (jnp.sqrt(v_hat) + ADAM_EPS) + ADAM_WD * w)
    return delta, m, v


def _adam(w, g, m, v, name, echo_g=False):
    rows, cols = w.shape
    fn = (lambda w_, g_, m_, v_: _adam_math(w_, g_, m_, v_) + (g_,)) if echo_g else _adam_math
    return _elementwise(fn, [(w, 0), (g, 0), (m, 0), (v, 0)], [F32] * (4 if echo_g else 3),
                        rows=rows, cols=cols, name=name)


def _adam_many(quads, name):
    n = len(quads)

    def body(*refs):
        ins, outs = refs[:4 * n], refs[4 * n:]
        for t in range(n):
            w, g, m, v = (r[...] for r in ins[4 * t:4 * t + 4])
            for o_ref, val in zip(outs[3 * t:3 * t + 3], _adam_math(w, g, m, v), strict=True):
                o_ref[...] = val

    res = pl.pallas_call(
        body, name=name,
        out_shape=[jax.ShapeDtypeStruct(q[0].shape, F32) for q in quads for _ in range(3)],
        compiler_params=_params(),
    )(*[a for q in quads for a in q])
    return [res[3 * t:3 * t + 3] for t in range(n)]


def _pack(arrs, row_quantum):
    flat = jnp.concatenate([a.reshape(-1).astype(F32) for a in arrs])
    n = flat.shape[0]
    q = row_quantum * PACK_COLS
    total = -(-n // q) * q
    flat = jnp.pad(flat, (0, total - n))
    return flat.reshape(total // PACK_COLS, PACK_COLS)


def _unpack(buf, shapes):
    flat = buf.reshape(-1)
    out, off = [], 0
    for s in shapes:
        n = 1
        for d in s:
            n *= d
        out.append(flat[off:off + n].reshape(s))
        off += n
    return out


def _allreduce8(buf, name):
    R, C = buf.shape
    assert R % (8 * SUBLANES) == 0
    m = R // 8

    def body(x_ref, o_ref, recv, red, s1, r1, s2, r2):
        x, y, c = _pos()
        me = 4 * x + 2 * y + c

        def peer(k):
            px, py, pc = _flip(x, (k >> 2) & 1), _flip(y, (k >> 1) & 1), _flip(c, k & 1)
            return (px, py, pc), 4 * px + 2 * py + pc

        def rows(ref, idx):
            return ref.at[pl.ds(pl.multiple_of(idx * m, SUBLANES), m), :]

        def scatter(k):
            dev, p = peer(k)
            return pltpu.make_async_remote_copy(src_ref=rows(x_ref, p), dst_ref=recv.at[k], send_sem=s1.at[k],
                                                recv_sem=r1.at[k], device_id=dev, device_id_type=MESH_ID)

        def share(k):
            dev, p = peer(k)
            return pltpu.make_async_remote_copy(src_ref=red, dst_ref=rows(o_ref, me), send_sem=s2.at[k],
                                                recv_sem=r2.at[k], device_id=dev, device_id_type=MESH_ID)

        def shared_from(k):
            dev, p = peer(k)
            return pltpu.make_async_remote_copy(src_ref=red, dst_ref=rows(o_ref, p), send_sem=s2.at[k],
                                                recv_sem=r2.at[k], device_id=dev, device_id_type=MESH_ID)

        for k in range(1, 8):
            scatter(k).start()
        acc = rows(x_ref, me)[...]
        for k in range(1, 8):
            scatter(k).wait_recv()
            acc = acc + recv[k]
        red[...] = acc
        rows(o_ref, me)[...] = acc
        for k in range(1, 8):
            share(k).start()
        for k in range(1, 8):
            shared_from(k).wait_recv()
        for k in range(1, 8):
            scatter(k).wait_send()
            share(k).wait_send()

    return pl.pallas_call(
        body, name=name,
        in_specs=[pl.BlockSpec(memory_space=pltpu.VMEM)],
        out_specs=pl.BlockSpec(memory_space=pltpu.VMEM),
        out_shape=jax.ShapeDtypeStruct((R, C), F32),
        scratch_shapes=[pltpu.VMEM((8, m, C), F32), pltpu.VMEM((m, C), F32),
                        pltpu.SemaphoreType.DMA((8,)), pltpu.SemaphoreType.DMA((8,)),
                        pltpu.SemaphoreType.DMA((8,)), pltpu.SemaphoreType.DMA((8,))],
        compiler_params=_params(),
    )(buf)


def _allreduce8_two_level(buf, name):
    R, C = buf.shape
    assert R % (8 * SUBLANES) == 0
    m, hr = R // 8, R // 2

    def body(x_ref, o_ref, got0, half, got1, red, ssem, rsem):
        x, y, c = _pos()
        q = 2 * x + y
        sibling = (x, y, 1 - c)

        def half_rows(ref, core):
            return ref.at[pl.ds(pl.multiple_of(core * hr, SUBLANES), hr), :]

        def chunk(ref, core, chip):
            return ref.at[pl.ds(pl.multiple_of(core * hr + chip * m, SUBLANES), m), :]

        def chip_of(k):
            px, py = _flip(x, (k >> 1) & 1), _flip(y, k & 1)
            return (px, py, c), 2 * px + py

        def copy(src, dst, phase, k, dev):
            return pltpu.make_async_remote_copy(src_ref=src, dst_ref=dst, send_sem=ssem.at[phase, k],
                                                recv_sem=rsem.at[phase, k], device_id=dev, device_id_type=MESH_ID)

        swap = copy(half_rows(x_ref, 1 - c), got0, 0, 0, sibling)
        swap.start()
        swap.wait()
        half[...] = half_rows(x_ref, c)[...] + got0[...]

        def scatter(k):
            dev, p = chip_of(k)
            return copy(half.at[pl.ds(pl.multiple_of(p * m, SUBLANES), m), :], got1.at[k], 1, k, dev)

        for k in range(1, 4):
            scatter(k).start()
        acc = half[pl.ds(pl.multiple_of(q * m, SUBLANES), m), :]
        for k in range(1, 4):
            scatter(k).wait_recv()
            acc = acc + got1[k]
        red[...] = acc
        chunk(o_ref, c, q)[...] = acc

        def share(k, landing_chip):
            return copy(red, chunk(o_ref, c, landing_chip), 2, k, chip_of(k)[0])

        for k in range(1, 4):
            share(k, q).start()
        for k in range(1, 4):
            share(k, chip_of(k)[1]).wait_recv()
        back = copy(half_rows(o_ref, c), half_rows(o_ref, c), 3, 0, sibling)
        back.start()
        copy(half_rows(o_ref, 1 - c), half_rows(o_ref, 1 - c), 3, 0, sibling).wait_recv()
        back.wait_send()
        for k in range(1, 4):
            scatter(k).wait_send()
            share(k, q).wait_send()

    return pl.pallas_call(
        body, name=name,
        in_specs=[pl.BlockSpec(memory_space=pltpu.VMEM)],
        out_specs=pl.BlockSpec(memory_space=pltpu.VMEM),
        out_shape=jax.ShapeDtypeStruct((R, C), F32),
        scratch_shapes=[pltpu.VMEM((hr, C), F32), pltpu.VMEM((hr, C), F32), pltpu.VMEM((4, m, C), F32), pltpu.VMEM((m, C), F32),
                        pltpu.SemaphoreType.DMA((4, 4)), pltpu.SemaphoreType.DMA((4, 4))],
        compiler_params=_params(),
    )(buf)


def _bounce(src, dst, buf, sem):
    cin = pltpu.make_async_copy(src, buf, sem)
    cin.start()
    cin.wait()
    cout = pltpu.make_async_copy(buf, dst, sem)
    cout.start()
    cout.wait()


def _chunk(ref, axis, idx, size):
    start = idx * size
    if axis == 0:
        return ref.at[pl.ds(start, size), :]
    return ref.at[:, pl.ds(start, size)]


def _in_proj_gather(hl, win, wout, q_arr, *, rows, tm):
    D, nq = win.shape
    dq, D2 = wout.shape
    ni = rows // tm
    ops = ((0, 1, nq, D // 2), (1, 0, dq, dq // 2))

    def body(q_ref, a_ref, win_ref, wout_ref, p_ref, gin_ref, gout_ref, b_scr, buf_out, lsem, ssem, rsem, fsem, gsem):
        j, i = pl.program_id(0), pl.program_id(1)
        x, y, c = _pos()
        q = 2 * x + y
        srcs = (win_ref, wout_ref)
        dsts = (gin_ref, gout_ref)

        def shard_window(o, chip):
            _, axis, size, _ = ops[o]
            return _chunk(dsts[o], axis, chip, size)

        def half(ref, o, core):
            return ref.at[pl.ds(core * ops[o][3], ops[o][3]), :]

        def half_window(o, chip, core):
            _, axis, size, hs = ops[o]
            if axis == 1:
                return dsts[o].at[pl.ds(core * hs, hs), pl.ds(chip * size, size)]
            return dsts[o].at[pl.ds(chip * size + core * hs, hs), :]

        def chip_of(k):
            px, py = _flip(x, (k >> 1) & 1), _flip(y, k & 1)
            return px, py, 2 * px + py

        def send(o, k):
            px, py, _ = chip_of(k)
            return pltpu.make_async_remote_copy(
                src_ref=half(srcs[o], o, c), dst_ref=half_window(o, q, c), send_sem=ssem.at[o, k],
                recv_sem=rsem.at[o, k], device_id=(px, py, c), device_id_type=MESH_ID)

        def chip_recv(o, k):
            px, py, pq = chip_of(k)
            landed = half_window(o, pq, c)
            pltpu.make_async_remote_copy(src_ref=landed, dst_ref=landed, send_sem=ssem.at[o, k], recv_sem=rsem.at[o, k],
                                         device_id=(px, py, c), device_id_type=MESH_ID).wait_recv()

        def to_sibling(o, k):
            landed = half_window(o, chip_of(k)[2], c)
            return pltpu.make_async_remote_copy(src_ref=landed, dst_ref=landed, send_sem=fsem.at[o, k],
                                                recv_sem=gsem.at[o, k], device_id=(x, y, 1 - c), device_id_type=MESH_ID)

        def from_sibling(o, k):
            theirs = half_window(o, chip_of(k)[2], 1 - c)
            pltpu.make_async_remote_copy(src_ref=theirs, dst_ref=theirs, send_sem=fsem.at[o, k], recv_sem=gsem.at[o, k],
                                         device_id=(x, y, 1 - c), device_id_type=MESH_ID).wait_recv()

        def relay(o, core):
            if core == 0:
                landed, target = half_window(o, chip_of(2)[2], 0), (x, 1 - y, 0)
            else:
                landed, target = half_window(o, chip_of(1)[2], 1), (1 - x, y, 1)
            return pltpu.make_async_remote_copy(src_ref=landed, dst_ref=landed, send_sem=ssem.at[o, 3],
                                                recv_sem=rsem.at[o, 3], device_id=target, device_id_type=MESH_ID)

        def on_core(core, fn):
            @pl.when(c == core)
            def _():
                fn()

        def land(o, k):
            chip_recv(o, k)
            if k == 2:
                on_core(0, lambda: relay(o, 0).start())
            if k == 1:
                on_core(1, lambda: relay(o, 1).start())
            to_sibling(o, k).start()

        def settle(o, k):
            from_sibling(o, k)
            to_sibling(o, k).wait_send()

        def b_load(k, slot):
            src = win_ref if k == 0 else shard_window(0, chip_of(k)[2])
            return pltpu.make_async_copy(src, b_scr.at[slot], lsem.at[0])

        def own_store():
            return pltpu.make_async_copy(b_scr.at[0], shard_window(0, q), lsem.at[2])

        order = (0, 2, 1, 3)
        early = max(ni - 2, 0)

        @pl.when((j == 0) & (i == 0))
        def _():
            for o in range(2):
                for k in (2, 1):
                    send(o, k).start()
            first = b_load(0, 0)
            first.start()
            first.wait()
            own_store().start()
            _bounce(wout_ref, shard_window(1, q), buf_out, lsem.at[1])

        for jj in range(3):
            nxt = order[jj + 1]

            @pl.when((j == jj) & (i == early))
            def _(nxt=nxt):
                land(0, nxt)

            @pl.when((j == jj) & (i == ni - 1))
            def _(jj=jj, nxt=nxt):
                settle(0, nxt)
                if jj == 1:
                    own_store().wait()
                b_load(nxt, (jj + 1) % 2).start()

            @pl.when((j == jj + 1) & (i == 0))
            def _(jj=jj, nxt=nxt):
                b_load(nxt, (jj + 1) % 2).wait()

        @pl.when((j == 3) & (i == 0))
        def _():
            land(1, 2)
            land(1, 1)

        p_ref[...] = jnp.dot(a_ref[...], b_scr[j % 2], preferred_element_type=F32).astype(p_ref.dtype)

        @pl.when((j == 3) & (i == ni - 1))
        def _():
            settle(1, 2)
            settle(1, 1)
            land(1, 3)
            settle(1, 3)
            for o in range(2):
                for k in (2, 1):
                    send(o, k).wait_send()
                for core in range(2):
                    on_core(core, lambda o=o, core=core: relay(o, core).wait_send())

    hbm = pl.BlockSpec(memory_space=pl.ANY)
    grid_spec = pltpu.PrefetchScalarGridSpec(
        num_scalar_prefetch=1, grid=(4, ni),
        in_specs=[pl.BlockSpec((tm, D), lambda j, i, qr: (i, 0)), hbm, hbm],
        out_specs=[pl.BlockSpec((tm, nq), lambda j, i, qr: (i, jnp.bitwise_xor(qr[0], ((j & 1) << 1) | (j >> 1)))),
                   hbm, hbm],
        scratch_shapes=[pltpu.VMEM((2,) + win.shape, win.dtype), pltpu.VMEM(wout.shape, wout.dtype), pltpu.SemaphoreType.DMA((3,))]
        + [pltpu.SemaphoreType.DMA((2, 4)) for _ in range(4)])
    return pl.pallas_call(
        body, name="in_proj_gather", grid_spec=grid_spec,
        out_shape=[jax.ShapeDtypeStruct((rows, 4 * nq), BF16), jax.ShapeDtypeStruct((D, 4 * nq), win.dtype),
                   jax.ShapeDtypeStruct((4 * dq, D2), wout.dtype)],
        compiler_params=_params(("arbitrary", "arbitrary")),
    )(q_arr, hl, win, wout)


def _rs_to_sibling(gs, axes):
    n = len(gs)
    shapes = []
    for g, ax in zip(gs, axes):
        s = list(g.shape)
        s[ax] //= 8
        shapes.append(tuple(s))

    def body(*refs):
        g_refs, mine, landed = refs[:n], refs[n:2 * n], refs[2 * n:3 * n]
        bufs = refs[3 * n:4 * n]
        lsem, ssem, rsem = refs[4 * n:]
        x, y, c = _pos()
        cps = []
        for o in range(n):
            size = shapes[o][axes[o]]
            for j in range(4):
                rc = pltpu.make_async_remote_copy(
                    src_ref=_chunk(g_refs[o], axes[o], 2 * j + 1 - c, size), dst_ref=landed[o].at[j],
                    send_sem=ssem.at[o, j], recv_sem=rsem.at[o, j], device_id=(x, y, 1 - c), device_id_type=MESH_ID)
                rc.start()
                cps.append(rc)
        for o in range(n):
            size = shapes[o][axes[o]]
            loads = [pltpu.make_async_copy(_chunk(g_refs[o], axes[o], 2 * j + c, size), bufs[o].at[j % 2], lsem.at[o, j % 2])
                     for j in range(4)]
            stores = [pltpu.make_async_copy(bufs[o].at[j % 2], mine[o].at[j], lsem.at[o, 2 + j % 2]) for j in range(4)]
            loads[0].start()
            for j in range(4):
                loads[j].wait()
                stores[j].start()
                if j + 1 < 4:
                    if j >= 1:
                        stores[j - 1].wait()
                    loads[j + 1].start()
            stores[2].wait()
            stores[3].wait()
        for rc in cps:
            rc.wait()

    hbm = pl.BlockSpec(memory_space=pl.ANY)
    outs = [jax.ShapeDtypeStruct((4,) + s, g.dtype) for s, g in zip(shapes, gs)]
    res = pl.pallas_call(
        body, name="rs_to_sibling", in_specs=[hbm] * n, out_specs=[hbm] * (2 * n), out_shape=outs + outs,
        scratch_shapes=[pltpu.VMEM((2,) + s, g.dtype) for s, g in zip(shapes, gs)]
        + [pltpu.SemaphoreType.DMA((n, 4)), pltpu.SemaphoreType.DMA((n, 4)), pltpu.SemaphoreType.DMA((n, 4))],
        compiler_params=_params(),
    )(*gs)
    return res[:n], res[n:]


def _gather_block_side(block):
    r, n_cols = block.shape

    def copies(ins, outs, scr):
        ssem, rsem = scr[1], scr[2]
        x, y, c = _pos()
        return [pltpu.make_async_remote_copy(
            src_ref=ins[0], dst_ref=outs[0].at[k], send_sem=ssem.at[k], recv_sem=rsem.at[k],
            device_id=(_flip(x, (k >> 2) & 1), _flip(y, (k >> 1) & 1), _flip(c, k & 1)), device_id_type=MESH_ID)
            for k in range(1, 8)]

    def start(ins, outs, scr):
        for cp in copies(ins, outs, scr):
            cp.start()

    def finish(ins, outs, scr):
        _bounce(ins[0], outs[0].at[0], scr[0], scr[3].at[0])
        for cp in copies(ins, outs, scr):
            cp.wait()

    return dict(ins=[block], outs=[jax.ShapeDtypeStruct((8, r, n_cols), block.dtype)],
                scratch=[pltpu.VMEM((r, n_cols), block.dtype), pltpu.SemaphoreType.DMA((8,)), pltpu.SemaphoreType.DMA((8,)),
                         pltpu.SemaphoreType.DMA((1,))],
                start=start, finish=finish)


def _join_sides(a, b):
    na_i, na_o, na_s = len(a["ins"]), len(a["outs"]), len(a["scratch"])

    def run(which):
        def fn(ins, outs, scr):
            a[which](ins[:na_i], outs[:na_o], scr[:na_s])
            b[which](ins[na_i:], outs[na_o:], scr[na_s:])
        return fn

    return dict(ins=a["ins"] + b["ins"], outs=a["outs"] + b["outs"], scratch=a["scratch"] + b["scratch"],
                start=run("start"), finish=run("finish"))


def _rs_chips_side(parts):
    n = len(parts)

    def copies(p_refs, slots, scr):
        ssem, rsem = scr[n + 1], scr[n + 2]
        x, y, c = _pos()
        cps = []
        for o in range(n):
            for k in range(1, 4):
                px, py = _flip(x, (k >> 1) & 1), _flip(y, k & 1)
                cps.append(pltpu.make_async_remote_copy(
                    src_ref=p_refs[o].at[2 * px + py], dst_ref=slots[o].at[k], send_sem=ssem.at[o, k],
                    recv_sem=rsem.at[o, k], device_id=(px, py, c), device_id_type=MESH_ID))
        return cps

    def start(p_refs, slots, scr):
        for cp in copies(p_refs, slots, scr):
            cp.start()

    def finish(p_refs, slots, scr):
        x, y, _ = _pos()
        q = 2 * x + y
        for o in range(n):
            _bounce(p_refs[o].at[q], slots[o].at[0], scr[o], scr[n].at[o])
        for cp in copies(p_refs, slots, scr):
            cp.wait()

    return dict(
        ins=list(parts), outs=[jax.ShapeDtypeStruct(p.shape, p.dtype) for p in parts],
        scratch=[pltpu.VMEM(p.shape[1:], p.dtype) for p in parts]
        + [pltpu.SemaphoreType.DMA((n,)), pltpu.SemaphoreType.DMA((n, 4)), pltpu.SemaphoreType.DMA((n, 4))],
        start=start, finish=finish)


def _rs_share(rs, axes):
    n = len(rs)
    shapes = []
    for r, ax in zip(rs, axes):
        s = list(r.shape)
        s[ax] *= 2
        shapes.append(tuple(s))

    def body(*refs):
        r_refs, outs = refs[:n], refs[n:2 * n]
        bufs = refs[2 * n:3 * n]
        lsem, ssem, rsem = refs[3 * n:]
        x, y, c = _pos()
        cps = []
        for o in range(n):
            size = r_refs[o].shape[axes[o]]
            window = _chunk(outs[o], axes[o], c, size)
            rc = pltpu.make_async_remote_copy(src_ref=r_refs[o], dst_ref=window, send_sem=ssem.at[o], recv_sem=rsem.at[o],
                                              device_id=(x, y, 1 - c), device_id_type=MESH_ID)
            rc.start()
            cps.append(rc)
        for o in range(n):
            size = r_refs[o].shape[axes[o]]
            _bounce(r_refs[o], _chunk(outs[o], axes[o], c, size), bufs[o], lsem.at[o])
        for cp in cps:
            cp.wait()

    hbm = pl.BlockSpec(memory_space=pl.ANY)
    return pl.pallas_call(
        body, name="rs_share", in_specs=[hbm] * n, out_specs=[hbm] * n,
        out_shape=[jax.ShapeDtypeStruct(s, r.dtype) for s, r in zip(shapes, rs)],
        scratch_shapes=[pltpu.VMEM(r.shape, r.dtype) for r in rs] + [pltpu.SemaphoreType.DMA((n,)) for _ in range(3)],
        compiler_params=_params(),
    )(*rs)


def _rs_pair_sums(gs, axes):
    mine, landed = _rs_to_sibling(gs, axes)
    pair_sums = []
    for o, (mi, la) in enumerate(zip(mine, landed)):
        rows, cols = mi.shape[0] * mi.shape[1], mi.shape[2]
        s = _elementwise(lambda a, b: a.astype(F32) + b.astype(F32), [(mi.reshape(rows, cols), 0), (la.reshape(rows, cols), 0)],
                         [BF16], rows=rows, cols=cols, name=f"rs_pair_sum{o}")[0]
        pair_sums.append(s.reshape(mi.shape))
    return pair_sums


def _rs_finish(slots, axes):
    reduced = []
    for o, sl in enumerate(slots):
        rows, cols = sl.shape[1], sl.shape[2]
        flat = sl.reshape(4 * rows, cols)
        r = _elementwise(lambda a, b, c, d: (a.astype(F32) + b.astype(F32)) + (c.astype(F32) + d.astype(F32)),
                         [(flat, k * rows) for k in range(4)], [F32], rows=rows, cols=cols, name=f"rs_chip_sum{o}")[0]
        reduced.append(r)
    return _rs_share(reduced, axes)


def _norm_in(x, ctx, g, sc_l, sh_l, sc_c, sh_c, tr):
    L, D = x.shape
    T = ctx.shape[0]
    nx, nc = L // tr, T // tr

    def body(x_ref, c_ref, g_ref, scl, shl, scc, shc, o_ref):
        i = pl.program_id(0)

        def run(src, sc, sh):
            v = src[...]
            r = lax.rsqrt(jnp.mean(v * v, axis=-1, keepdims=True) + EPS)
            o_ref[...] = ((v * r * g_ref[...]) * (1.0 + sc[...]) + sh[...]).astype(o_ref.dtype)

        @pl.when(i < nx)
        def _():
            run(x_ref, scl, shl)

        @pl.when(i >= nx)
        def _():
            run(c_ref, scc, shc)

    vec = pl.BlockSpec((1, D), lambda i: (0, 0))
    return pl.pallas_call(
        body, name="norm_in", grid=(nx + nc,),
        in_specs=[pl.BlockSpec((tr, D), lambda i: (jnp.minimum(i, nx - 1), 0)),
                  pl.BlockSpec((tr, D), lambda i: (jnp.maximum(i - nx, 0), 0)), vec, vec, vec, vec, vec],
        out_specs=pl.BlockSpec((tr, D), lambda i: (i, 0)),
        out_shape=jax.ShapeDtypeStruct((L + T, D), BF16),
        compiler_params=_params(("arbitrary",)),
    )(x, ctx, g, sc_l, sh_l, sc_c, sh_c)


def _tmod(tl, row_w):
    assert row_w & (row_w - 1) == 0
    return lax.broadcasted_iota(jnp.int32, (tl, 1), 0) & (row_w - 1)


def _shift(z, k, tmod, row_w):
    tl = z.shape[0]
    rolled = pltpu.roll(z, k % tl, 0)
    mask = (tmod >= k) if k > 0 else (tmod < row_w + k)
    return jnp.where(mask, rolled, 0.0)


def _conv(z, w_ref, taps, left, tmod, row_w, lanes=slice(None)):
    out = None
    for j in range(taps):
        k = left - j
        term = (z if k == 0 else _shift(z, k, tmod, row_w)) * w_ref[j:j + 1, lanes]
        out = term if out is None else out + term
    return out


def _conv_bwd(dz, z, w_ref, taps, left, tmod, row_w, lanes=slice(None)):
    din = None
    dws = []
    for j in range(taps):
        k = left - j
        shifted = dz if k == 0 else _shift(dz, -k, tmod, row_w)
        term = shifted * w_ref[j:j + 1, lanes]
        din = term if din is None else din + term
        dws.append(jnp.sum(shifted * z, axis=0, keepdims=True))
    return din, dws


def _gate_matmul(xb16_ref, wd_ref, pre_scr, W, ng, gs):
    for g in range(ng):
        pg = jnp.dot(xb16_ref[:, g * gs:(g + 1) * gs], wd_ref[g], preferred_element_type=F32)
        pre_scr[:, g * gs:(g + 1) * gs] = pg[:, :gs]
        pre_scr[:, W + g * gs:W + (g + 1) * gs] = pg[:, gs:]


def _f32(ref, rows, lanes):
    return ref[rows, lanes].astype(F32)


def _sub_loop(tl, sub_r, W, fn):
    def chunk(ci, carry):
        r0 = pl.multiple_of(ci * sub_r, sub_r)
        for lb in range(W // LANES):
            fn(r0, lb * LANES)
        return carry

    lax.fori_loop(0, tl // sub_r, chunk, 0)


def _row_loop(tl, rev, step, init):
    nchunk = tl // SUBLANES

    def chunk(j, carry):
        jj = (nchunk - 1 - j) if rev else j
        c0 = pl.multiple_of(jj * SUBLANES, SUBLANES)
        for r in (range(SUBLANES - 1, -1, -1) if rev else range(SUBLANES)):
            carry = step(c0 + r, carry)
        return carry

    return lax.fori_loop(0, nchunk, chunk, init)


def _mix_fwd(P, d, h_init, wts, *, rows, row_off, row_w, tl, saved0=None, name):
    W = P.shape[1] // 6
    nt = rows // tl
    ob = row_off // tl
    rev = d == 1
    gs = min(LRU_GROUP, W)
    ng = W // gs
    wca, wcb, bcb = wts["wca"], wts["wcb"], wts["bcb"]
    wd, ba, bx, lam = wts["wd"][d], wts["ba"][d], wts["bx"][d], wts["lam"][d]

    def tile(i):
        return (nt - 1 - i) if rev else i

    def pcol(j):
        return pl.BlockSpec((tl, W), lambda i: (tile(i) + ob, j))

    vec = pl.BlockSpec((1, W), lambda i: (0, 0))
    taps = pl.BlockSpec((SUBLANES, W), lambda i: (0, 0))
    wd_spec = pl.BlockSpec(wd.shape, lambda i: (0, 0, 0))
    seq = pl.BlockSpec((tl, W), lambda i: (tile(i), 0))

    sub_r = min(row_w, tl)
    assert tl % sub_r == 0

    def body(*refs):
        if rev:
            (bl, cl, ul, gl, ql, ho, xb_r, xb16_r, wca_r, wd_r, ba_r, bx_r, lam_r, hin, hseq, cat, a_o, r_o, ig_o, m2_o,
             b_scr, pre_scr, carry, sp_scr) = refs
        else:
            (vl, wcb_r, bcb_r, wd_r, ba_r, bx_r, lam_r, hin, hseq, xb_r, xb16_r, a_o, r_o, ig_o, m2_o,
             b_scr, pre_scr, carry, sp_scr) = refs
        i = pl.program_id(0)

        @pl.when(i == 0)
        def _():
            carry[...] = hin[...]

        sp_scr[...] = _softplus(-lam_r[...])
        tmod = _tmod(sub_r, sub_r)

        def conv_in(r0, l0):
            rs, ls = pl.ds(r0, sub_r), pl.ds(l0, LANES)
            xb = _conv(_f32(vl, rs, ls), wcb_r, 4, 2, tmod, sub_r, ls) + bcb_r[:, ls]
            xb_r[rs, ls] = xb
            xb16_r[rs, ls] = xb.astype(BF16)

        def gates(r0, l0):
            rs, ls = pl.ds(r0, sub_r), pl.ds(l0, LANES)
            r, ig, a, m2 = _gates(pre_scr[rs, ls] + ba_r[:, ls], pre_scr[rs, pl.ds(W + l0, LANES)] + bx_r[:, ls],
                                  sp_scr[:, ls])
            a_o[rs, ls] = a
            r_o[rs, ls] = r.astype(r_o.dtype)
            ig_o[rs, ls] = ig.astype(ig_o.dtype)
            m2_o[rs, ls] = m2.astype(m2_o.dtype)
            m = jnp.where(m2 > 0.0, m2 * lax.rsqrt(m2), 0.0)
            b_scr[rs, ls] = m * (ig * xb_r[rs, ls])

        if not rev:
            _sub_loop(tl, sub_r, W, conv_in)
        _gate_matmul(xb16_r, wd_r, pre_scr, W, ng, gs)
        _sub_loop(tl, sub_r, W, gates)

        def step(t, h):
            h = a_o[pl.ds(t, 1), :] * h + b_scr[pl.ds(t, 1), :]
            hseq[pl.ds(t, 1), :] = h
            return h

        carry[...] = _row_loop(tl, rev, step, carry[...])

        if rev:
            def mix_out(r0, l0):
                rs, ls = pl.ds(r0, sub_r), pl.ds(l0, LANES)
                yb = (ho[rs, ls] + hseq[rs, ls]) * _silu(_f32(ql, rs, ls))
                ya = (_f32(bl, rs, ls) * _conv(_f32(cl, rs, ls) * _f32(ul, rs, ls), wca_r, 3, 1, tmod, sub_r, ls)
                      * _silu(_f32(gl, rs, ls)))
                cat[rs, ls] = ya.astype(cat.dtype)
                cat[rs, pl.ds(W + l0, LANES)] = yb.astype(cat.dtype)

            _sub_loop(tl, sub_r, W, mix_out)

    scratch = [pltpu.VMEM((tl, W), F32), pltpu.VMEM((tl, 2 * W), F32), pltpu.VMEM((1, W), F32), pltpu.VMEM((1, W), F32)]
    f32_seq = jax.ShapeDtypeStruct((rows, W), F32)
    kept_gates = [f32_seq] + [jax.ShapeDtypeStruct((rows, W), BF16)] * 3
    if rev:
        in_specs = [pcol(j) for j in (0, 1, 2, 3, 5)] + [seq, seq, seq, taps, wd_spec, vec, vec, vec, vec]
        args = [P] * 5 + [saved0["h"], saved0["xb"], saved0["xb16"], wca, wd, ba, bx, lam, h_init]
        out_specs = [seq, pl.BlockSpec((tl, 2 * W), lambda i: (tile(i), 0))] + [seq] * 4
        out_shape = [f32_seq, jax.ShapeDtypeStruct((rows, 2 * W), BF16)] + kept_gates
    else:
        in_specs = [pcol(4), taps, vec, wd_spec, vec, vec, vec, vec]
        args = [P, wcb, bcb, wd, ba, bx, lam, h_init]
        out_specs = [seq] * 7
        out_shape = [f32_seq, f32_seq, jax.ShapeDtypeStruct((rows, W), BF16)] + kept_gates
    res = pl.pallas_call(
        body, name=name, grid=(nt,), in_specs=in_specs, out_specs=out_specs, out_shape=out_shape,
        scratch_shapes=scratch, compiler_params=_params(("arbitrary",)),
    )(*args)
    gates = dict(zip(("a", "r", "ig", "m2"), res[-4:]))
    if rev:
        return res[0], res[1], gates
    return dict(h=res[0], xb=res[1], xb16=res[2]), gates


_BWD_SCRATCH = ("dyl", "g", "dp16", "sp", "dlf", "edge", "c")
_FWD_SAVED = ("xb", "xb16", "a", "r", "ig", "m2")


def _bwd_scratch(tl, W):
    shapes = {"dyl": pltpu.VMEM((tl, W), F32), "g": pltpu.VMEM((tl, W), F32), "dp16": pltpu.VMEM((tl, 2 * W), BF16),
              "sp": pltpu.VMEM((1, W), F32), "dlf": pltpu.VMEM((1, W), F32), "edge": pltpu.VMEM((1, W), F32),
              "c": pltpu.VMEM((1, W), F32)}
    return [shapes[n] for n in _BWD_SCRATCH]


def _lru_bwd_tile(d, dy_fn, hs_ref, wd_r, lam_r, scr, acc, first, tl, sub_r, ng, gs):
    dwd_ref, dba_ref, dbx_ref, dlam_ref = acc
    W = hs_ref.shape[1]
    assert gs % LANES == 0
    rev = d == 0
    lam = lam_r[...]
    scr["sp"][...] = _softplus(-lam)
    scr["dlf"][...] = -_sigmoid(-lam)

    @pl.when(first)
    def _():
        dwd_ref[...] = jnp.zeros_like(dwd_ref)
        dba_ref[...] = jnp.zeros_like(dba_ref)
        dbx_ref[...] = jnp.zeros_like(dbx_ref)
        dlam_ref[...] = jnp.zeros_like(dlam_ref)

    def state_grad(r0, l0):
        rs, ls = pl.ds(r0, sub_r), pl.ds(l0, LANES)
        scr["dyl"][rs, ls] = dy_fn(rs, ls)

    _sub_loop(tl, sub_r, W, state_grad)

    def step(t, c):
        g = scr["dyl"][pl.ds(t, 1), :] + c
        scr["g"][pl.ds(t, 1), :] = g
        return scr["a"][pl.ds(t, 1), :] * g

    scr["c"][...] = _row_loop(tl, rev, step, scr["c"][...])
    row = lax.broadcasted_iota(jnp.int32, (sub_r, 1), 0)

    def grads(r0, l0):
        rs, ls = pl.ds(r0, sub_r), pl.ds(l0, LANES)
        g, a, m2 = scr["g"][rs, ls], scr["a"][rs, ls], _f32(scr["m2"], rs, ls)
        r, ig, xb = _f32(scr["r"], rs, ls), _f32(scr["ig"], rs, ls), scr["xb"][rs, ls]
        h = hs_ref[rs, ls]
        if d == 0:
            e0 = pl.multiple_of(jnp.maximum(r0 - SUBLANES, 0), SUBLANES)
            edge = jnp.where(r0 == 0, scr["edge"][:, ls], hs_ref[pl.ds(e0, SUBLANES), ls][SUBLANES - 1:, :])
            hprev = jnp.where(row == 0, edge, pltpu.roll(h, 1, 0))
        else:
            e0 = pl.multiple_of(jnp.minimum(r0 + sub_r, tl - SUBLANES), SUBLANES)
            edge = jnp.where(r0 == tl - sub_r, scr["edge"][:, ls], hs_ref[pl.ds(e0, SUBLANES), ls][:1, :])
            hprev = jnp.where(row == sub_r - 1, edge, pltpu.roll(h, sub_r - 1, 0))
        rsq = lax.rsqrt(m2)
        gm = g * (m2 * rsq)
        d_la = (g * hprev) * a - (g * (ig * xb)) * ((1.0 - m2) * rsq)
        d_pr = d_la * ((-LRU_C) * scr["sp"][:, ls]) * (r * (1.0 - r))
        d_pi = (gm * xb) * (ig * (1.0 - ig))
        scr["dyl"][rs, ls] = gm * ig
        dlam_ref[:, ls] += jnp.sum(d_la * ((-LRU_C) * r), axis=0, keepdims=True) * scr["dlf"][:, ls]
        dba_ref[:, ls] += jnp.sum(d_pr, axis=0, keepdims=True)
        dbx_ref[:, ls] += jnp.sum(d_pi, axis=0, keepdims=True)
        gi, off = divmod(l0, gs)
        scr["dp16"][rs, pl.ds(gi * 2 * gs + off, LANES)] = d_pr.astype(BF16)
        scr["dp16"][rs, pl.ds(gi * 2 * gs + gs + off, LANES)] = d_pi.astype(BF16)

    _sub_loop(tl, sub_r, W, grads)
    for gi in range(ng):
        dp = scr["dp16"][:, gi * 2 * gs:(gi + 1) * 2 * gs]
        scr["g"][:, gi * gs:(gi + 1) * gs] = lax.dot_general(dp, wd_r[gi], (((1,), (1,)), ((), ())),
                                                             preferred_element_type=F32)
        dwd_ref[gi] += lax.dot_general(scr["xb16"][:, gi * gs:(gi + 1) * gs], dp, (((0,), (0,)), ((), ())),
                                       preferred_element_type=F32)


def _edge_block(h, tl, nt, d):
    W = h.shape[1]
    per = tl // SUBLANES
    if d == 0:
        return pl.BlockSpec((SUBLANES, W), lambda i: (jnp.maximum((nt - 1 - i) * per - 1, 0), 0))
    return pl.BlockSpec((SUBLANES, W), lambda i: (jnp.minimum((i + 1) * per, nt * per - 1), 0))


def _mix_bwd0(P, dcat, saved0, gates0, h_init, c_init, wts, *, rows, row_off, row_w, tl, name):
    W = P.shape[1] // 6
    nt = rows // tl
    ob = row_off // tl
    gs = min(LRU_GROUP, W)
    ng = W // gs
    wd, lam = wts["wd"][0], wts["lam"][0]
    h0s = saved0["h"]
    kept = [saved0["xb"], saved0["xb16"]] + [gates0[n] for n in ("a", "r", "ig", "m2")]

    def tile(i):
        return nt - 1 - i

    vec = pl.BlockSpec((1, W), lambda i: (0, 0))
    wd_spec = pl.BlockSpec(wd.shape, lambda i: (0, 0, 0))
    seq = pl.BlockSpec((tl, W), lambda i: (tile(i), 0))

    sub_r = min(row_w, tl)
    assert tl % sub_r == 0

    def body(ql, dyb, hs, hedge8, xb_r, xb16_r, a_r, r_r, ig_r, m2_r, wd_r, lam_r, hin, cin,
             dxb_o, dwd_o, dba_o, dbx_o, dlam_o, cfin, *scratch):
        scr = dict(zip(_BWD_SCRATCH, scratch, strict=True))
        scr.update(zip(_FWD_SAVED, (xb_r, xb16_r, a_r, r_r, ig_r, m2_r), strict=True))
        i = pl.program_id(0)

        @pl.when(i == 0)
        def _():
            scr["c"][...] = cin[...]

        scr["edge"][...] = jnp.where(i == nt - 1, hin[...], hedge8[SUBLANES - 1:SUBLANES, :])
        _lru_bwd_tile(0, lambda rs, ls: _f32(dyb, rs, ls) * _silu(_f32(ql, rs, ls)), hs, wd_r, lam_r, scr,
                      (dwd_o, dba_o, dbx_o, dlam_o), i == 0, tl, sub_r, ng, gs)
        dxb_o[...] = scr["dyl"][...] + scr["g"][...]
        cfin[...] = scr["c"][...]

    return pl.pallas_call(
        body, name=name, grid=(nt,),
        in_specs=[pl.BlockSpec((tl, W), lambda i: (tile(i) + ob, 5)), pl.BlockSpec((tl, W), lambda i: (tile(i), 1)), seq,
                  _edge_block(h0s, tl, nt, 0)] + [seq] * 6 + [wd_spec, vec, vec, vec],
        out_specs=[seq, wd_spec, vec, vec, vec, vec],
        out_shape=[jax.ShapeDtypeStruct((rows, W), F32), jax.ShapeDtypeStruct(wd.shape, F32)]
        + [jax.ShapeDtypeStruct((1, W), F32)] * 4,
        scratch_shapes=_bwd_scratch(tl, W),
        compiler_params=_params(("arbitrary",)),
    )(P, dcat, h0s, h0s, *kept, wd, lam, h_init, c_init)


def _mix_bwd1(P, dcat, saved0, h1s, gates1, dxb0, h_init, c_init, wts, *, rows, row_off, row_w, tl, name,
              dp_rows=None, dp_off=0, dp_into=None):
    dp_rows = rows if dp_rows is None else dp_rows
    dpb = dp_off // tl
    W = P.shape[1] // 6
    nt = rows // tl
    ob = row_off // tl
    gs = min(LRU_GROUP, W)
    ng = W // gs
    wca, wcb = wts["wca"], wts["wcb"]
    wd, lam = wts["wd"][1], wts["lam"][1]
    h0s = saved0["h"]
    kept = [saved0["xb"], saved0["xb16"]] + [gates1[n] for n in ("a", "r", "ig", "m2")]

    vec = pl.BlockSpec((1, W), lambda i: (0, 0))
    taps = pl.BlockSpec((SUBLANES, W), lambda i: (0, 0))
    wd_spec = pl.BlockSpec(wd.shape, lambda i: (0, 0, 0))
    seq = pl.BlockSpec((tl, W), lambda i: (i, 0))

    sub_r = min(row_w, tl)
    assert tl % sub_r == 0

    def body(*refs):
        if dp_into is not None:
            refs = refs[1:]
        (bl, cl, ul, gl, vl, ql, dya, dyb, h0, h1, hedge8, dx0, xb_r, xb16_r, a_r, r_r, ig_r, m2_r, wca_r, wcb_r,
         wd_r, lam_r, hin, cin, dp_o, dwd_o, dba_o, dbx_o, dlam_o, dwca_o, dwcb_o, dbcb_o, cfin, *scratch) = refs
        scr = dict(zip(_BWD_SCRATCH, scratch, strict=True))
        scr.update(zip(_FWD_SAVED, (xb_r, xb16_r, a_r, r_r, ig_r, m2_r), strict=True))
        i = pl.program_id(0)

        @pl.when(i == 0)
        def _():
            scr["c"][...] = cin[...]
            dwca_o[...] = jnp.zeros_like(dwca_o)
            dwcb_o[...] = jnp.zeros_like(dwcb_o)
            dbcb_o[...] = jnp.zeros_like(dbcb_o)

        scr["edge"][...] = jnp.where(i == nt - 1, hin[...], hedge8[0:1, :])
        _lru_bwd_tile(1, lambda rs, ls: _f32(dyb, rs, ls) * _silu(_f32(ql, rs, ls)), h1, wd_r, lam_r, scr,
                      (dwd_o, dba_o, dbx_o, dlam_o), i == 0, tl, sub_r, ng, gs)
        cfin[...] = scr["c"][...]
        tmod = _tmod(sub_r, sub_r)

        def rest(r0, l0):
            rs, ls = pl.ds(r0, sub_r), pl.ds(l0, LANES)
            dxb = dx0[rs, ls] + scr["dyl"][rs, ls] + scr["g"][rs, ls]
            dv, dwb = _conv_bwd(dxb, _f32(vl, rs, ls), wcb_r, 4, 2, tmod, sub_r, ls)
            for j in range(4):
                dwcb_o[j:j + 1, ls] += dwb[j]
            dbcb_o[:, ls] += jnp.sum(dxb, axis=0, keepdims=True)
            q = _f32(ql, rs, ls)
            sq = _sigmoid(q)
            dq = _f32(dyb, rs, ls) * (h0[rs, ls] + h1[rs, ls]) * (sq * (1.0 + q * (1.0 - sq)))
            b_, c_, u_, g_ = _f32(bl, rs, ls), _f32(cl, rs, ls), _f32(ul, rs, ls), _f32(gl, rs, ls)
            z = c_ * u_
            cz = _conv(z, wca_r, 3, 1, tmod, sub_r, ls)
            sgm = _sigmoid(g_)
            sg = g_ * sgm
            da = _f32(dya, rs, ls)
            dz, dwa = _conv_bwd(da * b_ * sg, z, wca_r, 3, 1, tmod, sub_r, ls)
            for j in range(3):
                dwca_o[j:j + 1, ls] += dwa[j]
            parts = (da * cz * sg, dz * u_, dz * c_, da * b_ * cz * (sgm * (1.0 + g_ * (1.0 - sgm))), dv, dq)
            for k, val in enumerate(parts):
                dp_o[rs, pl.ds(k * W + l0, LANES)] = val.astype(dp_o.dtype)

        _sub_loop(tl, sub_r, W, rest)

    def pcol(j):
        return pl.BlockSpec((tl, W), lambda i: (i + ob, j))

    prev = [] if dp_into is None else [dp_into]
    return pl.pallas_call(
        body, name=name, grid=(nt,), input_output_aliases={} if dp_into is None else {0: 0},
        in_specs=[pl.BlockSpec(memory_space=pl.ANY)] * len(prev) + [pcol(j) for j in range(6)]
        + [pl.BlockSpec((tl, W), lambda i: (i, 0)), pl.BlockSpec((tl, W), lambda i: (i, 1)), seq, seq,
           _edge_block(h1s, tl, nt, 1), seq] + [seq] * 6 + [taps, taps, wd_spec, vec, vec, vec],
        out_specs=[pl.BlockSpec((tl, 6 * W), lambda i: (i + dpb, 0)), wd_spec, vec, vec, vec, taps, taps, vec, vec],
        out_shape=[jax.ShapeDtypeStruct((dp_rows, 6 * W), BF16), jax.ShapeDtypeStruct(wd.shape, F32)]
        + [jax.ShapeDtypeStruct((1, W), F32)] * 3
        + [jax.ShapeDtypeStruct((SUBLANES, W), F32)] * 2 + [jax.ShapeDtypeStruct((1, W), F32)] * 2,
        scratch_shapes=_bwd_scratch(tl, W),
        compiler_params=_params(("arbitrary",)),
    )(*prev, *([P] * 6), dcat, dcat, h0s, h1s, h1s, dxb0, *kept, wca, wcb, wd, lam, h_init, c_init)


def _loss_head(out, x, tgt, gt, fg, tr):
    L, D = x.shape

    def body(o_ref, x_ref, t_ref, gt_ref, fg_ref, dn_o, do_o, dfg_o, dgt_o, loss_o):
        i = pl.program_id(0)

        @pl.when(i == 0)
        def _():
            dfg_o[...] = jnp.zeros_like(dfg_o)
            dgt_o[...] = jnp.zeros_like(dgt_o)
            loss_o[...] = jnp.zeros_like(loss_o)

        o = o_ref[...].astype(F32)
        gt_v = gt_ref[...]
        fg_v = fg_ref[...]
        n = x_ref[...] + gt_v * o
        r = lax.rsqrt(jnp.mean(n * n, axis=-1, keepdims=True) + EPS)
        nr = n * r
        e = nr * fg_v - t_ref[...]
        loss_o[...] += 0.5 * jnp.sum(jnp.mean(e * e, axis=-1, keepdims=True))
        dy = e * (1.0 / D)
        dfg_o[...] += jnp.sum(dy * nr, axis=0, keepdims=True)
        qv = dy * fg_v
        dn = r * (qv - nr * jnp.mean(qv * nr, axis=-1, keepdims=True))
        dgt_o[...] += jnp.sum(dn * o, axis=0, keepdims=True)
        dn_o[...] = dn.astype(dn_o.dtype)
        do_o[...] = (dn * gt_v).astype(do_o.dtype)

    blk = pl.BlockSpec((tr, D), lambda i: (i, 0))
    vec = pl.BlockSpec((1, D), lambda i: (0, 0))
    return pl.pallas_call(
        body, name="loss_head", grid=(L // tr,), in_specs=[blk, blk, blk, vec, vec],
        out_specs=[blk, blk, vec, vec, pl.BlockSpec((SUBLANES, LANES), lambda i: (0, 0))],
        out_shape=[jax.ShapeDtypeStruct((L, D), BF16), jax.ShapeDtypeStruct((L, D), BF16),
                   jax.ShapeDtypeStruct((1, D), F32), jax.ShapeDtypeStruct((1, D), F32),
                   jax.ShapeDtypeStruct((SUBLANES, LANES), F32)],
        compiler_params=_params(("arbitrary",)),
    )(out, x, tgt, gt, fg)


def _norm_bwd(dhl, x, dn, g, sc, tr, name):
    L, D = x.shape
    with_x = dn is not None

    def body(*refs):
        if with_x:
            d_ref, x_ref, dn_ref, g_ref, sc_ref, gx_o, dsh_o, dsc_o, dg_o = refs
        else:
            d_ref, x_ref, g_ref, sc_ref, dsh_o, dsc_o, dg_o = refs
        i = pl.program_id(0)

        @pl.when(i == 0)
        def _():
            dsh_o[...] = jnp.zeros_like(dsh_o)
            dsc_o[...] = jnp.zeros_like(dsc_o)
            dg_o[...] = jnp.zeros_like(dg_o)

        d = d_ref[...].astype(F32)
        xv = x_ref[...]
        g_v = g_ref[...]
        r = lax.rsqrt(jnp.mean(xv * xv, axis=-1, keepdims=True) + EPS)
        xr = xv * r
        dsh_o[...] += jnp.sum(d, axis=0, keepdims=True)
        dsc_o[...] += jnp.sum(d * (xr * g_v), axis=0, keepdims=True)
        dxn = d * (1.0 + sc_ref[...])
        dg_o[...] += jnp.sum(dxn * xr, axis=0, keepdims=True)
        if with_x:
            qv = dxn * g_v
            gx_o[...] = r * (qv - xr * jnp.mean(qv * xr, axis=-1, keepdims=True)) + dn_ref[...].astype(F32)

    blk = pl.BlockSpec((tr, D), lambda i: (i, 0))
    vec = pl.BlockSpec((1, D), lambda i: (0, 0))
    vshape = jax.ShapeDtypeStruct((1, D), F32)
    res = pl.pallas_call(
        body, name=name, grid=(L // tr,),
        in_specs=[blk, blk] + ([blk] if with_x else []) + [vec, vec],
        out_specs=([blk] if with_x else []) + [vec, vec, vec],
        out_shape=([jax.ShapeDtypeStruct((L, D), F32)] if with_x else []) + [vshape] * 3,
        compiler_params=_params(("arbitrary",)),
    )(*([dhl, x] + ([dn] if with_x else []) + [g, sc]))
    return res if with_x else [None] + list(res)


def _pack_blockdiag(wa, wx, gs):
    H, hd, _ = wa.shape
    hp = gs // hd
    ng = H // hp
    on_diag = _diag_mask(gs, hd)

    def bd(w):
        return jnp.where(on_diag, jnp.tile(w.reshape(ng, gs, hd), (1, 1, hp)), 0.0)

    return jnp.concatenate([bd(wa), bd(wx)], axis=-1).astype(BF16)


def _diag_mask(gs, hd):
    idx = jnp.arange(gs) // hd
    return idx[:, None] == idx[None, :]


def _unpack_blockdiag(dwd, H, hd, gs):
    hp = gs // hd
    on_diag = _diag_mask(gs, hd)

    def diag(dm):
        kept = jnp.where(on_diag, dm, 0.0)
        return sum(kept[:, :, p * hd:(p + 1) * hd] for p in range(hp)).reshape(H, hd, hd)

    return diag(dwd[:, :, :gs]), diag(dwd[:, :, gs:])


def kernel(x, c, ctx, c_ctx, norm_g, w_ada, b_ada, w_in, w_conv_a, w_conv_b, b_conv_b, lru_wa, lru_ba, lru_wx, lru_bx, lru_lambda, w_out, final_g, loss_target, m_c_ctx, m_norm_g, m_w_ada, m_b_ada, m_w_in, m_w_conv_a, m_w_conv_b, m_b_conv_b, m_lru_wa, m_lru_ba, m_lru_wx, m_lru_bx, m_lru_lambda, m_w_out, m_final_g, v_c_ctx, v_norm_g, v_w_ada, v_b_ada, v_w_in, v_w_conv_a, v_w_conv_b, v_b_conv_b, v_lru_wa, v_lru_ba, v_lru_wx, v_lru_bx, v_lru_lambda, v_w_out, v_final_g):
    xi, yi, ci = _pos()
    me = 4 * xi + 2 * yi + ci
    q = 2 * xi + yi
    first_core = (ci == 0).astype(F32)

    L, D = x.shape[1], x.shape[2]
    T = ctx.shape[1]
    W = D // 2
    Wq = W // 4
    H, hd = lru_wa.shape[2], lru_wa.shape[3]
    gs = min(LRU_GROUP, W)
    nq = w_ada.shape[2]
    tl = min(256, T, L)
    tr = min(256, T, L)
    x2, ctx2, tgt2 = x[0], ctx[0], loss_target[0]

    def place(shard, full_cols):
        z = jnp.zeros((shard.shape[0], full_cols), F32)
        return lax.dynamic_update_slice(z, shard * first_core, (0, q * shard.shape[1]))

    c_rows = lax.dynamic_update_slice(jnp.zeros((8, D), F32), c, (me, 0))
    small_in = [c_rows, place(w_conv_a[0], W), place(w_conv_b[0], W), place(lru_ba[0], W), place(lru_bx[0], W),
                place(lru_lambda[0], W)]
    small_shapes = [a.shape for a in small_in]
    gathered = _allreduce8(_pack(small_in, 8 * SUBLANES), "gather_small")
    c_all, wca, wcb, ba_all, bx_all, lam_all = _unpack(gathered, small_shapes)

    s_rows = jnp.concatenate([c_all, c_ctx[None, :], jnp.zeros((7, D), F32)], axis=0)
    mod_part = _matmul(s_rows, w_ada[0], a_act="silu", bias=lax.dynamic_slice(b_ada, (0, q * nq), (1, nq)),
                       tm=16, tn=nq, tk=512, name="ada_fwd")
    mod_all = _allreduce8(_pack([place(mod_part[:9], 4 * nq)], 8 * SUBLANES), "gather_mod")
    mod_all = _unpack(mod_all, [(9, 4 * nq)])[0]
    mod_l = lax.dynamic_slice(mod_all, (me, 0), (1, 3 * D))
    mod_c = mod_all[8:9]
    sh_l, sc_l, gt_l = mod_l[:, :D], mod_l[:, D:2 * D], mod_l[:, 2 * D:]
    sh_c, sc_c = mod_c[:, :D], mod_c[:, D:2 * D]

    pad_taps = lambda w: jnp.pad(w, ((0, SUBLANES - w.shape[0]), (0, 0)))
    wts = {
        "wca": pad_taps(wca), "wcb": pad_taps(wcb), "bcb": b_conv_b,
        "wd": [_pack_blockdiag(lru_wa[0, d], lru_wx[0, d], gs) for d in range(2)],
        "ba": [ba_all[d:d + 1] for d in range(2)], "bx": [bx_all[d:d + 1] for d in range(2)],
        "lam": [lam_all[d:d + 1] for d in range(2)],
    }

    hl = _norm_in(x2, ctx2, norm_g, sc_l, sh_l, sc_c, sh_c, tr)
    p_lat, win_full, wout_full = _in_proj_gather(hl, w_in[0].astype(BF16), w_out[0].astype(BF16),
                                                 jnp.reshape(q, (1,)).astype(jnp.int32), rows=L, tm=min(1024, L))
    v_ctx = _matmul(hl, win_full, a_rows=T, a_off=L, n_range=(4 * W, W), tm=T, tn=W, tk=D, out_dtype=BF16,
                    name="in_proj_ctx")
    p_ctx = jnp.pad(v_ctx, ((0, 0), (4 * W, W)))
    zero_w = jnp.zeros((1, W), F32)
    ctx0, cgates0 = _mix_fwd(p_ctx, 0, zero_w, wts, rows=T, row_off=0, row_w=T, tl=tl, name="ctx_fwd0")
    c1s, _, cgates1 = _mix_fwd(p_ctx, 1, zero_w, wts, rows=T, row_off=0, row_w=T, tl=tl, saved0=ctx0, name="ctx_fwd1")
    h0_init, h1_init = ctx0["h"][T - 1:T], c1s[0:1]
    tl_tall = 2 * tl if L % (2 * tl) == 0 else tl
    lat0, gates0 = _mix_fwd(p_lat, 0, h0_init, wts, rows=L, row_off=0, row_w=GRID_W, tl=tl_tall, name="mix_fwd0")
    h1s, cat, gates1 = _mix_fwd(p_lat, 1, h1_init, wts, rows=L, row_off=0, row_w=GRID_W, tl=tl_tall, saved0=lat0,
                                name="mix_fwd1")
    out = _matmul(cat, wout_full, tm=1024, tn=D, tk=2 * W, out_dtype=BF16, name="out_proj")
    tr_lat = 2 * tr if L % (2 * tr) == 0 else tr
    dn, dout, dfg, dgt, loss_blk = _loss_head(out, x2, tgt2, gt_l, final_g[None, :], tr_lat)

    dcat = _matmul(dout, wout_full, tb=True, tm=1024, tn=2 * W, tk=D, out_dtype=BF16, name="out_proj_bwd")
    gw_out = _matmul(cat, dout, ta=True, tm=1024, tn=D, tk=2048, out_dtype=BF16, name="w_out_grad")
    dxb0, dwd0, dba0, dbx0, dlam0, ch0 = _mix_bwd0(p_lat, dcat, lat0, gates0, h0_init, zero_w, wts, rows=L, row_off=0,
                                                   row_w=GRID_W, tl=tl_tall, name="mix_bwd0")
    dp_lat, dwd1, dba1, dbx1, dlam1, dwca, dwcb, dbcb, ch1 = _mix_bwd1(
        p_lat, dcat, lat0, h1s, gates1, dxb0, h1_init, zero_w, wts, rows=L, row_off=0, row_w=GRID_W, tl=tl,
        name="mix_bwd1", dp_rows=L + T)
    zero_cat = jnp.zeros((T, 2 * W), BF16)
    cxb0, cwd0, cba0, cbx0, clam0, _ = _mix_bwd0(p_ctx, zero_cat, ctx0, cgates0, zero_w, ch0, wts, rows=T, row_off=0,
                                                 row_w=T, tl=tl, name="ctx_bwd0")
    dp, cwd1, cba1, cbx1, clam1, cwca, cwcb, cbcb, _ = _mix_bwd1(
        p_ctx, zero_cat, ctx0, c1s, cgates1, cxb0, zero_w, ch1, wts, rows=T, row_off=0, row_w=T, tl=tl, name="ctx_bwd1",
        dp_rows=L + T, dp_off=L, dp_into=dp_lat)

    gw_in = _matmul(hl, dp, ta=True, tm=1024, tn=1536, tk=2816, out_dtype=BF16, name="w_in_grad")
    rs_axes = [1, 0]
    pair_sums = _rs_pair_sums([gw_in, gw_out], rs_axes)
    dhc = _matmul(dp, win_full, tb=True, a_rows=T, a_off=L, k_range=(4 * W, W), tm=T, tn=D, tk=W,
                  name="in_proj_bwd_ctx")
    _, dsh_c, dsc_c, dng_c = _norm_bwd(dhc, ctx2, None, norm_g, sc_c, tr, "norm_bwd_ctx")
    zeros_d = jnp.zeros((1, D), F32)
    dmod_c = jnp.concatenate([dsh_c, dsc_c, zeros_d], axis=1)
    dhl, (*rs_slots, dmod_c_all) = _matmul(
        dp, win_full, tb=True, a_rows=L, tm=512, tn=D, tk=6 * W, out_dtype=BF16, name="in_proj_bwd",
        side=_join_sides(_rs_chips_side(pair_sums), _gather_block_side(jnp.pad(dmod_c, ((0, SUBLANES - 1), (0, 0))))))
    gx, dsh_l, dsc_l, dng_l = _norm_bwd(dhl, x2, dn, norm_g, sc_l, tr_lat, "norm_bwd")
    gc_rows = _matmul(lax.dynamic_slice(dmod_c_all.reshape(8 * SUBLANES, 3 * D), (0, q * nq), (8 * SUBLANES, nq)), w_ada[0],
                      tb=True, dsilu_mul=c_ctx[None, :], tm=8 * SUBLANES, tn=D, tk=512, name="c_ctx_grad")
    gc_part = jnp.sum(gc_rows, axis=0, keepdims=True) * first_core

    g_in_shard, g_out_shard = _rs_finish(rs_slots, rs_axes)

    dwa0, dwx0 = _unpack_blockdiag(dwd0 + cwd0, H, hd, gs)
    dwa1, dwx1 = _unpack_blockdiag(dwd1 + cwd1, H, hd, gs)
    dmod_l = jnp.concatenate([dsh_l, dsc_l, dgt], axis=1)
    small_g = [
        lax.dynamic_update_slice(jnp.zeros((8, 3 * D), F32), dmod_l, (me, 0)), dmod_c,
        dfg, dng_l + dng_c, (dwca + cwca)[:3], (dwcb + cwcb)[:4], dbcb + cbcb,
        jnp.stack([dwa0, dwa1]), jnp.stack([dwx0, dwx1]),
        jnp.concatenate([dba0 + cba0, dba1 + cba1], axis=0), jnp.concatenate([dbx0 + cbx0, dbx1 + cbx1], axis=0),
        jnp.concatenate([dlam0 + clam0, dlam1 + clam1], axis=0), loss_blk[0:1, 0:1], gc_part,
    ]
    g_shapes = [a.shape for a in small_g]
    (g_rows, g_modc, g_fg, g_ng, g_wca, g_wcb, g_bcb, g_wa, g_wx, g_ba, g_bx, g_lam, loss_sum, g_c_ctx) = _unpack(
        _allreduce8_two_level(_pack(small_g, 8 * SUBLANES), "reduce_small"), g_shapes)

    g_mod = jnp.concatenate([g_rows, g_modc, jnp.zeros((7, 3 * D), F32)], axis=0)
    g_mod_q = lax.dynamic_slice(g_mod, (0, q * nq), (16, nq))
    g_w_ada = _matmul(s_rows, g_mod_q, ta=True, a_act="silu", tm=1024, tn=nq, tk=16, name="w_ada_grad")
    g_b_ada = jnp.sum(g_mod[:9], axis=0, keepdims=True)

    def shard_cols(a, width):
        return lax.dynamic_slice(a, (0, q * width), (a.shape[0], width))

    grads = {
        "c_ctx": g_c_ctx, "norm_g": g_ng, "b_ada": g_b_ada,
        "w_conv_a": shard_cols(g_wca, Wq)[None], "w_conv_b": shard_cols(g_wcb, Wq)[None], "b_conv_b": g_bcb,
        "lru_wa": g_wa[None], "lru_ba": shard_cols(g_ba, Wq)[None], "lru_wx": g_wx[None],
        "lru_bx": shard_cols(g_bx, Wq)[None], "lru_lambda": shard_cols(g_lam, Wq)[None], "final_g": g_fg[0],
    }
    small_names = list(grads)
    given = dict(c_ctx=(c_ctx, m_c_ctx, v_c_ctx), norm_g=(norm_g, m_norm_g, v_norm_g), b_ada=(b_ada, m_b_ada, v_b_ada),
                 w_conv_a=(w_conv_a, m_w_conv_a, v_w_conv_a), w_conv_b=(w_conv_b, m_w_conv_b, v_w_conv_b),
                 b_conv_b=(b_conv_b, m_b_conv_b, v_b_conv_b), lru_wa=(lru_wa, m_lru_wa, v_lru_wa),
                 lru_ba=(lru_ba, m_lru_ba, v_lru_ba), lru_wx=(lru_wx, m_lru_wx, v_lru_wx),
                 lru_bx=(lru_bx, m_lru_bx, v_lru_bx), lru_lambda=(lru_lambda, m_lru_lambda, v_lru_lambda),
                 final_g=(final_g, m_final_g, v_final_g))
    def rows2d(a):
        return a.reshape(-1, a.shape[-1])

    grads = {n: grads[n].reshape(given[n][0].shape) for n in small_names}
    quads = [tuple(rows2d(a) for a in (given[n][0], grads[n], given[n][1], given[n][2])) for n in small_names]
    updated = _adam_many(quads, "adam_small")
    delta_s, newm_s, newv_s = ({n: u[j].reshape(given[n][0].shape) for n, u in zip(small_names, updated)} for j in range(3))

    big = {"w_ada": (w_ada, g_w_ada, m_w_ada, v_w_ada), "w_in": (w_in, g_in_shard, m_w_in, v_w_in),
           "w_out": (w_out, g_out_shard, m_w_out, v_w_out)}
    delta_b, newm_b, newv_b = {}, {}, {}
    for n, (w, g, m, v) in big.items():
        d_, m_, v_, *echo = _adam(w[0], g, m[0], v[0], "adam_" + n, echo_g=n != "w_ada")
        grads[n] = (echo[0] if echo else g)[None]
        delta_b[n], newm_b[n], newv_b[n] = d_[None], m_[None], v_[None]

    loss = loss_sum[0, 0]
    order = ["c_ctx", "norm_g", "w_ada", "b_ada", "w_in", "w_conv_a", "w_conv_b", "b_conv_b", "lru_wa", "lru_ba",
             "lru_wx", "lru_bx", "lru_lambda", "w_out", "final_g"]
    delta = {**delta_s, **delta_b}
    newm = {**newm_s, **newm_b}
    newv = {**newv_s, **newv_b}
    return (loss, gx[None], *[grads[n] for n in order], *[delta[n] for n in order], *[newm[n] for n in order],
            *[newv[n] for n in order])
```

```python
import jax
import jax.numpy as jnp
from jax import lax
from jax.experimental import pallas as pl
from jax.experimental.pallas import tpu as pltpu

F32 = jnp.float32
BF16 = jnp.bfloat16
MESH_ID = pl.DeviceIdType.MESH

EPS = 1e-6
LRU_C = 8.0
GRID_W = 64
ADAM_LR = 0.001
ADAM_B1 = 0.9
ADAM_B2 = 0.999
ADAM_EPS = 1e-08
ADAM_WD = 0.01
ADAM_STEP = 10

LANES = 128
SUBLANES = 8
PACK_COLS = 1024
VMEM_LIMIT = 56 * 2**20
LRU_GROUP = 256


def _params(sem=None):
    return pltpu.CompilerParams(vmem_limit_bytes=VMEM_LIMIT, dimension_semantics=sem)


def _pick(dim, pref, quantum=LANES):
    if dim <= pref:
        return dim
    best = None
    for t in range(quantum, pref + 1, quantum):
        if dim % t == 0:
            best = t
    assert best is not None, (dim, pref)
    return best


def _pos():
    return lax.axis_index("x"), lax.axis_index("y"), lax.axis_index("c")


def _flip(v, bit):
    return 1 - v if bit else v


def _sigmoid(v):
    return 0.5 * jnp.tanh(0.5 * v) + 0.5


def _silu(v):
    return v * _sigmoid(v)


def _dsilu(v):
    s = _sigmoid(v)
    return s * (1.0 + v * (1.0 - s))


def _gates(pre_r, pre_i, sp):
    r = _sigmoid(pre_r)
    ig = _sigmoid(pre_i)
    e = LRU_C * r * sp
    w = jnp.tanh(e)
    return r, ig, jnp.exp(-e), (2.0 * w) * pl.reciprocal(1.0 + w, approx=True)


def _softplus(z):
    return jnp.maximum(z, 0.0) + jnp.log1p(jnp.exp(-jnp.abs(z)))


def _matmul(a, b, *, ta=False, tb=False, tm=512, tn=512, tk=512, out_dtype=F32, name,
            a_rows=None, a_off=0, n_range=None, k_range=None, a_act=None, bias=None, dsilu_mul=None, side=None):
    rows_a = a.shape[0] if a_rows is None else a_rows
    if ta:
        K, M = rows_a, a.shape[1]
    else:
        M, K = rows_a, a.shape[1]
    N = b.shape[0] if tb else b.shape[1]
    n_off, k_off = 0, 0
    if n_range is not None:
        n_off, N = n_range
    if k_range is not None:
        assert not ta
        k_off, K = k_range
    tm, tn, tk = _pick(M, tm, SUBLANES), _pick(N, tn), _pick(K, tk)
    t_rows = tk if ta else tm
    assert a_off % t_rows == 0 and n_off % tn == 0 and k_off % tk == 0
    nk = K // tk
    gi, gj = M // tm, N // tn
    off_blocks, nb, kb = a_off // t_rows, n_off // tn, k_off // tk
    dims = (((0 if ta else 1,), (1 if tb else 0,)), ((), ()))
    extras = [e for e in (bias, dsilu_mul) if e is not None]
    n_sin = len(side["ins"]) if side else 0
    n_sout = len(side["outs"]) if side else 0

    def body(a_ref, b_ref, *rest):
        rest = list(rest)
        bias_ref = rest.pop(0) if bias is not None else None
        dsm_ref = rest.pop(0) if dsilu_mul is not None else None
        side_in = [rest.pop(0) for _ in range(n_sin)]
        o_ref = rest.pop(0)
        side_out = [rest.pop(0) for _ in range(n_sout)]
        acc_ref = rest.pop(0) if nk > 1 else None
        side_scr = rest
        i, j, k = pl.program_id(0), pl.program_id(1), pl.program_id(2)

        if side:
            @pl.when((i == 0) & (j == 0) & (k == 0))
            def _():
                side["start"](side_in, side_out, side_scr)

        av = a_ref[...]
        if a_act == "silu":
            av = _silu(av)
        prod = lax.dot_general(av, b_ref[...], dims, preferred_element_type=F32)

        def finish(r):
            if bias_ref is not None:
                r = r + bias_ref[...]
            if dsm_ref is not None:
                r = r * _dsilu(dsm_ref[...])
            o_ref[...] = r.astype(o_ref.dtype)

        if nk == 1:
            finish(prod)
        else:
            @pl.when(k == 0)
            def _():
                acc_ref[...] = prod

            @pl.when(k > 0)
            def _():
                acc_ref[...] += prod

            @pl.when(k == nk - 1)
            def _():
                finish(acc_ref[...])

        if side:
            @pl.when((i == gi - 1) & (j == gj - 1) & (k == nk - 1))
            def _():
                side["finish"](side_in, side_out, side_scr)

    if ta:
        a_spec = pl.BlockSpec((tk, tm), lambda i, j, k: (k + off_blocks, i))
    else:
        a_spec = pl.BlockSpec((tm, tk), lambda i, j, k: (i + off_blocks, k + kb))
    once = dict(pipeline_mode=pl.Buffered(1)) if (gj == 1 and nk == 1) else {}
    if tb:
        b_spec = pl.BlockSpec((tn, tk), lambda i, j, k: (j + nb, k + kb), **once)
    else:
        b_spec = pl.BlockSpec((tk, tn), lambda i, j, k: (k + kb, j + nb), **once)
    in_specs = [a_spec, b_spec]
    if bias is not None:
        in_specs.append(pl.BlockSpec((1, tn), lambda i, j, k: (0, j)))
    if dsilu_mul is not None:
        in_specs.append(pl.BlockSpec((1, tn), lambda i, j, k: (0, j)))
    hbm = pl.BlockSpec(memory_space=pl.ANY)
    res = pl.pallas_call(
        body, name=name, grid=(gi, gj, nk),
        in_specs=in_specs + [hbm] * n_sin,
        out_specs=[pl.BlockSpec((tm, tn), lambda i, j, k: (i, j))] + [hbm] * n_sout,
        out_shape=[jax.ShapeDtypeStruct((M, N), out_dtype)] + (list(side["outs"]) if side else []),
        scratch_shapes=([pltpu.VMEM((tm, tn), F32)] if nk > 1 else []) + (list(side["scratch"]) if side else []),
        compiler_params=_params(("arbitrary",) * 3 if side else ("parallel", "parallel", "arbitrary")),
    )(a, b, *extras, *(side["ins"] if side else []))
    return (res[0], res[1:]) if side else res[0]


def _elementwise(fn, ins, outs, *, rows, cols, name, tr=256):
    tr = _pick(rows, tr, 2 * SUBLANES)
    n_in = len(ins)

    def body(*refs):
        vals = fn(*[r[...] for r in refs[:n_in]])
        if not isinstance(vals, (tuple, list)):
            vals = (vals,)
        for r, v in zip(refs[n_in:], vals, strict=True):
            r[...] = v.astype(r.dtype)

    def spec(off):
        assert off % tr == 0
        ob = off // tr
        return pl.BlockSpec((tr, cols), lambda i: (i + ob, 0))

    res = pl.pallas_call(
        body, name=name, grid=(rows // tr,),
        in_specs=[spec(off) for _, off in ins],
        out_specs=[spec(0) for _ in outs],
        out_shape=[jax.ShapeDtypeStruct((rows, cols), dt) for dt in outs],
        compiler_params=_params(("parallel",)),
    )(*[a for a, _ in ins])
    return res


def _adam_math(w, g, m, v):
    m = ADAM_B1 * m + (1.0 - ADAM_B1) * g
    v = ADAM_B2 * v + (1.0 - ADAM_B2) * (g * g)
    m_hat = m / (1.0 - ADAM_B1 ** ADAM_STEP)
    v_hat = v / (1.0 - ADAM_B2 ** ADAM_STEP)
    delta = -ADAM_LR * (m_hat / (jnp.sqrt(v_hat) + ADAM_EPS) + ADAM_WD * w)
    return delta, m, v


def _adam(w, g, m, v, name, echo_g=False):
    rows, cols = w.shape
    fn = (lambda w_, g_, m_, v_: _adam_math(w_, g_, m_, v_) + (g_,)) if echo_g else _adam_math
    return _elementwise(fn, [(w, 0), (g, 0), (m, 0), (v, 0)], [F32] * (4 if echo_g else 3),
                        rows=rows, cols=cols, name=name)


def _adam_many(quads, name):
    n = len(quads)

    def body(*refs):
        ins, outs = refs[:4 * n], refs[4 * n:]
        for t in range(n):
            w, g, m, v = (r[...] for r in ins[4 * t:4 * t + 4])
            for o_ref, val in zip(outs[3 * t:3 * t + 3], _adam_math(w, g, m, v), strict=True):
                o_ref[...] = val

    res = pl.pallas_call(
        body, name=name,
        out_shape=[jax.ShapeDtypeStruct(q[0].shape, F32) for q in quads for _ in range(3)],
        compiler_params=_params(),
    )(*[a for q in quads for a in q])
    return [res[3 * t:3 * t + 3] for t in range(n)]


def _pack(arrs, row_quantum):
    flat = jnp.concatenate([a.reshape(-1).astype(F32) for a in arrs])
    n = flat.shape[0]
    q = row_quantum * PACK_COLS
    total = -(-n // q) * q
    flat = jnp.pad(flat, (0, total - n))
    return flat.reshape(total // PACK_COLS, PACK_COLS)


def _unpack(buf, shapes):
    flat = buf.reshape(-1)
    out, off = [], 0
    for s in shapes:
        n = 1
        for d in s:
            n *= d
        out.append(flat[off:off + n].reshape(s))
        off += n
    return out


def _allreduce8(buf, name):
    R, C = buf.shape
    assert R % (8 * SUBLANES) == 0
    m = R // 8

    def body(x_ref, o_ref, recv, red, s1, r1, s2, r2):
        x, y, c = _pos()
        me = 4 * x + 2 * y + c

        def peer(k):
            px, py, pc = _flip(x, (k >> 2) & 1), _flip(y, (k >> 1) & 1), _flip(c, k & 1)
            return (px, py, pc), 4 * px + 2 * py + pc

        def rows(ref, idx):
            return ref.at[pl.ds(pl.multiple_of(idx * m, SUBLANES), m), :]

        def scatter(k):
            dev, p = peer(k)
            return pltpu.make_async_remote_copy(src_ref=rows(x_ref, p), dst_ref=recv.at[k], send_sem=s1.at[k],
                                                recv_sem=r1.at[k], device_id=dev, device_id_type=MESH_ID)

        def share(k):
            dev, p = peer(k)
            return pltpu.make_async_remote_copy(src_ref=red, dst_ref=rows(o_ref, me), send_sem=s2.at[k],
                                                recv_sem=r2.at[k], device_id=dev, device_id_type=MESH_ID)

        def shared_from(k):
            dev, p = peer(k)
            return pltpu.make_async_remote_copy(src_ref=red, dst_ref=rows(o_ref, p), send_sem=s2.at[k],
                                                recv_sem=r2.at[k], device_id=dev, device_id_type=MESH_ID)

        for k in range(1, 8):
            scatter(k).start()
        acc = rows(x_ref, me)[...]
        for k in range(1, 8):
            scatter(k).wait_recv()
            acc = acc + recv[k]
        red[...] = acc
        rows(o_ref, me)[...] = acc
        for k in range(1, 8):
            share(k).start()
        for k in range(1, 8):
            shared_from(k).wait_recv()
        for k in range(1, 8):
            scatter(k).wait_send()
            share(k).wait_send()

    return pl.pallas_call(
        body, name=name,
        in_specs=[pl.BlockSpec(memory_space=pltpu.VMEM)],
        out_specs=pl.BlockSpec(memory_space=pltpu.VMEM),
        out_shape=jax.ShapeDtypeStruct((R, C), F32),
        scratch_shapes=[pltpu.VMEM((8, m, C), F32), pltpu.VMEM((m, C), F32),
                        pltpu.SemaphoreType.DMA((8,)), pltpu.SemaphoreType.DMA((8,)),
                        pltpu.SemaphoreType.DMA((8,)), pltpu.SemaphoreType.DMA((8,))],
        compiler_params=_params(),
    )(buf)


def _allreduce8_two_level(buf, name):
    R, C = buf.shape
    assert R % (8 * SUBLANES) == 0
    m, hr = R // 8, R // 2

    def body(x_ref, o_ref, got0, half, got1, red, ssem, rsem):
        x, y, c = _pos()
        q = 2 * x + y
        sibling = (x, y, 1 - c)

        def half_rows(ref, core):
            return ref.at[pl.ds(pl.multiple_of(core * hr, SUBLANES), hr), :]

        def chunk(ref, core, chip):
            return ref.at[pl.ds(pl.multiple_of(core * hr + chip * m, SUBLANES), m), :]

        def chip_of(k):
            px, py = _flip(x, (k >> 1) & 1), _flip(y, k & 1)
            return (px, py, c), 2 * px + py

        def copy(src, dst, phase, k, dev):
            return pltpu.make_async_remote_copy(src_ref=src, dst_ref=dst, send_sem=ssem.at[phase, k],
                                                recv_sem=rsem.at[phase, k], device_id=dev, device_id_type=MESH_ID)

        swap = copy(half_rows(x_ref, 1 - c), got0, 0, 0, sibling)
        swap.start()
        swap.wait()
        half[...] = half_rows(x_ref, c)[...] + got0[...]

        def scatter(k):
            dev, p = chip_of(k)
            return copy(half.at[pl.ds(pl.multiple_of(p * m, SUBLANES), m), :], got1.at[k], 1, k, dev)

        for k in range(1, 4):
            scatter(k).start()
        acc = half[pl.ds(pl.multiple_of(q * m, SUBLANES), m), :]
        for k in range(1, 4):
            scatter(k).wait_recv()
            acc = acc + got1[k]
        red[...] = acc
        chunk(o_ref, c, q)[...] = acc

        def share(k, landing_chip):
            return copy(red, chunk(o_ref, c, landing_chip), 2, k, chip_of(k)[0])

        for k in range(1, 4):
            share(k, q).start()
        for k in range(1, 4):
            share(k, chip_of(k)[1]).wait_recv()
        back = copy(half_rows(o_ref, c), half_rows(o_ref, c), 3, 0, sibling)
        back.start()
        copy(half_rows(o_ref, 1 - c), half_rows(o_ref, 1 - c), 3, 0, sibling).wait_recv()
        back.wait_send()
        for k in range(1, 4):
            scatter(k).wait_send()
            share(k, q).wait_send()

    return pl.pallas_call(
        body, name=name,
        in_specs=[pl.BlockSpec(memory_space=pltpu.VMEM)],
        out_specs=pl.BlockSpec(memory_space=pltpu.VMEM),
        out_shape=jax.ShapeDtypeStruct((R, C), F32),
        scratch_shapes=[pltpu.VMEM((hr, C), F32), pltpu.VMEM((hr, C), F32), pltpu.VMEM((4, m, C), F32), pltpu.VMEM((m, C), F32),
                        pltpu.SemaphoreType.DMA((4, 4)), pltpu.SemaphoreType.DMA((4, 4))],
        compiler_params=_params(),
    )(buf)


def _bounce(src, dst, buf, sem):
    cin = pltpu.make_async_copy(src, buf, sem)
    cin.start()
    cin.wait()
    cout = pltpu.make_async_copy(buf, dst, sem)
    cout.start()
    cout.wait()


def _chunk(ref, axis, idx, size):
    start = idx * size
    if axis == 0:
        return ref.at[pl.ds(start, size), :]
    return ref.at[:, pl.ds(start, size)]


def _in_proj_gather(hl, win, wout, q_arr, *, rows, tm):
    D, nq = win.shape
    dq, D2 = wout.shape
    ni = rows // tm
    ops = ((0, 1, nq, D // 2), (1, 0, dq, dq // 2))

    def body(q_ref, a_ref, win_ref, wout_ref, p_ref, gin_ref, gout_ref, b_scr, buf_out, lsem, ssem, rsem, fsem, gsem):
        j, i = pl.program_id(0), pl.program_id(1)
        x, y, c = _pos()
        q = 2 * x + y
        srcs = (win_ref, wout_ref)
        dsts = (gin_ref, gout_ref)

        def shard_window(o, chip):
            _, axis, size, _ = ops[o]
            return _chunk(dsts[o], axis, chip, size)

        def half(ref, o, core):
            return ref.at[pl.ds(core * ops[o][3], ops[o][3]), :]

        def half_window(o, chip, core):
            _, axis, size, hs = ops[o]
            if axis == 1:
                return dsts[o].at[pl.ds(core * hs, hs), pl.ds(chip * size, size)]
            return dsts[o].at[pl.ds(chip * size + core * hs, hs), :]

        def chip_of(k):
            px, py = _flip(x, (k >> 1) & 1), _flip(y, k & 1)
            return px, py, 2 * px + py

        def send(o, k):
            px, py, _ = chip_of(k)
            return pltpu.make_async_remote_copy(
                src_ref=half(srcs[o], o, c), dst_ref=half_window(o, q, c), send_sem=ssem.at[o, k],
                recv_sem=rsem.at[o, k], device_id=(px, py, c), device_id_type=MESH_ID)

        def chip_recv(o, k):
            px, py, pq = chip_of(k)
            landed = half_window(o, pq, c)
            pltpu.make_async_remote_copy(src_ref=landed, dst_ref=landed, send_sem=ssem.at[o, k], recv_sem=rsem.at[o, k],
                                         device_id=(px, py, c), device_id_type=MESH_ID).wait_recv()

        def to_sibling(o, k):
            landed = half_window(o, chip_of(k)[2], c)
            return pltpu.make_async_remote_copy(src_ref=landed, dst_ref=landed, send_sem=fsem.at[o, k],
                                                recv_sem=gsem.at[o, k], device_id=(x, y, 1 - c), device_id_type=MESH_ID)

        def from_sibling(o, k):
            theirs = half_window(o, chip_of(k)[2], 1 - c)
            pltpu.make_async_remote_copy(src_ref=theirs, dst_ref=theirs, send_sem=fsem.at[o, k], recv_sem=gsem.at[o, k],
                                         device_id=(x, y, 1 - c), device_id_type=MESH_ID).wait_recv()

        def relay(o, core):
            if core == 0:
                landed, target = half_window(o, chip_of(2)[2], 0), (x, 1 - y, 0)
            else:
                landed, target = half_window(o, chip_of(1)[2], 1), (1 - x, y, 1)
            return pltpu.make_async_remote_copy(src_ref=landed, dst_ref=landed, send_sem=ssem.at[o, 3],
                                                recv_sem=rsem.at[o, 3], device_id=target, device_id_type=MESH_ID)

        def on_core(core, fn):
            @pl.when(c == core)
            def _():
                fn()

        def land(o, k):
            chip_recv(o, k)
            if k == 2:
                on_core(0, lambda: relay(o, 0).start())
            if k == 1:
                on_core(1, lambda: relay(o, 1).start())
            to_sibling(o, k).start()

        def settle(o, k):
            from_sibling(o, k)
            to_sibling(o, k).wait_send()

        def b_load(k, slot):
            src = win_ref if k == 0 else shard_window(0, chip_of(k)[2])
            return pltpu.make_async_copy(src, b_scr.at[slot], lsem.at[0])

        def own_store():
            return pltpu.make_async_copy(b_scr.at[0], shard_window(0, q), lsem.at[2])

        order = (0, 2, 1, 3)
        early = max(ni - 2, 0)

        @pl.when((j == 0) & (i == 0))
        def _():
            for o in range(2):
                for k in (2, 1):
                    send(o, k).start()
            first = b_load(0, 0)
            first.start()
            first.wait()
            own_store().start()
            _bounce(wout_ref, shard_window(1, q), buf_out, lsem.at[1])

        for jj in range(3):
            nxt = order[jj + 1]

            @pl.when((j == jj) & (i == early))
            def _(nxt=nxt):
                land(0, nxt)

            @pl.when((j == jj) & (i == ni - 1))
            def _(jj=jj, nxt=nxt):
                settle(0, nxt)
                if jj == 1:
                    own_store().wait()
                b_load(nxt, (jj + 1) % 2).start()

            @pl.when((j == jj + 1) & (i == 0))
            def _(jj=jj, nxt=nxt):
                b_load(nxt, (jj + 1) % 2).wait()

        @pl.when((j == 3) & (i == 0))
        def _():
            land(1, 2)
            land(1, 1)

        p_ref[...] = jnp.dot(a_ref[...], b_scr[j % 2], preferred_element_type=F32).astype(p_ref.dtype)

        @pl.when((j == 3) & (i == ni - 1))
        def _():
            settle(1, 2)
            settle(1, 1)
            land(1, 3)
            settle(1, 3)
            for o in range(2):
                for k in (2, 1):
                    send(o, k).wait_send()
                for core in range(2):
                    on_core(core, lambda o=o, core=core: relay(o, core).wait_send())

    hbm = pl.BlockSpec(memory_space=pl.ANY)
    grid_spec = pltpu.PrefetchScalarGridSpec(
        num_scalar_prefetch=1, grid=(4, ni),
        in_specs=[pl.BlockSpec((tm, D), lambda j, i, qr: (i, 0)), hbm, hbm],
        out_specs=[pl.BlockSpec((tm, nq), lambda j, i, qr: (i, jnp.bitwise_xor(qr[0], ((j & 1) << 1) | (j >> 1)))),
                   hbm, hbm],
        scratch_shapes=[pltpu.VMEM((2,) + win.shape, win.dtype), pltpu.VMEM(wout.shape, wout.dtype), pltpu.SemaphoreType.DMA((3,))]
        + [pltpu.SemaphoreType.DMA((2, 4)) for _ in range(4)])
    return pl.pallas_call(
        body, name="in_proj_gather", grid_spec=grid_spec,
        out_shape=[jax.ShapeDtypeStruct((rows, 4 * nq), BF16), jax.ShapeDtypeStruct((D, 4 * nq), win.dtype),
                   jax.ShapeDtypeStruct((4 * dq, D2), wout.dtype)],
        compiler_params=_params(("arbitrary", "arbitrary")),
    )(q_arr, hl, win, wout)


def _rs_to_sibling(gs, axes):
    n = len(gs)
    shapes = []
    for g, ax in zip(gs, axes):
        s = list(g.shape)
        s[ax] //= 8
        shapes.append(tuple(s))

    def body(*refs):
        g_refs, mine, landed = refs[:n], refs[n:2 * n], refs[2 * n:3 * n]
        bufs = refs[3 * n:4 * n]
        lsem, ssem, rsem = refs[4 * n:]
        x, y, c = _pos()
        cps = []
        for o in range(n):
            size = shapes[o][axes[o]]
            for j in range(4):
                rc = pltpu.make_async_remote_copy(
                    src_ref=_chunk(g_refs[o], axes[o], 2 * j + 1 - c, size), dst_ref=landed[o].at[j],
                    send_sem=ssem.at[o, j], recv_sem=rsem.at[o, j], device_id=(x, y, 1 - c), device_id_type=MESH_ID)
                rc.start()
                cps.append(rc)
        for o in range(n):
            size = shapes[o][axes[o]]
            loads = [pltpu.make_async_copy(_chunk(g_refs[o], axes[o], 2 * j + c, size), bufs[o].at[j % 2], lsem.at[o, j % 2])
                     for j in range(4)]
            stores = [pltpu.make_async_copy(bufs[o].at[j % 2], mine[o].at[j], lsem.at[o, 2 + j % 2]) for j in range(4)]
            loads[0].start()
            for j in range(4):
                loads[j].wait()
                stores[j].start()
                if j + 1 < 4:
                    if j >= 1:
                        stores[j - 1].wait()
                    loads[j + 1].start()
            stores[2].wait()
            stores[3].wait()
        for rc in cps:
            rc.wait()

    hbm = pl.BlockSpec(memory_space=pl.ANY)
    outs = [jax.ShapeDtypeStruct((4,) + s, g.dtype) for s, g in zip(shapes, gs)]
    res = pl.pallas_call(
        body, name="rs_to_sibling", in_specs=[hbm] * n, out_specs=[hbm] * (2 * n), out_shape=outs + outs,
        scratch_shapes=[pltpu.VMEM((2,) + s, g.dtype) for s, g in zip(shapes, gs)]
        + [pltpu.SemaphoreType.DMA((n, 4)), pltpu.SemaphoreType.DMA((n, 4)), pltpu.SemaphoreType.DMA((n, 4))],
        compiler_params=_params(),
    )(*gs)
    return res[:n], res[n:]


def _gather_block_side(block):
    r, n_cols = block.shape

    def copies(ins, outs, scr):
        ssem, rsem = scr[1], scr[2]
        x, y, c = _pos()
        return [pltpu.make_async_remote_copy(
            src_ref=ins[0], dst_ref=outs[0].at[k], send_sem=ssem.at[k], recv_sem=rsem.at[k],
            device_id=(_flip(x, (k >> 2) & 1), _flip(y, (k >> 1) & 1), _flip(c, k & 1)), device_id_type=MESH_ID)
            for k in range(1, 8)]

    def start(ins, outs, scr):
        for cp in copies(ins, outs, scr):
            cp.start()

    def finish(ins, outs, scr):
        _bounce(ins[0], outs[0].at[0], scr[0], scr[3].at[0])
        for cp in copies(ins, outs, scr):
            cp.wait()

    return dict(ins=[block], outs=[jax.ShapeDtypeStruct((8, r, n_cols), block.dtype)],
                scratch=[pltpu.VMEM((r, n_cols), block.dtype), pltpu.SemaphoreType.DMA((8,)), pltpu.SemaphoreType.DMA((8,)),
                         pltpu.SemaphoreType.DMA((1,))],
                start=start, finish=finish)


def _join_sides(a, b):
    na_i, na_o, na_s = len(a["ins"]), len(a["outs"]), len(a["scratch"])

    def run(which):
        def fn(ins, outs, scr):
            a[which](ins[:na_i], outs[:na_o], scr[:na_s])
            b[which](ins[na_i:], outs[na_o:], scr[na_s:])
        return fn

    return dict(ins=a["ins"] + b["ins"], outs=a["outs"] + b["outs"], scratch=a["scratch"] + b["scratch"],
                start=run("start"), finish=run("finish"))


def _rs_chips_side(parts):
    n = len(parts)

    def copies(p_refs, slots, scr):
        ssem, rsem = scr[n + 1], scr[n + 2]
        x, y, c = _pos()
        cps = []
        for o in range(n):
            for k in range(1, 4):
                px, py = _flip(x, (k >> 1) & 1), _flip(y, k & 1)
                cps.append(pltpu.make_async_remote_copy(
                    src_ref=p_refs[o].at[2 * px + py], dst_ref=slots[o].at[k], send_sem=ssem.at[o, k],
                    recv_sem=rsem.at[o, k], device_id=(px, py, c), device_id_type=MESH_ID))
        return cps

    def start(p_refs, slots, scr):
        for cp in copies(p_refs, slots, scr):
            cp.start()

    def finish(p_refs, slots, scr):
        x, y, _ = _pos()
        q = 2 * x + y
        for o in range(n):
            _bounce(p_refs[o].at[q], slots[o].at[0], scr[o], scr[n].at[o])
        for cp in copies(p_refs, slots, scr):
            cp.wait()

    return dict(
        ins=list(parts), outs=[jax.ShapeDtypeStruct(p.shape, p.dtype) for p in parts],
        scratch=[pltpu.VMEM(p.shape[1:], p.dtype) for p in parts]
        + [pltpu.SemaphoreType.DMA((n,)), pltpu.SemaphoreType.DMA((n, 4)), pltpu.SemaphoreType.DMA((n, 4))],
        start=start, finish=finish)


def _rs_share(rs, axes):
    n = len(rs)
    shapes = []
    for r, ax in zip(rs, axes):
        s = list(r.shape)
        s[ax] *= 2
        shapes.append(tuple(s))

    def body(*refs):
        r_refs, outs = refs[:n], refs[n:2 * n]
        bufs = refs[2 * n:3 * n]
        lsem, ssem, rsem = refs[3 * n:]
        x, y, c = _pos()
        cps = []
        for o in range(n):
            size = r_refs[o].shape[axes[o]]
            window = _chunk(outs[o], axes[o], c, size)
            rc = pltpu.make_async_remote_copy(src_ref=r_refs[o], dst_ref=window, send_sem=ssem.at[o], recv_sem=rsem.at[o],
                                              device_id=(x, y, 1 - c), device_id_type=MESH_ID)
            rc.start()
            cps.append(rc)
        for o in range(n):
            size = r_refs[o].shape[axes[o]]
            _bounce(r_refs[o], _chunk(outs[o], axes[o], c, size), bufs[o], lsem.at[o])
        for cp in cps:
            cp.wait()

    hbm = pl.BlockSpec(memory_space=pl.ANY)
    return pl.pallas_call(
        body, name="rs_share", in_specs=[hbm] * n, out_specs=[hbm] * n,
        out_shape=[jax.ShapeDtypeStruct(s, r.dtype) for s, r in zip(shapes, rs)],
        scratch_shapes=[pltpu.VMEM(r.shape, r.dtype) for r in rs] + [pltpu.SemaphoreType.DMA((n,)) for _ in range(3)],
        compiler_params=_params(),
    )(*rs)


def _rs_pair_sums(gs, axes):
    mine, landed = _rs_to_sibling(gs, axes)
    pair_sums = []
    for o, (mi, la) in enumerate(zip(mine, landed)):
        rows, cols = mi.shape[0] * mi.shape[1], mi.shape[2]
        s = _elementwise(lambda a, b: a.astype(F32) + b.astype(F32), [(mi.reshape(rows, cols), 0), (la.reshape(rows, cols), 0)],
                         [BF16], rows=rows, cols=cols, name=f"rs_pair_sum{o}")[0]
        pair_sums.append(s.reshape(mi.shape))
    return pair_sums


def _rs_finish(slots, axes):
    reduced = []
    for o, sl in enumerate(slots):
        rows, cols = sl.shape[1], sl.shape[2]
        flat = sl.reshape(4 * rows, cols)
        r = _elementwise(lambda a, b, c, d: (a.astype(F32) + b.astype(F32)) + (c.astype(F32) + d.astype(F32)),
                         [(flat, k * rows) for k in range(4)], [F32], rows=rows, cols=cols, name=f"rs_chip_sum{o}")[0]
        reduced.append(r)
    return _rs_share(reduced, axes)


def _norm_in(x, ctx, g, sc_l, sh_l, sc_c, sh_c, tr):
    L, D = x.shape
    T = ctx.shape[0]
    nx, nc = L // tr, T // tr

    def body(x_ref, c_ref, g_ref, scl, shl, scc, shc, o_ref):
        i = pl.program_id(0)

        def run(src, sc, sh):
            v = src[...]
            r = lax.rsqrt(jnp.mean(v * v, axis=-1, keepdims=True) + EPS)
            o_ref[...] = ((v * r * g_ref[...]) * (1.0 + sc[...]) + sh[...]).astype(o_ref.dtype)

        @pl.when(i < nx)
        def _():
            run(x_ref, scl, shl)

        @pl.when(i >= nx)
        def _():
            run(c_ref, scc, shc)

    vec = pl.BlockSpec((1, D), lambda i: (0, 0))
    return pl.pallas_call(
        body, name="norm_in", grid=(nx + nc,),
        in_specs=[pl.BlockSpec((tr, D), lambda i: (jnp.minimum(i, nx - 1), 0)),
                  pl.BlockSpec((tr, D), lambda i: (jnp.maximum(i - nx, 0), 0)), vec, vec, vec, vec, vec],
        out_specs=pl.BlockSpec((tr, D), lambda i: (i, 0)),
        out_shape=jax.ShapeDtypeStruct((L + T, D), BF16),
        compiler_params=_params(("arbitrary",)),
    )(x, ctx, g, sc_l, sh_l, sc_c, sh_c)


def _tmod(tl, row_w):
    assert row_w & (row_w - 1) == 0
    return lax.broadcasted_iota(jnp.int32, (tl, 1), 0) & (row_w - 1)


def _shift(z, k, tmod, row_w):
    tl = z.shape[0]
    rolled = pltpu.roll(z, k % tl, 0)
    mask = (tmod >= k) if k > 0 else (tmod < row_w + k)
    return jnp.where(mask, rolled, 0.0)


def _conv(z, w_ref, taps, left, tmod, row_w, lanes=slice(None)):
    out = None
    for j in range(taps):
        k = left - j
        term = (z if k == 0 else _shift(z, k, tmod, row_w)) * w_ref[j:j + 1, lanes]
        out = term if out is None else out + term
    return out


def _conv_bwd(dz, z, w_ref, taps, left, tmod, row_w, lanes=slice(None)):
    din = None
    dws = []
    for j in range(taps):
        k = left - j
        shifted = dz if k == 0 else _shift(dz, -k, tmod, row_w)
        term = shifted * w_ref[j:j + 1, lanes]
        din = term if din is None else din + term
        dws.append(jnp.sum(shifted * z, axis=0, keepdims=True))
    return din, dws


def _gate_matmul(xb16_ref, wd_ref, pre_scr, W, ng, gs):
    for g in range(ng):
        pg = jnp.dot(xb16_ref[:, g * gs:(g + 1) * gs], wd_ref[g], preferred_element_type=F32)
        pre_scr[:, g * gs:(g + 1) * gs] = pg[:, :gs]
        pre_scr[:, W + g * gs:W + (g + 1) * gs] = pg[:, gs:]


def _f32(ref, rows, lanes):
    return ref[rows, lanes].astype(F32)


def _sub_loop(tl, sub_r, W, fn):
    def chunk(ci, carry):
        r0 = pl.multiple_of(ci * sub_r, sub_r)
        for lb in range(W // LANES):
            fn(r0, lb * LANES)
        return carry

    lax.fori_loop(0, tl // sub_r, chunk, 0)


def _row_loop(tl, rev, step, init):
    nchunk = tl // SUBLANES

    def chunk(j, carry):
        jj = (nchunk - 1 - j) if rev else j
        c0 = pl.multiple_of(jj * SUBLANES, SUBLANES)
        for r in (range(SUBLANES - 1, -1, -1) if rev else range(SUBLANES)):
            carry = step(c0 + r, carry)
        return carry

    return lax.fori_loop(0, nchunk, chunk, init)


def _mix_fwd(P, d, h_init, wts, *, rows, row_off, row_w, tl, saved0=None, name):
    W = P.shape[1] // 6
    nt = rows // tl
    ob = row_off // tl
    rev = d == 1
    gs = min(LRU_GROUP, W)
    ng = W // gs
    wca, wcb, bcb = wts["wca"], wts["wcb"], wts["bcb"]
    wd, ba, bx, lam = wts["wd"][d], wts["ba"][d], wts["bx"][d], wts["lam"][d]

    def tile(i):
        return (nt - 1 - i) if rev else i

    def pcol(j):
        return pl.BlockSpec((tl, W), lambda i: (tile(i) + ob, j))

    vec = pl.BlockSpec((1, W), lambda i: (0, 0))
    taps = pl.BlockSpec((SUBLANES, W), lambda i: (0, 0))
    wd_spec = pl.BlockSpec(wd.shape, lambda i: (0, 0, 0))
    seq = pl.BlockSpec((tl, W), lambda i: (tile(i), 0))

    sub_r = min(row_w, tl)
    assert tl % sub_r == 0

    def body(*refs):
        if rev:
            (bl, cl, ul, gl, ql, ho, xb_r, xb16_r, wca_r, wd_r, ba_r, bx_r, lam_r, hin, hseq, cat, a_o, r_o, ig_o, m2_o,
             b_scr, pre_scr, carry, sp_scr) = refs
        else:
            (vl, wcb_r, bcb_r, wd_r, ba_r, bx_r, lam_r, hin, hseq, xb_r, xb16_r, a_o, r_o, ig_o, m2_o,
             b_scr, pre_scr, carry, sp_scr) = refs
        i = pl.program_id(0)

        @pl.when(i == 0)
        def _():
            carry[...] = hin[...]

        sp_scr[...] = _softplus(-lam_r[...])
        tmod = _tmod(sub_r, sub_r)

        def conv_in(r0, l0):
            rs, ls = pl.ds(r0, sub_r), pl.ds(l0, LANES)
            xb = _conv(_f32(vl, rs, ls), wcb_r, 4, 2, tmod, sub_r, ls) + bcb_r[:, ls]
            xb_r[rs, ls] = xb
            xb16_r[rs, ls] = xb.astype(BF16)

        def gates(r0, l0):
            rs, ls = pl.ds(r0, sub_r), pl.ds(l0, LANES)
            r, ig, a, m2 = _gates(pre_scr[rs, ls] + ba_r[:, ls], pre_scr[rs, pl.ds(W + l0, LANES)] + bx_r[:, ls],
                                  sp_scr[:, ls])
            a_o[rs, ls] = a
            r_o[rs, ls] = r.astype(r_o.dtype)
            ig_o[rs, ls] = ig.astype(ig_o.dtype)
            m2_o[rs, ls] = m2.astype(m2_o.dtype)
            m = jnp.where(m2 > 0.0, m2 * lax.rsqrt(m2), 0.0)
            b_scr[rs, ls] = m * (ig * xb_r[rs, ls])

        if not rev:
            _sub_loop(tl, sub_r, W, conv_in)
        _gate_matmul(xb16_r, wd_r, pre_scr, W, ng, gs)
        _sub_loop(tl, sub_r, W, gates)

        def step(t, h):
            h = a_o[pl.ds(t, 1), :] * h + b_scr[pl.ds(t, 1), :]
            hseq[pl.ds(t, 1), :] = h
            return h

        carry[...] = _row_loop(tl, rev, step, carry[...])

        if rev:
            def mix_out(r0, l0):
                rs, ls = pl.ds(r0, sub_r), pl.ds(l0, LANES)
                yb = (ho[rs, ls] + hseq[rs, ls]) * _silu(_f32(ql, rs, ls))
                ya = (_f32(bl, rs, ls) * _conv(_f32(cl, rs, ls) * _f32(ul, rs, ls), wca_r, 3, 1, tmod, sub_r, ls)
                      * _silu(_f32(gl, rs, ls)))
                cat[rs, ls] = ya.astype(cat.dtype)
                cat[rs, pl.ds(W + l0, LANES)] = yb.astype(cat.dtype)

            _sub_loop(tl, sub_r, W, mix_out)

    scratch = [pltpu.VMEM((tl, W), F32), pltpu.VMEM((tl, 2 * W), F32), pltpu.VMEM((1, W), F32), pltpu.VMEM((1, W), F32)]
    f32_seq = jax.ShapeDtypeStruct((rows, W), F32)
    kept_gates = [f32_seq] + [jax.ShapeDtypeStruct((rows, W), BF16)] * 3
    if rev:
        in_specs = [pcol(j) for j in (0, 1, 2, 3, 5)] + [seq, seq, seq, taps, wd_spec, vec, vec, vec, vec]
        args = [P] * 5 + [saved0["h"], saved0["xb"], saved0["xb16"], wca, wd, ba, bx, lam, h_init]
        out_specs = [seq, pl.BlockSpec((tl, 2 * W), lambda i: (tile(i), 0))] + [seq] * 4
        out_shape = [f32_seq, jax.ShapeDtypeStruct((rows, 2 * W), BF16)] + kept_gates
    else:
        in_specs = [pcol(4), taps, vec, wd_spec, vec, vec, vec, vec]
        args = [P, wcb, bcb, wd, ba, bx, lam, h_init]
        out_specs = [seq] * 7
        out_shape = [f32_seq, f32_seq, jax.ShapeDtypeStruct((rows, W), BF16)] + kept_gates
    res = pl.pallas_call(
        body, name=name, grid=(nt,), in_specs=in_specs, out_specs=out_specs, out_shape=out_shape,
        scratch_shapes=scratch, compiler_params=_params(("arbitrary",)),
    )(*args)
    gates = dict(zip(("a", "r", "ig", "m2"), res[-4:]))
    if rev:
        return res[0], res[1], gates
    return dict(h=res[0], xb=res[1], xb16=res[2]), gates


_BWD_SCRATCH = ("dyl", "g", "dp16", "sp", "dlf", "edge", "c")
_FWD_SAVED = ("xb", "xb16", "a", "r", "ig", "m2")


def _bwd_scratch(tl, W):
    shapes = {"dyl": pltpu.VMEM((tl, W), F32), "g": pltpu.VMEM((tl, W), F32), "dp16": pltpu.VMEM((tl, 2 * W), BF16),
              "sp": pltpu.VMEM((1, W), F32), "dlf": pltpu.VMEM((1, W), F32), "edge": pltpu.VMEM((1, W), F32),
              "c": pltpu.VMEM((1, W), F32)}
    return [shapes[n] for n in _BWD_SCRATCH]


def _lru_bwd_tile(d, dy_fn, hs_ref, wd_r, lam_r, scr, acc, first, tl, sub_r, ng, gs):
    dwd_ref, dba_ref, dbx_ref, dlam_ref = acc
    W = hs_ref.shape[1]
    assert gs % LANES == 0
    rev = d == 0
    lam = lam_r[...]
    scr["sp"][...] = _softplus(-lam)
    scr["dlf"][...] = -_sigmoid(-lam)

    @pl.when(first)
    def _():
        dwd_ref[...] = jnp.zeros_like(dwd_ref)
        dba_ref[...] = jnp.zeros_like(dba_ref)
        dbx_ref[...] = jnp.zeros_like(dbx_ref)
        dlam_ref[...] = jnp.zeros_like(dlam_ref)

    def state_grad(r0, l0):
        rs, ls = pl.ds(r0, sub_r), pl.ds(l0, LANES)
        scr["dyl"][rs, ls] = dy_fn(rs, ls)

    _sub_loop(tl, sub_r, W, state_grad)

    def step(t, c):
        g = scr["dyl"][pl.ds(t, 1), :] + c
        scr["g"][pl.ds(t, 1), :] = g
        return scr["a"][pl.ds(t, 1), :] * g

    scr["c"][...] = _row_loop(tl, rev, step, scr["c"][...])
    row = lax.broadcasted_iota(jnp.int32, (sub_r, 1), 0)

    def grads(r0, l0):
        rs, ls = pl.ds(r0, sub_r), pl.ds(l0, LANES)
        g, a, m2 = scr["g"][rs, ls], scr["a"][rs, ls], _f32(scr["m2"], rs, ls)
        r, ig, xb = _f32(scr["r"], rs, ls), _f32(scr["ig"], rs, ls), scr["xb"][rs, ls]
        h = hs_ref[rs, ls]
        if d == 0:
            e0 = pl.multiple_of(jnp.maximum(r0 - SUBLANES, 0), SUBLANES)
            edge = jnp.where(r0 == 0, scr["edge"][:, ls], hs_ref[pl.ds(e0, SUBLANES), ls][SUBLANES - 1:, :])
            hprev = jnp.where(row == 0, edge, pltpu.roll(h, 1, 0))
        else:
            e0 = pl.multiple_of(jnp.minimum(r0 + sub_r, tl - SUBLANES), SUBLANES)
            edge = jnp.where(r0 == tl - sub_r, scr["edge"][:, ls], hs_ref[pl.ds(e0, SUBLANES), ls][:1, :])
            hprev = jnp.where(row == sub_r - 1, edge, pltpu.roll(h, sub_r - 1, 0))
        rsq = lax.rsqrt(m2)
        gm = g * (m2 * rsq)
        d_la = (g * hprev) * a - (g * (ig * xb)) * ((1.0 - m2) * rsq)
        d_pr = d_la * ((-LRU_C) * scr["sp"][:, ls]) * (r * (1.0 - r))
        d_pi = (gm * xb) * (ig * (1.0 - ig))
        scr["dyl"][rs, ls] = gm * ig
        dlam_ref[:, ls] += jnp.sum(d_la * ((-LRU_C) * r), axis=0, keepdims=True) * scr["dlf"][:, ls]
        dba_ref[:, ls] += jnp.sum(d_pr, axis=0, keepdims=True)
        dbx_ref[:, ls] += jnp.sum(d_pi, axis=0, keepdims=True)
        gi, off = divmod(l0, gs)
        scr["dp16"][rs, pl.ds(gi * 2 * gs + off, LANES)] = d_pr.astype(BF16)
        scr["dp16"][rs, pl.ds(gi * 2 * gs + gs + off, LANES)] = d_pi.astype(BF16)

    _sub_loop(tl, sub_r, W, grads)
    for gi in range(ng):
        dp = scr["dp16"][:, gi * 2 * gs:(gi + 1) * 2 * gs]
        scr["g"][:, gi * gs:(gi + 1) * gs] = lax.dot_general(dp, wd_r[gi], (((1,), (1,)), ((), ())),
                                                             preferred_element_type=F32)
        dwd_ref[gi] += lax.dot_general(scr["xb16"][:, gi * gs:(gi + 1) * gs], dp, (((0,), (0,)), ((), ())),
                                       preferred_element_type=F32)


def _edge_block(h, tl, nt, d):
    W = h.shape[1]
    per = tl // SUBLANES
    if d == 0:
        return pl.BlockSpec((SUBLANES, W), lambda i: (jnp.maximum((nt - 1 - i) * per - 1, 0), 0))
    return pl.BlockSpec((SUBLANES, W), lambda i: (jnp.minimum((i + 1) * per, nt * per - 1), 0))


def _mix_bwd0(P, dcat, saved0, gates0, h_init, c_init, wts, *, rows, row_off, row_w, tl, name):
    W = P.shape[1] // 6
    nt = rows // tl
    ob = row_off // tl
    gs = min(LRU_GROUP, W)
    ng = W // gs
    wd, lam = wts["wd"][0], wts["lam"][0]
    h0s = saved0["h"]
    kept = [saved0["xb"], saved0["xb16"]] + [gates0[n] for n in ("a", "r", "ig", "m2")]

    def tile(i):
        return nt - 1 - i

    vec = pl.BlockSpec((1, W), lambda i: (0, 0))
    wd_spec = pl.BlockSpec(wd.shape, lambda i: (0, 0, 0))
    seq = pl.BlockSpec((tl, W), lambda i: (tile(i), 0))

    sub_r = min(row_w, tl)
    assert tl % sub_r == 0

    def body(ql, dyb, hs, hedge8, xb_r, xb16_r, a_r, r_r, ig_r, m2_r, wd_r, lam_r, hin, cin,
             dxb_o, dwd_o, dba_o, dbx_o, dlam_o, cfin, *scratch):
        scr = dict(zip(_BWD_SCRATCH, scratch, strict=True))
        scr.update(zip(_FWD_SAVED, (xb_r, xb16_r, a_r, r_r, ig_r, m2_r), strict=True))
        i = pl.program_id(0)

        @pl.when(i == 0)
        def _():
            scr["c"][...] = cin[...]

        scr["edge"][...] = jnp.where(i == nt - 1, hin[...], hedge8[SUBLANES - 1:SUBLANES, :])
        _lru_bwd_tile(0, lambda rs, ls: _f32(dyb, rs, ls) * _silu(_f32(ql, rs, ls)), hs, wd_r, lam_r, scr,
                      (dwd_o, dba_o, dbx_o, dlam_o), i == 0, tl, sub_r, ng, gs)
        dxb_o[...] = scr["dyl"][...] + scr["g"][...]
        cfin[...] = scr["c"][...]

    return pl.pallas_call(
        body, name=name, grid=(nt,),
        in_specs=[pl.BlockSpec((tl, W), lambda i: (tile(i) + ob, 5)), pl.BlockSpec((tl, W), lambda i: (tile(i), 1)), seq,
                  _edge_block(h0s, tl, nt, 0)] + [seq] * 6 + [wd_spec, vec, vec, vec],
        out_specs=[seq, wd_spec, vec, vec, vec, vec],
        out_shape=[jax.ShapeDtypeStruct((rows, W), F32), jax.ShapeDtypeStruct(wd.shape, F32)]
        + [jax.ShapeDtypeStruct((1, W), F32)] * 4,
        scratch_shapes=_bwd_scratch(tl, W),
        compiler_params=_params(("arbitrary",)),
    )(P, dcat, h0s, h0s, *kept, wd, lam, h_init, c_init)


def _mix_bwd1(P, dcat, saved0, h1s, gates1, dxb0, h_init, c_init, wts, *, rows, row_off, row_w, tl, name,
              dp_rows=None, dp_off=0, dp_into=None):
    dp_rows = rows if dp_rows is None else dp_rows
    dpb = dp_off // tl
    W = P.shape[1] // 6
    nt = rows // tl
    ob = row_off // tl
    gs = min(LRU_GROUP, W)
    ng = W // gs
    wca, wcb = wts["wca"], wts["wcb"]
    wd, lam = wts["wd"][1], wts["lam"][1]
    h0s = saved0["h"]
    kept = [saved0["xb"], saved0["xb16"]] + [gates1[n] for n in ("a", "r", "ig", "m2")]

    vec = pl.BlockSpec((1, W), lambda i: (0, 0))
    taps = pl.BlockSpec((SUBLANES, W), lambda i: (0, 0))
    wd_spec = pl.BlockSpec(wd.shape, lambda i: (0, 0, 0))
    seq = pl.BlockSpec((tl, W), lambda i: (i, 0))

    sub_r = min(row_w, tl)
    assert tl % sub_r == 0

    def body(*refs):
        if dp_into is not None:
            refs = refs[1:]
        (bl, cl, ul, gl, vl, ql, dya, dyb, h0, h1, hedge8, dx0, xb_r, xb16_r, a_r, r_r, ig_r, m2_r, wca_r, wcb_r,
         wd_r, lam_r, hin, cin, dp_o, dwd_o, dba_o, dbx_o, dlam_o, dwca_o, dwcb_o, dbcb_o, cfin, *scratch) = refs
        scr = dict(zip(_BWD_SCRATCH, scratch, strict=True))
        scr.update(zip(_FWD_SAVED, (xb_r, xb16_r, a_r, r_r, ig_r, m2_r), strict=True))
        i = pl.program_id(0)

        @pl.when(i == 0)
        def _():
            scr["c"][...] = cin[...]
            dwca_o[...] = jnp.zeros_like(dwca_o)
            dwcb_o[...] = jnp.zeros_like(dwcb_o)
            dbcb_o[...] = jnp.zeros_like(dbcb_o)

        scr["edge"][...] = jnp.where(i == nt - 1, hin[...], hedge8[0:1, :])
        _lru_bwd_tile(1, lambda rs, ls: _f32(dyb, rs, ls) * _silu(_f32(ql, rs, ls)), h1, wd_r, lam_r, scr,
                      (dwd_o, dba_o, dbx_o, dlam_o), i == 0, tl, sub_r, ng, gs)
        cfin[...] = scr["c"][...]
        tmod = _tmod(sub_r, sub_r)

        def rest(r0, l0):
            rs, ls = pl.ds(r0, sub_r), pl.ds(l0, LANES)
            dxb = dx0[rs, ls] + scr["dyl"][rs, ls] + scr["g"][rs, ls]
            dv, dwb = _conv_bwd(dxb, _f32(vl, rs, ls), wcb_r, 4, 2, tmod, sub_r, ls)
            for j in range(4):
                dwcb_o[j:j + 1, ls] += dwb[j]
            dbcb_o[:, ls] += jnp.sum(dxb, axis=0, keepdims=True)
            q = _f32(ql, rs, ls)
            sq = _sigmoid(q)
            dq = _f32(dyb, rs, ls) * (h0[rs, ls] + h1[rs, ls]) * (sq * (1.0 + q * (1.0 - sq)))
            b_, c_, u_, g_ = _f32(bl, rs, ls), _f32(cl, rs, ls), _f32(ul, rs, ls), _f32(gl, rs, ls)
            z = c_ * u_
            cz = _conv(z, wca_r, 3, 1, tmod, sub_r, ls)
            sgm = _sigmoid(g_)
            sg = g_ * sgm
            da = _f32(dya, rs, ls)
            dz, dwa = _conv_bwd(da * b_ * sg, z, wca_r, 3, 1, tmod, sub_r, ls)
            for j in range(3):
                dwca_o[j:j + 1, ls] += dwa[j]
            parts = (da * cz * sg, dz * u_, dz * c_, da * b_ * cz * (sgm * (1.0 + g_ * (1.0 - sgm))), dv, dq)
            for k, val in enumerate(parts):
                dp_o[rs, pl.ds(k * W + l0, LANES)] = val.astype(dp_o.dtype)

        _sub_loop(tl, sub_r, W, rest)

    def pcol(j):
        return pl.BlockSpec((tl, W), lambda i: (i + ob, j))

    prev = [] if dp_into is None else [dp_into]
    return pl.pallas_call(
        body, name=name, grid=(nt,), input_output_aliases={} if dp_into is None else {0: 0},
        in_specs=[pl.BlockSpec(memory_space=pl.ANY)] * len(prev) + [pcol(j) for j in range(6)]
        + [pl.BlockSpec((tl, W), lambda i: (i, 0)), pl.BlockSpec((tl, W), lambda i: (i, 1)), seq, seq,
           _edge_block(h1s, tl, nt, 1), seq] + [seq] * 6 + [taps, taps, wd_spec, vec, vec, vec],
        out_specs=[pl.BlockSpec((tl, 6 * W), lambda i: (i + dpb, 0)), wd_spec, vec, vec, vec, taps, taps, vec, vec],
        out_shape=[jax.ShapeDtypeStruct((dp_rows, 6 * W), BF16), jax.ShapeDtypeStruct(wd.shape, F32)]
        + [jax.ShapeDtypeStruct((1, W), F32)] * 3
        + [jax.ShapeDtypeStruct((SUBLANES, W), F32)] * 2 + [jax.ShapeDtypeStruct((1, W), F32)] * 2,
        scratch_shapes=_bwd_scratch(tl, W),
        compiler_params=_params(("arbitrary",)),
    )(*prev, *([P] * 6), dcat, dcat, h0s, h1s, h1s, dxb0, *kept, wca, wcb, wd, lam, h_init, c_init)


ROW_PART = 128


def _row_parts(tr):
    part = min(ROW_PART, tr)
    return [slice(k * part, (k + 1) * part) for k in range(tr // part)]


def _loss_head(out, x, tgt, gt, fg, tr):
    L, D = x.shape

    def body(o_ref, x_ref, t_ref, gt_ref, fg_ref, dn_o, do_o, dfg_o, dgt_o, loss_o):
        i = pl.program_id(0)

        @pl.when(i == 0)
        def _():
            dfg_o[...] = jnp.zeros_like(dfg_o)
            dgt_o[...] = jnp.zeros_like(dgt_o)
            loss_o[...] = jnp.zeros_like(loss_o)

        gt_v = gt_ref[...]
        fg_v = fg_ref[...]
        for rs in _row_parts(tr):
            o = o_ref[rs, :].astype(F32)
            n = x_ref[rs, :] + gt_v * o
            r = lax.rsqrt(jnp.mean(n * n, axis=-1, keepdims=True) + EPS)
            nr = n * r
            e = nr * fg_v - t_ref[rs, :]
            loss_o[...] += 0.5 * jnp.sum(jnp.mean(e * e, axis=-1, keepdims=True))
            dy = e * (1.0 / D)
            dfg_o[...] += jnp.sum(dy * nr, axis=0, keepdims=True)
            qv = dy * fg_v
            dn = r * (qv - nr * jnp.mean(qv * nr, axis=-1, keepdims=True))
            dgt_o[...] += jnp.sum(dn * o, axis=0, keepdims=True)
            dn_o[rs, :] = dn.astype(dn_o.dtype)
            do_o[rs, :] = (dn * gt_v).astype(do_o.dtype)

    blk = pl.BlockSpec((tr, D), lambda i: (i, 0))
    vec = pl.BlockSpec((1, D), lambda i: (0, 0))
    return pl.pallas_call(
        body, name="loss_head", grid=(L // tr,), in_specs=[blk, blk, blk, vec, vec],
        out_specs=[blk, blk, vec, vec, pl.BlockSpec((SUBLANES, LANES), lambda i: (0, 0))],
        out_shape=[jax.ShapeDtypeStruct((L, D), BF16), jax.ShapeDtypeStruct((L, D), BF16),
                   jax.ShapeDtypeStruct((1, D), F32), jax.ShapeDtypeStruct((1, D), F32),
                   jax.ShapeDtypeStruct((SUBLANES, LANES), F32)],
        compiler_params=_params(("arbitrary",)),
    )(out, x, tgt, gt, fg)


def _norm_bwd(dhl, x, dn, g, sc, tr, name):
    L, D = x.shape
    with_x = dn is not None

    def body(*refs):
        if with_x:
            d_ref, x_ref, dn_ref, g_ref, sc_ref, gx_o, dsh_o, dsc_o, dg_o = refs
        else:
            d_ref, x_ref, g_ref, sc_ref, dsh_o, dsc_o, dg_o = refs
        i = pl.program_id(0)

        @pl.when(i == 0)
        def _():
            dsh_o[...] = jnp.zeros_like(dsh_o)
            dsc_o[...] = jnp.zeros_like(dsc_o)
            dg_o[...] = jnp.zeros_like(dg_o)

        g_v = g_ref[...]
        for rs in _row_parts(tr):
            d = d_ref[rs, :].astype(F32)
            xv = x_ref[rs, :]
            r = lax.rsqrt(jnp.mean(xv * xv, axis=-1, keepdims=True) + EPS)
            xr = xv * r
            dsh_o[...] += jnp.sum(d, axis=0, keepdims=True)
            dsc_o[...] += jnp.sum(d * (xr * g_v), axis=0, keepdims=True)
            dxn = d * (1.0 + sc_ref[...])
            dg_o[...] += jnp.sum(dxn * xr, axis=0, keepdims=True)
            if with_x:
                qv = dxn * g_v
                gx_o[rs, :] = r * (qv - xr * jnp.mean(qv * xr, axis=-1, keepdims=True)) + dn_ref[rs, :].astype(F32)

    blk = pl.BlockSpec((tr, D), lambda i: (i, 0))
    vec = pl.BlockSpec((1, D), lambda i: (0, 0))
    vshape = jax.ShapeDtypeStruct((1, D), F32)
    res = pl.pallas_call(
        body, name=name, grid=(L // tr,),
        in_specs=[blk, blk] + ([blk] if with_x else []) + [vec, vec],
        out_specs=([blk] if with_x else []) + [vec, vec, vec],
        out_shape=([jax.ShapeDtypeStruct((L, D), F32)] if with_x else []) + [vshape] * 3,
        compiler_params=_params(("arbitrary",)),
    )(*([dhl, x] + ([dn] if with_x else []) + [g, sc]))
    return res if with_x else [None] + list(res)


def _pack_blockdiag(wa, wx, gs):
    H, hd, _ = wa.shape
    hp = gs // hd
    ng = H // hp
    on_diag = _diag_mask(gs, hd)

    def bd(w):
        return jnp.where(on_diag, jnp.tile(w.reshape(ng, gs, hd), (1, 1, hp)), 0.0)

    return jnp.concatenate([bd(wa), bd(wx)], axis=-1).astype(BF16)


def _diag_mask(gs, hd):
    idx = jnp.arange(gs) // hd
    return idx[:, None] == idx[None, :]


def _unpack_blockdiag(dwd, H, hd, gs):
    hp = gs // hd
    on_diag = _diag_mask(gs, hd)

    def diag(dm):
        kept = jnp.where(on_diag, dm, 0.0)
        return sum(kept[:, :, p * hd:(p + 1) * hd] for p in range(hp)).reshape(H, hd, hd)

    return diag(dwd[:, :, :gs]), diag(dwd[:, :, gs:])


def kernel(x, c, ctx, c_ctx, norm_g, w_ada, b_ada, w_in, w_conv_a, w_conv_b, b_conv_b, lru_wa, lru_ba, lru_wx, lru_bx, lru_lambda, w_out, final_g, loss_target, m_c_ctx, m_norm_g, m_w_ada, m_b_ada, m_w_in, m_w_conv_a, m_w_conv_b, m_b_conv_b, m_lru_wa, m_lru_ba, m_lru_wx, m_lru_bx, m_lru_lambda, m_w_out, m_final_g, v_c_ctx, v_norm_g, v_w_ada, v_b_ada, v_w_in, v_w_conv_a, v_w_conv_b, v_b_conv_b, v_lru_wa, v_lru_ba, v_lru_wx, v_lru_bx, v_lru_lambda, v_w_out, v_final_g):
    xi, yi, ci = _pos()
    me = 4 * xi + 2 * yi + ci
    q = 2 * xi + yi
    first_core = (ci == 0).astype(F32)

    L, D = x.shape[1], x.shape[2]
    T = ctx.shape[1]
    W = D // 2
    Wq = W // 4
    H, hd = lru_wa.shape[2], lru_wa.shape[3]
    gs = min(LRU_GROUP, W)
    nq = w_ada.shape[2]
    tl = min(256, T, L)
    tr = min(256, T, L)
    x2, ctx2, tgt2 = x[0], ctx[0], loss_target[0]

    def place(shard, full_cols):
        z = jnp.zeros((shard.shape[0], full_cols), F32)
        return lax.dynamic_update_slice(z, shard * first_core, (0, q * shard.shape[1]))

    c_rows = lax.dynamic_update_slice(jnp.zeros((8, D), F32), c, (me, 0))
    small_in = [c_rows, place(w_conv_a[0], W), place(w_conv_b[0], W), place(lru_ba[0], W), place(lru_bx[0], W),
                place(lru_lambda[0], W)]
    small_shapes = [a.shape for a in small_in]
    gathered = _allreduce8(_pack(small_in, 8 * SUBLANES), "gather_small")
    c_all, wca, wcb, ba_all, bx_all, lam_all = _unpack(gathered, small_shapes)

    s_rows = jnp.concatenate([c_all, c_ctx[None, :], jnp.zeros((7, D), F32)], axis=0)
    mod_part = _matmul(s_rows, w_ada[0], a_act="silu", bias=lax.dynamic_slice(b_ada, (0, q * nq), (1, nq)),
                       tm=16, tn=nq, tk=512, name="ada_fwd")
    mod_all = _allreduce8(_pack([place(mod_part[:9], 4 * nq)], 8 * SUBLANES), "gather_mod")
    mod_all = _unpack(mod_all, [(9, 4 * nq)])[0]
    mod_l = lax.dynamic_slice(mod_all, (me, 0), (1, 3 * D))
    mod_c = mod_all[8:9]
    sh_l, sc_l, gt_l = mod_l[:, :D], mod_l[:, D:2 * D], mod_l[:, 2 * D:]
    sh_c, sc_c = mod_c[:, :D], mod_c[:, D:2 * D]

    pad_taps = lambda w: jnp.pad(w, ((0, SUBLANES - w.shape[0]), (0, 0)))
    wts = {
        "wca": pad_taps(wca), "wcb": pad_taps(wcb), "bcb": b_conv_b,
        "wd": [_pack_blockdiag(lru_wa[0, d], lru_wx[0, d], gs) for d in range(2)],
        "ba": [ba_all[d:d + 1] for d in range(2)], "bx": [bx_all[d:d + 1] for d in range(2)],
        "lam": [lam_all[d:d + 1] for d in range(2)],
    }

    hl = _norm_in(x2, ctx2, norm_g, sc_l, sh_l, sc_c, sh_c, tr)
    p_lat, win_full, wout_full = _in_proj_gather(hl, w_in[0].astype(BF16), w_out[0].astype(BF16),
                                                 jnp.reshape(q, (1,)).astype(jnp.int32), rows=L, tm=min(1024, L))
    v_ctx = _matmul(hl, win_full, a_rows=T, a_off=L, n_range=(4 * W, W), tm=T, tn=W, tk=D, out_dtype=BF16,
                    name="in_proj_ctx")
    p_ctx = jnp.pad(v_ctx, ((0, 0), (4 * W, W)))
    zero_w = jnp.zeros((1, W), F32)
    ctx0, cgates0 = _mix_fwd(p_ctx, 0, zero_w, wts, rows=T, row_off=0, row_w=T, tl=tl, name="ctx_fwd0")
    c1s, _, cgates1 = _mix_fwd(p_ctx, 1, zero_w, wts, rows=T, row_off=0, row_w=T, tl=tl, saved0=ctx0, name="ctx_fwd1")
    h0_init, h1_init = ctx0["h"][T - 1:T], c1s[0:1]
    tl_tall = 2 * tl if L % (2 * tl) == 0 else tl
    lat0, gates0 = _mix_fwd(p_lat, 0, h0_init, wts, rows=L, row_off=0, row_w=GRID_W, tl=tl_tall, name="mix_fwd0")
    h1s, cat, gates1 = _mix_fwd(p_lat, 1, h1_init, wts, rows=L, row_off=0, row_w=GRID_W, tl=tl_tall, saved0=lat0,
                                name="mix_fwd1")
    out = _matmul(cat, wout_full, tm=1024, tn=D, tk=2 * W, out_dtype=BF16, name="out_proj")
    tr_lat = 2 * tr if L % (2 * tr) == 0 else tr
    dn, dout, dfg, dgt, loss_blk = _loss_head(out, x2, tgt2, gt_l, final_g[None, :], tr_lat)

    dcat = _matmul(dout, wout_full, tb=True, tm=1024, tn=2 * W, tk=D, out_dtype=BF16, name="out_proj_bwd")
    gw_out = _matmul(cat, dout, ta=True, tm=1024, tn=D, tk=2048, out_dtype=BF16, name="w_out_grad")
    dxb0, dwd0, dba0, dbx0, dlam0, ch0 = _mix_bwd0(p_lat, dcat, lat0, gates0, h0_init, zero_w, wts, rows=L, row_off=0,
                                                   row_w=GRID_W, tl=tl_tall, name="mix_bwd0")
    dp_lat, dwd1, dba1, dbx1, dlam1, dwca, dwcb, dbcb, ch1 = _mix_bwd1(
        p_lat, dcat, lat0, h1s, gates1, dxb0, h1_init, zero_w, wts, rows=L, row_off=0, row_w=GRID_W, tl=tl,
        name="mix_bwd1", dp_rows=L + T)
    zero_cat = jnp.zeros((T, 2 * W), BF16)
    cxb0, cwd0, cba0, cbx0, clam0, _ = _mix_bwd0(p_ctx, zero_cat, ctx0, cgates0, zero_w, ch0, wts, rows=T, row_off=0,
                                                 row_w=T, tl=tl, name="ctx_bwd0")
    dp, cwd1, cba1, cbx1, clam1, cwca, cwcb, cbcb, _ = _mix_bwd1(
        p_ctx, zero_cat, ctx0, c1s, cgates1, cxb0, zero_w, ch1, wts, rows=T, row_off=0, row_w=T, tl=tl, name="ctx_bwd1",
        dp_rows=L + T, dp_off=L, dp_into=dp_lat)

    gw_in = _matmul(hl, dp, ta=True, tm=1024, tn=1536, tk=2816, out_dtype=BF16, name="w_in_grad")
    rs_axes = [1, 0]
    pair_sums = _rs_pair_sums([gw_in, gw_out], rs_axes)
    dhc = _matmul(dp, win_full, tb=True, a_rows=T, a_off=L, k_range=(4 * W, W), tm=T, tn=D, tk=W,
                  name="in_proj_bwd_ctx")
    _, dsh_c, dsc_c, dng_c = _norm_bwd(dhc, ctx2, None, norm_g, sc_c, tr, "norm_bwd_ctx")
    zeros_d = jnp.zeros((1, D), F32)
    dmod_c = jnp.concatenate([dsh_c, dsc_c, zeros_d], axis=1)
    dhl, (*rs_slots, dmod_c_all) = _matmul(
        dp, win_full, tb=True, a_rows=L, tm=512, tn=D, tk=6 * W, out_dtype=BF16, name="in_proj_bwd",
        side=_join_sides(_rs_chips_side(pair_sums), _gather_block_side(jnp.pad(dmod_c, ((0, SUBLANES - 1), (0, 0))))))
    gx, dsh_l, dsc_l, dng_l = _norm_bwd(dhl, x2, dn, norm_g, sc_l, tr_lat, "norm_bwd")
    gc_rows = _matmul(lax.dynamic_slice(dmod_c_all.reshape(8 * SUBLANES, 3 * D), (0, q * nq), (8 * SUBLANES, nq)), w_ada[0],
                      tb=True, dsilu_mul=c_ctx[None, :], tm=8 * SUBLANES, tn=D, tk=512, name="c_ctx_grad")
    gc_part = jnp.sum(gc_rows, axis=0, keepdims=True) * first_core

    g_in_shard, g_out_shard = _rs_finish(rs_slots, rs_axes)

    dwa0, dwx0 = _unpack_blockdiag(dwd0 + cwd0, H, hd, gs)
    dwa1, dwx1 = _unpack_blockdiag(dwd1 + cwd1, H, hd, gs)
    dmod_l = jnp.concatenate([dsh_l, dsc_l, dgt], axis=1)
    small_g = [
        lax.dynamic_update_slice(jnp.zeros((8, 3 * D), F32), dmod_l, (me, 0)), dmod_c,
        dfg, dng_l + dng_c, (dwca + cwca)[:3], (dwcb + cwcb)[:4], dbcb + cbcb,
        jnp.stack([dwa0, dwa1]), jnp.stack([dwx0, dwx1]),
        jnp.concatenate([dba0 + cba0, dba1 + cba1], axis=0), jnp.concatenate([dbx0 + cbx0, dbx1 + cbx1], axis=0),
        jnp.concatenate([dlam0 + clam0, dlam1 + clam1], axis=0), loss_blk[0:1, 0:1], gc_part,
    ]
    g_shapes = [a.shape for a in small_g]
    (g_rows, g_modc, g_fg, g_ng, g_wca, g_wcb, g_bcb, g_wa, g_wx, g_ba, g_bx, g_lam, loss_sum, g_c_ctx) = _unpack(
        _allreduce8_two_level(_pack(small_g, 8 * SUBLANES), "reduce_small"), g_shapes)

    g_mod = jnp.concatenate([g_rows, g_modc, jnp.zeros((7, 3 * D), F32)], axis=0)
    g_mod_q = lax.dynamic_slice(g_mod, (0, q * nq), (16, nq))
    g_w_ada = _matmul(s_rows, g_mod_q, ta=True, a_act="silu", tm=1024, tn=nq, tk=16, name="w_ada_grad")
    g_b_ada = jnp.sum(g_mod[:9], axis=0, keepdims=True)

    def shard_cols(a, width):
        return lax.dynamic_slice(a, (0, q * width), (a.shape[0], width))

    grads = {
        "c_ctx": g_c_ctx, "norm_g": g_ng, "b_ada": g_b_ada,
        "w_conv_a": shard_cols(g_wca, Wq)[None], "w_conv_b": shard_cols(g_wcb, Wq)[None], "b_conv_b": g_bcb,
        "lru_wa": g_wa[None], "lru_ba": shard_cols(g_ba, Wq)[None], "lru_wx": g_wx[None],
        "lru_bx": shard_cols(g_bx, Wq)[None], "lru_lambda": shard_cols(g_lam, Wq)[None], "final_g": g_fg[0],
    }
    small_names = list(grads)
    given = dict(c_ctx=(c_ctx, m_c_ctx, v_c_ctx), norm_g=(norm_g, m_norm_g, v_norm_g), b_ada=(b_ada, m_b_ada, v_b_ada),
                 w_conv_a=(w_conv_a, m_w_conv_a, v_w_conv_a), w_conv_b=(w_conv_b, m_w_conv_b, v_w_conv_b),
                 b_conv_b=(b_conv_b, m_b_conv_b, v_b_conv_b), lru_wa=(lru_wa, m_lru_wa, v_lru_wa),
                 lru_ba=(lru_ba, m_lru_ba, v_lru_ba), lru_wx=(lru_wx, m_lru_wx, v_lru_wx),
                 lru_bx=(lru_bx, m_lru_bx, v_lru_bx), lru_lambda=(lru_lambda, m_lru_lambda, v_lru_lambda),
                 final_g=(final_g, m_final_g, v_final_g))
    def rows2d(a):
        return a.reshape(-1, a.shape[-1])

    grads = {n: grads[n].reshape(given[n][0].shape) for n in small_names}
    quads = [tuple(rows2d(a) for a in (given[n][0], grads[n], given[n][1], given[n][2])) for n in small_names]
    updated = _adam_many(quads, "adam_small")
    delta_s, newm_s, newv_s = ({n: u[j].reshape(given[n][0].shape) for n, u in zip(small_names, updated)} for j in range(3))

    big = {"w_ada": (w_ada, g_w_ada, m_w_ada, v_w_ada), "w_in": (w_in, g_in_shard, m_w_in, v_w_in),
           "w_out": (w_out, g_out_shard, m_w_out, v_w_out)}
    delta_b, newm_b, newv_b = {}, {}, {}
    for n, (w, g, m, v) in big.items():
        d_, m_, v_, *echo = _adam(w[0], g, m[0], v[0], "adam_" + n, echo_g=n != "w_ada")
        grads[n] = (echo[0] if echo else g)[None]
        delta_b[n], newm_b[n], newv_b[n] = d_[None], m_[None], v_[None]

    loss = loss_sum[0, 0]
    order = ["c_ctx", "norm_g", "w_ada", "b_ada", "w_in", "w_conv_a", "w_conv_b", "b_conv_b", "lru_wa", "lru_ba",
             "lru_wx", "lru_bx", "lru_lambda", "w_out", "final_g"]
    delta = {**delta_s, **delta_b}
    newm = {**newm_s, **newm_b}
    newv = {**newv_s, **newv_b}
    return (loss, gx[None], *[grads[n] for n in order], *[delta[n] for n in order], *[newm[n] for n in order],
            *[newv[n] for n in order])
```

```python
import jax
import jax.numpy as jnp
from jax import lax
from jax.experimental import pallas as pl
from jax.experimental.pallas import tpu as pltpu

F32 = jnp.float32
BF16 = jnp.bfloat16
MESH_ID = pl.DeviceIdType.MESH

EPS = 1e-6
LRU_C = 8.0
GRID_W = 64
ADAM_LR = 0.001
ADAM_B1 = 0.9
ADAM_B2 = 0.999
ADAM_EPS = 1e-08
ADAM_WD = 0.01
ADAM_STEP = 10

LANES = 128
SUBLANES = 8
PACK_COLS = 1024
VMEM_LIMIT = 56 * 2**20
LRU_GROUP = 256


def _params(sem=None):
    return pltpu.CompilerParams(vmem_limit_bytes=VMEM_LIMIT, dimension_semantics=sem)


def _pick(dim, pref, quantum=LANES):
    if dim <= pref:
        return dim
    best = None
    for t in range(quantum, pref + 1, quantum):
        if dim % t == 0:
            best = t
    assert best is not None, (dim, pref)
    return best


def _pos():
    return lax.axis_index("x"), lax.axis_index("y"), lax.axis_index("c")


def _flip(v, bit):
    return 1 - v if bit else v


def _sigmoid(v):
    return 0.5 * jnp.tanh(0.5 * v) + 0.5


def _silu(v):
    return v * _sigmoid(v)


def _dsilu(v):
    s = _sigmoid(v)
    return s * (1.0 + v * (1.0 - s))


def _gates(pre_r, pre_i, sp):
    r = _sigmoid(pre_r)
    ig = _sigmoid(pre_i)
    e = LRU_C * r * sp
    w = jnp.tanh(e)
    return r, ig, jnp.exp(-e), (2.0 * w) * pl.reciprocal(1.0 + w, approx=True)


def _softplus(z):
    return jnp.maximum(z, 0.0) + jnp.log1p(jnp.exp(-jnp.abs(z)))


def _matmul(a, b, *, ta=False, tb=False, tm=512, tn=512, tk=512, out_dtype=F32, name,
            a_rows=None, a_off=0, n_range=None, k_range=None, a_act=None, bias=None, dsilu_mul=None, side=None):
    rows_a = a.shape[0] if a_rows is None else a_rows
    if ta:
        K, M = rows_a, a.shape[1]
    else:
        M, K = rows_a, a.shape[1]
    N = b.shape[0] if tb else b.shape[1]
    n_off, k_off = 0, 0
    if n_range is not None:
        n_off, N = n_range
    if k_range is not None:
        assert not ta
        k_off, K = k_range
    tm, tn, tk = _pick(M, tm, SUBLANES), _pick(N, tn), _pick(K, tk)
    t_rows = tk if ta else tm
    assert a_off % t_rows == 0 and n_off % tn == 0 and k_off % tk == 0
    nk = K // tk
    gi, gj = M // tm, N // tn
    off_blocks, nb, kb = a_off // t_rows, n_off // tn, k_off // tk
    dims = (((0 if ta else 1,), (1 if tb else 0,)), ((), ()))
    extras = [e for e in (bias, dsilu_mul) if e is not None]
    n_sin = len(side["ins"]) if side else 0
    n_sout = len(side["outs"]) if side else 0

    def body(a_ref, b_ref, *rest):
        rest = list(rest)
        bias_ref = rest.pop(0) if bias is not None else None
        dsm_ref = rest.pop(0) if dsilu_mul is not None else None
        side_in = [rest.pop(0) for _ in range(n_sin)]
        o_ref = rest.pop(0)
        side_out = [rest.pop(0) for _ in range(n_sout)]
        acc_ref = rest.pop(0) if nk > 1 else None
        side_scr = rest
        i, j, k = pl.program_id(0), pl.program_id(1), pl.program_id(2)

        if side:
            @pl.when((i == 0) & (j == 0) & (k == 0))
            def _():
                side["start"](side_in, side_out, side_scr)

        av = a_ref[...]
        if a_act == "silu":
            av = _silu(av)
        prod = lax.dot_general(av, b_ref[...], dims, preferred_element_type=F32)

        def finish(r):
            if bias_ref is not None:
                r = r + bias_ref[...]
            if dsm_ref is not None:
                r = r * _dsilu(dsm_ref[...])
            o_ref[...] = r.astype(o_ref.dtype)

        if nk == 1:
            finish(prod)
        else:
            @pl.when(k == 0)
            def _():
                acc_ref[...] = prod

            @pl.when(k > 0)
            def _():
                acc_ref[...] += prod

            @pl.when(k == nk - 1)
            def _():
                finish(acc_ref[...])

        if side:
            @pl.when((i == gi - 1) & (j == gj - 1) & (k == nk - 1))
            def _():
                side["finish"](side_in, side_out, side_scr)

    if ta:
        a_spec = pl.BlockSpec((tk, tm), lambda i, j, k: (k + off_blocks, i))
    else:
        a_spec = pl.BlockSpec((tm, tk), lambda i, j, k: (i + off_blocks, k + kb))
    once = dict(pipeline_mode=pl.Buffered(1)) if (gj == 1 and nk == 1) else {}
    if tb:
        b_spec = pl.BlockSpec((tn, tk), lambda i, j, k: (j + nb, k + kb), **once)
    else:
        b_spec = pl.BlockSpec((tk, tn), lambda i, j, k: (k + kb, j + nb), **once)
    in_specs = [a_spec, b_spec]
    if bias is not None:
        in_specs.append(pl.BlockSpec((1, tn), lambda i, j, k: (0, j)))
    if dsilu_mul is not None:
        in_specs.append(pl.BlockSpec((1, tn), lambda i, j, k: (0, j)))
    hbm = pl.BlockSpec(memory_space=pl.ANY)
    res = pl.pallas_call(
        body, name=name, grid=(gi, gj, nk),
        in_specs=in_specs + [hbm] * n_sin,
        out_specs=[pl.BlockSpec((tm, tn), lambda i, j, k: (i, j))] + [hbm] * n_sout,
        out_shape=[jax.ShapeDtypeStruct((M, N), out_dtype)] + (list(side["outs"]) if side else []),
        scratch_shapes=([pltpu.VMEM((tm, tn), F32)] if nk > 1 else []) + (list(side["scratch"]) if side else []),
        compiler_params=_params(("arbitrary",) * 3 if side else ("parallel", "parallel", "arbitrary")),
    )(a, b, *extras, *(side["ins"] if side else []))
    return (res[0], res[1:]) if side else res[0]


def _elementwise(fn, ins, outs, *, rows, cols, name, tr=256):
    tr = _pick(rows, tr, 2 * SUBLANES)
    n_in = len(ins)

    def body(*refs):
        vals = fn(*[r[...] for r in refs[:n_in]])
        if not isinstance(vals, (tuple, list)):
            vals = (vals,)
        for r, v in zip(refs[n_in:], vals, strict=True):
            r[...] = v.astype(r.dtype)

    def spec(off):
        assert off % tr == 0
        ob = off // tr
        return pl.BlockSpec((tr, cols), lambda i: (i + ob, 0))

    res = pl.pallas_call(
        body, name=name, grid=(rows // tr,),
        in_specs=[spec(off) for _, off in ins],
        out_specs=[spec(0) for _ in outs],
        out_shape=[jax.ShapeDtypeStruct((rows, cols), dt) for dt in outs],
        compiler_params=_params(("parallel",)),
    )(*[a for a, _ in ins])
    return res


def _adam_math(w, g, m, v):
    m = ADAM_B1 * m + (1.0 - ADAM_B1) * g
    v = ADAM_B2 * v + (1.0 - ADAM_B2) * (g * g)
    m_hat = m / (1.0 - ADAM_B1 ** ADAM_STEP)
    v_hat = v / (1.0 - ADAM_B2 ** ADAM_STEP)
    delta = -ADAM_LR * (m_hat / (jnp.sqrt(v_hat) + ADAM_EPS) + ADAM_WD * w)
    return delta, m, v


def _adam(w, g, m, v, name, echo_g=False):
    rows, cols = w.shape
    fn = (lambda w_, g_, m_, v_: _adam_math(w_, g_, m_, v_) + (g_,)) if echo_g else _adam_math
    return _elementwise(fn, [(w, 0), (g, 0), (m, 0), (v, 0)], [F32] * (4 if echo_g else 3),
                        rows=rows, cols=cols, name=name)


def _adam_many(quads, name):
    n = len(quads)

    def body(*refs):
        ins, outs = refs[:4 * n], refs[4 * n:]
        for t in range(n):
            w, g, m, v = (r[...] for r in ins[4 * t:4 * t + 4])
            for o_ref, val in zip(outs[3 * t:3 * t + 3], _adam_math(w, g, m, v), strict=True):
                o_ref[...] = val

    res = pl.pallas_call(
        body, name=name,
        out_shape=[jax.ShapeDtypeStruct(q[0].shape, F32) for q in quads for _ in range(3)],
        compiler_params=_params(),
    )(*[a for q in quads for a in q])
    return [res[3 * t:3 * t + 3] for t in range(n)]


def _pack(arrs, row_quantum):
    flat = jnp.concatenate([a.reshape(-1).astype(F32) for a in arrs])
    n = flat.shape[0]
    q = row_quantum * PACK_COLS
    total = -(-n // q) * q
    flat = jnp.pad(flat, (0, total - n))
    return flat.reshape(total // PACK_COLS, PACK_COLS)


def _unpack(buf, shapes):
    flat = buf.reshape(-1)
    out, off = [], 0
    for s in shapes:
        n = 1
        for d in s:
            n *= d
        out.append(flat[off:off + n].reshape(s))
        off += n
    return out


def _allreduce8(buf, name):
    R, C = buf.shape
    assert R % (8 * SUBLANES) == 0
    m = R // 8

    def body(x_ref, o_ref, recv, red, s1, r1, s2, r2):
        x, y, c = _pos()
        me = 4 * x + 2 * y + c

        def peer(k):
            px, py, pc = _flip(x, (k >> 2) & 1), _flip(y, (k >> 1) & 1), _flip(c, k & 1)
            return (px, py, pc), 4 * px + 2 * py + pc

        def rows(ref, idx):
            return ref.at[pl.ds(pl.multiple_of(idx * m, SUBLANES), m), :]

        def scatter(k):
            dev, p = peer(k)
            return pltpu.make_async_remote_copy(src_ref=rows(x_ref, p), dst_ref=recv.at[k], send_sem=s1.at[k],
                                                recv_sem=r1.at[k], device_id=dev, device_id_type=MESH_ID)

        def share(k):
            dev, p = peer(k)
            return pltpu.make_async_remote_copy(src_ref=red, dst_ref=rows(o_ref, me), send_sem=s2.at[k],
                                                recv_sem=r2.at[k], device_id=dev, device_id_type=MESH_ID)

        def shared_from(k):
            dev, p = peer(k)
            return pltpu.make_async_remote_copy(src_ref=red, dst_ref=rows(o_ref, p), send_sem=s2.at[k],
                                                recv_sem=r2.at[k], device_id=dev, device_id_type=MESH_ID)

        for k in range(1, 8):
            scatter(k).start()
        acc = rows(x_ref, me)[...]
        for k in range(1, 8):
            scatter(k).wait_recv()
            acc = acc + recv[k]
        red[...] = acc
        rows(o_ref, me)[...] = acc
        for k in range(1, 8):
            share(k).start()
        for k in range(1, 8):
            shared_from(k).wait_recv()
        for k in range(1, 8):
            scatter(k).wait_send()
            share(k).wait_send()

    return pl.pallas_call(
        body, name=name,
        in_specs=[pl.BlockSpec(memory_space=pltpu.VMEM)],
        out_specs=pl.BlockSpec(memory_space=pltpu.VMEM),
        out_shape=jax.ShapeDtypeStruct((R, C), F32),
        scratch_shapes=[pltpu.VMEM((8, m, C), F32), pltpu.VMEM((m, C), F32),
                        pltpu.SemaphoreType.DMA((8,)), pltpu.SemaphoreType.DMA((8,)),
                        pltpu.SemaphoreType.DMA((8,)), pltpu.SemaphoreType.DMA((8,))],
        compiler_params=_params(),
    )(buf)


def _allreduce8_two_level(buf, name):
    R, C = buf.shape
    assert R % (8 * SUBLANES) == 0
    m, hr = R // 8, R // 2

    def body(x_ref, o_ref, got0, half, got1, red, ssem, rsem):
        x, y, c = _pos()
        q = 2 * x + y
        sibling = (x, y, 1 - c)

        def half_rows(ref, core):
            return ref.at[pl.ds(pl.multiple_of(core * hr, SUBLANES), hr), :]

        def chunk(ref, core, chip):
            return ref.at[pl.ds(pl.multiple_of(core * hr + chip * m, SUBLANES), m), :]

        def chip_of(k):
            px, py = _flip(x, (k >> 1) & 1), _flip(y, k & 1)
            return (px, py, c), 2 * px + py

        def copy(src, dst, phase, k, dev):
            return pltpu.make_async_remote_copy(src_ref=src, dst_ref=dst, send_sem=ssem.at[phase, k],
                                                recv_sem=rsem.at[phase, k], device_id=dev, device_id_type=MESH_ID)

        swap = copy(half_rows(x_ref, 1 - c), got0, 0, 0, sibling)
        swap.start()
        swap.wait()
        half[...] = half_rows(x_ref, c)[...] + got0[...]

        def scatter(k):
            dev, p = chip_of(k)
            return copy(half.at[pl.ds(pl.multiple_of(p * m, SUBLANES), m), :], got1.at[k], 1, k, dev)

        for k in range(1, 4):
            scatter(k).start()
        acc = half[pl.ds(pl.multiple_of(q * m, SUBLANES), m), :]
        for k in range(1, 4):
            scatter(k).wait_recv()
            acc = acc + got1[k]
        red[...] = acc
        chunk(o_ref, c, q)[...] = acc

        def share(k, landing_chip):
            return copy(red, chunk(o_ref, c, landing_chip), 2, k, chip_of(k)[0])

        for k in range(1, 4):
            share(k, q).start()
        for k in range(1, 4):
            share(k, chip_of(k)[1]).wait_recv()
        back = copy(half_rows(o_ref, c), half_rows(o_ref, c), 3, 0, sibling)
        back.start()
        copy(half_rows(o_ref, 1 - c), half_rows(o_ref, 1 - c), 3, 0, sibling).wait_recv()
        back.wait_send()
        for k in range(1, 4):
            scatter(k).wait_send()
            share(k, q).wait_send()

    return pl.pallas_call(
        body, name=name,
        in_specs=[pl.BlockSpec(memory_space=pltpu.VMEM)],
        out_specs=pl.BlockSpec(memory_space=pltpu.VMEM),
        out_shape=jax.ShapeDtypeStruct((R, C), F32),
        scratch_shapes=[pltpu.VMEM((hr, C), F32), pltpu.VMEM((hr, C), F32), pltpu.VMEM((4, m, C), F32), pltpu.VMEM((m, C), F32),
                        pltpu.SemaphoreType.DMA((4, 4)), pltpu.SemaphoreType.DMA((4, 4))],
        compiler_params=_params(),
    )(buf)


def _bounce(src, dst, buf, sem):
    cin = pltpu.make_async_copy(src, buf, sem)
    cin.start()
    cin.wait()
    cout = pltpu.make_async_copy(buf, dst, sem)
    cout.start()
    cout.wait()


def _chunk(ref, axis, idx, size):
    start = idx * size
    if axis == 0:
        return ref.at[pl.ds(start, size), :]
    return ref.at[:, pl.ds(start, size)]


def _in_proj_gather(hl, win, wout, q_arr, *, rows, tm):
    D, nq = win.shape
    dq, D2 = wout.shape
    ni = rows // tm
    ops = ((0, 1, nq, D // 2), (1, 0, dq, dq // 2))

    def body(q_ref, a_ref, win_ref, wout_ref, p_ref, gin_ref, gout_ref, b_scr, buf_out, lsem, ssem, rsem, fsem, gsem):
        j, i = pl.program_id(0), pl.program_id(1)
        x, y, c = _pos()
        q = 2 * x + y
        srcs = (win_ref, wout_ref)
        dsts = (gin_ref, gout_ref)

        def shard_window(o, chip):
            _, axis, size, _ = ops[o]
            return _chunk(dsts[o], axis, chip, size)

        def half(ref, o, core):
            return ref.at[pl.ds(core * ops[o][3], ops[o][3]), :]

        def half_window(o, chip, core):
            _, axis, size, hs = ops[o]
            if axis == 1:
                return dsts[o].at[pl.ds(core * hs, hs), pl.ds(chip * size, size)]
            return dsts[o].at[pl.ds(chip * size + core * hs, hs), :]

        def chip_of(k):
            px, py = _flip(x, (k >> 1) & 1), _flip(y, k & 1)
            return px, py, 2 * px + py

        def send(o, k):
            px, py, _ = chip_of(k)
            return pltpu.make_async_remote_copy(
                src_ref=half(srcs[o], o, c), dst_ref=half_window(o, q, c), send_sem=ssem.at[o, k],
                recv_sem=rsem.at[o, k], device_id=(px, py, c), device_id_type=MESH_ID)

        def chip_recv(o, k):
            px, py, pq = chip_of(k)
            landed = half_window(o, pq, c)
            pltpu.make_async_remote_copy(src_ref=landed, dst_ref=landed, send_sem=ssem.at[o, k], recv_sem=rsem.at[o, k],
                                         device_id=(px, py, c), device_id_type=MESH_ID).wait_recv()

        def to_sibling(o, k):
            landed = half_window(o, chip_of(k)[2], c)
            return pltpu.make_async_remote_copy(src_ref=landed, dst_ref=landed, send_sem=fsem.at[o, k],
                                                recv_sem=gsem.at[o, k], device_id=(x, y, 1 - c), device_id_type=MESH_ID)

        def from_sibling(o, k):
            theirs = half_window(o, chip_of(k)[2], 1 - c)
            pltpu.make_async_remote_copy(src_ref=theirs, dst_ref=theirs, send_sem=fsem.at[o, k], recv_sem=gsem.at[o, k],
                                         device_id=(x, y, 1 - c), device_id_type=MESH_ID).wait_recv()

        def relay(o, core):
            if core == 0:
                landed, target = half_window(o, chip_of(2)[2], 0), (x, 1 - y, 0)
            else:
                landed, target = half_window(o, chip_of(1)[2], 1), (1 - x, y, 1)
            return pltpu.make_async_remote_copy(src_ref=landed, dst_ref=landed, send_sem=ssem.at[o, 3],
                                                recv_sem=rsem.at[o, 3], device_id=target, device_id_type=MESH_ID)

        def on_core(core, fn):
            @pl.when(c == core)
            def _():
                fn()

        def land(o, k):
            chip_recv(o, k)
            if k == 2:
                on_core(0, lambda: relay(o, 0).start())
            if k == 1:
                on_core(1, lambda: relay(o, 1).start())
            to_sibling(o, k).start()

        def settle(o, k):
            from_sibling(o, k)
            to_sibling(o, k).wait_send()

        def b_load(k, slot):
            src = win_ref if k == 0 else shard_window(0, chip_of(k)[2])
            return pltpu.make_async_copy(src, b_scr.at[slot], lsem.at[0])

        def own_store():
            return pltpu.make_async_copy(b_scr.at[0], shard_window(0, q), lsem.at[2])

        order = (0, 2, 1, 3)
        early = max(ni - 2, 0)

        @pl.when((j == 0) & (i == 0))
        def _():
            for o in range(2):
                for k in (2, 1):
                    send(o, k).start()
            first = b_load(0, 0)
            first.start()
            first.wait()
            own_store().start()
            _bounce(wout_ref, shard_window(1, q), buf_out, lsem.at[1])

        for jj in range(3):
            nxt = order[jj + 1]

            @pl.when((j == jj) & (i == early))
            def _(nxt=nxt):
                land(0, nxt)

            @pl.when((j == jj) & (i == ni - 1))
            def _(jj=jj, nxt=nxt):
                settle(0, nxt)
                if jj == 1:
                    own_store().wait()
                b_load(nxt, (jj + 1) % 2).start()

            @pl.when((j == jj + 1) & (i == 0))
            def _(jj=jj, nxt=nxt):
                b_load(nxt, (jj + 1) % 2).wait()

        @pl.when((j == 3) & (i == 0))
        def _():
            land(1, 2)
            land(1, 1)

        p_ref[...] = jnp.dot(a_ref[...], b_scr[j % 2], preferred_element_type=F32).astype(p_ref.dtype)

        @pl.when((j == 3) & (i == ni - 1))
        def _():
            settle(1, 2)
            settle(1, 1)
            land(1, 3)
            settle(1, 3)
            for o in range(2):
                for k in (2, 1):
                    send(o, k).wait_send()
                for core in range(2):
                    on_core(core, lambda o=o, core=core: relay(o, core).wait_send())

    hbm = pl.BlockSpec(memory_space=pl.ANY)
    grid_spec = pltpu.PrefetchScalarGridSpec(
        num_scalar_prefetch=1, grid=(4, ni),
        in_specs=[pl.BlockSpec((tm, D), lambda j, i, qr: (i, 0)), hbm, hbm],
        out_specs=[pl.BlockSpec((tm, nq), lambda j, i, qr: (i, jnp.bitwise_xor(qr[0], ((j & 1) << 1) | (j >> 1)))),
                   hbm, hbm],
        scratch_shapes=[pltpu.VMEM((2,) + win.shape, win.dtype), pltpu.VMEM(wout.shape, wout.dtype), pltpu.SemaphoreType.DMA((3,))]
        + [pltpu.SemaphoreType.DMA((2, 4)) for _ in range(4)])
    return pl.pallas_call(
        body, name="in_proj_gather", grid_spec=grid_spec,
        out_shape=[jax.ShapeDtypeStruct((rows, 4 * nq), BF16), jax.ShapeDtypeStruct((D, 4 * nq), win.dtype),
                   jax.ShapeDtypeStruct((4 * dq, D2), wout.dtype)],
        compiler_params=_params(("arbitrary", "arbitrary")),
    )(q_arr, hl, win, wout)


def _rs_to_sibling(gs, axes):
    n = len(gs)
    shapes = []
    for g, ax in zip(gs, axes):
        s = list(g.shape)
        s[ax] //= 8
        shapes.append(tuple(s))

    def body(*refs):
        g_refs, mine, landed = refs[:n], refs[n:2 * n], refs[2 * n:3 * n]
        bufs = refs[3 * n:4 * n]
        lsem, ssem, rsem = refs[4 * n:]
        x, y, c = _pos()
        cps = []
        for o in range(n):
            size = shapes[o][axes[o]]
            for j in range(4):
                rc = pltpu.make_async_remote_copy(
                    src_ref=_chunk(g_refs[o], axes[o], 2 * j + 1 - c, size), dst_ref=landed[o].at[j],
                    send_sem=ssem.at[o, j], recv_sem=rsem.at[o, j], device_id=(x, y, 1 - c), device_id_type=MESH_ID)
                rc.start()
                cps.append(rc)
        for o in range(n):
            size = shapes[o][axes[o]]
            loads = [pltpu.make_async_copy(_chunk(g_refs[o], axes[o], 2 * j + c, size), bufs[o].at[j % 2], lsem.at[o, j % 2])
                     for j in range(4)]
            stores = [pltpu.make_async_copy(bufs[o].at[j % 2], mine[o].at[j], lsem.at[o, 2 + j % 2]) for j in range(4)]
            loads[0].start()
            for j in range(4):
                loads[j].wait()
                stores[j].start()
                if j + 1 < 4:
                    if j >= 1:
                        stores[j - 1].wait()
                    loads[j + 1].start()
            stores[2].wait()
            stores[3].wait()
        for rc in cps:
            rc.wait()

    hbm = pl.BlockSpec(memory_space=pl.ANY)
    outs = [jax.ShapeDtypeStruct((4,) + s, g.dtype) for s, g in zip(shapes, gs)]
    res = pl.pallas_call(
        body, name="rs_to_sibling", in_specs=[hbm] * n, out_specs=[hbm] * (2 * n), out_shape=outs + outs,
        scratch_shapes=[pltpu.VMEM((2,) + s, g.dtype) for s, g in zip(shapes, gs)]
        + [pltpu.SemaphoreType.DMA((n, 4)), pltpu.SemaphoreType.DMA((n, 4)), pltpu.SemaphoreType.DMA((n, 4))],
        compiler_params=_params(),
    )(*gs)
    return res[:n], res[n:]


def _gather_block_side(block):
    r, n_cols = block.shape

    def copies(ins, outs, scr):
        ssem, rsem = scr[1], scr[2]
        x, y, c = _pos()
        return [pltpu.make_async_remote_copy(
            src_ref=ins[0], dst_ref=outs[0].at[k], send_sem=ssem.at[k], recv_sem=rsem.at[k],
            device_id=(_flip(x, (k >> 2) & 1), _flip(y, (k >> 1) & 1), _flip(c, k & 1)), device_id_type=MESH_ID)
            for k in range(1, 8)]

    def start(ins, outs, scr):
        for cp in copies(ins, outs, scr):
            cp.start()

    def finish(ins, outs, scr):
        _bounce(ins[0], outs[0].at[0], scr[0], scr[3].at[0])
        for cp in copies(ins, outs, scr):
            cp.wait()

    return dict(ins=[block], outs=[jax.ShapeDtypeStruct((8, r, n_cols), block.dtype)],
                scratch=[pltpu.VMEM((r, n_cols), block.dtype), pltpu.SemaphoreType.DMA((8,)), pltpu.SemaphoreType.DMA((8,)),
                         pltpu.SemaphoreType.DMA((1,))],
                start=start, finish=finish)


def _join_sides(a, b):
    na_i, na_o, na_s = len(a["ins"]), len(a["outs"]), len(a["scratch"])

    def run(which):
        def fn(ins, outs, scr):
            a[which](ins[:na_i], outs[:na_o], scr[:na_s])
            b[which](ins[na_i:], outs[na_o:], scr[na_s:])
        return fn

    return dict(ins=a["ins"] + b["ins"], outs=a["outs"] + b["outs"], scratch=a["scratch"] + b["scratch"],
                start=run("start"), finish=run("finish"))


def _rs_chips_side(parts):
    n = len(parts)

    def copies(p_refs, slots, scr):
        ssem, rsem = scr[n + 1], scr[n + 2]
        x, y, c = _pos()
        cps = []
        for o in range(n):
            for k in range(1, 4):
                px, py = _flip(x, (k >> 1) & 1), _flip(y, k & 1)
                cps.append(pltpu.make_async_remote_copy(
                    src_ref=p_refs[o].at[2 * px + py], dst_ref=slots[o].at[k], send_sem=ssem.at[o, k],
                    recv_sem=rsem.at[o, k], device_id=(px, py, c), device_id_type=MESH_ID))
        return cps

    def start(p_refs, slots, scr):
        for cp in copies(p_refs, slots, scr):
            cp.start()

    def finish(p_refs, slots, scr):
        x, y, _ = _pos()
        q = 2 * x + y
        for o in range(n):
            _bounce(p_refs[o].at[q], slots[o].at[0], scr[o], scr[n].at[o])
        for cp in copies(p_refs, slots, scr):
            cp.wait()

    return dict(
        ins=list(parts), outs=[jax.ShapeDtypeStruct(p.shape, p.dtype) for p in parts],
        scratch=[pltpu.VMEM(p.shape[1:], p.dtype) for p in parts]
        + [pltpu.SemaphoreType.DMA((n,)), pltpu.SemaphoreType.DMA((n, 4)), pltpu.SemaphoreType.DMA((n, 4))],
        start=start, finish=finish)


def _rs_share(rs, axes):
    n = len(rs)
    shapes = []
    for r, ax in zip(rs, axes):
        s = list(r.shape)
        s[ax] *= 2
        shapes.append(tuple(s))

    def body(*refs):
        r_refs, outs = refs[:n], refs[n:2 * n]
        bufs = refs[2 * n:3 * n]
        lsem, ssem, rsem = refs[3 * n:]
        x, y, c = _pos()
        cps = []
        for o in range(n):
            size = r_refs[o].shape[axes[o]]
            window = _chunk(outs[o], axes[o], c, size)
            rc = pltpu.make_async_remote_copy(src_ref=r_refs[o], dst_ref=window, send_sem=ssem.at[o], recv_sem=rsem.at[o],
                                              device_id=(x, y, 1 - c), device_id_type=MESH_ID)
            rc.start()
            cps.append(rc)
        for o in range(n):
            size = r_refs[o].shape[axes[o]]
            _bounce(r_refs[o], _chunk(outs[o], axes[o], c, size), bufs[o], lsem.at[o])
        for cp in cps:
            cp.wait()

    hbm = pl.BlockSpec(memory_space=pl.ANY)
    return pl.pallas_call(
        body, name="rs_share", in_specs=[hbm] * n, out_specs=[hbm] * n,
        out_shape=[jax.ShapeDtypeStruct(s, r.dtype) for s, r in zip(shapes, rs)],
        scratch_shapes=[pltpu.VMEM(r.shape, r.dtype) for r in rs] + [pltpu.SemaphoreType.DMA((n,)) for _ in range(3)],
        compiler_params=_params(),
    )(*rs)


def _rs_pair_sums(gs, axes):
    mine, landed = _rs_to_sibling(gs, axes)
    pair_sums = []
    for o, (mi, la) in enumerate(zip(mine, landed)):
        rows, cols = mi.shape[0] * mi.shape[1], mi.shape[2]
        s = _elementwise(lambda a, b: a.astype(F32) + b.astype(F32), [(mi.reshape(rows, cols), 0), (la.reshape(rows, cols), 0)],
                         [BF16], rows=rows, cols=cols, name=f"rs_pair_sum{o}")[0]
        pair_sums.append(s.reshape(mi.shape))
    return pair_sums


def _rs_finish(slots, axes):
    reduced = []
    for o, sl in enumerate(slots):
        rows, cols = sl.shape[1], sl.shape[2]
        flat = sl.reshape(4 * rows, cols)
        r = _elementwise(lambda a, b, c, d: (a.astype(F32) + b.astype(F32)) + (c.astype(F32) + d.astype(F32)),
                         [(flat, k * rows) for k in range(4)], [F32], rows=rows, cols=cols, name=f"rs_chip_sum{o}")[0]
        reduced.append(r)
    return _rs_share(reduced, axes)


def _norm_in(x, ctx, g, sc_l, sh_l, sc_c, sh_c, tr):
    L, D = x.shape
    T = ctx.shape[0]
    nx, nc = L // tr, T // tr

    def body(x_ref, c_ref, g_ref, scl, shl, scc, shc, o_ref):
        i = pl.program_id(0)

        def run(src, sc, sh):
            v = src[...]
            r = lax.rsqrt(jnp.mean(v * v, axis=-1, keepdims=True) + EPS)
            o_ref[...] = ((v * r * g_ref[...]) * (1.0 + sc[...]) + sh[...]).astype(o_ref.dtype)

        @pl.when(i < nx)
        def _():
            run(x_ref, scl, shl)

        @pl.when(i >= nx)
        def _():
            run(c_ref, scc, shc)

    vec = pl.BlockSpec((1, D), lambda i: (0, 0))
    return pl.pallas_call(
        body, name="norm_in", grid=(nx + nc,),
        in_specs=[pl.BlockSpec((tr, D), lambda i: (jnp.minimum(i, nx - 1), 0)),
                  pl.BlockSpec((tr, D), lambda i: (jnp.maximum(i - nx, 0), 0)), vec, vec, vec, vec, vec],
        out_specs=pl.BlockSpec((tr, D), lambda i: (i, 0)),
        out_shape=jax.ShapeDtypeStruct((L + T, D), BF16),
        compiler_params=_params(("arbitrary",)),
    )(x, ctx, g, sc_l, sh_l, sc_c, sh_c)


def _tmod(tl, row_w):
    assert row_w & (row_w - 1) == 0
    return lax.broadcasted_iota(jnp.int32, (tl, 1), 0) & (row_w - 1)


def _shift(z, k, tmod, row_w):
    tl = z.shape[0]
    rolled = pltpu.roll(z, k % tl, 0)
    mask = (tmod >= k) if k > 0 else (tmod < row_w + k)
    return jnp.where(mask, rolled, 0.0)


def _conv(z, w_ref, taps, left, tmod, row_w, lanes=slice(None)):
    out = None
    for j in range(taps):
        k = left - j
        term = (z if k == 0 else _shift(z, k, tmod, row_w)) * w_ref[j:j + 1, lanes]
        out = term if out is None else out + term
    return out


def _conv_bwd(dz, z, w_ref, taps, left, tmod, row_w, lanes=slice(None)):
    din = None
    dws = []
    for j in range(taps):
        k = left - j
        shifted = dz if k == 0 else _shift(dz, -k, tmod, row_w)
        term = shifted * w_ref[j:j + 1, lanes]
        din = term if din is None else din + term
        dws.append(jnp.sum(shifted * z, axis=0, keepdims=True))
    return din, dws


def _gate_matmul(xb16_ref, wd_ref, pre_scr, W, ng, gs):
    for g in range(ng):
        pg = jnp.dot(xb16_ref[:, g * gs:(g + 1) * gs], wd_ref[g], preferred_element_type=F32)
        pre_scr[:, g * gs:(g + 1) * gs] = pg[:, :gs]
        pre_scr[:, W + g * gs:W + (g + 1) * gs] = pg[:, gs:]


def _f32(ref, rows, lanes):
    return ref[rows, lanes].astype(F32)


def _sub_loop(tl, sub_r, W, fn):
    def chunk(ci, carry):
        r0 = pl.multiple_of(ci * sub_r, sub_r)
        for lb in range(W // LANES):
            fn(r0, lb * LANES)
        return carry

    lax.fori_loop(0, tl // sub_r, chunk, 0)


def _row_loop(tl, rev, step, init):
    nchunk = tl // SUBLANES

    def chunk(j, carry):
        jj = (nchunk - 1 - j) if rev else j
        c0 = pl.multiple_of(jj * SUBLANES, SUBLANES)
        for r in (range(SUBLANES - 1, -1, -1) if rev else range(SUBLANES)):
            carry = step(c0 + r, carry)
        return carry

    return lax.fori_loop(0, nchunk, chunk, init)


def _mix_fwd(P, d, h_init, wts, *, rows, row_off, row_w, tl, saved0=None, name):
    W = P.shape[1] // 6
    nt = rows // tl
    ob = row_off // tl
    rev = d == 1
    gs = min(LRU_GROUP, W)
    ng = W // gs
    wca, wcb, bcb = wts["wca"], wts["wcb"], wts["bcb"]
    wd, ba, bx, lam = wts["wd"][d], wts["ba"][d], wts["bx"][d], wts["lam"][d]

    def tile(i):
        return (nt - 1 - i) if rev else i

    def pcol(j):
        return pl.BlockSpec((tl, W), lambda i: (tile(i) + ob, j))

    vec = pl.BlockSpec((1, W), lambda i: (0, 0))
    taps = pl.BlockSpec((SUBLANES, W), lambda i: (0, 0))
    wd_spec = pl.BlockSpec(wd.shape, lambda i: (0, 0, 0))
    seq = pl.BlockSpec((tl, W), lambda i: (tile(i), 0))

    sub_r = min(row_w, tl)
    assert tl % sub_r == 0

    def body(*refs):
        if rev:
            (bl, cl, ul, gl, ql, ho, xb_r, xb16_r, wca_r, wd_r, ba_r, bx_r, lam_r, hin, hseq, cat, a_o, r_o, ig_o, m2_o,
             b_scr, pre_scr, carry, sp_scr) = refs
        else:
            (vl, wcb_r, bcb_r, wd_r, ba_r, bx_r, lam_r, hin, hseq, xb_r, xb16_r, a_o, r_o, ig_o, m2_o,
             b_scr, pre_scr, carry, sp_scr) = refs
        i = pl.program_id(0)

        @pl.when(i == 0)
        def _():
            carry[...] = hin[...]

        sp_scr[...] = _softplus(-lam_r[...])
        tmod = _tmod(sub_r, sub_r)

        def conv_in(r0, l0):
            rs, ls = pl.ds(r0, sub_r), pl.ds(l0, LANES)
            xb = _conv(_f32(vl, rs, ls), wcb_r, 4, 2, tmod, sub_r, ls) + bcb_r[:, ls]
            xb_r[rs, ls] = xb
            xb16_r[rs, ls] = xb.astype(BF16)

        def gates(r0, l0):
            rs, ls = pl.ds(r0, sub_r), pl.ds(l0, LANES)
            r, ig, a, m2 = _gates(pre_scr[rs, ls] + ba_r[:, ls], pre_scr[rs, pl.ds(W + l0, LANES)] + bx_r[:, ls],
                                  sp_scr[:, ls])
            a_o[rs, ls] = a
            r_o[rs, ls] = r.astype(r_o.dtype)
            ig_o[rs, ls] = ig.astype(ig_o.dtype)
            m2_o[rs, ls] = m2.astype(m2_o.dtype)
            m = jnp.where(m2 > 0.0, m2 * lax.rsqrt(m2), 0.0)
            b_scr[rs, ls] = m * (ig * xb_r[rs, ls])

        if not rev:
            _sub_loop(tl, sub_r, W, conv_in)
        _gate_matmul(xb16_r, wd_r, pre_scr, W, ng, gs)
        _sub_loop(tl, sub_r, W, gates)

        def step(t, h):
            h = a_o[pl.ds(t, 1), :] * h + b_scr[pl.ds(t, 1), :]
            hseq[pl.ds(t, 1), :] = h
            return h

        carry[...] = _row_loop(tl, rev, step, carry[...])

        if rev:
            def mix_out(r0, l0):
                rs, ls = pl.ds(r0, sub_r), pl.ds(l0, LANES)
                yb = (ho[rs, ls] + hseq[rs, ls]) * _silu(_f32(ql, rs, ls))
                ya = (_f32(bl, rs, ls) * _conv(_f32(cl, rs, ls) * _f32(ul, rs, ls), wca_r, 3, 1, tmod, sub_r, ls)
                      * _silu(_f32(gl, rs, ls)))
                cat[rs, ls] = ya.astype(cat.dtype)
                cat[rs, pl.ds(W + l0, LANES)] = yb.astype(cat.dtype)

            _sub_loop(tl, sub_r, W, mix_out)

    scratch = [pltpu.VMEM((tl, W), F32), pltpu.VMEM((tl, 2 * W), F32), pltpu.VMEM((1, W), F32), pltpu.VMEM((1, W), F32)]
    f32_seq = jax.ShapeDtypeStruct((rows, W), F32)
    kept_gates = [f32_seq] + [jax.ShapeDtypeStruct((rows, W), BF16)] * 3
    if rev:
        in_specs = [pcol(j) for j in (0, 1, 2, 3, 5)] + [seq, seq, seq, taps, wd_spec, vec, vec, vec, vec]
        args = [P] * 5 + [saved0["h"], saved0["xb"], saved0["xb16"], wca, wd, ba, bx, lam, h_init]
        out_specs = [seq, pl.BlockSpec((tl, 2 * W), lambda i: (tile(i), 0))] + [seq] * 4
        out_shape = [f32_seq, jax.ShapeDtypeStruct((rows, 2 * W), BF16)] + kept_gates
    else:
        in_specs = [pcol(4), taps, vec, wd_spec, vec, vec, vec, vec]
        args = [P, wcb, bcb, wd, ba, bx, lam, h_init]
        out_specs = [seq] * 7
        out_shape = [f32_seq, f32_seq, jax.ShapeDtypeStruct((rows, W), BF16)] + kept_gates
    res = pl.pallas_call(
        body, name=name, grid=(nt,), in_specs=in_specs, out_specs=out_specs, out_shape=out_shape,
        scratch_shapes=scratch, compiler_params=_params(("arbitrary",)),
    )(*args)
    gates = dict(zip(("a", "r", "ig", "m2"), res[-4:]))
    if rev:
        return res[0], res[1], gates
    return dict(h=res[0], xb=res[1], xb16=res[2]), gates


_BWD_SCRATCH = ("dyl", "g", "dp16", "sp", "dlf", "edge", "c")
_FWD_SAVED = ("xb", "xb16", "a", "r", "ig", "m2")


def _bwd_scratch(tl, W):
    shapes = {"dyl": pltpu.VMEM((tl, W), F32), "g": pltpu.VMEM((tl, W), F32), "dp16": pltpu.VMEM((tl, 2 * W), BF16),
              "sp": pltpu.VMEM((1, W), F32), "dlf": pltpu.VMEM((1, W), F32), "edge": pltpu.VMEM((1, W), F32),
              "c": pltpu.VMEM((1, W), F32)}
    return [shapes[n] for n in _BWD_SCRATCH]


def _lru_bwd_tile(d, dy_fn, hs_ref, wd_r, lam_r, scr, acc, first, tl, sub_r, ng, gs):
    dwd_ref, dba_ref, dbx_ref, dlam_ref = acc
    W = hs_ref.shape[1]
    assert gs % LANES == 0
    rev = d == 0
    lam = lam_r[...]
    scr["sp"][...] = _softplus(-lam)
    scr["dlf"][...] = -_sigmoid(-lam)

    @pl.when(first)
    def _():
        dwd_ref[...] = jnp.zeros_like(dwd_ref)
        dba_ref[...] = jnp.zeros_like(dba_ref)
        dbx_ref[...] = jnp.zeros_like(dbx_ref)
        dlam_ref[...] = jnp.zeros_like(dlam_ref)

    def state_grad(r0, l0):
        rs, ls = pl.ds(r0, sub_r), pl.ds(l0, LANES)
        scr["dyl"][rs, ls] = dy_fn(rs, ls)

    _sub_loop(tl, sub_r, W, state_grad)

    def step(t, c):
        g = scr["dyl"][pl.ds(t, 1), :] + c
        scr["g"][pl.ds(t, 1), :] = g
        return scr["a"][pl.ds(t, 1), :] * g

    scr["c"][...] = _row_loop(tl, rev, step, scr["c"][...])
    row = lax.broadcasted_iota(jnp.int32, (sub_r, 1), 0)

    def grads(r0, l0):
        rs, ls = pl.ds(r0, sub_r), pl.ds(l0, LANES)
        g, a, m2 = scr["g"][rs, ls], scr["a"][rs, ls], _f32(scr["m2"], rs, ls)
        r, ig, xb = _f32(scr["r"], rs, ls), _f32(scr["ig"], rs, ls), scr["xb"][rs, ls]
        h = hs_ref[rs, ls]
        if d == 0:
            e0 = pl.multiple_of(jnp.maximum(r0 - SUBLANES, 0), SUBLANES)
            edge = jnp.where(r0 == 0, scr["edge"][:, ls], hs_ref[pl.ds(e0, SUBLANES), ls][SUBLANES - 1:, :])
            hprev = jnp.where(row == 0, edge, pltpu.roll(h, 1, 0))
        else:
            e0 = pl.multiple_of(jnp.minimum(r0 + sub_r, tl - SUBLANES), SUBLANES)
            edge = jnp.where(r0 == tl - sub_r, scr["edge"][:, ls], hs_ref[pl.ds(e0, SUBLANES), ls][:1, :])
            hprev = jnp.where(row == sub_r - 1, edge, pltpu.roll(h, sub_r - 1, 0))
        rsq = lax.rsqrt(m2)
        gm = g * (m2 * rsq)
        d_la = (g * hprev) * a - (g * (ig * xb)) * ((1.0 - m2) * rsq)
        d_pr = d_la * ((-LRU_C) * scr["sp"][:, ls]) * (r * (1.0 - r))
        d_pi = (gm * xb) * (ig * (1.0 - ig))
        scr["dyl"][rs, ls] = gm * ig
        dlam_ref[:, ls] += jnp.sum(d_la * ((-LRU_C) * r), axis=0, keepdims=True) * scr["dlf"][:, ls]
        dba_ref[:, ls] += jnp.sum(d_pr, axis=0, keepdims=True)
        dbx_ref[:, ls] += jnp.sum(d_pi, axis=0, keepdims=True)
        gi, off = divmod(l0, gs)
        scr["dp16"][rs, pl.ds(gi * 2 * gs + off, LANES)] = d_pr.astype(BF16)
        scr["dp16"][rs, pl.ds(gi * 2 * gs + gs + off, LANES)] = d_pi.astype(BF16)

    _sub_loop(tl, sub_r, W, grads)
    for gi in range(ng):
        dp = scr["dp16"][:, gi * 2 * gs:(gi + 1) * 2 * gs]
        scr["g"][:, gi * gs:(gi + 1) * gs] = lax.dot_general(dp, wd_r[gi], (((1,), (1,)), ((), ())),
                                                             preferred_element_type=F32)
        dwd_ref[gi] += lax.dot_general(scr["xb16"][:, gi * gs:(gi + 1) * gs], dp, (((0,), (0,)), ((), ())),
                                       preferred_element_type=F32)


def _edge_block(h, tl, nt, d):
    W = h.shape[1]
    per = tl // SUBLANES
    if d == 0:
        return pl.BlockSpec((SUBLANES, W), lambda i: (jnp.maximum((nt - 1 - i) * per - 1, 0), 0))
    return pl.BlockSpec((SUBLANES, W), lambda i: (jnp.minimum((i + 1) * per, nt * per - 1), 0))


def _mix_bwd0(P, dcat, saved0, gates0, h_init, c_init, wts, *, rows, row_off, row_w, tl, name):
    W = P.shape[1] // 6
    nt = rows // tl
    ob = row_off // tl
    gs = min(LRU_GROUP, W)
    ng = W // gs
    wd, lam = wts["wd"][0], wts["lam"][0]
    h0s = saved0["h"]
    kept = [saved0["xb"], saved0["xb16"]] + [gates0[n] for n in ("a", "r", "ig", "m2")]

    def tile(i):
        return nt - 1 - i

    vec = pl.BlockSpec((1, W), lambda i: (0, 0))
    wd_spec = pl.BlockSpec(wd.shape, lambda i: (0, 0, 0))
    seq = pl.BlockSpec((tl, W), lambda i: (tile(i), 0))

    sub_r = min(row_w, tl)
    assert tl % sub_r == 0

    def body(ql, dyb, hs, hedge8, xb_r, xb16_r, a_r, r_r, ig_r, m2_r, wd_r, lam_r, hin, cin,
             dxb_o, dwd_o, dba_o, dbx_o, dlam_o, cfin, *scratch):
        scr = dict(zip(_BWD_SCRATCH, scratch, strict=True))
        scr.update(zip(_FWD_SAVED, (xb_r, xb16_r, a_r, r_r, ig_r, m2_r), strict=True))
        i = pl.program_id(0)

        @pl.when(i == 0)
        def _():
            scr["c"][...] = cin[...]

        scr["edge"][...] = jnp.where(i == nt - 1, hin[...], hedge8[SUBLANES - 1:SUBLANES, :])
        _lru_bwd_tile(0, lambda rs, ls: _f32(dyb, rs, ls) * _silu(_f32(ql, rs, ls)), hs, wd_r, lam_r, scr,
                      (dwd_o, dba_o, dbx_o, dlam_o), i == 0, tl, sub_r, ng, gs)
        dxb_o[...] = scr["dyl"][...] + scr["g"][...]
        cfin[...] = scr["c"][...]

    return pl.pallas_call(
        body, name=name, grid=(nt,),
        in_specs=[pl.BlockSpec((tl, W), lambda i: (tile(i) + ob, 5)), pl.BlockSpec((tl, W), lambda i: (tile(i), 1)), seq,
                  _edge_block(h0s, tl, nt, 0)] + [seq] * 6 + [wd_spec, vec, vec, vec],
        out_specs=[seq, wd_spec, vec, vec, vec, vec],
        out_shape=[jax.ShapeDtypeStruct((rows, W), F32), jax.ShapeDtypeStruct(wd.shape, F32)]
        + [jax.ShapeDtypeStruct((1, W), F32)] * 4,
        scratch_shapes=_bwd_scratch(tl, W),
        compiler_params=_params(("arbitrary",)),
    )(P, dcat, h0s, h0s, *kept, wd, lam, h_init, c_init)


def _mix_bwd1(P, dcat, saved0, h1s, gates1, dxb0, h_init, c_init, wts, *, rows, row_off, row_w, tl, name,
              dp_rows=None, dp_off=0, dp_into=None):
    dp_rows = rows if dp_rows is None else dp_rows
    dpb = dp_off // tl
    W = P.shape[1] // 6
    nt = rows // tl
    ob = row_off // tl
    gs = min(LRU_GROUP, W)
    ng = W // gs
    wca, wcb = wts["wca"], wts["wcb"]
    wd, lam = wts["wd"][1], wts["lam"][1]
    h0s = saved0["h"]
    kept = [saved0["xb"], saved0["xb16"]] + [gates1[n] for n in ("a", "r", "ig", "m2")]

    vec = pl.BlockSpec((1, W), lambda i: (0, 0))
    taps = pl.BlockSpec((SUBLANES, W), lambda i: (0, 0))
    wd_spec = pl.BlockSpec(wd.shape, lambda i: (0, 0, 0))
    seq = pl.BlockSpec((tl, W), lambda i: (i, 0))

    sub_r = min(row_w, tl)
    assert tl % sub_r == 0

    def body(*refs):
        if dp_into is not None:
            refs = refs[1:]
        (bl, cl, ul, gl, vl, ql, dya, dyb, h0, h1, hedge8, dx0, xb_r, xb16_r, a_r, r_r, ig_r, m2_r, wca_r, wcb_r,
         wd_r, lam_r, hin, cin, dp_o, dwd_o, dba_o, dbx_o, dlam_o, dwca_o, dwcb_o, dbcb_o, cfin, *scratch) = refs
        scr = dict(zip(_BWD_SCRATCH, scratch, strict=True))
        scr.update(zip(_FWD_SAVED, (xb_r, xb16_r, a_r, r_r, ig_r, m2_r), strict=True))
        i = pl.program_id(0)

        @pl.when(i == 0)
        def _():
            scr["c"][...] = cin[...]
            dwca_o[...] = jnp.zeros_like(dwca_o)
            dwcb_o[...] = jnp.zeros_like(dwcb_o)
            dbcb_o[...] = jnp.zeros_like(dbcb_o)

        scr["edge"][...] = jnp.where(i == nt - 1, hin[...], hedge8[0:1, :])
        _lru_bwd_tile(1, lambda rs, ls: _f32(dyb, rs, ls) * _silu(_f32(ql, rs, ls)), h1, wd_r, lam_r, scr,
                      (dwd_o, dba_o, dbx_o, dlam_o), i == 0, tl, sub_r, ng, gs)
        cfin[...] = scr["c"][...]
        tmod = _tmod(sub_r, sub_r)

        def rest(r0, l0):
            rs, ls = pl.ds(r0, sub_r), pl.ds(l0, LANES)
            dxb = dx0[rs, ls] + scr["dyl"][rs, ls] + scr["g"][rs, ls]
            dv, dwb = _conv_bwd(dxb, _f32(vl, rs, ls), wcb_r, 4, 2, tmod, sub_r, ls)
            for j in range(4):
                dwcb_o[j:j + 1, ls] += dwb[j]
            dbcb_o[:, ls] += jnp.sum(dxb, axis=0, keepdims=True)
            q = _f32(ql, rs, ls)
            sq = _sigmoid(q)
            dq = _f32(dyb, rs, ls) * (h0[rs, ls] + h1[rs, ls]) * (sq * (1.0 + q * (1.0 - sq)))
            b_, c_, u_, g_ = _f32(bl, rs, ls), _f32(cl, rs, ls), _f32(ul, rs, ls), _f32(gl, rs, ls)
            z = c_ * u_
            cz = _conv(z, wca_r, 3, 1, tmod, sub_r, ls)
            sgm = _sigmoid(g_)
            sg = g_ * sgm
            da = _f32(dya, rs, ls)
            dz, dwa = _conv_bwd(da * b_ * sg, z, wca_r, 3, 1, tmod, sub_r, ls)
            for j in range(3):
                dwca_o[j:j + 1, ls] += dwa[j]
            parts = (da * cz * sg, dz * u_, dz * c_, da * b_ * cz * (sgm * (1.0 + g_ * (1.0 - sgm))), dv, dq)
            for k, val in enumerate(parts):
                dp_o[rs, pl.ds(k * W + l0, LANES)] = val.astype(dp_o.dtype)

        _sub_loop(tl, sub_r, W, rest)

    def pcol(j):
        return pl.BlockSpec((tl, W), lambda i: (i + ob, j))

    prev = [] if dp_into is None else [dp_into]
    return pl.pallas_call(
        body, name=name, grid=(nt,), input_output_aliases={} if dp_into is None else {0: 0},
        in_specs=[pl.BlockSpec(memory_space=pl.ANY)] * len(prev) + [pcol(j) for j in range(6)]
        + [pl.BlockSpec((tl, W), lambda i: (i, 0)), pl.BlockSpec((tl, W), lambda i: (i, 1)), seq, seq,
           _edge_block(h1s, tl, nt, 1), seq] + [seq] * 6 + [taps, taps, wd_spec, vec, vec, vec],
        out_specs=[pl.BlockSpec((tl, 6 * W), lambda i: (i + dpb, 0)), wd_spec, vec, vec, vec, taps, taps, vec, vec],
        out_shape=[jax.ShapeDtypeStruct((dp_rows, 6 * W), BF16), jax.ShapeDtypeStruct(wd.shape, F32)]
        + [jax.ShapeDtypeStruct((1, W), F32)] * 3
        + [jax.ShapeDtypeStruct((SUBLANES, W), F32)] * 2 + [jax.ShapeDtypeStruct((1, W), F32)] * 2,
        scratch_shapes=_bwd_scratch(tl, W),
        compiler_params=_params(("arbitrary",)),
    )(*prev, *([P] * 6), dcat, dcat, h0s, h1s, h1s, dxb0, *kept, wca, wcb, wd, lam, h_init, c_init)


ROW_PART = 128


def _row_parts(tr):
    part = min(ROW_PART, tr)
    return [slice(k * part, (k + 1) * part) for k in range(tr // part)]


def _loss_head(out, x, tgt, gt, fg, tr):
    L, D = x.shape

    def body(o_ref, x_ref, t_ref, gt_ref, fg_ref, dn_o, do_o, dfg_o, dgt_o, loss_o):
        i = pl.program_id(0)

        @pl.when(i == 0)
        def _():
            dfg_o[...] = jnp.zeros_like(dfg_o)
            dgt_o[...] = jnp.zeros_like(dgt_o)
            loss_o[...] = jnp.zeros_like(loss_o)

        gt_v = gt_ref[...]
        fg_v = fg_ref[...]
        for rs in _row_parts(tr):
            o = o_ref[rs, :].astype(F32)
            n = x_ref[rs, :] + gt_v * o
            r = lax.rsqrt(jnp.mean(n * n, axis=-1, keepdims=True) + EPS)
            nr = n * r
            e = nr * fg_v - t_ref[rs, :]
            loss_o[...] += 0.5 * jnp.sum(jnp.mean(e * e, axis=-1, keepdims=True))
            dy = e * (1.0 / D)
            dfg_o[...] += jnp.sum(dy * nr, axis=0, keepdims=True)
            qv = dy * fg_v
            dn = r * (qv - nr * jnp.mean(qv * nr, axis=-1, keepdims=True))
            dgt_o[...] += jnp.sum(dn * o, axis=0, keepdims=True)
            dn_o[rs, :] = dn.astype(dn_o.dtype)
            do_o[rs, :] = (dn * gt_v).astype(do_o.dtype)

    blk = pl.BlockSpec((tr, D), lambda i: (i, 0))
    vec = pl.BlockSpec((1, D), lambda i: (0, 0))
    return pl.pallas_call(
        body, name="loss_head", grid=(L // tr,), in_specs=[blk, blk, blk, vec, vec],
        out_specs=[blk, blk, vec, vec, pl.BlockSpec((SUBLANES, LANES), lambda i: (0, 0))],
        out_shape=[jax.ShapeDtypeStruct((L, D), BF16), jax.ShapeDtypeStruct((L, D), BF16),
                   jax.ShapeDtypeStruct((1, D), F32), jax.ShapeDtypeStruct((1, D), F32),
                   jax.ShapeDtypeStruct((SUBLANES, LANES), F32)],
        compiler_params=_params(("arbitrary",)),
    )(out, x, tgt, gt, fg)


def _norm_bwd(dhl, x, dn, g, sc, tr, name):
    L, D = x.shape
    with_x = dn is not None

    def body(*refs):
        if with_x:
            d_ref, x_ref, dn_ref, g_ref, sc_ref, gx_o, dsh_o, dsc_o, dg_o = refs
        else:
            d_ref, x_ref, g_ref, sc_ref, dsh_o, dsc_o, dg_o = refs
        i = pl.program_id(0)

        @pl.when(i == 0)
        def _():
            dsh_o[...] = jnp.zeros_like(dsh_o)
            dsc_o[...] = jnp.zeros_like(dsc_o)
            dg_o[...] = jnp.zeros_like(dg_o)

        g_v = g_ref[...]
        for rs in _row_parts(tr):
            d = d_ref[rs, :].astype(F32)
            xv = x_ref[rs, :]
            r = lax.rsqrt(jnp.mean(xv * xv, axis=-1, keepdims=True) + EPS)
            xr = xv * r
            dsh_o[...] += jnp.sum(d, axis=0, keepdims=True)
            dsc_o[...] += jnp.sum(d * (xr * g_v), axis=0, keepdims=True)
            dxn = d * (1.0 + sc_ref[...])
            dg_o[...] += jnp.sum(dxn * xr, axis=0, keepdims=True)
            if with_x:
                qv = dxn * g_v
                gx_o[rs, :] = r * (qv - xr * jnp.mean(qv * xr, axis=-1, keepdims=True)) + dn_ref[rs, :].astype(F32)

    blk = pl.BlockSpec((tr, D), lambda i: (i, 0))
    vec = pl.BlockSpec((1, D), lambda i: (0, 0))
    vshape = jax.ShapeDtypeStruct((1, D), F32)
    res = pl.pallas_call(
        body, name=name, grid=(L // tr,),
        in_specs=[blk, blk] + ([blk] if with_x else []) + [vec, vec],
        out_specs=([blk] if with_x else []) + [vec, vec, vec],
        out_shape=([jax.ShapeDtypeStruct((L, D), F32)] if with_x else []) + [vshape] * 3,
        compiler_params=_params(("arbitrary",)),
    )(*([dhl, x] + ([dn] if with_x else []) + [g, sc]))
    return res if with_x else [None] + list(res)


def _pack_blockdiag(wa, wx, gs):
    H, hd, _ = wa.shape
    hp = gs // hd
    ng = H // hp
    on_diag = _diag_mask(gs, hd)

    def bd(w):
        return jnp.where(on_diag, jnp.tile(w.reshape(ng, gs, hd), (1, 1, hp)), 0.0)

    return jnp.concatenate([bd(wa), bd(wx)], axis=-1).astype(BF16)


def _diag_mask(gs, hd):
    idx = jnp.arange(gs) // hd
    return idx[:, None] == idx[None, :]


def _unpack_blockdiag(dwd, H, hd, gs):
    hp = gs // hd
    on_diag = _diag_mask(gs, hd)

    def diag(dm):
        kept = jnp.where(on_diag, dm, 0.0)
        return sum(kept[:, :, p * hd:(p + 1) * hd] for p in range(hp)).reshape(H, hd, hd)

    return diag(dwd[:, :, :gs]), diag(dwd[:, :, gs:])


def kernel(x, c, ctx, c_ctx, norm_g, w_ada, b_ada, w_in, w_conv_a, w_conv_b, b_conv_b, lru_wa, lru_ba, lru_wx, lru_bx, lru_lambda, w_out, final_g, loss_target, m_c_ctx, m_norm_g, m_w_ada, m_b_ada, m_w_in, m_w_conv_a, m_w_conv_b, m_b_conv_b, m_lru_wa, m_lru_ba, m_lru_wx, m_lru_bx, m_lru_lambda, m_w_out, m_final_g, v_c_ctx, v_norm_g, v_w_ada, v_b_ada, v_w_in, v_w_conv_a, v_w_conv_b, v_b_conv_b, v_lru_wa, v_lru_ba, v_lru_wx, v_lru_bx, v_lru_lambda, v_w_out, v_final_g):
    xi, yi, ci = _pos()
    me = 4 * xi + 2 * yi + ci
    q = 2 * xi + yi
    first_core = (ci == 0).astype(F32)

    L, D = x.shape[1], x.shape[2]
    T = ctx.shape[1]
    W = D // 2
    Wq = W // 4
    H, hd = lru_wa.shape[2], lru_wa.shape[3]
    gs = min(LRU_GROUP, W)
    nq = w_ada.shape[2]
    tl = min(256, T, L)
    tr = min(256, T, L)
    x2, ctx2, tgt2 = x[0], ctx[0], loss_target[0]

    def place(shard, full_cols):
        z = jnp.zeros((shard.shape[0], full_cols), F32)
        return lax.dynamic_update_slice(z, shard * first_core, (0, q * shard.shape[1]))

    c_rows = lax.dynamic_update_slice(jnp.zeros((8, D), F32), c, (me, 0))
    small_in = [c_rows, place(w_conv_a[0], W), place(w_conv_b[0], W), place(lru_ba[0], W), place(lru_bx[0], W),
                place(lru_lambda[0], W)]
    small_shapes = [a.shape for a in small_in]
    gathered = _allreduce8_two_level(_pack(small_in, 8 * SUBLANES), "gather_small")
    c_all, wca, wcb, ba_all, bx_all, lam_all = _unpack(gathered, small_shapes)

    s_rows = jnp.concatenate([c_all, c_ctx[None, :], jnp.zeros((7, D), F32)], axis=0)
    mod_part = _matmul(s_rows, w_ada[0], a_act="silu", bias=lax.dynamic_slice(b_ada, (0, q * nq), (1, nq)),
                       tm=16, tn=nq, tk=512, name="ada_fwd")
    mod_all = _allreduce8_two_level(_pack([place(mod_part[:9], 4 * nq)], 8 * SUBLANES), "gather_mod")
    mod_all = _unpack(mod_all, [(9, 4 * nq)])[0]
    mod_l = lax.dynamic_slice(mod_all, (me, 0), (1, 3 * D))
    mod_c = mod_all[8:9]
    sh_l, sc_l, gt_l = mod_l[:, :D], mod_l[:, D:2 * D], mod_l[:, 2 * D:]
    sh_c, sc_c = mod_c[:, :D], mod_c[:, D:2 * D]

    pad_taps = lambda w: jnp.pad(w, ((0, SUBLANES - w.shape[0]), (0, 0)))
    wts = {
        "wca": pad_taps(wca), "wcb": pad_taps(wcb), "bcb": b_conv_b,
        "wd": [_pack_blockdiag(lru_wa[0, d], lru_wx[0, d], gs) for d in range(2)],
        "ba": [ba_all[d:d + 1] for d in range(2)], "bx": [bx_all[d:d + 1] for d in range(2)],
        "lam": [lam_all[d:d + 1] for d in range(2)],
    }

    hl = _norm_in(x2, ctx2, norm_g, sc_l, sh_l, sc_c, sh_c, tr)
    p_lat, win_full, wout_full = _in_proj_gather(hl, w_in[0].astype(BF16), w_out[0].astype(BF16),
                                                 jnp.reshape(q, (1,)).astype(jnp.int32), rows=L, tm=min(1024, L))
    v_ctx = _matmul(hl, win_full, a_rows=T, a_off=L, n_range=(4 * W, W), tm=T, tn=W, tk=D, out_dtype=BF16,
                    name="in_proj_ctx")
    p_ctx = jnp.pad(v_ctx, ((0, 0), (4 * W, W)))
    zero_w = jnp.zeros((1, W), F32)
    ctx0, cgates0 = _mix_fwd(p_ctx, 0, zero_w, wts, rows=T, row_off=0, row_w=T, tl=tl, name="ctx_fwd0")
    c1s, _, cgates1 = _mix_fwd(p_ctx, 1, zero_w, wts, rows=T, row_off=0, row_w=T, tl=tl, saved0=ctx0, name="ctx_fwd1")
    h0_init, h1_init = ctx0["h"][T - 1:T], c1s[0:1]
    tl_tall = 2 * tl if L % (2 * tl) == 0 else tl
    lat0, gates0 = _mix_fwd(p_lat, 0, h0_init, wts, rows=L, row_off=0, row_w=GRID_W, tl=tl_tall, name="mix_fwd0")
    h1s, cat, gates1 = _mix_fwd(p_lat, 1, h1_init, wts, rows=L, row_off=0, row_w=GRID_W, tl=tl_tall, saved0=lat0,
                                name="mix_fwd1")
    out = _matmul(cat, wout_full, tm=1024, tn=D, tk=2 * W, out_dtype=BF16, name="out_proj")
    tr_lat = 2 * tr if L % (2 * tr) == 0 else tr
    dn, dout, dfg, dgt, loss_blk = _loss_head(out, x2, tgt2, gt_l, final_g[None, :], tr_lat)

    dcat = _matmul(dout, wout_full, tb=True, tm=1024, tn=2 * W, tk=D, out_dtype=BF16, name="out_proj_bwd")
    gw_out = _matmul(cat, dout, ta=True, tm=1024, tn=D, tk=2048, out_dtype=BF16, name="w_out_grad")
    dxb0, dwd0, dba0, dbx0, dlam0, ch0 = _mix_bwd0(p_lat, dcat, lat0, gates0, h0_init, zero_w, wts, rows=L, row_off=0,
                                                   row_w=GRID_W, tl=tl_tall, name="mix_bwd0")
    dp_lat, dwd1, dba1, dbx1, dlam1, dwca, dwcb, dbcb, ch1 = _mix_bwd1(
        p_lat, dcat, lat0, h1s, gates1, dxb0, h1_init, zero_w, wts, rows=L, row_off=0, row_w=GRID_W, tl=tl,
        name="mix_bwd1", dp_rows=L + T)
    zero_cat = jnp.zeros((T, 2 * W), BF16)
    cxb0, cwd0, cba0, cbx0, clam0, _ = _mix_bwd0(p_ctx, zero_cat, ctx0, cgates0, zero_w, ch0, wts, rows=T, row_off=0,
                                                 row_w=T, tl=tl, name="ctx_bwd0")
    dp, cwd1, cba1, cbx1, clam1, cwca, cwcb, cbcb, _ = _mix_bwd1(
        p_ctx, zero_cat, ctx0, c1s, cgates1, cxb0, zero_w, ch1, wts, rows=T, row_off=0, row_w=T, tl=tl, name="ctx_bwd1",
        dp_rows=L + T, dp_off=L, dp_into=dp_lat)

    gw_in = _matmul(hl, dp, ta=True, tm=1024, tn=1536, tk=2816, out_dtype=BF16, name="w_in_grad")
    rs_axes = [1, 0]
    pair_sums = _rs_pair_sums([gw_in, gw_out], rs_axes)
    dhc = _matmul(dp, win_full, tb=True, a_rows=T, a_off=L, k_range=(4 * W, W), tm=T, tn=D, tk=W,
                  name="in_proj_bwd_ctx")
    _, dsh_c, dsc_c, dng_c = _norm_bwd(dhc, ctx2, None, norm_g, sc_c, tr, "norm_bwd_ctx")
    zeros_d = jnp.zeros((1, D), F32)
    dmod_c = jnp.concatenate([dsh_c, dsc_c, zeros_d], axis=1)
    dhl, (*rs_slots, dmod_c_all) = _matmul(
        dp, win_full, tb=True, a_rows=L, tm=512, tn=D, tk=6 * W, out_dtype=BF16, name="in_proj_bwd",
        side=_join_sides(_rs_chips_side(pair_sums), _gather_block_side(jnp.pad(dmod_c, ((0, SUBLANES - 1), (0, 0))))))
    gx, dsh_l, dsc_l, dng_l = _norm_bwd(dhl, x2, dn, norm_g, sc_l, tr_lat, "norm_bwd")
    gc_rows = _matmul(lax.dynamic_slice(dmod_c_all.reshape(8 * SUBLANES, 3 * D), (0, q * nq), (8 * SUBLANES, nq)), w_ada[0],
                      tb=True, dsilu_mul=c_ctx[None, :], tm=8 * SUBLANES, tn=D, tk=512, name="c_ctx_grad")
    gc_part = jnp.sum(gc_rows, axis=0, keepdims=True) * first_core

    g_in_shard, g_out_shard = _rs_finish(rs_slots, rs_axes)

    dwa0, dwx0 = _unpack_blockdiag(dwd0 + cwd0, H, hd, gs)
    dwa1, dwx1 = _unpack_blockdiag(dwd1 + cwd1, H, hd, gs)
    dmod_l = jnp.concatenate([dsh_l, dsc_l, dgt], axis=1)
    small_g = [
        lax.dynamic_update_slice(jnp.zeros((8, 3 * D), F32), dmod_l, (me, 0)), dmod_c,
        dfg, dng_l + dng_c, (dwca + cwca)[:3], (dwcb + cwcb)[:4], dbcb + cbcb,
        jnp.stack([dwa0, dwa1]), jnp.stack([dwx0, dwx1]),
        jnp.concatenate([dba0 + cba0, dba1 + cba1], axis=0), jnp.concatenate([dbx0 + cbx0, dbx1 + cbx1], axis=0),
        jnp.concatenate([dlam0 + clam0, dlam1 + clam1], axis=0), loss_blk[0:1, 0:1], gc_part,
    ]
    g_shapes = [a.shape for a in small_g]
    (g_rows, g_modc, g_fg, g_ng, g_wca, g_wcb, g_bcb, g_wa, g_wx, g_ba, g_bx, g_lam, loss_sum, g_c_ctx) = _unpack(
        _allreduce8_two_level(_pack(small_g, 8 * SUBLANES), "reduce_small"), g_shapes)

    g_mod = jnp.concatenate([g_rows, g_modc, jnp.zeros((7, 3 * D), F32)], axis=0)
    g_mod_q = lax.dynamic_slice(g_mod, (0, q * nq), (16, nq))
    g_w_ada = _matmul(s_rows, g_mod_q, ta=True, a_act="silu", tm=1024, tn=nq, tk=16, name="w_ada_grad")
    g_b_ada = jnp.sum(g_mod[:9], axis=0, keepdims=True)

    def shard_cols(a, width):
        return lax.dynamic_slice(a, (0, q * width), (a.shape[0], width))

    grads = {
        "c_ctx": g_c_ctx, "norm_g": g_ng, "b_ada": g_b_ada,
        "w_conv_a": shard_cols(g_wca, Wq)[None], "w_conv_b": shard_cols(g_wcb, Wq)[None], "b_conv_b": g_bcb,
        "lru_wa": g_wa[None], "lru_ba": shard_cols(g_ba, Wq)[None], "lru_wx": g_wx[None],
        "lru_bx": shard_cols(g_bx, Wq)[None], "lru_lambda": shard_cols(g_lam, Wq)[None], "final_g": g_fg[0],
    }
    small_names = list(grads)
    given = dict(c_ctx=(c_ctx, m_c_ctx, v_c_ctx), norm_g=(norm_g, m_norm_g, v_norm_g), b_ada=(b_ada, m_b_ada, v_b_ada),
                 w_conv_a=(w_conv_a, m_w_conv_a, v_w_conv_a), w_conv_b=(w_conv_b, m_w_conv_b, v_w_conv_b),
                 b_conv_b=(b_conv_b, m_b_conv_b, v_b_conv_b), lru_wa=(lru_wa, m_lru_wa, v_lru_wa),
                 lru_ba=(lru_ba, m_lru_ba, v_lru_ba), lru_wx=(lru_wx, m_lru_wx, v_lru_wx),
                 lru_bx=(lru_bx, m_lru_bx, v_lru_bx), lru_lambda=(lru_lambda, m_lru_lambda, v_lru_lambda),
                 final_g=(final_g, m_final_g, v_final_g))
    def rows2d(a):
        return a.reshape(-1, a.shape[-1])

    grads = {n: grads[n].reshape(given[n][0].shape) for n in small_names}
    quads = [tuple(rows2d(a) for a in (given[n][0], grads[n], given[n][1], given[n][2])) for n in small_names]
    updated = _adam_many(quads, "adam_small")
    delta_s, newm_s, newv_s = ({n: u[j].reshape(given[n][0].shape) for n, u in zip(small_names, updated)} for j in range(3))

    big = {"w_ada": (w_ada, g_w_ada, m_w_ada, v_w_ada), "w_in": (w_in, g_in_shard, m_w_in, v_w_in),
           "w_out": (w_out, g_out_shard, m_w_out, v_w_out)}
    delta_b, newm_b, newv_b = {}, {}, {}
    for n, (w, g, m, v) in big.items():
        d_, m_, v_, *echo = _adam(w[0], g, m[0], v[0], "adam_" + n, echo_g=n != "w_ada")
        grads[n] = (echo[0] if echo else g)[None]
        delta_b[n], newm_b[n], newv_b[n] = d_[None], m_[None], v_[None]

    loss = loss_sum[0, 0]
    order = ["c_ctx", "norm_g", "w_ada", "b_ada", "w_in", "w_conv_a", "w_conv_b", "b_conv_b", "lru_wa", "lru_ba",
             "lru_wx", "lru_bx", "lru_lambda", "w_out", "final_g"]
    delta = {**delta_s, **delta_b}
    newm = {**newm_s, **newm_b}
    newv = {**newv_s, **newv_b}
    return (loss, gx[None], *[grads[n] for n in order], *[delta[n] for n in order], *[newm[n] for n in order],
            *[newv[n] for n in order])
```
